```python
import math
import jax, jax.numpy as jnp
from jax import lax
import numpy as np

D_MODEL = 1024
BATCH = 8
SEQ = 2048
DEPTH = 2

N_A_LAYERS = DEPTH // 2
N_B_LAYERS = DEPTH - N_A_LAYERS

SSM_EXPAND = 2
D_INNER = SSM_EXPAND * D_MODEL
SSM_HEAD_DIM = 64
SSM_HEADS = D_INNER // SSM_HEAD_DIM
SSM_GROUPS = 4
SSM_STATE = 128
CONV_WIDTH = 4
CHUNK = 128
GN = SSM_GROUPS * SSM_STATE
CONV_DIM = D_INNER + 2 * GN
IN_PROJ_DIM = D_INNER + CONV_DIM + SSM_HEADS

ATT_HEAD_DIM = 64
N_Q_HEADS = D_MODEL // ATT_HEAD_DIM
N_KV_HEADS = 4
WINDOW = 128
ROPE_THETA = 10000.0

D_FF = 2816
FFN_RES_WEIGHT = 0.5
EPS = 1e-5

kernel_name = 'yoco_ssd_swa_sink_macaron'


def rmsnorm(x, w):
    xf = x.astype(jnp.float32)
    xf = xf * lax.rsqrt(jnp.mean(xf * xf, axis=-1, keepdims=True) + EPS)
    return (xf * w.astype(jnp.float32)).astype(x.dtype)


def swiglu(h, w_gate, w_up, w_down):
    return (jax.nn.silu(h @ w_gate) * (h @ w_up)) @ w_down


def rope_tables(seqlen):
    pos = jnp.arange(seqlen, dtype=jnp.float32)
    inv = 1.0 / (ROPE_THETA ** (jnp.arange(0, ATT_HEAD_DIM, 2, dtype=jnp.float32) / ATT_HEAD_DIM))
    ang = pos[:, None] * inv[None, :]
    return jnp.cos(ang), jnp.sin(ang)


def apply_rope(t, cos, sin):
    tf = t.astype(jnp.float32)
    t1, t2 = jnp.split(tf, 2, axis=-1)
    c = cos[:, None, :]
    s = sin[:, None, :]
    return jnp.concatenate([t1 * c - t2 * s, t2 * c + t1 * s], axis=-1).astype(t.dtype)


def causal_depthwise_conv(u, w, b):
    out = lax.conv_general_dilated(
        u, w[:, None, :].astype(u.dtype), window_strides=(1,),
        padding=[(CONV_WIDTH - 1, 0)],
        dimension_numbers=('NWC', 'WIO', 'NWC'),
        feature_group_count=u.shape[-1])
    return out + b


def segsum(a):
    cs = jnp.cumsum(a, axis=-1)
    diff = cs[..., :, None] - cs[..., None, :]
    t = a.shape[-1]
    mask = jnp.tril(jnp.ones((t, t), dtype=bool))
    return jnp.where(mask, diff, -jnp.inf)


def ssd_chunked(xdt, a, b_ssm, c_ssm):
    bsz, seqlen, _, _ = xdt.shape
    nc = seqlen // CHUNK
    r = SSM_HEADS // SSM_GROUPS
    x = xdt.reshape(bsz, nc, CHUNK, SSM_GROUPS, r, SSM_HEAD_DIM)
    a = a.reshape(bsz, nc, CHUNK, SSM_GROUPS, r).transpose(0, 3, 4, 1, 2)
    bc = b_ssm.reshape(bsz, nc, CHUNK, SSM_GROUPS, SSM_STATE)
    cc = c_ssm.reshape(bsz, nc, CHUNK, SSM_GROUPS, SSM_STATE)
    a_cs = jnp.cumsum(a, axis=-1)
    decay_in = jnp.exp(segsum(a))
    cb = jnp.einsum('bclgn,bcsgn->bcgls', cc, bc)
    y_diag = jnp.einsum('bcgls,bgrcls,bcsgrp->bclgrp', cb, decay_in, x)
    decay_states = jnp.exp(a_cs[..., -1:] - a_cs)
    states = jnp.einsum('bclgn,bgrcl,bclgrp->bcgrpn', bc, decay_states, x)
    chunk_decay = jnp.exp(a_cs[..., -1])
    states_c = jnp.moveaxis(states, 1, 0)
    decay_c = jnp.moveaxis(chunk_decay, 3, 0)

    def step(carry, inp):
        s, d = inp
        return carry * d[..., None, None] + s, carry

    _, prev = lax.scan(step, jnp.zeros_like(states_c[0]), (states_c, decay_c))
    prev = jnp.moveaxis(prev, 0, 1)
    decay_out = jnp.exp(a_cs)
    y_off = jnp.einsum('bclgn,bcgrpn,bgrcl->bclgrp', cc, prev, decay_out)
    return (y_diag + y_off).reshape(bsz, seqlen, SSM_HEADS, SSM_HEAD_DIM)


def mamba2_mixer(h, w_in, conv_w, conv_b, dt_bias, a_log, d_skip, norm_w, w_out):
    bsz, seqlen, _ = h.shape
    zxbcdt = h @ w_in
    z = zxbcdt[..., :D_INNER]
    xbc = zxbcdt[..., D_INNER:D_INNER + CONV_DIM]
    dt_raw = zxbcdt[..., D_INNER + CONV_DIM:]
    xbc = jax.nn.silu(causal_depthwise_conv(xbc, conv_w, conv_b))
    xs = xbc[..., :D_INNER].reshape(bsz, seqlen, SSM_HEADS, SSM_HEAD_DIM).astype(jnp.float32)
    b_ssm = xbc[..., D_INNER:D_INNER + GN].reshape(bsz, seqlen, SSM_GROUPS, SSM_STATE).astype(jnp.float32)
    c_ssm = xbc[..., D_INNER + GN:].reshape(bsz, seqlen, SSM_GROUPS, SSM_STATE).astype(jnp.float32)
    dt = jax.nn.softplus(dt_raw.astype(jnp.float32) + dt_bias.astype(jnp.float32))
    a = -jnp.exp(a_log.astype(jnp.float32))
    y = ssd_chunked(xs * dt[..., None], dt * a, b_ssm, c_ssm)
    y = y + xs * d_skip.astype(jnp.float32)[:, None]
    y = y.reshape(bsz, seqlen, D_INNER) * jax.nn.silu(z.astype(jnp.float32))
    yg = y.reshape(bsz, seqlen, SSM_GROUPS, D_INNER // SSM_GROUPS)
    yg = yg * lax.rsqrt(jnp.mean(yg * yg, axis=-1, keepdims=True) + EPS)
    y = (yg.reshape(bsz, seqlen, D_INNER) * norm_w.astype(jnp.float32)).astype(h.dtype)
    return y @ w_out


def shared_kv(x, kv_norm_w, w_k, b_k, w_v, b_v, cos, sin):
    bsz, seqlen, _ = x.shape
    hkv = rmsnorm(x, kv_norm_w)
    k = (hkv @ w_k + b_k).reshape(bsz, seqlen, N_KV_HEADS, ATT_HEAD_DIM)
    v = (hkv @ w_v + b_v).reshape(bsz, seqlen, N_KV_HEADS, ATT_HEAD_DIM)
    return apply_rope(k, cos, sin), v


def band_blocks(t):
    prev = jnp.pad(t[:, :-1], ((0, 0), (1, 0), (0, 0), (0, 0), (0, 0)))
    return jnp.concatenate([prev, t], axis=2)


def swa_sink_attention(h, k_rot, v, w_q, b_q, sinks, w_o, b_o, cos, sin):
    bsz, seqlen, _ = h.shape
    nb = seqlen // WINDOW
    grp = N_Q_HEADS // N_KV_HEADS
    q = (h @ w_q + b_q).reshape(bsz, seqlen, N_Q_HEADS, ATT_HEAD_DIM)
    q = apply_rope(q, cos, sin).reshape(bsz, nb, WINDOW, N_KV_HEADS, grp, ATT_HEAD_DIM)
    k_band = band_blocks(k_rot.reshape(bsz, nb, WINDOW, N_KV_HEADS, ATT_HEAD_DIM))
    v_band = band_blocks(v.reshape(bsz, nb, WINDOW, N_KV_HEADS, ATT_HEAD_DIM))
    scale = 1.0 / math.sqrt(ATT_HEAD_DIM)
    scores = jnp.einsum('bnqkgd,bnskd->bnkgqs', q, k_band,
                        preferred_element_type=jnp.float32) * scale
    qpos = jnp.arange(WINDOW)[:, None] + WINDOW
    kpos = jnp.arange(2 * WINDOW)[None, :]
    in_window = (kpos <= qpos) & (kpos > qpos - WINDOW)
    has_prev = (jnp.arange(nb) > 0)[:, None, None] | (kpos >= WINDOW)[None]
    mask = in_window[None] & has_prev
    scores = jnp.where(mask[None, :, None, None], scores, -jnp.inf)
    sink = sinks.astype(jnp.float32).reshape(N_KV_HEADS, grp)[None, None, :, :, None]
    m = jnp.maximum(scores.max(axis=-1), sink)
    p = jnp.exp(scores - m[..., None])
    probs = p / (p.sum(axis=-1) + jnp.exp(sink - m))[..., None]
    out = jnp.einsum('bnkgqs,bnskd->bnqkgd', probs.astype(v.dtype), v_band)
    out = out.reshape(bsz, seqlen, N_Q_HEADS * ATT_HEAD_DIM)
    return out @ w_o + b_o


def _fwd_setup_inputs(seed: int = 0) -> dict:
    key = jax.random.key(seed)
    ks = jax.random.split(key, 24)
    f32 = jnp.float32

    def nrm(k, shape, scale):
        return jax.random.normal(k, shape, f32) * scale

    dt0 = jnp.exp(jax.random.uniform(ks[8], (N_A_LAYERS, SSM_HEADS), f32,
                                     math.log(1e-3), math.log(1e-1)))
    return {
        'x': nrm(ks[0], (BATCH, SEQ, D_MODEL), 1.0),
        'norm_w': 1.0 + nrm(ks[1], (DEPTH, 3, D_MODEL), 0.01),
        'ffn_w_gate': nrm(ks[2], (DEPTH, 2, D_MODEL, D_FF), D_MODEL ** -0.5),
        'ffn_w_up': nrm(ks[3], (DEPTH, 2, D_MODEL, D_FF), D_MODEL ** -0.5),
        'ffn_w_down': nrm(ks[4], (DEPTH, 2, D_FF, D_MODEL), D_FF ** -0.5),
        'ssm_w_in': nrm(ks[5], (N_A_LAYERS, D_MODEL, IN_PROJ_DIM), D_MODEL ** -0.5),
        'ssm_conv_w': nrm(ks[6], (N_A_LAYERS, CONV_WIDTH, CONV_DIM), CONV_WIDTH ** -0.5),
        'ssm_conv_b': nrm(ks[7], (N_A_LAYERS, CONV_DIM), 0.01),
        'ssm_dt_bias': dt0 + jnp.log(-jnp.expm1(-dt0)),
        'ssm_a_log': jnp.log(jax.random.uniform(ks[9], (N_A_LAYERS, SSM_HEADS), f32, 1.0, 16.0)),
        'ssm_d': 1.0 + nrm(ks[10], (N_A_LAYERS, SSM_HEADS), 0.01),
        'ssm_norm_w': 1.0 + nrm(ks[11], (N_A_LAYERS, D_INNER), 0.01),
        'ssm_w_out': nrm(ks[12], (N_A_LAYERS, D_INNER, D_MODEL), D_INNER ** -0.5),
        'kv_norm_w': 1.0 + nrm(ks[13], (D_MODEL,), 0.01),
        'w_k': nrm(ks[14], (D_MODEL, N_KV_HEADS * ATT_HEAD_DIM), D_MODEL ** -0.5),
        'b_k': nrm(ks[15], (N_KV_HEADS * ATT_HEAD_DIM,), 0.01),
        'w_v': nrm(ks[16], (D_MODEL, N_KV_HEADS * ATT_HEAD_DIM), D_MODEL ** -0.5),
        'b_v': nrm(ks[17], (N_KV_HEADS * ATT_HEAD_DIM,), 0.01),
        'attn_w_q': nrm(ks[18], (N_B_LAYERS, D_MODEL, N_Q_HEADS * ATT_HEAD_DIM), D_MODEL ** -0.5),
        'attn_b_q': nrm(ks[19], (N_B_LAYERS, N_Q_HEADS * ATT_HEAD_DIM), 0.01),
        'attn_sinks': nrm(ks[20], (N_B_LAYERS, N_Q_HEADS), 0.5),
        'attn_w_o': nrm(ks[21], (N_B_LAYERS, N_Q_HEADS * ATT_HEAD_DIM, D_MODEL),
                        (N_Q_HEADS * ATT_HEAD_DIM) ** -0.5),
        'attn_b_o': nrm(ks[22], (N_B_LAYERS, D_MODEL), 0.01),
        'final_norm_w': 1.0 + nrm(ks[23], (D_MODEL,), 0.01),
    }


def _fwd_reference(x, norm_w, ffn_w_gate, ffn_w_up, ffn_w_down,
              ssm_w_in, ssm_conv_w, ssm_conv_b, ssm_dt_bias, ssm_a_log, ssm_d, ssm_norm_w, ssm_w_out,
              kv_norm_w, w_k, b_k, w_v, b_v,
              attn_w_q, attn_b_q, attn_sinks, attn_w_o, attn_b_o,
              final_norm_w):
    cos, sin = rope_tables(x.shape[1])
    k_shared = None
    v_shared = None
    for layer in range(DEPTH):
        if layer == N_A_LAYERS:
            k_shared, v_shared = shared_kv(x, kv_norm_w, w_k, b_k, w_v, b_v, cos, sin)
        x = x + FFN_RES_WEIGHT * swiglu(rmsnorm(x, norm_w[layer, 0]), ffn_w_gate[layer, 0],
                                        ffn_w_up[layer, 0], ffn_w_down[layer, 0])
        h = rmsnorm(x, norm_w[layer, 1])
        if layer < N_A_LAYERS:
            i = layer
            x = x + mamba2_mixer(h, ssm_w_in[i], ssm_conv_w[i], ssm_conv_b[i], ssm_dt_bias[i],
                                 ssm_a_log[i], ssm_d[i], ssm_norm_w[i], ssm_w_out[i])
        else:
            j = layer - N_A_LAYERS
            x = x + swa_sink_attention(h, k_shared, v_shared, attn_w_q[j], attn_b_q[j],
                                       attn_sinks[j], attn_w_o[j], attn_b_o[j], cos, sin)
        x = x + FFN_RES_WEIGHT * swiglu(rmsnorm(x, norm_w[layer, 2]), ffn_w_gate[layer, 1],
                                        ffn_w_up[layer, 1], ffn_w_down[layer, 1])
    return rmsnorm(x, final_norm_w)


import jax as _jax
import jax.numpy as _jnp

TWIN_FORMAT = 'train_step'
FWD_PARAMS = ['x', 'norm_w', 'ffn_w_gate', 'ffn_w_up', 'ffn_w_down', 'ssm_w_in', 'ssm_conv_w', 'ssm_conv_b', 'ssm_dt_bias', 'ssm_a_log', 'ssm_d', 'ssm_norm_w', 'ssm_w_out', 'kv_norm_w', 'w_k', 'b_k', 'w_v', 'b_v', 'attn_w_q', 'attn_b_q', 'attn_sinks', 'attn_w_o', 'attn_b_o', 'final_norm_w']
TWIN_WEIGHTS = ['norm_w', 'ffn_w_gate', 'ffn_w_up', 'ffn_w_down', 'ssm_w_in', 'ssm_conv_w', 'ssm_conv_b', 'ssm_dt_bias', 'ssm_a_log', 'ssm_d', 'ssm_norm_w', 'ssm_w_out', 'kv_norm_w', 'w_k', 'b_k', 'w_v', 'b_v', 'attn_w_q', 'attn_b_q', 'attn_sinks', 'attn_w_o', 'attn_b_o', 'final_norm_w']
TWIN_DIFF_INPUT = 'x'
TWIN_INPUTS = ['x', 'norm_w', 'ffn_w_gate', 'ffn_w_up', 'ffn_w_down', 'ssm_w_in', 'ssm_conv_w', 'ssm_conv_b', 'ssm_dt_bias', 'ssm_a_log', 'ssm_d', 'ssm_norm_w', 'ssm_w_out', 'kv_norm_w', 'w_k', 'b_k', 'w_v', 'b_v', 'attn_w_q', 'attn_b_q', 'attn_sinks', 'attn_w_o', 'attn_b_o', 'final_norm_w', 'loss_target', 'm_norm_w', 'm_ffn_w_gate', 'm_ffn_w_up', 'm_ffn_w_down', 'm_ssm_w_in', 'm_ssm_conv_w', 'm_ssm_conv_b', 'm_ssm_dt_bias', 'm_ssm_a_log', 'm_ssm_d', 'm_ssm_norm_w', 'm_ssm_w_out', 'm_kv_norm_w', 'm_w_k', 'm_b_k', 'm_w_v', 'm_b_v', 'm_attn_w_q', 'm_attn_b_q', 'm_attn_sinks', 'm_attn_w_o', 'm_attn_b_o', 'm_final_norm_w', 'v_norm_w', 'v_ffn_w_gate', 'v_ffn_w_up', 'v_ffn_w_down', 'v_ssm_w_in', 'v_ssm_conv_w', 'v_ssm_conv_b', 'v_ssm_dt_bias', 'v_ssm_a_log', 'v_ssm_d', 'v_ssm_norm_w', 'v_ssm_w_out', 'v_kv_norm_w', 'v_w_k', 'v_b_k', 'v_w_v', 'v_b_v', 'v_attn_w_q', 'v_attn_b_q', 'v_attn_sinks', 'v_attn_w_o', 'v_attn_b_o', 'v_final_norm_w']
TWIN_OUTPUTS = ['loss', 'grad_x', 'grad_norm_w', 'grad_ffn_w_gate', 'grad_ffn_w_up', 'grad_ffn_w_down', 'grad_ssm_w_in', 'grad_ssm_conv_w', 'grad_ssm_conv_b', 'grad_ssm_dt_bias', 'grad_ssm_a_log', 'grad_ssm_d', 'grad_ssm_norm_w', 'grad_ssm_w_out', 'grad_kv_norm_w', 'grad_w_k', 'grad_b_k', 'grad_w_v', 'grad_b_v', 'grad_attn_w_q', 'grad_attn_b_q', 'grad_attn_sinks', 'grad_attn_w_o', 'grad_attn_b_o', 'grad_final_norm_w', 'delta_norm_w', 'delta_ffn_w_gate', 'delta_ffn_w_up', 'delta_ffn_w_down', 'delta_ssm_w_in', 'delta_ssm_conv_w', 'delta_ssm_conv_b', 'delta_ssm_dt_bias', 'delta_ssm_a_log', 'delta_ssm_d', 'delta_ssm_norm_w', 'delta_ssm_w_out', 'delta_kv_norm_w', 'delta_w_k', 'delta_b_k', 'delta_w_v', 'delta_b_v', 'delta_attn_w_q', 'delta_attn_b_q', 'delta_attn_sinks', 'delta_attn_w_o', 'delta_attn_b_o', 'delta_final_norm_w', 'new_m_norm_w', 'new_m_ffn_w_gate', 'new_m_ffn_w_up', 'new_m_ffn_w_down', 'new_m_ssm_w_in', 'new_m_ssm_conv_w', 'new_m_ssm_conv_b', 'new_m_ssm_dt_bias', 'new_m_ssm_a_log', 'new_m_ssm_d', 'new_m_ssm_norm_w', 'new_m_ssm_w_out', 'new_m_kv_norm_w', 'new_m_w_k', 'new_m_b_k', 'new_m_w_v', 'new_m_b_v', 'new_m_attn_w_q', 'new_m_attn_b_q', 'new_m_attn_sinks', 'new_m_attn_w_o', 'new_m_attn_b_o', 'new_m_final_norm_w', 'new_v_norm_w', 'new_v_ffn_w_gate', 'new_v_ffn_w_up', 'new_v_ffn_w_down', 'new_v_ssm_w_in', 'new_v_ssm_conv_w', 'new_v_ssm_conv_b', 'new_v_ssm_dt_bias', 'new_v_ssm_a_log', 'new_v_ssm_d', 'new_v_ssm_norm_w', 'new_v_ssm_w_out', 'new_v_kv_norm_w', 'new_v_w_k', 'new_v_b_k', 'new_v_w_v', 'new_v_b_v', 'new_v_attn_w_q', 'new_v_attn_b_q', 'new_v_attn_sinks', 'new_v_attn_w_o', 'new_v_attn_b_o', 'new_v_final_norm_w']
TWIN_LEAF_KINDS = {'loss': 'loss', 'grad_x': 'grad_x', 'grad_norm_w': 'grad_w', 'grad_ffn_w_gate': 'grad_w', 'grad_ffn_w_up': 'grad_w', 'grad_ffn_w_down': 'grad_w', 'grad_ssm_w_in': 'grad_w', 'grad_ssm_conv_w': 'grad_w', 'grad_ssm_conv_b': 'grad_w', 'grad_ssm_dt_bias': 'grad_w', 'grad_ssm_a_log': 'grad_w', 'grad_ssm_d': 'grad_w', 'grad_ssm_norm_w': 'grad_w', 'grad_ssm_w_out': 'grad_w', 'grad_kv_norm_w': 'grad_w', 'grad_w_k': 'grad_w', 'grad_b_k': 'grad_w', 'grad_w_v': 'grad_w', 'grad_b_v': 'grad_w', 'grad_attn_w_q': 'grad_w', 'grad_attn_b_q': 'grad_w', 'grad_attn_sinks': 'grad_w', 'grad_attn_w_o': 'grad_w', 'grad_attn_b_o': 'grad_w', 'grad_final_norm_w': 'grad_w', 'delta_norm_w': 'delta_w', 'delta_ffn_w_gate': 'delta_w', 'delta_ffn_w_up': 'delta_w', 'delta_ffn_w_down': 'delta_w', 'delta_ssm_w_in': 'delta_w', 'delta_ssm_conv_w': 'delta_w', 'delta_ssm_conv_b': 'delta_w', 'delta_ssm_dt_bias': 'delta_w', 'delta_ssm_a_log': 'delta_w', 'delta_ssm_d': 'delta_w', 'delta_ssm_norm_w': 'delta_w', 'delta_ssm_w_out': 'delta_w', 'delta_kv_norm_w': 'delta_w', 'delta_w_k': 'delta_w', 'delta_b_k': 'delta_w', 'delta_w_v': 'delta_w', 'delta_b_v': 'delta_w', 'delta_attn_w_q': 'delta_w', 'delta_attn_b_q': 'delta_w', 'delta_attn_sinks': 'delta_w', 'delta_attn_w_o': 'delta_w', 'delta_attn_b_o': 'delta_w', 'delta_final_norm_w': 'delta_w', 'new_m_norm_w': 'new_m', 'new_m_ffn_w_gate': 'new_m', 'new_m_ffn_w_up': 'new_m', 'new_m_ffn_w_down': 'new_m', 'new_m_ssm_w_in': 'new_m', 'new_m_ssm_conv_w': 'new_m', 'new_m_ssm_conv_b': 'new_m', 'new_m_ssm_dt_bias': 'new_m', 'new_m_ssm_a_log': 'new_m', 'new_m_ssm_d': 'new_m', 'new_m_ssm_norm_w': 'new_m', 'new_m_ssm_w_out': 'new_m', 'new_m_kv_norm_w': 'new_m', 'new_m_w_k': 'new_m', 'new_m_b_k': 'new_m', 'new_m_w_v': 'new_m', 'new_m_b_v': 'new_m', 'new_m_attn_w_q': 'new_m', 'new_m_attn_b_q': 'new_m', 'new_m_attn_sinks': 'new_m', 'new_m_attn_w_o': 'new_m', 'new_m_attn_b_o': 'new_m', 'new_m_final_norm_w': 'new_m', 'new_v_norm_w': 'new_v', 'new_v_ffn_w_gate': 'new_v', 'new_v_ffn_w_up': 'new_v', 'new_v_ffn_w_down': 'new_v', 'new_v_ssm_w_in': 'new_v', 'new_v_ssm_conv_w': 'new_v', 'new_v_ssm_conv_b': 'new_v', 'new_v_ssm_dt_bias': 'new_v', 'new_v_ssm_a_log': 'new_v', 'new_v_ssm_d': 'new_v', 'new_v_ssm_norm_w': 'new_v', 'new_v_ssm_w_out': 'new_v', 'new_v_kv_norm_w': 'new_v', 'new_v_w_k': 'new_v', 'new_v_b_k': 'new_v', 'new_v_w_v': 'new_v', 'new_v_b_v': 'new_v', 'new_v_attn_w_q': 'new_v', 'new_v_attn_b_q': 'new_v', 'new_v_attn_sinks': 'new_v', 'new_v_attn_w_o': 'new_v', 'new_v_attn_b_o': 'new_v', 'new_v_final_norm_w': 'new_v'}


def _forward(args):
    return _fwd_reference(*[args[k] for k in FWD_PARAMS])


def _output_shape():
    out = _jax.eval_shape(lambda: _forward(_fwd_setup_inputs(0)))
    return out.shape, out.dtype

N_MICROBATCH = 1
ADAM_LR = 0.001
ADAM_B1 = 0.9
ADAM_B2 = 0.999
ADAM_EPS = 1e-08
ADAM_WD = 0.01
ADAM_STEP = 10
PER_EXAMPLE_BATCH_AXIS = {'x': 0, 'loss_target': 0}
SHARED_INPUTS = []
_WEIGHT_DTYPES = {'norm_w': _jnp.float32, 'ffn_w_gate': _jnp.float32, 'ffn_w_up': _jnp.float32, 'ffn_w_down': _jnp.float32, 'ssm_w_in': _jnp.float32, 'ssm_conv_w': _jnp.float32, 'ssm_conv_b': _jnp.float32, 'ssm_dt_bias': _jnp.float32, 'ssm_a_log': _jnp.float32, 'ssm_d': _jnp.float32, 'ssm_norm_w': _jnp.float32, 'ssm_w_out': _jnp.float32, 'kv_norm_w': _jnp.float32, 'w_k': _jnp.float32, 'b_k': _jnp.float32, 'w_v': _jnp.float32, 'b_v': _jnp.float32, 'attn_w_q': _jnp.float32, 'attn_b_q': _jnp.float32, 'attn_sinks': _jnp.float32, 'attn_w_o': _jnp.float32, 'attn_b_o': _jnp.float32, 'final_norm_w': _jnp.float32}
MOMENT_SCALE = {'norm_w': 6.775106e-02, 'ffn_w_gate': 2.028834e-02, 'ffn_w_up': 1.962828e-02, 'ffn_w_down': 3.254392e-02, 'ssm_w_in': 6.145126e-02, 'ssm_conv_w': 5.694841e-02, 'ssm_conv_b': 8.149558e-02, 'ssm_dt_bias': 9.528646e-02, 'ssm_a_log': 1.252531e-01, 'ssm_d': 2.903996e-01, 'ssm_norm_w': 6.890129e-02, 'ssm_w_out': 9.348863e-02, 'kv_norm_w': 2.418079e-02, 'w_k': 2.933212e-02, 'b_k': 1.838725e-02, 'w_v': 3.941541e-02, 'b_v': 1.925188e-01, 'attn_w_q': 1.451793e-02, 'attn_b_q': 1.486749e-02, 'attn_sinks': 1.558161e-02, 'attn_w_o': 2.001218e-02, 'attn_b_o': 8.563918e-02, 'final_norm_w': 1.601776e+01}


def _to_microbatches(a, axis):
    t = _jnp.moveaxis(a, axis, 0)
    t = t.reshape((N_MICROBATCH, t.shape[0] // N_MICROBATCH) + t.shape[1:])
    return _jnp.moveaxis(t, 1, axis + 1)


def setup_inputs(seed: int = 0) -> dict:
    inp = _fwd_setup_inputs(seed)
    key = _jax.random.fold_in(_jax.random.key(seed), 7919)
    shape, _ = _output_shape()
    out = dict(inp)
    out["loss_target"] = _jax.random.normal(_jax.random.fold_in(key, 0), shape, _jnp.float32)
    for i, name in enumerate(TWIN_WEIGHTS):
        w = inp[name].astype(_jnp.float32)
        if MOMENT_SCALE is None:
            s = _jnp.sqrt(_jnp.mean(_jnp.square(w)) + 1e-30)
        else:
            s = MOMENT_SCALE[name]
        km, kv = _jax.random.split(_jax.random.fold_in(key, i + 1))
        out[name] = w
        out["m_" + name] = s * _jax.random.normal(km, w.shape, _jnp.float32)
        out["v_" + name] = (s * s) * _jax.random.uniform(kv, w.shape, _jnp.float32, 0.5, 1.5)
    if N_MICROBATCH > 1:
        for name, axis in PER_EXAMPLE_BATCH_AXIS.items():
            out[name] = _to_microbatches(out[name], axis)
    return {'x': out['x'], 'norm_w': out['norm_w'], 'ffn_w_gate': out['ffn_w_gate'], 'ffn_w_up': out['ffn_w_up'], 'ffn_w_down': out['ffn_w_down'], 'ssm_w_in': out['ssm_w_in'], 'ssm_conv_w': out['ssm_conv_w'], 'ssm_conv_b': out['ssm_conv_b'], 'ssm_dt_bias': out['ssm_dt_bias'], 'ssm_a_log': out['ssm_a_log'], 'ssm_d': out['ssm_d'], 'ssm_norm_w': out['ssm_norm_w'], 'ssm_w_out': out['ssm_w_out'], 'kv_norm_w': out['kv_norm_w'], 'w_k': out['w_k'], 'b_k': out['b_k'], 'w_v': out['w_v'], 'b_v': out['b_v'], 'attn_w_q': out['attn_w_q'], 'attn_b_q': out['attn_b_q'], 'attn_sinks': out['attn_sinks'], 'attn_w_o': out['attn_w_o'], 'attn_b_o': out['attn_b_o'], 'final_norm_w': out['final_norm_w'], 'loss_target': out['loss_target'], 'm_norm_w': out['m_norm_w'], 'm_ffn_w_gate': out['m_ffn_w_gate'], 'm_ffn_w_up': out['m_ffn_w_up'], 'm_ffn_w_down': out['m_ffn_w_down'], 'm_ssm_w_in': out['m_ssm_w_in'], 'm_ssm_conv_w': out['m_ssm_conv_w'], 'm_ssm_conv_b': out['m_ssm_conv_b'], 'm_ssm_dt_bias': out['m_ssm_dt_bias'], 'm_ssm_a_log': out['m_ssm_a_log'], 'm_ssm_d': out['m_ssm_d'], 'm_ssm_norm_w': out['m_ssm_norm_w'], 'm_ssm_w_out': out['m_ssm_w_out'], 'm_kv_norm_w': out['m_kv_norm_w'], 'm_w_k': out['m_w_k'], 'm_b_k': out['m_b_k'], 'm_w_v': out['m_w_v'], 'm_b_v': out['m_b_v'], 'm_attn_w_q': out['m_attn_w_q'], 'm_attn_b_q': out['m_attn_b_q'], 'm_attn_sinks': out['m_attn_sinks'], 'm_attn_w_o': out['m_attn_w_o'], 'm_attn_b_o': out['m_attn_b_o'], 'm_final_norm_w': out['m_final_norm_w'], 'v_norm_w': out['v_norm_w'], 'v_ffn_w_gate': out['v_ffn_w_gate'], 'v_ffn_w_up': out['v_ffn_w_up'], 'v_ffn_w_down': out['v_ffn_w_down'], 'v_ssm_w_in': out['v_ssm_w_in'], 'v_ssm_conv_w': out['v_ssm_conv_w'], 'v_ssm_conv_b': out['v_ssm_conv_b'], 'v_ssm_dt_bias': out['v_ssm_dt_bias'], 'v_ssm_a_log': out['v_ssm_a_log'], 'v_ssm_d': out['v_ssm_d'], 'v_ssm_norm_w': out['v_ssm_norm_w'], 'v_ssm_w_out': out['v_ssm_w_out'], 'v_kv_norm_w': out['v_kv_norm_w'], 'v_w_k': out['v_w_k'], 'v_b_k': out['v_b_k'], 'v_w_v': out['v_w_v'], 'v_b_v': out['v_b_v'], 'v_attn_w_q': out['v_attn_w_q'], 'v_attn_b_q': out['v_attn_b_q'], 'v_attn_sinks': out['v_attn_sinks'], 'v_attn_w_o': out['v_attn_w_o'], 'v_attn_b_o': out['v_attn_b_o'], 'v_final_norm_w': out['v_final_norm_w']}


def _loss(weights, diff, rest, loss_target):
    with _jax.named_scope("forward"):
        args = {**rest, TWIN_DIFF_INPUT: diff, **{k: w.astype(_WEIGHT_DTYPES[k]) for k, w in weights.items()}}
        y = _forward(args)
    with _jax.named_scope("loss_head"):
        err = _jnp.square(y.astype(_jnp.float32) - loss_target)
        return 0.5 * _jnp.sum(_jnp.mean(err, axis=-1)) if err.ndim else 0.5 * err


def _adamw(w, g, m, v):
    m = ADAM_B1 * m + (1.0 - ADAM_B1) * g
    v = ADAM_B2 * v + (1.0 - ADAM_B2) * _jnp.square(g)
    m_hat = m / (1.0 - ADAM_B1 ** ADAM_STEP)
    v_hat = v / (1.0 - ADAM_B2 ** ADAM_STEP)
    delta = -ADAM_LR * (m_hat / (_jnp.sqrt(v_hat) + ADAM_EPS) + ADAM_WD * w)
    return delta, m, v


def reference(x, norm_w, ffn_w_gate, ffn_w_up, ffn_w_down, ssm_w_in, ssm_conv_w, ssm_conv_b, ssm_dt_bias, ssm_a_log, ssm_d, ssm_norm_w, ssm_w_out, kv_norm_w, w_k, b_k, w_v, b_v, attn_w_q, attn_b_q, attn_sinks, attn_w_o, attn_b_o, final_norm_w, loss_target, m_norm_w, m_ffn_w_gate, m_ffn_w_up, m_ffn_w_down, m_ssm_w_in, m_ssm_conv_w, m_ssm_conv_b, m_ssm_dt_bias, m_ssm_a_log, m_ssm_d, m_ssm_norm_w, m_ssm_w_out, m_kv_norm_w, m_w_k, m_b_k, m_w_v, m_b_v, m_attn_w_q, m_attn_b_q, m_attn_sinks, m_attn_w_o, m_attn_b_o, m_final_norm_w, v_norm_w, v_ffn_w_gate, v_ffn_w_up, v_ffn_w_down, v_ssm_w_in, v_ssm_conv_w, v_ssm_conv_b, v_ssm_dt_bias, v_ssm_a_log, v_ssm_d, v_ssm_norm_w, v_ssm_w_out, v_kv_norm_w, v_w_k, v_b_k, v_w_v, v_b_v, v_attn_w_q, v_attn_b_q, v_attn_sinks, v_attn_w_o, v_attn_b_o, v_final_norm_w):
    given = dict(x=x, norm_w=norm_w, ffn_w_gate=ffn_w_gate, ffn_w_up=ffn_w_up, ffn_w_down=ffn_w_down, ssm_w_in=ssm_w_in, ssm_conv_w=ssm_conv_w, ssm_conv_b=ssm_conv_b, ssm_dt_bias=ssm_dt_bias, ssm_a_log=ssm_a_log, ssm_d=ssm_d, ssm_norm_w=ssm_norm_w, ssm_w_out=ssm_w_out, kv_norm_w=kv_norm_w, w_k=w_k, b_k=b_k, w_v=w_v, b_v=b_v, attn_w_q=attn_w_q, attn_b_q=attn_b_q, attn_sinks=attn_sinks, attn_w_o=attn_w_o, attn_b_o=attn_b_o, final_norm_w=final_norm_w, loss_target=loss_target, m_norm_w=m_norm_w, m_ffn_w_gate=m_ffn_w_gate, m_ffn_w_up=m_ffn_w_up, m_ffn_w_down=m_ffn_w_down, m_ssm_w_in=m_ssm_w_in, m_ssm_conv_w=m_ssm_conv_w, m_ssm_conv_b=m_ssm_conv_b, m_ssm_dt_bias=m_ssm_dt_bias, m_ssm_a_log=m_ssm_a_log, m_ssm_d=m_ssm_d, m_ssm_norm_w=m_ssm_norm_w, m_ssm_w_out=m_ssm_w_out, m_kv_norm_w=m_kv_norm_w, m_w_k=m_w_k, m_b_k=m_b_k, m_w_v=m_w_v, m_b_v=m_b_v, m_attn_w_q=m_attn_w_q, m_attn_b_q=m_attn_b_q, m_attn_sinks=m_attn_sinks, m_attn_w_o=m_attn_w_o, m_attn_b_o=m_attn_b_o, m_final_norm_w=m_final_norm_w, v_norm_w=v_norm_w, v_ffn_w_gate=v_ffn_w_gate, v_ffn_w_up=v_ffn_w_up, v_ffn_w_down=v_ffn_w_down, v_ssm_w_in=v_ssm_w_in, v_ssm_conv_w=v_ssm_conv_w, v_ssm_conv_b=v_ssm_conv_b, v_ssm_dt_bias=v_ssm_dt_bias, v_ssm_a_log=v_ssm_a_log, v_ssm_d=v_ssm_d, v_ssm_norm_w=v_ssm_norm_w, v_ssm_w_out=v_ssm_w_out, v_kv_norm_w=v_kv_norm_w, v_w_k=v_w_k, v_b_k=v_b_k, v_w_v=v_w_v, v_b_v=v_b_v, v_attn_w_q=v_attn_w_q, v_attn_b_q=v_attn_b_q, v_attn_sinks=v_attn_sinks, v_attn_w_o=v_attn_w_o, v_attn_b_o=v_attn_b_o, v_final_norm_w=v_final_norm_w)
    weights = {n: given[n] for n in TWIN_WEIGHTS}
    shared = {n: given[n] for n in SHARED_INPUTS}
    per_example = {n: given[n] for n in ['x']}
    grad_fn = _jax.value_and_grad(_loss, argnums=(0, 1))

    def one_microbatch(ex, loss_target):
        ex = dict(ex)
        diff = ex.pop(TWIN_DIFF_INPUT)
        return grad_fn(weights, diff, {**shared, **ex}, loss_target)

    if N_MICROBATCH == 1:
        loss, (grad_w, grad_x) = one_microbatch(per_example, given["loss_target"])
    else:
        def body(carry, xs):
            loss_sum, grad_sum = carry
            l_k, (gw_k, gx_k) = one_microbatch(xs[0], xs[1])
            with _jax.named_scope("update"):
                return (loss_sum + l_k, _jax.tree.map(_jnp.add, grad_sum, gw_k)), gx_k

        init = (_jnp.zeros((), _jnp.float32), _jax.tree.map(_jnp.zeros_like, weights))
        (loss, grad_w), grad_x = _jax.lax.scan(body, init, (per_example, given["loss_target"]))
    with _jax.named_scope("update"):
        delta_w, new_m, new_v = {}, {}, {}
        for n in TWIN_WEIGHTS:
            delta_w[n], new_m[n], new_v[n] = _adamw(weights[n], grad_w[n], given["m_" + n], given["v_" + n])
    return (loss, grad_x, *[grad_w[n] for n in TWIN_WEIGHTS], *[delta_w[n] for n in TWIN_WEIGHTS],
            *[new_m[n] for n in TWIN_WEIGHTS], *[new_v[n] for n in TWIN_WEIGHTS])
```

```python
import functools
import math

import jax
import jax.numpy as jnp
from jax import lax
from jax.experimental import pallas as pl
from jax.experimental.pallas import tpu as pltpu

F32 = jnp.float32
BF16 = jnp.bfloat16

N_DEV = 8
SEQ = 2048
D_MODEL = 1024
D_FF_SHARD = 352
N_FFN = 4
D_INNER = 2048
SSM_HEADS = 32
SSM_HEAD_DIM = 64
SSM_GROUPS = 4
HEADS_PER_GROUP = 8
SSM_STATE = 128
CHUNK = 128
N_CHUNKS = SEQ // CHUNK
GN = SSM_GROUPS * SSM_STATE
CONV_DIM = D_INNER + 2 * GN
CONV_WIDTH = 4
ZX_DIM = D_INNER + CONV_DIM
IN_PROJ_SHARD = 644
ATT_HEAD_DIM = 64
N_Q_HEADS = 16
N_KV_HEADS = 4
Q_PER_KV = 4
KV_DIM = N_KV_HEADS * ATT_HEAD_DIM
WINDOW = 128
ROPE_THETA = 10000.0
EPS = 1e-5
FFN_RES_WEIGHT = 0.5
ATT_SCALE = 1.0 / math.sqrt(ATT_HEAD_DIM)
NEG_BIG = -1e30

ADAM_LR = 0.001
ADAM_B1 = 0.9
ADAM_B2 = 0.999
ADAM_EPS = 1e-08
ADAM_WD = 0.01
ADAM_STEP = 10

VMEM_LIMIT_BYTES = 56 * 1024 * 1024

NN = (((1,), (0,)), ((), ()))
NT = (((1,), (1,)), ((), ()))
TN = (((0,), (0,)), ((), ()))
_DIMS = {"nn": NN, "nt": NT, "tn": TN}


def _params(*sem):
    return pltpu.CompilerParams(dimension_semantics=sem if sem else None, vmem_limit_bytes=VMEM_LIMIT_BYTES)


def _dot(a, b, dims=NN):
    return lax.dot_general(a.astype(BF16), b.astype(BF16), dims, preferred_element_type=F32)


def _dot_f32(a, b, dims=NN):
    return lax.dot_general(a, b, dims, precision=lax.Precision.HIGHEST, preferred_element_type=F32)


def _sigmoid(x):
    return 1.0 / (1.0 + jnp.exp(-x))


def _dsilu(x, s):
    return s * (1.0 + x * (1.0 - s))


def _rms(x):
    r = lax.rsqrt(jnp.mean(x * x, axis=-1, keepdims=True) + EPS)
    return x * r, r


def _sum_all(x):
    return jnp.sum(jnp.sum(x, axis=1, keepdims=True), axis=0, keepdims=True)


def _mm(a, b, *, dims="nn", bias=None, res=None, out_dtype=F32, name, tm=512, tn=512, tk=1024):
    if dims == "tn":
        k_dim, m_dim = a.shape
    else:
        m_dim, k_dim = a.shape
    n_dim = b.shape[0] if dims == "nt" else b.shape[1]
    tm, tn, tk = min(tm, m_dim), min(tn, n_dim), min(tk, k_dim)
    assert m_dim % tm == 0 and n_dim % tn == 0 and k_dim % tk == 0, (name, a.shape, b.shape)
    nk = k_dim // tk
    a_spec = pl.BlockSpec((tk, tm), lambda i, j, k: (k, i)) if dims == "tn" else pl.BlockSpec((tm, tk), lambda i, j, k: (i, k))
    b_spec = pl.BlockSpec((tn, tk), lambda i, j, k: (j, k)) if dims == "nt" else pl.BlockSpec((tk, tn), lambda i, j, k: (k, j))
    in_specs, args = [a_spec, b_spec], [a, b]
    if bias is not None:
        in_specs.append(pl.BlockSpec((1, tn), lambda i, j, k: (0, j)))
        args.append(bias)
    if res is not None:
        in_specs.append(pl.BlockSpec((tm, tn), lambda i, j, k: (i, j)))
        args.append(res)
    dn = _DIMS[dims]

    def body(*refs):
        a_ref, b_ref = refs[0], refs[1]
        o_ref, acc_ref = refs[-2], refs[-1]
        k = pl.program_id(2)

        @pl.when(k == 0)
        def _():
            acc_ref[...] = jnp.zeros_like(acc_ref)

        acc_ref[...] += _dot(a_ref[...], b_ref[...], dn)

        @pl.when(k == nk - 1)
        def _():
            r = acc_ref[...]
            pos = 2
            if bias is not None:
                r = r + refs[pos][...]
                pos += 1
            if res is not None:
                r = r + refs[pos][...]
            o_ref[...] = r.astype(out_dtype)

    return pl.pallas_call(
        body, name=name, grid=(m_dim // tm, n_dim // tn, nk), in_specs=in_specs,
        out_specs=pl.BlockSpec((tm, tn), lambda i, j, k: (i, j)),
        out_shape=jax.ShapeDtypeStruct((m_dim, n_dim), out_dtype),
        scratch_shapes=[pltpu.VMEM((tm, tn), F32)],
        compiler_params=_params("parallel", "parallel", "arbitrary"),
    )(*args)


def _norm_mm(x, nw, w, bias, *, name, tm=512, tn=512):
    t_dim, d_dim = x.shape
    n_dim = w.shape[1]
    tn = min(tn, n_dim)
    assert t_dim % tm == 0 and n_dim % tn == 0
    has_bias = bias is not None
    in_specs = [pl.BlockSpec((tm, d_dim), lambda i, j: (i, 0)), pl.BlockSpec((1, d_dim), lambda i, j: (0, 0)),
                pl.BlockSpec((d_dim, tn), lambda i, j: (0, j))]
    args = [x, nw, w]
    if has_bias:
        in_specs.append(pl.BlockSpec((1, tn), lambda i, j: (0, j)))
        args.append(bias)

    def body(*refs):
        x_ref, nw_ref, w_ref = refs[:3]
        o_ref, h_ref = refs[-2], refs[-1]

        @pl.when(pl.program_id(1) == 0)
        def _():
            xhat, _ = _rms(x_ref[...])
            h_ref[...] = (xhat * nw_ref[...]).astype(BF16)

        r = _dot(h_ref[...], w_ref[...])
        if has_bias:
            r = r + refs[3][...]
        o_ref[...] = r

    return pl.pallas_call(
        body, name=name, grid=(t_dim // tm, n_dim // tn), in_specs=in_specs,
        out_specs=[pl.BlockSpec((tm, tn), lambda i, j: (i, j)), pl.BlockSpec((tm, d_dim), lambda i, j: (i, 0))],
        out_shape=[jax.ShapeDtypeStruct((t_dim, n_dim), F32), jax.ShapeDtypeStruct((t_dim, d_dim), BF16)],
        compiler_params=_params("parallel", "arbitrary"),
    )(*args)


def _norm_bwd(x, nw, dh, res, *, name, tm=256):
    t_dim, d_dim = x.shape
    n_res = len(res)
    row = pl.BlockSpec((tm, d_dim), lambda i: (i, 0))
    vec = pl.BlockSpec((1, d_dim), lambda i: (0, 0))

    def body(*refs):
        x_ref, nw_ref, dh_ref = refs[:3]
        dx_ref, dnw_ref = refs[-2], refs[-1]
        xhat, r = _rms(x_ref[...])
        dh = dh_ref[...]
        dxhat = dh * nw_ref[...]
        dx = r * (dxhat - xhat * jnp.mean(dxhat * xhat, axis=-1, keepdims=True))
        for rr in refs[3:3 + n_res]:
            dx = dx + rr[...]
        dx_ref[...] = dx

        @pl.when(pl.program_id(0) == 0)
        def _():
            dnw_ref[...] = jnp.zeros_like(dnw_ref)

        dnw_ref[...] += jnp.sum(dh * xhat, axis=0, keepdims=True)

    return pl.pallas_call(
        body, name=name, grid=(t_dim // tm,), in_specs=[row, vec, row] + [row] * n_res,
        out_specs=[row, vec],
        out_shape=[jax.ShapeDtypeStruct((t_dim, d_dim), F32), jax.ShapeDtypeStruct((1, d_dim), F32)],
        compiler_params=_params("arbitrary"),
    )(x, nw, dh, *res)


def _colsum(x, *, name, tm=256):
    t_dim, n_dim = x.shape

    def body(x_ref, o_ref):
        @pl.when(pl.program_id(0) == 0)
        def _():
            o_ref[...] = jnp.zeros_like(o_ref)

        o_ref[...] += jnp.sum(x_ref[...], axis=0, keepdims=True)

    return pl.pallas_call(
        body, name=name, grid=(t_dim // tm,), in_specs=[pl.BlockSpec((tm, n_dim), lambda i: (i, 0))],
        out_specs=pl.BlockSpec((1, n_dim), lambda i: (0, 0)), out_shape=jax.ShapeDtypeStruct((1, n_dim), F32),
        compiler_params=_params("arbitrary"),
    )(x)


FFN_ROW_TILE = 512


def _ffn_fwd(x, nw, wg, wu, wd, blk, *, name):
    t_dim, d_dim = x.shape
    n_tiles = t_dim // FFN_ROW_TILE

    def body(x_ref, nw_ref, wg_ref, wu_ref, wd_ref, o_ref, h_scr):
        j = pl.program_id(0)

        @pl.when(j == 0)
        def _():
            xhat, _ = _rms(x_ref[...])
            h_scr[...] = (xhat * nw_ref[...]).astype(BF16)
            o_ref[...] = jnp.zeros_like(o_ref)

        for t in range(n_tiles):
            rows = pl.ds(t * FFN_ROW_TILE, FFN_ROW_TILE)
            h = h_scr[rows, :]
            g = _dot(h, wg_ref[...])
            u = _dot(h, wu_ref[...])
            act = g * _sigmoid(g) * u
            o_ref[rows, :] += _dot(act, wd_ref[...])

        @pl.when(j == N_DEV - 1)
        def _():
            o_ref[...] = x_ref[...] + FFN_RES_WEIGHT * o_ref[...]

    full = pl.BlockSpec((t_dim, d_dim), lambda j: (0, 0))
    return pl.pallas_call(
        body, name=name, grid=(N_DEV,),
        in_specs=[full, pl.BlockSpec((1, d_dim), lambda j: (0, 0)),
                  pl.BlockSpec((None, None, d_dim, D_FF_SHARD), lambda j: (j, blk, 0, 0)),
                  pl.BlockSpec((None, None, d_dim, D_FF_SHARD), lambda j: (j, blk, 0, 0)),
                  pl.BlockSpec((None, None, D_FF_SHARD, d_dim), lambda j: (j, blk, 0, 0))],
        out_specs=full, out_shape=jax.ShapeDtypeStruct((t_dim, d_dim), F32),
        scratch_shapes=[pltpu.VMEM((t_dim, d_dim), BF16)],
        compiler_params=_params("arbitrary"),
    )(x, nw, wg, wu, wd)


def _ffn_bwd_prep(x, nw, dout, *, name, tm=256):
    t_dim, d_dim = x.shape
    row = pl.BlockSpec((tm, d_dim), lambda i: (i, 0))

    def body(x_ref, nw_ref, dout_ref, h_ref, dob_ref):
        xhat, _ = _rms(x_ref[...])
        h_ref[...] = (xhat * nw_ref[...]).astype(BF16)
        dob_ref[...] = (FFN_RES_WEIGHT * dout_ref[...]).astype(BF16)

    return pl.pallas_call(
        body, name=name, grid=(t_dim // tm,), in_specs=[row, pl.BlockSpec((1, d_dim), lambda i: (0, 0)), row],
        out_specs=[row, row], out_shape=[jax.ShapeDtypeStruct((t_dim, d_dim), BF16)] * 2,
        compiler_params=_params("parallel"),
    )(x, nw, dout)


def _ffn_bwd(h, dob, wg, wu, wd, gg, gu, gd, blk, *, name):
    t_dim, d_dim = h.shape
    n_tiles = t_dim // FFN_ROW_TILE

    def body(h_ref, dob_ref, wg_ref, wu_ref, wd_ref, gg_in, gu_in, gd_in, dh_ref, gg_ref, gu_ref, gd_ref,
             dwg_scr, dwu_scr, dwd_scr):
        j = pl.program_id(0)

        @pl.when(j == 0)
        def _():
            dh_ref[...] = jnp.zeros_like(dh_ref)

        for t in range(n_tiles):
            rows = pl.ds(t * FFN_ROW_TILE, FFN_ROW_TILE)
            hh = h_ref[rows, :]
            do = dob_ref[rows, :]
            g = _dot(hh, wg_ref[...])
            u = _dot(hh, wu_ref[...])
            sg = _sigmoid(g)
            s = g * sg
            da = _dot(do, wd_ref[...], NT)
            dwd = _dot(s * u, do, TN)
            du = (da * s).astype(BF16)
            dg = (da * u * _dsilu(g, sg)).astype(BF16)
            dwg = _dot(hh, dg, TN)
            dwu = _dot(hh, du, TN)
            if t == 0:
                dwd_scr[...] = dwd
                dwg_scr[...] = dwg
                dwu_scr[...] = dwu
            else:
                dwd_scr[...] += dwd
                dwg_scr[...] += dwg
                dwu_scr[...] += dwu
            dh_ref[rows, :] += _dot(dg, wg_ref[...], NT) + _dot(du, wu_ref[...], NT)
        gg_ref[...] = dwg_scr[...].astype(BF16)
        gu_ref[...] = dwu_scr[...].astype(BF16)
        gd_ref[...] = dwd_scr[...].astype(BF16)

    full_bf = pl.BlockSpec((t_dim, d_dim), lambda j: (0, 0))
    col = pl.BlockSpec((None, None, d_dim, D_FF_SHARD), lambda j: (j, blk, 0, 0))
    rowb = pl.BlockSpec((None, None, D_FF_SHARD, d_dim), lambda j: (j, blk, 0, 0))
    anyspec = pl.BlockSpec(memory_space=pl.ANY)
    return pl.pallas_call(
        body, name=name, grid=(N_DEV,),
        in_specs=[full_bf, full_bf, col, col, rowb, anyspec, anyspec, anyspec],
        out_specs=[full_bf, col, col, rowb],
        out_shape=[jax.ShapeDtypeStruct((t_dim, d_dim), F32), jax.ShapeDtypeStruct(gg.shape, BF16),
                   jax.ShapeDtypeStruct(gu.shape, BF16), jax.ShapeDtypeStruct(gd.shape, BF16)],
        scratch_shapes=[pltpu.VMEM((d_dim, D_FF_SHARD), F32), pltpu.VMEM((d_dim, D_FF_SHARD), F32),
                        pltpu.VMEM((D_FF_SHARD, d_dim), F32)],
        input_output_aliases={5: 1, 6: 2, 7: 3},
        compiler_params=_params("arbitrary"),
    )(h, dob, wg, wu, wd, gg, gu, gd)


CONV_COLS = 256


def _shift_down(u, s, rows):
    return jnp.where(rows >= s, pltpu.roll(u, s, 0), 0.0)


def _shift_up(u, s, rows, t_dim):
    return jnp.where(rows < t_dim - s, pltpu.roll(u, t_dim - s, 0), 0.0)


def _conv_pre(u, w_ref, b_ref, rows):
    c = b_ref[...] + w_ref[CONV_WIDTH - 1:CONV_WIDTH, :] * u
    for k in range(CONV_WIDTH - 1):
        c = c + w_ref[k:k + 1, :] * _shift_down(u, CONV_WIDTH - 1 - k, rows)
    return c


def _conv_fwd(zx, cw, cb, *, name):
    t_dim = zx.shape[0]
    off = D_INNER // CONV_COLS

    def body(u_ref, w_ref, b_ref, o_ref):
        rows = lax.broadcasted_iota(jnp.int32, (t_dim, CONV_COLS), 0)
        c = _conv_pre(u_ref[...], w_ref, b_ref, rows)
        o_ref[...] = c * _sigmoid(c)

    return pl.pallas_call(
        body, name=name, grid=(CONV_DIM // CONV_COLS,),
        in_specs=[pl.BlockSpec((t_dim, CONV_COLS), lambda j: (0, off + j)),
                  pl.BlockSpec((CONV_WIDTH, CONV_COLS), lambda j: (0, j)), pl.BlockSpec((1, CONV_COLS), lambda j: (0, j))],
        out_specs=pl.BlockSpec((t_dim, CONV_COLS), lambda j: (0, j)),
        out_shape=jax.ShapeDtypeStruct((t_dim, CONV_DIM), F32),
        compiler_params=_params("parallel"),
    )(zx, cw, cb)


def _conv_bwd(zx, cw, cb, dxs, db, dc, dzx, *, name):
    t_dim = zx.shape[0]
    off = D_INNER // CONV_COLS
    n_xs = D_INNER // CONV_COLS
    n_b = GN // CONV_COLS

    def body(u_ref, w_ref, b_ref, dxs_ref, db_ref, dc_ref, dzx_in, dzx_ref, dw_ref, dbias_ref):
        j = pl.program_id(0)
        rows = lax.broadcasted_iota(jnp.int32, (t_dim, CONV_COLS), 0)
        u = u_ref[...]
        c = _conv_pre(u, w_ref, b_ref, rows)
        d = jnp.where(j < n_xs, dxs_ref[...], jnp.where(j < n_xs + n_b, db_ref[...], dc_ref[...]))
        dcv = d * _dsilu(c, _sigmoid(c))
        dpre = w_ref[CONV_WIDTH - 1:CONV_WIDTH, :] * dcv
        dw_ref[CONV_WIDTH - 1:CONV_WIDTH, :] = jnp.sum(dcv * u, axis=0, keepdims=True)
        for k in range(CONV_WIDTH - 1):
            s = CONV_WIDTH - 1 - k
            dpre = dpre + w_ref[k:k + 1, :] * _shift_up(dcv, s, rows, t_dim)
            dw_ref[k:k + 1, :] = jnp.sum(dcv * _shift_down(u, s, rows), axis=0, keepdims=True)
        dzx_ref[...] = dpre
        dbias_ref[...] = jnp.sum(dcv, axis=0, keepdims=True)

    blk = lambda n: pl.BlockSpec((t_dim, CONV_COLS), n)
    return pl.pallas_call(
        body, name=name, grid=(CONV_DIM // CONV_COLS,),
        in_specs=[blk(lambda j: (0, off + j)), pl.BlockSpec((CONV_WIDTH, CONV_COLS), lambda j: (0, j)),
                  pl.BlockSpec((1, CONV_COLS), lambda j: (0, j)),
                  blk(lambda j: (0, jnp.minimum(j, n_xs - 1))),
                  blk(lambda j: (0, jnp.clip(j - n_xs, 0, n_b - 1))),
                  blk(lambda j: (0, jnp.clip(j - n_xs - n_b, 0, n_b - 1))),
                  pl.BlockSpec(memory_space=pl.ANY)],
        out_specs=[blk(lambda j: (0, off + j)), pl.BlockSpec((CONV_WIDTH, CONV_COLS), lambda j: (0, j)),
                   pl.BlockSpec((1, CONV_COLS), lambda j: (0, j))],
        out_shape=[jax.ShapeDtypeStruct(dzx.shape, F32), jax.ShapeDtypeStruct((CONV_WIDTH, CONV_DIM), F32),
                   jax.ShapeDtypeStruct((1, CONV_DIM), F32)],
        input_output_aliases={6: 0},
        compiler_params=_params("parallel"),
    )(zx, cw, cb, dxs, db, dc, dzx)


def _softplus_parts(x):
    e = jnp.exp(-jnp.abs(x))
    u = 1.0 + e
    log1p_e = jnp.where(u == 1.0, e, jnp.log(u) * e / jnp.where(u == 1.0, 1.0, u - 1.0))
    return jnp.maximum(x, 0.0) + log1p_e


def _dt_prep(dtr, dt_bias, a_log, *, name):
    def body(dtr_ref, bias_ref, alog_ref, dt_ref, a_ref):
        dt = _softplus_parts(dtr_ref[...] + bias_ref[...])
        dt_ref[...] = dt
        a_ref[...] = dt * (-jnp.exp(alog_ref[...]))

    return pl.pallas_call(body, name=name, out_shape=[jax.ShapeDtypeStruct(dtr.shape, F32)] * 2,
                          compiler_params=_params())(dtr, dt_bias, a_log)


def _dt_bwd(dtr, dt_bias, a_log, dt, ddt, da, *, name):
    def body(dtr_ref, bias_ref, alog_ref, dt_ref, ddt_ref, da_ref, ddtr_ref, dbias_ref, dalog_ref):
        a_neg = -jnp.exp(alog_ref[...])
        da_v = da_ref[...]
        ddt_tot = ddt_ref[...] + da_v * a_neg
        ddtr = ddt_tot * _sigmoid(dtr_ref[...] + bias_ref[...])
        ddtr_ref[...] = ddtr
        dbias_ref[...] = jnp.sum(ddtr, axis=0, keepdims=True)
        dalog_ref[...] = jnp.sum(da_v * dt_ref[...], axis=0, keepdims=True) * a_neg

    return pl.pallas_call(
        body, name=name,
        out_shape=[jax.ShapeDtypeStruct(dtr.shape, F32), jax.ShapeDtypeStruct((1, SSM_HEADS), F32),
                   jax.ShapeDtypeStruct((1, SSM_HEADS), F32)],
        compiler_params=_params())(dtr, dt_bias, a_log, dt, ddt, da)


def _ssd_chunk_common(a, b_ref, c_ref):
    row = lax.broadcasted_iota(jnp.int32, (CHUNK, CHUNK), 0)
    col = lax.broadcasted_iota(jnp.int32, (CHUNK, CHUNK), 1)
    causal = col <= row
    lower = causal.astype(F32)
    upper = (col >= row).astype(F32)
    cs = _dot_f32(lower, a)
    cs_row = _dot_f32(a, upper, TN)
    bc = b_ref[...]
    cc = c_ref[...]
    cb = _dot(cc, bc, NT)
    return causal, upper, cs, cs_row, bc, cc, cb


def _ssd_specs():
    xs = pl.BlockSpec((CHUNK, HEADS_PER_GROUP * SSM_HEAD_DIM), lambda g, c: (c, g))
    bsp = pl.BlockSpec((CHUNK, SSM_STATE), lambda g, c: (c, D_INNER // SSM_STATE + g))
    csp = pl.BlockSpec((CHUNK, SSM_STATE), lambda g, c: (c, (D_INNER + GN) // SSM_STATE + g))
    per_head = pl.BlockSpec((None, CHUNK, HEADS_PER_GROUP), lambda g, c: (g, c, 0))
    dsk = pl.BlockSpec((None, 1, HEADS_PER_GROUP), lambda g, c: (g, 0, 0))
    return xs, bsp, csp, per_head, dsk


def _ssd_fwd(xbc, dtg, ag, dg, *, name):
    t_dim = xbc.shape[0]

    def body(xs_ref, b_ref, c_ref, dt_ref, a_ref, d_ref, y_ref, st_ref, s_scr):
        @pl.when(pl.program_id(1) == 0)
        def _():
            s_scr[...] = jnp.zeros_like(s_scr)

        causal, _, cs, cs_row, bc, cc, cb = _ssd_chunk_common(a_ref[...], b_ref, c_ref)
        for hh in range(HEADS_PER_GROUP):
            cols = slice(hh * SSM_HEAD_DIM, (hh + 1) * SSM_HEAD_DIM)
            csl = cs[:, hh:hh + 1]
            cl = cs[CHUNK - 1:CHUNK, hh:hh + 1]
            decay = jnp.exp(jnp.where(causal, csl - cs_row[hh:hh + 1, :], NEG_BIG))
            xs_h = xs_ref[:, cols]
            xdt = xs_h * dt_ref[:, hh:hh + 1]
            prev = s_scr[hh]
            y = _dot(cb * decay, xdt) + jnp.exp(csl) * _dot(cc, prev, NT) + xs_h * d_ref[:, hh:hh + 1]
            y_ref[:, cols] = y
            st_ref[hh] = prev
            s_scr[hh] = jnp.exp(cl) * prev + _dot(xdt * jnp.exp(cl - csl), bc, TN)

    xs, bsp, csp, per_head, dsk = _ssd_specs()
    return pl.pallas_call(
        body, name=name, grid=(SSM_GROUPS, N_CHUNKS),
        in_specs=[xs, bsp, csp, per_head, per_head, dsk],
        out_specs=[pl.BlockSpec((CHUNK, HEADS_PER_GROUP * SSM_HEAD_DIM), lambda g, c: (c, g)),
                   pl.BlockSpec((None, HEADS_PER_GROUP, SSM_HEAD_DIM, SSM_STATE), lambda g, c: (c, g, 0, 0))],
        out_shape=[jax.ShapeDtypeStruct((t_dim, D_INNER), F32),
                   jax.ShapeDtypeStruct((N_CHUNKS, SSM_HEADS, SSM_HEAD_DIM, SSM_STATE), F32)],
        scratch_shapes=[pltpu.VMEM((HEADS_PER_GROUP, SSM_HEAD_DIM, SSM_STATE), F32)],
        compiler_params=_params("parallel", "arbitrary"),
    )(xbc, xbc, xbc, dtg, ag, dg)


def _ssd_bwd(xbc, dtg, ag, dg, states, dy, *, name):
    t_dim = xbc.shape[0]
    last = N_CHUNKS - 1

    def body(xs_ref, b_ref, c_ref, dt_ref, a_ref, d_ref, st_ref, dy_ref,
             dxs_ref, db_ref, dc_ref, ddt_ref, da_ref, dd_ref, ds_scr):
        @pl.when(pl.program_id(1) == 0)
        def _():
            ds_scr[...] = jnp.zeros_like(ds_scr)
            dd_ref[...] = jnp.zeros_like(dd_ref)

        causal, upper, cs, cs_row, bc, cc, cb = _ssd_chunk_common(a_ref[...], b_ref, c_ref)
        lane8 = lax.broadcasted_iota(jnp.int32, (CHUNK, HEADS_PER_GROUP), 1)
        sub8 = lax.broadcasted_iota(jnp.int32, (HEADS_PER_GROUP, CHUNK), 0)
        lane8_row = lax.broadcasted_iota(jnp.int32, (1, HEADS_PER_GROUP), 1)
        is_last = lax.broadcasted_iota(jnp.int32, (CHUNK, 1), 0) == CHUNK - 1
        col_acc = jnp.zeros((CHUNK, HEADS_PER_GROUP), F32)
        row_acc = jnp.zeros((HEADS_PER_GROUP, CHUNK), F32)
        ddt_acc = jnp.zeros((CHUNK, HEADS_PER_GROUP), F32)
        dd_acc = jnp.zeros((1, HEADS_PER_GROUP), F32)
        d_b = jnp.zeros((CHUNK, SSM_STATE), F32)
        d_c = jnp.zeros((CHUNK, SSM_STATE), F32)
        for hh in range(HEADS_PER_GROUP):
            cols = slice(hh * SSM_HEAD_DIM, (hh + 1) * SSM_HEAD_DIM)
            csl = cs[:, hh:hh + 1]
            cl = cs[CHUNK - 1:CHUNK, hh:hh + 1]
            decay = jnp.exp(jnp.where(causal, csl - cs_row[hh:hh + 1, :], NEG_BIG))
            e_out = jnp.exp(csl)
            e_st = jnp.exp(cl - csl)
            e_ch = jnp.exp(cl)
            xs_h = xs_ref[:, cols]
            dt_h = dt_ref[:, hh:hh + 1]
            xdt = xs_h * dt_h
            m = cb * decay
            prev = st_ref[hh]
            d_s = ds_scr[hh]
            dyh = dy_ref[:, cols]
            g1 = _dot(bc, d_s, NT)
            dxdt = _dot(m, dyh, TN) + e_st * g1
            d_m = _dot(dyh, xdt, NT)
            d_cb = d_m * decay
            w = d_m * m
            cp = _dot(cc, prev, NT)
            tl = jnp.sum(xdt * g1, axis=1, keepdims=True) * e_st
            col_part = jnp.sum(w, axis=1, keepdims=True) + e_out * jnp.sum(dyh * cp, axis=1, keepdims=True) - tl
            last_add = jnp.sum(tl, axis=0, keepdims=True) + e_ch * _sum_all(d_s * prev)
            col_part = col_part + jnp.where(is_last, last_add, 0.0)
            row_part = jnp.sum(w, axis=0, keepdims=True)
            d_c = d_c + _dot(d_cb, bc) + e_out * _dot(dyh, prev)
            d_b = d_b + _dot(d_cb, cc, TN) + _dot(xdt * e_st, d_s)
            ds_scr[hh] = e_ch * d_s + _dot(dyh * e_out, cc, TN)
            dxs_ref[:, cols] = dxdt * dt_h + dyh * d_ref[:, hh:hh + 1]
            col_acc = jnp.where(lane8 == hh, col_part, col_acc)
            row_acc = jnp.where(sub8 == hh, row_part, row_acc)
            ddt_acc = jnp.where(lane8 == hh, jnp.sum(dxdt * xs_h, axis=1, keepdims=True), ddt_acc)
            dd_acc = jnp.where(lane8_row == hh, _sum_all(dyh * xs_h), dd_acc)
        da_ref[...] = _dot_f32(upper, col_acc) - _dot_f32(upper, row_acc, NT)
        ddt_ref[...] = ddt_acc
        db_ref[...] = d_b
        dc_ref[...] = d_c
        dd_ref[...] += dd_acc

    rev = lambda c: last - c
    xs = pl.BlockSpec((CHUNK, HEADS_PER_GROUP * SSM_HEAD_DIM), lambda g, c: (rev(c), g))
    bsp = pl.BlockSpec((CHUNK, SSM_STATE), lambda g, c: (rev(c), D_INNER // SSM_STATE + g))
    csp = pl.BlockSpec((CHUNK, SSM_STATE), lambda g, c: (rev(c), (D_INNER + GN) // SSM_STATE + g))
    per_head = pl.BlockSpec((None, CHUNK, HEADS_PER_GROUP), lambda g, c: (g, rev(c), 0))
    dsk = pl.BlockSpec((None, 1, HEADS_PER_GROUP), lambda g, c: (g, 0, 0))
    st = pl.BlockSpec((None, HEADS_PER_GROUP, SSM_HEAD_DIM, SSM_STATE), lambda g, c: (rev(c), g, 0, 0))
    grp = pl.BlockSpec((CHUNK, SSM_STATE), lambda g, c: (rev(c), g))
    return pl.pallas_call(
        body, name=name, grid=(SSM_GROUPS, N_CHUNKS),
        in_specs=[xs, bsp, csp, per_head, per_head, dsk, st, xs],
        out_specs=[xs, grp, grp, per_head, per_head, dsk],
        out_shape=[jax.ShapeDtypeStruct((t_dim, D_INNER), F32), jax.ShapeDtypeStruct((t_dim, GN), F32),
                   jax.ShapeDtypeStruct((t_dim, GN), F32),
                   jax.ShapeDtypeStruct((SSM_GROUPS, t_dim, HEADS_PER_GROUP), F32),
                   jax.ShapeDtypeStruct((SSM_GROUPS, t_dim, HEADS_PER_GROUP), F32),
                   jax.ShapeDtypeStruct((SSM_GROUPS, 1, HEADS_PER_GROUP), F32)],
        scratch_shapes=[pltpu.VMEM((HEADS_PER_GROUP, SSM_HEAD_DIM, SSM_STATE), F32)],
        compiler_params=_params("parallel", "arbitrary"),
    )(xbc, xbc, xbc, dtg, ag, dg, states, dy)


NORM_GROUP = D_INNER // SSM_GROUPS


def _gate_norm_fwd(y, zx, nw, *, name, tm=256):
    t_dim = y.shape[0]
    row = pl.BlockSpec((tm, D_INNER), lambda i: (i, 0))

    def body(y_ref, z_ref, nw_ref, o_ref):
        z = z_ref[...]
        yz = y_ref[...] * (z * _sigmoid(z))
        for g in range(SSM_GROUPS):
            cols = slice(g * NORM_GROUP, (g + 1) * NORM_GROUP)
            yhat, _ = _rms(yz[:, cols])
            o_ref[:, cols] = (yhat * nw_ref[:, cols]).astype(BF16)

    return pl.pallas_call(
        body, name=name, grid=(t_dim // tm,), in_specs=[row, row, pl.BlockSpec((1, D_INNER), lambda i: (0, 0))],
        out_specs=row, out_shape=jax.ShapeDtypeStruct((t_dim, D_INNER), BF16),
        compiler_params=_params("parallel"),
    )(y, zx, nw)


def _gate_norm_bwd(y, zx, nw, dyn, *, name, tm=256):
    t_dim = y.shape[0]
    row = pl.BlockSpec((tm, D_INNER), lambda i: (i, 0))
    vec = pl.BlockSpec((1, D_INNER), lambda i: (0, 0))

    def body(y_ref, z_ref, nw_ref, dyn_ref, dy_ref, dz_ref, dnw_ref):
        @pl.when(pl.program_id(0) == 0)
        def _():
            dnw_ref[...] = jnp.zeros_like(dnw_ref)

        z = z_ref[...]
        yv = y_ref[...]
        sg = _sigmoid(z)
        silu_z = z * sg
        yz = yv * silu_z
        dyn_v = dyn_ref[...]
        for g in range(SSM_GROUPS):
            cols = slice(g * NORM_GROUP, (g + 1) * NORM_GROUP)
            yhat, r = _rms(yz[:, cols])
            dn = dyn_v[:, cols]
            dnw_ref[:, cols] += jnp.sum(dn * yhat, axis=0, keepdims=True)
            dyhat = dn * nw_ref[:, cols]
            dyz = r * (dyhat - yhat * jnp.mean(dyhat * yhat, axis=-1, keepdims=True))
            dy_ref[:, cols] = dyz * silu_z[:, cols]
            dz_ref[:, cols] = dyz * yv[:, cols] * _dsilu(z[:, cols], sg[:, cols])

    return pl.pallas_call(
        body, name=name, grid=(t_dim // tm,), in_specs=[row, row, vec, row],
        out_specs=[row, row, vec],
        out_shape=[jax.ShapeDtypeStruct((t_dim, D_INNER), F32), jax.ShapeDtypeStruct((t_dim, ZX_DIM), F32),
                   jax.ShapeDtypeStruct((1, D_INNER), F32)],
        compiler_params=_params("arbitrary"),
    )(y, zx, nw, dyn)


def _rope(t, cos2, sin2, *, name, tm=256):
    t_dim, width = t.shape
    half = ATT_HEAD_DIM // 2
    reps = width // 128

    def body(t_ref, cos_ref, sin_ref, o_ref):
        x = t_ref[...]
        lane = lax.broadcasted_iota(jnp.int32, (tm, width), 1)
        first = (lane % ATT_HEAD_DIM) < half
        rot = jnp.where(first, -pltpu.roll(x, width - half, 1), pltpu.roll(x, half, 1))
        o_ref[...] = x * jnp.tile(cos_ref[...], (1, reps)) + rot * jnp.tile(sin_ref[...], (1, reps))

    row = pl.BlockSpec((tm, width), lambda i: (i, 0))
    tab = pl.BlockSpec((tm, 128), lambda i: (i, 0))
    return pl.pallas_call(
        body, name=name, grid=(t_dim // tm,), in_specs=[row, tab, tab], out_specs=row,
        out_shape=jax.ShapeDtypeStruct((t_dim, width), F32), compiler_params=_params("parallel"),
    )(t, cos2, sin2)


def _attn_masks(n):
    row = lax.broadcasted_iota(jnp.int32, (WINDOW, WINDOW), 0)
    col = lax.broadcasted_iota(jnp.int32, (WINDOW, WINDOW), 1)
    return col <= row, (col > row) & (n > 0)


def _attn_fwd(q, k, v, sinks, *, name):
    t_dim = q.shape[0]

    def body(q_ref, kc_ref, kp_ref, vc_ref, vp_ref, s_ref, o_ref, l_ref):
        n = pl.program_id(0)
        mask_c, mask_p = _attn_masks(n)
        lane = lax.broadcasted_iota(jnp.int32, (WINDOW, N_Q_HEADS), 1)
        lse = jnp.zeros((WINDOW, N_Q_HEADS), F32)
        for kvh in range(N_KV_HEADS):
            kcols = slice(kvh * ATT_HEAD_DIM, (kvh + 1) * ATT_HEAD_DIM)
            kc, kp = kc_ref[:, kcols].astype(BF16), kp_ref[:, kcols].astype(BF16)
            vc, vp = vc_ref[:, kcols].astype(BF16), vp_ref[:, kcols].astype(BF16)
            for g in range(Q_PER_KV):
                h = kvh * Q_PER_KV + g
                cols = slice(h * ATT_HEAD_DIM, (h + 1) * ATT_HEAD_DIM)
                qh = q_ref[:, cols].astype(BF16)
                sc = jnp.where(mask_c, _dot(qh, kc, NT) * ATT_SCALE, NEG_BIG)
                sp = jnp.where(mask_p, _dot(qh, kp, NT) * ATT_SCALE, NEG_BIG)
                sink = s_ref[:, h:h + 1]
                m = jnp.maximum(jnp.maximum(jnp.max(sc, axis=1, keepdims=True), jnp.max(sp, axis=1, keepdims=True)), sink)
                pc = jnp.exp(sc - m)
                pp = jnp.exp(sp - m)
                den = jnp.sum(pc, axis=1, keepdims=True) + jnp.sum(pp, axis=1, keepdims=True) + jnp.exp(sink - m)
                o_ref[:, cols] = (_dot(pc, vc) + _dot(pp, vp)) / den
                lse = jnp.where(lane == h, m + jnp.log(den), lse)
        l_ref[...] = lse

    cur = lambda w: pl.BlockSpec((WINDOW, w), lambda n: (n, 0))
    prv = lambda w: pl.BlockSpec((WINDOW, w), lambda n: (jnp.maximum(n - 1, 0), 0))
    return pl.pallas_call(
        body, name=name, grid=(t_dim // WINDOW,),
        in_specs=[cur(D_MODEL), cur(KV_DIM), prv(KV_DIM), cur(KV_DIM), prv(KV_DIM), pl.BlockSpec((1, N_Q_HEADS), lambda n: (0, 0))],
        out_specs=[cur(D_MODEL), cur(N_Q_HEADS)],
        out_shape=[jax.ShapeDtypeStruct((t_dim, D_MODEL), F32), jax.ShapeDtypeStruct((t_dim, N_Q_HEADS), F32)],
        compiler_params=_params("parallel"),
    )(q, k, k, v, v, sinks)


def _attn_bwd(q, k, v, sinks, o, lse, do, *, name):
    t_dim = q.shape[0]

    def body(q_ref, kc_ref, kp_ref, vc_ref, vp_ref, s_ref, o_ref, l_ref, do_ref, dq_ref, dk_ref, dv_ref, dsink_ref):
        n = pl.program_id(0)

        @pl.when(n == 0)
        def _():
            dk_ref[...] = jnp.zeros_like(dk_ref)
            dv_ref[...] = jnp.zeros_like(dv_ref)
            dsink_ref[...] = jnp.zeros_like(dsink_ref)

        mask_c, mask_p = _attn_masks(n)
        lane_row = lax.broadcasted_iota(jnp.int32, (1, N_Q_HEADS), 1)
        rows_c = pl.ds(pl.multiple_of(n * WINDOW, WINDOW), WINDOW)
        rows_p = pl.ds(pl.multiple_of(jnp.maximum(n - 1, 0) * WINDOW, WINDOW), WINDOW)
        dsink = jnp.zeros((1, N_Q_HEADS), F32)
        for kvh in range(N_KV_HEADS):
            kcols = slice(kvh * ATT_HEAD_DIM, (kvh + 1) * ATT_HEAD_DIM)
            kc, kp = kc_ref[:, kcols].astype(BF16), kp_ref[:, kcols].astype(BF16)
            vc, vp = vc_ref[:, kcols].astype(BF16), vp_ref[:, kcols].astype(BF16)
            dkc = jnp.zeros((WINDOW, ATT_HEAD_DIM), F32)
            dkp = jnp.zeros((WINDOW, ATT_HEAD_DIM), F32)
            dvc = jnp.zeros((WINDOW, ATT_HEAD_DIM), F32)
            dvp = jnp.zeros((WINDOW, ATT_HEAD_DIM), F32)
            for g in range(Q_PER_KV):
                h = kvh * Q_PER_KV + g
                cols = slice(h * ATT_HEAD_DIM, (h + 1) * ATT_HEAD_DIM)
                qh = q_ref[:, cols].astype(BF16)
                lh = l_ref[:, h:h + 1]
                pc = jnp.exp(jnp.where(mask_c, _dot(qh, kc, NT) * ATT_SCALE, NEG_BIG) - lh)
                pp = jnp.exp(jnp.where(mask_p, _dot(qh, kp, NT) * ATT_SCALE, NEG_BIG) - lh)
                doh = do_ref[:, cols]
                delta = jnp.sum(doh * o_ref[:, cols], axis=1, keepdims=True)
                dsc = pc * (_dot(doh, vc, NT) - delta)
                dsp = pp * (_dot(doh, vp, NT) - delta)
                dq_ref[:, cols] = (_dot(dsc, kc) + _dot(dsp, kp)) * ATT_SCALE
                dkc = dkc + _dot(dsc, qh, TN) * ATT_SCALE
                dkp = dkp + _dot(dsp, qh, TN) * ATT_SCALE
                dvc = dvc + _dot(pc, doh, TN)
                dvp = dvp + _dot(pp, doh, TN)
                p_sink = jnp.exp(s_ref[:, h:h + 1] - lh)
                dsink = jnp.where(lane_row == h, -jnp.sum(p_sink * delta, axis=0, keepdims=True), dsink)
            dk_ref[rows_c, kcols] += dkc
            dk_ref[rows_p, kcols] += dkp
            dv_ref[rows_c, kcols] += dvc
            dv_ref[rows_p, kcols] += dvp
        dsink_ref[...] += dsink

    cur = lambda w: pl.BlockSpec((WINDOW, w), lambda n: (n, 0))
    prv = lambda w: pl.BlockSpec((WINDOW, w), lambda n: (jnp.maximum(n - 1, 0), 0))
    whole = pl.BlockSpec((t_dim, KV_DIM), lambda n: (0, 0))
    svec = pl.BlockSpec((1, N_Q_HEADS), lambda n: (0, 0))
    return pl.pallas_call(
        body, name=name, grid=(t_dim // WINDOW,),
        in_specs=[cur(D_MODEL), cur(KV_DIM), prv(KV_DIM), cur(KV_DIM), prv(KV_DIM), svec, cur(D_MODEL), cur(N_Q_HEADS), cur(D_MODEL)],
        out_specs=[cur(D_MODEL), whole, whole, svec],
        out_shape=[jax.ShapeDtypeStruct((t_dim, D_MODEL), F32), jax.ShapeDtypeStruct((t_dim, KV_DIM), F32),
                   jax.ShapeDtypeStruct((t_dim, KV_DIM), F32), jax.ShapeDtypeStruct((1, N_Q_HEADS), F32)],
        compiler_params=_params("arbitrary"),
    )(q, k, k, v, v, sinks, o, lse, do)


def _loss_head(x, nw, target, *, name, tm=256):
    t_dim, d_dim = x.shape
    row = pl.BlockSpec((tm, d_dim), lambda i: (i, 0))
    vec = pl.BlockSpec((1, d_dim), lambda i: (0, 0))

    def body(x_ref, nw_ref, tgt_ref, loss_ref, dx_ref, dnw_ref):
        @pl.when(pl.program_id(0) == 0)
        def _():
            loss_ref[...] = jnp.zeros_like(loss_ref)
            dnw_ref[...] = jnp.zeros_like(dnw_ref)

        xhat, r = _rms(x_ref[...])
        err = xhat * nw_ref[...] - tgt_ref[...]
        loss_ref[...] += 0.5 * _sum_all(jnp.mean(err * err, axis=-1, keepdims=True))
        dy = err * (1.0 / d_dim)
        dnw_ref[...] += jnp.sum(dy * xhat, axis=0, keepdims=True)
        dxhat = dy * nw_ref[...]
        dx_ref[...] = r * (dxhat - xhat * jnp.mean(dxhat * xhat, axis=-1, keepdims=True))

    return pl.pallas_call(
        body, name=name, grid=(t_dim // tm,), in_specs=[row, vec, row],
        out_specs=[pl.BlockSpec((1, 1), lambda i: (0, 0)), row, vec],
        out_shape=[jax.ShapeDtypeStruct((1, 1), F32), jax.ShapeDtypeStruct((t_dim, d_dim), F32),
                   jax.ShapeDtypeStruct((1, d_dim), F32)],
        compiler_params=_params("arbitrary"),
    )(x, nw, target)


def _rope_tables():
    pos = jnp.arange(SEQ, dtype=F32)
    inv = 1.0 / (ROPE_THETA ** (jnp.arange(0, ATT_HEAD_DIM, 2, dtype=F32) / ATT_HEAD_DIM))
    ang = pos[:, None] * inv[None, :]
    cos, sin = jnp.cos(ang), jnp.sin(ang)
    return jnp.tile(cos, (1, 4)), jnp.tile(sin, (1, 4))


def _to_groups(t):
    return t.reshape(t.shape[0], SSM_GROUPS, HEADS_PER_GROUP).transpose(1, 0, 2)


def _from_groups(t):
    return t.transpose(1, 0, 2).reshape(t.shape[1], SSM_HEADS)


def _local_step(x0, target, w, gg, gu, gd):
    nw = [[w["norm_w"][l, i][None, :] for i in range(3)] for l in range(2)]
    cos2, sin2 = _rope_tables()
    ffn_norm = [nw[0][0], nw[0][2], nw[1][0], nw[1][2]]

    def ffn_f(x, blk):
        return _ffn_fwd(x, ffn_norm[blk], w["gate"], w["up"], w["down"], blk, name=f"ffn_fwd{blk}")

    x1 = ffn_f(x0, 0)
    zx, h1 = _norm_mm(x1, nw[0][1], w["wzx"], None, name="ssm_in_proj")
    dtr = _mm(h1, w["wdt"], name="ssm_dt_proj")
    xbc = _conv_fwd(zx, w["conv_w"], w["conv_b"], name="ssm_conv_fwd")
    dt, a_dt = _dt_prep(dtr, w["dt_bias"], w["a_log"], name="ssm_dt_prep")
    dtg, ag, dg = _to_groups(dt), _to_groups(a_dt), w["d_skip"].reshape(SSM_GROUPS, 1, HEADS_PER_GROUP)
    y_ssd, states = _ssd_fwd(xbc, dtg, ag, dg, name="ssd_fwd")
    yn = _gate_norm_fwd(y_ssd, zx, w["ssm_norm_w"], name="ssm_gate_norm_fwd")
    x2 = _mm(yn, w["wout"], res=x1, name="ssm_out_proj")
    x3 = ffn_f(x2, 1)
    k_pre, hk = _norm_mm(x3, w["kv_norm_w"], w["wk"], w["b_k"], name="k_proj")
    v = _mm(hk, w["wv"], bias=w["b_v"], name="v_proj")
    k_rot = _rope(k_pre, cos2, sin2, name="k_rope")
    x4 = ffn_f(x3, 2)
    q_pre, h4 = _norm_mm(x4, nw[1][1], w["wq"], w["b_q"], name="q_proj")
    q_rot = _rope(q_pre, cos2, sin2, name="q_rope")
    att, lse = _attn_fwd(q_rot, k_rot, v, w["sinks"], name="attn_fwd")
    x5 = _mm(att, w["wo"], bias=w["b_o"], res=x4, name="attn_out_proj")
    x6 = ffn_f(x5, 3)
    loss, dx6, d_final = _loss_head(x6, w["final_norm_w"], target, name="loss_head")

    d_norm = [[None] * 3 for _ in range(2)]

    def ffn_b(x, dout, blk, gg, gu, gd):
        h, dob = _ffn_bwd_prep(x, ffn_norm[blk], dout, name=f"ffn_bwd_prep{blk}")
        dh, gg, gu, gd = _ffn_bwd(h, dob, w["gate"], w["up"], w["down"], gg, gu, gd, blk, name=f"ffn_bwd{blk}")
        dx, dnw = _norm_bwd(x, ffn_norm[blk], dh, [dout], name=f"ffn_norm_bwd{blk}")
        return dx, dnw, gg, gu, gd

    dx5, d_norm[1][2], gg, gu, gd = ffn_b(x5, dx6, 3, gg, gu, gd)
    d_att = _mm(dx5, w["wo"], dims="nt", name="attn_out_proj_dx")
    g_o = _mm(att, dx5, dims="tn", out_dtype=BF16, name="attn_out_proj_dw")
    d_bo = _colsum(dx5, name="attn_bo_grad")
    dq_rot, dk_rot, dv, d_sinks = _attn_bwd(q_rot, k_rot, v, w["sinks"], att, lse, d_att, name="attn_bwd")
    dq = _rope(dq_rot, cos2, -sin2, name="q_rope_bwd")
    dk = _rope(dk_rot, cos2, -sin2, name="k_rope_bwd")
    dh4 = _mm(dq, w["wq"], dims="nt", name="q_proj_dx")
    g_q = _mm(h4, dq, dims="tn", out_dtype=BF16, name="q_proj_dw")
    d_bq = _colsum(dq, name="attn_bq_grad")
    dx4, d_norm[1][1] = _norm_bwd(x4, nw[1][1], dh4, [dx5], name="attn_norm_bwd")
    dx3a, d_norm[1][0], gg, gu, gd = ffn_b(x3, dx4, 2, gg, gu, gd)
    dhk = _mm(dk, w["wk"], dims="nt", name="k_proj_dx")
    dhk = _mm(dv, w["wv"], dims="nt", res=dhk, name="v_proj_dx")
    g_k = _mm(hk, dk, dims="tn", out_dtype=BF16, name="k_proj_dw")
    g_v = _mm(hk, dv, dims="tn", out_dtype=BF16, name="v_proj_dw")
    d_bk = _colsum(dk, name="bk_grad")
    d_bv = _colsum(dv, name="bv_grad")
    dx3, d_kvn = _norm_bwd(x3, w["kv_norm_w"], dhk, [dx3a], name="kv_norm_bwd")
    dx2, d_norm[0][2], gg, gu, gd = ffn_b(x2, dx3, 1, gg, gu, gd)
    d_yn = _mm(dx2, w["wout"], dims="nt", name="ssm_out_proj_dx")
    g_out = _mm(yn, dx2, dims="tn", out_dtype=BF16, name="ssm_out_proj_dw")
    dy_ssd, dzx, d_ssm_norm = _gate_norm_bwd(y_ssd, zx, w["ssm_norm_w"], d_yn, name="ssm_gate_norm_bwd")
    dxs, d_b, d_c, ddtg, dag, ddg = _ssd_bwd(xbc, dtg, ag, dg, states, dy_ssd, name="ssd_bwd")
    dzx, d_conv_w, d_conv_b = _conv_bwd(zx, w["conv_w"], w["conv_b"], dxs, d_b, d_c, dzx, name="ssm_conv_bwd")
    ddtr, d_dt_bias, d_a_log = _dt_bwd(dtr, w["dt_bias"], w["a_log"], dt, _from_groups(ddtg), _from_groups(dag), name="ssm_dt_bwd")
    dh1 = _mm(dzx, w["wzx"], dims="nt", name="ssm_in_proj_dx")
    dh1 = _mm(ddtr, w["wdt"], dims="nt", res=dh1, name="ssm_dt_proj_dx")
    g_zx = _mm(h1, dzx, dims="tn", out_dtype=BF16, name="ssm_in_proj_dw")
    g_dt = _mm(h1, ddtr, dims="tn", out_dtype=BF16, name="ssm_dt_proj_dw")
    dx1, d_norm[0][1] = _norm_bwd(x1, nw[0][1], dh1, [dx2], name="ssm_norm_bwd")
    dx0, d_norm[0][0], gg, gu, gd = ffn_b(x0, dx1, 0, gg, gu, gd)

    g_in = jnp.concatenate([g_zx, g_dt], axis=1).reshape(D_MODEL, N_DEV, IN_PROJ_SHARD).transpose(1, 0, 2)
    big = {"gate": gg, "up": gu, "down": gd, "w_in": g_in, "w_out": g_out.reshape(N_DEV, D_INNER // N_DEV, D_MODEL),
           "w_k": g_k.reshape(N_DEV, D_MODEL // N_DEV, KV_DIM), "w_v": g_v.reshape(N_DEV, D_MODEL // N_DEV, KV_DIM),
           "w_q": g_q.reshape(N_DEV, D_MODEL // N_DEV, D_MODEL), "w_o": g_o.reshape(N_DEV, D_MODEL // N_DEV, D_MODEL)}
    small = {"norm_w": jnp.concatenate([d_norm[l][i] for l in range(2) for i in range(3)], axis=0),
             "conv_w": d_conv_w, "conv_b": d_conv_b, "dt_bias": d_dt_bias, "a_log": d_a_log,
             "d_skip": ddg.reshape(1, SSM_HEADS), "ssm_norm_w": d_ssm_norm, "kv_norm_w": d_kvn,
             "b_k": d_bk, "b_v": d_bv, "b_q": d_bq, "sinks": d_sinks, "b_o": d_bo, "final_norm_w": d_final}
    return loss, dx0, big, small


BLOCK_BYTES = 1 << 20


def _row_tile(rows, cols):
    for t in (512, 256, 128, 64, 32, 16):
        if rows % t == 0 and t * cols * 4 <= BLOCK_BYTES:
            return t
    return rows


def _cast_bf16(x, *, name):
    shape = x.shape
    x2 = x.reshape(-1, shape[-1])
    rows, cols = x2.shape
    tm = _row_tile(rows, cols)
    spec = pl.BlockSpec((tm, cols), lambda i: (i, 0))

    def body(x_ref, o_ref):
        o_ref[...] = x_ref[...].astype(BF16)

    out = pl.pallas_call(body, name=name, grid=(rows // tm,), in_specs=[spec], out_specs=spec,
                         out_shape=jax.ShapeDtypeStruct((rows, cols), BF16), compiler_params=_params("parallel"))(x2)
    return out.reshape(shape)


def _adam_update(g, w, m, v):
    m = ADAM_B1 * m + (1.0 - ADAM_B1) * g
    v = ADAM_B2 * v + (1.0 - ADAM_B2) * (g * g)
    m_hat = m / (1.0 - ADAM_B1 ** ADAM_STEP)
    v_hat = v / (1.0 - ADAM_B2 ** ADAM_STEP)
    delta = -ADAM_LR * (m_hat / (jnp.sqrt(v_hat) + ADAM_EPS) + ADAM_WD * w)
    return delta, m, v


def _adamw(parts, w, m, v, *, name):
    shape = w.shape
    cols = shape[-1]
    p3 = parts.reshape(N_DEV, -1, cols)
    rows = p3.shape[1]
    tm = _row_tile(rows, cols)
    spec = pl.BlockSpec((tm, cols), lambda i: (i, 0))

    def body(p_ref, w_ref, m_ref, v_ref, g_ref, d_ref, nm_ref, nv_ref):
        g = p_ref[0].astype(F32)
        for s in range(1, N_DEV):
            g = g + p_ref[s].astype(F32)
        delta, nm, nv = _adam_update(g, w_ref[...], m_ref[...], v_ref[...])
        g_ref[...] = g
        d_ref[...] = delta
        nm_ref[...] = nm
        nv_ref[...] = nv

    outs = pl.pallas_call(
        body, name=name, grid=(rows // tm,),
        in_specs=[pl.BlockSpec((N_DEV, tm, cols), lambda i: (0, i, 0)), spec, spec, spec],
        out_specs=[spec] * 4, out_shape=[jax.ShapeDtypeStruct((rows, cols), F32)] * 4,
        compiler_params=_params("parallel"),
    )(p3, w.reshape(rows, cols), m.reshape(rows, cols), v.reshape(rows, cols))
    return [o.reshape(shape) for o in outs]


def _sum_parts(parts, *, name):
    def body(p_ref, o_ref):
        g = p_ref[0]
        for s in range(1, N_DEV):
            g = g + p_ref[s]
        o_ref[...] = g

    return pl.pallas_call(body, name=name, out_shape=jax.ShapeDtypeStruct(parts.shape[1:], F32), compiler_params=_params())(parts)


def _adamw_packed(g, w, m, v, *, name):
    def body(g_ref, w_ref, m_ref, v_ref, d_ref, nm_ref, nv_ref):
        delta, nm, nv = _adam_update(g_ref[...], w_ref[...], m_ref[...], v_ref[...])
        d_ref[...] = delta
        nm_ref[...] = nm
        nv_ref[...] = nv

    return pl.pallas_call(body, name=name, out_shape=[jax.ShapeDtypeStruct(g.shape, F32)] * 3, compiler_params=_params())(g, w, m, v)


LANES = 128
SUBLANES = 8


def _pack(arrs):
    rows = []
    for a in arrs:
        flat = a.reshape(-1)
        pad = (-flat.shape[0]) % LANES
        rows.append(jnp.pad(flat, (0, pad)).reshape(-1, LANES))
    out = jnp.concatenate(rows, axis=0)
    return jnp.pad(out, ((0, (-out.shape[0]) % SUBLANES), (0, 0)))


def _unpack(packed, shapes):
    outs, r = [], 0
    for shp in shapes:
        n = math.prod(shp)
        nr = -(-n // LANES)
        outs.append(packed[r:r + nr].reshape(-1)[:n].reshape(shp))
        r += nr
    return outs


MESH = pl.DeviceIdType.MESH
N_PEERS = N_DEV - 1


def _position():
    return lax.axis_index("x"), lax.axis_index("y"), lax.axis_index("c")


def _slot(p):
    return 4 * p[0] + 2 * p[1] + p[2]


def _all_gather(arrs, *, name):
    n = len(arrs)

    def body(*refs):
        ins, outs = refs[:n], refs[n:2 * n]
        send_sems, recv_sems, local_sems = refs[2 * n:]
        x, y, c = _position()
        me, sibling = (x, y, c), (x, y, 1 - c)
        chips = [(1 - x, y), (x, 1 - y), (1 - x, 1 - y)]

        def copy(a, k, block, to, src=None):
            dst = outs[a].at[_slot(block)]
            return pltpu.make_async_remote_copy(src_ref=dst if src is None else src, dst_ref=dst, send_sem=send_sems.at[a, k],
                                                recv_sem=recv_sems.at[a, k], device_id=to, device_id_type=MESH)

        mine = [pltpu.make_async_copy(ins[a], outs[a].at[_slot(me)], local_sems.at[a]) for a in range(n)]
        for cp in mine:
            cp.start()
        first = []
        for a in range(n):
            first.append(copy(a, 0, me, sibling, src=ins[a]))
            first += [copy(a, 1 + j, me, (*chip, c), src=ins[a]) for j, chip in enumerate(chips)]
        for cp in first:
            cp.start()
        passed = []
        for j, chip in enumerate(chips):
            for a in range(n):
                copy(a, 1 + j, (*chip, c), me).wait_recv()
                fwd = copy(a, 4 + j, (*chip, c), sibling)
                fwd.start()
                passed.append(fwd)
        for a in range(n):
            copy(a, 0, sibling, me).wait_recv()
            for j, chip in enumerate(chips):
                copy(a, 4 + j, (*chip, 1 - c), me).wait_recv()
        for cp in first + passed:
            cp.wait_send()
        for cp in mine:
            cp.wait()

    anyspec = pl.BlockSpec(memory_space=pl.ANY)
    return pl.pallas_call(
        body, name=name, in_specs=[anyspec] * n, out_specs=[anyspec] * n,
        out_shape=[jax.ShapeDtypeStruct((N_DEV,) + a.shape, a.dtype) for a in arrs],
        scratch_shapes=[pltpu.SemaphoreType.DMA((n, N_PEERS)), pltpu.SemaphoreType.DMA((n, N_PEERS)), pltpu.SemaphoreType.DMA((n,))],
        compiler_params=pltpu.CompilerParams(has_side_effects=True),
    )(*arrs)


def _all_to_all(arrs, *, name):
    n = len(arrs)

    def body(*refs):
        ins, outs = refs[:n], refs[n:2 * n]
        send_sems, recv_sems, local_sems = refs[2 * n:]
        x, y, c = _position()
        me = (x, y, c)
        peers = [(x ^ ((k >> 2) & 1), y ^ ((k >> 1) & 1), c ^ (k & 1)) for k in range(1, N_DEV)]

        def copy(a, k, peer):
            return pltpu.make_async_remote_copy(src_ref=ins[a].at[_slot(peer)], dst_ref=outs[a].at[_slot(me)],
                                                send_sem=send_sems.at[a, k], recv_sem=recv_sems.at[a, k],
                                                device_id=peer, device_id_type=MESH)

        def arrival(a, k, peer):
            return pltpu.make_async_remote_copy(src_ref=ins[a].at[_slot(me)], dst_ref=outs[a].at[_slot(peer)],
                                                send_sem=send_sems.at[a, k], recv_sem=recv_sems.at[a, k],
                                                device_id=peer, device_id_type=MESH)

        mine = [pltpu.make_async_copy(ins[a].at[_slot(me)], outs[a].at[_slot(me)], local_sems.at[a]) for a in range(n)]
        for cp in mine:
            cp.start()
        sends = [copy(a, k, peer) for a in range(n) for k, peer in enumerate(peers)]
        for cp in sends:
            cp.start()
        for a in range(n):
            for k, peer in enumerate(peers):
                arrival(a, k, peer).wait_recv()
        for cp in sends:
            cp.wait_send()
        for cp in mine:
            cp.wait()

    anyspec = pl.BlockSpec(memory_space=pl.ANY)
    return pl.pallas_call(
        body, name=name, in_specs=[anyspec] * n, out_specs=[anyspec] * n,
        out_shape=[jax.ShapeDtypeStruct(a.shape, a.dtype) for a in arrs],
        scratch_shapes=[pltpu.SemaphoreType.DMA((n, N_PEERS)), pltpu.SemaphoreType.DMA((n, N_PEERS)), pltpu.SemaphoreType.DMA((n,))],
        compiler_params=pltpu.CompilerParams(has_side_effects=True),
    )(*arrs)


WEIGHT_NAMES = ("norm_w", "ffn_w_gate", "ffn_w_up", "ffn_w_down", "ssm_w_in", "ssm_conv_w", "ssm_conv_b", "ssm_dt_bias",
                "ssm_a_log", "ssm_d", "ssm_norm_w", "ssm_w_out", "kv_norm_w", "w_k", "b_k", "w_v", "b_v", "attn_w_q",
                "attn_b_q", "attn_sinks", "attn_w_o", "attn_b_o", "final_norm_w")
MATRIX_NAMES = ("ffn_w_gate", "ffn_w_up", "ffn_w_down", "ssm_w_in", "ssm_w_out", "w_k", "w_v", "attn_w_q", "attn_w_o")
VECTOR_NAMES = tuple(n for n in WEIGHT_NAMES if n not in MATRIX_NAMES)
SHARDED_VECTORS = ("norm_w", "ssm_conv_w", "ssm_conv_b", "ssm_norm_w")


def _step(x, target, p, m, v):
    pos = _slot(_position())
    shard = {
        "ffn_w_gate": p["ffn_w_gate"].reshape(N_FFN, D_MODEL, D_FF_SHARD), "ffn_w_up": p["ffn_w_up"].reshape(N_FFN, D_MODEL, D_FF_SHARD),
        "ffn_w_down": p["ffn_w_down"].reshape(N_FFN, D_FF_SHARD, D_MODEL), "ssm_w_in": p["ssm_w_in"][0], "ssm_w_out": p["ssm_w_out"][0],
        "w_k": p["w_k"], "w_v": p["w_v"], "attn_w_q": p["attn_w_q"][0], "attn_w_o": p["attn_w_o"][0]}
    vec_shard = _pack([p[n] for n in SHARDED_VECTORS])
    gathered = _all_gather([_cast_bf16(shard[n], name=f"cast_{n}") for n in MATRIX_NAMES] + [vec_shard], name="gather_weights")
    full = dict(zip(MATRIX_NAMES, gathered[:-1]))
    vecs = gathered[-1]
    w_in = full["ssm_w_in"].transpose(1, 0, 2).reshape(D_MODEL, N_DEV * IN_PROJ_SHARD)

    def by_device(r0, r1, lead):
        t = vecs[:, r0:r1, :].reshape(N_DEV, lead, -1)
        return t.transpose(1, 0, 2).reshape(lead, -1)

    w = {"gate": full["ffn_w_gate"], "up": full["ffn_w_up"], "down": full["ffn_w_down"],
         "wzx": w_in[:, :ZX_DIM], "wdt": w_in[:, ZX_DIM:], "wout": full["ssm_w_out"].reshape(D_INNER, D_MODEL),
         "wk": full["w_k"].reshape(D_MODEL, KV_DIM), "wv": full["w_v"].reshape(D_MODEL, KV_DIM),
         "wq": full["attn_w_q"].reshape(D_MODEL, D_MODEL), "wo": full["attn_w_o"].reshape(D_MODEL, D_MODEL),
         "norm_w": by_device(0, 6, 6).reshape(2, 3, D_MODEL), "conv_w": by_device(6, 18, CONV_WIDTH),
         "conv_b": by_device(18, 21, 1), "ssm_norm_w": by_device(21, 23, 1),
         "dt_bias": p["ssm_dt_bias"], "a_log": p["ssm_a_log"], "d_skip": p["ssm_d"],
         "kv_norm_w": p["kv_norm_w"][None], "b_k": p["b_k"][None], "b_v": p["b_v"][None], "b_q": p["attn_b_q"],
         "sinks": p["attn_sinks"], "b_o": p["attn_b_o"], "final_norm_w": p["final_norm_w"][None]}

    gg = lax.empty((N_DEV, N_FFN, D_MODEL, D_FF_SHARD), BF16)
    gu = lax.empty((N_DEV, N_FFN, D_MODEL, D_FF_SHARD), BF16)
    gd = lax.empty((N_DEV, N_FFN, D_FF_SHARD, D_MODEL), BF16)
    loss, grad_x, big, small = _local_step(x, target, w, gg, gu, gd)

    big_by_name = {"ffn_w_gate": big["gate"], "ffn_w_up": big["up"], "ffn_w_down": big["down"], "ssm_w_in": big["w_in"],
                   "ssm_w_out": big["w_out"], "w_k": big["w_k"], "w_v": big["w_v"], "attn_w_q": big["w_q"], "attn_w_o": big["w_o"]}
    parts = dict(zip(MATRIX_NAMES, _all_to_all([big_by_name[n] for n in MATRIX_NAMES], name="exchange_matrix_grads")))
    small_by_name = {"norm_w": small["norm_w"], "ssm_conv_w": small["conv_w"], "ssm_conv_b": small["conv_b"],
                     "ssm_dt_bias": small["dt_bias"], "ssm_a_log": small["a_log"], "ssm_d": small["d_skip"],
                     "ssm_norm_w": small["ssm_norm_w"], "kv_norm_w": small["kv_norm_w"], "b_k": small["b_k"], "b_v": small["b_v"],
                     "attn_b_q": small["b_q"], "attn_sinks": small["sinks"], "attn_b_o": small["b_o"],
                     "final_norm_w": small["final_norm_w"]}
    vec_parts = _all_gather([_pack([small_by_name[n] for n in VECTOR_NAMES])], name="gather_vector_grads")[0]
    vec_sum = _sum_parts(vec_parts, name="sum_vector_grads")
    full_shapes = {"norm_w": (2, 3, D_MODEL), "ssm_conv_w": (1, CONV_WIDTH, CONV_DIM), "ssm_conv_b": (1, CONV_DIM),
                   "ssm_norm_w": (1, D_INNER)}
    vec_full = dict(zip(VECTOR_NAMES, _unpack(vec_sum, [full_shapes.get(n, p[n].shape) for n in VECTOR_NAMES])))

    grads, deltas, new_m, new_v = {}, {}, {}, {}
    for n in MATRIX_NAMES:
        grads[n], deltas[n], new_m[n], new_v[n] = _adamw(parts[n], p[n], m[n], v[n], name=f"adamw_{n}")
    for n in VECTOR_NAMES:
        g = vec_full[n]
        if n in SHARDED_VECTORS:
            per = p[n].shape[-1]
            g = lax.dynamic_slice_in_dim(g, pos * per, per, axis=g.ndim - 1)
        grads[n] = g
    packed = _adamw_packed(*[_pack([d[n] for n in VECTOR_NAMES]) for d in (grads, p, m, v)], name="adamw_vectors")
    shapes = [p[n].shape for n in VECTOR_NAMES]
    for d, pk in zip((deltas, new_m, new_v), packed):
        d.update(zip(VECTOR_NAMES, _unpack(pk, shapes)))
    return loss, grad_x, grads, deltas, new_m, new_v


def kernel(x, norm_w, ffn_w_gate, ffn_w_up, ffn_w_down, ssm_w_in, ssm_conv_w, ssm_conv_b, ssm_dt_bias, ssm_a_log, ssm_d, ssm_norm_w, ssm_w_out, kv_norm_w, w_k, b_k, w_v, b_v, attn_w_q, attn_b_q, attn_sinks, attn_w_o, attn_b_o, final_norm_w, loss_target, m_norm_w, m_ffn_w_gate, m_ffn_w_up, m_ffn_w_down, m_ssm_w_in, m_ssm_conv_w, m_ssm_conv_b, m_ssm_dt_bias, m_ssm_a_log, m_ssm_d, m_ssm_norm_w, m_ssm_w_out, m_kv_norm_w, m_w_k, m_b_k, m_w_v, m_b_v, m_attn_w_q, m_attn_b_q, m_attn_sinks, m_attn_w_o, m_attn_b_o, m_final_norm_w, v_norm_w, v_ffn_w_gate, v_ffn_w_up, v_ffn_w_down, v_ssm_w_in, v_ssm_conv_w, v_ssm_conv_b, v_ssm_dt_bias, v_ssm_a_log, v_ssm_d, v_ssm_norm_w, v_ssm_w_out, v_kv_norm_w, v_w_k, v_b_k, v_w_v, v_b_v, v_attn_w_q, v_attn_b_q, v_attn_sinks, v_attn_w_o, v_attn_b_o, v_final_norm_w):
    p = dict(zip(WEIGHT_NAMES, (norm_w, ffn_w_gate, ffn_w_up, ffn_w_down, ssm_w_in, ssm_conv_w, ssm_conv_b, ssm_dt_bias, ssm_a_log, ssm_d, ssm_norm_w, ssm_w_out, kv_norm_w, w_k, b_k, w_v, b_v, attn_w_q, attn_b_q, attn_sinks, attn_w_o, attn_b_o, final_norm_w)))
    m = dict(zip(WEIGHT_NAMES, (m_norm_w, m_ffn_w_gate, m_ffn_w_up, m_ffn_w_down, m_ssm_w_in, m_ssm_conv_w, m_ssm_conv_b, m_ssm_dt_bias, m_ssm_a_log, m_ssm_d, m_ssm_norm_w, m_ssm_w_out, m_kv_norm_w, m_w_k, m_b_k, m_w_v, m_b_v, m_attn_w_q, m_attn_b_q, m_attn_sinks, m_attn_w_o, m_attn_b_o, m_final_norm_w)))
    v = dict(zip(WEIGHT_NAMES, (v_norm_w, v_ffn_w_gate, v_ffn_w_up, v_ffn_w_down, v_ssm_w_in, v_ssm_conv_w, v_ssm_conv_b, v_ssm_dt_bias, v_ssm_a_log, v_ssm_d, v_ssm_norm_w, v_ssm_w_out, v_kv_norm_w, v_w_k, v_b_k, v_w_v, v_b_v, v_attn_w_q, v_attn_b_q, v_attn_sinks, v_attn_w_o, v_attn_b_o, v_final_norm_w)))
    loss, grad_x, grads, deltas, new_m, new_v = _step(x[0], loss_target[0], p, m, v)
    loss = lax.psum(loss[0, 0], ("x", "y", "c"))
    return (loss, grad_x[None], *[grads[n] for n in WEIGHT_NAMES], *[deltas[n] for n in WEIGHT_NAMES],
            *[new_m[n] for n in WEIGHT_NAMES], *[new_v[n] for n in WEIGHT_NAMES])
```

```python
import functools
import math

import jax
import jax.numpy as jnp
from jax import lax
from jax.experimental import pallas as pl
from jax.experimental.pallas import tpu as pltpu

F32 = jnp.float32
BF16 = jnp.bfloat16

N_DEV = 8
SEQ = 2048
D_MODEL = 1024
D_FF_SHARD = 352
N_FFN = 4
D_INNER = 2048
SSM_HEADS = 32
SSM_HEAD_DIM = 64
SSM_GROUPS = 4
HEADS_PER_GROUP = 8
SSM_STATE = 128
CHUNK = 128
N_CHUNKS = SEQ // CHUNK
GN = SSM_GROUPS * SSM_STATE
CONV_DIM = D_INNER + 2 * GN
CONV_WIDTH = 4
ZX_DIM = D_INNER + CONV_DIM
IN_PROJ_SHARD = 644
ATT_HEAD_DIM = 64
N_Q_HEADS = 16
N_KV_HEADS = 4
Q_PER_KV = 4
KV_DIM = N_KV_HEADS * ATT_HEAD_DIM
WINDOW = 128
ROPE_THETA = 10000.0
EPS = 1e-5
FFN_RES_WEIGHT = 0.5
ATT_SCALE = 1.0 / math.sqrt(ATT_HEAD_DIM)
NEG_BIG = -1e30

ADAM_LR = 0.001
ADAM_B1 = 0.9
ADAM_B2 = 0.999
ADAM_EPS = 1e-08
ADAM_WD = 0.01
ADAM_STEP = 10

VMEM_LIMIT_BYTES = 56 * 1024 * 1024

NN = (((1,), (0,)), ((), ()))
NT = (((1,), (1,)), ((), ()))
TN = (((0,), (0,)), ((), ()))
_DIMS = {"nn": NN, "nt": NT, "tn": TN}


def _params(*sem):
    return pltpu.CompilerParams(dimension_semantics=sem if sem else None, vmem_limit_bytes=VMEM_LIMIT_BYTES)


def _dot(a, b, dims=NN):
    return lax.dot_general(a.astype(BF16), b.astype(BF16), dims, preferred_element_type=F32)


def _dot_f32(a, b, dims=NN):
    return lax.dot_general(a, b, dims, precision=lax.Precision.HIGHEST, preferred_element_type=F32)


def _sigmoid(x):
    return 1.0 / (1.0 + jnp.exp(-x))


def _dsilu(x, s):
    return s * (1.0 + x * (1.0 - s))


def _rms(x):
    r = lax.rsqrt(jnp.mean(x * x, axis=-1, keepdims=True) + EPS)
    return x * r, r


def _sum_all(x):
    return jnp.sum(jnp.sum(x, axis=1, keepdims=True), axis=0, keepdims=True)


MESH = pl.DeviceIdType.MESH
N_PEERS = N_DEV - 1


def _position():
    return lax.axis_index("x"), lax.axis_index("y"), lax.axis_index("c")


def _slot(p):
    return 4 * p[0] + 2 * p[1] + p[2]


class _Exchange:
    def __init__(self, arrays, out_shapes):
        n = len(arrays)
        self.arrays = list(arrays)
        self.out_shapes = out_shapes
        self.scratch = [pltpu.SemaphoreType.DMA((n, N_PEERS)), pltpu.SemaphoreType.DMA((n, N_PEERS)), pltpu.SemaphoreType.DMA((n,))]
        self.results = None


class _Gather(_Exchange):
    def __init__(self, arrays):
        super().__init__(arrays, [jax.ShapeDtypeStruct((N_DEV,) + a.shape, a.dtype) for a in arrays])

    def _plan(self, ins, outs, sems):
        send_sems, recv_sems, local_sems = sems
        x, y, c = _position()
        me, sibling = (x, y, c), (x, y, 1 - c)
        chips = [(1 - x, y), (x, 1 - y), (1 - x, 1 - y)]
        n = len(ins)

        def copy(a, k, block, to, src=None):
            dst = outs[a].at[_slot(block)]
            return pltpu.make_async_remote_copy(src_ref=dst if src is None else src, dst_ref=dst, send_sem=send_sems.at[a, k],
                                                recv_sem=recv_sems.at[a, k], device_id=to, device_id_type=MESH)

        mine = [pltpu.make_async_copy(ins[a], outs[a].at[_slot(me)], local_sems.at[a]) for a in range(n)]
        first = []
        for a in range(n):
            first.append(copy(a, 0, me, sibling, src=ins[a]))
            first += [copy(a, 1 + j, me, (*chip, c), src=ins[a]) for j, chip in enumerate(chips)]
        return n, c, me, sibling, chips, copy, mine, first

    def start(self, ins, outs, sems):
        _, _, _, _, _, _, mine, first = self._plan(ins, outs, sems)
        for cp in mine + first:
            cp.start()

    def finish(self, ins, outs, sems):
        n, c, me, sibling, chips, copy, mine, first = self._plan(ins, outs, sems)
        passed = []
        for j, chip in enumerate(chips):
            for a in range(n):
                copy(a, 1 + j, (*chip, c), me).wait_recv()
                fwd = copy(a, 4 + j, (*chip, c), sibling)
                fwd.start()
                passed.append(fwd)
        for a in range(n):
            copy(a, 0, sibling, me).wait_recv()
            for j, chip in enumerate(chips):
                copy(a, 4 + j, (*chip, 1 - c), me).wait_recv()
        for cp in first + passed:
            cp.wait_send()
        for cp in mine:
            cp.wait()


class _AllToAll(_Exchange):
    def __init__(self, arrays):
        super().__init__(arrays, [jax.ShapeDtypeStruct(a.shape, a.dtype) for a in arrays])

    def _plan(self, ins, outs, sems):
        send_sems, recv_sems, local_sems = sems
        x, y, c = _position()
        me = (x, y, c)
        peers = [(x ^ ((k >> 2) & 1), y ^ ((k >> 1) & 1), c ^ (k & 1)) for k in range(1, N_DEV)]
        n = len(ins)

        def copy(a, k, src_slot, dst_slot):
            return pltpu.make_async_remote_copy(src_ref=ins[a].at[_slot(src_slot)], dst_ref=outs[a].at[_slot(dst_slot)],
                                                send_sem=send_sems.at[a, k], recv_sem=recv_sems.at[a, k],
                                                device_id=peers[k], device_id_type=MESH)

        mine = [pltpu.make_async_copy(ins[a].at[_slot(me)], outs[a].at[_slot(me)], local_sems.at[a]) for a in range(n)]
        sends = [copy(a, k, peers[k], me) for a in range(n) for k in range(N_PEERS)]
        arrivals = lambda: [copy(a, k, me, peers[k]) for a in range(n) for k in range(N_PEERS)]
        return mine, sends, arrivals

    def start(self, ins, outs, sems):
        mine, sends, _ = self._plan(ins, outs, sems)
        for cp in mine + sends:
            cp.start()

    def finish(self, ins, outs, sems):
        mine, sends, arrivals = self._plan(ins, outs, sems)
        for cp in arrivals():
            cp.wait_recv()
        for cp in sends:
            cp.wait_send()
        for cp in mine:
            cp.wait()


def _call(body, *, name, grid, in_specs, out_specs, out_shape, args, scratch_shapes=(), sem=(), comm=None, aliases=None):
    single = not isinstance(out_shape, (list, tuple))
    out_shape = [out_shape] if single else list(out_shape)
    out_specs = [out_specs] if single else list(out_specs)
    n_in, n_out, n_scr = len(args), len(out_shape), len(scratch_shapes)
    params = pltpu.CompilerParams(dimension_semantics=tuple(sem) if sem else None, vmem_limit_bytes=VMEM_LIMIT_BYTES)
    if comm is None:
        res = pl.pallas_call(body, name=name, grid=grid, in_specs=list(in_specs), out_specs=out_specs, out_shape=out_shape,
                             scratch_shapes=list(scratch_shapes), input_output_aliases=aliases or {}, compiler_params=params)(*args)
        return res[0] if single else res
    n_ci, n_co = len(comm.arrays), len(comm.out_shapes)

    def carried(*refs):
        pos = 0
        parts = []
        for cnt in (n_in, n_ci, n_out, n_co, n_scr, len(comm.scratch)):
            parts.append(refs[pos:pos + cnt])
            pos += cnt
        ins, c_ins, outs, c_outs, scr, c_sems = parts
        ids = [pl.program_id(d) for d in range(len(grid))]
        is_first = functools.reduce(jnp.logical_and, [i == 0 for i in ids])
        is_last = functools.reduce(jnp.logical_and, [i == g - 1 for i, g in zip(ids, grid)])

        @pl.when(is_first)
        def _():
            comm.start(c_ins, c_outs, c_sems)

        body(*ins, *outs, *scr)

        @pl.when(is_last)
        def _():
            comm.finish(c_ins, c_outs, c_sems)

    anyspec = pl.BlockSpec(memory_space=pl.ANY)
    res = pl.pallas_call(
        carried, name=name, grid=grid, in_specs=list(in_specs) + [anyspec] * n_ci, out_specs=out_specs + [anyspec] * n_co,
        out_shape=out_shape + list(comm.out_shapes), scratch_shapes=list(scratch_shapes) + list(comm.scratch),
        input_output_aliases=aliases or {}, compiler_params=params)(*args, *comm.arrays)
    comm.results = list(res[n_out:])
    return res[0] if single else list(res[:n_out])


def _run_exchange(comm, *, name):
    def body(*refs):
        n_ci, n_co = len(comm.arrays), len(comm.out_shapes)
        ins, outs, sems = refs[:n_ci], refs[n_ci:n_ci + n_co], refs[n_ci + n_co:]
        comm.start(ins, outs, sems)
        comm.finish(ins, outs, sems)

    anyspec = pl.BlockSpec(memory_space=pl.ANY)
    comm.results = list(pl.pallas_call(
        body, name=name, in_specs=[anyspec] * len(comm.arrays), out_specs=[anyspec] * len(comm.out_shapes),
        out_shape=list(comm.out_shapes), scratch_shapes=list(comm.scratch))(*comm.arrays))
    return comm.results


def _mm(a, b, *, dims="nn", bias=None, res=None, out_dtype=F32, name, tm=512, tn=512, tk=1024, comm=None):
    if dims == "tn":
        k_dim, m_dim = a.shape
    else:
        m_dim, k_dim = a.shape
    n_dim = b.shape[0] if dims == "nt" else b.shape[1]
    tm, tn, tk = min(tm, m_dim), min(tn, n_dim), min(tk, k_dim)
    assert m_dim % tm == 0 and n_dim % tn == 0 and k_dim % tk == 0, (name, a.shape, b.shape)
    nk = k_dim // tk
    a_spec = pl.BlockSpec((tk, tm), lambda i, j, k: (k, i)) if dims == "tn" else pl.BlockSpec((tm, tk), lambda i, j, k: (i, k))
    b_spec = pl.BlockSpec((tn, tk), lambda i, j, k: (j, k)) if dims == "nt" else pl.BlockSpec((tk, tn), lambda i, j, k: (k, j))
    in_specs, args = [a_spec, b_spec], [a, b]
    if bias is not None:
        in_specs.append(pl.BlockSpec((1, tn), lambda i, j, k: (0, j)))
        args.append(bias)
    if res is not None:
        in_specs.append(pl.BlockSpec((tm, tn), lambda i, j, k: (i, j)))
        args.append(res)
    dn = _DIMS[dims]

    def body(*refs):
        a_ref, b_ref = refs[0], refs[1]
        o_ref, acc_ref = refs[-2], refs[-1]
        k = pl.program_id(2)

        @pl.when(k == 0)
        def _():
            acc_ref[...] = jnp.zeros_like(acc_ref)

        acc_ref[...] += _dot(a_ref[...], b_ref[...], dn)

        @pl.when(k == nk - 1)
        def _():
            r = acc_ref[...]
            pos = 2
            if bias is not None:
                r = r + refs[pos][...]
                pos += 1
            if res is not None:
                r = r + refs[pos][...]
            o_ref[...] = r.astype(out_dtype)

    return _call(
        body, name=name, grid=(m_dim // tm, n_dim // tn, nk), in_specs=in_specs,
        out_specs=pl.BlockSpec((tm, tn), lambda i, j, k: (i, j)),
        out_shape=jax.ShapeDtypeStruct((m_dim, n_dim), out_dtype),
        scratch_shapes=[pltpu.VMEM((tm, tn), F32)], sem=("parallel", "parallel", "arbitrary"), args=args, comm=comm)


def _norm_mm(x, nw, w, bias, *, name, tm=512, tn=512, comm=None):
    t_dim, d_dim = x.shape
    n_dim = w.shape[1]
    tn = min(tn, n_dim)
    assert t_dim % tm == 0 and n_dim % tn == 0
    has_bias = bias is not None
    in_specs = [pl.BlockSpec((tm, d_dim), lambda i, j: (i, 0)), pl.BlockSpec((1, d_dim), lambda i, j: (0, 0)),
                pl.BlockSpec((d_dim, tn), lambda i, j: (0, j))]
    args = [x, nw, w]
    if has_bias:
        in_specs.append(pl.BlockSpec((1, tn), lambda i, j: (0, j)))
        args.append(bias)

    def body(*refs):
        x_ref, nw_ref, w_ref = refs[:3]
        o_ref, h_ref = refs[-2], refs[-1]

        @pl.when(pl.program_id(1) == 0)
        def _():
            xhat, _ = _rms(x_ref[...])
            h_ref[...] = (xhat * nw_ref[...]).astype(BF16)

        r = _dot(h_ref[...], w_ref[...])
        if has_bias:
            r = r + refs[3][...]
        o_ref[...] = r

    return _call(
        body, name=name, grid=(t_dim // tm, n_dim // tn), in_specs=in_specs,
        out_specs=[pl.BlockSpec((tm, tn), lambda i, j: (i, j)), pl.BlockSpec((tm, d_dim), lambda i, j: (i, 0))],
        out_shape=[jax.ShapeDtypeStruct((t_dim, n_dim), F32), jax.ShapeDtypeStruct((t_dim, d_dim), BF16)],
        sem=("parallel", "arbitrary"), args=args, comm=comm)


def _norm_bwd(x, nw, dh, res, *, name, tm=256):
    t_dim, d_dim = x.shape
    n_res = len(res)
    row = pl.BlockSpec((tm, d_dim), lambda i: (i, 0))
    vec = pl.BlockSpec((1, d_dim), lambda i: (0, 0))

    def body(*refs):
        x_ref, nw_ref, dh_ref = refs[:3]
        dx_ref, dnw_ref = refs[-2], refs[-1]
        xhat, r = _rms(x_ref[...])
        dh = dh_ref[...]
        dxhat = dh * nw_ref[...]
        dx = r * (dxhat - xhat * jnp.mean(dxhat * xhat, axis=-1, keepdims=True))
        for rr in refs[3:3 + n_res]:
            dx = dx + rr[...]
        dx_ref[...] = dx

        @pl.when(pl.program_id(0) == 0)
        def _():
            dnw_ref[...] = jnp.zeros_like(dnw_ref)

        dnw_ref[...] += jnp.sum(dh * xhat, axis=0, keepdims=True)

    return pl.pallas_call(
        body, name=name, grid=(t_dim // tm,), in_specs=[row, vec, row] + [row] * n_res,
        out_specs=[row, vec],
        out_shape=[jax.ShapeDtypeStruct((t_dim, d_dim), F32), jax.ShapeDtypeStruct((1, d_dim), F32)],
        compiler_params=_params("arbitrary"),
    )(x, nw, dh, *res)


def _colsum(x, *, name, tm=256):
    t_dim, n_dim = x.shape

    def body(x_ref, o_ref):
        @pl.when(pl.program_id(0) == 0)
        def _():
            o_ref[...] = jnp.zeros_like(o_ref)

        o_ref[...] += jnp.sum(x_ref[...], axis=0, keepdims=True)

    return pl.pallas_call(
        body, name=name, grid=(t_dim // tm,), in_specs=[pl.BlockSpec((tm, n_dim), lambda i: (i, 0))],
        out_specs=pl.BlockSpec((1, n_dim), lambda i: (0, 0)), out_shape=jax.ShapeDtypeStruct((1, n_dim), F32),
        compiler_params=_params("arbitrary"),
    )(x)


FFN_ROW_TILE = 512


def _ffn_fwd(x, nw, wg, wu, wd, *, name, comm=None):
    t_dim, d_dim = x.shape
    n_tiles = t_dim // FFN_ROW_TILE

    def body(x_ref, nw_ref, wg_ref, wu_ref, wd_ref, o_ref, h_scr):
        j = pl.program_id(0)

        @pl.when(j == 0)
        def _():
            xhat, _ = _rms(x_ref[...])
            h_scr[...] = (xhat * nw_ref[...]).astype(BF16)
            o_ref[...] = jnp.zeros_like(o_ref)

        for t in range(n_tiles):
            rows = pl.ds(t * FFN_ROW_TILE, FFN_ROW_TILE)
            h = h_scr[rows, :]
            g = _dot(h, wg_ref[...])
            u = _dot(h, wu_ref[...])
            act = g * _sigmoid(g) * u
            o_ref[rows, :] += _dot(act, wd_ref[...])

        @pl.when(j == N_DEV - 1)
        def _():
            o_ref[...] = x_ref[...] + FFN_RES_WEIGHT * o_ref[...]

    full = pl.BlockSpec((t_dim, d_dim), lambda j: (0, 0))
    return _call(
        body, name=name, grid=(N_DEV,),
        in_specs=[full, pl.BlockSpec((1, d_dim), lambda j: (0, 0)),
                  pl.BlockSpec((None, d_dim, D_FF_SHARD), lambda j: (j, 0, 0)),
                  pl.BlockSpec((None, d_dim, D_FF_SHARD), lambda j: (j, 0, 0)),
                  pl.BlockSpec((None, D_FF_SHARD, d_dim), lambda j: (j, 0, 0))],
        out_specs=full, out_shape=jax.ShapeDtypeStruct((t_dim, d_dim), F32),
        scratch_shapes=[pltpu.VMEM((t_dim, d_dim), BF16)], sem=("arbitrary",), args=[x, nw, wg, wu, wd], comm=comm)


def _ffn_bwd_prep(x, nw, dout, *, name, tm=256):
    t_dim, d_dim = x.shape
    row = pl.BlockSpec((tm, d_dim), lambda i: (i, 0))

    def body(x_ref, nw_ref, dout_ref, h_ref, dob_ref):
        xhat, _ = _rms(x_ref[...])
        h_ref[...] = (xhat * nw_ref[...]).astype(BF16)
        dob_ref[...] = (FFN_RES_WEIGHT * dout_ref[...]).astype(BF16)

    return pl.pallas_call(
        body, name=name, grid=(t_dim // tm,), in_specs=[row, pl.BlockSpec((1, d_dim), lambda i: (0, 0)), row],
        out_specs=[row, row], out_shape=[jax.ShapeDtypeStruct((t_dim, d_dim), BF16)] * 2,
        compiler_params=_params("parallel"),
    )(x, nw, dout)


def _ffn_bwd(h, dob, wg, wu, wd, *, name, comm=None):
    t_dim, d_dim = h.shape
    n_tiles = t_dim // FFN_ROW_TILE

    def body(h_ref, dob_ref, wg_ref, wu_ref, wd_ref, dh_ref, gg_ref, gu_ref, gd_ref, dwg_scr, dwu_scr, dwd_scr):
        j = pl.program_id(0)

        @pl.when(j == 0)
        def _():
            dh_ref[...] = jnp.zeros_like(dh_ref)

        for t in range(n_tiles):
            rows = pl.ds(t * FFN_ROW_TILE, FFN_ROW_TILE)
            hh = h_ref[rows, :]
            do = dob_ref[rows, :]
            g = _dot(hh, wg_ref[...])
            u = _dot(hh, wu_ref[...])
            sg = _sigmoid(g)
            s = g * sg
            da = _dot(do, wd_ref[...], NT)
            dwd = _dot(s * u, do, TN)
            du = (da * s).astype(BF16)
            dg = (da * u * _dsilu(g, sg)).astype(BF16)
            dwg = _dot(hh, dg, TN)
            dwu = _dot(hh, du, TN)
            if t == 0:
                dwd_scr[...] = dwd
                dwg_scr[...] = dwg
                dwu_scr[...] = dwu
            else:
                dwd_scr[...] += dwd
                dwg_scr[...] += dwg
                dwu_scr[...] += dwu
            dh_ref[rows, :] += _dot(dg, wg_ref[...], NT) + _dot(du, wu_ref[...], NT)
        gg_ref[...] = dwg_scr[...].astype(BF16)
        gu_ref[...] = dwu_scr[...].astype(BF16)
        gd_ref[...] = dwd_scr[...].astype(BF16)

    full_bf = pl.BlockSpec((t_dim, d_dim), lambda j: (0, 0))
    col = pl.BlockSpec((None, d_dim, D_FF_SHARD), lambda j: (j, 0, 0))
    rowb = pl.BlockSpec((None, D_FF_SHARD, d_dim), lambda j: (j, 0, 0))
    return _call(
        body, name=name, grid=(N_DEV,),
        in_specs=[full_bf, full_bf, col, col, rowb], out_specs=[full_bf, col, col, rowb],
        out_shape=[jax.ShapeDtypeStruct((t_dim, d_dim), F32), jax.ShapeDtypeStruct(wg.shape, BF16),
                   jax.ShapeDtypeStruct(wu.shape, BF16), jax.ShapeDtypeStruct(wd.shape, BF16)],
        scratch_shapes=[pltpu.VMEM((d_dim, D_FF_SHARD), F32), pltpu.VMEM((d_dim, D_FF_SHARD), F32),
                        pltpu.VMEM((D_FF_SHARD, d_dim), F32)],
        sem=("arbitrary",), args=[h, dob, wg, wu, wd], comm=comm)


CONV_COLS = 256


def _shift_down(u, s, rows):
    return jnp.where(rows >= s, pltpu.roll(u, s, 0), 0.0)


def _shift_up(u, s, rows, t_dim):
    return jnp.where(rows < t_dim - s, pltpu.roll(u, t_dim - s, 0), 0.0)


def _conv_pre(u, w_ref, b_ref, rows):
    c = b_ref[...] + w_ref[CONV_WIDTH - 1:CONV_WIDTH, :] * u
    for k in range(CONV_WIDTH - 1):
        c = c + w_ref[k:k + 1, :] * _shift_down(u, CONV_WIDTH - 1 - k, rows)
    return c


def _conv_fwd(zx, cw, cb, *, name, comm=None):
    t_dim = zx.shape[0]
    off = D_INNER // CONV_COLS

    def body(u_ref, w_ref, b_ref, o_ref):
        rows = lax.broadcasted_iota(jnp.int32, (t_dim, CONV_COLS), 0)
        c = _conv_pre(u_ref[...], w_ref, b_ref, rows)
        o_ref[...] = c * _sigmoid(c)

    return _call(
        body, name=name, grid=(CONV_DIM // CONV_COLS,),
        in_specs=[pl.BlockSpec((t_dim, CONV_COLS), lambda j: (0, off + j)),
                  pl.BlockSpec((CONV_WIDTH, CONV_COLS), lambda j: (0, j)), pl.BlockSpec((1, CONV_COLS), lambda j: (0, j))],
        out_specs=pl.BlockSpec((t_dim, CONV_COLS), lambda j: (0, j)),
        out_shape=jax.ShapeDtypeStruct((t_dim, CONV_DIM), F32), sem=("parallel",), args=[zx, cw, cb], comm=comm)


def _conv_bwd(zx, cw, cb, dxs, db, dc, dzx, *, name):
    t_dim = zx.shape[0]
    off = D_INNER // CONV_COLS
    n_xs = D_INNER // CONV_COLS
    n_b = GN // CONV_COLS

    def body(u_ref, w_ref, b_ref, dxs_ref, db_ref, dc_ref, dzx_in, dzx_ref, dw_ref, dbias_ref):
        j = pl.program_id(0)
        rows = lax.broadcasted_iota(jnp.int32, (t_dim, CONV_COLS), 0)
        u = u_ref[...]
        c = _conv_pre(u, w_ref, b_ref, rows)
        d = jnp.where(j < n_xs, dxs_ref[...], jnp.where(j < n_xs + n_b, db_ref[...], dc_ref[...]))
        dcv = d * _dsilu(c, _sigmoid(c))
        dpre = w_ref[CONV_WIDTH - 1:CONV_WIDTH, :] * dcv
        dw_ref[CONV_WIDTH - 1:CONV_WIDTH, :] = jnp.sum(dcv * u, axis=0, keepdims=True)
        for k in range(CONV_WIDTH - 1):
            s = CONV_WIDTH - 1 - k
            dpre = dpre + w_ref[k:k + 1, :] * _shift_up(dcv, s, rows, t_dim)
            dw_ref[k:k + 1, :] = jnp.sum(dcv * _shift_down(u, s, rows), axis=0, keepdims=True)
        dzx_ref[...] = dpre
        dbias_ref[...] = jnp.sum(dcv, axis=0, keepdims=True)

    blk = lambda n: pl.BlockSpec((t_dim, CONV_COLS), n)
    return pl.pallas_call(
        body, name=name, grid=(CONV_DIM // CONV_COLS,),
        in_specs=[blk(lambda j: (0, off + j)), pl.BlockSpec((CONV_WIDTH, CONV_COLS), lambda j: (0, j)),
                  pl.BlockSpec((1, CONV_COLS), lambda j: (0, j)),
                  blk(lambda j: (0, jnp.minimum(j, n_xs - 1))),
                  blk(lambda j: (0, jnp.clip(j - n_xs, 0, n_b - 1))),
                  blk(lambda j: (0, jnp.clip(j - n_xs - n_b, 0, n_b - 1))),
                  pl.BlockSpec(memory_space=pl.ANY)],
        out_specs=[blk(lambda j: (0, off + j)), pl.BlockSpec((CONV_WIDTH, CONV_COLS), lambda j: (0, j)),
                   pl.BlockSpec((1, CONV_COLS), lambda j: (0, j))],
        out_shape=[jax.ShapeDtypeStruct(dzx.shape, F32), jax.ShapeDtypeStruct((CONV_WIDTH, CONV_DIM), F32),
                   jax.ShapeDtypeStruct((1, CONV_DIM), F32)],
        input_output_aliases={6: 0},
        compiler_params=_params("parallel"),
    )(zx, cw, cb, dxs, db, dc, dzx)


def _softplus_parts(x):
    e = jnp.exp(-jnp.abs(x))
    u = 1.0 + e
    log1p_e = jnp.where(u == 1.0, e, jnp.log(u) * e / jnp.where(u == 1.0, 1.0, u - 1.0))
    return jnp.maximum(x, 0.0) + log1p_e


def _dt_prep(dtr, dt_bias, a_log, *, name):
    def body(dtr_ref, bias_ref, alog_ref, dt_ref, a_ref):
        dt = _softplus_parts(dtr_ref[...] + bias_ref[...])
        dt_ref[...] = dt
        a_ref[...] = dt * (-jnp.exp(alog_ref[...]))

    return pl.pallas_call(body, name=name, out_shape=[jax.ShapeDtypeStruct(dtr.shape, F32)] * 2,
                          compiler_params=_params())(dtr, dt_bias, a_log)


def _dt_bwd(dtr, dt_bias, a_log, dt, ddt, da, *, name):
    def body(dtr_ref, bias_ref, alog_ref, dt_ref, ddt_ref, da_ref, ddtr_ref, dbias_ref, dalog_ref):
        a_neg = -jnp.exp(alog_ref[...])
        da_v = da_ref[...]
        ddt_tot = ddt_ref[...] + da_v * a_neg
        ddtr = ddt_tot * _sigmoid(dtr_ref[...] + bias_ref[...])
        ddtr_ref[...] = ddtr
        dbias_ref[...] = jnp.sum(ddtr, axis=0, keepdims=True)
        dalog_ref[...] = jnp.sum(da_v * dt_ref[...], axis=0, keepdims=True) * a_neg

    return pl.pallas_call(
        body, name=name,
        out_shape=[jax.ShapeDtypeStruct(dtr.shape, F32), jax.ShapeDtypeStruct((1, SSM_HEADS), F32),
                   jax.ShapeDtypeStruct((1, SSM_HEADS), F32)],
        compiler_params=_params())(dtr, dt_bias, a_log, dt, ddt, da)


def _ssd_chunk_common(a, b_ref, c_ref):
    row = lax.broadcasted_iota(jnp.int32, (CHUNK, CHUNK), 0)
    col = lax.broadcasted_iota(jnp.int32, (CHUNK, CHUNK), 1)
    causal = col <= row
    lower = causal.astype(F32)
    upper = (col >= row).astype(F32)
    cs = _dot_f32(lower, a)
    cs_row = _dot_f32(a, upper, TN)
    bc = b_ref[...]
    cc = c_ref[...]
    cb = _dot(cc, bc, NT)
    return causal, upper, cs, cs_row, bc, cc, cb


def _ssd_specs():
    xs = pl.BlockSpec((CHUNK, HEADS_PER_GROUP * SSM_HEAD_DIM), lambda g, c: (c, g))
    bsp = pl.BlockSpec((CHUNK, SSM_STATE), lambda g, c: (c, D_INNER // SSM_STATE + g))
    csp = pl.BlockSpec((CHUNK, SSM_STATE), lambda g, c: (c, (D_INNER + GN) // SSM_STATE + g))
    per_head = pl.BlockSpec((None, CHUNK, HEADS_PER_GROUP), lambda g, c: (g, c, 0))
    dsk = pl.BlockSpec((None, 1, HEADS_PER_GROUP), lambda g, c: (g, 0, 0))
    return xs, bsp, csp, per_head, dsk


def _ssd_fwd(xbc, dtg, ag, dg, *, name, comm=None):
    t_dim = xbc.shape[0]

    def body(xs_ref, b_ref, c_ref, dt_ref, a_ref, d_ref, y_ref, st_ref, s_scr):
        @pl.when(pl.program_id(1) == 0)
        def _():
            s_scr[...] = jnp.zeros_like(s_scr)

        causal, _, cs, cs_row, bc, cc, cb = _ssd_chunk_common(a_ref[...], b_ref, c_ref)
        for hh in range(HEADS_PER_GROUP):
            cols = slice(hh * SSM_HEAD_DIM, (hh + 1) * SSM_HEAD_DIM)
            csl = cs[:, hh:hh + 1]
            cl = cs[CHUNK - 1:CHUNK, hh:hh + 1]
            decay = jnp.exp(jnp.where(causal, csl - cs_row[hh:hh + 1, :], NEG_BIG))
            xs_h = xs_ref[:, cols]
            xdt = xs_h * dt_ref[:, hh:hh + 1]
            prev = s_scr[hh]
            y = _dot(cb * decay, xdt) + jnp.exp(csl) * _dot(cc, prev, NT) + xs_h * d_ref[:, hh:hh + 1]
            y_ref[:, cols] = y
            st_ref[hh] = prev
            s_scr[hh] = jnp.exp(cl) * prev + _dot(xdt * jnp.exp(cl - csl), bc, TN)

    xs, bsp, csp, per_head, dsk = _ssd_specs()
    return _call(
        body, name=name, grid=(SSM_GROUPS, N_CHUNKS),
        in_specs=[xs, bsp, csp, per_head, per_head, dsk],
        out_specs=[pl.BlockSpec((CHUNK, HEADS_PER_GROUP * SSM_HEAD_DIM), lambda g, c: (c, g)),
                   pl.BlockSpec((None, HEADS_PER_GROUP, SSM_HEAD_DIM, SSM_STATE), lambda g, c: (c, g, 0, 0))],
        out_shape=[jax.ShapeDtypeStruct((t_dim, D_INNER), F32),
                   jax.ShapeDtypeStruct((N_CHUNKS, SSM_HEADS, SSM_HEAD_DIM, SSM_STATE), F32)],
        scratch_shapes=[pltpu.VMEM((HEADS_PER_GROUP, SSM_HEAD_DIM, SSM_STATE), F32)],
        sem=("parallel", "arbitrary"), args=[xbc, xbc, xbc, dtg, ag, dg], comm=comm)


def _ssd_bwd(xbc, dtg, ag, dg, states, dy, *, name, comm=None):
    t_dim = xbc.shape[0]
    last = N_CHUNKS - 1

    def body(xs_ref, b_ref, c_ref, dt_ref, a_ref, d_ref, st_ref, dy_ref,
             dxs_ref, db_ref, dc_ref, ddt_ref, da_ref, dd_ref, ds_scr):
        @pl.when(pl.program_id(1) == 0)
        def _():
            ds_scr[...] = jnp.zeros_like(ds_scr)
            dd_ref[...] = jnp.zeros_like(dd_ref)

        causal, upper, cs, cs_row, bc, cc, cb = _ssd_chunk_common(a_ref[...], b_ref, c_ref)
        lane8 = lax.broadcasted_iota(jnp.int32, (CHUNK, HEADS_PER_GROUP), 1)
        sub8 = lax.broadcasted_iota(jnp.int32, (HEADS_PER_GROUP, CHUNK), 0)
        lane8_row = lax.broadcasted_iota(jnp.int32, (1, HEADS_PER_GROUP), 1)
        is_last = lax.broadcasted_iota(jnp.int32, (CHUNK, 1), 0) == CHUNK - 1
        col_acc = jnp.zeros((CHUNK, HEADS_PER_GROUP), F32)
        row_acc = jnp.zeros((HEADS_PER_GROUP, CHUNK), F32)
        ddt_acc = jnp.zeros((CHUNK, HEADS_PER_GROUP), F32)
        dd_acc = jnp.zeros((1, HEADS_PER_GROUP), F32)
        d_b = jnp.zeros((CHUNK, SSM_STATE), F32)
        d_c = jnp.zeros((CHUNK, SSM_STATE), F32)
        for hh in range(HEADS_PER_GROUP):
            cols = slice(hh * SSM_HEAD_DIM, (hh + 1) * SSM_HEAD_DIM)
            csl = cs[:, hh:hh + 1]
            cl = cs[CHUNK - 1:CHUNK, hh:hh + 1]
            decay = jnp.exp(jnp.where(causal, csl - cs_row[hh:hh + 1, :], NEG_BIG))
            e_out = jnp.exp(csl)
            e_st = jnp.exp(cl - csl)
            e_ch = jnp.exp(cl)
            xs_h = xs_ref[:, cols]
            dt_h = dt_ref[:, hh:hh + 1]
            xdt = xs_h * dt_h
            m = cb * decay
            prev = st_ref[hh]
            d_s = ds_scr[hh]
            dyh = dy_ref[:, cols]
            g1 = _dot(bc, d_s, NT)
            dxdt = _dot(m, dyh, TN) + e_st * g1
            d_m = _dot(dyh, xdt, NT)
            d_cb = d_m * decay
            w = d_m * m
            cp = _dot(cc, prev, NT)
            tl = jnp.sum(xdt * g1, axis=1, keepdims=True) * e_st
            col_part = jnp.sum(w, axis=1, keepdims=True) + e_out * jnp.sum(dyh * cp, axis=1, keepdims=True) - tl
            last_add = jnp.sum(tl, axis=0, keepdims=True) + e_ch * _sum_all(d_s * prev)
            col_part = col_part + jnp.where(is_last, last_add, 0.0)
            row_part = jnp.sum(w, axis=0, keepdims=True)
            d_c = d_c + _dot(d_cb, bc) + e_out * _dot(dyh, prev)
            d_b = d_b + _dot(d_cb, cc, TN) + _dot(xdt * e_st, d_s)
            ds_scr[hh] = e_ch * d_s + _dot(dyh * e_out, cc, TN)
            dxs_ref[:, cols] = dxdt * dt_h + dyh * d_ref[:, hh:hh + 1]
            col_acc = jnp.where(lane8 == hh, col_part, col_acc)
            row_acc = jnp.where(sub8 == hh, row_part, row_acc)
            ddt_acc = jnp.where(lane8 == hh, jnp.sum(dxdt * xs_h, axis=1, keepdims=True), ddt_acc)
            dd_acc = jnp.where(lane8_row == hh, _sum_all(dyh * xs_h), dd_acc)
        da_ref[...] = _dot_f32(upper, col_acc) - _dot_f32(upper, row_acc, NT)
        ddt_ref[...] = ddt_acc
        db_ref[...] = d_b
        dc_ref[...] = d_c
        dd_ref[...] += dd_acc

    rev = lambda c: last - c
    xs = pl.BlockSpec((CHUNK, HEADS_PER_GROUP * SSM_HEAD_DIM), lambda g, c: (rev(c), g))
    bsp = pl.BlockSpec((CHUNK, SSM_STATE), lambda g, c: (rev(c), D_INNER // SSM_STATE + g))
    csp = pl.BlockSpec((CHUNK, SSM_STATE), lambda g, c: (rev(c), (D_INNER + GN) // SSM_STATE + g))
    per_head = pl.BlockSpec((None, CHUNK, HEADS_PER_GROUP), lambda g, c: (g, rev(c), 0))
    dsk = pl.BlockSpec((None, 1, HEADS_PER_GROUP), lambda g, c: (g, 0, 0))
    st = pl.BlockSpec((None, HEADS_PER_GROUP, SSM_HEAD_DIM, SSM_STATE), lambda g, c: (rev(c), g, 0, 0))
    grp = pl.BlockSpec((CHUNK, SSM_STATE), lambda g, c: (rev(c), g))
    return _call(
        body, name=name, grid=(SSM_GROUPS, N_CHUNKS),
        in_specs=[xs, bsp, csp, per_head, per_head, dsk, st, xs],
        out_specs=[xs, grp, grp, per_head, per_head, dsk],
        out_shape=[jax.ShapeDtypeStruct((t_dim, D_INNER), F32), jax.ShapeDtypeStruct((t_dim, GN), F32),
                   jax.ShapeDtypeStruct((t_dim, GN), F32),
                   jax.ShapeDtypeStruct((SSM_GROUPS, t_dim, HEADS_PER_GROUP), F32),
                   jax.ShapeDtypeStruct((SSM_GROUPS, t_dim, HEADS_PER_GROUP), F32),
                   jax.ShapeDtypeStruct((SSM_GROUPS, 1, HEADS_PER_GROUP), F32)],
        scratch_shapes=[pltpu.VMEM((HEADS_PER_GROUP, SSM_HEAD_DIM, SSM_STATE), F32)],
        sem=("parallel", "arbitrary"), args=[xbc, xbc, xbc, dtg, ag, dg, states, dy], comm=comm)


NORM_GROUP = D_INNER // SSM_GROUPS


def _gate_norm_fwd(y, zx, nw, *, name, tm=256):
    t_dim = y.shape[0]
    row = pl.BlockSpec((tm, D_INNER), lambda i: (i, 0))

    def body(y_ref, z_ref, nw_ref, o_ref):
        z = z_ref[...]
        yz = y_ref[...] * (z * _sigmoid(z))
        for g in range(SSM_GROUPS):
            cols = slice(g * NORM_GROUP, (g + 1) * NORM_GROUP)
            yhat, _ = _rms(yz[:, cols])
            o_ref[:, cols] = (yhat * nw_ref[:, cols]).astype(BF16)

    return pl.pallas_call(
        body, name=name, grid=(t_dim // tm,), in_specs=[row, row, pl.BlockSpec((1, D_INNER), lambda i: (0, 0))],
        out_specs=row, out_shape=jax.ShapeDtypeStruct((t_dim, D_INNER), BF16),
        compiler_params=_params("parallel"),
    )(y, zx, nw)


def _gate_norm_bwd(y, zx, nw, dyn, *, name, tm=256):
    t_dim = y.shape[0]
    row = pl.BlockSpec((tm, D_INNER), lambda i: (i, 0))
    vec = pl.BlockSpec((1, D_INNER), lambda i: (0, 0))

    def body(y_ref, z_ref, nw_ref, dyn_ref, dy_ref, dz_ref, dnw_ref):
        @pl.when(pl.program_id(0) == 0)
        def _():
            dnw_ref[...] = jnp.zeros_like(dnw_ref)

        z = z_ref[...]
        yv = y_ref[...]
        sg = _sigmoid(z)
        silu_z = z * sg
        yz = yv * silu_z
        dyn_v = dyn_ref[...]
        for g in range(SSM_GROUPS):
            cols = slice(g * NORM_GROUP, (g + 1) * NORM_GROUP)
            yhat, r = _rms(yz[:, cols])
            dn = dyn_v[:, cols]
            dnw_ref[:, cols] += jnp.sum(dn * yhat, axis=0, keepdims=True)
            dyhat = dn * nw_ref[:, cols]
            dyz = r * (dyhat - yhat * jnp.mean(dyhat * yhat, axis=-1, keepdims=True))
            dy_ref[:, cols] = dyz * silu_z[:, cols]
            dz_ref[:, cols] = dyz * yv[:, cols] * _dsilu(z[:, cols], sg[:, cols])

    return pl.pallas_call(
        body, name=name, grid=(t_dim // tm,), in_specs=[row, row, vec, row],
        out_specs=[row, row, vec],
        out_shape=[jax.ShapeDtypeStruct((t_dim, D_INNER), F32), jax.ShapeDtypeStruct((t_dim, ZX_DIM), F32),
                   jax.ShapeDtypeStruct((1, D_INNER), F32)],
        compiler_params=_params("arbitrary"),
    )(y, zx, nw, dyn)


def _rope(t, cos2, sin2, *, name, tm=256):
    t_dim, width = t.shape
    half = ATT_HEAD_DIM // 2
    reps = width // 128

    def body(t_ref, cos_ref, sin_ref, o_ref):
        x = t_ref[...]
        lane = lax.broadcasted_iota(jnp.int32, (tm, width), 1)
        first = (lane % ATT_HEAD_DIM) < half
        rot = jnp.where(first, -pltpu.roll(x, width - half, 1), pltpu.roll(x, half, 1))
        o_ref[...] = x * jnp.tile(cos_ref[...], (1, reps)) + rot * jnp.tile(sin_ref[...], (1, reps))

    row = pl.BlockSpec((tm, width), lambda i: (i, 0))
    tab = pl.BlockSpec((tm, 128), lambda i: (i, 0))
    return pl.pallas_call(
        body, name=name, grid=(t_dim // tm,), in_specs=[row, tab, tab], out_specs=row,
        out_shape=jax.ShapeDtypeStruct((t_dim, width), F32), compiler_params=_params("parallel"),
    )(t, cos2, sin2)


def _attn_masks(n):
    row = lax.broadcasted_iota(jnp.int32, (WINDOW, WINDOW), 0)
    col = lax.broadcasted_iota(jnp.int32, (WINDOW, WINDOW), 1)
    return col <= row, (col > row) & (n > 0)


def _attn_fwd(q, k, v, sinks, *, name):
    t_dim = q.shape[0]

    def body(q_ref, kc_ref, kp_ref, vc_ref, vp_ref, s_ref, o_ref, l_ref):
        n = pl.program_id(0)
        mask_c, mask_p = _attn_masks(n)
        lane = lax.broadcasted_iota(jnp.int32, (WINDOW, N_Q_HEADS), 1)
        lse = jnp.zeros((WINDOW, N_Q_HEADS), F32)
        for kvh in range(N_KV_HEADS):
            kcols = slice(kvh * ATT_HEAD_DIM, (kvh + 1) * ATT_HEAD_DIM)
            kc, kp = kc_ref[:, kcols].astype(BF16), kp_ref[:, kcols].astype(BF16)
            vc, vp = vc_ref[:, kcols].astype(BF16), vp_ref[:, kcols].astype(BF16)
            for g in range(Q_PER_KV):
                h = kvh * Q_PER_KV + g
                cols = slice(h * ATT_HEAD_DIM, (h + 1) * ATT_HEAD_DIM)
                qh = q_ref[:, cols].astype(BF16)
                sc = jnp.where(mask_c, _dot(qh, kc, NT) * ATT_SCALE, NEG_BIG)
                sp = jnp.where(mask_p, _dot(qh, kp, NT) * ATT_SCALE, NEG_BIG)
                sink = s_ref[:, h:h + 1]
                m = jnp.maximum(jnp.maximum(jnp.max(sc, axis=1, keepdims=True), jnp.max(sp, axis=1, keepdims=True)), sink)
                pc = jnp.exp(sc - m)
                pp = jnp.exp(sp - m)
                den = jnp.sum(pc, axis=1, keepdims=True) + jnp.sum(pp, axis=1, keepdims=True) + jnp.exp(sink - m)
                o_ref[:, cols] = (_dot(pc, vc) + _dot(pp, vp)) / den
                lse = jnp.where(lane == h, m + jnp.log(den), lse)
        l_ref[...] = lse

    cur = lambda w: pl.BlockSpec((WINDOW, w), lambda n: (n, 0))
    prv = lambda w: pl.BlockSpec((WINDOW, w), lambda n: (jnp.maximum(n - 1, 0), 0))
    return pl.pallas_call(
        body, name=name, grid=(t_dim // WINDOW,),
        in_specs=[cur(D_MODEL), cur(KV_DIM), prv(KV_DIM), cur(KV_DIM), prv(KV_DIM), pl.BlockSpec((1, N_Q_HEADS), lambda n: (0, 0))],
        out_specs=[cur(D_MODEL), cur(N_Q_HEADS)],
        out_shape=[jax.ShapeDtypeStruct((t_dim, D_MODEL), F32), jax.ShapeDtypeStruct((t_dim, N_Q_HEADS), F32)],
        compiler_params=_params("parallel"),
    )(q, k, k, v, v, sinks)


def _attn_bwd(q, k, v, sinks, o, lse, do, *, name, comm=None):
    t_dim = q.shape[0]

    def body(q_ref, kc_ref, kp_ref, vc_ref, vp_ref, s_ref, o_ref, l_ref, do_ref, dq_ref, dk_ref, dv_ref, dsink_ref):
        n = pl.program_id(0)

        @pl.when(n == 0)
        def _():
            dk_ref[...] = jnp.zeros_like(dk_ref)
            dv_ref[...] = jnp.zeros_like(dv_ref)
            dsink_ref[...] = jnp.zeros_like(dsink_ref)

        mask_c, mask_p = _attn_masks(n)
        lane_row = lax.broadcasted_iota(jnp.int32, (1, N_Q_HEADS), 1)
        rows_c = pl.ds(pl.multiple_of(n * WINDOW, WINDOW), WINDOW)
        rows_p = pl.ds(pl.multiple_of(jnp.maximum(n - 1, 0) * WINDOW, WINDOW), WINDOW)
        dsink = jnp.zeros((1, N_Q_HEADS), F32)
        for kvh in range(N_KV_HEADS):
            kcols = slice(kvh * ATT_HEAD_DIM, (kvh + 1) * ATT_HEAD_DIM)
            kc, kp = kc_ref[:, kcols].astype(BF16), kp_ref[:, kcols].astype(BF16)
            vc, vp = vc_ref[:, kcols].astype(BF16), vp_ref[:, kcols].astype(BF16)
            dkc = jnp.zeros((WINDOW, ATT_HEAD_DIM), F32)
            dkp = jnp.zeros((WINDOW, ATT_HEAD_DIM), F32)
            dvc = jnp.zeros((WINDOW, ATT_HEAD_DIM), F32)
            dvp = jnp.zeros((WINDOW, ATT_HEAD_DIM), F32)
            for g in range(Q_PER_KV):
                h = kvh * Q_PER_KV + g
                cols = slice(h * ATT_HEAD_DIM, (h + 1) * ATT_HEAD_DIM)
                qh = q_ref[:, cols].astype(BF16)
                lh = l_ref[:, h:h + 1]
                pc = jnp.exp(jnp.where(mask_c, _dot(qh, kc, NT) * ATT_SCALE, NEG_BIG) - lh)
                pp = jnp.exp(jnp.where(mask_p, _dot(qh, kp, NT) * ATT_SCALE, NEG_BIG) - lh)
                doh = do_ref[:, cols]
                delta = jnp.sum(doh * o_ref[:, cols], axis=1, keepdims=True)
                dsc = pc * (_dot(doh, vc, NT) - delta)
                dsp = pp * (_dot(doh, vp, NT) - delta)
                dq_ref[:, cols] = (_dot(dsc, kc) + _dot(dsp, kp)) * ATT_SCALE
                dkc = dkc + _dot(dsc, qh, TN) * ATT_SCALE
                dkp = dkp + _dot(dsp, qh, TN) * ATT_SCALE
                dvc = dvc + _dot(pc, doh, TN)
                dvp = dvp + _dot(pp, doh, TN)
                p_sink = jnp.exp(s_ref[:, h:h + 1] - lh)
                dsink = jnp.where(lane_row == h, -jnp.sum(p_sink * delta, axis=0, keepdims=True), dsink)
            dk_ref[rows_c, kcols] += dkc
            dk_ref[rows_p, kcols] += dkp
            dv_ref[rows_c, kcols] += dvc
            dv_ref[rows_p, kcols] += dvp
        dsink_ref[...] += dsink

    cur = lambda w: pl.BlockSpec((WINDOW, w), lambda n: (n, 0))
    prv = lambda w: pl.BlockSpec((WINDOW, w), lambda n: (jnp.maximum(n - 1, 0), 0))
    whole = pl.BlockSpec((t_dim, KV_DIM), lambda n: (0, 0))
    svec = pl.BlockSpec((1, N_Q_HEADS), lambda n: (0, 0))
    return _call(
        body, name=name, grid=(t_dim // WINDOW,),
        in_specs=[cur(D_MODEL), cur(KV_DIM), prv(KV_DIM), cur(KV_DIM), prv(KV_DIM), svec, cur(D_MODEL), cur(N_Q_HEADS), cur(D_MODEL)],
        out_specs=[cur(D_MODEL), whole, whole, svec],
        out_shape=[jax.ShapeDtypeStruct((t_dim, D_MODEL), F32), jax.ShapeDtypeStruct((t_dim, KV_DIM), F32),
                   jax.ShapeDtypeStruct((t_dim, KV_DIM), F32), jax.ShapeDtypeStruct((1, N_Q_HEADS), F32)],
        sem=("arbitrary",), args=[q, k, k, v, v, sinks, o, lse, do], comm=comm)


def _loss_head(x, nw, target, *, name, tm=256):
    t_dim, d_dim = x.shape
    row = pl.BlockSpec((tm, d_dim), lambda i: (i, 0))
    vec = pl.BlockSpec((1, d_dim), lambda i: (0, 0))

    def body(x_ref, nw_ref, tgt_ref, loss_ref, dx_ref, dnw_ref):
        @pl.when(pl.program_id(0) == 0)
        def _():
            loss_ref[...] = jnp.zeros_like(loss_ref)
            dnw_ref[...] = jnp.zeros_like(dnw_ref)

        xhat, r = _rms(x_ref[...])
        err = xhat * nw_ref[...] - tgt_ref[...]
        loss_ref[...] += 0.5 * _sum_all(jnp.mean(err * err, axis=-1, keepdims=True))
        dy = err * (1.0 / d_dim)
        dnw_ref[...] += jnp.sum(dy * xhat, axis=0, keepdims=True)
        dxhat = dy * nw_ref[...]
        dx_ref[...] = r * (dxhat - xhat * jnp.mean(dxhat * xhat, axis=-1, keepdims=True))

    return pl.pallas_call(
        body, name=name, grid=(t_dim // tm,), in_specs=[row, vec, row],
        out_specs=[pl.BlockSpec((1, 1), lambda i: (0, 0)), row, vec],
        out_shape=[jax.ShapeDtypeStruct((1, 1), F32), jax.ShapeDtypeStruct((t_dim, d_dim), F32),
                   jax.ShapeDtypeStruct((1, d_dim), F32)],
        compiler_params=_params("arbitrary"),
    )(x, nw, target)


def _rope_tables():
    pos = jnp.arange(SEQ, dtype=F32)
    inv = 1.0 / (ROPE_THETA ** (jnp.arange(0, ATT_HEAD_DIM, 2, dtype=F32) / ATT_HEAD_DIM))
    ang = pos[:, None] * inv[None, :]
    cos, sin = jnp.cos(ang), jnp.sin(ang)
    return jnp.tile(cos, (1, 4)), jnp.tile(sin, (1, 4))


def _to_groups(t):
    return t.reshape(t.shape[0], SSM_GROUPS, HEADS_PER_GROUP).transpose(1, 0, 2)


def _from_groups(t):
    return t.transpose(1, 0, 2).reshape(t.shape[1], SSM_HEADS)


def _forward_backward(x0, target, net):
    w = net.w
    nw = [[w("norm_w")[l, i][None, :] for i in range(3)] for l in range(2)]
    cos2, sin2 = _rope_tables()
    ffn_norm = [nw[0][0], nw[0][2], nw[1][0], nw[1][2]]

    def ffn_f(x, blk):
        name = f"ffn_fwd{blk}"
        return _ffn_fwd(x, ffn_norm[blk], w(f"gate{blk}"), w(f"up{blk}"), w(f"down{blk}"), name=name, comm=net.carry(name))

    x1 = ffn_f(x0, 0)
    zx, h1 = _norm_mm(x1, nw[0][1], w("wzx"), None, name="ssm_in_proj", comm=net.carry("ssm_in_proj"))
    dtr = _mm(h1, w("wdt"), name="ssm_dt_proj")
    xbc = _conv_fwd(zx, w("conv_w"), w("conv_b"), name="ssm_conv_fwd", comm=net.carry("ssm_conv_fwd"))
    dt, a_dt = _dt_prep(dtr, w("dt_bias"), w("a_log"), name="ssm_dt_prep")
    dtg, ag, dg = _to_groups(dt), _to_groups(a_dt), w("d_skip").reshape(SSM_GROUPS, 1, HEADS_PER_GROUP)
    y_ssd, states = _ssd_fwd(xbc, dtg, ag, dg, name="ssd_fwd", comm=net.carry("ssd_fwd"))
    yn = _gate_norm_fwd(y_ssd, zx, w("ssm_norm_w"), name="ssm_gate_norm_fwd")
    x2 = _mm(yn, w("wout"), res=x1, name="ssm_out_proj", comm=net.carry("ssm_out_proj"))
    x3 = ffn_f(x2, 1)
    k_pre, hk = _norm_mm(x3, w("kv_norm_w"), w("wk"), w("b_k"), name="k_proj")
    v = _mm(hk, w("wv"), bias=w("b_v"), name="v_proj")
    k_rot = _rope(k_pre, cos2, sin2, name="k_rope")
    x4 = ffn_f(x3, 2)
    q_pre, h4 = _norm_mm(x4, nw[1][1], w("wq"), w("b_q"), name="q_proj")
    q_rot = _rope(q_pre, cos2, sin2, name="q_rope")
    att, lse = _attn_fwd(q_rot, k_rot, v, w("sinks"), name="attn_fwd")
    x5 = _mm(att, w("wo"), bias=w("b_o"), res=x4, name="attn_out_proj")
    x6 = ffn_f(x5, 3)
    loss, dx6, d_final = _loss_head(x6, w("final_norm_w"), target, name="loss_head")

    d_norm = [[None] * 3 for _ in range(2)]

    def ffn_b(x, dout, blk):
        h, dob = _ffn_bwd_prep(x, ffn_norm[blk], dout, name=f"ffn_bwd_prep{blk}")
        name = f"ffn_bwd{blk}"
        dh, gg, gu, gd = _ffn_bwd(h, dob, w(f"gate{blk}"), w(f"up{blk}"), w(f"down{blk}"), name=name, comm=net.carry(name))
        net.give(f"gate{blk}", gg)
        net.give(f"up{blk}", gu)
        net.give(f"down{blk}", gd)
        return _norm_bwd(x, ffn_norm[blk], dh, [dout], name=f"ffn_norm_bwd{blk}")

    by_rows = lambda g: g.reshape(N_DEV, g.shape[0] // N_DEV, g.shape[1])
    dx5, d_norm[1][2] = ffn_b(x5, dx6, 3)
    d_att = _mm(dx5, w("wo"), dims="nt", name="attn_out_proj_dx")
    net.give("w_o", by_rows(_mm(att, dx5, dims="tn", out_dtype=BF16, name="attn_out_proj_dw")))
    d_bo = _colsum(dx5, name="attn_bo_grad")
    dq_rot, dk_rot, dv, d_sinks = _attn_bwd(q_rot, k_rot, v, w("sinks"), att, lse, d_att, name="attn_bwd", comm=net.carry("attn_bwd"))
    dq = _rope(dq_rot, cos2, -sin2, name="q_rope_bwd")
    dk = _rope(dk_rot, cos2, -sin2, name="k_rope_bwd")
    dh4 = _mm(dq, w("wq"), dims="nt", name="q_proj_dx")
    net.give("w_q", by_rows(_mm(h4, dq, dims="tn", out_dtype=BF16, name="q_proj_dw")))
    d_bq = _colsum(dq, name="attn_bq_grad")
    dx4, d_norm[1][1] = _norm_bwd(x4, nw[1][1], dh4, [dx5], name="attn_norm_bwd")
    dx3a, d_norm[1][0] = ffn_b(x3, dx4, 2)
    dhk = _mm(dk, w("wk"), dims="nt", name="k_proj_dx")
    dhk = _mm(dv, w("wv"), dims="nt", res=dhk, name="v_proj_dx")
    net.give("w_k", by_rows(_mm(hk, dk, dims="tn", out_dtype=BF16, name="k_proj_dw")))
    net.give("w_v", by_rows(_mm(hk, dv, dims="tn", out_dtype=BF16, name="v_proj_dw")))
    d_bk = _colsum(dk, name="bk_grad")
    d_bv = _colsum(dv, name="bv_grad")
    dx3, d_kvn = _norm_bwd(x3, w("kv_norm_w"), dhk, [dx3a], name="kv_norm_bwd")
    dx2, d_norm[0][2] = ffn_b(x2, dx3, 1)
    d_yn = _mm(dx2, w("wout"), dims="nt", name="ssm_out_proj_dx")
    net.give("w_out", by_rows(_mm(yn, dx2, dims="tn", out_dtype=BF16, name="ssm_out_proj_dw")))
    dy_ssd, dzx, d_ssm_norm = _gate_norm_bwd(y_ssd, zx, w("ssm_norm_w"), d_yn, name="ssm_gate_norm_bwd")
    dxs, d_b, d_c, ddtg, dag, ddg = _ssd_bwd(xbc, dtg, ag, dg, states, dy_ssd, name="ssd_bwd", comm=net.carry("ssd_bwd"))
    dzx, d_conv_w, d_conv_b = _conv_bwd(zx, w("conv_w"), w("conv_b"), dxs, d_b, d_c, dzx, name="ssm_conv_bwd")
    ddtr, d_dt_bias, d_a_log = _dt_bwd(dtr, w("dt_bias"), w("a_log"), dt, _from_groups(ddtg), _from_groups(dag), name="ssm_dt_bwd")
    dh1 = _mm(dzx, w("wzx"), dims="nt", name="ssm_in_proj_dx")
    dh1 = _mm(ddtr, w("wdt"), dims="nt", res=dh1, name="ssm_dt_proj_dx")
    g_zx = _mm(h1, dzx, dims="tn", out_dtype=BF16, name="ssm_in_proj_dw")
    g_dt = _mm(h1, ddtr, dims="tn", out_dtype=BF16, name="ssm_dt_proj_dw")
    net.give("w_in", jnp.concatenate([g_zx, g_dt], axis=1).reshape(D_MODEL, N_DEV, IN_PROJ_SHARD).transpose(1, 0, 2))
    dx1, d_norm[0][1] = _norm_bwd(x1, nw[0][1], dh1, [dx2], name="ssm_norm_bwd")
    dx0, d_norm[0][0] = ffn_b(x0, dx1, 0)

    small = {"norm_w": jnp.concatenate([d_norm[l][i] for l in range(2) for i in range(3)], axis=0),
             "ssm_conv_w": d_conv_w, "ssm_conv_b": d_conv_b, "ssm_dt_bias": d_dt_bias, "ssm_a_log": d_a_log,
             "ssm_d": ddg.reshape(1, SSM_HEADS), "ssm_norm_w": d_ssm_norm, "kv_norm_w": d_kvn,
             "b_k": d_bk, "b_v": d_bv, "attn_b_q": d_bq, "attn_sinks": d_sinks, "attn_b_o": d_bo, "final_norm_w": d_final}
    return loss, dx0, small


BLOCK_BYTES = 1 << 20


def _row_tile(rows, cols):
    for t in (512, 256, 128, 64, 32, 16):
        if rows % t == 0 and t * cols * 4 <= BLOCK_BYTES:
            return t
    return rows


def _cast_bf16(x, blk, *, name):
    _, rows, cols = x.shape
    tm = _row_tile(rows, cols)

    def body(x_ref, o_ref):
        o_ref[...] = x_ref[...].astype(BF16)

    return pl.pallas_call(body, name=name, grid=(rows // tm,), in_specs=[pl.BlockSpec((None, tm, cols), lambda i: (blk, i, 0))],
                          out_specs=pl.BlockSpec((tm, cols), lambda i: (i, 0)),
                          out_shape=jax.ShapeDtypeStruct((rows, cols), BF16), compiler_params=_params("parallel"))(x)


def _adam_update(g, w, m, v):
    m = ADAM_B1 * m + (1.0 - ADAM_B1) * g
    v = ADAM_B2 * v + (1.0 - ADAM_B2) * (g * g)
    m_hat = m / (1.0 - ADAM_B1 ** ADAM_STEP)
    v_hat = v / (1.0 - ADAM_B2 ** ADAM_STEP)
    delta = -ADAM_LR * (m_hat / (jnp.sqrt(v_hat) + ADAM_EPS) + ADAM_WD * w)
    return delta, m, v


def _adamw(parts, w, m, v, blk, prev, *, name):
    n_blk, rows, cols = w.shape
    tm = _row_tile(rows, cols)
    spec = pl.BlockSpec((None, tm, cols), lambda i: (blk, i, 0))
    n_prev = len(prev)

    def body(p_ref, w_ref, m_ref, v_ref, *refs):
        g_ref, d_ref, nm_ref, nv_ref = refs[n_prev:]
        g = p_ref[0].astype(F32)
        for s in range(1, N_DEV):
            g = g + p_ref[s].astype(F32)
        delta, nm, nv = _adam_update(g, w_ref[...], m_ref[...], v_ref[...])
        g_ref[...] = g
        d_ref[...] = delta
        nm_ref[...] = nm
        nv_ref[...] = nv

    return pl.pallas_call(
        body, name=name, grid=(rows // tm,),
        in_specs=[pl.BlockSpec((N_DEV, tm, cols), lambda i: (0, i, 0)), spec, spec, spec] + [pl.BlockSpec(memory_space=pl.ANY)] * n_prev,
        out_specs=[spec] * 4, out_shape=[jax.ShapeDtypeStruct((n_blk, rows, cols), F32)] * 4,
        input_output_aliases={4 + q: q for q in range(n_prev)},
        compiler_params=_params("parallel"),
    )(parts, w, m, v, *prev)


def _sum_parts(parts, *, name):
    def body(p_ref, o_ref):
        g = p_ref[0]
        for s in range(1, N_DEV):
            g = g + p_ref[s]
        o_ref[...] = g

    return pl.pallas_call(body, name=name, out_shape=jax.ShapeDtypeStruct(parts.shape[1:], F32), compiler_params=_params())(parts)


def _adamw_packed(g, w, m, v, *, name):
    def body(g_ref, w_ref, m_ref, v_ref, d_ref, nm_ref, nv_ref):
        delta, nm, nv = _adam_update(g_ref[...], w_ref[...], m_ref[...], v_ref[...])
        d_ref[...] = delta
        nm_ref[...] = nm
        nv_ref[...] = nv

    return pl.pallas_call(body, name=name, out_shape=[jax.ShapeDtypeStruct(g.shape, F32)] * 3, compiler_params=_params())(g, w, m, v)


LANES = 128
SUBLANES = 8


def _pack(arrs):
    rows = []
    for a in arrs:
        flat = a.reshape(-1)
        pad = (-flat.shape[0]) % LANES
        rows.append(jnp.pad(flat, (0, pad)).reshape(-1, LANES))
    out = jnp.concatenate(rows, axis=0)
    return jnp.pad(out, ((0, (-out.shape[0]) % SUBLANES), (0, 0)))


def _unpack(packed, shapes):
    outs, r = [], 0
    for shp in shapes:
        n = math.prod(shp)
        nr = -(-n // LANES)
        outs.append(packed[r:r + nr].reshape(-1)[:n].reshape(shp))
        r += nr
    return outs


WEIGHT_NAMES = ("norm_w", "ffn_w_gate", "ffn_w_up", "ffn_w_down", "ssm_w_in", "ssm_conv_w", "ssm_conv_b", "ssm_dt_bias",
                "ssm_a_log", "ssm_d", "ssm_norm_w", "ssm_w_out", "kv_norm_w", "w_k", "b_k", "w_v", "b_v", "attn_w_q",
                "attn_b_q", "attn_sinks", "attn_w_o", "attn_b_o", "final_norm_w")
MATRIX_NAMES = ("ffn_w_gate", "ffn_w_up", "ffn_w_down", "ssm_w_in", "ssm_w_out", "w_k", "w_v", "attn_w_q", "attn_w_o")
VECTOR_NAMES = tuple(n for n in WEIGHT_NAMES if n not in MATRIX_NAMES)
SHARDED_VECTORS = ("norm_w", "ssm_conv_w", "ssm_conv_b", "ssm_norm_w")


GATHER_PLAN = {
    "gather_stage0": ("gate0", "up0", "down0", "vec"),
    "ffn_fwd0": ("w_in",),
    "ssm_in_proj": ("w_out", "gate1"),
    "ssm_conv_fwd": ("w_k", "w_v"),
    "ssd_fwd": ("up1", "down1", "gate2", "up2"),
    "ssm_out_proj": ("w_q", "w_o"),
    "ffn_fwd1": ("down2", "gate3"),
    "ffn_fwd2": ("up3", "down3"),
}
GRAD_PLAN = {
    "attn_bwd": ("gate3",),
    "ffn_bwd2": ("up3", "down3"),
    "ffn_bwd1": ("w_q", "w_o", "gate2"),
    "ssd_bwd": ("up2", "down2", "w_k", "w_v", "gate1", "up1", "down1"),
    "ffn_bwd0": ("w_in", "w_out"),
    "exchange_last_grads": ("gate0", "up0", "down0"),
}
FFN_PARAMS = {"gate": "ffn_w_gate", "up": "ffn_w_up", "down": "ffn_w_down"}
SINGLE_MATRICES = {"w_in": "ssm_w_in", "w_out": "ssm_w_out", "w_k": "w_k", "w_v": "w_v", "w_q": "attn_w_q", "w_o": "attn_w_o"}


class _MeshNet:
    def __init__(self, p):
        self.p = p
        self.views = {n: p[n].reshape((-1,) + p[n].shape[-2:]) for n in MATRIX_NAMES}
        self.local = {"vec": _pack([p[n] for n in SHARDED_VECTORS])}
        for short, n in FFN_PARAMS.items():
            for k in range(N_FFN):
                self.local[f"{short}{k}"] = _cast_bf16(self.views[n], k, name=f"cast_{short}{k}")
        for short, n in SINGLE_MATRICES.items():
            self.local[short] = _cast_bf16(self.views[n], 0, name=f"cast_{short}")
        self.gathered_at, self.parts_at, self.grads, self.cache = {}, {}, {}, {}

    def carry(self, name):
        if name in GATHER_PLAN:
            keys, comm = GATHER_PLAN[name], _Gather([self.local[k] for k in GATHER_PLAN[name]])
            self.gathered_at.update({k: (comm, i) for i, k in enumerate(keys)})
            return comm
        if name in GRAD_PLAN:
            keys, comm = GRAD_PLAN[name], _AllToAll([self.grads[k] for k in GRAD_PLAN[name]])
            self.parts_at.update({k: (comm, i) for i, k in enumerate(keys)})
            return comm
        return None

    def run(self, name):
        _run_exchange(self.carry(name), name=name)

    def give(self, key, grad):
        self.grads[key] = grad

    def parts(self, key):
        comm, i = self.parts_at[key]
        return comm.results[i]

    def _gathered(self, key):
        comm, i = self.gathered_at[key]
        return comm.results[i]

    def _vec(self, r0, r1, lead):
        t = self._gathered("vec")[:, r0:r1, :].reshape(N_DEV, lead, -1)
        return t.transpose(1, 0, 2).reshape(lead, -1)

    def _derive(self, name):
        p = self.p
        if name[:-1] in FFN_PARAMS:
            return self._gathered(name)
        if name in ("wzx", "wdt"):
            w_in = self._gathered("w_in").transpose(1, 0, 2).reshape(D_MODEL, N_DEV * IN_PROJ_SHARD)
            return w_in[:, :ZX_DIM] if name == "wzx" else w_in[:, ZX_DIM:]
        by_rows = {"wout": "w_out", "wk": "w_k", "wv": "w_v", "wq": "w_q", "wo": "w_o"}
        if name in by_rows:
            g = self._gathered(by_rows[name])
            return g.reshape(N_DEV * g.shape[1], g.shape[2])
        vectors = {"norm_w": lambda: self._vec(0, 6, 6).reshape(2, 3, D_MODEL), "conv_w": lambda: self._vec(6, 18, CONV_WIDTH),
                   "conv_b": lambda: self._vec(18, 21, 1), "ssm_norm_w": lambda: self._vec(21, 23, 1)}
        if name in vectors:
            return vectors[name]()
        replicated = {"dt_bias": p["ssm_dt_bias"], "a_log": p["ssm_a_log"], "d_skip": p["ssm_d"], "kv_norm_w": p["kv_norm_w"][None],
                      "b_k": p["b_k"][None], "b_v": p["b_v"][None], "b_q": p["attn_b_q"], "sinks": p["attn_sinks"],
                      "b_o": p["attn_b_o"], "final_norm_w": p["final_norm_w"][None]}
        return replicated[name]

    def w(self, name):
        if name not in self.cache:
            self.cache[name] = self._derive(name)
        return self.cache[name]


def _step(x, target, p, m, v):
    pos = _slot(_position())
    net = _MeshNet(p)
    net.run("gather_stage0")
    loss, grad_x, small = _forward_backward(x, target, net)
    net.run("exchange_last_grads")
    vec_gather = _Gather([_pack([small[n] for n in VECTOR_NAMES])])
    vec_sum = _sum_parts(_run_exchange(vec_gather, name="gather_vector_grads")[0], name="sum_vector_grads")
    full_shapes = {"norm_w": (2, 3, D_MODEL), "ssm_conv_w": (1, CONV_WIDTH, CONV_DIM), "ssm_conv_b": (1, CONV_DIM),
                   "ssm_norm_w": (1, D_INNER)}
    vec_full = dict(zip(VECTOR_NAMES, _unpack(vec_sum, [full_shapes.get(n, p[n].shape) for n in VECTOR_NAMES])))

    grads, deltas, new_m, new_v = {}, {}, {}, {}
    view = lambda d, n: d[n].reshape(net.views[n].shape)
    for short, n in FFN_PARAMS.items():
        outs = []
        for k in reversed(range(N_FFN)):
            outs = _adamw(net.parts(f"{short}{k}"), net.views[n], view(m, n), view(v, n), k, outs, name=f"adamw_{short}{k}")
        grads[n], deltas[n], new_m[n], new_v[n] = [o.reshape(p[n].shape) for o in outs]
    for short, n in SINGLE_MATRICES.items():
        outs = _adamw(net.parts(short), net.views[n], view(m, n), view(v, n), 0, [], name=f"adamw_{short}")
        grads[n], deltas[n], new_m[n], new_v[n] = [o.reshape(p[n].shape) for o in outs]
    for n in VECTOR_NAMES:
        g = vec_full[n]
        if n in SHARDED_VECTORS:
            per = p[n].shape[-1]
            g = lax.dynamic_slice_in_dim(g, pos * per, per, axis=g.ndim - 1)
        grads[n] = g
    packed = _adamw_packed(*[_pack([d[n] for n in VECTOR_NAMES]) for d in (grads, p, m, v)], name="adamw_vectors")
    shapes = [p[n].shape for n in VECTOR_NAMES]
    for d, pk in zip((deltas, new_m, new_v), packed):
        d.update(zip(VECTOR_NAMES, _unpack(pk, shapes)))
    return loss, grad_x, grads, deltas, new_m, new_v


def kernel(x, norm_w, ffn_w_gate, ffn_w_up, ffn_w_down, ssm_w_in, ssm_conv_w, ssm_conv_b, ssm_dt_bias, ssm_a_log, ssm_d, ssm_norm_w, ssm_w_out, kv_norm_w, w_k, b_k, w_v, b_v, attn_w_q, attn_b_q, attn_sinks, attn_w_o, attn_b_o, final_norm_w, loss_target, m_norm_w, m_ffn_w_gate, m_ffn_w_up, m_ffn_w_down, m_ssm_w_in, m_ssm_conv_w, m_ssm_conv_b, m_ssm_dt_bias, m_ssm_a_log, m_ssm_d, m_ssm_norm_w, m_ssm_w_out, m_kv_norm_w, m_w_k, m_b_k, m_w_v, m_b_v, m_attn_w_q, m_attn_b_q, m_attn_sinks, m_attn_w_o, m_attn_b_o, m_final_norm_w, v_norm_w, v_ffn_w_gate, v_ffn_w_up, v_ffn_w_down, v_ssm_w_in, v_ssm_conv_w, v_ssm_conv_b, v_ssm_dt_bias, v_ssm_a_log, v_ssm_d, v_ssm_norm_w, v_ssm_w_out, v_kv_norm_w, v_w_k, v_b_k, v_w_v, v_b_v, v_attn_w_q, v_attn_b_q, v_attn_sinks, v_attn_w_o, v_attn_b_o, v_final_norm_w):
    p = dict(zip(WEIGHT_NAMES, (norm_w, ffn_w_gate, ffn_w_up, ffn_w_down, ssm_w_in, ssm_conv_w, ssm_conv_b, ssm_dt_bias, ssm_a_log, ssm_d, ssm_norm_w, ssm_w_out, kv_norm_w, w_k, b_k, w_v, b_v, attn_w_q, attn_b_q, attn_sinks, attn_w_o, attn_b_o, final_norm_w)))
    m = dict(zip(WEIGHT_NAMES, (m_norm_w, m_ffn_w_gate, m_ffn_w_up, m_ffn_w_down, m_ssm_w_in, m_ssm_conv_w, m_ssm_conv_b, m_ssm_dt_bias, m_ssm_a_log, m_ssm_d, m_ssm_norm_w, m_ssm_w_out, m_kv_norm_w, m_w_k, m_b_k, m_w_v, m_b_v, m_attn_w_q, m_attn_b_q, m_attn_sinks, m_attn_w_o, m_attn_b_o, m_final_norm_w)))
    v = dict(zip(WEIGHT_NAMES, (v_norm_w, v_ffn_w_gate, v_ffn_w_up, v_ffn_w_down, v_ssm_w_in, v_ssm_conv_w, v_ssm_conv_b, v_ssm_dt_bias, v_ssm_a_log, v_ssm_d, v_ssm_norm_w, v_ssm_w_out, v_kv_norm_w, v_w_k, v_b_k, v_w_v, v_b_v, v_attn_w_q, v_attn_b_q, v_attn_sinks, v_attn_w_o, v_attn_b_o, v_final_norm_w)))
    loss, grad_x, grads, deltas, new_m, new_v = _step(x[0], loss_target[0], p, m, v)
    loss = lax.psum(loss[0, 0], ("x", "y", "c"))
    return (loss, grad_x[None], *[grads[n] for n in WEIGHT_NAMES], *[deltas[n] for n in WEIGHT_NAMES],
            *[new_m[n] for n in WEIGHT_NAMES], *[new_v[n] for n in WEIGHT_NAMES])
```

```python
import functools
import math

import jax
import jax.numpy as jnp
from jax import lax
from jax.experimental import pallas as pl
from jax.experimental.pallas import tpu as pltpu

F32 = jnp.float32
BF16 = jnp.bfloat16

N_DEV = 8
SEQ = 2048
D_MODEL = 1024
D_FF_SHARD = 352
N_FFN = 4
D_INNER = 2048
SSM_HEADS = 32
SSM_HEAD_DIM = 64
SSM_GROUPS = 4
HEADS_PER_GROUP = 8
SSM_STATE = 128
CHUNK = 128
N_CHUNKS = SEQ // CHUNK
GN = SSM_GROUPS * SSM_STATE
CONV_DIM = D_INNER + 2 * GN
CONV_WIDTH = 4
ZX_DIM = D_INNER + CONV_DIM
IN_PROJ_SHARD = 644
ATT_HEAD_DIM = 64
N_Q_HEADS = 16
N_KV_HEADS = 4
Q_PER_KV = 4
KV_DIM = N_KV_HEADS * ATT_HEAD_DIM
WINDOW = 128
ROPE_THETA = 10000.0
EPS = 1e-5
FFN_RES_WEIGHT = 0.5
ATT_SCALE = 1.0 / math.sqrt(ATT_HEAD_DIM)
NEG_BIG = -1e30

ADAM_LR = 0.001
ADAM_B1 = 0.9
ADAM_B2 = 0.999
ADAM_EPS = 1e-08
ADAM_WD = 0.01
ADAM_STEP = 10

VMEM_LIMIT_BYTES = 56 * 1024 * 1024

NN = (((1,), (0,)), ((), ()))
NT = (((1,), (1,)), ((), ()))
TN = (((0,), (0,)), ((), ()))
_DIMS = {"nn": NN, "nt": NT, "tn": TN}


def _params(*sem):
    return pltpu.CompilerParams(dimension_semantics=sem if sem else None, vmem_limit_bytes=VMEM_LIMIT_BYTES)


def _dot(a, b, dims=NN):
    return lax.dot_general(a.astype(BF16), b.astype(BF16), dims, preferred_element_type=F32)


def _dot_f32(a, b, dims=NN):
    return lax.dot_general(a, b, dims, precision=lax.Precision.HIGHEST, preferred_element_type=F32)


def _sigmoid(x):
    return 1.0 / (1.0 + jnp.exp(-x))


def _dsilu(x, s):
    return s * (1.0 + x * (1.0 - s))


def _rms(x):
    r = lax.rsqrt(jnp.mean(x * x, axis=-1, keepdims=True) + EPS)
    return x * r, r


def _sum_all(x):
    return jnp.sum(jnp.sum(x, axis=1, keepdims=True), axis=0, keepdims=True)


MESH = pl.DeviceIdType.MESH
N_PEERS = N_DEV - 1


def _position():
    return lax.axis_index("x"), lax.axis_index("y"), lax.axis_index("c")


def _slot(p):
    return 4 * p[0] + 2 * p[1] + p[2]


class _Exchange:
    def __init__(self, arrays, out_shapes):
        n = len(arrays)
        self.arrays = list(arrays)
        self.out_shapes = out_shapes
        self.scratch = [pltpu.SemaphoreType.DMA((n, N_PEERS)), pltpu.SemaphoreType.DMA((n, N_PEERS)), pltpu.SemaphoreType.DMA((n,))]
        self.results = None


class _Gather(_Exchange):
    def __init__(self, arrays):
        super().__init__(arrays, [jax.ShapeDtypeStruct((N_DEV,) + a.shape, a.dtype) for a in arrays])

    def _plan(self, ins, outs, sems):
        send_sems, recv_sems, local_sems = sems
        x, y, c = _position()
        me, sibling = (x, y, c), (x, y, 1 - c)
        chips = [(1 - x, y), (x, 1 - y), (1 - x, 1 - y)]
        n = len(ins)

        def copy(a, k, block, to, src=None):
            dst = outs[a].at[_slot(block)]
            return pltpu.make_async_remote_copy(src_ref=dst if src is None else src, dst_ref=dst, send_sem=send_sems.at[a, k],
                                                recv_sem=recv_sems.at[a, k], device_id=to, device_id_type=MESH)

        mine = [pltpu.make_async_copy(ins[a], outs[a].at[_slot(me)], local_sems.at[a]) for a in range(n)]
        first = []
        for a in range(n):
            first.append(copy(a, 0, me, sibling, src=ins[a]))
            first += [copy(a, 1 + j, me, (*chip, c), src=ins[a]) for j, chip in enumerate(chips)]
        return n, c, me, sibling, chips, copy, mine, first

    def start(self, ins, outs, sems):
        _, _, _, _, _, _, mine, first = self._plan(ins, outs, sems)
        for cp in mine + first:
            cp.start()

    def finish(self, ins, outs, sems):
        n, c, me, sibling, chips, copy, mine, first = self._plan(ins, outs, sems)
        passed = []
        for j, chip in enumerate(chips):
            for a in range(n):
                copy(a, 1 + j, (*chip, c), me).wait_recv()
                fwd = copy(a, 4 + j, (*chip, c), sibling)
                fwd.start()
                passed.append(fwd)
        for a in range(n):
            copy(a, 0, sibling, me).wait_recv()
            for j, chip in enumerate(chips):
                copy(a, 4 + j, (*chip, 1 - c), me).wait_recv()
        for cp in first + passed:
            cp.wait_send()
        for cp in mine:
            cp.wait()


class _AllToAll(_Exchange):
    def __init__(self, arrays):
        super().__init__(arrays, [jax.ShapeDtypeStruct(a.shape, a.dtype) for a in arrays])

    def _plan(self, ins, outs, sems):
        send_sems, recv_sems, local_sems = sems
        x, y, c = _position()
        me = (x, y, c)
        peers = [(x ^ ((k >> 2) & 1), y ^ ((k >> 1) & 1), c ^ (k & 1)) for k in range(1, N_DEV)]
        n = len(ins)

        def copy(a, k, src_slot, dst_slot):
            return pltpu.make_async_remote_copy(src_ref=ins[a].at[_slot(src_slot)], dst_ref=outs[a].at[_slot(dst_slot)],
                                                send_sem=send_sems.at[a, k], recv_sem=recv_sems.at[a, k],
                                                device_id=peers[k], device_id_type=MESH)

        mine = [pltpu.make_async_copy(ins[a].at[_slot(me)], outs[a].at[_slot(me)], local_sems.at[a]) for a in range(n)]
        sends = [copy(a, k, peers[k], me) for a in range(n) for k in range(N_PEERS)]
        arrivals = lambda: [copy(a, k, me, peers[k]) for a in range(n) for k in range(N_PEERS)]
        return mine, sends, arrivals

    def start(self, ins, outs, sems):
        mine, sends, _ = self._plan(ins, outs, sems)
        for cp in mine + sends:
            cp.start()

    def finish(self, ins, outs, sems):
        mine, sends, arrivals = self._plan(ins, outs, sems)
        for cp in arrivals():
            cp.wait_recv()
        for cp in sends:
            cp.wait_send()
        for cp in mine:
            cp.wait()


def _call(body, *, name, grid, in_specs, out_specs, out_shape, args, scratch_shapes=(), sem=(), comm=None, aliases=None):
    single = not isinstance(out_shape, (list, tuple))
    out_shape = [out_shape] if single else list(out_shape)
    out_specs = [out_specs] if single else list(out_specs)
    n_in, n_out, n_scr = len(args), len(out_shape), len(scratch_shapes)
    params = pltpu.CompilerParams(dimension_semantics=tuple(sem) if sem else None, vmem_limit_bytes=VMEM_LIMIT_BYTES)
    if comm is None:
        res = pl.pallas_call(body, name=name, grid=grid, in_specs=list(in_specs), out_specs=out_specs, out_shape=out_shape,
                             scratch_shapes=list(scratch_shapes), input_output_aliases=aliases or {}, compiler_params=params)(*args)
        return res[0] if single else res
    n_ci, n_co = len(comm.arrays), len(comm.out_shapes)

    def carried(*refs):
        pos = 0
        parts = []
        for cnt in (n_in, n_ci, n_out, n_co, n_scr, len(comm.scratch)):
            parts.append(refs[pos:pos + cnt])
            pos += cnt
        ins, c_ins, outs, c_outs, scr, c_sems = parts
        ids = [pl.program_id(d) for d in range(len(grid))]
        is_first = functools.reduce(jnp.logical_and, [i == 0 for i in ids])
        is_last = functools.reduce(jnp.logical_and, [i == g - 1 for i, g in zip(ids, grid)])

        @pl.when(is_first)
        def _():
            comm.start(c_ins, c_outs, c_sems)

        body(*ins, *outs, *scr)

        @pl.when(is_last)
        def _():
            comm.finish(c_ins, c_outs, c_sems)

    anyspec = pl.BlockSpec(memory_space=pl.ANY)
    res = pl.pallas_call(
        carried, name=name, grid=grid, in_specs=list(in_specs) + [anyspec] * n_ci, out_specs=out_specs + [anyspec] * n_co,
        out_shape=out_shape + list(comm.out_shapes), scratch_shapes=list(scratch_shapes) + list(comm.scratch),
        input_output_aliases=aliases or {}, compiler_params=params)(*args, *comm.arrays)
    comm.results = list(res[n_out:])
    return res[0] if single else list(res[:n_out])


def _run_exchange(comm, *, name):
    def body(*refs):
        n_ci, n_co = len(comm.arrays), len(comm.out_shapes)
        ins, outs, sems = refs[:n_ci], refs[n_ci:n_ci + n_co], refs[n_ci + n_co:]
        comm.start(ins, outs, sems)
        comm.finish(ins, outs, sems)

    anyspec = pl.BlockSpec(memory_space=pl.ANY)
    comm.results = list(pl.pallas_call(
        body, name=name, in_specs=[anyspec] * len(comm.arrays), out_specs=[anyspec] * len(comm.out_shapes),
        out_shape=list(comm.out_shapes), scratch_shapes=list(comm.scratch))(*comm.arrays))
    return comm.results


def _mm(a, b, *, dims="nn", bias=None, res=None, out_dtype=F32, name, tm=512, tn=512, tk=1024, comm=None):
    if dims == "tn":
        k_dim, m_dim = a.shape
    else:
        m_dim, k_dim = a.shape
    n_dim = b.shape[0] if dims == "nt" else b.shape[1]
    tm, tn, tk = min(tm, m_dim), min(tn, n_dim), min(tk, k_dim)
    assert m_dim % tm == 0 and n_dim % tn == 0 and k_dim % tk == 0, (name, a.shape, b.shape)
    nk = k_dim // tk
    a_spec = pl.BlockSpec((tk, tm), lambda i, j, k: (k, i)) if dims == "tn" else pl.BlockSpec((tm, tk), lambda i, j, k: (i, k))
    b_spec = pl.BlockSpec((tn, tk), lambda i, j, k: (j, k)) if dims == "nt" else pl.BlockSpec((tk, tn), lambda i, j, k: (k, j))
    in_specs, args = [a_spec, b_spec], [a, b]
    if bias is not None:
        in_specs.append(pl.BlockSpec((1, tn), lambda i, j, k: (0, j)))
        args.append(bias)
    if res is not None:
        in_specs.append(pl.BlockSpec((tm, tn), lambda i, j, k: (i, j)))
        args.append(res)
    dn = _DIMS[dims]

    def body(*refs):
        a_ref, b_ref = refs[0], refs[1]
        o_ref, acc_ref = refs[-2], refs[-1]
        k = pl.program_id(2)

        @pl.when(k == 0)
        def _():
            acc_ref[...] = jnp.zeros_like(acc_ref)

        acc_ref[...] += _dot(a_ref[...], b_ref[...], dn)

        @pl.when(k == nk - 1)
        def _():
            r = acc_ref[...]
            pos = 2
            if bias is not None:
                r = r + refs[pos][...]
                pos += 1
            if res is not None:
                r = r + refs[pos][...]
            o_ref[...] = r.astype(out_dtype)

    return _call(
        body, name=name, grid=(m_dim // tm, n_dim // tn, nk), in_specs=in_specs,
        out_specs=pl.BlockSpec((tm, tn), lambda i, j, k: (i, j)),
        out_shape=jax.ShapeDtypeStruct((m_dim, n_dim), out_dtype),
        scratch_shapes=[pltpu.VMEM((tm, tn), F32)], sem=("parallel", "parallel", "arbitrary"), args=args, comm=comm)


def _norm_mm(x, nw, w, bias, *, name, tm=512, tn=512, comm=None):
    t_dim, d_dim = x.shape
    n_dim = w.shape[1]
    tn = min(tn, n_dim)
    assert t_dim % tm == 0 and n_dim % tn == 0
    has_bias = bias is not None
    in_specs = [pl.BlockSpec((tm, d_dim), lambda i, j: (i, 0)), pl.BlockSpec((1, d_dim), lambda i, j: (0, 0)),
                pl.BlockSpec((d_dim, tn), lambda i, j: (0, j))]
    args = [x, nw, w]
    if has_bias:
        in_specs.append(pl.BlockSpec((1, tn), lambda i, j: (0, j)))
        args.append(bias)

    def body(*refs):
        x_ref, nw_ref, w_ref = refs[:3]
        o_ref, h_ref = refs[-2], refs[-1]

        @pl.when(pl.program_id(1) == 0)
        def _():
            xhat, _ = _rms(x_ref[...])
            h_ref[...] = (xhat * nw_ref[...]).astype(BF16)

        r = _dot(h_ref[...], w_ref[...])
        if has_bias:
            r = r + refs[3][...]
        o_ref[...] = r

    return _call(
        body, name=name, grid=(t_dim // tm, n_dim // tn), in_specs=in_specs,
        out_specs=[pl.BlockSpec((tm, tn), lambda i, j: (i, j)), pl.BlockSpec((tm, d_dim), lambda i, j: (i, 0))],
        out_shape=[jax.ShapeDtypeStruct((t_dim, n_dim), F32), jax.ShapeDtypeStruct((t_dim, d_dim), BF16)],
        sem=("parallel", "arbitrary"), args=args, comm=comm)


def _norm_bwd(x, nw, dh, res, *, name, tm=256):
    t_dim, d_dim = x.shape
    n_res = len(res)
    row = pl.BlockSpec((tm, d_dim), lambda i: (i, 0))
    vec = pl.BlockSpec((1, d_dim), lambda i: (0, 0))

    def body(*refs):
        x_ref, nw_ref, dh_ref = refs[:3]
        dx_ref, dnw_ref = refs[-2], refs[-1]
        xhat, r = _rms(x_ref[...])
        dh = dh_ref[...]
        dxhat = dh * nw_ref[...]
        dx = r * (dxhat - xhat * jnp.mean(dxhat * xhat, axis=-1, keepdims=True))
        for rr in refs[3:3 + n_res]:
            dx = dx + rr[...]
        dx_ref[...] = dx

        @pl.when(pl.program_id(0) == 0)
        def _():
            dnw_ref[...] = jnp.zeros_like(dnw_ref)

        dnw_ref[...] += jnp.sum(dh * xhat, axis=0, keepdims=True)

    return pl.pallas_call(
        body, name=name, grid=(t_dim // tm,), in_specs=[row, vec, row] + [row] * n_res,
        out_specs=[row, vec],
        out_shape=[jax.ShapeDtypeStruct((t_dim, d_dim), F32), jax.ShapeDtypeStruct((1, d_dim), F32)],
        compiler_params=_params("arbitrary"),
    )(x, nw, dh, *res)


def _colsum(x, *, name, tm=256):
    t_dim, n_dim = x.shape

    def body(x_ref, o_ref):
        @pl.when(pl.program_id(0) == 0)
        def _():
            o_ref[...] = jnp.zeros_like(o_ref)

        o_ref[...] += jnp.sum(x_ref[...], axis=0, keepdims=True)

    return pl.pallas_call(
        body, name=name, grid=(t_dim // tm,), in_specs=[pl.BlockSpec((tm, n_dim), lambda i: (i, 0))],
        out_specs=pl.BlockSpec((1, n_dim), lambda i: (0, 0)), out_shape=jax.ShapeDtypeStruct((1, n_dim), F32),
        compiler_params=_params("arbitrary"),
    )(x)


FFN_ROW_TILE = 512
FFN_SHARDS_PER_STEP = 2
FFN_STEPS = N_DEV // FFN_SHARDS_PER_STEP
FFN_STEP_COLS = FFN_SHARDS_PER_STEP * D_FF_SHARD


def _ffn_step_weights(wg_ref, wu_ref, wd_ref, wg_scr, wu_scr):
    for s in range(FFN_SHARDS_PER_STEP):
        cols = slice(s * D_FF_SHARD, (s + 1) * D_FF_SHARD)
        wg_scr[:, cols] = wg_ref[s]
        wu_scr[:, cols] = wu_ref[s]
    return wg_scr[...], wu_scr[...], jnp.concatenate([wd_ref[s] for s in range(FFN_SHARDS_PER_STEP)], axis=0)


def _ffn_specs(d_dim):
    col = pl.BlockSpec((FFN_SHARDS_PER_STEP, d_dim, D_FF_SHARD), lambda j: (j, 0, 0))
    rowb = pl.BlockSpec((FFN_SHARDS_PER_STEP, D_FF_SHARD, d_dim), lambda j: (j, 0, 0))
    return col, rowb


def _ffn_fwd(x, nw, wg, wu, wd, *, name, comm=None):
    t_dim, d_dim = x.shape
    n_tiles = t_dim // FFN_ROW_TILE

    def body(x_ref, nw_ref, wg_ref, wu_ref, wd_ref, o_ref, h_scr, wg_scr, wu_scr):
        j = pl.program_id(0)

        @pl.when(j == 0)
        def _():
            xhat, _ = _rms(x_ref[...])
            h_scr[...] = (xhat * nw_ref[...]).astype(BF16)
            o_ref[...] = jnp.zeros_like(o_ref)

        w_gate, w_up, w_down = _ffn_step_weights(wg_ref, wu_ref, wd_ref, wg_scr, wu_scr)
        for t in range(n_tiles):
            rows = pl.ds(t * FFN_ROW_TILE, FFN_ROW_TILE)
            h = h_scr[rows, :]
            g = _dot(h, w_gate)
            u = _dot(h, w_up)
            act = g * _sigmoid(g) * u
            o_ref[rows, :] += _dot(act, w_down)

        @pl.when(j == FFN_STEPS - 1)
        def _():
            o_ref[...] = x_ref[...] + FFN_RES_WEIGHT * o_ref[...]

    full = pl.BlockSpec((t_dim, d_dim), lambda j: (0, 0))
    col, rowb = _ffn_specs(d_dim)
    return _call(
        body, name=name, grid=(FFN_STEPS,),
        in_specs=[full, pl.BlockSpec((1, d_dim), lambda j: (0, 0)), col, col, rowb],
        out_specs=full, out_shape=jax.ShapeDtypeStruct((t_dim, d_dim), F32),
        scratch_shapes=[pltpu.VMEM((t_dim, d_dim), BF16), pltpu.VMEM((d_dim, FFN_STEP_COLS), BF16),
                        pltpu.VMEM((d_dim, FFN_STEP_COLS), BF16)],
        sem=("arbitrary",), args=[x, nw, wg, wu, wd], comm=comm)


def _ffn_bwd_prep(x, nw, dout, *, name, tm=256):
    t_dim, d_dim = x.shape
    row = pl.BlockSpec((tm, d_dim), lambda i: (i, 0))

    def body(x_ref, nw_ref, dout_ref, h_ref, dob_ref):
        xhat, _ = _rms(x_ref[...])
        h_ref[...] = (xhat * nw_ref[...]).astype(BF16)
        dob_ref[...] = (FFN_RES_WEIGHT * dout_ref[...]).astype(BF16)

    return pl.pallas_call(
        body, name=name, grid=(t_dim // tm,), in_specs=[row, pl.BlockSpec((1, d_dim), lambda i: (0, 0)), row],
        out_specs=[row, row], out_shape=[jax.ShapeDtypeStruct((t_dim, d_dim), BF16)] * 2,
        compiler_params=_params("parallel"),
    )(x, nw, dout)


def _ffn_bwd(h, dob, wg, wu, wd, *, name, comm=None):
    t_dim, d_dim = h.shape
    n_tiles = t_dim // FFN_ROW_TILE

    def body(h_ref, dob_ref, wg_ref, wu_ref, wd_ref, dh_ref, gg_ref, gu_ref, gd_ref, dwg_scr, dwu_scr, dwd_scr, wg_scr, wu_scr):
        j = pl.program_id(0)

        @pl.when(j == 0)
        def _():
            dh_ref[...] = jnp.zeros_like(dh_ref)

        w_gate, w_up, w_down = _ffn_step_weights(wg_ref, wu_ref, wd_ref, wg_scr, wu_scr)
        for t in range(n_tiles):
            rows = pl.ds(t * FFN_ROW_TILE, FFN_ROW_TILE)
            hh = h_ref[rows, :]
            do = dob_ref[rows, :]
            g = _dot(hh, w_gate)
            u = _dot(hh, w_up)
            sg = _sigmoid(g)
            s = g * sg
            da = _dot(do, w_down, NT)
            dwd = _dot(s * u, do, TN)
            du = (da * s).astype(BF16)
            dg = (da * u * _dsilu(g, sg)).astype(BF16)
            dwg = _dot(hh, dg, TN)
            dwu = _dot(hh, du, TN)
            if t == 0:
                dwd_scr[...] = dwd
                dwg_scr[...] = dwg
                dwu_scr[...] = dwu
            else:
                dwd_scr[...] += dwd
                dwg_scr[...] += dwg
                dwu_scr[...] += dwu
            dh_ref[rows, :] += _dot(dg, w_gate, NT) + _dot(du, w_up, NT)
        for s in range(FFN_SHARDS_PER_STEP):
            cols = slice(s * D_FF_SHARD, (s + 1) * D_FF_SHARD)
            gg_ref[s] = dwg_scr[:, cols].astype(BF16)
            gu_ref[s] = dwu_scr[:, cols].astype(BF16)
            gd_ref[s] = dwd_scr[cols, :].astype(BF16)

    full_bf = pl.BlockSpec((t_dim, d_dim), lambda j: (0, 0))
    col, rowb = _ffn_specs(d_dim)
    return _call(
        body, name=name, grid=(FFN_STEPS,),
        in_specs=[full_bf, full_bf, col, col, rowb], out_specs=[full_bf, col, col, rowb],
        out_shape=[jax.ShapeDtypeStruct((t_dim, d_dim), F32), jax.ShapeDtypeStruct(wg.shape, BF16),
                   jax.ShapeDtypeStruct(wu.shape, BF16), jax.ShapeDtypeStruct(wd.shape, BF16)],
        scratch_shapes=[pltpu.VMEM((d_dim, FFN_STEP_COLS), F32), pltpu.VMEM((d_dim, FFN_STEP_COLS), F32),
                        pltpu.VMEM((FFN_STEP_COLS, d_dim), F32), pltpu.VMEM((d_dim, FFN_STEP_COLS), BF16),
                        pltpu.VMEM((d_dim, FFN_STEP_COLS), BF16)],
        sem=("arbitrary",), args=[h, dob, wg, wu, wd], comm=comm)


CONV_COLS = 256


def _shift_down(u, s, rows):
    return jnp.where(rows >= s, pltpu.roll(u, s, 0), 0.0)


def _shift_up(u, s, rows, t_dim):
    return jnp.where(rows < t_dim - s, pltpu.roll(u, t_dim - s, 0), 0.0)


def _conv_pre(u, w_ref, b_ref, rows):
    c = b_ref[...] + w_ref[CONV_WIDTH - 1:CONV_WIDTH, :] * u
    for k in range(CONV_WIDTH - 1):
        c = c + w_ref[k:k + 1, :] * _shift_down(u, CONV_WIDTH - 1 - k, rows)
    return c


def _conv_fwd(zx, cw, cb, *, name, comm=None):
    t_dim = zx.shape[0]
    off = D_INNER // CONV_COLS

    def body(u_ref, w_ref, b_ref, o_ref):
        rows = lax.broadcasted_iota(jnp.int32, (t_dim, CONV_COLS), 0)
        c = _conv_pre(u_ref[...], w_ref, b_ref, rows)
        o_ref[...] = c * _sigmoid(c)

    return _call(
        body, name=name, grid=(CONV_DIM // CONV_COLS,),
        in_specs=[pl.BlockSpec((t_dim, CONV_COLS), lambda j: (0, off + j)),
                  pl.BlockSpec((CONV_WIDTH, CONV_COLS), lambda j: (0, j)), pl.BlockSpec((1, CONV_COLS), lambda j: (0, j))],
        out_specs=pl.BlockSpec((t_dim, CONV_COLS), lambda j: (0, j)),
        out_shape=jax.ShapeDtypeStruct((t_dim, CONV_DIM), F32), sem=("parallel",), args=[zx, cw, cb], comm=comm)


def _conv_bwd(zx, cw, cb, dxs, db, dc, dzx, *, name):
    t_dim = zx.shape[0]
    off = D_INNER // CONV_COLS
    n_xs = D_INNER // CONV_COLS
    n_b = GN // CONV_COLS

    def body(u_ref, w_ref, b_ref, dxs_ref, db_ref, dc_ref, dzx_in, dzx_ref, dw_ref, dbias_ref):
        j = pl.program_id(0)
        rows = lax.broadcasted_iota(jnp.int32, (t_dim, CONV_COLS), 0)
        u = u_ref[...]
        c = _conv_pre(u, w_ref, b_ref, rows)
        d = jnp.where(j < n_xs, dxs_ref[...], jnp.where(j < n_xs + n_b, db_ref[...], dc_ref[...]))
        dcv = d * _dsilu(c, _sigmoid(c))
        dpre = w_ref[CONV_WIDTH - 1:CONV_WIDTH, :] * dcv
        dw_ref[CONV_WIDTH - 1:CONV_WIDTH, :] = jnp.sum(dcv * u, axis=0, keepdims=True)
        for k in range(CONV_WIDTH - 1):
            s = CONV_WIDTH - 1 - k
            dpre = dpre + w_ref[k:k + 1, :] * _shift_up(dcv, s, rows, t_dim)
            dw_ref[k:k + 1, :] = jnp.sum(dcv * _shift_down(u, s, rows), axis=0, keepdims=True)
        dzx_ref[...] = dpre
        dbias_ref[...] = jnp.sum(dcv, axis=0, keepdims=True)

    blk = lambda n: pl.BlockSpec((t_dim, CONV_COLS), n)
    return pl.pallas_call(
        body, name=name, grid=(CONV_DIM // CONV_COLS,),
        in_specs=[blk(lambda j: (0, off + j)), pl.BlockSpec((CONV_WIDTH, CONV_COLS), lambda j: (0, j)),
                  pl.BlockSpec((1, CONV_COLS), lambda j: (0, j)),
                  blk(lambda j: (0, jnp.minimum(j, n_xs - 1))),
                  blk(lambda j: (0, jnp.clip(j - n_xs, 0, n_b - 1))),
                  blk(lambda j: (0, jnp.clip(j - n_xs - n_b, 0, n_b - 1))),
                  pl.BlockSpec(memory_space=pl.ANY)],
        out_specs=[blk(lambda j: (0, off + j)), pl.BlockSpec((CONV_WIDTH, CONV_COLS), lambda j: (0, j)),
                   pl.BlockSpec((1, CONV_COLS), lambda j: (0, j))],
        out_shape=[jax.ShapeDtypeStruct(dzx.shape, F32), jax.ShapeDtypeStruct((CONV_WIDTH, CONV_DIM), F32),
                   jax.ShapeDtypeStruct((1, CONV_DIM), F32)],
        input_output_aliases={6: 0},
        compiler_params=_params("parallel"),
    )(zx, cw, cb, dxs, db, dc, dzx)


def _softplus_parts(x):
    e = jnp.exp(-jnp.abs(x))
    u = 1.0 + e
    log1p_e = jnp.where(u == 1.0, e, jnp.log(u) * e / jnp.where(u == 1.0, 1.0, u - 1.0))
    return jnp.maximum(x, 0.0) + log1p_e


def _dt_prep(dtr, dt_bias, a_log, *, name):
    def body(dtr_ref, bias_ref, alog_ref, dt_ref, a_ref):
        dt = _softplus_parts(dtr_ref[...] + bias_ref[...])
        dt_ref[...] = dt
        a_ref[...] = dt * (-jnp.exp(alog_ref[...]))

    return pl.pallas_call(body, name=name, out_shape=[jax.ShapeDtypeStruct(dtr.shape, F32)] * 2,
                          compiler_params=_params())(dtr, dt_bias, a_log)


def _dt_bwd(dtr, dt_bias, a_log, dt, ddt, da, *, name):
    def body(dtr_ref, bias_ref, alog_ref, dt_ref, ddt_ref, da_ref, ddtr_ref, dbias_ref, dalog_ref):
        a_neg = -jnp.exp(alog_ref[...])
        da_v = da_ref[...]
        ddt_tot = ddt_ref[...] + da_v * a_neg
        ddtr = ddt_tot * _sigmoid(dtr_ref[...] + bias_ref[...])
        ddtr_ref[...] = ddtr
        dbias_ref[...] = jnp.sum(ddtr, axis=0, keepdims=True)
        dalog_ref[...] = jnp.sum(da_v * dt_ref[...], axis=0, keepdims=True) * a_neg

    return pl.pallas_call(
        body, name=name,
        out_shape=[jax.ShapeDtypeStruct(dtr.shape, F32), jax.ShapeDtypeStruct((1, SSM_HEADS), F32),
                   jax.ShapeDtypeStruct((1, SSM_HEADS), F32)],
        compiler_params=_params())(dtr, dt_bias, a_log, dt, ddt, da)


GROUP_COLS = HEADS_PER_GROUP * SSM_HEAD_DIM
LANES = 128
HEADS_PER_LANE_BLOCK = LANES // SSM_HEAD_DIM


def _split3(x):
    hi = x.astype(BF16)
    r1 = x - hi.astype(F32)
    mid = r1.astype(BF16)
    lo = (r1 - mid.astype(F32)).astype(BF16)
    return hi, mid, lo


def _group_sums(vals, expand):
    x = jnp.concatenate(vals, axis=0)
    out = None
    for part in _split3(x):
        t = lax.dot_general(part, expand, NT, preferred_element_type=F32)
        out = t if out is None else out + t
    return [out[i * CHUNK:(i + 1) * CHUNK] for i in range(len(vals))]


def _ssd_chunk_common(a_ref, dt_ref, b_ref, c_ref):
    row = lax.broadcasted_iota(jnp.int32, (CHUNK, CHUNK), 0)
    col = lax.broadcasted_iota(jnp.int32, (CHUNK, CHUNK), 1)
    causal = col <= row
    lower = causal.astype(F32)
    upper = (col >= row).astype(F32)
    head = lax.broadcasted_iota(jnp.int32, (HEADS_PER_GROUP, GROUP_COLS), 0)
    lane = lax.broadcasted_iota(jnp.int32, (HEADS_PER_GROUP, GROUP_COLS), 1)
    expand = ((lane >= head * SSM_HEAD_DIM) & (lane < (head + 1) * SSM_HEAD_DIM)).astype(F32)
    a = a_ref[...]
    cs = _dot_f32(lower, a)
    cs_row = _dot_f32(a, upper, TN)
    cs_x = _dot_f32(cs, expand)
    dt_x = _dot_f32(dt_ref[...], expand)
    e_out_x = jnp.exp(cs_x)
    e_st_x = jnp.exp(cs_x[CHUNK - 1:CHUNK, :] - cs_x)
    bc = b_ref[...]
    cc = c_ref[...]
    cb = _dot(cc, bc, NT)
    return causal, upper, expand.astype(BF16), cs, cs_row, dt_x, e_out_x, e_st_x, bc, cc, cb


def _head_decay(causal, cs, cs_row, h):
    return jnp.exp(jnp.where(causal, cs[:, h:h + 1] - cs_row[h:h + 1, :], NEG_BIG))


def _lane_block_head_masks():
    lane = lax.broadcasted_iota(jnp.int32, (CHUNK, LANES), 1)
    return [(lane >= i * SSM_HEAD_DIM) & (lane < (i + 1) * SSM_HEAD_DIM) for i in range(HEADS_PER_LANE_BLOCK)]


def _decay_state(dst_ref, old, new, cs):
    for h in range(HEADS_PER_GROUP):
        rows = slice(h * SSM_HEAD_DIM, (h + 1) * SSM_HEAD_DIM)
        dst_ref[rows, :] = jnp.exp(cs[CHUNK - 1:CHUNK, h:h + 1]) * old[rows, :] + new[rows, :]


def _ssd_fwd(xbc, dtg, ag, dgx, *, name, comm=None):
    t_dim = xbc.shape[0]

    def body(xs_ref, b_ref, c_ref, dt_ref, a_ref, d_ref, y_ref, st_ref, s_scr):
        @pl.when(pl.program_id(1) == 0)
        def _():
            s_scr[...] = jnp.zeros_like(s_scr)

        causal, _, _, cs, cs_row, dt_x, e_out_x, e_st_x, bc, cc, cb = _ssd_chunk_common(a_ref, dt_ref, b_ref, c_ref)
        masks = _lane_block_head_masks()
        xs = xs_ref[...]
        xdt_x = xs * dt_x
        prev = s_scr[...]
        st_ref[...] = prev
        y_off = e_out_x * _dot(cc, prev, NT) + xs * d_ref[...]
        for blk in range(GROUP_COLS // LANES):
            lanes = slice(blk * LANES, (blk + 1) * LANES)
            x_b = xdt_x[:, lanes].astype(BF16)
            acc = y_off[:, lanes]
            for i in range(HEADS_PER_LANE_BLOCK):
                m = cb * _head_decay(causal, cs, cs_row, blk * HEADS_PER_LANE_BLOCK + i)
                acc = acc + _dot(m, jnp.where(masks[i], x_b, jnp.zeros_like(x_b)))
            y_ref[:, lanes] = acc
        _decay_state(s_scr, prev, _dot(xdt_x * e_st_x, bc, TN), cs)

    xs = pl.BlockSpec((CHUNK, GROUP_COLS), lambda g, c: (c, g))
    bsp = pl.BlockSpec((CHUNK, SSM_STATE), lambda g, c: (c, D_INNER // SSM_STATE + g))
    csp = pl.BlockSpec((CHUNK, SSM_STATE), lambda g, c: (c, (D_INNER + GN) // SSM_STATE + g))
    per_head = pl.BlockSpec((None, CHUNK, HEADS_PER_GROUP), lambda g, c: (g, c, 0))
    dsk = pl.BlockSpec((None, 1, GROUP_COLS), lambda g, c: (g, 0, 0))
    return _call(
        body, name=name, grid=(SSM_GROUPS, N_CHUNKS),
        in_specs=[xs, bsp, csp, per_head, per_head, dsk],
        out_specs=[xs, pl.BlockSpec((None, GROUP_COLS, SSM_STATE), lambda g, c: (c, g, 0))],
        out_shape=[jax.ShapeDtypeStruct((t_dim, D_INNER), F32),
                   jax.ShapeDtypeStruct((N_CHUNKS, D_INNER, SSM_STATE), F32)],
        scratch_shapes=[pltpu.VMEM((GROUP_COLS, SSM_STATE), F32)],
        sem=("parallel", "arbitrary"), args=[xbc, xbc, xbc, dtg, ag, dgx], comm=comm)


def _ssd_bwd(xbc, dtg, ag, dgx, states, dy, *, name, comm=None):
    t_dim = xbc.shape[0]
    last = N_CHUNKS - 1

    def body(xs_ref, b_ref, c_ref, dt_ref, a_ref, d_ref, st_ref, dy_ref,
             dxs_ref, db_ref, dc_ref, ddt_ref, da_ref, dd_ref, ds_scr):
        @pl.when(pl.program_id(1) == 0)
        def _():
            ds_scr[...] = jnp.zeros_like(ds_scr)
            dd_ref[...] = jnp.zeros_like(dd_ref)

        causal, upper, expand, cs, cs_row, dt_x, e_out_x, e_st_x, bc, cc, cb = _ssd_chunk_common(a_ref, dt_ref, b_ref, c_ref)
        masks = _lane_block_head_masks()
        xs = xs_ref[...]
        dy_x = dy_ref[...]
        xdt_x = xs * dt_x
        prev = st_ref[...]
        d_s = ds_scr[...]
        g1_x = _dot(bc, d_s, NT)
        cp_x = _dot(cc, prev, NT)
        d_cb = jnp.zeros((CHUNK, CHUNK), F32)
        lane8 = lax.broadcasted_iota(jnp.int32, (CHUNK, HEADS_PER_GROUP), 1)
        sub8 = lax.broadcasted_iota(jnp.int32, (HEADS_PER_GROUP, CHUNK), 0)
        row_w = jnp.zeros((CHUNK, HEADS_PER_GROUP), F32)
        col_w = jnp.zeros((HEADS_PER_GROUP, CHUNK), F32)
        dxdt_blocks = []
        for blk in range(GROUP_COLS // LANES):
            lanes = slice(blk * LANES, (blk + 1) * LANES)
            dy_b = dy_x[:, lanes].astype(BF16)
            x_b = xdt_x[:, lanes].astype(BF16)
            acc_dx = jnp.zeros((CHUNK, LANES), F32)
            for i in range(HEADS_PER_LANE_BLOCK):
                h = blk * HEADS_PER_LANE_BLOCK + i
                decay = _head_decay(causal, cs, cs_row, h)
                m = cb * decay
                dy_h = jnp.where(masks[i], dy_b, jnp.zeros_like(dy_b))
                acc_dx = acc_dx + _dot(m, dy_h, TN)
                d_m = _dot(dy_h, x_b, NT)
                d_cb = d_cb + d_m * decay
                w = d_m * m
                row_w = jnp.where(lane8 == h, jnp.sum(w, axis=1, keepdims=True), row_w)
                col_w = jnp.where(sub8 == h, jnp.sum(w, axis=0, keepdims=True), col_w)
            dxdt_blocks.append(acc_dx)
        dxdt_x = jnp.concatenate(dxdt_blocks, axis=1) + e_st_x * g1_x
        dxs_ref[...] = dxdt_x * dt_x + dy_x * d_ref[...]
        dye = dy_x * e_out_x
        xde = xdt_x * e_st_x
        ddt, y_off, tl, dskip = _group_sums([dxdt_x * xs, dye * cp_x, xde * g1_x, dy_x * xs], expand)
        ddt_ref[...] = ddt
        dd_ref[...] += jnp.sum(dskip, axis=0, keepdims=True)
        sp = None
        for part in _split3(d_s * prev):
            t = lax.dot_general(expand, part, NN, preferred_element_type=F32)
            sp = t if sp is None else sp + t
        last_col = jnp.exp(cs_row[:, CHUNK - 1:CHUNK]) * jnp.sum(sp, axis=1, keepdims=True)
        eye = lax.broadcasted_iota(jnp.int32, (HEADS_PER_GROUP, HEADS_PER_GROUP), 0) == lax.broadcasted_iota(
            jnp.int32, (HEADS_PER_GROUP, HEADS_PER_GROUP), 1)
        last_row = jnp.sum(jnp.where(eye, last_col, 0.0), axis=0, keepdims=True) + jnp.sum(tl, axis=0, keepdims=True)
        is_last = lax.broadcasted_iota(jnp.int32, (CHUNK, 1), 0) == CHUNK - 1
        d_cs = row_w + y_off - tl + jnp.where(is_last, last_row, 0.0)
        da_ref[...] = _dot_f32(upper, d_cs) - _dot_f32(upper, col_w, NT)
        dc_ref[...] = _dot(d_cb, bc) + _dot(dye, prev)
        db_ref[...] = _dot(d_cb, cc, TN) + _dot(xde, d_s)
        _decay_state(ds_scr, d_s, _dot(dye, cc, TN), cs)

    rev = lambda c: last - c
    xs = pl.BlockSpec((CHUNK, GROUP_COLS), lambda g, c: (rev(c), g))
    bsp = pl.BlockSpec((CHUNK, SSM_STATE), lambda g, c: (rev(c), D_INNER // SSM_STATE + g))
    csp = pl.BlockSpec((CHUNK, SSM_STATE), lambda g, c: (rev(c), (D_INNER + GN) // SSM_STATE + g))
    per_head = pl.BlockSpec((None, CHUNK, HEADS_PER_GROUP), lambda g, c: (g, rev(c), 0))
    dsk = pl.BlockSpec((None, 1, GROUP_COLS), lambda g, c: (g, 0, 0))
    dsum = pl.BlockSpec((None, 1, HEADS_PER_GROUP), lambda g, c: (g, 0, 0))
    st = pl.BlockSpec((None, GROUP_COLS, SSM_STATE), lambda g, c: (rev(c), g, 0))
    grp = pl.BlockSpec((CHUNK, SSM_STATE), lambda g, c: (rev(c), g))
    return _call(
        body, name=name, grid=(SSM_GROUPS, N_CHUNKS),
        in_specs=[xs, bsp, csp, per_head, per_head, dsk, st, xs],
        out_specs=[xs, grp, grp, per_head, per_head, dsum],
        out_shape=[jax.ShapeDtypeStruct((t_dim, D_INNER), F32), jax.ShapeDtypeStruct((t_dim, GN), F32),
                   jax.ShapeDtypeStruct((t_dim, GN), F32),
                   jax.ShapeDtypeStruct((SSM_GROUPS, t_dim, HEADS_PER_GROUP), F32),
                   jax.ShapeDtypeStruct((SSM_GROUPS, t_dim, HEADS_PER_GROUP), F32),
                   jax.ShapeDtypeStruct((SSM_GROUPS, 1, HEADS_PER_GROUP), F32)],
        scratch_shapes=[pltpu.VMEM((GROUP_COLS, SSM_STATE), F32)],
        sem=("parallel", "arbitrary"), args=[xbc, xbc, xbc, dtg, ag, dgx, states, dy], comm=comm)


NORM_GROUP = D_INNER // SSM_GROUPS


def _gate_norm_fwd(y, zx, nw, *, name, tm=256):
    t_dim = y.shape[0]
    row = pl.BlockSpec((tm, D_INNER), lambda i: (i, 0))

    def body(y_ref, z_ref, nw_ref, o_ref):
        z = z_ref[...]
        yz = y_ref[...] * (z * _sigmoid(z))
        for g in range(SSM_GROUPS):
            cols = slice(g * NORM_GROUP, (g + 1) * NORM_GROUP)
            yhat, _ = _rms(yz[:, cols])
            o_ref[:, cols] = (yhat * nw_ref[:, cols]).astype(BF16)

    return pl.pallas_call(
        body, name=name, grid=(t_dim // tm,), in_specs=[row, row, pl.BlockSpec((1, D_INNER), lambda i: (0, 0))],
        out_specs=row, out_shape=jax.ShapeDtypeStruct((t_dim, D_INNER), BF16),
        compiler_params=_params("parallel"),
    )(y, zx, nw)


def _gate_norm_bwd(y, zx, nw, dyn, *, name, tm=256):
    t_dim = y.shape[0]
    row = pl.BlockSpec((tm, D_INNER), lambda i: (i, 0))
    vec = pl.BlockSpec((1, D_INNER), lambda i: (0, 0))

    def body(y_ref, z_ref, nw_ref, dyn_ref, dy_ref, dz_ref, dnw_ref):
        @pl.when(pl.program_id(0) == 0)
        def _():
            dnw_ref[...] = jnp.zeros_like(dnw_ref)

        z = z_ref[...]
        yv = y_ref[...]
        sg = _sigmoid(z)
        silu_z = z * sg
        yz = yv * silu_z
        dyn_v = dyn_ref[...]
        for g in range(SSM_GROUPS):
            cols = slice(g * NORM_GROUP, (g + 1) * NORM_GROUP)
            yhat, r = _rms(yz[:, cols])
            dn = dyn_v[:, cols]
            dnw_ref[:, cols] += jnp.sum(dn * yhat, axis=0, keepdims=True)
            dyhat = dn * nw_ref[:, cols]
            dyz = r * (dyhat - yhat * jnp.mean(dyhat * yhat, axis=-1, keepdims=True))
            dy_ref[:, cols] = dyz * silu_z[:, cols]
            dz_ref[:, cols] = dyz * yv[:, cols] * _dsilu(z[:, cols], sg[:, cols])

    return pl.pallas_call(
        body, name=name, grid=(t_dim // tm,), in_specs=[row, row, vec, row],
        out_specs=[row, row, vec],
        out_shape=[jax.ShapeDtypeStruct((t_dim, D_INNER), F32), jax.ShapeDtypeStruct((t_dim, ZX_DIM), F32),
                   jax.ShapeDtypeStruct((1, D_INNER), F32)],
        compiler_params=_params("arbitrary"),
    )(y, zx, nw, dyn)


def _rope(t, cos2, sin2, *, name, tm=256):
    t_dim, width = t.shape
    half = ATT_HEAD_DIM // 2
    reps = width // 128

    def body(t_ref, cos_ref, sin_ref, o_ref):
        x = t_ref[...]
        lane = lax.broadcasted_iota(jnp.int32, (tm, width), 1)
        first = (lane % ATT_HEAD_DIM) < half
        rot = jnp.where(first, -pltpu.roll(x, width - half, 1), pltpu.roll(x, half, 1))
        o_ref[...] = x * jnp.tile(cos_ref[...], (1, reps)) + rot * jnp.tile(sin_ref[...], (1, reps))

    row = pl.BlockSpec((tm, width), lambda i: (i, 0))
    tab = pl.BlockSpec((tm, 128), lambda i: (i, 0))
    return pl.pallas_call(
        body, name=name, grid=(t_dim // tm,), in_specs=[row, tab, tab], out_specs=row,
        out_shape=jax.ShapeDtypeStruct((t_dim, width), F32), compiler_params=_params("parallel"),
    )(t, cos2, sin2)


def _attn_masks(n):
    row = lax.broadcasted_iota(jnp.int32, (WINDOW, WINDOW), 0)
    col = lax.broadcasted_iota(jnp.int32, (WINDOW, WINDOW), 1)
    return col <= row, (col > row) & (n > 0)


def _attn_fwd(q, k, v, sinks, *, name):
    t_dim = q.shape[0]

    def body(q_ref, kc_ref, kp_ref, vc_ref, vp_ref, s_ref, o_ref, l_ref):
        n = pl.program_id(0)
        mask_c, mask_p = _attn_masks(n)
        lane = lax.broadcasted_iota(jnp.int32, (WINDOW, N_Q_HEADS), 1)
        lse = jnp.zeros((WINDOW, N_Q_HEADS), F32)
        for kvh in range(N_KV_HEADS):
            kcols = slice(kvh * ATT_HEAD_DIM, (kvh + 1) * ATT_HEAD_DIM)
            kc, kp = kc_ref[:, kcols].astype(BF16), kp_ref[:, kcols].astype(BF16)
            vc, vp = vc_ref[:, kcols].astype(BF16), vp_ref[:, kcols].astype(BF16)
            for g in range(Q_PER_KV):
                h = kvh * Q_PER_KV + g
                cols = slice(h * ATT_HEAD_DIM, (h + 1) * ATT_HEAD_DIM)
                qh = q_ref[:, cols].astype(BF16)
                sc = jnp.where(mask_c, _dot(qh, kc, NT) * ATT_SCALE, NEG_BIG)
                sp = jnp.where(mask_p, _dot(qh, kp, NT) * ATT_SCALE, NEG_BIG)
                sink = s_ref[:, h:h + 1]
                m = jnp.maximum(jnp.maximum(jnp.max(sc, axis=1, keepdims=True), jnp.max(sp, axis=1, keepdims=True)), sink)
                pc = jnp.exp(sc - m)
                pp = jnp.exp(sp - m)
                den = jnp.sum(pc, axis=1, keepdims=True) + jnp.sum(pp, axis=1, keepdims=True) + jnp.exp(sink - m)
                o_ref[:, cols] = (_dot(pc, vc) + _dot(pp, vp)) / den
                lse = jnp.where(lane == h, m + jnp.log(den), lse)
        l_ref[...] = lse

    cur = lambda w: pl.BlockSpec((WINDOW, w), lambda n: (n, 0))
    prv = lambda w: pl.BlockSpec((WINDOW, w), lambda n: (jnp.maximum(n - 1, 0), 0))
    return pl.pallas_call(
        body, name=name, grid=(t_dim // WINDOW,),
        in_specs=[cur(D_MODEL), cur(KV_DIM), prv(KV_DIM), cur(KV_DIM), prv(KV_DIM), pl.BlockSpec((1, N_Q_HEADS), lambda n: (0, 0))],
        out_specs=[cur(D_MODEL), cur(N_Q_HEADS)],
        out_shape=[jax.ShapeDtypeStruct((t_dim, D_MODEL), F32), jax.ShapeDtypeStruct((t_dim, N_Q_HEADS), F32)],
        compiler_params=_params("parallel"),
    )(q, k, k, v, v, sinks)


def _attn_bwd(q, k, v, sinks, o, lse, do, *, name, comm=None):
    t_dim = q.shape[0]

    def body(q_ref, kc_ref, kp_ref, vc_ref, vp_ref, s_ref, o_ref, l_ref, do_ref, dq_ref, dk_ref, dv_ref, dsink_ref):
        n = pl.program_id(0)

        @pl.when(n == 0)
        def _():
            dk_ref[...] = jnp.zeros_like(dk_ref)
            dv_ref[...] = jnp.zeros_like(dv_ref)
            dsink_ref[...] = jnp.zeros_like(dsink_ref)

        mask_c, mask_p = _attn_masks(n)
        lane_row = lax.broadcasted_iota(jnp.int32, (1, N_Q_HEADS), 1)
        rows_c = pl.ds(pl.multiple_of(n * WINDOW, WINDOW), WINDOW)
        rows_p = pl.ds(pl.multiple_of(jnp.maximum(n - 1, 0) * WINDOW, WINDOW), WINDOW)
        dsink = jnp.zeros((1, N_Q_HEADS), F32)
        for kvh in range(N_KV_HEADS):
            kcols = slice(kvh * ATT_HEAD_DIM, (kvh + 1) * ATT_HEAD_DIM)
            kc, kp = kc_ref[:, kcols].astype(BF16), kp_ref[:, kcols].astype(BF16)
            vc, vp = vc_ref[:, kcols].astype(BF16), vp_ref[:, kcols].astype(BF16)
            dkc = jnp.zeros((WINDOW, ATT_HEAD_DIM), F32)
            dkp = jnp.zeros((WINDOW, ATT_HEAD_DIM), F32)
            dvc = jnp.zeros((WINDOW, ATT_HEAD_DIM), F32)
            dvp = jnp.zeros((WINDOW, ATT_HEAD_DIM), F32)
            for g in range(Q_PER_KV):
                h = kvh * Q_PER_KV + g
                cols = slice(h * ATT_HEAD_DIM, (h + 1) * ATT_HEAD_DIM)
                qh = q_ref[:, cols].astype(BF16)
                lh = l_ref[:, h:h + 1]
                pc = jnp.exp(jnp.where(mask_c, _dot(qh, kc, NT) * ATT_SCALE, NEG_BIG) - lh)
                pp = jnp.exp(jnp.where(mask_p, _dot(qh, kp, NT) * ATT_SCALE, NEG_BIG) - lh)
                doh = do_ref[:, cols]
                delta = jnp.sum(doh * o_ref[:, cols], axis=1, keepdims=True)
                dsc = pc * (_dot(doh, vc, NT) - delta)
                dsp = pp * (_dot(doh, vp, NT) - delta)
                dq_ref[:, cols] = (_dot(dsc, kc) + _dot(dsp, kp)) * ATT_SCALE
                dkc = dkc + _dot(dsc, qh, TN) * ATT_SCALE
                dkp = dkp + _dot(dsp, qh, TN) * ATT_SCALE
                dvc = dvc + _dot(pc, doh, TN)
                dvp = dvp + _dot(pp, doh, TN)
                p_sink = jnp.exp(s_ref[:, h:h + 1] - lh)
                dsink = jnp.where(lane_row == h, -jnp.sum(p_sink * delta, axis=0, keepdims=True), dsink)
            dk_ref[rows_c, kcols] += dkc
            dk_ref[rows_p, kcols] += dkp
            dv_ref[rows_c, kcols] += dvc
            dv_ref[rows_p, kcols] += dvp
        dsink_ref[...] += dsink

    cur = lambda w: pl.BlockSpec((WINDOW, w), lambda n: (n, 0))
    prv = lambda w: pl.BlockSpec((WINDOW, w), lambda n: (jnp.maximum(n - 1, 0), 0))
    whole = pl.BlockSpec((t_dim, KV_DIM), lambda n: (0, 0))
    svec = pl.BlockSpec((1, N_Q_HEADS), lambda n: (0, 0))
    return _call(
        body, name=name, grid=(t_dim // WINDOW,),
        in_specs=[cur(D_MODEL), cur(KV_DIM), prv(KV_DIM), cur(KV_DIM), prv(KV_DIM), svec, cur(D_MODEL), cur(N_Q_HEADS), cur(D_MODEL)],
        out_specs=[cur(D_MODEL), whole, whole, svec],
        out_shape=[jax.ShapeDtypeStruct((t_dim, D_MODEL), F32), jax.ShapeDtypeStruct((t_dim, KV_DIM), F32),
                   jax.ShapeDtypeStruct((t_dim, KV_DIM), F32), jax.ShapeDtypeStruct((1, N_Q_HEADS), F32)],
        sem=("arbitrary",), args=[q, k, k, v, v, sinks, o, lse, do], comm=comm)


def _loss_head(x, nw, target, *, name, tm=256):
    t_dim, d_dim = x.shape
    row = pl.BlockSpec((tm, d_dim), lambda i: (i, 0))
    vec = pl.BlockSpec((1, d_dim), lambda i: (0, 0))

    def body(x_ref, nw_ref, tgt_ref, loss_ref, dx_ref, dnw_ref):
        @pl.when(pl.program_id(0) == 0)
        def _():
            loss_ref[...] = jnp.zeros_like(loss_ref)
            dnw_ref[...] = jnp.zeros_like(dnw_ref)

        xhat, r = _rms(x_ref[...])
        err = xhat * nw_ref[...] - tgt_ref[...]
        loss_ref[...] += 0.5 * _sum_all(jnp.mean(err * err, axis=-1, keepdims=True))
        dy = err * (1.0 / d_dim)
        dnw_ref[...] += jnp.sum(dy * xhat, axis=0, keepdims=True)
        dxhat = dy * nw_ref[...]
        dx_ref[...] = r * (dxhat - xhat * jnp.mean(dxhat * xhat, axis=-1, keepdims=True))

    return pl.pallas_call(
        body, name=name, grid=(t_dim // tm,), in_specs=[row, vec, row],
        out_specs=[pl.BlockSpec((1, 1), lambda i: (0, 0)), row, vec],
        out_shape=[jax.ShapeDtypeStruct((1, 1), F32), jax.ShapeDtypeStruct((t_dim, d_dim), F32),
                   jax.ShapeDtypeStruct((1, d_dim), F32)],
        compiler_params=_params("arbitrary"),
    )(x, nw, target)


def _rope_tables():
    pos = jnp.arange(SEQ, dtype=F32)
    inv = 1.0 / (ROPE_THETA ** (jnp.arange(0, ATT_HEAD_DIM, 2, dtype=F32) / ATT_HEAD_DIM))
    ang = pos[:, None] * inv[None, :]
    cos, sin = jnp.cos(ang), jnp.sin(ang)
    return jnp.tile(cos, (1, 4)), jnp.tile(sin, (1, 4))


def _to_groups(t):
    return t.reshape(t.shape[0], SSM_GROUPS, HEADS_PER_GROUP).transpose(1, 0, 2)


def _from_groups(t):
    return t.transpose(1, 0, 2).reshape(t.shape[1], SSM_HEADS)


def _forward_backward(x0, target, net):
    w = net.w
    nw = [[w("norm_w")[l, i][None, :] for i in range(3)] for l in range(2)]
    cos2, sin2 = _rope_tables()
    ffn_norm = [nw[0][0], nw[0][2], nw[1][0], nw[1][2]]

    def ffn_f(x, blk):
        name = f"ffn_fwd{blk}"
        return _ffn_fwd(x, ffn_norm[blk], w(f"gate{blk}"), w(f"up{blk}"), w(f"down{blk}"), name=name, comm=net.carry(name))

    x1 = ffn_f(x0, 0)
    zx, h1 = _norm_mm(x1, nw[0][1], w("wzx"), None, name="ssm_in_proj", comm=net.carry("ssm_in_proj"))
    dtr = _mm(h1, w("wdt"), name="ssm_dt_proj")
    xbc = _conv_fwd(zx, w("conv_w"), w("conv_b"), name="ssm_conv_fwd", comm=net.carry("ssm_conv_fwd"))
    dt, a_dt = _dt_prep(dtr, w("dt_bias"), w("a_log"), name="ssm_dt_prep")
    dtg, ag = _to_groups(dt), _to_groups(a_dt)
    dg = jnp.repeat(w("d_skip").reshape(SSM_GROUPS, 1, HEADS_PER_GROUP), SSM_HEAD_DIM, axis=2)
    y_ssd, states = _ssd_fwd(xbc, dtg, ag, dg, name="ssd_fwd", comm=net.carry("ssd_fwd"))
    yn = _gate_norm_fwd(y_ssd, zx, w("ssm_norm_w"), name="ssm_gate_norm_fwd")
    x2 = _mm(yn, w("wout"), res=x1, name="ssm_out_proj", comm=net.carry("ssm_out_proj"))
    x3 = ffn_f(x2, 1)
    k_pre, hk = _norm_mm(x3, w("kv_norm_w"), w("wk"), w("b_k"), name="k_proj")
    v = _mm(hk, w("wv"), bias=w("b_v"), name="v_proj")
    k_rot = _rope(k_pre, cos2, sin2, name="k_rope")
    x4 = ffn_f(x3, 2)
    q_pre, h4 = _norm_mm(x4, nw[1][1], w("wq"), w("b_q"), name="q_proj")
    q_rot = _rope(q_pre, cos2, sin2, name="q_rope")
    att, lse = _attn_fwd(q_rot, k_rot, v, w("sinks"), name="attn_fwd")
    x5 = _mm(att, w("wo"), bias=w("b_o"), res=x4, name="attn_out_proj")
    x6 = ffn_f(x5, 3)
    loss, dx6, d_final = _loss_head(x6, w("final_norm_w"), target, name="loss_head")

    d_norm = [[None] * 3 for _ in range(2)]

    def ffn_b(x, dout, blk):
        h, dob = _ffn_bwd_prep(x, ffn_norm[blk], dout, name=f"ffn_bwd_prep{blk}")
        name = f"ffn_bwd{blk}"
        dh, gg, gu, gd = _ffn_bwd(h, dob, w(f"gate{blk}"), w(f"up{blk}"), w(f"down{blk}"), name=name, comm=net.carry(name))
        net.give(f"gate{blk}", gg)
        net.give(f"up{blk}", gu)
        net.give(f"down{blk}", gd)
        return _norm_bwd(x, ffn_norm[blk], dh, [dout], name=f"ffn_norm_bwd{blk}")

    by_rows = lambda g: g.reshape(N_DEV, g.shape[0] // N_DEV, g.shape[1])
    dx5, d_norm[1][2] = ffn_b(x5, dx6, 3)
    d_att = _mm(dx5, w("wo"), dims="nt", name="attn_out_proj_dx")
    net.give("w_o", by_rows(_mm(att, dx5, dims="tn", out_dtype=BF16, name="attn_out_proj_dw")))
    d_bo = _colsum(dx5, name="attn_bo_grad")
    dq_rot, dk_rot, dv, d_sinks = _attn_bwd(q_rot, k_rot, v, w("sinks"), att, lse, d_att, name="attn_bwd", comm=net.carry("attn_bwd"))
    dq = _rope(dq_rot, cos2, -sin2, name="q_rope_bwd")
    dk = _rope(dk_rot, cos2, -sin2, name="k_rope_bwd")
    dh4 = _mm(dq, w("wq"), dims="nt", name="q_proj_dx")
    net.give("w_q", by_rows(_mm(h4, dq, dims="tn", out_dtype=BF16, name="q_proj_dw")))
    d_bq = _colsum(dq, name="attn_bq_grad")
    dx4, d_norm[1][1] = _norm_bwd(x4, nw[1][1], dh4, [dx5], name="attn_norm_bwd")
    dx3a, d_norm[1][0] = ffn_b(x3, dx4, 2)
    dhk = _mm(dk, w("wk"), dims="nt", name="k_proj_dx")
    dhk = _mm(dv, w("wv"), dims="nt", res=dhk, name="v_proj_dx")
    net.give("w_k", by_rows(_mm(hk, dk, dims="tn", out_dtype=BF16, name="k_proj_dw")))
    net.give("w_v", by_rows(_mm(hk, dv, dims="tn", out_dtype=BF16, name="v_proj_dw")))
    d_bk = _colsum(dk, name="bk_grad")
    d_bv = _colsum(dv, name="bv_grad")
    dx3, d_kvn = _norm_bwd(x3, w("kv_norm_w"), dhk, [dx3a], name="kv_norm_bwd")
    dx2, d_norm[0][2] = ffn_b(x2, dx3, 1)
    d_yn = _mm(dx2, w("wout"), dims="nt", name="ssm_out_proj_dx")
    net.give("w_out", by_rows(_mm(yn, dx2, dims="tn", out_dtype=BF16, name="ssm_out_proj_dw")))
    dy_ssd, dzx, d_ssm_norm = _gate_norm_bwd(y_ssd, zx, w("ssm_norm_w"), d_yn, name="ssm_gate_norm_bwd")
    dxs, d_b, d_c, ddtg, dag, ddg = _ssd_bwd(xbc, dtg, ag, dg, states, dy_ssd, name="ssd_bwd", comm=net.carry("ssd_bwd"))
    dzx, d_conv_w, d_conv_b = _conv_bwd(zx, w("conv_w"), w("conv_b"), dxs, d_b, d_c, dzx, name="ssm_conv_bwd")
    ddtr, d_dt_bias, d_a_log = _dt_bwd(dtr, w("dt_bias"), w("a_log"), dt, _from_groups(ddtg), _from_groups(dag), name="ssm_dt_bwd")
    dh1 = _mm(dzx, w("wzx"), dims="nt", name="ssm_in_proj_dx")
    dh1 = _mm(ddtr, w("wdt"), dims="nt", res=dh1, name="ssm_dt_proj_dx")
    g_zx = _mm(h1, dzx, dims="tn", out_dtype=BF16, name="ssm_in_proj_dw")
    g_dt = _mm(h1, ddtr, dims="tn", out_dtype=BF16, name="ssm_dt_proj_dw")
    net.give("w_in", jnp.concatenate([g_zx, g_dt], axis=1).reshape(D_MODEL, N_DEV, IN_PROJ_SHARD).transpose(1, 0, 2))
    dx1, d_norm[0][1] = _norm_bwd(x1, nw[0][1], dh1, [dx2], name="ssm_norm_bwd")
    dx0, d_norm[0][0] = ffn_b(x0, dx1, 0)

    small = {"norm_w": jnp.concatenate([d_norm[l][i] for l in range(2) for i in range(3)], axis=0),
             "ssm_conv_w": d_conv_w, "ssm_conv_b": d_conv_b, "ssm_dt_bias": d_dt_bias, "ssm_a_log": d_a_log,
             "ssm_d": ddg.reshape(1, SSM_HEADS), "ssm_norm_w": d_ssm_norm, "kv_norm_w": d_kvn,
             "b_k": d_bk, "b_v": d_bv, "attn_b_q": d_bq, "attn_sinks": d_sinks, "attn_b_o": d_bo, "final_norm_w": d_final}
    return loss, dx0, small


BLOCK_BYTES = 1 << 20


def _row_tile(rows, cols):
    for t in (512, 256, 128, 64, 32, 16):
        if rows % t == 0 and t * cols * 4 <= BLOCK_BYTES:
            return t
    return rows


def _cast_bf16(x, blk, *, name):
    _, rows, cols = x.shape
    tm = _row_tile(rows, cols)

    def body(x_ref, o_ref):
        o_ref[...] = x_ref[...].astype(BF16)

    return pl.pallas_call(body, name=name, grid=(rows // tm,), in_specs=[pl.BlockSpec((None, tm, cols), lambda i: (blk, i, 0))],
                          out_specs=pl.BlockSpec((tm, cols), lambda i: (i, 0)),
                          out_shape=jax.ShapeDtypeStruct((rows, cols), BF16), compiler_params=_params("parallel"))(x)


def _adam_update(g, w, m, v):
    m = ADAM_B1 * m + (1.0 - ADAM_B1) * g
    v = ADAM_B2 * v + (1.0 - ADAM_B2) * (g * g)
    m_hat = m / (1.0 - ADAM_B1 ** ADAM_STEP)
    v_hat = v / (1.0 - ADAM_B2 ** ADAM_STEP)
    delta = -ADAM_LR * (m_hat / (jnp.sqrt(v_hat) + ADAM_EPS) + ADAM_WD * w)
    return delta, m, v


def _adamw(parts, w, m, v, blk, prev, *, name):
    n_blk, rows, cols = w.shape
    tm = _row_tile(rows, cols)
    spec = pl.BlockSpec((None, tm, cols), lambda i: (blk, i, 0))
    n_prev = len(prev)

    def body(p_ref, w_ref, m_ref, v_ref, *refs):
        g_ref, d_ref, nm_ref, nv_ref = refs[n_prev:]
        g = p_ref[0].astype(F32)
        for s in range(1, N_DEV):
            g = g + p_ref[s].astype(F32)
        delta, nm, nv = _adam_update(g, w_ref[...], m_ref[...], v_ref[...])
        g_ref[...] = g
        d_ref[...] = delta
        nm_ref[...] = nm
        nv_ref[...] = nv

    return pl.pallas_call(
        body, name=name, grid=(rows // tm,),
        in_specs=[pl.BlockSpec((N_DEV, tm, cols), lambda i: (0, i, 0)), spec, spec, spec] + [pl.BlockSpec(memory_space=pl.ANY)] * n_prev,
        out_specs=[spec] * 4, out_shape=[jax.ShapeDtypeStruct((n_blk, rows, cols), F32)] * 4,
        input_output_aliases={4 + q: q for q in range(n_prev)},
        compiler_params=_params("parallel"),
    )(parts, w, m, v, *prev)


def _sum_parts(parts, *, name):
    def body(p_ref, o_ref):
        g = p_ref[0]
        for s in range(1, N_DEV):
            g = g + p_ref[s]
        o_ref[...] = g

    return pl.pallas_call(body, name=name, out_shape=jax.ShapeDtypeStruct(parts.shape[1:], F32), compiler_params=_params())(parts)


def _adamw_packed(g, w, m, v, *, name):
    def body(g_ref, w_ref, m_ref, v_ref, d_ref, nm_ref, nv_ref):
        delta, nm, nv = _adam_update(g_ref[...], w_ref[...], m_ref[...], v_ref[...])
        d_ref[...] = delta
        nm_ref[...] = nm
        nv_ref[...] = nv

    return pl.pallas_call(body, name=name, out_shape=[jax.ShapeDtypeStruct(g.shape, F32)] * 3, compiler_params=_params())(g, w, m, v)


SUBLANES = 8


def _pack(arrs):
    rows = []
    for a in arrs:
        flat = a.reshape(-1)
        pad = (-flat.shape[0]) % LANES
        rows.append(jnp.pad(flat, (0, pad)).reshape(-1, LANES))
    out = jnp.concatenate(rows, axis=0)
    return jnp.pad(out, ((0, (-out.shape[0]) % SUBLANES), (0, 0)))


def _unpack(packed, shapes):
    outs, r = [], 0
    for shp in shapes:
        n = math.prod(shp)
        nr = -(-n // LANES)
        outs.append(packed[r:r + nr].reshape(-1)[:n].reshape(shp))
        r += nr
    return outs


WEIGHT_NAMES = ("norm_w", "ffn_w_gate", "ffn_w_up", "ffn_w_down", "ssm_w_in", "ssm_conv_w", "ssm_conv_b", "ssm_dt_bias",
                "ssm_a_log", "ssm_d", "ssm_norm_w", "ssm_w_out", "kv_norm_w", "w_k", "b_k", "w_v", "b_v", "attn_w_q",
                "attn_b_q", "attn_sinks", "attn_w_o", "attn_b_o", "final_norm_w")
MATRIX_NAMES = ("ffn_w_gate", "ffn_w_up", "ffn_w_down", "ssm_w_in", "ssm_w_out", "w_k", "w_v", "attn_w_q", "attn_w_o")
VECTOR_NAMES = tuple(n for n in WEIGHT_NAMES if n not in MATRIX_NAMES)
SHARDED_VECTORS = ("norm_w", "ssm_conv_w", "ssm_conv_b", "ssm_norm_w")


GATHER_PLAN = {
    "gather_stage0": ("gate0", "up0", "down0", "vec"),
    "ffn_fwd0": ("w_in",),
    "ssm_in_proj": ("w_out", "gate1"),
    "ssm_conv_fwd": ("w_k", "w_v"),
    "ssd_fwd": ("up1", "down1", "gate2", "up2"),
    "ssm_out_proj": ("w_q", "w_o"),
    "ffn_fwd1": ("down2", "gate3"),
    "ffn_fwd2": ("up3", "down3"),
}
GRAD_PLAN = {
    "attn_bwd": ("gate3",),
    "ffn_bwd2": ("up3", "down3"),
    "ffn_bwd1": ("w_q", "w_o", "gate2"),
    "ssd_bwd": ("up2", "down2", "w_k", "w_v", "gate1", "up1", "down1"),
    "ffn_bwd0": ("w_in", "w_out"),
    "exchange_last_grads": ("gate0", "up0", "down0"),
}
FFN_PARAMS = {"gate": "ffn_w_gate", "up": "ffn_w_up", "down": "ffn_w_down"}
SINGLE_MATRICES = {"w_in": "ssm_w_in", "w_out": "ssm_w_out", "w_k": "w_k", "w_v": "w_v", "w_q": "attn_w_q", "w_o": "attn_w_o"}


class _MeshNet:
    def __init__(self, p):
        self.p = p
        self.views = {n: p[n].reshape((-1,) + p[n].shape[-2:]) for n in MATRIX_NAMES}
        self.local = {"vec": _pack([p[n] for n in SHARDED_VECTORS])}
        for short, n in FFN_PARAMS.items():
            for k in range(N_FFN):
                self.local[f"{short}{k}"] = _cast_bf16(self.views[n], k, name=f"cast_{short}{k}")
        for short, n in SINGLE_MATRICES.items():
            self.local[short] = _cast_bf16(self.views[n], 0, name=f"cast_{short}")
        self.gathered_at, self.parts_at, self.grads, self.cache = {}, {}, {}, {}

    def carry(self, name):
        if name in GATHER_PLAN:
            keys, comm = GATHER_PLAN[name], _Gather([self.local[k] for k in GATHER_PLAN[name]])
            self.gathered_at.update({k: (comm, i) for i, k in enumerate(keys)})
            return comm
        if name in GRAD_PLAN:
            keys, comm = GRAD_PLAN[name], _AllToAll([self.grads[k] for k in GRAD_PLAN[name]])
            self.parts_at.update({k: (comm, i) for i, k in enumerate(keys)})
            return comm
        return None

    def run(self, name):
        _run_exchange(self.carry(name), name=name)

    def give(self, key, grad):
        self.grads[key] = grad

    def parts(self, key):
        comm, i = self.parts_at[key]
        return comm.results[i]

    def _gathered(self, key):
        comm, i = self.gathered_at[key]
        return comm.results[i]

    def _vec(self, r0, r1, lead):
        t = self._gathered("vec")[:, r0:r1, :].reshape(N_DEV, lead, -1)
        return t.transpose(1, 0, 2).reshape(lead, -1)

    def _derive(self, name):
        p = self.p
        if name[:-1] in FFN_PARAMS:
            return self._gathered(name)
        if name in ("wzx", "wdt"):
            w_in = self._gathered("w_in").transpose(1, 0, 2).reshape(D_MODEL, N_DEV * IN_PROJ_SHARD)
            return w_in[:, :ZX_DIM] if name == "wzx" else w_in[:, ZX_DIM:]
        by_rows = {"wout": "w_out", "wk": "w_k", "wv": "w_v", "wq": "w_q", "wo": "w_o"}
        if name in by_rows:
            g = self._gathered(by_rows[name])
            return g.reshape(N_DEV * g.shape[1], g.shape[2])
        vectors = {"norm_w": lambda: self._vec(0, 6, 6).reshape(2, 3, D_MODEL), "conv_w": lambda: self._vec(6, 18, CONV_WIDTH),
                   "conv_b": lambda: self._vec(18, 21, 1), "ssm_norm_w": lambda: self._vec(21, 23, 1)}
        if name in vectors:
            return vectors[name]()
        replicated = {"dt_bias": p["ssm_dt_bias"], "a_log": p["ssm_a_log"], "d_skip": p["ssm_d"], "kv_norm_w": p["kv_norm_w"][None],
                      "b_k": p["b_k"][None], "b_v": p["b_v"][None], "b_q": p["attn_b_q"], "sinks": p["attn_sinks"],
                      "b_o": p["attn_b_o"], "final_norm_w": p["final_norm_w"][None]}
        return replicated[name]

    def w(self, name):
        if name not in self.cache:
            self.cache[name] = self._derive(name)
        return self.cache[name]


def _step(x, target, p, m, v):
    pos = _slot(_position())
    net = _MeshNet(p)
    net.run("gather_stage0")
    loss, grad_x, small = _forward_backward(x, target, net)
    net.run("exchange_last_grads")
    vec_gather = _Gather([_pack([small[n] for n in VECTOR_NAMES])])
    vec_sum = _sum_parts(_run_exchange(vec_gather, name="gather_vector_grads")[0], name="sum_vector_grads")
    full_shapes = {"norm_w": (2, 3, D_MODEL), "ssm_conv_w": (1, CONV_WIDTH, CONV_DIM), "ssm_conv_b": (1, CONV_DIM),
                   "ssm_norm_w": (1, D_INNER)}
    vec_full = dict(zip(VECTOR_NAMES, _unpack(vec_sum, [full_shapes.get(n, p[n].shape) for n in VECTOR_NAMES])))

    grads, deltas, new_m, new_v = {}, {}, {}, {}
    view = lambda d, n: d[n].reshape(net.views[n].shape)
    for short, n in FFN_PARAMS.items():
        outs = []
        for k in reversed(range(N_FFN)):
            outs = _adamw(net.parts(f"{short}{k}"), net.views[n], view(m, n), view(v, n), k, outs, name=f"adamw_{short}{k}")
        grads[n], deltas[n], new_m[n], new_v[n] = [o.reshape(p[n].shape) for o in outs]
    for short, n in SINGLE_MATRICES.items():
        outs = _adamw(net.parts(short), net.views[n], view(m, n), view(v, n), 0, [], name=f"adamw_{short}")
        grads[n], deltas[n], new_m[n], new_v[n] = [o.reshape(p[n].shape) for o in outs]
    for n in VECTOR_NAMES:
        g = vec_full[n]
        if n in SHARDED_VECTORS:
            per = p[n].shape[-1]
            g = lax.dynamic_slice_in_dim(g, pos * per, per, axis=g.ndim - 1)
        grads[n] = g
    packed = _adamw_packed(*[_pack([d[n] for n in VECTOR_NAMES]) for d in (grads, p, m, v)], name="adamw_vectors")
    shapes = [p[n].shape for n in VECTOR_NAMES]
    for d, pk in zip((deltas, new_m, new_v), packed):
        d.update(zip(VECTOR_NAMES, _unpack(pk, shapes)))
    return loss, grad_x, grads, deltas, new_m, new_v


def kernel(x, norm_w, ffn_w_gate, ffn_w_up, ffn_w_down, ssm_w_in, ssm_conv_w, ssm_conv_b, ssm_dt_bias, ssm_a_log, ssm_d, ssm_norm_w, ssm_w_out, kv_norm_w, w_k, b_k, w_v, b_v, attn_w_q, attn_b_q, attn_sinks, attn_w_o, attn_b_o, final_norm_w, loss_target, m_norm_w, m_ffn_w_gate, m_ffn_w_up, m_ffn_w_down, m_ssm_w_in, m_ssm_conv_w, m_ssm_conv_b, m_ssm_dt_bias, m_ssm_a_log, m_ssm_d, m_ssm_norm_w, m_ssm_w_out, m_kv_norm_w, m_w_k, m_b_k, m_w_v, m_b_v, m_attn_w_q, m_attn_b_q, m_attn_sinks, m_attn_w_o, m_attn_b_o, m_final_norm_w, v_norm_w, v_ffn_w_gate, v_ffn_w_up, v_ffn_w_down, v_ssm_w_in, v_ssm_conv_w, v_ssm_conv_b, v_ssm_dt_bias, v_ssm_a_log, v_ssm_d, v_ssm_norm_w, v_ssm_w_out, v_kv_norm_w, v_w_k, v_b_k, v_w_v, v_b_v, v_attn_w_q, v_attn_b_q, v_attn_sinks, v_attn_w_o, v_attn_b_o, v_final_norm_w):
    p = dict(zip(WEIGHT_NAMES, (norm_w, ffn_w_gate, ffn_w_up, ffn_w_down, ssm_w_in, ssm_conv_w, ssm_conv_b, ssm_dt_bias, ssm_a_log, ssm_d, ssm_norm_w, ssm_w_out, kv_norm_w, w_k, b_k, w_v, b_v, attn_w_q, attn_b_q, attn_sinks, attn_w_o, attn_b_o, final_norm_w)))
    m = dict(zip(WEIGHT_NAMES, (m_norm_w, m_ffn_w_gate, m_ffn_w_up, m_ffn_w_down, m_ssm_w_in, m_ssm_conv_w, m_ssm_conv_b, m_ssm_dt_bias, m_ssm_a_log, m_ssm_d, m_ssm_norm_w, m_ssm_w_out, m_kv_norm_w, m_w_k, m_b_k, m_w_v, m_b_v, m_attn_w_q, m_attn_b_q, m_attn_sinks, m_attn_w_o, m_attn_b_o, m_final_norm_w)))
    v = dict(zip(WEIGHT_NAMES, (v_norm_w, v_ffn_w_gate, v_ffn_w_up, v_ffn_w_down, v_ssm_w_in, v_ssm_conv_w, v_ssm_conv_b, v_ssm_dt_bias, v_ssm_a_log, v_ssm_d, v_ssm_norm_w, v_ssm_w_out, v_kv_norm_w, v_w_k, v_b_k, v_w_v, v_b_v, v_attn_w_q, v_attn_b_q, v_attn_sinks, v_attn_w_o, v_attn_b_o, v_final_norm_w)))
    loss, grad_x, grads, deltas, new_m, new_v = _step(x[0], loss_target[0], p, m, v)
    loss = lax.psum(loss[0, 0], ("x", "y", "c"))
    return (loss, grad_x[None], *[grads[n] for n in WEIGHT_NAMES], *[deltas[n] for n in WEIGHT_NAMES],
            *[new_m[n] for n in WEIGHT_NAMES], *[new_v[n] for n in WEIGHT_NAMES])
```

```python
import functools
import math

import jax
import jax.numpy as jnp
from jax import lax
from jax.experimental import pallas as pl
from jax.experimental.pallas import tpu as pltpu

F32 = jnp.float32
BF16 = jnp.bfloat16

N_DEV = 8
SEQ = 2048
D_MODEL = 1024
D_FF_SHARD = 352
N_FFN = 4
D_INNER = 2048
SSM_HEADS = 32
SSM_HEAD_DIM = 64
SSM_GROUPS = 4
HEADS_PER_GROUP = 8
SSM_STATE = 128
CHUNK = 128
N_CHUNKS = SEQ // CHUNK
GN = SSM_GROUPS * SSM_STATE
CONV_DIM = D_INNER + 2 * GN
CONV_WIDTH = 4
ZX_DIM = D_INNER + CONV_DIM
IN_PROJ_SHARD = 644
ATT_HEAD_DIM = 64
N_Q_HEADS = 16
N_KV_HEADS = 4
Q_PER_KV = 4
KV_DIM = N_KV_HEADS * ATT_HEAD_DIM
WINDOW = 128
ROPE_THETA = 10000.0
EPS = 1e-5
FFN_RES_WEIGHT = 0.5
ATT_SCALE = 1.0 / math.sqrt(ATT_HEAD_DIM)
NEG_BIG = -1e30

ADAM_LR = 0.001
ADAM_B1 = 0.9
ADAM_B2 = 0.999
ADAM_EPS = 1e-08
ADAM_WD = 0.01
ADAM_STEP = 10

VMEM_LIMIT_BYTES = 56 * 1024 * 1024

NN = (((1,), (0,)), ((), ()))
NT = (((1,), (1,)), ((), ()))
TN = (((0,), (0,)), ((), ()))
_DIMS = {"nn": NN, "nt": NT, "tn": TN}


def _params(*sem):
    return pltpu.CompilerParams(dimension_semantics=sem if sem else None, vmem_limit_bytes=VMEM_LIMIT_BYTES)


def _dot(a, b, dims=NN):
    return lax.dot_general(a.astype(BF16), b.astype(BF16), dims, preferred_element_type=F32)


def _dot_f32(a, b, dims=NN):
    return lax.dot_general(a, b, dims, precision=lax.Precision.HIGHEST, preferred_element_type=F32)


def _sigmoid(x):
    return 1.0 / (1.0 + jnp.exp(-x))


def _dsilu(x, s):
    return s * (1.0 + x * (1.0 - s))


def _rms(x):
    r = lax.rsqrt(jnp.mean(x * x, axis=-1, keepdims=True) + EPS)
    return x * r, r


def _sum_all(x):
    return jnp.sum(jnp.sum(x, axis=1, keepdims=True), axis=0, keepdims=True)


MESH = pl.DeviceIdType.MESH
N_PEERS = N_DEV - 1


def _position():
    return lax.axis_index("x"), lax.axis_index("y"), lax.axis_index("c")


def _slot(p):
    return 4 * p[0] + 2 * p[1] + p[2]


class _Exchange:
    def __init__(self, arrays, out_shapes):
        n = len(arrays)
        self.arrays = list(arrays)
        self.out_shapes = out_shapes
        self.scratch = [pltpu.SemaphoreType.DMA((n, N_PEERS)), pltpu.SemaphoreType.DMA((n, N_PEERS)), pltpu.SemaphoreType.DMA((n,))]
        self.results = None


class _Gather(_Exchange):
    def __init__(self, arrays):
        super().__init__(arrays, [jax.ShapeDtypeStruct((N_DEV,) + a.shape, a.dtype) for a in arrays])

    def _plan(self, ins, outs, sems):
        send_sems, recv_sems, local_sems = sems
        x, y, c = _position()
        me, sibling = (x, y, c), (x, y, 1 - c)
        chips = [(1 - x, y), (x, 1 - y), (1 - x, 1 - y)]
        n = len(ins)

        def copy(a, k, block, to, src=None):
            dst = outs[a].at[_slot(block)]
            return pltpu.make_async_remote_copy(src_ref=dst if src is None else src, dst_ref=dst, send_sem=send_sems.at[a, k],
                                                recv_sem=recv_sems.at[a, k], device_id=to, device_id_type=MESH)

        mine = [pltpu.make_async_copy(ins[a], outs[a].at[_slot(me)], local_sems.at[a]) for a in range(n)]
        first = []
        for a in range(n):
            first.append(copy(a, 0, me, sibling, src=ins[a]))
            first += [copy(a, 1 + j, me, (*chip, c), src=ins[a]) for j, chip in enumerate(chips)]
        return n, c, me, sibling, chips, copy, mine, first

    def start(self, ins, outs, sems):
        _, _, _, _, _, _, mine, first = self._plan(ins, outs, sems)
        for cp in mine + first:
            cp.start()

    def finish(self, ins, outs, sems):
        n, c, me, sibling, chips, copy, mine, first = self._plan(ins, outs, sems)
        passed = []
        for j, chip in enumerate(chips):
            for a in range(n):
                copy(a, 1 + j, (*chip, c), me).wait_recv()
                fwd = copy(a, 4 + j, (*chip, c), sibling)
                fwd.start()
                passed.append(fwd)
        for a in range(n):
            copy(a, 0, sibling, me).wait_recv()
            for j, chip in enumerate(chips):
                copy(a, 4 + j, (*chip, 1 - c), me).wait_recv()
        for cp in first + passed:
            cp.wait_send()
        for cp in mine:
            cp.wait()


class _PairSwap(_Exchange):
    def __init__(self, arrays):
        n = len(arrays)
        self.arrays = [a.reshape((N_DEV // 2, 2) + a.shape[1:]) for a in arrays]
        self.out_shapes = [jax.ShapeDtypeStruct((N_DEV // 2,) + a.shape[1:], a.dtype) for a in arrays for _ in range(2)]
        self.scratch = [pltpu.SemaphoreType.DMA((n,)), pltpu.SemaphoreType.DMA((n,)), pltpu.SemaphoreType.DMA((n,))]
        self.results = None

    def _plan(self, ins, outs, sems):
        send_sems, recv_sems, local_sems = sems
        x, y, c = _position()
        mine = [pltpu.make_async_copy(ins[a].at[:, c], outs[2 * a], local_sems.at[a]) for a in range(len(ins))]
        swaps = [pltpu.make_async_remote_copy(src_ref=ins[a].at[:, 1 - c], dst_ref=outs[2 * a + 1], send_sem=send_sems.at[a],
                                              recv_sem=recv_sems.at[a], device_id=(x, y, 1 - c), device_id_type=MESH)
                 for a in range(len(ins))]
        return mine, swaps

    def start(self, ins, outs, sems):
        mine, swaps = self._plan(ins, outs, sems)
        for cp in mine + swaps:
            cp.start()

    def finish(self, ins, outs, sems):
        mine, swaps = self._plan(ins, outs, sems)
        for cp in swaps + mine:
            cp.wait()


class _ChipExchange(_Exchange):
    def __init__(self, arrays):
        n = len(arrays)
        self.arrays = list(arrays)
        self.out_shapes = [jax.ShapeDtypeStruct(a.shape, a.dtype) for a in arrays]
        self.scratch = [pltpu.SemaphoreType.DMA((n, 3)), pltpu.SemaphoreType.DMA((n, 3)), pltpu.SemaphoreType.DMA((n,))]
        self.results = None

    def _plan(self, ins, outs, sems):
        send_sems, recv_sems, local_sems = sems
        x, y, c = _position()
        here = 2 * x + y
        chips = [(1 - x, y), (x, 1 - y), (1 - x, 1 - y)]
        n = len(ins)

        def copy(a, k, src_slot, dst_slot):
            return pltpu.make_async_remote_copy(src_ref=ins[a].at[src_slot], dst_ref=outs[a].at[dst_slot], send_sem=send_sems.at[a, k],
                                                recv_sem=recv_sems.at[a, k], device_id=(*chips[k], c), device_id_type=MESH)

        there = [2 * qx + qy for qx, qy in chips]
        mine = [pltpu.make_async_copy(ins[a].at[here], outs[a].at[here], local_sems.at[a]) for a in range(n)]
        sends = [copy(a, k, there[k], here) for a in range(n) for k in range(3)]
        arrivals = lambda: [copy(a, k, here, there[k]) for a in range(n) for k in range(3)]
        return mine, sends, arrivals

    def start(self, ins, outs, sems):
        mine, sends, _ = self._plan(ins, outs, sems)
        for cp in mine + sends:
            cp.start()

    def finish(self, ins, outs, sems):
        mine, sends, arrivals = self._plan(ins, outs, sems)
        for cp in arrivals():
            cp.wait_recv()
        for cp in sends:
            cp.wait_send()
        for cp in mine:
            cp.wait()


def _call(body, *, name, grid, in_specs, out_specs, out_shape, args, scratch_shapes=(), sem=(), comm=(), aliases=None):
    single = not isinstance(out_shape, (list, tuple))
    out_shape = [out_shape] if single else list(out_shape)
    out_specs = [out_specs] if single else list(out_specs)
    comms = list(comm or ())
    n_in, n_out, n_scr = len(args), len(out_shape), len(scratch_shapes)
    params = pltpu.CompilerParams(dimension_semantics=tuple(sem) if sem else None, vmem_limit_bytes=VMEM_LIMIT_BYTES)
    if not comms:
        res = pl.pallas_call(body, name=name, grid=grid, in_specs=list(in_specs), out_specs=out_specs, out_shape=out_shape,
                             scratch_shapes=list(scratch_shapes), input_output_aliases=aliases or {}, compiler_params=params)(*args)
        return res[0] if single else res
    counts = [n_in] + [len(c.arrays) for c in comms] + [n_out] + [len(c.out_shapes) for c in comms] + [n_scr] + [len(c.scratch) for c in comms]
    nc = len(comms)

    def carried(*refs):
        pos, groups = 0, []
        for cnt in counts:
            groups.append(refs[pos:pos + cnt])
            pos += cnt
        ins, c_ins = groups[0], groups[1:1 + nc]
        outs, c_outs = groups[1 + nc], groups[2 + nc:2 + 2 * nc]
        scr, c_sems = groups[2 + 2 * nc], groups[3 + 2 * nc:]
        ids = [pl.program_id(d) for d in range(len(grid))]
        is_first = functools.reduce(jnp.logical_and, [i == 0 for i in ids])
        is_last = functools.reduce(jnp.logical_and, [i == g - 1 for i, g in zip(ids, grid)])

        @pl.when(is_first)
        def _():
            for q, c in enumerate(comms):
                c.start(c_ins[q], c_outs[q], c_sems[q])

        body(*ins, *outs, *scr)

        @pl.when(is_last)
        def _():
            for q, c in enumerate(comms):
                c.finish(c_ins[q], c_outs[q], c_sems[q])

    anyspec = pl.BlockSpec(memory_space=pl.ANY)
    c_arrays = [a for c in comms for a in c.arrays]
    c_shapes = [s for c in comms for s in c.out_shapes]
    res = pl.pallas_call(
        carried, name=name, grid=grid, in_specs=list(in_specs) + [anyspec] * len(c_arrays), out_specs=out_specs + [anyspec] * len(c_shapes),
        out_shape=out_shape + c_shapes, scratch_shapes=list(scratch_shapes) + [s for c in comms for s in c.scratch],
        input_output_aliases=aliases or {}, compiler_params=params)(*args, *c_arrays)
    pos = n_out
    for c in comms:
        c.results = list(res[pos:pos + len(c.out_shapes)])
        pos += len(c.out_shapes)
    return res[0] if single else list(res[:n_out])


def _run_exchange(comm, *, name):
    def body(*refs):
        n_ci, n_co = len(comm.arrays), len(comm.out_shapes)
        ins, outs, sems = refs[:n_ci], refs[n_ci:n_ci + n_co], refs[n_ci + n_co:]
        comm.start(ins, outs, sems)
        comm.finish(ins, outs, sems)

    anyspec = pl.BlockSpec(memory_space=pl.ANY)
    comm.results = list(pl.pallas_call(
        body, name=name, in_specs=[anyspec] * len(comm.arrays), out_specs=[anyspec] * len(comm.out_shapes),
        out_shape=list(comm.out_shapes), scratch_shapes=list(comm.scratch))(*comm.arrays))
    return comm.results


def _mm(a, b, *, dims="nn", bias=None, res=None, out_dtype=F32, name, tm=512, tn=512, tk=1024, comm=None):
    if dims == "tn":
        k_dim, m_dim = a.shape
    else:
        m_dim, k_dim = a.shape
    n_dim = b.shape[0] if dims == "nt" else b.shape[1]
    tm, tn, tk = min(tm, m_dim), min(tn, n_dim), min(tk, k_dim)
    assert m_dim % tm == 0 and n_dim % tn == 0 and k_dim % tk == 0, (name, a.shape, b.shape)
    nk = k_dim // tk
    a_spec = pl.BlockSpec((tk, tm), lambda i, j, k: (k, i)) if dims == "tn" else pl.BlockSpec((tm, tk), lambda i, j, k: (i, k))
    b_spec = pl.BlockSpec((tn, tk), lambda i, j, k: (j, k)) if dims == "nt" else pl.BlockSpec((tk, tn), lambda i, j, k: (k, j))
    in_specs, args = [a_spec, b_spec], [a, b]
    if bias is not None:
        in_specs.append(pl.BlockSpec((1, tn), lambda i, j, k: (0, j)))
        args.append(bias)
    if res is not None:
        in_specs.append(pl.BlockSpec((tm, tn), lambda i, j, k: (i, j)))
        args.append(res)
    dn = _DIMS[dims]

    def body(*refs):
        a_ref, b_ref = refs[0], refs[1]
        o_ref, acc_ref = refs[-2], refs[-1]
        k = pl.program_id(2)

        @pl.when(k == 0)
        def _():
            acc_ref[...] = jnp.zeros_like(acc_ref)

        acc_ref[...] += _dot(a_ref[...], b_ref[...], dn)

        @pl.when(k == nk - 1)
        def _():
            r = acc_ref[...]
            pos = 2
            if bias is not None:
                r = r + refs[pos][...]
                pos += 1
            if res is not None:
                r = r + refs[pos][...]
            o_ref[...] = r.astype(out_dtype)

    return _call(
        body, name=name, grid=(m_dim // tm, n_dim // tn, nk), in_specs=in_specs,
        out_specs=pl.BlockSpec((tm, tn), lambda i, j, k: (i, j)),
        out_shape=jax.ShapeDtypeStruct((m_dim, n_dim), out_dtype),
        scratch_shapes=[pltpu.VMEM((tm, tn), F32)], sem=("parallel", "parallel", "arbitrary"), args=args, comm=comm)


def _norm_mm(x, nw, w, bias, *, name, tm=512, tn=512, comm=None):
    t_dim, d_dim = x.shape
    n_dim = w.shape[1]
    tn = min(tn, n_dim)
    assert t_dim % tm == 0 and n_dim % tn == 0
    has_bias = bias is not None
    in_specs = [pl.BlockSpec((tm, d_dim), lambda i, j: (i, 0)), pl.BlockSpec((1, d_dim), lambda i, j: (0, 0)),
                pl.BlockSpec((d_dim, tn), lambda i, j: (0, j))]
    args = [x, nw, w]
    if has_bias:
        in_specs.append(pl.BlockSpec((1, tn), lambda i, j: (0, j)))
        args.append(bias)

    def body(*refs):
        x_ref, nw_ref, w_ref = refs[:3]
        o_ref, h_ref = refs[-2], refs[-1]

        @pl.when(pl.program_id(1) == 0)
        def _():
            xhat, _ = _rms(x_ref[...])
            h_ref[...] = (xhat * nw_ref[...]).astype(BF16)

        r = _dot(h_ref[...], w_ref[...])
        if has_bias:
            r = r + refs[3][...]
        o_ref[...] = r

    return _call(
        body, name=name, grid=(t_dim // tm, n_dim // tn), in_specs=in_specs,
        out_specs=[pl.BlockSpec((tm, tn), lambda i, j: (i, j)), pl.BlockSpec((tm, d_dim), lambda i, j: (i, 0))],
        out_shape=[jax.ShapeDtypeStruct((t_dim, n_dim), F32), jax.ShapeDtypeStruct((t_dim, d_dim), BF16)],
        sem=("parallel", "arbitrary"), args=args, comm=comm)


def _norm_bwd(x, nw, dh, res, *, name, tm=256, comm=None):
    t_dim, d_dim = x.shape
    n_res = len(res)
    row = pl.BlockSpec((tm, d_dim), lambda i: (i, 0))
    vec = pl.BlockSpec((1, d_dim), lambda i: (0, 0))

    def body(*refs):
        x_ref, nw_ref, dh_ref = refs[:3]
        dx_ref, dnw_ref = refs[-2], refs[-1]
        xhat, r = _rms(x_ref[...])
        dh = dh_ref[...]
        dxhat = dh * nw_ref[...]
        dx = r * (dxhat - xhat * jnp.mean(dxhat * xhat, axis=-1, keepdims=True))
        for rr in refs[3:3 + n_res]:
            dx = dx + rr[...]
        dx_ref[...] = dx

        @pl.when(pl.program_id(0) == 0)
        def _():
            dnw_ref[...] = jnp.zeros_like(dnw_ref)

        dnw_ref[...] += jnp.sum(dh * xhat, axis=0, keepdims=True)

    return _call(
        body, name=name, grid=(t_dim // tm,), in_specs=[row, vec, row] + [row] * n_res,
        out_specs=[row, vec],
        out_shape=[jax.ShapeDtypeStruct((t_dim, d_dim), F32), jax.ShapeDtypeStruct((1, d_dim), F32)],
        sem=("arbitrary",), args=[x, nw, dh, *res], comm=comm)


def _colsum(x, *, name, tm=256):
    t_dim, n_dim = x.shape

    def body(x_ref, o_ref):
        @pl.when(pl.program_id(0) == 0)
        def _():
            o_ref[...] = jnp.zeros_like(o_ref)

        o_ref[...] += jnp.sum(x_ref[...], axis=0, keepdims=True)

    return pl.pallas_call(
        body, name=name, grid=(t_dim // tm,), in_specs=[pl.BlockSpec((tm, n_dim), lambda i: (i, 0))],
        out_specs=pl.BlockSpec((1, n_dim), lambda i: (0, 0)), out_shape=jax.ShapeDtypeStruct((1, n_dim), F32),
        compiler_params=_params("arbitrary"),
    )(x)


FFN_ROW_TILE = 512
FFN_SHARDS_PER_STEP = 2
FFN_STEPS = N_DEV // FFN_SHARDS_PER_STEP
FFN_STEP_COLS = FFN_SHARDS_PER_STEP * D_FF_SHARD


def _ffn_step_weights(wg_ref, wu_ref, wd_ref, wg_scr, wu_scr):
    for s in range(FFN_SHARDS_PER_STEP):
        cols = slice(s * D_FF_SHARD, (s + 1) * D_FF_SHARD)
        wg_scr[:, cols] = wg_ref[s]
        wu_scr[:, cols] = wu_ref[s]
    return wg_scr[...], wu_scr[...], jnp.concatenate([wd_ref[s] for s in range(FFN_SHARDS_PER_STEP)], axis=0)


def _ffn_specs(d_dim):
    col = pl.BlockSpec((FFN_SHARDS_PER_STEP, d_dim, D_FF_SHARD), lambda j: (j, 0, 0))
    rowb = pl.BlockSpec((FFN_SHARDS_PER_STEP, D_FF_SHARD, d_dim), lambda j: (j, 0, 0))
    return col, rowb


def _ffn_fwd(x, nw, wg, wu, wd, *, name, comm=None):
    t_dim, d_dim = x.shape
    n_tiles = t_dim // FFN_ROW_TILE

    def body(x_ref, nw_ref, wg_ref, wu_ref, wd_ref, o_ref, h_scr, wg_scr, wu_scr):
        j = pl.program_id(0)

        @pl.when(j == 0)
        def _():
            xhat, _ = _rms(x_ref[...])
            h_scr[...] = (xhat * nw_ref[...]).astype(BF16)
            o_ref[...] = jnp.zeros_like(o_ref)

        w_gate, w_up, w_down = _ffn_step_weights(wg_ref, wu_ref, wd_ref, wg_scr, wu_scr)
        for t in range(n_tiles):
            rows = pl.ds(t * FFN_ROW_TILE, FFN_ROW_TILE)
            h = h_scr[rows, :]
            g = _dot(h, w_gate)
            u = _dot(h, w_up)
            act = g * _sigmoid(g) * u
            o_ref[rows, :] += _dot(act, w_down)

        @pl.when(j == FFN_STEPS - 1)
        def _():
            o_ref[...] = x_ref[...] + FFN_RES_WEIGHT * o_ref[...]

    full = pl.BlockSpec((t_dim, d_dim), lambda j: (0, 0))
    col, rowb = _ffn_specs(d_dim)
    return _call(
        body, name=name, grid=(FFN_STEPS,),
        in_specs=[full, pl.BlockSpec((1, d_dim), lambda j: (0, 0)), col, col, rowb],
        out_specs=full, out_shape=jax.ShapeDtypeStruct((t_dim, d_dim), F32),
        scratch_shapes=[pltpu.VMEM((t_dim, d_dim), BF16), pltpu.VMEM((d_dim, FFN_STEP_COLS), BF16),
                        pltpu.VMEM((d_dim, FFN_STEP_COLS), BF16)],
        sem=("arbitrary",), args=[x, nw, wg, wu, wd], comm=comm)


def _ffn_bwd_prep(x, nw, dout, *, name, tm=256):
    t_dim, d_dim = x.shape
    row = pl.BlockSpec((tm, d_dim), lambda i: (i, 0))

    def body(x_ref, nw_ref, dout_ref, h_ref, dob_ref):
        xhat, _ = _rms(x_ref[...])
        h_ref[...] = (xhat * nw_ref[...]).astype(BF16)
        dob_ref[...] = (FFN_RES_WEIGHT * dout_ref[...]).astype(BF16)

    return pl.pallas_call(
        body, name=name, grid=(t_dim // tm,), in_specs=[row, pl.BlockSpec((1, d_dim), lambda i: (0, 0)), row],
        out_specs=[row, row], out_shape=[jax.ShapeDtypeStruct((t_dim, d_dim), BF16)] * 2,
        compiler_params=_params("parallel"),
    )(x, nw, dout)


def _ffn_bwd(h, dob, wg, wu, wd, *, name, comm=None):
    t_dim, d_dim = h.shape
    n_tiles = t_dim // FFN_ROW_TILE

    def body(h_ref, dob_ref, wg_ref, wu_ref, wd_ref, dh_ref, gg_ref, gu_ref, gd_ref, dwg_scr, dwu_scr, dwd_scr, wg_scr, wu_scr):
        j = pl.program_id(0)

        @pl.when(j == 0)
        def _():
            dh_ref[...] = jnp.zeros_like(dh_ref)

        w_gate, w_up, w_down = _ffn_step_weights(wg_ref, wu_ref, wd_ref, wg_scr, wu_scr)
        for t in range(n_tiles):
            rows = pl.ds(t * FFN_ROW_TILE, FFN_ROW_TILE)
            hh = h_ref[rows, :]
            do = dob_ref[rows, :]
            g = _dot(hh, w_gate)
            u = _dot(hh, w_up)
            sg = _sigmoid(g)
            s = g * sg
            da = _dot(do, w_down, NT)
            dwd = _dot(s * u, do, TN)
            du = (da * s).astype(BF16)
            dg = (da * u * _dsilu(g, sg)).astype(BF16)
            dwg = _dot(hh, dg, TN)
            dwu = _dot(hh, du, TN)
            if t == 0:
                dwd_scr[...] = dwd
                dwg_scr[...] = dwg
                dwu_scr[...] = dwu
            else:
                dwd_scr[...] += dwd
                dwg_scr[...] += dwg
                dwu_scr[...] += dwu
            dh_ref[rows, :] += _dot(dg, w_gate, NT) + _dot(du, w_up, NT)
        for s in range(FFN_SHARDS_PER_STEP):
            cols = slice(s * D_FF_SHARD, (s + 1) * D_FF_SHARD)
            gg_ref[s] = dwg_scr[:, cols].astype(BF16)
            gu_ref[s] = dwu_scr[:, cols].astype(BF16)
            gd_ref[s] = dwd_scr[cols, :].astype(BF16)

    full_bf = pl.BlockSpec((t_dim, d_dim), lambda j: (0, 0))
    col, rowb = _ffn_specs(d_dim)
    return _call(
        body, name=name, grid=(FFN_STEPS,),
        in_specs=[full_bf, full_bf, col, col, rowb], out_specs=[full_bf, col, col, rowb],
        out_shape=[jax.ShapeDtypeStruct((t_dim, d_dim), F32), jax.ShapeDtypeStruct(wg.shape, BF16),
                   jax.ShapeDtypeStruct(wu.shape, BF16), jax.ShapeDtypeStruct(wd.shape, BF16)],
        scratch_shapes=[pltpu.VMEM((d_dim, FFN_STEP_COLS), F32), pltpu.VMEM((d_dim, FFN_STEP_COLS), F32),
                        pltpu.VMEM((FFN_STEP_COLS, d_dim), F32), pltpu.VMEM((d_dim, FFN_STEP_COLS), BF16),
                        pltpu.VMEM((d_dim, FFN_STEP_COLS), BF16)],
        sem=("arbitrary",), args=[h, dob, wg, wu, wd], comm=comm)


CONV_COLS = 256


def _shift_down(u, s, rows):
    return jnp.where(rows >= s, pltpu.roll(u, s, 0), 0.0)


def _shift_up(u, s, rows, t_dim):
    return jnp.where(rows < t_dim - s, pltpu.roll(u, t_dim - s, 0), 0.0)


def _conv_pre(u, w_ref, b_ref, rows):
    c = b_ref[...] + w_ref[CONV_WIDTH - 1:CONV_WIDTH, :] * u
    for k in range(CONV_WIDTH - 1):
        c = c + w_ref[k:k + 1, :] * _shift_down(u, CONV_WIDTH - 1 - k, rows)
    return c


def _conv_fwd(zx, cw, cb, *, name, comm=None):
    t_dim = zx.shape[0]
    off = D_INNER // CONV_COLS

    def body(u_ref, w_ref, b_ref, o_ref):
        rows = lax.broadcasted_iota(jnp.int32, (t_dim, CONV_COLS), 0)
        c = _conv_pre(u_ref[...], w_ref, b_ref, rows)
        o_ref[...] = c * _sigmoid(c)

    return _call(
        body, name=name, grid=(CONV_DIM // CONV_COLS,),
        in_specs=[pl.BlockSpec((t_dim, CONV_COLS), lambda j: (0, off + j)),
                  pl.BlockSpec((CONV_WIDTH, CONV_COLS), lambda j: (0, j)), pl.BlockSpec((1, CONV_COLS), lambda j: (0, j))],
        out_specs=pl.BlockSpec((t_dim, CONV_COLS), lambda j: (0, j)),
        out_shape=jax.ShapeDtypeStruct((t_dim, CONV_DIM), F32), sem=("parallel",), args=[zx, cw, cb], comm=comm)


def _conv_bwd(zx, cw, cb, dxs, db, dc, dzx, *, name, comm=None):
    t_dim = zx.shape[0]
    off = D_INNER // CONV_COLS
    n_xs = D_INNER // CONV_COLS
    n_b = GN // CONV_COLS

    def body(u_ref, w_ref, b_ref, dxs_ref, db_ref, dc_ref, dzx_in, dzx_ref, dw_ref, dbias_ref):
        j = pl.program_id(0)
        rows = lax.broadcasted_iota(jnp.int32, (t_dim, CONV_COLS), 0)
        u = u_ref[...]
        c = _conv_pre(u, w_ref, b_ref, rows)
        d = jnp.where(j < n_xs, dxs_ref[...], jnp.where(j < n_xs + n_b, db_ref[...], dc_ref[...]))
        dcv = d * _dsilu(c, _sigmoid(c))
        dpre = w_ref[CONV_WIDTH - 1:CONV_WIDTH, :] * dcv
        dw_ref[CONV_WIDTH - 1:CONV_WIDTH, :] = jnp.sum(dcv * u, axis=0, keepdims=True)
        for k in range(CONV_WIDTH - 1):
            s = CONV_WIDTH - 1 - k
            dpre = dpre + w_ref[k:k + 1, :] * _shift_up(dcv, s, rows, t_dim)
            dw_ref[k:k + 1, :] = jnp.sum(dcv * _shift_down(u, s, rows), axis=0, keepdims=True)
        dzx_ref[...] = dpre
        dbias_ref[...] = jnp.sum(dcv, axis=0, keepdims=True)

    blk = lambda n: pl.BlockSpec((t_dim, CONV_COLS), n)
    return _call(
        body, name=name, grid=(CONV_DIM // CONV_COLS,),
        in_specs=[blk(lambda j: (0, off + j)), pl.BlockSpec((CONV_WIDTH, CONV_COLS), lambda j: (0, j)),
                  pl.BlockSpec((1, CONV_COLS), lambda j: (0, j)),
                  blk(lambda j: (0, jnp.minimum(j, n_xs - 1))),
                  blk(lambda j: (0, jnp.clip(j - n_xs, 0, n_b - 1))),
                  blk(lambda j: (0, jnp.clip(j - n_xs - n_b, 0, n_b - 1))),
                  pl.BlockSpec(memory_space=pl.ANY)],
        out_specs=[blk(lambda j: (0, off + j)), pl.BlockSpec((CONV_WIDTH, CONV_COLS), lambda j: (0, j)),
                   pl.BlockSpec((1, CONV_COLS), lambda j: (0, j))],
        out_shape=[jax.ShapeDtypeStruct(dzx.shape, F32), jax.ShapeDtypeStruct((CONV_WIDTH, CONV_DIM), F32),
                   jax.ShapeDtypeStruct((1, CONV_DIM), F32)],
        aliases={6: 0}, sem=("parallel",), args=[zx, cw, cb, dxs, db, dc, dzx], comm=comm)


def _softplus_parts(x):
    e = jnp.exp(-jnp.abs(x))
    u = 1.0 + e
    log1p_e = jnp.where(u == 1.0, e, jnp.log(u) * e / jnp.where(u == 1.0, 1.0, u - 1.0))
    return jnp.maximum(x, 0.0) + log1p_e


def _dt_prep(dtr, dt_bias, a_log, *, name):
    def body(dtr_ref, bias_ref, alog_ref, dt_ref, a_ref):
        dt = _softplus_parts(dtr_ref[...] + bias_ref[...])
        dt_ref[...] = dt
        a_ref[...] = dt * (-jnp.exp(alog_ref[...]))

    return pl.pallas_call(body, name=name, out_shape=[jax.ShapeDtypeStruct(dtr.shape, F32)] * 2,
                          compiler_params=_params())(dtr, dt_bias, a_log)


def _dt_bwd(dtr, dt_bias, a_log, dt, ddt, da, *, name):
    def body(dtr_ref, bias_ref, alog_ref, dt_ref, ddt_ref, da_ref, ddtr_ref, dbias_ref, dalog_ref):
        a_neg = -jnp.exp(alog_ref[...])
        da_v = da_ref[...]
        ddt_tot = ddt_ref[...] + da_v * a_neg
        ddtr = ddt_tot * _sigmoid(dtr_ref[...] + bias_ref[...])
        ddtr_ref[...] = ddtr
        dbias_ref[...] = jnp.sum(ddtr, axis=0, keepdims=True)
        dalog_ref[...] = jnp.sum(da_v * dt_ref[...], axis=0, keepdims=True) * a_neg

    return pl.pallas_call(
        body, name=name,
        out_shape=[jax.ShapeDtypeStruct(dtr.shape, F32), jax.ShapeDtypeStruct((1, SSM_HEADS), F32),
                   jax.ShapeDtypeStruct((1, SSM_HEADS), F32)],
        compiler_params=_params())(dtr, dt_bias, a_log, dt, ddt, da)


GROUP_COLS = HEADS_PER_GROUP * SSM_HEAD_DIM
LANES = 128
HEADS_PER_LANE_BLOCK = LANES // SSM_HEAD_DIM


def _split3(x):
    hi = x.astype(BF16)
    r1 = x - hi.astype(F32)
    mid = r1.astype(BF16)
    lo = (r1 - mid.astype(F32)).astype(BF16)
    return hi, mid, lo


def _group_sums(vals, expand):
    x = jnp.concatenate(vals, axis=0)
    out = None
    for part in _split3(x):
        t = lax.dot_general(part, expand, NT, preferred_element_type=F32)
        out = t if out is None else out + t
    return [out[i * CHUNK:(i + 1) * CHUNK] for i in range(len(vals))]


def _ssd_chunk_common(a_ref, dt_ref, b_ref, c_ref):
    row = lax.broadcasted_iota(jnp.int32, (CHUNK, CHUNK), 0)
    col = lax.broadcasted_iota(jnp.int32, (CHUNK, CHUNK), 1)
    causal = col <= row
    lower = causal.astype(F32)
    upper = (col >= row).astype(F32)
    head = lax.broadcasted_iota(jnp.int32, (HEADS_PER_GROUP, GROUP_COLS), 0)
    lane = lax.broadcasted_iota(jnp.int32, (HEADS_PER_GROUP, GROUP_COLS), 1)
    expand = ((lane >= head * SSM_HEAD_DIM) & (lane < (head + 1) * SSM_HEAD_DIM)).astype(F32)
    a = a_ref[...]
    cs = _dot_f32(lower, a)
    cs_row = _dot_f32(a, upper, TN)
    cs_x = _dot_f32(cs, expand)
    dt_x = _dot_f32(dt_ref[...], expand)
    e_out_x = jnp.exp(cs_x)
    e_st_x = jnp.exp(cs_x[CHUNK - 1:CHUNK, :] - cs_x)
    bc = b_ref[...]
    cc = c_ref[...]
    cb = _dot(cc, bc, NT)
    return causal, upper, expand.astype(BF16), cs, cs_row, dt_x, e_out_x, e_st_x, bc, cc, cb


def _head_decay(causal, cs, cs_row, h):
    return jnp.exp(jnp.where(causal, cs[:, h:h + 1] - cs_row[h:h + 1, :], NEG_BIG))


def _lane_block_head_masks():
    lane = lax.broadcasted_iota(jnp.int32, (CHUNK, LANES), 1)
    return [(lane >= i * SSM_HEAD_DIM) & (lane < (i + 1) * SSM_HEAD_DIM) for i in range(HEADS_PER_LANE_BLOCK)]


def _decay_state(dst_ref, old, new, cs):
    for h in range(HEADS_PER_GROUP):
        rows = slice(h * SSM_HEAD_DIM, (h + 1) * SSM_HEAD_DIM)
        dst_ref[rows, :] = jnp.exp(cs[CHUNK - 1:CHUNK, h:h + 1]) * old[rows, :] + new[rows, :]


def _ssd_fwd(xbc, dtg, ag, dgx, *, name, comm=None):
    t_dim = xbc.shape[0]

    def body(xs_ref, b_ref, c_ref, dt_ref, a_ref, d_ref, y_ref, st_ref, s_scr):
        @pl.when(pl.program_id(1) == 0)
        def _():
            s_scr[...] = jnp.zeros_like(s_scr)

        causal, _, _, cs, cs_row, dt_x, e_out_x, e_st_x, bc, cc, cb = _ssd_chunk_common(a_ref, dt_ref, b_ref, c_ref)
        masks = _lane_block_head_masks()
        xs = xs_ref[...]
        xdt_x = xs * dt_x
        prev = s_scr[...]
        st_ref[...] = prev
        y_off = e_out_x * _dot(cc, prev, NT) + xs * d_ref[...]
        for blk in range(GROUP_COLS // LANES):
            lanes = slice(blk * LANES, (blk + 1) * LANES)
            x_b = xdt_x[:, lanes].astype(BF16)
            acc = y_off[:, lanes]
            for i in range(HEADS_PER_LANE_BLOCK):
                m = cb * _head_decay(causal, cs, cs_row, blk * HEADS_PER_LANE_BLOCK + i)
                acc = acc + _dot(m, jnp.where(masks[i], x_b, jnp.zeros_like(x_b)))
            y_ref[:, lanes] = acc
        _decay_state(s_scr, prev, _dot(xdt_x * e_st_x, bc, TN), cs)

    xs = pl.BlockSpec((CHUNK, GROUP_COLS), lambda g, c: (c, g))
    bsp = pl.BlockSpec((CHUNK, SSM_STATE), lambda g, c: (c, D_INNER // SSM_STATE + g))
    csp = pl.BlockSpec((CHUNK, SSM_STATE), lambda g, c: (c, (D_INNER + GN) // SSM_STATE + g))
    per_head = pl.BlockSpec((None, CHUNK, HEADS_PER_GROUP), lambda g, c: (g, c, 0))
    dsk = pl.BlockSpec((None, 1, GROUP_COLS), lambda g, c: (g, 0, 0))
    return _call(
        body, name=name, grid=(SSM_GROUPS, N_CHUNKS),
        in_specs=[xs, bsp, csp, per_head, per_head, dsk],
        out_specs=[xs, pl.BlockSpec((None, GROUP_COLS, SSM_STATE), lambda g, c: (c, g, 0))],
        out_shape=[jax.ShapeDtypeStruct((t_dim, D_INNER), F32),
                   jax.ShapeDtypeStruct((N_CHUNKS, D_INNER, SSM_STATE), F32)],
        scratch_shapes=[pltpu.VMEM((GROUP_COLS, SSM_STATE), F32)],
        sem=("parallel", "arbitrary"), args=[xbc, xbc, xbc, dtg, ag, dgx], comm=comm)


def _ssd_bwd(xbc, dtg, ag, dgx, states, dy, *, name, comm=None):
    t_dim = xbc.shape[0]
    last = N_CHUNKS - 1

    def body(xs_ref, b_ref, c_ref, dt_ref, a_ref, d_ref, st_ref, dy_ref,
             dxs_ref, db_ref, dc_ref, ddt_ref, da_ref, dd_ref, ds_scr):
        @pl.when(pl.program_id(1) == 0)
        def _():
            ds_scr[...] = jnp.zeros_like(ds_scr)
            dd_ref[...] = jnp.zeros_like(dd_ref)

        causal, upper, expand, cs, cs_row, dt_x, e_out_x, e_st_x, bc, cc, cb = _ssd_chunk_common(a_ref, dt_ref, b_ref, c_ref)
        masks = _lane_block_head_masks()
        xs = xs_ref[...]
        dy_x = dy_ref[...]
        xdt_x = xs * dt_x
        prev = st_ref[...]
        d_s = ds_scr[...]
        g1_x = _dot(bc, d_s, NT)
        cp_x = _dot(cc, prev, NT)
        d_cb = jnp.zeros((CHUNK, CHUNK), F32)
        lane8 = lax.broadcasted_iota(jnp.int32, (CHUNK, HEADS_PER_GROUP), 1)
        sub8 = lax.broadcasted_iota(jnp.int32, (HEADS_PER_GROUP, CHUNK), 0)
        row_w = jnp.zeros((CHUNK, HEADS_PER_GROUP), F32)
        col_w = jnp.zeros((HEADS_PER_GROUP, CHUNK), F32)
        dxdt_blocks = []
        for blk in range(GROUP_COLS // LANES):
            lanes = slice(blk * LANES, (blk + 1) * LANES)
            dy_b = dy_x[:, lanes].astype(BF16)
            x_b = xdt_x[:, lanes].astype(BF16)
            acc_dx = jnp.zeros((CHUNK, LANES), F32)
            for i in range(HEADS_PER_LANE_BLOCK):
                h = blk * HEADS_PER_LANE_BLOCK + i
                decay = _head_decay(causal, cs, cs_row, h)
                m = cb * decay
                dy_h = jnp.where(masks[i], dy_b, jnp.zeros_like(dy_b))
                acc_dx = acc_dx + _dot(m, dy_h, TN)
                d_m = _dot(dy_h, x_b, NT)
                d_cb = d_cb + d_m * decay
                w = d_m * m
                row_w = jnp.where(lane8 == h, jnp.sum(w, axis=1, keepdims=True), row_w)
                col_w = jnp.where(sub8 == h, jnp.sum(w, axis=0, keepdims=True), col_w)
            dxdt_blocks.append(acc_dx)
        dxdt_x = jnp.concatenate(dxdt_blocks, axis=1) + e_st_x * g1_x
        dxs_ref[...] = dxdt_x * dt_x + dy_x * d_ref[...]
        dye = dy_x * e_out_x
        xde = xdt_x * e_st_x
        ddt, y_off, tl, dskip = _group_sums([dxdt_x * xs, dye * cp_x, xde * g1_x, dy_x * xs], expand)
        ddt_ref[...] = ddt
        dd_ref[...] += jnp.sum(dskip, axis=0, keepdims=True)
        sp = None
        for part in _split3(d_s * prev):
            t = lax.dot_general(expand, part, NN, preferred_element_type=F32)
            sp = t if sp is None else sp + t
        last_col = jnp.exp(cs_row[:, CHUNK - 1:CHUNK]) * jnp.sum(sp, axis=1, keepdims=True)
        eye = lax.broadcasted_iota(jnp.int32, (HEADS_PER_GROUP, HEADS_PER_GROUP), 0) == lax.broadcasted_iota(
            jnp.int32, (HEADS_PER_GROUP, HEADS_PER_GROUP), 1)
        last_row = jnp.sum(jnp.where(eye, last_col, 0.0), axis=0, keepdims=True) + jnp.sum(tl, axis=0, keepdims=True)
        is_last = lax.broadcasted_iota(jnp.int32, (CHUNK, 1), 0) == CHUNK - 1
        d_cs = row_w + y_off - tl + jnp.where(is_last, last_row, 0.0)
        da_ref[...] = _dot_f32(upper, d_cs) - _dot_f32(upper, col_w, NT)
        dc_ref[...] = _dot(d_cb, bc) + _dot(dye, prev)
        db_ref[...] = _dot(d_cb, cc, TN) + _dot(xde, d_s)
        _decay_state(ds_scr, d_s, _dot(dye, cc, TN), cs)

    rev = lambda c: last - c
    xs = pl.BlockSpec((CHUNK, GROUP_COLS), lambda g, c: (rev(c), g))
    bsp = pl.BlockSpec((CHUNK, SSM_STATE), lambda g, c: (rev(c), D_INNER // SSM_STATE + g))
    csp = pl.BlockSpec((CHUNK, SSM_STATE), lambda g, c: (rev(c), (D_INNER + GN) // SSM_STATE + g))
    per_head = pl.BlockSpec((None, CHUNK, HEADS_PER_GROUP), lambda g, c: (g, rev(c), 0))
    dsk = pl.BlockSpec((None, 1, GROUP_COLS), lambda g, c: (g, 0, 0))
    dsum = pl.BlockSpec((None, 1, HEADS_PER_GROUP), lambda g, c: (g, 0, 0))
    st = pl.BlockSpec((None, GROUP_COLS, SSM_STATE), lambda g, c: (rev(c), g, 0))
    grp = pl.BlockSpec((CHUNK, SSM_STATE), lambda g, c: (rev(c), g))
    return _call(
        body, name=name, grid=(SSM_GROUPS, N_CHUNKS),
        in_specs=[xs, bsp, csp, per_head, per_head, dsk, st, xs],
        out_specs=[xs, grp, grp, per_head, per_head, dsum],
        out_shape=[jax.ShapeDtypeStruct((t_dim, D_INNER), F32), jax.ShapeDtypeStruct((t_dim, GN), F32),
                   jax.ShapeDtypeStruct((t_dim, GN), F32),
                   jax.ShapeDtypeStruct((SSM_GROUPS, t_dim, HEADS_PER_GROUP), F32),
                   jax.ShapeDtypeStruct((SSM_GROUPS, t_dim, HEADS_PER_GROUP), F32),
                   jax.ShapeDtypeStruct((SSM_GROUPS, 1, HEADS_PER_GROUP), F32)],
        scratch_shapes=[pltpu.VMEM((GROUP_COLS, SSM_STATE), F32)],
        sem=("parallel", "arbitrary"), args=[xbc, xbc, xbc, dtg, ag, dgx, states, dy], comm=comm)


NORM_GROUP = D_INNER // SSM_GROUPS


def _gate_norm_fwd(y, zx, nw, *, name, tm=256):
    t_dim = y.shape[0]
    row = pl.BlockSpec((tm, D_INNER), lambda i: (i, 0))

    def body(y_ref, z_ref, nw_ref, o_ref):
        z = z_ref[...]
        yz = y_ref[...] * (z * _sigmoid(z))
        for g in range(SSM_GROUPS):
            cols = slice(g * NORM_GROUP, (g + 1) * NORM_GROUP)
            yhat, _ = _rms(yz[:, cols])
            o_ref[:, cols] = (yhat * nw_ref[:, cols]).astype(BF16)

    return pl.pallas_call(
        body, name=name, grid=(t_dim // tm,), in_specs=[row, row, pl.BlockSpec((1, D_INNER), lambda i: (0, 0))],
        out_specs=row, out_shape=jax.ShapeDtypeStruct((t_dim, D_INNER), BF16),
        compiler_params=_params("parallel"),
    )(y, zx, nw)


def _gate_norm_bwd(y, zx, nw, dyn, *, name, tm=256):
    t_dim = y.shape[0]
    row = pl.BlockSpec((tm, D_INNER), lambda i: (i, 0))
    vec = pl.BlockSpec((1, D_INNER), lambda i: (0, 0))

    def body(y_ref, z_ref, nw_ref, dyn_ref, dy_ref, dz_ref, dnw_ref):
        @pl.when(pl.program_id(0) == 0)
        def _():
            dnw_ref[...] = jnp.zeros_like(dnw_ref)

        z = z_ref[...]
        yv = y_ref[...]
        sg = _sigmoid(z)
        silu_z = z * sg
        yz = yv * silu_z
        dyn_v = dyn_ref[...]
        for g in range(SSM_GROUPS):
            cols = slice(g * NORM_GROUP, (g + 1) * NORM_GROUP)
            yhat, r = _rms(yz[:, cols])
            dn = dyn_v[:, cols]
            dnw_ref[:, cols] += jnp.sum(dn * yhat, axis=0, keepdims=True)
            dyhat = dn * nw_ref[:, cols]
            dyz = r * (dyhat - yhat * jnp.mean(dyhat * yhat, axis=-1, keepdims=True))
            dy_ref[:, cols] = dyz * silu_z[:, cols]
            dz_ref[:, cols] = dyz * yv[:, cols] * _dsilu(z[:, cols], sg[:, cols])

    return pl.pallas_call(
        body, name=name, grid=(t_dim // tm,), in_specs=[row, row, vec, row],
        out_specs=[row, row, vec],
        out_shape=[jax.ShapeDtypeStruct((t_dim, D_INNER), F32), jax.ShapeDtypeStruct((t_dim, ZX_DIM), F32),
                   jax.ShapeDtypeStruct((1, D_INNER), F32)],
        compiler_params=_params("arbitrary"),
    )(y, zx, nw, dyn)


def _rope(t, cos2, sin2, *, name, tm=256):
    t_dim, width = t.shape
    half = ATT_HEAD_DIM // 2
    reps = width // 128

    def body(t_ref, cos_ref, sin_ref, o_ref):
        x = t_ref[...]
        lane = lax.broadcasted_iota(jnp.int32, (tm, width), 1)
        first = (lane % ATT_HEAD_DIM) < half
        rot = jnp.where(first, -pltpu.roll(x, width - half, 1), pltpu.roll(x, half, 1))
        o_ref[...] = x * jnp.tile(cos_ref[...], (1, reps)) + rot * jnp.tile(sin_ref[...], (1, reps))

    row = pl.BlockSpec((tm, width), lambda i: (i, 0))
    tab = pl.BlockSpec((tm, 128), lambda i: (i, 0))
    return pl.pallas_call(
        body, name=name, grid=(t_dim // tm,), in_specs=[row, tab, tab], out_specs=row,
        out_shape=jax.ShapeDtypeStruct((t_dim, width), F32), compiler_params=_params("parallel"),
    )(t, cos2, sin2)


def _attn_masks(n):
    row = lax.broadcasted_iota(jnp.int32, (WINDOW, WINDOW), 0)
    col = lax.broadcasted_iota(jnp.int32, (WINDOW, WINDOW), 1)
    return col <= row, (col > row) & (n > 0)


def _attn_fwd(q, k, v, sinks, *, name, comm=None):
    t_dim = q.shape[0]

    def body(q_ref, kc_ref, kp_ref, vc_ref, vp_ref, s_ref, o_ref, l_ref):
        n = pl.program_id(0)
        mask_c, mask_p = _attn_masks(n)
        lane = lax.broadcasted_iota(jnp.int32, (WINDOW, N_Q_HEADS), 1)
        lse = jnp.zeros((WINDOW, N_Q_HEADS), F32)
        for kvh in range(N_KV_HEADS):
            kcols = slice(kvh * ATT_HEAD_DIM, (kvh + 1) * ATT_HEAD_DIM)
            kc, kp = kc_ref[:, kcols].astype(BF16), kp_ref[:, kcols].astype(BF16)
            vc, vp = vc_ref[:, kcols].astype(BF16), vp_ref[:, kcols].astype(BF16)
            for g in range(Q_PER_KV):
                h = kvh * Q_PER_KV + g
                cols = slice(h * ATT_HEAD_DIM, (h + 1) * ATT_HEAD_DIM)
                qh = q_ref[:, cols].astype(BF16)
                sc = jnp.where(mask_c, _dot(qh, kc, NT) * ATT_SCALE, NEG_BIG)
                sp = jnp.where(mask_p, _dot(qh, kp, NT) * ATT_SCALE, NEG_BIG)
                sink = s_ref[:, h:h + 1]
                m = jnp.maximum(jnp.maximum(jnp.max(sc, axis=1, keepdims=True), jnp.max(sp, axis=1, keepdims=True)), sink)
                pc = jnp.exp(sc - m)
                pp = jnp.exp(sp - m)
                den = jnp.sum(pc, axis=1, keepdims=True) + jnp.sum(pp, axis=1, keepdims=True) + jnp.exp(sink - m)
                o_ref[:, cols] = (_dot(pc, vc) + _dot(pp, vp)) / den
                lse = jnp.where(lane == h, m + jnp.log(den), lse)
        l_ref[...] = lse

    cur = lambda w: pl.BlockSpec((WINDOW, w), lambda n: (n, 0))
    prv = lambda w: pl.BlockSpec((WINDOW, w), lambda n: (jnp.maximum(n - 1, 0), 0))
    return _call(
        body, name=name, grid=(t_dim // WINDOW,),
        in_specs=[cur(D_MODEL), cur(KV_DIM), prv(KV_DIM), cur(KV_DIM), prv(KV_DIM), pl.BlockSpec((1, N_Q_HEADS), lambda n: (0, 0))],
        out_specs=[cur(D_MODEL), cur(N_Q_HEADS)],
        out_shape=[jax.ShapeDtypeStruct((t_dim, D_MODEL), F32), jax.ShapeDtypeStruct((t_dim, N_Q_HEADS), F32)],
        sem=("parallel",), args=[q, k, k, v, v, sinks], comm=comm)


def _attn_bwd(q, k, v, sinks, o, lse, do, *, name, comm=None):
    t_dim = q.shape[0]

    def body(q_ref, kc_ref, kp_ref, vc_ref, vp_ref, s_ref, o_ref, l_ref, do_ref, dq_ref, dk_ref, dv_ref, dsink_ref):
        n = pl.program_id(0)

        @pl.when(n == 0)
        def _():
            dk_ref[...] = jnp.zeros_like(dk_ref)
            dv_ref[...] = jnp.zeros_like(dv_ref)
            dsink_ref[...] = jnp.zeros_like(dsink_ref)

        mask_c, mask_p = _attn_masks(n)
        lane_row = lax.broadcasted_iota(jnp.int32, (1, N_Q_HEADS), 1)
        rows_c = pl.ds(pl.multiple_of(n * WINDOW, WINDOW), WINDOW)
        rows_p = pl.ds(pl.multiple_of(jnp.maximum(n - 1, 0) * WINDOW, WINDOW), WINDOW)
        dsink = jnp.zeros((1, N_Q_HEADS), F32)
        for kvh in range(N_KV_HEADS):
            kcols = slice(kvh * ATT_HEAD_DIM, (kvh + 1) * ATT_HEAD_DIM)
            kc, kp = kc_ref[:, kcols].astype(BF16), kp_ref[:, kcols].astype(BF16)
            vc, vp = vc_ref[:, kcols].astype(BF16), vp_ref[:, kcols].astype(BF16)
            dkc = jnp.zeros((WINDOW, ATT_HEAD_DIM), F32)
            dkp = jnp.zeros((WINDOW, ATT_HEAD_DIM), F32)
            dvc = jnp.zeros((WINDOW, ATT_HEAD_DIM), F32)
            dvp = jnp.zeros((WINDOW, ATT_HEAD_DIM), F32)
            for g in range(Q_PER_KV):
                h = kvh * Q_PER_KV + g
                cols = slice(h * ATT_HEAD_DIM, (h + 1) * ATT_HEAD_DIM)
                qh = q_ref[:, cols].astype(BF16)
                lh = l_ref[:, h:h + 1]
                pc = jnp.exp(jnp.where(mask_c, _dot(qh, kc, NT) * ATT_SCALE, NEG_BIG) - lh)
                pp = jnp.exp(jnp.where(mask_p, _dot(qh, kp, NT) * ATT_SCALE, NEG_BIG) - lh)
                doh = do_ref[:, cols]
                delta = jnp.sum(doh * o_ref[:, cols], axis=1, keepdims=True)
                dsc = pc * (_dot(doh, vc, NT) - delta)
                dsp = pp * (_dot(doh, vp, NT) - delta)
                dq_ref[:, cols] = (_dot(dsc, kc) + _dot(dsp, kp)) * ATT_SCALE
                dkc = dkc + _dot(dsc, qh, TN) * ATT_SCALE
                dkp = dkp + _dot(dsp, qh, TN) * ATT_SCALE
                dvc = dvc + _dot(pc, doh, TN)
                dvp = dvp + _dot(pp, doh, TN)
                p_sink = jnp.exp(s_ref[:, h:h + 1] - lh)
                dsink = jnp.where(lane_row == h, -jnp.sum(p_sink * delta, axis=0, keepdims=True), dsink)
            dk_ref[rows_c, kcols] += dkc
            dk_ref[rows_p, kcols] += dkp
            dv_ref[rows_c, kcols] += dvc
            dv_ref[rows_p, kcols] += dvp
        dsink_ref[...] += dsink

    cur = lambda w: pl.BlockSpec((WINDOW, w), lambda n: (n, 0))
    prv = lambda w: pl.BlockSpec((WINDOW, w), lambda n: (jnp.maximum(n - 1, 0), 0))
    whole = pl.BlockSpec((t_dim, KV_DIM), lambda n: (0, 0))
    svec = pl.BlockSpec((1, N_Q_HEADS), lambda n: (0, 0))
    return _call(
        body, name=name, grid=(t_dim // WINDOW,),
        in_specs=[cur(D_MODEL), cur(KV_DIM), prv(KV_DIM), cur(KV_DIM), prv(KV_DIM), svec, cur(D_MODEL), cur(N_Q_HEADS), cur(D_MODEL)],
        out_specs=[cur(D_MODEL), whole, whole, svec],
        out_shape=[jax.ShapeDtypeStruct((t_dim, D_MODEL), F32), jax.ShapeDtypeStruct((t_dim, KV_DIM), F32),
                   jax.ShapeDtypeStruct((t_dim, KV_DIM), F32), jax.ShapeDtypeStruct((1, N_Q_HEADS), F32)],
        sem=("arbitrary",), args=[q, k, k, v, v, sinks, o, lse, do], comm=comm)


def _loss_head(x, nw, target, *, name, tm=256):
    t_dim, d_dim = x.shape
    row = pl.BlockSpec((tm, d_dim), lambda i: (i, 0))
    vec = pl.BlockSpec((1, d_dim), lambda i: (0, 0))

    def body(x_ref, nw_ref, tgt_ref, loss_ref, dx_ref, dnw_ref):
        @pl.when(pl.program_id(0) == 0)
        def _():
            loss_ref[...] = jnp.zeros_like(loss_ref)
            dnw_ref[...] = jnp.zeros_like(dnw_ref)

        xhat, r = _rms(x_ref[...])
        err = xhat * nw_ref[...] - tgt_ref[...]
        loss_ref[...] += 0.5 * _sum_all(jnp.mean(err * err, axis=-1, keepdims=True))
        dy = err * (1.0 / d_dim)
        dnw_ref[...] += jnp.sum(dy * xhat, axis=0, keepdims=True)
        dxhat = dy * nw_ref[...]
        dx_ref[...] = r * (dxhat - xhat * jnp.mean(dxhat * xhat, axis=-1, keepdims=True))

    return pl.pallas_call(
        body, name=name, grid=(t_dim // tm,), in_specs=[row, vec, row],
        out_specs=[pl.BlockSpec((1, 1), lambda i: (0, 0)), row, vec],
        out_shape=[jax.ShapeDtypeStruct((1, 1), F32), jax.ShapeDtypeStruct((t_dim, d_dim), F32),
                   jax.ShapeDtypeStruct((1, d_dim), F32)],
        compiler_params=_params("arbitrary"),
    )(x, nw, target)


def _rope_tables():
    pos = jnp.arange(SEQ, dtype=F32)
    inv = 1.0 / (ROPE_THETA ** (jnp.arange(0, ATT_HEAD_DIM, 2, dtype=F32) / ATT_HEAD_DIM))
    ang = pos[:, None] * inv[None, :]
    cos, sin = jnp.cos(ang), jnp.sin(ang)
    return jnp.tile(cos, (1, 4)), jnp.tile(sin, (1, 4))


def _to_groups(t):
    return t.reshape(t.shape[0], SSM_GROUPS, HEADS_PER_GROUP).transpose(1, 0, 2)


def _from_groups(t):
    return t.transpose(1, 0, 2).reshape(t.shape[1], SSM_HEADS)


def _forward_backward(x0, target, net):
    w = net.w
    nw = [[w("norm_w")[l, i][None, :] for i in range(3)] for l in range(2)]
    cos2, sin2 = _rope_tables()
    ffn_norm = [nw[0][0], nw[0][2], nw[1][0], nw[1][2]]

    def ffn_f(x, blk):
        name = f"ffn_fwd{blk}"
        return _ffn_fwd(x, ffn_norm[blk], w(f"gate{blk}"), w(f"up{blk}"), w(f"down{blk}"), name=name, comm=net.carry(name))

    x1 = ffn_f(x0, 0)
    zx, h1 = _norm_mm(x1, nw[0][1], w("wzx"), None, name="ssm_in_proj", comm=net.carry("ssm_in_proj"))
    dtr = _mm(h1, w("wdt"), name="ssm_dt_proj")
    xbc = _conv_fwd(zx, w("conv_w"), w("conv_b"), name="ssm_conv_fwd", comm=net.carry("ssm_conv_fwd"))
    dt, a_dt = _dt_prep(dtr, w("dt_bias"), w("a_log"), name="ssm_dt_prep")
    dtg, ag = _to_groups(dt), _to_groups(a_dt)
    dg = jnp.repeat(w("d_skip").reshape(SSM_GROUPS, 1, HEADS_PER_GROUP), SSM_HEAD_DIM, axis=2)
    y_ssd, states = _ssd_fwd(xbc, dtg, ag, dg, name="ssd_fwd", comm=net.carry("ssd_fwd"))
    yn = _gate_norm_fwd(y_ssd, zx, w("ssm_norm_w"), name="ssm_gate_norm_fwd")
    x2 = _mm(yn, w("wout"), res=x1, name="ssm_out_proj", comm=net.carry("ssm_out_proj"))
    x3 = ffn_f(x2, 1)
    k_pre, hk = _norm_mm(x3, w("kv_norm_w"), w("wk"), w("b_k"), name="k_proj")
    v = _mm(hk, w("wv"), bias=w("b_v"), name="v_proj")
    k_rot = _rope(k_pre, cos2, sin2, name="k_rope")
    x4 = ffn_f(x3, 2)
    q_pre, h4 = _norm_mm(x4, nw[1][1], w("wq"), w("b_q"), name="q_proj")
    q_rot = _rope(q_pre, cos2, sin2, name="q_rope")
    att, lse = _attn_fwd(q_rot, k_rot, v, w("sinks"), name="attn_fwd", comm=net.carry("attn_fwd"))
    x5 = _mm(att, w("wo"), bias=w("b_o"), res=x4, name="attn_out_proj")
    x6 = ffn_f(x5, 3)
    loss, dx6, d_final = _loss_head(x6, w("final_norm_w"), target, name="loss_head")

    d_norm = [[None] * 3 for _ in range(2)]

    def ffn_b(x, dout, blk):
        h, dob = _ffn_bwd_prep(x, ffn_norm[blk], dout, name=f"ffn_bwd_prep{blk}")
        name = f"ffn_bwd{blk}"
        dh, gg, gu, gd = _ffn_bwd(h, dob, w(f"gate{blk}"), w(f"up{blk}"), w(f"down{blk}"), name=name, comm=net.carry(name))
        net.give(f"gate{blk}", gg)
        net.give(f"up{blk}", gu)
        net.give(f"down{blk}", gd)
        return _norm_bwd(x, ffn_norm[blk], dh, [dout], name=f"ffn_norm_bwd{blk}")

    by_rows = lambda g: g.reshape(N_DEV, g.shape[0] // N_DEV, g.shape[1])
    dx5, d_norm[1][2] = ffn_b(x5, dx6, 3)
    d_att = _mm(dx5, w("wo"), dims="nt", name="attn_out_proj_dx", comm=net.carry("attn_out_proj_dx"))
    net.give("w_o", by_rows(_mm(att, dx5, dims="tn", out_dtype=BF16, name="attn_out_proj_dw")))
    d_bo = _colsum(dx5, name="attn_bo_grad")
    dq_rot, dk_rot, dv, d_sinks = _attn_bwd(q_rot, k_rot, v, w("sinks"), att, lse, d_att, name="attn_bwd", comm=net.carry("attn_bwd"))
    dq = _rope(dq_rot, cos2, -sin2, name="q_rope_bwd")
    dk = _rope(dk_rot, cos2, -sin2, name="k_rope_bwd")
    dh4 = _mm(dq, w("wq"), dims="nt", name="q_proj_dx")
    net.give("w_q", by_rows(_mm(h4, dq, dims="tn", out_dtype=BF16, name="q_proj_dw")))
    d_bq = _colsum(dq, name="attn_bq_grad")
    dx4, d_norm[1][1] = _norm_bwd(x4, nw[1][1], dh4, [dx5], name="attn_norm_bwd")
    dx3a, d_norm[1][0] = ffn_b(x3, dx4, 2)
    dhk = _mm(dk, w("wk"), dims="nt", name="k_proj_dx", comm=net.carry("k_proj_dx"))
    dhk = _mm(dv, w("wv"), dims="nt", res=dhk, name="v_proj_dx")
    net.give("w_k", by_rows(_mm(hk, dk, dims="tn", out_dtype=BF16, name="k_proj_dw")))
    net.give("w_v", by_rows(_mm(hk, dv, dims="tn", out_dtype=BF16, name="v_proj_dw")))
    d_bk = _colsum(dk, name="bk_grad")
    d_bv = _colsum(dv, name="bv_grad")
    dx3, d_kvn = _norm_bwd(x3, w("kv_norm_w"), dhk, [dx3a], name="kv_norm_bwd")
    dx2, d_norm[0][2] = ffn_b(x2, dx3, 1)
    d_yn = _mm(dx2, w("wout"), dims="nt", name="ssm_out_proj_dx", comm=net.carry("ssm_out_proj_dx"))
    net.give("w_out", by_rows(_mm(yn, dx2, dims="tn", out_dtype=BF16, name="ssm_out_proj_dw")))
    dy_ssd, dzx, d_ssm_norm = _gate_norm_bwd(y_ssd, zx, w("ssm_norm_w"), d_yn, name="ssm_gate_norm_bwd")
    dxs, d_b, d_c, ddtg, dag, ddg = _ssd_bwd(xbc, dtg, ag, dg, states, dy_ssd, name="ssd_bwd", comm=net.carry("ssd_bwd"))
    dzx, d_conv_w, d_conv_b = _conv_bwd(zx, w("conv_w"), w("conv_b"), dxs, d_b, d_c, dzx, name="ssm_conv_bwd",
                                        comm=net.carry("ssm_conv_bwd"))
    ddtr, d_dt_bias, d_a_log = _dt_bwd(dtr, w("dt_bias"), w("a_log"), dt, _from_groups(ddtg), _from_groups(dag), name="ssm_dt_bwd")
    dh1 = _mm(dzx, w("wzx"), dims="nt", name="ssm_in_proj_dx")
    dh1 = _mm(ddtr, w("wdt"), dims="nt", res=dh1, name="ssm_dt_proj_dx")
    g_zx = _mm(h1, dzx, dims="tn", out_dtype=BF16, name="ssm_in_proj_dw")
    g_dt = _mm(h1, ddtr, dims="tn", out_dtype=BF16, name="ssm_dt_proj_dw")
    net.give("w_in", jnp.concatenate([g_zx, g_dt], axis=1).reshape(D_MODEL, N_DEV, IN_PROJ_SHARD).transpose(1, 0, 2))
    dx1, d_norm[0][1] = _norm_bwd(x1, nw[0][1], dh1, [dx2], name="ssm_norm_bwd", comm=net.carry("ssm_norm_bwd"))
    dx0, d_norm[0][0] = ffn_b(x0, dx1, 0)

    small = {"norm_w": jnp.concatenate([d_norm[l][i] for l in range(2) for i in range(3)], axis=0),
             "ssm_conv_w": d_conv_w, "ssm_conv_b": d_conv_b, "ssm_dt_bias": d_dt_bias, "ssm_a_log": d_a_log,
             "ssm_d": ddg.reshape(1, SSM_HEADS), "ssm_norm_w": d_ssm_norm, "kv_norm_w": d_kvn,
             "b_k": d_bk, "b_v": d_bv, "attn_b_q": d_bq, "attn_sinks": d_sinks, "attn_b_o": d_bo, "final_norm_w": d_final}
    return loss, dx0, small


BLOCK_BYTES = 1 << 20


def _row_tile(rows, cols):
    for t in (512, 256, 128, 64, 32, 16):
        if rows % t == 0 and t * cols * 4 <= BLOCK_BYTES:
            return t
    return rows


def _cast_bf16(x, blk, *, name):
    _, rows, cols = x.shape
    tm = _row_tile(rows, cols)

    def body(x_ref, o_ref):
        o_ref[...] = x_ref[...].astype(BF16)

    return pl.pallas_call(body, name=name, grid=(rows // tm,), in_specs=[pl.BlockSpec((None, tm, cols), lambda i: (blk, i, 0))],
                          out_specs=pl.BlockSpec((tm, cols), lambda i: (i, 0)),
                          out_shape=jax.ShapeDtypeStruct((rows, cols), BF16), compiler_params=_params("parallel"))(x)


def _pair_add(mine, theirs, *, name):
    n_slots, rows, cols = mine.shape
    tm = _row_tile(rows, cols)
    spec = pl.BlockSpec((None, tm, cols), lambda s, i: (s, i, 0))

    def body(a_ref, b_ref, o_ref):
        o_ref[...] = (a_ref[...].astype(F32) + b_ref[...].astype(F32)).astype(BF16)

    return pl.pallas_call(body, name=name, grid=(n_slots, rows // tm), in_specs=[spec, spec], out_specs=spec,
                          out_shape=jax.ShapeDtypeStruct(mine.shape, BF16), compiler_params=_params("parallel", "parallel"))(mine, theirs)


def _adam_update(g, w, m, v):
    m = ADAM_B1 * m + (1.0 - ADAM_B1) * g
    v = ADAM_B2 * v + (1.0 - ADAM_B2) * (g * g)
    m_hat = m / (1.0 - ADAM_B1 ** ADAM_STEP)
    v_hat = v / (1.0 - ADAM_B2 ** ADAM_STEP)
    delta = -ADAM_LR * (m_hat / (jnp.sqrt(v_hat) + ADAM_EPS) + ADAM_WD * w)
    return delta, m, v


def _adamw(parts, w, m, v, blk, prev, *, name):
    n_blk, rows, cols = w.shape
    tm = _row_tile(rows, cols)
    spec = pl.BlockSpec((None, tm, cols), lambda i: (blk, i, 0))
    n_prev = len(prev)

    n_parts = parts.shape[0]

    def body(p_ref, w_ref, m_ref, v_ref, *refs):
        g_ref, d_ref, nm_ref, nv_ref = refs[n_prev:]
        g = p_ref[0].astype(F32)
        for s in range(1, n_parts):
            g = g + p_ref[s].astype(F32)
        delta, nm, nv = _adam_update(g, w_ref[...], m_ref[...], v_ref[...])
        g_ref[...] = g
        d_ref[...] = delta
        nm_ref[...] = nm
        nv_ref[...] = nv

    return pl.pallas_call(
        body, name=name, grid=(rows // tm,),
        in_specs=[pl.BlockSpec((n_parts, tm, cols), lambda i: (0, i, 0)), spec, spec, spec] + [pl.BlockSpec(memory_space=pl.ANY)] * n_prev,
        out_specs=[spec] * 4, out_shape=[jax.ShapeDtypeStruct((n_blk, rows, cols), F32)] * 4,
        input_output_aliases={4 + q: q for q in range(n_prev)},
        compiler_params=_params("parallel"),
    )(parts, w, m, v, *prev)


def _sum_parts(parts, *, name):
    def body(p_ref, o_ref):
        g = p_ref[0]
        for s in range(1, N_DEV):
            g = g + p_ref[s]
        o_ref[...] = g

    return pl.pallas_call(body, name=name, out_shape=jax.ShapeDtypeStruct(parts.shape[1:], F32), compiler_params=_params())(parts)


def _adamw_packed(g, w, m, v, *, name):
    def body(g_ref, w_ref, m_ref, v_ref, d_ref, nm_ref, nv_ref):
        delta, nm, nv = _adam_update(g_ref[...], w_ref[...], m_ref[...], v_ref[...])
        d_ref[...] = delta
        nm_ref[...] = nm
        nv_ref[...] = nv

    return pl.pallas_call(body, name=name, out_shape=[jax.ShapeDtypeStruct(g.shape, F32)] * 3, compiler_params=_params())(g, w, m, v)


SUBLANES = 8


def _pack(arrs):
    rows = []
    for a in arrs:
        flat = a.reshape(-1)
        pad = (-flat.shape[0]) % LANES
        rows.append(jnp.pad(flat, (0, pad)).reshape(-1, LANES))
    out = jnp.concatenate(rows, axis=0)
    return jnp.pad(out, ((0, (-out.shape[0]) % SUBLANES), (0, 0)))


def _unpack(packed, shapes):
    outs, r = [], 0
    for shp in shapes:
        n = math.prod(shp)
        nr = -(-n // LANES)
        outs.append(packed[r:r + nr].reshape(-1)[:n].reshape(shp))
        r += nr
    return outs


WEIGHT_NAMES = ("norm_w", "ffn_w_gate", "ffn_w_up", "ffn_w_down", "ssm_w_in", "ssm_conv_w", "ssm_conv_b", "ssm_dt_bias",
                "ssm_a_log", "ssm_d", "ssm_norm_w", "ssm_w_out", "kv_norm_w", "w_k", "b_k", "w_v", "b_v", "attn_w_q",
                "attn_b_q", "attn_sinks", "attn_w_o", "attn_b_o", "final_norm_w")
MATRIX_NAMES = ("ffn_w_gate", "ffn_w_up", "ffn_w_down", "ssm_w_in", "ssm_w_out", "w_k", "w_v", "attn_w_q", "attn_w_o")
VECTOR_NAMES = tuple(n for n in WEIGHT_NAMES if n not in MATRIX_NAMES)
SHARDED_VECTORS = ("norm_w", "ssm_conv_w", "ssm_conv_b", "ssm_norm_w")


GATHER_PLAN = {
    "gather_stage0": ("gate0", "up0", "down0", "vec"),
    "ffn_fwd0": ("w_in",),
    "ssm_in_proj": ("w_out", "gate1"),
    "ssm_conv_fwd": ("w_k", "w_v"),
    "ssd_fwd": ("up1", "down1", "gate2"),
    "ssm_out_proj": ("w_q", "w_o"),
    "ffn_fwd1": ("up2", "down2"),
    "ffn_fwd2": ("gate3",),
    "attn_fwd": ("up3", "down3"),
}
PAIR_PLAN = {
    "attn_out_proj_dx": ("gate3", "up3", "down3"),
    "ffn_bwd2": ("w_q", "w_o"),
    "k_proj_dx": ("gate2", "up2", "down2"),
    "ssm_out_proj_dx": ("w_k", "w_v", "gate1", "up1", "down1"),
    "ssd_bwd": ("w_out",),
    "ssm_norm_bwd": ("w_in",),
    "pair_last_grads": ("gate0", "up0", "down0"),
}
CHIP_PLAN = {
    "attn_bwd": ("gate3", "up3", "down3"),
    "ffn_bwd1": ("gate2", "up2", "down2", "w_q", "w_o"),
    "ssd_bwd": ("gate1", "up1", "down1", "w_k", "w_v"),
    "ssm_conv_bwd": ("w_out",),
    "ffn_bwd0": ("w_in",),
    "exchange_last_grads": ("gate0", "up0", "down0"),
}
FFN_PARAMS = {"gate": "ffn_w_gate", "up": "ffn_w_up", "down": "ffn_w_down"}
SINGLE_MATRICES = {"w_in": "ssm_w_in", "w_out": "ssm_w_out", "w_k": "w_k", "w_v": "w_v", "w_q": "attn_w_q", "w_o": "attn_w_o"}


class _MeshNet:
    def __init__(self, p):
        self.p = p
        self.views = {n: p[n].reshape((-1,) + p[n].shape[-2:]) for n in MATRIX_NAMES}
        self.local = {"vec": _pack([p[n] for n in SHARDED_VECTORS])}
        for short, n in FFN_PARAMS.items():
            for k in range(N_FFN):
                self.local[f"{short}{k}"] = _cast_bf16(self.views[n], k, name=f"cast_{short}{k}")
        for short, n in SINGLE_MATRICES.items():
            self.local[short] = _cast_bf16(self.views[n], 0, name=f"cast_{short}")
        self.gathered_at, self.pairs_at, self.parts_at, self.grads, self.cache = {}, {}, {}, {}, {}

    def carry(self, name):
        comms = []
        if name in GATHER_PLAN:
            keys, comm = GATHER_PLAN[name], _Gather([self.local[k] for k in GATHER_PLAN[name]])
            self.gathered_at.update({k: (comm, i) for i, k in enumerate(keys)})
            comms.append(comm)
        if name in CHIP_PLAN:
            sums = []
            for k in CHIP_PLAN[name]:
                comm, i = self.pairs_at[k]
                sums.append(_pair_add(comm.results[2 * i], comm.results[2 * i + 1], name=f"pair_add_{k}"))
            comm = _ChipExchange(sums)
            self.parts_at.update({k: (comm, i) for i, k in enumerate(CHIP_PLAN[name])})
            comms.append(comm)
        if name in PAIR_PLAN:
            keys, comm = PAIR_PLAN[name], _PairSwap([self.grads[k] for k in PAIR_PLAN[name]])
            self.pairs_at.update({k: (comm, i) for i, k in enumerate(keys)})
            comms.append(comm)
        return comms

    def run(self, name):
        for comm in self.carry(name):
            _run_exchange(comm, name=name)

    def give(self, key, grad):
        self.grads[key] = grad

    def parts(self, key):
        comm, i = self.parts_at[key]
        return comm.results[i]

    def _gathered(self, key):
        comm, i = self.gathered_at[key]
        return comm.results[i]

    def _vec(self, r0, r1, lead):
        t = self._gathered("vec")[:, r0:r1, :].reshape(N_DEV, lead, -1)
        return t.transpose(1, 0, 2).reshape(lead, -1)

    def _derive(self, name):
        p = self.p
        if name[:-1] in FFN_PARAMS:
            return self._gathered(name)
        if name in ("wzx", "wdt"):
            w_in = self._gathered("w_in").transpose(1, 0, 2).reshape(D_MODEL, N_DEV * IN_PROJ_SHARD)
            return w_in[:, :ZX_DIM] if name == "wzx" else w_in[:, ZX_DIM:]
        by_rows = {"wout": "w_out", "wk": "w_k", "wv": "w_v", "wq": "w_q", "wo": "w_o"}
        if name in by_rows:
            g = self._gathered(by_rows[name])
            return g.reshape(N_DEV * g.shape[1], g.shape[2])
        vectors = {"norm_w": lambda: self._vec(0, 6, 6).reshape(2, 3, D_MODEL), "conv_w": lambda: self._vec(6, 18, CONV_WIDTH),
                   "conv_b": lambda: self._vec(18, 21, 1), "ssm_norm_w": lambda: self._vec(21, 23, 1)}
        if name in vectors:
            return vectors[name]()
        replicated = {"dt_bias": p["ssm_dt_bias"], "a_log": p["ssm_a_log"], "d_skip": p["ssm_d"], "kv_norm_w": p["kv_norm_w"][None],
                      "b_k": p["b_k"][None], "b_v": p["b_v"][None], "b_q": p["attn_b_q"], "sinks": p["attn_sinks"],
                      "b_o": p["attn_b_o"], "final_norm_w": p["final_norm_w"][None]}
        return replicated[name]

    def w(self, name):
        if name not in self.cache:
            self.cache[name] = self._derive(name)
        return self.cache[name]


def _step(x, target, p, m, v):
    pos = _slot(_position())
    net = _MeshNet(p)
    net.run("gather_stage0")
    loss, grad_x, small = _forward_backward(x, target, net)
    net.run("pair_last_grads")
    net.run("exchange_last_grads")
    vec_gather = _Gather([_pack([small[n] for n in VECTOR_NAMES])])
    vec_sum = _sum_parts(_run_exchange(vec_gather, name="gather_vector_grads")[0], name="sum_vector_grads")
    full_shapes = {"norm_w": (2, 3, D_MODEL), "ssm_conv_w": (1, CONV_WIDTH, CONV_DIM), "ssm_conv_b": (1, CONV_DIM),
                   "ssm_norm_w": (1, D_INNER)}
    vec_full = dict(zip(VECTOR_NAMES, _unpack(vec_sum, [full_shapes.get(n, p[n].shape) for n in VECTOR_NAMES])))

    grads, deltas, new_m, new_v = {}, {}, {}, {}
    view = lambda d, n: d[n].reshape(net.views[n].shape)
    for short, n in FFN_PARAMS.items():
        outs = []
        for k in reversed(range(N_FFN)):
            outs = _adamw(net.parts(f"{short}{k}"), net.views[n], view(m, n), view(v, n), k, outs, name=f"adamw_{short}{k}")
        grads[n], deltas[n], new_m[n], new_v[n] = [o.reshape(p[n].shape) for o in outs]
    for short, n in SINGLE_MATRICES.items():
        outs = _adamw(net.parts(short), net.views[n], view(m, n), view(v, n), 0, [], name=f"adamw_{short}")
        grads[n], deltas[n], new_m[n], new_v[n] = [o.reshape(p[n].shape) for o in outs]
    for n in VECTOR_NAMES:
        g = vec_full[n]
        if n in SHARDED_VECTORS:
            per = p[n].shape[-1]
            g = lax.dynamic_slice_in_dim(g, pos * per, per, axis=g.ndim - 1)
        grads[n] = g
    packed = _adamw_packed(*[_pack([d[n] for n in VECTOR_NAMES]) for d in (grads, p, m, v)], name="adamw_vectors")
    shapes = [p[n].shape for n in VECTOR_NAMES]
    for d, pk in zip((deltas, new_m, new_v), packed):
        d.update(zip(VECTOR_NAMES, _unpack(pk, shapes)))
    return loss, grad_x, grads, deltas, new_m, new_v


def kernel(x, norm_w, ffn_w_gate, ffn_w_up, ffn_w_down, ssm_w_in, ssm_conv_w, ssm_conv_b, ssm_dt_bias, ssm_a_log, ssm_d, ssm_norm_w, ssm_w_out, kv_norm_w, w_k, b_k, w_v, b_v, attn_w_q, attn_b_q, attn_sinks, attn_w_o, attn_b_o, final_norm_w, loss_target, m_norm_w, m_ffn_w_gate, m_ffn_w_up, m_ffn_w_down, m_ssm_w_in, m_ssm_conv_w, m_ssm_conv_b, m_ssm_dt_bias, m_ssm_a_log, m_ssm_d, m_ssm_norm_w, m_ssm_w_out, m_kv_norm_w, m_w_k, m_b_k, m_w_v, m_b_v, m_attn_w_q, m_attn_b_q, m_attn_sinks, m_attn_w_o, m_attn_b_o, m_final_norm_w, v_norm_w, v_ffn_w_gate, v_ffn_w_up, v_ffn_w_down, v_ssm_w_in, v_ssm_conv_w, v_ssm_conv_b, v_ssm_dt_bias, v_ssm_a_log, v_ssm_d, v_ssm_norm_w, v_ssm_w_out, v_kv_norm_w, v_w_k, v_b_k, v_w_v, v_b_v, v_attn_w_q, v_attn_b_q, v_attn_sinks, v_attn_w_o, v_attn_b_o, v_final_norm_w):
    p = dict(zip(WEIGHT_NAMES, (norm_w, ffn_w_gate, ffn_w_up, ffn_w_down, ssm_w_in, ssm_conv_w, ssm_conv_b, ssm_dt_bias, ssm_a_log, ssm_d, ssm_norm_w, ssm_w_out, kv_norm_w, w_k, b_k, w_v, b_v, attn_w_q, attn_b_q, attn_sinks, attn_w_o, attn_b_o, final_norm_w)))
    m = dict(zip(WEIGHT_NAMES, (m_norm_w, m_ffn_w_gate, m_ffn_w_up, m_ffn_w_down, m_ssm_w_in, m_ssm_conv_w, m_ssm_conv_b, m_ssm_dt_bias, m_ssm_a_log, m_ssm_d, m_ssm_norm_w, m_ssm_w_out, m_kv_norm_w, m_w_k, m_b_k, m_w_v, m_b_v, m_attn_w_q, m_attn_b_q, m_attn_sinks, m_attn_w_o, m_attn_b_o, m_final_norm_w)))
    v = dict(zip(WEIGHT_NAMES, (v_norm_w, v_ffn_w_gate, v_ffn_w_up, v_ffn_w_down, v_ssm_w_in, v_ssm_conv_w, v_ssm_conv_b, v_ssm_dt_bias, v_ssm_a_log, v_ssm_d, v_ssm_norm_w, v_ssm_w_out, v_kv_norm_w, v_w_k, v_b_k, v_w_v, v_b_v, v_attn_w_q, v_attn_b_q, v_attn_sinks, v_attn_w_o, v_attn_b_o, v_final_norm_w)))
    loss, grad_x, grads, deltas, new_m, new_v = _step(x[0], loss_target[0], p, m, v)
    loss = lax.psum(loss[0, 0], ("x", "y", "c"))
    return (loss, grad_x[None], *[grads[n] for n in WEIGHT_NAMES], *[deltas[n] for n in WEIGHT_NAMES],
            *[new_m[n] for n in WEIGHT_NAMES], *[new_v[n] for n in WEIGHT_NAMES])
```

```python
import functools
import math

import jax
import jax.numpy as jnp
from jax import lax
from jax.experimental import pallas as pl
from jax.experimental.pallas import tpu as pltpu

F32 = jnp.float32
BF16 = jnp.bfloat16

N_DEV = 8
SEQ = 2048
D_MODEL = 1024
D_FF_SHARD = 352
N_FFN = 4
D_INNER = 2048
SSM_HEADS = 32
SSM_HEAD_DIM = 64
SSM_GROUPS = 4
HEADS_PER_GROUP = 8
SSM_STATE = 128
CHUNK = 128
N_CHUNKS = SEQ // CHUNK
GN = SSM_GROUPS * SSM_STATE
CONV_DIM = D_INNER + 2 * GN
CONV_WIDTH = 4
ZX_DIM = D_INNER + CONV_DIM
IN_PROJ_SHARD = 644
ATT_HEAD_DIM = 64
N_Q_HEADS = 16
N_KV_HEADS = 4
Q_PER_KV = 4
KV_DIM = N_KV_HEADS * ATT_HEAD_DIM
WINDOW = 128
ROPE_THETA = 10000.0
EPS = 1e-5
FFN_RES_WEIGHT = 0.5
ATT_SCALE = 1.0 / math.sqrt(ATT_HEAD_DIM)
NEG_BIG = -1e30

ADAM_LR = 0.001
ADAM_B1 = 0.9
ADAM_B2 = 0.999
ADAM_EPS = 1e-08
ADAM_WD = 0.01
ADAM_STEP = 10

VMEM_LIMIT_BYTES = 56 * 1024 * 1024

NN = (((1,), (0,)), ((), ()))
NT = (((1,), (1,)), ((), ()))
TN = (((0,), (0,)), ((), ()))
_DIMS = {"nn": NN, "nt": NT, "tn": TN}


def _params(*sem):
    return pltpu.CompilerParams(dimension_semantics=sem if sem else None, vmem_limit_bytes=VMEM_LIMIT_BYTES)


def _dot(a, b, dims=NN):
    return lax.dot_general(a.astype(BF16), b.astype(BF16), dims, preferred_element_type=F32)


def _dot_f32(a, b, dims=NN):
    return lax.dot_general(a, b, dims, precision=lax.Precision.HIGHEST, preferred_element_type=F32)


def _sigmoid(x):
    return 1.0 / (1.0 + jnp.exp(-x))


def _dsilu(x, s):
    return s * (1.0 + x * (1.0 - s))


def _rms(x):
    r = lax.rsqrt(jnp.mean(x * x, axis=-1, keepdims=True) + EPS)
    return x * r, r


def _sum_all(x):
    return jnp.sum(jnp.sum(x, axis=1, keepdims=True), axis=0, keepdims=True)


MESH = pl.DeviceIdType.MESH
N_PEERS = N_DEV - 1
N_CHIPS = N_DEV // 2


def _position():
    return lax.axis_index("x"), lax.axis_index("y"), lax.axis_index("c")


def _slot(p):
    return 4 * p[0] + 2 * p[1] + p[2]


class _Exchange:
    def __init__(self, arrays, out_shapes):
        n = len(arrays)
        self.arrays = list(arrays)
        self.out_shapes = out_shapes
        self.scratch = [pltpu.SemaphoreType.DMA((n, N_PEERS)), pltpu.SemaphoreType.DMA((n, N_PEERS)), pltpu.SemaphoreType.DMA((n,))]
        self.results = None


class _Gather(_Exchange):
    def __init__(self, arrays):
        super().__init__(arrays, [jax.ShapeDtypeStruct((N_DEV,) + a.shape, a.dtype) for a in arrays])

    def _plan(self, ins, outs, sems):
        send_sems, recv_sems, local_sems = sems
        x, y, c = _position()
        me, sibling = (x, y, c), (x, y, 1 - c)
        chips = [(1 - x, y), (x, 1 - y), (1 - x, 1 - y)]
        n = len(ins)

        def copy(a, k, block, to, src=None):
            dst = outs[a].at[_slot(block)]
            return pltpu.make_async_remote_copy(src_ref=dst if src is None else src, dst_ref=dst, send_sem=send_sems.at[a, k],
                                                recv_sem=recv_sems.at[a, k], device_id=to, device_id_type=MESH)

        mine = [pltpu.make_async_copy(ins[a], outs[a].at[_slot(me)], local_sems.at[a]) for a in range(n)]
        first = []
        for a in range(n):
            first.append(copy(a, 0, me, sibling, src=ins[a]))
            first += [copy(a, 1 + j, me, (*chip, c), src=ins[a]) for j, chip in enumerate(chips)]
        return n, c, me, sibling, chips, copy, mine, first

    def start(self, ins, outs, sems):
        _, _, _, _, _, _, mine, first = self._plan(ins, outs, sems)
        for cp in mine + first:
            cp.start()

    def finish(self, ins, outs, sems):
        n, c, me, sibling, chips, copy, mine, first = self._plan(ins, outs, sems)
        passed = []
        for j, chip in enumerate(chips):
            for a in range(n):
                copy(a, 1 + j, (*chip, c), me).wait_recv()
                fwd = copy(a, 4 + j, (*chip, c), sibling)
                fwd.start()
                passed.append(fwd)
        for a in range(n):
            copy(a, 0, sibling, me).wait_recv()
            for j, chip in enumerate(chips):
                copy(a, 4 + j, (*chip, 1 - c), me).wait_recv()
        for cp in first + passed:
            cp.wait_send()
        for cp in mine:
            cp.wait()


class _PairSwap(_Exchange):
    def __init__(self, arrays):
        n = len(arrays)
        self.arrays = list(arrays)
        self.out_shapes = [jax.ShapeDtypeStruct((N_CHIPS,) + a.shape[1:], a.dtype) for a in arrays]
        self.scratch = [pltpu.SemaphoreType.DMA((n, N_CHIPS)), pltpu.SemaphoreType.DMA((n, N_CHIPS))]
        self.results = None

    def _plan(self, ins, outs, sems):
        send_sems, recv_sems = sems
        x, y, c = _position()
        return [pltpu.make_async_remote_copy(src_ref=ins[a].at[2 * q + 1 - c], dst_ref=outs[a].at[q], send_sem=send_sems.at[a, q],
                                             recv_sem=recv_sems.at[a, q], device_id=(x, y, 1 - c), device_id_type=MESH)
                for a in range(len(ins)) for q in range(N_CHIPS)]

    def start(self, ins, outs, sems):
        for cp in self._plan(ins, outs, sems):
            cp.start()

    def finish(self, ins, outs, sems):
        for cp in self._plan(ins, outs, sems):
            cp.wait()


class _ChipExchange(_Exchange):
    def __init__(self, arrays):
        n = len(arrays)
        self.arrays = list(arrays)
        self.out_shapes = [jax.ShapeDtypeStruct(a.shape, a.dtype) for a in arrays]
        self.scratch = [pltpu.SemaphoreType.DMA((n, 3)), pltpu.SemaphoreType.DMA((n, 3)), pltpu.SemaphoreType.DMA((n,))]
        self.results = None

    def _plan(self, ins, outs, sems):
        send_sems, recv_sems, local_sems = sems
        x, y, c = _position()
        here = 2 * x + y
        chips = [(1 - x, y), (x, 1 - y), (1 - x, 1 - y)]
        n = len(ins)

        def copy(a, k, src_slot, dst_slot):
            return pltpu.make_async_remote_copy(src_ref=ins[a].at[src_slot], dst_ref=outs[a].at[dst_slot], send_sem=send_sems.at[a, k],
                                                recv_sem=recv_sems.at[a, k], device_id=(*chips[k], c), device_id_type=MESH)

        there = [2 * qx + qy for qx, qy in chips]
        mine = [pltpu.make_async_copy(ins[a].at[here], outs[a].at[here], local_sems.at[a]) for a in range(n)]
        sends = [copy(a, k, there[k], here) for a in range(n) for k in range(3)]
        arrivals = lambda: [copy(a, k, here, there[k]) for a in range(n) for k in range(3)]
        return mine, sends, arrivals

    def start(self, ins, outs, sems):
        mine, sends, _ = self._plan(ins, outs, sems)
        for cp in mine + sends:
            cp.start()

    def finish(self, ins, outs, sems):
        mine, sends, arrivals = self._plan(ins, outs, sems)
        for cp in arrivals():
            cp.wait_recv()
        for cp in sends:
            cp.wait_send()
        for cp in mine:
            cp.wait()


def _call(body, *, name, grid, in_specs, out_specs, out_shape, args, scratch_shapes=(), sem=(), comm=(), aliases=None):
    single = not isinstance(out_shape, (list, tuple))
    out_shape = [out_shape] if single else list(out_shape)
    out_specs = [out_specs] if single else list(out_specs)
    comms = list(comm or ())
    n_in, n_out, n_scr = len(args), len(out_shape), len(scratch_shapes)
    params = pltpu.CompilerParams(dimension_semantics=tuple(sem) if sem else None, vmem_limit_bytes=VMEM_LIMIT_BYTES)
    if not comms:
        res = pl.pallas_call(body, name=name, grid=grid, in_specs=list(in_specs), out_specs=out_specs, out_shape=out_shape,
                             scratch_shapes=list(scratch_shapes), input_output_aliases=aliases or {}, compiler_params=params)(*args)
        return res[0] if single else res
    counts = [n_in] + [len(c.arrays) for c in comms] + [n_out] + [len(c.out_shapes) for c in comms] + [n_scr] + [len(c.scratch) for c in comms]
    nc = len(comms)

    def carried(*refs):
        pos, groups = 0, []
        for cnt in counts:
            groups.append(refs[pos:pos + cnt])
            pos += cnt
        ins, c_ins = groups[0], groups[1:1 + nc]
        outs, c_outs = groups[1 + nc], groups[2 + nc:2 + 2 * nc]
        scr, c_sems = groups[2 + 2 * nc], groups[3 + 2 * nc:]
        ids = [pl.program_id(d) for d in range(len(grid))]
        is_first = functools.reduce(jnp.logical_and, [i == 0 for i in ids])
        is_last = functools.reduce(jnp.logical_and, [i == g - 1 for i, g in zip(ids, grid)])

        @pl.when(is_first)
        def _():
            for q, c in enumerate(comms):
                c.start(c_ins[q], c_outs[q], c_sems[q])

        body(*ins, *outs, *scr)

        @pl.when(is_last)
        def _():
            for q, c in enumerate(comms):
                c.finish(c_ins[q], c_outs[q], c_sems[q])

    anyspec = pl.BlockSpec(memory_space=pl.ANY)
    c_arrays = [a for c in comms for a in c.arrays]
    c_shapes = [s for c in comms for s in c.out_shapes]
    res = pl.pallas_call(
        carried, name=name, grid=grid, in_specs=list(in_specs) + [anyspec] * len(c_arrays), out_specs=out_specs + [anyspec] * len(c_shapes),
        out_shape=out_shape + c_shapes, scratch_shapes=list(scratch_shapes) + [s for c in comms for s in c.scratch],
        input_output_aliases=aliases or {}, compiler_params=params)(*args, *c_arrays)
    pos = n_out
    for c in comms:
        c.results = list(res[pos:pos + len(c.out_shapes)])
        pos += len(c.out_shapes)
    return res[0] if single else list(res[:n_out])


def _run_exchange(comm, *, name):
    def body(*refs):
        n_ci, n_co = len(comm.arrays), len(comm.out_shapes)
        ins, outs, sems = refs[:n_ci], refs[n_ci:n_ci + n_co], refs[n_ci + n_co:]
        comm.start(ins, outs, sems)
        comm.finish(ins, outs, sems)

    anyspec = pl.BlockSpec(memory_space=pl.ANY)
    comm.results = list(pl.pallas_call(
        body, name=name, in_specs=[anyspec] * len(comm.arrays), out_specs=[anyspec] * len(comm.out_shapes),
        out_shape=list(comm.out_shapes), scratch_shapes=list(comm.scratch))(*comm.arrays))
    return comm.results


def _mm(a, b, *, dims="nn", bias=None, res=None, out_dtype=F32, name, tm=512, tn=512, tk=1024, comm=None):
    if dims == "tn":
        k_dim, m_dim = a.shape
    else:
        m_dim, k_dim = a.shape
    n_dim = b.shape[0] if dims == "nt" else b.shape[1]
    tm, tn, tk = min(tm, m_dim), min(tn, n_dim), min(tk, k_dim)
    assert m_dim % tm == 0 and n_dim % tn == 0 and k_dim % tk == 0, (name, a.shape, b.shape)
    nk = k_dim // tk
    a_spec = pl.BlockSpec((tk, tm), lambda i, j, k: (k, i)) if dims == "tn" else pl.BlockSpec((tm, tk), lambda i, j, k: (i, k))
    b_spec = pl.BlockSpec((tn, tk), lambda i, j, k: (j, k)) if dims == "nt" else pl.BlockSpec((tk, tn), lambda i, j, k: (k, j))
    in_specs, args = [a_spec, b_spec], [a, b]
    if bias is not None:
        in_specs.append(pl.BlockSpec((1, tn), lambda i, j, k: (0, j)))
        args.append(bias)
    if res is not None:
        in_specs.append(pl.BlockSpec((tm, tn), lambda i, j, k: (i, j)))
        args.append(res)
    dn = _DIMS[dims]

    def body(*refs):
        a_ref, b_ref = refs[0], refs[1]
        o_ref, acc_ref = refs[-2], refs[-1]
        k = pl.program_id(2)

        @pl.when(k == 0)
        def _():
            acc_ref[...] = jnp.zeros_like(acc_ref)

        acc_ref[...] += _dot(a_ref[...], b_ref[...], dn)

        @pl.when(k == nk - 1)
        def _():
            r = acc_ref[...]
            pos = 2
            if bias is not None:
                r = r + refs[pos][...]
                pos += 1
            if res is not None:
                r = r + refs[pos][...]
            o_ref[...] = r.astype(out_dtype)

    return _call(
        body, name=name, grid=(m_dim // tm, n_dim // tn, nk), in_specs=in_specs,
        out_specs=pl.BlockSpec((tm, tn), lambda i, j, k: (i, j)),
        out_shape=jax.ShapeDtypeStruct((m_dim, n_dim), out_dtype),
        scratch_shapes=[pltpu.VMEM((tm, tn), F32)], sem=("parallel", "parallel", "arbitrary"), args=args, comm=comm)


def _norm_mm(x, nw, w, bias, *, name, tm=512, tn=512, comm=None):
    t_dim, d_dim = x.shape
    n_dim = w.shape[1]
    tn = min(tn, n_dim)
    assert t_dim % tm == 0 and n_dim % tn == 0
    has_bias = bias is not None
    in_specs = [pl.BlockSpec((tm, d_dim), lambda i, j: (i, 0)), pl.BlockSpec((1, d_dim), lambda i, j: (0, 0)),
                pl.BlockSpec((d_dim, tn), lambda i, j: (0, j))]
    args = [x, nw, w]
    if has_bias:
        in_specs.append(pl.BlockSpec((1, tn), lambda i, j: (0, j)))
        args.append(bias)

    def body(*refs):
        x_ref, nw_ref, w_ref = refs[:3]
        o_ref, h_ref = refs[-2], refs[-1]

        @pl.when(pl.program_id(1) == 0)
        def _():
            xhat, _ = _rms(x_ref[...])
            h_ref[...] = (xhat * nw_ref[...]).astype(BF16)

        r = _dot(h_ref[...], w_ref[...])
        if has_bias:
            r = r + refs[3][...]
        o_ref[...] = r

    return _call(
        body, name=name, grid=(t_dim // tm, n_dim // tn), in_specs=in_specs,
        out_specs=[pl.BlockSpec((tm, tn), lambda i, j: (i, j)), pl.BlockSpec((tm, d_dim), lambda i, j: (i, 0))],
        out_shape=[jax.ShapeDtypeStruct((t_dim, n_dim), F32), jax.ShapeDtypeStruct((t_dim, d_dim), BF16)],
        sem=("parallel", "arbitrary"), args=args, comm=comm)


def _norm_bwd(x, nw, dh, res, *, name, tm=256, comm=None):
    t_dim, d_dim = x.shape
    n_res = len(res)
    row = pl.BlockSpec((tm, d_dim), lambda i: (i, 0))
    vec = pl.BlockSpec((1, d_dim), lambda i: (0, 0))

    def body(*refs):
        x_ref, nw_ref, dh_ref = refs[:3]
        dx_ref, dnw_ref = refs[-2], refs[-1]
        xhat, r = _rms(x_ref[...])
        dh = dh_ref[...]
        dxhat = dh * nw_ref[...]
        dx = r * (dxhat - xhat * jnp.mean(dxhat * xhat, axis=-1, keepdims=True))
        for rr in refs[3:3 + n_res]:
            dx = dx + rr[...]
        dx_ref[...] = dx

        @pl.when(pl.program_id(0) == 0)
        def _():
            dnw_ref[...] = jnp.zeros_like(dnw_ref)

        dnw_ref[...] += jnp.sum(dh * xhat, axis=0, keepdims=True)

    return _call(
        body, name=name, grid=(t_dim // tm,), in_specs=[row, vec, row] + [row] * n_res,
        out_specs=[row, vec],
        out_shape=[jax.ShapeDtypeStruct((t_dim, d_dim), F32), jax.ShapeDtypeStruct((1, d_dim), F32)],
        sem=("arbitrary",), args=[x, nw, dh, *res], comm=comm)


def _colsum(x, *, name, tm=256):
    t_dim, n_dim = x.shape

    def body(x_ref, o_ref):
        @pl.when(pl.program_id(0) == 0)
        def _():
            o_ref[...] = jnp.zeros_like(o_ref)

        o_ref[...] += jnp.sum(x_ref[...], axis=0, keepdims=True)

    return pl.pallas_call(
        body, name=name, grid=(t_dim // tm,), in_specs=[pl.BlockSpec((tm, n_dim), lambda i: (i, 0))],
        out_specs=pl.BlockSpec((1, n_dim), lambda i: (0, 0)), out_shape=jax.ShapeDtypeStruct((1, n_dim), F32),
        compiler_params=_params("arbitrary"),
    )(x)


FFN_ROW_TILE = 512
FFN_SHARDS_PER_STEP = 2
FFN_STEPS = N_DEV // FFN_SHARDS_PER_STEP
FFN_STEP_COLS = FFN_SHARDS_PER_STEP * D_FF_SHARD


def _ffn_step_weights(wg_ref, wu_ref, wd_ref, wg_scr, wu_scr):
    for s in range(FFN_SHARDS_PER_STEP):
        cols = slice(s * D_FF_SHARD, (s + 1) * D_FF_SHARD)
        wg_scr[:, cols] = wg_ref[s]
        wu_scr[:, cols] = wu_ref[s]
    return wg_scr[...], wu_scr[...], jnp.concatenate([wd_ref[s] for s in range(FFN_SHARDS_PER_STEP)], axis=0)


def _ffn_specs(d_dim):
    col = pl.BlockSpec((FFN_SHARDS_PER_STEP, d_dim, D_FF_SHARD), lambda j: (j, 0, 0))
    rowb = pl.BlockSpec((FFN_SHARDS_PER_STEP, D_FF_SHARD, d_dim), lambda j: (j, 0, 0))
    return col, rowb


def _ffn_fwd(x, nw, wg, wu, wd, *, name, comm=None):
    t_dim, d_dim = x.shape
    n_tiles = t_dim // FFN_ROW_TILE

    def body(x_ref, nw_ref, wg_ref, wu_ref, wd_ref, o_ref, h_scr, wg_scr, wu_scr):
        j = pl.program_id(0)

        @pl.when(j == 0)
        def _():
            xhat, _ = _rms(x_ref[...])
            h_scr[...] = (xhat * nw_ref[...]).astype(BF16)
            o_ref[...] = jnp.zeros_like(o_ref)

        w_gate, w_up, w_down = _ffn_step_weights(wg_ref, wu_ref, wd_ref, wg_scr, wu_scr)
        for t in range(n_tiles):
            rows = pl.ds(t * FFN_ROW_TILE, FFN_ROW_TILE)
            h = h_scr[rows, :]
            g = _dot(h, w_gate)
            u = _dot(h, w_up)
            act = g * _sigmoid(g) * u
            o_ref[rows, :] += _dot(act, w_down)

        @pl.when(j == FFN_STEPS - 1)
        def _():
            o_ref[...] = x_ref[...] + FFN_RES_WEIGHT * o_ref[...]

    full = pl.BlockSpec((t_dim, d_dim), lambda j: (0, 0))
    col, rowb = _ffn_specs(d_dim)
    return _call(
        body, name=name, grid=(FFN_STEPS,),
        in_specs=[full, pl.BlockSpec((1, d_dim), lambda j: (0, 0)), col, col, rowb],
        out_specs=full, out_shape=jax.ShapeDtypeStruct((t_dim, d_dim), F32),
        scratch_shapes=[pltpu.VMEM((t_dim, d_dim), BF16), pltpu.VMEM((d_dim, FFN_STEP_COLS), BF16),
                        pltpu.VMEM((d_dim, FFN_STEP_COLS), BF16)],
        sem=("arbitrary",), args=[x, nw, wg, wu, wd], comm=comm)


def _ffn_bwd_prep(x, nw, dout, *, name, tm=256):
    t_dim, d_dim = x.shape
    row = pl.BlockSpec((tm, d_dim), lambda i: (i, 0))

    def body(x_ref, nw_ref, dout_ref, h_ref, dob_ref):
        xhat, _ = _rms(x_ref[...])
        h_ref[...] = (xhat * nw_ref[...]).astype(BF16)
        dob_ref[...] = (FFN_RES_WEIGHT * dout_ref[...]).astype(BF16)

    return pl.pallas_call(
        body, name=name, grid=(t_dim // tm,), in_specs=[row, pl.BlockSpec((1, d_dim), lambda i: (0, 0)), row],
        out_specs=[row, row], out_shape=[jax.ShapeDtypeStruct((t_dim, d_dim), BF16)] * 2,
        compiler_params=_params("parallel"),
    )(x, nw, dout)


def _ffn_bwd(h, dob, wg, wu, wd, *, name, comm=None):
    t_dim, d_dim = h.shape
    n_tiles = t_dim // FFN_ROW_TILE

    def body(h_ref, dob_ref, wg_ref, wu_ref, wd_ref, dh_ref, gg_ref, gu_ref, gd_ref, dwg_scr, dwu_scr, dwd_scr, wg_scr, wu_scr):
        j = pl.program_id(0)

        @pl.when(j == 0)
        def _():
            dh_ref[...] = jnp.zeros_like(dh_ref)

        w_gate, w_up, w_down = _ffn_step_weights(wg_ref, wu_ref, wd_ref, wg_scr, wu_scr)
        for t in range(n_tiles):
            rows = pl.ds(t * FFN_ROW_TILE, FFN_ROW_TILE)
            hh = h_ref[rows, :]
            do = dob_ref[rows, :]
            g = _dot(hh, w_gate)
            u = _dot(hh, w_up)
            sg = _sigmoid(g)
            s = g * sg
            da = _dot(do, w_down, NT)
            dwd = _dot(s * u, do, TN)
            du = (da * s).astype(BF16)
            dg = (da * u * _dsilu(g, sg)).astype(BF16)
            dwg = _dot(hh, dg, TN)
            dwu = _dot(hh, du, TN)
            if t == 0:
                dwd_scr[...] = dwd
                dwg_scr[...] = dwg
                dwu_scr[...] = dwu
            else:
                dwd_scr[...] += dwd
                dwg_scr[...] += dwg
                dwu_scr[...] += dwu
            dh_ref[rows, :] += _dot(dg, w_gate, NT) + _dot(du, w_up, NT)
        for s in range(FFN_SHARDS_PER_STEP):
            cols = slice(s * D_FF_SHARD, (s + 1) * D_FF_SHARD)
            gg_ref[s] = dwg_scr[:, cols].astype(BF16)
            gu_ref[s] = dwu_scr[:, cols].astype(BF16)
            gd_ref[s] = dwd_scr[cols, :].astype(BF16)

    full_bf = pl.BlockSpec((t_dim, d_dim), lambda j: (0, 0))
    col, rowb = _ffn_specs(d_dim)
    return _call(
        body, name=name, grid=(FFN_STEPS,),
        in_specs=[full_bf, full_bf, col, col, rowb], out_specs=[full_bf, col, col, rowb],
        out_shape=[jax.ShapeDtypeStruct((t_dim, d_dim), F32), jax.ShapeDtypeStruct(wg.shape, BF16),
                   jax.ShapeDtypeStruct(wu.shape, BF16), jax.ShapeDtypeStruct(wd.shape, BF16)],
        scratch_shapes=[pltpu.VMEM((d_dim, FFN_STEP_COLS), F32), pltpu.VMEM((d_dim, FFN_STEP_COLS), F32),
                        pltpu.VMEM((FFN_STEP_COLS, d_dim), F32), pltpu.VMEM((d_dim, FFN_STEP_COLS), BF16),
                        pltpu.VMEM((d_dim, FFN_STEP_COLS), BF16)],
        sem=("arbitrary",), args=[h, dob, wg, wu, wd], comm=comm)


CONV_COLS = 256


def _shift_down(u, s, rows):
    return jnp.where(rows >= s, pltpu.roll(u, s, 0), 0.0)


def _shift_up(u, s, rows, t_dim):
    return jnp.where(rows < t_dim - s, pltpu.roll(u, t_dim - s, 0), 0.0)


def _conv_pre(u, w_ref, b_ref, rows):
    c = b_ref[...] + w_ref[CONV_WIDTH - 1:CONV_WIDTH, :] * u
    for k in range(CONV_WIDTH - 1):
        c = c + w_ref[k:k + 1, :] * _shift_down(u, CONV_WIDTH - 1 - k, rows)
    return c


def _conv_fwd(zx, cw, cb, *, name, comm=None):
    t_dim = zx.shape[0]
    off = D_INNER // CONV_COLS

    def body(u_ref, w_ref, b_ref, o_ref):
        rows = lax.broadcasted_iota(jnp.int32, (t_dim, CONV_COLS), 0)
        c = _conv_pre(u_ref[...], w_ref, b_ref, rows)
        o_ref[...] = c * _sigmoid(c)

    return _call(
        body, name=name, grid=(CONV_DIM // CONV_COLS,),
        in_specs=[pl.BlockSpec((t_dim, CONV_COLS), lambda j: (0, off + j)),
                  pl.BlockSpec((CONV_WIDTH, CONV_COLS), lambda j: (0, j)), pl.BlockSpec((1, CONV_COLS), lambda j: (0, j))],
        out_specs=pl.BlockSpec((t_dim, CONV_COLS), lambda j: (0, j)),
        out_shape=jax.ShapeDtypeStruct((t_dim, CONV_DIM), F32), sem=("parallel",), args=[zx, cw, cb], comm=comm)


def _conv_bwd(zx, cw, cb, dxs, db, dc, dzx, *, name, comm=None):
    t_dim = zx.shape[0]
    off = D_INNER // CONV_COLS
    n_xs = D_INNER // CONV_COLS
    n_b = GN // CONV_COLS

    def body(u_ref, w_ref, b_ref, dxs_ref, db_ref, dc_ref, dzx_in, dzx_ref, dw_ref, dbias_ref):
        j = pl.program_id(0)
        rows = lax.broadcasted_iota(jnp.int32, (t_dim, CONV_COLS), 0)
        u = u_ref[...]
        c = _conv_pre(u, w_ref, b_ref, rows)
        d = jnp.where(j < n_xs, dxs_ref[...], jnp.where(j < n_xs + n_b, db_ref[...], dc_ref[...]))
        dcv = d * _dsilu(c, _sigmoid(c))
        dpre = w_ref[CONV_WIDTH - 1:CONV_WIDTH, :] * dcv
        dw_ref[CONV_WIDTH - 1:CONV_WIDTH, :] = jnp.sum(dcv * u, axis=0, keepdims=True)
        for k in range(CONV_WIDTH - 1):
            s = CONV_WIDTH - 1 - k
            dpre = dpre + w_ref[k:k + 1, :] * _shift_up(dcv, s, rows, t_dim)
            dw_ref[k:k + 1, :] = jnp.sum(dcv * _shift_down(u, s, rows), axis=0, keepdims=True)
        dzx_ref[...] = dpre
        dbias_ref[...] = jnp.sum(dcv, axis=0, keepdims=True)

    blk = lambda n: pl.BlockSpec((t_dim, CONV_COLS), n)
    return _call(
        body, name=name, grid=(CONV_DIM // CONV_COLS,),
        in_specs=[blk(lambda j: (0, off + j)), pl.BlockSpec((CONV_WIDTH, CONV_COLS), lambda j: (0, j)),
                  pl.BlockSpec((1, CONV_COLS), lambda j: (0, j)),
                  blk(lambda j: (0, jnp.minimum(j, n_xs - 1))),
                  blk(lambda j: (0, jnp.clip(j - n_xs, 0, n_b - 1))),
                  blk(lambda j: (0, jnp.clip(j - n_xs - n_b, 0, n_b - 1))),
                  pl.BlockSpec(memory_space=pl.ANY)],
        out_specs=[blk(lambda j: (0, off + j)), pl.BlockSpec((CONV_WIDTH, CONV_COLS), lambda j: (0, j)),
                   pl.BlockSpec((1, CONV_COLS), lambda j: (0, j))],
        out_shape=[jax.ShapeDtypeStruct(dzx.shape, F32), jax.ShapeDtypeStruct((CONV_WIDTH, CONV_DIM), F32),
                   jax.ShapeDtypeStruct((1, CONV_DIM), F32)],
        aliases={6: 0}, sem=("parallel",), args=[zx, cw, cb, dxs, db, dc, dzx], comm=comm)


def _softplus_parts(x):
    e = jnp.exp(-jnp.abs(x))
    u = 1.0 + e
    log1p_e = jnp.where(u == 1.0, e, jnp.log(u) * e / jnp.where(u == 1.0, 1.0, u - 1.0))
    return jnp.maximum(x, 0.0) + log1p_e


def _dt_prep(dtr, dt_bias, a_log, *, name):
    def body(dtr_ref, bias_ref, alog_ref, dt_ref, a_ref):
        dt = _softplus_parts(dtr_ref[...] + bias_ref[...])
        dt_ref[...] = dt
        a_ref[...] = dt * (-jnp.exp(alog_ref[...]))

    return pl.pallas_call(body, name=name, out_shape=[jax.ShapeDtypeStruct(dtr.shape, F32)] * 2,
                          compiler_params=_params())(dtr, dt_bias, a_log)


def _dt_bwd(dtr, dt_bias, a_log, dt, ddt, da, *, name):
    def body(dtr_ref, bias_ref, alog_ref, dt_ref, ddt_ref, da_ref, ddtr_ref, dbias_ref, dalog_ref):
        a_neg = -jnp.exp(alog_ref[...])
        da_v = da_ref[...]
        ddt_tot = ddt_ref[...] + da_v * a_neg
        ddtr = ddt_tot * _sigmoid(dtr_ref[...] + bias_ref[...])
        ddtr_ref[...] = ddtr
        dbias_ref[...] = jnp.sum(ddtr, axis=0, keepdims=True)
        dalog_ref[...] = jnp.sum(da_v * dt_ref[...], axis=0, keepdims=True) * a_neg

    return pl.pallas_call(
        body, name=name,
        out_shape=[jax.ShapeDtypeStruct(dtr.shape, F32), jax.ShapeDtypeStruct((1, SSM_HEADS), F32),
                   jax.ShapeDtypeStruct((1, SSM_HEADS), F32)],
        compiler_params=_params())(dtr, dt_bias, a_log, dt, ddt, da)


GROUP_COLS = HEADS_PER_GROUP * SSM_HEAD_DIM
LANES = 128
HEADS_PER_LANE_BLOCK = LANES // SSM_HEAD_DIM


def _split3(x):
    hi = x.astype(BF16)
    r1 = x - hi.astype(F32)
    mid = r1.astype(BF16)
    lo = (r1 - mid.astype(F32)).astype(BF16)
    return hi, mid, lo


def _group_sums(vals, expand):
    x = jnp.concatenate(vals, axis=0)
    out = None
    for part in _split3(x):
        t = lax.dot_general(part, expand, NT, preferred_element_type=F32)
        out = t if out is None else out + t
    return [out[i * CHUNK:(i + 1) * CHUNK] for i in range(len(vals))]


def _ssd_chunk_common(a_ref, dt_ref, b_ref, c_ref):
    row = lax.broadcasted_iota(jnp.int32, (CHUNK, CHUNK), 0)
    col = lax.broadcasted_iota(jnp.int32, (CHUNK, CHUNK), 1)
    causal = col <= row
    lower = causal.astype(F32)
    upper = (col >= row).astype(F32)
    head = lax.broadcasted_iota(jnp.int32, (HEADS_PER_GROUP, GROUP_COLS), 0)
    lane = lax.broadcasted_iota(jnp.int32, (HEADS_PER_GROUP, GROUP_COLS), 1)
    expand = ((lane >= head * SSM_HEAD_DIM) & (lane < (head + 1) * SSM_HEAD_DIM)).astype(F32)
    a = a_ref[...]
    cs = _dot_f32(lower, a)
    cs_row = _dot_f32(a, upper, TN)
    cs_x = _dot_f32(cs, expand)
    dt_x = _dot_f32(dt_ref[...], expand)
    e_out_x = jnp.exp(cs_x)
    e_st_x = jnp.exp(cs_x[CHUNK - 1:CHUNK, :] - cs_x)
    bc = b_ref[...]
    cc = c_ref[...]
    cb = _dot(cc, bc, NT)
    return causal, upper, expand.astype(BF16), cs, cs_row, dt_x, e_out_x, e_st_x, bc, cc, cb


def _head_decay(causal, cs, cs_row, h):
    return jnp.exp(jnp.where(causal, cs[:, h:h + 1] - cs_row[h:h + 1, :], NEG_BIG))


def _lane_block_head_masks():
    lane = lax.broadcasted_iota(jnp.int32, (CHUNK, LANES), 1)
    return [(lane >= i * SSM_HEAD_DIM) & (lane < (i + 1) * SSM_HEAD_DIM) for i in range(HEADS_PER_LANE_BLOCK)]


def _decay_state(dst_ref, old, new, cs):
    for h in range(HEADS_PER_GROUP):
        rows = slice(h * SSM_HEAD_DIM, (h + 1) * SSM_HEAD_DIM)
        dst_ref[rows, :] = jnp.exp(cs[CHUNK - 1:CHUNK, h:h + 1]) * old[rows, :] + new[rows, :]


def _ssd_fwd(xbc, dtg, ag, dgx, *, name, comm=None):
    t_dim = xbc.shape[0]

    def body(xs_ref, b_ref, c_ref, dt_ref, a_ref, d_ref, y_ref, st_ref, s_scr):
        @pl.when(pl.program_id(1) == 0)
        def _():
            s_scr[...] = jnp.zeros_like(s_scr)

        causal, _, _, cs, cs_row, dt_x, e_out_x, e_st_x, bc, cc, cb = _ssd_chunk_common(a_ref, dt_ref, b_ref, c_ref)
        masks = _lane_block_head_masks()
        xs = xs_ref[...]
        xdt_x = xs * dt_x
        prev = s_scr[...]
        st_ref[...] = prev
        y_off = e_out_x * _dot(cc, prev, NT) + xs * d_ref[...]
        for blk in range(GROUP_COLS // LANES):
            lanes = slice(blk * LANES, (blk + 1) * LANES)
            x_b = xdt_x[:, lanes].astype(BF16)
            acc = y_off[:, lanes]
            for i in range(HEADS_PER_LANE_BLOCK):
                m = cb * _head_decay(causal, cs, cs_row, blk * HEADS_PER_LANE_BLOCK + i)
                acc = acc + _dot(m, jnp.where(masks[i], x_b, jnp.zeros_like(x_b)))
            y_ref[:, lanes] = acc
        _decay_state(s_scr, prev, _dot(xdt_x * e_st_x, bc, TN), cs)

    xs = pl.BlockSpec((CHUNK, GROUP_COLS), lambda g, c: (c, g))
    bsp = pl.BlockSpec((CHUNK, SSM_STATE), lambda g, c: (c, D_INNER // SSM_STATE + g))
    csp = pl.BlockSpec((CHUNK, SSM_STATE), lambda g, c: (c, (D_INNER + GN) // SSM_STATE + g))
    per_head = pl.BlockSpec((None, CHUNK, HEADS_PER_GROUP), lambda g, c: (g, c, 0))
    dsk = pl.BlockSpec((None, 1, GROUP_COLS), lambda g, c: (g, 0, 0))
    return _call(
        body, name=name, grid=(SSM_GROUPS, N_CHUNKS),
        in_specs=[xs, bsp, csp, per_head, per_head, dsk],
        out_specs=[xs, pl.BlockSpec((None, GROUP_COLS, SSM_STATE), lambda g, c: (c, g, 0))],
        out_shape=[jax.ShapeDtypeStruct((t_dim, D_INNER), F32),
                   jax.ShapeDtypeStruct((N_CHUNKS, D_INNER, SSM_STATE), F32)],
        scratch_shapes=[pltpu.VMEM((GROUP_COLS, SSM_STATE), F32)],
        sem=("parallel", "arbitrary"), args=[xbc, xbc, xbc, dtg, ag, dgx], comm=comm)


def _ssd_bwd(xbc, dtg, ag, dgx, states, dy, *, name, comm=None):
    t_dim = xbc.shape[0]
    last = N_CHUNKS - 1

    def body(xs_ref, b_ref, c_ref, dt_ref, a_ref, d_ref, st_ref, dy_ref,
             dxs_ref, db_ref, dc_ref, ddt_ref, da_ref, dd_ref, ds_scr):
        @pl.when(pl.program_id(1) == 0)
        def _():
            ds_scr[...] = jnp.zeros_like(ds_scr)
            dd_ref[...] = jnp.zeros_like(dd_ref)

        causal, upper, expand, cs, cs_row, dt_x, e_out_x, e_st_x, bc, cc, cb = _ssd_chunk_common(a_ref, dt_ref, b_ref, c_ref)
        masks = _lane_block_head_masks()
        xs = xs_ref[...]
        dy_x = dy_ref[...]
        xdt_x = xs * dt_x
        prev = st_ref[...]
        d_s = ds_scr[...]
        g1_x = _dot(bc, d_s, NT)
        cp_x = _dot(cc, prev, NT)
        d_cb = jnp.zeros((CHUNK, CHUNK), F32)
        lane8 = lax.broadcasted_iota(jnp.int32, (CHUNK, HEADS_PER_GROUP), 1)
        sub8 = lax.broadcasted_iota(jnp.int32, (HEADS_PER_GROUP, CHUNK), 0)
        row_w = jnp.zeros((CHUNK, HEADS_PER_GROUP), F32)
        col_w = jnp.zeros((HEADS_PER_GROUP, CHUNK), F32)
        dxdt_blocks = []
        for blk in range(GROUP_COLS // LANES):
            lanes = slice(blk * LANES, (blk + 1) * LANES)
            dy_b = dy_x[:, lanes].astype(BF16)
            x_b = xdt_x[:, lanes].astype(BF16)
            acc_dx = jnp.zeros((CHUNK, LANES), F32)
            for i in range(HEADS_PER_LANE_BLOCK):
                h = blk * HEADS_PER_LANE_BLOCK + i
                decay = _head_decay(causal, cs, cs_row, h)
                m = cb * decay
                dy_h = jnp.where(masks[i], dy_b, jnp.zeros_like(dy_b))
                acc_dx = acc_dx + _dot(m, dy_h, TN)
                d_m = _dot(dy_h, x_b, NT)
                d_cb = d_cb + d_m * decay
                w = d_m * m
                row_w = jnp.where(lane8 == h, jnp.sum(w, axis=1, keepdims=True), row_w)
                col_w = jnp.where(sub8 == h, jnp.sum(w, axis=0, keepdims=True), col_w)
            dxdt_blocks.append(acc_dx)
        dxdt_x = jnp.concatenate(dxdt_blocks, axis=1) + e_st_x * g1_x
        dxs_ref[...] = dxdt_x * dt_x + dy_x * d_ref[...]
        dye = dy_x * e_out_x
        xde = xdt_x * e_st_x
        ddt, y_off, tl, dskip = _group_sums([dxdt_x * xs, dye * cp_x, xde * g1_x, dy_x * xs], expand)
        ddt_ref[...] = ddt
        dd_ref[...] += jnp.sum(dskip, axis=0, keepdims=True)
        sp = None
        for part in _split3(d_s * prev):
            t = lax.dot_general(expand, part, NN, preferred_element_type=F32)
            sp = t if sp is None else sp + t
        last_col = jnp.exp(cs_row[:, CHUNK - 1:CHUNK]) * jnp.sum(sp, axis=1, keepdims=True)
        eye = lax.broadcasted_iota(jnp.int32, (HEADS_PER_GROUP, HEADS_PER_GROUP), 0) == lax.broadcasted_iota(
            jnp.int32, (HEADS_PER_GROUP, HEADS_PER_GROUP), 1)
        last_row = jnp.sum(jnp.where(eye, last_col, 0.0), axis=0, keepdims=True) + jnp.sum(tl, axis=0, keepdims=True)
        is_last = lax.broadcasted_iota(jnp.int32, (CHUNK, 1), 0) == CHUNK - 1
        d_cs = row_w + y_off - tl + jnp.where(is_last, last_row, 0.0)
        da_ref[...] = _dot_f32(upper, d_cs) - _dot_f32(upper, col_w, NT)
        dc_ref[...] = _dot(d_cb, bc) + _dot(dye, prev)
        db_ref[...] = _dot(d_cb, cc, TN) + _dot(xde, d_s)
        _decay_state(ds_scr, d_s, _dot(dye, cc, TN), cs)

    rev = lambda c: last - c
    xs = pl.BlockSpec((CHUNK, GROUP_COLS), lambda g, c: (rev(c), g))
    bsp = pl.BlockSpec((CHUNK, SSM_STATE), lambda g, c: (rev(c), D_INNER // SSM_STATE + g))
    csp = pl.BlockSpec((CHUNK, SSM_STATE), lambda g, c: (rev(c), (D_INNER + GN) // SSM_STATE + g))
    per_head = pl.BlockSpec((None, CHUNK, HEADS_PER_GROUP), lambda g, c: (g, rev(c), 0))
    dsk = pl.BlockSpec((None, 1, GROUP_COLS), lambda g, c: (g, 0, 0))
    dsum = pl.BlockSpec((None, 1, HEADS_PER_GROUP), lambda g, c: (g, 0, 0))
    st = pl.BlockSpec((None, GROUP_COLS, SSM_STATE), lambda g, c: (rev(c), g, 0))
    grp = pl.BlockSpec((CHUNK, SSM_STATE), lambda g, c: (rev(c), g))
    return _call(
        body, name=name, grid=(SSM_GROUPS, N_CHUNKS),
        in_specs=[xs, bsp, csp, per_head, per_head, dsk, st, xs],
        out_specs=[xs, grp, grp, per_head, per_head, dsum],
        out_shape=[jax.ShapeDtypeStruct((t_dim, D_INNER), F32), jax.ShapeDtypeStruct((t_dim, GN), F32),
                   jax.ShapeDtypeStruct((t_dim, GN), F32),
                   jax.ShapeDtypeStruct((SSM_GROUPS, t_dim, HEADS_PER_GROUP), F32),
                   jax.ShapeDtypeStruct((SSM_GROUPS, t_dim, HEADS_PER_GROUP), F32),
                   jax.ShapeDtypeStruct((SSM_GROUPS, 1, HEADS_PER_GROUP), F32)],
        scratch_shapes=[pltpu.VMEM((GROUP_COLS, SSM_STATE), F32)],
        sem=("parallel", "arbitrary"), args=[xbc, xbc, xbc, dtg, ag, dgx, states, dy], comm=comm)


NORM_GROUP = D_INNER // SSM_GROUPS


def _gate_norm_fwd(y, zx, nw, *, name, tm=256):
    t_dim = y.shape[0]
    row = pl.BlockSpec((tm, D_INNER), lambda i: (i, 0))

    def body(y_ref, z_ref, nw_ref, o_ref):
        z = z_ref[...]
        yz = y_ref[...] * (z * _sigmoid(z))
        for g in range(SSM_GROUPS):
            cols = slice(g * NORM_GROUP, (g + 1) * NORM_GROUP)
            yhat, _ = _rms(yz[:, cols])
            o_ref[:, cols] = (yhat * nw_ref[:, cols]).astype(BF16)

    return pl.pallas_call(
        body, name=name, grid=(t_dim // tm,), in_specs=[row, row, pl.BlockSpec((1, D_INNER), lambda i: (0, 0))],
        out_specs=row, out_shape=jax.ShapeDtypeStruct((t_dim, D_INNER), BF16),
        compiler_params=_params("parallel"),
    )(y, zx, nw)


def _gate_norm_bwd(y, zx, nw, dyn, *, name, tm=256):
    t_dim = y.shape[0]
    row = pl.BlockSpec((tm, D_INNER), lambda i: (i, 0))
    vec = pl.BlockSpec((1, D_INNER), lambda i: (0, 0))

    def body(y_ref, z_ref, nw_ref, dyn_ref, dy_ref, dz_ref, dnw_ref):
        @pl.when(pl.program_id(0) == 0)
        def _():
            dnw_ref[...] = jnp.zeros_like(dnw_ref)

        z = z_ref[...]
        yv = y_ref[...]
        sg = _sigmoid(z)
        silu_z = z * sg
        yz = yv * silu_z
        dyn_v = dyn_ref[...]
        for g in range(SSM_GROUPS):
            cols = slice(g * NORM_GROUP, (g + 1) * NORM_GROUP)
            yhat, r = _rms(yz[:, cols])
            dn = dyn_v[:, cols]
            dnw_ref[:, cols] += jnp.sum(dn * yhat, axis=0, keepdims=True)
            dyhat = dn * nw_ref[:, cols]
            dyz = r * (dyhat - yhat * jnp.mean(dyhat * yhat, axis=-1, keepdims=True))
            dy_ref[:, cols] = dyz * silu_z[:, cols]
            dz_ref[:, cols] = dyz * yv[:, cols] * _dsilu(z[:, cols], sg[:, cols])

    return pl.pallas_call(
        body, name=name, grid=(t_dim // tm,), in_specs=[row, row, vec, row],
        out_specs=[row, row, vec],
        out_shape=[jax.ShapeDtypeStruct((t_dim, D_INNER), F32), jax.ShapeDtypeStruct((t_dim, ZX_DIM), F32),
                   jax.ShapeDtypeStruct((1, D_INNER), F32)],
        compiler_params=_params("arbitrary"),
    )(y, zx, nw, dyn)


def _rope(t, cos2, sin2, *, name, tm=256):
    t_dim, width = t.shape
    half = ATT_HEAD_DIM // 2
    reps = width // 128

    def body(t_ref, cos_ref, sin_ref, o_ref):
        x = t_ref[...]
        lane = lax.broadcasted_iota(jnp.int32, (tm, width), 1)
        first = (lane % ATT_HEAD_DIM) < half
        rot = jnp.where(first, -pltpu.roll(x, width - half, 1), pltpu.roll(x, half, 1))
        o_ref[...] = x * jnp.tile(cos_ref[...], (1, reps)) + rot * jnp.tile(sin_ref[...], (1, reps))

    row = pl.BlockSpec((tm, width), lambda i: (i, 0))
    tab = pl.BlockSpec((tm, 128), lambda i: (i, 0))
    return pl.pallas_call(
        body, name=name, grid=(t_dim // tm,), in_specs=[row, tab, tab], out_specs=row,
        out_shape=jax.ShapeDtypeStruct((t_dim, width), F32), compiler_params=_params("parallel"),
    )(t, cos2, sin2)


def _attn_masks(n):
    row = lax.broadcasted_iota(jnp.int32, (WINDOW, WINDOW), 0)
    col = lax.broadcasted_iota(jnp.int32, (WINDOW, WINDOW), 1)
    return col <= row, (col > row) & (n > 0)


def _attn_fwd(q, k, v, sinks, *, name, comm=None):
    t_dim = q.shape[0]

    def body(q_ref, kc_ref, kp_ref, vc_ref, vp_ref, s_ref, o_ref, l_ref):
        n = pl.program_id(0)
        mask_c, mask_p = _attn_masks(n)
        lane = lax.broadcasted_iota(jnp.int32, (WINDOW, N_Q_HEADS), 1)
        lse = jnp.zeros((WINDOW, N_Q_HEADS), F32)
        for kvh in range(N_KV_HEADS):
            kcols = slice(kvh * ATT_HEAD_DIM, (kvh + 1) * ATT_HEAD_DIM)
            kc, kp = kc_ref[:, kcols].astype(BF16), kp_ref[:, kcols].astype(BF16)
            vc, vp = vc_ref[:, kcols].astype(BF16), vp_ref[:, kcols].astype(BF16)
            for g in range(Q_PER_KV):
                h = kvh * Q_PER_KV + g
                cols = slice(h * ATT_HEAD_DIM, (h + 1) * ATT_HEAD_DIM)
                qh = q_ref[:, cols].astype(BF16)
                sc = jnp.where(mask_c, _dot(qh, kc, NT) * ATT_SCALE, NEG_BIG)
                sp = jnp.where(mask_p, _dot(qh, kp, NT) * ATT_SCALE, NEG_BIG)
                sink = s_ref[:, h:h + 1]
                m = jnp.maximum(jnp.maximum(jnp.max(sc, axis=1, keepdims=True), jnp.max(sp, axis=1, keepdims=True)), sink)
                pc = jnp.exp(sc - m)
                pp = jnp.exp(sp - m)
                den = jnp.sum(pc, axis=1, keepdims=True) + jnp.sum(pp, axis=1, keepdims=True) + jnp.exp(sink - m)
                o_ref[:, cols] = (_dot(pc, vc) + _dot(pp, vp)) / den
                lse = jnp.where(lane == h, m + jnp.log(den), lse)
        l_ref[...] = lse

    cur = lambda w: pl.BlockSpec((WINDOW, w), lambda n: (n, 0))
    prv = lambda w: pl.BlockSpec((WINDOW, w), lambda n: (jnp.maximum(n - 1, 0), 0))
    return _call(
        body, name=name, grid=(t_dim // WINDOW,),
        in_specs=[cur(D_MODEL), cur(KV_DIM), prv(KV_DIM), cur(KV_DIM), prv(KV_DIM), pl.BlockSpec((1, N_Q_HEADS), lambda n: (0, 0))],
        out_specs=[cur(D_MODEL), cur(N_Q_HEADS)],
        out_shape=[jax.ShapeDtypeStruct((t_dim, D_MODEL), F32), jax.ShapeDtypeStruct((t_dim, N_Q_HEADS), F32)],
        sem=("parallel",), args=[q, k, k, v, v, sinks], comm=comm)


def _attn_bwd(q, k, v, sinks, o, lse, do, *, name, comm=None):
    t_dim = q.shape[0]

    def body(q_ref, kc_ref, kp_ref, vc_ref, vp_ref, s_ref, o_ref, l_ref, do_ref, dq_ref, dk_ref, dv_ref, dsink_ref):
        n = pl.program_id(0)

        @pl.when(n == 0)
        def _():
            dk_ref[...] = jnp.zeros_like(dk_ref)
            dv_ref[...] = jnp.zeros_like(dv_ref)
            dsink_ref[...] = jnp.zeros_like(dsink_ref)

        mask_c, mask_p = _attn_masks(n)
        lane_row = lax.broadcasted_iota(jnp.int32, (1, N_Q_HEADS), 1)
        rows_c = pl.ds(pl.multiple_of(n * WINDOW, WINDOW), WINDOW)
        rows_p = pl.ds(pl.multiple_of(jnp.maximum(n - 1, 0) * WINDOW, WINDOW), WINDOW)
        dsink = jnp.zeros((1, N_Q_HEADS), F32)
        for kvh in range(N_KV_HEADS):
            kcols = slice(kvh * ATT_HEAD_DIM, (kvh + 1) * ATT_HEAD_DIM)
            kc, kp = kc_ref[:, kcols].astype(BF16), kp_ref[:, kcols].astype(BF16)
            vc, vp = vc_ref[:, kcols].astype(BF16), vp_ref[:, kcols].astype(BF16)
            dkc = jnp.zeros((WINDOW, ATT_HEAD_DIM), F32)
            dkp = jnp.zeros((WINDOW, ATT_HEAD_DIM), F32)
            dvc = jnp.zeros((WINDOW, ATT_HEAD_DIM), F32)
            dvp = jnp.zeros((WINDOW, ATT_HEAD_DIM), F32)
            for g in range(Q_PER_KV):
                h = kvh * Q_PER_KV + g
                cols = slice(h * ATT_HEAD_DIM, (h + 1) * ATT_HEAD_DIM)
                qh = q_ref[:, cols].astype(BF16)
                lh = l_ref[:, h:h + 1]
                pc = jnp.exp(jnp.where(mask_c, _dot(qh, kc, NT) * ATT_SCALE, NEG_BIG) - lh)
                pp = jnp.exp(jnp.where(mask_p, _dot(qh, kp, NT) * ATT_SCALE, NEG_BIG) - lh)
                doh = do_ref[:, cols]
                delta = jnp.sum(doh * o_ref[:, cols], axis=1, keepdims=True)
                dsc = pc * (_dot(doh, vc, NT) - delta)
                dsp = pp * (_dot(doh, vp, NT) - delta)
                dq_ref[:, cols] = (_dot(dsc, kc) + _dot(dsp, kp)) * ATT_SCALE
                dkc = dkc + _dot(dsc, qh, TN) * ATT_SCALE
                dkp = dkp + _dot(dsp, qh, TN) * ATT_SCALE
                dvc = dvc + _dot(pc, doh, TN)
                dvp = dvp + _dot(pp, doh, TN)
                p_sink = jnp.exp(s_ref[:, h:h + 1] - lh)
                dsink = jnp.where(lane_row == h, -jnp.sum(p_sink * delta, axis=0, keepdims=True), dsink)
            dk_ref[rows_c, kcols] += dkc
            dk_ref[rows_p, kcols] += dkp
            dv_ref[rows_c, kcols] += dvc
            dv_ref[rows_p, kcols] += dvp
        dsink_ref[...] += dsink

    cur = lambda w: pl.BlockSpec((WINDOW, w), lambda n: (n, 0))
    prv = lambda w: pl.BlockSpec((WINDOW, w), lambda n: (jnp.maximum(n - 1, 0), 0))
    whole = pl.BlockSpec((t_dim, KV_DIM), lambda n: (0, 0))
    svec = pl.BlockSpec((1, N_Q_HEADS), lambda n: (0, 0))
    return _call(
        body, name=name, grid=(t_dim // WINDOW,),
        in_specs=[cur(D_MODEL), cur(KV_DIM), prv(KV_DIM), cur(KV_DIM), prv(KV_DIM), svec, cur(D_MODEL), cur(N_Q_HEADS), cur(D_MODEL)],
        out_specs=[cur(D_MODEL), whole, whole, svec],
        out_shape=[jax.ShapeDtypeStruct((t_dim, D_MODEL), F32), jax.ShapeDtypeStruct((t_dim, KV_DIM), F32),
                   jax.ShapeDtypeStruct((t_dim, KV_DIM), F32), jax.ShapeDtypeStruct((1, N_Q_HEADS), F32)],
        sem=("arbitrary",), args=[q, k, k, v, v, sinks, o, lse, do], comm=comm)


def _loss_head(x, nw, target, *, name, tm=256):
    t_dim, d_dim = x.shape
    row = pl.BlockSpec((tm, d_dim), lambda i: (i, 0))
    vec = pl.BlockSpec((1, d_dim), lambda i: (0, 0))

    def body(x_ref, nw_ref, tgt_ref, loss_ref, dx_ref, dnw_ref):
        @pl.when(pl.program_id(0) == 0)
        def _():
            loss_ref[...] = jnp.zeros_like(loss_ref)
            dnw_ref[...] = jnp.zeros_like(dnw_ref)

        xhat, r = _rms(x_ref[...])
        err = xhat * nw_ref[...] - tgt_ref[...]
        loss_ref[...] += 0.5 * _sum_all(jnp.mean(err * err, axis=-1, keepdims=True))
        dy = err * (1.0 / d_dim)
        dnw_ref[...] += jnp.sum(dy * xhat, axis=0, keepdims=True)
        dxhat = dy * nw_ref[...]
        dx_ref[...] = r * (dxhat - xhat * jnp.mean(dxhat * xhat, axis=-1, keepdims=True))

    return pl.pallas_call(
        body, name=name, grid=(t_dim // tm,), in_specs=[row, vec, row],
        out_specs=[pl.BlockSpec((1, 1), lambda i: (0, 0)), row, vec],
        out_shape=[jax.ShapeDtypeStruct((1, 1), F32), jax.ShapeDtypeStruct((t_dim, d_dim), F32),
                   jax.ShapeDtypeStruct((1, d_dim), F32)],
        compiler_params=_params("arbitrary"),
    )(x, nw, target)


def _rope_tables():
    pos = jnp.arange(SEQ, dtype=F32)
    inv = 1.0 / (ROPE_THETA ** (jnp.arange(0, ATT_HEAD_DIM, 2, dtype=F32) / ATT_HEAD_DIM))
    ang = pos[:, None] * inv[None, :]
    cos, sin = jnp.cos(ang), jnp.sin(ang)
    return jnp.tile(cos, (1, 4)), jnp.tile(sin, (1, 4))


def _to_groups(t):
    return t.reshape(t.shape[0], SSM_GROUPS, HEADS_PER_GROUP).transpose(1, 0, 2)


def _from_groups(t):
    return t.transpose(1, 0, 2).reshape(t.shape[1], SSM_HEADS)


def _forward_backward(x0, target, net):
    w = net.w
    nw = [[w("norm_w")[l, i][None, :] for i in range(3)] for l in range(2)]
    cos2, sin2 = _rope_tables()
    ffn_norm = [nw[0][0], nw[0][2], nw[1][0], nw[1][2]]

    def ffn_f(x, blk):
        name = f"ffn_fwd{blk}"
        return _ffn_fwd(x, ffn_norm[blk], w(f"gate{blk}"), w(f"up{blk}"), w(f"down{blk}"), name=name, comm=net.carry(name))

    x1 = ffn_f(x0, 0)
    zx, h1 = _norm_mm(x1, nw[0][1], w("wzx"), None, name="ssm_in_proj", comm=net.carry("ssm_in_proj"))
    dtr = _mm(h1, w("wdt"), name="ssm_dt_proj")
    xbc = _conv_fwd(zx, w("conv_w"), w("conv_b"), name="ssm_conv_fwd", comm=net.carry("ssm_conv_fwd"))
    dt, a_dt = _dt_prep(dtr, w("dt_bias"), w("a_log"), name="ssm_dt_prep")
    dtg, ag = _to_groups(dt), _to_groups(a_dt)
    dg = jnp.repeat(w("d_skip").reshape(SSM_GROUPS, 1, HEADS_PER_GROUP), SSM_HEAD_DIM, axis=2)
    y_ssd, states = _ssd_fwd(xbc, dtg, ag, dg, name="ssd_fwd", comm=net.carry("ssd_fwd"))
    yn = _gate_norm_fwd(y_ssd, zx, w("ssm_norm_w"), name="ssm_gate_norm_fwd")
    x2 = _mm(yn, w("wout"), res=x1, name="ssm_out_proj", comm=net.carry("ssm_out_proj"))
    x3 = ffn_f(x2, 1)
    k_pre, hk = _norm_mm(x3, w("kv_norm_w"), w("wk"), w("b_k"), name="k_proj")
    v = _mm(hk, w("wv"), bias=w("b_v"), name="v_proj")
    k_rot = _rope(k_pre, cos2, sin2, name="k_rope")
    x4 = ffn_f(x3, 2)
    q_pre, h4 = _norm_mm(x4, nw[1][1], w("wq"), w("b_q"), name="q_proj")
    q_rot = _rope(q_pre, cos2, sin2, name="q_rope")
    att, lse = _attn_fwd(q_rot, k_rot, v, w("sinks"), name="attn_fwd", comm=net.carry("attn_fwd"))
    x5 = _mm(att, w("wo"), bias=w("b_o"), res=x4, name="attn_out_proj")
    x6 = ffn_f(x5, 3)
    loss, dx6, d_final = _loss_head(x6, w("final_norm_w"), target, name="loss_head")

    d_norm = [[None] * 3 for _ in range(2)]

    def ffn_b(x, dout, blk):
        h, dob = _ffn_bwd_prep(x, ffn_norm[blk], dout, name=f"ffn_bwd_prep{blk}")
        name = f"ffn_bwd{blk}"
        dh, gg, gu, gd = _ffn_bwd(h, dob, w(f"gate{blk}"), w(f"up{blk}"), w(f"down{blk}"), name=name, comm=net.carry(name))
        net.give(f"gate{blk}", gg)
        net.give(f"up{blk}", gu)
        net.give(f"down{blk}", gd)
        return _norm_bwd(x, ffn_norm[blk], dh, [dout], name=f"ffn_norm_bwd{blk}")

    by_rows = lambda g: g.reshape(N_DEV, g.shape[0] // N_DEV, g.shape[1])
    dx5, d_norm[1][2] = ffn_b(x5, dx6, 3)
    d_att = _mm(dx5, w("wo"), dims="nt", name="attn_out_proj_dx", comm=net.carry("attn_out_proj_dx"))
    net.give("w_o", by_rows(_mm(att, dx5, dims="tn", out_dtype=BF16, name="attn_out_proj_dw")))
    d_bo = _colsum(dx5, name="attn_bo_grad")
    dq_rot, dk_rot, dv, d_sinks = _attn_bwd(q_rot, k_rot, v, w("sinks"), att, lse, d_att, name="attn_bwd", comm=net.carry("attn_bwd"))
    dq = _rope(dq_rot, cos2, -sin2, name="q_rope_bwd")
    dk = _rope(dk_rot, cos2, -sin2, name="k_rope_bwd")
    dh4 = _mm(dq, w("wq"), dims="nt", name="q_proj_dx")
    net.give("w_q", by_rows(_mm(h4, dq, dims="tn", out_dtype=BF16, name="q_proj_dw")))
    d_bq = _colsum(dq, name="attn_bq_grad")
    dx4, d_norm[1][1] = _norm_bwd(x4, nw[1][1], dh4, [dx5], name="attn_norm_bwd")
    dx3a, d_norm[1][0] = ffn_b(x3, dx4, 2)
    dhk = _mm(dk, w("wk"), dims="nt", name="k_proj_dx", comm=net.carry("k_proj_dx"))
    dhk = _mm(dv, w("wv"), dims="nt", res=dhk, name="v_proj_dx")
    net.give("w_k", by_rows(_mm(hk, dk, dims="tn", out_dtype=BF16, name="k_proj_dw")))
    net.give("w_v", by_rows(_mm(hk, dv, dims="tn", out_dtype=BF16, name="v_proj_dw")))
    d_bk = _colsum(dk, name="bk_grad")
    d_bv = _colsum(dv, name="bv_grad")
    dx3, d_kvn = _norm_bwd(x3, w("kv_norm_w"), dhk, [dx3a], name="kv_norm_bwd")
    dx2, d_norm[0][2] = ffn_b(x2, dx3, 1)
    d_yn = _mm(dx2, w("wout"), dims="nt", name="ssm_out_proj_dx", comm=net.carry("ssm_out_proj_dx"))
    net.give("w_out", by_rows(_mm(yn, dx2, dims="tn", out_dtype=BF16, name="ssm_out_proj_dw")))
    dy_ssd, dzx, d_ssm_norm = _gate_norm_bwd(y_ssd, zx, w("ssm_norm_w"), d_yn, name="ssm_gate_norm_bwd")
    dxs, d_b, d_c, ddtg, dag, ddg = _ssd_bwd(xbc, dtg, ag, dg, states, dy_ssd, name="ssd_bwd", comm=net.carry("ssd_bwd"))
    dzx, d_conv_w, d_conv_b = _conv_bwd(zx, w("conv_w"), w("conv_b"), dxs, d_b, d_c, dzx, name="ssm_conv_bwd",
                                        comm=net.carry("ssm_conv_bwd"))
    ddtr, d_dt_bias, d_a_log = _dt_bwd(dtr, w("dt_bias"), w("a_log"), dt, _from_groups(ddtg), _from_groups(dag), name="ssm_dt_bwd")
    dh1 = _mm(dzx, w("wzx"), dims="nt", name="ssm_in_proj_dx")
    dh1 = _mm(ddtr, w("wdt"), dims="nt", res=dh1, name="ssm_dt_proj_dx")
    g_zx = _mm(h1, dzx, dims="tn", out_dtype=BF16, name="ssm_in_proj_dw")
    g_dt = _mm(h1, ddtr, dims="tn", out_dtype=BF16, name="ssm_dt_proj_dw")
    net.give("w_in", jnp.concatenate([g_zx, g_dt], axis=1).reshape(D_MODEL, N_DEV, IN_PROJ_SHARD).transpose(1, 0, 2))
    dx1, d_norm[0][1] = _norm_bwd(x1, nw[0][1], dh1, [dx2], name="ssm_norm_bwd", comm=net.carry("ssm_norm_bwd"))
    dx0, d_norm[0][0] = ffn_b(x0, dx1, 0)

    small = {"norm_w": jnp.concatenate([d_norm[l][i] for l in range(2) for i in range(3)], axis=0),
             "ssm_conv_w": d_conv_w, "ssm_conv_b": d_conv_b, "ssm_dt_bias": d_dt_bias, "ssm_a_log": d_a_log,
             "ssm_d": ddg.reshape(1, SSM_HEADS), "ssm_norm_w": d_ssm_norm, "kv_norm_w": d_kvn,
             "b_k": d_bk, "b_v": d_bv, "attn_b_q": d_bq, "attn_sinks": d_sinks, "attn_b_o": d_bo, "final_norm_w": d_final}
    return loss, dx0, small


BLOCK_BYTES = 1 << 20


def _row_tile(rows, cols):
    for t in (512, 256, 128, 64, 32, 16):
        if rows % t == 0 and t * cols * 4 <= BLOCK_BYTES:
            return t
    return rows


def _cast_bf16(x, blk, *, name):
    _, rows, cols = x.shape
    tm = _row_tile(rows, cols)

    def body(x_ref, o_ref):
        o_ref[...] = x_ref[...].astype(BF16)

    return pl.pallas_call(body, name=name, grid=(rows // tm,), in_specs=[pl.BlockSpec((None, tm, cols), lambda i: (blk, i, 0))],
                          out_specs=pl.BlockSpec((tm, cols), lambda i: (i, 0)),
                          out_shape=jax.ShapeDtypeStruct((rows, cols), BF16), compiler_params=_params("parallel"))(x)


def _pair_add(grad, theirs, *, name):
    n_slots, rows, cols = theirs.shape
    tm = rows if rows * cols * 4 <= 2 * BLOCK_BYTES else _row_tile(rows, cols)

    def body(g_ref, t_ref, o_ref):
        mine = jnp.where(lax.axis_index("c") == 0, g_ref[0].astype(F32), g_ref[1].astype(F32))
        o_ref[...] = (mine + t_ref[...].astype(F32)).astype(BF16)

    spec = pl.BlockSpec((None, tm, cols), lambda s, i: (s, i, 0))
    return pl.pallas_call(
        body, name=name, grid=(n_slots, rows // tm),
        in_specs=[pl.BlockSpec((None, 2, tm, cols), lambda s, i: (s, 0, i, 0)), spec], out_specs=spec,
        out_shape=jax.ShapeDtypeStruct(theirs.shape, BF16), compiler_params=_params("parallel", "parallel"),
    )(grad.reshape((n_slots, 2, rows, cols)), theirs)


def _adam_update(g, w, m, v):
    m = ADAM_B1 * m + (1.0 - ADAM_B1) * g
    v = ADAM_B2 * v + (1.0 - ADAM_B2) * (g * g)
    m_hat = m / (1.0 - ADAM_B1 ** ADAM_STEP)
    v_hat = v / (1.0 - ADAM_B2 ** ADAM_STEP)
    delta = -ADAM_LR * (m_hat / (jnp.sqrt(v_hat) + ADAM_EPS) + ADAM_WD * w)
    return delta, m, v


def _adamw(parts, w, m, v, blk, prev, *, name):
    n_blk, rows, cols = w.shape
    tm = _row_tile(rows, cols)
    spec = pl.BlockSpec((None, tm, cols), lambda i: (blk, i, 0))
    n_prev = len(prev)

    n_parts = parts.shape[0]

    def body(p_ref, w_ref, m_ref, v_ref, *refs):
        g_ref, d_ref, nm_ref, nv_ref = refs[n_prev:]
        g = p_ref[0].astype(F32)
        for s in range(1, n_parts):
            g = g + p_ref[s].astype(F32)
        delta, nm, nv = _adam_update(g, w_ref[...], m_ref[...], v_ref[...])
        g_ref[...] = g
        d_ref[...] = delta
        nm_ref[...] = nm
        nv_ref[...] = nv

    return pl.pallas_call(
        body, name=name, grid=(rows // tm,),
        in_specs=[pl.BlockSpec((n_parts, tm, cols), lambda i: (0, i, 0)), spec, spec, spec] + [pl.BlockSpec(memory_space=pl.ANY)] * n_prev,
        out_specs=[spec] * 4, out_shape=[jax.ShapeDtypeStruct((n_blk, rows, cols), F32)] * 4,
        input_output_aliases={4 + q: q for q in range(n_prev)},
        compiler_params=_params("parallel"),
    )(parts, w, m, v, *prev)


def _sum_parts(parts, *, name):
    def body(p_ref, o_ref):
        g = p_ref[0]
        for s in range(1, N_DEV):
            g = g + p_ref[s]
        o_ref[...] = g

    return pl.pallas_call(body, name=name, out_shape=jax.ShapeDtypeStruct(parts.shape[1:], F32), compiler_params=_params())(parts)


def _adamw_packed(g, w, m, v, *, name):
    def body(g_ref, w_ref, m_ref, v_ref, d_ref, nm_ref, nv_ref):
        delta, nm, nv = _adam_update(g_ref[...], w_ref[...], m_ref[...], v_ref[...])
        d_ref[...] = delta
        nm_ref[...] = nm
        nv_ref[...] = nv

    return pl.pallas_call(body, name=name, out_shape=[jax.ShapeDtypeStruct(g.shape, F32)] * 3, compiler_params=_params())(g, w, m, v)


SUBLANES = 8


def _pack(arrs):
    rows = []
    for a in arrs:
        flat = a.reshape(-1)
        pad = (-flat.shape[0]) % LANES
        rows.append(jnp.pad(flat, (0, pad)).reshape(-1, LANES))
    out = jnp.concatenate(rows, axis=0)
    return jnp.pad(out, ((0, (-out.shape[0]) % SUBLANES), (0, 0)))


def _unpack(packed, shapes):
    outs, r = [], 0
    for shp in shapes:
        n = math.prod(shp)
        nr = -(-n // LANES)
        outs.append(packed[r:r + nr].reshape(-1)[:n].reshape(shp))
        r += nr
    return outs


WEIGHT_NAMES = ("norm_w", "ffn_w_gate", "ffn_w_up", "ffn_w_down", "ssm_w_in", "ssm_conv_w", "ssm_conv_b", "ssm_dt_bias",
                "ssm_a_log", "ssm_d", "ssm_norm_w", "ssm_w_out", "kv_norm_w", "w_k", "b_k", "w_v", "b_v", "attn_w_q",
                "attn_b_q", "attn_sinks", "attn_w_o", "attn_b_o", "final_norm_w")
MATRIX_NAMES = ("ffn_w_gate", "ffn_w_up", "ffn_w_down", "ssm_w_in", "ssm_w_out", "w_k", "w_v", "attn_w_q", "attn_w_o")
VECTOR_NAMES = tuple(n for n in WEIGHT_NAMES if n not in MATRIX_NAMES)
SHARDED_VECTORS = ("norm_w", "ssm_conv_w", "ssm_conv_b", "ssm_norm_w")


GATHER_PLAN = {
    "gather_stage0": ("gate0", "up0", "down0", "vec"),
    "ffn_fwd0": ("w_in",),
    "ssm_in_proj": ("w_out", "gate1"),
    "ssm_conv_fwd": ("w_k", "w_v"),
    "ssd_fwd": ("up1", "down1", "gate2"),
    "ssm_out_proj": ("w_q", "w_o"),
    "ffn_fwd1": ("up2", "down2"),
    "ffn_fwd2": ("gate3",),
    "attn_fwd": ("up3", "down3"),
}
PAIR_PLAN = {
    "attn_out_proj_dx": ("gate3", "up3", "down3"),
    "ffn_bwd2": ("w_q", "w_o"),
    "k_proj_dx": ("gate2", "up2", "down2"),
    "ssm_out_proj_dx": ("w_k", "w_v", "gate1", "up1", "down1"),
    "ssd_bwd": ("w_out",),
    "ssm_norm_bwd": ("w_in",),
    "pair_last_grads": ("gate0", "up0", "down0"),
}
CHIP_PLAN = {
    "attn_bwd": ("gate3", "up3", "down3"),
    "ffn_bwd1": ("gate2", "up2", "down2", "w_q", "w_o"),
    "ssd_bwd": ("gate1", "up1", "down1", "w_k", "w_v"),
    "ssm_conv_bwd": ("w_out",),
    "ffn_bwd0": ("w_in",),
    "exchange_last_grads": ("gate0", "up0", "down0"),
}
FFN_PARAMS = {"gate": "ffn_w_gate", "up": "ffn_w_up", "down": "ffn_w_down"}
SINGLE_MATRICES = {"w_in": "ssm_w_in", "w_out": "ssm_w_out", "w_k": "w_k", "w_v": "w_v", "w_q": "attn_w_q", "w_o": "attn_w_o"}


class _MeshNet:
    def __init__(self, p):
        self.p = p
        self.views = {n: p[n].reshape((-1,) + p[n].shape[-2:]) for n in MATRIX_NAMES}
        self.local = {"vec": _pack([p[n] for n in SHARDED_VECTORS])}
        for short, n in FFN_PARAMS.items():
            for k in range(N_FFN):
                self.local[f"{short}{k}"] = _cast_bf16(self.views[n], k, name=f"cast_{short}{k}")
        for short, n in SINGLE_MATRICES.items():
            self.local[short] = _cast_bf16(self.views[n], 0, name=f"cast_{short}")
        self.gathered_at, self.pairs_at, self.parts_at, self.grads, self.cache = {}, {}, {}, {}, {}

    def carry(self, name):
        comms = []
        if name in GATHER_PLAN:
            keys, comm = GATHER_PLAN[name], _Gather([self.local[k] for k in GATHER_PLAN[name]])
            self.gathered_at.update({k: (comm, i) for i, k in enumerate(keys)})
            comms.append(comm)
        if name in CHIP_PLAN:
            sums = []
            for k in CHIP_PLAN[name]:
                comm, i = self.pairs_at[k]
                sums.append(_pair_add(self.grads[k], comm.results[i], name=f"pair_add_{k}"))
            comm = _ChipExchange(sums)
            self.parts_at.update({k: (comm, i) for i, k in enumerate(CHIP_PLAN[name])})
            comms.append(comm)
        if name in PAIR_PLAN:
            keys, comm = PAIR_PLAN[name], _PairSwap([self.grads[k] for k in PAIR_PLAN[name]])
            self.pairs_at.update({k: (comm, i) for i, k in enumerate(keys)})
            comms.append(comm)
        return comms

    def run(self, name):
        for comm in self.carry(name):
            _run_exchange(comm, name=name)

    def give(self, key, grad):
        self.grads[key] = grad

    def parts(self, key):
        comm, i = self.parts_at[key]
        return comm.results[i]

    def _gathered(self, key):
        comm, i = self.gathered_at[key]
        return comm.results[i]

    def _vec(self, r0, r1, lead):
        t = self._gathered("vec")[:, r0:r1, :].reshape(N_DEV, lead, -1)
        return t.transpose(1, 0, 2).reshape(lead, -1)

    def _derive(self, name):
        p = self.p
        if name[:-1] in FFN_PARAMS:
            return self._gathered(name)
        if name in ("wzx", "wdt"):
            w_in = self._gathered("w_in").transpose(1, 0, 2).reshape(D_MODEL, N_DEV * IN_PROJ_SHARD)
            return w_in[:, :ZX_DIM] if name == "wzx" else w_in[:, ZX_DIM:]
        by_rows = {"wout": "w_out", "wk": "w_k", "wv": "w_v", "wq": "w_q", "wo": "w_o"}
        if name in by_rows:
            g = self._gathered(by_rows[name])
            return g.reshape(N_DEV * g.shape[1], g.shape[2])
        vectors = {"norm_w": lambda: self._vec(0, 6, 6).reshape(2, 3, D_MODEL), "conv_w": lambda: self._vec(6, 18, CONV_WIDTH),
                   "conv_b": lambda: self._vec(18, 21, 1), "ssm_norm_w": lambda: self._vec(21, 23, 1)}
        if name in vectors:
            return vectors[name]()
        replicated = {"dt_bias": p["ssm_dt_bias"], "a_log": p["ssm_a_log"], "d_skip": p["ssm_d"], "kv_norm_w": p["kv_norm_w"][None],
                      "b_k": p["b_k"][None], "b_v": p["b_v"][None], "b_q": p["attn_b_q"], "sinks": p["attn_sinks"],
                      "b_o": p["attn_b_o"], "final_norm_w": p["final_norm_w"][None]}
        return replicated[name]

    def w(self, name):
        if name not in self.cache:
            self.cache[name] = self._derive(name)
        return self.cache[name]


def _step(x, target, p, m, v):
    pos = _slot(_position())
    net = _MeshNet(p)
    net.run("gather_stage0")
    loss, grad_x, small = _forward_backward(x, target, net)
    net.run("pair_last_grads")
    net.run("exchange_last_grads")
    vec_gather = _Gather([_pack([small[n] for n in VECTOR_NAMES])])
    vec_sum = _sum_parts(_run_exchange(vec_gather, name="gather_vector_grads")[0], name="sum_vector_grads")
    full_shapes = {"norm_w": (2, 3, D_MODEL), "ssm_conv_w": (1, CONV_WIDTH, CONV_DIM), "ssm_conv_b": (1, CONV_DIM),
                   "ssm_norm_w": (1, D_INNER)}
    vec_full = dict(zip(VECTOR_NAMES, _unpack(vec_sum, [full_shapes.get(n, p[n].shape) for n in VECTOR_NAMES])))

    grads, deltas, new_m, new_v = {}, {}, {}, {}
    view = lambda d, n: d[n].reshape(net.views[n].shape)
    for short, n in FFN_PARAMS.items():
        outs = []
        for k in reversed(range(N_FFN)):
            outs = _adamw(net.parts(f"{short}{k}"), net.views[n], view(m, n), view(v, n), k, outs, name=f"adamw_{short}{k}")
        grads[n], deltas[n], new_m[n], new_v[n] = [o.reshape(p[n].shape) for o in outs]
    for short, n in SINGLE_MATRICES.items():
        outs = _adamw(net.parts(short), net.views[n], view(m, n), view(v, n), 0, [], name=f"adamw_{short}")
        grads[n], deltas[n], new_m[n], new_v[n] = [o.reshape(p[n].shape) for o in outs]
    for n in VECTOR_NAMES:
        g = vec_full[n]
        if n in SHARDED_VECTORS:
            per = p[n].shape[-1]
            g = lax.dynamic_slice_in_dim(g, pos * per, per, axis=g.ndim - 1)
        grads[n] = g
    packed = _adamw_packed(*[_pack([d[n] for n in VECTOR_NAMES]) for d in (grads, p, m, v)], name="adamw_vectors")
    shapes = [p[n].shape for n in VECTOR_NAMES]
    for d, pk in zip((deltas, new_m, new_v), packed):
        d.update(zip(VECTOR_NAMES, _unpack(pk, shapes)))
    return loss, grad_x, grads, deltas, new_m, new_v


def kernel(x, norm_w, ffn_w_gate, ffn_w_up, ffn_w_down, ssm_w_in, ssm_conv_w, ssm_conv_b, ssm_dt_bias, ssm_a_log, ssm_d, ssm_norm_w, ssm_w_out, kv_norm_w, w_k, b_k, w_v, b_v, attn_w_q, attn_b_q, attn_sinks, attn_w_o, attn_b_o, final_norm_w, loss_target, m_norm_w, m_ffn_w_gate, m_ffn_w_up, m_ffn_w_down, m_ssm_w_in, m_ssm_conv_w, m_ssm_conv_b, m_ssm_dt_bias, m_ssm_a_log, m_ssm_d, m_ssm_norm_w, m_ssm_w_out, m_kv_norm_w, m_w_k, m_b_k, m_w_v, m_b_v, m_attn_w_q, m_attn_b_q, m_attn_sinks, m_attn_w_o, m_attn_b_o, m_final_norm_w, v_norm_w, v_ffn_w_gate, v_ffn_w_up, v_ffn_w_down, v_ssm_w_in, v_ssm_conv_w, v_ssm_conv_b, v_ssm_dt_bias, v_ssm_a_log, v_ssm_d, v_ssm_norm_w, v_ssm_w_out, v_kv_norm_w, v_w_k, v_b_k, v_w_v, v_b_v, v_attn_w_q, v_attn_b_q, v_attn_sinks, v_attn_w_o, v_attn_b_o, v_final_norm_w):
    p = dict(zip(WEIGHT_NAMES, (norm_w, ffn_w_gate, ffn_w_up, ffn_w_down, ssm_w_in, ssm_conv_w, ssm_conv_b, ssm_dt_bias, ssm_a_log, ssm_d, ssm_norm_w, ssm_w_out, kv_norm_w, w_k, b_k, w_v, b_v, attn_w_q, attn_b_q, attn_sinks, attn_w_o, attn_b_o, final_norm_w)))
    m = dict(zip(WEIGHT_NAMES, (m_norm_w, m_ffn_w_gate, m_ffn_w_up, m_ffn_w_down, m_ssm_w_in, m_ssm_conv_w, m_ssm_conv_b, m_ssm_dt_bias, m_ssm_a_log, m_ssm_d, m_ssm_norm_w, m_ssm_w_out, m_kv_norm_w, m_w_k, m_b_k, m_w_v, m_b_v, m_attn_w_q, m_attn_b_q, m_attn_sinks, m_attn_w_o, m_attn_b_o, m_final_norm_w)))
    v = dict(zip(WEIGHT_NAMES, (v_norm_w, v_ffn_w_gate, v_ffn_w_up, v_ffn_w_down, v_ssm_w_in, v_ssm_conv_w, v_ssm_conv_b, v_ssm_dt_bias, v_ssm_a_log, v_ssm_d, v_ssm_norm_w, v_ssm_w_out, v_kv_norm_w, v_w_k, v_b_k, v_w_v, v_b_v, v_attn_w_q, v_attn_b_q, v_attn_sinks, v_attn_w_o, v_attn_b_o, v_final_norm_w)))
    loss, grad_x, grads, deltas, new_m, new_v = _step(x[0], loss_target[0], p, m, v)
    loss = lax.psum(loss[0, 0], ("x", "y", "c"))
    return (loss, grad_x[None], *[grads[n] for n in WEIGHT_NAMES], *[deltas[n] for n in WEIGHT_NAMES],
            *[new_m[n] for n in WEIGHT_NAMES], *[new_v[n] for n in WEIGHT_NAMES])
```

```python
import functools
import math

import jax
import jax.numpy as jnp
from jax import lax
from jax.experimental import pallas as pl
from jax.experimental.pallas import tpu as pltpu

F32 = jnp.float32
BF16 = jnp.bfloat16

N_DEV = 8
SEQ = 2048
D_MODEL = 1024
D_FF_SHARD = 352
N_FFN = 4
D_INNER = 2048
SSM_HEADS = 32
SSM_HEAD_DIM = 64
SSM_GROUPS = 4
HEADS_PER_GROUP = 8
SSM_STATE = 128
CHUNK = 128
N_CHUNKS = SEQ // CHUNK
GN = SSM_GROUPS * SSM_STATE
CONV_DIM = D_INNER + 2 * GN
CONV_WIDTH = 4
ZX_DIM = D_INNER + CONV_DIM
IN_PROJ_SHARD = 644
ATT_HEAD_DIM = 64
N_Q_HEADS = 16
N_KV_HEADS = 4
Q_PER_KV = 4
KV_DIM = N_KV_HEADS * ATT_HEAD_DIM
WINDOW = 128
ROPE_THETA = 10000.0
EPS = 1e-5
FFN_RES_WEIGHT = 0.5
ATT_SCALE = 1.0 / math.sqrt(ATT_HEAD_DIM)
NEG_BIG = -1e30

ADAM_LR = 0.001
ADAM_B1 = 0.9
ADAM_B2 = 0.999
ADAM_EPS = 1e-08
ADAM_WD = 0.01
ADAM_STEP = 10

VMEM_LIMIT_BYTES = 56 * 1024 * 1024

NN = (((1,), (0,)), ((), ()))
NT = (((1,), (1,)), ((), ()))
TN = (((0,), (0,)), ((), ()))
_DIMS = {"nn": NN, "nt": NT, "tn": TN}


def _params(*sem):
    return pltpu.CompilerParams(dimension_semantics=sem if sem else None, vmem_limit_bytes=VMEM_LIMIT_BYTES)


def _dot(a, b, dims=NN):
    return lax.dot_general(a.astype(BF16), b.astype(BF16), dims, preferred_element_type=F32)


def _dot_f32(a, b, dims=NN):
    return lax.dot_general(a, b, dims, precision=lax.Precision.HIGHEST, preferred_element_type=F32)


def _sigmoid(x):
    return 1.0 / (1.0 + jnp.exp(-x))


def _dsilu(x, s):
    return s * (1.0 + x * (1.0 - s))


def _rms(x):
    r = lax.rsqrt(jnp.mean(x * x, axis=-1, keepdims=True) + EPS)
    return x * r, r


def _sum_all(x):
    return jnp.sum(jnp.sum(x, axis=1, keepdims=True), axis=0, keepdims=True)


MESH = pl.DeviceIdType.MESH
N_PEERS = N_DEV - 1
N_CHIPS = N_DEV // 2


def _position():
    return lax.axis_index("x"), lax.axis_index("y"), lax.axis_index("c")


def _slot(p):
    return 4 * p[0] + 2 * p[1] + p[2]


class _Exchange:
    def __init__(self, arrays, out_shapes):
        n = len(arrays)
        self.arrays = list(arrays)
        self.out_shapes = out_shapes
        self.scratch = [pltpu.SemaphoreType.DMA((n, N_PEERS)), pltpu.SemaphoreType.DMA((n, N_PEERS)), pltpu.SemaphoreType.DMA((n,))]
        self.results = None


class _Gather(_Exchange):
    def __init__(self, arrays):
        super().__init__(arrays, [jax.ShapeDtypeStruct((N_DEV,) + a.shape, a.dtype) for a in arrays])

    def _plan(self, ins, outs, sems):
        send_sems, recv_sems, local_sems = sems
        x, y, c = _position()
        me, sibling = (x, y, c), (x, y, 1 - c)
        chips = [(1 - x, y), (x, 1 - y), (1 - x, 1 - y)]
        n = len(ins)

        def copy(a, k, block, to, src=None):
            dst = outs[a].at[_slot(block)]
            return pltpu.make_async_remote_copy(src_ref=dst if src is None else src, dst_ref=dst, send_sem=send_sems.at[a, k],
                                                recv_sem=recv_sems.at[a, k], device_id=to, device_id_type=MESH)

        mine = [pltpu.make_async_copy(ins[a], outs[a].at[_slot(me)], local_sems.at[a]) for a in range(n)]
        first = []
        for a in range(n):
            first.append(copy(a, 0, me, sibling, src=ins[a]))
            first += [copy(a, 1 + j, me, (*chip, c), src=ins[a]) for j, chip in enumerate(chips)]
        return n, c, me, sibling, chips, copy, mine, first

    def start(self, ins, outs, sems):
        _, _, _, _, _, _, mine, first = self._plan(ins, outs, sems)
        for cp in mine + first:
            cp.start()

    def finish(self, ins, outs, sems):
        n, c, me, sibling, chips, copy, mine, first = self._plan(ins, outs, sems)
        passed = []
        for j, chip in enumerate(chips):
            for a in range(n):
                copy(a, 1 + j, (*chip, c), me).wait_recv()
                fwd = copy(a, 4 + j, (*chip, c), sibling)
                fwd.start()
                passed.append(fwd)
        for a in range(n):
            copy(a, 0, sibling, me).wait_recv()
            for j, chip in enumerate(chips):
                copy(a, 4 + j, (*chip, 1 - c), me).wait_recv()
        for cp in first + passed:
            cp.wait_send()
        for cp in mine:
            cp.wait()


class _PairSwap(_Exchange):
    def __init__(self, arrays):
        n = len(arrays)
        self.arrays = list(arrays)
        self.out_shapes = [jax.ShapeDtypeStruct((N_CHIPS,) + a.shape[1:], a.dtype) for a in arrays]
        self.scratch = [pltpu.SemaphoreType.DMA((n, N_CHIPS)), pltpu.SemaphoreType.DMA((n, N_CHIPS))]
        self.results = None

    def _plan(self, ins, outs, sems):
        send_sems, recv_sems = sems
        x, y, c = _position()
        return [pltpu.make_async_remote_copy(src_ref=ins[a].at[2 * q + 1 - c], dst_ref=outs[a].at[q], send_sem=send_sems.at[a, q],
                                             recv_sem=recv_sems.at[a, q], device_id=(x, y, 1 - c), device_id_type=MESH)
                for a in range(len(ins)) for q in range(N_CHIPS)]

    def start(self, ins, outs, sems):
        for cp in self._plan(ins, outs, sems):
            cp.start()

    def finish(self, ins, outs, sems):
        for cp in self._plan(ins, outs, sems):
            cp.wait()


class _ChipExchange(_Exchange):
    def __init__(self, arrays):
        n = len(arrays)
        self.arrays = list(arrays)
        self.out_shapes = [jax.ShapeDtypeStruct(a.shape, a.dtype) for a in arrays]
        self.scratch = [pltpu.SemaphoreType.DMA((n, 3)), pltpu.SemaphoreType.DMA((n, 3)), pltpu.SemaphoreType.DMA((n,))]
        self.results = None

    def _plan(self, ins, outs, sems):
        send_sems, recv_sems, local_sems = sems
        x, y, c = _position()
        here = 2 * x + y
        chips = [(1 - x, y), (x, 1 - y), (1 - x, 1 - y)]
        n = len(ins)

        def copy(a, k, src_slot, dst_slot):
            return pltpu.make_async_remote_copy(src_ref=ins[a].at[src_slot], dst_ref=outs[a].at[dst_slot], send_sem=send_sems.at[a, k],
                                                recv_sem=recv_sems.at[a, k], device_id=(*chips[k], c), device_id_type=MESH)

        there = [2 * qx + qy for qx, qy in chips]
        mine = [pltpu.make_async_copy(ins[a].at[here], outs[a].at[here], local_sems.at[a]) for a in range(n)]
        sends = [copy(a, k, there[k], here) for a in range(n) for k in range(3)]
        arrivals = lambda: [copy(a, k, here, there[k]) for a in range(n) for k in range(3)]
        return mine, sends, arrivals

    def start(self, ins, outs, sems):
        mine, sends, _ = self._plan(ins, outs, sems)
        for cp in mine + sends:
            cp.start()

    def finish(self, ins, outs, sems):
        mine, sends, arrivals = self._plan(ins, outs, sems)
        for cp in arrivals():
            cp.wait_recv()
        for cp in sends:
            cp.wait_send()
        for cp in mine:
            cp.wait()


def _call(body, *, name, grid, in_specs, out_specs, out_shape, args, scratch_shapes=(), sem=(), comm=(), aliases=None):
    single = not isinstance(out_shape, (list, tuple))
    out_shape = [out_shape] if single else list(out_shape)
    out_specs = [out_specs] if single else list(out_specs)
    comms = list(comm or ())
    n_in, n_out, n_scr = len(args), len(out_shape), len(scratch_shapes)
    params = pltpu.CompilerParams(dimension_semantics=tuple(sem) if sem else None, vmem_limit_bytes=VMEM_LIMIT_BYTES)
    if not comms:
        res = pl.pallas_call(body, name=name, grid=grid, in_specs=list(in_specs), out_specs=out_specs, out_shape=out_shape,
                             scratch_shapes=list(scratch_shapes), input_output_aliases=aliases or {}, compiler_params=params)(*args)
        return res[0] if single else res
    counts = [n_in] + [len(c.arrays) for c in comms] + [n_out] + [len(c.out_shapes) for c in comms] + [n_scr] + [len(c.scratch) for c in comms]
    nc = len(comms)

    def carried(*refs):
        pos, groups = 0, []
        for cnt in counts:
            groups.append(refs[pos:pos + cnt])
            pos += cnt
        ins, c_ins = groups[0], groups[1:1 + nc]
        outs, c_outs = groups[1 + nc], groups[2 + nc:2 + 2 * nc]
        scr, c_sems = groups[2 + 2 * nc], groups[3 + 2 * nc:]
        ids = [pl.program_id(d) for d in range(len(grid))]
        is_first = functools.reduce(jnp.logical_and, [i == 0 for i in ids])
        is_last = functools.reduce(jnp.logical_and, [i == g - 1 for i, g in zip(ids, grid)])

        @pl.when(is_first)
        def _():
            for q, c in enumerate(comms):
                c.start(c_ins[q], c_outs[q], c_sems[q])

        body(*ins, *outs, *scr)

        @pl.when(is_last)
        def _():
            for q, c in enumerate(comms):
                c.finish(c_ins[q], c_outs[q], c_sems[q])

    anyspec = pl.BlockSpec(memory_space=pl.ANY)
    c_arrays = [a for c in comms for a in c.arrays]
    c_shapes = [s for c in comms for s in c.out_shapes]
    res = pl.pallas_call(
        carried, name=name, grid=grid, in_specs=list(in_specs) + [anyspec] * len(c_arrays), out_specs=out_specs + [anyspec] * len(c_shapes),
        out_shape=out_shape + c_shapes, scratch_shapes=list(scratch_shapes) + [s for c in comms for s in c.scratch],
        input_output_aliases=aliases or {}, compiler_params=params)(*args, *c_arrays)
    pos = n_out
    for c in comms:
        c.results = list(res[pos:pos + len(c.out_shapes)])
        pos += len(c.out_shapes)
    return res[0] if single else list(res[:n_out])


def _run_exchange(comm, *, name):
    def body(*refs):
        n_ci, n_co = len(comm.arrays), len(comm.out_shapes)
        ins, outs, sems = refs[:n_ci], refs[n_ci:n_ci + n_co], refs[n_ci + n_co:]
        comm.start(ins, outs, sems)
        comm.finish(ins, outs, sems)

    anyspec = pl.BlockSpec(memory_space=pl.ANY)
    comm.results = list(pl.pallas_call(
        body, name=name, in_specs=[anyspec] * len(comm.arrays), out_specs=[anyspec] * len(comm.out_shapes),
        out_shape=list(comm.out_shapes), scratch_shapes=list(comm.scratch))(*comm.arrays))
    return comm.results


def _mm(a, b, *, dims="nn", bias=None, res=None, out_dtype=F32, name, tm=1024, tn=1024, tk=1024, comm=None, b_rows=None):
    if dims == "tn":
        k_dim, m_dim = a.shape
    else:
        m_dim, k_dim = a.shape
    row0, n_rows = b_rows if b_rows is not None else (0, b.shape[0])
    n_dim = n_rows if dims == "nt" else b.shape[1]
    assert dims == "nt" or n_rows == k_dim, (name, a.shape, b.shape, b_rows)
    tm, tn, tk = min(tm, m_dim), min(tn, n_dim), min(tk, k_dim)
    assert m_dim % tm == 0 and n_dim % tn == 0 and k_dim % tk == 0, (name, a.shape, b.shape)
    nk = k_dim // tk
    a_spec = pl.BlockSpec((tk, tm), lambda i, j, k: (k, i)) if dims == "tn" else pl.BlockSpec((tm, tk), lambda i, j, k: (i, k))
    if dims == "nt":
        assert row0 % tn == 0
        b_spec = pl.BlockSpec((tn, tk), lambda i, j, k: (row0 // tn + j, k))
    else:
        assert row0 % tk == 0
        b_spec = pl.BlockSpec((tk, tn), lambda i, j, k: (row0 // tk + k, j))
    in_specs, args = [a_spec, b_spec], [a, b]
    if bias is not None:
        in_specs.append(pl.BlockSpec((1, tn), lambda i, j, k: (0, j)))
        args.append(bias)
    if res is not None:
        in_specs.append(pl.BlockSpec((tm, tn), lambda i, j, k: (i, j)))
        args.append(res)
    dn = _DIMS[dims]

    def body(*refs):
        a_ref, b_ref = refs[0], refs[1]
        o_ref, acc_ref = refs[-2], refs[-1]
        k = pl.program_id(2)

        @pl.when(k == 0)
        def _():
            acc_ref[...] = jnp.zeros_like(acc_ref)

        acc_ref[...] += _dot(a_ref[...], b_ref[...], dn)

        @pl.when(k == nk - 1)
        def _():
            r = acc_ref[...]
            pos = 2
            if bias is not None:
                r = r + refs[pos][...]
                pos += 1
            if res is not None:
                r = r + refs[pos][...]
            o_ref[...] = r.astype(out_dtype)

    return _call(
        body, name=name, grid=(m_dim // tm, n_dim // tn, nk), in_specs=in_specs,
        out_specs=pl.BlockSpec((tm, tn), lambda i, j, k: (i, j)),
        out_shape=jax.ShapeDtypeStruct((m_dim, n_dim), out_dtype),
        scratch_shapes=[pltpu.VMEM((tm, tn), F32)], sem=("parallel", "parallel", "arbitrary"), args=args, comm=comm)


def _norm_mm(x, nw, w, bias, *, name, tm=1024, tn=1024, comm=None, w_rows=None):
    t_dim, d_dim = x.shape
    transposed = w_rows is not None
    n_dim = w_rows if transposed else w.shape[1]
    tn = min(tn, n_dim)
    assert t_dim % tm == 0 and n_dim % tn == 0
    has_bias = bias is not None
    w_spec = pl.BlockSpec((tn, d_dim), lambda i, j: (j, 0)) if transposed else pl.BlockSpec((d_dim, tn), lambda i, j: (0, j))
    dn = NT if transposed else NN
    in_specs = [pl.BlockSpec((tm, d_dim), lambda i, j: (i, 0)), pl.BlockSpec((1, d_dim), lambda i, j: (0, 0)), w_spec]
    args = [x, nw, w]
    if has_bias:
        in_specs.append(pl.BlockSpec((1, tn), lambda i, j: (0, j)))
        args.append(bias)

    def body(*refs):
        x_ref, nw_ref, w_ref = refs[:3]
        o_ref, h_ref = refs[-2], refs[-1]

        @pl.when(pl.program_id(1) == 0)
        def _():
            xhat, _ = _rms(x_ref[...])
            h_ref[...] = (xhat * nw_ref[...]).astype(BF16)

        r = _dot(h_ref[...], w_ref[...], dn)
        if has_bias:
            r = r + refs[3][...]
        o_ref[...] = r

    return _call(
        body, name=name, grid=(t_dim // tm, n_dim // tn), in_specs=in_specs,
        out_specs=[pl.BlockSpec((tm, tn), lambda i, j: (i, j)), pl.BlockSpec((tm, d_dim), lambda i, j: (i, 0))],
        out_shape=[jax.ShapeDtypeStruct((t_dim, n_dim), F32), jax.ShapeDtypeStruct((t_dim, d_dim), BF16)],
        sem=("parallel", "arbitrary"), args=args, comm=comm)


def _norm_bwd(x, nw, dh, res, *, name, tm=256, comm=None):
    t_dim, d_dim = x.shape
    n_res = len(res)
    row = pl.BlockSpec((tm, d_dim), lambda i: (i, 0))
    vec = pl.BlockSpec((1, d_dim), lambda i: (0, 0))

    def body(*refs):
        x_ref, nw_ref, dh_ref = refs[:3]
        dx_ref, dnw_ref = refs[-2], refs[-1]
        xhat, r = _rms(x_ref[...])
        dh = dh_ref[...]
        dxhat = dh * nw_ref[...]
        dx = r * (dxhat - xhat * jnp.mean(dxhat * xhat, axis=-1, keepdims=True))
        for rr in refs[3:3 + n_res]:
            dx = dx + rr[...]
        dx_ref[...] = dx

        @pl.when(pl.program_id(0) == 0)
        def _():
            dnw_ref[...] = jnp.zeros_like(dnw_ref)

        dnw_ref[...] += jnp.sum(dh * xhat, axis=0, keepdims=True)

    return _call(
        body, name=name, grid=(t_dim // tm,), in_specs=[row, vec, row] + [row] * n_res,
        out_specs=[row, vec],
        out_shape=[jax.ShapeDtypeStruct((t_dim, d_dim), F32), jax.ShapeDtypeStruct((1, d_dim), F32)],
        sem=("arbitrary",), args=[x, nw, dh, *res], comm=comm)


def _colsum(x, *, name, tm=256):
    t_dim, n_dim = x.shape

    def body(x_ref, o_ref):
        @pl.when(pl.program_id(0) == 0)
        def _():
            o_ref[...] = jnp.zeros_like(o_ref)

        o_ref[...] += jnp.sum(x_ref[...], axis=0, keepdims=True)

    return pl.pallas_call(
        body, name=name, grid=(t_dim // tm,), in_specs=[pl.BlockSpec((tm, n_dim), lambda i: (i, 0))],
        out_specs=pl.BlockSpec((1, n_dim), lambda i: (0, 0)), out_shape=jax.ShapeDtypeStruct((1, n_dim), F32),
        compiler_params=_params("arbitrary"),
    )(x)


FFN_ROW_TILE = 512
FFN_SHARDS_PER_STEP = 2
FFN_STEPS = N_DEV // FFN_SHARDS_PER_STEP
FFN_STEP_COLS = FFN_SHARDS_PER_STEP * D_FF_SHARD


def _ffn_step_weights(*refs):
    return [jnp.concatenate([r[s] for s in range(FFN_SHARDS_PER_STEP)], axis=0) for r in refs]


def _ffn_spec(d_dim):
    return pl.BlockSpec((FFN_SHARDS_PER_STEP, D_FF_SHARD, d_dim), lambda j: (j, 0, 0))


def _ffn_fwd(x, nw, wg, wu, wd, *, name, comm=None):
    t_dim, d_dim = x.shape
    n_tiles = t_dim // FFN_ROW_TILE

    def body(x_ref, nw_ref, wg_ref, wu_ref, wd_ref, o_ref, h_scr):
        j = pl.program_id(0)

        @pl.when(j == 0)
        def _():
            xhat, _ = _rms(x_ref[...])
            h_scr[...] = (xhat * nw_ref[...]).astype(BF16)
            o_ref[...] = jnp.zeros_like(o_ref)

        w_gate, w_up, w_down = _ffn_step_weights(wg_ref, wu_ref, wd_ref)
        for t in range(n_tiles):
            rows = pl.ds(t * FFN_ROW_TILE, FFN_ROW_TILE)
            h = h_scr[rows, :]
            g = _dot(h, w_gate, NT)
            u = _dot(h, w_up, NT)
            act = g * _sigmoid(g) * u
            o_ref[rows, :] += _dot(act, w_down)

        @pl.when(j == FFN_STEPS - 1)
        def _():
            o_ref[...] = x_ref[...] + FFN_RES_WEIGHT * o_ref[...]

    full = pl.BlockSpec((t_dim, d_dim), lambda j: (0, 0))
    wspec = _ffn_spec(d_dim)
    return _call(
        body, name=name, grid=(FFN_STEPS,),
        in_specs=[full, pl.BlockSpec((1, d_dim), lambda j: (0, 0)), wspec, wspec, wspec],
        out_specs=full, out_shape=jax.ShapeDtypeStruct((t_dim, d_dim), F32),
        scratch_shapes=[pltpu.VMEM((t_dim, d_dim), BF16)],
        sem=("arbitrary",), args=[x, nw, wg, wu, wd], comm=comm)


def _ffn_bwd_prep(x, nw, dout, *, name, tm=256):
    t_dim, d_dim = x.shape
    row = pl.BlockSpec((tm, d_dim), lambda i: (i, 0))

    def body(x_ref, nw_ref, dout_ref, h_ref, dob_ref):
        xhat, _ = _rms(x_ref[...])
        h_ref[...] = (xhat * nw_ref[...]).astype(BF16)
        dob_ref[...] = (FFN_RES_WEIGHT * dout_ref[...]).astype(BF16)

    return pl.pallas_call(
        body, name=name, grid=(t_dim // tm,), in_specs=[row, pl.BlockSpec((1, d_dim), lambda i: (0, 0)), row],
        out_specs=[row, row], out_shape=[jax.ShapeDtypeStruct((t_dim, d_dim), BF16)] * 2,
        compiler_params=_params("parallel"),
    )(x, nw, dout)


def _ffn_bwd(h, dob, wg, wu, wd, *, name, comm=None):
    t_dim, d_dim = h.shape
    n_tiles = t_dim // FFN_ROW_TILE

    def body(h_ref, dob_ref, wg_ref, wu_ref, wd_ref, dh_ref, gg_ref, gu_ref, gd_ref, dwg_scr, dwu_scr, dwd_scr):
        j = pl.program_id(0)

        @pl.when(j == 0)
        def _():
            dh_ref[...] = jnp.zeros_like(dh_ref)

        w_gate, w_up, w_down = _ffn_step_weights(wg_ref, wu_ref, wd_ref)
        for t in range(n_tiles):
            rows = pl.ds(t * FFN_ROW_TILE, FFN_ROW_TILE)
            hh = h_ref[rows, :]
            do = dob_ref[rows, :]
            g = _dot(hh, w_gate, NT)
            u = _dot(hh, w_up, NT)
            sg = _sigmoid(g)
            s = g * sg
            da = _dot(do, w_down, NT)
            dwd = _dot(s * u, do, TN)
            du = (da * s).astype(BF16)
            dg = (da * u * _dsilu(g, sg)).astype(BF16)
            dwg = _dot(dg, hh, TN)
            dwu = _dot(du, hh, TN)
            if t == 0:
                dwd_scr[...] = dwd
                dwg_scr[...] = dwg
                dwu_scr[...] = dwu
            else:
                dwd_scr[...] += dwd
                dwg_scr[...] += dwg
                dwu_scr[...] += dwu
            dh_ref[rows, :] += _dot(dg, w_gate) + _dot(du, w_up)
        for s in range(FFN_SHARDS_PER_STEP):
            rows = slice(s * D_FF_SHARD, (s + 1) * D_FF_SHARD)
            gg_ref[s] = dwg_scr[rows, :].astype(BF16)
            gu_ref[s] = dwu_scr[rows, :].astype(BF16)
            gd_ref[s] = dwd_scr[rows, :].astype(BF16)

    full_bf = pl.BlockSpec((t_dim, d_dim), lambda j: (0, 0))
    wspec = _ffn_spec(d_dim)
    return _call(
        body, name=name, grid=(FFN_STEPS,),
        in_specs=[full_bf, full_bf, wspec, wspec, wspec], out_specs=[full_bf, wspec, wspec, wspec],
        out_shape=[jax.ShapeDtypeStruct((t_dim, d_dim), F32)] + [jax.ShapeDtypeStruct(wd.shape, BF16)] * 3,
        scratch_shapes=[pltpu.VMEM((FFN_STEP_COLS, d_dim), F32)] * 3,
        sem=("arbitrary",), args=[h, dob, wg, wu, wd], comm=comm)


CONV_COLS = 256


def _shift_down(u, s, rows):
    return jnp.where(rows >= s, pltpu.roll(u, s, 0), 0.0)


def _shift_up(u, s, rows, t_dim):
    return jnp.where(rows < t_dim - s, pltpu.roll(u, t_dim - s, 0), 0.0)


def _conv_pre(u, w_ref, b_ref, rows):
    c = b_ref[...] + w_ref[CONV_WIDTH - 1:CONV_WIDTH, :] * u
    for k in range(CONV_WIDTH - 1):
        c = c + w_ref[k:k + 1, :] * _shift_down(u, CONV_WIDTH - 1 - k, rows)
    return c


def _conv_fwd(zx, cw, cb, *, name, comm=None):
    t_dim = zx.shape[0]
    off = D_INNER // CONV_COLS

    def body(u_ref, w_ref, b_ref, o_ref):
        rows = lax.broadcasted_iota(jnp.int32, (t_dim, CONV_COLS), 0)
        c = _conv_pre(u_ref[...], w_ref, b_ref, rows)
        o_ref[...] = c * _sigmoid(c)

    return _call(
        body, name=name, grid=(CONV_DIM // CONV_COLS,),
        in_specs=[pl.BlockSpec((t_dim, CONV_COLS), lambda j: (0, off + j)),
                  pl.BlockSpec((CONV_WIDTH, CONV_COLS), lambda j: (0, j)), pl.BlockSpec((1, CONV_COLS), lambda j: (0, j))],
        out_specs=pl.BlockSpec((t_dim, CONV_COLS), lambda j: (0, j)),
        out_shape=jax.ShapeDtypeStruct((t_dim, CONV_DIM), F32), sem=("parallel",), args=[zx, cw, cb], comm=comm)


def _conv_bwd(zx, cw, cb, dxs, db, dc, dzx, *, name, comm=None):
    t_dim = zx.shape[0]
    off = D_INNER // CONV_COLS
    n_xs = D_INNER // CONV_COLS
    n_b = GN // CONV_COLS

    def body(u_ref, w_ref, b_ref, dxs_ref, db_ref, dc_ref, dzx_in, dzx_ref, dw_ref, dbias_ref):
        j = pl.program_id(0)
        rows = lax.broadcasted_iota(jnp.int32, (t_dim, CONV_COLS), 0)
        u = u_ref[...]
        c = _conv_pre(u, w_ref, b_ref, rows)
        d = jnp.where(j < n_xs, dxs_ref[...], jnp.where(j < n_xs + n_b, db_ref[...], dc_ref[...]))
        dcv = d * _dsilu(c, _sigmoid(c))
        dpre = w_ref[CONV_WIDTH - 1:CONV_WIDTH, :] * dcv
        dw_ref[CONV_WIDTH - 1:CONV_WIDTH, :] = jnp.sum(dcv * u, axis=0, keepdims=True)
        for k in range(CONV_WIDTH - 1):
            s = CONV_WIDTH - 1 - k
            dpre = dpre + w_ref[k:k + 1, :] * _shift_up(dcv, s, rows, t_dim)
            dw_ref[k:k + 1, :] = jnp.sum(dcv * _shift_down(u, s, rows), axis=0, keepdims=True)
        dzx_ref[...] = dpre
        dbias_ref[...] = jnp.sum(dcv, axis=0, keepdims=True)

    blk = lambda n: pl.BlockSpec((t_dim, CONV_COLS), n)
    return _call(
        body, name=name, grid=(CONV_DIM // CONV_COLS,),
        in_specs=[blk(lambda j: (0, off + j)), pl.BlockSpec((CONV_WIDTH, CONV_COLS), lambda j: (0, j)),
                  pl.BlockSpec((1, CONV_COLS), lambda j: (0, j)),
                  blk(lambda j: (0, jnp.minimum(j, n_xs - 1))),
                  blk(lambda j: (0, jnp.clip(j - n_xs, 0, n_b - 1))),
                  blk(lambda j: (0, jnp.clip(j - n_xs - n_b, 0, n_b - 1))),
                  pl.BlockSpec(memory_space=pl.ANY)],
        out_specs=[blk(lambda j: (0, off + j)), pl.BlockSpec((CONV_WIDTH, CONV_COLS), lambda j: (0, j)),
                   pl.BlockSpec((1, CONV_COLS), lambda j: (0, j))],
        out_shape=[jax.ShapeDtypeStruct(dzx.shape, F32), jax.ShapeDtypeStruct((CONV_WIDTH, CONV_DIM), F32),
                   jax.ShapeDtypeStruct((1, CONV_DIM), F32)],
        aliases={6: 0}, sem=("parallel",), args=[zx, cw, cb, dxs, db, dc, dzx], comm=comm)


def _softplus_parts(x):
    e = jnp.exp(-jnp.abs(x))
    u = 1.0 + e
    log1p_e = jnp.where(u == 1.0, e, jnp.log(u) * e / jnp.where(u == 1.0, 1.0, u - 1.0))
    return jnp.maximum(x, 0.0) + log1p_e


def _dt_prep(dtr, dt_bias, a_log, *, name):
    def body(dtr_ref, bias_ref, alog_ref, dt_ref, a_ref):
        dt = _softplus_parts(dtr_ref[...] + bias_ref[...])
        dt_ref[...] = dt
        a_ref[...] = dt * (-jnp.exp(alog_ref[...]))

    return pl.pallas_call(body, name=name, out_shape=[jax.ShapeDtypeStruct(dtr.shape, F32)] * 2,
                          compiler_params=_params())(dtr, dt_bias, a_log)


def _dt_bwd(dtr, dt_bias, a_log, dt, ddt, da, *, name):
    def body(dtr_ref, bias_ref, alog_ref, dt_ref, ddt_ref, da_ref, ddtr_ref, dbias_ref, dalog_ref):
        a_neg = -jnp.exp(alog_ref[...])
        da_v = da_ref[...]
        ddt_tot = ddt_ref[...] + da_v * a_neg
        ddtr = ddt_tot * _sigmoid(dtr_ref[...] + bias_ref[...])
        ddtr_ref[...] = ddtr
        dbias_ref[...] = jnp.sum(ddtr, axis=0, keepdims=True)
        dalog_ref[...] = jnp.sum(da_v * dt_ref[...], axis=0, keepdims=True) * a_neg

    return pl.pallas_call(
        body, name=name,
        out_shape=[jax.ShapeDtypeStruct(dtr.shape, F32), jax.ShapeDtypeStruct((1, SSM_HEADS), F32),
                   jax.ShapeDtypeStruct((1, SSM_HEADS), F32)],
        compiler_params=_params())(dtr, dt_bias, a_log, dt, ddt, da)


GROUP_COLS = HEADS_PER_GROUP * SSM_HEAD_DIM
LANES = 128
HEADS_PER_LANE_BLOCK = LANES // SSM_HEAD_DIM


def _split3(x):
    hi = x.astype(BF16)
    r1 = x - hi.astype(F32)
    mid = r1.astype(BF16)
    lo = (r1 - mid.astype(F32)).astype(BF16)
    return hi, mid, lo


def _group_sums(vals, expand):
    x = jnp.concatenate(vals, axis=0)
    out = None
    for part in _split3(x):
        t = lax.dot_general(part, expand, NT, preferred_element_type=F32)
        out = t if out is None else out + t
    return [out[i * CHUNK:(i + 1) * CHUNK] for i in range(len(vals))]


def _ssd_chunk_common(a_ref, dt_ref, b_ref, c_ref):
    row = lax.broadcasted_iota(jnp.int32, (CHUNK, CHUNK), 0)
    col = lax.broadcasted_iota(jnp.int32, (CHUNK, CHUNK), 1)
    causal = col <= row
    lower = causal.astype(F32)
    upper = (col >= row).astype(F32)
    head = lax.broadcasted_iota(jnp.int32, (HEADS_PER_GROUP, GROUP_COLS), 0)
    lane = lax.broadcasted_iota(jnp.int32, (HEADS_PER_GROUP, GROUP_COLS), 1)
    expand = ((lane >= head * SSM_HEAD_DIM) & (lane < (head + 1) * SSM_HEAD_DIM)).astype(F32)
    a = a_ref[...]
    cs = _dot_f32(lower, a)
    cs_row = _dot_f32(a, upper, TN)
    cs_x = _dot_f32(cs, expand)
    dt_x = _dot_f32(dt_ref[...], expand)
    e_out_x = jnp.exp(cs_x)
    e_st_x = jnp.exp(cs_x[CHUNK - 1:CHUNK, :] - cs_x)
    bc = b_ref[...]
    cc = c_ref[...]
    cb = _dot(cc, bc, NT)
    return causal, upper, expand.astype(BF16), cs, cs_row, dt_x, e_out_x, e_st_x, bc, cc, cb


def _head_decay(causal, cs, cs_row, h):
    return jnp.exp(jnp.where(causal, cs[:, h:h + 1] - cs_row[h:h + 1, :], NEG_BIG))


def _lane_block_head_masks():
    lane = lax.broadcasted_iota(jnp.int32, (CHUNK, LANES), 1)
    return [(lane >= i * SSM_HEAD_DIM) & (lane < (i + 1) * SSM_HEAD_DIM) for i in range(HEADS_PER_LANE_BLOCK)]


def _decay_state(dst_ref, old, new, cs):
    for h in range(HEADS_PER_GROUP):
        rows = slice(h * SSM_HEAD_DIM, (h + 1) * SSM_HEAD_DIM)
        dst_ref[rows, :] = jnp.exp(cs[CHUNK - 1:CHUNK, h:h + 1]) * old[rows, :] + new[rows, :]


def _ssd_fwd(xbc, dtg, ag, dgx, *, name, comm=None):
    t_dim = xbc.shape[0]

    def body(xs_ref, b_ref, c_ref, dt_ref, a_ref, d_ref, y_ref, st_ref, s_scr):
        @pl.when(pl.program_id(1) == 0)
        def _():
            s_scr[...] = jnp.zeros_like(s_scr)

        causal, _, _, cs, cs_row, dt_x, e_out_x, e_st_x, bc, cc, cb = _ssd_chunk_common(a_ref, dt_ref, b_ref, c_ref)
        masks = _lane_block_head_masks()
        xs = xs_ref[...]
        xdt_x = xs * dt_x
        prev = s_scr[...]
        st_ref[...] = prev
        y_off = e_out_x * _dot(cc, prev, NT) + xs * d_ref[...]
        for blk in range(GROUP_COLS // LANES):
            lanes = slice(blk * LANES, (blk + 1) * LANES)
            x_b = xdt_x[:, lanes].astype(BF16)
            acc = y_off[:, lanes]
            for i in range(HEADS_PER_LANE_BLOCK):
                m = cb * _head_decay(causal, cs, cs_row, blk * HEADS_PER_LANE_BLOCK + i)
                acc = acc + _dot(m, jnp.where(masks[i], x_b, jnp.zeros_like(x_b)))
            y_ref[:, lanes] = acc
        _decay_state(s_scr, prev, _dot(xdt_x * e_st_x, bc, TN), cs)

    xs = pl.BlockSpec((CHUNK, GROUP_COLS), lambda g, c: (c, g))
    bsp = pl.BlockSpec((CHUNK, SSM_STATE), lambda g, c: (c, D_INNER // SSM_STATE + g))
    csp = pl.BlockSpec((CHUNK, SSM_STATE), lambda g, c: (c, (D_INNER + GN) // SSM_STATE + g))
    per_head = pl.BlockSpec((None, CHUNK, HEADS_PER_GROUP), lambda g, c: (g, c, 0))
    dsk = pl.BlockSpec((None, 1, GROUP_COLS), lambda g, c: (g, 0, 0))
    return _call(
        body, name=name, grid=(SSM_GROUPS, N_CHUNKS),
        in_specs=[xs, bsp, csp, per_head, per_head, dsk],
        out_specs=[xs, pl.BlockSpec((None, GROUP_COLS, SSM_STATE), lambda g, c: (c, g, 0))],
        out_shape=[jax.ShapeDtypeStruct((t_dim, D_INNER), F32),
                   jax.ShapeDtypeStruct((N_CHUNKS, D_INNER, SSM_STATE), F32)],
        scratch_shapes=[pltpu.VMEM((GROUP_COLS, SSM_STATE), F32)],
        sem=("parallel", "arbitrary"), args=[xbc, xbc, xbc, dtg, ag, dgx], comm=comm)


def _ssd_bwd(xbc, dtg, ag, dgx, states, dy, *, name, comm=None):
    t_dim = xbc.shape[0]
    last = N_CHUNKS - 1

    def body(xs_ref, b_ref, c_ref, dt_ref, a_ref, d_ref, st_ref, dy_ref,
             dxs_ref, db_ref, dc_ref, ddt_ref, da_ref, dd_ref, ds_scr):
        @pl.when(pl.program_id(1) == 0)
        def _():
            ds_scr[...] = jnp.zeros_like(ds_scr)
            dd_ref[...] = jnp.zeros_like(dd_ref)

        causal, upper, expand, cs, cs_row, dt_x, e_out_x, e_st_x, bc, cc, cb = _ssd_chunk_common(a_ref, dt_ref, b_ref, c_ref)
        masks = _lane_block_head_masks()
        xs = xs_ref[...]
        dy_x = dy_ref[...]
        xdt_x = xs * dt_x
        prev = st_ref[...]
        d_s = ds_scr[...]
        g1_x = _dot(bc, d_s, NT)
        cp_x = _dot(cc, prev, NT)
        d_cb = jnp.zeros((CHUNK, CHUNK), F32)
        lane8 = lax.broadcasted_iota(jnp.int32, (CHUNK, HEADS_PER_GROUP), 1)
        sub8 = lax.broadcasted_iota(jnp.int32, (HEADS_PER_GROUP, CHUNK), 0)
        row_w = jnp.zeros((CHUNK, HEADS_PER_GROUP), F32)
        col_w = jnp.zeros((HEADS_PER_GROUP, CHUNK), F32)
        dxdt_blocks = []
        for blk in range(GROUP_COLS // LANES):
            lanes = slice(blk * LANES, (blk + 1) * LANES)
            dy_b = dy_x[:, lanes].astype(BF16)
            x_b = xdt_x[:, lanes].astype(BF16)
            acc_dx = jnp.zeros((CHUNK, LANES), F32)
            for i in range(HEADS_PER_LANE_BLOCK):
                h = blk * HEADS_PER_LANE_BLOCK + i
                decay = _head_decay(causal, cs, cs_row, h)
                m = cb * decay
                dy_h = jnp.where(masks[i], dy_b, jnp.zeros_like(dy_b))
                acc_dx = acc_dx + _dot(m, dy_h, TN)
                d_m = _dot(dy_h, x_b, NT)
                d_cb = d_cb + d_m * decay
                w = d_m * m
                row_w = jnp.where(lane8 == h, jnp.sum(w, axis=1, keepdims=True), row_w)
                col_w = jnp.where(sub8 == h, jnp.sum(w, axis=0, keepdims=True), col_w)
            dxdt_blocks.append(acc_dx)
        dxdt_x = jnp.concatenate(dxdt_blocks, axis=1) + e_st_x * g1_x
        dxs_ref[...] = dxdt_x * dt_x + dy_x * d_ref[...]
        dye = dy_x * e_out_x
        xde = xdt_x * e_st_x
        ddt, y_off, tl, dskip = _group_sums([dxdt_x * xs, dye * cp_x, xde * g1_x, dy_x * xs], expand)
        ddt_ref[...] = ddt
        dd_ref[...] += jnp.sum(dskip, axis=0, keepdims=True)
        sp = None
        for part in _split3(d_s * prev):
            t = lax.dot_general(expand, part, NN, preferred_element_type=F32)
            sp = t if sp is None else sp + t
        last_col = jnp.exp(cs_row[:, CHUNK - 1:CHUNK]) * jnp.sum(sp, axis=1, keepdims=True)
        eye = lax.broadcasted_iota(jnp.int32, (HEADS_PER_GROUP, HEADS_PER_GROUP), 0) == lax.broadcasted_iota(
            jnp.int32, (HEADS_PER_GROUP, HEADS_PER_GROUP), 1)
        last_row = jnp.sum(jnp.where(eye, last_col, 0.0), axis=0, keepdims=True) + jnp.sum(tl, axis=0, keepdims=True)
        is_last = lax.broadcasted_iota(jnp.int32, (CHUNK, 1), 0) == CHUNK - 1
        d_cs = row_w + y_off - tl + jnp.where(is_last, last_row, 0.0)
        da_ref[...] = _dot_f32(upper, d_cs) - _dot_f32(upper, col_w, NT)
        dc_ref[...] = _dot(d_cb, bc) + _dot(dye, prev)
        db_ref[...] = _dot(d_cb, cc, TN) + _dot(xde, d_s)
        _decay_state(ds_scr, d_s, _dot(dye, cc, TN), cs)

    rev = lambda c: last - c
    xs = pl.BlockSpec((CHUNK, GROUP_COLS), lambda g, c: (rev(c), g))
    bsp = pl.BlockSpec((CHUNK, SSM_STATE), lambda g, c: (rev(c), D_INNER // SSM_STATE + g))
    csp = pl.BlockSpec((CHUNK, SSM_STATE), lambda g, c: (rev(c), (D_INNER + GN) // SSM_STATE + g))
    per_head = pl.BlockSpec((None, CHUNK, HEADS_PER_GROUP), lambda g, c: (g, rev(c), 0))
    dsk = pl.BlockSpec((None, 1, GROUP_COLS), lambda g, c: (g, 0, 0))
    dsum = pl.BlockSpec((None, 1, HEADS_PER_GROUP), lambda g, c: (g, 0, 0))
    st = pl.BlockSpec((None, GROUP_COLS, SSM_STATE), lambda g, c: (rev(c), g, 0))
    grp = pl.BlockSpec((CHUNK, SSM_STATE), lambda g, c: (rev(c), g))
    return _call(
        body, name=name, grid=(SSM_GROUPS, N_CHUNKS),
        in_specs=[xs, bsp, csp, per_head, per_head, dsk, st, xs],
        out_specs=[xs, grp, grp, per_head, per_head, dsum],
        out_shape=[jax.ShapeDtypeStruct((t_dim, D_INNER), F32), jax.ShapeDtypeStruct((t_dim, GN), F32),
                   jax.ShapeDtypeStruct((t_dim, GN), F32),
                   jax.ShapeDtypeStruct((SSM_GROUPS, t_dim, HEADS_PER_GROUP), F32),
                   jax.ShapeDtypeStruct((SSM_GROUPS, t_dim, HEADS_PER_GROUP), F32),
                   jax.ShapeDtypeStruct((SSM_GROUPS, 1, HEADS_PER_GROUP), F32)],
        scratch_shapes=[pltpu.VMEM((GROUP_COLS, SSM_STATE), F32)],
        sem=("parallel", "arbitrary"), args=[xbc, xbc, xbc, dtg, ag, dgx, states, dy], comm=comm)


NORM_GROUP = D_INNER // SSM_GROUPS


def _gate_norm_fwd(y, zx, nw, *, name, tm=256):
    t_dim = y.shape[0]
    row = pl.BlockSpec((tm, D_INNER), lambda i: (i, 0))

    def body(y_ref, z_ref, nw_ref, o_ref):
        z = z_ref[...]
        yz = y_ref[...] * (z * _sigmoid(z))
        for g in range(SSM_GROUPS):
            cols = slice(g * NORM_GROUP, (g + 1) * NORM_GROUP)
            yhat, _ = _rms(yz[:, cols])
            o_ref[:, cols] = (yhat * nw_ref[:, cols]).astype(BF16)

    return pl.pallas_call(
        body, name=name, grid=(t_dim // tm,), in_specs=[row, row, pl.BlockSpec((1, D_INNER), lambda i: (0, 0))],
        out_specs=row, out_shape=jax.ShapeDtypeStruct((t_dim, D_INNER), BF16),
        compiler_params=_params("parallel"),
    )(y, zx, nw)


def _gate_norm_bwd(y, zx, nw, dyn, *, name, tm=256):
    t_dim = y.shape[0]
    row = pl.BlockSpec((tm, D_INNER), lambda i: (i, 0))
    vec = pl.BlockSpec((1, D_INNER), lambda i: (0, 0))

    def body(y_ref, z_ref, nw_ref, dyn_ref, dy_ref, dz_ref, dnw_ref):
        @pl.when(pl.program_id(0) == 0)
        def _():
            dnw_ref[...] = jnp.zeros_like(dnw_ref)

        z = z_ref[...]
        yv = y_ref[...]
        sg = _sigmoid(z)
        silu_z = z * sg
        yz = yv * silu_z
        dyn_v = dyn_ref[...]
        for g in range(SSM_GROUPS):
            cols = slice(g * NORM_GROUP, (g + 1) * NORM_GROUP)
            yhat, r = _rms(yz[:, cols])
            dn = dyn_v[:, cols]
            dnw_ref[:, cols] += jnp.sum(dn * yhat, axis=0, keepdims=True)
            dyhat = dn * nw_ref[:, cols]
            dyz = r * (dyhat - yhat * jnp.mean(dyhat * yhat, axis=-1, keepdims=True))
            dy_ref[:, cols] = dyz * silu_z[:, cols]
            dz_ref[:, cols] = dyz * yv[:, cols] * _dsilu(z[:, cols], sg[:, cols])

    return pl.pallas_call(
        body, name=name, grid=(t_dim // tm,), in_specs=[row, row, vec, row],
        out_specs=[row, row, vec],
        out_shape=[jax.ShapeDtypeStruct((t_dim, D_INNER), F32), jax.ShapeDtypeStruct((t_dim, ZX_DIM), F32),
                   jax.ShapeDtypeStruct((1, D_INNER), F32)],
        compiler_params=_params("arbitrary"),
    )(y, zx, nw, dyn)


def _rope(t, cos2, sin2, *, name, tm=256):
    t_dim, width = t.shape
    half = ATT_HEAD_DIM // 2
    reps = width // 128

    def body(t_ref, cos_ref, sin_ref, o_ref):
        x = t_ref[...]
        lane = lax.broadcasted_iota(jnp.int32, (tm, width), 1)
        first = (lane % ATT_HEAD_DIM) < half
        rot = jnp.where(first, -pltpu.roll(x, width - half, 1), pltpu.roll(x, half, 1))
        o_ref[...] = x * jnp.tile(cos_ref[...], (1, reps)) + rot * jnp.tile(sin_ref[...], (1, reps))

    row = pl.BlockSpec((tm, width), lambda i: (i, 0))
    tab = pl.BlockSpec((tm, 128), lambda i: (i, 0))
    return pl.pallas_call(
        body, name=name, grid=(t_dim // tm,), in_specs=[row, tab, tab], out_specs=row,
        out_shape=jax.ShapeDtypeStruct((t_dim, width), F32), compiler_params=_params("parallel"),
    )(t, cos2, sin2)


def _attn_masks(n):
    row = lax.broadcasted_iota(jnp.int32, (WINDOW, WINDOW), 0)
    col = lax.broadcasted_iota(jnp.int32, (WINDOW, WINDOW), 1)
    return col <= row, (col > row) & (n > 0)


def _attn_fwd(q, k, v, sinks, *, name, comm=None):
    t_dim = q.shape[0]

    def body(q_ref, kc_ref, kp_ref, vc_ref, vp_ref, s_ref, o_ref, l_ref):
        n = pl.program_id(0)
        mask_c, mask_p = _attn_masks(n)
        lane = lax.broadcasted_iota(jnp.int32, (WINDOW, N_Q_HEADS), 1)
        lse = jnp.zeros((WINDOW, N_Q_HEADS), F32)
        for kvh in range(N_KV_HEADS):
            kcols = slice(kvh * ATT_HEAD_DIM, (kvh + 1) * ATT_HEAD_DIM)
            kc, kp = kc_ref[:, kcols].astype(BF16), kp_ref[:, kcols].astype(BF16)
            vc, vp = vc_ref[:, kcols].astype(BF16), vp_ref[:, kcols].astype(BF16)
            for g in range(Q_PER_KV):
                h = kvh * Q_PER_KV + g
                cols = slice(h * ATT_HEAD_DIM, (h + 1) * ATT_HEAD_DIM)
                qh = q_ref[:, cols].astype(BF16)
                sc = jnp.where(mask_c, _dot(qh, kc, NT) * ATT_SCALE, NEG_BIG)
                sp = jnp.where(mask_p, _dot(qh, kp, NT) * ATT_SCALE, NEG_BIG)
                sink = s_ref[:, h:h + 1]
                m = jnp.maximum(jnp.maximum(jnp.max(sc, axis=1, keepdims=True), jnp.max(sp, axis=1, keepdims=True)), sink)
                pc = jnp.exp(sc - m)
                pp = jnp.exp(sp - m)
                den = jnp.sum(pc, axis=1, keepdims=True) + jnp.sum(pp, axis=1, keepdims=True) + jnp.exp(sink - m)
                o_ref[:, cols] = (_dot(pc, vc) + _dot(pp, vp)) / den
                lse = jnp.where(lane == h, m + jnp.log(den), lse)
        l_ref[...] = lse

    cur = lambda w: pl.BlockSpec((WINDOW, w), lambda n: (n, 0))
    prv = lambda w: pl.BlockSpec((WINDOW, w), lambda n: (jnp.maximum(n - 1, 0), 0))
    return _call(
        body, name=name, grid=(t_dim // WINDOW,),
        in_specs=[cur(D_MODEL), cur(KV_DIM), prv(KV_DIM), cur(KV_DIM), prv(KV_DIM), pl.BlockSpec((1, N_Q_HEADS), lambda n: (0, 0))],
        out_specs=[cur(D_MODEL), cur(N_Q_HEADS)],
        out_shape=[jax.ShapeDtypeStruct((t_dim, D_MODEL), F32), jax.ShapeDtypeStruct((t_dim, N_Q_HEADS), F32)],
        sem=("parallel",), args=[q, k, k, v, v, sinks], comm=comm)


def _attn_bwd(q, k, v, sinks, o, lse, do, *, name, comm=None):
    t_dim = q.shape[0]

    def body(q_ref, kc_ref, kp_ref, vc_ref, vp_ref, s_ref, o_ref, l_ref, do_ref, dq_ref, dk_ref, dv_ref, dsink_ref):
        n = pl.program_id(0)

        @pl.when(n == 0)
        def _():
            dk_ref[...] = jnp.zeros_like(dk_ref)
            dv_ref[...] = jnp.zeros_like(dv_ref)
            dsink_ref[...] = jnp.zeros_like(dsink_ref)

        mask_c, mask_p = _attn_masks(n)
        lane_row = lax.broadcasted_iota(jnp.int32, (1, N_Q_HEADS), 1)
        rows_c = pl.ds(pl.multiple_of(n * WINDOW, WINDOW), WINDOW)
        rows_p = pl.ds(pl.multiple_of(jnp.maximum(n - 1, 0) * WINDOW, WINDOW), WINDOW)
        dsink = jnp.zeros((1, N_Q_HEADS), F32)
        for kvh in range(N_KV_HEADS):
            kcols = slice(kvh * ATT_HEAD_DIM, (kvh + 1) * ATT_HEAD_DIM)
            kc, kp = kc_ref[:, kcols].astype(BF16), kp_ref[:, kcols].astype(BF16)
            vc, vp = vc_ref[:, kcols].astype(BF16), vp_ref[:, kcols].astype(BF16)
            dkc = jnp.zeros((WINDOW, ATT_HEAD_DIM), F32)
            dkp = jnp.zeros((WINDOW, ATT_HEAD_DIM), F32)
            dvc = jnp.zeros((WINDOW, ATT_HEAD_DIM), F32)
            dvp = jnp.zeros((WINDOW, ATT_HEAD_DIM), F32)
            for g in range(Q_PER_KV):
                h = kvh * Q_PER_KV + g
                cols = slice(h * ATT_HEAD_DIM, (h + 1) * ATT_HEAD_DIM)
                qh = q_ref[:, cols].astype(BF16)
                lh = l_ref[:, h:h + 1]
                pc = jnp.exp(jnp.where(mask_c, _dot(qh, kc, NT) * ATT_SCALE, NEG_BIG) - lh)
                pp = jnp.exp(jnp.where(mask_p, _dot(qh, kp, NT) * ATT_SCALE, NEG_BIG) - lh)
                doh = do_ref[:, cols]
                delta = jnp.sum(doh * o_ref[:, cols], axis=1, keepdims=True)
                dsc = pc * (_dot(doh, vc, NT) - delta)
                dsp = pp * (_dot(doh, vp, NT) - delta)
                dq_ref[:, cols] = (_dot(dsc, kc) + _dot(dsp, kp)) * ATT_SCALE
                dkc = dkc + _dot(dsc, qh, TN) * ATT_SCALE
                dkp = dkp + _dot(dsp, qh, TN) * ATT_SCALE
                dvc = dvc + _dot(pc, doh, TN)
                dvp = dvp + _dot(pp, doh, TN)
                p_sink = jnp.exp(s_ref[:, h:h + 1] - lh)
                dsink = jnp.where(lane_row == h, -jnp.sum(p_sink * delta, axis=0, keepdims=True), dsink)
            dk_ref[rows_c, kcols] += dkc
            dk_ref[rows_p, kcols] += dkp
            dv_ref[rows_c, kcols] += dvc
            dv_ref[rows_p, kcols] += dvp
        dsink_ref[...] += dsink

    cur = lambda w: pl.BlockSpec((WINDOW, w), lambda n: (n, 0))
    prv = lambda w: pl.BlockSpec((WINDOW, w), lambda n: (jnp.maximum(n - 1, 0), 0))
    whole = pl.BlockSpec((t_dim, KV_DIM), lambda n: (0, 0))
    svec = pl.BlockSpec((1, N_Q_HEADS), lambda n: (0, 0))
    return _call(
        body, name=name, grid=(t_dim // WINDOW,),
        in_specs=[cur(D_MODEL), cur(KV_DIM), prv(KV_DIM), cur(KV_DIM), prv(KV_DIM), svec, cur(D_MODEL), cur(N_Q_HEADS), cur(D_MODEL)],
        out_specs=[cur(D_MODEL), whole, whole, svec],
        out_shape=[jax.ShapeDtypeStruct((t_dim, D_MODEL), F32), jax.ShapeDtypeStruct((t_dim, KV_DIM), F32),
                   jax.ShapeDtypeStruct((t_dim, KV_DIM), F32), jax.ShapeDtypeStruct((1, N_Q_HEADS), F32)],
        sem=("arbitrary",), args=[q, k, k, v, v, sinks, o, lse, do], comm=comm)


def _loss_head(x, nw, target, *, name, tm=256):
    t_dim, d_dim = x.shape
    row = pl.BlockSpec((tm, d_dim), lambda i: (i, 0))
    vec = pl.BlockSpec((1, d_dim), lambda i: (0, 0))

    def body(x_ref, nw_ref, tgt_ref, loss_ref, dx_ref, dnw_ref):
        @pl.when(pl.program_id(0) == 0)
        def _():
            loss_ref[...] = jnp.zeros_like(loss_ref)
            dnw_ref[...] = jnp.zeros_like(dnw_ref)

        xhat, r = _rms(x_ref[...])
        err = xhat * nw_ref[...] - tgt_ref[...]
        loss_ref[...] += 0.5 * _sum_all(jnp.mean(err * err, axis=-1, keepdims=True))
        dy = err * (1.0 / d_dim)
        dnw_ref[...] += jnp.sum(dy * xhat, axis=0, keepdims=True)
        dxhat = dy * nw_ref[...]
        dx_ref[...] = r * (dxhat - xhat * jnp.mean(dxhat * xhat, axis=-1, keepdims=True))

    return pl.pallas_call(
        body, name=name, grid=(t_dim // tm,), in_specs=[row, vec, row],
        out_specs=[pl.BlockSpec((1, 1), lambda i: (0, 0)), row, vec],
        out_shape=[jax.ShapeDtypeStruct((1, 1), F32), jax.ShapeDtypeStruct((t_dim, d_dim), F32),
                   jax.ShapeDtypeStruct((1, d_dim), F32)],
        compiler_params=_params("arbitrary"),
    )(x, nw, target)


def _rope_tables():
    pos = jnp.arange(SEQ, dtype=F32)
    inv = 1.0 / (ROPE_THETA ** (jnp.arange(0, ATT_HEAD_DIM, 2, dtype=F32) / ATT_HEAD_DIM))
    ang = pos[:, None] * inv[None, :]
    cos, sin = jnp.cos(ang), jnp.sin(ang)
    return jnp.tile(cos, (1, 4)), jnp.tile(sin, (1, 4))


def _to_groups(t):
    return t.reshape(t.shape[0], SSM_GROUPS, HEADS_PER_GROUP).transpose(1, 0, 2)


def _from_groups(t):
    return t.transpose(1, 0, 2).reshape(t.shape[1], SSM_HEADS)


def _forward_backward(x0, target, net):
    w = net.w
    nw = [[w("norm_w")[l, i][None, :] for i in range(3)] for l in range(2)]
    cos2, sin2 = _rope_tables()
    ffn_norm = [nw[0][0], nw[0][2], nw[1][0], nw[1][2]]

    def ffn_f(x, blk):
        name = f"ffn_fwd{blk}"
        return _ffn_fwd(x, ffn_norm[blk], w(f"gate{blk}"), w(f"up{blk}"), w(f"down{blk}"), name=name, comm=net.carry(name))

    x1 = ffn_f(x0, 0)
    zx, h1 = _norm_mm(x1, nw[0][1], w("w_in_t"), None, w_rows=ZX_DIM, name="ssm_in_proj", comm=net.carry("ssm_in_proj"))
    dtr = _mm(h1, w("w_in_t"), dims="nt", b_rows=(ZX_DIM, SSM_HEADS), name="ssm_dt_proj")
    xbc = _conv_fwd(zx, w("conv_w"), w("conv_b"), name="ssm_conv_fwd", comm=net.carry("ssm_conv_fwd"))
    dt, a_dt = _dt_prep(dtr, w("dt_bias"), w("a_log"), name="ssm_dt_prep")
    dtg, ag = _to_groups(dt), _to_groups(a_dt)
    dg = jnp.repeat(w("d_skip").reshape(SSM_GROUPS, 1, HEADS_PER_GROUP), SSM_HEAD_DIM, axis=2)
    y_ssd, states = _ssd_fwd(xbc, dtg, ag, dg, name="ssd_fwd", comm=net.carry("ssd_fwd"))
    yn = _gate_norm_fwd(y_ssd, zx, w("ssm_norm_w"), name="ssm_gate_norm_fwd")
    x2 = _mm(yn, w("wout"), res=x1, name="ssm_out_proj", comm=net.carry("ssm_out_proj"))
    x3 = ffn_f(x2, 1)
    k_pre, hk = _norm_mm(x3, w("kv_norm_w"), w("wk"), w("b_k"), name="k_proj")
    v = _mm(hk, w("wv"), bias=w("b_v"), name="v_proj")
    k_rot = _rope(k_pre, cos2, sin2, name="k_rope")
    x4 = ffn_f(x3, 2)
    q_pre, h4 = _norm_mm(x4, nw[1][1], w("wq"), w("b_q"), name="q_proj")
    q_rot = _rope(q_pre, cos2, sin2, name="q_rope")
    att, lse = _attn_fwd(q_rot, k_rot, v, w("sinks"), name="attn_fwd", comm=net.carry("attn_fwd"))
    x5 = _mm(att, w("wo"), bias=w("b_o"), res=x4, name="attn_out_proj")
    x6 = ffn_f(x5, 3)
    loss, dx6, d_final = _loss_head(x6, w("final_norm_w"), target, name="loss_head")

    d_norm = [[None] * 3 for _ in range(2)]

    def ffn_b(x, dout, blk):
        h, dob = _ffn_bwd_prep(x, ffn_norm[blk], dout, name=f"ffn_bwd_prep{blk}")
        name = f"ffn_bwd{blk}"
        dh, gg, gu, gd = _ffn_bwd(h, dob, w(f"gate{blk}"), w(f"up{blk}"), w(f"down{blk}"), name=name, comm=net.carry(name))
        net.give(f"gate{blk}", gg)
        net.give(f"up{blk}", gu)
        net.give(f"down{blk}", gd)
        return _norm_bwd(x, ffn_norm[blk], dh, [dout], name=f"ffn_norm_bwd{blk}")

    by_rows = lambda g: g.reshape(N_DEV, g.shape[0] // N_DEV, g.shape[1])
    dx5, d_norm[1][2] = ffn_b(x5, dx6, 3)
    d_att = _mm(dx5, w("wo"), dims="nt", name="attn_out_proj_dx", comm=net.carry("attn_out_proj_dx"))
    net.give("w_o", by_rows(_mm(att, dx5, dims="tn", out_dtype=BF16, name="attn_out_proj_dw")))
    d_bo = _colsum(dx5, name="attn_bo_grad")
    dq_rot, dk_rot, dv, d_sinks = _attn_bwd(q_rot, k_rot, v, w("sinks"), att, lse, d_att, name="attn_bwd", comm=net.carry("attn_bwd"))
    dq = _rope(dq_rot, cos2, -sin2, name="q_rope_bwd")
    dk = _rope(dk_rot, cos2, -sin2, name="k_rope_bwd")
    dh4 = _mm(dq, w("wq"), dims="nt", name="q_proj_dx")
    net.give("w_q", by_rows(_mm(h4, dq, dims="tn", out_dtype=BF16, name="q_proj_dw")))
    d_bq = _colsum(dq, name="attn_bq_grad")
    dx4, d_norm[1][1] = _norm_bwd(x4, nw[1][1], dh4, [dx5], name="attn_norm_bwd")
    dx3a, d_norm[1][0] = ffn_b(x3, dx4, 2)
    dhk = _mm(dk, w("wk"), dims="nt", name="k_proj_dx", comm=net.carry("k_proj_dx"))
    dhk = _mm(dv, w("wv"), dims="nt", res=dhk, name="v_proj_dx")
    net.give("w_k", by_rows(_mm(hk, dk, dims="tn", out_dtype=BF16, name="k_proj_dw")))
    net.give("w_v", by_rows(_mm(hk, dv, dims="tn", out_dtype=BF16, name="v_proj_dw")))
    d_bk = _colsum(dk, name="bk_grad")
    d_bv = _colsum(dv, name="bv_grad")
    dx3, d_kvn = _norm_bwd(x3, w("kv_norm_w"), dhk, [dx3a], name="kv_norm_bwd")
    dx2, d_norm[0][2] = ffn_b(x2, dx3, 1)
    d_yn = _mm(dx2, w("wout"), dims="nt", name="ssm_out_proj_dx", comm=net.carry("ssm_out_proj_dx"))
    net.give("w_out", by_rows(_mm(yn, dx2, dims="tn", out_dtype=BF16, name="ssm_out_proj_dw")))
    dy_ssd, dzx, d_ssm_norm = _gate_norm_bwd(y_ssd, zx, w("ssm_norm_w"), d_yn, name="ssm_gate_norm_bwd")
    dxs, d_b, d_c, ddtg, dag, ddg = _ssd_bwd(xbc, dtg, ag, dg, states, dy_ssd, name="ssd_bwd", comm=net.carry("ssd_bwd"))
    dzx, d_conv_w, d_conv_b = _conv_bwd(zx, w("conv_w"), w("conv_b"), dxs, d_b, d_c, dzx, name="ssm_conv_bwd",
                                        comm=net.carry("ssm_conv_bwd"))
    ddtr, d_dt_bias, d_a_log = _dt_bwd(dtr, w("dt_bias"), w("a_log"), dt, _from_groups(ddtg), _from_groups(dag), name="ssm_dt_bwd")
    dh1 = _mm(dzx, w("w_in_t"), b_rows=(0, ZX_DIM), name="ssm_in_proj_dx")
    dh1 = _mm(ddtr, w("w_in_t"), b_rows=(ZX_DIM, SSM_HEADS), res=dh1, name="ssm_dt_proj_dx")
    g_zx = _mm(dzx, h1, dims="tn", out_dtype=BF16, name="ssm_in_proj_dw")
    g_dt = _mm(ddtr, h1, dims="tn", out_dtype=BF16, name="ssm_dt_proj_dw")
    net.give("w_in", jnp.concatenate([g_zx, g_dt], axis=0).reshape(N_DEV, IN_PROJ_SHARD, D_MODEL))
    dx1, d_norm[0][1] = _norm_bwd(x1, nw[0][1], dh1, [dx2], name="ssm_norm_bwd", comm=net.carry("ssm_norm_bwd"))
    dx0, d_norm[0][0] = ffn_b(x0, dx1, 0)

    small = {"norm_w": jnp.concatenate([d_norm[l][i] for l in range(2) for i in range(3)], axis=0),
             "ssm_conv_w": d_conv_w, "ssm_conv_b": d_conv_b, "ssm_dt_bias": d_dt_bias, "ssm_a_log": d_a_log,
             "ssm_d": ddg.reshape(1, SSM_HEADS), "ssm_norm_w": d_ssm_norm, "kv_norm_w": d_kvn,
             "b_k": d_bk, "b_v": d_bv, "attn_b_q": d_bq, "attn_sinks": d_sinks, "attn_b_o": d_bo, "final_norm_w": d_final}
    return loss, dx0, small


BLOCK_BYTES = 1 << 20


def _row_tile(rows, cols):
    for t in (512, 256, 128, 64, 32, 16):
        if rows % t == 0 and t * cols * 4 <= BLOCK_BYTES:
            return t
    return rows


def _cast_bf16(x, blk, *, name):
    _, rows, cols = x.shape
    tm = _row_tile(rows, cols)

    def body(x_ref, o_ref):
        o_ref[...] = x_ref[...].astype(BF16)

    return pl.pallas_call(body, name=name, grid=(rows // tm,), in_specs=[pl.BlockSpec((None, tm, cols), lambda i: (blk, i, 0))],
                          out_specs=pl.BlockSpec((tm, cols), lambda i: (i, 0)),
                          out_shape=jax.ShapeDtypeStruct((rows, cols), BF16), compiler_params=_params("parallel"))(x)


def _pair_add(grad, theirs, *, name):
    n_slots, rows, cols = theirs.shape
    tm = rows if rows * cols * 4 <= 2 * BLOCK_BYTES else _row_tile(rows, cols)

    def body(g_ref, t_ref, o_ref):
        mine = jnp.where(lax.axis_index("c") == 0, g_ref[0].astype(F32), g_ref[1].astype(F32))
        o_ref[...] = (mine + t_ref[...].astype(F32)).astype(BF16)

    spec = pl.BlockSpec((None, tm, cols), lambda s, i: (s, i, 0))
    return pl.pallas_call(
        body, name=name, grid=(n_slots, rows // tm),
        in_specs=[pl.BlockSpec((None, 2, tm, cols), lambda s, i: (s, 0, i, 0)), spec], out_specs=spec,
        out_shape=jax.ShapeDtypeStruct(theirs.shape, BF16), compiler_params=_params("parallel", "parallel"),
    )(grad.reshape((n_slots, 2, rows, cols)), theirs)


def _adam_update(g, w, m, v):
    m = ADAM_B1 * m + (1.0 - ADAM_B1) * g
    v = ADAM_B2 * v + (1.0 - ADAM_B2) * (g * g)
    m_hat = m / (1.0 - ADAM_B1 ** ADAM_STEP)
    v_hat = v / (1.0 - ADAM_B2 ** ADAM_STEP)
    delta = -ADAM_LR * (m_hat / (jnp.sqrt(v_hat) + ADAM_EPS) + ADAM_WD * w)
    return delta, m, v


def _adamw(parts, w, m, v, blk, prev, *, name):
    n_blk, rows, cols = w.shape
    tm = _row_tile(rows, cols)
    spec = pl.BlockSpec((None, tm, cols), lambda i: (blk, i, 0))
    n_prev = len(prev)

    n_parts = parts.shape[0]

    def body(p_ref, w_ref, m_ref, v_ref, *refs):
        g_ref, d_ref, nm_ref, nv_ref = refs[n_prev:]
        g = p_ref[0].astype(F32)
        for s in range(1, n_parts):
            g = g + p_ref[s].astype(F32)
        delta, nm, nv = _adam_update(g, w_ref[...], m_ref[...], v_ref[...])
        g_ref[...] = g
        d_ref[...] = delta
        nm_ref[...] = nm
        nv_ref[...] = nv

    return pl.pallas_call(
        body, name=name, grid=(rows // tm,),
        in_specs=[pl.BlockSpec((n_parts, tm, cols), lambda i: (0, i, 0)), spec, spec, spec] + [pl.BlockSpec(memory_space=pl.ANY)] * n_prev,
        out_specs=[spec] * 4, out_shape=[jax.ShapeDtypeStruct((n_blk, rows, cols), F32)] * 4,
        input_output_aliases={4 + q: q for q in range(n_prev)},
        compiler_params=_params("parallel"),
    )(parts, w, m, v, *prev)


def _sum_parts(parts, *, name):
    def body(p_ref, o_ref):
        g = p_ref[0]
        for s in range(1, N_DEV):
            g = g + p_ref[s]
        o_ref[...] = g

    return pl.pallas_call(body, name=name, out_shape=jax.ShapeDtypeStruct(parts.shape[1:], F32), compiler_params=_params())(parts)


def _adamw_packed(g, w, m, v, *, name):
    def body(g_ref, w_ref, m_ref, v_ref, d_ref, nm_ref, nv_ref):
        delta, nm, nv = _adam_update(g_ref[...], w_ref[...], m_ref[...], v_ref[...])
        d_ref[...] = delta
        nm_ref[...] = nm
        nv_ref[...] = nv

    return pl.pallas_call(body, name=name, out_shape=[jax.ShapeDtypeStruct(g.shape, F32)] * 3, compiler_params=_params())(g, w, m, v)


SUBLANES = 8


def _pack(arrs):
    rows = []
    for a in arrs:
        flat = a.reshape(-1)
        pad = (-flat.shape[0]) % LANES
        rows.append(jnp.pad(flat, (0, pad)).reshape(-1, LANES))
    out = jnp.concatenate(rows, axis=0)
    return jnp.pad(out, ((0, (-out.shape[0]) % SUBLANES), (0, 0)))


def _unpack(packed, shapes):
    outs, r = [], 0
    for shp in shapes:
        n = math.prod(shp)
        nr = -(-n // LANES)
        outs.append(packed[r:r + nr].reshape(-1)[:n].reshape(shp))
        r += nr
    return outs


WEIGHT_NAMES = ("norm_w", "ffn_w_gate", "ffn_w_up", "ffn_w_down", "ssm_w_in", "ssm_conv_w", "ssm_conv_b", "ssm_dt_bias",
                "ssm_a_log", "ssm_d", "ssm_norm_w", "ssm_w_out", "kv_norm_w", "w_k", "b_k", "w_v", "b_v", "attn_w_q",
                "attn_b_q", "attn_sinks", "attn_w_o", "attn_b_o", "final_norm_w")
MATRIX_NAMES = ("ffn_w_gate", "ffn_w_up", "ffn_w_down", "ssm_w_in", "ssm_w_out", "w_k", "w_v", "attn_w_q", "attn_w_o")
VECTOR_NAMES = tuple(n for n in WEIGHT_NAMES if n not in MATRIX_NAMES)
SHARDED_VECTORS = ("norm_w", "ssm_conv_w", "ssm_conv_b", "ssm_norm_w")


GATHER_PLAN = {
    "gather_stage0": ("gate0", "up0", "down0", "vec"),
    "ffn_fwd0": ("w_in",),
    "ssm_in_proj": ("w_out", "gate1"),
    "ssm_conv_fwd": ("w_k", "w_v"),
    "ssd_fwd": ("up1", "down1", "gate2"),
    "ssm_out_proj": ("w_q", "w_o"),
    "ffn_fwd1": ("up2", "down2"),
    "ffn_fwd2": ("gate3",),
    "attn_fwd": ("up3", "down3"),
}
PAIR_PLAN = {
    "attn_out_proj_dx": ("gate3", "up3", "down3"),
    "ffn_bwd2": ("w_q", "w_o"),
    "k_proj_dx": ("gate2", "up2", "down2"),
    "ssm_out_proj_dx": ("w_k", "w_v", "gate1", "up1", "down1"),
    "ssd_bwd": ("w_out",),
    "ssm_norm_bwd": ("w_in",),
    "pair_last_grads": ("gate0", "up0", "down0"),
}
CHIP_PLAN = {
    "attn_bwd": ("gate3", "up3", "down3"),
    "ffn_bwd1": ("gate2", "up2", "down2", "w_q", "w_o"),
    "ssd_bwd": ("gate1", "up1", "down1", "w_k", "w_v"),
    "ssm_conv_bwd": ("w_out",),
    "ffn_bwd0": ("w_in",),
    "exchange_last_grads": ("gate0", "up0", "down0"),
}
FFN_PARAMS = {"gate": "ffn_w_gate", "up": "ffn_w_up", "down": "ffn_w_down"}
SINGLE_MATRICES = {"w_in": "ssm_w_in", "w_out": "ssm_w_out", "w_k": "w_k", "w_v": "w_v", "w_q": "attn_w_q", "w_o": "attn_w_o"}


TRANSPOSED = ("ffn_w_gate", "ffn_w_up", "ssm_w_in")


def _matrix_view(name, a):
    if name in TRANSPOSED:
        a = jnp.swapaxes(a, -1, -2)
    return a.reshape((-1,) + a.shape[-2:])


def _from_matrix_view(name, a, shape):
    if name in TRANSPOSED:
        return jnp.swapaxes(a.reshape(shape[:-2] + (shape[-1], shape[-2])), -1, -2)
    return a.reshape(shape)


class _MeshNet:
    def __init__(self, p):
        self.p = p
        self.views = {n: _matrix_view(n, p[n]) for n in MATRIX_NAMES}
        self.local = {"vec": _pack([p[n] for n in SHARDED_VECTORS])}
        for short, n in FFN_PARAMS.items():
            for k in range(N_FFN):
                self.local[f"{short}{k}"] = _cast_bf16(self.views[n], k, name=f"cast_{short}{k}")
        for short, n in SINGLE_MATRICES.items():
            self.local[short] = _cast_bf16(self.views[n], 0, name=f"cast_{short}")
        self.gathered_at, self.pairs_at, self.parts_at, self.grads, self.cache = {}, {}, {}, {}, {}

    def carry(self, name):
        comms = []
        if name in GATHER_PLAN:
            keys, comm = GATHER_PLAN[name], _Gather([self.local[k] for k in GATHER_PLAN[name]])
            self.gathered_at.update({k: (comm, i) for i, k in enumerate(keys)})
            comms.append(comm)
        if name in CHIP_PLAN:
            sums = []
            for k in CHIP_PLAN[name]:
                comm, i = self.pairs_at[k]
                sums.append(_pair_add(self.grads[k], comm.results[i], name=f"pair_add_{k}"))
            comm = _ChipExchange(sums)
            self.parts_at.update({k: (comm, i) for i, k in enumerate(CHIP_PLAN[name])})
            comms.append(comm)
        if name in PAIR_PLAN:
            keys, comm = PAIR_PLAN[name], _PairSwap([self.grads[k] for k in PAIR_PLAN[name]])
            self.pairs_at.update({k: (comm, i) for i, k in enumerate(keys)})
            comms.append(comm)
        return comms

    def run(self, name):
        for comm in self.carry(name):
            _run_exchange(comm, name=name)

    def give(self, key, grad):
        self.grads[key] = grad

    def parts(self, key):
        comm, i = self.parts_at[key]
        return comm.results[i]

    def _gathered(self, key):
        comm, i = self.gathered_at[key]
        return comm.results[i]

    def _vec(self, r0, r1, lead):
        t = self._gathered("vec")[:, r0:r1, :].reshape(N_DEV, lead, -1)
        return t.transpose(1, 0, 2).reshape(lead, -1)

    def _derive(self, name):
        p = self.p
        if name[:-1] in FFN_PARAMS:
            return self._gathered(name)
        if name == "w_in_t":
            return self._gathered("w_in").reshape(N_DEV * IN_PROJ_SHARD, D_MODEL)
        by_rows = {"wout": "w_out", "wk": "w_k", "wv": "w_v", "wq": "w_q", "wo": "w_o"}
        if name in by_rows:
            g = self._gathered(by_rows[name])
            return g.reshape(N_DEV * g.shape[1], g.shape[2])
        vectors = {"norm_w": lambda: self._vec(0, 6, 6).reshape(2, 3, D_MODEL), "conv_w": lambda: self._vec(6, 18, CONV_WIDTH),
                   "conv_b": lambda: self._vec(18, 21, 1), "ssm_norm_w": lambda: self._vec(21, 23, 1)}
        if name in vectors:
            return vectors[name]()
        replicated = {"dt_bias": p["ssm_dt_bias"], "a_log": p["ssm_a_log"], "d_skip": p["ssm_d"], "kv_norm_w": p["kv_norm_w"][None],
                      "b_k": p["b_k"][None], "b_v": p["b_v"][None], "b_q": p["attn_b_q"], "sinks": p["attn_sinks"],
                      "b_o": p["attn_b_o"], "final_norm_w": p["final_norm_w"][None]}
        return replicated[name]

    def w(self, name):
        if name not in self.cache:
            self.cache[name] = self._derive(name)
        return self.cache[name]


def _step(x, target, p, m, v):
    pos = _slot(_position())
    net = _MeshNet(p)
    net.run("gather_stage0")
    loss, grad_x, small = _forward_backward(x, target, net)
    net.run("pair_last_grads")
    net.run("exchange_last_grads")
    vec_gather = _Gather([_pack([small[n] for n in VECTOR_NAMES])])
    vec_sum = _sum_parts(_run_exchange(vec_gather, name="gather_vector_grads")[0], name="sum_vector_grads")
    full_shapes = {"norm_w": (2, 3, D_MODEL), "ssm_conv_w": (1, CONV_WIDTH, CONV_DIM), "ssm_conv_b": (1, CONV_DIM),
                   "ssm_norm_w": (1, D_INNER)}
    vec_full = dict(zip(VECTOR_NAMES, _unpack(vec_sum, [full_shapes.get(n, p[n].shape) for n in VECTOR_NAMES])))

    grads, deltas, new_m, new_v = {}, {}, {}, {}
    view = lambda d, n: _matrix_view(n, d[n])
    for short, n in FFN_PARAMS.items():
        outs = []
        for k in reversed(range(N_FFN)):
            outs = _adamw(net.parts(f"{short}{k}"), net.views[n], view(m, n), view(v, n), k, outs, name=f"adamw_{short}{k}")
        grads[n], deltas[n], new_m[n], new_v[n] = [_from_matrix_view(n, o, p[n].shape) for o in outs]
    for short, n in SINGLE_MATRICES.items():
        outs = _adamw(net.parts(short), net.views[n], view(m, n), view(v, n), 0, [], name=f"adamw_{short}")
        grads[n], deltas[n], new_m[n], new_v[n] = [_from_matrix_view(n, o, p[n].shape) for o in outs]
    for n in VECTOR_NAMES:
        g = vec_full[n]
        if n in SHARDED_VECTORS:
            per = p[n].shape[-1]
            g = lax.dynamic_slice_in_dim(g, pos * per, per, axis=g.ndim - 1)
        grads[n] = g
    packed = _adamw_packed(*[_pack([d[n] for n in VECTOR_NAMES]) for d in (grads, p, m, v)], name="adamw_vectors")
    shapes = [p[n].shape for n in VECTOR_NAMES]
    for d, pk in zip((deltas, new_m, new_v), packed):
        d.update(zip(VECTOR_NAMES, _unpack(pk, shapes)))
    return loss, grad_x, grads, deltas, new_m, new_v


def kernel(x, norm_w, ffn_w_gate, ffn_w_up, ffn_w_down, ssm_w_in, ssm_conv_w, ssm_conv_b, ssm_dt_bias, ssm_a_log, ssm_d, ssm_norm_w, ssm_w_out, kv_norm_w, w_k, b_k, w_v, b_v, attn_w_q, attn_b_q, attn_sinks, attn_w_o, attn_b_o, final_norm_w, loss_target, m_norm_w, m_ffn_w_gate, m_ffn_w_up, m_ffn_w_down, m_ssm_w_in, m_ssm_conv_w, m_ssm_conv_b, m_ssm_dt_bias, m_ssm_a_log, m_ssm_d, m_ssm_norm_w, m_ssm_w_out, m_kv_norm_w, m_w_k, m_b_k, m_w_v, m_b_v, m_attn_w_q, m_attn_b_q, m_attn_sinks, m_attn_w_o, m_attn_b_o, m_final_norm_w, v_norm_w, v_ffn_w_gate, v_ffn_w_up, v_ffn_w_down, v_ssm_w_in, v_ssm_conv_w, v_ssm_conv_b, v_ssm_dt_bias, v_ssm_a_log, v_ssm_d, v_ssm_norm_w, v_ssm_w_out, v_kv_norm_w, v_w_k, v_b_k, v_w_v, v_b_v, v_attn_w_q, v_attn_b_q, v_attn_sinks, v_attn_w_o, v_attn_b_o, v_final_norm_w):
    p = dict(zip(WEIGHT_NAMES, (norm_w, ffn_w_gate, ffn_w_up, ffn_w_down, ssm_w_in, ssm_conv_w, ssm_conv_b, ssm_dt_bias, ssm_a_log, ssm_d, ssm_norm_w, ssm_w_out, kv_norm_w, w_k, b_k, w_v, b_v, attn_w_q, attn_b_q, attn_sinks, attn_w_o, attn_b_o, final_norm_w)))
    m = dict(zip(WEIGHT_NAMES, (m_norm_w, m_ffn_w_gate, m_ffn_w_up, m_ffn_w_down, m_ssm_w_in, m_ssm_conv_w, m_ssm_conv_b, m_ssm_dt_bias, m_ssm_a_log, m_ssm_d, m_ssm_norm_w, m_ssm_w_out, m_kv_norm_w, m_w_k, m_b_k, m_w_v, m_b_v, m_attn_w_q, m_attn_b_q, m_attn_sinks, m_attn_w_o, m_attn_b_o, m_final_norm_w)))
    v = dict(zip(WEIGHT_NAMES, (v_norm_w, v_ffn_w_gate, v_ffn_w_up, v_ffn_w_down, v_ssm_w_in, v_ssm_conv_w, v_ssm_conv_b, v_ssm_dt_bias, v_ssm_a_log, v_ssm_d, v_ssm_norm_w, v_ssm_w_out, v_kv_norm_w, v_w_k, v_b_k, v_w_v, v_b_v, v_attn_w_q, v_attn_b_q, v_attn_sinks, v_attn_w_o, v_attn_b_o, v_final_norm_w)))
    loss, grad_x, grads, deltas, new_m, new_v = _step(x[0], loss_target[0], p, m, v)
    loss = lax.psum(loss[0, 0], ("x", "y", "c"))
    return (loss, grad_x[None], *[grads[n] for n in WEIGHT_NAMES], *[deltas[n] for n in WEIGHT_NAMES],
            *[new_m[n] for n in WEIGHT_NAMES], *[new_v[n] for n in WEIGHT_NAMES])
```

```python
import functools
import math

import jax
import jax.numpy as jnp
from jax import lax
from jax.experimental import pallas as pl
from jax.experimental.pallas import tpu as pltpu

F32 = jnp.float32
BF16 = jnp.bfloat16

N_DEV = 8
SEQ = 2048
D_MODEL = 1024
D_FF_SHARD = 352
N_FFN = 4
D_INNER = 2048
SSM_HEADS = 32
SSM_HEAD_DIM = 64
SSM_GROUPS = 4
HEADS_PER_GROUP = 8
SSM_STATE = 128
CHUNK = 128
N_CHUNKS = SEQ // CHUNK
GN = SSM_GROUPS * SSM_STATE
CONV_DIM = D_INNER + 2 * GN
CONV_WIDTH = 4
ZX_DIM = D_INNER + CONV_DIM
IN_PROJ_SHARD = 644
ATT_HEAD_DIM = 64
N_Q_HEADS = 16
N_KV_HEADS = 4
Q_PER_KV = 4
KV_DIM = N_KV_HEADS * ATT_HEAD_DIM
WINDOW = 128
ROPE_THETA = 10000.0
EPS = 1e-5
FFN_RES_WEIGHT = 0.5
ATT_SCALE = 1.0 / math.sqrt(ATT_HEAD_DIM)
NEG_BIG = -1e30

ADAM_LR = 0.001
ADAM_B1 = 0.9
ADAM_B2 = 0.999
ADAM_EPS = 1e-08
ADAM_WD = 0.01
ADAM_STEP = 10

VMEM_LIMIT_BYTES = 56 * 1024 * 1024

NN = (((1,), (0,)), ((), ()))
NT = (((1,), (1,)), ((), ()))
TN = (((0,), (0,)), ((), ()))
_DIMS = {"nn": NN, "nt": NT, "tn": TN}


def _params(*sem):
    return pltpu.CompilerParams(dimension_semantics=sem if sem else None, vmem_limit_bytes=VMEM_LIMIT_BYTES)


def _dot(a, b, dims=NN):
    return lax.dot_general(a.astype(BF16), b.astype(BF16), dims, preferred_element_type=F32)


def _dot_f32(a, b, dims=NN):
    return lax.dot_general(a, b, dims, precision=lax.Precision.HIGHEST, preferred_element_type=F32)


def _sigmoid(x):
    return 1.0 / (1.0 + jnp.exp(-x))


def _dsilu(x, s):
    return s * (1.0 + x * (1.0 - s))


def _rms(x):
    r = lax.rsqrt(jnp.mean(x * x, axis=-1, keepdims=True) + EPS)
    return x * r, r


def _sum_all(x):
    return jnp.sum(jnp.sum(x, axis=1, keepdims=True), axis=0, keepdims=True)


MESH = pl.DeviceIdType.MESH
N_PEERS = N_DEV - 1
N_CHIPS = N_DEV // 2


def _position():
    return lax.axis_index("x"), lax.axis_index("y"), lax.axis_index("c")


def _slot(p):
    return 4 * p[0] + 2 * p[1] + p[2]


class _Exchange:
    def __init__(self, arrays, out_shapes):
        n = len(arrays)
        self.arrays = list(arrays)
        self.out_shapes = out_shapes
        self.scratch = [pltpu.SemaphoreType.DMA((n, N_PEERS)), pltpu.SemaphoreType.DMA((n, N_PEERS)), pltpu.SemaphoreType.DMA((n,))]
        self.results = None


class _Gather(_Exchange):
    def __init__(self, pieces):
        pieces = [p if isinstance(p, tuple) else (p, None) for p in pieces]
        self.blocks = [k for _, k in pieces]
        shapes = [a.shape if k is None else a.shape[1:] for a, k in pieces]
        super().__init__([a for a, _ in pieces], [jax.ShapeDtypeStruct((N_DEV,) + s, a.dtype) for s, (a, _) in zip(shapes, pieces)])

    def _plan(self, ins, outs, sems):
        send_sems, recv_sems, local_sems = sems
        x, y, c = _position()
        me, sibling = (x, y, c), (x, y, 1 - c)
        chips = [(1 - x, y), (x, 1 - y), (1 - x, 1 - y)]
        n = len(ins)
        ins = [r if k is None else r.at[k] for r, k in zip(ins, self.blocks)]

        def copy(a, k, block, to, src=None):
            dst = outs[a].at[_slot(block)]
            return pltpu.make_async_remote_copy(src_ref=dst if src is None else src, dst_ref=dst, send_sem=send_sems.at[a, k],
                                                recv_sem=recv_sems.at[a, k], device_id=to, device_id_type=MESH)

        mine = [pltpu.make_async_copy(ins[a], outs[a].at[_slot(me)], local_sems.at[a]) for a in range(n)]
        first = []
        for a in range(n):
            first.append(copy(a, 0, me, sibling, src=ins[a]))
            first += [copy(a, 1 + j, me, (*chip, c), src=ins[a]) for j, chip in enumerate(chips)]
        return n, c, me, sibling, chips, copy, mine, first

    def start(self, ins, outs, sems):
        _, _, _, _, _, _, mine, first = self._plan(ins, outs, sems)
        for cp in mine + first:
            cp.start()

    def finish(self, ins, outs, sems):
        n, c, me, sibling, chips, copy, mine, first = self._plan(ins, outs, sems)
        passed = []
        for j, chip in enumerate(chips):
            for a in range(n):
                copy(a, 1 + j, (*chip, c), me).wait_recv()
                fwd = copy(a, 4 + j, (*chip, c), sibling)
                fwd.start()
                passed.append(fwd)
        for a in range(n):
            copy(a, 0, sibling, me).wait_recv()
            for j, chip in enumerate(chips):
                copy(a, 4 + j, (*chip, 1 - c), me).wait_recv()
        for cp in first + passed:
            cp.wait_send()
        for cp in mine:
            cp.wait()


class _PairSwap(_Exchange):
    def __init__(self, arrays):
        n = len(arrays)
        self.arrays = list(arrays)
        self.out_shapes = [jax.ShapeDtypeStruct((N_CHIPS,) + a.shape[1:], a.dtype) for a in arrays]
        self.scratch = [pltpu.SemaphoreType.DMA((n, N_CHIPS)), pltpu.SemaphoreType.DMA((n, N_CHIPS))]
        self.results = None

    def _plan(self, ins, outs, sems):
        send_sems, recv_sems = sems
        x, y, c = _position()
        return [pltpu.make_async_remote_copy(src_ref=ins[a].at[2 * q + 1 - c], dst_ref=outs[a].at[q], send_sem=send_sems.at[a, q],
                                             recv_sem=recv_sems.at[a, q], device_id=(x, y, 1 - c), device_id_type=MESH)
                for a in range(len(ins)) for q in range(N_CHIPS)]

    def start(self, ins, outs, sems):
        for cp in self._plan(ins, outs, sems):
            cp.start()

    def finish(self, ins, outs, sems):
        for cp in self._plan(ins, outs, sems):
            cp.wait()


class _ChipExchange(_Exchange):
    def __init__(self, arrays):
        n = len(arrays)
        self.arrays = list(arrays)
        self.out_shapes = [jax.ShapeDtypeStruct(a.shape, a.dtype) for a in arrays]
        self.scratch = [pltpu.SemaphoreType.DMA((n, 3)), pltpu.SemaphoreType.DMA((n, 3)), pltpu.SemaphoreType.DMA((n,))]
        self.results = None

    def _plan(self, ins, outs, sems):
        send_sems, recv_sems, local_sems = sems
        x, y, c = _position()
        here = 2 * x + y
        chips = [(1 - x, y), (x, 1 - y), (1 - x, 1 - y)]
        n = len(ins)

        def copy(a, k, src_slot, dst_slot):
            return pltpu.make_async_remote_copy(src_ref=ins[a].at[src_slot], dst_ref=outs[a].at[dst_slot], send_sem=send_sems.at[a, k],
                                                recv_sem=recv_sems.at[a, k], device_id=(*chips[k], c), device_id_type=MESH)

        there = [2 * qx + qy for qx, qy in chips]
        mine = [pltpu.make_async_copy(ins[a].at[here], outs[a].at[here], local_sems.at[a]) for a in range(n)]
        sends = [copy(a, k, there[k], here) for a in range(n) for k in range(3)]
        arrivals = lambda: [copy(a, k, here, there[k]) for a in range(n) for k in range(3)]
        return mine, sends, arrivals

    def start(self, ins, outs, sems):
        mine, sends, _ = self._plan(ins, outs, sems)
        for cp in mine + sends:
            cp.start()

    def finish(self, ins, outs, sems):
        mine, sends, arrivals = self._plan(ins, outs, sems)
        for cp in arrivals():
            cp.wait_recv()
        for cp in sends:
            cp.wait_send()
        for cp in mine:
            cp.wait()


def _call(body, *, name, grid, in_specs, out_specs, out_shape, args, scratch_shapes=(), sem=(), comm=(), aliases=None):
    single = not isinstance(out_shape, (list, tuple))
    out_shape = [out_shape] if single else list(out_shape)
    out_specs = [out_specs] if single else list(out_specs)
    comms = list(comm or ())
    n_in, n_out, n_scr = len(args), len(out_shape), len(scratch_shapes)
    params = pltpu.CompilerParams(dimension_semantics=tuple(sem) if sem else None, vmem_limit_bytes=VMEM_LIMIT_BYTES)
    if not comms:
        res = pl.pallas_call(body, name=name, grid=grid, in_specs=list(in_specs), out_specs=out_specs, out_shape=out_shape,
                             scratch_shapes=list(scratch_shapes), input_output_aliases=aliases or {}, compiler_params=params)(*args)
        return res[0] if single else res
    counts = [n_in] + [len(c.arrays) for c in comms] + [n_out] + [len(c.out_shapes) for c in comms] + [n_scr] + [len(c.scratch) for c in comms]
    nc = len(comms)

    def carried(*refs):
        pos, groups = 0, []
        for cnt in counts:
            groups.append(refs[pos:pos + cnt])
            pos += cnt
        ins, c_ins = groups[0], groups[1:1 + nc]
        outs, c_outs = groups[1 + nc], groups[2 + nc:2 + 2 * nc]
        scr, c_sems = groups[2 + 2 * nc], groups[3 + 2 * nc:]
        ids = [pl.program_id(d) for d in range(len(grid))]
        is_first = functools.reduce(jnp.logical_and, [i == 0 for i in ids])
        is_last = functools.reduce(jnp.logical_and, [i == g - 1 for i, g in zip(ids, grid)])

        @pl.when(is_first)
        def _():
            for q, c in enumerate(comms):
                c.start(c_ins[q], c_outs[q], c_sems[q])

        body(*ins, *outs, *scr)

        @pl.when(is_last)
        def _():
            for q, c in enumerate(comms):
                c.finish(c_ins[q], c_outs[q], c_sems[q])

    anyspec = pl.BlockSpec(memory_space=pl.ANY)
    c_arrays = [a for c in comms for a in c.arrays]
    c_shapes = [s for c in comms for s in c.out_shapes]
    res = pl.pallas_call(
        carried, name=name, grid=grid, in_specs=list(in_specs) + [anyspec] * len(c_arrays), out_specs=out_specs + [anyspec] * len(c_shapes),
        out_shape=out_shape + c_shapes, scratch_shapes=list(scratch_shapes) + [s for c in comms for s in c.scratch],
        input_output_aliases=aliases or {}, compiler_params=params)(*args, *c_arrays)
    pos = n_out
    for c in comms:
        c.results = list(res[pos:pos + len(c.out_shapes)])
        pos += len(c.out_shapes)
    return res[0] if single else list(res[:n_out])


def _run_exchange(comm, *, name):
    def body(*refs):
        n_ci, n_co = len(comm.arrays), len(comm.out_shapes)
        ins, outs, sems = refs[:n_ci], refs[n_ci:n_ci + n_co], refs[n_ci + n_co:]
        comm.start(ins, outs, sems)
        comm.finish(ins, outs, sems)

    anyspec = pl.BlockSpec(memory_space=pl.ANY)
    comm.results = list(pl.pallas_call(
        body, name=name, in_specs=[anyspec] * len(comm.arrays), out_specs=[anyspec] * len(comm.out_shapes),
        out_shape=list(comm.out_shapes), scratch_shapes=list(comm.scratch))(*comm.arrays))
    return comm.results


def _mm(a, b, *, dims="nn", bias=None, res=None, out_dtype=F32, name, tm=1024, tn=1024, tk=1024, comm=None, b_rows=None):
    if dims == "tn":
        k_dim, m_dim = a.shape
    else:
        m_dim, k_dim = a.shape
    row0, n_rows = b_rows if b_rows is not None else (0, b.shape[0])
    n_dim = n_rows if dims == "nt" else b.shape[1]
    assert dims == "nt" or n_rows == k_dim, (name, a.shape, b.shape, b_rows)
    tm, tn, tk = min(tm, m_dim), min(tn, n_dim), min(tk, k_dim)
    assert m_dim % tm == 0 and n_dim % tn == 0 and k_dim % tk == 0, (name, a.shape, b.shape)
    nk = k_dim // tk
    a_spec = pl.BlockSpec((tk, tm), lambda i, j, k: (k, i)) if dims == "tn" else pl.BlockSpec((tm, tk), lambda i, j, k: (i, k))
    if dims == "nt":
        assert row0 % tn == 0
        b_spec = pl.BlockSpec((tn, tk), lambda i, j, k: (row0 // tn + j, k))
    else:
        assert row0 % tk == 0
        b_spec = pl.BlockSpec((tk, tn), lambda i, j, k: (row0 // tk + k, j))
    in_specs, args = [a_spec, b_spec], [a, b]
    if bias is not None:
        in_specs.append(pl.BlockSpec((1, tn), lambda i, j, k: (0, j)))
        args.append(bias)
    if res is not None:
        in_specs.append(pl.BlockSpec((tm, tn), lambda i, j, k: (i, j)))
        args.append(res)
    dn = _DIMS[dims]

    def body(*refs):
        a_ref, b_ref = refs[0], refs[1]
        o_ref, acc_ref = refs[-2], refs[-1]
        k = pl.program_id(2)

        @pl.when(k == 0)
        def _():
            acc_ref[...] = jnp.zeros_like(acc_ref)

        acc_ref[...] += _dot(a_ref[...], b_ref[...], dn)

        @pl.when(k == nk - 1)
        def _():
            r = acc_ref[...]
            pos = 2
            if bias is not None:
                r = r + refs[pos][...]
                pos += 1
            if res is not None:
                r = r + refs[pos][...]
            o_ref[...] = r.astype(out_dtype)

    return _call(
        body, name=name, grid=(m_dim // tm, n_dim // tn, nk), in_specs=in_specs,
        out_specs=pl.BlockSpec((tm, tn), lambda i, j, k: (i, j)),
        out_shape=jax.ShapeDtypeStruct((m_dim, n_dim), out_dtype),
        scratch_shapes=[pltpu.VMEM((tm, tn), F32)], sem=("parallel", "parallel", "arbitrary"), args=args, comm=comm)


def _norm_mm(x, nw, w, bias, *, name, tm=1024, tn=1024, comm=None, w_rows=None):
    t_dim, d_dim = x.shape
    transposed = w_rows is not None
    n_dim = w_rows if transposed else w.shape[1]
    tn = min(tn, n_dim)
    assert t_dim % tm == 0 and n_dim % tn == 0
    has_bias = bias is not None
    w_spec = pl.BlockSpec((tn, d_dim), lambda i, j: (j, 0)) if transposed else pl.BlockSpec((d_dim, tn), lambda i, j: (0, j))
    dn = NT if transposed else NN
    in_specs = [pl.BlockSpec((tm, d_dim), lambda i, j: (i, 0)), pl.BlockSpec((1, d_dim), lambda i, j: (0, 0)), w_spec]
    args = [x, nw, w]
    if has_bias:
        in_specs.append(pl.BlockSpec((1, tn), lambda i, j: (0, j)))
        args.append(bias)

    def body(*refs):
        x_ref, nw_ref, w_ref = refs[:3]
        o_ref, h_ref = refs[-2], refs[-1]

        @pl.when(pl.program_id(1) == 0)
        def _():
            xhat, _ = _rms(x_ref[...])
            h_ref[...] = (xhat * nw_ref[...]).astype(BF16)

        r = _dot(h_ref[...], w_ref[...], dn)
        if has_bias:
            r = r + refs[3][...]
        o_ref[...] = r

    return _call(
        body, name=name, grid=(t_dim // tm, n_dim // tn), in_specs=in_specs,
        out_specs=[pl.BlockSpec((tm, tn), lambda i, j: (i, j)), pl.BlockSpec((tm, d_dim), lambda i, j: (i, 0))],
        out_shape=[jax.ShapeDtypeStruct((t_dim, n_dim), F32), jax.ShapeDtypeStruct((t_dim, d_dim), BF16)],
        sem=("parallel", "arbitrary"), args=args, comm=comm)


def _norm_bwd(x, nw, dh, res, *, name, tm=256, comm=None):
    t_dim, d_dim = x.shape
    n_res = len(res)
    row = pl.BlockSpec((tm, d_dim), lambda i: (i, 0))
    vec = pl.BlockSpec((1, d_dim), lambda i: (0, 0))

    def body(*refs):
        x_ref, nw_ref, dh_ref = refs[:3]
        dx_ref, dnw_ref = refs[-2], refs[-1]
        xhat, r = _rms(x_ref[...])
        dh = dh_ref[...]
        dxhat = dh * nw_ref[...]
        dx = r * (dxhat - xhat * jnp.mean(dxhat * xhat, axis=-1, keepdims=True))
        for rr in refs[3:3 + n_res]:
            dx = dx + rr[...]
        dx_ref[...] = dx

        @pl.when(pl.program_id(0) == 0)
        def _():
            dnw_ref[...] = jnp.zeros_like(dnw_ref)

        dnw_ref[...] += jnp.sum(dh * xhat, axis=0, keepdims=True)

    return _call(
        body, name=name, grid=(t_dim // tm,), in_specs=[row, vec, row] + [row] * n_res,
        out_specs=[row, vec],
        out_shape=[jax.ShapeDtypeStruct((t_dim, d_dim), F32), jax.ShapeDtypeStruct((1, d_dim), F32)],
        sem=("arbitrary",), args=[x, nw, dh, *res], comm=comm)


def _colsum(x, *, name, tm=256):
    t_dim, n_dim = x.shape

    def body(x_ref, o_ref):
        @pl.when(pl.program_id(0) == 0)
        def _():
            o_ref[...] = jnp.zeros_like(o_ref)

        o_ref[...] += jnp.sum(x_ref[...], axis=0, keepdims=True)

    return pl.pallas_call(
        body, name=name, grid=(t_dim // tm,), in_specs=[pl.BlockSpec((tm, n_dim), lambda i: (i, 0))],
        out_specs=pl.BlockSpec((1, n_dim), lambda i: (0, 0)), out_shape=jax.ShapeDtypeStruct((1, n_dim), F32),
        compiler_params=_params("arbitrary"),
    )(x)


FFN_ROW_TILE = 512
FFN_SHARDS_PER_STEP = 2
FFN_STEPS = N_DEV // FFN_SHARDS_PER_STEP
FFN_STEP_COLS = FFN_SHARDS_PER_STEP * D_FF_SHARD


def _ffn_step_weights(*refs):
    return [jnp.concatenate([r[s] for s in range(FFN_SHARDS_PER_STEP)], axis=0) for r in refs]


def _ffn_spec(d_dim):
    return pl.BlockSpec((FFN_SHARDS_PER_STEP, D_FF_SHARD, d_dim), lambda j: (j, 0, 0))


def _ffn_fwd(x, nw, wg, wu, wd, *, name, comm=None):
    t_dim, d_dim = x.shape
    n_tiles = t_dim // FFN_ROW_TILE

    def body(x_ref, nw_ref, wg_ref, wu_ref, wd_ref, o_ref, h_scr):
        j = pl.program_id(0)

        @pl.when(j == 0)
        def _():
            xhat, _ = _rms(x_ref[...])
            h_scr[...] = (xhat * nw_ref[...]).astype(BF16)
            o_ref[...] = jnp.zeros_like(o_ref)

        w_gate, w_up, w_down = _ffn_step_weights(wg_ref, wu_ref, wd_ref)
        for t in range(n_tiles):
            rows = pl.ds(t * FFN_ROW_TILE, FFN_ROW_TILE)
            h = h_scr[rows, :]
            g = _dot(h, w_gate, NT)
            u = _dot(h, w_up, NT)
            act = g * _sigmoid(g) * u
            o_ref[rows, :] += _dot(act, w_down)

        @pl.when(j == FFN_STEPS - 1)
        def _():
            o_ref[...] = x_ref[...] + FFN_RES_WEIGHT * o_ref[...]

    full = pl.BlockSpec((t_dim, d_dim), lambda j: (0, 0))
    wspec = _ffn_spec(d_dim)
    return _call(
        body, name=name, grid=(FFN_STEPS,),
        in_specs=[full, pl.BlockSpec((1, d_dim), lambda j: (0, 0)), wspec, wspec, wspec],
        out_specs=full, out_shape=jax.ShapeDtypeStruct((t_dim, d_dim), F32),
        scratch_shapes=[pltpu.VMEM((t_dim, d_dim), BF16)],
        sem=("arbitrary",), args=[x, nw, wg, wu, wd], comm=comm)


def _ffn_bwd_prep(x, nw, dout, *, name, tm=256):
    t_dim, d_dim = x.shape
    row = pl.BlockSpec((tm, d_dim), lambda i: (i, 0))

    def body(x_ref, nw_ref, dout_ref, h_ref, dob_ref):
        xhat, _ = _rms(x_ref[...])
        h_ref[...] = (xhat * nw_ref[...]).astype(BF16)
        dob_ref[...] = (FFN_RES_WEIGHT * dout_ref[...]).astype(BF16)

    return pl.pallas_call(
        body, name=name, grid=(t_dim // tm,), in_specs=[row, pl.BlockSpec((1, d_dim), lambda i: (0, 0)), row],
        out_specs=[row, row], out_shape=[jax.ShapeDtypeStruct((t_dim, d_dim), BF16)] * 2,
        compiler_params=_params("parallel"),
    )(x, nw, dout)


def _ffn_bwd(h, dob, wg, wu, wd, *, name, comm=None):
    t_dim, d_dim = h.shape
    n_tiles = t_dim // FFN_ROW_TILE

    def body(h_ref, dob_ref, wg_ref, wu_ref, wd_ref, dh_ref, gg_ref, gu_ref, gd_ref, dwg_scr, dwu_scr, dwd_scr):
        j = pl.program_id(0)

        @pl.when(j == 0)
        def _():
            dh_ref[...] = jnp.zeros_like(dh_ref)

        w_gate, w_up, w_down = _ffn_step_weights(wg_ref, wu_ref, wd_ref)
        for t in range(n_tiles):
            rows = pl.ds(t * FFN_ROW_TILE, FFN_ROW_TILE)
            hh = h_ref[rows, :]
            do = dob_ref[rows, :]
            g = _dot(hh, w_gate, NT)
            u = _dot(hh, w_up, NT)
            sg = _sigmoid(g)
            s = g * sg
            da = _dot(do, w_down, NT)
            dwd = _dot(s * u, do, TN)
            du = (da * s).astype(BF16)
            dg = (da * u * _dsilu(g, sg)).astype(BF16)
            dwg = _dot(dg, hh, TN)
            dwu = _dot(du, hh, TN)
            if t == 0:
                dwd_scr[...] = dwd
                dwg_scr[...] = dwg
                dwu_scr[...] = dwu
            else:
                dwd_scr[...] += dwd
                dwg_scr[...] += dwg
                dwu_scr[...] += dwu
            dh_ref[rows, :] += _dot(dg, w_gate) + _dot(du, w_up)
        for s in range(FFN_SHARDS_PER_STEP):
            rows = slice(s * D_FF_SHARD, (s + 1) * D_FF_SHARD)
            gg_ref[s] = dwg_scr[rows, :].astype(BF16)
            gu_ref[s] = dwu_scr[rows, :].astype(BF16)
            gd_ref[s] = dwd_scr[rows, :].astype(BF16)

    full_bf = pl.BlockSpec((t_dim, d_dim), lambda j: (0, 0))
    wspec = _ffn_spec(d_dim)
    return _call(
        body, name=name, grid=(FFN_STEPS,),
        in_specs=[full_bf, full_bf, wspec, wspec, wspec], out_specs=[full_bf, wspec, wspec, wspec],
        out_shape=[jax.ShapeDtypeStruct((t_dim, d_dim), F32)] + [jax.ShapeDtypeStruct(wd.shape, BF16)] * 3,
        scratch_shapes=[pltpu.VMEM((FFN_STEP_COLS, d_dim), F32)] * 3,
        sem=("arbitrary",), args=[h, dob, wg, wu, wd], comm=comm)


CONV_COLS = 256


def _shift_down(u, s, rows):
    return jnp.where(rows >= s, pltpu.roll(u, s, 0), 0.0)


def _shift_up(u, s, rows, t_dim):
    return jnp.where(rows < t_dim - s, pltpu.roll(u, t_dim - s, 0), 0.0)


def _conv_pre(u, w_ref, b_ref, rows):
    c = b_ref[...] + w_ref[CONV_WIDTH - 1:CONV_WIDTH, :] * u
    for k in range(CONV_WIDTH - 1):
        c = c + w_ref[k:k + 1, :] * _shift_down(u, CONV_WIDTH - 1 - k, rows)
    return c


def _conv_fwd(zx, cw, cb, *, name, comm=None):
    t_dim = zx.shape[0]
    off = D_INNER // CONV_COLS

    def body(u_ref, w_ref, b_ref, o_ref):
        rows = lax.broadcasted_iota(jnp.int32, (t_dim, CONV_COLS), 0)
        c = _conv_pre(u_ref[...], w_ref, b_ref, rows)
        o_ref[...] = c * _sigmoid(c)

    return _call(
        body, name=name, grid=(CONV_DIM // CONV_COLS,),
        in_specs=[pl.BlockSpec((t_dim, CONV_COLS), lambda j: (0, off + j)),
                  pl.BlockSpec((CONV_WIDTH, CONV_COLS), lambda j: (0, j)), pl.BlockSpec((1, CONV_COLS), lambda j: (0, j))],
        out_specs=pl.BlockSpec((t_dim, CONV_COLS), lambda j: (0, j)),
        out_shape=jax.ShapeDtypeStruct((t_dim, CONV_DIM), F32), sem=("parallel",), args=[zx, cw, cb], comm=comm)


def _conv_bwd(zx, cw, cb, dxs, db, dc, dzx, *, name, comm=None):
    t_dim = zx.shape[0]
    off = D_INNER // CONV_COLS
    n_xs = D_INNER // CONV_COLS
    n_b = GN // CONV_COLS

    def body(u_ref, w_ref, b_ref, dxs_ref, db_ref, dc_ref, dzx_in, dzx_ref, dw_ref, dbias_ref):
        j = pl.program_id(0)
        rows = lax.broadcasted_iota(jnp.int32, (t_dim, CONV_COLS), 0)
        u = u_ref[...]
        c = _conv_pre(u, w_ref, b_ref, rows)
        d = jnp.where(j < n_xs, dxs_ref[...], jnp.where(j < n_xs + n_b, db_ref[...], dc_ref[...]))
        dcv = d * _dsilu(c, _sigmoid(c))
        dpre = w_ref[CONV_WIDTH - 1:CONV_WIDTH, :] * dcv
        dw_ref[CONV_WIDTH - 1:CONV_WIDTH, :] = jnp.sum(dcv * u, axis=0, keepdims=True)
        for k in range(CONV_WIDTH - 1):
            s = CONV_WIDTH - 1 - k
            dpre = dpre + w_ref[k:k + 1, :] * _shift_up(dcv, s, rows, t_dim)
            dw_ref[k:k + 1, :] = jnp.sum(dcv * _shift_down(u, s, rows), axis=0, keepdims=True)
        dzx_ref[...] = dpre
        dbias_ref[...] = jnp.sum(dcv, axis=0, keepdims=True)

    blk = lambda n: pl.BlockSpec((t_dim, CONV_COLS), n)
    return _call(
        body, name=name, grid=(CONV_DIM // CONV_COLS,),
        in_specs=[blk(lambda j: (0, off + j)), pl.BlockSpec((CONV_WIDTH, CONV_COLS), lambda j: (0, j)),
                  pl.BlockSpec((1, CONV_COLS), lambda j: (0, j)),
                  blk(lambda j: (0, jnp.minimum(j, n_xs - 1))),
                  blk(lambda j: (0, jnp.clip(j - n_xs, 0, n_b - 1))),
                  blk(lambda j: (0, jnp.clip(j - n_xs - n_b, 0, n_b - 1))),
                  pl.BlockSpec(memory_space=pl.ANY)],
        out_specs=[blk(lambda j: (0, off + j)), pl.BlockSpec((CONV_WIDTH, CONV_COLS), lambda j: (0, j)),
                   pl.BlockSpec((1, CONV_COLS), lambda j: (0, j))],
        out_shape=[jax.ShapeDtypeStruct(dzx.shape, F32), jax.ShapeDtypeStruct((CONV_WIDTH, CONV_DIM), F32),
                   jax.ShapeDtypeStruct((1, CONV_DIM), F32)],
        aliases={6: 0}, sem=("parallel",), args=[zx, cw, cb, dxs, db, dc, dzx], comm=comm)


def _softplus_parts(x):
    e = jnp.exp(-jnp.abs(x))
    u = 1.0 + e
    log1p_e = jnp.where(u == 1.0, e, jnp.log(u) * e / jnp.where(u == 1.0, 1.0, u - 1.0))
    return jnp.maximum(x, 0.0) + log1p_e


def _dt_prep(dtr, dt_bias, a_log, *, name):
    def body(dtr_ref, bias_ref, alog_ref, dt_ref, a_ref):
        dt = _softplus_parts(dtr_ref[...] + bias_ref[...])
        dt_ref[...] = dt
        a_ref[...] = dt * (-jnp.exp(alog_ref[...]))

    return pl.pallas_call(body, name=name, out_shape=[jax.ShapeDtypeStruct(dtr.shape, F32)] * 2,
                          compiler_params=_params())(dtr, dt_bias, a_log)


def _dt_bwd(dtr, dt_bias, a_log, dt, ddt, da, *, name):
    def body(dtr_ref, bias_ref, alog_ref, dt_ref, ddt_ref, da_ref, ddtr_ref, dbias_ref, dalog_ref):
        a_neg = -jnp.exp(alog_ref[...])
        da_v = da_ref[...]
        ddt_tot = ddt_ref[...] + da_v * a_neg
        ddtr = ddt_tot * _sigmoid(dtr_ref[...] + bias_ref[...])
        ddtr_ref[...] = ddtr
        dbias_ref[...] = jnp.sum(ddtr, axis=0, keepdims=True)
        dalog_ref[...] = jnp.sum(da_v * dt_ref[...], axis=0, keepdims=True) * a_neg

    return pl.pallas_call(
        body, name=name,
        out_shape=[jax.ShapeDtypeStruct(dtr.shape, F32), jax.ShapeDtypeStruct((1, SSM_HEADS), F32),
                   jax.ShapeDtypeStruct((1, SSM_HEADS), F32)],
        compiler_params=_params())(dtr, dt_bias, a_log, dt, ddt, da)


GROUP_COLS = HEADS_PER_GROUP * SSM_HEAD_DIM
LANES = 128
HEADS_PER_LANE_BLOCK = LANES // SSM_HEAD_DIM


def _split3(x):
    hi = x.astype(BF16)
    r1 = x - hi.astype(F32)
    mid = r1.astype(BF16)
    lo = (r1 - mid.astype(F32)).astype(BF16)
    return hi, mid, lo


def _group_sums(vals, expand):
    x = jnp.concatenate(vals, axis=0)
    out = None
    for part in _split3(x):
        t = lax.dot_general(part, expand, NT, preferred_element_type=F32)
        out = t if out is None else out + t
    return [out[i * CHUNK:(i + 1) * CHUNK] for i in range(len(vals))]


def _ssd_chunk_common(a_ref, dt_ref, b_ref, c_ref):
    row = lax.broadcasted_iota(jnp.int32, (CHUNK, CHUNK), 0)
    col = lax.broadcasted_iota(jnp.int32, (CHUNK, CHUNK), 1)
    causal = col <= row
    lower = causal.astype(F32)
    upper = (col >= row).astype(F32)
    head = lax.broadcasted_iota(jnp.int32, (HEADS_PER_GROUP, GROUP_COLS), 0)
    lane = lax.broadcasted_iota(jnp.int32, (HEADS_PER_GROUP, GROUP_COLS), 1)
    expand = ((lane >= head * SSM_HEAD_DIM) & (lane < (head + 1) * SSM_HEAD_DIM)).astype(F32)
    a = a_ref[...]
    cs = _dot_f32(lower, a)
    cs_row = _dot_f32(a, upper, TN)
    cs_x = _dot_f32(cs, expand)
    dt_x = _dot_f32(dt_ref[...], expand)
    e_out_x = jnp.exp(cs_x)
    e_st_x = jnp.exp(cs_x[CHUNK - 1:CHUNK, :] - cs_x)
    bc = b_ref[...]
    cc = c_ref[...]
    cb = _dot(cc, bc, NT)
    return causal, upper, expand.astype(BF16), cs, cs_row, dt_x, e_out_x, e_st_x, bc, cc, cb


def _head_decay(causal, cs, cs_row, h):
    return jnp.exp(jnp.where(causal, cs[:, h:h + 1] - cs_row[h:h + 1, :], NEG_BIG))


def _lane_block_head_masks():
    lane = lax.broadcasted_iota(jnp.int32, (CHUNK, LANES), 1)
    return [(lane >= i * SSM_HEAD_DIM) & (lane < (i + 1) * SSM_HEAD_DIM) for i in range(HEADS_PER_LANE_BLOCK)]


def _decay_state(dst_ref, old, new, cs):
    for h in range(HEADS_PER_GROUP):
        rows = slice(h * SSM_HEAD_DIM, (h + 1) * SSM_HEAD_DIM)
        dst_ref[rows, :] = jnp.exp(cs[CHUNK - 1:CHUNK, h:h + 1]) * old[rows, :] + new[rows, :]


def _ssd_fwd(xbc, dtg, ag, dgx, *, name, comm=None):
    t_dim = xbc.shape[0]

    def body(xs_ref, b_ref, c_ref, dt_ref, a_ref, d_ref, y_ref, st_ref, s_scr):
        @pl.when(pl.program_id(1) == 0)
        def _():
            s_scr[...] = jnp.zeros_like(s_scr)

        causal, _, _, cs, cs_row, dt_x, e_out_x, e_st_x, bc, cc, cb = _ssd_chunk_common(a_ref, dt_ref, b_ref, c_ref)
        masks = _lane_block_head_masks()
        xs = xs_ref[...]
        xdt_x = xs * dt_x
        prev = s_scr[...]
        st_ref[...] = prev
        y_off = e_out_x * _dot(cc, prev, NT) + xs * d_ref[...]
        for blk in range(GROUP_COLS // LANES):
            lanes = slice(blk * LANES, (blk + 1) * LANES)
            x_b = xdt_x[:, lanes].astype(BF16)
            acc = y_off[:, lanes]
            for i in range(HEADS_PER_LANE_BLOCK):
                m = cb * _head_decay(causal, cs, cs_row, blk * HEADS_PER_LANE_BLOCK + i)
                acc = acc + _dot(m, jnp.where(masks[i], x_b, jnp.zeros_like(x_b)))
            y_ref[:, lanes] = acc
        _decay_state(s_scr, prev, _dot(xdt_x * e_st_x, bc, TN), cs)

    xs = pl.BlockSpec((CHUNK, GROUP_COLS), lambda g, c: (c, g))
    bsp = pl.BlockSpec((CHUNK, SSM_STATE), lambda g, c: (c, D_INNER // SSM_STATE + g))
    csp = pl.BlockSpec((CHUNK, SSM_STATE), lambda g, c: (c, (D_INNER + GN) // SSM_STATE + g))
    per_head = pl.BlockSpec((None, CHUNK, HEADS_PER_GROUP), lambda g, c: (g, c, 0))
    dsk = pl.BlockSpec((None, 1, GROUP_COLS), lambda g, c: (g, 0, 0))
    return _call(
        body, name=name, grid=(SSM_GROUPS, N_CHUNKS),
        in_specs=[xs, bsp, csp, per_head, per_head, dsk],
        out_specs=[xs, pl.BlockSpec((None, GROUP_COLS, SSM_STATE), lambda g, c: (c, g, 0))],
        out_shape=[jax.ShapeDtypeStruct((t_dim, D_INNER), F32),
                   jax.ShapeDtypeStruct((N_CHUNKS, D_INNER, SSM_STATE), F32)],
        scratch_shapes=[pltpu.VMEM((GROUP_COLS, SSM_STATE), F32)],
        sem=("parallel", "arbitrary"), args=[xbc, xbc, xbc, dtg, ag, dgx], comm=comm)


def _ssd_bwd(xbc, dtg, ag, dgx, states, dy, *, name, comm=None):
    t_dim = xbc.shape[0]
    last = N_CHUNKS - 1

    def body(xs_ref, b_ref, c_ref, dt_ref, a_ref, d_ref, st_ref, dy_ref,
             dxs_ref, db_ref, dc_ref, ddt_ref, da_ref, dd_ref, ds_scr):
        @pl.when(pl.program_id(1) == 0)
        def _():
            ds_scr[...] = jnp.zeros_like(ds_scr)
            dd_ref[...] = jnp.zeros_like(dd_ref)

        causal, upper, expand, cs, cs_row, dt_x, e_out_x, e_st_x, bc, cc, cb = _ssd_chunk_common(a_ref, dt_ref, b_ref, c_ref)
        masks = _lane_block_head_masks()
        xs = xs_ref[...]
        dy_x = dy_ref[...]
        xdt_x = xs * dt_x
        prev = st_ref[...]
        d_s = ds_scr[...]
        g1_x = _dot(bc, d_s, NT)
        cp_x = _dot(cc, prev, NT)
        d_cb = jnp.zeros((CHUNK, CHUNK), F32)
        lane8 = lax.broadcasted_iota(jnp.int32, (CHUNK, HEADS_PER_GROUP), 1)
        sub8 = lax.broadcasted_iota(jnp.int32, (HEADS_PER_GROUP, CHUNK), 0)
        row_w = jnp.zeros((CHUNK, HEADS_PER_GROUP), F32)
        col_w = jnp.zeros((HEADS_PER_GROUP, CHUNK), F32)
        dxdt_blocks = []
        for blk in range(GROUP_COLS // LANES):
            lanes = slice(blk * LANES, (blk + 1) * LANES)
            dy_b = dy_x[:, lanes].astype(BF16)
            x_b = xdt_x[:, lanes].astype(BF16)
            acc_dx = jnp.zeros((CHUNK, LANES), F32)
            for i in range(HEADS_PER_LANE_BLOCK):
                h = blk * HEADS_PER_LANE_BLOCK + i
                decay = _head_decay(causal, cs, cs_row, h)
                m = cb * decay
                dy_h = jnp.where(masks[i], dy_b, jnp.zeros_like(dy_b))
                acc_dx = acc_dx + _dot(m, dy_h, TN)
                d_m = _dot(dy_h, x_b, NT)
                d_cb = d_cb + d_m * decay
                w = d_m * m
                row_w = jnp.where(lane8 == h, jnp.sum(w, axis=1, keepdims=True), row_w)
                col_w = jnp.where(sub8 == h, jnp.sum(w, axis=0, keepdims=True), col_w)
            dxdt_blocks.append(acc_dx)
        dxdt_x = jnp.concatenate(dxdt_blocks, axis=1) + e_st_x * g1_x
        dxs_ref[...] = dxdt_x * dt_x + dy_x * d_ref[...]
        dye = dy_x * e_out_x
        xde = xdt_x * e_st_x
        ddt, y_off, tl, dskip = _group_sums([dxdt_x * xs, dye * cp_x, xde * g1_x, dy_x * xs], expand)
        ddt_ref[...] = ddt
        dd_ref[...] += jnp.sum(dskip, axis=0, keepdims=True)
        sp = None
        for part in _split3(d_s * prev):
            t = lax.dot_general(expand, part, NN, preferred_element_type=F32)
            sp = t if sp is None else sp + t
        last_col = jnp.exp(cs_row[:, CHUNK - 1:CHUNK]) * jnp.sum(sp, axis=1, keepdims=True)
        eye = lax.broadcasted_iota(jnp.int32, (HEADS_PER_GROUP, HEADS_PER_GROUP), 0) == lax.broadcasted_iota(
            jnp.int32, (HEADS_PER_GROUP, HEADS_PER_GROUP), 1)
        last_row = jnp.sum(jnp.where(eye, last_col, 0.0), axis=0, keepdims=True) + jnp.sum(tl, axis=0, keepdims=True)
        is_last = lax.broadcasted_iota(jnp.int32, (CHUNK, 1), 0) == CHUNK - 1
        d_cs = row_w + y_off - tl + jnp.where(is_last, last_row, 0.0)
        da_ref[...] = _dot_f32(upper, d_cs) - _dot_f32(upper, col_w, NT)
        dc_ref[...] = _dot(d_cb, bc) + _dot(dye, prev)
        db_ref[...] = _dot(d_cb, cc, TN) + _dot(xde, d_s)
        _decay_state(ds_scr, d_s, _dot(dye, cc, TN), cs)

    rev = lambda c: last - c
    xs = pl.BlockSpec((CHUNK, GROUP_COLS), lambda g, c: (rev(c), g))
    bsp = pl.BlockSpec((CHUNK, SSM_STATE), lambda g, c: (rev(c), D_INNER // SSM_STATE + g))
    csp = pl.BlockSpec((CHUNK, SSM_STATE), lambda g, c: (rev(c), (D_INNER + GN) // SSM_STATE + g))
    per_head = pl.BlockSpec((None, CHUNK, HEADS_PER_GROUP), lambda g, c: (g, rev(c), 0))
    dsk = pl.BlockSpec((None, 1, GROUP_COLS), lambda g, c: (g, 0, 0))
    dsum = pl.BlockSpec((None, 1, HEADS_PER_GROUP), lambda g, c: (g, 0, 0))
    st = pl.BlockSpec((None, GROUP_COLS, SSM_STATE), lambda g, c: (rev(c), g, 0))
    grp = pl.BlockSpec((CHUNK, SSM_STATE), lambda g, c: (rev(c), g))
    return _call(
        body, name=name, grid=(SSM_GROUPS, N_CHUNKS),
        in_specs=[xs, bsp, csp, per_head, per_head, dsk, st, xs],
        out_specs=[xs, grp, grp, per_head, per_head, dsum],
        out_shape=[jax.ShapeDtypeStruct((t_dim, D_INNER), F32), jax.ShapeDtypeStruct((t_dim, GN), F32),
                   jax.ShapeDtypeStruct((t_dim, GN), F32),
                   jax.ShapeDtypeStruct((SSM_GROUPS, t_dim, HEADS_PER_GROUP), F32),
                   jax.ShapeDtypeStruct((SSM_GROUPS, t_dim, HEADS_PER_GROUP), F32),
                   jax.ShapeDtypeStruct((SSM_GROUPS, 1, HEADS_PER_GROUP), F32)],
        scratch_shapes=[pltpu.VMEM((GROUP_COLS, SSM_STATE), F32)],
        sem=("parallel", "arbitrary"), args=[xbc, xbc, xbc, dtg, ag, dgx, states, dy], comm=comm)


NORM_GROUP = D_INNER // SSM_GROUPS


def _gate_norm_fwd(y, zx, nw, *, name, tm=256):
    t_dim = y.shape[0]
    row = pl.BlockSpec((tm, D_INNER), lambda i: (i, 0))

    def body(y_ref, z_ref, nw_ref, o_ref):
        z = z_ref[...]
        yz = y_ref[...] * (z * _sigmoid(z))
        for g in range(SSM_GROUPS):
            cols = slice(g * NORM_GROUP, (g + 1) * NORM_GROUP)
            yhat, _ = _rms(yz[:, cols])
            o_ref[:, cols] = (yhat * nw_ref[:, cols]).astype(BF16)

    return pl.pallas_call(
        body, name=name, grid=(t_dim // tm,), in_specs=[row, row, pl.BlockSpec((1, D_INNER), lambda i: (0, 0))],
        out_specs=row, out_shape=jax.ShapeDtypeStruct((t_dim, D_INNER), BF16),
        compiler_params=_params("parallel"),
    )(y, zx, nw)


def _gate_norm_bwd(y, zx, nw, dyn, *, name, tm=256):
    t_dim = y.shape[0]
    row = pl.BlockSpec((tm, D_INNER), lambda i: (i, 0))
    vec = pl.BlockSpec((1, D_INNER), lambda i: (0, 0))

    def body(y_ref, z_ref, nw_ref, dyn_ref, dy_ref, dz_ref, dnw_ref):
        @pl.when(pl.program_id(0) == 0)
        def _():
            dnw_ref[...] = jnp.zeros_like(dnw_ref)

        z = z_ref[...]
        yv = y_ref[...]
        sg = _sigmoid(z)
        silu_z = z * sg
        yz = yv * silu_z
        dyn_v = dyn_ref[...]
        for g in range(SSM_GROUPS):
            cols = slice(g * NORM_GROUP, (g + 1) * NORM_GROUP)
            yhat, r = _rms(yz[:, cols])
            dn = dyn_v[:, cols]
            dnw_ref[:, cols] += jnp.sum(dn * yhat, axis=0, keepdims=True)
            dyhat = dn * nw_ref[:, cols]
            dyz = r * (dyhat - yhat * jnp.mean(dyhat * yhat, axis=-1, keepdims=True))
            dy_ref[:, cols] = dyz * silu_z[:, cols]
            dz_ref[:, cols] = dyz * yv[:, cols] * _dsilu(z[:, cols], sg[:, cols])

    return pl.pallas_call(
        body, name=name, grid=(t_dim // tm,), in_specs=[row, row, vec, row],
        out_specs=[row, row, vec],
        out_shape=[jax.ShapeDtypeStruct((t_dim, D_INNER), F32), jax.ShapeDtypeStruct((t_dim, ZX_DIM), F32),
                   jax.ShapeDtypeStruct((1, D_INNER), F32)],
        compiler_params=_params("arbitrary"),
    )(y, zx, nw, dyn)


def _rope(t, cos2, sin2, *, name, tm=256):
    t_dim, width = t.shape
    half = ATT_HEAD_DIM // 2
    reps = width // 128

    def body(t_ref, cos_ref, sin_ref, o_ref):
        x = t_ref[...]
        lane = lax.broadcasted_iota(jnp.int32, (tm, width), 1)
        first = (lane % ATT_HEAD_DIM) < half
        rot = jnp.where(first, -pltpu.roll(x, width - half, 1), pltpu.roll(x, half, 1))
        o_ref[...] = x * jnp.tile(cos_ref[...], (1, reps)) + rot * jnp.tile(sin_ref[...], (1, reps))

    row = pl.BlockSpec((tm, width), lambda i: (i, 0))
    tab = pl.BlockSpec((tm, 128), lambda i: (i, 0))
    return pl.pallas_call(
        body, name=name, grid=(t_dim // tm,), in_specs=[row, tab, tab], out_specs=row,
        out_shape=jax.ShapeDtypeStruct((t_dim, width), F32), compiler_params=_params("parallel"),
    )(t, cos2, sin2)


def _attn_masks(n):
    row = lax.broadcasted_iota(jnp.int32, (WINDOW, WINDOW), 0)
    col = lax.broadcasted_iota(jnp.int32, (WINDOW, WINDOW), 1)
    return col <= row, (col > row) & (n > 0)


def _attn_fwd(q, k, v, sinks, *, name, comm=None):
    t_dim = q.shape[0]

    def body(q_ref, kc_ref, kp_ref, vc_ref, vp_ref, s_ref, o_ref, l_ref):
        n = pl.program_id(0)
        mask_c, mask_p = _attn_masks(n)
        lane = lax.broadcasted_iota(jnp.int32, (WINDOW, N_Q_HEADS), 1)
        lse = jnp.zeros((WINDOW, N_Q_HEADS), F32)
        for kvh in range(N_KV_HEADS):
            kcols = slice(kvh * ATT_HEAD_DIM, (kvh + 1) * ATT_HEAD_DIM)
            kc, kp = kc_ref[:, kcols].astype(BF16), kp_ref[:, kcols].astype(BF16)
            vc, vp = vc_ref[:, kcols].astype(BF16), vp_ref[:, kcols].astype(BF16)
            for g in range(Q_PER_KV):
                h = kvh * Q_PER_KV + g
                cols = slice(h * ATT_HEAD_DIM, (h + 1) * ATT_HEAD_DIM)
                qh = q_ref[:, cols].astype(BF16)
                sc = jnp.where(mask_c, _dot(qh, kc, NT) * ATT_SCALE, NEG_BIG)
                sp = jnp.where(mask_p, _dot(qh, kp, NT) * ATT_SCALE, NEG_BIG)
                sink = s_ref[:, h:h + 1]
                m = jnp.maximum(jnp.maximum(jnp.max(sc, axis=1, keepdims=True), jnp.max(sp, axis=1, keepdims=True)), sink)
                pc = jnp.exp(sc - m)
                pp = jnp.exp(sp - m)
                den = jnp.sum(pc, axis=1, keepdims=True) + jnp.sum(pp, axis=1, keepdims=True) + jnp.exp(sink - m)
                o_ref[:, cols] = (_dot(pc, vc) + _dot(pp, vp)) / den
                lse = jnp.where(lane == h, m + jnp.log(den), lse)
        l_ref[...] = lse

    cur = lambda w: pl.BlockSpec((WINDOW, w), lambda n: (n, 0))
    prv = lambda w: pl.BlockSpec((WINDOW, w), lambda n: (jnp.maximum(n - 1, 0), 0))
    return _call(
        body, name=name, grid=(t_dim // WINDOW,),
        in_specs=[cur(D_MODEL), cur(KV_DIM), prv(KV_DIM), cur(KV_DIM), prv(KV_DIM), pl.BlockSpec((1, N_Q_HEADS), lambda n: (0, 0))],
        out_specs=[cur(D_MODEL), cur(N_Q_HEADS)],
        out_shape=[jax.ShapeDtypeStruct((t_dim, D_MODEL), F32), jax.ShapeDtypeStruct((t_dim, N_Q_HEADS), F32)],
        sem=("parallel",), args=[q, k, k, v, v, sinks], comm=comm)


def _attn_bwd(q, k, v, sinks, o, lse, do, *, name, comm=None):
    t_dim = q.shape[0]

    def body(q_ref, kc_ref, kp_ref, vc_ref, vp_ref, s_ref, o_ref, l_ref, do_ref, dq_ref, dk_ref, dv_ref, dsink_ref):
        n = pl.program_id(0)

        @pl.when(n == 0)
        def _():
            dk_ref[...] = jnp.zeros_like(dk_ref)
            dv_ref[...] = jnp.zeros_like(dv_ref)
            dsink_ref[...] = jnp.zeros_like(dsink_ref)

        mask_c, mask_p = _attn_masks(n)
        lane_row = lax.broadcasted_iota(jnp.int32, (1, N_Q_HEADS), 1)
        rows_c = pl.ds(pl.multiple_of(n * WINDOW, WINDOW), WINDOW)
        rows_p = pl.ds(pl.multiple_of(jnp.maximum(n - 1, 0) * WINDOW, WINDOW), WINDOW)
        dsink = jnp.zeros((1, N_Q_HEADS), F32)
        for kvh in range(N_KV_HEADS):
            kcols = slice(kvh * ATT_HEAD_DIM, (kvh + 1) * ATT_HEAD_DIM)
            kc, kp = kc_ref[:, kcols].astype(BF16), kp_ref[:, kcols].astype(BF16)
            vc, vp = vc_ref[:, kcols].astype(BF16), vp_ref[:, kcols].astype(BF16)
            dkc = jnp.zeros((WINDOW, ATT_HEAD_DIM), F32)
            dkp = jnp.zeros((WINDOW, ATT_HEAD_DIM), F32)
            dvc = jnp.zeros((WINDOW, ATT_HEAD_DIM), F32)
            dvp = jnp.zeros((WINDOW, ATT_HEAD_DIM), F32)
            for g in range(Q_PER_KV):
                h = kvh * Q_PER_KV + g
                cols = slice(h * ATT_HEAD_DIM, (h + 1) * ATT_HEAD_DIM)
                qh = q_ref[:, cols].astype(BF16)
                lh = l_ref[:, h:h + 1]
                pc = jnp.exp(jnp.where(mask_c, _dot(qh, kc, NT) * ATT_SCALE, NEG_BIG) - lh)
                pp = jnp.exp(jnp.where(mask_p, _dot(qh, kp, NT) * ATT_SCALE, NEG_BIG) - lh)
                doh = do_ref[:, cols]
                delta = jnp.sum(doh * o_ref[:, cols], axis=1, keepdims=True)
                dsc = pc * (_dot(doh, vc, NT) - delta)
                dsp = pp * (_dot(doh, vp, NT) - delta)
                dq_ref[:, cols] = (_dot(dsc, kc) + _dot(dsp, kp)) * ATT_SCALE
                dkc = dkc + _dot(dsc, qh, TN) * ATT_SCALE
                dkp = dkp + _dot(dsp, qh, TN) * ATT_SCALE
                dvc = dvc + _dot(pc, doh, TN)
                dvp = dvp + _dot(pp, doh, TN)
                p_sink = jnp.exp(s_ref[:, h:h + 1] - lh)
                dsink = jnp.where(lane_row == h, -jnp.sum(p_sink * delta, axis=0, keepdims=True), dsink)
            dk_ref[rows_c, kcols] += dkc
            dk_ref[rows_p, kcols] += dkp
            dv_ref[rows_c, kcols] += dvc
            dv_ref[rows_p, kcols] += dvp
        dsink_ref[...] += dsink

    cur = lambda w: pl.BlockSpec((WINDOW, w), lambda n: (n, 0))
    prv = lambda w: pl.BlockSpec((WINDOW, w), lambda n: (jnp.maximum(n - 1, 0), 0))
    whole = pl.BlockSpec((t_dim, KV_DIM), lambda n: (0, 0))
    svec = pl.BlockSpec((1, N_Q_HEADS), lambda n: (0, 0))
    return _call(
        body, name=name, grid=(t_dim // WINDOW,),
        in_specs=[cur(D_MODEL), cur(KV_DIM), prv(KV_DIM), cur(KV_DIM), prv(KV_DIM), svec, cur(D_MODEL), cur(N_Q_HEADS), cur(D_MODEL)],
        out_specs=[cur(D_MODEL), whole, whole, svec],
        out_shape=[jax.ShapeDtypeStruct((t_dim, D_MODEL), F32), jax.ShapeDtypeStruct((t_dim, KV_DIM), F32),
                   jax.ShapeDtypeStruct((t_dim, KV_DIM), F32), jax.ShapeDtypeStruct((1, N_Q_HEADS), F32)],
        sem=("arbitrary",), args=[q, k, k, v, v, sinks, o, lse, do], comm=comm)


def _loss_head(x, nw, target, *, name, tm=256):
    t_dim, d_dim = x.shape
    row = pl.BlockSpec((tm, d_dim), lambda i: (i, 0))
    vec = pl.BlockSpec((1, d_dim), lambda i: (0, 0))

    def body(x_ref, nw_ref, tgt_ref, loss_ref, dx_ref, dnw_ref):
        @pl.when(pl.program_id(0) == 0)
        def _():
            loss_ref[...] = jnp.zeros_like(loss_ref)
            dnw_ref[...] = jnp.zeros_like(dnw_ref)

        xhat, r = _rms(x_ref[...])
        err = xhat * nw_ref[...] - tgt_ref[...]
        loss_ref[...] += 0.5 * _sum_all(jnp.mean(err * err, axis=-1, keepdims=True))
        dy = err * (1.0 / d_dim)
        dnw_ref[...] += jnp.sum(dy * xhat, axis=0, keepdims=True)
        dxhat = dy * nw_ref[...]
        dx_ref[...] = r * (dxhat - xhat * jnp.mean(dxhat * xhat, axis=-1, keepdims=True))

    return pl.pallas_call(
        body, name=name, grid=(t_dim // tm,), in_specs=[row, vec, row],
        out_specs=[pl.BlockSpec((1, 1), lambda i: (0, 0)), row, vec],
        out_shape=[jax.ShapeDtypeStruct((1, 1), F32), jax.ShapeDtypeStruct((t_dim, d_dim), F32),
                   jax.ShapeDtypeStruct((1, d_dim), F32)],
        compiler_params=_params("arbitrary"),
    )(x, nw, target)


def _rope_tables():
    pos = jnp.arange(SEQ, dtype=F32)
    inv = 1.0 / (ROPE_THETA ** (jnp.arange(0, ATT_HEAD_DIM, 2, dtype=F32) / ATT_HEAD_DIM))
    ang = pos[:, None] * inv[None, :]
    cos, sin = jnp.cos(ang), jnp.sin(ang)
    return jnp.tile(cos, (1, 4)), jnp.tile(sin, (1, 4))


def _to_groups(t):
    return t.reshape(t.shape[0], SSM_GROUPS, HEADS_PER_GROUP).transpose(1, 0, 2)


def _from_groups(t):
    return t.transpose(1, 0, 2).reshape(t.shape[1], SSM_HEADS)


def _forward_backward(x0, target, net):
    w = net.w
    nw = [[w("norm_w")[l, i][None, :] for i in range(3)] for l in range(2)]
    cos2, sin2 = _rope_tables()
    ffn_norm = [nw[0][0], nw[0][2], nw[1][0], nw[1][2]]

    def ffn_f(x, blk):
        name = f"ffn_fwd{blk}"
        return _ffn_fwd(x, ffn_norm[blk], w(f"gate{blk}"), w(f"up{blk}"), w(f"down{blk}"), name=name, comm=net.carry(name))

    x1 = ffn_f(x0, 0)
    zx, h1 = _norm_mm(x1, nw[0][1], w("w_in_t"), None, w_rows=ZX_DIM, name="ssm_in_proj", comm=net.carry("ssm_in_proj"))
    dtr = _mm(h1, w("w_in_t"), dims="nt", b_rows=(ZX_DIM, SSM_HEADS), name="ssm_dt_proj")
    xbc = _conv_fwd(zx, w("conv_w"), w("conv_b"), name="ssm_conv_fwd", comm=net.carry("ssm_conv_fwd"))
    dt, a_dt = _dt_prep(dtr, w("dt_bias"), w("a_log"), name="ssm_dt_prep")
    dtg, ag = _to_groups(dt), _to_groups(a_dt)
    dg = jnp.repeat(w("d_skip").reshape(SSM_GROUPS, 1, HEADS_PER_GROUP), SSM_HEAD_DIM, axis=2)
    y_ssd, states = _ssd_fwd(xbc, dtg, ag, dg, name="ssd_fwd", comm=net.carry("ssd_fwd"))
    yn = _gate_norm_fwd(y_ssd, zx, w("ssm_norm_w"), name="ssm_gate_norm_fwd")
    x2 = _mm(yn, w("wout"), res=x1, name="ssm_out_proj", comm=net.carry("ssm_out_proj"))
    x3 = ffn_f(x2, 1)
    k_pre, hk = _norm_mm(x3, w("kv_norm_w"), w("wk"), w("b_k"), name="k_proj")
    v = _mm(hk, w("wv"), bias=w("b_v"), name="v_proj")
    k_rot = _rope(k_pre, cos2, sin2, name="k_rope")
    x4 = ffn_f(x3, 2)
    q_pre, h4 = _norm_mm(x4, nw[1][1], w("wq"), w("b_q"), name="q_proj")
    q_rot = _rope(q_pre, cos2, sin2, name="q_rope")
    att, lse = _attn_fwd(q_rot, k_rot, v, w("sinks"), name="attn_fwd", comm=net.carry("attn_fwd"))
    x5 = _mm(att, w("wo"), bias=w("b_o"), res=x4, name="attn_out_proj")
    x6 = ffn_f(x5, 3)
    loss, dx6, d_final = _loss_head(x6, w("final_norm_w"), target, name="loss_head")

    d_norm = [[None] * 3 for _ in range(2)]

    def ffn_b(x, dout, blk):
        h, dob = _ffn_bwd_prep(x, ffn_norm[blk], dout, name=f"ffn_bwd_prep{blk}")
        name = f"ffn_bwd{blk}"
        dh, gg, gu, gd = _ffn_bwd(h, dob, w(f"gate{blk}"), w(f"up{blk}"), w(f"down{blk}"), name=name, comm=net.carry(name))
        net.give(f"gate{blk}", gg)
        net.give(f"up{blk}", gu)
        net.give(f"down{blk}", gd)
        return _norm_bwd(x, ffn_norm[blk], dh, [dout], name=f"ffn_norm_bwd{blk}", comm=net.carry(f"ffn_norm_bwd{blk}"))

    by_rows = lambda g: g.reshape(N_DEV, g.shape[0] // N_DEV, g.shape[1])
    dx5, d_norm[1][2] = ffn_b(x5, dx6, 3)
    d_att = _mm(dx5, w("wo"), dims="nt", name="attn_out_proj_dx", comm=net.carry("attn_out_proj_dx"))
    net.give("w_o", by_rows(_mm(att, dx5, dims="tn", out_dtype=BF16, name="attn_out_proj_dw")))
    d_bo = _colsum(dx5, name="attn_bo_grad")
    dq_rot, dk_rot, dv, d_sinks = _attn_bwd(q_rot, k_rot, v, w("sinks"), att, lse, d_att, name="attn_bwd", comm=net.carry("attn_bwd"))
    dq = _rope(dq_rot, cos2, -sin2, name="q_rope_bwd")
    dk = _rope(dk_rot, cos2, -sin2, name="k_rope_bwd")
    dh4 = _mm(dq, w("wq"), dims="nt", name="q_proj_dx")
    net.give("w_q", by_rows(_mm(h4, dq, dims="tn", out_dtype=BF16, name="q_proj_dw")))
    d_bq = _colsum(dq, name="attn_bq_grad")
    dx4, d_norm[1][1] = _norm_bwd(x4, nw[1][1], dh4, [dx5], name="attn_norm_bwd")
    dx3a, d_norm[1][0] = ffn_b(x3, dx4, 2)
    dhk = _mm(dk, w("wk"), dims="nt", name="k_proj_dx", comm=net.carry("k_proj_dx"))
    dhk = _mm(dv, w("wv"), dims="nt", res=dhk, name="v_proj_dx")
    net.give("w_k", by_rows(_mm(hk, dk, dims="tn", out_dtype=BF16, name="k_proj_dw")))
    net.give("w_v", by_rows(_mm(hk, dv, dims="tn", out_dtype=BF16, name="v_proj_dw")))
    d_bk = _colsum(dk, name="bk_grad")
    d_bv = _colsum(dv, name="bv_grad")
    dx3, d_kvn = _norm_bwd(x3, w("kv_norm_w"), dhk, [dx3a], name="kv_norm_bwd")
    dx2, d_norm[0][2] = ffn_b(x2, dx3, 1)
    d_yn = _mm(dx2, w("wout"), dims="nt", name="ssm_out_proj_dx", comm=net.carry("ssm_out_proj_dx"))
    net.give("w_out", by_rows(_mm(yn, dx2, dims="tn", out_dtype=BF16, name="ssm_out_proj_dw")))
    dy_ssd, dzx, d_ssm_norm = _gate_norm_bwd(y_ssd, zx, w("ssm_norm_w"), d_yn, name="ssm_gate_norm_bwd")
    dxs, d_b, d_c, ddtg, dag, ddg = _ssd_bwd(xbc, dtg, ag, dg, states, dy_ssd, name="ssd_bwd", comm=net.carry("ssd_bwd"))
    dzx, d_conv_w, d_conv_b = _conv_bwd(zx, w("conv_w"), w("conv_b"), dxs, d_b, d_c, dzx, name="ssm_conv_bwd",
                                        comm=net.carry("ssm_conv_bwd"))
    ddtr, d_dt_bias, d_a_log = _dt_bwd(dtr, w("dt_bias"), w("a_log"), dt, _from_groups(ddtg), _from_groups(dag), name="ssm_dt_bwd")
    dh1 = _mm(dzx, w("w_in_t"), b_rows=(0, ZX_DIM), name="ssm_in_proj_dx")
    dh1 = _mm(ddtr, w("w_in_t"), b_rows=(ZX_DIM, SSM_HEADS), res=dh1, name="ssm_dt_proj_dx")
    g_zx = _mm(dzx, h1, dims="tn", out_dtype=BF16, name="ssm_in_proj_dw")
    g_dt = _mm(ddtr, h1, dims="tn", out_dtype=BF16, name="ssm_dt_proj_dw")
    net.give("w_in", jnp.concatenate([g_zx, g_dt], axis=0).reshape(N_DEV, IN_PROJ_SHARD, D_MODEL))
    dx1, d_norm[0][1] = _norm_bwd(x1, nw[0][1], dh1, [dx2], name="ssm_norm_bwd", comm=net.carry("ssm_norm_bwd"))
    dx0, d_norm[0][0] = ffn_b(x0, dx1, 0)

    small = {"norm_w": jnp.concatenate([d_norm[l][i] for l in range(2) for i in range(3)], axis=0),
             "ssm_conv_w": d_conv_w, "ssm_conv_b": d_conv_b, "ssm_dt_bias": d_dt_bias, "ssm_a_log": d_a_log,
             "ssm_d": ddg.reshape(1, SSM_HEADS), "ssm_norm_w": d_ssm_norm, "kv_norm_w": d_kvn,
             "b_k": d_bk, "b_v": d_bv, "attn_b_q": d_bq, "attn_sinks": d_sinks, "attn_b_o": d_bo, "final_norm_w": d_final}
    return loss, dx0, small


BLOCK_BYTES = 1 << 20


def _row_tile(rows, cols):
    for t in (512, 256, 128, 64, 32, 16):
        if rows % t == 0 and t * cols * 4 <= BLOCK_BYTES:
            return t
    return rows


def _cast_bf16(x, *, name):
    n_blk, rows, cols = x.shape
    tm = rows if rows * cols * 4 <= 2 * BLOCK_BYTES else _row_tile(rows, cols)
    spec = pl.BlockSpec((None, tm, cols), lambda b, i: (b, i, 0))

    def body(x_ref, o_ref):
        o_ref[...] = x_ref[...].astype(BF16)

    return pl.pallas_call(body, name=name, grid=(n_blk, rows // tm), in_specs=[spec], out_specs=spec,
                          out_shape=jax.ShapeDtypeStruct(x.shape, BF16), compiler_params=_params("parallel", "parallel"))(x)


def _pair_add(grad, theirs, *, name):
    n_slots, rows, cols = theirs.shape
    tm = rows if rows * cols * 4 <= 2 * BLOCK_BYTES else _row_tile(rows, cols)

    def body(g_ref, t_ref, o_ref):
        mine = jnp.where(lax.axis_index("c") == 0, g_ref[0].astype(F32), g_ref[1].astype(F32))
        o_ref[...] = (mine + t_ref[...].astype(F32)).astype(BF16)

    spec = pl.BlockSpec((None, tm, cols), lambda s, i: (s, i, 0))
    return pl.pallas_call(
        body, name=name, grid=(n_slots, rows // tm),
        in_specs=[pl.BlockSpec((None, 2, tm, cols), lambda s, i: (s, 0, i, 0)), spec], out_specs=spec,
        out_shape=jax.ShapeDtypeStruct(theirs.shape, BF16), compiler_params=_params("parallel", "parallel"),
    )(grad.reshape((n_slots, 2, rows, cols)), theirs)


def _adam_update(g, w, m, v):
    m = ADAM_B1 * m + (1.0 - ADAM_B1) * g
    v = ADAM_B2 * v + (1.0 - ADAM_B2) * (g * g)
    m_hat = m / (1.0 - ADAM_B1 ** ADAM_STEP)
    v_hat = v / (1.0 - ADAM_B2 ** ADAM_STEP)
    delta = -ADAM_LR * (m_hat / (jnp.sqrt(v_hat) + ADAM_EPS) + ADAM_WD * w)
    return delta, m, v


def _adamw(parts, w, m, v, first_blk, prev, *, name, comm=None):
    n_blk, rows, cols = w.shape
    tm = _row_tile(rows, cols)
    n_tiles = rows // tm
    spec = pl.BlockSpec((None, tm, cols), lambda b, i: (first_blk + b, i, 0))
    n_prev, n_here = len(prev), len(parts)
    n_parts = parts[0].shape[0]

    def part_spec(q):
        return pl.BlockSpec((n_parts, tm, cols), lambda b, i: (0, jnp.where(b < q, 0, jnp.where(b == q, i, n_tiles - 1)), 0))

    def body(*refs):
        p_refs = refs[:n_here]
        w_ref, m_ref, v_ref = refs[n_here:n_here + 3]
        g_ref, d_ref, nm_ref, nv_ref = refs[n_here + 3 + n_prev:]
        b = pl.program_id(0)
        g = None
        for s in range(n_parts):
            t = p_refs[0][s]
            for q in range(1, n_here):
                t = jnp.where(b == q, p_refs[q][s], t)
            g = t.astype(F32) if g is None else g + t.astype(F32)
        delta, nm, nv = _adam_update(g, w_ref[...], m_ref[...], v_ref[...])
        g_ref[...] = g
        d_ref[...] = delta
        nm_ref[...] = nm
        nv_ref[...] = nv

    return _call(
        body, name=name, grid=(n_here, n_tiles),
        in_specs=[part_spec(q) for q in range(n_here)] + [spec, spec, spec] + [pl.BlockSpec(memory_space=pl.ANY)] * n_prev,
        out_specs=[spec] * 4, out_shape=[jax.ShapeDtypeStruct((n_blk, rows, cols), F32)] * 4,
        aliases={n_here + 3 + q: q for q in range(n_prev)}, sem=("arbitrary", "arbitrary"),
        args=[*parts, w, m, v, *prev], comm=comm)


def _sum_parts(parts, *, name):
    def body(p_ref, o_ref):
        g = p_ref[0]
        for s in range(1, N_DEV):
            g = g + p_ref[s]
        o_ref[...] = g

    return pl.pallas_call(body, name=name, out_shape=jax.ShapeDtypeStruct(parts.shape[1:], F32), compiler_params=_params())(parts)


def _adamw_packed(g, w, m, v, *, name):
    def body(g_ref, w_ref, m_ref, v_ref, d_ref, nm_ref, nv_ref):
        delta, nm, nv = _adam_update(g_ref[...], w_ref[...], m_ref[...], v_ref[...])
        d_ref[...] = delta
        nm_ref[...] = nm
        nv_ref[...] = nv

    return pl.pallas_call(body, name=name, out_shape=[jax.ShapeDtypeStruct(g.shape, F32)] * 3, compiler_params=_params())(g, w, m, v)


SUBLANES = 8


def _pack(arrs):
    rows = []
    for a in arrs:
        flat = a.reshape(-1)
        pad = (-flat.shape[0]) % LANES
        rows.append(jnp.pad(flat, (0, pad)).reshape(-1, LANES))
    out = jnp.concatenate(rows, axis=0)
    return jnp.pad(out, ((0, (-out.shape[0]) % SUBLANES), (0, 0)))


def _unpack(packed, shapes):
    outs, r = [], 0
    for shp in shapes:
        n = math.prod(shp)
        nr = -(-n // LANES)
        outs.append(packed[r:r + nr].reshape(-1)[:n].reshape(shp))
        r += nr
    return outs


WEIGHT_NAMES = ("norm_w", "ffn_w_gate", "ffn_w_up", "ffn_w_down", "ssm_w_in", "ssm_conv_w", "ssm_conv_b", "ssm_dt_bias",
                "ssm_a_log", "ssm_d", "ssm_norm_w", "ssm_w_out", "kv_norm_w", "w_k", "b_k", "w_v", "b_v", "attn_w_q",
                "attn_b_q", "attn_sinks", "attn_w_o", "attn_b_o", "final_norm_w")
MATRIX_NAMES = ("ffn_w_gate", "ffn_w_up", "ffn_w_down", "ssm_w_in", "ssm_w_out", "w_k", "w_v", "attn_w_q", "attn_w_o")
VECTOR_NAMES = tuple(n for n in WEIGHT_NAMES if n not in MATRIX_NAMES)
SHARDED_VECTORS = ("norm_w", "ssm_conv_w", "ssm_conv_b", "ssm_norm_w")


GATHER_PLAN = {
    "gather_stage0": ("gate0", "up0", "down0", "vec"),
    "ffn_fwd0": ("w_in",),
    "ssm_in_proj": ("w_out", "gate1"),
    "ssm_conv_fwd": ("w_k", "w_v"),
    "ssd_fwd": ("up1", "down1", "gate2"),
    "ssm_out_proj": ("w_q", "w_o"),
    "ffn_fwd1": ("up2", "down2"),
    "ffn_fwd2": ("gate3",),
    "attn_fwd": ("up3", "down3"),
}
PAIR_PLAN = {
    "attn_out_proj_dx": ("gate3", "up3", "down3"),
    "ffn_bwd2": ("w_q", "w_o"),
    "k_proj_dx": ("gate2", "up2", "down2"),
    "ssm_out_proj_dx": ("w_k", "w_v", "gate1", "up1", "down1"),
    "ssd_bwd": ("w_out",),
    "ssm_norm_bwd": ("w_in",),
    "ffn_norm_bwd0": ("gate0", "up0", "down0"),
}
CHIP_PLAN = {
    "attn_bwd": ("gate3", "up3", "down3"),
    "ffn_bwd1": ("gate2", "up2", "down2", "w_q", "w_o"),
    "ssd_bwd": ("gate1", "up1", "down1", "w_k", "w_v"),
    "ssm_conv_bwd": ("w_out",),
    "ffn_bwd0": ("w_in",),
    "adamw_gate": ("gate0",),
    "adamw_up": ("up0",),
    "adamw_down": ("down0",),
}
FFN_PARAMS = {"gate": "ffn_w_gate", "up": "ffn_w_up", "down": "ffn_w_down"}
SINGLE_MATRICES = {"w_in": "ssm_w_in", "w_out": "ssm_w_out", "w_k": "w_k", "w_v": "w_v", "w_q": "attn_w_q", "w_o": "attn_w_o"}


TRANSPOSED = ("ffn_w_gate", "ffn_w_up", "ssm_w_in")


def _matrix_view(name, a):
    if name in TRANSPOSED:
        a = jnp.swapaxes(a, -1, -2)
    return a.reshape((-1,) + a.shape[-2:])


def _from_matrix_view(name, a, shape):
    if name in TRANSPOSED:
        return jnp.swapaxes(a.reshape(shape[:-2] + (shape[-1], shape[-2])), -1, -2)
    return a.reshape(shape)


class _MeshNet:
    def __init__(self, p):
        self.p = p
        self.views = {n: _matrix_view(n, p[n]) for n in MATRIX_NAMES}
        self.local = {"vec": _pack([p[n] for n in SHARDED_VECTORS])}
        for short, n in FFN_PARAMS.items():
            cast = _cast_bf16(self.views[n], name=f"cast_{short}")
            self.local.update({f"{short}{k}": (cast, k) for k in range(N_FFN)})
        for short, n in SINGLE_MATRICES.items():
            self.local[short] = (_cast_bf16(self.views[n], name=f"cast_{short}"), 0)
        self.gathered_at, self.pairs_at, self.parts_at, self.grads, self.cache = {}, {}, {}, {}, {}

    def carry(self, name):
        comms = []
        if name in GATHER_PLAN:
            keys, comm = GATHER_PLAN[name], _Gather([self.local[k] for k in GATHER_PLAN[name]])
            self.gathered_at.update({k: (comm, i) for i, k in enumerate(keys)})
            comms.append(comm)
        if name in CHIP_PLAN:
            sums = []
            for k in CHIP_PLAN[name]:
                comm, i = self.pairs_at[k]
                sums.append(_pair_add(self.grads[k], comm.results[i], name=f"pair_add_{k}"))
            comm = _ChipExchange(sums)
            self.parts_at.update({k: (comm, i) for i, k in enumerate(CHIP_PLAN[name])})
            comms.append(comm)
        if name in PAIR_PLAN:
            keys, comm = PAIR_PLAN[name], _PairSwap([self.grads[k] for k in PAIR_PLAN[name]])
            self.pairs_at.update({k: (comm, i) for i, k in enumerate(keys)})
            comms.append(comm)
        return comms

    def run(self, name):
        for comm in self.carry(name):
            _run_exchange(comm, name=name)

    def give(self, key, grad):
        self.grads[key] = grad

    def parts(self, key):
        comm, i = self.parts_at[key]
        return comm.results[i]

    def _gathered(self, key):
        comm, i = self.gathered_at[key]
        return comm.results[i]

    def _vec(self, r0, r1, lead):
        t = self._gathered("vec")[:, r0:r1, :].reshape(N_DEV, lead, -1)
        return t.transpose(1, 0, 2).reshape(lead, -1)

    def _derive(self, name):
        p = self.p
        if name[:-1] in FFN_PARAMS:
            return self._gathered(name)
        if name == "w_in_t":
            return self._gathered("w_in").reshape(N_DEV * IN_PROJ_SHARD, D_MODEL)
        by_rows = {"wout": "w_out", "wk": "w_k", "wv": "w_v", "wq": "w_q", "wo": "w_o"}
        if name in by_rows:
            g = self._gathered(by_rows[name])
            return g.reshape(N_DEV * g.shape[1], g.shape[2])
        vectors = {"norm_w": lambda: self._vec(0, 6, 6).reshape(2, 3, D_MODEL), "conv_w": lambda: self._vec(6, 18, CONV_WIDTH),
                   "conv_b": lambda: self._vec(18, 21, 1), "ssm_norm_w": lambda: self._vec(21, 23, 1)}
        if name in vectors:
            return vectors[name]()
        replicated = {"dt_bias": p["ssm_dt_bias"], "a_log": p["ssm_a_log"], "d_skip": p["ssm_d"], "kv_norm_w": p["kv_norm_w"][None],
                      "b_k": p["b_k"][None], "b_v": p["b_v"][None], "b_q": p["attn_b_q"], "sinks": p["attn_sinks"],
                      "b_o": p["attn_b_o"], "final_norm_w": p["final_norm_w"][None]}
        return replicated[name]

    def w(self, name):
        if name not in self.cache:
            self.cache[name] = self._derive(name)
        return self.cache[name]


def _step(x, target, p, m, v):
    pos = _slot(_position())
    net = _MeshNet(p)
    net.run("gather_stage0")
    loss, grad_x, small = _forward_backward(x, target, net)

    grads, deltas, new_m, new_v = {}, {}, {}, {}
    view = lambda d, n: _matrix_view(n, d[n])
    vec_gather = _Gather([_pack([small[n] for n in VECTOR_NAMES])])
    for short, n in SINGLE_MATRICES.items():
        outs = _adamw([net.parts(short)], net.views[n], view(m, n), view(v, n), 0, [], name=f"adamw_{short}",
                      comm=[vec_gather] if short == "w_in" else None)
        grads[n], deltas[n], new_m[n], new_v[n] = [_from_matrix_view(n, o, p[n].shape) for o in outs]
    ffn_outs = {}
    for short, n in FFN_PARAMS.items():
        ffn_outs[short] = _adamw([net.parts(f"{short}{k}") for k in range(1, N_FFN)], net.views[n], view(m, n), view(v, n), 1, [],
                                 name=f"adamw_{short}", comm=net.carry(f"adamw_{short}"))
    for short, n in FFN_PARAMS.items():
        outs = _adamw([net.parts(f"{short}0")], net.views[n], view(m, n), view(v, n), 0, ffn_outs[short], name=f"adamw_{short}0")
        grads[n], deltas[n], new_m[n], new_v[n] = [_from_matrix_view(n, o, p[n].shape) for o in outs]
    vec_sum = _sum_parts(vec_gather.results[0], name="sum_vector_grads")
    full_shapes = {"norm_w": (2, 3, D_MODEL), "ssm_conv_w": (1, CONV_WIDTH, CONV_DIM), "ssm_conv_b": (1, CONV_DIM),
                   "ssm_norm_w": (1, D_INNER)}
    vec_full = dict(zip(VECTOR_NAMES, _unpack(vec_sum, [full_shapes.get(n, p[n].shape) for n in VECTOR_NAMES])))
    for n in VECTOR_NAMES:
        g = vec_full[n]
        if n in SHARDED_VECTORS:
            per = p[n].shape[-1]
            g = lax.dynamic_slice_in_dim(g, pos * per, per, axis=g.ndim - 1)
        grads[n] = g
    packed = _adamw_packed(*[_pack([d[n] for n in VECTOR_NAMES]) for d in (grads, p, m, v)], name="adamw_vectors")
    shapes = [p[n].shape for n in VECTOR_NAMES]
    for d, pk in zip((deltas, new_m, new_v), packed):
        d.update(zip(VECTOR_NAMES, _unpack(pk, shapes)))
    return loss, grad_x, grads, deltas, new_m, new_v


def kernel(x, norm_w, ffn_w_gate, ffn_w_up, ffn_w_down, ssm_w_in, ssm_conv_w, ssm_conv_b, ssm_dt_bias, ssm_a_log, ssm_d, ssm_norm_w, ssm_w_out, kv_norm_w, w_k, b_k, w_v, b_v, attn_w_q, attn_b_q, attn_sinks, attn_w_o, attn_b_o, final_norm_w, loss_target, m_norm_w, m_ffn_w_gate, m_ffn_w_up, m_ffn_w_down, m_ssm_w_in, m_ssm_conv_w, m_ssm_conv_b, m_ssm_dt_bias, m_ssm_a_log, m_ssm_d, m_ssm_norm_w, m_ssm_w_out, m_kv_norm_w, m_w_k, m_b_k, m_w_v, m_b_v, m_attn_w_q, m_attn_b_q, m_attn_sinks, m_attn_w_o, m_attn_b_o, m_final_norm_w, v_norm_w, v_ffn_w_gate, v_ffn_w_up, v_ffn_w_down, v_ssm_w_in, v_ssm_conv_w, v_ssm_conv_b, v_ssm_dt_bias, v_ssm_a_log, v_ssm_d, v_ssm_norm_w, v_ssm_w_out, v_kv_norm_w, v_w_k, v_b_k, v_w_v, v_b_v, v_attn_w_q, v_attn_b_q, v_attn_sinks, v_attn_w_o, v_attn_b_o, v_final_norm_w):
    p = dict(zip(WEIGHT_NAMES, (norm_w, ffn_w_gate, ffn_w_up, ffn_w_down, ssm_w_in, ssm_conv_w, ssm_conv_b, ssm_dt_bias, ssm_a_log, ssm_d, ssm_norm_w, ssm_w_out, kv_norm_w, w_k, b_k, w_v, b_v, attn_w_q, attn_b_q, attn_sinks, attn_w_o, attn_b_o, final_norm_w)))
    m = dict(zip(WEIGHT_NAMES, (m_norm_w, m_ffn_w_gate, m_ffn_w_up, m_ffn_w_down, m_ssm_w_in, m_ssm_conv_w, m_ssm_conv_b, m_ssm_dt_bias, m_ssm_a_log, m_ssm_d, m_ssm_norm_w, m_ssm_w_out, m_kv_norm_w, m_w_k, m_b_k, m_w_v, m_b_v, m_attn_w_q, m_attn_b_q, m_attn_sinks, m_attn_w_o, m_attn_b_o, m_final_norm_w)))
    v = dict(zip(WEIGHT_NAMES, (v_norm_w, v_ffn_w_gate, v_ffn_w_up, v_ffn_w_down, v_ssm_w_in, v_ssm_conv_w, v_ssm_conv_b, v_ssm_dt_bias, v_ssm_a_log, v_ssm_d, v_ssm_norm_w, v_ssm_w_out, v_kv_norm_w, v_w_k, v_b_k, v_w_v, v_b_v, v_attn_w_q, v_attn_b_q, v_attn_sinks, v_attn_w_o, v_attn_b_o, v_final_norm_w)))
    loss, grad_x, grads, deltas, new_m, new_v = _step(x[0], loss_target[0], p, m, v)
    loss = lax.psum(loss[0, 0], ("x", "y", "c"))
    return (loss, grad_x[None], *[grads[n] for n in WEIGHT_NAMES], *[deltas[n] for n in WEIGHT_NAMES],
            *[new_m[n] for n in WEIGHT_NAMES], *[new_v[n] for n in WEIGHT_NAMES])
```

```python
import functools
import math

import jax
import jax.numpy as jnp
from jax import lax
from jax.experimental import pallas as pl
from jax.experimental.pallas import tpu as pltpu

F32 = jnp.float32
BF16 = jnp.bfloat16

N_DEV = 8
SEQ = 2048
D_MODEL = 1024
D_FF_SHARD = 352
N_FFN = 4
D_INNER = 2048
SSM_HEADS = 32
SSM_HEAD_DIM = 64
SSM_GROUPS = 4
HEADS_PER_GROUP = 8
SSM_STATE = 128
CHUNK = 128
N_CHUNKS = SEQ // CHUNK
GN = SSM_GROUPS * SSM_STATE
CONV_DIM = D_INNER + 2 * GN
CONV_WIDTH = 4
ZX_DIM = D_INNER + CONV_DIM
IN_PROJ_SHARD = 644
ATT_HEAD_DIM = 64
N_Q_HEADS = 16
N_KV_HEADS = 4
Q_PER_KV = 4
KV_DIM = N_KV_HEADS * ATT_HEAD_DIM
WINDOW = 128
ROPE_THETA = 10000.0
EPS = 1e-5
FFN_RES_WEIGHT = 0.5
ATT_SCALE = 1.0 / math.sqrt(ATT_HEAD_DIM)
NEG_BIG = -1e30

ADAM_LR = 0.001
ADAM_B1 = 0.9
ADAM_B2 = 0.999
ADAM_EPS = 1e-08
ADAM_WD = 0.01
ADAM_STEP = 10

VMEM_LIMIT_BYTES = 56 * 1024 * 1024

NN = (((1,), (0,)), ((), ()))
NT = (((1,), (1,)), ((), ()))
TN = (((0,), (0,)), ((), ()))
_DIMS = {"nn": NN, "nt": NT, "tn": TN}


def _params(*sem):
    return pltpu.CompilerParams(dimension_semantics=sem if sem else None, vmem_limit_bytes=VMEM_LIMIT_BYTES)


def _dot(a, b, dims=NN):
    return lax.dot_general(a.astype(BF16), b.astype(BF16), dims, preferred_element_type=F32)


def _dot_f32(a, b, dims=NN):
    return lax.dot_general(a, b, dims, precision=lax.Precision.HIGHEST, preferred_element_type=F32)


def _sigmoid(x):
    return 1.0 / (1.0 + jnp.exp(-x))


def _dsilu(x, s):
    return s * (1.0 + x * (1.0 - s))


def _rms(x):
    r = lax.rsqrt(jnp.mean(x * x, axis=-1, keepdims=True) + EPS)
    return x * r, r


def _sum_all(x):
    return jnp.sum(jnp.sum(x, axis=1, keepdims=True), axis=0, keepdims=True)


MESH = pl.DeviceIdType.MESH
N_PEERS = N_DEV - 1
N_CHIPS = N_DEV // 2


def _position():
    return lax.axis_index("x"), lax.axis_index("y"), lax.axis_index("c")


def _slot(p):
    return 4 * p[0] + 2 * p[1] + p[2]


class _Exchange:
    def __init__(self, arrays, out_shapes):
        n = len(arrays)
        self.arrays = list(arrays)
        self.out_shapes = out_shapes
        self.scratch = [pltpu.SemaphoreType.DMA((n, N_PEERS)), pltpu.SemaphoreType.DMA((n, N_PEERS)), pltpu.SemaphoreType.DMA((n,))]
        self.results = None


class _Gather(_Exchange):
    def __init__(self, pieces):
        pieces = [p if isinstance(p, tuple) else (p, None) for p in pieces]
        self.blocks = [k for _, k in pieces]
        shapes = [a.shape if k is None else a.shape[1:] for a, k in pieces]
        super().__init__([a for a, _ in pieces], [jax.ShapeDtypeStruct((N_DEV,) + s, a.dtype) for s, (a, _) in zip(shapes, pieces)])

    def _plan(self, ins, outs, sems):
        send_sems, recv_sems, local_sems = sems
        x, y, c = _position()
        me, sibling = (x, y, c), (x, y, 1 - c)
        chips = [(1 - x, y), (x, 1 - y), (1 - x, 1 - y)]
        n = len(ins)
        ins = [r if k is None else r.at[k] for r, k in zip(ins, self.blocks)]

        def copy(a, k, block, to, src=None):
            dst = outs[a].at[_slot(block)]
            return pltpu.make_async_remote_copy(src_ref=dst if src is None else src, dst_ref=dst, send_sem=send_sems.at[a, k],
                                                recv_sem=recv_sems.at[a, k], device_id=to, device_id_type=MESH)

        mine = [pltpu.make_async_copy(ins[a], outs[a].at[_slot(me)], local_sems.at[a]) for a in range(n)]
        first = []
        for a in range(n):
            first.append(copy(a, 0, me, sibling, src=ins[a]))
            first += [copy(a, 1 + j, me, (*chip, c), src=ins[a]) for j, chip in enumerate(chips)]
        return n, c, me, sibling, chips, copy, mine, first

    def start(self, ins, outs, sems):
        _, _, _, _, _, _, mine, first = self._plan(ins, outs, sems)
        for cp in mine + first:
            cp.start()

    def finish(self, ins, outs, sems):
        n, c, me, sibling, chips, copy, mine, first = self._plan(ins, outs, sems)
        passed = []
        for j, chip in enumerate(chips):
            for a in range(n):
                copy(a, 1 + j, (*chip, c), me).wait_recv()
                fwd = copy(a, 4 + j, (*chip, c), sibling)
                fwd.start()
                passed.append(fwd)
        for a in range(n):
            copy(a, 0, sibling, me).wait_recv()
            for j, chip in enumerate(chips):
                copy(a, 4 + j, (*chip, 1 - c), me).wait_recv()
        for cp in first + passed:
            cp.wait_send()
        for cp in mine:
            cp.wait()


class _PairSwap(_Exchange):
    def __init__(self, arrays):
        n = len(arrays)
        self.arrays = list(arrays)
        self.out_shapes = [jax.ShapeDtypeStruct((N_CHIPS,) + a.shape[1:], a.dtype) for a in arrays]
        self.scratch = [pltpu.SemaphoreType.DMA((n, N_CHIPS)), pltpu.SemaphoreType.DMA((n, N_CHIPS))]
        self.results = None

    def _plan(self, ins, outs, sems):
        send_sems, recv_sems = sems
        x, y, c = _position()
        return [pltpu.make_async_remote_copy(src_ref=ins[a].at[2 * q + 1 - c], dst_ref=outs[a].at[q], send_sem=send_sems.at[a, q],
                                             recv_sem=recv_sems.at[a, q], device_id=(x, y, 1 - c), device_id_type=MESH)
                for a in range(len(ins)) for q in range(N_CHIPS)]

    def start(self, ins, outs, sems):
        for cp in self._plan(ins, outs, sems):
            cp.start()

    def finish(self, ins, outs, sems):
        for cp in self._plan(ins, outs, sems):
            cp.wait()


class _ChipExchange(_Exchange):
    def __init__(self, arrays):
        n = len(arrays)
        self.arrays = list(arrays)
        self.out_shapes = [jax.ShapeDtypeStruct(a.shape, a.dtype) for a in arrays]
        self.scratch = [pltpu.SemaphoreType.DMA((n, 3)), pltpu.SemaphoreType.DMA((n, 3)), pltpu.SemaphoreType.DMA((n,))]
        self.results = None

    def _plan(self, ins, outs, sems):
        send_sems, recv_sems, local_sems = sems
        x, y, c = _position()
        here = 2 * x + y
        chips = [(1 - x, y), (x, 1 - y), (1 - x, 1 - y)]
        n = len(ins)

        def copy(a, k, src_slot, dst_slot):
            return pltpu.make_async_remote_copy(src_ref=ins[a].at[src_slot], dst_ref=outs[a].at[dst_slot], send_sem=send_sems.at[a, k],
                                                recv_sem=recv_sems.at[a, k], device_id=(*chips[k], c), device_id_type=MESH)

        there = [2 * qx + qy for qx, qy in chips]
        mine = [pltpu.make_async_copy(ins[a].at[here], outs[a].at[here], local_sems.at[a]) for a in range(n)]
        sends = [copy(a, k, there[k], here) for a in range(n) for k in range(3)]
        arrivals = lambda: [copy(a, k, here, there[k]) for a in range(n) for k in range(3)]
        return mine, sends, arrivals

    def start(self, ins, outs, sems):
        mine, sends, _ = self._plan(ins, outs, sems)
        for cp in mine + sends:
            cp.start()

    def finish(self, ins, outs, sems):
        mine, sends, arrivals = self._plan(ins, outs, sems)
        for cp in arrivals():
            cp.wait_recv()
        for cp in sends:
            cp.wait_send()
        for cp in mine:
            cp.wait()


def _call(body, *, name, grid, in_specs, out_specs, out_shape, args, scratch_shapes=(), sem=(), comm=(), aliases=None):
    single = not isinstance(out_shape, (list, tuple))
    out_shape = [out_shape] if single else list(out_shape)
    out_specs = [out_specs] if single else list(out_specs)
    comms = list(comm or ())
    n_in, n_out, n_scr = len(args), len(out_shape), len(scratch_shapes)
    params = pltpu.CompilerParams(dimension_semantics=tuple(sem) if sem else None, vmem_limit_bytes=VMEM_LIMIT_BYTES)
    if not comms:
        res = pl.pallas_call(body, name=name, grid=grid, in_specs=list(in_specs), out_specs=out_specs, out_shape=out_shape,
                             scratch_shapes=list(scratch_shapes), input_output_aliases=aliases or {}, compiler_params=params)(*args)
        return res[0] if single else res
    counts = [n_in] + [len(c.arrays) for c in comms] + [n_out] + [len(c.out_shapes) for c in comms] + [n_scr] + [len(c.scratch) for c in comms]
    nc = len(comms)

    def carried(*refs):
        pos, groups = 0, []
        for cnt in counts:
            groups.append(refs[pos:pos + cnt])
            pos += cnt
        ins, c_ins = groups[0], groups[1:1 + nc]
        outs, c_outs = groups[1 + nc], groups[2 + nc:2 + 2 * nc]
        scr, c_sems = groups[2 + 2 * nc], groups[3 + 2 * nc:]
        ids = [pl.program_id(d) for d in range(len(grid))]
        is_first = functools.reduce(jnp.logical_and, [i == 0 for i in ids])
        is_last = functools.reduce(jnp.logical_and, [i == g - 1 for i, g in zip(ids, grid)])

        @pl.when(is_first)
        def _():
            for q, c in enumerate(comms):
                c.start(c_ins[q], c_outs[q], c_sems[q])

        body(*ins, *outs, *scr)

        @pl.when(is_last)
        def _():
            for q, c in enumerate(comms):
                c.finish(c_ins[q], c_outs[q], c_sems[q])

    anyspec = pl.BlockSpec(memory_space=pl.ANY)
    c_arrays = [a for c in comms for a in c.arrays]
    c_shapes = [s for c in comms for s in c.out_shapes]
    res = pl.pallas_call(
        carried, name=name, grid=grid, in_specs=list(in_specs) + [anyspec] * len(c_arrays), out_specs=out_specs + [anyspec] * len(c_shapes),
        out_shape=out_shape + c_shapes, scratch_shapes=list(scratch_shapes) + [s for c in comms for s in c.scratch],
        input_output_aliases=aliases or {}, compiler_params=params)(*args, *c_arrays)
    pos = n_out
    for c in comms:
        c.results = list(res[pos:pos + len(c.out_shapes)])
        pos += len(c.out_shapes)
    return res[0] if single else list(res[:n_out])


def _run_exchange(comm, *, name):
    def body(*refs):
        n_ci, n_co = len(comm.arrays), len(comm.out_shapes)
        ins, outs, sems = refs[:n_ci], refs[n_ci:n_ci + n_co], refs[n_ci + n_co:]
        comm.start(ins, outs, sems)
        comm.finish(ins, outs, sems)

    anyspec = pl.BlockSpec(memory_space=pl.ANY)
    comm.results = list(pl.pallas_call(
        body, name=name, in_specs=[anyspec] * len(comm.arrays), out_specs=[anyspec] * len(comm.out_shapes),
        out_shape=list(comm.out_shapes), scratch_shapes=list(comm.scratch))(*comm.arrays))
    return comm.results


def _mm(a, b, *, dims="nn", bias=None, res=None, out_dtype=F32, name, tm=1024, tn=1024, tk=1024, comm=None, b_rows=None):
    if dims == "tn":
        k_dim, m_dim = a.shape
    else:
        m_dim, k_dim = a.shape
    row0, n_rows = b_rows if b_rows is not None else (0, b.shape[0])
    n_dim = n_rows if dims == "nt" else b.shape[1]
    assert dims == "nt" or n_rows == k_dim, (name, a.shape, b.shape, b_rows)
    tm, tn, tk = min(tm, m_dim), min(tn, n_dim), min(tk, k_dim)
    assert m_dim % tm == 0 and n_dim % tn == 0 and k_dim % tk == 0, (name, a.shape, b.shape)
    nk = k_dim // tk
    a_spec = pl.BlockSpec((tk, tm), lambda i, j, k: (k, i)) if dims == "tn" else pl.BlockSpec((tm, tk), lambda i, j, k: (i, k))
    if dims == "nt":
        assert row0 % tn == 0
        b_spec = pl.BlockSpec((tn, tk), lambda i, j, k: (row0 // tn + j, k))
    else:
        assert row0 % tk == 0
        b_spec = pl.BlockSpec((tk, tn), lambda i, j, k: (row0 // tk + k, j))
    in_specs, args = [a_spec, b_spec], [a, b]
    if bias is not None:
        in_specs.append(pl.BlockSpec((1, tn), lambda i, j, k: (0, j)))
        args.append(bias)
    if res is not None:
        in_specs.append(pl.BlockSpec((tm, tn), lambda i, j, k: (i, j)))
        args.append(res)
    dn = _DIMS[dims]

    def body(*refs):
        a_ref, b_ref = refs[0], refs[1]
        o_ref, acc_ref = refs[-2], refs[-1]
        k = pl.program_id(2)

        @pl.when(k == 0)
        def _():
            acc_ref[...] = jnp.zeros_like(acc_ref)

        acc_ref[...] += _dot(a_ref[...], b_ref[...], dn)

        @pl.when(k == nk - 1)
        def _():
            r = acc_ref[...]
            pos = 2
            if bias is not None:
                r = r + refs[pos][...]
                pos += 1
            if res is not None:
                r = r + refs[pos][...]
            o_ref[...] = r.astype(out_dtype)

    return _call(
        body, name=name, grid=(m_dim // tm, n_dim // tn, nk), in_specs=in_specs,
        out_specs=pl.BlockSpec((tm, tn), lambda i, j, k: (i, j)),
        out_shape=jax.ShapeDtypeStruct((m_dim, n_dim), out_dtype),
        scratch_shapes=[pltpu.VMEM((tm, tn), F32)], sem=("parallel", "parallel", "arbitrary"), args=args, comm=comm)


def _norm_mm(x, nw, w, bias, *, name, tm=1024, tn=1024, comm=None, w_rows=None):
    t_dim, d_dim = x.shape
    transposed = w_rows is not None
    n_dim = w_rows if transposed else w.shape[1]
    tn = min(tn, n_dim)
    assert t_dim % tm == 0 and n_dim % tn == 0
    has_bias = bias is not None
    w_spec = pl.BlockSpec((tn, d_dim), lambda i, j: (j, 0)) if transposed else pl.BlockSpec((d_dim, tn), lambda i, j: (0, j))
    dn = NT if transposed else NN
    in_specs = [pl.BlockSpec((tm, d_dim), lambda i, j: (i, 0)), pl.BlockSpec((1, d_dim), lambda i, j: (0, 0)), w_spec]
    args = [x, nw, w]
    if has_bias:
        in_specs.append(pl.BlockSpec((1, tn), lambda i, j: (0, j)))
        args.append(bias)

    def body(*refs):
        x_ref, nw_ref, w_ref = refs[:3]
        o_ref, h_ref = refs[-2], refs[-1]

        @pl.when(pl.program_id(1) == 0)
        def _():
            xhat, _ = _rms(x_ref[...])
            h_ref[...] = (xhat * nw_ref[...]).astype(BF16)

        r = _dot(h_ref[...], w_ref[...], dn)
        if has_bias:
            r = r + refs[3][...]
        o_ref[...] = r

    return _call(
        body, name=name, grid=(t_dim // tm, n_dim // tn), in_specs=in_specs,
        out_specs=[pl.BlockSpec((tm, tn), lambda i, j: (i, j)), pl.BlockSpec((tm, d_dim), lambda i, j: (i, 0))],
        out_shape=[jax.ShapeDtypeStruct((t_dim, n_dim), F32), jax.ShapeDtypeStruct((t_dim, d_dim), BF16)],
        sem=("parallel", "arbitrary"), args=args, comm=comm)


def _norm_bwd(x, nw, dh, res, *, name, tm=256, comm=None):
    t_dim, d_dim = x.shape
    n_res = len(res)
    row = pl.BlockSpec((tm, d_dim), lambda i: (i, 0))
    vec = pl.BlockSpec((1, d_dim), lambda i: (0, 0))

    def body(*refs):
        x_ref, nw_ref, dh_ref = refs[:3]
        dx_ref, dnw_ref = refs[-2], refs[-1]
        xhat, r = _rms(x_ref[...])
        dh = dh_ref[...]
        dxhat = dh * nw_ref[...]
        dx = r * (dxhat - xhat * jnp.mean(dxhat * xhat, axis=-1, keepdims=True))
        for rr in refs[3:3 + n_res]:
            dx = dx + rr[...]
        dx_ref[...] = dx

        @pl.when(pl.program_id(0) == 0)
        def _():
            dnw_ref[...] = jnp.zeros_like(dnw_ref)

        dnw_ref[...] += jnp.sum(dh * xhat, axis=0, keepdims=True)

    return _call(
        body, name=name, grid=(t_dim // tm,), in_specs=[row, vec, row] + [row] * n_res,
        out_specs=[row, vec],
        out_shape=[jax.ShapeDtypeStruct((t_dim, d_dim), F32), jax.ShapeDtypeStruct((1, d_dim), F32)],
        sem=("arbitrary",), args=[x, nw, dh, *res], comm=comm)


def _colsum(x, *, name, tm=256):
    t_dim, n_dim = x.shape

    def body(x_ref, o_ref):
        @pl.when(pl.program_id(0) == 0)
        def _():
            o_ref[...] = jnp.zeros_like(o_ref)

        o_ref[...] += jnp.sum(x_ref[...], axis=0, keepdims=True)

    return pl.pallas_call(
        body, name=name, grid=(t_dim // tm,), in_specs=[pl.BlockSpec((tm, n_dim), lambda i: (i, 0))],
        out_specs=pl.BlockSpec((1, n_dim), lambda i: (0, 0)), out_shape=jax.ShapeDtypeStruct((1, n_dim), F32),
        compiler_params=_params("arbitrary"),
    )(x)


FFN_ROW_TILE = 512
FFN_SHARDS_PER_STEP = 2
FFN_STEPS = N_DEV // FFN_SHARDS_PER_STEP
FFN_STEP_COLS = FFN_SHARDS_PER_STEP * D_FF_SHARD


def _ffn_step_weights(*refs):
    return [jnp.concatenate([r[s] for s in range(FFN_SHARDS_PER_STEP)], axis=0) for r in refs]


def _ffn_spec(d_dim):
    return pl.BlockSpec((FFN_SHARDS_PER_STEP, D_FF_SHARD, d_dim), lambda j: (j, 0, 0))


def _ffn_fwd(x, nw, wg, wu, wd, *, name, comm=None):
    t_dim, d_dim = x.shape
    n_tiles = t_dim // FFN_ROW_TILE

    def body(x_ref, nw_ref, wg_ref, wu_ref, wd_ref, o_ref, h_scr):
        j = pl.program_id(0)

        @pl.when(j == 0)
        def _():
            xhat, _ = _rms(x_ref[...])
            h_scr[...] = (xhat * nw_ref[...]).astype(BF16)
            o_ref[...] = jnp.zeros_like(o_ref)

        w_gate, w_up, w_down = _ffn_step_weights(wg_ref, wu_ref, wd_ref)
        for t in range(n_tiles):
            rows = pl.ds(t * FFN_ROW_TILE, FFN_ROW_TILE)
            h = h_scr[rows, :]
            g = _dot(h, w_gate, NT)
            u = _dot(h, w_up, NT)
            act = g * _sigmoid(g) * u
            o_ref[rows, :] += _dot(act, w_down)

        @pl.when(j == FFN_STEPS - 1)
        def _():
            o_ref[...] = x_ref[...] + FFN_RES_WEIGHT * o_ref[...]

    full = pl.BlockSpec((t_dim, d_dim), lambda j: (0, 0))
    wspec = _ffn_spec(d_dim)
    return _call(
        body, name=name, grid=(FFN_STEPS,),
        in_specs=[full, pl.BlockSpec((1, d_dim), lambda j: (0, 0)), wspec, wspec, wspec],
        out_specs=full, out_shape=jax.ShapeDtypeStruct((t_dim, d_dim), F32),
        scratch_shapes=[pltpu.VMEM((t_dim, d_dim), BF16)],
        sem=("arbitrary",), args=[x, nw, wg, wu, wd], comm=comm)


def _ffn_bwd_prep(x, nw, dout, *, name, tm=256):
    t_dim, d_dim = x.shape
    row = pl.BlockSpec((tm, d_dim), lambda i: (i, 0))

    def body(x_ref, nw_ref, dout_ref, h_ref, dob_ref):
        xhat, _ = _rms(x_ref[...])
        h_ref[...] = (xhat * nw_ref[...]).astype(BF16)
        dob_ref[...] = (FFN_RES_WEIGHT * dout_ref[...]).astype(BF16)

    return pl.pallas_call(
        body, name=name, grid=(t_dim // tm,), in_specs=[row, pl.BlockSpec((1, d_dim), lambda i: (0, 0)), row],
        out_specs=[row, row], out_shape=[jax.ShapeDtypeStruct((t_dim, d_dim), BF16)] * 2,
        compiler_params=_params("parallel"),
    )(x, nw, dout)


def _ffn_bwd(h, dob, wg, wu, wd, *, name, comm=None):
    t_dim, d_dim = h.shape
    n_tiles = t_dim // FFN_ROW_TILE

    def body(h_ref, dob_ref, wg_ref, wu_ref, wd_ref, dh_ref, gg_ref, gu_ref, gd_ref, dwg_scr, dwu_scr, dwd_scr):
        j = pl.program_id(0)

        @pl.when(j == 0)
        def _():
            dh_ref[...] = jnp.zeros_like(dh_ref)

        w_gate, w_up, w_down = _ffn_step_weights(wg_ref, wu_ref, wd_ref)
        for t in range(n_tiles):
            rows = pl.ds(t * FFN_ROW_TILE, FFN_ROW_TILE)
            hh = h_ref[rows, :]
            do = dob_ref[rows, :]
            g = _dot(hh, w_gate, NT)
            u = _dot(hh, w_up, NT)
            sg = _sigmoid(g)
            s = g * sg
            da = _dot(do, w_down, NT)
            dwd = _dot(s * u, do, TN)
            du = (da * s).astype(BF16)
            dg = (da * u * _dsilu(g, sg)).astype(BF16)
            dwg = _dot(dg, hh, TN)
            dwu = _dot(du, hh, TN)
            if t == 0:
                dwd_scr[...] = dwd
                dwg_scr[...] = dwg
                dwu_scr[...] = dwu
            else:
                dwd_scr[...] += dwd
                dwg_scr[...] += dwg
                dwu_scr[...] += dwu
            dh_ref[rows, :] += _dot(dg, w_gate) + _dot(du, w_up)
        for s in range(FFN_SHARDS_PER_STEP):
            rows = slice(s * D_FF_SHARD, (s + 1) * D_FF_SHARD)
            gg_ref[s] = dwg_scr[rows, :].astype(BF16)
            gu_ref[s] = dwu_scr[rows, :].astype(BF16)
            gd_ref[s] = dwd_scr[rows, :].astype(BF16)

    full_bf = pl.BlockSpec((t_dim, d_dim), lambda j: (0, 0))
    wspec = _ffn_spec(d_dim)
    return _call(
        body, name=name, grid=(FFN_STEPS,),
        in_specs=[full_bf, full_bf, wspec, wspec, wspec], out_specs=[full_bf, wspec, wspec, wspec],
        out_shape=[jax.ShapeDtypeStruct((t_dim, d_dim), F32)] + [jax.ShapeDtypeStruct(wd.shape, BF16)] * 3,
        scratch_shapes=[pltpu.VMEM((FFN_STEP_COLS, d_dim), F32)] * 3,
        sem=("arbitrary",), args=[h, dob, wg, wu, wd], comm=comm)


CONV_COLS = 256


def _shift_down(u, s, rows):
    return jnp.where(rows >= s, pltpu.roll(u, s, 0), 0.0)


def _shift_up(u, s, rows, t_dim):
    return jnp.where(rows < t_dim - s, pltpu.roll(u, t_dim - s, 0), 0.0)


def _conv_pre(u, w_ref, b_ref, rows):
    c = b_ref[...] + w_ref[CONV_WIDTH - 1:CONV_WIDTH, :] * u
    for k in range(CONV_WIDTH - 1):
        c = c + w_ref[k:k + 1, :] * _shift_down(u, CONV_WIDTH - 1 - k, rows)
    return c


def _conv_fwd(zx, cw, cb, *, name, comm=None):
    t_dim = zx.shape[0]
    off = D_INNER // CONV_COLS

    def body(u_ref, w_ref, b_ref, o_ref):
        rows = lax.broadcasted_iota(jnp.int32, (t_dim, CONV_COLS), 0)
        c = _conv_pre(u_ref[...], w_ref, b_ref, rows)
        o_ref[...] = c * _sigmoid(c)

    return _call(
        body, name=name, grid=(CONV_DIM // CONV_COLS,),
        in_specs=[pl.BlockSpec((t_dim, CONV_COLS), lambda j: (0, off + j)),
                  pl.BlockSpec((CONV_WIDTH, CONV_COLS), lambda j: (0, j)), pl.BlockSpec((1, CONV_COLS), lambda j: (0, j))],
        out_specs=pl.BlockSpec((t_dim, CONV_COLS), lambda j: (0, j)),
        out_shape=jax.ShapeDtypeStruct((t_dim, CONV_DIM), F32), sem=("parallel",), args=[zx, cw, cb], comm=comm)


def _conv_bwd(zx, cw, cb, dxs, db, dc, dzx, *, name, comm=None):
    t_dim = zx.shape[0]
    off = D_INNER // CONV_COLS
    n_xs = D_INNER // CONV_COLS
    n_b = GN // CONV_COLS

    def body(u_ref, w_ref, b_ref, dxs_ref, db_ref, dc_ref, dzx_in, dzx_ref, dw_ref, dbias_ref):
        j = pl.program_id(0)
        rows = lax.broadcasted_iota(jnp.int32, (t_dim, CONV_COLS), 0)
        u = u_ref[...]
        c = _conv_pre(u, w_ref, b_ref, rows)
        d = jnp.where(j < n_xs, dxs_ref[...], jnp.where(j < n_xs + n_b, db_ref[...], dc_ref[...]))
        dcv = d * _dsilu(c, _sigmoid(c))
        dpre = w_ref[CONV_WIDTH - 1:CONV_WIDTH, :] * dcv
        dw_ref[CONV_WIDTH - 1:CONV_WIDTH, :] = jnp.sum(dcv * u, axis=0, keepdims=True)
        for k in range(CONV_WIDTH - 1):
            s = CONV_WIDTH - 1 - k
            dpre = dpre + w_ref[k:k + 1, :] * _shift_up(dcv, s, rows, t_dim)
            dw_ref[k:k + 1, :] = jnp.sum(dcv * _shift_down(u, s, rows), axis=0, keepdims=True)
        dzx_ref[...] = dpre
        dbias_ref[...] = jnp.sum(dcv, axis=0, keepdims=True)

    blk = lambda n: pl.BlockSpec((t_dim, CONV_COLS), n)
    return _call(
        body, name=name, grid=(CONV_DIM // CONV_COLS,),
        in_specs=[blk(lambda j: (0, off + j)), pl.BlockSpec((CONV_WIDTH, CONV_COLS), lambda j: (0, j)),
                  pl.BlockSpec((1, CONV_COLS), lambda j: (0, j)),
                  blk(lambda j: (0, jnp.minimum(j, n_xs - 1))),
                  blk(lambda j: (0, jnp.clip(j - n_xs, 0, n_b - 1))),
                  blk(lambda j: (0, jnp.clip(j - n_xs - n_b, 0, n_b - 1))),
                  pl.BlockSpec(memory_space=pl.ANY)],
        out_specs=[blk(lambda j: (0, off + j)), pl.BlockSpec((CONV_WIDTH, CONV_COLS), lambda j: (0, j)),
                   pl.BlockSpec((1, CONV_COLS), lambda j: (0, j))],
        out_shape=[jax.ShapeDtypeStruct(dzx.shape, F32), jax.ShapeDtypeStruct((CONV_WIDTH, CONV_DIM), F32),
                   jax.ShapeDtypeStruct((1, CONV_DIM), F32)],
        aliases={6: 0}, sem=("parallel",), args=[zx, cw, cb, dxs, db, dc, dzx], comm=comm)


def _softplus_parts(x):
    e = jnp.exp(-jnp.abs(x))
    u = 1.0 + e
    log1p_e = jnp.where(u == 1.0, e, jnp.log(u) * e / jnp.where(u == 1.0, 1.0, u - 1.0))
    return jnp.maximum(x, 0.0) + log1p_e


def _dt_prep(dtr, dt_bias, a_log, *, name):
    def body(dtr_ref, bias_ref, alog_ref, dt_ref, a_ref):
        dt = _softplus_parts(dtr_ref[...] + bias_ref[...])
        dt_ref[...] = dt
        a_ref[...] = dt * (-jnp.exp(alog_ref[...]))

    return pl.pallas_call(body, name=name, out_shape=[jax.ShapeDtypeStruct(dtr.shape, F32)] * 2,
                          compiler_params=_params())(dtr, dt_bias, a_log)


def _dt_bwd(dtr, dt_bias, a_log, dt, ddt, da, *, name):
    def body(dtr_ref, bias_ref, alog_ref, dt_ref, ddt_ref, da_ref, ddtr_ref, dbias_ref, dalog_ref):
        a_neg = -jnp.exp(alog_ref[...])
        da_v = da_ref[...]
        ddt_tot = ddt_ref[...] + da_v * a_neg
        ddtr = ddt_tot * _sigmoid(dtr_ref[...] + bias_ref[...])
        ddtr_ref[...] = ddtr
        dbias_ref[...] = jnp.sum(ddtr, axis=0, keepdims=True)
        dalog_ref[...] = jnp.sum(da_v * dt_ref[...], axis=0, keepdims=True) * a_neg

    return pl.pallas_call(
        body, name=name,
        out_shape=[jax.ShapeDtypeStruct(dtr.shape, F32), jax.ShapeDtypeStruct((1, SSM_HEADS), F32),
                   jax.ShapeDtypeStruct((1, SSM_HEADS), F32)],
        compiler_params=_params())(dtr, dt_bias, a_log, dt, ddt, da)


GROUP_COLS = HEADS_PER_GROUP * SSM_HEAD_DIM
LANES = 128
HEADS_PER_LANE_BLOCK = LANES // SSM_HEAD_DIM


def _split3(x):
    hi = x.astype(BF16)
    r1 = x - hi.astype(F32)
    mid = r1.astype(BF16)
    lo = (r1 - mid.astype(F32)).astype(BF16)
    return hi, mid, lo


def _group_sums(vals, expand):
    x = jnp.concatenate(vals, axis=0)
    out = None
    for part in _split3(x):
        t = lax.dot_general(part, expand, NT, preferred_element_type=F32)
        out = t if out is None else out + t
    return [out[i * CHUNK:(i + 1) * CHUNK] for i in range(len(vals))]


def _ssd_chunk_common(a_ref, dt_ref, b_ref, c_ref):
    row = lax.broadcasted_iota(jnp.int32, (CHUNK, CHUNK), 0)
    col = lax.broadcasted_iota(jnp.int32, (CHUNK, CHUNK), 1)
    causal = col <= row
    lower = causal.astype(F32)
    upper = (col >= row).astype(F32)
    head = lax.broadcasted_iota(jnp.int32, (HEADS_PER_GROUP, GROUP_COLS), 0)
    lane = lax.broadcasted_iota(jnp.int32, (HEADS_PER_GROUP, GROUP_COLS), 1)
    expand = ((lane >= head * SSM_HEAD_DIM) & (lane < (head + 1) * SSM_HEAD_DIM)).astype(F32)
    a = a_ref[...]
    cs = _dot_f32(lower, a)
    cs_row = _dot_f32(a, upper, TN)
    cs_x = _dot_f32(cs, expand)
    dt_x = _dot_f32(dt_ref[...], expand)
    e_out_x = jnp.exp(cs_x)
    e_st_x = jnp.exp(cs_x[CHUNK - 1:CHUNK, :] - cs_x)
    bc = b_ref[...]
    cc = c_ref[...]
    cb = _dot(cc, bc, NT)
    return causal, upper, expand.astype(BF16), cs, cs_row, dt_x, e_out_x, e_st_x, bc, cc, cb


def _head_decay(causal, cs, cs_row, h):
    return jnp.exp(jnp.where(causal, cs[:, h:h + 1] - cs_row[h:h + 1, :], NEG_BIG))


def _lane_block_head_masks():
    lane = lax.broadcasted_iota(jnp.int32, (CHUNK, LANES), 1)
    return [(lane >= i * SSM_HEAD_DIM) & (lane < (i + 1) * SSM_HEAD_DIM) for i in range(HEADS_PER_LANE_BLOCK)]


def _decay_state(dst_ref, old, new, cs):
    for h in range(HEADS_PER_GROUP):
        rows = slice(h * SSM_HEAD_DIM, (h + 1) * SSM_HEAD_DIM)
        dst_ref[rows, :] = jnp.exp(cs[CHUNK - 1:CHUNK, h:h + 1]) * old[rows, :] + new[rows, :]


def _ssd_fwd(xbc, dtg, ag, dgx, *, name, comm=None):
    t_dim = xbc.shape[0]

    def body(xs_ref, b_ref, c_ref, dt_ref, a_ref, d_ref, y_ref, st_ref, s_scr):
        @pl.when(pl.program_id(1) == 0)
        def _():
            s_scr[...] = jnp.zeros_like(s_scr)

        causal, _, _, cs, cs_row, dt_x, e_out_x, e_st_x, bc, cc, cb = _ssd_chunk_common(a_ref, dt_ref, b_ref, c_ref)
        masks = _lane_block_head_masks()
        xs = xs_ref[...]
        xdt_x = xs * dt_x
        prev = s_scr[...]
        st_ref[...] = prev
        y_off = e_out_x * _dot(cc, prev, NT) + xs * d_ref[...]
        for blk in range(GROUP_COLS // LANES):
            lanes = slice(blk * LANES, (blk + 1) * LANES)
            x_b = xdt_x[:, lanes].astype(BF16)
            acc = y_off[:, lanes]
            for i in range(HEADS_PER_LANE_BLOCK):
                m = cb * _head_decay(causal, cs, cs_row, blk * HEADS_PER_LANE_BLOCK + i)
                acc = acc + _dot(m, jnp.where(masks[i], x_b, jnp.zeros_like(x_b)))
            y_ref[:, lanes] = acc
        _decay_state(s_scr, prev, _dot(xdt_x * e_st_x, bc, TN), cs)

    xs = pl.BlockSpec((CHUNK, GROUP_COLS), lambda g, c: (c, g))
    bsp = pl.BlockSpec((CHUNK, SSM_STATE), lambda g, c: (c, D_INNER // SSM_STATE + g))
    csp = pl.BlockSpec((CHUNK, SSM_STATE), lambda g, c: (c, (D_INNER + GN) // SSM_STATE + g))
    per_head = pl.BlockSpec((None, CHUNK, HEADS_PER_GROUP), lambda g, c: (g, c, 0))
    dsk = pl.BlockSpec((None, 1, GROUP_COLS), lambda g, c: (g, 0, 0))
    return _call(
        body, name=name, grid=(SSM_GROUPS, N_CHUNKS),
        in_specs=[xs, bsp, csp, per_head, per_head, dsk],
        out_specs=[xs, pl.BlockSpec((None, GROUP_COLS, SSM_STATE), lambda g, c: (c, g, 0))],
        out_shape=[jax.ShapeDtypeStruct((t_dim, D_INNER), F32),
                   jax.ShapeDtypeStruct((N_CHUNKS, D_INNER, SSM_STATE), F32)],
        scratch_shapes=[pltpu.VMEM((GROUP_COLS, SSM_STATE), F32)],
        sem=("parallel", "arbitrary"), args=[xbc, xbc, xbc, dtg, ag, dgx], comm=comm)


def _ssd_bwd(xbc, dtg, ag, dgx, states, dy, *, name, comm=None):
    t_dim = xbc.shape[0]
    last = N_CHUNKS - 1

    def body(xs_ref, b_ref, c_ref, dt_ref, a_ref, d_ref, st_ref, dy_ref,
             dxs_ref, db_ref, dc_ref, ddt_ref, da_ref, dd_ref, ds_scr):
        @pl.when(pl.program_id(1) == 0)
        def _():
            ds_scr[...] = jnp.zeros_like(ds_scr)
            dd_ref[...] = jnp.zeros_like(dd_ref)

        causal, upper, expand, cs, cs_row, dt_x, e_out_x, e_st_x, bc, cc, cb = _ssd_chunk_common(a_ref, dt_ref, b_ref, c_ref)
        masks = _lane_block_head_masks()
        xs = xs_ref[...]
        dy_x = dy_ref[...]
        xdt_x = xs * dt_x
        prev = st_ref[...]
        d_s = ds_scr[...]
        g1_x = _dot(bc, d_s, NT)
        cp_x = _dot(cc, prev, NT)
        d_cb = jnp.zeros((CHUNK, CHUNK), F32)
        lane8 = lax.broadcasted_iota(jnp.int32, (CHUNK, HEADS_PER_GROUP), 1)
        sub8 = lax.broadcasted_iota(jnp.int32, (HEADS_PER_GROUP, CHUNK), 0)
        row_w = jnp.zeros((CHUNK, HEADS_PER_GROUP), F32)
        col_w = jnp.zeros((HEADS_PER_GROUP, CHUNK), F32)
        dxdt_blocks = []
        for blk in range(GROUP_COLS // LANES):
            lanes = slice(blk * LANES, (blk + 1) * LANES)
            dy_b = dy_x[:, lanes].astype(BF16)
            x_b = xdt_x[:, lanes].astype(BF16)
            acc_dx = jnp.zeros((CHUNK, LANES), F32)
            for i in range(HEADS_PER_LANE_BLOCK):
                h = blk * HEADS_PER_LANE_BLOCK + i
                decay = _head_decay(causal, cs, cs_row, h)
                m = cb * decay
                dy_h = jnp.where(masks[i], dy_b, jnp.zeros_like(dy_b))
                acc_dx = acc_dx + _dot(m, dy_h, TN)
                d_m = _dot(dy_h, x_b, NT)
                d_cb = d_cb + d_m * decay
                w = d_m * m
                row_w = jnp.where(lane8 == h, jnp.sum(w, axis=1, keepdims=True), row_w)
                col_w = jnp.where(sub8 == h, jnp.sum(w, axis=0, keepdims=True), col_w)
            dxdt_blocks.append(acc_dx)
        dxdt_x = jnp.concatenate(dxdt_blocks, axis=1) + e_st_x * g1_x
        dxs_ref[...] = dxdt_x * dt_x + dy_x * d_ref[...]
        dye = dy_x * e_out_x
        xde = xdt_x * e_st_x
        ddt, y_off, tl, dskip = _group_sums([dxdt_x * xs, dye * cp_x, xde * g1_x, dy_x * xs], expand)
        ddt_ref[...] = ddt
        dd_ref[...] += jnp.sum(dskip, axis=0, keepdims=True)
        sp = None
        for part in _split3(d_s * prev):
            t = lax.dot_general(expand, part, NN, preferred_element_type=F32)
            sp = t if sp is None else sp + t
        last_col = jnp.exp(cs_row[:, CHUNK - 1:CHUNK]) * jnp.sum(sp, axis=1, keepdims=True)
        eye = lax.broadcasted_iota(jnp.int32, (HEADS_PER_GROUP, HEADS_PER_GROUP), 0) == lax.broadcasted_iota(
            jnp.int32, (HEADS_PER_GROUP, HEADS_PER_GROUP), 1)
        last_row = jnp.sum(jnp.where(eye, last_col, 0.0), axis=0, keepdims=True) + jnp.sum(tl, axis=0, keepdims=True)
        is_last = lax.broadcasted_iota(jnp.int32, (CHUNK, 1), 0) == CHUNK - 1
        d_cs = row_w + y_off - tl + jnp.where(is_last, last_row, 0.0)
        da_ref[...] = _dot_f32(upper, d_cs) - _dot_f32(upper, col_w, NT)
        dc_ref[...] = _dot(d_cb, bc) + _dot(dye, prev)
        db_ref[...] = _dot(d_cb, cc, TN) + _dot(xde, d_s)
        _decay_state(ds_scr, d_s, _dot(dye, cc, TN), cs)

    rev = lambda c: last - c
    xs = pl.BlockSpec((CHUNK, GROUP_COLS), lambda g, c: (rev(c), g))
    bsp = pl.BlockSpec((CHUNK, SSM_STATE), lambda g, c: (rev(c), D_INNER // SSM_STATE + g))
    csp = pl.BlockSpec((CHUNK, SSM_STATE), lambda g, c: (rev(c), (D_INNER + GN) // SSM_STATE + g))
    per_head = pl.BlockSpec((None, CHUNK, HEADS_PER_GROUP), lambda g, c: (g, rev(c), 0))
    dsk = pl.BlockSpec((None, 1, GROUP_COLS), lambda g, c: (g, 0, 0))
    dsum = pl.BlockSpec((None, 1, HEADS_PER_GROUP), lambda g, c: (g, 0, 0))
    st = pl.BlockSpec((None, GROUP_COLS, SSM_STATE), lambda g, c: (rev(c), g, 0))
    grp = pl.BlockSpec((CHUNK, SSM_STATE), lambda g, c: (rev(c), g))
    return _call(
        body, name=name, grid=(SSM_GROUPS, N_CHUNKS),
        in_specs=[xs, bsp, csp, per_head, per_head, dsk, st, xs],
        out_specs=[xs, grp, grp, per_head, per_head, dsum],
        out_shape=[jax.ShapeDtypeStruct((t_dim, D_INNER), F32), jax.ShapeDtypeStruct((t_dim, GN), F32),
                   jax.ShapeDtypeStruct((t_dim, GN), F32),
                   jax.ShapeDtypeStruct((SSM_GROUPS, t_dim, HEADS_PER_GROUP), F32),
                   jax.ShapeDtypeStruct((SSM_GROUPS, t_dim, HEADS_PER_GROUP), F32),
                   jax.ShapeDtypeStruct((SSM_GROUPS, 1, HEADS_PER_GROUP), F32)],
        scratch_shapes=[pltpu.VMEM((GROUP_COLS, SSM_STATE), F32)],
        sem=("parallel", "arbitrary"), args=[xbc, xbc, xbc, dtg, ag, dgx, states, dy], comm=comm)


NORM_GROUP = D_INNER // SSM_GROUPS


def _gate_norm_fwd(y, zx, nw, *, name, tm=256):
    t_dim = y.shape[0]
    row = pl.BlockSpec((tm, D_INNER), lambda i: (i, 0))

    def body(y_ref, z_ref, nw_ref, o_ref):
        z = z_ref[...]
        yz = y_ref[...] * (z * _sigmoid(z))
        for g in range(SSM_GROUPS):
            cols = slice(g * NORM_GROUP, (g + 1) * NORM_GROUP)
            yhat, _ = _rms(yz[:, cols])
            o_ref[:, cols] = (yhat * nw_ref[:, cols]).astype(BF16)

    return pl.pallas_call(
        body, name=name, grid=(t_dim // tm,), in_specs=[row, row, pl.BlockSpec((1, D_INNER), lambda i: (0, 0))],
        out_specs=row, out_shape=jax.ShapeDtypeStruct((t_dim, D_INNER), BF16),
        compiler_params=_params("parallel"),
    )(y, zx, nw)


def _gate_norm_bwd(y, zx, nw, dyn, *, name, tm=256):
    t_dim = y.shape[0]
    row = pl.BlockSpec((tm, D_INNER), lambda i: (i, 0))
    vec = pl.BlockSpec((1, D_INNER), lambda i: (0, 0))

    def body(y_ref, z_ref, nw_ref, dyn_ref, dy_ref, dz_ref, dnw_ref):
        @pl.when(pl.program_id(0) == 0)
        def _():
            dnw_ref[...] = jnp.zeros_like(dnw_ref)

        z = z_ref[...]
        yv = y_ref[...]
        sg = _sigmoid(z)
        silu_z = z * sg
        yz = yv * silu_z
        dyn_v = dyn_ref[...]
        for g in range(SSM_GROUPS):
            cols = slice(g * NORM_GROUP, (g + 1) * NORM_GROUP)
            yhat, r = _rms(yz[:, cols])
            dn = dyn_v[:, cols]
            dnw_ref[:, cols] += jnp.sum(dn * yhat, axis=0, keepdims=True)
            dyhat = dn * nw_ref[:, cols]
            dyz = r * (dyhat - yhat * jnp.mean(dyhat * yhat, axis=-1, keepdims=True))
            dy_ref[:, cols] = dyz * silu_z[:, cols]
            dz_ref[:, cols] = dyz * yv[:, cols] * _dsilu(z[:, cols], sg[:, cols])

    return pl.pallas_call(
        body, name=name, grid=(t_dim // tm,), in_specs=[row, row, vec, row],
        out_specs=[row, row, vec],
        out_shape=[jax.ShapeDtypeStruct((t_dim, D_INNER), F32), jax.ShapeDtypeStruct((t_dim, ZX_DIM), F32),
                   jax.ShapeDtypeStruct((1, D_INNER), F32)],
        compiler_params=_params("arbitrary"),
    )(y, zx, nw, dyn)


def _rope(t, cos2, sin2, *, name, tm=256):
    t_dim, width = t.shape
    half = ATT_HEAD_DIM // 2
    reps = width // 128

    def body(t_ref, cos_ref, sin_ref, o_ref):
        x = t_ref[...]
        lane = lax.broadcasted_iota(jnp.int32, (tm, width), 1)
        first = (lane % ATT_HEAD_DIM) < half
        rot = jnp.where(first, -pltpu.roll(x, width - half, 1), pltpu.roll(x, half, 1))
        o_ref[...] = x * jnp.tile(cos_ref[...], (1, reps)) + rot * jnp.tile(sin_ref[...], (1, reps))

    row = pl.BlockSpec((tm, width), lambda i: (i, 0))
    tab = pl.BlockSpec((tm, 128), lambda i: (i, 0))
    return pl.pallas_call(
        body, name=name, grid=(t_dim // tm,), in_specs=[row, tab, tab], out_specs=row,
        out_shape=jax.ShapeDtypeStruct((t_dim, width), F32), compiler_params=_params("parallel"),
    )(t, cos2, sin2)


HEADS_PER_LANE_TILE = LANES // ATT_HEAD_DIM
STACKED_ROWS = Q_PER_KV * WINDOW


def _att_half_masks():
    lane = lax.broadcasted_iota(jnp.int32, (WINDOW, LANES), 1)
    return [(lane >= i * ATT_HEAD_DIM) & (lane < (i + 1) * ATT_HEAD_DIM) for i in range(HEADS_PER_LANE_TILE)]


def _att_stack_heads(ref, kvh, masks):
    parts = []
    for g in range(Q_PER_KV):
        h = kvh * Q_PER_KV + g
        blk = ref[:, (h // HEADS_PER_LANE_TILE) * LANES:(h // HEADS_PER_LANE_TILE + 1) * LANES]
        parts.append(jnp.where(masks[h % HEADS_PER_LANE_TILE], blk, jnp.zeros_like(blk)))
    return jnp.concatenate(parts, axis=0)


def _att_kv_tile(ref, kvh, masks):
    blk = ref[:, (kvh // HEADS_PER_LANE_TILE) * LANES:(kvh // HEADS_PER_LANE_TILE + 1) * LANES]
    return jnp.where(masks[kvh % HEADS_PER_LANE_TILE], blk, pltpu.roll(blk, ATT_HEAD_DIM, 1)).astype(BF16)


def _att_stacked_masks(n):
    row = lax.bitwise_and(lax.broadcasted_iota(jnp.int32, (STACKED_ROWS, WINDOW), 0), WINDOW - 1)
    col = lax.broadcasted_iota(jnp.int32, (STACKED_ROWS, WINDOW), 1)
    return col <= row, (col > row) & (n > 0)


def _att_stack_columns(ref, kvh, rows):
    cols = [ref[:, kvh * Q_PER_KV + g:kvh * Q_PER_KV + g + 1] for g in range(Q_PER_KV)]
    return jnp.concatenate([jnp.broadcast_to(c, (rows, 1)) for c in cols], axis=0)


def _att_scores(q4, k_tile, mask):
    return jnp.where(mask, _dot(q4, k_tile, NT) * ATT_SCALE, NEG_BIG)


def _att_unstack(x4, kvh, masks, tiles):
    for g in range(Q_PER_KV):
        h = kvh * Q_PER_KV + g
        piece = x4[g * WINDOW:(g + 1) * WINDOW]
        t = h // HEADS_PER_LANE_TILE
        tiles[t] = piece if h % HEADS_PER_LANE_TILE == 0 else jnp.where(masks[1], piece, tiles[t])


def _attn_fwd(q, k, v, sinks, *, name, comm=None):
    t_dim = q.shape[0]

    def body(q_ref, kc_ref, kp_ref, vc_ref, vp_ref, s_ref, o_ref, l_ref):
        n = pl.program_id(0)
        masks = _att_half_masks()
        mask_c, mask_p = _att_stacked_masks(n)
        out_tiles = [None] * (D_MODEL // LANES)
        for kvh in range(N_KV_HEADS):
            q4 = _att_stack_heads(q_ref, kvh, masks).astype(BF16)
            kc, kp = _att_kv_tile(kc_ref, kvh, masks), _att_kv_tile(kp_ref, kvh, masks)
            vc, vp = _att_kv_tile(vc_ref, kvh, masks), _att_kv_tile(vp_ref, kvh, masks)
            sc = _att_scores(q4, kc, mask_c)
            sp = _att_scores(q4, kp, mask_p)
            sink = _att_stack_columns(s_ref, kvh, WINDOW)
            m = jnp.maximum(jnp.maximum(jnp.max(sc, axis=1, keepdims=True), jnp.max(sp, axis=1, keepdims=True)), sink)
            pc = jnp.exp(sc - m)
            pp = jnp.exp(sp - m)
            den = jnp.sum(pc, axis=1, keepdims=True) + jnp.sum(pp, axis=1, keepdims=True) + jnp.exp(sink - m)
            _att_unstack((_dot(pc, vc) + _dot(pp, vp)) / den, kvh, masks, out_tiles)
            lse4 = m + jnp.log(den)
            for g in range(Q_PER_KV):
                h = kvh * Q_PER_KV + g
                l_ref[:, h:h + 1] = lse4[g * WINDOW:(g + 1) * WINDOW]
        for t, tile in enumerate(out_tiles):
            o_ref[:, t * LANES:(t + 1) * LANES] = tile

    cur = lambda w: pl.BlockSpec((WINDOW, w), lambda n: (n, 0))
    prv = lambda w: pl.BlockSpec((WINDOW, w), lambda n: (jnp.maximum(n - 1, 0), 0))
    return _call(
        body, name=name, grid=(t_dim // WINDOW,),
        in_specs=[cur(D_MODEL), cur(KV_DIM), prv(KV_DIM), cur(KV_DIM), prv(KV_DIM), pl.BlockSpec((1, N_Q_HEADS), lambda n: (0, 0))],
        out_specs=[cur(D_MODEL), cur(N_Q_HEADS)],
        out_shape=[jax.ShapeDtypeStruct((t_dim, D_MODEL), F32), jax.ShapeDtypeStruct((t_dim, N_Q_HEADS), F32)],
        sem=("parallel",), args=[q, k, k, v, v, sinks], comm=comm)


def _attn_bwd(q, k, v, sinks, o, lse, do, *, name, comm=None):
    t_dim = q.shape[0]

    def body(q_ref, kc_ref, kp_ref, vc_ref, vp_ref, s_ref, o_ref, l_ref, do_ref, dq_ref, dk_ref, dv_ref, dsink_ref):
        n = pl.program_id(0)

        @pl.when(n == 0)
        def _():
            dk_ref[...] = jnp.zeros_like(dk_ref)
            dv_ref[...] = jnp.zeros_like(dv_ref)
            dsink_ref[...] = jnp.zeros_like(dsink_ref)

        masks = _att_half_masks()
        mask_c, mask_p = _att_stacked_masks(n)
        lane_row = lax.broadcasted_iota(jnp.int32, (1, N_Q_HEADS), 1)
        rows_c = pl.ds(pl.multiple_of(n * WINDOW, WINDOW), WINDOW)
        rows_p = pl.ds(pl.multiple_of(jnp.maximum(n - 1, 0) * WINDOW, WINDOW), WINDOW)
        dsink = jnp.zeros((1, N_Q_HEADS), F32)
        dq_tiles = [None] * (D_MODEL // LANES)
        kv_tiles = KV_DIM // LANES
        dkc_tiles, dkp_tiles, dvc_tiles, dvp_tiles = ([None] * kv_tiles for _ in range(4))

        def place(tiles, kvh, x):
            folded = x + pltpu.roll(x, ATT_HEAD_DIM, 1)
            t = kvh // HEADS_PER_LANE_TILE
            tiles[t] = folded if kvh % HEADS_PER_LANE_TILE == 0 else jnp.where(masks[1], folded, tiles[t])

        for kvh in range(N_KV_HEADS):
            q4 = _att_stack_heads(q_ref, kvh, masks).astype(BF16)
            do4 = _att_stack_heads(do_ref, kvh, masks)
            o4 = _att_stack_heads(o_ref, kvh, masks)
            kc, kp = _att_kv_tile(kc_ref, kvh, masks), _att_kv_tile(kp_ref, kvh, masks)
            vc, vp = _att_kv_tile(vc_ref, kvh, masks), _att_kv_tile(vp_ref, kvh, masks)
            l4 = _att_stack_columns(l_ref, kvh, WINDOW)
            pc = jnp.exp(_att_scores(q4, kc, mask_c) - l4)
            pp = jnp.exp(_att_scores(q4, kp, mask_p) - l4)
            delta = jnp.sum(do4 * o4, axis=1, keepdims=True)
            do4b = do4.astype(BF16)
            dsc = pc * (_dot(do4b, vc, NT) - delta)
            dsp = pp * (_dot(do4b, vp, NT) - delta)
            _att_unstack((_dot(dsc, kc) + _dot(dsp, kp)) * ATT_SCALE, kvh, masks, dq_tiles)
            place(dkc_tiles, kvh, _dot(dsc, q4, TN) * ATT_SCALE)
            place(dkp_tiles, kvh, _dot(dsp, q4, TN) * ATT_SCALE)
            place(dvc_tiles, kvh, _dot(pc, do4b, TN))
            place(dvp_tiles, kvh, _dot(pp, do4b, TN))
            p_sink = jnp.exp(_att_stack_columns(s_ref, kvh, WINDOW) - l4) * delta
            for g in range(Q_PER_KV):
                h = kvh * Q_PER_KV + g
                dsink = jnp.where(lane_row == h, -jnp.sum(p_sink[g * WINDOW:(g + 1) * WINDOW], axis=0, keepdims=True), dsink)
        for t, tile in enumerate(dq_tiles):
            dq_ref[:, t * LANES:(t + 1) * LANES] = tile
        for t in range(kv_tiles):
            lanes = slice(t * LANES, (t + 1) * LANES)
            dk_ref[rows_c, lanes] += dkc_tiles[t]
            dk_ref[rows_p, lanes] += dkp_tiles[t]
            dv_ref[rows_c, lanes] += dvc_tiles[t]
            dv_ref[rows_p, lanes] += dvp_tiles[t]
        dsink_ref[...] += dsink

    cur = lambda w: pl.BlockSpec((WINDOW, w), lambda n: (n, 0))
    prv = lambda w: pl.BlockSpec((WINDOW, w), lambda n: (jnp.maximum(n - 1, 0), 0))
    whole = pl.BlockSpec((t_dim, KV_DIM), lambda n: (0, 0))
    svec = pl.BlockSpec((1, N_Q_HEADS), lambda n: (0, 0))
    return _call(
        body, name=name, grid=(t_dim // WINDOW,),
        in_specs=[cur(D_MODEL), cur(KV_DIM), prv(KV_DIM), cur(KV_DIM), prv(KV_DIM), svec, cur(D_MODEL), cur(N_Q_HEADS), cur(D_MODEL)],
        out_specs=[cur(D_MODEL), whole, whole, svec],
        out_shape=[jax.ShapeDtypeStruct((t_dim, D_MODEL), F32), jax.ShapeDtypeStruct((t_dim, KV_DIM), F32),
                   jax.ShapeDtypeStruct((t_dim, KV_DIM), F32), jax.ShapeDtypeStruct((1, N_Q_HEADS), F32)],
        sem=("arbitrary",), args=[q, k, k, v, v, sinks, o, lse, do], comm=comm)


def _loss_head(x, nw, target, *, name, tm=256):
    t_dim, d_dim = x.shape
    row = pl.BlockSpec((tm, d_dim), lambda i: (i, 0))
    vec = pl.BlockSpec((1, d_dim), lambda i: (0, 0))

    def body(x_ref, nw_ref, tgt_ref, loss_ref, dx_ref, dnw_ref):
        @pl.when(pl.program_id(0) == 0)
        def _():
            loss_ref[...] = jnp.zeros_like(loss_ref)
            dnw_ref[...] = jnp.zeros_like(dnw_ref)

        xhat, r = _rms(x_ref[...])
        err = xhat * nw_ref[...] - tgt_ref[...]
        loss_ref[...] += 0.5 * _sum_all(jnp.mean(err * err, axis=-1, keepdims=True))
        dy = err * (1.0 / d_dim)
        dnw_ref[...] += jnp.sum(dy * xhat, axis=0, keepdims=True)
        dxhat = dy * nw_ref[...]
        dx_ref[...] = r * (dxhat - xhat * jnp.mean(dxhat * xhat, axis=-1, keepdims=True))

    return pl.pallas_call(
        body, name=name, grid=(t_dim // tm,), in_specs=[row, vec, row],
        out_specs=[pl.BlockSpec((1, 1), lambda i: (0, 0)), row, vec],
        out_shape=[jax.ShapeDtypeStruct((1, 1), F32), jax.ShapeDtypeStruct((t_dim, d_dim), F32),
                   jax.ShapeDtypeStruct((1, d_dim), F32)],
        compiler_params=_params("arbitrary"),
    )(x, nw, target)


def _rope_tables():
    pos = jnp.arange(SEQ, dtype=F32)
    inv = 1.0 / (ROPE_THETA ** (jnp.arange(0, ATT_HEAD_DIM, 2, dtype=F32) / ATT_HEAD_DIM))
    ang = pos[:, None] * inv[None, :]
    cos, sin = jnp.cos(ang), jnp.sin(ang)
    return jnp.tile(cos, (1, 4)), jnp.tile(sin, (1, 4))


def _to_groups(t):
    return t.reshape(t.shape[0], SSM_GROUPS, HEADS_PER_GROUP).transpose(1, 0, 2)


def _from_groups(t):
    return t.transpose(1, 0, 2).reshape(t.shape[1], SSM_HEADS)


def _forward_backward(x0, target, net):
    w = net.w
    nw = [[w("norm_w")[l, i][None, :] for i in range(3)] for l in range(2)]
    cos2, sin2 = _rope_tables()
    ffn_norm = [nw[0][0], nw[0][2], nw[1][0], nw[1][2]]

    def ffn_f(x, blk):
        name = f"ffn_fwd{blk}"
        return _ffn_fwd(x, ffn_norm[blk], w(f"gate{blk}"), w(f"up{blk}"), w(f"down{blk}"), name=name, comm=net.carry(name))

    x1 = ffn_f(x0, 0)
    zx, h1 = _norm_mm(x1, nw[0][1], w("w_in_t"), None, w_rows=ZX_DIM, name="ssm_in_proj", comm=net.carry("ssm_in_proj"))
    dtr = _mm(h1, w("w_in_t"), dims="nt", b_rows=(ZX_DIM, SSM_HEADS), name="ssm_dt_proj")
    xbc = _conv_fwd(zx, w("conv_w"), w("conv_b"), name="ssm_conv_fwd", comm=net.carry("ssm_conv_fwd"))
    dt, a_dt = _dt_prep(dtr, w("dt_bias"), w("a_log"), name="ssm_dt_prep")
    dtg, ag = _to_groups(dt), _to_groups(a_dt)
    dg = jnp.repeat(w("d_skip").reshape(SSM_GROUPS, 1, HEADS_PER_GROUP), SSM_HEAD_DIM, axis=2)
    y_ssd, states = _ssd_fwd(xbc, dtg, ag, dg, name="ssd_fwd", comm=net.carry("ssd_fwd"))
    yn = _gate_norm_fwd(y_ssd, zx, w("ssm_norm_w"), name="ssm_gate_norm_fwd")
    x2 = _mm(yn, w("wout"), res=x1, name="ssm_out_proj", comm=net.carry("ssm_out_proj"))
    x3 = ffn_f(x2, 1)
    k_pre, hk = _norm_mm(x3, w("kv_norm_w"), w("wk"), w("b_k"), name="k_proj")
    v = _mm(hk, w("wv"), bias=w("b_v"), name="v_proj")
    k_rot = _rope(k_pre, cos2, sin2, name="k_rope")
    x4 = ffn_f(x3, 2)
    q_pre, h4 = _norm_mm(x4, nw[1][1], w("wq"), w("b_q"), name="q_proj")
    q_rot = _rope(q_pre, cos2, sin2, name="q_rope")
    att, lse = _attn_fwd(q_rot, k_rot, v, w("sinks"), name="attn_fwd", comm=net.carry("attn_fwd"))
    x5 = _mm(att, w("wo"), bias=w("b_o"), res=x4, name="attn_out_proj")
    x6 = ffn_f(x5, 3)
    loss, dx6, d_final = _loss_head(x6, w("final_norm_w"), target, name="loss_head")

    d_norm = [[None] * 3 for _ in range(2)]

    def ffn_b(x, dout, blk):
        h, dob = _ffn_bwd_prep(x, ffn_norm[blk], dout, name=f"ffn_bwd_prep{blk}")
        name = f"ffn_bwd{blk}"
        dh, gg, gu, gd = _ffn_bwd(h, dob, w(f"gate{blk}"), w(f"up{blk}"), w(f"down{blk}"), name=name, comm=net.carry(name))
        net.give(f"gate{blk}", gg)
        net.give(f"up{blk}", gu)
        net.give(f"down{blk}", gd)
        return _norm_bwd(x, ffn_norm[blk], dh, [dout], name=f"ffn_norm_bwd{blk}", comm=net.carry(f"ffn_norm_bwd{blk}"))

    by_rows = lambda g: g.reshape(N_DEV, g.shape[0] // N_DEV, g.shape[1])
    dx5, d_norm[1][2] = ffn_b(x5, dx6, 3)
    d_att = _mm(dx5, w("wo"), dims="nt", name="attn_out_proj_dx", comm=net.carry("attn_out_proj_dx"))
    net.give("w_o", by_rows(_mm(att, dx5, dims="tn", out_dtype=BF16, name="attn_out_proj_dw")))
    d_bo = _colsum(dx5, name="attn_bo_grad")
    dq_rot, dk_rot, dv, d_sinks = _attn_bwd(q_rot, k_rot, v, w("sinks"), att, lse, d_att, name="attn_bwd", comm=net.carry("attn_bwd"))
    dq = _rope(dq_rot, cos2, -sin2, name="q_rope_bwd")
    dk = _rope(dk_rot, cos2, -sin2, name="k_rope_bwd")
    dh4 = _mm(dq, w("wq"), dims="nt", name="q_proj_dx")
    net.give("w_q", by_rows(_mm(h4, dq, dims="tn", out_dtype=BF16, name="q_proj_dw")))
    d_bq = _colsum(dq, name="attn_bq_grad")
    dx4, d_norm[1][1] = _norm_bwd(x4, nw[1][1], dh4, [dx5], name="attn_norm_bwd")
    dx3a, d_norm[1][0] = ffn_b(x3, dx4, 2)
    dhk = _mm(dk, w("wk"), dims="nt", name="k_proj_dx", comm=net.carry("k_proj_dx"))
    dhk = _mm(dv, w("wv"), dims="nt", res=dhk, name="v_proj_dx")
    net.give("w_k", by_rows(_mm(hk, dk, dims="tn", out_dtype=BF16, name="k_proj_dw")))
    net.give("w_v", by_rows(_mm(hk, dv, dims="tn", out_dtype=BF16, name="v_proj_dw")))
    d_bk = _colsum(dk, name="bk_grad")
    d_bv = _colsum(dv, name="bv_grad")
    dx3, d_kvn = _norm_bwd(x3, w("kv_norm_w"), dhk, [dx3a], name="kv_norm_bwd")
    dx2, d_norm[0][2] = ffn_b(x2, dx3, 1)
    d_yn = _mm(dx2, w("wout"), dims="nt", name="ssm_out_proj_dx", comm=net.carry("ssm_out_proj_dx"))
    net.give("w_out", by_rows(_mm(yn, dx2, dims="tn", out_dtype=BF16, name="ssm_out_proj_dw")))
    dy_ssd, dzx, d_ssm_norm = _gate_norm_bwd(y_ssd, zx, w("ssm_norm_w"), d_yn, name="ssm_gate_norm_bwd")
    dxs, d_b, d_c, ddtg, dag, ddg = _ssd_bwd(xbc, dtg, ag, dg, states, dy_ssd, name="ssd_bwd", comm=net.carry("ssd_bwd"))
    dzx, d_conv_w, d_conv_b = _conv_bwd(zx, w("conv_w"), w("conv_b"), dxs, d_b, d_c, dzx, name="ssm_conv_bwd",
                                        comm=net.carry("ssm_conv_bwd"))
    ddtr, d_dt_bias, d_a_log = _dt_bwd(dtr, w("dt_bias"), w("a_log"), dt, _from_groups(ddtg), _from_groups(dag), name="ssm_dt_bwd")
    dh1 = _mm(dzx, w("w_in_t"), b_rows=(0, ZX_DIM), name="ssm_in_proj_dx")
    dh1 = _mm(ddtr, w("w_in_t"), b_rows=(ZX_DIM, SSM_HEADS), res=dh1, name="ssm_dt_proj_dx")
    g_zx = _mm(dzx, h1, dims="tn", out_dtype=BF16, name="ssm_in_proj_dw")
    g_dt = _mm(ddtr, h1, dims="tn", out_dtype=BF16, name="ssm_dt_proj_dw")
    net.give("w_in", jnp.concatenate([g_zx, g_dt], axis=0).reshape(N_DEV, IN_PROJ_SHARD, D_MODEL))
    dx1, d_norm[0][1] = _norm_bwd(x1, nw[0][1], dh1, [dx2], name="ssm_norm_bwd", comm=net.carry("ssm_norm_bwd"))
    dx0, d_norm[0][0] = ffn_b(x0, dx1, 0)

    small = {"norm_w": jnp.concatenate([d_norm[l][i] for l in range(2) for i in range(3)], axis=0),
             "ssm_conv_w": d_conv_w, "ssm_conv_b": d_conv_b, "ssm_dt_bias": d_dt_bias, "ssm_a_log": d_a_log,
             "ssm_d": ddg.reshape(1, SSM_HEADS), "ssm_norm_w": d_ssm_norm, "kv_norm_w": d_kvn,
             "b_k": d_bk, "b_v": d_bv, "attn_b_q": d_bq, "attn_sinks": d_sinks, "attn_b_o": d_bo, "final_norm_w": d_final}
    return loss, dx0, small


BLOCK_BYTES = 1 << 20


def _row_tile(rows, cols):
    for t in (512, 256, 128, 64, 32, 16):
        if rows % t == 0 and t * cols * 4 <= BLOCK_BYTES:
            return t
    return rows


def _cast_bf16(x, *, name):
    n_blk, rows, cols = x.shape
    tm = rows if rows * cols * 4 <= 2 * BLOCK_BYTES else _row_tile(rows, cols)
    spec = pl.BlockSpec((None, tm, cols), lambda b, i: (b, i, 0))

    def body(x_ref, o_ref):
        o_ref[...] = x_ref[...].astype(BF16)

    return pl.pallas_call(body, name=name, grid=(n_blk, rows // tm), in_specs=[spec], out_specs=spec,
                          out_shape=jax.ShapeDtypeStruct(x.shape, BF16), compiler_params=_params("parallel", "parallel"))(x)


def _pair_add(grad, theirs, *, name):
    n_slots, rows, cols = theirs.shape
    tm = rows if rows * cols * 4 <= 2 * BLOCK_BYTES else _row_tile(rows, cols)

    def body(g_ref, t_ref, o_ref):
        mine = jnp.where(lax.axis_index("c") == 0, g_ref[0].astype(F32), g_ref[1].astype(F32))
        o_ref[...] = (mine + t_ref[...].astype(F32)).astype(BF16)

    spec = pl.BlockSpec((None, tm, cols), lambda s, i: (s, i, 0))
    return pl.pallas_call(
        body, name=name, grid=(n_slots, rows // tm),
        in_specs=[pl.BlockSpec((None, 2, tm, cols), lambda s, i: (s, 0, i, 0)), spec], out_specs=spec,
        out_shape=jax.ShapeDtypeStruct(theirs.shape, BF16), compiler_params=_params("parallel", "parallel"),
    )(grad.reshape((n_slots, 2, rows, cols)), theirs)


def _adam_update(g, w, m, v):
    m = ADAM_B1 * m + (1.0 - ADAM_B1) * g
    v = ADAM_B2 * v + (1.0 - ADAM_B2) * (g * g)
    m_hat = m / (1.0 - ADAM_B1 ** ADAM_STEP)
    v_hat = v / (1.0 - ADAM_B2 ** ADAM_STEP)
    delta = -ADAM_LR * (m_hat / (jnp.sqrt(v_hat) + ADAM_EPS) + ADAM_WD * w)
    return delta, m, v


def _adamw(parts, w, m, v, first_blk, prev, *, name, comm=None):
    n_blk, rows, cols = w.shape
    tm = _row_tile(rows, cols)
    n_tiles = rows // tm
    spec = pl.BlockSpec((None, tm, cols), lambda b, i: (first_blk + b, i, 0))
    n_prev, n_here = len(prev), len(parts)
    n_parts = parts[0].shape[0]

    def part_spec(q):
        return pl.BlockSpec((n_parts, tm, cols), lambda b, i: (0, jnp.where(b < q, 0, jnp.where(b == q, i, n_tiles - 1)), 0))

    def body(*refs):
        p_refs = refs[:n_here]
        w_ref, m_ref, v_ref = refs[n_here:n_here + 3]
        g_ref, d_ref, nm_ref, nv_ref = refs[n_here + 3 + n_prev:]
        b = pl.program_id(0)
        g = None
        for s in range(n_parts):
            t = p_refs[0][s]
            for q in range(1, n_here):
                t = jnp.where(b == q, p_refs[q][s], t)
            g = t.astype(F32) if g is None else g + t.astype(F32)
        delta, nm, nv = _adam_update(g, w_ref[...], m_ref[...], v_ref[...])
        g_ref[...] = g
        d_ref[...] = delta
        nm_ref[...] = nm
        nv_ref[...] = nv

    return _call(
        body, name=name, grid=(n_here, n_tiles),
        in_specs=[part_spec(q) for q in range(n_here)] + [spec, spec, spec] + [pl.BlockSpec(memory_space=pl.ANY)] * n_prev,
        out_specs=[spec] * 4, out_shape=[jax.ShapeDtypeStruct((n_blk, rows, cols), F32)] * 4,
        aliases={n_here + 3 + q: q for q in range(n_prev)}, sem=("arbitrary", "arbitrary"),
        args=[*parts, w, m, v, *prev], comm=comm)


def _sum_parts(parts, *, name):
    def body(p_ref, o_ref):
        g = p_ref[0]
        for s in range(1, N_DEV):
            g = g + p_ref[s]
        o_ref[...] = g

    return pl.pallas_call(body, name=name, out_shape=jax.ShapeDtypeStruct(parts.shape[1:], F32), compiler_params=_params())(parts)


def _adamw_packed(g, w, m, v, *, name):
    def body(g_ref, w_ref, m_ref, v_ref, d_ref, nm_ref, nv_ref):
        delta, nm, nv = _adam_update(g_ref[...], w_ref[...], m_ref[...], v_ref[...])
        d_ref[...] = delta
        nm_ref[...] = nm
        nv_ref[...] = nv

    return pl.pallas_call(body, name=name, out_shape=[jax.ShapeDtypeStruct(g.shape, F32)] * 3, compiler_params=_params())(g, w, m, v)


SUBLANES = 8


def _pack(arrs):
    rows = []
    for a in arrs:
        flat = a.reshape(-1)
        pad = (-flat.shape[0]) % LANES
        rows.append(jnp.pad(flat, (0, pad)).reshape(-1, LANES))
    out = jnp.concatenate(rows, axis=0)
    return jnp.pad(out, ((0, (-out.shape[0]) % SUBLANES), (0, 0)))


def _unpack(packed, shapes):
    outs, r = [], 0
    for shp in shapes:
        n = math.prod(shp)
        nr = -(-n // LANES)
        outs.append(packed[r:r + nr].reshape(-1)[:n].reshape(shp))
        r += nr
    return outs


WEIGHT_NAMES = ("norm_w", "ffn_w_gate", "ffn_w_up", "ffn_w_down", "ssm_w_in", "ssm_conv_w", "ssm_conv_b", "ssm_dt_bias",
                "ssm_a_log", "ssm_d", "ssm_norm_w", "ssm_w_out", "kv_norm_w", "w_k", "b_k", "w_v", "b_v", "attn_w_q",
                "attn_b_q", "attn_sinks", "attn_w_o", "attn_b_o", "final_norm_w")
MATRIX_NAMES = ("ffn_w_gate", "ffn_w_up", "ffn_w_down", "ssm_w_in", "ssm_w_out", "w_k", "w_v", "attn_w_q", "attn_w_o")
VECTOR_NAMES = tuple(n for n in WEIGHT_NAMES if n not in MATRIX_NAMES)
SHARDED_VECTORS = ("norm_w", "ssm_conv_w", "ssm_conv_b", "ssm_norm_w")


GATHER_PLAN = {
    "gather_stage0": ("gate0", "up0", "down0", "vec"),
    "ffn_fwd0": ("w_in",),
    "ssm_in_proj": ("w_out", "gate1"),
    "ssm_conv_fwd": ("w_k", "w_v"),
    "ssd_fwd": ("up1", "down1", "gate2"),
    "ssm_out_proj": ("w_q", "w_o"),
    "ffn_fwd1": ("up2", "down2"),
    "ffn_fwd2": ("gate3",),
    "attn_fwd": ("up3", "down3"),
}
PAIR_PLAN = {
    "attn_out_proj_dx": ("gate3", "up3", "down3"),
    "ffn_bwd2": ("w_q", "w_o"),
    "k_proj_dx": ("gate2", "up2", "down2"),
    "ssm_out_proj_dx": ("w_k", "w_v", "gate1", "up1", "down1"),
    "ssd_bwd": ("w_out",),
    "ssm_norm_bwd": ("w_in",),
    "ffn_norm_bwd0": ("gate0", "up0", "down0"),
}
CHIP_PLAN = {
    "attn_bwd": ("gate3", "up3", "down3"),
    "ffn_bwd1": ("gate2", "up2", "down2", "w_q", "w_o"),
    "ssd_bwd": ("gate1", "up1", "down1", "w_k", "w_v"),
    "ssm_conv_bwd": ("w_out",),
    "ffn_bwd0": ("w_in",),
    "adamw_gate": ("gate0",),
    "adamw_up": ("up0",),
    "adamw_down": ("down0",),
}
FFN_PARAMS = {"gate": "ffn_w_gate", "up": "ffn_w_up", "down": "ffn_w_down"}
SINGLE_MATRICES = {"w_in": "ssm_w_in", "w_out": "ssm_w_out", "w_k": "w_k", "w_v": "w_v", "w_q": "attn_w_q", "w_o": "attn_w_o"}


TRANSPOSED = ("ffn_w_gate", "ffn_w_up", "ssm_w_in")


def _matrix_view(name, a):
    if name in TRANSPOSED:
        a = jnp.swapaxes(a, -1, -2)
    return a.reshape((-1,) + a.shape[-2:])


def _from_matrix_view(name, a, shape):
    if name in TRANSPOSED:
        return jnp.swapaxes(a.reshape(shape[:-2] + (shape[-1], shape[-2])), -1, -2)
    return a.reshape(shape)


class _MeshNet:
    def __init__(self, p):
        self.p = p
        self.views = {n: _matrix_view(n, p[n]) for n in MATRIX_NAMES}
        self.local = {"vec": _pack([p[n] for n in SHARDED_VECTORS])}
        for short, n in FFN_PARAMS.items():
            cast = _cast_bf16(self.views[n], name=f"cast_{short}")
            self.local.update({f"{short}{k}": (cast, k) for k in range(N_FFN)})
        for short, n in SINGLE_MATRICES.items():
            self.local[short] = (_cast_bf16(self.views[n], name=f"cast_{short}"), 0)
        self.gathered_at, self.pairs_at, self.parts_at, self.grads, self.cache = {}, {}, {}, {}, {}

    def carry(self, name):
        comms = []
        if name in GATHER_PLAN:
            keys, comm = GATHER_PLAN[name], _Gather([self.local[k] for k in GATHER_PLAN[name]])
            self.gathered_at.update({k: (comm, i) for i, k in enumerate(keys)})
            comms.append(comm)
        if name in CHIP_PLAN:
            sums = []
            for k in CHIP_PLAN[name]:
                comm, i = self.pairs_at[k]
                sums.append(_pair_add(self.grads[k], comm.results[i], name=f"pair_add_{k}"))
            comm = _ChipExchange(sums)
            self.parts_at.update({k: (comm, i) for i, k in enumerate(CHIP_PLAN[name])})
            comms.append(comm)
        if name in PAIR_PLAN:
            keys, comm = PAIR_PLAN[name], _PairSwap([self.grads[k] for k in PAIR_PLAN[name]])
            self.pairs_at.update({k: (comm, i) for i, k in enumerate(keys)})
            comms.append(comm)
        return comms

    def run(self, name):
        for comm in self.carry(name):
            _run_exchange(comm, name=name)

    def give(self, key, grad):
        self.grads[key] = grad

    def parts(self, key):
        comm, i = self.parts_at[key]
        return comm.results[i]

    def _gathered(self, key):
        comm, i = self.gathered_at[key]
        return comm.results[i]

    def _vec(self, r0, r1, lead):
        t = self._gathered("vec")[:, r0:r1, :].reshape(N_DEV, lead, -1)
        return t.transpose(1, 0, 2).reshape(lead, -1)

    def _derive(self, name):
        p = self.p
        if name[:-1] in FFN_PARAMS:
            return self._gathered(name)
        if name == "w_in_t":
            return self._gathered("w_in").reshape(N_DEV * IN_PROJ_SHARD, D_MODEL)
        by_rows = {"wout": "w_out", "wk": "w_k", "wv": "w_v", "wq": "w_q", "wo": "w_o"}
        if name in by_rows:
            g = self._gathered(by_rows[name])
            return g.reshape(N_DEV * g.shape[1], g.shape[2])
        vectors = {"norm_w": lambda: self._vec(0, 6, 6).reshape(2, 3, D_MODEL), "conv_w": lambda: self._vec(6, 18, CONV_WIDTH),
                   "conv_b": lambda: self._vec(18, 21, 1), "ssm_norm_w": lambda: self._vec(21, 23, 1)}
        if name in vectors:
            return vectors[name]()
        replicated = {"dt_bias": p["ssm_dt_bias"], "a_log": p["ssm_a_log"], "d_skip": p["ssm_d"], "kv_norm_w": p["kv_norm_w"][None],
                      "b_k": p["b_k"][None], "b_v": p["b_v"][None], "b_q": p["attn_b_q"], "sinks": p["attn_sinks"],
                      "b_o": p["attn_b_o"], "final_norm_w": p["final_norm_w"][None]}
        return replicated[name]

    def w(self, name):
        if name not in self.cache:
            self.cache[name] = self._derive(name)
        return self.cache[name]


def _step(x, target, p, m, v):
    pos = _slot(_position())
    net = _MeshNet(p)
    net.run("gather_stage0")
    loss, grad_x, small = _forward_backward(x, target, net)

    grads, deltas, new_m, new_v = {}, {}, {}, {}
    view = lambda d, n: _matrix_view(n, d[n])
    vec_gather = _Gather([_pack([small[n] for n in VECTOR_NAMES])])
    for short, n in SINGLE_MATRICES.items():
        outs = _adamw([net.parts(short)], net.views[n], view(m, n), view(v, n), 0, [], name=f"adamw_{short}",
                      comm=[vec_gather] if short == "w_in" else None)
        grads[n], deltas[n], new_m[n], new_v[n] = [_from_matrix_view(n, o, p[n].shape) for o in outs]
    ffn_outs = {}
    for short, n in FFN_PARAMS.items():
        ffn_outs[short] = _adamw([net.parts(f"{short}{k}") for k in range(1, N_FFN)], net.views[n], view(m, n), view(v, n), 1, [],
                                 name=f"adamw_{short}", comm=net.carry(f"adamw_{short}"))
    for short, n in FFN_PARAMS.items():
        outs = _adamw([net.parts(f"{short}0")], net.views[n], view(m, n), view(v, n), 0, ffn_outs[short], name=f"adamw_{short}0")
        grads[n], deltas[n], new_m[n], new_v[n] = [_from_matrix_view(n, o, p[n].shape) for o in outs]
    vec_sum = _sum_parts(vec_gather.results[0], name="sum_vector_grads")
    full_shapes = {"norm_w": (2, 3, D_MODEL), "ssm_conv_w": (1, CONV_WIDTH, CONV_DIM), "ssm_conv_b": (1, CONV_DIM),
                   "ssm_norm_w": (1, D_INNER)}
    vec_full = dict(zip(VECTOR_NAMES, _unpack(vec_sum, [full_shapes.get(n, p[n].shape) for n in VECTOR_NAMES])))
    for n in VECTOR_NAMES:
        g = vec_full[n]
        if n in SHARDED_VECTORS:
            per = p[n].shape[-1]
            g = lax.dynamic_slice_in_dim(g, pos * per, per, axis=g.ndim - 1)
        grads[n] = g
    packed = _adamw_packed(*[_pack([d[n] for n in VECTOR_NAMES]) for d in (grads, p, m, v)], name="adamw_vectors")
    shapes = [p[n].shape for n in VECTOR_NAMES]
    for d, pk in zip((deltas, new_m, new_v), packed):
        d.update(zip(VECTOR_NAMES, _unpack(pk, shapes)))
    return loss, grad_x, grads, deltas, new_m, new_v


def kernel(x, norm_w, ffn_w_gate, ffn_w_up, ffn_w_down, ssm_w_in, ssm_conv_w, ssm_conv_b, ssm_dt_bias, ssm_a_log, ssm_d, ssm_norm_w, ssm_w_out, kv_norm_w, w_k, b_k, w_v, b_v, attn_w_q, attn_b_q, attn_sinks, attn_w_o, attn_b_o, final_norm_w, loss_target, m_norm_w, m_ffn_w_gate, m_ffn_w_up, m_ffn_w_down, m_ssm_w_in, m_ssm_conv_w, m_ssm_conv_b, m_ssm_dt_bias, m_ssm_a_log, m_ssm_d, m_ssm_norm_w, m_ssm_w_out, m_kv_norm_w, m_w_k, m_b_k, m_w_v, m_b_v, m_attn_w_q, m_attn_b_q, m_attn_sinks, m_attn_w_o, m_attn_b_o, m_final_norm_w, v_norm_w, v_ffn_w_gate, v_ffn_w_up, v_ffn_w_down, v_ssm_w_in, v_ssm_conv_w, v_ssm_conv_b, v_ssm_dt_bias, v_ssm_a_log, v_ssm_d, v_ssm_norm_w, v_ssm_w_out, v_kv_norm_w, v_w_k, v_b_k, v_w_v, v_b_v, v_attn_w_q, v_attn_b_q, v_attn_sinks, v_attn_w_o, v_attn_b_o, v_final_norm_w):
    p = dict(zip(WEIGHT_NAMES, (norm_w, ffn_w_gate, ffn_w_up, ffn_w_down, ssm_w_in, ssm_conv_w, ssm_conv_b, ssm_dt_bias, ssm_a_log, ssm_d, ssm_norm_w, ssm_w_out, kv_norm_w, w_k, b_k, w_v, b_v, attn_w_q, attn_b_q, attn_sinks, attn_w_o, attn_b_o, final_norm_w)))
    m = dict(zip(WEIGHT_NAMES, (m_norm_w, m_ffn_w_gate, m_ffn_w_up, m_ffn_w_down, m_ssm_w_in, m_ssm_conv_w, m_ssm_conv_b, m_ssm_dt_bias, m_ssm_a_log, m_ssm_d, m_ssm_norm_w, m_ssm_w_out, m_kv_norm_w, m_w_k, m_b_k, m_w_v, m_b_v, m_attn_w_q, m_attn_b_q, m_attn_sinks, m_attn_w_o, m_attn_b_o, m_final_norm_w)))
    v = dict(zip(WEIGHT_NAMES, (v_norm_w, v_ffn_w_gate, v_ffn_w_up, v_ffn_w_down, v_ssm_w_in, v_ssm_conv_w, v_ssm_conv_b, v_ssm_dt_bias, v_ssm_a_log, v_ssm_d, v_ssm_norm_w, v_ssm_w_out, v_kv_norm_w, v_w_k, v_b_k, v_w_v, v_b_v, v_attn_w_q, v_attn_b_q, v_attn_sinks, v_attn_w_o, v_attn_b_o, v_final_norm_w)))
    loss, grad_x, grads, deltas, new_m, new_v = _step(x[0], loss_target[0], p, m, v)
    loss = lax.psum(loss[0, 0], ("x", "y", "c"))
    return (loss, grad_x[None], *[grads[n] for n in WEIGHT_NAMES], *[deltas[n] for n in WEIGHT_NAMES],
            *[new_m[n] for n in WEIGHT_NAMES], *[new_v[n] for n in WEIGHT_NAMES])
```

```python
import functools
import math

import jax
import jax.numpy as jnp
from jax import lax
from jax.experimental import pallas as pl
from jax.experimental.pallas import tpu as pltpu

F32 = jnp.float32
BF16 = jnp.bfloat16

N_DEV = 8
SEQ = 2048
D_MODEL = 1024
D_FF_SHARD = 352
N_FFN = 4
D_INNER = 2048
SSM_HEADS = 32
SSM_HEAD_DIM = 64
SSM_GROUPS = 4
HEADS_PER_GROUP = 8
SSM_STATE = 128
CHUNK = 128
N_CHUNKS = SEQ // CHUNK
GN = SSM_GROUPS * SSM_STATE
CONV_DIM = D_INNER + 2 * GN
CONV_WIDTH = 4
ZX_DIM = D_INNER + CONV_DIM
IN_PROJ_SHARD = 644
ATT_HEAD_DIM = 64
N_Q_HEADS = 16
N_KV_HEADS = 4
Q_PER_KV = 4
KV_DIM = N_KV_HEADS * ATT_HEAD_DIM
WINDOW = 128
ROPE_THETA = 10000.0
EPS = 1e-5
FFN_RES_WEIGHT = 0.5
ATT_SCALE = 1.0 / math.sqrt(ATT_HEAD_DIM)
NEG_BIG = -1e30

ADAM_LR = 0.001
ADAM_B1 = 0.9
ADAM_B2 = 0.999
ADAM_EPS = 1e-08
ADAM_WD = 0.01
ADAM_STEP = 10

VMEM_LIMIT_BYTES = 56 * 1024 * 1024
FFN_BWD_VMEM_LIMIT_BYTES = 61 * 1024 * 1024

NN = (((1,), (0,)), ((), ()))
NT = (((1,), (1,)), ((), ()))
TN = (((0,), (0,)), ((), ()))
_DIMS = {"nn": NN, "nt": NT, "tn": TN}


def _params(*sem):
    return pltpu.CompilerParams(dimension_semantics=sem if sem else None, vmem_limit_bytes=VMEM_LIMIT_BYTES)


def _dot(a, b, dims=NN):
    return lax.dot_general(a.astype(BF16), b.astype(BF16), dims, preferred_element_type=F32)


def _dot_f32(a, b, dims=NN):
    return lax.dot_general(a, b, dims, precision=lax.Precision.HIGHEST, preferred_element_type=F32)


def _sigmoid(x):
    return 1.0 / (1.0 + jnp.exp(-x))


def _dsilu(x, s):
    return s * (1.0 + x * (1.0 - s))


def _rms(x):
    r = lax.rsqrt(jnp.mean(x * x, axis=-1, keepdims=True) + EPS)
    return x * r, r


def _sum_all(x):
    return jnp.sum(jnp.sum(x, axis=1, keepdims=True), axis=0, keepdims=True)


MESH = pl.DeviceIdType.MESH
N_PEERS = N_DEV - 1
N_CHIPS = N_DEV // 2


def _position():
    return lax.axis_index("x"), lax.axis_index("y"), lax.axis_index("c")


def _slot(p):
    return 4 * p[0] + 2 * p[1] + p[2]


class _Exchange:
    def __init__(self, arrays, out_shapes):
        n = len(arrays)
        self.arrays = list(arrays)
        self.out_shapes = out_shapes
        self.scratch = [pltpu.SemaphoreType.DMA((n, N_PEERS)), pltpu.SemaphoreType.DMA((n, N_PEERS)), pltpu.SemaphoreType.DMA((n,))]
        self.results = None


class _Gather(_Exchange):
    def __init__(self, pieces):
        pieces = [p if isinstance(p, tuple) else (p, None) for p in pieces]
        self.blocks = [k for _, k in pieces]
        shapes = [a.shape if k is None else a.shape[1:] for a, k in pieces]
        super().__init__([a for a, _ in pieces], [jax.ShapeDtypeStruct((N_DEV,) + s, a.dtype) for s, (a, _) in zip(shapes, pieces)])

    def _plan(self, ins, outs, sems):
        send_sems, recv_sems, local_sems = sems
        x, y, c = _position()
        me, sibling = (x, y, c), (x, y, 1 - c)
        chips = [(1 - x, y), (x, 1 - y), (1 - x, 1 - y)]
        n = len(ins)
        ins = [r if k is None else r.at[k] for r, k in zip(ins, self.blocks)]

        def copy(a, k, block, to, src=None):
            dst = outs[a].at[_slot(block)]
            return pltpu.make_async_remote_copy(src_ref=dst if src is None else src, dst_ref=dst, send_sem=send_sems.at[a, k],
                                                recv_sem=recv_sems.at[a, k], device_id=to, device_id_type=MESH)

        mine = [pltpu.make_async_copy(ins[a], outs[a].at[_slot(me)], local_sems.at[a]) for a in range(n)]
        first = []
        for a in range(n):
            first.append(copy(a, 0, me, sibling, src=ins[a]))
            first += [copy(a, 1 + j, me, (*chip, c), src=ins[a]) for j, chip in enumerate(chips)]
        return n, c, me, sibling, chips, copy, mine, first

    def start(self, ins, outs, sems):
        _, _, _, _, _, _, mine, first = self._plan(ins, outs, sems)
        for cp in mine + first:
            cp.start()

    def finish(self, ins, outs, sems):
        n, c, me, sibling, chips, copy, mine, first = self._plan(ins, outs, sems)
        passed = []
        for j, chip in enumerate(chips):
            for a in range(n):
                copy(a, 1 + j, (*chip, c), me).wait_recv()
                fwd = copy(a, 4 + j, (*chip, c), sibling)
                fwd.start()
                passed.append(fwd)
        for a in range(n):
            copy(a, 0, sibling, me).wait_recv()
            for j, chip in enumerate(chips):
                copy(a, 4 + j, (*chip, 1 - c), me).wait_recv()
        for cp in first + passed:
            cp.wait_send()
        for cp in mine:
            cp.wait()


class _PairSwap(_Exchange):
    def __init__(self, arrays):
        n = len(arrays)
        self.arrays = list(arrays)
        self.out_shapes = [jax.ShapeDtypeStruct((N_CHIPS,) + a.shape[1:], a.dtype) for a in arrays]
        self.scratch = [pltpu.SemaphoreType.DMA((n, N_CHIPS)), pltpu.SemaphoreType.DMA((n, N_CHIPS))]
        self.results = None

    def _plan(self, ins, outs, sems):
        send_sems, recv_sems = sems
        x, y, c = _position()
        return [pltpu.make_async_remote_copy(src_ref=ins[a].at[2 * q + 1 - c], dst_ref=outs[a].at[q], send_sem=send_sems.at[a, q],
                                             recv_sem=recv_sems.at[a, q], device_id=(x, y, 1 - c), device_id_type=MESH)
                for a in range(len(ins)) for q in range(N_CHIPS)]

    def start(self, ins, outs, sems):
        for cp in self._plan(ins, outs, sems):
            cp.start()

    def finish(self, ins, outs, sems):
        for cp in self._plan(ins, outs, sems):
            cp.wait()


class _ChipExchange(_Exchange):
    def __init__(self, arrays):
        n = len(arrays)
        self.arrays = list(arrays)
        self.out_shapes = [jax.ShapeDtypeStruct(a.shape, a.dtype) for a in arrays]
        self.scratch = [pltpu.SemaphoreType.DMA((n, 3)), pltpu.SemaphoreType.DMA((n, 3)), pltpu.SemaphoreType.DMA((n,))]
        self.results = None

    def _plan(self, ins, outs, sems):
        send_sems, recv_sems, local_sems = sems
        x, y, c = _position()
        here = 2 * x + y
        chips = [(1 - x, y), (x, 1 - y), (1 - x, 1 - y)]
        n = len(ins)

        def copy(a, k, src_slot, dst_slot):
            return pltpu.make_async_remote_copy(src_ref=ins[a].at[src_slot], dst_ref=outs[a].at[dst_slot], send_sem=send_sems.at[a, k],
                                                recv_sem=recv_sems.at[a, k], device_id=(*chips[k], c), device_id_type=MESH)

        there = [2 * qx + qy for qx, qy in chips]
        mine = [pltpu.make_async_copy(ins[a].at[here], outs[a].at[here], local_sems.at[a]) for a in range(n)]
        sends = [copy(a, k, there[k], here) for a in range(n) for k in range(3)]
        arrivals = lambda: [copy(a, k, here, there[k]) for a in range(n) for k in range(3)]
        return mine, sends, arrivals

    def start(self, ins, outs, sems):
        mine, sends, _ = self._plan(ins, outs, sems)
        for cp in mine + sends:
            cp.start()

    def finish(self, ins, outs, sems):
        mine, sends, arrivals = self._plan(ins, outs, sems)
        for cp in arrivals():
            cp.wait_recv()
        for cp in sends:
            cp.wait_send()
        for cp in mine:
            cp.wait()


def _call(body, *, name, grid, in_specs, out_specs, out_shape, args, scratch_shapes=(), sem=(), comm=(), aliases=None,
          vmem_limit=VMEM_LIMIT_BYTES):
    single = not isinstance(out_shape, (list, tuple))
    out_shape = [out_shape] if single else list(out_shape)
    out_specs = [out_specs] if single else list(out_specs)
    comms = list(comm or ())
    n_in, n_out, n_scr = len(args), len(out_shape), len(scratch_shapes)
    params = pltpu.CompilerParams(dimension_semantics=tuple(sem) if sem else None, vmem_limit_bytes=vmem_limit)
    if not comms:
        res = pl.pallas_call(body, name=name, grid=grid, in_specs=list(in_specs), out_specs=out_specs, out_shape=out_shape,
                             scratch_shapes=list(scratch_shapes), input_output_aliases=aliases or {}, compiler_params=params)(*args)
        return res[0] if single else res
    counts = [n_in] + [len(c.arrays) for c in comms] + [n_out] + [len(c.out_shapes) for c in comms] + [n_scr] + [len(c.scratch) for c in comms]
    nc = len(comms)

    def carried(*refs):
        pos, groups = 0, []
        for cnt in counts:
            groups.append(refs[pos:pos + cnt])
            pos += cnt
        ins, c_ins = groups[0], groups[1:1 + nc]
        outs, c_outs = groups[1 + nc], groups[2 + nc:2 + 2 * nc]
        scr, c_sems = groups[2 + 2 * nc], groups[3 + 2 * nc:]
        ids = [pl.program_id(d) for d in range(len(grid))]
        is_first = functools.reduce(jnp.logical_and, [i == 0 for i in ids])
        is_last = functools.reduce(jnp.logical_and, [i == g - 1 for i, g in zip(ids, grid)])

        @pl.when(is_first)
        def _():
            for q, c in enumerate(comms):
                c.start(c_ins[q], c_outs[q], c_sems[q])

        body(*ins, *outs, *scr)

        @pl.when(is_last)
        def _():
            for q, c in enumerate(comms):
                c.finish(c_ins[q], c_outs[q], c_sems[q])

    anyspec = pl.BlockSpec(memory_space=pl.ANY)
    c_arrays = [a for c in comms for a in c.arrays]
    c_shapes = [s for c in comms for s in c.out_shapes]
    res = pl.pallas_call(
        carried, name=name, grid=grid, in_specs=list(in_specs) + [anyspec] * len(c_arrays), out_specs=out_specs + [anyspec] * len(c_shapes),
        out_shape=out_shape + c_shapes, scratch_shapes=list(scratch_shapes) + [s for c in comms for s in c.scratch],
        input_output_aliases=aliases or {}, compiler_params=params)(*args, *c_arrays)
    pos = n_out
    for c in comms:
        c.results = list(res[pos:pos + len(c.out_shapes)])
        pos += len(c.out_shapes)
    return res[0] if single else list(res[:n_out])


def _run_exchange(comm, *, name):
    def body(*refs):
        n_ci, n_co = len(comm.arrays), len(comm.out_shapes)
        ins, outs, sems = refs[:n_ci], refs[n_ci:n_ci + n_co], refs[n_ci + n_co:]
        comm.start(ins, outs, sems)
        comm.finish(ins, outs, sems)

    anyspec = pl.BlockSpec(memory_space=pl.ANY)
    comm.results = list(pl.pallas_call(
        body, name=name, in_specs=[anyspec] * len(comm.arrays), out_specs=[anyspec] * len(comm.out_shapes),
        out_shape=list(comm.out_shapes), scratch_shapes=list(comm.scratch))(*comm.arrays))
    return comm.results


def _mm(a, b, *, dims="nn", bias=None, res=None, out_dtype=F32, name, tm=1024, tn=1024, tk=1024, comm=None, b_rows=None):
    if dims == "tn":
        k_dim, m_dim = a.shape
    else:
        m_dim, k_dim = a.shape
    row0, n_rows = b_rows if b_rows is not None else (0, b.shape[0])
    n_dim = n_rows if dims == "nt" else b.shape[1]
    assert dims == "nt" or n_rows == k_dim, (name, a.shape, b.shape, b_rows)
    tm, tn, tk = min(tm, m_dim), min(tn, n_dim), min(tk, k_dim)
    assert m_dim % tm == 0 and n_dim % tn == 0 and k_dim % tk == 0, (name, a.shape, b.shape)
    nk = k_dim // tk
    a_spec = pl.BlockSpec((tk, tm), lambda i, j, k: (k, i)) if dims == "tn" else pl.BlockSpec((tm, tk), lambda i, j, k: (i, k))
    if dims == "nt":
        assert row0 % tn == 0
        b_spec = pl.BlockSpec((tn, tk), lambda i, j, k: (row0 // tn + j, k))
    else:
        assert row0 % tk == 0
        b_spec = pl.BlockSpec((tk, tn), lambda i, j, k: (row0 // tk + k, j))
    in_specs, args = [a_spec, b_spec], [a, b]
    if bias is not None:
        in_specs.append(pl.BlockSpec((1, tn), lambda i, j, k: (0, j)))
        args.append(bias)
    if res is not None:
        in_specs.append(pl.BlockSpec((tm, tn), lambda i, j, k: (i, j)))
        args.append(res)
    dn = _DIMS[dims]

    def body(*refs):
        a_ref, b_ref = refs[0], refs[1]
        o_ref, acc_ref = refs[-2], refs[-1]
        k = pl.program_id(2)

        @pl.when(k == 0)
        def _():
            acc_ref[...] = jnp.zeros_like(acc_ref)

        acc_ref[...] += _dot(a_ref[...], b_ref[...], dn)

        @pl.when(k == nk - 1)
        def _():
            r = acc_ref[...]
            pos = 2
            if bias is not None:
                r = r + refs[pos][...]
                pos += 1
            if res is not None:
                r = r + refs[pos][...]
            o_ref[...] = r.astype(out_dtype)

    return _call(
        body, name=name, grid=(m_dim // tm, n_dim // tn, nk), in_specs=in_specs,
        out_specs=pl.BlockSpec((tm, tn), lambda i, j, k: (i, j)),
        out_shape=jax.ShapeDtypeStruct((m_dim, n_dim), out_dtype),
        scratch_shapes=[pltpu.VMEM((tm, tn), F32)], sem=("parallel", "parallel", "arbitrary"), args=args, comm=comm)


def _norm_mm(x, nw, w, bias, *, name, tm=1024, tn=1024, comm=None, w_rows=None):
    t_dim, d_dim = x.shape
    transposed = w_rows is not None
    n_dim = w_rows if transposed else w.shape[1]
    tn = min(tn, n_dim)
    assert t_dim % tm == 0 and n_dim % tn == 0
    has_bias = bias is not None
    w_spec = pl.BlockSpec((tn, d_dim), lambda i, j: (j, 0)) if transposed else pl.BlockSpec((d_dim, tn), lambda i, j: (0, j))
    dn = NT if transposed else NN
    in_specs = [pl.BlockSpec((tm, d_dim), lambda i, j: (i, 0)), pl.BlockSpec((1, d_dim), lambda i, j: (0, 0)), w_spec]
    args = [x, nw, w]
    if has_bias:
        in_specs.append(pl.BlockSpec((1, tn), lambda i, j: (0, j)))
        args.append(bias)

    def body(*refs):
        x_ref, nw_ref, w_ref = refs[:3]
        o_ref, h_ref = refs[-2], refs[-1]

        @pl.when(pl.program_id(1) == 0)
        def _():
            xhat, _ = _rms(x_ref[...])
            h_ref[...] = (xhat * nw_ref[...]).astype(BF16)

        r = _dot(h_ref[...], w_ref[...], dn)
        if has_bias:
            r = r + refs[3][...]
        o_ref[...] = r

    return _call(
        body, name=name, grid=(t_dim // tm, n_dim // tn), in_specs=in_specs,
        out_specs=[pl.BlockSpec((tm, tn), lambda i, j: (i, j)), pl.BlockSpec((tm, d_dim), lambda i, j: (i, 0))],
        out_shape=[jax.ShapeDtypeStruct((t_dim, n_dim), F32), jax.ShapeDtypeStruct((t_dim, d_dim), BF16)],
        sem=("parallel", "arbitrary"), args=args, comm=comm)


def _norm_bwd(x, nw, dh, res, *, name, tm=256, comm=None):
    t_dim, d_dim = x.shape
    n_res = len(res)
    row = pl.BlockSpec((tm, d_dim), lambda i: (i, 0))
    vec = pl.BlockSpec((1, d_dim), lambda i: (0, 0))

    def body(*refs):
        x_ref, nw_ref, dh_ref = refs[:3]
        dx_ref, dnw_ref = refs[-2], refs[-1]
        xhat, r = _rms(x_ref[...])
        dh = dh_ref[...]
        dxhat = dh * nw_ref[...]
        dx = r * (dxhat - xhat * jnp.mean(dxhat * xhat, axis=-1, keepdims=True))
        for rr in refs[3:3 + n_res]:
            dx = dx + rr[...]
        dx_ref[...] = dx

        @pl.when(pl.program_id(0) == 0)
        def _():
            dnw_ref[...] = jnp.zeros_like(dnw_ref)

        dnw_ref[...] += jnp.sum(dh * xhat, axis=0, keepdims=True)

    return _call(
        body, name=name, grid=(t_dim // tm,), in_specs=[row, vec, row] + [row] * n_res,
        out_specs=[row, vec],
        out_shape=[jax.ShapeDtypeStruct((t_dim, d_dim), F32), jax.ShapeDtypeStruct((1, d_dim), F32)],
        sem=("arbitrary",), args=[x, nw, dh, *res], comm=comm)


def _colsum(x, *, name, tm=256):
    t_dim, n_dim = x.shape

    def body(x_ref, o_ref):
        @pl.when(pl.program_id(0) == 0)
        def _():
            o_ref[...] = jnp.zeros_like(o_ref)

        o_ref[...] += jnp.sum(x_ref[...], axis=0, keepdims=True)

    return pl.pallas_call(
        body, name=name, grid=(t_dim // tm,), in_specs=[pl.BlockSpec((tm, n_dim), lambda i: (i, 0))],
        out_specs=pl.BlockSpec((1, n_dim), lambda i: (0, 0)), out_shape=jax.ShapeDtypeStruct((1, n_dim), F32),
        compiler_params=_params("arbitrary"),
    )(x)


FFN_ROW_TILE = 512
FFN_SHARDS_PER_STEP = 2
FFN_STEPS = N_DEV // FFN_SHARDS_PER_STEP
FFN_STEP_COLS = FFN_SHARDS_PER_STEP * D_FF_SHARD


def _ffn_step_view(w):
    return w.reshape(FFN_STEPS, FFN_STEP_COLS, w.shape[-1])


def _ffn_specs(t_dim, d_dim):
    full = pl.BlockSpec((t_dim, d_dim), lambda j: (0, 0))
    wspec = pl.BlockSpec((None, FFN_STEP_COLS, d_dim), lambda j: (j, 0, 0))
    pre = pl.BlockSpec((None, t_dim, FFN_STEP_COLS), lambda j: (j, 0, 0))
    return full, wspec, pre


def _ffn_fwd(x, nw, wg, wu, wd, *, name, comm=None):
    t_dim, d_dim = x.shape
    n_tiles = t_dim // FFN_ROW_TILE

    def body(x_ref, nw_ref, wg_ref, wu_ref, wd_ref, o_ref, g_ref, u_ref, h_scr):
        j = pl.program_id(0)

        @pl.when(j == 0)
        def _():
            xhat, _ = _rms(x_ref[...])
            h_scr[...] = (xhat * nw_ref[...]).astype(BF16)
            o_ref[...] = jnp.zeros_like(o_ref)

        for t in range(n_tiles):
            rows = pl.ds(t * FFN_ROW_TILE, FFN_ROW_TILE)
            h = h_scr[rows, :]
            g = _dot(h, wg_ref[...], NT)
            u = _dot(h, wu_ref[...], NT)
            g_ref[rows, :] = g.astype(BF16)
            u_ref[rows, :] = u.astype(BF16)
            o_ref[rows, :] += _dot(g * _sigmoid(g) * u, wd_ref[...])

        @pl.when(j == FFN_STEPS - 1)
        def _():
            o_ref[...] = x_ref[...] + FFN_RES_WEIGHT * o_ref[...]

    full, wspec, pre = _ffn_specs(t_dim, d_dim)
    pre_shape = jax.ShapeDtypeStruct((FFN_STEPS, t_dim, FFN_STEP_COLS), BF16)
    return _call(
        body, name=name, grid=(FFN_STEPS,),
        in_specs=[full, pl.BlockSpec((1, d_dim), lambda j: (0, 0)), wspec, wspec, wspec],
        out_specs=[full, pre, pre], out_shape=[jax.ShapeDtypeStruct((t_dim, d_dim), F32), pre_shape, pre_shape],
        scratch_shapes=[pltpu.VMEM((t_dim, d_dim), BF16)],
        sem=("arbitrary",), args=[x, nw, _ffn_step_view(wg), _ffn_step_view(wu), _ffn_step_view(wd)], comm=comm)


def _ffn_bwd_prep(x, nw, dout, *, name, tm=256):
    t_dim, d_dim = x.shape
    row = pl.BlockSpec((tm, d_dim), lambda i: (i, 0))

    def body(x_ref, nw_ref, dout_ref, h_ref, dob_ref):
        xhat, _ = _rms(x_ref[...])
        h_ref[...] = (xhat * nw_ref[...]).astype(BF16)
        dob_ref[...] = (FFN_RES_WEIGHT * dout_ref[...]).astype(BF16)

    return pl.pallas_call(
        body, name=name, grid=(t_dim // tm,), in_specs=[row, pl.BlockSpec((1, d_dim), lambda i: (0, 0)), row],
        out_specs=[row, row], out_shape=[jax.ShapeDtypeStruct((t_dim, d_dim), BF16)] * 2,
        compiler_params=_params("parallel"),
    )(x, nw, dout)


def _ffn_bwd(h, dob, pre_g, pre_u, wg, wu, wd, *, name, comm=None):
    t_dim, d_dim = h.shape
    n_tiles = t_dim // FFN_ROW_TILE

    def body(h_ref, dob_ref, g_ref, u_ref, wg_ref, wu_ref, wd_ref, dh_ref, gg_ref, gu_ref, gd_ref, dwg_scr, dwu_scr, dwd_scr):
        @pl.when(pl.program_id(0) == 0)
        def _():
            dh_ref[...] = jnp.zeros_like(dh_ref)

        for t in range(n_tiles):
            rows = pl.ds(t * FFN_ROW_TILE, FFN_ROW_TILE)
            hh = h_ref[rows, :]
            do = dob_ref[rows, :]
            g = g_ref[rows, :].astype(F32)
            u = u_ref[rows, :].astype(F32)
            sg = _sigmoid(g)
            s = g * sg
            da = _dot(do, wd_ref[...], NT)
            dwd = _dot(s * u, do, TN)
            du = (da * s).astype(BF16)
            dg = (da * u * _dsilu(g, sg)).astype(BF16)
            dwg = _dot(dg, hh, TN)
            dwu = _dot(du, hh, TN)
            if t == 0:
                dwd_scr[...] = dwd
                dwg_scr[...] = dwg
                dwu_scr[...] = dwu
            else:
                dwd_scr[...] += dwd
                dwg_scr[...] += dwg
                dwu_scr[...] += dwu
            dh_ref[rows, :] += _dot(dg, wg_ref[...]) + _dot(du, wu_ref[...])
        gg_ref[...] = dwg_scr[...].astype(BF16)
        gu_ref[...] = dwu_scr[...].astype(BF16)
        gd_ref[...] = dwd_scr[...].astype(BF16)

    full, wspec, pre = _ffn_specs(t_dim, d_dim)
    gspec = pl.BlockSpec((None, FFN_STEP_COLS, d_dim), lambda j: (j, 0, 0), pipeline_mode=pl.Buffered(1))
    grad_shape = jax.ShapeDtypeStruct((FFN_STEPS, FFN_STEP_COLS, d_dim), BF16)
    dh, gg, gu, gd = _call(
        body, name=name, grid=(FFN_STEPS,),
        in_specs=[full, full, pre, pre, wspec, wspec, wspec], out_specs=[full, gspec, gspec, gspec],
        out_shape=[jax.ShapeDtypeStruct((t_dim, d_dim), F32)] + [grad_shape] * 3,
        scratch_shapes=[pltpu.VMEM((FFN_STEP_COLS, d_dim), F32)] * 3, sem=("arbitrary",), vmem_limit=FFN_BWD_VMEM_LIMIT_BYTES,
        args=[h, dob, pre_g, pre_u, _ffn_step_view(wg), _ffn_step_view(wu), _ffn_step_view(wd)], comm=comm)
    return dh, gg.reshape(wg.shape), gu.reshape(wu.shape), gd.reshape(wd.shape)


CONV_COLS = 256


def _shift_down(u, s, rows):
    return jnp.where(rows >= s, pltpu.roll(u, s, 0), 0.0)


def _shift_up(u, s, rows, t_dim):
    return jnp.where(rows < t_dim - s, pltpu.roll(u, t_dim - s, 0), 0.0)


def _conv_pre(u, w_ref, b_ref, rows):
    c = b_ref[...] + w_ref[CONV_WIDTH - 1:CONV_WIDTH, :] * u
    for k in range(CONV_WIDTH - 1):
        c = c + w_ref[k:k + 1, :] * _shift_down(u, CONV_WIDTH - 1 - k, rows)
    return c


def _conv_fwd(zx, cw, cb, *, name, comm=None):
    t_dim = zx.shape[0]
    off = D_INNER // CONV_COLS

    def body(u_ref, w_ref, b_ref, o_ref):
        rows = lax.broadcasted_iota(jnp.int32, (t_dim, CONV_COLS), 0)
        c = _conv_pre(u_ref[...], w_ref, b_ref, rows)
        o_ref[...] = c * _sigmoid(c)

    return _call(
        body, name=name, grid=(CONV_DIM // CONV_COLS,),
        in_specs=[pl.BlockSpec((t_dim, CONV_COLS), lambda j: (0, off + j)),
                  pl.BlockSpec((CONV_WIDTH, CONV_COLS), lambda j: (0, j)), pl.BlockSpec((1, CONV_COLS), lambda j: (0, j))],
        out_specs=pl.BlockSpec((t_dim, CONV_COLS), lambda j: (0, j)),
        out_shape=jax.ShapeDtypeStruct((t_dim, CONV_DIM), F32), sem=("parallel",), args=[zx, cw, cb], comm=comm)


def _conv_bwd(zx, cw, cb, dxs, db, dc, dzx, *, name, comm=None):
    t_dim = zx.shape[0]
    off = D_INNER // CONV_COLS
    n_xs = D_INNER // CONV_COLS
    n_b = GN // CONV_COLS

    def body(u_ref, w_ref, b_ref, dxs_ref, db_ref, dc_ref, dzx_in, dzx_ref, dw_ref, dbias_ref):
        j = pl.program_id(0)
        rows = lax.broadcasted_iota(jnp.int32, (t_dim, CONV_COLS), 0)
        u = u_ref[...]
        c = _conv_pre(u, w_ref, b_ref, rows)
        d = jnp.where(j < n_xs, dxs_ref[...], jnp.where(j < n_xs + n_b, db_ref[...], dc_ref[...]))
        dcv = d * _dsilu(c, _sigmoid(c))
        dpre = w_ref[CONV_WIDTH - 1:CONV_WIDTH, :] * dcv
        dw_ref[CONV_WIDTH - 1:CONV_WIDTH, :] = jnp.sum(dcv * u, axis=0, keepdims=True)
        for k in range(CONV_WIDTH - 1):
            s = CONV_WIDTH - 1 - k
            dpre = dpre + w_ref[k:k + 1, :] * _shift_up(dcv, s, rows, t_dim)
            dw_ref[k:k + 1, :] = jnp.sum(dcv * _shift_down(u, s, rows), axis=0, keepdims=True)
        dzx_ref[...] = dpre
        dbias_ref[...] = jnp.sum(dcv, axis=0, keepdims=True)

    blk = lambda n: pl.BlockSpec((t_dim, CONV_COLS), n)
    return _call(
        body, name=name, grid=(CONV_DIM // CONV_COLS,),
        in_specs=[blk(lambda j: (0, off + j)), pl.BlockSpec((CONV_WIDTH, CONV_COLS), lambda j: (0, j)),
                  pl.BlockSpec((1, CONV_COLS), lambda j: (0, j)),
                  blk(lambda j: (0, jnp.minimum(j, n_xs - 1))),
                  blk(lambda j: (0, jnp.clip(j - n_xs, 0, n_b - 1))),
                  blk(lambda j: (0, jnp.clip(j - n_xs - n_b, 0, n_b - 1))),
                  pl.BlockSpec(memory_space=pl.ANY)],
        out_specs=[blk(lambda j: (0, off + j)), pl.BlockSpec((CONV_WIDTH, CONV_COLS), lambda j: (0, j)),
                   pl.BlockSpec((1, CONV_COLS), lambda j: (0, j))],
        out_shape=[jax.ShapeDtypeStruct(dzx.shape, F32), jax.ShapeDtypeStruct((CONV_WIDTH, CONV_DIM), F32),
                   jax.ShapeDtypeStruct((1, CONV_DIM), F32)],
        aliases={6: 0}, sem=("parallel",), args=[zx, cw, cb, dxs, db, dc, dzx], comm=comm)


def _softplus_parts(x):
    e = jnp.exp(-jnp.abs(x))
    u = 1.0 + e
    log1p_e = jnp.where(u == 1.0, e, jnp.log(u) * e / jnp.where(u == 1.0, 1.0, u - 1.0))
    return jnp.maximum(x, 0.0) + log1p_e


def _dt_prep(dtr, dt_bias, a_log, *, name):
    def body(dtr_ref, bias_ref, alog_ref, dt_ref, a_ref):
        dt = _softplus_parts(dtr_ref[...] + bias_ref[...])
        dt_ref[...] = dt
        a_ref[...] = dt * (-jnp.exp(alog_ref[...]))

    return pl.pallas_call(body, name=name, out_shape=[jax.ShapeDtypeStruct(dtr.shape, F32)] * 2,
                          compiler_params=_params())(dtr, dt_bias, a_log)


def _dt_bwd(dtr, dt_bias, a_log, dt, ddt, da, *, name):
    def body(dtr_ref, bias_ref, alog_ref, dt_ref, ddt_ref, da_ref, ddtr_ref, dbias_ref, dalog_ref):
        a_neg = -jnp.exp(alog_ref[...])
        da_v = da_ref[...]
        ddt_tot = ddt_ref[...] + da_v * a_neg
        ddtr = ddt_tot * _sigmoid(dtr_ref[...] + bias_ref[...])
        ddtr_ref[...] = ddtr
        dbias_ref[...] = jnp.sum(ddtr, axis=0, keepdims=True)
        dalog_ref[...] = jnp.sum(da_v * dt_ref[...], axis=0, keepdims=True) * a_neg

    return pl.pallas_call(
        body, name=name,
        out_shape=[jax.ShapeDtypeStruct(dtr.shape, F32), jax.ShapeDtypeStruct((1, SSM_HEADS), F32),
                   jax.ShapeDtypeStruct((1, SSM_HEADS), F32)],
        compiler_params=_params())(dtr, dt_bias, a_log, dt, ddt, da)


GROUP_COLS = HEADS_PER_GROUP * SSM_HEAD_DIM
LANES = 128
HEADS_PER_LANE_BLOCK = LANES // SSM_HEAD_DIM


def _split3(x):
    hi = x.astype(BF16)
    r1 = x - hi.astype(F32)
    mid = r1.astype(BF16)
    lo = (r1 - mid.astype(F32)).astype(BF16)
    return hi, mid, lo


def _group_sums(vals, expand):
    x = jnp.concatenate(vals, axis=0)
    out = None
    for part in _split3(x):
        t = lax.dot_general(part, expand, NT, preferred_element_type=F32)
        out = t if out is None else out + t
    return [out[i * CHUNK:(i + 1) * CHUNK] for i in range(len(vals))]


def _ssd_chunk_common(a_ref, dt_ref, b_ref, c_ref):
    row = lax.broadcasted_iota(jnp.int32, (CHUNK, CHUNK), 0)
    col = lax.broadcasted_iota(jnp.int32, (CHUNK, CHUNK), 1)
    causal = col <= row
    lower = causal.astype(F32)
    upper = (col >= row).astype(F32)
    head = lax.broadcasted_iota(jnp.int32, (HEADS_PER_GROUP, GROUP_COLS), 0)
    lane = lax.broadcasted_iota(jnp.int32, (HEADS_PER_GROUP, GROUP_COLS), 1)
    expand = ((lane >= head * SSM_HEAD_DIM) & (lane < (head + 1) * SSM_HEAD_DIM)).astype(F32)
    a = a_ref[...]
    cs = _dot_f32(lower, a)
    cs_row = _dot_f32(a, upper, TN)
    cs_x = _dot_f32(cs, expand)
    dt_x = _dot_f32(dt_ref[...], expand)
    e_out_x = jnp.exp(cs_x)
    e_st_x = jnp.exp(cs_x[CHUNK - 1:CHUNK, :] - cs_x)
    bc = b_ref[...]
    cc = c_ref[...]
    cb = _dot(cc, bc, NT)
    return causal, upper, expand.astype(BF16), cs, cs_row, dt_x, e_out_x, e_st_x, bc, cc, cb


def _head_decay(causal, cs, cs_row, h):
    return jnp.exp(jnp.where(causal, cs[:, h:h + 1] - cs_row[h:h + 1, :], NEG_BIG))


def _lane_block_head_masks():
    lane = lax.broadcasted_iota(jnp.int32, (CHUNK, LANES), 1)
    return [(lane >= i * SSM_HEAD_DIM) & (lane < (i + 1) * SSM_HEAD_DIM) for i in range(HEADS_PER_LANE_BLOCK)]


def _decay_state(dst_ref, old, new, cs):
    for h in range(HEADS_PER_GROUP):
        rows = slice(h * SSM_HEAD_DIM, (h + 1) * SSM_HEAD_DIM)
        dst_ref[rows, :] = jnp.exp(cs[CHUNK - 1:CHUNK, h:h + 1]) * old[rows, :] + new[rows, :]


def _ssd_fwd(xbc, dtg, ag, dgx, *, name, comm=None):
    t_dim = xbc.shape[0]

    def body(xs_ref, b_ref, c_ref, dt_ref, a_ref, d_ref, y_ref, st_ref, s_scr):
        @pl.when(pl.program_id(1) == 0)
        def _():
            s_scr[...] = jnp.zeros_like(s_scr)

        causal, _, _, cs, cs_row, dt_x, e_out_x, e_st_x, bc, cc, cb = _ssd_chunk_common(a_ref, dt_ref, b_ref, c_ref)
        masks = _lane_block_head_masks()
        xs = xs_ref[...]
        xdt_x = xs * dt_x
        prev = s_scr[...]
        st_ref[...] = prev
        y_off = e_out_x * _dot(cc, prev, NT) + xs * d_ref[...]
        for blk in range(GROUP_COLS // LANES):
            lanes = slice(blk * LANES, (blk + 1) * LANES)
            x_b = xdt_x[:, lanes].astype(BF16)
            acc = y_off[:, lanes]
            for i in range(HEADS_PER_LANE_BLOCK):
                m = cb * _head_decay(causal, cs, cs_row, blk * HEADS_PER_LANE_BLOCK + i)
                acc = acc + _dot(m, jnp.where(masks[i], x_b, jnp.zeros_like(x_b)))
            y_ref[:, lanes] = acc
        _decay_state(s_scr, prev, _dot(xdt_x * e_st_x, bc, TN), cs)

    xs = pl.BlockSpec((CHUNK, GROUP_COLS), lambda g, c: (c, g))
    bsp = pl.BlockSpec((CHUNK, SSM_STATE), lambda g, c: (c, D_INNER // SSM_STATE + g))
    csp = pl.BlockSpec((CHUNK, SSM_STATE), lambda g, c: (c, (D_INNER + GN) // SSM_STATE + g))
    per_head = pl.BlockSpec((None, CHUNK, HEADS_PER_GROUP), lambda g, c: (g, c, 0))
    dsk = pl.BlockSpec((None, 1, GROUP_COLS), lambda g, c: (g, 0, 0))
    return _call(
        body, name=name, grid=(SSM_GROUPS, N_CHUNKS),
        in_specs=[xs, bsp, csp, per_head, per_head, dsk],
        out_specs=[xs, pl.BlockSpec((None, GROUP_COLS, SSM_STATE), lambda g, c: (c, g, 0))],
        out_shape=[jax.ShapeDtypeStruct((t_dim, D_INNER), F32),
                   jax.ShapeDtypeStruct((N_CHUNKS, D_INNER, SSM_STATE), F32)],
        scratch_shapes=[pltpu.VMEM((GROUP_COLS, SSM_STATE), F32)],
        sem=("parallel", "arbitrary"), args=[xbc, xbc, xbc, dtg, ag, dgx], comm=comm)


def _ssd_bwd(xbc, dtg, ag, dgx, states, dy, *, name, comm=None):
    t_dim = xbc.shape[0]
    last = N_CHUNKS - 1

    def body(xs_ref, b_ref, c_ref, dt_ref, a_ref, d_ref, st_ref, dy_ref,
             dxs_ref, db_ref, dc_ref, ddt_ref, da_ref, dd_ref, ds_scr):
        @pl.when(pl.program_id(1) == 0)
        def _():
            ds_scr[...] = jnp.zeros_like(ds_scr)
            dd_ref[...] = jnp.zeros_like(dd_ref)

        causal, upper, expand, cs, cs_row, dt_x, e_out_x, e_st_x, bc, cc, cb = _ssd_chunk_common(a_ref, dt_ref, b_ref, c_ref)
        masks = _lane_block_head_masks()
        xs = xs_ref[...]
        dy_x = dy_ref[...]
        xdt_x = xs * dt_x
        prev = st_ref[...]
        d_s = ds_scr[...]
        g1_x = _dot(bc, d_s, NT)
        cp_x = _dot(cc, prev, NT)
        d_cb = jnp.zeros((CHUNK, CHUNK), F32)
        lane8 = lax.broadcasted_iota(jnp.int32, (CHUNK, HEADS_PER_GROUP), 1)
        sub8 = lax.broadcasted_iota(jnp.int32, (HEADS_PER_GROUP, CHUNK), 0)
        row_w = jnp.zeros((CHUNK, HEADS_PER_GROUP), F32)
        col_w = jnp.zeros((HEADS_PER_GROUP, CHUNK), F32)
        dxdt_blocks = []
        for blk in range(GROUP_COLS // LANES):
            lanes = slice(blk * LANES, (blk + 1) * LANES)
            dy_b = dy_x[:, lanes].astype(BF16)
            x_b = xdt_x[:, lanes].astype(BF16)
            acc_dx = jnp.zeros((CHUNK, LANES), F32)
            for i in range(HEADS_PER_LANE_BLOCK):
                h = blk * HEADS_PER_LANE_BLOCK + i
                decay = _head_decay(causal, cs, cs_row, h)
                m = cb * decay
                dy_h = jnp.where(masks[i], dy_b, jnp.zeros_like(dy_b))
                acc_dx = acc_dx + _dot(m, dy_h, TN)
                d_m = _dot(dy_h, x_b, NT)
                d_cb = d_cb + d_m * decay
                w = d_m * m
                row_w = jnp.where(lane8 == h, jnp.sum(w, axis=1, keepdims=True), row_w)
                col_w = jnp.where(sub8 == h, jnp.sum(w, axis=0, keepdims=True), col_w)
            dxdt_blocks.append(acc_dx)
        dxdt_x = jnp.concatenate(dxdt_blocks, axis=1) + e_st_x * g1_x
        dxs_ref[...] = dxdt_x * dt_x + dy_x * d_ref[...]
        dye = dy_x * e_out_x
        xde = xdt_x * e_st_x
        ddt, y_off, tl, dskip = _group_sums([dxdt_x * xs, dye * cp_x, xde * g1_x, dy_x * xs], expand)
        ddt_ref[...] = ddt
        dd_ref[...] += jnp.sum(dskip, axis=0, keepdims=True)
        sp = None
        for part in _split3(d_s * prev):
            t = lax.dot_general(expand, part, NN, preferred_element_type=F32)
            sp = t if sp is None else sp + t
        last_col = jnp.exp(cs_row[:, CHUNK - 1:CHUNK]) * jnp.sum(sp, axis=1, keepdims=True)
        eye = lax.broadcasted_iota(jnp.int32, (HEADS_PER_GROUP, HEADS_PER_GROUP), 0) == lax.broadcasted_iota(
            jnp.int32, (HEADS_PER_GROUP, HEADS_PER_GROUP), 1)
        last_row = jnp.sum(jnp.where(eye, last_col, 0.0), axis=0, keepdims=True) + jnp.sum(tl, axis=0, keepdims=True)
        is_last = lax.broadcasted_iota(jnp.int32, (CHUNK, 1), 0) == CHUNK - 1
        d_cs = row_w + y_off - tl + jnp.where(is_last, last_row, 0.0)
        da_ref[...] = _dot_f32(upper, d_cs) - _dot_f32(upper, col_w, NT)
        dc_ref[...] = _dot(d_cb, bc) + _dot(dye, prev)
        db_ref[...] = _dot(d_cb, cc, TN) + _dot(xde, d_s)
        _decay_state(ds_scr, d_s, _dot(dye, cc, TN), cs)

    rev = lambda c: last - c
    xs = pl.BlockSpec((CHUNK, GROUP_COLS), lambda g, c: (rev(c), g))
    bsp = pl.BlockSpec((CHUNK, SSM_STATE), lambda g, c: (rev(c), D_INNER // SSM_STATE + g))
    csp = pl.BlockSpec((CHUNK, SSM_STATE), lambda g, c: (rev(c), (D_INNER + GN) // SSM_STATE + g))
    per_head = pl.BlockSpec((None, CHUNK, HEADS_PER_GROUP), lambda g, c: (g, rev(c), 0))
    dsk = pl.BlockSpec((None, 1, GROUP_COLS), lambda g, c: (g, 0, 0))
    dsum = pl.BlockSpec((None, 1, HEADS_PER_GROUP), lambda g, c: (g, 0, 0))
    st = pl.BlockSpec((None, GROUP_COLS, SSM_STATE), lambda g, c: (rev(c), g, 0))
    grp = pl.BlockSpec((CHUNK, SSM_STATE), lambda g, c: (rev(c), g))
    return _call(
        body, name=name, grid=(SSM_GROUPS, N_CHUNKS),
        in_specs=[xs, bsp, csp, per_head, per_head, dsk, st, xs],
        out_specs=[xs, grp, grp, per_head, per_head, dsum],
        out_shape=[jax.ShapeDtypeStruct((t_dim, D_INNER), F32), jax.ShapeDtypeStruct((t_dim, GN), F32),
                   jax.ShapeDtypeStruct((t_dim, GN), F32),
                   jax.ShapeDtypeStruct((SSM_GROUPS, t_dim, HEADS_PER_GROUP), F32),
                   jax.ShapeDtypeStruct((SSM_GROUPS, t_dim, HEADS_PER_GROUP), F32),
                   jax.ShapeDtypeStruct((SSM_GROUPS, 1, HEADS_PER_GROUP), F32)],
        scratch_shapes=[pltpu.VMEM((GROUP_COLS, SSM_STATE), F32)],
        sem=("parallel", "arbitrary"), args=[xbc, xbc, xbc, dtg, ag, dgx, states, dy], comm=comm)


NORM_GROUP = D_INNER // SSM_GROUPS


def _gate_norm_fwd(y, zx, nw, *, name, tm=256):
    t_dim = y.shape[0]
    row = pl.BlockSpec((tm, D_INNER), lambda i: (i, 0))

    def body(y_ref, z_ref, nw_ref, o_ref):
        z = z_ref[...]
        yz = y_ref[...] * (z * _sigmoid(z))
        for g in range(SSM_GROUPS):
            cols = slice(g * NORM_GROUP, (g + 1) * NORM_GROUP)
            yhat, _ = _rms(yz[:, cols])
            o_ref[:, cols] = (yhat * nw_ref[:, cols]).astype(BF16)

    return pl.pallas_call(
        body, name=name, grid=(t_dim // tm,), in_specs=[row, row, pl.BlockSpec((1, D_INNER), lambda i: (0, 0))],
        out_specs=row, out_shape=jax.ShapeDtypeStruct((t_dim, D_INNER), BF16),
        compiler_params=_params("parallel"),
    )(y, zx, nw)


def _gate_norm_bwd(y, zx, nw, dyn, *, name, tm=256):
    t_dim = y.shape[0]
    row = pl.BlockSpec((tm, D_INNER), lambda i: (i, 0))
    vec = pl.BlockSpec((1, D_INNER), lambda i: (0, 0))

    def body(y_ref, z_ref, nw_ref, dyn_ref, dy_ref, dz_ref, dnw_ref):
        @pl.when(pl.program_id(0) == 0)
        def _():
            dnw_ref[...] = jnp.zeros_like(dnw_ref)

        z = z_ref[...]
        yv = y_ref[...]
        sg = _sigmoid(z)
        silu_z = z * sg
        yz = yv * silu_z
        dyn_v = dyn_ref[...]
        for g in range(SSM_GROUPS):
            cols = slice(g * NORM_GROUP, (g + 1) * NORM_GROUP)
            yhat, r = _rms(yz[:, cols])
            dn = dyn_v[:, cols]
            dnw_ref[:, cols] += jnp.sum(dn * yhat, axis=0, keepdims=True)
            dyhat = dn * nw_ref[:, cols]
            dyz = r * (dyhat - yhat * jnp.mean(dyhat * yhat, axis=-1, keepdims=True))
            dy_ref[:, cols] = dyz * silu_z[:, cols]
            dz_ref[:, cols] = dyz * yv[:, cols] * _dsilu(z[:, cols], sg[:, cols])

    return pl.pallas_call(
        body, name=name, grid=(t_dim // tm,), in_specs=[row, row, vec, row],
        out_specs=[row, row, vec],
        out_shape=[jax.ShapeDtypeStruct((t_dim, D_INNER), F32), jax.ShapeDtypeStruct((t_dim, ZX_DIM), F32),
                   jax.ShapeDtypeStruct((1, D_INNER), F32)],
        compiler_params=_params("arbitrary"),
    )(y, zx, nw, dyn)


def _rope(t, cos2, sin2, *, name, tm=256):
    t_dim, width = t.shape
    half = ATT_HEAD_DIM // 2
    reps = width // 128

    def body(t_ref, cos_ref, sin_ref, o_ref):
        x = t_ref[...]
        lane = lax.broadcasted_iota(jnp.int32, (tm, width), 1)
        first = (lane % ATT_HEAD_DIM) < half
        rot = jnp.where(first, -pltpu.roll(x, width - half, 1), pltpu.roll(x, half, 1))
        o_ref[...] = x * jnp.tile(cos_ref[...], (1, reps)) + rot * jnp.tile(sin_ref[...], (1, reps))

    row = pl.BlockSpec((tm, width), lambda i: (i, 0))
    tab = pl.BlockSpec((tm, 128), lambda i: (i, 0))
    return pl.pallas_call(
        body, name=name, grid=(t_dim // tm,), in_specs=[row, tab, tab], out_specs=row,
        out_shape=jax.ShapeDtypeStruct((t_dim, width), F32), compiler_params=_params("parallel"),
    )(t, cos2, sin2)


HEADS_PER_LANE_TILE = LANES // ATT_HEAD_DIM
STACKED_ROWS = Q_PER_KV * WINDOW


def _att_half_masks():
    lane = lax.broadcasted_iota(jnp.int32, (WINDOW, LANES), 1)
    return [(lane >= i * ATT_HEAD_DIM) & (lane < (i + 1) * ATT_HEAD_DIM) for i in range(HEADS_PER_LANE_TILE)]


def _att_stack_heads(ref, kvh, masks):
    parts = []
    for g in range(Q_PER_KV):
        h = kvh * Q_PER_KV + g
        blk = ref[:, (h // HEADS_PER_LANE_TILE) * LANES:(h // HEADS_PER_LANE_TILE + 1) * LANES]
        parts.append(jnp.where(masks[h % HEADS_PER_LANE_TILE], blk, jnp.zeros_like(blk)))
    return jnp.concatenate(parts, axis=0)


def _att_kv_tile(ref, kvh, masks):
    blk = ref[:, (kvh // HEADS_PER_LANE_TILE) * LANES:(kvh // HEADS_PER_LANE_TILE + 1) * LANES]
    return jnp.where(masks[kvh % HEADS_PER_LANE_TILE], blk, pltpu.roll(blk, ATT_HEAD_DIM, 1)).astype(BF16)


def _att_stacked_masks(n):
    row = lax.bitwise_and(lax.broadcasted_iota(jnp.int32, (STACKED_ROWS, WINDOW), 0), WINDOW - 1)
    col = lax.broadcasted_iota(jnp.int32, (STACKED_ROWS, WINDOW), 1)
    return col <= row, (col > row) & (n > 0)


def _att_stack_columns(ref, kvh, rows):
    cols = [ref[:, kvh * Q_PER_KV + g:kvh * Q_PER_KV + g + 1] for g in range(Q_PER_KV)]
    return jnp.concatenate([jnp.broadcast_to(c, (rows, 1)) for c in cols], axis=0)


def _att_scores(q4, k_tile, mask):
    return jnp.where(mask, _dot(q4, k_tile, NT) * ATT_SCALE, NEG_BIG)


def _att_unstack(x4, kvh, masks, tiles):
    for g in range(Q_PER_KV):
        h = kvh * Q_PER_KV + g
        piece = x4[g * WINDOW:(g + 1) * WINDOW]
        t = h // HEADS_PER_LANE_TILE
        tiles[t] = piece if h % HEADS_PER_LANE_TILE == 0 else jnp.where(masks[1], piece, tiles[t])


def _attn_fwd(q, k, v, sinks, *, name, comm=None):
    t_dim = q.shape[0]

    def body(q_ref, kc_ref, kp_ref, vc_ref, vp_ref, s_ref, o_ref, l_ref):
        n = pl.program_id(0)
        masks = _att_half_masks()
        mask_c, mask_p = _att_stacked_masks(n)
        out_tiles = [None] * (D_MODEL // LANES)
        for kvh in range(N_KV_HEADS):
            q4 = _att_stack_heads(q_ref, kvh, masks).astype(BF16)
            kc, kp = _att_kv_tile(kc_ref, kvh, masks), _att_kv_tile(kp_ref, kvh, masks)
            vc, vp = _att_kv_tile(vc_ref, kvh, masks), _att_kv_tile(vp_ref, kvh, masks)
            sc = _att_scores(q4, kc, mask_c)
            sp = _att_scores(q4, kp, mask_p)
            sink = _att_stack_columns(s_ref, kvh, WINDOW)
            m = jnp.maximum(jnp.maximum(jnp.max(sc, axis=1, keepdims=True), jnp.max(sp, axis=1, keepdims=True)), sink)
            pc = jnp.exp(sc - m)
            pp = jnp.exp(sp - m)
            den = jnp.sum(pc, axis=1, keepdims=True) + jnp.sum(pp, axis=1, keepdims=True) + jnp.exp(sink - m)
            _att_unstack((_dot(pc, vc) + _dot(pp, vp)) / den, kvh, masks, out_tiles)
            lse4 = m + jnp.log(den)
            for g in range(Q_PER_KV):
                h = kvh * Q_PER_KV + g
                l_ref[:, h:h + 1] = lse4[g * WINDOW:(g + 1) * WINDOW]
        for t, tile in enumerate(out_tiles):
            o_ref[:, t * LANES:(t + 1) * LANES] = tile

    cur = lambda w: pl.BlockSpec((WINDOW, w), lambda n: (n, 0))
    prv = lambda w: pl.BlockSpec((WINDOW, w), lambda n: (jnp.maximum(n - 1, 0), 0))
    return _call(
        body, name=name, grid=(t_dim // WINDOW,),
        in_specs=[cur(D_MODEL), cur(KV_DIM), prv(KV_DIM), cur(KV_DIM), prv(KV_DIM), pl.BlockSpec((1, N_Q_HEADS), lambda n: (0, 0))],
        out_specs=[cur(D_MODEL), cur(N_Q_HEADS)],
        out_shape=[jax.ShapeDtypeStruct((t_dim, D_MODEL), F32), jax.ShapeDtypeStruct((t_dim, N_Q_HEADS), F32)],
        sem=("parallel",), args=[q, k, k, v, v, sinks], comm=comm)


def _attn_bwd(q, k, v, sinks, o, lse, do, *, name, comm=None):
    t_dim = q.shape[0]

    def body(q_ref, kc_ref, kp_ref, vc_ref, vp_ref, s_ref, o_ref, l_ref, do_ref, dq_ref, dk_ref, dv_ref, dsink_ref):
        n = pl.program_id(0)

        @pl.when(n == 0)
        def _():
            dk_ref[...] = jnp.zeros_like(dk_ref)
            dv_ref[...] = jnp.zeros_like(dv_ref)
            dsink_ref[...] = jnp.zeros_like(dsink_ref)

        masks = _att_half_masks()
        mask_c, mask_p = _att_stacked_masks(n)
        lane_row = lax.broadcasted_iota(jnp.int32, (1, N_Q_HEADS), 1)
        rows_c = pl.ds(pl.multiple_of(n * WINDOW, WINDOW), WINDOW)
        rows_p = pl.ds(pl.multiple_of(jnp.maximum(n - 1, 0) * WINDOW, WINDOW), WINDOW)
        dsink = jnp.zeros((1, N_Q_HEADS), F32)
        dq_tiles = [None] * (D_MODEL // LANES)
        kv_tiles = KV_DIM // LANES
        dkc_tiles, dkp_tiles, dvc_tiles, dvp_tiles = ([None] * kv_tiles for _ in range(4))

        def place(tiles, kvh, x):
            folded = x + pltpu.roll(x, ATT_HEAD_DIM, 1)
            t = kvh // HEADS_PER_LANE_TILE
            tiles[t] = folded if kvh % HEADS_PER_LANE_TILE == 0 else jnp.where(masks[1], folded, tiles[t])

        for kvh in range(N_KV_HEADS):
            q4 = _att_stack_heads(q_ref, kvh, masks).astype(BF16)
            do4 = _att_stack_heads(do_ref, kvh, masks)
            o4 = _att_stack_heads(o_ref, kvh, masks)
            kc, kp = _att_kv_tile(kc_ref, kvh, masks), _att_kv_tile(kp_ref, kvh, masks)
            vc, vp = _att_kv_tile(vc_ref, kvh, masks), _att_kv_tile(vp_ref, kvh, masks)
            l4 = _att_stack_columns(l_ref, kvh, WINDOW)
            pc = jnp.exp(_att_scores(q4, kc, mask_c) - l4)
            pp = jnp.exp(_att_scores(q4, kp, mask_p) - l4)
            delta = jnp.sum(do4 * o4, axis=1, keepdims=True)
            do4b = do4.astype(BF16)
            dsc = pc * (_dot(do4b, vc, NT) - delta)
            dsp = pp * (_dot(do4b, vp, NT) - delta)
            _att_unstack((_dot(dsc, kc) + _dot(dsp, kp)) * ATT_SCALE, kvh, masks, dq_tiles)
            place(dkc_tiles, kvh, _dot(dsc, q4, TN) * ATT_SCALE)
            place(dkp_tiles, kvh, _dot(dsp, q4, TN) * ATT_SCALE)
            place(dvc_tiles, kvh, _dot(pc, do4b, TN))
            place(dvp_tiles, kvh, _dot(pp, do4b, TN))
            p_sink = jnp.exp(_att_stack_columns(s_ref, kvh, WINDOW) - l4) * delta
            for g in range(Q_PER_KV):
                h = kvh * Q_PER_KV + g
                dsink = jnp.where(lane_row == h, -jnp.sum(p_sink[g * WINDOW:(g + 1) * WINDOW], axis=0, keepdims=True), dsink)
        for t, tile in enumerate(dq_tiles):
            dq_ref[:, t * LANES:(t + 1) * LANES] = tile
        for t in range(kv_tiles):
            lanes = slice(t * LANES, (t + 1) * LANES)
            dk_ref[rows_c, lanes] += dkc_tiles[t]
            dk_ref[rows_p, lanes] += dkp_tiles[t]
            dv_ref[rows_c, lanes] += dvc_tiles[t]
            dv_ref[rows_p, lanes] += dvp_tiles[t]
        dsink_ref[...] += dsink

    cur = lambda w: pl.BlockSpec((WINDOW, w), lambda n: (n, 0))
    prv = lambda w: pl.BlockSpec((WINDOW, w), lambda n: (jnp.maximum(n - 1, 0), 0))
    whole = pl.BlockSpec((t_dim, KV_DIM), lambda n: (0, 0))
    svec = pl.BlockSpec((1, N_Q_HEADS), lambda n: (0, 0))
    return _call(
        body, name=name, grid=(t_dim // WINDOW,),
        in_specs=[cur(D_MODEL), cur(KV_DIM), prv(KV_DIM), cur(KV_DIM), prv(KV_DIM), svec, cur(D_MODEL), cur(N_Q_HEADS), cur(D_MODEL)],
        out_specs=[cur(D_MODEL), whole, whole, svec],
        out_shape=[jax.ShapeDtypeStruct((t_dim, D_MODEL), F32), jax.ShapeDtypeStruct((t_dim, KV_DIM), F32),
                   jax.ShapeDtypeStruct((t_dim, KV_DIM), F32), jax.ShapeDtypeStruct((1, N_Q_HEADS), F32)],
        sem=("arbitrary",), args=[q, k, k, v, v, sinks, o, lse, do], comm=comm)


def _loss_head(x, nw, target, *, name, tm=256):
    t_dim, d_dim = x.shape
    row = pl.BlockSpec((tm, d_dim), lambda i: (i, 0))
    vec = pl.BlockSpec((1, d_dim), lambda i: (0, 0))

    def body(x_ref, nw_ref, tgt_ref, loss_ref, dx_ref, dnw_ref):
        @pl.when(pl.program_id(0) == 0)
        def _():
            loss_ref[...] = jnp.zeros_like(loss_ref)
            dnw_ref[...] = jnp.zeros_like(dnw_ref)

        xhat, r = _rms(x_ref[...])
        err = xhat * nw_ref[...] - tgt_ref[...]
        loss_ref[...] += 0.5 * _sum_all(jnp.mean(err * err, axis=-1, keepdims=True))
        dy = err * (1.0 / d_dim)
        dnw_ref[...] += jnp.sum(dy * xhat, axis=0, keepdims=True)
        dxhat = dy * nw_ref[...]
        dx_ref[...] = r * (dxhat - xhat * jnp.mean(dxhat * xhat, axis=-1, keepdims=True))

    return pl.pallas_call(
        body, name=name, grid=(t_dim // tm,), in_specs=[row, vec, row],
        out_specs=[pl.BlockSpec((1, 1), lambda i: (0, 0)), row, vec],
        out_shape=[jax.ShapeDtypeStruct((1, 1), F32), jax.ShapeDtypeStruct((t_dim, d_dim), F32),
                   jax.ShapeDtypeStruct((1, d_dim), F32)],
        compiler_params=_params("arbitrary"),
    )(x, nw, target)


def _rope_tables():
    pos = jnp.arange(SEQ, dtype=F32)
    inv = 1.0 / (ROPE_THETA ** (jnp.arange(0, ATT_HEAD_DIM, 2, dtype=F32) / ATT_HEAD_DIM))
    ang = pos[:, None] * inv[None, :]
    cos, sin = jnp.cos(ang), jnp.sin(ang)
    return jnp.tile(cos, (1, 4)), jnp.tile(sin, (1, 4))


def _to_groups(t):
    return t.reshape(t.shape[0], SSM_GROUPS, HEADS_PER_GROUP).transpose(1, 0, 2)


def _from_groups(t):
    return t.transpose(1, 0, 2).reshape(t.shape[1], SSM_HEADS)


def _forward_backward(x0, target, net):
    w = net.w
    nw = [[w("norm_w")[l, i][None, :] for i in range(3)] for l in range(2)]
    cos2, sin2 = _rope_tables()
    ffn_norm = [nw[0][0], nw[0][2], nw[1][0], nw[1][2]]

    ffn_pre = {}

    def ffn_f(x, blk):
        name = f"ffn_fwd{blk}"
        out, *ffn_pre[blk] = _ffn_fwd(x, ffn_norm[blk], w(f"gate{blk}"), w(f"up{blk}"), w(f"down{blk}"), name=name,
                                      comm=net.carry(name))
        return out

    x1 = ffn_f(x0, 0)
    zx, h1 = _norm_mm(x1, nw[0][1], w("w_in_t"), None, w_rows=ZX_DIM, name="ssm_in_proj", comm=net.carry("ssm_in_proj"))
    dtr = _mm(h1, w("w_in_t"), dims="nt", b_rows=(ZX_DIM, SSM_HEADS), name="ssm_dt_proj")
    xbc = _conv_fwd(zx, w("conv_w"), w("conv_b"), name="ssm_conv_fwd", comm=net.carry("ssm_conv_fwd"))
    dt, a_dt = _dt_prep(dtr, w("dt_bias"), w("a_log"), name="ssm_dt_prep")
    dtg, ag = _to_groups(dt), _to_groups(a_dt)
    dg = jnp.repeat(w("d_skip").reshape(SSM_GROUPS, 1, HEADS_PER_GROUP), SSM_HEAD_DIM, axis=2)
    y_ssd, states = _ssd_fwd(xbc, dtg, ag, dg, name="ssd_fwd", comm=net.carry("ssd_fwd"))
    yn = _gate_norm_fwd(y_ssd, zx, w("ssm_norm_w"), name="ssm_gate_norm_fwd")
    x2 = _mm(yn, w("wout"), res=x1, name="ssm_out_proj", comm=net.carry("ssm_out_proj"))
    x3 = ffn_f(x2, 1)
    k_pre, hk = _norm_mm(x3, w("kv_norm_w"), w("wk"), w("b_k"), name="k_proj")
    v = _mm(hk, w("wv"), bias=w("b_v"), name="v_proj")
    k_rot = _rope(k_pre, cos2, sin2, name="k_rope")
    x4 = ffn_f(x3, 2)
    q_pre, h4 = _norm_mm(x4, nw[1][1], w("wq"), w("b_q"), name="q_proj")
    q_rot = _rope(q_pre, cos2, sin2, name="q_rope")
    att, lse = _attn_fwd(q_rot, k_rot, v, w("sinks"), name="attn_fwd", comm=net.carry("attn_fwd"))
    x5 = _mm(att, w("wo"), bias=w("b_o"), res=x4, name="attn_out_proj")
    x6 = ffn_f(x5, 3)
    loss, dx6, d_final = _loss_head(x6, w("final_norm_w"), target, name="loss_head")

    d_norm = [[None] * 3 for _ in range(2)]

    def ffn_b(x, dout, blk):
        h, dob = _ffn_bwd_prep(x, ffn_norm[blk], dout, name=f"ffn_bwd_prep{blk}")
        name = f"ffn_bwd{blk}"
        dh, gg, gu, gd = _ffn_bwd(h, dob, *ffn_pre[blk], w(f"gate{blk}"), w(f"up{blk}"), w(f"down{blk}"), name=name,
                                  comm=net.carry(name))
        net.give(f"gate{blk}", gg)
        net.give(f"up{blk}", gu)
        net.give(f"down{blk}", gd)
        return _norm_bwd(x, ffn_norm[blk], dh, [dout], name=f"ffn_norm_bwd{blk}", comm=net.carry(f"ffn_norm_bwd{blk}"))

    by_rows = lambda g: g.reshape(N_DEV, g.shape[0] // N_DEV, g.shape[1])
    dx5, d_norm[1][2] = ffn_b(x5, dx6, 3)
    d_att = _mm(dx5, w("wo"), dims="nt", name="attn_out_proj_dx", comm=net.carry("attn_out_proj_dx"))
    net.give("w_o", by_rows(_mm(att, dx5, dims="tn", out_dtype=BF16, name="attn_out_proj_dw")))
    d_bo = _colsum(dx5, name="attn_bo_grad")
    dq_rot, dk_rot, dv, d_sinks = _attn_bwd(q_rot, k_rot, v, w("sinks"), att, lse, d_att, name="attn_bwd", comm=net.carry("attn_bwd"))
    dq = _rope(dq_rot, cos2, -sin2, name="q_rope_bwd")
    dk = _rope(dk_rot, cos2, -sin2, name="k_rope_bwd")
    dh4 = _mm(dq, w("wq"), dims="nt", name="q_proj_dx")
    net.give("w_q", by_rows(_mm(h4, dq, dims="tn", out_dtype=BF16, name="q_proj_dw")))
    d_bq = _colsum(dq, name="attn_bq_grad")
    dx4, d_norm[1][1] = _norm_bwd(x4, nw[1][1], dh4, [dx5], name="attn_norm_bwd")
    dx3a, d_norm[1][0] = ffn_b(x3, dx4, 2)
    dhk = _mm(dk, w("wk"), dims="nt", name="k_proj_dx", comm=net.carry("k_proj_dx"))
    dhk = _mm(dv, w("wv"), dims="nt", res=dhk, name="v_proj_dx")
    net.give("w_k", by_rows(_mm(hk, dk, dims="tn", out_dtype=BF16, name="k_proj_dw")))
    net.give("w_v", by_rows(_mm(hk, dv, dims="tn", out_dtype=BF16, name="v_proj_dw")))
    d_bk = _colsum(dk, name="bk_grad")
    d_bv = _colsum(dv, name="bv_grad")
    dx3, d_kvn = _norm_bwd(x3, w("kv_norm_w"), dhk, [dx3a], name="kv_norm_bwd")
    dx2, d_norm[0][2] = ffn_b(x2, dx3, 1)
    d_yn = _mm(dx2, w("wout"), dims="nt", name="ssm_out_proj_dx", comm=net.carry("ssm_out_proj_dx"))
    net.give("w_out", by_rows(_mm(yn, dx2, dims="tn", out_dtype=BF16, name="ssm_out_proj_dw")))
    dy_ssd, dzx, d_ssm_norm = _gate_norm_bwd(y_ssd, zx, w("ssm_norm_w"), d_yn, name="ssm_gate_norm_bwd")
    dxs, d_b, d_c, ddtg, dag, ddg = _ssd_bwd(xbc, dtg, ag, dg, states, dy_ssd, name="ssd_bwd", comm=net.carry("ssd_bwd"))
    dzx, d_conv_w, d_conv_b = _conv_bwd(zx, w("conv_w"), w("conv_b"), dxs, d_b, d_c, dzx, name="ssm_conv_bwd",
                                        comm=net.carry("ssm_conv_bwd"))
    ddtr, d_dt_bias, d_a_log = _dt_bwd(dtr, w("dt_bias"), w("a_log"), dt, _from_groups(ddtg), _from_groups(dag), name="ssm_dt_bwd")
    dh1 = _mm(dzx, w("w_in_t"), b_rows=(0, ZX_DIM), name="ssm_in_proj_dx")
    dh1 = _mm(ddtr, w("w_in_t"), b_rows=(ZX_DIM, SSM_HEADS), res=dh1, name="ssm_dt_proj_dx")
    g_zx = _mm(dzx, h1, dims="tn", out_dtype=BF16, name="ssm_in_proj_dw")
    g_dt = _mm(ddtr, h1, dims="tn", out_dtype=BF16, name="ssm_dt_proj_dw")
    net.give("w_in", jnp.concatenate([g_zx, g_dt], axis=0).reshape(N_DEV, IN_PROJ_SHARD, D_MODEL))
    dx1, d_norm[0][1] = _norm_bwd(x1, nw[0][1], dh1, [dx2], name="ssm_norm_bwd", comm=net.carry("ssm_norm_bwd"))
    dx0, d_norm[0][0] = ffn_b(x0, dx1, 0)

    small = {"norm_w": jnp.concatenate([d_norm[l][i] for l in range(2) for i in range(3)], axis=0),
             "ssm_conv_w": d_conv_w, "ssm_conv_b": d_conv_b, "ssm_dt_bias": d_dt_bias, "ssm_a_log": d_a_log,
             "ssm_d": ddg.reshape(1, SSM_HEADS), "ssm_norm_w": d_ssm_norm, "kv_norm_w": d_kvn,
             "b_k": d_bk, "b_v": d_bv, "attn_b_q": d_bq, "attn_sinks": d_sinks, "attn_b_o": d_bo, "final_norm_w": d_final}
    return loss, dx0, small


BLOCK_BYTES = 1 << 20


def _row_tile(rows, cols):
    for t in (512, 256, 128, 64, 32, 16):
        if rows % t == 0 and t * cols * 4 <= BLOCK_BYTES:
            return t
    return rows


def _cast_bf16(x, *, name):
    n_blk, rows, cols = x.shape
    tm = rows if rows * cols * 4 <= 2 * BLOCK_BYTES else _row_tile(rows, cols)
    spec = pl.BlockSpec((None, tm, cols), lambda b, i: (b, i, 0))

    def body(x_ref, o_ref):
        o_ref[...] = x_ref[...].astype(BF16)

    return pl.pallas_call(body, name=name, grid=(n_blk, rows // tm), in_specs=[spec], out_specs=spec,
                          out_shape=jax.ShapeDtypeStruct(x.shape, BF16), compiler_params=_params("parallel", "parallel"))(x)


def _pair_add(grad, theirs, *, name):
    n_slots, rows, cols = theirs.shape
    tm = rows if rows * cols * 4 <= 2 * BLOCK_BYTES else _row_tile(rows, cols)

    def body(g_ref, t_ref, o_ref):
        mine = jnp.where(lax.axis_index("c") == 0, g_ref[0].astype(F32), g_ref[1].astype(F32))
        o_ref[...] = (mine + t_ref[...].astype(F32)).astype(BF16)

    spec = pl.BlockSpec((None, tm, cols), lambda s, i: (s, i, 0))
    return pl.pallas_call(
        body, name=name, grid=(n_slots, rows // tm),
        in_specs=[pl.BlockSpec((None, 2, tm, cols), lambda s, i: (s, 0, i, 0)), spec], out_specs=spec,
        out_shape=jax.ShapeDtypeStruct(theirs.shape, BF16), compiler_params=_params("parallel", "parallel"),
    )(grad.reshape((n_slots, 2, rows, cols)), theirs)


def _adam_update(g, w, m, v):
    m = ADAM_B1 * m + (1.0 - ADAM_B1) * g
    v = ADAM_B2 * v + (1.0 - ADAM_B2) * (g * g)
    m_hat = m / (1.0 - ADAM_B1 ** ADAM_STEP)
    v_hat = v / (1.0 - ADAM_B2 ** ADAM_STEP)
    delta = -ADAM_LR * (m_hat / (jnp.sqrt(v_hat) + ADAM_EPS) + ADAM_WD * w)
    return delta, m, v


def _adamw(parts, w, m, v, first_blk, prev, *, name, comm=None):
    n_blk, rows, cols = w.shape
    tm = _row_tile(rows, cols)
    n_tiles = rows // tm
    spec = pl.BlockSpec((None, tm, cols), lambda b, i: (first_blk + b, i, 0))
    n_prev, n_here = len(prev), len(parts)
    n_parts = parts[0].shape[0]

    def part_spec(q):
        return pl.BlockSpec((n_parts, tm, cols), lambda b, i: (0, jnp.where(b < q, 0, jnp.where(b == q, i, n_tiles - 1)), 0))

    def body(*refs):
        p_refs = refs[:n_here]
        w_ref, m_ref, v_ref = refs[n_here:n_here + 3]
        g_ref, d_ref, nm_ref, nv_ref = refs[n_here + 3 + n_prev:]
        b = pl.program_id(0)
        g = None
        for s in range(n_parts):
            t = p_refs[0][s]
            for q in range(1, n_here):
                t = jnp.where(b == q, p_refs[q][s], t)
            g = t.astype(F32) if g is None else g + t.astype(F32)
        delta, nm, nv = _adam_update(g, w_ref[...], m_ref[...], v_ref[...])
        g_ref[...] = g
        d_ref[...] = delta
        nm_ref[...] = nm
        nv_ref[...] = nv

    return _call(
        body, name=name, grid=(n_here, n_tiles),
        in_specs=[part_spec(q) for q in range(n_here)] + [spec, spec, spec] + [pl.BlockSpec(memory_space=pl.ANY)] * n_prev,
        out_specs=[spec] * 4, out_shape=[jax.ShapeDtypeStruct((n_blk, rows, cols), F32)] * 4,
        aliases={n_here + 3 + q: q for q in range(n_prev)}, sem=("arbitrary", "arbitrary"),
        args=[*parts, w, m, v, *prev], comm=comm)


def _sum_parts(parts, *, name):
    def body(p_ref, o_ref):
        g = p_ref[0]
        for s in range(1, N_DEV):
            g = g + p_ref[s]
        o_ref[...] = g

    return pl.pallas_call(body, name=name, out_shape=jax.ShapeDtypeStruct(parts.shape[1:], F32), compiler_params=_params())(parts)


def _adamw_packed(g, w, m, v, *, name):
    def body(g_ref, w_ref, m_ref, v_ref, d_ref, nm_ref, nv_ref):
        delta, nm, nv = _adam_update(g_ref[...], w_ref[...], m_ref[...], v_ref[...])
        d_ref[...] = delta
        nm_ref[...] = nm
        nv_ref[...] = nv

    return pl.pallas_call(body, name=name, out_shape=[jax.ShapeDtypeStruct(g.shape, F32)] * 3, compiler_params=_params())(g, w, m, v)


SUBLANES = 8


def _pack(arrs):
    rows = []
    for a in arrs:
        flat = a.reshape(-1)
        pad = (-flat.shape[0]) % LANES
        rows.append(jnp.pad(flat, (0, pad)).reshape(-1, LANES))
    out = jnp.concatenate(rows, axis=0)
    return jnp.pad(out, ((0, (-out.shape[0]) % SUBLANES), (0, 0)))


def _unpack(packed, shapes):
    outs, r = [], 0
    for shp in shapes:
        n = math.prod(shp)
        nr = -(-n // LANES)
        outs.append(packed[r:r + nr].reshape(-1)[:n].reshape(shp))
        r += nr
    return outs


WEIGHT_NAMES = ("norm_w", "ffn_w_gate", "ffn_w_up", "ffn_w_down", "ssm_w_in", "ssm_conv_w", "ssm_conv_b", "ssm_dt_bias",
                "ssm_a_log", "ssm_d", "ssm_norm_w", "ssm_w_out", "kv_norm_w", "w_k", "b_k", "w_v", "b_v", "attn_w_q",
                "attn_b_q", "attn_sinks", "attn_w_o", "attn_b_o", "final_norm_w")
MATRIX_NAMES = ("ffn_w_gate", "ffn_w_up", "ffn_w_down", "ssm_w_in", "ssm_w_out", "w_k", "w_v", "attn_w_q", "attn_w_o")
VECTOR_NAMES = tuple(n for n in WEIGHT_NAMES if n not in MATRIX_NAMES)
SHARDED_VECTORS = ("norm_w", "ssm_conv_w", "ssm_conv_b", "ssm_norm_w")


GATHER_PLAN = {
    "gather_stage0": ("gate0", "up0", "down0", "vec"),
    "ffn_fwd0": ("w_in",),
    "ssm_in_proj": ("w_out", "gate1"),
    "ssm_conv_fwd": ("w_k", "w_v", "up1"),
    "ssd_fwd": ("down1", "gate2", "up2"),
    "ssm_out_proj": ("w_q", "w_o"),
    "ffn_fwd1": ("down2", "gate3"),
    "ffn_fwd2": ("up3",),
    "attn_fwd": ("down3",),
}
PAIR_PLAN = {
    "attn_out_proj_dx": ("gate3", "up3", "down3"),
    "ffn_bwd2": ("w_q", "w_o"),
    "k_proj_dx": ("gate2", "up2", "down2"),
    "ssm_out_proj_dx": ("w_k", "w_v", "gate1", "up1", "down1"),
    "ssd_bwd": ("w_out",),
    "ssm_norm_bwd": ("w_in",),
    "ffn_norm_bwd0": ("gate0", "up0", "down0"),
}
CHIP_PLAN = {
    "attn_bwd": ("gate3", "up3"),
    "ffn_bwd2": ("down3",),
    "ffn_bwd1": ("gate2", "up2", "w_q", "w_o"),
    "ssd_bwd": ("down2", "gate1", "up1", "down1", "w_k", "w_v"),
    "ssm_conv_bwd": ("w_out",),
    "ffn_bwd0": ("w_in",),
    "adamw_gate": ("gate0",),
    "adamw_up": ("up0",),
    "adamw_down": ("down0",),
}
FFN_PARAMS = {"gate": "ffn_w_gate", "up": "ffn_w_up", "down": "ffn_w_down"}
SINGLE_MATRICES = {"w_in": "ssm_w_in", "w_out": "ssm_w_out", "w_k": "w_k", "w_v": "w_v", "w_q": "attn_w_q", "w_o": "attn_w_o"}


TRANSPOSED = ("ffn_w_gate", "ffn_w_up", "ssm_w_in")


def _matrix_view(name, a):
    if name in TRANSPOSED:
        a = jnp.swapaxes(a, -1, -2)
    return a.reshape((-1,) + a.shape[-2:])


def _from_matrix_view(name, a, shape):
    if name in TRANSPOSED:
        return jnp.swapaxes(a.reshape(shape[:-2] + (shape[-1], shape[-2])), -1, -2)
    return a.reshape(shape)


class _MeshNet:
    def __init__(self, p):
        self.p = p
        self.views = {n: _matrix_view(n, p[n]) for n in MATRIX_NAMES}
        self.local = {"vec": _pack([p[n] for n in SHARDED_VECTORS])}
        for short, n in FFN_PARAMS.items():
            cast = _cast_bf16(self.views[n], name=f"cast_{short}")
            self.local.update({f"{short}{k}": (cast, k) for k in range(N_FFN)})
        for short, n in SINGLE_MATRICES.items():
            self.local[short] = (_cast_bf16(self.views[n], name=f"cast_{short}"), 0)
        self.gathered_at, self.pairs_at, self.parts_at, self.grads, self.cache = {}, {}, {}, {}, {}

    def carry(self, name):
        comms = []
        if name in GATHER_PLAN:
            keys, comm = GATHER_PLAN[name], _Gather([self.local[k] for k in GATHER_PLAN[name]])
            self.gathered_at.update({k: (comm, i) for i, k in enumerate(keys)})
            comms.append(comm)
        if name in CHIP_PLAN:
            sums = []
            for k in CHIP_PLAN[name]:
                comm, i = self.pairs_at[k]
                sums.append(_pair_add(self.grads[k], comm.results[i], name=f"pair_add_{k}"))
            comm = _ChipExchange(sums)
            self.parts_at.update({k: (comm, i) for i, k in enumerate(CHIP_PLAN[name])})
            comms.append(comm)
        if name in PAIR_PLAN:
            keys, comm = PAIR_PLAN[name], _PairSwap([self.grads[k] for k in PAIR_PLAN[name]])
            self.pairs_at.update({k: (comm, i) for i, k in enumerate(keys)})
            comms.append(comm)
        return comms

    def run(self, name):
        for comm in self.carry(name):
            _run_exchange(comm, name=name)

    def give(self, key, grad):
        self.grads[key] = grad

    def parts(self, key):
        comm, i = self.parts_at[key]
        return comm.results[i]

    def _gathered(self, key):
        comm, i = self.gathered_at[key]
        return comm.results[i]

    def _vec(self, r0, r1, lead):
        t = self._gathered("vec")[:, r0:r1, :].reshape(N_DEV, lead, -1)
        return t.transpose(1, 0, 2).reshape(lead, -1)

    def _derive(self, name):
        p = self.p
        if name[:-1] in FFN_PARAMS:
            return self._gathered(name)
        if name == "w_in_t":
            return self._gathered("w_in").reshape(N_DEV * IN_PROJ_SHARD, D_MODEL)
        by_rows = {"wout": "w_out", "wk": "w_k", "wv": "w_v", "wq": "w_q", "wo": "w_o"}
        if name in by_rows:
            g = self._gathered(by_rows[name])
            return g.reshape(N_DEV * g.shape[1], g.shape[2])
        vectors = {"norm_w": lambda: self._vec(0, 6, 6).reshape(2, 3, D_MODEL), "conv_w": lambda: self._vec(6, 18, CONV_WIDTH),
                   "conv_b": lambda: self._vec(18, 21, 1), "ssm_norm_w": lambda: self._vec(21, 23, 1)}
        if name in vectors:
            return vectors[name]()
        replicated = {"dt_bias": p["ssm_dt_bias"], "a_log": p["ssm_a_log"], "d_skip": p["ssm_d"], "kv_norm_w": p["kv_norm_w"][None],
                      "b_k": p["b_k"][None], "b_v": p["b_v"][None], "b_q": p["attn_b_q"], "sinks": p["attn_sinks"],
                      "b_o": p["attn_b_o"], "final_norm_w": p["final_norm_w"][None]}
        return replicated[name]

    def w(self, name):
        if name not in self.cache:
            self.cache[name] = self._derive(name)
        return self.cache[name]


def _step(x, target, p, m, v):
    pos = _slot(_position())
    net = _MeshNet(p)
    net.run("gather_stage0")
    loss, grad_x, small = _forward_backward(x, target, net)

    grads, deltas, new_m, new_v = {}, {}, {}, {}
    view = lambda d, n: _matrix_view(n, d[n])
    vec_gather = _Gather([_pack([small[n] for n in VECTOR_NAMES])])
    for short, n in SINGLE_MATRICES.items():
        outs = _adamw([net.parts(short)], net.views[n], view(m, n), view(v, n), 0, [], name=f"adamw_{short}",
                      comm=[vec_gather] if short == "w_in" else None)
        grads[n], deltas[n], new_m[n], new_v[n] = [_from_matrix_view(n, o, p[n].shape) for o in outs]
    ffn_outs = {}
    for short, n in FFN_PARAMS.items():
        ffn_outs[short] = _adamw([net.parts(f"{short}{k}") for k in range(1, N_FFN)], net.views[n], view(m, n), view(v, n), 1, [],
                                 name=f"adamw_{short}", comm=net.carry(f"adamw_{short}"))
    for short, n in FFN_PARAMS.items():
        outs = _adamw([net.parts(f"{short}0")], net.views[n], view(m, n), view(v, n), 0, ffn_outs[short], name=f"adamw_{short}0")
        grads[n], deltas[n], new_m[n], new_v[n] = [_from_matrix_view(n, o, p[n].shape) for o in outs]
    vec_sum = _sum_parts(vec_gather.results[0], name="sum_vector_grads")
    full_shapes = {"norm_w": (2, 3, D_MODEL), "ssm_conv_w": (1, CONV_WIDTH, CONV_DIM), "ssm_conv_b": (1, CONV_DIM),
                   "ssm_norm_w": (1, D_INNER)}
    vec_full = dict(zip(VECTOR_NAMES, _unpack(vec_sum, [full_shapes.get(n, p[n].shape) for n in VECTOR_NAMES])))
    for n in VECTOR_NAMES:
        g = vec_full[n]
        if n in SHARDED_VECTORS:
            per = p[n].shape[-1]
            g = lax.dynamic_slice_in_dim(g, pos * per, per, axis=g.ndim - 1)
        grads[n] = g
    packed = _adamw_packed(*[_pack([d[n] for n in VECTOR_NAMES]) for d in (grads, p, m, v)], name="adamw_vectors")
    shapes = [p[n].shape for n in VECTOR_NAMES]
    for d, pk in zip((deltas, new_m, new_v), packed):
        d.update(zip(VECTOR_NAMES, _unpack(pk, shapes)))
    return loss, grad_x, grads, deltas, new_m, new_v


def kernel(x, norm_w, ffn_w_gate, ffn_w_up, ffn_w_down, ssm_w_in, ssm_conv_w, ssm_conv_b, ssm_dt_bias, ssm_a_log, ssm_d, ssm_norm_w, ssm_w_out, kv_norm_w, w_k, b_k, w_v, b_v, attn_w_q, attn_b_q, attn_sinks, attn_w_o, attn_b_o, final_norm_w, loss_target, m_norm_w, m_ffn_w_gate, m_ffn_w_up, m_ffn_w_down, m_ssm_w_in, m_ssm_conv_w, m_ssm_conv_b, m_ssm_dt_bias, m_ssm_a_log, m_ssm_d, m_ssm_norm_w, m_ssm_w_out, m_kv_norm_w, m_w_k, m_b_k, m_w_v, m_b_v, m_attn_w_q, m_attn_b_q, m_attn_sinks, m_attn_w_o, m_attn_b_o, m_final_norm_w, v_norm_w, v_ffn_w_gate, v_ffn_w_up, v_ffn_w_down, v_ssm_w_in, v_ssm_conv_w, v_ssm_conv_b, v_ssm_dt_bias, v_ssm_a_log, v_ssm_d, v_ssm_norm_w, v_ssm_w_out, v_kv_norm_w, v_w_k, v_b_k, v_w_v, v_b_v, v_attn_w_q, v_attn_b_q, v_attn_sinks, v_attn_w_o, v_attn_b_o, v_final_norm_w):
    p = dict(zip(WEIGHT_NAMES, (norm_w, ffn_w_gate, ffn_w_up, ffn_w_down, ssm_w_in, ssm_conv_w, ssm_conv_b, ssm_dt_bias, ssm_a_log, ssm_d, ssm_norm_w, ssm_w_out, kv_norm_w, w_k, b_k, w_v, b_v, attn_w_q, attn_b_q, attn_sinks, attn_w_o, attn_b_o, final_norm_w)))
    m = dict(zip(WEIGHT_NAMES, (m_norm_w, m_ffn_w_gate, m_ffn_w_up, m_ffn_w_down, m_ssm_w_in, m_ssm_conv_w, m_ssm_conv_b, m_ssm_dt_bias, m_ssm_a_log, m_ssm_d, m_ssm_norm_w, m_ssm_w_out, m_kv_norm_w, m_w_k, m_b_k, m_w_v, m_b_v, m_attn_w_q, m_attn_b_q, m_attn_sinks, m_attn_w_o, m_attn_b_o, m_final_norm_w)))
    v = dict(zip(WEIGHT_NAMES, (v_norm_w, v_ffn_w_gate, v_ffn_w_up, v_ffn_w_down, v_ssm_w_in, v_ssm_conv_w, v_ssm_conv_b, v_ssm_dt_bias, v_ssm_a_log, v_ssm_d, v_ssm_norm_w, v_ssm_w_out, v_kv_norm_w, v_w_k, v_b_k, v_w_v, v_b_v, v_attn_w_q, v_attn_b_q, v_attn_sinks, v_attn_w_o, v_attn_b_o, v_final_norm_w)))
    loss, grad_x, grads, deltas, new_m, new_v = _step(x[0], loss_target[0], p, m, v)
    loss = lax.psum(loss[0, 0], ("x", "y", "c"))
    return (loss, grad_x[None], *[grads[n] for n in WEIGHT_NAMES], *[deltas[n] for n in WEIGHT_NAMES],
            *[new_m[n] for n in WEIGHT_NAMES], *[new_v[n] for n in WEIGHT_NAMES])
```

```python
import functools
import math

import jax
import jax.numpy as jnp
from jax import lax
from jax.experimental import pallas as pl
from jax.experimental.pallas import tpu as pltpu

F32 = jnp.float32
BF16 = jnp.bfloat16

N_DEV = 8
SEQ = 2048
D_MODEL = 1024
D_FF_SHARD = 352
N_FFN = 4
D_INNER = 2048
SSM_HEADS = 32
SSM_HEAD_DIM = 64
SSM_GROUPS = 4
HEADS_PER_GROUP = 8
SSM_STATE = 128
CHUNK = 128
N_CHUNKS = SEQ // CHUNK
GN = SSM_GROUPS * SSM_STATE
CONV_DIM = D_INNER + 2 * GN
CONV_WIDTH = 4
ZX_DIM = D_INNER + CONV_DIM
IN_PROJ_SHARD = 644
ATT_HEAD_DIM = 64
N_Q_HEADS = 16
N_KV_HEADS = 4
Q_PER_KV = 4
KV_DIM = N_KV_HEADS * ATT_HEAD_DIM
WINDOW = 128
ROPE_THETA = 10000.0
EPS = 1e-5
FFN_RES_WEIGHT = 0.5
ATT_SCALE = 1.0 / math.sqrt(ATT_HEAD_DIM)
NEG_BIG = -1e30

ADAM_LR = 0.001
ADAM_B1 = 0.9
ADAM_B2 = 0.999
ADAM_EPS = 1e-08
ADAM_WD = 0.01
ADAM_STEP = 10

VMEM_LIMIT_BYTES = 56 * 1024 * 1024
FFN_BWD_VMEM_LIMIT_BYTES = 61 * 1024 * 1024

NN = (((1,), (0,)), ((), ()))
NT = (((1,), (1,)), ((), ()))
TN = (((0,), (0,)), ((), ()))
_DIMS = {"nn": NN, "nt": NT, "tn": TN}


def _params(*sem):
    return pltpu.CompilerParams(dimension_semantics=sem if sem else None, vmem_limit_bytes=VMEM_LIMIT_BYTES)


def _dot(a, b, dims=NN):
    return lax.dot_general(a.astype(BF16), b.astype(BF16), dims, preferred_element_type=F32)


def _sigmoid(x):
    return 1.0 / (1.0 + jnp.exp(-x))


def _dsilu(x, s):
    return s * (1.0 + x * (1.0 - s))


def _rms(x):
    r = lax.rsqrt(jnp.mean(x * x, axis=-1, keepdims=True) + EPS)
    return x * r, r


def _sum_all(x):
    return jnp.sum(jnp.sum(x, axis=1, keepdims=True), axis=0, keepdims=True)


MESH = pl.DeviceIdType.MESH
N_PEERS = N_DEV - 1
N_CHIPS = N_DEV // 2


def _position():
    return lax.axis_index("x"), lax.axis_index("y"), lax.axis_index("c")


def _slot(p):
    return 4 * p[0] + 2 * p[1] + p[2]


class _Exchange:
    def __init__(self, arrays, out_shapes):
        n = len(arrays)
        self.arrays = list(arrays)
        self.out_shapes = out_shapes
        self.scratch = [pltpu.SemaphoreType.DMA((n, N_PEERS)), pltpu.SemaphoreType.DMA((n, N_PEERS)), pltpu.SemaphoreType.DMA((n,))]
        self.results = None


class _Gather(_Exchange):
    def __init__(self, pieces):
        pieces = [p if isinstance(p, tuple) else (p, None) for p in pieces]
        self.blocks = [k for _, k in pieces]
        shapes = [a.shape if k is None else a.shape[1:] for a, k in pieces]
        super().__init__([a for a, _ in pieces], [jax.ShapeDtypeStruct((N_DEV,) + s, a.dtype) for s, (a, _) in zip(shapes, pieces)])

    def _plan(self, ins, outs, sems):
        send_sems, recv_sems, local_sems = sems
        x, y, c = _position()
        me, sibling = (x, y, c), (x, y, 1 - c)
        chips = [(1 - x, y), (x, 1 - y), (1 - x, 1 - y)]
        n = len(ins)
        ins = [r if k is None else r.at[k] for r, k in zip(ins, self.blocks)]

        def copy(a, k, block, to, src=None):
            dst = outs[a].at[_slot(block)]
            return pltpu.make_async_remote_copy(src_ref=dst if src is None else src, dst_ref=dst, send_sem=send_sems.at[a, k],
                                                recv_sem=recv_sems.at[a, k], device_id=to, device_id_type=MESH)

        mine = [pltpu.make_async_copy(ins[a], outs[a].at[_slot(me)], local_sems.at[a]) for a in range(n)]
        first = []
        for a in range(n):
            first.append(copy(a, 0, me, sibling, src=ins[a]))
            first += [copy(a, 1 + j, me, (*chip, c), src=ins[a]) for j, chip in enumerate(chips)]
        return n, c, me, sibling, chips, copy, mine, first

    def start(self, ins, outs, sems):
        _, _, _, _, _, _, mine, first = self._plan(ins, outs, sems)
        for cp in mine + first:
            cp.start()

    def finish(self, ins, outs, sems):
        n, c, me, sibling, chips, copy, mine, first = self._plan(ins, outs, sems)
        passed = []
        for j, chip in enumerate(chips):
            for a in range(n):
                copy(a, 1 + j, (*chip, c), me).wait_recv()
                fwd = copy(a, 4 + j, (*chip, c), sibling)
                fwd.start()
                passed.append(fwd)
        for a in range(n):
            copy(a, 0, sibling, me).wait_recv()
            for j, chip in enumerate(chips):
                copy(a, 4 + j, (*chip, 1 - c), me).wait_recv()
        for cp in first + passed:
            cp.wait_send()
        for cp in mine:
            cp.wait()


class _PairSwap(_Exchange):
    def __init__(self, arrays):
        n = len(arrays)
        self.arrays = list(arrays)
        self.out_shapes = [jax.ShapeDtypeStruct((N_CHIPS,) + a.shape[1:], a.dtype) for a in arrays]
        self.scratch = [pltpu.SemaphoreType.DMA((n, N_CHIPS)), pltpu.SemaphoreType.DMA((n, N_CHIPS))]
        self.results = None

    def _plan(self, ins, outs, sems):
        send_sems, recv_sems = sems
        x, y, c = _position()
        return [pltpu.make_async_remote_copy(src_ref=ins[a].at[2 * q + 1 - c], dst_ref=outs[a].at[q], send_sem=send_sems.at[a, q],
                                             recv_sem=recv_sems.at[a, q], device_id=(x, y, 1 - c), device_id_type=MESH)
                for a in range(len(ins)) for q in range(N_CHIPS)]

    def start(self, ins, outs, sems):
        for cp in self._plan(ins, outs, sems):
            cp.start()

    def finish(self, ins, outs, sems):
        for cp in self._plan(ins, outs, sems):
            cp.wait()


class _ChipExchange(_Exchange):
    def __init__(self, arrays):
        n = len(arrays)
        self.arrays = list(arrays)
        self.out_shapes = [jax.ShapeDtypeStruct(a.shape, a.dtype) for a in arrays]
        self.scratch = [pltpu.SemaphoreType.DMA((n, 3)), pltpu.SemaphoreType.DMA((n, 3)), pltpu.SemaphoreType.DMA((n,))]
        self.results = None

    def _plan(self, ins, outs, sems):
        send_sems, recv_sems, local_sems = sems
        x, y, c = _position()
        here = 2 * x + y
        chips = [(1 - x, y), (x, 1 - y), (1 - x, 1 - y)]
        n = len(ins)

        def copy(a, k, src_slot, dst_slot):
            return pltpu.make_async_remote_copy(src_ref=ins[a].at[src_slot], dst_ref=outs[a].at[dst_slot], send_sem=send_sems.at[a, k],
                                                recv_sem=recv_sems.at[a, k], device_id=(*chips[k], c), device_id_type=MESH)

        there = [2 * qx + qy for qx, qy in chips]
        mine = [pltpu.make_async_copy(ins[a].at[here], outs[a].at[here], local_sems.at[a]) for a in range(n)]
        sends = [copy(a, k, there[k], here) for a in range(n) for k in range(3)]
        arrivals = lambda: [copy(a, k, here, there[k]) for a in range(n) for k in range(3)]
        return mine, sends, arrivals

    def start(self, ins, outs, sems):
        mine, sends, _ = self._plan(ins, outs, sems)
        for cp in mine + sends:
            cp.start()

    def finish(self, ins, outs, sems):
        mine, sends, arrivals = self._plan(ins, outs, sems)
        for cp in arrivals():
            cp.wait_recv()
        for cp in sends:
            cp.wait_send()
        for cp in mine:
            cp.wait()


def _call(body, *, name, grid, in_specs, out_specs, out_shape, args, scratch_shapes=(), sem=(), comm=(), aliases=None,
          vmem_limit=VMEM_LIMIT_BYTES):
    single = not isinstance(out_shape, (list, tuple))
    out_shape = [out_shape] if single else list(out_shape)
    out_specs = [out_specs] if single else list(out_specs)
    comms = list(comm or ())
    n_in, n_out, n_scr = len(args), len(out_shape), len(scratch_shapes)
    params = pltpu.CompilerParams(dimension_semantics=tuple(sem) if sem else None, vmem_limit_bytes=vmem_limit)
    if not comms:
        res = pl.pallas_call(body, name=name, grid=grid, in_specs=list(in_specs), out_specs=out_specs, out_shape=out_shape,
                             scratch_shapes=list(scratch_shapes), input_output_aliases=aliases or {}, compiler_params=params)(*args)
        return res[0] if single else res
    counts = [n_in] + [len(c.arrays) for c in comms] + [n_out] + [len(c.out_shapes) for c in comms] + [n_scr] + [len(c.scratch) for c in comms]
    nc = len(comms)

    def carried(*refs):
        pos, groups = 0, []
        for cnt in counts:
            groups.append(refs[pos:pos + cnt])
            pos += cnt
        ins, c_ins = groups[0], groups[1:1 + nc]
        outs, c_outs = groups[1 + nc], groups[2 + nc:2 + 2 * nc]
        scr, c_sems = groups[2 + 2 * nc], groups[3 + 2 * nc:]
        ids = [pl.program_id(d) for d in range(len(grid))]
        is_first = functools.reduce(jnp.logical_and, [i == 0 for i in ids])
        is_last = functools.reduce(jnp.logical_and, [i == g - 1 for i, g in zip(ids, grid)])

        @pl.when(is_first)
        def _():
            for q, c in enumerate(comms):
                c.start(c_ins[q], c_outs[q], c_sems[q])

        body(*ins, *outs, *scr)

        @pl.when(is_last)
        def _():
            for q, c in enumerate(comms):
                c.finish(c_ins[q], c_outs[q], c_sems[q])

    anyspec = pl.BlockSpec(memory_space=pl.ANY)
    c_arrays = [a for c in comms for a in c.arrays]
    c_shapes = [s for c in comms for s in c.out_shapes]
    res = pl.pallas_call(
        carried, name=name, grid=grid, in_specs=list(in_specs) + [anyspec] * len(c_arrays), out_specs=out_specs + [anyspec] * len(c_shapes),
        out_shape=out_shape + c_shapes, scratch_shapes=list(scratch_shapes) + [s for c in comms for s in c.scratch],
        input_output_aliases=aliases or {}, compiler_params=params)(*args, *c_arrays)
    pos = n_out
    for c in comms:
        c.results = list(res[pos:pos + len(c.out_shapes)])
        pos += len(c.out_shapes)
    return res[0] if single else list(res[:n_out])


def _run_exchange(comm, *, name):
    def body(*refs):
        n_ci, n_co = len(comm.arrays), len(comm.out_shapes)
        ins, outs, sems = refs[:n_ci], refs[n_ci:n_ci + n_co], refs[n_ci + n_co:]
        comm.start(ins, outs, sems)
        comm.finish(ins, outs, sems)

    anyspec = pl.BlockSpec(memory_space=pl.ANY)
    comm.results = list(pl.pallas_call(
        body, name=name, in_specs=[anyspec] * len(comm.arrays), out_specs=[anyspec] * len(comm.out_shapes),
        out_shape=list(comm.out_shapes), scratch_shapes=list(comm.scratch))(*comm.arrays))
    return comm.results


def _mm(a, b, *, dims="nn", bias=None, res=None, out_dtype=F32, name, tm=1024, tn=1024, tk=1024, comm=None, b_rows=None,
        out_window=None, into=None):
    if dims == "tn":
        k_dim, m_dim = a.shape
    else:
        m_dim, k_dim = a.shape
    row0, n_rows = b_rows if b_rows is not None else (0, b.shape[0])
    n_dim = n_rows if dims == "nt" else b.shape[1]
    assert dims == "nt" or n_rows == k_dim, (name, a.shape, b.shape, b_rows)
    tm, tn, tk = min(tm, m_dim), min(tn, n_dim), min(tk, k_dim)
    assert m_dim % tm == 0 and n_dim % tn == 0 and k_dim % tk == 0, (name, a.shape, b.shape)
    nk = k_dim // tk
    a_spec = pl.BlockSpec((tk, tm), lambda i, j, k: (k, i)) if dims == "tn" else pl.BlockSpec((tm, tk), lambda i, j, k: (i, k))
    if dims == "nt":
        assert row0 % tn == 0
        b_spec = pl.BlockSpec((tn, tk), lambda i, j, k: (row0 // tn + j, k))
    else:
        assert row0 % tk == 0
        b_spec = pl.BlockSpec((tk, tn), lambda i, j, k: (row0 // tk + k, j))
    in_specs, args = [a_spec, b_spec], [a, b]
    if bias is not None:
        in_specs.append(pl.BlockSpec((1, tn), lambda i, j, k: (0, j)))
        args.append(bias)
    if res is not None:
        in_specs.append(pl.BlockSpec((tm, tn), lambda i, j, k: (i, j)))
        args.append(res)
    dn = _DIMS[dims]

    def body(*refs):
        a_ref, b_ref = refs[0], refs[1]
        o_ref, acc_ref = refs[-2], refs[-1]
        k = pl.program_id(2)

        @pl.when(k == 0)
        def _():
            acc_ref[...] = jnp.zeros_like(acc_ref)

        acc_ref[...] += _dot(a_ref[...], b_ref[...], dn)

        @pl.when(k == nk - 1)
        def _():
            r = acc_ref[...]
            pos = 2
            if bias is not None:
                r = r + refs[pos][...]
                pos += 1
            if res is not None:
                r = r + refs[pos][...]
            o_ref[...] = r.astype(out_dtype)

    out_row0, out_rows = out_window if out_window is not None else (0, m_dim)
    assert out_row0 % tm == 0
    aliases = None
    if into is not None:
        assert into.shape == (out_rows, n_dim) and into.dtype == out_dtype
        in_specs.append(pl.BlockSpec(memory_space=pl.ANY))
        args.append(into)
        aliases = {len(args) - 1: 0}
    return _call(
        body, name=name, grid=(m_dim // tm, n_dim // tn, nk), in_specs=in_specs,
        out_specs=pl.BlockSpec((tm, tn), lambda i, j, k: (out_row0 // tm + i, j)),
        out_shape=jax.ShapeDtypeStruct((out_rows, n_dim), out_dtype), aliases=aliases,
        scratch_shapes=[pltpu.VMEM((tm, tn), F32)], sem=("parallel", "parallel", "arbitrary"), args=args, comm=comm)


def _norm_mm(x, nw, w, bias, *, name, tm=1024, tn=1024, comm=None, w_rows=None):
    t_dim, d_dim = x.shape
    transposed = w_rows is not None
    n_dim = w_rows if transposed else w.shape[1]
    tn = min(tn, n_dim)
    assert t_dim % tm == 0 and n_dim % tn == 0
    has_bias = bias is not None
    w_spec = pl.BlockSpec((tn, d_dim), lambda i, j: (j, 0)) if transposed else pl.BlockSpec((d_dim, tn), lambda i, j: (0, j))
    dn = NT if transposed else NN
    in_specs = [pl.BlockSpec((tm, d_dim), lambda i, j: (i, 0)), pl.BlockSpec((1, d_dim), lambda i, j: (0, 0)), w_spec]
    args = [x, nw, w]
    if has_bias:
        in_specs.append(pl.BlockSpec((1, tn), lambda i, j: (0, j)))
        args.append(bias)

    def body(*refs):
        x_ref, nw_ref, w_ref = refs[:3]
        o_ref, h_ref = refs[-2], refs[-1]

        @pl.when(pl.program_id(1) == 0)
        def _():
            xhat, _ = _rms(x_ref[...])
            h_ref[...] = (xhat * nw_ref[...]).astype(BF16)

        r = _dot(h_ref[...], w_ref[...], dn)
        if has_bias:
            r = r + refs[3][...]
        o_ref[...] = r

    return _call(
        body, name=name, grid=(t_dim // tm, n_dim // tn), in_specs=in_specs,
        out_specs=[pl.BlockSpec((tm, tn), lambda i, j: (i, j)), pl.BlockSpec((tm, d_dim), lambda i, j: (i, 0))],
        out_shape=[jax.ShapeDtypeStruct((t_dim, n_dim), F32), jax.ShapeDtypeStruct((t_dim, d_dim), BF16)],
        sem=("parallel", "arbitrary"), args=args, comm=comm)


def _norm_bwd(x, nw, dh, res, *, name, tm=256, comm=None):
    t_dim, d_dim = x.shape
    n_res = len(res)
    row = pl.BlockSpec((tm, d_dim), lambda i: (i, 0))
    vec = pl.BlockSpec((1, d_dim), lambda i: (0, 0))

    def body(*refs):
        x_ref, nw_ref, dh_ref = refs[:3]
        dx_ref, dnw_ref = refs[-2], refs[-1]
        xhat, r = _rms(x_ref[...])
        dh = dh_ref[...]
        dxhat = dh * nw_ref[...]
        dx = r * (dxhat - xhat * jnp.mean(dxhat * xhat, axis=-1, keepdims=True))
        for rr in refs[3:3 + n_res]:
            dx = dx + rr[...]
        dx_ref[...] = dx

        @pl.when(pl.program_id(0) == 0)
        def _():
            dnw_ref[...] = jnp.zeros_like(dnw_ref)

        dnw_ref[...] += jnp.sum(dh * xhat, axis=0, keepdims=True)

    return _call(
        body, name=name, grid=(t_dim // tm,), in_specs=[row, vec, row] + [row] * n_res,
        out_specs=[row, vec],
        out_shape=[jax.ShapeDtypeStruct((t_dim, d_dim), F32), jax.ShapeDtypeStruct((1, d_dim), F32)],
        sem=("arbitrary",), args=[x, nw, dh, *res], comm=comm)


def _colsum(x, *, name, tm=256):
    t_dim, n_dim = x.shape

    def body(x_ref, o_ref):
        @pl.when(pl.program_id(0) == 0)
        def _():
            o_ref[...] = jnp.zeros_like(o_ref)

        o_ref[...] += jnp.sum(x_ref[...], axis=0, keepdims=True)

    return pl.pallas_call(
        body, name=name, grid=(t_dim // tm,), in_specs=[pl.BlockSpec((tm, n_dim), lambda i: (i, 0))],
        out_specs=pl.BlockSpec((1, n_dim), lambda i: (0, 0)), out_shape=jax.ShapeDtypeStruct((1, n_dim), F32),
        compiler_params=_params("arbitrary"),
    )(x)


FFN_ROW_TILE = 512
FFN_SHARDS_PER_STEP = 2
FFN_STEPS = N_DEV // FFN_SHARDS_PER_STEP
FFN_STEP_COLS = FFN_SHARDS_PER_STEP * D_FF_SHARD


def _ffn_step_view(w):
    return w.reshape(FFN_STEPS, FFN_STEP_COLS, w.shape[-1])


def _ffn_specs(t_dim, d_dim):
    full = pl.BlockSpec((t_dim, d_dim), lambda j: (0, 0))
    wspec = pl.BlockSpec((None, FFN_STEP_COLS, d_dim), lambda j: (j, 0, 0))
    pre = pl.BlockSpec((None, t_dim, FFN_STEP_COLS), lambda j: (j, 0, 0))
    return full, wspec, pre


def _ffn_fwd(x, nw, wg, wu, wd, *, name, comm=None):
    t_dim, d_dim = x.shape
    n_tiles = t_dim // FFN_ROW_TILE

    def body(x_ref, nw_ref, wg_ref, wu_ref, wd_ref, o_ref, g_ref, u_ref, h_scr):
        j = pl.program_id(0)

        @pl.when(j == 0)
        def _():
            xhat, _ = _rms(x_ref[...])
            h_scr[...] = (xhat * nw_ref[...]).astype(BF16)
            o_ref[...] = jnp.zeros_like(o_ref)

        for t in range(n_tiles):
            rows = pl.ds(t * FFN_ROW_TILE, FFN_ROW_TILE)
            h = h_scr[rows, :]
            g = _dot(h, wg_ref[...], NT)
            u = _dot(h, wu_ref[...], NT)
            g_ref[rows, :] = g.astype(BF16)
            u_ref[rows, :] = u.astype(BF16)
            o_ref[rows, :] += _dot(g * _sigmoid(g) * u, wd_ref[...])

        @pl.when(j == FFN_STEPS - 1)
        def _():
            o_ref[...] = x_ref[...] + FFN_RES_WEIGHT * o_ref[...]

    full, wspec, pre = _ffn_specs(t_dim, d_dim)
    pre_shape = jax.ShapeDtypeStruct((FFN_STEPS, t_dim, FFN_STEP_COLS), BF16)
    return _call(
        body, name=name, grid=(FFN_STEPS,),
        in_specs=[full, pl.BlockSpec((1, d_dim), lambda j: (0, 0)), wspec, wspec, wspec],
        out_specs=[full, pre, pre], out_shape=[jax.ShapeDtypeStruct((t_dim, d_dim), F32), pre_shape, pre_shape],
        scratch_shapes=[pltpu.VMEM((t_dim, d_dim), BF16)],
        sem=("arbitrary",), args=[x, nw, _ffn_step_view(wg), _ffn_step_view(wu), _ffn_step_view(wd)], comm=comm)


def _ffn_bwd_prep(x, nw, dout, *, name, tm=256):
    t_dim, d_dim = x.shape
    row = pl.BlockSpec((tm, d_dim), lambda i: (i, 0))

    def body(x_ref, nw_ref, dout_ref, h_ref, dob_ref):
        xhat, _ = _rms(x_ref[...])
        h_ref[...] = (xhat * nw_ref[...]).astype(BF16)
        dob_ref[...] = (FFN_RES_WEIGHT * dout_ref[...]).astype(BF16)

    return pl.pallas_call(
        body, name=name, grid=(t_dim // tm,), in_specs=[row, pl.BlockSpec((1, d_dim), lambda i: (0, 0)), row],
        out_specs=[row, row], out_shape=[jax.ShapeDtypeStruct((t_dim, d_dim), BF16)] * 2,
        compiler_params=_params("parallel"),
    )(x, nw, dout)


def _ffn_bwd(h, dob, pre_g, pre_u, wg, wu, wd, *, name, comm=None):
    t_dim, d_dim = h.shape
    n_tiles = t_dim // FFN_ROW_TILE

    def body(h_ref, dob_ref, g_ref, u_ref, wg_ref, wu_ref, wd_ref, dh_ref, gg_ref, gu_ref, gd_ref, dwg_scr, dwu_scr, dwd_scr):
        @pl.when(pl.program_id(0) == 0)
        def _():
            dh_ref[...] = jnp.zeros_like(dh_ref)

        for t in range(n_tiles):
            rows = pl.ds(t * FFN_ROW_TILE, FFN_ROW_TILE)
            hh = h_ref[rows, :]
            do = dob_ref[rows, :]
            g = g_ref[rows, :].astype(F32)
            u = u_ref[rows, :].astype(F32)
            sg = _sigmoid(g)
            s = g * sg
            da = _dot(do, wd_ref[...], NT)
            dwd = _dot(s * u, do, TN)
            du = (da * s).astype(BF16)
            dg = (da * u * _dsilu(g, sg)).astype(BF16)
            dwg = _dot(dg, hh, TN)
            dwu = _dot(du, hh, TN)
            if t == 0:
                dwd_scr[...] = dwd
                dwg_scr[...] = dwg
                dwu_scr[...] = dwu
            else:
                dwd_scr[...] += dwd
                dwg_scr[...] += dwg
                dwu_scr[...] += dwu
            dh_ref[rows, :] += _dot(dg, wg_ref[...]) + _dot(du, wu_ref[...])
        gg_ref[...] = dwg_scr[...].astype(BF16)
        gu_ref[...] = dwu_scr[...].astype(BF16)
        gd_ref[...] = dwd_scr[...].astype(BF16)

    full, wspec, pre = _ffn_specs(t_dim, d_dim)
    gspec = pl.BlockSpec((None, FFN_STEP_COLS, d_dim), lambda j: (j, 0, 0), pipeline_mode=pl.Buffered(1))
    grad_shape = jax.ShapeDtypeStruct((FFN_STEPS, FFN_STEP_COLS, d_dim), BF16)
    dh, gg, gu, gd = _call(
        body, name=name, grid=(FFN_STEPS,),
        in_specs=[full, full, pre, pre, wspec, wspec, wspec], out_specs=[full, gspec, gspec, gspec],
        out_shape=[jax.ShapeDtypeStruct((t_dim, d_dim), F32)] + [grad_shape] * 3,
        scratch_shapes=[pltpu.VMEM((FFN_STEP_COLS, d_dim), F32)] * 3, sem=("arbitrary",), vmem_limit=FFN_BWD_VMEM_LIMIT_BYTES,
        args=[h, dob, pre_g, pre_u, _ffn_step_view(wg), _ffn_step_view(wu), _ffn_step_view(wd)], comm=comm)
    return dh, gg.reshape(wg.shape), gu.reshape(wu.shape), gd.reshape(wd.shape)


CONV_COLS = 256


def _shift_down(u, s, rows):
    return jnp.where(rows >= s, pltpu.roll(u, s, 0), 0.0)


def _shift_up(u, s, rows, t_dim):
    return jnp.where(rows < t_dim - s, pltpu.roll(u, t_dim - s, 0), 0.0)


def _conv_pre(u, w_ref, b_ref, rows):
    c = b_ref[...] + w_ref[CONV_WIDTH - 1:CONV_WIDTH, :] * u
    for k in range(CONV_WIDTH - 1):
        c = c + w_ref[k:k + 1, :] * _shift_down(u, CONV_WIDTH - 1 - k, rows)
    return c


def _conv_fwd(zx, cw, cb, *, name, comm=None):
    t_dim = zx.shape[0]
    off = D_INNER // CONV_COLS

    def body(u_ref, w_ref, b_ref, o_ref):
        rows = lax.broadcasted_iota(jnp.int32, (t_dim, CONV_COLS), 0)
        c = _conv_pre(u_ref[...], w_ref, b_ref, rows)
        o_ref[...] = c * _sigmoid(c)

    return _call(
        body, name=name, grid=(CONV_DIM // CONV_COLS,),
        in_specs=[pl.BlockSpec((t_dim, CONV_COLS), lambda j: (0, off + j)),
                  pl.BlockSpec((CONV_WIDTH, CONV_COLS), lambda j: (0, j)), pl.BlockSpec((1, CONV_COLS), lambda j: (0, j))],
        out_specs=pl.BlockSpec((t_dim, CONV_COLS), lambda j: (0, j)),
        out_shape=jax.ShapeDtypeStruct((t_dim, CONV_DIM), F32), sem=("parallel",), args=[zx, cw, cb], comm=comm)


def _conv_bwd(zx, cw, cb, dxs, db, dc, dzx, *, name, comm=None):
    t_dim = zx.shape[0]
    off = D_INNER // CONV_COLS
    n_xs = D_INNER // CONV_COLS
    n_b = GN // CONV_COLS

    def body(u_ref, w_ref, b_ref, dxs_ref, db_ref, dc_ref, dzx_in, dzx_ref, dw_ref, dbias_ref):
        j = pl.program_id(0)
        rows = lax.broadcasted_iota(jnp.int32, (t_dim, CONV_COLS), 0)
        u = u_ref[...]
        c = _conv_pre(u, w_ref, b_ref, rows)
        d = jnp.where(j < n_xs, dxs_ref[...], jnp.where(j < n_xs + n_b, db_ref[...], dc_ref[...]))
        dcv = d * _dsilu(c, _sigmoid(c))
        dpre = w_ref[CONV_WIDTH - 1:CONV_WIDTH, :] * dcv
        dw_ref[CONV_WIDTH - 1:CONV_WIDTH, :] = jnp.sum(dcv * u, axis=0, keepdims=True)
        for k in range(CONV_WIDTH - 1):
            s = CONV_WIDTH - 1 - k
            dpre = dpre + w_ref[k:k + 1, :] * _shift_up(dcv, s, rows, t_dim)
            dw_ref[k:k + 1, :] = jnp.sum(dcv * _shift_down(u, s, rows), axis=0, keepdims=True)
        dzx_ref[...] = dpre
        dbias_ref[...] = jnp.sum(dcv, axis=0, keepdims=True)

    blk = lambda n: pl.BlockSpec((t_dim, CONV_COLS), n)
    return _call(
        body, name=name, grid=(CONV_DIM // CONV_COLS,),
        in_specs=[blk(lambda j: (0, off + j)), pl.BlockSpec((CONV_WIDTH, CONV_COLS), lambda j: (0, j)),
                  pl.BlockSpec((1, CONV_COLS), lambda j: (0, j)),
                  blk(lambda j: (0, jnp.minimum(j, n_xs - 1))),
                  blk(lambda j: (0, jnp.clip(j - n_xs, 0, n_b - 1))),
                  blk(lambda j: (0, jnp.clip(j - n_xs - n_b, 0, n_b - 1))),
                  pl.BlockSpec(memory_space=pl.ANY)],
        out_specs=[blk(lambda j: (0, off + j)), pl.BlockSpec((CONV_WIDTH, CONV_COLS), lambda j: (0, j)),
                   pl.BlockSpec((1, CONV_COLS), lambda j: (0, j))],
        out_shape=[jax.ShapeDtypeStruct(dzx.shape, F32), jax.ShapeDtypeStruct((CONV_WIDTH, CONV_DIM), F32),
                   jax.ShapeDtypeStruct((1, CONV_DIM), F32)],
        aliases={6: 0}, sem=("parallel",), args=[zx, cw, cb, dxs, db, dc, dzx], comm=comm)


def _softplus_parts(x):
    e = jnp.exp(-jnp.abs(x))
    u = 1.0 + e
    log1p_e = jnp.where(u == 1.0, e, jnp.log(u) * e / jnp.where(u == 1.0, 1.0, u - 1.0))
    return jnp.maximum(x, 0.0) + log1p_e


def _dt_prep(dtr, dt_bias, a_log, *, name):
    def body(dtr_ref, bias_ref, alog_ref, dt_ref, a_ref):
        dt = _softplus_parts(dtr_ref[...] + bias_ref[...])
        dt_ref[...] = dt
        a_ref[...] = dt * (-jnp.exp(alog_ref[...]))

    return pl.pallas_call(body, name=name, out_shape=[jax.ShapeDtypeStruct(dtr.shape, F32)] * 2,
                          compiler_params=_params())(dtr, dt_bias, a_log)


def _dt_bwd(dtr, dt_bias, a_log, dt, ddt, da, *, name):
    def body(dtr_ref, bias_ref, alog_ref, dt_ref, ddt_ref, da_ref, ddtr_ref, dbias_ref, dalog_ref):
        a_neg = -jnp.exp(alog_ref[...])
        da_v = da_ref[...]
        ddt_tot = ddt_ref[...] + da_v * a_neg
        ddtr = ddt_tot * _sigmoid(dtr_ref[...] + bias_ref[...])
        ddtr_ref[...] = ddtr
        dbias_ref[...] = jnp.sum(ddtr, axis=0, keepdims=True)
        dalog_ref[...] = jnp.sum(da_v * dt_ref[...], axis=0, keepdims=True) * a_neg

    return pl.pallas_call(
        body, name=name,
        out_shape=[jax.ShapeDtypeStruct(dtr.shape, F32), jax.ShapeDtypeStruct((1, SSM_HEADS), F32),
                   jax.ShapeDtypeStruct((1, SSM_HEADS), F32)],
        compiler_params=_params())(dtr, dt_bias, a_log, dt, ddt, da)


GROUP_COLS = HEADS_PER_GROUP * SSM_HEAD_DIM
LANES = 128
HEADS_PER_LANE_BLOCK = LANES // SSM_HEAD_DIM


def _split3(x):
    hi = x.astype(BF16)
    r1 = x - hi.astype(F32)
    mid = r1.astype(BF16)
    lo = (r1 - mid.astype(F32)).astype(BF16)
    return hi, mid, lo


def _dot_select(a, b, dims=NN, data=0):
    out = None
    for part in _split3(a if data == 0 else b):
        lhs, rhs = (part, b.astype(BF16)) if data == 0 else (a.astype(BF16), part)
        t = lax.dot_general(lhs, rhs, dims, preferred_element_type=F32)
        out = t if out is None else out + t
    return out


def _group_sums(vals, expand):
    out = _dot_select(jnp.concatenate(vals, axis=0), expand, NT)
    return [out[i * CHUNK:(i + 1) * CHUNK] for i in range(len(vals))]


def _ssd_chunk_common(a_ref, dt_ref, b_ref, c_ref):
    row = lax.broadcasted_iota(jnp.int32, (CHUNK, CHUNK), 0)
    col = lax.broadcasted_iota(jnp.int32, (CHUNK, CHUNK), 1)
    causal = col <= row
    lower = causal.astype(F32)
    upper = (col >= row).astype(F32)
    head = lax.broadcasted_iota(jnp.int32, (HEADS_PER_GROUP, GROUP_COLS), 0)
    lane = lax.broadcasted_iota(jnp.int32, (HEADS_PER_GROUP, GROUP_COLS), 1)
    expand = ((lane >= head * SSM_HEAD_DIM) & (lane < (head + 1) * SSM_HEAD_DIM)).astype(F32)
    a = a_ref[...]
    cs = _dot_select(lower, a, data=1)
    cs_row = _dot_select(a, upper, TN)
    cs_x = _dot_select(cs, expand)
    dt_x = _dot_select(dt_ref[...], expand)
    e_out_x = jnp.exp(cs_x)
    e_st_x = jnp.exp(cs_x[CHUNK - 1:CHUNK, :] - cs_x)
    bc = b_ref[...]
    cc = c_ref[...]
    cb = _dot(cc, bc, NT)
    return causal, upper, expand.astype(BF16), cs, cs_row, dt_x, e_out_x, e_st_x, bc, cc, cb


def _head_decay(causal, cs, cs_row, h):
    return jnp.exp(jnp.where(causal, cs[:, h:h + 1] - cs_row[h:h + 1, :], NEG_BIG))


def _lane_block_head_masks():
    lane = lax.broadcasted_iota(jnp.int32, (CHUNK, LANES), 1)
    return [(lane >= i * SSM_HEAD_DIM) & (lane < (i + 1) * SSM_HEAD_DIM) for i in range(HEADS_PER_LANE_BLOCK)]


def _decay_state(dst_ref, old, new, cs):
    for h in range(HEADS_PER_GROUP):
        rows = slice(h * SSM_HEAD_DIM, (h + 1) * SSM_HEAD_DIM)
        dst_ref[rows, :] = jnp.exp(cs[CHUNK - 1:CHUNK, h:h + 1]) * old[rows, :] + new[rows, :]


def _ssd_fwd(xbc, dtg, ag, dgx, *, name, comm=None):
    t_dim = xbc.shape[0]

    def body(xs_ref, b_ref, c_ref, dt_ref, a_ref, d_ref, y_ref, st_ref, s_scr):
        @pl.when(pl.program_id(1) == 0)
        def _():
            s_scr[...] = jnp.zeros_like(s_scr)

        causal, _, _, cs, cs_row, dt_x, e_out_x, e_st_x, bc, cc, cb = _ssd_chunk_common(a_ref, dt_ref, b_ref, c_ref)
        masks = _lane_block_head_masks()
        xs = xs_ref[...]
        xdt_x = xs * dt_x
        prev = s_scr[...]
        st_ref[...] = prev
        y_off = e_out_x * _dot(cc, prev, NT) + xs * d_ref[...]
        for blk in range(GROUP_COLS // LANES):
            lanes = slice(blk * LANES, (blk + 1) * LANES)
            x_b = xdt_x[:, lanes].astype(BF16)
            acc = y_off[:, lanes]
            for i in range(HEADS_PER_LANE_BLOCK):
                m = cb * _head_decay(causal, cs, cs_row, blk * HEADS_PER_LANE_BLOCK + i)
                acc = acc + _dot(m, jnp.where(masks[i], x_b, jnp.zeros_like(x_b)))
            y_ref[:, lanes] = acc
        _decay_state(s_scr, prev, _dot(xdt_x * e_st_x, bc, TN), cs)

    xs = pl.BlockSpec((CHUNK, GROUP_COLS), lambda g, c: (c, g))
    bsp = pl.BlockSpec((CHUNK, SSM_STATE), lambda g, c: (c, D_INNER // SSM_STATE + g))
    csp = pl.BlockSpec((CHUNK, SSM_STATE), lambda g, c: (c, (D_INNER + GN) // SSM_STATE + g))
    per_head = pl.BlockSpec((None, CHUNK, HEADS_PER_GROUP), lambda g, c: (g, c, 0))
    dsk = pl.BlockSpec((None, 1, GROUP_COLS), lambda g, c: (g, 0, 0))
    return _call(
        body, name=name, grid=(SSM_GROUPS, N_CHUNKS),
        in_specs=[xs, bsp, csp, per_head, per_head, dsk],
        out_specs=[xs, pl.BlockSpec((None, GROUP_COLS, SSM_STATE), lambda g, c: (c, g, 0))],
        out_shape=[jax.ShapeDtypeStruct((t_dim, D_INNER), F32),
                   jax.ShapeDtypeStruct((N_CHUNKS, D_INNER, SSM_STATE), F32)],
        scratch_shapes=[pltpu.VMEM((GROUP_COLS, SSM_STATE), F32)],
        sem=("parallel", "arbitrary"), args=[xbc, xbc, xbc, dtg, ag, dgx], comm=comm)


def _ssd_bwd(xbc, dtg, ag, dgx, states, dy, *, name, comm=None):
    t_dim = xbc.shape[0]
    last = N_CHUNKS - 1

    def body(xs_ref, b_ref, c_ref, dt_ref, a_ref, d_ref, st_ref, dy_ref,
             dxs_ref, db_ref, dc_ref, ddt_ref, da_ref, dd_ref, ds_scr):
        @pl.when(pl.program_id(1) == 0)
        def _():
            ds_scr[...] = jnp.zeros_like(ds_scr)
            dd_ref[...] = jnp.zeros_like(dd_ref)

        causal, upper, expand, cs, cs_row, dt_x, e_out_x, e_st_x, bc, cc, cb = _ssd_chunk_common(a_ref, dt_ref, b_ref, c_ref)
        masks = _lane_block_head_masks()
        xs = xs_ref[...]
        dy_x = dy_ref[...]
        xdt_x = xs * dt_x
        prev = st_ref[...]
        d_s = ds_scr[...]
        g1_x = _dot(bc, d_s, NT)
        cp_x = _dot(cc, prev, NT)
        d_cb = jnp.zeros((CHUNK, CHUNK), F32)
        lane8 = lax.broadcasted_iota(jnp.int32, (CHUNK, HEADS_PER_GROUP), 1)
        sub8 = lax.broadcasted_iota(jnp.int32, (HEADS_PER_GROUP, CHUNK), 0)
        row_w = jnp.zeros((CHUNK, HEADS_PER_GROUP), F32)
        col_w = jnp.zeros((HEADS_PER_GROUP, CHUNK), F32)
        dxdt_blocks = []
        for blk in range(GROUP_COLS // LANES):
            lanes = slice(blk * LANES, (blk + 1) * LANES)
            dy_b = dy_x[:, lanes].astype(BF16)
            x_b = xdt_x[:, lanes].astype(BF16)
            acc_dx = jnp.zeros((CHUNK, LANES), F32)
            for i in range(HEADS_PER_LANE_BLOCK):
                h = blk * HEADS_PER_LANE_BLOCK + i
                decay = _head_decay(causal, cs, cs_row, h)
                m = cb * decay
                dy_h = jnp.where(masks[i], dy_b, jnp.zeros_like(dy_b))
                acc_dx = acc_dx + _dot(m, dy_h, TN)
                d_m = _dot(dy_h, x_b, NT)
                d_cb = d_cb + d_m * decay
                w = d_m * m
                row_w = jnp.where(lane8 == h, jnp.sum(w, axis=1, keepdims=True), row_w)
                col_w = jnp.where(sub8 == h, jnp.sum(w, axis=0, keepdims=True), col_w)
            dxdt_blocks.append(acc_dx)
        dxdt_x = jnp.concatenate(dxdt_blocks, axis=1) + e_st_x * g1_x
        dxs_ref[...] = dxdt_x * dt_x + dy_x * d_ref[...]
        dye = dy_x * e_out_x
        xde = xdt_x * e_st_x
        ddt, y_off, tl, dskip = _group_sums([dxdt_x * xs, dye * cp_x, xde * g1_x, dy_x * xs], expand)
        ddt_ref[...] = ddt
        dd_ref[...] += jnp.sum(dskip, axis=0, keepdims=True)
        sp = None
        for part in _split3(d_s * prev):
            t = lax.dot_general(expand, part, NN, preferred_element_type=F32)
            sp = t if sp is None else sp + t
        last_col = jnp.exp(cs_row[:, CHUNK - 1:CHUNK]) * jnp.sum(sp, axis=1, keepdims=True)
        eye = lax.broadcasted_iota(jnp.int32, (HEADS_PER_GROUP, HEADS_PER_GROUP), 0) == lax.broadcasted_iota(
            jnp.int32, (HEADS_PER_GROUP, HEADS_PER_GROUP), 1)
        last_row = jnp.sum(jnp.where(eye, last_col, 0.0), axis=0, keepdims=True) + jnp.sum(tl, axis=0, keepdims=True)
        is_last = lax.broadcasted_iota(jnp.int32, (CHUNK, 1), 0) == CHUNK - 1
        d_cs = row_w + y_off - tl + jnp.where(is_last, last_row, 0.0)
        da_ref[...] = _dot_select(upper, d_cs, data=1) - _dot_select(upper, col_w, NT, data=1)
        dc_ref[...] = _dot(d_cb, bc) + _dot(dye, prev)
        db_ref[...] = _dot(d_cb, cc, TN) + _dot(xde, d_s)
        _decay_state(ds_scr, d_s, _dot(dye, cc, TN), cs)

    rev = lambda c: last - c
    xs = pl.BlockSpec((CHUNK, GROUP_COLS), lambda g, c: (rev(c), g))
    bsp = pl.BlockSpec((CHUNK, SSM_STATE), lambda g, c: (rev(c), D_INNER // SSM_STATE + g))
    csp = pl.BlockSpec((CHUNK, SSM_STATE), lambda g, c: (rev(c), (D_INNER + GN) // SSM_STATE + g))
    per_head = pl.BlockSpec((None, CHUNK, HEADS_PER_GROUP), lambda g, c: (g, rev(c), 0))
    dsk = pl.BlockSpec((None, 1, GROUP_COLS), lambda g, c: (g, 0, 0))
    dsum = pl.BlockSpec((None, 1, HEADS_PER_GROUP), lambda g, c: (g, 0, 0))
    st = pl.BlockSpec((None, GROUP_COLS, SSM_STATE), lambda g, c: (rev(c), g, 0))
    grp = pl.BlockSpec((CHUNK, SSM_STATE), lambda g, c: (rev(c), g))
    return _call(
        body, name=name, grid=(SSM_GROUPS, N_CHUNKS),
        in_specs=[xs, bsp, csp, per_head, per_head, dsk, st, xs],
        out_specs=[xs, grp, grp, per_head, per_head, dsum],
        out_shape=[jax.ShapeDtypeStruct((t_dim, D_INNER), F32), jax.ShapeDtypeStruct((t_dim, GN), F32),
                   jax.ShapeDtypeStruct((t_dim, GN), F32),
                   jax.ShapeDtypeStruct((SSM_GROUPS, t_dim, HEADS_PER_GROUP), F32),
                   jax.ShapeDtypeStruct((SSM_GROUPS, t_dim, HEADS_PER_GROUP), F32),
                   jax.ShapeDtypeStruct((SSM_GROUPS, 1, HEADS_PER_GROUP), F32)],
        scratch_shapes=[pltpu.VMEM((GROUP_COLS, SSM_STATE), F32)],
        sem=("parallel", "arbitrary"), args=[xbc, xbc, xbc, dtg, ag, dgx, states, dy], comm=comm)


NORM_GROUP = D_INNER // SSM_GROUPS


def _gate_norm_fwd(y, zx, nw, *, name, tm=256):
    t_dim = y.shape[0]
    row = pl.BlockSpec((tm, D_INNER), lambda i: (i, 0))

    def body(y_ref, z_ref, nw_ref, o_ref):
        z = z_ref[...]
        yz = y_ref[...] * (z * _sigmoid(z))
        for g in range(SSM_GROUPS):
            cols = slice(g * NORM_GROUP, (g + 1) * NORM_GROUP)
            yhat, _ = _rms(yz[:, cols])
            o_ref[:, cols] = (yhat * nw_ref[:, cols]).astype(BF16)

    return pl.pallas_call(
        body, name=name, grid=(t_dim // tm,), in_specs=[row, row, pl.BlockSpec((1, D_INNER), lambda i: (0, 0))],
        out_specs=row, out_shape=jax.ShapeDtypeStruct((t_dim, D_INNER), BF16),
        compiler_params=_params("parallel"),
    )(y, zx, nw)


def _gate_norm_bwd(y, zx, nw, dyn, *, name, tm=256):
    t_dim = y.shape[0]
    row = pl.BlockSpec((tm, D_INNER), lambda i: (i, 0))
    vec = pl.BlockSpec((1, D_INNER), lambda i: (0, 0))

    def body(y_ref, z_ref, nw_ref, dyn_ref, dy_ref, dz_ref, dnw_ref):
        @pl.when(pl.program_id(0) == 0)
        def _():
            dnw_ref[...] = jnp.zeros_like(dnw_ref)

        z = z_ref[...]
        yv = y_ref[...]
        sg = _sigmoid(z)
        silu_z = z * sg
        yz = yv * silu_z
        dyn_v = dyn_ref[...]
        for g in range(SSM_GROUPS):
            cols = slice(g * NORM_GROUP, (g + 1) * NORM_GROUP)
            yhat, r = _rms(yz[:, cols])
            dn = dyn_v[:, cols]
            dnw_ref[:, cols] += jnp.sum(dn * yhat, axis=0, keepdims=True)
            dyhat = dn * nw_ref[:, cols]
            dyz = r * (dyhat - yhat * jnp.mean(dyhat * yhat, axis=-1, keepdims=True))
            dy_ref[:, cols] = dyz * silu_z[:, cols]
            dz_ref[:, cols] = dyz * yv[:, cols] * _dsilu(z[:, cols], sg[:, cols])

    return pl.pallas_call(
        body, name=name, grid=(t_dim // tm,), in_specs=[row, row, vec, row],
        out_specs=[row, row, vec],
        out_shape=[jax.ShapeDtypeStruct((t_dim, D_INNER), F32), jax.ShapeDtypeStruct((t_dim, ZX_DIM), F32),
                   jax.ShapeDtypeStruct((1, D_INNER), F32)],
        compiler_params=_params("arbitrary"),
    )(y, zx, nw, dyn)


def _rope(t, cos2, sin2, *, name, tm=256):
    t_dim, width = t.shape
    half = ATT_HEAD_DIM // 2
    reps = width // 128

    def body(t_ref, cos_ref, sin_ref, o_ref):
        x = t_ref[...]
        lane = lax.broadcasted_iota(jnp.int32, (tm, width), 1)
        first = (lane % ATT_HEAD_DIM) < half
        rot = jnp.where(first, -pltpu.roll(x, width - half, 1), pltpu.roll(x, half, 1))
        o_ref[...] = x * jnp.tile(cos_ref[...], (1, reps)) + rot * jnp.tile(sin_ref[...], (1, reps))

    row = pl.BlockSpec((tm, width), lambda i: (i, 0))
    tab = pl.BlockSpec((tm, 128), lambda i: (i, 0))
    return pl.pallas_call(
        body, name=name, grid=(t_dim // tm,), in_specs=[row, tab, tab], out_specs=row,
        out_shape=jax.ShapeDtypeStruct((t_dim, width), F32), compiler_params=_params("parallel"),
    )(t, cos2, sin2)


HEADS_PER_LANE_TILE = LANES // ATT_HEAD_DIM
STACKED_ROWS = Q_PER_KV * WINDOW


def _att_half_masks():
    lane = lax.broadcasted_iota(jnp.int32, (WINDOW, LANES), 1)
    return [(lane >= i * ATT_HEAD_DIM) & (lane < (i + 1) * ATT_HEAD_DIM) for i in range(HEADS_PER_LANE_TILE)]


def _att_stack_heads(ref, kvh, masks):
    parts = []
    for g in range(Q_PER_KV):
        h = kvh * Q_PER_KV + g
        blk = ref[:, (h // HEADS_PER_LANE_TILE) * LANES:(h // HEADS_PER_LANE_TILE + 1) * LANES]
        parts.append(jnp.where(masks[h % HEADS_PER_LANE_TILE], blk, jnp.zeros_like(blk)))
    return jnp.concatenate(parts, axis=0)


def _att_kv_tile(ref, kvh, masks):
    blk = ref[:, (kvh // HEADS_PER_LANE_TILE) * LANES:(kvh // HEADS_PER_LANE_TILE + 1) * LANES]
    return jnp.where(masks[kvh % HEADS_PER_LANE_TILE], blk, pltpu.roll(blk, ATT_HEAD_DIM, 1)).astype(BF16)


def _att_stacked_masks(n):
    row = lax.bitwise_and(lax.broadcasted_iota(jnp.int32, (STACKED_ROWS, WINDOW), 0), WINDOW - 1)
    col = lax.broadcasted_iota(jnp.int32, (STACKED_ROWS, WINDOW), 1)
    return col <= row, (col > row) & (n > 0)


def _att_stack_columns(ref, kvh, rows):
    cols = [ref[:, kvh * Q_PER_KV + g:kvh * Q_PER_KV + g + 1] for g in range(Q_PER_KV)]
    return jnp.concatenate([jnp.broadcast_to(c, (rows, 1)) for c in cols], axis=0)


def _att_scores(q4, k_tile, mask):
    return jnp.where(mask, _dot(q4, k_tile, NT) * ATT_SCALE, NEG_BIG)


def _att_unstack(x4, kvh, masks, tiles):
    for g in range(Q_PER_KV):
        h = kvh * Q_PER_KV + g
        piece = x4[g * WINDOW:(g + 1) * WINDOW]
        t = h // HEADS_PER_LANE_TILE
        tiles[t] = piece if h % HEADS_PER_LANE_TILE == 0 else jnp.where(masks[1], piece, tiles[t])


def _attn_fwd(q, k, v, sinks, *, name, comm=None):
    t_dim = q.shape[0]

    def body(q_ref, kc_ref, kp_ref, vc_ref, vp_ref, s_ref, o_ref, l_ref):
        n = pl.program_id(0)
        masks = _att_half_masks()
        mask_c, mask_p = _att_stacked_masks(n)
        out_tiles = [None] * (D_MODEL // LANES)
        for kvh in range(N_KV_HEADS):
            q4 = _att_stack_heads(q_ref, kvh, masks).astype(BF16)
            kc, kp = _att_kv_tile(kc_ref, kvh, masks), _att_kv_tile(kp_ref, kvh, masks)
            vc, vp = _att_kv_tile(vc_ref, kvh, masks), _att_kv_tile(vp_ref, kvh, masks)
            sc = _att_scores(q4, kc, mask_c)
            sp = _att_scores(q4, kp, mask_p)
            sink = _att_stack_columns(s_ref, kvh, WINDOW)
            m = jnp.maximum(jnp.maximum(jnp.max(sc, axis=1, keepdims=True), jnp.max(sp, axis=1, keepdims=True)), sink)
            pc = jnp.exp(sc - m)
            pp = jnp.exp(sp - m)
            den = jnp.sum(pc, axis=1, keepdims=True) + jnp.sum(pp, axis=1, keepdims=True) + jnp.exp(sink - m)
            _att_unstack((_dot(pc, vc) + _dot(pp, vp)) / den, kvh, masks, out_tiles)
            lse4 = m + jnp.log(den)
            for g in range(Q_PER_KV):
                h = kvh * Q_PER_KV + g
                l_ref[:, h:h + 1] = lse4[g * WINDOW:(g + 1) * WINDOW]
        for t, tile in enumerate(out_tiles):
            o_ref[:, t * LANES:(t + 1) * LANES] = tile

    cur = lambda w: pl.BlockSpec((WINDOW, w), lambda n: (n, 0))
    prv = lambda w: pl.BlockSpec((WINDOW, w), lambda n: (jnp.maximum(n - 1, 0), 0))
    return _call(
        body, name=name, grid=(t_dim // WINDOW,),
        in_specs=[cur(D_MODEL), cur(KV_DIM), prv(KV_DIM), cur(KV_DIM), prv(KV_DIM), pl.BlockSpec((1, N_Q_HEADS), lambda n: (0, 0))],
        out_specs=[cur(D_MODEL), cur(N_Q_HEADS)],
        out_shape=[jax.ShapeDtypeStruct((t_dim, D_MODEL), F32), jax.ShapeDtypeStruct((t_dim, N_Q_HEADS), F32)],
        sem=("parallel",), args=[q, k, k, v, v, sinks], comm=comm)


def _attn_bwd(q, k, v, sinks, o, lse, do, *, name, comm=None):
    t_dim = q.shape[0]

    def body(q_ref, kc_ref, kp_ref, vc_ref, vp_ref, s_ref, o_ref, l_ref, do_ref, dq_ref, dk_ref, dv_ref, dsink_ref):
        n = pl.program_id(0)

        @pl.when(n == 0)
        def _():
            dk_ref[...] = jnp.zeros_like(dk_ref)
            dv_ref[...] = jnp.zeros_like(dv_ref)
            dsink_ref[...] = jnp.zeros_like(dsink_ref)

        masks = _att_half_masks()
        mask_c, mask_p = _att_stacked_masks(n)
        lane_row = lax.broadcasted_iota(jnp.int32, (1, N_Q_HEADS), 1)
        rows_c = pl.ds(pl.multiple_of(n * WINDOW, WINDOW), WINDOW)
        rows_p = pl.ds(pl.multiple_of(jnp.maximum(n - 1, 0) * WINDOW, WINDOW), WINDOW)
        dsink = jnp.zeros((1, N_Q_HEADS), F32)
        dq_tiles = [None] * (D_MODEL // LANES)
        kv_tiles = KV_DIM // LANES
        dkc_tiles, dkp_tiles, dvc_tiles, dvp_tiles = ([None] * kv_tiles for _ in range(4))

        def place(tiles, kvh, x):
            folded = x + pltpu.roll(x, ATT_HEAD_DIM, 1)
            t = kvh // HEADS_PER_LANE_TILE
            tiles[t] = folded if kvh % HEADS_PER_LANE_TILE == 0 else jnp.where(masks[1], folded, tiles[t])

        for kvh in range(N_KV_HEADS):
            q4 = _att_stack_heads(q_ref, kvh, masks).astype(BF16)
            do4 = _att_stack_heads(do_ref, kvh, masks)
            o4 = _att_stack_heads(o_ref, kvh, masks)
            kc, kp = _att_kv_tile(kc_ref, kvh, masks), _att_kv_tile(kp_ref, kvh, masks)
            vc, vp = _att_kv_tile(vc_ref, kvh, masks), _att_kv_tile(vp_ref, kvh, masks)
            l4 = _att_stack_columns(l_ref, kvh, WINDOW)
            pc = jnp.exp(_att_scores(q4, kc, mask_c) - l4)
            pp = jnp.exp(_att_scores(q4, kp, mask_p) - l4)
            delta = jnp.sum(do4 * o4, axis=1, keepdims=True)
            do4b = do4.astype(BF16)
            dsc = pc * (_dot(do4b, vc, NT) - delta)
            dsp = pp * (_dot(do4b, vp, NT) - delta)
            _att_unstack((_dot(dsc, kc) + _dot(dsp, kp)) * ATT_SCALE, kvh, masks, dq_tiles)
            place(dkc_tiles, kvh, _dot(dsc, q4, TN) * ATT_SCALE)
            place(dkp_tiles, kvh, _dot(dsp, q4, TN) * ATT_SCALE)
            place(dvc_tiles, kvh, _dot(pc, do4b, TN))
            place(dvp_tiles, kvh, _dot(pp, do4b, TN))
            p_sink = jnp.exp(_att_stack_columns(s_ref, kvh, WINDOW) - l4) * delta
            for g in range(Q_PER_KV):
                h = kvh * Q_PER_KV + g
                dsink = jnp.where(lane_row == h, -jnp.sum(p_sink[g * WINDOW:(g + 1) * WINDOW], axis=0, keepdims=True), dsink)
        for t, tile in enumerate(dq_tiles):
            dq_ref[:, t * LANES:(t + 1) * LANES] = tile
        for t in range(kv_tiles):
            lanes = slice(t * LANES, (t + 1) * LANES)
            dk_ref[rows_c, lanes] += dkc_tiles[t]
            dk_ref[rows_p, lanes] += dkp_tiles[t]
            dv_ref[rows_c, lanes] += dvc_tiles[t]
            dv_ref[rows_p, lanes] += dvp_tiles[t]
        dsink_ref[...] += dsink

    cur = lambda w: pl.BlockSpec((WINDOW, w), lambda n: (n, 0))
    prv = lambda w: pl.BlockSpec((WINDOW, w), lambda n: (jnp.maximum(n - 1, 0), 0))
    whole = pl.BlockSpec((t_dim, KV_DIM), lambda n: (0, 0))
    svec = pl.BlockSpec((1, N_Q_HEADS), lambda n: (0, 0))
    return _call(
        body, name=name, grid=(t_dim // WINDOW,),
        in_specs=[cur(D_MODEL), cur(KV_DIM), prv(KV_DIM), cur(KV_DIM), prv(KV_DIM), svec, cur(D_MODEL), cur(N_Q_HEADS), cur(D_MODEL)],
        out_specs=[cur(D_MODEL), whole, whole, svec],
        out_shape=[jax.ShapeDtypeStruct((t_dim, D_MODEL), F32), jax.ShapeDtypeStruct((t_dim, KV_DIM), F32),
                   jax.ShapeDtypeStruct((t_dim, KV_DIM), F32), jax.ShapeDtypeStruct((1, N_Q_HEADS), F32)],
        sem=("arbitrary",), args=[q, k, k, v, v, sinks, o, lse, do], comm=comm)


def _loss_head(x, nw, target, *, name, tm=256):
    t_dim, d_dim = x.shape
    row = pl.BlockSpec((tm, d_dim), lambda i: (i, 0))
    vec = pl.BlockSpec((1, d_dim), lambda i: (0, 0))

    def body(x_ref, nw_ref, tgt_ref, loss_ref, dx_ref, dnw_ref):
        @pl.when(pl.program_id(0) == 0)
        def _():
            loss_ref[...] = jnp.zeros_like(loss_ref)
            dnw_ref[...] = jnp.zeros_like(dnw_ref)

        xhat, r = _rms(x_ref[...])
        err = xhat * nw_ref[...] - tgt_ref[...]
        loss_ref[...] += 0.5 * _sum_all(jnp.mean(err * err, axis=-1, keepdims=True))
        dy = err * (1.0 / d_dim)
        dnw_ref[...] += jnp.sum(dy * xhat, axis=0, keepdims=True)
        dxhat = dy * nw_ref[...]
        dx_ref[...] = r * (dxhat - xhat * jnp.mean(dxhat * xhat, axis=-1, keepdims=True))

    return pl.pallas_call(
        body, name=name, grid=(t_dim // tm,), in_specs=[row, vec, row],
        out_specs=[pl.BlockSpec((1, 1), lambda i: (0, 0)), row, vec],
        out_shape=[jax.ShapeDtypeStruct((1, 1), F32), jax.ShapeDtypeStruct((t_dim, d_dim), F32),
                   jax.ShapeDtypeStruct((1, d_dim), F32)],
        compiler_params=_params("arbitrary"),
    )(x, nw, target)


def _rope_tables():
    pos = jnp.arange(SEQ, dtype=F32)
    inv = 1.0 / (ROPE_THETA ** (jnp.arange(0, ATT_HEAD_DIM, 2, dtype=F32) / ATT_HEAD_DIM))
    ang = pos[:, None] * inv[None, :]
    cos, sin = jnp.cos(ang), jnp.sin(ang)
    return jnp.tile(cos, (1, 4)), jnp.tile(sin, (1, 4))


def _to_groups(t):
    return t.reshape(t.shape[0], SSM_GROUPS, HEADS_PER_GROUP).transpose(1, 0, 2)


def _from_groups(t):
    return t.transpose(1, 0, 2).reshape(t.shape[1], SSM_HEADS)


def _forward_backward(x0, target, net):
    w = net.w
    nw = [[w("norm_w")[l, i][None, :] for i in range(3)] for l in range(2)]
    cos2, sin2 = _rope_tables()
    ffn_norm = [nw[0][0], nw[0][2], nw[1][0], nw[1][2]]

    ffn_pre = {}

    def ffn_f(x, blk):
        name = f"ffn_fwd{blk}"
        out, *ffn_pre[blk] = _ffn_fwd(x, ffn_norm[blk], w(f"gate{blk}"), w(f"up{blk}"), w(f"down{blk}"), name=name,
                                      comm=net.carry(name))
        return out

    x1 = ffn_f(x0, 0)
    zx, h1 = _norm_mm(x1, nw[0][1], w("w_in_t"), None, w_rows=ZX_DIM, name="ssm_in_proj", comm=net.carry("ssm_in_proj"))
    dtr = _mm(h1, w("w_in_t"), dims="nt", b_rows=(ZX_DIM, SSM_HEADS), name="ssm_dt_proj")
    xbc = _conv_fwd(zx, w("conv_w"), w("conv_b"), name="ssm_conv_fwd", comm=net.carry("ssm_conv_fwd"))
    dt, a_dt = _dt_prep(dtr, w("dt_bias"), w("a_log"), name="ssm_dt_prep")
    dtg, ag = _to_groups(dt), _to_groups(a_dt)
    dg = jnp.repeat(w("d_skip").reshape(SSM_GROUPS, 1, HEADS_PER_GROUP), SSM_HEAD_DIM, axis=2)
    y_ssd, states = _ssd_fwd(xbc, dtg, ag, dg, name="ssd_fwd", comm=net.carry("ssd_fwd"))
    yn = _gate_norm_fwd(y_ssd, zx, w("ssm_norm_w"), name="ssm_gate_norm_fwd")
    x2 = _mm(yn, w("wout"), res=x1, name="ssm_out_proj", comm=net.carry("ssm_out_proj"))
    x3 = ffn_f(x2, 1)
    k_pre, hk = _norm_mm(x3, w("kv_norm_w"), w("wk"), w("b_k"), name="k_proj")
    v = _mm(hk, w("wv"), bias=w("b_v"), name="v_proj")
    k_rot = _rope(k_pre, cos2, sin2, name="k_rope")
    x4 = ffn_f(x3, 2)
    q_pre, h4 = _norm_mm(x4, nw[1][1], w("wq"), w("b_q"), name="q_proj")
    q_rot = _rope(q_pre, cos2, sin2, name="q_rope")
    att, lse = _attn_fwd(q_rot, k_rot, v, w("sinks"), name="attn_fwd", comm=net.carry("attn_fwd"))
    x5 = _mm(att, w("wo"), bias=w("b_o"), res=x4, name="attn_out_proj")
    x6 = ffn_f(x5, 3)
    loss, dx6, d_final = _loss_head(x6, w("final_norm_w"), target, name="loss_head")

    d_norm = [[None] * 3 for _ in range(2)]

    def ffn_b(x, dout, blk):
        h, dob = _ffn_bwd_prep(x, ffn_norm[blk], dout, name=f"ffn_bwd_prep{blk}")
        name = f"ffn_bwd{blk}"
        dh, gg, gu, gd = _ffn_bwd(h, dob, *ffn_pre[blk], w(f"gate{blk}"), w(f"up{blk}"), w(f"down{blk}"), name=name,
                                  comm=net.carry(name))
        net.give(f"gate{blk}", gg)
        net.give(f"up{blk}", gu)
        net.give(f"down{blk}", gd)
        return _norm_bwd(x, ffn_norm[blk], dh, [dout], name=f"ffn_norm_bwd{blk}", comm=net.carry(f"ffn_norm_bwd{blk}"))

    by_rows = lambda g: g.reshape(N_DEV, g.shape[0] // N_DEV, g.shape[1])
    dx5, d_norm[1][2] = ffn_b(x5, dx6, 3)
    d_att = _mm(dx5, w("wo"), dims="nt", name="attn_out_proj_dx", comm=net.carry("attn_out_proj_dx"))
    net.give("w_o", by_rows(_mm(att, dx5, dims="tn", out_dtype=BF16, name="attn_out_proj_dw")))
    d_bo = _colsum(dx5, name="attn_bo_grad")
    dq_rot, dk_rot, dv, d_sinks = _attn_bwd(q_rot, k_rot, v, w("sinks"), att, lse, d_att, name="attn_bwd", comm=net.carry("attn_bwd"))
    dq = _rope(dq_rot, cos2, -sin2, name="q_rope_bwd")
    dk = _rope(dk_rot, cos2, -sin2, name="k_rope_bwd")
    dh4 = _mm(dq, w("wq"), dims="nt", name="q_proj_dx")
    net.give("w_q", by_rows(_mm(h4, dq, dims="tn", out_dtype=BF16, name="q_proj_dw")))
    d_bq = _colsum(dq, name="attn_bq_grad")
    dx4, d_norm[1][1] = _norm_bwd(x4, nw[1][1], dh4, [dx5], name="attn_norm_bwd")
    dx3a, d_norm[1][0] = ffn_b(x3, dx4, 2)
    dhk = _mm(dk, w("wk"), dims="nt", name="k_proj_dx", comm=net.carry("k_proj_dx"))
    dhk = _mm(dv, w("wv"), dims="nt", res=dhk, name="v_proj_dx")
    net.give("w_k", by_rows(_mm(hk, dk, dims="tn", out_dtype=BF16, name="k_proj_dw")))
    net.give("w_v", by_rows(_mm(hk, dv, dims="tn", out_dtype=BF16, name="v_proj_dw")))
    d_bk = _colsum(dk, name="bk_grad")
    d_bv = _colsum(dv, name="bv_grad")
    dx3, d_kvn = _norm_bwd(x3, w("kv_norm_w"), dhk, [dx3a], name="kv_norm_bwd")
    dx2, d_norm[0][2] = ffn_b(x2, dx3, 1)
    d_yn = _mm(dx2, w("wout"), dims="nt", name="ssm_out_proj_dx", comm=net.carry("ssm_out_proj_dx"))
    net.give("w_out", by_rows(_mm(yn, dx2, dims="tn", out_dtype=BF16, name="ssm_out_proj_dw")))
    dy_ssd, dzx, d_ssm_norm = _gate_norm_bwd(y_ssd, zx, w("ssm_norm_w"), d_yn, name="ssm_gate_norm_bwd")
    dxs, d_b, d_c, ddtg, dag, ddg = _ssd_bwd(xbc, dtg, ag, dg, states, dy_ssd, name="ssd_bwd", comm=net.carry("ssd_bwd"))
    dzx, d_conv_w, d_conv_b = _conv_bwd(zx, w("conv_w"), w("conv_b"), dxs, d_b, d_c, dzx, name="ssm_conv_bwd",
                                        comm=net.carry("ssm_conv_bwd"))
    ddtr, d_dt_bias, d_a_log = _dt_bwd(dtr, w("dt_bias"), w("a_log"), dt, _from_groups(ddtg), _from_groups(dag), name="ssm_dt_bwd")
    dh1 = _mm(dzx, w("w_in_t"), b_rows=(0, ZX_DIM), name="ssm_in_proj_dx")
    dh1 = _mm(ddtr, w("w_in_t"), b_rows=(ZX_DIM, SSM_HEADS), res=dh1, name="ssm_dt_proj_dx")
    in_rows = N_DEV * IN_PROJ_SHARD
    g_in = _mm(dzx, h1, dims="tn", out_dtype=BF16, out_window=(0, in_rows), name="ssm_in_proj_dw")
    g_in = _mm(ddtr, h1, dims="tn", out_dtype=BF16, out_window=(ZX_DIM, in_rows), into=g_in, name="ssm_dt_proj_dw")
    net.give("w_in", g_in.reshape(N_DEV, IN_PROJ_SHARD, D_MODEL))
    dx1, d_norm[0][1] = _norm_bwd(x1, nw[0][1], dh1, [dx2], name="ssm_norm_bwd", comm=net.carry("ssm_norm_bwd"))
    dx0, d_norm[0][0] = ffn_b(x0, dx1, 0)

    small = {"norm_w": jnp.concatenate([d_norm[l][i] for l in range(2) for i in range(3)], axis=0),
             "ssm_conv_w": d_conv_w, "ssm_conv_b": d_conv_b, "ssm_dt_bias": d_dt_bias, "ssm_a_log": d_a_log,
             "ssm_d": ddg.reshape(1, SSM_HEADS), "ssm_norm_w": d_ssm_norm, "kv_norm_w": d_kvn,
             "b_k": d_bk, "b_v": d_bv, "attn_b_q": d_bq, "attn_sinks": d_sinks, "attn_b_o": d_bo, "final_norm_w": d_final}
    return loss, dx0, small


BLOCK_BYTES = 1 << 20


def _row_tile(rows, cols):
    for t in (512, 256, 128, 64, 32, 16):
        if rows % t == 0 and t * cols * 4 <= BLOCK_BYTES:
            return t
    return rows


def _cast_bf16(x, *, name):
    n_blk, rows, cols = x.shape
    tm = rows if rows * cols * 4 <= 2 * BLOCK_BYTES else _row_tile(rows, cols)
    spec = pl.BlockSpec((None, tm, cols), lambda b, i: (b, i, 0))

    def body(x_ref, o_ref):
        o_ref[...] = x_ref[...].astype(BF16)

    return pl.pallas_call(body, name=name, grid=(n_blk, rows // tm), in_specs=[spec], out_specs=spec,
                          out_shape=jax.ShapeDtypeStruct(x.shape, BF16), compiler_params=_params("parallel", "parallel"))(x)


def _pair_add(grad, theirs, *, name):
    n_slots, rows, cols = theirs.shape
    tm = rows if rows * cols * 4 <= 2 * BLOCK_BYTES else _row_tile(rows, cols)

    def body(g_ref, t_ref, o_ref):
        mine = jnp.where(lax.axis_index("c") == 0, g_ref[0].astype(F32), g_ref[1].astype(F32))
        o_ref[...] = (mine + t_ref[...].astype(F32)).astype(BF16)

    spec = pl.BlockSpec((None, tm, cols), lambda s, i: (s, i, 0))
    return pl.pallas_call(
        body, name=name, grid=(n_slots, rows // tm),
        in_specs=[pl.BlockSpec((None, 2, tm, cols), lambda s, i: (s, 0, i, 0)), spec], out_specs=spec,
        out_shape=jax.ShapeDtypeStruct(theirs.shape, BF16), compiler_params=_params("parallel", "parallel"),
    )(grad.reshape((n_slots, 2, rows, cols)), theirs)


def _adam_update(g, w, m, v):
    m = ADAM_B1 * m + (1.0 - ADAM_B1) * g
    v = ADAM_B2 * v + (1.0 - ADAM_B2) * (g * g)
    m_hat = m / (1.0 - ADAM_B1 ** ADAM_STEP)
    v_hat = v / (1.0 - ADAM_B2 ** ADAM_STEP)
    delta = -ADAM_LR * (m_hat / (jnp.sqrt(v_hat) + ADAM_EPS) + ADAM_WD * w)
    return delta, m, v


def _adamw(parts, w, m, v, first_blk, prev, *, name, comm=None):
    n_blk, rows, cols = w.shape
    tm = _row_tile(rows, cols)
    n_tiles = rows // tm
    spec = pl.BlockSpec((None, tm, cols), lambda b, i: (first_blk + b, i, 0))
    n_prev, n_here = len(prev), len(parts)
    n_parts = parts[0].shape[0]

    def part_spec(q):
        return pl.BlockSpec((n_parts, tm, cols), lambda b, i: (0, jnp.where(b < q, 0, jnp.where(b == q, i, n_tiles - 1)), 0))

    def body(*refs):
        p_refs = refs[:n_here]
        w_ref, m_ref, v_ref = refs[n_here:n_here + 3]
        g_ref, d_ref, nm_ref, nv_ref = refs[n_here + 3 + n_prev:]
        b = pl.program_id(0)
        g = None
        for s in range(n_parts):
            t = p_refs[0][s]
            for q in range(1, n_here):
                t = jnp.where(b == q, p_refs[q][s], t)
            g = t.astype(F32) if g is None else g + t.astype(F32)
        delta, nm, nv = _adam_update(g, w_ref[...], m_ref[...], v_ref[...])
        g_ref[...] = g
        d_ref[...] = delta
        nm_ref[...] = nm
        nv_ref[...] = nv

    return _call(
        body, name=name, grid=(n_here, n_tiles),
        in_specs=[part_spec(q) for q in range(n_here)] + [spec, spec, spec] + [pl.BlockSpec(memory_space=pl.ANY)] * n_prev,
        out_specs=[spec] * 4, out_shape=[jax.ShapeDtypeStruct((n_blk, rows, cols), F32)] * 4,
        aliases={n_here + 3 + q: q for q in range(n_prev)}, sem=("arbitrary", "arbitrary"),
        args=[*parts, w, m, v, *prev], comm=comm)


def _sum_parts(parts, *, name):
    def body(p_ref, o_ref):
        g = p_ref[0]
        for s in range(1, N_DEV):
            g = g + p_ref[s]
        o_ref[...] = g

    return pl.pallas_call(body, name=name, out_shape=jax.ShapeDtypeStruct(parts.shape[1:], F32), compiler_params=_params())(parts)


def _adamw_packed(g, w, m, v, *, name):
    def body(g_ref, w_ref, m_ref, v_ref, d_ref, nm_ref, nv_ref):
        delta, nm, nv = _adam_update(g_ref[...], w_ref[...], m_ref[...], v_ref[...])
        d_ref[...] = delta
        nm_ref[...] = nm
        nv_ref[...] = nv

    return pl.pallas_call(body, name=name, out_shape=[jax.ShapeDtypeStruct(g.shape, F32)] * 3, compiler_params=_params())(g, w, m, v)


SUBLANES = 8


def _pack(arrs):
    rows = []
    for a in arrs:
        a2 = a.reshape(-1, a.shape[-1])
        a2 = jnp.pad(a2, ((0, 0), (0, (-a2.shape[1]) % LANES)))
        rows += [a2[:, i * LANES:(i + 1) * LANES] for i in range(a2.shape[1] // LANES)]
    out = jnp.concatenate(rows, axis=0)
    return jnp.pad(out, ((0, (-out.shape[0]) % SUBLANES), (0, 0)))


def _unpack(packed, shapes):
    outs, r = [], 0
    for shp in shapes:
        lead, cols = math.prod(shp[:-1]), shp[-1]
        n_blocks = -(-cols // LANES)
        blocks = [packed[r + i * lead:r + (i + 1) * lead] for i in range(n_blocks)]
        outs.append(jnp.concatenate(blocks, axis=1)[:, :cols].reshape(shp))
        r += n_blocks * lead
    return outs


WEIGHT_NAMES = ("norm_w", "ffn_w_gate", "ffn_w_up", "ffn_w_down", "ssm_w_in", "ssm_conv_w", "ssm_conv_b", "ssm_dt_bias",
                "ssm_a_log", "ssm_d", "ssm_norm_w", "ssm_w_out", "kv_norm_w", "w_k", "b_k", "w_v", "b_v", "attn_w_q",
                "attn_b_q", "attn_sinks", "attn_w_o", "attn_b_o", "final_norm_w")
MATRIX_NAMES = ("ffn_w_gate", "ffn_w_up", "ffn_w_down", "ssm_w_in", "ssm_w_out", "w_k", "w_v", "attn_w_q", "attn_w_o")
VECTOR_NAMES = tuple(n for n in WEIGHT_NAMES if n not in MATRIX_NAMES)
SHARDED_VECTORS = ("norm_w", "ssm_conv_w", "ssm_conv_b", "ssm_norm_w")


GATHER_PLAN = {
    "gather_stage0": ("gate0", "up0", "down0", "vec"),
    "ffn_fwd0": ("w_in",),
    "ssm_in_proj": ("w_out", "gate1"),
    "ssm_conv_fwd": ("w_k", "w_v", "up1"),
    "ssd_fwd": ("down1", "gate2", "up2"),
    "ssm_out_proj": ("w_q", "w_o"),
    "ffn_fwd1": ("down2", "gate3"),
    "ffn_fwd2": ("up3",),
    "attn_fwd": ("down3",),
}
PAIR_PLAN = {
    "attn_out_proj_dx": ("gate3", "up3", "down3"),
    "ffn_bwd2": ("w_q", "w_o"),
    "k_proj_dx": ("gate2", "up2", "down2"),
    "ssm_out_proj_dx": ("w_k", "w_v", "gate1", "up1", "down1"),
    "ssd_bwd": ("w_out",),
    "ssm_norm_bwd": ("w_in",),
    "ffn_norm_bwd0": ("gate0", "up0", "down0"),
}
CHIP_PLAN = {
    "attn_bwd": ("gate3", "up3"),
    "ffn_bwd2": ("down3",),
    "ffn_bwd1": ("gate2", "up2", "w_q", "w_o"),
    "ssd_bwd": ("down2", "gate1", "up1", "down1", "w_k", "w_v"),
    "ssm_conv_bwd": ("w_out",),
    "ffn_bwd0": ("w_in",),
    "adamw_gate": ("gate0",),
    "adamw_up": ("up0",),
    "adamw_down": ("down0",),
}
FFN_PARAMS = {"gate": "ffn_w_gate", "up": "ffn_w_up", "down": "ffn_w_down"}
SINGLE_MATRICES = {"w_in": "ssm_w_in", "w_out": "ssm_w_out", "w_k": "w_k", "w_v": "w_v", "w_q": "attn_w_q", "w_o": "attn_w_o"}


TRANSPOSED = ("ffn_w_gate", "ffn_w_up", "ssm_w_in")


def _matrix_view(name, a):
    if name in TRANSPOSED:
        a = jnp.swapaxes(a, -1, -2)
    return a.reshape((-1,) + a.shape[-2:])


def _from_matrix_view(name, a, shape):
    if name in TRANSPOSED:
        return jnp.swapaxes(a.reshape(shape[:-2] + (shape[-1], shape[-2])), -1, -2)
    return a.reshape(shape)


class _MeshNet:
    def __init__(self, p):
        self.p = p
        self.views = {n: _matrix_view(n, p[n]) for n in MATRIX_NAMES}
        self.local = {"vec": _pack([p[n] for n in SHARDED_VECTORS])}
        for short, n in FFN_PARAMS.items():
            cast = _cast_bf16(self.views[n], name=f"cast_{short}")
            self.local.update({f"{short}{k}": (cast, k) for k in range(N_FFN)})
        for short, n in SINGLE_MATRICES.items():
            self.local[short] = (_cast_bf16(self.views[n], name=f"cast_{short}"), 0)
        self.gathered_at, self.pairs_at, self.parts_at, self.grads, self.cache = {}, {}, {}, {}, {}

    def carry(self, name):
        comms = []
        if name in GATHER_PLAN:
            keys, comm = GATHER_PLAN[name], _Gather([self.local[k] for k in GATHER_PLAN[name]])
            self.gathered_at.update({k: (comm, i) for i, k in enumerate(keys)})
            comms.append(comm)
        if name in CHIP_PLAN:
            sums = []
            for k in CHIP_PLAN[name]:
                comm, i = self.pairs_at[k]
                sums.append(_pair_add(self.grads[k], comm.results[i], name=f"pair_add_{k}"))
            comm = _ChipExchange(sums)
            self.parts_at.update({k: (comm, i) for i, k in enumerate(CHIP_PLAN[name])})
            comms.append(comm)
        if name in PAIR_PLAN:
            keys, comm = PAIR_PLAN[name], _PairSwap([self.grads[k] for k in PAIR_PLAN[name]])
            self.pairs_at.update({k: (comm, i) for i, k in enumerate(keys)})
            comms.append(comm)
        return comms

    def run(self, name):
        for comm in self.carry(name):
            _run_exchange(comm, name=name)

    def give(self, key, grad):
        self.grads[key] = grad

    def parts(self, key):
        comm, i = self.parts_at[key]
        return comm.results[i]

    def _gathered(self, key):
        comm, i = self.gathered_at[key]
        return comm.results[i]

    def _vec(self, r0, lead, n_blocks):
        vecs = self._gathered("vec")
        return jnp.concatenate([vecs[d, r0 + i * lead:r0 + (i + 1) * lead, :] for d in range(N_DEV) for i in range(n_blocks)], axis=1)

    def _derive(self, name):
        p = self.p
        if name[:-1] in FFN_PARAMS:
            return self._gathered(name)
        if name == "w_in_t":
            return self._gathered("w_in").reshape(N_DEV * IN_PROJ_SHARD, D_MODEL)
        by_rows = {"wout": "w_out", "wk": "w_k", "wv": "w_v", "wq": "w_q", "wo": "w_o"}
        if name in by_rows:
            g = self._gathered(by_rows[name])
            return g.reshape(N_DEV * g.shape[1], g.shape[2])
        vectors = {"norm_w": lambda: self._vec(0, 6, 1).reshape(2, 3, D_MODEL), "conv_w": lambda: self._vec(6, CONV_WIDTH, 3),
                   "conv_b": lambda: self._vec(18, 1, 3), "ssm_norm_w": lambda: self._vec(21, 1, 2)}
        if name in vectors:
            return vectors[name]()
        replicated = {"dt_bias": p["ssm_dt_bias"], "a_log": p["ssm_a_log"], "d_skip": p["ssm_d"], "kv_norm_w": p["kv_norm_w"][None],
                      "b_k": p["b_k"][None], "b_v": p["b_v"][None], "b_q": p["attn_b_q"], "sinks": p["attn_sinks"],
                      "b_o": p["attn_b_o"], "final_norm_w": p["final_norm_w"][None]}
        return replicated[name]

    def w(self, name):
        if name not in self.cache:
            self.cache[name] = self._derive(name)
        return self.cache[name]


def _step(x, target, p, m, v):
    pos = _slot(_position())
    net = _MeshNet(p)
    net.run("gather_stage0")
    loss, grad_x, small = _forward_backward(x, target, net)

    grads, deltas, new_m, new_v = {}, {}, {}, {}
    view = lambda d, n: _matrix_view(n, d[n])
    vec_gather = _Gather([_pack([small[n] for n in VECTOR_NAMES])])
    for short, n in SINGLE_MATRICES.items():
        outs = _adamw([net.parts(short)], net.views[n], view(m, n), view(v, n), 0, [], name=f"adamw_{short}",
                      comm=[vec_gather] if short == "w_in" else None)
        grads[n], deltas[n], new_m[n], new_v[n] = [_from_matrix_view(n, o, p[n].shape) for o in outs]
    ffn_outs = {}
    for short, n in FFN_PARAMS.items():
        ffn_outs[short] = _adamw([net.parts(f"{short}{k}") for k in range(1, N_FFN)], net.views[n], view(m, n), view(v, n), 1, [],
                                 name=f"adamw_{short}", comm=net.carry(f"adamw_{short}"))
    for short, n in FFN_PARAMS.items():
        outs = _adamw([net.parts(f"{short}0")], net.views[n], view(m, n), view(v, n), 0, ffn_outs[short], name=f"adamw_{short}0")
        grads[n], deltas[n], new_m[n], new_v[n] = [_from_matrix_view(n, o, p[n].shape) for o in outs]
    vec_sum = _sum_parts(vec_gather.results[0], name="sum_vector_grads")
    full_shapes = {"norm_w": (2, 3, D_MODEL), "ssm_conv_w": (1, CONV_WIDTH, CONV_DIM), "ssm_conv_b": (1, CONV_DIM),
                   "ssm_norm_w": (1, D_INNER)}
    vec_full = dict(zip(VECTOR_NAMES, _unpack(vec_sum, [full_shapes.get(n, p[n].shape) for n in VECTOR_NAMES])))
    for n in VECTOR_NAMES:
        g = vec_full[n]
        if n in SHARDED_VECTORS:
            per = p[n].shape[-1]
            g = lax.dynamic_slice_in_dim(g, pos * per, per, axis=g.ndim - 1)
        grads[n] = g
    packed = _adamw_packed(*[_pack([d[n] for n in VECTOR_NAMES]) for d in (grads, p, m, v)], name="adamw_vectors")
    shapes = [p[n].shape for n in VECTOR_NAMES]
    for d, pk in zip((deltas, new_m, new_v), packed):
        d.update(zip(VECTOR_NAMES, _unpack(pk, shapes)))
    return loss, grad_x, grads, deltas, new_m, new_v


def kernel(x, norm_w, ffn_w_gate, ffn_w_up, ffn_w_down, ssm_w_in, ssm_conv_w, ssm_conv_b, ssm_dt_bias, ssm_a_log, ssm_d, ssm_norm_w, ssm_w_out, kv_norm_w, w_k, b_k, w_v, b_v, attn_w_q, attn_b_q, attn_sinks, attn_w_o, attn_b_o, final_norm_w, loss_target, m_norm_w, m_ffn_w_gate, m_ffn_w_up, m_ffn_w_down, m_ssm_w_in, m_ssm_conv_w, m_ssm_conv_b, m_ssm_dt_bias, m_ssm_a_log, m_ssm_d, m_ssm_norm_w, m_ssm_w_out, m_kv_norm_w, m_w_k, m_b_k, m_w_v, m_b_v, m_attn_w_q, m_attn_b_q, m_attn_sinks, m_attn_w_o, m_attn_b_o, m_final_norm_w, v_norm_w, v_ffn_w_gate, v_ffn_w_up, v_ffn_w_down, v_ssm_w_in, v_ssm_conv_w, v_ssm_conv_b, v_ssm_dt_bias, v_ssm_a_log, v_ssm_d, v_ssm_norm_w, v_ssm_w_out, v_kv_norm_w, v_w_k, v_b_k, v_w_v, v_b_v, v_attn_w_q, v_attn_b_q, v_attn_sinks, v_attn_w_o, v_attn_b_o, v_final_norm_w):
    p = dict(zip(WEIGHT_NAMES, (norm_w, ffn_w_gate, ffn_w_up, ffn_w_down, ssm_w_in, ssm_conv_w, ssm_conv_b, ssm_dt_bias, ssm_a_log, ssm_d, ssm_norm_w, ssm_w_out, kv_norm_w, w_k, b_k, w_v, b_v, attn_w_q, attn_b_q, attn_sinks, attn_w_o, attn_b_o, final_norm_w)))
    m = dict(zip(WEIGHT_NAMES, (m_norm_w, m_ffn_w_gate, m_ffn_w_up, m_ffn_w_down, m_ssm_w_in, m_ssm_conv_w, m_ssm_conv_b, m_ssm_dt_bias, m_ssm_a_log, m_ssm_d, m_ssm_norm_w, m_ssm_w_out, m_kv_norm_w, m_w_k, m_b_k, m_w_v, m_b_v, m_attn_w_q, m_attn_b_q, m_attn_sinks, m_attn_w_o, m_attn_b_o, m_final_norm_w)))
    v = dict(zip(WEIGHT_NAMES, (v_norm_w, v_ffn_w_gate, v_ffn_w_up, v_ffn_w_down, v_ssm_w_in, v_ssm_conv_w, v_ssm_conv_b, v_ssm_dt_bias, v_ssm_a_log, v_ssm_d, v_ssm_norm_w, v_ssm_w_out, v_kv_norm_w, v_w_k, v_b_k, v_w_v, v_b_v, v_attn_w_q, v_attn_b_q, v_attn_sinks, v_attn_w_o, v_attn_b_o, v_final_norm_w)))
    loss, grad_x, grads, deltas, new_m, new_v = _step(x[0], loss_target[0], p, m, v)
    loss = lax.psum(loss[0, 0], ("x", "y", "c"))
    return (loss, grad_x[None], *[grads[n] for n in WEIGHT_NAMES], *[deltas[n] for n in WEIGHT_NAMES],
            *[new_m[n] for n in WEIGHT_NAMES], *[new_v[n] for n in WEIGHT_NAMES])
```

```python
import functools
import math

import jax
import jax.numpy as jnp
from jax import lax
from jax.experimental import pallas as pl
from jax.experimental.pallas import tpu as pltpu

F32 = jnp.float32
BF16 = jnp.bfloat16

N_DEV = 8
SEQ = 2048
D_MODEL = 1024
D_FF_SHARD = 352
N_FFN = 4
D_INNER = 2048
SSM_HEADS = 32
SSM_HEAD_DIM = 64
SSM_GROUPS = 4
HEADS_PER_GROUP = 8
SSM_STATE = 128
CHUNK = 128
N_CHUNKS = SEQ // CHUNK
GN = SSM_GROUPS * SSM_STATE
CONV_DIM = D_INNER + 2 * GN
CONV_WIDTH = 4
ZX_DIM = D_INNER + CONV_DIM
IN_PROJ_SHARD = 644
ATT_HEAD_DIM = 64
N_Q_HEADS = 16
N_KV_HEADS = 4
Q_PER_KV = 4
KV_DIM = N_KV_HEADS * ATT_HEAD_DIM
WINDOW = 128
ROPE_THETA = 10000.0
EPS = 1e-5
FFN_RES_WEIGHT = 0.5
ATT_SCALE = 1.0 / math.sqrt(ATT_HEAD_DIM)
NEG_BIG = -1e30

ADAM_LR = 0.001
ADAM_B1 = 0.9
ADAM_B2 = 0.999
ADAM_EPS = 1e-08
ADAM_WD = 0.01
ADAM_STEP = 10

VMEM_LIMIT_BYTES = 56 * 1024 * 1024
FFN_BWD_VMEM_LIMIT_BYTES = 61 * 1024 * 1024

NN = (((1,), (0,)), ((), ()))
NT = (((1,), (1,)), ((), ()))
TN = (((0,), (0,)), ((), ()))
_DIMS = {"nn": NN, "nt": NT, "tn": TN}


def _params(*sem):
    return pltpu.CompilerParams(dimension_semantics=sem if sem else None, vmem_limit_bytes=VMEM_LIMIT_BYTES)


def _dot(a, b, dims=NN):
    return lax.dot_general(a.astype(BF16), b.astype(BF16), dims, preferred_element_type=F32)


def _sigmoid(x):
    return 1.0 / (1.0 + jnp.exp(-x))


def _dsilu(x, s):
    return s * (1.0 + x * (1.0 - s))


def _rms(x):
    r = lax.rsqrt(jnp.mean(x * x, axis=-1, keepdims=True) + EPS)
    return x * r, r


def _sum_all(x):
    return jnp.sum(jnp.sum(x, axis=1, keepdims=True), axis=0, keepdims=True)


MESH = pl.DeviceIdType.MESH
N_PEERS = N_DEV - 1
N_CHIPS = N_DEV // 2


def _position():
    return lax.axis_index("x"), lax.axis_index("y"), lax.axis_index("c")


def _slot(p):
    return 4 * p[0] + 2 * p[1] + p[2]


class _Exchange:
    def __init__(self, arrays, out_shapes):
        n = len(arrays)
        self.arrays = list(arrays)
        self.out_shapes = out_shapes
        self.scratch = [pltpu.SemaphoreType.DMA((n, N_PEERS)), pltpu.SemaphoreType.DMA((n, N_PEERS)), pltpu.SemaphoreType.DMA((n,))]
        self.results = None


class _Gather(_Exchange):
    def __init__(self, pieces):
        pieces = [p if isinstance(p, tuple) else (p, None) for p in pieces]
        self.blocks = [k for _, k in pieces]
        shapes = [a.shape if k is None else a.shape[1:] for a, k in pieces]
        super().__init__([a for a, _ in pieces], [jax.ShapeDtypeStruct((N_DEV,) + s, a.dtype) for s, (a, _) in zip(shapes, pieces)])

    def _plan(self, ins, outs, sems):
        send_sems, recv_sems, local_sems = sems
        x, y, c = _position()
        me, sibling = (x, y, c), (x, y, 1 - c)
        chips = [(1 - x, y), (x, 1 - y), (1 - x, 1 - y)]
        n = len(ins)
        ins = [r if k is None else r.at[k] for r, k in zip(ins, self.blocks)]

        def copy(a, k, block, to, src=None):
            dst = outs[a].at[_slot(block)]
            return pltpu.make_async_remote_copy(src_ref=dst if src is None else src, dst_ref=dst, send_sem=send_sems.at[a, k],
                                                recv_sem=recv_sems.at[a, k], device_id=to, device_id_type=MESH)

        mine = [pltpu.make_async_copy(ins[a], outs[a].at[_slot(me)], local_sems.at[a]) for a in range(n)]
        first = []
        for a in range(n):
            first.append(copy(a, 0, me, sibling, src=ins[a]))
            first += [copy(a, 1 + j, me, (*chip, c), src=ins[a]) for j, chip in enumerate(chips)]
        return n, c, me, sibling, chips, copy, mine, first

    def start(self, ins, outs, sems):
        _, _, _, _, _, _, mine, first = self._plan(ins, outs, sems)
        for cp in mine + first:
            cp.start()

    def finish(self, ins, outs, sems):
        n, c, me, sibling, chips, copy, mine, first = self._plan(ins, outs, sems)
        passed = []
        for j, chip in enumerate(chips):
            for a in range(n):
                copy(a, 1 + j, (*chip, c), me).wait_recv()
                fwd = copy(a, 4 + j, (*chip, c), sibling)
                fwd.start()
                passed.append(fwd)
        for a in range(n):
            copy(a, 0, sibling, me).wait_recv()
            for j, chip in enumerate(chips):
                copy(a, 4 + j, (*chip, 1 - c), me).wait_recv()
        for cp in first + passed:
            cp.wait_send()
        for cp in mine:
            cp.wait()


class _PairSwap(_Exchange):
    def __init__(self, arrays):
        n = len(arrays)
        self.arrays = list(arrays)
        self.out_shapes = [jax.ShapeDtypeStruct((N_CHIPS,) + a.shape[1:], a.dtype) for a in arrays]
        self.scratch = [pltpu.SemaphoreType.DMA((n, N_CHIPS)), pltpu.SemaphoreType.DMA((n, N_CHIPS))]
        self.results = None

    def _plan(self, ins, outs, sems):
        send_sems, recv_sems = sems
        x, y, c = _position()
        return [pltpu.make_async_remote_copy(src_ref=ins[a].at[2 * q + 1 - c], dst_ref=outs[a].at[q], send_sem=send_sems.at[a, q],
                                             recv_sem=recv_sems.at[a, q], device_id=(x, y, 1 - c), device_id_type=MESH)
                for a in range(len(ins)) for q in range(N_CHIPS)]

    def start(self, ins, outs, sems):
        for cp in self._plan(ins, outs, sems):
            cp.start()

    def finish(self, ins, outs, sems):
        for cp in self._plan(ins, outs, sems):
            cp.wait()


class _ChipExchange(_Exchange):
    def __init__(self, arrays):
        n = len(arrays)
        self.arrays = list(arrays)
        self.out_shapes = [jax.ShapeDtypeStruct(a.shape, a.dtype) for a in arrays]
        self.scratch = [pltpu.SemaphoreType.DMA((n, 3)), pltpu.SemaphoreType.DMA((n, 3)), pltpu.SemaphoreType.DMA((n,))]
        self.results = None

    def _plan(self, ins, outs, sems):
        send_sems, recv_sems, local_sems = sems
        x, y, c = _position()
        here = 2 * x + y
        chips = [(1 - x, y), (x, 1 - y), (1 - x, 1 - y)]
        n = len(ins)

        def copy(a, k, src_slot, dst_slot):
            return pltpu.make_async_remote_copy(src_ref=ins[a].at[src_slot], dst_ref=outs[a].at[dst_slot], send_sem=send_sems.at[a, k],
                                                recv_sem=recv_sems.at[a, k], device_id=(*chips[k], c), device_id_type=MESH)

        there = [2 * qx + qy for qx, qy in chips]
        mine = [pltpu.make_async_copy(ins[a].at[here], outs[a].at[here], local_sems.at[a]) for a in range(n)]
        sends = [copy(a, k, there[k], here) for a in range(n) for k in range(3)]
        arrivals = lambda: [copy(a, k, here, there[k]) for a in range(n) for k in range(3)]
        return mine, sends, arrivals

    def start(self, ins, outs, sems):
        mine, sends, _ = self._plan(ins, outs, sems)
        for cp in mine + sends:
            cp.start()

    def finish(self, ins, outs, sems):
        mine, sends, arrivals = self._plan(ins, outs, sems)
        for cp in arrivals():
            cp.wait_recv()
        for cp in sends:
            cp.wait_send()
        for cp in mine:
            cp.wait()


def _call(body, *, name, grid, in_specs, out_specs, out_shape, args, scratch_shapes=(), sem=(), comm=(), aliases=None,
          vmem_limit=VMEM_LIMIT_BYTES):
    single = not isinstance(out_shape, (list, tuple))
    out_shape = [out_shape] if single else list(out_shape)
    out_specs = [out_specs] if single else list(out_specs)
    comms = list(comm or ())
    n_in, n_out, n_scr = len(args), len(out_shape), len(scratch_shapes)
    params = pltpu.CompilerParams(dimension_semantics=tuple(sem) if sem else None, vmem_limit_bytes=vmem_limit)
    if not comms:
        res = pl.pallas_call(body, name=name, grid=grid, in_specs=list(in_specs), out_specs=out_specs, out_shape=out_shape,
                             scratch_shapes=list(scratch_shapes), input_output_aliases=aliases or {}, compiler_params=params)(*args)
        return res[0] if single else res
    counts = [n_in] + [len(c.arrays) for c in comms] + [n_out] + [len(c.out_shapes) for c in comms] + [n_scr] + [len(c.scratch) for c in comms]
    nc = len(comms)

    def carried(*refs):
        pos, groups = 0, []
        for cnt in counts:
            groups.append(refs[pos:pos + cnt])
            pos += cnt
        ins, c_ins = groups[0], groups[1:1 + nc]
        outs, c_outs = groups[1 + nc], groups[2 + nc:2 + 2 * nc]
        scr, c_sems = groups[2 + 2 * nc], groups[3 + 2 * nc:]
        ids = [pl.program_id(d) for d in range(len(grid))]
        is_first = functools.reduce(jnp.logical_and, [i == 0 for i in ids])
        is_last = functools.reduce(jnp.logical_and, [i == g - 1 for i, g in zip(ids, grid)])

        @pl.when(is_first)
        def _():
            for q, c in enumerate(comms):
                c.start(c_ins[q], c_outs[q], c_sems[q])

        body(*ins, *outs, *scr)

        @pl.when(is_last)
        def _():
            for q, c in enumerate(comms):
                c.finish(c_ins[q], c_outs[q], c_sems[q])

    anyspec = pl.BlockSpec(memory_space=pl.ANY)
    c_arrays = [a for c in comms for a in c.arrays]
    c_shapes = [s for c in comms for s in c.out_shapes]
    res = pl.pallas_call(
        carried, name=name, grid=grid, in_specs=list(in_specs) + [anyspec] * len(c_arrays), out_specs=out_specs + [anyspec] * len(c_shapes),
        out_shape=out_shape + c_shapes, scratch_shapes=list(scratch_shapes) + [s for c in comms for s in c.scratch],
        input_output_aliases=aliases or {}, compiler_params=params)(*args, *c_arrays)
    pos = n_out
    for c in comms:
        c.results = list(res[pos:pos + len(c.out_shapes)])
        pos += len(c.out_shapes)
    return res[0] if single else list(res[:n_out])


def _run_exchange(comm, *, name):
    def body(*refs):
        n_ci, n_co = len(comm.arrays), len(comm.out_shapes)
        ins, outs, sems = refs[:n_ci], refs[n_ci:n_ci + n_co], refs[n_ci + n_co:]
        comm.start(ins, outs, sems)
        comm.finish(ins, outs, sems)

    anyspec = pl.BlockSpec(memory_space=pl.ANY)
    comm.results = list(pl.pallas_call(
        body, name=name, in_specs=[anyspec] * len(comm.arrays), out_specs=[anyspec] * len(comm.out_shapes),
        out_shape=list(comm.out_shapes), scratch_shapes=list(comm.scratch))(*comm.arrays))
    return comm.results


def _mm(a, b, *, dims="nn", bias=None, res=None, out_dtype=F32, name, tm=1024, tn=1024, tk=1024, comm=None, b_rows=None,
        out_window=None, into=None, colsum_b=False):
    if dims == "tn":
        k_dim, m_dim = a.shape
    else:
        m_dim, k_dim = a.shape
    row0, n_rows = b_rows if b_rows is not None else (0, b.shape[0])
    n_dim = n_rows if dims == "nt" else b.shape[1]
    assert dims == "nt" or n_rows == k_dim, (name, a.shape, b.shape, b_rows)
    tm, tn, tk = min(tm, m_dim), min(tn, n_dim), min(tk, k_dim)
    assert m_dim % tm == 0 and n_dim % tn == 0 and k_dim % tk == 0, (name, a.shape, b.shape)
    nk = k_dim // tk
    a_spec = pl.BlockSpec((tk, tm), lambda i, j, k: (k, i)) if dims == "tn" else pl.BlockSpec((tm, tk), lambda i, j, k: (i, k))
    if dims == "nt":
        assert row0 % tn == 0
        b_spec = pl.BlockSpec((tn, tk), lambda i, j, k: (row0 // tn + j, k))
    else:
        assert row0 % tk == 0
        b_spec = pl.BlockSpec((tk, tn), lambda i, j, k: (row0 // tk + k, j))
    in_specs, args = [a_spec, b_spec], [a, b]
    if bias is not None:
        in_specs.append(pl.BlockSpec((1, tn), lambda i, j, k: (0, j)))
        args.append(bias)
    if res is not None:
        in_specs.append(pl.BlockSpec((tm, tn), lambda i, j, k: (i, j)))
        args.append(res)
    dn = _DIMS[dims]

    if colsum_b:
        assert dims == "tn" and m_dim == tm and into is None and out_window is None

    def body(*refs):
        a_ref, b_ref = refs[0], refs[1]
        acc_ref = refs[-1]
        o_ref = refs[-3] if colsum_b else refs[-2]
        k = pl.program_id(2)

        @pl.when(k == 0)
        def _():
            acc_ref[...] = jnp.zeros_like(acc_ref)
            if colsum_b:
                refs[-2][...] = jnp.zeros_like(refs[-2])

        acc_ref[...] += _dot(a_ref[...], b_ref[...], dn)
        if colsum_b:
            refs[-2][...] += jnp.sum(b_ref[...].astype(F32), axis=0, keepdims=True)

        @pl.when(k == nk - 1)
        def _():
            r = acc_ref[...]
            pos = 2
            if bias is not None:
                r = r + refs[pos][...]
                pos += 1
            if res is not None:
                r = r + refs[pos][...]
            o_ref[...] = r.astype(out_dtype)

    out_row0, out_rows = out_window if out_window is not None else (0, m_dim)
    assert out_row0 % tm == 0
    aliases = None
    if into is not None:
        assert into.shape == (out_rows, n_dim) and into.dtype == out_dtype
        in_specs.append(pl.BlockSpec(memory_space=pl.ANY))
        args.append(into)
        aliases = {len(args) - 1: 0}
    out_spec = pl.BlockSpec((tm, tn), lambda i, j, k: (out_row0 // tm + i, j))
    out_shape = jax.ShapeDtypeStruct((out_rows, n_dim), out_dtype)
    if colsum_b:
        out_spec = [out_spec, pl.BlockSpec((1, tn), lambda i, j, k: (0, j))]
        out_shape = [out_shape, jax.ShapeDtypeStruct((1, n_dim), F32)]
    return _call(
        body, name=name, grid=(m_dim // tm, n_dim // tn, nk), in_specs=in_specs, out_specs=out_spec, out_shape=out_shape,
        aliases=aliases, scratch_shapes=[pltpu.VMEM((tm, tn), F32)], sem=("parallel", "parallel", "arbitrary"), args=args, comm=comm)


def _rope_rotate(x, cos_t, sin_t):
    rows, width = x.shape
    half = ATT_HEAD_DIM // 2
    lane = lax.broadcasted_iota(jnp.int32, (rows, width), 1)
    first = (lane % ATT_HEAD_DIM) < half
    rot = jnp.where(first, -pltpu.roll(x, width - half, 1), pltpu.roll(x, half, 1))
    reps = width // 128
    return x * jnp.tile(cos_t, (1, reps)) + rot * jnp.tile(sin_t, (1, reps))


def _norm_mm(x, nw, w, bias, *, name, tm=1024, tn=1024, comm=None, w_rows=None, rope=None):
    t_dim, d_dim = x.shape
    transposed = w_rows is not None
    n_dim = w_rows if transposed else w.shape[1]
    tn = min(tn, n_dim)
    assert t_dim % tm == 0 and n_dim % tn == 0
    has_bias = bias is not None
    w_spec = pl.BlockSpec((tn, d_dim), lambda i, j: (j, 0)) if transposed else pl.BlockSpec((d_dim, tn), lambda i, j: (0, j))
    dn = NT if transposed else NN
    in_specs = [pl.BlockSpec((tm, d_dim), lambda i, j: (i, 0)), pl.BlockSpec((1, d_dim), lambda i, j: (0, 0)), w_spec]
    args = [x, nw, w]
    if has_bias:
        in_specs.append(pl.BlockSpec((1, tn), lambda i, j: (0, j)))
        args.append(bias)
    if rope is not None:
        in_specs += [pl.BlockSpec((tm, LANES), lambda i, j: (i, 0))] * 2
        args += list(rope)

    def body(*refs):
        x_ref, nw_ref, w_ref = refs[:3]
        o_ref, h_ref = refs[-2], refs[-1]

        @pl.when(pl.program_id(1) == 0)
        def _():
            xhat, _ = _rms(x_ref[...])
            h_ref[...] = (xhat * nw_ref[...]).astype(BF16)

        r = _dot(h_ref[...], w_ref[...], dn)
        if has_bias:
            r = r + refs[3][...]
        if rope is not None:
            r = _rope_rotate(r, refs[-4][...], refs[-3][...])
        o_ref[...] = r

    return _call(
        body, name=name, grid=(t_dim // tm, n_dim // tn), in_specs=in_specs,
        out_specs=[pl.BlockSpec((tm, tn), lambda i, j: (i, j)), pl.BlockSpec((tm, d_dim), lambda i, j: (i, 0))],
        out_shape=[jax.ShapeDtypeStruct((t_dim, n_dim), F32), jax.ShapeDtypeStruct((t_dim, d_dim), BF16)],
        sem=("parallel", "arbitrary"), args=args, comm=comm)


def _norm_bwd(x, nw, dh, res, *, name, tm=256, comm=None):
    t_dim, d_dim = x.shape
    n_res = len(res)
    row = pl.BlockSpec((tm, d_dim), lambda i: (i, 0))
    vec = pl.BlockSpec((1, d_dim), lambda i: (0, 0))

    def body(*refs):
        x_ref, nw_ref, dh_ref = refs[:3]
        dx_ref, dnw_ref = refs[-2], refs[-1]
        xhat, r = _rms(x_ref[...])
        dh = dh_ref[...]
        dxhat = dh * nw_ref[...]
        dx = r * (dxhat - xhat * jnp.mean(dxhat * xhat, axis=-1, keepdims=True))
        for rr in refs[3:3 + n_res]:
            dx = dx + rr[...]
        dx_ref[...] = dx

        @pl.when(pl.program_id(0) == 0)
        def _():
            dnw_ref[...] = jnp.zeros_like(dnw_ref)

        dnw_ref[...] += jnp.sum(dh * xhat, axis=0, keepdims=True)

    return _call(
        body, name=name, grid=(t_dim // tm,), in_specs=[row, vec, row] + [row] * n_res,
        out_specs=[row, vec],
        out_shape=[jax.ShapeDtypeStruct((t_dim, d_dim), F32), jax.ShapeDtypeStruct((1, d_dim), F32)],
        sem=("arbitrary",), args=[x, nw, dh, *res], comm=comm)


FFN_ROW_TILE = 512
FFN_SHARDS_PER_STEP = 2
FFN_STEPS = N_DEV // FFN_SHARDS_PER_STEP
FFN_STEP_COLS = FFN_SHARDS_PER_STEP * D_FF_SHARD


def _ffn_step_view(w):
    return w.reshape(FFN_STEPS, FFN_STEP_COLS, w.shape[-1])


def _ffn_specs(t_dim, d_dim):
    full = pl.BlockSpec((t_dim, d_dim), lambda j: (0, 0))
    wspec = pl.BlockSpec((None, FFN_STEP_COLS, d_dim), lambda j: (j, 0, 0))
    pre = pl.BlockSpec((None, t_dim, FFN_STEP_COLS), lambda j: (j, 0, 0))
    return full, wspec, pre


def _ffn_fwd(x, nw, wg, wu, wd, *, name, comm=None):
    t_dim, d_dim = x.shape
    n_tiles = t_dim // FFN_ROW_TILE

    def body(x_ref, nw_ref, wg_ref, wu_ref, wd_ref, o_ref, g_ref, u_ref, h_scr):
        j = pl.program_id(0)

        @pl.when(j == 0)
        def _():
            xhat, _ = _rms(x_ref[...])
            h_scr[...] = (xhat * nw_ref[...]).astype(BF16)
            o_ref[...] = jnp.zeros_like(o_ref)

        for t in range(n_tiles):
            rows = pl.ds(t * FFN_ROW_TILE, FFN_ROW_TILE)
            h = h_scr[rows, :]
            g = _dot(h, wg_ref[...], NT)
            u = _dot(h, wu_ref[...], NT)
            g_ref[rows, :] = g.astype(BF16)
            u_ref[rows, :] = u.astype(BF16)
            o_ref[rows, :] += _dot(g * _sigmoid(g) * u, wd_ref[...])

        @pl.when(j == FFN_STEPS - 1)
        def _():
            o_ref[...] = x_ref[...] + FFN_RES_WEIGHT * o_ref[...]

    full, wspec, pre = _ffn_specs(t_dim, d_dim)
    pre_shape = jax.ShapeDtypeStruct((FFN_STEPS, t_dim, FFN_STEP_COLS), BF16)
    return _call(
        body, name=name, grid=(FFN_STEPS,),
        in_specs=[full, pl.BlockSpec((1, d_dim), lambda j: (0, 0)), wspec, wspec, wspec],
        out_specs=[full, pre, pre], out_shape=[jax.ShapeDtypeStruct((t_dim, d_dim), F32), pre_shape, pre_shape],
        scratch_shapes=[pltpu.VMEM((t_dim, d_dim), BF16)],
        sem=("arbitrary",), args=[x, nw, _ffn_step_view(wg), _ffn_step_view(wu), _ffn_step_view(wd)], comm=comm)


def _ffn_bwd_prep(x, nw, dout, *, name, tm=256):
    t_dim, d_dim = x.shape
    row = pl.BlockSpec((tm, d_dim), lambda i: (i, 0))

    def body(x_ref, nw_ref, dout_ref, h_ref, dob_ref):
        xhat, _ = _rms(x_ref[...])
        h_ref[...] = (xhat * nw_ref[...]).astype(BF16)
        dob_ref[...] = (FFN_RES_WEIGHT * dout_ref[...]).astype(BF16)

    return pl.pallas_call(
        body, name=name, grid=(t_dim // tm,), in_specs=[row, pl.BlockSpec((1, d_dim), lambda i: (0, 0)), row],
        out_specs=[row, row], out_shape=[jax.ShapeDtypeStruct((t_dim, d_dim), BF16)] * 2,
        compiler_params=_params("parallel"),
    )(x, nw, dout)


def _ffn_bwd(h, dob, pre_g, pre_u, wg, wu, wd, *, name, comm=None):
    t_dim, d_dim = h.shape
    n_tiles = t_dim // FFN_ROW_TILE

    def body(h_ref, dob_ref, g_ref, u_ref, wg_ref, wu_ref, wd_ref, dh_ref, gg_ref, gu_ref, gd_ref, dwg_scr, dwu_scr, dwd_scr):
        @pl.when(pl.program_id(0) == 0)
        def _():
            dh_ref[...] = jnp.zeros_like(dh_ref)

        for t in range(n_tiles):
            rows = pl.ds(t * FFN_ROW_TILE, FFN_ROW_TILE)
            hh = h_ref[rows, :]
            do = dob_ref[rows, :]
            g = g_ref[rows, :].astype(F32)
            u = u_ref[rows, :].astype(F32)
            sg = _sigmoid(g)
            s = g * sg
            da = _dot(do, wd_ref[...], NT)
            dwd = _dot(s * u, do, TN)
            du = (da * s).astype(BF16)
            dg = (da * u * _dsilu(g, sg)).astype(BF16)
            dwg = _dot(dg, hh, TN)
            dwu = _dot(du, hh, TN)
            if t == 0:
                dwd_scr[...] = dwd
                dwg_scr[...] = dwg
                dwu_scr[...] = dwu
            else:
                dwd_scr[...] += dwd
                dwg_scr[...] += dwg
                dwu_scr[...] += dwu
            dh_ref[rows, :] += _dot(dg, wg_ref[...]) + _dot(du, wu_ref[...])
        gg_ref[...] = dwg_scr[...].astype(BF16)
        gu_ref[...] = dwu_scr[...].astype(BF16)
        gd_ref[...] = dwd_scr[...].astype(BF16)

    full, wspec, pre = _ffn_specs(t_dim, d_dim)
    gspec = pl.BlockSpec((None, FFN_STEP_COLS, d_dim), lambda j: (j, 0, 0), pipeline_mode=pl.Buffered(1))
    grad_shape = jax.ShapeDtypeStruct((FFN_STEPS, FFN_STEP_COLS, d_dim), BF16)
    dh, gg, gu, gd = _call(
        body, name=name, grid=(FFN_STEPS,),
        in_specs=[full, full, pre, pre, wspec, wspec, wspec], out_specs=[full, gspec, gspec, gspec],
        out_shape=[jax.ShapeDtypeStruct((t_dim, d_dim), F32)] + [grad_shape] * 3,
        scratch_shapes=[pltpu.VMEM((FFN_STEP_COLS, d_dim), F32)] * 3, sem=("arbitrary",), vmem_limit=FFN_BWD_VMEM_LIMIT_BYTES,
        args=[h, dob, pre_g, pre_u, _ffn_step_view(wg), _ffn_step_view(wu), _ffn_step_view(wd)], comm=comm)
    return dh, gg.reshape(wg.shape), gu.reshape(wu.shape), gd.reshape(wd.shape)


CONV_COLS = 256


def _shift_down(u, s, rows):
    return jnp.where(rows >= s, pltpu.roll(u, s, 0), 0.0)


def _shift_up(u, s, rows, t_dim):
    return jnp.where(rows < t_dim - s, pltpu.roll(u, t_dim - s, 0), 0.0)


def _conv_pre(u, w_ref, b_ref, rows):
    c = b_ref[...] + w_ref[CONV_WIDTH - 1:CONV_WIDTH, :] * u
    for k in range(CONV_WIDTH - 1):
        c = c + w_ref[k:k + 1, :] * _shift_down(u, CONV_WIDTH - 1 - k, rows)
    return c


def _conv_fwd(zx, cw, cb, *, name, comm=None):
    t_dim = zx.shape[0]
    off = D_INNER // CONV_COLS

    def body(u_ref, w_ref, b_ref, o_ref):
        rows = lax.broadcasted_iota(jnp.int32, (t_dim, CONV_COLS), 0)
        c = _conv_pre(u_ref[...], w_ref, b_ref, rows)
        o_ref[...] = c * _sigmoid(c)

    return _call(
        body, name=name, grid=(CONV_DIM // CONV_COLS,),
        in_specs=[pl.BlockSpec((t_dim, CONV_COLS), lambda j: (0, off + j)),
                  pl.BlockSpec((CONV_WIDTH, CONV_COLS), lambda j: (0, j)), pl.BlockSpec((1, CONV_COLS), lambda j: (0, j))],
        out_specs=pl.BlockSpec((t_dim, CONV_COLS), lambda j: (0, j)),
        out_shape=jax.ShapeDtypeStruct((t_dim, CONV_DIM), F32), sem=("parallel",), args=[zx, cw, cb], comm=comm)


def _conv_bwd(zx, cw, cb, dxs, db, dc, dzx, *, name, comm=None):
    t_dim = zx.shape[0]
    off = D_INNER // CONV_COLS
    n_xs = D_INNER // CONV_COLS
    n_b = GN // CONV_COLS

    def body(u_ref, w_ref, b_ref, dxs_ref, db_ref, dc_ref, dzx_in, dzx_ref, dw_ref, dbias_ref):
        j = pl.program_id(0)
        rows = lax.broadcasted_iota(jnp.int32, (t_dim, CONV_COLS), 0)
        u = u_ref[...]
        c = _conv_pre(u, w_ref, b_ref, rows)
        d = jnp.where(j < n_xs, dxs_ref[...], jnp.where(j < n_xs + n_b, db_ref[...], dc_ref[...]))
        dcv = d * _dsilu(c, _sigmoid(c))
        dpre = w_ref[CONV_WIDTH - 1:CONV_WIDTH, :] * dcv
        dw_ref[CONV_WIDTH - 1:CONV_WIDTH, :] = jnp.sum(dcv * u, axis=0, keepdims=True)
        for k in range(CONV_WIDTH - 1):
            s = CONV_WIDTH - 1 - k
            dpre = dpre + w_ref[k:k + 1, :] * _shift_up(dcv, s, rows, t_dim)
            dw_ref[k:k + 1, :] = jnp.sum(dcv * _shift_down(u, s, rows), axis=0, keepdims=True)
        dzx_ref[...] = dpre
        dbias_ref[...] = jnp.sum(dcv, axis=0, keepdims=True)

    blk = lambda n: pl.BlockSpec((t_dim, CONV_COLS), n)
    return _call(
        body, name=name, grid=(CONV_DIM // CONV_COLS,),
        in_specs=[blk(lambda j: (0, off + j)), pl.BlockSpec((CONV_WIDTH, CONV_COLS), lambda j: (0, j)),
                  pl.BlockSpec((1, CONV_COLS), lambda j: (0, j)),
                  blk(lambda j: (0, jnp.minimum(j, n_xs - 1))),
                  blk(lambda j: (0, jnp.clip(j - n_xs, 0, n_b - 1))),
                  blk(lambda j: (0, jnp.clip(j - n_xs - n_b, 0, n_b - 1))),
                  pl.BlockSpec(memory_space=pl.ANY)],
        out_specs=[blk(lambda j: (0, off + j)), pl.BlockSpec((CONV_WIDTH, CONV_COLS), lambda j: (0, j)),
                   pl.BlockSpec((1, CONV_COLS), lambda j: (0, j))],
        out_shape=[jax.ShapeDtypeStruct(dzx.shape, F32), jax.ShapeDtypeStruct((CONV_WIDTH, CONV_DIM), F32),
                   jax.ShapeDtypeStruct((1, CONV_DIM), F32)],
        aliases={6: 0}, sem=("parallel",), args=[zx, cw, cb, dxs, db, dc, dzx], comm=comm)


def _softplus_parts(x):
    e = jnp.exp(-jnp.abs(x))
    u = 1.0 + e
    log1p_e = jnp.where(u == 1.0, e, jnp.log(u) * e / jnp.where(u == 1.0, 1.0, u - 1.0))
    return jnp.maximum(x, 0.0) + log1p_e


def _dt_prep(dtr, dt_bias, a_log, *, name):
    def body(dtr_ref, bias_ref, alog_ref, dt_ref, a_ref):
        dt = _softplus_parts(dtr_ref[...] + bias_ref[...])
        dt_ref[...] = dt
        a_ref[...] = dt * (-jnp.exp(alog_ref[...]))

    return pl.pallas_call(body, name=name, out_shape=[jax.ShapeDtypeStruct(dtr.shape, F32)] * 2,
                          compiler_params=_params())(dtr, dt_bias, a_log)


def _dt_bwd(dtr, dt_bias, a_log, dt, ddt, da, *, name):
    def body(dtr_ref, bias_ref, alog_ref, dt_ref, ddt_ref, da_ref, ddtr_ref, dbias_ref, dalog_ref):
        a_neg = -jnp.exp(alog_ref[...])
        da_v = da_ref[...]
        ddt_tot = ddt_ref[...] + da_v * a_neg
        ddtr = ddt_tot * _sigmoid(dtr_ref[...] + bias_ref[...])
        ddtr_ref[...] = ddtr
        dbias_ref[...] = jnp.sum(ddtr, axis=0, keepdims=True)
        dalog_ref[...] = jnp.sum(da_v * dt_ref[...], axis=0, keepdims=True) * a_neg

    return pl.pallas_call(
        body, name=name,
        out_shape=[jax.ShapeDtypeStruct(dtr.shape, F32), jax.ShapeDtypeStruct((1, SSM_HEADS), F32),
                   jax.ShapeDtypeStruct((1, SSM_HEADS), F32)],
        compiler_params=_params())(dtr, dt_bias, a_log, dt, ddt, da)


GROUP_COLS = HEADS_PER_GROUP * SSM_HEAD_DIM
LANES = 128
HEADS_PER_LANE_BLOCK = LANES // SSM_HEAD_DIM


def _split3(x):
    hi = x.astype(BF16)
    r1 = x - hi.astype(F32)
    mid = r1.astype(BF16)
    lo = (r1 - mid.astype(F32)).astype(BF16)
    return hi, mid, lo


def _dot_select(a, b, dims=NN, data=0):
    out = None
    for part in _split3(a if data == 0 else b):
        lhs, rhs = (part, b.astype(BF16)) if data == 0 else (a.astype(BF16), part)
        t = lax.dot_general(lhs, rhs, dims, preferred_element_type=F32)
        out = t if out is None else out + t
    return out


def _group_sums(vals, expand):
    out = _dot_select(jnp.concatenate(vals, axis=0), expand, NT)
    return [out[i * CHUNK:(i + 1) * CHUNK] for i in range(len(vals))]


def _ssd_chunk_common(a_ref, dt_ref, b_ref, c_ref):
    row = lax.broadcasted_iota(jnp.int32, (CHUNK, CHUNK), 0)
    col = lax.broadcasted_iota(jnp.int32, (CHUNK, CHUNK), 1)
    causal = col <= row
    lower = causal.astype(F32)
    upper = (col >= row).astype(F32)
    head = lax.broadcasted_iota(jnp.int32, (HEADS_PER_GROUP, GROUP_COLS), 0)
    lane = lax.broadcasted_iota(jnp.int32, (HEADS_PER_GROUP, GROUP_COLS), 1)
    expand = ((lane >= head * SSM_HEAD_DIM) & (lane < (head + 1) * SSM_HEAD_DIM)).astype(F32)
    a = a_ref[...]
    cs = _dot_select(lower, a, data=1)
    cs_row = _dot_select(a, upper, TN)
    cs_x = _dot_select(cs, expand)
    dt_x = _dot_select(dt_ref[...], expand)
    e_out_x = jnp.exp(cs_x)
    e_st_x = jnp.exp(cs_x[CHUNK - 1:CHUNK, :] - cs_x)
    bc = b_ref[...]
    cc = c_ref[...]
    cb = _dot(cc, bc, NT)
    return causal, upper, expand.astype(BF16), cs, cs_row, dt_x, e_out_x, e_st_x, bc, cc, cb


def _head_decay(causal, cs, cs_row, h):
    return jnp.exp(jnp.where(causal, cs[:, h:h + 1] - cs_row[h:h + 1, :], NEG_BIG))


def _lane_block_head_masks():
    lane = lax.broadcasted_iota(jnp.int32, (CHUNK, LANES), 1)
    return [(lane >= i * SSM_HEAD_DIM) & (lane < (i + 1) * SSM_HEAD_DIM) for i in range(HEADS_PER_LANE_BLOCK)]


def _decay_state(dst_ref, old, new, cs):
    for h in range(HEADS_PER_GROUP):
        rows = slice(h * SSM_HEAD_DIM, (h + 1) * SSM_HEAD_DIM)
        dst_ref[rows, :] = jnp.exp(cs[CHUNK - 1:CHUNK, h:h + 1]) * old[rows, :] + new[rows, :]


def _ssd_fwd(xbc, dtg, ag, dgx, *, name, comm=None):
    t_dim = xbc.shape[0]

    def body(xs_ref, b_ref, c_ref, dt_ref, a_ref, d_ref, y_ref, st_ref, s_scr):
        @pl.when(pl.program_id(1) == 0)
        def _():
            s_scr[...] = jnp.zeros_like(s_scr)

        causal, _, _, cs, cs_row, dt_x, e_out_x, e_st_x, bc, cc, cb = _ssd_chunk_common(a_ref, dt_ref, b_ref, c_ref)
        masks = _lane_block_head_masks()
        xs = xs_ref[...]
        xdt_x = xs * dt_x
        prev = s_scr[...]
        st_ref[...] = prev
        y_off = e_out_x * _dot(cc, prev, NT) + xs * d_ref[...]
        for blk in range(GROUP_COLS // LANES):
            lanes = slice(blk * LANES, (blk + 1) * LANES)
            x_b = xdt_x[:, lanes].astype(BF16)
            acc = y_off[:, lanes]
            for i in range(HEADS_PER_LANE_BLOCK):
                m = cb * _head_decay(causal, cs, cs_row, blk * HEADS_PER_LANE_BLOCK + i)
                acc = acc + _dot(m, jnp.where(masks[i], x_b, jnp.zeros_like(x_b)))
            y_ref[:, lanes] = acc
        _decay_state(s_scr, prev, _dot(xdt_x * e_st_x, bc, TN), cs)

    xs = pl.BlockSpec((CHUNK, GROUP_COLS), lambda g, c: (c, g))
    bsp = pl.BlockSpec((CHUNK, SSM_STATE), lambda g, c: (c, D_INNER // SSM_STATE + g))
    csp = pl.BlockSpec((CHUNK, SSM_STATE), lambda g, c: (c, (D_INNER + GN) // SSM_STATE + g))
    per_head = pl.BlockSpec((None, CHUNK, HEADS_PER_GROUP), lambda g, c: (g, c, 0))
    dsk = pl.BlockSpec((None, 1, GROUP_COLS), lambda g, c: (g, 0, 0))
    return _call(
        body, name=name, grid=(SSM_GROUPS, N_CHUNKS),
        in_specs=[xs, bsp, csp, per_head, per_head, dsk],
        out_specs=[xs, pl.BlockSpec((None, GROUP_COLS, SSM_STATE), lambda g, c: (c, g, 0))],
        out_shape=[jax.ShapeDtypeStruct((t_dim, D_INNER), F32),
                   jax.ShapeDtypeStruct((N_CHUNKS, D_INNER, SSM_STATE), F32)],
        scratch_shapes=[pltpu.VMEM((GROUP_COLS, SSM_STATE), F32)],
        sem=("parallel", "arbitrary"), args=[xbc, xbc, xbc, dtg, ag, dgx], comm=comm)


def _ssd_bwd(xbc, dtg, ag, dgx, states, dy, *, name, comm=None):
    t_dim = xbc.shape[0]
    last = N_CHUNKS - 1

    def body(xs_ref, b_ref, c_ref, dt_ref, a_ref, d_ref, st_ref, dy_ref,
             dxs_ref, db_ref, dc_ref, ddt_ref, da_ref, dd_ref, ds_scr):
        @pl.when(pl.program_id(1) == 0)
        def _():
            ds_scr[...] = jnp.zeros_like(ds_scr)
            dd_ref[...] = jnp.zeros_like(dd_ref)

        causal, upper, expand, cs, cs_row, dt_x, e_out_x, e_st_x, bc, cc, cb = _ssd_chunk_common(a_ref, dt_ref, b_ref, c_ref)
        masks = _lane_block_head_masks()
        xs = xs_ref[...]
        dy_x = dy_ref[...]
        xdt_x = xs * dt_x
        prev = st_ref[...]
        d_s = ds_scr[...]
        g1_x = _dot(bc, d_s, NT)
        cp_x = _dot(cc, prev, NT)
        d_cb = jnp.zeros((CHUNK, CHUNK), F32)
        lane8 = lax.broadcasted_iota(jnp.int32, (CHUNK, HEADS_PER_GROUP), 1)
        sub8 = lax.broadcasted_iota(jnp.int32, (HEADS_PER_GROUP, CHUNK), 0)
        row_w = jnp.zeros((CHUNK, HEADS_PER_GROUP), F32)
        col_w = jnp.zeros((HEADS_PER_GROUP, CHUNK), F32)
        dxdt_blocks = []
        for blk in range(GROUP_COLS // LANES):
            lanes = slice(blk * LANES, (blk + 1) * LANES)
            dy_b = dy_x[:, lanes].astype(BF16)
            x_b = xdt_x[:, lanes].astype(BF16)
            acc_dx = jnp.zeros((CHUNK, LANES), F32)
            for i in range(HEADS_PER_LANE_BLOCK):
                h = blk * HEADS_PER_LANE_BLOCK + i
                decay = _head_decay(causal, cs, cs_row, h)
                m = cb * decay
                dy_h = jnp.where(masks[i], dy_b, jnp.zeros_like(dy_b))
                acc_dx = acc_dx + _dot(m, dy_h, TN)
                d_m = _dot(dy_h, x_b, NT)
                d_cb = d_cb + d_m * decay
                w = d_m * m
                row_w = jnp.where(lane8 == h, jnp.sum(w, axis=1, keepdims=True), row_w)
                col_w = jnp.where(sub8 == h, jnp.sum(w, axis=0, keepdims=True), col_w)
            dxdt_blocks.append(acc_dx)
        dxdt_x = jnp.concatenate(dxdt_blocks, axis=1) + e_st_x * g1_x
        dxs_ref[...] = dxdt_x * dt_x + dy_x * d_ref[...]
        dye = dy_x * e_out_x
        xde = xdt_x * e_st_x
        ddt, y_off, tl, dskip = _group_sums([dxdt_x * xs, dye * cp_x, xde * g1_x, dy_x * xs], expand)
        ddt_ref[...] = ddt
        dd_ref[...] += jnp.sum(dskip, axis=0, keepdims=True)
        sp = None
        for part in _split3(d_s * prev):
            t = lax.dot_general(expand, part, NN, preferred_element_type=F32)
            sp = t if sp is None else sp + t
        last_col = jnp.exp(cs_row[:, CHUNK - 1:CHUNK]) * jnp.sum(sp, axis=1, keepdims=True)
        eye = lax.broadcasted_iota(jnp.int32, (HEADS_PER_GROUP, HEADS_PER_GROUP), 0) == lax.broadcasted_iota(
            jnp.int32, (HEADS_PER_GROUP, HEADS_PER_GROUP), 1)
        last_row = jnp.sum(jnp.where(eye, last_col, 0.0), axis=0, keepdims=True) + jnp.sum(tl, axis=0, keepdims=True)
        is_last = lax.broadcasted_iota(jnp.int32, (CHUNK, 1), 0) == CHUNK - 1
        d_cs = row_w + y_off - tl + jnp.where(is_last, last_row, 0.0)
        da_ref[...] = _dot_select(upper, d_cs, data=1) - _dot_select(upper, col_w, NT, data=1)
        dc_ref[...] = _dot(d_cb, bc) + _dot(dye, prev)
        db_ref[...] = _dot(d_cb, cc, TN) + _dot(xde, d_s)
        _decay_state(ds_scr, d_s, _dot(dye, cc, TN), cs)

    rev = lambda c: last - c
    xs = pl.BlockSpec((CHUNK, GROUP_COLS), lambda g, c: (rev(c), g))
    bsp = pl.BlockSpec((CHUNK, SSM_STATE), lambda g, c: (rev(c), D_INNER // SSM_STATE + g))
    csp = pl.BlockSpec((CHUNK, SSM_STATE), lambda g, c: (rev(c), (D_INNER + GN) // SSM_STATE + g))
    per_head = pl.BlockSpec((None, CHUNK, HEADS_PER_GROUP), lambda g, c: (g, rev(c), 0))
    dsk = pl.BlockSpec((None, 1, GROUP_COLS), lambda g, c: (g, 0, 0))
    dsum = pl.BlockSpec((None, 1, HEADS_PER_GROUP), lambda g, c: (g, 0, 0))
    st = pl.BlockSpec((None, GROUP_COLS, SSM_STATE), lambda g, c: (rev(c), g, 0))
    grp = pl.BlockSpec((CHUNK, SSM_STATE), lambda g, c: (rev(c), g))
    return _call(
        body, name=name, grid=(SSM_GROUPS, N_CHUNKS),
        in_specs=[xs, bsp, csp, per_head, per_head, dsk, st, xs],
        out_specs=[xs, grp, grp, per_head, per_head, dsum],
        out_shape=[jax.ShapeDtypeStruct((t_dim, D_INNER), F32), jax.ShapeDtypeStruct((t_dim, GN), F32),
                   jax.ShapeDtypeStruct((t_dim, GN), F32),
                   jax.ShapeDtypeStruct((SSM_GROUPS, t_dim, HEADS_PER_GROUP), F32),
                   jax.ShapeDtypeStruct((SSM_GROUPS, t_dim, HEADS_PER_GROUP), F32),
                   jax.ShapeDtypeStruct((SSM_GROUPS, 1, HEADS_PER_GROUP), F32)],
        scratch_shapes=[pltpu.VMEM((GROUP_COLS, SSM_STATE), F32)],
        sem=("parallel", "arbitrary"), args=[xbc, xbc, xbc, dtg, ag, dgx, states, dy], comm=comm)


NORM_GROUP = D_INNER // SSM_GROUPS


def _gate_norm_fwd(y, zx, nw, *, name, tm=256):
    t_dim = y.shape[0]
    row = pl.BlockSpec((tm, D_INNER), lambda i: (i, 0))

    def body(y_ref, z_ref, nw_ref, o_ref):
        z = z_ref[...]
        yz = y_ref[...] * (z * _sigmoid(z))
        for g in range(SSM_GROUPS):
            cols = slice(g * NORM_GROUP, (g + 1) * NORM_GROUP)
            yhat, _ = _rms(yz[:, cols])
            o_ref[:, cols] = (yhat * nw_ref[:, cols]).astype(BF16)

    return pl.pallas_call(
        body, name=name, grid=(t_dim // tm,), in_specs=[row, row, pl.BlockSpec((1, D_INNER), lambda i: (0, 0))],
        out_specs=row, out_shape=jax.ShapeDtypeStruct((t_dim, D_INNER), BF16),
        compiler_params=_params("parallel"),
    )(y, zx, nw)


def _gate_norm_bwd(y, zx, nw, dyn, *, name, tm=256):
    t_dim = y.shape[0]
    row = pl.BlockSpec((tm, D_INNER), lambda i: (i, 0))
    vec = pl.BlockSpec((1, D_INNER), lambda i: (0, 0))

    def body(y_ref, z_ref, nw_ref, dyn_ref, dy_ref, dz_ref, dnw_ref):
        @pl.when(pl.program_id(0) == 0)
        def _():
            dnw_ref[...] = jnp.zeros_like(dnw_ref)

        z = z_ref[...]
        yv = y_ref[...]
        sg = _sigmoid(z)
        silu_z = z * sg
        yz = yv * silu_z
        dyn_v = dyn_ref[...]
        for g in range(SSM_GROUPS):
            cols = slice(g * NORM_GROUP, (g + 1) * NORM_GROUP)
            yhat, r = _rms(yz[:, cols])
            dn = dyn_v[:, cols]
            dnw_ref[:, cols] += jnp.sum(dn * yhat, axis=0, keepdims=True)
            dyhat = dn * nw_ref[:, cols]
            dyz = r * (dyhat - yhat * jnp.mean(dyhat * yhat, axis=-1, keepdims=True))
            dy_ref[:, cols] = dyz * silu_z[:, cols]
            dz_ref[:, cols] = dyz * yv[:, cols] * _dsilu(z[:, cols], sg[:, cols])

    return pl.pallas_call(
        body, name=name, grid=(t_dim // tm,), in_specs=[row, row, vec, row],
        out_specs=[row, row, vec],
        out_shape=[jax.ShapeDtypeStruct((t_dim, D_INNER), F32), jax.ShapeDtypeStruct((t_dim, ZX_DIM), F32),
                   jax.ShapeDtypeStruct((1, D_INNER), F32)],
        compiler_params=_params("arbitrary"),
    )(y, zx, nw, dyn)


HEADS_PER_LANE_TILE = LANES // ATT_HEAD_DIM
STACKED_ROWS = Q_PER_KV * WINDOW


def _att_half_masks():
    lane = lax.broadcasted_iota(jnp.int32, (WINDOW, LANES), 1)
    return [(lane >= i * ATT_HEAD_DIM) & (lane < (i + 1) * ATT_HEAD_DIM) for i in range(HEADS_PER_LANE_TILE)]


def _att_stack_heads(ref, kvh, masks):
    parts = []
    for g in range(Q_PER_KV):
        h = kvh * Q_PER_KV + g
        blk = ref[:, (h // HEADS_PER_LANE_TILE) * LANES:(h // HEADS_PER_LANE_TILE + 1) * LANES]
        parts.append(jnp.where(masks[h % HEADS_PER_LANE_TILE], blk, jnp.zeros_like(blk)))
    return jnp.concatenate(parts, axis=0)


def _att_kv_tile(ref, kvh, masks):
    blk = ref[:, (kvh // HEADS_PER_LANE_TILE) * LANES:(kvh // HEADS_PER_LANE_TILE + 1) * LANES]
    return jnp.where(masks[kvh % HEADS_PER_LANE_TILE], blk, pltpu.roll(blk, ATT_HEAD_DIM, 1)).astype(BF16)


def _att_stacked_masks(n):
    row = lax.bitwise_and(lax.broadcasted_iota(jnp.int32, (STACKED_ROWS, WINDOW), 0), WINDOW - 1)
    col = lax.broadcasted_iota(jnp.int32, (STACKED_ROWS, WINDOW), 1)
    return col <= row, (col > row) & (n > 0)


def _att_stack_columns(ref, kvh, rows):
    cols = [ref[:, kvh * Q_PER_KV + g:kvh * Q_PER_KV + g + 1] for g in range(Q_PER_KV)]
    return jnp.concatenate([jnp.broadcast_to(c, (rows, 1)) for c in cols], axis=0)


def _att_scores(q4, k_tile, mask):
    return jnp.where(mask, _dot(q4, k_tile, NT) * ATT_SCALE, NEG_BIG)


def _att_unstack(x4, kvh, masks, tiles):
    for g in range(Q_PER_KV):
        h = kvh * Q_PER_KV + g
        piece = x4[g * WINDOW:(g + 1) * WINDOW]
        t = h // HEADS_PER_LANE_TILE
        tiles[t] = piece if h % HEADS_PER_LANE_TILE == 0 else jnp.where(masks[1], piece, tiles[t])


def _attn_fwd(q, k, v, sinks, *, name, comm=None):
    t_dim = q.shape[0]

    def body(q_ref, kc_ref, kp_ref, vc_ref, vp_ref, s_ref, o_ref, l_ref):
        n = pl.program_id(0)
        masks = _att_half_masks()
        mask_c, mask_p = _att_stacked_masks(n)
        out_tiles = [None] * (D_MODEL // LANES)
        for kvh in range(N_KV_HEADS):
            q4 = _att_stack_heads(q_ref, kvh, masks).astype(BF16)
            kc, kp = _att_kv_tile(kc_ref, kvh, masks), _att_kv_tile(kp_ref, kvh, masks)
            vc, vp = _att_kv_tile(vc_ref, kvh, masks), _att_kv_tile(vp_ref, kvh, masks)
            sc = _att_scores(q4, kc, mask_c)
            sp = _att_scores(q4, kp, mask_p)
            sink = _att_stack_columns(s_ref, kvh, WINDOW)
            m = jnp.maximum(jnp.maximum(jnp.max(sc, axis=1, keepdims=True), jnp.max(sp, axis=1, keepdims=True)), sink)
            pc = jnp.exp(sc - m)
            pp = jnp.exp(sp - m)
            den = jnp.sum(pc, axis=1, keepdims=True) + jnp.sum(pp, axis=1, keepdims=True) + jnp.exp(sink - m)
            _att_unstack((_dot(pc, vc) + _dot(pp, vp)) / den, kvh, masks, out_tiles)
            lse4 = m + jnp.log(den)
            for g in range(Q_PER_KV):
                h = kvh * Q_PER_KV + g
                l_ref[:, h:h + 1] = lse4[g * WINDOW:(g + 1) * WINDOW]
        for t, tile in enumerate(out_tiles):
            o_ref[:, t * LANES:(t + 1) * LANES] = tile

    cur = lambda w: pl.BlockSpec((WINDOW, w), lambda n: (n, 0))
    prv = lambda w: pl.BlockSpec((WINDOW, w), lambda n: (jnp.maximum(n - 1, 0), 0))
    return _call(
        body, name=name, grid=(t_dim // WINDOW,),
        in_specs=[cur(D_MODEL), cur(KV_DIM), prv(KV_DIM), cur(KV_DIM), prv(KV_DIM), pl.BlockSpec((1, N_Q_HEADS), lambda n: (0, 0))],
        out_specs=[cur(D_MODEL), cur(N_Q_HEADS)],
        out_shape=[jax.ShapeDtypeStruct((t_dim, D_MODEL), F32), jax.ShapeDtypeStruct((t_dim, N_Q_HEADS), F32)],
        sem=("parallel",), args=[q, k, k, v, v, sinks], comm=comm)


def _attn_bwd(q, k, v, sinks, o, lse, do, cos2, sin2, *, name, comm=None):
    t_dim = q.shape[0]

    def body(q_ref, kc_ref, kp_ref, vc_ref, vp_ref, s_ref, o_ref, l_ref, do_ref, cos_ref, sin_ref, cos_all_ref, sin_all_ref,
             dq_ref, dk_ref, dv_ref, dsink_ref):
        n = pl.program_id(0)

        @pl.when(n == 0)
        def _():
            dk_ref[...] = jnp.zeros_like(dk_ref)
            dv_ref[...] = jnp.zeros_like(dv_ref)
            dsink_ref[...] = jnp.zeros_like(dsink_ref)

        masks = _att_half_masks()
        mask_c, mask_p = _att_stacked_masks(n)
        lane_row = lax.broadcasted_iota(jnp.int32, (1, N_Q_HEADS), 1)
        rows_c = pl.ds(pl.multiple_of(n * WINDOW, WINDOW), WINDOW)
        rows_p = pl.ds(pl.multiple_of(jnp.maximum(n - 1, 0) * WINDOW, WINDOW), WINDOW)
        dsink = jnp.zeros((1, N_Q_HEADS), F32)
        dq_tiles = [None] * (D_MODEL // LANES)
        kv_tiles = KV_DIM // LANES
        dkc_tiles, dkp_tiles, dvc_tiles, dvp_tiles = ([None] * kv_tiles for _ in range(4))

        def place(tiles, kvh, x):
            folded = x + pltpu.roll(x, ATT_HEAD_DIM, 1)
            t = kvh // HEADS_PER_LANE_TILE
            tiles[t] = folded if kvh % HEADS_PER_LANE_TILE == 0 else jnp.where(masks[1], folded, tiles[t])

        for kvh in range(N_KV_HEADS):
            q4 = _att_stack_heads(q_ref, kvh, masks).astype(BF16)
            do4 = _att_stack_heads(do_ref, kvh, masks)
            o4 = _att_stack_heads(o_ref, kvh, masks)
            kc, kp = _att_kv_tile(kc_ref, kvh, masks), _att_kv_tile(kp_ref, kvh, masks)
            vc, vp = _att_kv_tile(vc_ref, kvh, masks), _att_kv_tile(vp_ref, kvh, masks)
            l4 = _att_stack_columns(l_ref, kvh, WINDOW)
            pc = jnp.exp(_att_scores(q4, kc, mask_c) - l4)
            pp = jnp.exp(_att_scores(q4, kp, mask_p) - l4)
            delta = jnp.sum(do4 * o4, axis=1, keepdims=True)
            do4b = do4.astype(BF16)
            dsc = pc * (_dot(do4b, vc, NT) - delta)
            dsp = pp * (_dot(do4b, vp, NT) - delta)
            _att_unstack((_dot(dsc, kc) + _dot(dsp, kp)) * ATT_SCALE, kvh, masks, dq_tiles)
            place(dkc_tiles, kvh, _dot(dsc, q4, TN) * ATT_SCALE)
            place(dkp_tiles, kvh, _dot(dsp, q4, TN) * ATT_SCALE)
            place(dvc_tiles, kvh, _dot(pc, do4b, TN))
            place(dvp_tiles, kvh, _dot(pp, do4b, TN))
            p_sink = jnp.exp(_att_stack_columns(s_ref, kvh, WINDOW) - l4) * delta
            for g in range(Q_PER_KV):
                h = kvh * Q_PER_KV + g
                dsink = jnp.where(lane_row == h, -jnp.sum(p_sink[g * WINDOW:(g + 1) * WINDOW], axis=0, keepdims=True), dsink)
        for t, tile in enumerate(dq_tiles):
            dq_ref[:, t * LANES:(t + 1) * LANES] = _rope_rotate(tile, cos_ref[...], -sin_ref[...])
        for t in range(kv_tiles):
            lanes = slice(t * LANES, (t + 1) * LANES)
            dk_ref[rows_c, lanes] += dkc_tiles[t]
            dk_ref[rows_p, lanes] += dkp_tiles[t]
            dv_ref[rows_c, lanes] += dvc_tiles[t]
            dv_ref[rows_p, lanes] += dvp_tiles[t]
        dsink_ref[...] += dsink

        @pl.when(n == t_dim // WINDOW - 1)
        def _():
            dk_ref[...] = _rope_rotate(dk_ref[...], cos_all_ref[...], -sin_all_ref[...])

    cur = lambda w: pl.BlockSpec((WINDOW, w), lambda n: (n, 0))
    prv = lambda w: pl.BlockSpec((WINDOW, w), lambda n: (jnp.maximum(n - 1, 0), 0))
    whole = lambda w: pl.BlockSpec((t_dim, w), lambda n: (0, 0))
    svec = pl.BlockSpec((1, N_Q_HEADS), lambda n: (0, 0))
    return _call(
        body, name=name, grid=(t_dim // WINDOW,),
        in_specs=[cur(D_MODEL), cur(KV_DIM), prv(KV_DIM), cur(KV_DIM), prv(KV_DIM), svec, cur(D_MODEL), cur(N_Q_HEADS), cur(D_MODEL),
                  cur(LANES), cur(LANES), whole(LANES), whole(LANES)],
        out_specs=[cur(D_MODEL), whole(KV_DIM), whole(KV_DIM), svec],
        out_shape=[jax.ShapeDtypeStruct((t_dim, D_MODEL), F32), jax.ShapeDtypeStruct((t_dim, KV_DIM), F32),
                   jax.ShapeDtypeStruct((t_dim, KV_DIM), F32), jax.ShapeDtypeStruct((1, N_Q_HEADS), F32)],
        sem=("arbitrary",), args=[q, k, k, v, v, sinks, o, lse, do, cos2, sin2, cos2, sin2], comm=comm)


def _loss_head(x, nw, target, *, name, tm=256):
    t_dim, d_dim = x.shape
    row = pl.BlockSpec((tm, d_dim), lambda i: (i, 0))
    vec = pl.BlockSpec((1, d_dim), lambda i: (0, 0))

    def body(x_ref, nw_ref, tgt_ref, loss_ref, dx_ref, dnw_ref):
        @pl.when(pl.program_id(0) == 0)
        def _():
            loss_ref[...] = jnp.zeros_like(loss_ref)
            dnw_ref[...] = jnp.zeros_like(dnw_ref)

        xhat, r = _rms(x_ref[...])
        err = xhat * nw_ref[...] - tgt_ref[...]
        loss_ref[...] += 0.5 * _sum_all(jnp.mean(err * err, axis=-1, keepdims=True))
        dy = err * (1.0 / d_dim)
        dnw_ref[...] += jnp.sum(dy * xhat, axis=0, keepdims=True)
        dxhat = dy * nw_ref[...]
        dx_ref[...] = r * (dxhat - xhat * jnp.mean(dxhat * xhat, axis=-1, keepdims=True))

    return pl.pallas_call(
        body, name=name, grid=(t_dim // tm,), in_specs=[row, vec, row],
        out_specs=[pl.BlockSpec((1, 1), lambda i: (0, 0)), row, vec],
        out_shape=[jax.ShapeDtypeStruct((1, 1), F32), jax.ShapeDtypeStruct((t_dim, d_dim), F32),
                   jax.ShapeDtypeStruct((1, d_dim), F32)],
        compiler_params=_params("arbitrary"),
    )(x, nw, target)


def _rope_tables():
    pos = jnp.arange(SEQ, dtype=F32)
    inv = 1.0 / (ROPE_THETA ** (jnp.arange(0, ATT_HEAD_DIM, 2, dtype=F32) / ATT_HEAD_DIM))
    ang = pos[:, None] * inv[None, :]
    cos, sin = jnp.cos(ang), jnp.sin(ang)
    return jnp.tile(cos, (1, 4)), jnp.tile(sin, (1, 4))


def _to_groups(t):
    return t.reshape(t.shape[0], SSM_GROUPS, HEADS_PER_GROUP).transpose(1, 0, 2)


def _from_groups(t):
    return t.transpose(1, 0, 2).reshape(t.shape[1], SSM_HEADS)


def _forward_backward(x0, target, net):
    w = net.w
    nw = [[w("norm_w")[l, i][None, :] for i in range(3)] for l in range(2)]
    cos2, sin2 = _rope_tables()
    ffn_norm = [nw[0][0], nw[0][2], nw[1][0], nw[1][2]]

    ffn_pre = {}

    def ffn_f(x, blk):
        name = f"ffn_fwd{blk}"
        out, *ffn_pre[blk] = _ffn_fwd(x, ffn_norm[blk], w(f"gate{blk}"), w(f"up{blk}"), w(f"down{blk}"), name=name,
                                      comm=net.carry(name))
        return out

    x1 = ffn_f(x0, 0)
    zx, h1 = _norm_mm(x1, nw[0][1], w("w_in_t"), None, w_rows=ZX_DIM, name="ssm_in_proj", comm=net.carry("ssm_in_proj"))
    dtr = _mm(h1, w("w_in_t"), dims="nt", b_rows=(ZX_DIM, SSM_HEADS), name="ssm_dt_proj")
    xbc = _conv_fwd(zx, w("conv_w"), w("conv_b"), name="ssm_conv_fwd", comm=net.carry("ssm_conv_fwd"))
    dt, a_dt = _dt_prep(dtr, w("dt_bias"), w("a_log"), name="ssm_dt_prep")
    dtg, ag = _to_groups(dt), _to_groups(a_dt)
    dg = jnp.repeat(w("d_skip").reshape(SSM_GROUPS, 1, HEADS_PER_GROUP), SSM_HEAD_DIM, axis=2)
    y_ssd, states = _ssd_fwd(xbc, dtg, ag, dg, name="ssd_fwd", comm=net.carry("ssd_fwd"))
    yn = _gate_norm_fwd(y_ssd, zx, w("ssm_norm_w"), name="ssm_gate_norm_fwd")
    x2 = _mm(yn, w("wout"), res=x1, name="ssm_out_proj", comm=net.carry("ssm_out_proj"))
    x3 = ffn_f(x2, 1)
    k_rot, hk = _norm_mm(x3, w("kv_norm_w"), w("wk"), w("b_k"), rope=(cos2, sin2), name="k_proj")
    v = _mm(hk, w("wv"), bias=w("b_v"), name="v_proj")
    x4 = ffn_f(x3, 2)
    q_rot, h4 = _norm_mm(x4, nw[1][1], w("wq"), w("b_q"), rope=(cos2, sin2), name="q_proj")
    att, lse = _attn_fwd(q_rot, k_rot, v, w("sinks"), name="attn_fwd", comm=net.carry("attn_fwd"))
    x5 = _mm(att, w("wo"), bias=w("b_o"), res=x4, name="attn_out_proj")
    x6 = ffn_f(x5, 3)
    loss, dx6, d_final = _loss_head(x6, w("final_norm_w"), target, name="loss_head")

    d_norm = [[None] * 3 for _ in range(2)]

    def ffn_b(x, dout, blk):
        h, dob = _ffn_bwd_prep(x, ffn_norm[blk], dout, name=f"ffn_bwd_prep{blk}")
        name = f"ffn_bwd{blk}"
        dh, gg, gu, gd = _ffn_bwd(h, dob, *ffn_pre[blk], w(f"gate{blk}"), w(f"up{blk}"), w(f"down{blk}"), name=name,
                                  comm=net.carry(name))
        net.give(f"gate{blk}", gg)
        net.give(f"up{blk}", gu)
        net.give(f"down{blk}", gd)
        return _norm_bwd(x, ffn_norm[blk], dh, [dout], name=f"ffn_norm_bwd{blk}", comm=net.carry(f"ffn_norm_bwd{blk}"))

    by_rows = lambda g: g.reshape(N_DEV, g.shape[0] // N_DEV, g.shape[1])
    dx5, d_norm[1][2] = ffn_b(x5, dx6, 3)
    d_att = _mm(dx5, w("wo"), dims="nt", name="attn_out_proj_dx", comm=net.carry("attn_out_proj_dx"))
    g_o, d_bo = _mm(att, dx5, dims="tn", out_dtype=BF16, colsum_b=True, name="attn_out_proj_dw")
    net.give("w_o", by_rows(g_o))
    dq, dk, dv, d_sinks = _attn_bwd(q_rot, k_rot, v, w("sinks"), att, lse, d_att, cos2, sin2, name="attn_bwd",
                                    comm=net.carry("attn_bwd"))
    dh4 = _mm(dq, w("wq"), dims="nt", name="q_proj_dx")
    g_q, d_bq = _mm(h4, dq, dims="tn", out_dtype=BF16, colsum_b=True, name="q_proj_dw")
    net.give("w_q", by_rows(g_q))
    dx4, d_norm[1][1] = _norm_bwd(x4, nw[1][1], dh4, [dx5], name="attn_norm_bwd")
    dx3a, d_norm[1][0] = ffn_b(x3, dx4, 2)
    dhk = _mm(dk, w("wk"), dims="nt", name="k_proj_dx", comm=net.carry("k_proj_dx"))
    dhk = _mm(dv, w("wv"), dims="nt", res=dhk, name="v_proj_dx")
    g_k, d_bk = _mm(hk, dk, dims="tn", out_dtype=BF16, colsum_b=True, name="k_proj_dw")
    g_v, d_bv = _mm(hk, dv, dims="tn", out_dtype=BF16, colsum_b=True, name="v_proj_dw")
    net.give("w_k", by_rows(g_k))
    net.give("w_v", by_rows(g_v))
    dx3, d_kvn = _norm_bwd(x3, w("kv_norm_w"), dhk, [dx3a], name="kv_norm_bwd")
    dx2, d_norm[0][2] = ffn_b(x2, dx3, 1)
    d_yn = _mm(dx2, w("wout"), dims="nt", name="ssm_out_proj_dx", comm=net.carry("ssm_out_proj_dx"))
    net.give("w_out", by_rows(_mm(yn, dx2, dims="tn", out_dtype=BF16, name="ssm_out_proj_dw")))
    dy_ssd, dzx, d_ssm_norm = _gate_norm_bwd(y_ssd, zx, w("ssm_norm_w"), d_yn, name="ssm_gate_norm_bwd")
    dxs, d_b, d_c, ddtg, dag, ddg = _ssd_bwd(xbc, dtg, ag, dg, states, dy_ssd, name="ssd_bwd", comm=net.carry("ssd_bwd"))
    dzx, d_conv_w, d_conv_b = _conv_bwd(zx, w("conv_w"), w("conv_b"), dxs, d_b, d_c, dzx, name="ssm_conv_bwd",
                                        comm=net.carry("ssm_conv_bwd"))
    ddtr, d_dt_bias, d_a_log = _dt_bwd(dtr, w("dt_bias"), w("a_log"), dt, _from_groups(ddtg), _from_groups(dag), name="ssm_dt_bwd")
    dh1 = _mm(dzx, w("w_in_t"), b_rows=(0, ZX_DIM), name="ssm_in_proj_dx")
    dh1 = _mm(ddtr, w("w_in_t"), b_rows=(ZX_DIM, SSM_HEADS), res=dh1, name="ssm_dt_proj_dx")
    in_rows = N_DEV * IN_PROJ_SHARD
    g_in = _mm(dzx, h1, dims="tn", out_dtype=BF16, out_window=(0, in_rows), name="ssm_in_proj_dw")
    g_in = _mm(ddtr, h1, dims="tn", out_dtype=BF16, out_window=(ZX_DIM, in_rows), into=g_in, name="ssm_dt_proj_dw")
    net.give("w_in", g_in.reshape(N_DEV, IN_PROJ_SHARD, D_MODEL))
    dx1, d_norm[0][1] = _norm_bwd(x1, nw[0][1], dh1, [dx2], name="ssm_norm_bwd", comm=net.carry("ssm_norm_bwd"))
    dx0, d_norm[0][0] = ffn_b(x0, dx1, 0)

    small = {"norm_w": jnp.concatenate([d_norm[l][i] for l in range(2) for i in range(3)], axis=0),
             "ssm_conv_w": d_conv_w, "ssm_conv_b": d_conv_b, "ssm_dt_bias": d_dt_bias, "ssm_a_log": d_a_log,
             "ssm_d": ddg.reshape(1, SSM_HEADS), "ssm_norm_w": d_ssm_norm, "kv_norm_w": d_kvn,
             "b_k": d_bk, "b_v": d_bv, "attn_b_q": d_bq, "attn_sinks": d_sinks, "attn_b_o": d_bo, "final_norm_w": d_final}
    return loss, dx0, small


BLOCK_BYTES = 1 << 20


def _row_tile(rows, cols):
    for t in (512, 256, 128, 64, 32, 16):
        if rows % t == 0 and t * cols * 4 <= BLOCK_BYTES:
            return t
    return rows


def _cast_bf16(x, *, name):
    n_blk, rows, cols = x.shape
    tm = rows if rows * cols * 4 <= 2 * BLOCK_BYTES else _row_tile(rows, cols)
    spec = pl.BlockSpec((None, tm, cols), lambda b, i: (b, i, 0))

    def body(x_ref, o_ref):
        o_ref[...] = x_ref[...].astype(BF16)

    return pl.pallas_call(body, name=name, grid=(n_blk, rows // tm), in_specs=[spec], out_specs=spec,
                          out_shape=jax.ShapeDtypeStruct(x.shape, BF16), compiler_params=_params("parallel", "parallel"))(x)


def _pair_add(grad, theirs, *, name):
    n_slots, rows, cols = theirs.shape
    tm = rows if rows * cols * 4 <= 2 * BLOCK_BYTES else _row_tile(rows, cols)

    def body(g_ref, t_ref, o_ref):
        mine = jnp.where(lax.axis_index("c") == 0, g_ref[0].astype(F32), g_ref[1].astype(F32))
        o_ref[...] = (mine + t_ref[...].astype(F32)).astype(BF16)

    spec = pl.BlockSpec((None, tm, cols), lambda s, i: (s, i, 0))
    return pl.pallas_call(
        body, name=name, grid=(n_slots, rows // tm),
        in_specs=[pl.BlockSpec((2, tm, cols), lambda s, i: (s, i, 0)), spec], out_specs=spec,
        out_shape=jax.ShapeDtypeStruct(theirs.shape, BF16), compiler_params=_params("parallel", "parallel"),
    )(grad, theirs)


def _adam_update(g, w, m, v):
    m = ADAM_B1 * m + (1.0 - ADAM_B1) * g
    v = ADAM_B2 * v + (1.0 - ADAM_B2) * (g * g)
    m_hat = m / (1.0 - ADAM_B1 ** ADAM_STEP)
    v_hat = v / (1.0 - ADAM_B2 ** ADAM_STEP)
    delta = -ADAM_LR * (m_hat / (jnp.sqrt(v_hat) + ADAM_EPS) + ADAM_WD * w)
    return delta, m, v


def _adamw(parts, w, m, v, first_blk, prev, *, name, comm=None):
    n_blk, rows, cols = w.shape
    tm = _row_tile(rows, cols)
    n_tiles = rows // tm
    spec = pl.BlockSpec((None, tm, cols), lambda b, i: (first_blk + b, i, 0))
    n_prev, n_here = len(prev), len(parts)
    n_parts = parts[0].shape[0]

    def part_spec(q):
        return pl.BlockSpec((n_parts, tm, cols), lambda b, i: (0, jnp.where(b < q, 0, jnp.where(b == q, i, n_tiles - 1)), 0))

    def body(*refs):
        p_refs = refs[:n_here]
        w_ref, m_ref, v_ref = refs[n_here:n_here + 3]
        g_ref, d_ref, nm_ref, nv_ref = refs[n_here + 3 + n_prev:]
        b = pl.program_id(0)
        g = None
        for s in range(n_parts):
            t = p_refs[0][s]
            for q in range(1, n_here):
                t = jnp.where(b == q, p_refs[q][s], t)
            g = t.astype(F32) if g is None else g + t.astype(F32)
        delta, nm, nv = _adam_update(g, w_ref[...], m_ref[...], v_ref[...])
        g_ref[...] = g
        d_ref[...] = delta
        nm_ref[...] = nm
        nv_ref[...] = nv

    return _call(
        body, name=name, grid=(n_here, n_tiles),
        in_specs=[part_spec(q) for q in range(n_here)] + [spec, spec, spec] + [pl.BlockSpec(memory_space=pl.ANY)] * n_prev,
        out_specs=[spec] * 4, out_shape=[jax.ShapeDtypeStruct((n_blk, rows, cols), F32)] * 4,
        aliases={n_here + 3 + q: q for q in range(n_prev)}, sem=("arbitrary", "arbitrary"),
        args=[*parts, w, m, v, *prev], comm=comm)


def _sum_parts(parts, *, name):
    def body(p_ref, o_ref):
        g = p_ref[0]
        for s in range(1, N_DEV):
            g = g + p_ref[s]
        o_ref[...] = g

    return pl.pallas_call(body, name=name, out_shape=jax.ShapeDtypeStruct(parts.shape[1:], F32), compiler_params=_params())(parts)


def _adamw_packed(g, w, m, v, *, name):
    def body(g_ref, w_ref, m_ref, v_ref, d_ref, nm_ref, nv_ref):
        delta, nm, nv = _adam_update(g_ref[...], w_ref[...], m_ref[...], v_ref[...])
        d_ref[...] = delta
        nm_ref[...] = nm
        nv_ref[...] = nv

    return pl.pallas_call(body, name=name, out_shape=[jax.ShapeDtypeStruct(g.shape, F32)] * 3, compiler_params=_params())(g, w, m, v)


SUBLANES = 8


def _pack(arrs):
    rows = []
    for a in arrs:
        a2 = a.reshape(-1, a.shape[-1])
        a2 = jnp.pad(a2, ((0, 0), (0, (-a2.shape[1]) % LANES)))
        rows += [a2[:, i * LANES:(i + 1) * LANES] for i in range(a2.shape[1] // LANES)]
    out = jnp.concatenate(rows, axis=0)
    return jnp.pad(out, ((0, (-out.shape[0]) % SUBLANES), (0, 0)))


def _unpack(packed, shapes):
    outs, r = [], 0
    for shp in shapes:
        lead, cols = math.prod(shp[:-1]), shp[-1]
        n_blocks = -(-cols // LANES)
        blocks = [packed[r + i * lead:r + (i + 1) * lead] for i in range(n_blocks)]
        outs.append(jnp.concatenate(blocks, axis=1)[:, :cols].reshape(shp))
        r += n_blocks * lead
    return outs


WEIGHT_NAMES = ("norm_w", "ffn_w_gate", "ffn_w_up", "ffn_w_down", "ssm_w_in", "ssm_conv_w", "ssm_conv_b", "ssm_dt_bias",
                "ssm_a_log", "ssm_d", "ssm_norm_w", "ssm_w_out", "kv_norm_w", "w_k", "b_k", "w_v", "b_v", "attn_w_q",
                "attn_b_q", "attn_sinks", "attn_w_o", "attn_b_o", "final_norm_w")
MATRIX_NAMES = ("ffn_w_gate", "ffn_w_up", "ffn_w_down", "ssm_w_in", "ssm_w_out", "w_k", "w_v", "attn_w_q", "attn_w_o")
VECTOR_NAMES = tuple(n for n in WEIGHT_NAMES if n not in MATRIX_NAMES)
SHARDED_VECTORS = ("norm_w", "ssm_conv_w", "ssm_conv_b", "ssm_norm_w")


GATHER_PLAN = {
    "gather_stage0": ("gate0", "up0", "down0", "vec"),
    "ffn_fwd0": ("w_in",),
    "ssm_in_proj": ("w_out", "gate1"),
    "ssm_conv_fwd": ("w_k", "w_v", "up1"),
    "ssd_fwd": ("down1", "gate2", "up2"),
    "ssm_out_proj": ("w_q", "w_o"),
    "ffn_fwd1": ("down2", "gate3"),
    "ffn_fwd2": ("up3",),
    "attn_fwd": ("down3",),
}
PAIR_PLAN = {
    "attn_out_proj_dx": ("gate3", "up3", "down3"),
    "ffn_bwd2": ("w_q", "w_o"),
    "k_proj_dx": ("gate2", "up2", "down2"),
    "ssm_out_proj_dx": ("w_k", "w_v", "gate1", "up1", "down1"),
    "ssd_bwd": ("w_out",),
    "ssm_norm_bwd": ("w_in",),
    "ffn_norm_bwd0": ("gate0", "up0", "down0"),
}
CHIP_PLAN = {
    "attn_bwd": ("gate3", "up3"),
    "ffn_bwd2": ("down3",),
    "ffn_bwd1": ("gate2", "up2", "w_q", "w_o"),
    "ssd_bwd": ("down2", "gate1", "up1", "down1", "w_k", "w_v"),
    "ssm_conv_bwd": ("w_out",),
    "ffn_bwd0": ("w_in",),
    "adamw_gate": ("gate0",),
    "adamw_up": ("up0",),
    "adamw_down": ("down0",),
}
FFN_PARAMS = {"gate": "ffn_w_gate", "up": "ffn_w_up", "down": "ffn_w_down"}
SINGLE_MATRICES = {"w_in": "ssm_w_in", "w_out": "ssm_w_out", "w_k": "w_k", "w_v": "w_v", "w_q": "attn_w_q", "w_o": "attn_w_o"}


TRANSPOSED = ("ffn_w_gate", "ffn_w_up", "ssm_w_in")


def _matrix_view(name, a):
    if name in TRANSPOSED:
        a = jnp.swapaxes(a, -1, -2)
    return a.reshape((-1,) + a.shape[-2:])


def _from_matrix_view(name, a, shape):
    if name in TRANSPOSED:
        return jnp.swapaxes(a.reshape(shape[:-2] + (shape[-1], shape[-2])), -1, -2)
    return a.reshape(shape)


class _MeshNet:
    def __init__(self, p):
        self.p = p
        self.views = {n: _matrix_view(n, p[n]) for n in MATRIX_NAMES}
        self.local = {"vec": _pack([p[n] for n in SHARDED_VECTORS])}
        for short, n in FFN_PARAMS.items():
            cast = _cast_bf16(self.views[n], name=f"cast_{short}")
            self.local.update({f"{short}{k}": (cast, k) for k in range(N_FFN)})
        for short, n in SINGLE_MATRICES.items():
            self.local[short] = (_cast_bf16(self.views[n], name=f"cast_{short}"), 0)
        self.gathered_at, self.pairs_at, self.parts_at, self.grads, self.cache = {}, {}, {}, {}, {}

    def carry(self, name):
        comms = []
        if name in GATHER_PLAN:
            keys, comm = GATHER_PLAN[name], _Gather([self.local[k] for k in GATHER_PLAN[name]])
            self.gathered_at.update({k: (comm, i) for i, k in enumerate(keys)})
            comms.append(comm)
        if name in CHIP_PLAN:
            sums = []
            for k in CHIP_PLAN[name]:
                comm, i = self.pairs_at[k]
                sums.append(_pair_add(self.grads[k], comm.results[i], name=f"pair_add_{k}"))
            comm = _ChipExchange(sums)
            self.parts_at.update({k: (comm, i) for i, k in enumerate(CHIP_PLAN[name])})
            comms.append(comm)
        if name in PAIR_PLAN:
            keys, comm = PAIR_PLAN[name], _PairSwap([self.grads[k] for k in PAIR_PLAN[name]])
            self.pairs_at.update({k: (comm, i) for i, k in enumerate(keys)})
            comms.append(comm)
        return comms

    def run(self, name):
        for comm in self.carry(name):
            _run_exchange(comm, name=name)

    def give(self, key, grad):
        self.grads[key] = grad

    def parts(self, key):
        comm, i = self.parts_at[key]
        return comm.results[i]

    def _gathered(self, key):
        comm, i = self.gathered_at[key]
        return comm.results[i]

    def _vec(self, r0, lead, n_blocks):
        vecs = self._gathered("vec")
        return jnp.concatenate([vecs[d, r0 + i * lead:r0 + (i + 1) * lead, :] for d in range(N_DEV) for i in range(n_blocks)], axis=1)

    def _derive(self, name):
        p = self.p
        if name[:-1] in FFN_PARAMS:
            return self._gathered(name)
        if name == "w_in_t":
            return self._gathered("w_in").reshape(N_DEV * IN_PROJ_SHARD, D_MODEL)
        by_rows = {"wout": "w_out", "wk": "w_k", "wv": "w_v", "wq": "w_q", "wo": "w_o"}
        if name in by_rows:
            g = self._gathered(by_rows[name])
            return g.reshape(N_DEV * g.shape[1], g.shape[2])
        vectors = {"norm_w": lambda: self._vec(0, 6, 1).reshape(2, 3, D_MODEL), "conv_w": lambda: self._vec(6, CONV_WIDTH, 3),
                   "conv_b": lambda: self._vec(18, 1, 3), "ssm_norm_w": lambda: self._vec(21, 1, 2)}
        if name in vectors:
            return vectors[name]()
        replicated = {"dt_bias": p["ssm_dt_bias"], "a_log": p["ssm_a_log"], "d_skip": p["ssm_d"], "kv_norm_w": p["kv_norm_w"][None],
                      "b_k": p["b_k"][None], "b_v": p["b_v"][None], "b_q": p["attn_b_q"], "sinks": p["attn_sinks"],
                      "b_o": p["attn_b_o"], "final_norm_w": p["final_norm_w"][None]}
        return replicated[name]

    def w(self, name):
        if name not in self.cache:
            self.cache[name] = self._derive(name)
        return self.cache[name]


def _step(x, target, p, m, v):
    pos = _slot(_position())
    net = _MeshNet(p)
    net.run("gather_stage0")
    loss, grad_x, small = _forward_backward(x, target, net)

    grads, deltas, new_m, new_v = {}, {}, {}, {}
    view = lambda d, n: _matrix_view(n, d[n])
    vec_gather = _Gather([_pack([small[n] for n in VECTOR_NAMES])])
    for short, n in SINGLE_MATRICES.items():
        outs = _adamw([net.parts(short)], net.views[n], view(m, n), view(v, n), 0, [], name=f"adamw_{short}",
                      comm=[vec_gather] if short == "w_in" else None)
        grads[n], deltas[n], new_m[n], new_v[n] = [_from_matrix_view(n, o, p[n].shape) for o in outs]
    ffn_outs = {}
    for short, n in FFN_PARAMS.items():
        ffn_outs[short] = _adamw([net.parts(f"{short}{k}") for k in range(1, N_FFN)], net.views[n], view(m, n), view(v, n), 1, [],
                                 name=f"adamw_{short}", comm=net.carry(f"adamw_{short}"))
    for short, n in FFN_PARAMS.items():
        outs = _adamw([net.parts(f"{short}0")], net.views[n], view(m, n), view(v, n), 0, ffn_outs[short], name=f"adamw_{short}0")
        grads[n], deltas[n], new_m[n], new_v[n] = [_from_matrix_view(n, o, p[n].shape) for o in outs]
    vec_sum = _sum_parts(vec_gather.results[0], name="sum_vector_grads")
    full_shapes = {"norm_w": (2, 3, D_MODEL), "ssm_conv_w": (1, CONV_WIDTH, CONV_DIM), "ssm_conv_b": (1, CONV_DIM),
                   "ssm_norm_w": (1, D_INNER)}
    vec_full = dict(zip(VECTOR_NAMES, _unpack(vec_sum, [full_shapes.get(n, p[n].shape) for n in VECTOR_NAMES])))
    for n in VECTOR_NAMES:
        g = vec_full[n]
        if n in SHARDED_VECTORS:
            per = p[n].shape[-1]
            g = lax.dynamic_slice_in_dim(g, pos * per, per, axis=g.ndim - 1)
        grads[n] = g
    packed = _adamw_packed(*[_pack([d[n] for n in VECTOR_NAMES]) for d in (grads, p, m, v)], name="adamw_vectors")
    shapes = [p[n].shape for n in VECTOR_NAMES]
    for d, pk in zip((deltas, new_m, new_v), packed):
        d.update(zip(VECTOR_NAMES, _unpack(pk, shapes)))
    return loss, grad_x, grads, deltas, new_m, new_v


def kernel(x, norm_w, ffn_w_gate, ffn_w_up, ffn_w_down, ssm_w_in, ssm_conv_w, ssm_conv_b, ssm_dt_bias, ssm_a_log, ssm_d, ssm_norm_w, ssm_w_out, kv_norm_w, w_k, b_k, w_v, b_v, attn_w_q, attn_b_q, attn_sinks, attn_w_o, attn_b_o, final_norm_w, loss_target, m_norm_w, m_ffn_w_gate, m_ffn_w_up, m_ffn_w_down, m_ssm_w_in, m_ssm_conv_w, m_ssm_conv_b, m_ssm_dt_bias, m_ssm_a_log, m_ssm_d, m_ssm_norm_w, m_ssm_w_out, m_kv_norm_w, m_w_k, m_b_k, m_w_v, m_b_v, m_attn_w_q, m_attn_b_q, m_attn_sinks, m_attn_w_o, m_attn_b_o, m_final_norm_w, v_norm_w, v_ffn_w_gate, v_ffn_w_up, v_ffn_w_down, v_ssm_w_in, v_ssm_conv_w, v_ssm_conv_b, v_ssm_dt_bias, v_ssm_a_log, v_ssm_d, v_ssm_norm_w, v_ssm_w_out, v_kv_norm_w, v_w_k, v_b_k, v_w_v, v_b_v, v_attn_w_q, v_attn_b_q, v_attn_sinks, v_attn_w_o, v_attn_b_o, v_final_norm_w):
    p = dict(zip(WEIGHT_NAMES, (norm_w, ffn_w_gate, ffn_w_up, ffn_w_down, ssm_w_in, ssm_conv_w, ssm_conv_b, ssm_dt_bias, ssm_a_log, ssm_d, ssm_norm_w, ssm_w_out, kv_norm_w, w_k, b_k, w_v, b_v, attn_w_q, attn_b_q, attn_sinks, attn_w_o, attn_b_o, final_norm_w)))
    m = dict(zip(WEIGHT_NAMES, (m_norm_w, m_ffn_w_gate, m_ffn_w_up, m_ffn_w_down, m_ssm_w_in, m_ssm_conv_w, m_ssm_conv_b, m_ssm_dt_bias, m_ssm_a_log, m_ssm_d, m_ssm_norm_w, m_ssm_w_out, m_kv_norm_w, m_w_k, m_b_k, m_w_v, m_b_v, m_attn_w_q, m_attn_b_q, m_attn_sinks, m_attn_w_o, m_attn_b_o, m_final_norm_w)))
    v = dict(zip(WEIGHT_NAMES, (v_norm_w, v_ffn_w_gate, v_ffn_w_up, v_ffn_w_down, v_ssm_w_in, v_ssm_conv_w, v_ssm_conv_b, v_ssm_dt_bias, v_ssm_a_log, v_ssm_d, v_ssm_norm_w, v_ssm_w_out, v_kv_norm_w, v_w_k, v_b_k, v_w_v, v_b_v, v_attn_w_q, v_attn_b_q, v_attn_sinks, v_attn_w_o, v_attn_b_o, v_final_norm_w)))
    loss, grad_x, grads, deltas, new_m, new_v = _step(x[0], loss_target[0], p, m, v)
    loss = lax.psum(loss[0, 0], ("x", "y", "c"))
    return (loss, grad_x[None], *[grads[n] for n in WEIGHT_NAMES], *[deltas[n] for n in WEIGHT_NAMES],
            *[new_m[n] for n in WEIGHT_NAMES], *[new_v[n] for n in WEIGHT_NAMES])
```

```python
import functools
import math

import jax
import jax.numpy as jnp
from jax import lax
from jax.experimental import pallas as pl
from jax.experimental.pallas import tpu as pltpu

F32 = jnp.float32
BF16 = jnp.bfloat16

N_DEV = 8
SEQ = 2048
D_MODEL = 1024
D_FF_SHARD = 352
N_FFN = 4
D_INNER = 2048
SSM_HEADS = 32
SSM_HEAD_DIM = 64
SSM_GROUPS = 4
HEADS_PER_GROUP = 8
SSM_STATE = 128
CHUNK = 128
N_CHUNKS = SEQ // CHUNK
GN = SSM_GROUPS * SSM_STATE
CONV_DIM = D_INNER + 2 * GN
CONV_WIDTH = 4
ZX_DIM = D_INNER + CONV_DIM
IN_PROJ_SHARD = 644
ATT_HEAD_DIM = 64
N_Q_HEADS = 16
N_KV_HEADS = 4
Q_PER_KV = 4
KV_DIM = N_KV_HEADS * ATT_HEAD_DIM
WINDOW = 128
ROPE_THETA = 10000.0
EPS = 1e-5
FFN_RES_WEIGHT = 0.5
ATT_SCALE = 1.0 / math.sqrt(ATT_HEAD_DIM)
NEG_BIG = -1e30

ADAM_LR = 0.001
ADAM_B1 = 0.9
ADAM_B2 = 0.999
ADAM_EPS = 1e-08
ADAM_WD = 0.01
ADAM_STEP = 10

VMEM_LIMIT_BYTES = 56 * 1024 * 1024
FFN_BWD_VMEM_LIMIT_BYTES = 61 * 1024 * 1024

NN = (((1,), (0,)), ((), ()))
NT = (((1,), (1,)), ((), ()))
TN = (((0,), (0,)), ((), ()))
_DIMS = {"nn": NN, "nt": NT, "tn": TN}


def _params(*sem):
    return pltpu.CompilerParams(dimension_semantics=sem if sem else None, vmem_limit_bytes=VMEM_LIMIT_BYTES)


def _dot(a, b, dims=NN):
    return lax.dot_general(a.astype(BF16), b.astype(BF16), dims, preferred_element_type=F32)


def _sigmoid(x):
    return 1.0 / (1.0 + jnp.exp(-x))


def _dsilu(x, s):
    return s * (1.0 + x * (1.0 - s))


def _rms(x):
    r = lax.rsqrt(jnp.mean(x * x, axis=-1, keepdims=True) + EPS)
    return x * r, r


def _sum_all(x):
    return jnp.sum(jnp.sum(x, axis=1, keepdims=True), axis=0, keepdims=True)


MESH = pl.DeviceIdType.MESH
N_PEERS = N_DEV - 1
N_CHIPS = N_DEV // 2


def _position():
    return lax.axis_index("x"), lax.axis_index("y"), lax.axis_index("c")


def _slot(p):
    return 4 * p[0] + 2 * p[1] + p[2]


class _Exchange:
    def __init__(self, arrays, out_shapes):
        n = len(arrays)
        self.arrays = list(arrays)
        self.out_shapes = out_shapes
        self.scratch = [pltpu.SemaphoreType.DMA((n, N_PEERS)), pltpu.SemaphoreType.DMA((n, N_PEERS)), pltpu.SemaphoreType.DMA((n,))]
        self.results = None


class _Gather(_Exchange):
    def __init__(self, pieces):
        pieces = [p if isinstance(p, tuple) else (p, None) for p in pieces]
        self.blocks = [k for _, k in pieces]
        shapes = [a.shape if k is None else a.shape[1:] for a, k in pieces]
        super().__init__([a for a, _ in pieces], [jax.ShapeDtypeStruct((N_DEV,) + s, a.dtype) for s, (a, _) in zip(shapes, pieces)])

    def _plan(self, ins, outs, sems):
        send_sems, recv_sems, local_sems = sems
        x, y, c = _position()
        me, sibling = (x, y, c), (x, y, 1 - c)
        chips = [(1 - x, y), (x, 1 - y), (1 - x, 1 - y)]
        n = len(ins)
        ins = [r if k is None else r.at[k] for r, k in zip(ins, self.blocks)]

        def copy(a, k, block, to, src=None):
            dst = outs[a].at[_slot(block)]
            return pltpu.make_async_remote_copy(src_ref=dst if src is None else src, dst_ref=dst, send_sem=send_sems.at[a, k],
                                                recv_sem=recv_sems.at[a, k], device_id=to, device_id_type=MESH)

        mine = [pltpu.make_async_copy(ins[a], outs[a].at[_slot(me)], local_sems.at[a]) for a in range(n)]
        first = []
        for a in range(n):
            first.append(copy(a, 0, me, sibling, src=ins[a]))
            first += [copy(a, 1 + j, me, (*chip, c), src=ins[a]) for j, chip in enumerate(chips)]
        return n, c, me, sibling, chips, copy, mine, first

    def start(self, ins, outs, sems):
        _, _, _, _, _, _, mine, first = self._plan(ins, outs, sems)
        for cp in mine + first:
            cp.start()

    def finish(self, ins, outs, sems):
        n, c, me, sibling, chips, copy, mine, first = self._plan(ins, outs, sems)
        passed = []
        for j, chip in enumerate(chips):
            for a in range(n):
                copy(a, 1 + j, (*chip, c), me).wait_recv()
                fwd = copy(a, 4 + j, (*chip, c), sibling)
                fwd.start()
                passed.append(fwd)
        for a in range(n):
            copy(a, 0, sibling, me).wait_recv()
            for j, chip in enumerate(chips):
                copy(a, 4 + j, (*chip, 1 - c), me).wait_recv()
        for cp in first + passed:
            cp.wait_send()
        for cp in mine:
            cp.wait()


class _PairSwap(_Exchange):
    def __init__(self, arrays):
        n = len(arrays)
        self.arrays = list(arrays)
        self.out_shapes = [jax.ShapeDtypeStruct((N_CHIPS,) + a.shape[1:], a.dtype) for a in arrays]
        self.scratch = [pltpu.SemaphoreType.DMA((n, N_CHIPS)), pltpu.SemaphoreType.DMA((n, N_CHIPS))]
        self.results = None

    def _plan(self, ins, outs, sems):
        send_sems, recv_sems = sems
        x, y, c = _position()
        return [pltpu.make_async_remote_copy(src_ref=ins[a].at[2 * q + 1 - c], dst_ref=outs[a].at[q], send_sem=send_sems.at[a, q],
                                             recv_sem=recv_sems.at[a, q], device_id=(x, y, 1 - c), device_id_type=MESH)
                for a in range(len(ins)) for q in range(N_CHIPS)]

    def start(self, ins, outs, sems):
        for cp in self._plan(ins, outs, sems):
            cp.start()

    def finish(self, ins, outs, sems):
        for cp in self._plan(ins, outs, sems):
            cp.wait()


class _ChipExchange(_Exchange):
    def __init__(self, arrays):
        n = len(arrays)
        self.arrays = list(arrays)
        self.out_shapes = [jax.ShapeDtypeStruct(a.shape, a.dtype) for a in arrays]
        self.scratch = [pltpu.SemaphoreType.DMA((n, 3)), pltpu.SemaphoreType.DMA((n, 3)), pltpu.SemaphoreType.DMA((n,))]
        self.results = None

    def _plan(self, ins, outs, sems):
        send_sems, recv_sems, local_sems = sems
        x, y, c = _position()
        here = 2 * x + y
        chips = [(1 - x, y), (x, 1 - y), (1 - x, 1 - y)]
        n = len(ins)

        def copy(a, k, src_slot, dst_slot):
            return pltpu.make_async_remote_copy(src_ref=ins[a].at[src_slot], dst_ref=outs[a].at[dst_slot], send_sem=send_sems.at[a, k],
                                                recv_sem=recv_sems.at[a, k], device_id=(*chips[k], c), device_id_type=MESH)

        there = [2 * qx + qy for qx, qy in chips]
        mine = [pltpu.make_async_copy(ins[a].at[here], outs[a].at[here], local_sems.at[a]) for a in range(n)]
        sends = [copy(a, k, there[k], here) for a in range(n) for k in range(3)]
        arrivals = lambda: [copy(a, k, here, there[k]) for a in range(n) for k in range(3)]
        return mine, sends, arrivals

    def start(self, ins, outs, sems):
        mine, sends, _ = self._plan(ins, outs, sems)
        for cp in mine + sends:
            cp.start()

    def finish(self, ins, outs, sems):
        mine, sends, arrivals = self._plan(ins, outs, sems)
        for cp in arrivals():
            cp.wait_recv()
        for cp in sends:
            cp.wait_send()
        for cp in mine:
            cp.wait()


def _call(body, *, name, grid, in_specs, out_specs, out_shape, args, scratch_shapes=(), sem=(), comm=(), aliases=None,
          vmem_limit=VMEM_LIMIT_BYTES):
    single = not isinstance(out_shape, (list, tuple))
    out_shape = [out_shape] if single else list(out_shape)
    out_specs = [out_specs] if single else list(out_specs)
    comms = list(comm or ())
    n_in, n_out, n_scr = len(args), len(out_shape), len(scratch_shapes)
    params = pltpu.CompilerParams(dimension_semantics=tuple(sem) if sem else None, vmem_limit_bytes=vmem_limit)
    if not comms:
        res = pl.pallas_call(body, name=name, grid=grid, in_specs=list(in_specs), out_specs=out_specs, out_shape=out_shape,
                             scratch_shapes=list(scratch_shapes), input_output_aliases=aliases or {}, compiler_params=params)(*args)
        return res[0] if single else res
    counts = [n_in] + [len(c.arrays) for c in comms] + [n_out] + [len(c.out_shapes) for c in comms] + [n_scr] + [len(c.scratch) for c in comms]
    nc = len(comms)

    def carried(*refs):
        pos, groups = 0, []
        for cnt in counts:
            groups.append(refs[pos:pos + cnt])
            pos += cnt
        ins, c_ins = groups[0], groups[1:1 + nc]
        outs, c_outs = groups[1 + nc], groups[2 + nc:2 + 2 * nc]
        scr, c_sems = groups[2 + 2 * nc], groups[3 + 2 * nc:]
        ids = [pl.program_id(d) for d in range(len(grid))]
        is_first = functools.reduce(jnp.logical_and, [i == 0 for i in ids])
        is_last = functools.reduce(jnp.logical_and, [i == g - 1 for i, g in zip(ids, grid)])

        @pl.when(is_first)
        def _():
            for q, c in enumerate(comms):
                c.start(c_ins[q], c_outs[q], c_sems[q])

        body(*ins, *outs, *scr)

        @pl.when(is_last)
        def _():
            for q, c in enumerate(comms):
                c.finish(c_ins[q], c_outs[q], c_sems[q])

    anyspec = pl.BlockSpec(memory_space=pl.ANY)
    c_arrays = [a for c in comms for a in c.arrays]
    c_shapes = [s for c in comms for s in c.out_shapes]
    res = pl.pallas_call(
        carried, name=name, grid=grid, in_specs=list(in_specs) + [anyspec] * len(c_arrays), out_specs=out_specs + [anyspec] * len(c_shapes),
        out_shape=out_shape + c_shapes, scratch_shapes=list(scratch_shapes) + [s for c in comms for s in c.scratch],
        input_output_aliases=aliases or {}, compiler_params=params)(*args, *c_arrays)
    pos = n_out
    for c in comms:
        c.results = list(res[pos:pos + len(c.out_shapes)])
        pos += len(c.out_shapes)
    return res[0] if single else list(res[:n_out])


def _run_exchange(comm, *, name):
    def body(*refs):
        n_ci, n_co = len(comm.arrays), len(comm.out_shapes)
        ins, outs, sems = refs[:n_ci], refs[n_ci:n_ci + n_co], refs[n_ci + n_co:]
        comm.start(ins, outs, sems)
        comm.finish(ins, outs, sems)

    anyspec = pl.BlockSpec(memory_space=pl.ANY)
    comm.results = list(pl.pallas_call(
        body, name=name, in_specs=[anyspec] * len(comm.arrays), out_specs=[anyspec] * len(comm.out_shapes),
        out_shape=list(comm.out_shapes), scratch_shapes=list(comm.scratch))(*comm.arrays))
    return comm.results


def _mm(a, b, *, dims="nn", bias=None, res=None, out_dtype=F32, name, tm=1024, tn=1024, tk=1024, comm=None, b_rows=None,
        out_window=None, into=None, colsum_b=False):
    if dims == "tn":
        k_dim, m_dim = a.shape
    else:
        m_dim, k_dim = a.shape
    row0, n_rows = b_rows if b_rows is not None else (0, b.shape[0])
    n_dim = n_rows if dims == "nt" else b.shape[1]
    assert dims == "nt" or n_rows == k_dim, (name, a.shape, b.shape, b_rows)
    tm, tn, tk = min(tm, m_dim), min(tn, n_dim), min(tk, k_dim)
    assert m_dim % tm == 0 and n_dim % tn == 0 and k_dim % tk == 0, (name, a.shape, b.shape)
    nk = k_dim // tk
    a_spec = pl.BlockSpec((tk, tm), lambda i, j, k: (k, i)) if dims == "tn" else pl.BlockSpec((tm, tk), lambda i, j, k: (i, k))
    if dims == "nt":
        assert row0 % tn == 0
        b_spec = pl.BlockSpec((tn, tk), lambda i, j, k: (row0 // tn + j, k))
    else:
        assert row0 % tk == 0
        b_spec = pl.BlockSpec((tk, tn), lambda i, j, k: (row0 // tk + k, j))
    in_specs, args = [a_spec, b_spec], [a, b]
    if bias is not None:
        in_specs.append(pl.BlockSpec((1, tn), lambda i, j, k: (0, j)))
        args.append(bias)
    if res is not None:
        in_specs.append(pl.BlockSpec((tm, tn), lambda i, j, k: (i, j)))
        args.append(res)
    dn = _DIMS[dims]

    if colsum_b:
        assert dims == "tn" and m_dim == tm and into is None and out_window is None

    def body(*refs):
        a_ref, b_ref = refs[0], refs[1]
        acc_ref = refs[-1]
        o_ref = refs[-3] if colsum_b else refs[-2]
        k = pl.program_id(2)

        @pl.when(k == 0)
        def _():
            acc_ref[...] = jnp.zeros_like(acc_ref)
            if colsum_b:
                refs[-2][...] = jnp.zeros_like(refs[-2])

        acc_ref[...] += _dot(a_ref[...], b_ref[...], dn)
        if colsum_b:
            refs[-2][...] += jnp.sum(b_ref[...].astype(F32), axis=0, keepdims=True)

        @pl.when(k == nk - 1)
        def _():
            r = acc_ref[...]
            pos = 2
            if bias is not None:
                r = r + refs[pos][...]
                pos += 1
            if res is not None:
                r = r + refs[pos][...]
            o_ref[...] = r.astype(out_dtype)

    out_row0, out_rows = out_window if out_window is not None else (0, m_dim)
    assert out_row0 % tm == 0
    aliases = None
    if into is not None:
        assert into.shape == (out_rows, n_dim) and into.dtype == out_dtype
        in_specs.append(pl.BlockSpec(memory_space=pl.ANY))
        args.append(into)
        aliases = {len(args) - 1: 0}
    out_spec = pl.BlockSpec((tm, tn), lambda i, j, k: (out_row0 // tm + i, j))
    out_shape = jax.ShapeDtypeStruct((out_rows, n_dim), out_dtype)
    if colsum_b:
        out_spec = [out_spec, pl.BlockSpec((1, tn), lambda i, j, k: (0, j))]
        out_shape = [out_shape, jax.ShapeDtypeStruct((1, n_dim), F32)]
    return _call(
        body, name=name, grid=(m_dim // tm, n_dim // tn, nk), in_specs=in_specs, out_specs=out_spec, out_shape=out_shape,
        aliases=aliases, scratch_shapes=[pltpu.VMEM((tm, tn), F32)], sem=("parallel", "parallel", "arbitrary"), args=args, comm=comm)


def _mm_norm_bwd(a, b, x, nw, res, *, dims="nn", b_rows=None, add=None, name, tm=1024, tk=1024, comm=None):
    m_dim, k_dim = a.shape
    row0, n_rows = b_rows if b_rows is not None else (0, b.shape[0])
    d_dim = x.shape[1]
    tm, tk = min(tm, m_dim), min(tk, k_dim)
    assert m_dim % tm == 0 and k_dim % tk == 0 and (n_rows if dims == "nt" else b.shape[1]) == d_dim, (name, a.shape, b.shape)
    nk = k_dim // tk
    if dims == "nt":
        assert row0 % d_dim == 0
        b_spec = pl.BlockSpec((d_dim, tk), lambda i, k: (row0 // d_dim, k))
    else:
        assert row0 % tk == 0 and n_rows == k_dim
        b_spec = pl.BlockSpec((tk, d_dim), lambda i, k: (row0 // tk + k, 0))
    row = pl.BlockSpec((tm, d_dim), lambda i, k: (i, 0))
    vec = pl.BlockSpec((1, d_dim), lambda i, k: (0, 0))
    extra = ([add] if add is not None else []) + list(res)
    dn = _DIMS[dims]

    def body(*refs):
        a_ref, b_ref, x_ref, nw_ref = refs[:4]
        extra_refs = refs[4:4 + len(extra)]
        dx_ref, dnw_ref, acc_ref = refs[-3:]
        i, k = pl.program_id(0), pl.program_id(1)

        @pl.when(k == 0)
        def _():
            acc_ref[...] = jnp.zeros_like(acc_ref)

        @pl.when((i == 0) & (k == 0))
        def _():
            dnw_ref[...] = jnp.zeros_like(dnw_ref)

        acc_ref[...] += _dot(a_ref[...], b_ref[...], dn)

        @pl.when(k == nk - 1)
        def _():
            dh = acc_ref[...]
            rest = list(extra_refs)
            if add is not None:
                dh = dh + rest.pop(0)[...]
            xhat, r = _rms(x_ref[...])
            dxhat = dh * nw_ref[...]
            dx = r * (dxhat - xhat * jnp.mean(dxhat * xhat, axis=-1, keepdims=True))
            for rr in rest:
                dx = dx + rr[...]
            dx_ref[...] = dx
            dnw_ref[...] += jnp.sum(dh * xhat, axis=0, keepdims=True)

    return _call(
        body, name=name, grid=(m_dim // tm, nk),
        in_specs=[pl.BlockSpec((tm, tk), lambda i, k: (i, k)), b_spec, row, vec] + [row] * len(extra), out_specs=[row, vec],
        out_shape=[jax.ShapeDtypeStruct((m_dim, d_dim), F32), jax.ShapeDtypeStruct((1, d_dim), F32)],
        scratch_shapes=[pltpu.VMEM((tm, d_dim), F32)], sem=("arbitrary", "arbitrary"), args=[a, b, x, nw] + extra, comm=comm)


def _rope_rotate(x, cos_t, sin_t):
    rows, width = x.shape
    half = ATT_HEAD_DIM // 2
    lane = lax.broadcasted_iota(jnp.int32, (rows, width), 1)
    first = (lane % ATT_HEAD_DIM) < half
    rot = jnp.where(first, -pltpu.roll(x, width - half, 1), pltpu.roll(x, half, 1))
    reps = width // 128
    return x * jnp.tile(cos_t, (1, reps)) + rot * jnp.tile(sin_t, (1, reps))


def _norm_mm(x, nw, w, bias, *, name, tm=1024, tn=1024, comm=None, w_rows=None, rope=None):
    t_dim, d_dim = x.shape
    transposed = w_rows is not None
    n_dim = w_rows if transposed else w.shape[1]
    tn = min(tn, n_dim)
    assert t_dim % tm == 0 and n_dim % tn == 0
    has_bias = bias is not None
    w_spec = pl.BlockSpec((tn, d_dim), lambda i, j: (j, 0)) if transposed else pl.BlockSpec((d_dim, tn), lambda i, j: (0, j))
    dn = NT if transposed else NN
    in_specs = [pl.BlockSpec((tm, d_dim), lambda i, j: (i, 0)), pl.BlockSpec((1, d_dim), lambda i, j: (0, 0)), w_spec]
    args = [x, nw, w]
    if has_bias:
        in_specs.append(pl.BlockSpec((1, tn), lambda i, j: (0, j)))
        args.append(bias)
    if rope is not None:
        in_specs += [pl.BlockSpec((tm, LANES), lambda i, j: (i, 0))] * 2
        args += list(rope)

    def body(*refs):
        x_ref, nw_ref, w_ref = refs[:3]
        o_ref, h_ref = refs[-2], refs[-1]

        @pl.when(pl.program_id(1) == 0)
        def _():
            xhat, _ = _rms(x_ref[...])
            h_ref[...] = (xhat * nw_ref[...]).astype(BF16)

        r = _dot(h_ref[...], w_ref[...], dn)
        if has_bias:
            r = r + refs[3][...]
        if rope is not None:
            r = _rope_rotate(r, refs[-4][...], refs[-3][...])
        o_ref[...] = r

    return _call(
        body, name=name, grid=(t_dim // tm, n_dim // tn), in_specs=in_specs,
        out_specs=[pl.BlockSpec((tm, tn), lambda i, j: (i, j)), pl.BlockSpec((tm, d_dim), lambda i, j: (i, 0))],
        out_shape=[jax.ShapeDtypeStruct((t_dim, n_dim), F32), jax.ShapeDtypeStruct((t_dim, d_dim), BF16)],
        sem=("parallel", "arbitrary"), args=args, comm=comm)


def _norm_bwd(x, nw, dh, res, *, name, tm=256, comm=None):
    t_dim, d_dim = x.shape
    n_res = len(res)
    row = pl.BlockSpec((tm, d_dim), lambda i: (i, 0))
    vec = pl.BlockSpec((1, d_dim), lambda i: (0, 0))

    def body(*refs):
        x_ref, nw_ref, dh_ref = refs[:3]
        dx_ref, dnw_ref = refs[-2], refs[-1]
        xhat, r = _rms(x_ref[...])
        dh = dh_ref[...]
        dxhat = dh * nw_ref[...]
        dx = r * (dxhat - xhat * jnp.mean(dxhat * xhat, axis=-1, keepdims=True))
        for rr in refs[3:3 + n_res]:
            dx = dx + rr[...]
        dx_ref[...] = dx

        @pl.when(pl.program_id(0) == 0)
        def _():
            dnw_ref[...] = jnp.zeros_like(dnw_ref)

        dnw_ref[...] += jnp.sum(dh * xhat, axis=0, keepdims=True)

    return _call(
        body, name=name, grid=(t_dim // tm,), in_specs=[row, vec, row] + [row] * n_res,
        out_specs=[row, vec],
        out_shape=[jax.ShapeDtypeStruct((t_dim, d_dim), F32), jax.ShapeDtypeStruct((1, d_dim), F32)],
        sem=("arbitrary",), args=[x, nw, dh, *res], comm=comm)


FFN_ROW_TILE = 512
FFN_SHARDS_PER_STEP = 2
FFN_STEPS = N_DEV // FFN_SHARDS_PER_STEP
FFN_STEP_COLS = FFN_SHARDS_PER_STEP * D_FF_SHARD


def _ffn_step_view(w):
    return w.reshape(FFN_STEPS, FFN_STEP_COLS, w.shape[-1])


def _ffn_specs(t_dim, d_dim):
    full = pl.BlockSpec((t_dim, d_dim), lambda j: (0, 0))
    wspec = pl.BlockSpec((None, FFN_STEP_COLS, d_dim), lambda j: (j, 0, 0))
    pre = pl.BlockSpec((None, t_dim, FFN_STEP_COLS), lambda j: (j, 0, 0))
    return full, wspec, pre


def _ffn_fwd(x, nw, wg, wu, wd, *, name, comm=None):
    t_dim, d_dim = x.shape
    n_tiles = t_dim // FFN_ROW_TILE

    def body(x_ref, nw_ref, wg_ref, wu_ref, wd_ref, o_ref, g_ref, u_ref, h_scr):
        j = pl.program_id(0)

        @pl.when(j == 0)
        def _():
            xhat, _ = _rms(x_ref[...])
            h_scr[...] = (xhat * nw_ref[...]).astype(BF16)
            o_ref[...] = jnp.zeros_like(o_ref)

        for t in range(n_tiles):
            rows = pl.ds(t * FFN_ROW_TILE, FFN_ROW_TILE)
            h = h_scr[rows, :]
            g = _dot(h, wg_ref[...], NT)
            u = _dot(h, wu_ref[...], NT)
            g_ref[rows, :] = g.astype(BF16)
            u_ref[rows, :] = u.astype(BF16)
            o_ref[rows, :] += _dot(g * _sigmoid(g) * u, wd_ref[...])

        @pl.when(j == FFN_STEPS - 1)
        def _():
            o_ref[...] = x_ref[...] + FFN_RES_WEIGHT * o_ref[...]

    full, wspec, pre = _ffn_specs(t_dim, d_dim)
    pre_shape = jax.ShapeDtypeStruct((FFN_STEPS, t_dim, FFN_STEP_COLS), BF16)
    return _call(
        body, name=name, grid=(FFN_STEPS,),
        in_specs=[full, pl.BlockSpec((1, d_dim), lambda j: (0, 0)), wspec, wspec, wspec],
        out_specs=[full, pre, pre], out_shape=[jax.ShapeDtypeStruct((t_dim, d_dim), F32), pre_shape, pre_shape],
        scratch_shapes=[pltpu.VMEM((t_dim, d_dim), BF16)],
        sem=("arbitrary",), args=[x, nw, _ffn_step_view(wg), _ffn_step_view(wu), _ffn_step_view(wd)], comm=comm)


def _ffn_bwd_prep(x, nw, dout, *, name, tm=256):
    t_dim, d_dim = x.shape
    row = pl.BlockSpec((tm, d_dim), lambda i: (i, 0))

    def body(x_ref, nw_ref, dout_ref, h_ref, dob_ref):
        xhat, _ = _rms(x_ref[...])
        h_ref[...] = (xhat * nw_ref[...]).astype(BF16)
        dob_ref[...] = (FFN_RES_WEIGHT * dout_ref[...]).astype(BF16)

    return pl.pallas_call(
        body, name=name, grid=(t_dim // tm,), in_specs=[row, pl.BlockSpec((1, d_dim), lambda i: (0, 0)), row],
        out_specs=[row, row], out_shape=[jax.ShapeDtypeStruct((t_dim, d_dim), BF16)] * 2,
        compiler_params=_params("parallel"),
    )(x, nw, dout)


def _ffn_bwd(h, dob, pre_g, pre_u, wg, wu, wd, *, name, comm=None):
    t_dim, d_dim = h.shape
    n_tiles = t_dim // FFN_ROW_TILE

    def body(h_ref, dob_ref, g_ref, u_ref, wg_ref, wu_ref, wd_ref, dh_ref, gg_ref, gu_ref, gd_ref, dwg_scr, dwu_scr, dwd_scr):
        @pl.when(pl.program_id(0) == 0)
        def _():
            dh_ref[...] = jnp.zeros_like(dh_ref)

        for t in range(n_tiles):
            rows = pl.ds(t * FFN_ROW_TILE, FFN_ROW_TILE)
            hh = h_ref[rows, :]
            do = dob_ref[rows, :]
            g = g_ref[rows, :].astype(F32)
            u = u_ref[rows, :].astype(F32)
            sg = _sigmoid(g)
            s = g * sg
            da = _dot(do, wd_ref[...], NT)
            dwd = _dot(s * u, do, TN)
            du = (da * s).astype(BF16)
            dg = (da * u * _dsilu(g, sg)).astype(BF16)
            dwg = _dot(dg, hh, TN)
            dwu = _dot(du, hh, TN)
            if t == 0:
                dwd_scr[...] = dwd
                dwg_scr[...] = dwg
                dwu_scr[...] = dwu
            else:
                dwd_scr[...] += dwd
                dwg_scr[...] += dwg
                dwu_scr[...] += dwu
            dh_ref[rows, :] += _dot(dg, wg_ref[...]) + _dot(du, wu_ref[...])
        gg_ref[...] = dwg_scr[...].astype(BF16)
        gu_ref[...] = dwu_scr[...].astype(BF16)
        gd_ref[...] = dwd_scr[...].astype(BF16)

    full, wspec, pre = _ffn_specs(t_dim, d_dim)
    gspec = pl.BlockSpec((None, FFN_STEP_COLS, d_dim), lambda j: (j, 0, 0), pipeline_mode=pl.Buffered(1))
    grad_shape = jax.ShapeDtypeStruct((FFN_STEPS, FFN_STEP_COLS, d_dim), BF16)
    dh, gg, gu, gd = _call(
        body, name=name, grid=(FFN_STEPS,),
        in_specs=[full, full, pre, pre, wspec, wspec, wspec], out_specs=[full, gspec, gspec, gspec],
        out_shape=[jax.ShapeDtypeStruct((t_dim, d_dim), F32)] + [grad_shape] * 3,
        scratch_shapes=[pltpu.VMEM((FFN_STEP_COLS, d_dim), F32)] * 3, sem=("arbitrary",), vmem_limit=FFN_BWD_VMEM_LIMIT_BYTES,
        args=[h, dob, pre_g, pre_u, _ffn_step_view(wg), _ffn_step_view(wu), _ffn_step_view(wd)], comm=comm)
    return dh, gg.reshape(wg.shape), gu.reshape(wu.shape), gd.reshape(wd.shape)


CONV_COLS = 256


def _shift_down(u, s, rows):
    return jnp.where(rows >= s, pltpu.roll(u, s, 0), 0.0)


def _shift_up(u, s, rows, t_dim):
    return jnp.where(rows < t_dim - s, pltpu.roll(u, t_dim - s, 0), 0.0)


def _conv_pre(u, w_ref, b_ref, rows):
    c = b_ref[...] + w_ref[CONV_WIDTH - 1:CONV_WIDTH, :] * u
    for k in range(CONV_WIDTH - 1):
        c = c + w_ref[k:k + 1, :] * _shift_down(u, CONV_WIDTH - 1 - k, rows)
    return c


def _conv_fwd(zx, cw, cb, *, name, comm=None):
    t_dim = zx.shape[0]
    off = D_INNER // CONV_COLS

    def body(u_ref, w_ref, b_ref, o_ref):
        rows = lax.broadcasted_iota(jnp.int32, (t_dim, CONV_COLS), 0)
        c = _conv_pre(u_ref[...], w_ref, b_ref, rows)
        o_ref[...] = c * _sigmoid(c)

    return _call(
        body, name=name, grid=(CONV_DIM // CONV_COLS,),
        in_specs=[pl.BlockSpec((t_dim, CONV_COLS), lambda j: (0, off + j)),
                  pl.BlockSpec((CONV_WIDTH, CONV_COLS), lambda j: (0, j)), pl.BlockSpec((1, CONV_COLS), lambda j: (0, j))],
        out_specs=pl.BlockSpec((t_dim, CONV_COLS), lambda j: (0, j)),
        out_shape=jax.ShapeDtypeStruct((t_dim, CONV_DIM), F32), sem=("parallel",), args=[zx, cw, cb], comm=comm)


def _conv_bwd(zx, cw, cb, dxs, db, dc, dzx, *, name, comm=None):
    t_dim = zx.shape[0]
    off = D_INNER // CONV_COLS
    n_xs = D_INNER // CONV_COLS
    n_b = GN // CONV_COLS

    def body(u_ref, w_ref, b_ref, dxs_ref, db_ref, dc_ref, dzx_in, dzx_ref, dw_ref, dbias_ref):
        j = pl.program_id(0)
        rows = lax.broadcasted_iota(jnp.int32, (t_dim, CONV_COLS), 0)
        u = u_ref[...]
        c = _conv_pre(u, w_ref, b_ref, rows)
        d = jnp.where(j < n_xs, dxs_ref[...], jnp.where(j < n_xs + n_b, db_ref[...], dc_ref[...]))
        dcv = d * _dsilu(c, _sigmoid(c))
        dpre = w_ref[CONV_WIDTH - 1:CONV_WIDTH, :] * dcv
        dw_ref[CONV_WIDTH - 1:CONV_WIDTH, :] = jnp.sum(dcv * u, axis=0, keepdims=True)
        for k in range(CONV_WIDTH - 1):
            s = CONV_WIDTH - 1 - k
            dpre = dpre + w_ref[k:k + 1, :] * _shift_up(dcv, s, rows, t_dim)
            dw_ref[k:k + 1, :] = jnp.sum(dcv * _shift_down(u, s, rows), axis=0, keepdims=True)
        dzx_ref[...] = dpre
        dbias_ref[...] = jnp.sum(dcv, axis=0, keepdims=True)

    blk = lambda n: pl.BlockSpec((t_dim, CONV_COLS), n)
    return _call(
        body, name=name, grid=(CONV_DIM // CONV_COLS,),
        in_specs=[blk(lambda j: (0, off + j)), pl.BlockSpec((CONV_WIDTH, CONV_COLS), lambda j: (0, j)),
                  pl.BlockSpec((1, CONV_COLS), lambda j: (0, j)),
                  blk(lambda j: (0, jnp.minimum(j, n_xs - 1))),
                  blk(lambda j: (0, jnp.clip(j - n_xs, 0, n_b - 1))),
                  blk(lambda j: (0, jnp.clip(j - n_xs - n_b, 0, n_b - 1))),
                  pl.BlockSpec(memory_space=pl.ANY)],
        out_specs=[blk(lambda j: (0, off + j)), pl.BlockSpec((CONV_WIDTH, CONV_COLS), lambda j: (0, j)),
                   pl.BlockSpec((1, CONV_COLS), lambda j: (0, j))],
        out_shape=[jax.ShapeDtypeStruct(dzx.shape, F32), jax.ShapeDtypeStruct((CONV_WIDTH, CONV_DIM), F32),
                   jax.ShapeDtypeStruct((1, CONV_DIM), F32)],
        aliases={6: 0}, sem=("parallel",), args=[zx, cw, cb, dxs, db, dc, dzx], comm=comm)


def _softplus_parts(x):
    e = jnp.exp(-jnp.abs(x))
    u = 1.0 + e
    log1p_e = jnp.where(u == 1.0, e, jnp.log(u) * e / jnp.where(u == 1.0, 1.0, u - 1.0))
    return jnp.maximum(x, 0.0) + log1p_e


def _dt_prep(dtr, dt_bias, a_log, *, name):
    def body(dtr_ref, bias_ref, alog_ref, dt_ref, a_ref):
        dt = _softplus_parts(dtr_ref[...] + bias_ref[...])
        dt_ref[...] = dt
        a_ref[...] = dt * (-jnp.exp(alog_ref[...]))

    return pl.pallas_call(body, name=name, out_shape=[jax.ShapeDtypeStruct(dtr.shape, F32)] * 2,
                          compiler_params=_params())(dtr, dt_bias, a_log)


def _dt_bwd(dtr, dt_bias, a_log, dt, ddt, da, *, name):
    def body(dtr_ref, bias_ref, alog_ref, dt_ref, ddt_ref, da_ref, ddtr_ref, dbias_ref, dalog_ref):
        a_neg = -jnp.exp(alog_ref[...])
        da_v = da_ref[...]
        ddt_tot = ddt_ref[...] + da_v * a_neg
        ddtr = ddt_tot * _sigmoid(dtr_ref[...] + bias_ref[...])
        ddtr_ref[...] = ddtr
        dbias_ref[...] = jnp.sum(ddtr, axis=0, keepdims=True)
        dalog_ref[...] = jnp.sum(da_v * dt_ref[...], axis=0, keepdims=True) * a_neg

    return pl.pallas_call(
        body, name=name,
        out_shape=[jax.ShapeDtypeStruct(dtr.shape, F32), jax.ShapeDtypeStruct((1, SSM_HEADS), F32),
                   jax.ShapeDtypeStruct((1, SSM_HEADS), F32)],
        compiler_params=_params())(dtr, dt_bias, a_log, dt, ddt, da)


GROUP_COLS = HEADS_PER_GROUP * SSM_HEAD_DIM
LANES = 128
HEADS_PER_LANE_BLOCK = LANES // SSM_HEAD_DIM


def _split3(x):
    hi = x.astype(BF16)
    r1 = x - hi.astype(F32)
    mid = r1.astype(BF16)
    lo = (r1 - mid.astype(F32)).astype(BF16)
    return hi, mid, lo


def _dot_select(a, b, dims=NN, data=0):
    out = None
    for part in _split3(a if data == 0 else b):
        lhs, rhs = (part, b.astype(BF16)) if data == 0 else (a.astype(BF16), part)
        t = lax.dot_general(lhs, rhs, dims, preferred_element_type=F32)
        out = t if out is None else out + t
    return out


def _group_sums(vals, expand):
    out = _dot_select(jnp.concatenate(vals, axis=0), expand, NT)
    return [out[i * CHUNK:(i + 1) * CHUNK] for i in range(len(vals))]


def _ssd_chunk_common(a_ref, dt_ref, b_ref, c_ref):
    row = lax.broadcasted_iota(jnp.int32, (CHUNK, CHUNK), 0)
    col = lax.broadcasted_iota(jnp.int32, (CHUNK, CHUNK), 1)
    causal = col <= row
    lower = causal.astype(F32)
    upper = (col >= row).astype(F32)
    head = lax.broadcasted_iota(jnp.int32, (HEADS_PER_GROUP, GROUP_COLS), 0)
    lane = lax.broadcasted_iota(jnp.int32, (HEADS_PER_GROUP, GROUP_COLS), 1)
    expand = ((lane >= head * SSM_HEAD_DIM) & (lane < (head + 1) * SSM_HEAD_DIM)).astype(F32)
    a = a_ref[...]
    cs = _dot_select(lower, a, data=1)
    cs_row = _dot_select(a, upper, TN)
    cs_x = _dot_select(cs, expand)
    dt_x = _dot_select(dt_ref[...], expand)
    e_out_x = jnp.exp(cs_x)
    e_st_x = jnp.exp(cs_x[CHUNK - 1:CHUNK, :] - cs_x)
    bc = b_ref[...]
    cc = c_ref[...]
    cb = _dot(cc, bc, NT)
    return causal, upper, expand.astype(BF16), cs, cs_row, dt_x, e_out_x, e_st_x, bc, cc, cb


def _head_decay(causal, cs, cs_row, h):
    return jnp.exp(jnp.where(causal, cs[:, h:h + 1] - cs_row[h:h + 1, :], NEG_BIG))


def _lane_block_head_masks():
    lane = lax.broadcasted_iota(jnp.int32, (CHUNK, LANES), 1)
    return [(lane >= i * SSM_HEAD_DIM) & (lane < (i + 1) * SSM_HEAD_DIM) for i in range(HEADS_PER_LANE_BLOCK)]


def _decay_state(dst_ref, old, new, cs):
    for h in range(HEADS_PER_GROUP):
        rows = slice(h * SSM_HEAD_DIM, (h + 1) * SSM_HEAD_DIM)
        dst_ref[rows, :] = jnp.exp(cs[CHUNK - 1:CHUNK, h:h + 1]) * old[rows, :] + new[rows, :]


def _ssd_fwd(xbc, dtg, ag, dgx, *, name, comm=None):
    t_dim = xbc.shape[0]

    def body(xs_ref, b_ref, c_ref, dt_ref, a_ref, d_ref, y_ref, st_ref, s_scr):
        @pl.when(pl.program_id(1) == 0)
        def _():
            s_scr[...] = jnp.zeros_like(s_scr)

        causal, _, _, cs, cs_row, dt_x, e_out_x, e_st_x, bc, cc, cb = _ssd_chunk_common(a_ref, dt_ref, b_ref, c_ref)
        masks = _lane_block_head_masks()
        xs = xs_ref[...]
        xdt_x = xs * dt_x
        prev = s_scr[...]
        st_ref[...] = prev
        y_off = e_out_x * _dot(cc, prev, NT) + xs * d_ref[...]
        for blk in range(GROUP_COLS // LANES):
            lanes = slice(blk * LANES, (blk + 1) * LANES)
            x_b = xdt_x[:, lanes].astype(BF16)
            acc = y_off[:, lanes]
            for i in range(HEADS_PER_LANE_BLOCK):
                m = cb * _head_decay(causal, cs, cs_row, blk * HEADS_PER_LANE_BLOCK + i)
                acc = acc + _dot(m, jnp.where(masks[i], x_b, jnp.zeros_like(x_b)))
            y_ref[:, lanes] = acc
        _decay_state(s_scr, prev, _dot(xdt_x * e_st_x, bc, TN), cs)

    xs = pl.BlockSpec((CHUNK, GROUP_COLS), lambda g, c: (c, g))
    bsp = pl.BlockSpec((CHUNK, SSM_STATE), lambda g, c: (c, D_INNER // SSM_STATE + g))
    csp = pl.BlockSpec((CHUNK, SSM_STATE), lambda g, c: (c, (D_INNER + GN) // SSM_STATE + g))
    per_head = pl.BlockSpec((None, CHUNK, HEADS_PER_GROUP), lambda g, c: (g, c, 0))
    dsk = pl.BlockSpec((None, 1, GROUP_COLS), lambda g, c: (g, 0, 0))
    return _call(
        body, name=name, grid=(SSM_GROUPS, N_CHUNKS),
        in_specs=[xs, bsp, csp, per_head, per_head, dsk],
        out_specs=[xs, pl.BlockSpec((None, GROUP_COLS, SSM_STATE), lambda g, c: (c, g, 0))],
        out_shape=[jax.ShapeDtypeStruct((t_dim, D_INNER), F32),
                   jax.ShapeDtypeStruct((N_CHUNKS, D_INNER, SSM_STATE), F32)],
        scratch_shapes=[pltpu.VMEM((GROUP_COLS, SSM_STATE), F32)],
        sem=("parallel", "arbitrary"), args=[xbc, xbc, xbc, dtg, ag, dgx], comm=comm)


def _ssd_bwd(xbc, dtg, ag, dgx, states, dy, *, name, comm=None):
    t_dim = xbc.shape[0]
    last = N_CHUNKS - 1

    def body(xs_ref, b_ref, c_ref, dt_ref, a_ref, d_ref, st_ref, dy_ref,
             dxs_ref, db_ref, dc_ref, ddt_ref, da_ref, dd_ref, ds_scr):
        @pl.when(pl.program_id(1) == 0)
        def _():
            ds_scr[...] = jnp.zeros_like(ds_scr)
            dd_ref[...] = jnp.zeros_like(dd_ref)

        causal, upper, expand, cs, cs_row, dt_x, e_out_x, e_st_x, bc, cc, cb = _ssd_chunk_common(a_ref, dt_ref, b_ref, c_ref)
        masks = _lane_block_head_masks()
        xs = xs_ref[...]
        dy_x = dy_ref[...]
        xdt_x = xs * dt_x
        prev = st_ref[...]
        d_s = ds_scr[...]
        g1_x = _dot(bc, d_s, NT)
        cp_x = _dot(cc, prev, NT)
        d_cb = jnp.zeros((CHUNK, CHUNK), F32)
        lane8 = lax.broadcasted_iota(jnp.int32, (CHUNK, HEADS_PER_GROUP), 1)
        sub8 = lax.broadcasted_iota(jnp.int32, (HEADS_PER_GROUP, CHUNK), 0)
        row_w = jnp.zeros((CHUNK, HEADS_PER_GROUP), F32)
        col_w = jnp.zeros((HEADS_PER_GROUP, CHUNK), F32)
        dxdt_blocks = []
        for blk in range(GROUP_COLS // LANES):
            lanes = slice(blk * LANES, (blk + 1) * LANES)
            dy_b = dy_x[:, lanes].astype(BF16)
            x_b = xdt_x[:, lanes].astype(BF16)
            acc_dx = jnp.zeros((CHUNK, LANES), F32)
            for i in range(HEADS_PER_LANE_BLOCK):
                h = blk * HEADS_PER_LANE_BLOCK + i
                decay = _head_decay(causal, cs, cs_row, h)
                m = cb * decay
                dy_h = jnp.where(masks[i], dy_b, jnp.zeros_like(dy_b))
                acc_dx = acc_dx + _dot(m, dy_h, TN)
                d_m = _dot(dy_h, x_b, NT)
                d_cb = d_cb + d_m * decay
                w = d_m * m
                row_w = jnp.where(lane8 == h, jnp.sum(w, axis=1, keepdims=True), row_w)
                col_w = jnp.where(sub8 == h, jnp.sum(w, axis=0, keepdims=True), col_w)
            dxdt_blocks.append(acc_dx)
        dxdt_x = jnp.concatenate(dxdt_blocks, axis=1) + e_st_x * g1_x
        dxs_ref[...] = dxdt_x * dt_x + dy_x * d_ref[...]
        dye = dy_x * e_out_x
        xde = xdt_x * e_st_x
        ddt, y_off, tl, dskip = _group_sums([dxdt_x * xs, dye * cp_x, xde * g1_x, dy_x * xs], expand)
        ddt_ref[...] = ddt
        dd_ref[...] += jnp.sum(dskip, axis=0, keepdims=True)
        sp = None
        for part in _split3(d_s * prev):
            t = lax.dot_general(expand, part, NN, preferred_element_type=F32)
            sp = t if sp is None else sp + t
        last_col = jnp.exp(cs_row[:, CHUNK - 1:CHUNK]) * jnp.sum(sp, axis=1, keepdims=True)
        eye = lax.broadcasted_iota(jnp.int32, (HEADS_PER_GROUP, HEADS_PER_GROUP), 0) == lax.broadcasted_iota(
            jnp.int32, (HEADS_PER_GROUP, HEADS_PER_GROUP), 1)
        last_row = jnp.sum(jnp.where(eye, last_col, 0.0), axis=0, keepdims=True) + jnp.sum(tl, axis=0, keepdims=True)
        is_last = lax.broadcasted_iota(jnp.int32, (CHUNK, 1), 0) == CHUNK - 1
        d_cs = row_w + y_off - tl + jnp.where(is_last, last_row, 0.0)
        da_ref[...] = _dot_select(upper, d_cs, data=1) - _dot_select(upper, col_w, NT, data=1)
        dc_ref[...] = _dot(d_cb, bc) + _dot(dye, prev)
        db_ref[...] = _dot(d_cb, cc, TN) + _dot(xde, d_s)
        _decay_state(ds_scr, d_s, _dot(dye, cc, TN), cs)

    rev = lambda c: last - c
    xs = pl.BlockSpec((CHUNK, GROUP_COLS), lambda g, c: (rev(c), g))
    bsp = pl.BlockSpec((CHUNK, SSM_STATE), lambda g, c: (rev(c), D_INNER // SSM_STATE + g))
    csp = pl.BlockSpec((CHUNK, SSM_STATE), lambda g, c: (rev(c), (D_INNER + GN) // SSM_STATE + g))
    per_head = pl.BlockSpec((None, CHUNK, HEADS_PER_GROUP), lambda g, c: (g, rev(c), 0))
    dsk = pl.BlockSpec((None, 1, GROUP_COLS), lambda g, c: (g, 0, 0))
    dsum = pl.BlockSpec((None, 1, HEADS_PER_GROUP), lambda g, c: (g, 0, 0))
    st = pl.BlockSpec((None, GROUP_COLS, SSM_STATE), lambda g, c: (rev(c), g, 0))
    grp = pl.BlockSpec((CHUNK, SSM_STATE), lambda g, c: (rev(c), g))
    return _call(
        body, name=name, grid=(SSM_GROUPS, N_CHUNKS),
        in_specs=[xs, bsp, csp, per_head, per_head, dsk, st, xs],
        out_specs=[xs, grp, grp, per_head, per_head, dsum],
        out_shape=[jax.ShapeDtypeStruct((t_dim, D_INNER), F32), jax.ShapeDtypeStruct((t_dim, GN), F32),
                   jax.ShapeDtypeStruct((t_dim, GN), F32),
                   jax.ShapeDtypeStruct((SSM_GROUPS, t_dim, HEADS_PER_GROUP), F32),
                   jax.ShapeDtypeStruct((SSM_GROUPS, t_dim, HEADS_PER_GROUP), F32),
                   jax.ShapeDtypeStruct((SSM_GROUPS, 1, HEADS_PER_GROUP), F32)],
        scratch_shapes=[pltpu.VMEM((GROUP_COLS, SSM_STATE), F32)],
        sem=("parallel", "arbitrary"), args=[xbc, xbc, xbc, dtg, ag, dgx, states, dy], comm=comm)


NORM_GROUP = D_INNER // SSM_GROUPS


def _gate_norm_fwd(y, zx, nw, *, name, tm=256):
    t_dim = y.shape[0]
    row = pl.BlockSpec((tm, D_INNER), lambda i: (i, 0))

    def body(y_ref, z_ref, nw_ref, o_ref):
        z = z_ref[...]
        yz = y_ref[...] * (z * _sigmoid(z))
        for g in range(SSM_GROUPS):
            cols = slice(g * NORM_GROUP, (g + 1) * NORM_GROUP)
            yhat, _ = _rms(yz[:, cols])
            o_ref[:, cols] = (yhat * nw_ref[:, cols]).astype(BF16)

    return pl.pallas_call(
        body, name=name, grid=(t_dim // tm,), in_specs=[row, row, pl.BlockSpec((1, D_INNER), lambda i: (0, 0))],
        out_specs=row, out_shape=jax.ShapeDtypeStruct((t_dim, D_INNER), BF16),
        compiler_params=_params("parallel"),
    )(y, zx, nw)


def _gate_norm_bwd(y, zx, nw, dyn, *, name, tm=256):
    t_dim = y.shape[0]
    row = pl.BlockSpec((tm, D_INNER), lambda i: (i, 0))
    vec = pl.BlockSpec((1, D_INNER), lambda i: (0, 0))

    def body(y_ref, z_ref, nw_ref, dyn_ref, dy_ref, dz_ref, dnw_ref):
        @pl.when(pl.program_id(0) == 0)
        def _():
            dnw_ref[...] = jnp.zeros_like(dnw_ref)

        z = z_ref[...]
        yv = y_ref[...]
        sg = _sigmoid(z)
        silu_z = z * sg
        yz = yv * silu_z
        dyn_v = dyn_ref[...]
        for g in range(SSM_GROUPS):
            cols = slice(g * NORM_GROUP, (g + 1) * NORM_GROUP)
            yhat, r = _rms(yz[:, cols])
            dn = dyn_v[:, cols]
            dnw_ref[:, cols] += jnp.sum(dn * yhat, axis=0, keepdims=True)
            dyhat = dn * nw_ref[:, cols]
            dyz = r * (dyhat - yhat * jnp.mean(dyhat * yhat, axis=-1, keepdims=True))
            dy_ref[:, cols] = dyz * silu_z[:, cols]
            dz_ref[:, cols] = dyz * yv[:, cols] * _dsilu(z[:, cols], sg[:, cols])

    return pl.pallas_call(
        body, name=name, grid=(t_dim // tm,), in_specs=[row, row, vec, row],
        out_specs=[row, row, vec],
        out_shape=[jax.ShapeDtypeStruct((t_dim, D_INNER), F32), jax.ShapeDtypeStruct((t_dim, ZX_DIM), F32),
                   jax.ShapeDtypeStruct((1, D_INNER), F32)],
        compiler_params=_params("arbitrary"),
    )(y, zx, nw, dyn)


HEADS_PER_LANE_TILE = LANES // ATT_HEAD_DIM
STACKED_ROWS = Q_PER_KV * WINDOW


def _att_half_masks():
    lane = lax.broadcasted_iota(jnp.int32, (WINDOW, LANES), 1)
    return [(lane >= i * ATT_HEAD_DIM) & (lane < (i + 1) * ATT_HEAD_DIM) for i in range(HEADS_PER_LANE_TILE)]


def _att_stack_heads(ref, kvh, masks):
    parts = []
    for g in range(Q_PER_KV):
        h = kvh * Q_PER_KV + g
        blk = ref[:, (h // HEADS_PER_LANE_TILE) * LANES:(h // HEADS_PER_LANE_TILE + 1) * LANES]
        parts.append(jnp.where(masks[h % HEADS_PER_LANE_TILE], blk, jnp.zeros_like(blk)))
    return jnp.concatenate(parts, axis=0)


def _att_kv_tile(ref, kvh, masks):
    blk = ref[:, (kvh // HEADS_PER_LANE_TILE) * LANES:(kvh // HEADS_PER_LANE_TILE + 1) * LANES]
    return jnp.where(masks[kvh % HEADS_PER_LANE_TILE], blk, pltpu.roll(blk, ATT_HEAD_DIM, 1)).astype(BF16)


def _att_stacked_masks(n):
    row = lax.bitwise_and(lax.broadcasted_iota(jnp.int32, (STACKED_ROWS, WINDOW), 0), WINDOW - 1)
    col = lax.broadcasted_iota(jnp.int32, (STACKED_ROWS, WINDOW), 1)
    return col <= row, (col > row) & (n > 0)


def _att_stack_columns(ref, kvh, rows):
    cols = [ref[:, kvh * Q_PER_KV + g:kvh * Q_PER_KV + g + 1] for g in range(Q_PER_KV)]
    return jnp.concatenate([jnp.broadcast_to(c, (rows, 1)) for c in cols], axis=0)


def _att_scores(q4, k_tile, mask):
    return jnp.where(mask, _dot(q4, k_tile, NT) * ATT_SCALE, NEG_BIG)


def _att_unstack(x4, kvh, masks, tiles):
    for g in range(Q_PER_KV):
        h = kvh * Q_PER_KV + g
        piece = x4[g * WINDOW:(g + 1) * WINDOW]
        t = h // HEADS_PER_LANE_TILE
        tiles[t] = piece if h % HEADS_PER_LANE_TILE == 0 else jnp.where(masks[1], piece, tiles[t])


def _attn_fwd(q, k, v, sinks, *, name, comm=None):
    t_dim = q.shape[0]

    def body(q_ref, kc_ref, kp_ref, vc_ref, vp_ref, s_ref, o_ref, l_ref):
        n = pl.program_id(0)
        masks = _att_half_masks()
        mask_c, mask_p = _att_stacked_masks(n)
        out_tiles = [None] * (D_MODEL // LANES)
        for kvh in range(N_KV_HEADS):
            q4 = _att_stack_heads(q_ref, kvh, masks).astype(BF16)
            kc, kp = _att_kv_tile(kc_ref, kvh, masks), _att_kv_tile(kp_ref, kvh, masks)
            vc, vp = _att_kv_tile(vc_ref, kvh, masks), _att_kv_tile(vp_ref, kvh, masks)
            sc = _att_scores(q4, kc, mask_c)
            sp = _att_scores(q4, kp, mask_p)
            sink = _att_stack_columns(s_ref, kvh, WINDOW)
            m = jnp.maximum(jnp.maximum(jnp.max(sc, axis=1, keepdims=True), jnp.max(sp, axis=1, keepdims=True)), sink)
            pc = jnp.exp(sc - m)
            pp = jnp.exp(sp - m)
            den = jnp.sum(pc, axis=1, keepdims=True) + jnp.sum(pp, axis=1, keepdims=True) + jnp.exp(sink - m)
            _att_unstack((_dot(pc, vc) + _dot(pp, vp)) / den, kvh, masks, out_tiles)
            lse4 = m + jnp.log(den)
            for g in range(Q_PER_KV):
                h = kvh * Q_PER_KV + g
                l_ref[:, h:h + 1] = lse4[g * WINDOW:(g + 1) * WINDOW]
        for t, tile in enumerate(out_tiles):
            o_ref[:, t * LANES:(t + 1) * LANES] = tile

    cur = lambda w: pl.BlockSpec((WINDOW, w), lambda n: (n, 0))
    prv = lambda w: pl.BlockSpec((WINDOW, w), lambda n: (jnp.maximum(n - 1, 0), 0))
    return _call(
        body, name=name, grid=(t_dim // WINDOW,),
        in_specs=[cur(D_MODEL), cur(KV_DIM), prv(KV_DIM), cur(KV_DIM), prv(KV_DIM), pl.BlockSpec((1, N_Q_HEADS), lambda n: (0, 0))],
        out_specs=[cur(D_MODEL), cur(N_Q_HEADS)],
        out_shape=[jax.ShapeDtypeStruct((t_dim, D_MODEL), F32), jax.ShapeDtypeStruct((t_dim, N_Q_HEADS), F32)],
        sem=("parallel",), args=[q, k, k, v, v, sinks], comm=comm)


def _attn_bwd(q, k, v, sinks, o, lse, do, cos2, sin2, *, name, comm=None):
    t_dim = q.shape[0]

    def body(q_ref, kc_ref, kp_ref, vc_ref, vp_ref, s_ref, o_ref, l_ref, do_ref, cos_ref, sin_ref, cos_all_ref, sin_all_ref,
             dq_ref, dk_ref, dv_ref, dsink_ref):
        n = pl.program_id(0)

        @pl.when(n == 0)
        def _():
            dk_ref[...] = jnp.zeros_like(dk_ref)
            dv_ref[...] = jnp.zeros_like(dv_ref)
            dsink_ref[...] = jnp.zeros_like(dsink_ref)

        masks = _att_half_masks()
        mask_c, mask_p = _att_stacked_masks(n)
        lane_row = lax.broadcasted_iota(jnp.int32, (1, N_Q_HEADS), 1)
        rows_c = pl.ds(pl.multiple_of(n * WINDOW, WINDOW), WINDOW)
        rows_p = pl.ds(pl.multiple_of(jnp.maximum(n - 1, 0) * WINDOW, WINDOW), WINDOW)
        dsink = jnp.zeros((1, N_Q_HEADS), F32)
        dq_tiles = [None] * (D_MODEL // LANES)
        kv_tiles = KV_DIM // LANES
        dkc_tiles, dkp_tiles, dvc_tiles, dvp_tiles = ([None] * kv_tiles for _ in range(4))

        def place(tiles, kvh, x):
            folded = x + pltpu.roll(x, ATT_HEAD_DIM, 1)
            t = kvh // HEADS_PER_LANE_TILE
            tiles[t] = folded if kvh % HEADS_PER_LANE_TILE == 0 else jnp.where(masks[1], folded, tiles[t])

        for kvh in range(N_KV_HEADS):
            q4 = _att_stack_heads(q_ref, kvh, masks).astype(BF16)
            do4 = _att_stack_heads(do_ref, kvh, masks)
            o4 = _att_stack_heads(o_ref, kvh, masks)
            kc, kp = _att_kv_tile(kc_ref, kvh, masks), _att_kv_tile(kp_ref, kvh, masks)
            vc, vp = _att_kv_tile(vc_ref, kvh, masks), _att_kv_tile(vp_ref, kvh, masks)
            l4 = _att_stack_columns(l_ref, kvh, WINDOW)
            pc = jnp.exp(_att_scores(q4, kc, mask_c) - l4)
            pp = jnp.exp(_att_scores(q4, kp, mask_p) - l4)
            delta = jnp.sum(do4 * o4, axis=1, keepdims=True)
            do4b = do4.astype(BF16)
            dsc = pc * (_dot(do4b, vc, NT) - delta)
            dsp = pp * (_dot(do4b, vp, NT) - delta)
            _att_unstack((_dot(dsc, kc) + _dot(dsp, kp)) * ATT_SCALE, kvh, masks, dq_tiles)
            place(dkc_tiles, kvh, _dot(dsc, q4, TN) * ATT_SCALE)
            place(dkp_tiles, kvh, _dot(dsp, q4, TN) * ATT_SCALE)
            place(dvc_tiles, kvh, _dot(pc, do4b, TN))
            place(dvp_tiles, kvh, _dot(pp, do4b, TN))
            p_sink = jnp.exp(_att_stack_columns(s_ref, kvh, WINDOW) - l4) * delta
            for g in range(Q_PER_KV):
                h = kvh * Q_PER_KV + g
                dsink = jnp.where(lane_row == h, -jnp.sum(p_sink[g * WINDOW:(g + 1) * WINDOW], axis=0, keepdims=True), dsink)
        for t, tile in enumerate(dq_tiles):
            dq_ref[:, t * LANES:(t + 1) * LANES] = _rope_rotate(tile, cos_ref[...], -sin_ref[...])
        for t in range(kv_tiles):
            lanes = slice(t * LANES, (t + 1) * LANES)
            dk_ref[rows_c, lanes] += dkc_tiles[t]
            dk_ref[rows_p, lanes] += dkp_tiles[t]
            dv_ref[rows_c, lanes] += dvc_tiles[t]
            dv_ref[rows_p, lanes] += dvp_tiles[t]
        dsink_ref[...] += dsink

        @pl.when(n == t_dim // WINDOW - 1)
        def _():
            dk_ref[...] = _rope_rotate(dk_ref[...], cos_all_ref[...], -sin_all_ref[...])

    cur = lambda w: pl.BlockSpec((WINDOW, w), lambda n: (n, 0))
    prv = lambda w: pl.BlockSpec((WINDOW, w), lambda n: (jnp.maximum(n - 1, 0), 0))
    whole = lambda w: pl.BlockSpec((t_dim, w), lambda n: (0, 0))
    svec = pl.BlockSpec((1, N_Q_HEADS), lambda n: (0, 0))
    return _call(
        body, name=name, grid=(t_dim // WINDOW,),
        in_specs=[cur(D_MODEL), cur(KV_DIM), prv(KV_DIM), cur(KV_DIM), prv(KV_DIM), svec, cur(D_MODEL), cur(N_Q_HEADS), cur(D_MODEL),
                  cur(LANES), cur(LANES), whole(LANES), whole(LANES)],
        out_specs=[cur(D_MODEL), whole(KV_DIM), whole(KV_DIM), svec],
        out_shape=[jax.ShapeDtypeStruct((t_dim, D_MODEL), F32), jax.ShapeDtypeStruct((t_dim, KV_DIM), F32),
                   jax.ShapeDtypeStruct((t_dim, KV_DIM), F32), jax.ShapeDtypeStruct((1, N_Q_HEADS), F32)],
        sem=("arbitrary",), args=[q, k, k, v, v, sinks, o, lse, do, cos2, sin2, cos2, sin2], comm=comm)


def _loss_head(x, nw, target, *, name, tm=256):
    t_dim, d_dim = x.shape
    row = pl.BlockSpec((tm, d_dim), lambda i: (i, 0))
    vec = pl.BlockSpec((1, d_dim), lambda i: (0, 0))

    def body(x_ref, nw_ref, tgt_ref, loss_ref, dx_ref, dnw_ref):
        @pl.when(pl.program_id(0) == 0)
        def _():
            loss_ref[...] = jnp.zeros_like(loss_ref)
            dnw_ref[...] = jnp.zeros_like(dnw_ref)

        xhat, r = _rms(x_ref[...])
        err = xhat * nw_ref[...] - tgt_ref[...]
        loss_ref[...] += 0.5 * _sum_all(jnp.mean(err * err, axis=-1, keepdims=True))
        dy = err * (1.0 / d_dim)
        dnw_ref[...] += jnp.sum(dy * xhat, axis=0, keepdims=True)
        dxhat = dy * nw_ref[...]
        dx_ref[...] = r * (dxhat - xhat * jnp.mean(dxhat * xhat, axis=-1, keepdims=True))

    return pl.pallas_call(
        body, name=name, grid=(t_dim // tm,), in_specs=[row, vec, row],
        out_specs=[pl.BlockSpec((1, 1), lambda i: (0, 0)), row, vec],
        out_shape=[jax.ShapeDtypeStruct((1, 1), F32), jax.ShapeDtypeStruct((t_dim, d_dim), F32),
                   jax.ShapeDtypeStruct((1, d_dim), F32)],
        compiler_params=_params("arbitrary"),
    )(x, nw, target)


def _rope_tables():
    pos = jnp.arange(SEQ, dtype=F32)
    inv = 1.0 / (ROPE_THETA ** (jnp.arange(0, ATT_HEAD_DIM, 2, dtype=F32) / ATT_HEAD_DIM))
    ang = pos[:, None] * inv[None, :]
    cos, sin = jnp.cos(ang), jnp.sin(ang)
    return jnp.tile(cos, (1, 4)), jnp.tile(sin, (1, 4))


def _to_groups(t):
    return t.reshape(t.shape[0], SSM_GROUPS, HEADS_PER_GROUP).transpose(1, 0, 2)


def _from_groups(t):
    return t.transpose(1, 0, 2).reshape(t.shape[1], SSM_HEADS)


def _forward_backward(x0, target, net):
    w = net.w
    nw = [[w("norm_w")[l, i][None, :] for i in range(3)] for l in range(2)]
    cos2, sin2 = _rope_tables()
    ffn_norm = [nw[0][0], nw[0][2], nw[1][0], nw[1][2]]

    ffn_pre = {}

    def ffn_f(x, blk):
        name = f"ffn_fwd{blk}"
        out, *ffn_pre[blk] = _ffn_fwd(x, ffn_norm[blk], w(f"gate{blk}"), w(f"up{blk}"), w(f"down{blk}"), name=name,
                                      comm=net.carry(name))
        return out

    x1 = ffn_f(x0, 0)
    zx, h1 = _norm_mm(x1, nw[0][1], w("w_in_t"), None, w_rows=ZX_DIM, name="ssm_in_proj", comm=net.carry("ssm_in_proj"))
    dtr = _mm(h1, w("w_in_t"), dims="nt", b_rows=(ZX_DIM, SSM_HEADS), name="ssm_dt_proj")
    xbc = _conv_fwd(zx, w("conv_w"), w("conv_b"), name="ssm_conv_fwd", comm=net.carry("ssm_conv_fwd"))
    dt, a_dt = _dt_prep(dtr, w("dt_bias"), w("a_log"), name="ssm_dt_prep")
    dtg, ag = _to_groups(dt), _to_groups(a_dt)
    dg = jnp.repeat(w("d_skip").reshape(SSM_GROUPS, 1, HEADS_PER_GROUP), SSM_HEAD_DIM, axis=2)
    y_ssd, states = _ssd_fwd(xbc, dtg, ag, dg, name="ssd_fwd", comm=net.carry("ssd_fwd"))
    yn = _gate_norm_fwd(y_ssd, zx, w("ssm_norm_w"), name="ssm_gate_norm_fwd")
    x2 = _mm(yn, w("wout"), res=x1, name="ssm_out_proj", comm=net.carry("ssm_out_proj"))
    x3 = ffn_f(x2, 1)
    k_rot, hk = _norm_mm(x3, w("kv_norm_w"), w("wk"), w("b_k"), rope=(cos2, sin2), name="k_proj")
    v = _mm(hk, w("wv"), bias=w("b_v"), name="v_proj")
    x4 = ffn_f(x3, 2)
    q_rot, h4 = _norm_mm(x4, nw[1][1], w("wq"), w("b_q"), rope=(cos2, sin2), name="q_proj")
    att, lse = _attn_fwd(q_rot, k_rot, v, w("sinks"), name="attn_fwd", comm=net.carry("attn_fwd"))
    x5 = _mm(att, w("wo"), bias=w("b_o"), res=x4, name="attn_out_proj")
    x6 = ffn_f(x5, 3)
    loss, dx6, d_final = _loss_head(x6, w("final_norm_w"), target, name="loss_head")

    d_norm = [[None] * 3 for _ in range(2)]

    def ffn_b(x, dout, blk):
        h, dob = _ffn_bwd_prep(x, ffn_norm[blk], dout, name=f"ffn_bwd_prep{blk}")
        name = f"ffn_bwd{blk}"
        dh, gg, gu, gd = _ffn_bwd(h, dob, *ffn_pre[blk], w(f"gate{blk}"), w(f"up{blk}"), w(f"down{blk}"), name=name,
                                  comm=net.carry(name))
        net.give(f"gate{blk}", gg)
        net.give(f"up{blk}", gu)
        net.give(f"down{blk}", gd)
        return _norm_bwd(x, ffn_norm[blk], dh, [dout], name=f"ffn_norm_bwd{blk}", comm=net.carry(f"ffn_norm_bwd{blk}"))

    by_rows = lambda g: g.reshape(N_DEV, g.shape[0] // N_DEV, g.shape[1])
    dx5, d_norm[1][2] = ffn_b(x5, dx6, 3)
    d_att = _mm(dx5, w("wo"), dims="nt", name="attn_out_proj_dx", comm=net.carry("attn_out_proj_dx"))
    g_o, d_bo = _mm(att, dx5, dims="tn", out_dtype=BF16, colsum_b=True, name="attn_out_proj_dw")
    net.give("w_o", by_rows(g_o))
    dq, dk, dv, d_sinks = _attn_bwd(q_rot, k_rot, v, w("sinks"), att, lse, d_att, cos2, sin2, name="attn_bwd",
                                    comm=net.carry("attn_bwd"))
    dx4, d_norm[1][1] = _mm_norm_bwd(dq, w("wq"), x4, nw[1][1], [dx5], dims="nt", name="q_proj_dx")
    g_q, d_bq = _mm(h4, dq, dims="tn", out_dtype=BF16, colsum_b=True, name="q_proj_dw")
    net.give("w_q", by_rows(g_q))
    dx3a, d_norm[1][0] = ffn_b(x3, dx4, 2)
    dhk = _mm(dk, w("wk"), dims="nt", name="k_proj_dx", comm=net.carry("k_proj_dx"))
    dx3, d_kvn = _mm_norm_bwd(dv, w("wv"), x3, w("kv_norm_w"), [dx3a], dims="nt", add=dhk, name="v_proj_dx")
    g_k, d_bk = _mm(hk, dk, dims="tn", out_dtype=BF16, colsum_b=True, name="k_proj_dw")
    g_v, d_bv = _mm(hk, dv, dims="tn", out_dtype=BF16, colsum_b=True, name="v_proj_dw")
    net.give("w_k", by_rows(g_k))
    net.give("w_v", by_rows(g_v))
    dx2, d_norm[0][2] = ffn_b(x2, dx3, 1)
    d_yn = _mm(dx2, w("wout"), dims="nt", name="ssm_out_proj_dx", comm=net.carry("ssm_out_proj_dx"))
    net.give("w_out", by_rows(_mm(yn, dx2, dims="tn", out_dtype=BF16, name="ssm_out_proj_dw")))
    dy_ssd, dzx, d_ssm_norm = _gate_norm_bwd(y_ssd, zx, w("ssm_norm_w"), d_yn, name="ssm_gate_norm_bwd")
    dxs, d_b, d_c, ddtg, dag, ddg = _ssd_bwd(xbc, dtg, ag, dg, states, dy_ssd, name="ssd_bwd", comm=net.carry("ssd_bwd"))
    dzx, d_conv_w, d_conv_b = _conv_bwd(zx, w("conv_w"), w("conv_b"), dxs, d_b, d_c, dzx, name="ssm_conv_bwd",
                                        comm=net.carry("ssm_conv_bwd"))
    ddtr, d_dt_bias, d_a_log = _dt_bwd(dtr, w("dt_bias"), w("a_log"), dt, _from_groups(ddtg), _from_groups(dag), name="ssm_dt_bwd")
    dh1 = _mm(dzx, w("w_in_t"), b_rows=(0, ZX_DIM), name="ssm_in_proj_dx")
    in_rows = N_DEV * IN_PROJ_SHARD
    g_in = _mm(dzx, h1, dims="tn", out_dtype=BF16, out_window=(0, in_rows), name="ssm_in_proj_dw")
    g_in = _mm(ddtr, h1, dims="tn", out_dtype=BF16, out_window=(ZX_DIM, in_rows), into=g_in, name="ssm_dt_proj_dw")
    net.give("w_in", g_in.reshape(N_DEV, IN_PROJ_SHARD, D_MODEL))
    dx1, d_norm[0][1] = _mm_norm_bwd(ddtr, w("w_in_t"), x1, nw[0][1], [dx2], b_rows=(ZX_DIM, SSM_HEADS), add=dh1,
                                     name="ssm_dt_proj_dx", comm=net.carry("ssm_norm_bwd"))
    dx0, d_norm[0][0] = ffn_b(x0, dx1, 0)

    small = {"norm_w": jnp.concatenate([d_norm[l][i] for l in range(2) for i in range(3)], axis=0),
             "ssm_conv_w": d_conv_w, "ssm_conv_b": d_conv_b, "ssm_dt_bias": d_dt_bias, "ssm_a_log": d_a_log,
             "ssm_d": ddg.reshape(1, SSM_HEADS), "ssm_norm_w": d_ssm_norm, "kv_norm_w": d_kvn,
             "b_k": d_bk, "b_v": d_bv, "attn_b_q": d_bq, "attn_sinks": d_sinks, "attn_b_o": d_bo, "final_norm_w": d_final}
    return loss, dx0, small


BLOCK_BYTES = 1 << 20


def _row_tile(rows, cols):
    for t in (512, 256, 128, 64, 32, 16):
        if rows % t == 0 and t * cols * 4 <= BLOCK_BYTES:
            return t
    return rows


def _cast_bf16(x, *, name):
    n_blk, rows, cols = x.shape
    tm = rows if rows * cols * 4 <= 2 * BLOCK_BYTES else _row_tile(rows, cols)
    spec = pl.BlockSpec((None, tm, cols), lambda b, i: (b, i, 0))

    def body(x_ref, o_ref):
        o_ref[...] = x_ref[...].astype(BF16)

    return pl.pallas_call(body, name=name, grid=(n_blk, rows // tm), in_specs=[spec], out_specs=spec,
                          out_shape=jax.ShapeDtypeStruct(x.shape, BF16), compiler_params=_params("parallel", "parallel"))(x)


def _pair_add(grad, theirs, *, name):
    n_slots, rows, cols = theirs.shape
    tm = rows if rows * cols * 4 <= 2 * BLOCK_BYTES else _row_tile(rows, cols)

    def body(g_ref, t_ref, o_ref):
        mine = jnp.where(lax.axis_index("c") == 0, g_ref[0].astype(F32), g_ref[1].astype(F32))
        o_ref[...] = (mine + t_ref[...].astype(F32)).astype(BF16)

    spec = pl.BlockSpec((None, tm, cols), lambda s, i: (s, i, 0))
    return pl.pallas_call(
        body, name=name, grid=(n_slots, rows // tm),
        in_specs=[pl.BlockSpec((2, tm, cols), lambda s, i: (s, i, 0)), spec], out_specs=spec,
        out_shape=jax.ShapeDtypeStruct(theirs.shape, BF16), compiler_params=_params("parallel", "parallel"),
    )(grad, theirs)


def _adam_update(g, w, m, v):
    m = ADAM_B1 * m + (1.0 - ADAM_B1) * g
    v = ADAM_B2 * v + (1.0 - ADAM_B2) * (g * g)
    m_hat = m / (1.0 - ADAM_B1 ** ADAM_STEP)
    v_hat = v / (1.0 - ADAM_B2 ** ADAM_STEP)
    delta = -ADAM_LR * (m_hat / (jnp.sqrt(v_hat) + ADAM_EPS) + ADAM_WD * w)
    return delta, m, v


def _adamw(parts, w, m, v, first_blk, prev, *, name, comm=None):
    n_blk, rows, cols = w.shape
    tm = _row_tile(rows, cols)
    n_tiles = rows // tm
    spec = pl.BlockSpec((None, tm, cols), lambda b, i: (first_blk + b, i, 0))
    n_prev, n_here = len(prev), len(parts)
    n_parts = parts[0].shape[0]

    def part_spec(q):
        return pl.BlockSpec((n_parts, tm, cols), lambda b, i: (0, jnp.where(b < q, 0, jnp.where(b == q, i, n_tiles - 1)), 0))

    def body(*refs):
        p_refs = refs[:n_here]
        w_ref, m_ref, v_ref = refs[n_here:n_here + 3]
        g_ref, d_ref, nm_ref, nv_ref = refs[n_here + 3 + n_prev:]
        b = pl.program_id(0)
        g = None
        for s in range(n_parts):
            t = p_refs[0][s]
            for q in range(1, n_here):
                t = jnp.where(b == q, p_refs[q][s], t)
            g = t.astype(F32) if g is None else g + t.astype(F32)
        delta, nm, nv = _adam_update(g, w_ref[...], m_ref[...], v_ref[...])
        g_ref[...] = g
        d_ref[...] = delta
        nm_ref[...] = nm
        nv_ref[...] = nv

    return _call(
        body, name=name, grid=(n_here, n_tiles),
        in_specs=[part_spec(q) for q in range(n_here)] + [spec, spec, spec] + [pl.BlockSpec(memory_space=pl.ANY)] * n_prev,
        out_specs=[spec] * 4, out_shape=[jax.ShapeDtypeStruct((n_blk, rows, cols), F32)] * 4,
        aliases={n_here + 3 + q: q for q in range(n_prev)}, sem=("arbitrary", "arbitrary"),
        args=[*parts, w, m, v, *prev], comm=comm)


def _sum_parts(parts, *, name):
    def body(p_ref, o_ref):
        g = p_ref[0]
        for s in range(1, N_DEV):
            g = g + p_ref[s]
        o_ref[...] = g

    return pl.pallas_call(body, name=name, out_shape=jax.ShapeDtypeStruct(parts.shape[1:], F32), compiler_params=_params())(parts)


def _adamw_packed(g, w, m, v, *, name):
    def body(g_ref, w_ref, m_ref, v_ref, d_ref, nm_ref, nv_ref):
        delta, nm, nv = _adam_update(g_ref[...], w_ref[...], m_ref[...], v_ref[...])
        d_ref[...] = delta
        nm_ref[...] = nm
        nv_ref[...] = nv

    return pl.pallas_call(body, name=name, out_shape=[jax.ShapeDtypeStruct(g.shape, F32)] * 3, compiler_params=_params())(g, w, m, v)


SUBLANES = 8


WIDE_PACK = 1024


def _pack(arrs, width=LANES):
    rows = []
    for a in arrs:
        a2 = a.reshape(-1, a.shape[-1])
        a2 = jnp.pad(a2, ((0, 0), (0, (-a2.shape[1]) % width)))
        rows += [a2[:, i * width:(i + 1) * width] for i in range(a2.shape[1] // width)]
    out = jnp.concatenate(rows, axis=0)
    return jnp.pad(out, ((0, (-out.shape[0]) % SUBLANES), (0, 0)))


def _unpack(packed, shapes, width=LANES):
    outs, r = [], 0
    for shp in shapes:
        lead, cols = math.prod(shp[:-1]), shp[-1]
        n_blocks = -(-cols // width)
        blocks = [packed[r + i * lead:r + (i + 1) * lead] for i in range(n_blocks)]
        outs.append(jnp.concatenate(blocks, axis=1)[:, :cols].reshape(shp))
        r += n_blocks * lead
    return outs


WEIGHT_NAMES = ("norm_w", "ffn_w_gate", "ffn_w_up", "ffn_w_down", "ssm_w_in", "ssm_conv_w", "ssm_conv_b", "ssm_dt_bias",
                "ssm_a_log", "ssm_d", "ssm_norm_w", "ssm_w_out", "kv_norm_w", "w_k", "b_k", "w_v", "b_v", "attn_w_q",
                "attn_b_q", "attn_sinks", "attn_w_o", "attn_b_o", "final_norm_w")
MATRIX_NAMES = ("ffn_w_gate", "ffn_w_up", "ffn_w_down", "ssm_w_in", "ssm_w_out", "w_k", "w_v", "attn_w_q", "attn_w_o")
VECTOR_NAMES = tuple(n for n in WEIGHT_NAMES if n not in MATRIX_NAMES)
SHARDED_VECTORS = ("norm_w", "ssm_conv_w", "ssm_conv_b", "ssm_norm_w")


GATHER_PLAN = {
    "gather_stage0": ("gate0", "up0", "down0", "vec"),
    "ffn_fwd0": ("w_in",),
    "ssm_in_proj": ("w_out", "gate1"),
    "ssm_conv_fwd": ("w_k", "w_v", "up1"),
    "ssd_fwd": ("down1", "gate2", "up2"),
    "ssm_out_proj": ("w_q", "w_o"),
    "ffn_fwd1": ("down2", "gate3"),
    "ffn_fwd2": ("up3",),
    "attn_fwd": ("down3",),
}
PAIR_PLAN = {
    "attn_out_proj_dx": ("gate3", "up3", "down3"),
    "ffn_bwd2": ("w_q", "w_o"),
    "k_proj_dx": ("gate2", "up2", "down2"),
    "ssm_out_proj_dx": ("w_k", "w_v", "gate1", "up1", "down1"),
    "ssd_bwd": ("w_out",),
    "ssm_norm_bwd": ("w_in",),
    "ffn_norm_bwd0": ("gate0", "up0", "down0"),
}
CHIP_PLAN = {
    "attn_bwd": ("gate3", "up3"),
    "ffn_bwd2": ("down3",),
    "ffn_bwd1": ("gate2", "up2", "w_q", "w_o"),
    "ssd_bwd": ("down2", "gate1", "up1", "down1", "w_k", "w_v"),
    "ssm_conv_bwd": ("w_out",),
    "ffn_bwd0": ("w_in",),
    "adamw_gate": ("gate0",),
    "adamw_up": ("up0",),
    "adamw_down": ("down0",),
}
FFN_PARAMS = {"gate": "ffn_w_gate", "up": "ffn_w_up", "down": "ffn_w_down"}
SINGLE_MATRICES = {"w_in": "ssm_w_in", "w_out": "ssm_w_out", "w_k": "w_k", "w_v": "w_v", "w_q": "attn_w_q", "w_o": "attn_w_o"}


TRANSPOSED = ("ffn_w_gate", "ffn_w_up", "ssm_w_in")


def _matrix_view(name, a):
    if name in TRANSPOSED:
        a = jnp.swapaxes(a, -1, -2)
    return a.reshape((-1,) + a.shape[-2:])


def _from_matrix_view(name, a, shape):
    if name in TRANSPOSED:
        return jnp.swapaxes(a.reshape(shape[:-2] + (shape[-1], shape[-2])), -1, -2)
    return a.reshape(shape)


class _MeshNet:
    def __init__(self, p):
        self.p = p
        self.views = {n: _matrix_view(n, p[n]) for n in MATRIX_NAMES}
        self.local = {"vec": _pack([p[n] for n in SHARDED_VECTORS])}
        for short, n in FFN_PARAMS.items():
            cast = _cast_bf16(self.views[n], name=f"cast_{short}")
            self.local.update({f"{short}{k}": (cast, k) for k in range(N_FFN)})
        for short, n in SINGLE_MATRICES.items():
            self.local[short] = (_cast_bf16(self.views[n], name=f"cast_{short}"), 0)
        self.gathered_at, self.pairs_at, self.parts_at, self.grads, self.cache = {}, {}, {}, {}, {}

    def carry(self, name):
        comms = []
        if name in GATHER_PLAN:
            keys, comm = GATHER_PLAN[name], _Gather([self.local[k] for k in GATHER_PLAN[name]])
            self.gathered_at.update({k: (comm, i) for i, k in enumerate(keys)})
            comms.append(comm)
        if name in CHIP_PLAN:
            sums = []
            for k in CHIP_PLAN[name]:
                comm, i = self.pairs_at[k]
                sums.append(_pair_add(self.grads[k], comm.results[i], name=f"pair_add_{k}"))
            comm = _ChipExchange(sums)
            self.parts_at.update({k: (comm, i) for i, k in enumerate(CHIP_PLAN[name])})
            comms.append(comm)
        if name in PAIR_PLAN:
            keys, comm = PAIR_PLAN[name], _PairSwap([self.grads[k] for k in PAIR_PLAN[name]])
            self.pairs_at.update({k: (comm, i) for i, k in enumerate(keys)})
            comms.append(comm)
        return comms

    def run(self, name):
        for comm in self.carry(name):
            _run_exchange(comm, name=name)

    def give(self, key, grad):
        self.grads[key] = grad

    def parts(self, key):
        comm, i = self.parts_at[key]
        return comm.results[i]

    def _gathered(self, key):
        comm, i = self.gathered_at[key]
        return comm.results[i]

    def _vec(self, r0, lead, n_blocks):
        vecs = self._gathered("vec")
        return jnp.concatenate([vecs[d, r0 + i * lead:r0 + (i + 1) * lead, :] for d in range(N_DEV) for i in range(n_blocks)], axis=1)

    def _derive(self, name):
        p = self.p
        if name[:-1] in FFN_PARAMS:
            return self._gathered(name)
        if name == "w_in_t":
            return self._gathered("w_in").reshape(N_DEV * IN_PROJ_SHARD, D_MODEL)
        by_rows = {"wout": "w_out", "wk": "w_k", "wv": "w_v", "wq": "w_q", "wo": "w_o"}
        if name in by_rows:
            g = self._gathered(by_rows[name])
            return g.reshape(N_DEV * g.shape[1], g.shape[2])
        vectors = {"norm_w": lambda: self._vec(0, 6, 1).reshape(2, 3, D_MODEL), "conv_w": lambda: self._vec(6, CONV_WIDTH, 3),
                   "conv_b": lambda: self._vec(18, 1, 3), "ssm_norm_w": lambda: self._vec(21, 1, 2)}
        if name in vectors:
            return vectors[name]()
        replicated = {"dt_bias": p["ssm_dt_bias"], "a_log": p["ssm_a_log"], "d_skip": p["ssm_d"], "kv_norm_w": p["kv_norm_w"][None],
                      "b_k": p["b_k"][None], "b_v": p["b_v"][None], "b_q": p["attn_b_q"], "sinks": p["attn_sinks"],
                      "b_o": p["attn_b_o"], "final_norm_w": p["final_norm_w"][None]}
        return replicated[name]

    def w(self, name):
        if name not in self.cache:
            self.cache[name] = self._derive(name)
        return self.cache[name]


def _step(x, target, p, m, v):
    pos = _slot(_position())
    net = _MeshNet(p)
    net.run("gather_stage0")
    loss, grad_x, small = _forward_backward(x, target, net)

    grads, deltas, new_m, new_v = {}, {}, {}, {}
    view = lambda d, n: _matrix_view(n, d[n])
    vec_gather = _Gather([_pack([small[n] for n in VECTOR_NAMES], WIDE_PACK)])
    for short, n in SINGLE_MATRICES.items():
        outs = _adamw([net.parts(short)], net.views[n], view(m, n), view(v, n), 0, [], name=f"adamw_{short}",
                      comm=[vec_gather] if short == "w_in" else None)
        grads[n], deltas[n], new_m[n], new_v[n] = [_from_matrix_view(n, o, p[n].shape) for o in outs]
    ffn_outs = {}
    for short, n in FFN_PARAMS.items():
        ffn_outs[short] = _adamw([net.parts(f"{short}{k}") for k in range(1, N_FFN)], net.views[n], view(m, n), view(v, n), 1, [],
                                 name=f"adamw_{short}", comm=net.carry(f"adamw_{short}"))
    for short, n in FFN_PARAMS.items():
        outs = _adamw([net.parts(f"{short}0")], net.views[n], view(m, n), view(v, n), 0, ffn_outs[short], name=f"adamw_{short}0")
        grads[n], deltas[n], new_m[n], new_v[n] = [_from_matrix_view(n, o, p[n].shape) for o in outs]
    vec_sum = _sum_parts(vec_gather.results[0], name="sum_vector_grads")
    full_shapes = {"norm_w": (2, 3, D_MODEL), "ssm_conv_w": (1, CONV_WIDTH, CONV_DIM), "ssm_conv_b": (1, CONV_DIM),
                   "ssm_norm_w": (1, D_INNER)}
    vec_full = dict(zip(VECTOR_NAMES, _unpack(vec_sum, [full_shapes.get(n, p[n].shape) for n in VECTOR_NAMES], WIDE_PACK)))
    for n in VECTOR_NAMES:
        g = vec_full[n]
        if n in SHARDED_VECTORS:
            per = p[n].shape[-1]
            g = lax.dynamic_slice_in_dim(g, pos * per, per, axis=g.ndim - 1)
        grads[n] = g
    packed = _adamw_packed(*[_pack([d[n] for n in VECTOR_NAMES]) for d in (grads, p, m, v)], name="adamw_vectors")
    shapes = [p[n].shape for n in VECTOR_NAMES]
    for d, pk in zip((deltas, new_m, new_v), packed):
        d.update(zip(VECTOR_NAMES, _unpack(pk, shapes)))
    return loss, grad_x, grads, deltas, new_m, new_v


def kernel(x, norm_w, ffn_w_gate, ffn_w_up, ffn_w_down, ssm_w_in, ssm_conv_w, ssm_conv_b, ssm_dt_bias, ssm_a_log, ssm_d, ssm_norm_w, ssm_w_out, kv_norm_w, w_k, b_k, w_v, b_v, attn_w_q, attn_b_q, attn_sinks, attn_w_o, attn_b_o, final_norm_w, loss_target, m_norm_w, m_ffn_w_gate, m_ffn_w_up, m_ffn_w_down, m_ssm_w_in, m_ssm_conv_w, m_ssm_conv_b, m_ssm_dt_bias, m_ssm_a_log, m_ssm_d, m_ssm_norm_w, m_ssm_w_out, m_kv_norm_w, m_w_k, m_b_k, m_w_v, m_b_v, m_attn_w_q, m_attn_b_q, m_attn_sinks, m_attn_w_o, m_attn_b_o, m_final_norm_w, v_norm_w, v_ffn_w_gate, v_ffn_w_up, v_ffn_w_down, v_ssm_w_in, v_ssm_conv_w, v_ssm_conv_b, v_ssm_dt_bias, v_ssm_a_log, v_ssm_d, v_ssm_norm_w, v_ssm_w_out, v_kv_norm_w, v_w_k, v_b_k, v_w_v, v_b_v, v_attn_w_q, v_attn_b_q, v_attn_sinks, v_attn_w_o, v_attn_b_o, v_final_norm_w):
    p = dict(zip(WEIGHT_NAMES, (norm_w, ffn_w_gate, ffn_w_up, ffn_w_down, ssm_w_in, ssm_conv_w, ssm_conv_b, ssm_dt_bias, ssm_a_log, ssm_d, ssm_norm_w, ssm_w_out, kv_norm_w, w_k, b_k, w_v, b_v, attn_w_q, attn_b_q, attn_sinks, attn_w_o, attn_b_o, final_norm_w)))
    m = dict(zip(WEIGHT_NAMES, (m_norm_w, m_ffn_w_gate, m_ffn_w_up, m_ffn_w_down, m_ssm_w_in, m_ssm_conv_w, m_ssm_conv_b, m_ssm_dt_bias, m_ssm_a_log, m_ssm_d, m_ssm_norm_w, m_ssm_w_out, m_kv_norm_w, m_w_k, m_b_k, m_w_v, m_b_v, m_attn_w_q, m_attn_b_q, m_attn_sinks, m_attn_w_o, m_attn_b_o, m_final_norm_w)))
    v = dict(zip(WEIGHT_NAMES, (v_norm_w, v_ffn_w_gate, v_ffn_w_up, v_ffn_w_down, v_ssm_w_in, v_ssm_conv_w, v_ssm_conv_b, v_ssm_dt_bias, v_ssm_a_log, v_ssm_d, v_ssm_norm_w, v_ssm_w_out, v_kv_norm_w, v_w_k, v_b_k, v_w_v, v_b_v, v_attn_w_q, v_attn_b_q, v_attn_sinks, v_attn_w_o, v_attn_b_o, v_final_norm_w)))
    loss, grad_x, grads, deltas, new_m, new_v = _step(x[0], loss_target[0], p, m, v)
    loss = lax.psum(loss[0, 0], ("x", "y", "c"))
    return (loss, grad_x[None], *[grads[n] for n in WEIGHT_NAMES], *[deltas[n] for n in WEIGHT_NAMES],
            *[new_m[n] for n in WEIGHT_NAMES], *[new_v[n] for n in WEIGHT_NAMES])
```

```python
import functools
import math

import jax
import jax.numpy as jnp
from jax import lax
from jax.experimental import pallas as pl
from jax.experimental.pallas import tpu as pltpu

F32 = jnp.float32
BF16 = jnp.bfloat16

N_DEV = 8
SEQ = 2048
D_MODEL = 1024
D_FF_SHARD = 352
N_FFN = 4
D_INNER = 2048
SSM_HEADS = 32
SSM_HEAD_DIM = 64
SSM_GROUPS = 4
HEADS_PER_GROUP = 8
SSM_STATE = 128
CHUNK = 128
N_CHUNKS = SEQ // CHUNK
GN = SSM_GROUPS * SSM_STATE
CONV_DIM = D_INNER + 2 * GN
CONV_WIDTH = 4
ZX_DIM = D_INNER + CONV_DIM
IN_PROJ_SHARD = 644
ATT_HEAD_DIM = 64
N_Q_HEADS = 16
N_KV_HEADS = 4
Q_PER_KV = 4
KV_DIM = N_KV_HEADS * ATT_HEAD_DIM
WINDOW = 128
ROPE_THETA = 10000.0
EPS = 1e-5
FFN_RES_WEIGHT = 0.5
ATT_SCALE = 1.0 / math.sqrt(ATT_HEAD_DIM)
NEG_BIG = -1e30

ADAM_LR = 0.001
ADAM_B1 = 0.9
ADAM_B2 = 0.999
ADAM_EPS = 1e-08
ADAM_WD = 0.01
ADAM_STEP = 10

VMEM_LIMIT_BYTES = 56 * 1024 * 1024
FFN_BWD_VMEM_LIMIT_BYTES = 61 * 1024 * 1024

NN = (((1,), (0,)), ((), ()))
NT = (((1,), (1,)), ((), ()))
TN = (((0,), (0,)), ((), ()))
_DIMS = {"nn": NN, "nt": NT, "tn": TN}


def _params(*sem):
    return pltpu.CompilerParams(dimension_semantics=sem if sem else None, vmem_limit_bytes=VMEM_LIMIT_BYTES)


def _dot(a, b, dims=NN):
    return lax.dot_general(a.astype(BF16), b.astype(BF16), dims, preferred_element_type=F32)


def _sigmoid(x):
    return 1.0 / (1.0 + jnp.exp(-x))


def _dsilu(x, s):
    return s * (1.0 + x * (1.0 - s))


def _rms(x):
    r = lax.rsqrt(jnp.mean(x * x, axis=-1, keepdims=True) + EPS)
    return x * r, r


def _sum_all(x):
    return jnp.sum(jnp.sum(x, axis=1, keepdims=True), axis=0, keepdims=True)


MESH = pl.DeviceIdType.MESH
N_PEERS = N_DEV - 1
N_CHIPS = N_DEV // 2


def _position():
    return lax.axis_index("x"), lax.axis_index("y"), lax.axis_index("c")


def _slot(p):
    return 4 * p[0] + 2 * p[1] + p[2]


class _Exchange:
    def __init__(self, arrays, out_shapes):
        n = len(arrays)
        self.arrays = list(arrays)
        self.out_shapes = out_shapes
        self.scratch = [pltpu.SemaphoreType.DMA((n, N_PEERS)), pltpu.SemaphoreType.DMA((n, N_PEERS)), pltpu.SemaphoreType.DMA((n,))]
        self.results = None


class _Gather(_Exchange):
    def __init__(self, pieces):
        pieces = [p if isinstance(p, tuple) else (p, None) for p in pieces]
        self.blocks = [k for _, k in pieces]
        shapes = [a.shape if k is None else a.shape[1:] for a, k in pieces]
        super().__init__([a for a, _ in pieces], [jax.ShapeDtypeStruct((N_DEV,) + s, a.dtype) for s, (a, _) in zip(shapes, pieces)])

    def _plan(self, ins, outs, sems):
        send_sems, recv_sems, local_sems = sems
        x, y, c = _position()
        me, sibling = (x, y, c), (x, y, 1 - c)
        chips = [(1 - x, y), (x, 1 - y), (1 - x, 1 - y)]
        n = len(ins)
        ins = [r if k is None else r.at[k] for r, k in zip(ins, self.blocks)]

        def copy(a, k, block, to, src=None):
            dst = outs[a].at[_slot(block)]
            return pltpu.make_async_remote_copy(src_ref=dst if src is None else src, dst_ref=dst, send_sem=send_sems.at[a, k],
                                                recv_sem=recv_sems.at[a, k], device_id=to, device_id_type=MESH)

        mine = [pltpu.make_async_copy(ins[a], outs[a].at[_slot(me)], local_sems.at[a]) for a in range(n)]
        first = []
        for a in range(n):
            first.append(copy(a, 0, me, sibling, src=ins[a]))
            first += [copy(a, 1 + j, me, (*chip, c), src=ins[a]) for j, chip in enumerate(chips)]
        return n, c, me, sibling, chips, copy, mine, first

    def start(self, ins, outs, sems):
        _, _, _, _, _, _, mine, first = self._plan(ins, outs, sems)
        for cp in mine + first:
            cp.start()

    def finish(self, ins, outs, sems):
        n, c, me, sibling, chips, copy, mine, first = self._plan(ins, outs, sems)
        passed = []
        for j, chip in enumerate(chips):
            for a in range(n):
                copy(a, 1 + j, (*chip, c), me).wait_recv()
                fwd = copy(a, 4 + j, (*chip, c), sibling)
                fwd.start()
                passed.append(fwd)
        for a in range(n):
            copy(a, 0, sibling, me).wait_recv()
            for j, chip in enumerate(chips):
                copy(a, 4 + j, (*chip, 1 - c), me).wait_recv()
        for cp in first + passed:
            cp.wait_send()
        for cp in mine:
            cp.wait()


class _PairSwap(_Exchange):
    def __init__(self, arrays):
        n = len(arrays)
        self.arrays = list(arrays)
        self.out_shapes = [jax.ShapeDtypeStruct((N_CHIPS,) + a.shape[1:], a.dtype) for a in arrays]
        self.scratch = [pltpu.SemaphoreType.DMA((n, N_CHIPS)), pltpu.SemaphoreType.DMA((n, N_CHIPS))]
        self.results = None

    def _plan(self, ins, outs, sems):
        send_sems, recv_sems = sems
        x, y, c = _position()
        return [pltpu.make_async_remote_copy(src_ref=ins[a].at[2 * q + 1 - c], dst_ref=outs[a].at[q], send_sem=send_sems.at[a, q],
                                             recv_sem=recv_sems.at[a, q], device_id=(x, y, 1 - c), device_id_type=MESH)
                for a in range(len(ins)) for q in range(N_CHIPS)]

    def start(self, ins, outs, sems):
        for cp in self._plan(ins, outs, sems):
            cp.start()

    def finish(self, ins, outs, sems):
        for cp in self._plan(ins, outs, sems):
            cp.wait()


class _ChipExchange(_Exchange):
    def __init__(self, arrays):
        n = len(arrays)
        self.arrays = list(arrays)
        self.out_shapes = [jax.ShapeDtypeStruct(a.shape, a.dtype) for a in arrays]
        self.scratch = [pltpu.SemaphoreType.DMA((n, 3)), pltpu.SemaphoreType.DMA((n, 3)), pltpu.SemaphoreType.DMA((n,))]
        self.results = None

    def _plan(self, ins, outs, sems):
        send_sems, recv_sems, local_sems = sems
        x, y, c = _position()
        here = 2 * x + y
        chips = [(1 - x, y), (x, 1 - y), (1 - x, 1 - y)]
        n = len(ins)

        def copy(a, k, src_slot, dst_slot):
            return pltpu.make_async_remote_copy(src_ref=ins[a].at[src_slot], dst_ref=outs[a].at[dst_slot], send_sem=send_sems.at[a, k],
                                                recv_sem=recv_sems.at[a, k], device_id=(*chips[k], c), device_id_type=MESH)

        there = [2 * qx + qy for qx, qy in chips]
        mine = [pltpu.make_async_copy(ins[a].at[here], outs[a].at[here], local_sems.at[a]) for a in range(n)]
        sends = [copy(a, k, there[k], here) for a in range(n) for k in range(3)]
        arrivals = lambda: [copy(a, k, here, there[k]) for a in range(n) for k in range(3)]
        return mine, sends, arrivals

    def start(self, ins, outs, sems):
        mine, sends, _ = self._plan(ins, outs, sems)
        for cp in mine + sends:
            cp.start()

    def finish(self, ins, outs, sems):
        mine, sends, arrivals = self._plan(ins, outs, sems)
        for cp in arrivals():
            cp.wait_recv()
        for cp in sends:
            cp.wait_send()
        for cp in mine:
            cp.wait()


def _call(body, *, name, grid, in_specs, out_specs, out_shape, args, scratch_shapes=(), sem=(), comm=(), aliases=None,
          vmem_limit=VMEM_LIMIT_BYTES):
    single = not isinstance(out_shape, (list, tuple))
    out_shape = [out_shape] if single else list(out_shape)
    out_specs = [out_specs] if single else list(out_specs)
    comms = list(comm or ())
    n_in, n_out, n_scr = len(args), len(out_shape), len(scratch_shapes)
    params = pltpu.CompilerParams(dimension_semantics=tuple(sem) if sem else None, vmem_limit_bytes=vmem_limit)
    if not comms:
        res = pl.pallas_call(body, name=name, grid=grid, in_specs=list(in_specs), out_specs=out_specs, out_shape=out_shape,
                             scratch_shapes=list(scratch_shapes), input_output_aliases=aliases or {}, compiler_params=params)(*args)
        return res[0] if single else res
    counts = [n_in] + [len(c.arrays) for c in comms] + [n_out] + [len(c.out_shapes) for c in comms] + [n_scr] + [len(c.scratch) for c in comms]
    nc = len(comms)

    def carried(*refs):
        pos, groups = 0, []
        for cnt in counts:
            groups.append(refs[pos:pos + cnt])
            pos += cnt
        ins, c_ins = groups[0], groups[1:1 + nc]
        outs, c_outs = groups[1 + nc], groups[2 + nc:2 + 2 * nc]
        scr, c_sems = groups[2 + 2 * nc], groups[3 + 2 * nc:]
        ids = [pl.program_id(d) for d in range(len(grid))]
        is_first = functools.reduce(jnp.logical_and, [i == 0 for i in ids])
        is_last = functools.reduce(jnp.logical_and, [i == g - 1 for i, g in zip(ids, grid)])

        @pl.when(is_first)
        def _():
            for q, c in enumerate(comms):
                c.start(c_ins[q], c_outs[q], c_sems[q])

        body(*ins, *outs, *scr)

        @pl.when(is_last)
        def _():
            for q, c in enumerate(comms):
                c.finish(c_ins[q], c_outs[q], c_sems[q])

    anyspec = pl.BlockSpec(memory_space=pl.ANY)
    c_arrays = [a for c in comms for a in c.arrays]
    c_shapes = [s for c in comms for s in c.out_shapes]
    res = pl.pallas_call(
        carried, name=name, grid=grid, in_specs=list(in_specs) + [anyspec] * len(c_arrays), out_specs=out_specs + [anyspec] * len(c_shapes),
        out_shape=out_shape + c_shapes, scratch_shapes=list(scratch_shapes) + [s for c in comms for s in c.scratch],
        input_output_aliases=aliases or {}, compiler_params=params)(*args, *c_arrays)
    pos = n_out
    for c in comms:
        c.results = list(res[pos:pos + len(c.out_shapes)])
        pos += len(c.out_shapes)
    return res[0] if single else list(res[:n_out])


def _run_exchange(comm, *, name):
    def body(*refs):
        n_ci, n_co = len(comm.arrays), len(comm.out_shapes)
        ins, outs, sems = refs[:n_ci], refs[n_ci:n_ci + n_co], refs[n_ci + n_co:]
        comm.start(ins, outs, sems)
        comm.finish(ins, outs, sems)

    anyspec = pl.BlockSpec(memory_space=pl.ANY)
    comm.results = list(pl.pallas_call(
        body, name=name, in_specs=[anyspec] * len(comm.arrays), out_specs=[anyspec] * len(comm.out_shapes),
        out_shape=list(comm.out_shapes), scratch_shapes=list(comm.scratch))(*comm.arrays))
    return comm.results


def _mm(a, b, *, dims="nn", bias=None, res=None, out_dtype=F32, name, tm=1024, tn=1024, tk=1024, comm=None, b_rows=None,
        out_window=None, into=None, colsum_b=False):
    if dims == "tn":
        k_dim, m_dim = a.shape
    else:
        m_dim, k_dim = a.shape
    row0, n_rows = b_rows if b_rows is not None else (0, b.shape[0])
    n_dim = n_rows if dims == "nt" else b.shape[1]
    assert dims == "nt" or n_rows == k_dim, (name, a.shape, b.shape, b_rows)
    tm, tn, tk = min(tm, m_dim), min(tn, n_dim), min(tk, k_dim)
    assert m_dim % tm == 0 and n_dim % tn == 0 and k_dim % tk == 0, (name, a.shape, b.shape)
    nk = k_dim // tk
    a_spec = pl.BlockSpec((tk, tm), lambda i, j, k: (k, i)) if dims == "tn" else pl.BlockSpec((tm, tk), lambda i, j, k: (i, k))
    if dims == "nt":
        assert row0 % tn == 0
        b_spec = pl.BlockSpec((tn, tk), lambda i, j, k: (row0 // tn + j, k))
    else:
        assert row0 % tk == 0
        b_spec = pl.BlockSpec((tk, tn), lambda i, j, k: (row0 // tk + k, j))
    in_specs, args = [a_spec, b_spec], [a, b]
    if bias is not None:
        in_specs.append(pl.BlockSpec((1, tn), lambda i, j, k: (0, j)))
        args.append(bias)
    if res is not None:
        in_specs.append(pl.BlockSpec((tm, tn), lambda i, j, k: (i, j)))
        args.append(res)
    dn = _DIMS[dims]

    if colsum_b:
        assert dims == "tn" and m_dim == tm and into is None and out_window is None

    def body(*refs):
        a_ref, b_ref = refs[0], refs[1]
        acc_ref = refs[-1]
        o_ref = refs[-3] if colsum_b else refs[-2]
        k = pl.program_id(2)

        @pl.when(k == 0)
        def _():
            acc_ref[...] = jnp.zeros_like(acc_ref)
            if colsum_b:
                refs[-2][...] = jnp.zeros_like(refs[-2])

        acc_ref[...] += _dot(a_ref[...], b_ref[...], dn)
        if colsum_b:
            refs[-2][...] += jnp.sum(b_ref[...].astype(F32), axis=0, keepdims=True)

        @pl.when(k == nk - 1)
        def _():
            r = acc_ref[...]
            pos = 2
            if bias is not None:
                r = r + refs[pos][...]
                pos += 1
            if res is not None:
                r = r + refs[pos][...]
            o_ref[...] = r.astype(out_dtype)

    out_row0, out_rows = out_window if out_window is not None else (0, m_dim)
    assert out_row0 % tm == 0
    aliases = None
    if into is not None:
        assert into.shape == (out_rows, n_dim) and into.dtype == out_dtype
        in_specs.append(pl.BlockSpec(memory_space=pl.ANY))
        args.append(into)
        aliases = {len(args) - 1: 0}
    out_spec = pl.BlockSpec((tm, tn), lambda i, j, k: (out_row0 // tm + i, j))
    out_shape = jax.ShapeDtypeStruct((out_rows, n_dim), out_dtype)
    if colsum_b:
        out_spec = [out_spec, pl.BlockSpec((1, tn), lambda i, j, k: (0, j))]
        out_shape = [out_shape, jax.ShapeDtypeStruct((1, n_dim), F32)]
    return _call(
        body, name=name, grid=(m_dim // tm, n_dim // tn, nk), in_specs=in_specs, out_specs=out_spec, out_shape=out_shape,
        aliases=aliases, scratch_shapes=[pltpu.VMEM((tm, tn), F32)], sem=("parallel", "parallel", "arbitrary"), args=args, comm=comm)


def _mm_norm_bwd(a, b, x, nw, res, *, dims="nn", b_rows=None, add=None, name, tm=1024, tk=1024, comm=None):
    m_dim, k_dim = a.shape
    row0, n_rows = b_rows if b_rows is not None else (0, b.shape[0])
    d_dim = x.shape[1]
    tm, tk = min(tm, m_dim), min(tk, k_dim)
    assert m_dim % tm == 0 and k_dim % tk == 0 and (n_rows if dims == "nt" else b.shape[1]) == d_dim, (name, a.shape, b.shape)
    nk = k_dim // tk
    if dims == "nt":
        assert row0 % d_dim == 0
        b_spec = pl.BlockSpec((d_dim, tk), lambda i, k: (row0 // d_dim, k))
    else:
        assert row0 % tk == 0 and n_rows == k_dim
        b_spec = pl.BlockSpec((tk, d_dim), lambda i, k: (row0 // tk + k, 0))
    row = pl.BlockSpec((tm, d_dim), lambda i, k: (i, 0))
    vec = pl.BlockSpec((1, d_dim), lambda i, k: (0, 0))
    extra = ([add] if add is not None else []) + list(res)
    dn = _DIMS[dims]

    def body(*refs):
        a_ref, b_ref, x_ref, nw_ref = refs[:4]
        extra_refs = refs[4:4 + len(extra)]
        dx_ref, dnw_ref, dob_ref, acc_ref = refs[-4:]
        i, k = pl.program_id(0), pl.program_id(1)

        @pl.when(k == 0)
        def _():
            acc_ref[...] = jnp.zeros_like(acc_ref)

        @pl.when((i == 0) & (k == 0))
        def _():
            dnw_ref[...] = jnp.zeros_like(dnw_ref)

        acc_ref[...] += _dot(a_ref[...], b_ref[...], dn)

        @pl.when(k == nk - 1)
        def _():
            dh = acc_ref[...]
            rest = list(extra_refs)
            if add is not None:
                dh = dh + rest.pop(0)[...]
            xhat, r = _rms(x_ref[...])
            dxhat = dh * nw_ref[...]
            dx = r * (dxhat - xhat * jnp.mean(dxhat * xhat, axis=-1, keepdims=True))
            for rr in rest:
                dx = dx + rr[...]
            dx_ref[...] = dx
            dob_ref[...] = (FFN_RES_WEIGHT * dx).astype(BF16)
            dnw_ref[...] += jnp.sum(dh * xhat, axis=0, keepdims=True)

    return _call(
        body, name=name, grid=(m_dim // tm, nk),
        in_specs=[pl.BlockSpec((tm, tk), lambda i, k: (i, k)), b_spec, row, vec] + [row] * len(extra), out_specs=[row, vec, row],
        out_shape=[jax.ShapeDtypeStruct((m_dim, d_dim), F32), jax.ShapeDtypeStruct((1, d_dim), F32),
                   jax.ShapeDtypeStruct((m_dim, d_dim), BF16)],
        scratch_shapes=[pltpu.VMEM((tm, d_dim), F32)], sem=("arbitrary", "arbitrary"), args=[a, b, x, nw] + extra, comm=comm)


def _rope_rotate(x, cos_t, sin_t):
    rows, width = x.shape
    half = ATT_HEAD_DIM // 2
    lane = lax.broadcasted_iota(jnp.int32, (rows, width), 1)
    first = (lane % ATT_HEAD_DIM) < half
    rot = jnp.where(first, -pltpu.roll(x, width - half, 1), pltpu.roll(x, half, 1))
    reps = width // 128
    return x * jnp.tile(cos_t, (1, reps)) + rot * jnp.tile(sin_t, (1, reps))


def _norm_mm(x, nw, w, bias, *, name, tm=1024, tn=1024, comm=None, w_rows=None, rope=None):
    t_dim, d_dim = x.shape
    transposed = w_rows is not None
    n_dim = w_rows if transposed else w.shape[1]
    tn = min(tn, n_dim)
    assert t_dim % tm == 0 and n_dim % tn == 0
    has_bias = bias is not None
    w_spec = pl.BlockSpec((tn, d_dim), lambda i, j: (j, 0)) if transposed else pl.BlockSpec((d_dim, tn), lambda i, j: (0, j))
    dn = NT if transposed else NN
    in_specs = [pl.BlockSpec((tm, d_dim), lambda i, j: (i, 0)), pl.BlockSpec((1, d_dim), lambda i, j: (0, 0)), w_spec]
    args = [x, nw, w]
    if has_bias:
        in_specs.append(pl.BlockSpec((1, tn), lambda i, j: (0, j)))
        args.append(bias)
    if rope is not None:
        in_specs += [pl.BlockSpec((tm, LANES), lambda i, j: (i, 0))] * 2
        args += list(rope)

    def body(*refs):
        x_ref, nw_ref, w_ref = refs[:3]
        o_ref, h_ref = refs[-2], refs[-1]

        @pl.when(pl.program_id(1) == 0)
        def _():
            xhat, _ = _rms(x_ref[...])
            h_ref[...] = (xhat * nw_ref[...]).astype(BF16)

        r = _dot(h_ref[...], w_ref[...], dn)
        if has_bias:
            r = r + refs[3][...]
        if rope is not None:
            r = _rope_rotate(r, refs[-4][...], refs[-3][...])
        o_ref[...] = r

    return _call(
        body, name=name, grid=(t_dim // tm, n_dim // tn), in_specs=in_specs,
        out_specs=[pl.BlockSpec((tm, tn), lambda i, j: (i, j)), pl.BlockSpec((tm, d_dim), lambda i, j: (i, 0))],
        out_shape=[jax.ShapeDtypeStruct((t_dim, n_dim), F32), jax.ShapeDtypeStruct((t_dim, d_dim), BF16)],
        sem=("parallel", "arbitrary"), args=args, comm=comm)


def _norm_bwd(x, nw, dh, res, *, name, tm=256, comm=None):
    t_dim, d_dim = x.shape
    n_res = len(res)
    row = pl.BlockSpec((tm, d_dim), lambda i: (i, 0))
    vec = pl.BlockSpec((1, d_dim), lambda i: (0, 0))

    def body(*refs):
        x_ref, nw_ref, dh_ref = refs[:3]
        dx_ref, dnw_ref = refs[-2], refs[-1]
        xhat, r = _rms(x_ref[...])
        dh = dh_ref[...]
        dxhat = dh * nw_ref[...]
        dx = r * (dxhat - xhat * jnp.mean(dxhat * xhat, axis=-1, keepdims=True))
        for rr in refs[3:3 + n_res]:
            dx = dx + rr[...]
        dx_ref[...] = dx

        @pl.when(pl.program_id(0) == 0)
        def _():
            dnw_ref[...] = jnp.zeros_like(dnw_ref)

        dnw_ref[...] += jnp.sum(dh * xhat, axis=0, keepdims=True)

    return _call(
        body, name=name, grid=(t_dim // tm,), in_specs=[row, vec, row] + [row] * n_res,
        out_specs=[row, vec],
        out_shape=[jax.ShapeDtypeStruct((t_dim, d_dim), F32), jax.ShapeDtypeStruct((1, d_dim), F32)],
        sem=("arbitrary",), args=[x, nw, dh, *res], comm=comm)


FFN_ROW_TILE = 512
FFN_SHARDS_PER_STEP = 2
FFN_STEPS = N_DEV // FFN_SHARDS_PER_STEP
FFN_STEP_COLS = FFN_SHARDS_PER_STEP * D_FF_SHARD


def _ffn_step_view(w):
    return w.reshape(FFN_STEPS, FFN_STEP_COLS, w.shape[-1])


def _ffn_specs(t_dim, d_dim):
    full = pl.BlockSpec((t_dim, d_dim), lambda j: (0, 0))
    wspec = pl.BlockSpec((None, FFN_STEP_COLS, d_dim), lambda j: (j, 0, 0))
    pre = pl.BlockSpec((None, t_dim, FFN_STEP_COLS), lambda j: (j, 0, 0))
    return full, wspec, pre


def _ffn_fwd(x, nw, wg, wu, wd, *, name, comm=None):
    t_dim, d_dim = x.shape
    n_tiles = t_dim // FFN_ROW_TILE

    def body(x_ref, nw_ref, wg_ref, wu_ref, wd_ref, o_ref, g_ref, u_ref, h_ref):
        j = pl.program_id(0)

        @pl.when(j == 0)
        def _():
            xhat, _ = _rms(x_ref[...])
            h_ref[...] = (xhat * nw_ref[...]).astype(BF16)
            o_ref[...] = jnp.zeros_like(o_ref)

        for t in range(n_tiles):
            rows = pl.ds(t * FFN_ROW_TILE, FFN_ROW_TILE)
            h = h_ref[rows, :]
            g = _dot(h, wg_ref[...], NT)
            u = _dot(h, wu_ref[...], NT)
            g_ref[rows, :] = g.astype(BF16)
            u_ref[rows, :] = u.astype(BF16)
            o_ref[rows, :] += _dot(g * _sigmoid(g) * u, wd_ref[...])

        @pl.when(j == FFN_STEPS - 1)
        def _():
            o_ref[...] = x_ref[...] + FFN_RES_WEIGHT * o_ref[...]

    full, wspec, pre = _ffn_specs(t_dim, d_dim)
    pre_shape = jax.ShapeDtypeStruct((FFN_STEPS, t_dim, FFN_STEP_COLS), BF16)
    return _call(
        body, name=name, grid=(FFN_STEPS,),
        in_specs=[full, pl.BlockSpec((1, d_dim), lambda j: (0, 0)), wspec, wspec, wspec],
        out_specs=[full, pre, pre, full],
        out_shape=[jax.ShapeDtypeStruct((t_dim, d_dim), F32), pre_shape, pre_shape, jax.ShapeDtypeStruct((t_dim, d_dim), BF16)],
        sem=("arbitrary",), args=[x, nw, _ffn_step_view(wg), _ffn_step_view(wu), _ffn_step_view(wd)], comm=comm)


def _ffn_bwd(h, dob, pre_g, pre_u, wg, wu, wd, *, name, comm=None):
    t_dim, d_dim = h.shape
    n_tiles = t_dim // FFN_ROW_TILE

    def body(h_ref, dob_ref, g_ref, u_ref, wg_ref, wu_ref, wd_ref, dh_ref, gg_ref, gu_ref, gd_ref, dwg_scr, dwu_scr, dwd_scr):
        @pl.when(pl.program_id(0) == 0)
        def _():
            dh_ref[...] = jnp.zeros_like(dh_ref)

        for t in range(n_tiles):
            rows = pl.ds(t * FFN_ROW_TILE, FFN_ROW_TILE)
            hh = h_ref[rows, :]
            do = dob_ref[rows, :]
            g = g_ref[rows, :].astype(F32)
            u = u_ref[rows, :].astype(F32)
            sg = _sigmoid(g)
            s = g * sg
            da = _dot(do, wd_ref[...], NT)
            dwd = _dot(s * u, do, TN)
            du = (da * s).astype(BF16)
            dg = (da * u * _dsilu(g, sg)).astype(BF16)
            dwg = _dot(dg, hh, TN)
            dwu = _dot(du, hh, TN)
            if t == 0:
                dwd_scr[...] = dwd
                dwg_scr[...] = dwg
                dwu_scr[...] = dwu
            else:
                dwd_scr[...] += dwd
                dwg_scr[...] += dwg
                dwu_scr[...] += dwu
            dh_ref[rows, :] += _dot(dg, wg_ref[...]) + _dot(du, wu_ref[...])
        gg_ref[...] = dwg_scr[...].astype(BF16)
        gu_ref[...] = dwu_scr[...].astype(BF16)
        gd_ref[...] = dwd_scr[...].astype(BF16)

    full, wspec, pre = _ffn_specs(t_dim, d_dim)
    gspec = pl.BlockSpec((None, FFN_STEP_COLS, d_dim), lambda j: (j, 0, 0), pipeline_mode=pl.Buffered(1))
    grad_shape = jax.ShapeDtypeStruct((FFN_STEPS, FFN_STEP_COLS, d_dim), BF16)
    dh, gg, gu, gd = _call(
        body, name=name, grid=(FFN_STEPS,),
        in_specs=[full, full, pre, pre, wspec, wspec, wspec], out_specs=[full, gspec, gspec, gspec],
        out_shape=[jax.ShapeDtypeStruct((t_dim, d_dim), F32)] + [grad_shape] * 3,
        scratch_shapes=[pltpu.VMEM((FFN_STEP_COLS, d_dim), F32)] * 3, sem=("arbitrary",), vmem_limit=FFN_BWD_VMEM_LIMIT_BYTES,
        args=[h, dob, pre_g, pre_u, _ffn_step_view(wg), _ffn_step_view(wu), _ffn_step_view(wd)], comm=comm)
    return dh, gg.reshape(wg.shape), gu.reshape(wu.shape), gd.reshape(wd.shape)


CONV_COLS = 256


def _shift_down(u, s, rows):
    return jnp.where(rows >= s, pltpu.roll(u, s, 0), 0.0)


def _shift_up(u, s, rows, t_dim):
    return jnp.where(rows < t_dim - s, pltpu.roll(u, t_dim - s, 0), 0.0)


def _conv_pre(u, w_ref, b_ref, rows):
    c = b_ref[...] + w_ref[CONV_WIDTH - 1:CONV_WIDTH, :] * u
    for k in range(CONV_WIDTH - 1):
        c = c + w_ref[k:k + 1, :] * _shift_down(u, CONV_WIDTH - 1 - k, rows)
    return c


def _conv_fwd(zx, cw, cb, *, name, comm=None):
    t_dim = zx.shape[0]
    off = D_INNER // CONV_COLS

    def body(u_ref, w_ref, b_ref, o_ref):
        rows = lax.broadcasted_iota(jnp.int32, (t_dim, CONV_COLS), 0)
        c = _conv_pre(u_ref[...], w_ref, b_ref, rows)
        o_ref[...] = c * _sigmoid(c)

    return _call(
        body, name=name, grid=(CONV_DIM // CONV_COLS,),
        in_specs=[pl.BlockSpec((t_dim, CONV_COLS), lambda j: (0, off + j)),
                  pl.BlockSpec((CONV_WIDTH, CONV_COLS), lambda j: (0, j)), pl.BlockSpec((1, CONV_COLS), lambda j: (0, j))],
        out_specs=pl.BlockSpec((t_dim, CONV_COLS), lambda j: (0, j)),
        out_shape=jax.ShapeDtypeStruct((t_dim, CONV_DIM), F32), sem=("parallel",), args=[zx, cw, cb], comm=comm)


def _conv_bwd(zx, cw, cb, dxs, db, dc, dzx, *, name, comm=None):
    t_dim = zx.shape[0]
    off = D_INNER // CONV_COLS
    n_xs = D_INNER // CONV_COLS
    n_b = GN // CONV_COLS

    def body(u_ref, w_ref, b_ref, dxs_ref, db_ref, dc_ref, dzx_in, dzx_ref, dw_ref, dbias_ref):
        j = pl.program_id(0)
        rows = lax.broadcasted_iota(jnp.int32, (t_dim, CONV_COLS), 0)
        u = u_ref[...]
        c = _conv_pre(u, w_ref, b_ref, rows)
        d = jnp.where(j < n_xs, dxs_ref[...], jnp.where(j < n_xs + n_b, db_ref[...], dc_ref[...]))
        dcv = d * _dsilu(c, _sigmoid(c))
        dpre = w_ref[CONV_WIDTH - 1:CONV_WIDTH, :] * dcv
        dw_ref[CONV_WIDTH - 1:CONV_WIDTH, :] = jnp.sum(dcv * u, axis=0, keepdims=True)
        for k in range(CONV_WIDTH - 1):
            s = CONV_WIDTH - 1 - k
            dpre = dpre + w_ref[k:k + 1, :] * _shift_up(dcv, s, rows, t_dim)
            dw_ref[k:k + 1, :] = jnp.sum(dcv * _shift_down(u, s, rows), axis=0, keepdims=True)
        dzx_ref[...] = dpre
        dbias_ref[...] = jnp.sum(dcv, axis=0, keepdims=True)

    blk = lambda n: pl.BlockSpec((t_dim, CONV_COLS), n)
    return _call(
        body, name=name, grid=(CONV_DIM // CONV_COLS,),
        in_specs=[blk(lambda j: (0, off + j)), pl.BlockSpec((CONV_WIDTH, CONV_COLS), lambda j: (0, j)),
                  pl.BlockSpec((1, CONV_COLS), lambda j: (0, j)),
                  blk(lambda j: (0, jnp.minimum(j, n_xs - 1))),
                  blk(lambda j: (0, jnp.clip(j - n_xs, 0, n_b - 1))),
                  blk(lambda j: (0, jnp.clip(j - n_xs - n_b, 0, n_b - 1))),
                  pl.BlockSpec(memory_space=pl.ANY)],
        out_specs=[blk(lambda j: (0, off + j)), pl.BlockSpec((CONV_WIDTH, CONV_COLS), lambda j: (0, j)),
                   pl.BlockSpec((1, CONV_COLS), lambda j: (0, j))],
        out_shape=[jax.ShapeDtypeStruct(dzx.shape, F32), jax.ShapeDtypeStruct((CONV_WIDTH, CONV_DIM), F32),
                   jax.ShapeDtypeStruct((1, CONV_DIM), F32)],
        aliases={6: 0}, sem=("parallel",), args=[zx, cw, cb, dxs, db, dc, dzx], comm=comm)


def _softplus_parts(x):
    e = jnp.exp(-jnp.abs(x))
    u = 1.0 + e
    log1p_e = jnp.where(u == 1.0, e, jnp.log(u) * e / jnp.where(u == 1.0, 1.0, u - 1.0))
    return jnp.maximum(x, 0.0) + log1p_e


def _dt_prep(dtr, dt_bias, a_log, *, name):
    def body(dtr_ref, bias_ref, alog_ref, dt_ref, a_ref):
        dt = _softplus_parts(dtr_ref[...] + bias_ref[...])
        dt_ref[...] = dt
        a_ref[...] = dt * (-jnp.exp(alog_ref[...]))

    return pl.pallas_call(body, name=name, out_shape=[jax.ShapeDtypeStruct(dtr.shape, F32)] * 2,
                          compiler_params=_params())(dtr, dt_bias, a_log)


def _dt_bwd(dtr, dt_bias, a_log, dt, ddt, da, *, name):
    def body(dtr_ref, bias_ref, alog_ref, dt_ref, ddt_ref, da_ref, ddtr_ref, dbias_ref, dalog_ref):
        a_neg = -jnp.exp(alog_ref[...])
        da_v = da_ref[...]
        ddt_tot = ddt_ref[...] + da_v * a_neg
        ddtr = ddt_tot * _sigmoid(dtr_ref[...] + bias_ref[...])
        ddtr_ref[...] = ddtr
        dbias_ref[...] = jnp.sum(ddtr, axis=0, keepdims=True)
        dalog_ref[...] = jnp.sum(da_v * dt_ref[...], axis=0, keepdims=True) * a_neg

    return pl.pallas_call(
        body, name=name,
        out_shape=[jax.ShapeDtypeStruct(dtr.shape, F32), jax.ShapeDtypeStruct((1, SSM_HEADS), F32),
                   jax.ShapeDtypeStruct((1, SSM_HEADS), F32)],
        compiler_params=_params())(dtr, dt_bias, a_log, dt, ddt, da)


GROUP_COLS = HEADS_PER_GROUP * SSM_HEAD_DIM
LANES = 128
HEADS_PER_LANE_BLOCK = LANES // SSM_HEAD_DIM


def _split3(x):
    hi = x.astype(BF16)
    r1 = x - hi.astype(F32)
    mid = r1.astype(BF16)
    lo = (r1 - mid.astype(F32)).astype(BF16)
    return hi, mid, lo


def _dot_select(a, b, dims=NN, data=0):
    out = None
    for part in _split3(a if data == 0 else b):
        lhs, rhs = (part, b.astype(BF16)) if data == 0 else (a.astype(BF16), part)
        t = lax.dot_general(lhs, rhs, dims, preferred_element_type=F32)
        out = t if out is None else out + t
    return out


def _group_sums(vals, expand):
    out = _dot_select(jnp.concatenate(vals, axis=0), expand, NT)
    return [out[i * CHUNK:(i + 1) * CHUNK] for i in range(len(vals))]


def _ssd_chunk_common(a_ref, dt_ref, b_ref, c_ref):
    row = lax.broadcasted_iota(jnp.int32, (CHUNK, CHUNK), 0)
    col = lax.broadcasted_iota(jnp.int32, (CHUNK, CHUNK), 1)
    causal = col <= row
    lower = causal.astype(F32)
    upper = (col >= row).astype(F32)
    head = lax.broadcasted_iota(jnp.int32, (HEADS_PER_GROUP, GROUP_COLS), 0)
    lane = lax.broadcasted_iota(jnp.int32, (HEADS_PER_GROUP, GROUP_COLS), 1)
    expand = ((lane >= head * SSM_HEAD_DIM) & (lane < (head + 1) * SSM_HEAD_DIM)).astype(F32)
    a = a_ref[...]
    cs = _dot_select(lower, a, data=1)
    cs_row = _dot_select(a, upper, TN)
    cs_x = _dot_select(cs, expand)
    dt_x = _dot_select(dt_ref[...], expand)
    e_out_x = jnp.exp(cs_x)
    e_st_x = jnp.exp(cs_x[CHUNK - 1:CHUNK, :] - cs_x)
    bc = b_ref[...]
    cc = c_ref[...]
    cb = _dot(cc, bc, NT)
    return causal, upper, expand.astype(BF16), cs, cs_row, dt_x, e_out_x, e_st_x, bc, cc, cb


def _head_decay(causal, cs, cs_row, h):
    return jnp.exp(jnp.where(causal, cs[:, h:h + 1] - cs_row[h:h + 1, :], NEG_BIG))


def _lane_block_head_masks():
    lane = lax.broadcasted_iota(jnp.int32, (CHUNK, LANES), 1)
    return [(lane >= i * SSM_HEAD_DIM) & (lane < (i + 1) * SSM_HEAD_DIM) for i in range(HEADS_PER_LANE_BLOCK)]


def _decay_state(dst_ref, old, new, cs):
    for h in range(HEADS_PER_GROUP):
        rows = slice(h * SSM_HEAD_DIM, (h + 1) * SSM_HEAD_DIM)
        dst_ref[rows, :] = jnp.exp(cs[CHUNK - 1:CHUNK, h:h + 1]) * old[rows, :] + new[rows, :]


def _ssd_fwd(xbc, dtg, ag, dgx, *, name, comm=None):
    t_dim = xbc.shape[0]

    def body(xs_ref, b_ref, c_ref, dt_ref, a_ref, d_ref, y_ref, st_ref, s_scr):
        @pl.when(pl.program_id(1) == 0)
        def _():
            s_scr[...] = jnp.zeros_like(s_scr)

        causal, _, _, cs, cs_row, dt_x, e_out_x, e_st_x, bc, cc, cb = _ssd_chunk_common(a_ref, dt_ref, b_ref, c_ref)
        masks = _lane_block_head_masks()
        xs = xs_ref[...]
        xdt_x = xs * dt_x
        prev = s_scr[...]
        st_ref[...] = prev
        y_off = e_out_x * _dot(cc, prev, NT) + xs * d_ref[...]
        for blk in range(GROUP_COLS // LANES):
            lanes = slice(blk * LANES, (blk + 1) * LANES)
            x_b = xdt_x[:, lanes].astype(BF16)
            acc = y_off[:, lanes]
            for i in range(HEADS_PER_LANE_BLOCK):
                m = cb * _head_decay(causal, cs, cs_row, blk * HEADS_PER_LANE_BLOCK + i)
                acc = acc + _dot(m, jnp.where(masks[i], x_b, jnp.zeros_like(x_b)))
            y_ref[:, lanes] = acc
        _decay_state(s_scr, prev, _dot(xdt_x * e_st_x, bc, TN), cs)

    xs = pl.BlockSpec((CHUNK, GROUP_COLS), lambda g, c: (c, g))
    bsp = pl.BlockSpec((CHUNK, SSM_STATE), lambda g, c: (c, D_INNER // SSM_STATE + g))
    csp = pl.BlockSpec((CHUNK, SSM_STATE), lambda g, c: (c, (D_INNER + GN) // SSM_STATE + g))
    per_head = pl.BlockSpec((None, CHUNK, HEADS_PER_GROUP), lambda g, c: (g, c, 0))
    dsk = pl.BlockSpec((None, 1, GROUP_COLS), lambda g, c: (g, 0, 0))
    return _call(
        body, name=name, grid=(SSM_GROUPS, N_CHUNKS),
        in_specs=[xs, bsp, csp, per_head, per_head, dsk],
        out_specs=[xs, pl.BlockSpec((None, GROUP_COLS, SSM_STATE), lambda g, c: (c, g, 0))],
        out_shape=[jax.ShapeDtypeStruct((t_dim, D_INNER), F32),
                   jax.ShapeDtypeStruct((N_CHUNKS, D_INNER, SSM_STATE), F32)],
        scratch_shapes=[pltpu.VMEM((GROUP_COLS, SSM_STATE), F32)],
        sem=("parallel", "arbitrary"), args=[xbc, xbc, xbc, dtg, ag, dgx], comm=comm)


def _ssd_bwd(xbc, dtg, ag, dgx, states, dy, *, name, comm=None):
    t_dim = xbc.shape[0]
    last = N_CHUNKS - 1

    def body(xs_ref, b_ref, c_ref, dt_ref, a_ref, d_ref, st_ref, dy_ref,
             dxs_ref, db_ref, dc_ref, ddt_ref, da_ref, dd_ref, ds_scr):
        @pl.when(pl.program_id(1) == 0)
        def _():
            ds_scr[...] = jnp.zeros_like(ds_scr)
            dd_ref[...] = jnp.zeros_like(dd_ref)

        causal, upper, expand, cs, cs_row, dt_x, e_out_x, e_st_x, bc, cc, cb = _ssd_chunk_common(a_ref, dt_ref, b_ref, c_ref)
        masks = _lane_block_head_masks()
        xs = xs_ref[...]
        dy_x = dy_ref[...]
        xdt_x = xs * dt_x
        prev = st_ref[...]
        d_s = ds_scr[...]
        g1_x = _dot(bc, d_s, NT)
        cp_x = _dot(cc, prev, NT)
        d_cb = jnp.zeros((CHUNK, CHUNK), F32)
        lane8 = lax.broadcasted_iota(jnp.int32, (CHUNK, HEADS_PER_GROUP), 1)
        sub8 = lax.broadcasted_iota(jnp.int32, (HEADS_PER_GROUP, CHUNK), 0)
        row_w = jnp.zeros((CHUNK, HEADS_PER_GROUP), F32)
        col_w = jnp.zeros((HEADS_PER_GROUP, CHUNK), F32)
        dxdt_blocks = []
        for blk in range(GROUP_COLS // LANES):
            lanes = slice(blk * LANES, (blk + 1) * LANES)
            dy_b = dy_x[:, lanes].astype(BF16)
            x_b = xdt_x[:, lanes].astype(BF16)
            acc_dx = jnp.zeros((CHUNK, LANES), F32)
            for i in range(HEADS_PER_LANE_BLOCK):
                h = blk * HEADS_PER_LANE_BLOCK + i
                decay = _head_decay(causal, cs, cs_row, h)
                m = cb * decay
                dy_h = jnp.where(masks[i], dy_b, jnp.zeros_like(dy_b))
                acc_dx = acc_dx + _dot(m, dy_h, TN)
                d_m = _dot(dy_h, x_b, NT)
                d_cb = d_cb + d_m * decay
                w = d_m * m
                row_w = jnp.where(lane8 == h, jnp.sum(w, axis=1, keepdims=True), row_w)
                col_w = jnp.where(sub8 == h, jnp.sum(w, axis=0, keepdims=True), col_w)
            dxdt_blocks.append(acc_dx)
        dxdt_x = jnp.concatenate(dxdt_blocks, axis=1) + e_st_x * g1_x
        dxs_ref[...] = dxdt_x * dt_x + dy_x * d_ref[...]
        dye = dy_x * e_out_x
        xde = xdt_x * e_st_x
        ddt, y_off, tl, dskip = _group_sums([dxdt_x * xs, dye * cp_x, xde * g1_x, dy_x * xs], expand)
        ddt_ref[...] = ddt
        dd_ref[...] += jnp.sum(dskip, axis=0, keepdims=True)
        sp = None
        for part in _split3(d_s * prev):
            t = lax.dot_general(expand, part, NN, preferred_element_type=F32)
            sp = t if sp is None else sp + t
        last_col = jnp.exp(cs_row[:, CHUNK - 1:CHUNK]) * jnp.sum(sp, axis=1, keepdims=True)
        eye = lax.broadcasted_iota(jnp.int32, (HEADS_PER_GROUP, HEADS_PER_GROUP), 0) == lax.broadcasted_iota(
            jnp.int32, (HEADS_PER_GROUP, HEADS_PER_GROUP), 1)
        last_row = jnp.sum(jnp.where(eye, last_col, 0.0), axis=0, keepdims=True) + jnp.sum(tl, axis=0, keepdims=True)
        is_last = lax.broadcasted_iota(jnp.int32, (CHUNK, 1), 0) == CHUNK - 1
        d_cs = row_w + y_off - tl + jnp.where(is_last, last_row, 0.0)
        da_ref[...] = _dot_select(upper, d_cs, data=1) - _dot_select(upper, col_w, NT, data=1)
        dc_ref[...] = _dot(d_cb, bc) + _dot(dye, prev)
        db_ref[...] = _dot(d_cb, cc, TN) + _dot(xde, d_s)
        _decay_state(ds_scr, d_s, _dot(dye, cc, TN), cs)

    rev = lambda c: last - c
    xs = pl.BlockSpec((CHUNK, GROUP_COLS), lambda g, c: (rev(c), g))
    bsp = pl.BlockSpec((CHUNK, SSM_STATE), lambda g, c: (rev(c), D_INNER // SSM_STATE + g))
    csp = pl.BlockSpec((CHUNK, SSM_STATE), lambda g, c: (rev(c), (D_INNER + GN) // SSM_STATE + g))
    per_head = pl.BlockSpec((None, CHUNK, HEADS_PER_GROUP), lambda g, c: (g, rev(c), 0))
    dsk = pl.BlockSpec((None, 1, GROUP_COLS), lambda g, c: (g, 0, 0))
    dsum = pl.BlockSpec((None, 1, HEADS_PER_GROUP), lambda g, c: (g, 0, 0))
    st = pl.BlockSpec((None, GROUP_COLS, SSM_STATE), lambda g, c: (rev(c), g, 0))
    grp = pl.BlockSpec((CHUNK, SSM_STATE), lambda g, c: (rev(c), g))
    return _call(
        body, name=name, grid=(SSM_GROUPS, N_CHUNKS),
        in_specs=[xs, bsp, csp, per_head, per_head, dsk, st, xs],
        out_specs=[xs, grp, grp, per_head, per_head, dsum],
        out_shape=[jax.ShapeDtypeStruct((t_dim, D_INNER), F32), jax.ShapeDtypeStruct((t_dim, GN), F32),
                   jax.ShapeDtypeStruct((t_dim, GN), F32),
                   jax.ShapeDtypeStruct((SSM_GROUPS, t_dim, HEADS_PER_GROUP), F32),
                   jax.ShapeDtypeStruct((SSM_GROUPS, t_dim, HEADS_PER_GROUP), F32),
                   jax.ShapeDtypeStruct((SSM_GROUPS, 1, HEADS_PER_GROUP), F32)],
        scratch_shapes=[pltpu.VMEM((GROUP_COLS, SSM_STATE), F32)],
        sem=("parallel", "arbitrary"), args=[xbc, xbc, xbc, dtg, ag, dgx, states, dy], comm=comm)


NORM_GROUP = D_INNER // SSM_GROUPS


def _gate_norm_fwd(y, zx, nw, *, name, tm=256):
    t_dim = y.shape[0]
    row = pl.BlockSpec((tm, D_INNER), lambda i: (i, 0))

    def body(y_ref, z_ref, nw_ref, o_ref):
        z = z_ref[...]
        yz = y_ref[...] * (z * _sigmoid(z))
        for g in range(SSM_GROUPS):
            cols = slice(g * NORM_GROUP, (g + 1) * NORM_GROUP)
            yhat, _ = _rms(yz[:, cols])
            o_ref[:, cols] = (yhat * nw_ref[:, cols]).astype(BF16)

    return pl.pallas_call(
        body, name=name, grid=(t_dim // tm,), in_specs=[row, row, pl.BlockSpec((1, D_INNER), lambda i: (0, 0))],
        out_specs=row, out_shape=jax.ShapeDtypeStruct((t_dim, D_INNER), BF16),
        compiler_params=_params("parallel"),
    )(y, zx, nw)


def _gate_norm_bwd(y, zx, nw, dyn, *, name, tm=256):
    t_dim = y.shape[0]
    row = pl.BlockSpec((tm, D_INNER), lambda i: (i, 0))
    vec = pl.BlockSpec((1, D_INNER), lambda i: (0, 0))

    def body(y_ref, z_ref, nw_ref, dyn_ref, dy_ref, dz_ref, dnw_ref):
        @pl.when(pl.program_id(0) == 0)
        def _():
            dnw_ref[...] = jnp.zeros_like(dnw_ref)

        z = z_ref[...]
        yv = y_ref[...]
        sg = _sigmoid(z)
        silu_z = z * sg
        yz = yv * silu_z
        dyn_v = dyn_ref[...]
        for g in range(SSM_GROUPS):
            cols = slice(g * NORM_GROUP, (g + 1) * NORM_GROUP)
            yhat, r = _rms(yz[:, cols])
            dn = dyn_v[:, cols]
            dnw_ref[:, cols] += jnp.sum(dn * yhat, axis=0, keepdims=True)
            dyhat = dn * nw_ref[:, cols]
            dyz = r * (dyhat - yhat * jnp.mean(dyhat * yhat, axis=-1, keepdims=True))
            dy_ref[:, cols] = dyz * silu_z[:, cols]
            dz_ref[:, cols] = dyz * yv[:, cols] * _dsilu(z[:, cols], sg[:, cols])

    return pl.pallas_call(
        body, name=name, grid=(t_dim // tm,), in_specs=[row, row, vec, row],
        out_specs=[row, row, vec],
        out_shape=[jax.ShapeDtypeStruct((t_dim, D_INNER), F32), jax.ShapeDtypeStruct((t_dim, ZX_DIM), F32),
                   jax.ShapeDtypeStruct((1, D_INNER), F32)],
        compiler_params=_params("arbitrary"),
    )(y, zx, nw, dyn)


HEADS_PER_LANE_TILE = LANES // ATT_HEAD_DIM
STACKED_ROWS = Q_PER_KV * WINDOW


def _att_half_masks():
    lane = lax.broadcasted_iota(jnp.int32, (WINDOW, LANES), 1)
    return [(lane >= i * ATT_HEAD_DIM) & (lane < (i + 1) * ATT_HEAD_DIM) for i in range(HEADS_PER_LANE_TILE)]


def _att_stack_heads(ref, kvh, masks):
    parts = []
    for g in range(Q_PER_KV):
        h = kvh * Q_PER_KV + g
        blk = ref[:, (h // HEADS_PER_LANE_TILE) * LANES:(h // HEADS_PER_LANE_TILE + 1) * LANES]
        parts.append(jnp.where(masks[h % HEADS_PER_LANE_TILE], blk, jnp.zeros_like(blk)))
    return jnp.concatenate(parts, axis=0)


def _att_kv_tile(ref, kvh, masks):
    blk = ref[:, (kvh // HEADS_PER_LANE_TILE) * LANES:(kvh // HEADS_PER_LANE_TILE + 1) * LANES]
    return jnp.where(masks[kvh % HEADS_PER_LANE_TILE], blk, pltpu.roll(blk, ATT_HEAD_DIM, 1)).astype(BF16)


def _att_stacked_masks(n):
    row = lax.bitwise_and(lax.broadcasted_iota(jnp.int32, (STACKED_ROWS, WINDOW), 0), WINDOW - 1)
    col = lax.broadcasted_iota(jnp.int32, (STACKED_ROWS, WINDOW), 1)
    return col <= row, (col > row) & (n > 0)


def _att_stack_columns(ref, kvh, rows):
    cols = [ref[:, kvh * Q_PER_KV + g:kvh * Q_PER_KV + g + 1] for g in range(Q_PER_KV)]
    return jnp.concatenate([jnp.broadcast_to(c, (rows, 1)) for c in cols], axis=0)


def _att_scores(q4, k_tile, mask):
    return jnp.where(mask, _dot(q4, k_tile, NT) * ATT_SCALE, NEG_BIG)


def _att_unstack(x4, kvh, masks, tiles):
    for g in range(Q_PER_KV):
        h = kvh * Q_PER_KV + g
        piece = x4[g * WINDOW:(g + 1) * WINDOW]
        t = h // HEADS_PER_LANE_TILE
        tiles[t] = piece if h % HEADS_PER_LANE_TILE == 0 else jnp.where(masks[1], piece, tiles[t])


def _attn_fwd(q, k, v, sinks, *, name, comm=None):
    t_dim = q.shape[0]

    def body(q_ref, kc_ref, kp_ref, vc_ref, vp_ref, s_ref, o_ref, l_ref):
        n = pl.program_id(0)
        masks = _att_half_masks()
        mask_c, mask_p = _att_stacked_masks(n)
        out_tiles = [None] * (D_MODEL // LANES)
        for kvh in range(N_KV_HEADS):
            q4 = _att_stack_heads(q_ref, kvh, masks).astype(BF16)
            kc, kp = _att_kv_tile(kc_ref, kvh, masks), _att_kv_tile(kp_ref, kvh, masks)
            vc, vp = _att_kv_tile(vc_ref, kvh, masks), _att_kv_tile(vp_ref, kvh, masks)
            sc = _att_scores(q4, kc, mask_c)
            sp = _att_scores(q4, kp, mask_p)
            sink = _att_stack_columns(s_ref, kvh, WINDOW)
            m = jnp.maximum(jnp.maximum(jnp.max(sc, axis=1, keepdims=True), jnp.max(sp, axis=1, keepdims=True)), sink)
            pc = jnp.exp(sc - m)
            pp = jnp.exp(sp - m)
            den = jnp.sum(pc, axis=1, keepdims=True) + jnp.sum(pp, axis=1, keepdims=True) + jnp.exp(sink - m)
            _att_unstack((_dot(pc, vc) + _dot(pp, vp)) / den, kvh, masks, out_tiles)
            lse4 = m + jnp.log(den)
            for g in range(Q_PER_KV):
                h = kvh * Q_PER_KV + g
                l_ref[:, h:h + 1] = lse4[g * WINDOW:(g + 1) * WINDOW]
        for t, tile in enumerate(out_tiles):
            o_ref[:, t * LANES:(t + 1) * LANES] = tile

    cur = lambda w: pl.BlockSpec((WINDOW, w), lambda n: (n, 0))
    prv = lambda w: pl.BlockSpec((WINDOW, w), lambda n: (jnp.maximum(n - 1, 0), 0))
    return _call(
        body, name=name, grid=(t_dim // WINDOW,),
        in_specs=[cur(D_MODEL), cur(KV_DIM), prv(KV_DIM), cur(KV_DIM), prv(KV_DIM), pl.BlockSpec((1, N_Q_HEADS), lambda n: (0, 0))],
        out_specs=[cur(D_MODEL), cur(N_Q_HEADS)],
        out_shape=[jax.ShapeDtypeStruct((t_dim, D_MODEL), F32), jax.ShapeDtypeStruct((t_dim, N_Q_HEADS), F32)],
        sem=("parallel",), args=[q, k, k, v, v, sinks], comm=comm)


def _attn_bwd(q, k, v, sinks, o, lse, do, cos2, sin2, *, name, comm=None):
    t_dim = q.shape[0]

    def body(q_ref, kc_ref, kp_ref, vc_ref, vp_ref, s_ref, o_ref, l_ref, do_ref, cos_ref, sin_ref, cos_all_ref, sin_all_ref,
             dq_ref, dk_ref, dv_ref, dsink_ref):
        n = pl.program_id(0)

        @pl.when(n == 0)
        def _():
            dk_ref[...] = jnp.zeros_like(dk_ref)
            dv_ref[...] = jnp.zeros_like(dv_ref)
            dsink_ref[...] = jnp.zeros_like(dsink_ref)

        masks = _att_half_masks()
        mask_c, mask_p = _att_stacked_masks(n)
        lane_row = lax.broadcasted_iota(jnp.int32, (1, N_Q_HEADS), 1)
        rows_c = pl.ds(pl.multiple_of(n * WINDOW, WINDOW), WINDOW)
        rows_p = pl.ds(pl.multiple_of(jnp.maximum(n - 1, 0) * WINDOW, WINDOW), WINDOW)
        dsink = jnp.zeros((1, N_Q_HEADS), F32)
        dq_tiles = [None] * (D_MODEL // LANES)
        kv_tiles = KV_DIM // LANES
        dkc_tiles, dkp_tiles, dvc_tiles, dvp_tiles = ([None] * kv_tiles for _ in range(4))

        def place(tiles, kvh, x):
            folded = x + pltpu.roll(x, ATT_HEAD_DIM, 1)
            t = kvh // HEADS_PER_LANE_TILE
            tiles[t] = folded if kvh % HEADS_PER_LANE_TILE == 0 else jnp.where(masks[1], folded, tiles[t])

        for kvh in range(N_KV_HEADS):
            q4 = _att_stack_heads(q_ref, kvh, masks).astype(BF16)
            do4 = _att_stack_heads(do_ref, kvh, masks)
            o4 = _att_stack_heads(o_ref, kvh, masks)
            kc, kp = _att_kv_tile(kc_ref, kvh, masks), _att_kv_tile(kp_ref, kvh, masks)
            vc, vp = _att_kv_tile(vc_ref, kvh, masks), _att_kv_tile(vp_ref, kvh, masks)
            l4 = _att_stack_columns(l_ref, kvh, WINDOW)
            pc = jnp.exp(_att_scores(q4, kc, mask_c) - l4)
            pp = jnp.exp(_att_scores(q4, kp, mask_p) - l4)
            delta = jnp.sum(do4 * o4, axis=1, keepdims=True)
            do4b = do4.astype(BF16)
            dsc = pc * (_dot(do4b, vc, NT) - delta)
            dsp = pp * (_dot(do4b, vp, NT) - delta)
            _att_unstack((_dot(dsc, kc) + _dot(dsp, kp)) * ATT_SCALE, kvh, masks, dq_tiles)
            place(dkc_tiles, kvh, _dot(dsc, q4, TN) * ATT_SCALE)
            place(dkp_tiles, kvh, _dot(dsp, q4, TN) * ATT_SCALE)
            place(dvc_tiles, kvh, _dot(pc, do4b, TN))
            place(dvp_tiles, kvh, _dot(pp, do4b, TN))
            p_sink = jnp.exp(_att_stack_columns(s_ref, kvh, WINDOW) - l4) * delta
            for g in range(Q_PER_KV):
                h = kvh * Q_PER_KV + g
                dsink = jnp.where(lane_row == h, -jnp.sum(p_sink[g * WINDOW:(g + 1) * WINDOW], axis=0, keepdims=True), dsink)
        for t, tile in enumerate(dq_tiles):
            dq_ref[:, t * LANES:(t + 1) * LANES] = _rope_rotate(tile, cos_ref[...], -sin_ref[...])
        for t in range(kv_tiles):
            lanes = slice(t * LANES, (t + 1) * LANES)
            dk_ref[rows_c, lanes] += dkc_tiles[t]
            dk_ref[rows_p, lanes] += dkp_tiles[t]
            dv_ref[rows_c, lanes] += dvc_tiles[t]
            dv_ref[rows_p, lanes] += dvp_tiles[t]
        dsink_ref[...] += dsink

        @pl.when(n == t_dim // WINDOW - 1)
        def _():
            dk_ref[...] = _rope_rotate(dk_ref[...], cos_all_ref[...], -sin_all_ref[...])

    cur = lambda w: pl.BlockSpec((WINDOW, w), lambda n: (n, 0))
    prv = lambda w: pl.BlockSpec((WINDOW, w), lambda n: (jnp.maximum(n - 1, 0), 0))
    whole = lambda w: pl.BlockSpec((t_dim, w), lambda n: (0, 0))
    svec = pl.BlockSpec((1, N_Q_HEADS), lambda n: (0, 0))
    return _call(
        body, name=name, grid=(t_dim // WINDOW,),
        in_specs=[cur(D_MODEL), cur(KV_DIM), prv(KV_DIM), cur(KV_DIM), prv(KV_DIM), svec, cur(D_MODEL), cur(N_Q_HEADS), cur(D_MODEL),
                  cur(LANES), cur(LANES), whole(LANES), whole(LANES)],
        out_specs=[cur(D_MODEL), whole(KV_DIM), whole(KV_DIM), svec],
        out_shape=[jax.ShapeDtypeStruct((t_dim, D_MODEL), F32), jax.ShapeDtypeStruct((t_dim, KV_DIM), F32),
                   jax.ShapeDtypeStruct((t_dim, KV_DIM), F32), jax.ShapeDtypeStruct((1, N_Q_HEADS), F32)],
        sem=("arbitrary",), args=[q, k, k, v, v, sinks, o, lse, do, cos2, sin2, cos2, sin2], comm=comm)


def _loss_head(x, nw, target, *, name, tm=256):
    t_dim, d_dim = x.shape
    row = pl.BlockSpec((tm, d_dim), lambda i: (i, 0))
    vec = pl.BlockSpec((1, d_dim), lambda i: (0, 0))

    def body(x_ref, nw_ref, tgt_ref, loss_ref, dx_ref, dnw_ref, dob_ref):
        @pl.when(pl.program_id(0) == 0)
        def _():
            loss_ref[...] = jnp.zeros_like(loss_ref)
            dnw_ref[...] = jnp.zeros_like(dnw_ref)

        xhat, r = _rms(x_ref[...])
        err = xhat * nw_ref[...] - tgt_ref[...]
        loss_ref[...] += 0.5 * _sum_all(jnp.mean(err * err, axis=-1, keepdims=True))
        dy = err * (1.0 / d_dim)
        dnw_ref[...] += jnp.sum(dy * xhat, axis=0, keepdims=True)
        dxhat = dy * nw_ref[...]
        dx = r * (dxhat - xhat * jnp.mean(dxhat * xhat, axis=-1, keepdims=True))
        dx_ref[...] = dx
        dob_ref[...] = (FFN_RES_WEIGHT * dx).astype(BF16)

    return pl.pallas_call(
        body, name=name, grid=(t_dim // tm,), in_specs=[row, vec, row],
        out_specs=[pl.BlockSpec((1, 1), lambda i: (0, 0)), row, vec, row],
        out_shape=[jax.ShapeDtypeStruct((1, 1), F32), jax.ShapeDtypeStruct((t_dim, d_dim), F32),
                   jax.ShapeDtypeStruct((1, d_dim), F32), jax.ShapeDtypeStruct((t_dim, d_dim), BF16)],
        compiler_params=_params("arbitrary"),
    )(x, nw, target)


def _rope_tables():
    pos = jnp.arange(SEQ, dtype=F32)
    inv = 1.0 / (ROPE_THETA ** (jnp.arange(0, ATT_HEAD_DIM, 2, dtype=F32) / ATT_HEAD_DIM))
    ang = pos[:, None] * inv[None, :]
    cos, sin = jnp.cos(ang), jnp.sin(ang)
    return jnp.tile(cos, (1, 4)), jnp.tile(sin, (1, 4))


def _to_groups(t):
    return t.reshape(t.shape[0], SSM_GROUPS, HEADS_PER_GROUP).transpose(1, 0, 2)


def _from_groups(t):
    return t.transpose(1, 0, 2).reshape(t.shape[1], SSM_HEADS)


def _forward_backward(x0, target, net):
    w = net.w
    nw = [[w("norm_w")[l, i][None, :] for i in range(3)] for l in range(2)]
    cos2, sin2 = _rope_tables()
    ffn_norm = [nw[0][0], nw[0][2], nw[1][0], nw[1][2]]

    ffn_pre = {}

    def ffn_f(x, blk):
        name = f"ffn_fwd{blk}"
        out, *ffn_pre[blk] = _ffn_fwd(x, ffn_norm[blk], w(f"gate{blk}"), w(f"up{blk}"), w(f"down{blk}"), name=name,
                                      comm=net.carry(name))
        return out

    x1 = ffn_f(x0, 0)
    zx, h1 = _norm_mm(x1, nw[0][1], w("w_in_t"), None, w_rows=ZX_DIM, name="ssm_in_proj", comm=net.carry("ssm_in_proj"))
    dtr = _mm(h1, w("w_in_t"), dims="nt", b_rows=(ZX_DIM, SSM_HEADS), name="ssm_dt_proj")
    xbc = _conv_fwd(zx, w("conv_w"), w("conv_b"), name="ssm_conv_fwd", comm=net.carry("ssm_conv_fwd"))
    dt, a_dt = _dt_prep(dtr, w("dt_bias"), w("a_log"), name="ssm_dt_prep")
    dtg, ag = _to_groups(dt), _to_groups(a_dt)
    dg = jnp.repeat(w("d_skip").reshape(SSM_GROUPS, 1, HEADS_PER_GROUP), SSM_HEAD_DIM, axis=2)
    y_ssd, states = _ssd_fwd(xbc, dtg, ag, dg, name="ssd_fwd", comm=net.carry("ssd_fwd"))
    yn = _gate_norm_fwd(y_ssd, zx, w("ssm_norm_w"), name="ssm_gate_norm_fwd")
    x2 = _mm(yn, w("wout"), res=x1, name="ssm_out_proj", comm=net.carry("ssm_out_proj"))
    x3 = ffn_f(x2, 1)
    k_rot, hk = _norm_mm(x3, w("kv_norm_w"), w("wk"), w("b_k"), rope=(cos2, sin2), name="k_proj")
    v = _mm(hk, w("wv"), bias=w("b_v"), name="v_proj")
    x4 = ffn_f(x3, 2)
    q_rot, h4 = _norm_mm(x4, nw[1][1], w("wq"), w("b_q"), rope=(cos2, sin2), name="q_proj")
    att, lse = _attn_fwd(q_rot, k_rot, v, w("sinks"), name="attn_fwd", comm=net.carry("attn_fwd"))
    x5 = _mm(att, w("wo"), bias=w("b_o"), res=x4, name="attn_out_proj")
    x6 = ffn_f(x5, 3)
    loss, dx6, d_final, dob6 = _loss_head(x6, w("final_norm_w"), target, name="loss_head")

    d_norm = [[None] * 3 for _ in range(2)]

    def ffn_b(x, dout, dob, blk):
        pre_g, pre_u, h = ffn_pre[blk]
        name = f"ffn_bwd{blk}"
        dh, gg, gu, gd = _ffn_bwd(h, dob, pre_g, pre_u, w(f"gate{blk}"), w(f"up{blk}"), w(f"down{blk}"), name=name,
                                  comm=net.carry(name))
        net.give(f"gate{blk}", gg)
        net.give(f"up{blk}", gu)
        net.give(f"down{blk}", gd)
        return _norm_bwd(x, ffn_norm[blk], dh, [dout], name=f"ffn_norm_bwd{blk}", comm=net.carry(f"ffn_norm_bwd{blk}"))

    by_rows = lambda g: g.reshape(N_DEV, g.shape[0] // N_DEV, g.shape[1])
    dx5, d_norm[1][2] = ffn_b(x5, dx6, dob6, 3)
    d_att = _mm(dx5, w("wo"), dims="nt", name="attn_out_proj_dx", comm=net.carry("attn_out_proj_dx"))
    g_o, d_bo = _mm(att, dx5, dims="tn", out_dtype=BF16, colsum_b=True, name="attn_out_proj_dw")
    net.give("w_o", by_rows(g_o))
    dq, dk, dv, d_sinks = _attn_bwd(q_rot, k_rot, v, w("sinks"), att, lse, d_att, cos2, sin2, name="attn_bwd",
                                    comm=net.carry("attn_bwd"))
    dx4, d_norm[1][1], dob4 = _mm_norm_bwd(dq, w("wq"), x4, nw[1][1], [dx5], dims="nt", name="q_proj_dx")
    g_q, d_bq = _mm(h4, dq, dims="tn", out_dtype=BF16, colsum_b=True, name="q_proj_dw")
    net.give("w_q", by_rows(g_q))
    dx3a, d_norm[1][0] = ffn_b(x3, dx4, dob4, 2)
    dhk = _mm(dk, w("wk"), dims="nt", name="k_proj_dx", comm=net.carry("k_proj_dx"))
    dx3, d_kvn, dob3 = _mm_norm_bwd(dv, w("wv"), x3, w("kv_norm_w"), [dx3a], dims="nt", add=dhk, name="v_proj_dx")
    g_k, d_bk = _mm(hk, dk, dims="tn", out_dtype=BF16, colsum_b=True, name="k_proj_dw")
    g_v, d_bv = _mm(hk, dv, dims="tn", out_dtype=BF16, colsum_b=True, name="v_proj_dw")
    net.give("w_k", by_rows(g_k))
    net.give("w_v", by_rows(g_v))
    dx2, d_norm[0][2] = ffn_b(x2, dx3, dob3, 1)
    d_yn = _mm(dx2, w("wout"), dims="nt", name="ssm_out_proj_dx", comm=net.carry("ssm_out_proj_dx"))
    net.give("w_out", by_rows(_mm(yn, dx2, dims="tn", out_dtype=BF16, name="ssm_out_proj_dw")))
    dy_ssd, dzx, d_ssm_norm = _gate_norm_bwd(y_ssd, zx, w("ssm_norm_w"), d_yn, name="ssm_gate_norm_bwd")
    dxs, d_b, d_c, ddtg, dag, ddg = _ssd_bwd(xbc, dtg, ag, dg, states, dy_ssd, name="ssd_bwd", comm=net.carry("ssd_bwd"))
    dzx, d_conv_w, d_conv_b = _conv_bwd(zx, w("conv_w"), w("conv_b"), dxs, d_b, d_c, dzx, name="ssm_conv_bwd",
                                        comm=net.carry("ssm_conv_bwd"))
    ddtr, d_dt_bias, d_a_log = _dt_bwd(dtr, w("dt_bias"), w("a_log"), dt, _from_groups(ddtg), _from_groups(dag), name="ssm_dt_bwd")
    dh1 = _mm(dzx, w("w_in_t"), b_rows=(0, ZX_DIM), name="ssm_in_proj_dx")
    in_rows = N_DEV * IN_PROJ_SHARD
    g_in = _mm(dzx, h1, dims="tn", out_dtype=BF16, out_window=(0, in_rows), name="ssm_in_proj_dw")
    g_in = _mm(ddtr, h1, dims="tn", out_dtype=BF16, out_window=(ZX_DIM, in_rows), into=g_in, name="ssm_dt_proj_dw")
    net.give("w_in", g_in.reshape(N_DEV, IN_PROJ_SHARD, D_MODEL))
    dx1, d_norm[0][1], dob1 = _mm_norm_bwd(ddtr, w("w_in_t"), x1, nw[0][1], [dx2], b_rows=(ZX_DIM, SSM_HEADS), add=dh1,
                                           name="ssm_dt_proj_dx", comm=net.carry("ssm_norm_bwd"))
    dx0, d_norm[0][0] = ffn_b(x0, dx1, dob1, 0)

    small = {"norm_w": jnp.concatenate([d_norm[l][i] for l in range(2) for i in range(3)], axis=0),
             "ssm_conv_w": d_conv_w, "ssm_conv_b": d_conv_b, "ssm_dt_bias": d_dt_bias, "ssm_a_log": d_a_log,
             "ssm_d": ddg.reshape(1, SSM_HEADS), "ssm_norm_w": d_ssm_norm, "kv_norm_w": d_kvn,
             "b_k": d_bk, "b_v": d_bv, "attn_b_q": d_bq, "attn_sinks": d_sinks, "attn_b_o": d_bo, "final_norm_w": d_final}
    return loss, dx0, small


BLOCK_BYTES = 1 << 20


def _row_tile(rows, cols):
    for t in (512, 256, 128, 64, 32, 16):
        if rows % t == 0 and t * cols * 4 <= BLOCK_BYTES:
            return t
    return rows


def _cast_bf16(x, *, name):
    n_blk, rows, cols = x.shape
    tm = rows if rows * cols * 4 <= 2 * BLOCK_BYTES else _row_tile(rows, cols)
    spec = pl.BlockSpec((None, tm, cols), lambda b, i: (b, i, 0))

    def body(x_ref, o_ref):
        o_ref[...] = x_ref[...].astype(BF16)

    return pl.pallas_call(body, name=name, grid=(n_blk, rows // tm), in_specs=[spec], out_specs=spec,
                          out_shape=jax.ShapeDtypeStruct(x.shape, BF16), compiler_params=_params("parallel", "parallel"))(x)


def _pair_add(grad, theirs, *, name):
    n_slots, rows, cols = theirs.shape
    tm = rows if rows * cols * 4 <= 2 * BLOCK_BYTES else _row_tile(rows, cols)

    def body(g_ref, t_ref, o_ref):
        mine = jnp.where(lax.axis_index("c") == 0, g_ref[0].astype(F32), g_ref[1].astype(F32))
        o_ref[...] = (mine + t_ref[...].astype(F32)).astype(BF16)

    spec = pl.BlockSpec((None, tm, cols), lambda s, i: (s, i, 0))
    return pl.pallas_call(
        body, name=name, grid=(n_slots, rows // tm),
        in_specs=[pl.BlockSpec((2, tm, cols), lambda s, i: (s, i, 0)), spec], out_specs=spec,
        out_shape=jax.ShapeDtypeStruct(theirs.shape, BF16), compiler_params=_params("parallel", "parallel"),
    )(grad, theirs)


def _adam_update(g, w, m, v):
    m = ADAM_B1 * m + (1.0 - ADAM_B1) * g
    v = ADAM_B2 * v + (1.0 - ADAM_B2) * (g * g)
    m_hat = m / (1.0 - ADAM_B1 ** ADAM_STEP)
    v_hat = v / (1.0 - ADAM_B2 ** ADAM_STEP)
    delta = -ADAM_LR * (m_hat / (jnp.sqrt(v_hat) + ADAM_EPS) + ADAM_WD * w)
    return delta, m, v


def _adamw(parts, w, m, v, first_blk, prev, *, name, comm=None):
    n_blk, rows, cols = w.shape
    tm = _row_tile(rows, cols)
    n_tiles = rows // tm
    spec = pl.BlockSpec((None, tm, cols), lambda b, i: (first_blk + b, i, 0))
    n_prev, n_here = len(prev), len(parts)
    n_parts = parts[0].shape[0]

    def part_spec(q):
        return pl.BlockSpec((n_parts, tm, cols), lambda b, i: (0, jnp.where(b < q, 0, jnp.where(b == q, i, n_tiles - 1)), 0))

    def body(*refs):
        p_refs = refs[:n_here]
        w_ref, m_ref, v_ref = refs[n_here:n_here + 3]
        g_ref, d_ref, nm_ref, nv_ref = refs[n_here + 3 + n_prev:]
        b = pl.program_id(0)
        g = None
        for s in range(n_parts):
            t = p_refs[0][s]
            for q in range(1, n_here):
                t = jnp.where(b == q, p_refs[q][s], t)
            g = t.astype(F32) if g is None else g + t.astype(F32)
        delta, nm, nv = _adam_update(g, w_ref[...], m_ref[...], v_ref[...])
        g_ref[...] = g
        d_ref[...] = delta
        nm_ref[...] = nm
        nv_ref[...] = nv

    return _call(
        body, name=name, grid=(n_here, n_tiles),
        in_specs=[part_spec(q) for q in range(n_here)] + [spec, spec, spec] + [pl.BlockSpec(memory_space=pl.ANY)] * n_prev,
        out_specs=[spec] * 4, out_shape=[jax.ShapeDtypeStruct((n_blk, rows, cols), F32)] * 4,
        aliases={n_here + 3 + q: q for q in range(n_prev)}, sem=("arbitrary", "arbitrary"),
        args=[*parts, w, m, v, *prev], comm=comm)


def _sum_parts(parts, *, name):
    def body(p_ref, o_ref):
        g = p_ref[0]
        for s in range(1, N_DEV):
            g = g + p_ref[s]
        o_ref[...] = g

    return pl.pallas_call(body, name=name, out_shape=jax.ShapeDtypeStruct(parts.shape[1:], F32), compiler_params=_params())(parts)


def _adamw_packed(g, w, m, v, *, name):
    def body(g_ref, w_ref, m_ref, v_ref, d_ref, nm_ref, nv_ref):
        delta, nm, nv = _adam_update(g_ref[...], w_ref[...], m_ref[...], v_ref[...])
        d_ref[...] = delta
        nm_ref[...] = nm
        nv_ref[...] = nv

    return pl.pallas_call(body, name=name, out_shape=[jax.ShapeDtypeStruct(g.shape, F32)] * 3, compiler_params=_params())(g, w, m, v)


SUBLANES = 8


WIDE_PACK = 1024


def _pack(arrs, width=LANES):
    rows = []
    for a in arrs:
        a2 = a.reshape(-1, a.shape[-1])
        a2 = jnp.pad(a2, ((0, 0), (0, (-a2.shape[1]) % width)))
        rows += [a2[:, i * width:(i + 1) * width] for i in range(a2.shape[1] // width)]
    out = jnp.concatenate(rows, axis=0)
    return jnp.pad(out, ((0, (-out.shape[0]) % SUBLANES), (0, 0)))


def _unpack(packed, shapes, width=LANES):
    outs, r = [], 0
    for shp in shapes:
        lead, cols = math.prod(shp[:-1]), shp[-1]
        n_blocks = -(-cols // width)
        blocks = [packed[r + i * lead:r + (i + 1) * lead] for i in range(n_blocks)]
        outs.append(jnp.concatenate(blocks, axis=1)[:, :cols].reshape(shp))
        r += n_blocks * lead
    return outs


WEIGHT_NAMES = ("norm_w", "ffn_w_gate", "ffn_w_up", "ffn_w_down", "ssm_w_in", "ssm_conv_w", "ssm_conv_b", "ssm_dt_bias",
                "ssm_a_log", "ssm_d", "ssm_norm_w", "ssm_w_out", "kv_norm_w", "w_k", "b_k", "w_v", "b_v", "attn_w_q",
                "attn_b_q", "attn_sinks", "attn_w_o", "attn_b_o", "final_norm_w")
MATRIX_NAMES = ("ffn_w_gate", "ffn_w_up", "ffn_w_down", "ssm_w_in", "ssm_w_out", "w_k", "w_v", "attn_w_q", "attn_w_o")
VECTOR_NAMES = tuple(n for n in WEIGHT_NAMES if n not in MATRIX_NAMES)
SHARDED_VECTORS = ("norm_w", "ssm_conv_w", "ssm_conv_b", "ssm_norm_w")


GATHER_PLAN = {
    "gather_stage0": ("gate0", "up0", "down0", "vec"),
    "ffn_fwd0": ("w_in",),
    "ssm_in_proj": ("w_out", "gate1"),
    "ssm_conv_fwd": ("w_k", "w_v", "up1"),
    "ssd_fwd": ("down1", "gate2", "up2"),
    "ssm_out_proj": ("w_q", "w_o"),
    "ffn_fwd1": ("down2", "gate3"),
    "ffn_fwd2": ("up3",),
    "attn_fwd": ("down3",),
}
PAIR_PLAN = {
    "attn_out_proj_dx": ("gate3", "up3", "down3"),
    "ffn_bwd2": ("w_q", "w_o"),
    "k_proj_dx": ("gate2", "up2", "down2"),
    "ssm_out_proj_dx": ("w_k", "w_v", "gate1", "up1", "down1"),
    "ssd_bwd": ("w_out",),
    "ssm_norm_bwd": ("w_in",),
    "ffn_norm_bwd0": ("gate0", "up0", "down0"),
}
CHIP_PLAN = {
    "attn_bwd": ("gate3", "up3"),
    "ffn_bwd2": ("down3",),
    "ffn_bwd1": ("gate2", "up2", "w_q", "w_o"),
    "ssd_bwd": ("down2", "gate1", "up1", "down1", "w_k", "w_v"),
    "ssm_conv_bwd": ("w_out",),
    "ffn_bwd0": ("w_in",),
    "adamw_gate": ("gate0",),
    "adamw_up": ("up0",),
    "adamw_down": ("down0",),
}
FFN_PARAMS = {"gate": "ffn_w_gate", "up": "ffn_w_up", "down": "ffn_w_down"}
SINGLE_MATRICES = {"w_in": "ssm_w_in", "w_out": "ssm_w_out", "w_k": "w_k", "w_v": "w_v", "w_q": "attn_w_q", "w_o": "attn_w_o"}


TRANSPOSED = ("ffn_w_gate", "ffn_w_up", "ssm_w_in")


def _matrix_view(name, a):
    if name in TRANSPOSED:
        a = jnp.swapaxes(a, -1, -2)
    return a.reshape((-1,) + a.shape[-2:])


def _from_matrix_view(name, a, shape):
    if name in TRANSPOSED:
        return jnp.swapaxes(a.reshape(shape[:-2] + (shape[-1], shape[-2])), -1, -2)
    return a.reshape(shape)


class _MeshNet:
    def __init__(self, p):
        self.p = p
        self.views = {n: _matrix_view(n, p[n]) for n in MATRIX_NAMES}
        self.local = {"vec": _pack([p[n] for n in SHARDED_VECTORS])}
        for short, n in FFN_PARAMS.items():
            cast = _cast_bf16(self.views[n], name=f"cast_{short}")
            self.local.update({f"{short}{k}": (cast, k) for k in range(N_FFN)})
        for short, n in SINGLE_MATRICES.items():
            self.local[short] = (_cast_bf16(self.views[n], name=f"cast_{short}"), 0)
        self.gathered_at, self.pairs_at, self.parts_at, self.grads, self.cache = {}, {}, {}, {}, {}

    def carry(self, name):
        comms = []
        if name in GATHER_PLAN:
            keys, comm = GATHER_PLAN[name], _Gather([self.local[k] for k in GATHER_PLAN[name]])
            self.gathered_at.update({k: (comm, i) for i, k in enumerate(keys)})
            comms.append(comm)
        if name in CHIP_PLAN:
            sums = []
            for k in CHIP_PLAN[name]:
                comm, i = self.pairs_at[k]
                sums.append(_pair_add(self.grads[k], comm.results[i], name=f"pair_add_{k}"))
            comm = _ChipExchange(sums)
            self.parts_at.update({k: (comm, i) for i, k in enumerate(CHIP_PLAN[name])})
            comms.append(comm)
        if name in PAIR_PLAN:
            keys, comm = PAIR_PLAN[name], _PairSwap([self.grads[k] for k in PAIR_PLAN[name]])
            self.pairs_at.update({k: (comm, i) for i, k in enumerate(keys)})
            comms.append(comm)
        return comms

    def run(self, name):
        for comm in self.carry(name):
            _run_exchange(comm, name=name)

    def give(self, key, grad):
        self.grads[key] = grad

    def parts(self, key):
        comm, i = self.parts_at[key]
        return comm.results[i]

    def _gathered(self, key):
        comm, i = self.gathered_at[key]
        return comm.results[i]

    def _vec(self, r0, lead, n_blocks):
        vecs = self._gathered("vec")
        return jnp.concatenate([vecs[d, r0 + i * lead:r0 + (i + 1) * lead, :] for d in range(N_DEV) for i in range(n_blocks)], axis=1)

    def _derive(self, name):
        p = self.p
        if name[:-1] in FFN_PARAMS:
            return self._gathered(name)
        if name == "w_in_t":
            return self._gathered("w_in").reshape(N_DEV * IN_PROJ_SHARD, D_MODEL)
        by_rows = {"wout": "w_out", "wk": "w_k", "wv": "w_v", "wq": "w_q", "wo": "w_o"}
        if name in by_rows:
            g = self._gathered(by_rows[name])
            return g.reshape(N_DEV * g.shape[1], g.shape[2])
        vectors = {"norm_w": lambda: self._vec(0, 6, 1).reshape(2, 3, D_MODEL), "conv_w": lambda: self._vec(6, CONV_WIDTH, 3),
                   "conv_b": lambda: self._vec(18, 1, 3), "ssm_norm_w": lambda: self._vec(21, 1, 2)}
        if name in vectors:
            return vectors[name]()
        replicated = {"dt_bias": p["ssm_dt_bias"], "a_log": p["ssm_a_log"], "d_skip": p["ssm_d"], "kv_norm_w": p["kv_norm_w"][None],
                      "b_k": p["b_k"][None], "b_v": p["b_v"][None], "b_q": p["attn_b_q"], "sinks": p["attn_sinks"],
                      "b_o": p["attn_b_o"], "final_norm_w": p["final_norm_w"][None]}
        return replicated[name]

    def w(self, name):
        if name not in self.cache:
            self.cache[name] = self._derive(name)
        return self.cache[name]


def _step(x, target, p, m, v):
    pos = _slot(_position())
    net = _MeshNet(p)
    net.run("gather_stage0")
    loss, grad_x, small = _forward_backward(x, target, net)

    grads, deltas, new_m, new_v = {}, {}, {}, {}
    view = lambda d, n: _matrix_view(n, d[n])
    vec_gather = _Gather([_pack([small[n] for n in VECTOR_NAMES], WIDE_PACK)])
    for short, n in SINGLE_MATRICES.items():
        outs = _adamw([net.parts(short)], net.views[n], view(m, n), view(v, n), 0, [], name=f"adamw_{short}",
                      comm=[vec_gather] if short == "w_in" else None)
        grads[n], deltas[n], new_m[n], new_v[n] = [_from_matrix_view(n, o, p[n].shape) for o in outs]
    ffn_outs = {}
    for short, n in FFN_PARAMS.items():
        ffn_outs[short] = _adamw([net.parts(f"{short}{k}") for k in range(1, N_FFN)], net.views[n], view(m, n), view(v, n), 1, [],
                                 name=f"adamw_{short}", comm=net.carry(f"adamw_{short}"))
    for short, n in FFN_PARAMS.items():
        outs = _adamw([net.parts(f"{short}0")], net.views[n], view(m, n), view(v, n), 0, ffn_outs[short], name=f"adamw_{short}0")
        grads[n], deltas[n], new_m[n], new_v[n] = [_from_matrix_view(n, o, p[n].shape) for o in outs]
    vec_sum = _sum_parts(vec_gather.results[0], name="sum_vector_grads")
    full_shapes = {"norm_w": (2, 3, D_MODEL), "ssm_conv_w": (1, CONV_WIDTH, CONV_DIM), "ssm_conv_b": (1, CONV_DIM),
                   "ssm_norm_w": (1, D_INNER)}
    vec_full = dict(zip(VECTOR_NAMES, _unpack(vec_sum, [full_shapes.get(n, p[n].shape) for n in VECTOR_NAMES], WIDE_PACK)))
    for n in VECTOR_NAMES:
        g = vec_full[n]
        if n in SHARDED_VECTORS:
            per = p[n].shape[-1]
            g = lax.dynamic_slice_in_dim(g, pos * per, per, axis=g.ndim - 1)
        grads[n] = g
    packed = _adamw_packed(*[_pack([d[n] for n in VECTOR_NAMES]) for d in (grads, p, m, v)], name="adamw_vectors")
    shapes = [p[n].shape for n in VECTOR_NAMES]
    for d, pk in zip((deltas, new_m, new_v), packed):
        d.update(zip(VECTOR_NAMES, _unpack(pk, shapes)))
    return loss, grad_x, grads, deltas, new_m, new_v


def kernel(x, norm_w, ffn_w_gate, ffn_w_up, ffn_w_down, ssm_w_in, ssm_conv_w, ssm_conv_b, ssm_dt_bias, ssm_a_log, ssm_d, ssm_norm_w, ssm_w_out, kv_norm_w, w_k, b_k, w_v, b_v, attn_w_q, attn_b_q, attn_sinks, attn_w_o, attn_b_o, final_norm_w, loss_target, m_norm_w, m_ffn_w_gate, m_ffn_w_up, m_ffn_w_down, m_ssm_w_in, m_ssm_conv_w, m_ssm_conv_b, m_ssm_dt_bias, m_ssm_a_log, m_ssm_d, m_ssm_norm_w, m_ssm_w_out, m_kv_norm_w, m_w_k, m_b_k, m_w_v, m_b_v, m_attn_w_q, m_attn_b_q, m_attn_sinks, m_attn_w_o, m_attn_b_o, m_final_norm_w, v_norm_w, v_ffn_w_gate, v_ffn_w_up, v_ffn_w_down, v_ssm_w_in, v_ssm_conv_w, v_ssm_conv_b, v_ssm_dt_bias, v_ssm_a_log, v_ssm_d, v_ssm_norm_w, v_ssm_w_out, v_kv_norm_w, v_w_k, v_b_k, v_w_v, v_b_v, v_attn_w_q, v_attn_b_q, v_attn_sinks, v_attn_w_o, v_attn_b_o, v_final_norm_w):
    p = dict(zip(WEIGHT_NAMES, (norm_w, ffn_w_gate, ffn_w_up, ffn_w_down, ssm_w_in, ssm_conv_w, ssm_conv_b, ssm_dt_bias, ssm_a_log, ssm_d, ssm_norm_w, ssm_w_out, kv_norm_w, w_k, b_k, w_v, b_v, attn_w_q, attn_b_q, attn_sinks, attn_w_o, attn_b_o, final_norm_w)))
    m = dict(zip(WEIGHT_NAMES, (m_norm_w, m_ffn_w_gate, m_ffn_w_up, m_ffn_w_down, m_ssm_w_in, m_ssm_conv_w, m_ssm_conv_b, m_ssm_dt_bias, m_ssm_a_log, m_ssm_d, m_ssm_norm_w, m_ssm_w_out, m_kv_norm_w, m_w_k, m_b_k, m_w_v, m_b_v, m_attn_w_q, m_attn_b_q, m_attn_sinks, m_attn_w_o, m_attn_b_o, m_final_norm_w)))
    v = dict(zip(WEIGHT_NAMES, (v_norm_w, v_ffn_w_gate, v_ffn_w_up, v_ffn_w_down, v_ssm_w_in, v_ssm_conv_w, v_ssm_conv_b, v_ssm_dt_bias, v_ssm_a_log, v_ssm_d, v_ssm_norm_w, v_ssm_w_out, v_kv_norm_w, v_w_k, v_b_k, v_w_v, v_b_v, v_attn_w_q, v_attn_b_q, v_attn_sinks, v_attn_w_o, v_attn_b_o, v_final_norm_w)))
    loss, grad_x, grads, deltas, new_m, new_v = _step(x[0], loss_target[0], p, m, v)
    loss = lax.psum(loss[0, 0], ("x", "y", "c"))
    return (loss, grad_x[None], *[grads[n] for n in WEIGHT_NAMES], *[deltas[n] for n in WEIGHT_NAMES],
            *[new_m[n] for n in WEIGHT_NAMES], *[new_v[n] for n in WEIGHT_NAMES])
```

```python
import functools
import math

import jax
import jax.numpy as jnp
from jax import lax
from jax.experimental import pallas as pl
from jax.experimental.pallas import tpu as pltpu

F32 = jnp.float32
BF16 = jnp.bfloat16

N_DEV = 8
SEQ = 2048
D_MODEL = 1024
D_FF_SHARD = 352
N_FFN = 4
D_INNER = 2048
SSM_HEADS = 32
SSM_HEAD_DIM = 64
SSM_GROUPS = 4
HEADS_PER_GROUP = 8
SSM_STATE = 128
CHUNK = 128
N_CHUNKS = SEQ // CHUNK
GN = SSM_GROUPS * SSM_STATE
CONV_DIM = D_INNER + 2 * GN
CONV_WIDTH = 4
ZX_DIM = D_INNER + CONV_DIM
IN_PROJ_SHARD = 644
ATT_HEAD_DIM = 64
N_Q_HEADS = 16
N_KV_HEADS = 4
Q_PER_KV = 4
KV_DIM = N_KV_HEADS * ATT_HEAD_DIM
WINDOW = 128
ROPE_THETA = 10000.0
EPS = 1e-5
FFN_RES_WEIGHT = 0.5
ATT_SCALE = 1.0 / math.sqrt(ATT_HEAD_DIM)
NEG_BIG = -1e30

ADAM_LR = 0.001
ADAM_B1 = 0.9
ADAM_B2 = 0.999
ADAM_EPS = 1e-08
ADAM_WD = 0.01
ADAM_STEP = 10

VMEM_LIMIT_BYTES = 56 * 1024 * 1024
FFN_BWD_VMEM_LIMIT_BYTES = 61 * 1024 * 1024

NN = (((1,), (0,)), ((), ()))
NT = (((1,), (1,)), ((), ()))
TN = (((0,), (0,)), ((), ()))
_DIMS = {"nn": NN, "nt": NT, "tn": TN}


def _params(*sem):
    return pltpu.CompilerParams(dimension_semantics=sem if sem else None, vmem_limit_bytes=VMEM_LIMIT_BYTES)


def _dot(a, b, dims=NN):
    return lax.dot_general(a.astype(BF16), b.astype(BF16), dims, preferred_element_type=F32)


def _sigmoid(x):
    return 1.0 / (1.0 + jnp.exp(-x))


def _dsilu(x, s):
    return s * (1.0 + x * (1.0 - s))


def _rms(x):
    r = lax.rsqrt(jnp.mean(x * x, axis=-1, keepdims=True) + EPS)
    return x * r, r


def _sum_all(x):
    return jnp.sum(jnp.sum(x, axis=1, keepdims=True), axis=0, keepdims=True)


MESH = pl.DeviceIdType.MESH
N_PEERS = N_DEV - 1
N_CHIPS = N_DEV // 2


def _position():
    return lax.axis_index("x"), lax.axis_index("y"), lax.axis_index("c")


def _slot(p):
    return 4 * p[0] + 2 * p[1] + p[2]


class _Exchange:
    def __init__(self, arrays, out_shapes):
        n = len(arrays)
        self.arrays = list(arrays)
        self.out_shapes = out_shapes
        self.scratch = [pltpu.SemaphoreType.DMA((n, N_PEERS)), pltpu.SemaphoreType.DMA((n, N_PEERS)), pltpu.SemaphoreType.DMA((n,))]
        self.results = None


class _Gather(_Exchange):
    def __init__(self, pieces):
        pieces = [p if isinstance(p, tuple) else (p, None) for p in pieces]
        self.blocks = [k for _, k in pieces]
        shapes = [a.shape if k is None else a.shape[1:] for a, k in pieces]
        super().__init__([a for a, _ in pieces], [jax.ShapeDtypeStruct((N_DEV,) + s, a.dtype) for s, (a, _) in zip(shapes, pieces)])

    def _plan(self, ins, outs, sems):
        send_sems, recv_sems, local_sems = sems
        x, y, c = _position()
        me, sibling = (x, y, c), (x, y, 1 - c)
        chips = [(1 - x, y), (x, 1 - y), (1 - x, 1 - y)]
        n = len(ins)
        ins = [r if k is None else r.at[k] for r, k in zip(ins, self.blocks)]

        def copy(a, k, block, to, src=None):
            dst = outs[a].at[_slot(block)]
            return pltpu.make_async_remote_copy(src_ref=dst if src is None else src, dst_ref=dst, send_sem=send_sems.at[a, k],
                                                recv_sem=recv_sems.at[a, k], device_id=to, device_id_type=MESH)

        mine = [pltpu.make_async_copy(ins[a], outs[a].at[_slot(me)], local_sems.at[a]) for a in range(n)]
        first = []
        for a in range(n):
            first.append(copy(a, 0, me, sibling, src=ins[a]))
            first += [copy(a, 1 + j, me, (*chip, c), src=ins[a]) for j, chip in enumerate(chips)]
        return n, c, me, sibling, chips, copy, mine, first

    def start(self, ins, outs, sems):
        _, _, _, _, _, _, mine, first = self._plan(ins, outs, sems)
        for cp in mine + first:
            cp.start()

    def finish(self, ins, outs, sems):
        n, c, me, sibling, chips, copy, mine, first = self._plan(ins, outs, sems)
        passed = []
        for j, chip in enumerate(chips):
            for a in range(n):
                copy(a, 1 + j, (*chip, c), me).wait_recv()
                fwd = copy(a, 4 + j, (*chip, c), sibling)
                fwd.start()
                passed.append(fwd)
        for a in range(n):
            copy(a, 0, sibling, me).wait_recv()
            for j, chip in enumerate(chips):
                copy(a, 4 + j, (*chip, 1 - c), me).wait_recv()
        for cp in first + passed:
            cp.wait_send()
        for cp in mine:
            cp.wait()


class _PairSwap(_Exchange):
    def __init__(self, arrays):
        n = len(arrays)
        self.arrays = list(arrays)
        self.out_shapes = [jax.ShapeDtypeStruct((N_CHIPS,) + a.shape[1:], a.dtype) for a in arrays]
        self.scratch = [pltpu.SemaphoreType.DMA((n, N_CHIPS)), pltpu.SemaphoreType.DMA((n, N_CHIPS))]
        self.results = None

    def _plan(self, ins, outs, sems):
        send_sems, recv_sems = sems
        x, y, c = _position()
        return [pltpu.make_async_remote_copy(src_ref=ins[a].at[2 * q + 1 - c], dst_ref=outs[a].at[q], send_sem=send_sems.at[a, q],
                                             recv_sem=recv_sems.at[a, q], device_id=(x, y, 1 - c), device_id_type=MESH)
                for a in range(len(ins)) for q in range(N_CHIPS)]

    def start(self, ins, outs, sems):
        for cp in self._plan(ins, outs, sems):
            cp.start()

    def finish(self, ins, outs, sems):
        for cp in self._plan(ins, outs, sems):
            cp.wait()


class _ChipExchange(_Exchange):
    def __init__(self, arrays):
        n = len(arrays)
        self.arrays = list(arrays)
        self.out_shapes = [jax.ShapeDtypeStruct(a.shape, a.dtype) for a in arrays]
        self.scratch = [pltpu.SemaphoreType.DMA((n, 3)), pltpu.SemaphoreType.DMA((n, 3)), pltpu.SemaphoreType.DMA((n,))]
        self.results = None

    def _plan(self, ins, outs, sems):
        send_sems, recv_sems, local_sems = sems
        x, y, c = _position()
        here = 2 * x + y
        chips = [(1 - x, y), (x, 1 - y), (1 - x, 1 - y)]
        n = len(ins)

        def copy(a, k, src_slot, dst_slot):
            return pltpu.make_async_remote_copy(src_ref=ins[a].at[src_slot], dst_ref=outs[a].at[dst_slot], send_sem=send_sems.at[a, k],
                                                recv_sem=recv_sems.at[a, k], device_id=(*chips[k], c), device_id_type=MESH)

        there = [2 * qx + qy for qx, qy in chips]
        mine = [pltpu.make_async_copy(ins[a].at[here], outs[a].at[here], local_sems.at[a]) for a in range(n)]
        sends = [copy(a, k, there[k], here) for a in range(n) for k in range(3)]
        arrivals = lambda: [copy(a, k, here, there[k]) for a in range(n) for k in range(3)]
        return mine, sends, arrivals

    def start(self, ins, outs, sems):
        mine, sends, _ = self._plan(ins, outs, sems)
        for cp in mine + sends:
            cp.start()

    def finish(self, ins, outs, sems):
        mine, sends, arrivals = self._plan(ins, outs, sems)
        for cp in arrivals():
            cp.wait_recv()
        for cp in sends:
            cp.wait_send()
        for cp in mine:
            cp.wait()


def _call(body, *, name, grid, in_specs, out_specs, out_shape, args, scratch_shapes=(), sem=(), comm=(), aliases=None,
          vmem_limit=VMEM_LIMIT_BYTES):
    single = not isinstance(out_shape, (list, tuple))
    out_shape = [out_shape] if single else list(out_shape)
    out_specs = [out_specs] if single else list(out_specs)
    comms = list(comm or ())
    n_in, n_out, n_scr = len(args), len(out_shape), len(scratch_shapes)
    params = pltpu.CompilerParams(dimension_semantics=tuple(sem) if sem else None, vmem_limit_bytes=vmem_limit)
    if not comms:
        res = pl.pallas_call(body, name=name, grid=grid, in_specs=list(in_specs), out_specs=out_specs, out_shape=out_shape,
                             scratch_shapes=list(scratch_shapes), input_output_aliases=aliases or {}, compiler_params=params)(*args)
        return res[0] if single else res
    counts = [n_in] + [len(c.arrays) for c in comms] + [n_out] + [len(c.out_shapes) for c in comms] + [n_scr] + [len(c.scratch) for c in comms]
    nc = len(comms)

    def carried(*refs):
        pos, groups = 0, []
        for cnt in counts:
            groups.append(refs[pos:pos + cnt])
            pos += cnt
        ins, c_ins = groups[0], groups[1:1 + nc]
        outs, c_outs = groups[1 + nc], groups[2 + nc:2 + 2 * nc]
        scr, c_sems = groups[2 + 2 * nc], groups[3 + 2 * nc:]
        ids = [pl.program_id(d) for d in range(len(grid))]
        is_first = functools.reduce(jnp.logical_and, [i == 0 for i in ids])
        is_last = functools.reduce(jnp.logical_and, [i == g - 1 for i, g in zip(ids, grid)])

        @pl.when(is_first)
        def _():
            for q, c in enumerate(comms):
                c.start(c_ins[q], c_outs[q], c_sems[q])

        body(*ins, *outs, *scr)

        @pl.when(is_last)
        def _():
            for q, c in enumerate(comms):
                c.finish(c_ins[q], c_outs[q], c_sems[q])

    anyspec = pl.BlockSpec(memory_space=pl.ANY)
    c_arrays = [a for c in comms for a in c.arrays]
    c_shapes = [s for c in comms for s in c.out_shapes]
    res = pl.pallas_call(
        carried, name=name, grid=grid, in_specs=list(in_specs) + [anyspec] * len(c_arrays), out_specs=out_specs + [anyspec] * len(c_shapes),
        out_shape=out_shape + c_shapes, scratch_shapes=list(scratch_shapes) + [s for c in comms for s in c.scratch],
        input_output_aliases=aliases or {}, compiler_params=params)(*args, *c_arrays)
    pos = n_out
    for c in comms:
        c.results = list(res[pos:pos + len(c.out_shapes)])
        pos += len(c.out_shapes)
    return res[0] if single else list(res[:n_out])


def _run_exchange(comm, *, name):
    def body(*refs):
        n_ci, n_co = len(comm.arrays), len(comm.out_shapes)
        ins, outs, sems = refs[:n_ci], refs[n_ci:n_ci + n_co], refs[n_ci + n_co:]
        comm.start(ins, outs, sems)
        comm.finish(ins, outs, sems)

    anyspec = pl.BlockSpec(memory_space=pl.ANY)
    comm.results = list(pl.pallas_call(
        body, name=name, in_specs=[anyspec] * len(comm.arrays), out_specs=[anyspec] * len(comm.out_shapes),
        out_shape=list(comm.out_shapes), scratch_shapes=list(comm.scratch))(*comm.arrays))
    return comm.results


def _mm(a, b, *, dims="nn", bias=None, res=None, out_dtype=F32, name, tm=1024, tn=1024, tk=1024, comm=None, b_rows=None,
        out_window=None, into=None, colsum_b=False):
    if dims == "tn":
        k_dim, m_dim = a.shape
    else:
        m_dim, k_dim = a.shape
    row0, n_rows = b_rows if b_rows is not None else (0, b.shape[0])
    n_dim = n_rows if dims == "nt" else b.shape[1]
    assert dims == "nt" or n_rows == k_dim, (name, a.shape, b.shape, b_rows)
    tm, tn, tk = min(tm, m_dim), min(tn, n_dim), min(tk, k_dim)
    assert m_dim % tm == 0 and n_dim % tn == 0 and k_dim % tk == 0, (name, a.shape, b.shape)
    nk = k_dim // tk
    a_spec = pl.BlockSpec((tk, tm), lambda i, j, k: (k, i)) if dims == "tn" else pl.BlockSpec((tm, tk), lambda i, j, k: (i, k))
    if dims == "nt":
        assert row0 % tn == 0
        b_spec = pl.BlockSpec((tn, tk), lambda i, j, k: (row0 // tn + j, k))
    else:
        assert row0 % tk == 0
        b_spec = pl.BlockSpec((tk, tn), lambda i, j, k: (row0 // tk + k, j))
    in_specs, args = [a_spec, b_spec], [a, b]
    if bias is not None:
        in_specs.append(pl.BlockSpec((1, tn), lambda i, j, k: (0, j)))
        args.append(bias)
    if res is not None:
        in_specs.append(pl.BlockSpec((tm, tn), lambda i, j, k: (i, j)))
        args.append(res)
    dn = _DIMS[dims]

    if colsum_b:
        assert dims == "tn" and m_dim == tm and into is None and out_window is None

    def body(*refs):
        a_ref, b_ref = refs[0], refs[1]
        acc_ref = refs[-1]
        o_ref = refs[-3] if colsum_b else refs[-2]
        k = pl.program_id(2)

        @pl.when(k == 0)
        def _():
            acc_ref[...] = jnp.zeros_like(acc_ref)
            if colsum_b:
                refs[-2][...] = jnp.zeros_like(refs[-2])

        acc_ref[...] += _dot(a_ref[...], b_ref[...], dn)
        if colsum_b:
            refs[-2][...] += jnp.sum(b_ref[...].astype(F32), axis=0, keepdims=True)

        @pl.when(k == nk - 1)
        def _():
            r = acc_ref[...]
            pos = 2
            if bias is not None:
                r = r + refs[pos][...]
                pos += 1
            if res is not None:
                r = r + refs[pos][...]
            o_ref[...] = r.astype(out_dtype)

    out_row0, out_rows = out_window if out_window is not None else (0, m_dim)
    assert out_row0 % tm == 0
    aliases = None
    if into is not None:
        assert into.shape == (out_rows, n_dim) and into.dtype == out_dtype
        in_specs.append(pl.BlockSpec(memory_space=pl.ANY))
        args.append(into)
        aliases = {len(args) - 1: 0}
    out_spec = pl.BlockSpec((tm, tn), lambda i, j, k: (out_row0 // tm + i, j))
    out_shape = jax.ShapeDtypeStruct((out_rows, n_dim), out_dtype)
    if colsum_b:
        out_spec = [out_spec, pl.BlockSpec((1, tn), lambda i, j, k: (0, j))]
        out_shape = [out_shape, jax.ShapeDtypeStruct((1, n_dim), F32)]
    return _call(
        body, name=name, grid=(m_dim // tm, n_dim // tn, nk), in_specs=in_specs, out_specs=out_spec, out_shape=out_shape,
        aliases=aliases, scratch_shapes=[pltpu.VMEM((tm, tn), F32)], sem=("parallel", "parallel", "arbitrary"), args=args, comm=comm)


def _mm_norm_bwd(a, b, x, nw, res, *, dims="nn", b_rows=None, add=None, name, tm=1024, tk=1024, comm=None):
    m_dim, k_dim = a.shape
    row0, n_rows = b_rows if b_rows is not None else (0, b.shape[0])
    d_dim = x.shape[1]
    tm, tk = min(tm, m_dim), min(tk, k_dim)
    assert m_dim % tm == 0 and k_dim % tk == 0 and (n_rows if dims == "nt" else b.shape[1]) == d_dim, (name, a.shape, b.shape)
    nk = k_dim // tk
    if dims == "nt":
        assert row0 % d_dim == 0
        b_spec = pl.BlockSpec((d_dim, tk), lambda i, k: (row0 // d_dim, k))
    else:
        assert row0 % tk == 0 and n_rows == k_dim
        b_spec = pl.BlockSpec((tk, d_dim), lambda i, k: (row0 // tk + k, 0))
    row = pl.BlockSpec((tm, d_dim), lambda i, k: (i, 0))
    vec = pl.BlockSpec((1, d_dim), lambda i, k: (0, 0))
    extra = ([add] if add is not None else []) + list(res)
    dn = _DIMS[dims]

    def body(*refs):
        a_ref, b_ref, x_ref, nw_ref = refs[:4]
        extra_refs = refs[4:4 + len(extra)]
        dx_ref, dnw_ref, dob_ref, acc_ref = refs[-4:]
        i, k = pl.program_id(0), pl.program_id(1)

        @pl.when(k == 0)
        def _():
            acc_ref[...] = jnp.zeros_like(acc_ref)

        @pl.when((i == 0) & (k == 0))
        def _():
            dnw_ref[...] = jnp.zeros_like(dnw_ref)

        acc_ref[...] += _dot(a_ref[...], b_ref[...], dn)

        @pl.when(k == nk - 1)
        def _():
            dh = acc_ref[...]
            rest = list(extra_refs)
            if add is not None:
                dh = dh + rest.pop(0)[...]
            xhat, r = _rms(x_ref[...])
            dxhat = dh * nw_ref[...]
            dx = r * (dxhat - xhat * jnp.mean(dxhat * xhat, axis=-1, keepdims=True))
            for rr in rest:
                dx = dx + rr[...]
            dx_ref[...] = dx
            dob_ref[...] = (FFN_RES_WEIGHT * dx).astype(BF16)
            dnw_ref[...] += jnp.sum(dh * xhat, axis=0, keepdims=True)

    return _call(
        body, name=name, grid=(m_dim // tm, nk),
        in_specs=[pl.BlockSpec((tm, tk), lambda i, k: (i, k)), b_spec, row, vec] + [row] * len(extra), out_specs=[row, vec, row],
        out_shape=[jax.ShapeDtypeStruct((m_dim, d_dim), F32), jax.ShapeDtypeStruct((1, d_dim), F32),
                   jax.ShapeDtypeStruct((m_dim, d_dim), BF16)],
        scratch_shapes=[pltpu.VMEM((tm, d_dim), F32)], sem=("arbitrary", "arbitrary"), args=[a, b, x, nw] + extra, comm=comm)


def _rope_rotate(x, cos_t, sin_t):
    rows, width = x.shape
    half = ATT_HEAD_DIM // 2
    lane = lax.broadcasted_iota(jnp.int32, (rows, width), 1)
    first = (lane % ATT_HEAD_DIM) < half
    rot = jnp.where(first, -pltpu.roll(x, width - half, 1), pltpu.roll(x, half, 1))
    reps = width // 128
    return x * jnp.tile(cos_t, (1, reps)) + rot * jnp.tile(sin_t, (1, reps))


def _norm_mm(x, nw, w, bias, *, name, tm=1024, tn=1024, comm=None, w_rows=None, rope=None):
    t_dim, d_dim = x.shape
    transposed = w_rows is not None
    n_dim = w_rows if transposed else w.shape[1]
    tn = min(tn, n_dim)
    assert t_dim % tm == 0 and n_dim % tn == 0
    has_bias = bias is not None
    w_spec = pl.BlockSpec((tn, d_dim), lambda i, j: (j, 0)) if transposed else pl.BlockSpec((d_dim, tn), lambda i, j: (0, j))
    dn = NT if transposed else NN
    in_specs = [pl.BlockSpec((tm, d_dim), lambda i, j: (i, 0)), pl.BlockSpec((1, d_dim), lambda i, j: (0, 0)), w_spec]
    args = [x, nw, w]
    if has_bias:
        in_specs.append(pl.BlockSpec((1, tn), lambda i, j: (0, j)))
        args.append(bias)
    if rope is not None:
        in_specs += [pl.BlockSpec((tm, LANES), lambda i, j: (i, 0))] * 2
        args += list(rope)

    def body(*refs):
        x_ref, nw_ref, w_ref = refs[:3]
        o_ref, h_ref = refs[-2], refs[-1]

        @pl.when(pl.program_id(1) == 0)
        def _():
            xhat, _ = _rms(x_ref[...])
            h_ref[...] = (xhat * nw_ref[...]).astype(BF16)

        r = _dot(h_ref[...], w_ref[...], dn)
        if has_bias:
            r = r + refs[3][...]
        if rope is not None:
            r = _rope_rotate(r, refs[-4][...], refs[-3][...])
        o_ref[...] = r

    return _call(
        body, name=name, grid=(t_dim // tm, n_dim // tn), in_specs=in_specs,
        out_specs=[pl.BlockSpec((tm, tn), lambda i, j: (i, j)), pl.BlockSpec((tm, d_dim), lambda i, j: (i, 0))],
        out_shape=[jax.ShapeDtypeStruct((t_dim, n_dim), F32), jax.ShapeDtypeStruct((t_dim, d_dim), BF16)],
        sem=("parallel", "arbitrary"), args=args, comm=comm)


def _norm_bwd(x, nw, dh, res, *, name, tm=512, comm=None):
    t_dim, d_dim = x.shape
    n_res = len(res)
    row = pl.BlockSpec((tm, d_dim), lambda i: (i, 0))
    vec = pl.BlockSpec((1, d_dim), lambda i: (0, 0))

    def body(*refs):
        x_ref, nw_ref, dh_ref = refs[:3]
        dx_ref, dnw_ref = refs[-2], refs[-1]
        xhat, r = _rms(x_ref[...])
        dh = dh_ref[...]
        dxhat = dh * nw_ref[...]
        dx = r * (dxhat - xhat * jnp.mean(dxhat * xhat, axis=-1, keepdims=True))
        for rr in refs[3:3 + n_res]:
            dx = dx + rr[...]
        dx_ref[...] = dx

        @pl.when(pl.program_id(0) == 0)
        def _():
            dnw_ref[...] = jnp.zeros_like(dnw_ref)

        dnw_ref[...] += jnp.sum(dh * xhat, axis=0, keepdims=True)

    return _call(
        body, name=name, grid=(t_dim // tm,), in_specs=[row, vec, row] + [row] * n_res,
        out_specs=[row, vec],
        out_shape=[jax.ShapeDtypeStruct((t_dim, d_dim), F32), jax.ShapeDtypeStruct((1, d_dim), F32)],
        sem=("arbitrary",), args=[x, nw, dh, *res], comm=comm)


FFN_ROW_TILE = 512
FFN_SHARDS_PER_STEP = 2
FFN_STEPS = N_DEV // FFN_SHARDS_PER_STEP
FFN_STEP_COLS = FFN_SHARDS_PER_STEP * D_FF_SHARD


def _ffn_step_view(w):
    return w.reshape(FFN_STEPS, FFN_STEP_COLS, w.shape[-1])


def _ffn_specs(t_dim, d_dim):
    full = pl.BlockSpec((t_dim, d_dim), lambda j: (0, 0))
    wspec = pl.BlockSpec((None, FFN_STEP_COLS, d_dim), lambda j: (j, 0, 0))
    pre = pl.BlockSpec((None, t_dim, FFN_STEP_COLS), lambda j: (j, 0, 0))
    return full, wspec, pre


def _ffn_fwd(x, nw, wg, wu, wd, *, name, comm=None):
    t_dim, d_dim = x.shape
    n_tiles = t_dim // FFN_ROW_TILE

    def body(x_ref, nw_ref, wg_ref, wu_ref, wd_ref, o_ref, g_ref, u_ref, h_ref):
        j = pl.program_id(0)

        @pl.when(j == 0)
        def _():
            xhat, _ = _rms(x_ref[...])
            h_ref[...] = (xhat * nw_ref[...]).astype(BF16)
            o_ref[...] = jnp.zeros_like(o_ref)

        for t in range(n_tiles):
            rows = pl.ds(t * FFN_ROW_TILE, FFN_ROW_TILE)
            h = h_ref[rows, :]
            g = _dot(h, wg_ref[...], NT)
            u = _dot(h, wu_ref[...], NT)
            g_ref[rows, :] = g.astype(BF16)
            u_ref[rows, :] = u.astype(BF16)
            o_ref[rows, :] += _dot(g * _sigmoid(g) * u, wd_ref[...])

        @pl.when(j == FFN_STEPS - 1)
        def _():
            o_ref[...] = x_ref[...] + FFN_RES_WEIGHT * o_ref[...]

    full, wspec, pre = _ffn_specs(t_dim, d_dim)
    pre_shape = jax.ShapeDtypeStruct((FFN_STEPS, t_dim, FFN_STEP_COLS), BF16)
    return _call(
        body, name=name, grid=(FFN_STEPS,),
        in_specs=[full, pl.BlockSpec((1, d_dim), lambda j: (0, 0)), wspec, wspec, wspec],
        out_specs=[full, pre, pre, full],
        out_shape=[jax.ShapeDtypeStruct((t_dim, d_dim), F32), pre_shape, pre_shape, jax.ShapeDtypeStruct((t_dim, d_dim), BF16)],
        sem=("arbitrary",), args=[x, nw, _ffn_step_view(wg), _ffn_step_view(wu), _ffn_step_view(wd)], comm=comm)


def _ffn_bwd(h, dob, pre_g, pre_u, wg, wu, wd, *, name, comm=None):
    t_dim, d_dim = h.shape
    n_tiles = t_dim // FFN_ROW_TILE

    def body(h_ref, dob_ref, g_ref, u_ref, wg_ref, wu_ref, wd_ref, dh_ref, gg_ref, gu_ref, gd_ref, dwg_scr, dwu_scr, dwd_scr):
        @pl.when(pl.program_id(0) == 0)
        def _():
            dh_ref[...] = jnp.zeros_like(dh_ref)

        for t in range(n_tiles):
            rows = pl.ds(t * FFN_ROW_TILE, FFN_ROW_TILE)
            hh = h_ref[rows, :]
            do = dob_ref[rows, :]
            g = g_ref[rows, :].astype(F32)
            u = u_ref[rows, :].astype(F32)
            sg = _sigmoid(g)
            s = g * sg
            da = _dot(do, wd_ref[...], NT)
            dwd = _dot(s * u, do, TN)
            du = (da * s).astype(BF16)
            dg = (da * u * _dsilu(g, sg)).astype(BF16)
            dwg = _dot(dg, hh, TN)
            dwu = _dot(du, hh, TN)
            if t == 0:
                dwd_scr[...] = dwd
                dwg_scr[...] = dwg
                dwu_scr[...] = dwu
            else:
                dwd_scr[...] += dwd
                dwg_scr[...] += dwg
                dwu_scr[...] += dwu
            dh_ref[rows, :] += _dot(dg, wg_ref[...]) + _dot(du, wu_ref[...])
        gg_ref[...] = dwg_scr[...].astype(BF16)
        gu_ref[...] = dwu_scr[...].astype(BF16)
        gd_ref[...] = dwd_scr[...].astype(BF16)

    full, wspec, pre = _ffn_specs(t_dim, d_dim)
    gspec = pl.BlockSpec((None, FFN_STEP_COLS, d_dim), lambda j: (j, 0, 0), pipeline_mode=pl.Buffered(1))
    grad_shape = jax.ShapeDtypeStruct((FFN_STEPS, FFN_STEP_COLS, d_dim), BF16)
    dh, gg, gu, gd = _call(
        body, name=name, grid=(FFN_STEPS,),
        in_specs=[full, full, pre, pre, wspec, wspec, wspec], out_specs=[full, gspec, gspec, gspec],
        out_shape=[jax.ShapeDtypeStruct((t_dim, d_dim), F32)] + [grad_shape] * 3,
        scratch_shapes=[pltpu.VMEM((FFN_STEP_COLS, d_dim), F32)] * 3, sem=("arbitrary",), vmem_limit=FFN_BWD_VMEM_LIMIT_BYTES,
        args=[h, dob, pre_g, pre_u, _ffn_step_view(wg), _ffn_step_view(wu), _ffn_step_view(wd)], comm=comm)
    return dh, gg.reshape(wg.shape), gu.reshape(wu.shape), gd.reshape(wd.shape)


CONV_COLS = 256


def _shift_down(u, s, rows):
    return jnp.where(rows >= s, pltpu.roll(u, s, 0), 0.0)


def _shift_up(u, s, rows, t_dim):
    return jnp.where(rows < t_dim - s, pltpu.roll(u, t_dim - s, 0), 0.0)


def _conv_pre(u, w_ref, b_ref, rows):
    c = b_ref[...] + w_ref[CONV_WIDTH - 1:CONV_WIDTH, :] * u
    for k in range(CONV_WIDTH - 1):
        c = c + w_ref[k:k + 1, :] * _shift_down(u, CONV_WIDTH - 1 - k, rows)
    return c


def _conv_fwd(zx, cw, cb, *, name, comm=None):
    t_dim = zx.shape[0]
    off = D_INNER // CONV_COLS

    def body(u_ref, w_ref, b_ref, o_ref):
        rows = lax.broadcasted_iota(jnp.int32, (t_dim, CONV_COLS), 0)
        c = _conv_pre(u_ref[...], w_ref, b_ref, rows)
        o_ref[...] = c * _sigmoid(c)

    return _call(
        body, name=name, grid=(CONV_DIM // CONV_COLS,),
        in_specs=[pl.BlockSpec((t_dim, CONV_COLS), lambda j: (0, off + j)),
                  pl.BlockSpec((CONV_WIDTH, CONV_COLS), lambda j: (0, j)), pl.BlockSpec((1, CONV_COLS), lambda j: (0, j))],
        out_specs=pl.BlockSpec((t_dim, CONV_COLS), lambda j: (0, j)),
        out_shape=jax.ShapeDtypeStruct((t_dim, CONV_DIM), F32), sem=("parallel",), args=[zx, cw, cb], comm=comm)


def _conv_bwd(zx, cw, cb, dxs, db, dc, dzx, *, name, comm=None):
    t_dim = zx.shape[0]
    off = D_INNER // CONV_COLS
    n_xs = D_INNER // CONV_COLS
    n_b = GN // CONV_COLS

    def body(u_ref, w_ref, b_ref, dxs_ref, db_ref, dc_ref, dzx_in, dzx_ref, dw_ref, dbias_ref):
        j = pl.program_id(0)
        rows = lax.broadcasted_iota(jnp.int32, (t_dim, CONV_COLS), 0)
        u = u_ref[...]
        c = _conv_pre(u, w_ref, b_ref, rows)
        d = jnp.where(j < n_xs, dxs_ref[...], jnp.where(j < n_xs + n_b, db_ref[...], dc_ref[...]))
        dcv = d * _dsilu(c, _sigmoid(c))
        dpre = w_ref[CONV_WIDTH - 1:CONV_WIDTH, :] * dcv
        dw_ref[CONV_WIDTH - 1:CONV_WIDTH, :] = jnp.sum(dcv * u, axis=0, keepdims=True)
        for k in range(CONV_WIDTH - 1):
            s = CONV_WIDTH - 1 - k
            dpre = dpre + w_ref[k:k + 1, :] * _shift_up(dcv, s, rows, t_dim)
            dw_ref[k:k + 1, :] = jnp.sum(dcv * _shift_down(u, s, rows), axis=0, keepdims=True)
        dzx_ref[...] = dpre
        dbias_ref[...] = jnp.sum(dcv, axis=0, keepdims=True)

    blk = lambda n: pl.BlockSpec((t_dim, CONV_COLS), n)
    return _call(
        body, name=name, grid=(CONV_DIM // CONV_COLS,),
        in_specs=[blk(lambda j: (0, off + j)), pl.BlockSpec((CONV_WIDTH, CONV_COLS), lambda j: (0, j)),
                  pl.BlockSpec((1, CONV_COLS), lambda j: (0, j)),
                  blk(lambda j: (0, jnp.minimum(j, n_xs - 1))),
                  blk(lambda j: (0, jnp.clip(j - n_xs, 0, n_b - 1))),
                  blk(lambda j: (0, jnp.clip(j - n_xs - n_b, 0, n_b - 1))),
                  pl.BlockSpec(memory_space=pl.ANY)],
        out_specs=[blk(lambda j: (0, off + j)), pl.BlockSpec((CONV_WIDTH, CONV_COLS), lambda j: (0, j)),
                   pl.BlockSpec((1, CONV_COLS), lambda j: (0, j))],
        out_shape=[jax.ShapeDtypeStruct(dzx.shape, F32), jax.ShapeDtypeStruct((CONV_WIDTH, CONV_DIM), F32),
                   jax.ShapeDtypeStruct((1, CONV_DIM), F32)],
        aliases={6: 0}, sem=("parallel",), args=[zx, cw, cb, dxs, db, dc, dzx], comm=comm)


def _softplus_parts(x):
    e = jnp.exp(-jnp.abs(x))
    u = 1.0 + e
    log1p_e = jnp.where(u == 1.0, e, jnp.log(u) * e / jnp.where(u == 1.0, 1.0, u - 1.0))
    return jnp.maximum(x, 0.0) + log1p_e


def _dt_prep(dtr, dt_bias, a_log, *, name):
    def body(dtr_ref, bias_ref, alog_ref, dt_ref, a_ref):
        dt = _softplus_parts(dtr_ref[...] + bias_ref[...])
        dt_ref[...] = dt
        a_ref[...] = dt * (-jnp.exp(alog_ref[...]))

    return pl.pallas_call(body, name=name, out_shape=[jax.ShapeDtypeStruct(dtr.shape, F32)] * 2,
                          compiler_params=_params())(dtr, dt_bias, a_log)


def _dt_bwd(dtr, dt_bias, a_log, dt, ddt, da, *, name):
    def body(dtr_ref, bias_ref, alog_ref, dt_ref, ddt_ref, da_ref, ddtr_ref, dbias_ref, dalog_ref):
        a_neg = -jnp.exp(alog_ref[...])
        da_v = da_ref[...]
        ddt_tot = ddt_ref[...] + da_v * a_neg
        ddtr = ddt_tot * _sigmoid(dtr_ref[...] + bias_ref[...])
        ddtr_ref[...] = ddtr
        dbias_ref[...] = jnp.sum(ddtr, axis=0, keepdims=True)
        dalog_ref[...] = jnp.sum(da_v * dt_ref[...], axis=0, keepdims=True) * a_neg

    return pl.pallas_call(
        body, name=name,
        out_shape=[jax.ShapeDtypeStruct(dtr.shape, F32), jax.ShapeDtypeStruct((1, SSM_HEADS), F32),
                   jax.ShapeDtypeStruct((1, SSM_HEADS), F32)],
        compiler_params=_params())(dtr, dt_bias, a_log, dt, ddt, da)


GROUP_COLS = HEADS_PER_GROUP * SSM_HEAD_DIM
LANES = 128
HEADS_PER_LANE_BLOCK = LANES // SSM_HEAD_DIM


def _split3(x):
    hi = x.astype(BF16)
    r1 = x - hi.astype(F32)
    mid = r1.astype(BF16)
    lo = (r1 - mid.astype(F32)).astype(BF16)
    return hi, mid, lo


def _dot_select(a, b, dims=NN, data=0):
    out = None
    for part in _split3(a if data == 0 else b):
        lhs, rhs = (part, b.astype(BF16)) if data == 0 else (a.astype(BF16), part)
        t = lax.dot_general(lhs, rhs, dims, preferred_element_type=F32)
        out = t if out is None else out + t
    return out


def _group_sums(vals, expand):
    out = _dot_select(jnp.concatenate(vals, axis=0), expand, NT)
    return [out[i * CHUNK:(i + 1) * CHUNK] for i in range(len(vals))]


def _ssd_chunk_common(a_ref, dt_ref, b_ref, c_ref):
    row = lax.broadcasted_iota(jnp.int32, (CHUNK, CHUNK), 0)
    col = lax.broadcasted_iota(jnp.int32, (CHUNK, CHUNK), 1)
    causal = col <= row
    lower = causal.astype(F32)
    upper = (col >= row).astype(F32)
    head = lax.broadcasted_iota(jnp.int32, (HEADS_PER_GROUP, GROUP_COLS), 0)
    lane = lax.broadcasted_iota(jnp.int32, (HEADS_PER_GROUP, GROUP_COLS), 1)
    expand = ((lane >= head * SSM_HEAD_DIM) & (lane < (head + 1) * SSM_HEAD_DIM)).astype(F32)
    a = a_ref[...]
    cs = _dot_select(lower, a, data=1)
    cs_row = _dot_select(a, upper, TN)
    cs_x = _dot_select(cs, expand)
    dt_x = _dot_select(dt_ref[...], expand)
    e_out_x = jnp.exp(cs_x)
    e_st_x = jnp.exp(cs_x[CHUNK - 1:CHUNK, :] - cs_x)
    bc = b_ref[...]
    cc = c_ref[...]
    cb = _dot(cc, bc, NT)
    return causal, upper, expand.astype(BF16), cs, cs_row, dt_x, e_out_x, e_st_x, bc, cc, cb


def _head_decay(causal, cs, cs_row, h):
    return jnp.exp(jnp.where(causal, cs[:, h:h + 1] - cs_row[h:h + 1, :], NEG_BIG))


def _lane_block_head_masks():
    lane = lax.broadcasted_iota(jnp.int32, (CHUNK, LANES), 1)
    return [(lane >= i * SSM_HEAD_DIM) & (lane < (i + 1) * SSM_HEAD_DIM) for i in range(HEADS_PER_LANE_BLOCK)]


def _decay_state(dst_ref, old, new, cs):
    for h in range(HEADS_PER_GROUP):
        rows = slice(h * SSM_HEAD_DIM, (h + 1) * SSM_HEAD_DIM)
        dst_ref[rows, :] = jnp.exp(cs[CHUNK - 1:CHUNK, h:h + 1]) * old[rows, :] + new[rows, :]


def _ssd_fwd(xbc, dtg, ag, dgx, *, name, comm=None):
    t_dim = xbc.shape[0]

    def body(xs_ref, b_ref, c_ref, dt_ref, a_ref, d_ref, y_ref, st_ref, s_scr):
        @pl.when(pl.program_id(1) == 0)
        def _():
            s_scr[...] = jnp.zeros_like(s_scr)

        causal, _, _, cs, cs_row, dt_x, e_out_x, e_st_x, bc, cc, cb = _ssd_chunk_common(a_ref, dt_ref, b_ref, c_ref)
        masks = _lane_block_head_masks()
        xs = xs_ref[...]
        xdt_x = xs * dt_x
        prev = s_scr[...]
        st_ref[...] = prev
        y_off = e_out_x * _dot(cc, prev, NT) + xs * d_ref[...]
        for blk in range(GROUP_COLS // LANES):
            lanes = slice(blk * LANES, (blk + 1) * LANES)
            x_b = xdt_x[:, lanes].astype(BF16)
            acc = y_off[:, lanes]
            for i in range(HEADS_PER_LANE_BLOCK):
                m = cb * _head_decay(causal, cs, cs_row, blk * HEADS_PER_LANE_BLOCK + i)
                acc = acc + _dot(m, jnp.where(masks[i], x_b, jnp.zeros_like(x_b)))
            y_ref[:, lanes] = acc
        _decay_state(s_scr, prev, _dot(xdt_x * e_st_x, bc, TN), cs)

    xs = pl.BlockSpec((CHUNK, GROUP_COLS), lambda g, c: (c, g))
    bsp = pl.BlockSpec((CHUNK, SSM_STATE), lambda g, c: (c, D_INNER // SSM_STATE + g))
    csp = pl.BlockSpec((CHUNK, SSM_STATE), lambda g, c: (c, (D_INNER + GN) // SSM_STATE + g))
    per_head = pl.BlockSpec((None, CHUNK, HEADS_PER_GROUP), lambda g, c: (g, c, 0))
    dsk = pl.BlockSpec((None, 1, GROUP_COLS), lambda g, c: (g, 0, 0))
    return _call(
        body, name=name, grid=(SSM_GROUPS, N_CHUNKS),
        in_specs=[xs, bsp, csp, per_head, per_head, dsk],
        out_specs=[xs, pl.BlockSpec((None, GROUP_COLS, SSM_STATE), lambda g, c: (c, g, 0))],
        out_shape=[jax.ShapeDtypeStruct((t_dim, D_INNER), F32),
                   jax.ShapeDtypeStruct((N_CHUNKS, D_INNER, SSM_STATE), F32)],
        scratch_shapes=[pltpu.VMEM((GROUP_COLS, SSM_STATE), F32)],
        sem=("parallel", "arbitrary"), args=[xbc, xbc, xbc, dtg, ag, dgx], comm=comm)


def _ssd_bwd(xbc, dtg, ag, dgx, states, dy, *, name, comm=None):
    t_dim = xbc.shape[0]
    last = N_CHUNKS - 1

    def body(xs_ref, b_ref, c_ref, dt_ref, a_ref, d_ref, st_ref, dy_ref,
             dxs_ref, db_ref, dc_ref, ddt_ref, da_ref, dd_ref, ds_scr):
        @pl.when(pl.program_id(1) == 0)
        def _():
            ds_scr[...] = jnp.zeros_like(ds_scr)
            dd_ref[...] = jnp.zeros_like(dd_ref)

        causal, upper, expand, cs, cs_row, dt_x, e_out_x, e_st_x, bc, cc, cb = _ssd_chunk_common(a_ref, dt_ref, b_ref, c_ref)
        masks = _lane_block_head_masks()
        xs = xs_ref[...]
        dy_x = dy_ref[...]
        xdt_x = xs * dt_x
        prev = st_ref[...]
        d_s = ds_scr[...]
        g1_x = _dot(bc, d_s, NT)
        cp_x = _dot(cc, prev, NT)
        d_cb = jnp.zeros((CHUNK, CHUNK), F32)
        lane8 = lax.broadcasted_iota(jnp.int32, (CHUNK, HEADS_PER_GROUP), 1)
        sub8 = lax.broadcasted_iota(jnp.int32, (HEADS_PER_GROUP, CHUNK), 0)
        row_w = jnp.zeros((CHUNK, HEADS_PER_GROUP), F32)
        col_w = jnp.zeros((HEADS_PER_GROUP, CHUNK), F32)
        dxdt_blocks = []
        for blk in range(GROUP_COLS // LANES):
            lanes = slice(blk * LANES, (blk + 1) * LANES)
            dy_b = dy_x[:, lanes].astype(BF16)
            x_b = xdt_x[:, lanes].astype(BF16)
            acc_dx = jnp.zeros((CHUNK, LANES), F32)
            for i in range(HEADS_PER_LANE_BLOCK):
                h = blk * HEADS_PER_LANE_BLOCK + i
                decay = _head_decay(causal, cs, cs_row, h)
                m = cb * decay
                dy_h = jnp.where(masks[i], dy_b, jnp.zeros_like(dy_b))
                acc_dx = acc_dx + _dot(m, dy_h, TN)
                d_m = _dot(dy_h, x_b, NT)
                d_cb = d_cb + d_m * decay
                w = d_m * m
                row_w = jnp.where(lane8 == h, jnp.sum(w, axis=1, keepdims=True), row_w)
                col_w = jnp.where(sub8 == h, jnp.sum(w, axis=0, keepdims=True), col_w)
            dxdt_blocks.append(acc_dx)
        dxdt_x = jnp.concatenate(dxdt_blocks, axis=1) + e_st_x * g1_x
        dxs_ref[...] = dxdt_x * dt_x + dy_x * d_ref[...]
        dye = dy_x * e_out_x
        xde = xdt_x * e_st_x
        ddt, y_off, tl, dskip = _group_sums([dxdt_x * xs, dye * cp_x, xde * g1_x, dy_x * xs], expand)
        ddt_ref[...] = ddt
        dd_ref[...] += jnp.sum(dskip, axis=0, keepdims=True)
        sp = None
        for part in _split3(d_s * prev):
            t = lax.dot_general(expand, part, NN, preferred_element_type=F32)
            sp = t if sp is None else sp + t
        last_col = jnp.exp(cs_row[:, CHUNK - 1:CHUNK]) * jnp.sum(sp, axis=1, keepdims=True)
        eye = lax.broadcasted_iota(jnp.int32, (HEADS_PER_GROUP, HEADS_PER_GROUP), 0) == lax.broadcasted_iota(
            jnp.int32, (HEADS_PER_GROUP, HEADS_PER_GROUP), 1)
        last_row = jnp.sum(jnp.where(eye, last_col, 0.0), axis=0, keepdims=True) + jnp.sum(tl, axis=0, keepdims=True)
        is_last = lax.broadcasted_iota(jnp.int32, (CHUNK, 1), 0) == CHUNK - 1
        d_cs = row_w + y_off - tl + jnp.where(is_last, last_row, 0.0)
        da_ref[...] = _dot_select(upper, d_cs, data=1) - _dot_select(upper, col_w, NT, data=1)
        dc_ref[...] = _dot(d_cb, bc) + _dot(dye, prev)
        db_ref[...] = _dot(d_cb, cc, TN) + _dot(xde, d_s)
        _decay_state(ds_scr, d_s, _dot(dye, cc, TN), cs)

    rev = lambda c: last - c
    xs = pl.BlockSpec((CHUNK, GROUP_COLS), lambda g, c: (rev(c), g))
    bsp = pl.BlockSpec((CHUNK, SSM_STATE), lambda g, c: (rev(c), D_INNER // SSM_STATE + g))
    csp = pl.BlockSpec((CHUNK, SSM_STATE), lambda g, c: (rev(c), (D_INNER + GN) // SSM_STATE + g))
    per_head = pl.BlockSpec((None, CHUNK, HEADS_PER_GROUP), lambda g, c: (g, rev(c), 0))
    dsk = pl.BlockSpec((None, 1, GROUP_COLS), lambda g, c: (g, 0, 0))
    dsum = pl.BlockSpec((None, 1, HEADS_PER_GROUP), lambda g, c: (g, 0, 0))
    st = pl.BlockSpec((None, GROUP_COLS, SSM_STATE), lambda g, c: (rev(c), g, 0))
    grp = pl.BlockSpec((CHUNK, SSM_STATE), lambda g, c: (rev(c), g))
    return _call(
        body, name=name, grid=(SSM_GROUPS, N_CHUNKS),
        in_specs=[xs, bsp, csp, per_head, per_head, dsk, st, xs],
        out_specs=[xs, grp, grp, per_head, per_head, dsum],
        out_shape=[jax.ShapeDtypeStruct((t_dim, D_INNER), F32), jax.ShapeDtypeStruct((t_dim, GN), F32),
                   jax.ShapeDtypeStruct((t_dim, GN), F32),
                   jax.ShapeDtypeStruct((SSM_GROUPS, t_dim, HEADS_PER_GROUP), F32),
                   jax.ShapeDtypeStruct((SSM_GROUPS, t_dim, HEADS_PER_GROUP), F32),
                   jax.ShapeDtypeStruct((SSM_GROUPS, 1, HEADS_PER_GROUP), F32)],
        scratch_shapes=[pltpu.VMEM((GROUP_COLS, SSM_STATE), F32)],
        sem=("parallel", "arbitrary"), args=[xbc, xbc, xbc, dtg, ag, dgx, states, dy], comm=comm)


NORM_GROUP = D_INNER // SSM_GROUPS


def _gate_norm_fwd(y, zx, nw, *, name, tm=256):
    t_dim = y.shape[0]
    row = pl.BlockSpec((tm, D_INNER), lambda i: (i, 0))

    def body(y_ref, z_ref, nw_ref, o_ref):
        z = z_ref[...]
        yz = y_ref[...] * (z * _sigmoid(z))
        for g in range(SSM_GROUPS):
            cols = slice(g * NORM_GROUP, (g + 1) * NORM_GROUP)
            yhat, _ = _rms(yz[:, cols])
            o_ref[:, cols] = (yhat * nw_ref[:, cols]).astype(BF16)

    return pl.pallas_call(
        body, name=name, grid=(t_dim // tm,), in_specs=[row, row, pl.BlockSpec((1, D_INNER), lambda i: (0, 0))],
        out_specs=row, out_shape=jax.ShapeDtypeStruct((t_dim, D_INNER), BF16),
        compiler_params=_params("parallel"),
    )(y, zx, nw)


def _gate_norm_bwd(y, zx, nw, dyn, *, name, tm=256):
    t_dim = y.shape[0]
    row = pl.BlockSpec((tm, D_INNER), lambda i: (i, 0))
    vec = pl.BlockSpec((1, D_INNER), lambda i: (0, 0))

    def body(y_ref, z_ref, nw_ref, dyn_ref, dy_ref, dz_ref, dnw_ref):
        @pl.when(pl.program_id(0) == 0)
        def _():
            dnw_ref[...] = jnp.zeros_like(dnw_ref)

        z = z_ref[...]
        yv = y_ref[...]
        sg = _sigmoid(z)
        silu_z = z * sg
        yz = yv * silu_z
        dyn_v = dyn_ref[...]
        for g in range(SSM_GROUPS):
            cols = slice(g * NORM_GROUP, (g + 1) * NORM_GROUP)
            yhat, r = _rms(yz[:, cols])
            dn = dyn_v[:, cols]
            dnw_ref[:, cols] += jnp.sum(dn * yhat, axis=0, keepdims=True)
            dyhat = dn * nw_ref[:, cols]
            dyz = r * (dyhat - yhat * jnp.mean(dyhat * yhat, axis=-1, keepdims=True))
            dy_ref[:, cols] = dyz * silu_z[:, cols]
            dz_ref[:, cols] = dyz * yv[:, cols] * _dsilu(z[:, cols], sg[:, cols])

    return pl.pallas_call(
        body, name=name, grid=(t_dim // tm,), in_specs=[row, row, vec, row],
        out_specs=[row, row, vec],
        out_shape=[jax.ShapeDtypeStruct((t_dim, D_INNER), F32), jax.ShapeDtypeStruct((t_dim, ZX_DIM), F32),
                   jax.ShapeDtypeStruct((1, D_INNER), F32)],
        compiler_params=_params("arbitrary"),
    )(y, zx, nw, dyn)


HEADS_PER_LANE_TILE = LANES // ATT_HEAD_DIM
STACKED_ROWS = Q_PER_KV * WINDOW


def _att_half_masks():
    lane = lax.broadcasted_iota(jnp.int32, (WINDOW, LANES), 1)
    return [(lane >= i * ATT_HEAD_DIM) & (lane < (i + 1) * ATT_HEAD_DIM) for i in range(HEADS_PER_LANE_TILE)]


def _att_stack_heads(ref, kvh, masks):
    parts = []
    for g in range(Q_PER_KV):
        h = kvh * Q_PER_KV + g
        blk = ref[:, (h // HEADS_PER_LANE_TILE) * LANES:(h // HEADS_PER_LANE_TILE + 1) * LANES]
        parts.append(jnp.where(masks[h % HEADS_PER_LANE_TILE], blk, jnp.zeros_like(blk)))
    return jnp.concatenate(parts, axis=0)


def _att_kv_tile(ref, kvh, masks):
    blk = ref[:, (kvh // HEADS_PER_LANE_TILE) * LANES:(kvh // HEADS_PER_LANE_TILE + 1) * LANES]
    return jnp.where(masks[kvh % HEADS_PER_LANE_TILE], blk, pltpu.roll(blk, ATT_HEAD_DIM, 1)).astype(BF16)


def _att_stacked_masks(n):
    row = lax.bitwise_and(lax.broadcasted_iota(jnp.int32, (STACKED_ROWS, WINDOW), 0), WINDOW - 1)
    col = lax.broadcasted_iota(jnp.int32, (STACKED_ROWS, WINDOW), 1)
    return col <= row, (col > row) & (n > 0)


def _att_stack_columns(ref, kvh, rows):
    cols = [ref[:, kvh * Q_PER_KV + g:kvh * Q_PER_KV + g + 1] for g in range(Q_PER_KV)]
    return jnp.concatenate([jnp.broadcast_to(c, (rows, 1)) for c in cols], axis=0)


def _att_scores(q4, k_tile, mask):
    return jnp.where(mask, _dot(q4, k_tile, NT) * ATT_SCALE, NEG_BIG)


def _att_unstack(x4, kvh, masks, tiles):
    for g in range(Q_PER_KV):
        h = kvh * Q_PER_KV + g
        piece = x4[g * WINDOW:(g + 1) * WINDOW]
        t = h // HEADS_PER_LANE_TILE
        tiles[t] = piece if h % HEADS_PER_LANE_TILE == 0 else jnp.where(masks[1], piece, tiles[t])


def _attn_fwd(q, k, v, sinks, *, name, comm=None):
    t_dim = q.shape[0]

    def body(q_ref, kc_ref, kp_ref, vc_ref, vp_ref, s_ref, o_ref, l_ref):
        n = pl.program_id(0)
        masks = _att_half_masks()
        mask_c, mask_p = _att_stacked_masks(n)
        out_tiles = [None] * (D_MODEL // LANES)
        for kvh in range(N_KV_HEADS):
            q4 = _att_stack_heads(q_ref, kvh, masks).astype(BF16)
            kc, kp = _att_kv_tile(kc_ref, kvh, masks), _att_kv_tile(kp_ref, kvh, masks)
            vc, vp = _att_kv_tile(vc_ref, kvh, masks), _att_kv_tile(vp_ref, kvh, masks)
            sc = _att_scores(q4, kc, mask_c)
            sp = _att_scores(q4, kp, mask_p)
            sink = _att_stack_columns(s_ref, kvh, WINDOW)
            m = jnp.maximum(jnp.maximum(jnp.max(sc, axis=1, keepdims=True), jnp.max(sp, axis=1, keepdims=True)), sink)
            pc = jnp.exp(sc - m)
            pp = jnp.exp(sp - m)
            den = jnp.sum(pc, axis=1, keepdims=True) + jnp.sum(pp, axis=1, keepdims=True) + jnp.exp(sink - m)
            _att_unstack((_dot(pc, vc) + _dot(pp, vp)) / den, kvh, masks, out_tiles)
            lse4 = m + jnp.log(den)
            for g in range(Q_PER_KV):
                h = kvh * Q_PER_KV + g
                l_ref[:, h:h + 1] = lse4[g * WINDOW:(g + 1) * WINDOW]
        for t, tile in enumerate(out_tiles):
            o_ref[:, t * LANES:(t + 1) * LANES] = tile

    cur = lambda w: pl.BlockSpec((WINDOW, w), lambda n: (n, 0))
    prv = lambda w: pl.BlockSpec((WINDOW, w), lambda n: (jnp.maximum(n - 1, 0), 0))
    return _call(
        body, name=name, grid=(t_dim // WINDOW,),
        in_specs=[cur(D_MODEL), cur(KV_DIM), prv(KV_DIM), cur(KV_DIM), prv(KV_DIM), pl.BlockSpec((1, N_Q_HEADS), lambda n: (0, 0))],
        out_specs=[cur(D_MODEL), cur(N_Q_HEADS)],
        out_shape=[jax.ShapeDtypeStruct((t_dim, D_MODEL), F32), jax.ShapeDtypeStruct((t_dim, N_Q_HEADS), F32)],
        sem=("parallel",), args=[q, k, k, v, v, sinks], comm=comm)


def _attn_bwd(q, k, v, sinks, o, lse, do, cos2, sin2, *, name, comm=None):
    t_dim = q.shape[0]

    def body(q_ref, kc_ref, kp_ref, vc_ref, vp_ref, s_ref, o_ref, l_ref, do_ref, cos_ref, sin_ref, cos_all_ref, sin_all_ref,
             dq_ref, dk_ref, dv_ref, dsink_ref):
        n = pl.program_id(0)

        @pl.when(n == 0)
        def _():
            dk_ref[...] = jnp.zeros_like(dk_ref)
            dv_ref[...] = jnp.zeros_like(dv_ref)
            dsink_ref[...] = jnp.zeros_like(dsink_ref)

        masks = _att_half_masks()
        mask_c, mask_p = _att_stacked_masks(n)
        lane_row = lax.broadcasted_iota(jnp.int32, (1, N_Q_HEADS), 1)
        rows_c = pl.ds(pl.multiple_of(n * WINDOW, WINDOW), WINDOW)
        rows_p = pl.ds(pl.multiple_of(jnp.maximum(n - 1, 0) * WINDOW, WINDOW), WINDOW)
        dsink = jnp.zeros((1, N_Q_HEADS), F32)
        dq_tiles = [None] * (D_MODEL // LANES)
        kv_tiles = KV_DIM // LANES
        dkc_tiles, dkp_tiles, dvc_tiles, dvp_tiles = ([None] * kv_tiles for _ in range(4))

        def place(tiles, kvh, x):
            folded = x + pltpu.roll(x, ATT_HEAD_DIM, 1)
            t = kvh // HEADS_PER_LANE_TILE
            tiles[t] = folded if kvh % HEADS_PER_LANE_TILE == 0 else jnp.where(masks[1], folded, tiles[t])

        for kvh in range(N_KV_HEADS):
            q4 = _att_stack_heads(q_ref, kvh, masks).astype(BF16)
            do4 = _att_stack_heads(do_ref, kvh, masks)
            o4 = _att_stack_heads(o_ref, kvh, masks)
            kc, kp = _att_kv_tile(kc_ref, kvh, masks), _att_kv_tile(kp_ref, kvh, masks)
            vc, vp = _att_kv_tile(vc_ref, kvh, masks), _att_kv_tile(vp_ref, kvh, masks)
            l4 = _att_stack_columns(l_ref, kvh, WINDOW)
            pc = jnp.exp(_att_scores(q4, kc, mask_c) - l4)
            pp = jnp.exp(_att_scores(q4, kp, mask_p) - l4)
            delta = jnp.sum(do4 * o4, axis=1, keepdims=True)
            do4b = do4.astype(BF16)
            dsc = pc * (_dot(do4b, vc, NT) - delta)
            dsp = pp * (_dot(do4b, vp, NT) - delta)
            _att_unstack((_dot(dsc, kc) + _dot(dsp, kp)) * ATT_SCALE, kvh, masks, dq_tiles)
            place(dkc_tiles, kvh, _dot(dsc, q4, TN) * ATT_SCALE)
            place(dkp_tiles, kvh, _dot(dsp, q4, TN) * ATT_SCALE)
            place(dvc_tiles, kvh, _dot(pc, do4b, TN))
            place(dvp_tiles, kvh, _dot(pp, do4b, TN))
            p_sink = jnp.exp(_att_stack_columns(s_ref, kvh, WINDOW) - l4) * delta
            for g in range(Q_PER_KV):
                h = kvh * Q_PER_KV + g
                dsink = jnp.where(lane_row == h, -jnp.sum(p_sink[g * WINDOW:(g + 1) * WINDOW], axis=0, keepdims=True), dsink)
        for t, tile in enumerate(dq_tiles):
            dq_ref[:, t * LANES:(t + 1) * LANES] = _rope_rotate(tile, cos_ref[...], -sin_ref[...])
        for t in range(kv_tiles):
            lanes = slice(t * LANES, (t + 1) * LANES)
            dk_ref[rows_c, lanes] += dkc_tiles[t]
            dk_ref[rows_p, lanes] += dkp_tiles[t]
            dv_ref[rows_c, lanes] += dvc_tiles[t]
            dv_ref[rows_p, lanes] += dvp_tiles[t]
        dsink_ref[...] += dsink

        @pl.when(n == t_dim // WINDOW - 1)
        def _():
            dk_ref[...] = _rope_rotate(dk_ref[...], cos_all_ref[...], -sin_all_ref[...])

    cur = lambda w: pl.BlockSpec((WINDOW, w), lambda n: (n, 0))
    prv = lambda w: pl.BlockSpec((WINDOW, w), lambda n: (jnp.maximum(n - 1, 0), 0))
    whole = lambda w: pl.BlockSpec((t_dim, w), lambda n: (0, 0))
    svec = pl.BlockSpec((1, N_Q_HEADS), lambda n: (0, 0))
    return _call(
        body, name=name, grid=(t_dim // WINDOW,),
        in_specs=[cur(D_MODEL), cur(KV_DIM), prv(KV_DIM), cur(KV_DIM), prv(KV_DIM), svec, cur(D_MODEL), cur(N_Q_HEADS), cur(D_MODEL),
                  cur(LANES), cur(LANES), whole(LANES), whole(LANES)],
        out_specs=[cur(D_MODEL), whole(KV_DIM), whole(KV_DIM), svec],
        out_shape=[jax.ShapeDtypeStruct((t_dim, D_MODEL), F32), jax.ShapeDtypeStruct((t_dim, KV_DIM), F32),
                   jax.ShapeDtypeStruct((t_dim, KV_DIM), F32), jax.ShapeDtypeStruct((1, N_Q_HEADS), F32)],
        sem=("arbitrary",), args=[q, k, k, v, v, sinks, o, lse, do, cos2, sin2, cos2, sin2], comm=comm)


def _loss_head(x, nw, target, *, name, tm=512):
    t_dim, d_dim = x.shape
    row = pl.BlockSpec((tm, d_dim), lambda i: (i, 0))
    vec = pl.BlockSpec((1, d_dim), lambda i: (0, 0))

    def body(x_ref, nw_ref, tgt_ref, loss_ref, dx_ref, dnw_ref, dob_ref):
        @pl.when(pl.program_id(0) == 0)
        def _():
            loss_ref[...] = jnp.zeros_like(loss_ref)
            dnw_ref[...] = jnp.zeros_like(dnw_ref)

        xhat, r = _rms(x_ref[...])
        err = xhat * nw_ref[...] - tgt_ref[...]
        loss_ref[...] += 0.5 * _sum_all(jnp.mean(err * err, axis=-1, keepdims=True))
        dy = err * (1.0 / d_dim)
        dnw_ref[...] += jnp.sum(dy * xhat, axis=0, keepdims=True)
        dxhat = dy * nw_ref[...]
        dx = r * (dxhat - xhat * jnp.mean(dxhat * xhat, axis=-1, keepdims=True))
        dx_ref[...] = dx
        dob_ref[...] = (FFN_RES_WEIGHT * dx).astype(BF16)

    return pl.pallas_call(
        body, name=name, grid=(t_dim // tm,), in_specs=[row, vec, row],
        out_specs=[pl.BlockSpec((1, 1), lambda i: (0, 0)), row, vec, row],
        out_shape=[jax.ShapeDtypeStruct((1, 1), F32), jax.ShapeDtypeStruct((t_dim, d_dim), F32),
                   jax.ShapeDtypeStruct((1, d_dim), F32), jax.ShapeDtypeStruct((t_dim, d_dim), BF16)],
        compiler_params=_params("arbitrary"),
    )(x, nw, target)


def _rope_tables():
    pos = jnp.arange(SEQ, dtype=F32)
    inv = 1.0 / (ROPE_THETA ** (jnp.arange(0, ATT_HEAD_DIM, 2, dtype=F32) / ATT_HEAD_DIM))
    ang = pos[:, None] * inv[None, :]
    cos, sin = jnp.cos(ang), jnp.sin(ang)
    return jnp.tile(cos, (1, 4)), jnp.tile(sin, (1, 4))


def _to_groups(t):
    return t.reshape(t.shape[0], SSM_GROUPS, HEADS_PER_GROUP).transpose(1, 0, 2)


def _from_groups(t):
    return t.transpose(1, 0, 2).reshape(t.shape[1], SSM_HEADS)


def _forward_backward(x0, target, net):
    w = net.w
    nw = [[w("norm_w")[l, i][None, :] for i in range(3)] for l in range(2)]
    cos2, sin2 = _rope_tables()
    ffn_norm = [nw[0][0], nw[0][2], nw[1][0], nw[1][2]]

    ffn_pre = {}

    def ffn_f(x, blk):
        name = f"ffn_fwd{blk}"
        out, *ffn_pre[blk] = _ffn_fwd(x, ffn_norm[blk], w(f"gate{blk}"), w(f"up{blk}"), w(f"down{blk}"), name=name,
                                      comm=net.carry(name))
        return out

    x1 = ffn_f(x0, 0)
    zx, h1 = _norm_mm(x1, nw[0][1], w("w_in_t"), None, w_rows=ZX_DIM, name="ssm_in_proj", comm=net.carry("ssm_in_proj"))
    dtr = _mm(h1, w("w_in_t"), dims="nt", b_rows=(ZX_DIM, SSM_HEADS), name="ssm_dt_proj")
    xbc = _conv_fwd(zx, w("conv_w"), w("conv_b"), name="ssm_conv_fwd", comm=net.carry("ssm_conv_fwd"))
    dt, a_dt = _dt_prep(dtr, w("dt_bias"), w("a_log"), name="ssm_dt_prep")
    dtg, ag = _to_groups(dt), _to_groups(a_dt)
    dg = jnp.repeat(w("d_skip").reshape(SSM_GROUPS, 1, HEADS_PER_GROUP), SSM_HEAD_DIM, axis=2)
    y_ssd, states = _ssd_fwd(xbc, dtg, ag, dg, name="ssd_fwd", comm=net.carry("ssd_fwd"))
    yn = _gate_norm_fwd(y_ssd, zx, w("ssm_norm_w"), name="ssm_gate_norm_fwd")
    x2 = _mm(yn, w("wout"), res=x1, name="ssm_out_proj", comm=net.carry("ssm_out_proj"))
    x3 = ffn_f(x2, 1)
    k_rot, hk = _norm_mm(x3, w("kv_norm_w"), w("wk"), w("b_k"), rope=(cos2, sin2), name="k_proj")
    v = _mm(hk, w("wv"), bias=w("b_v"), name="v_proj")
    x4 = ffn_f(x3, 2)
    q_rot, h4 = _norm_mm(x4, nw[1][1], w("wq"), w("b_q"), rope=(cos2, sin2), name="q_proj")
    att, lse = _attn_fwd(q_rot, k_rot, v, w("sinks"), name="attn_fwd", comm=net.carry("attn_fwd"))
    x5 = _mm(att, w("wo"), bias=w("b_o"), res=x4, name="attn_out_proj")
    x6 = ffn_f(x5, 3)
    loss, dx6, d_final, dob6 = _loss_head(x6, w("final_norm_w"), target, name="loss_head")

    d_norm = [[None] * 3 for _ in range(2)]

    def ffn_b(x, dout, dob, blk):
        pre_g, pre_u, h = ffn_pre[blk]
        name = f"ffn_bwd{blk}"
        dh, gg, gu, gd = _ffn_bwd(h, dob, pre_g, pre_u, w(f"gate{blk}"), w(f"up{blk}"), w(f"down{blk}"), name=name,
                                  comm=net.carry(name))
        net.give(f"gate{blk}", gg)
        net.give(f"up{blk}", gu)
        net.give(f"down{blk}", gd)
        return _norm_bwd(x, ffn_norm[blk], dh, [dout], name=f"ffn_norm_bwd{blk}", comm=net.carry(f"ffn_norm_bwd{blk}"))

    by_rows = lambda g: g.reshape(N_DEV, g.shape[0] // N_DEV, g.shape[1])
    dx5, d_norm[1][2] = ffn_b(x5, dx6, dob6, 3)
    d_att = _mm(dx5, w("wo"), dims="nt", name="attn_out_proj_dx", comm=net.carry("attn_out_proj_dx"))
    g_o, d_bo = _mm(att, dx5, dims="tn", out_dtype=BF16, colsum_b=True, name="attn_out_proj_dw")
    net.give("w_o", by_rows(g_o))
    dq, dk, dv, d_sinks = _attn_bwd(q_rot, k_rot, v, w("sinks"), att, lse, d_att, cos2, sin2, name="attn_bwd",
                                    comm=net.carry("attn_bwd"))
    dx4, d_norm[1][1], dob4 = _mm_norm_bwd(dq, w("wq"), x4, nw[1][1], [dx5], dims="nt", name="q_proj_dx")
    g_q, d_bq = _mm(h4, dq, dims="tn", out_dtype=BF16, colsum_b=True, name="q_proj_dw")
    net.give("w_q", by_rows(g_q))
    dx3a, d_norm[1][0] = ffn_b(x3, dx4, dob4, 2)
    dhk = _mm(dk, w("wk"), dims="nt", name="k_proj_dx", comm=net.carry("k_proj_dx"))
    dx3, d_kvn, dob3 = _mm_norm_bwd(dv, w("wv"), x3, w("kv_norm_w"), [dx3a], dims="nt", add=dhk, name="v_proj_dx")
    g_k, d_bk = _mm(hk, dk, dims="tn", out_dtype=BF16, colsum_b=True, name="k_proj_dw")
    g_v, d_bv = _mm(hk, dv, dims="tn", out_dtype=BF16, colsum_b=True, name="v_proj_dw")
    net.give("w_k", by_rows(g_k))
    net.give("w_v", by_rows(g_v))
    dx2, d_norm[0][2] = ffn_b(x2, dx3, dob3, 1)
    d_yn = _mm(dx2, w("wout"), dims="nt", name="ssm_out_proj_dx", comm=net.carry("ssm_out_proj_dx"))
    net.give("w_out", by_rows(_mm(yn, dx2, dims="tn", out_dtype=BF16, name="ssm_out_proj_dw")))
    dy_ssd, dzx, d_ssm_norm = _gate_norm_bwd(y_ssd, zx, w("ssm_norm_w"), d_yn, name="ssm_gate_norm_bwd")
    dxs, d_b, d_c, ddtg, dag, ddg = _ssd_bwd(xbc, dtg, ag, dg, states, dy_ssd, name="ssd_bwd", comm=net.carry("ssd_bwd"))
    dzx, d_conv_w, d_conv_b = _conv_bwd(zx, w("conv_w"), w("conv_b"), dxs, d_b, d_c, dzx, name="ssm_conv_bwd",
                                        comm=net.carry("ssm_conv_bwd"))
    ddtr, d_dt_bias, d_a_log = _dt_bwd(dtr, w("dt_bias"), w("a_log"), dt, _from_groups(ddtg), _from_groups(dag), name="ssm_dt_bwd")
    dh1 = _mm(dzx, w("w_in_t"), b_rows=(0, ZX_DIM), name="ssm_in_proj_dx")
    in_rows = N_DEV * IN_PROJ_SHARD
    g_in = _mm(dzx, h1, dims="tn", out_dtype=BF16, out_window=(0, in_rows), name="ssm_in_proj_dw")
    g_in = _mm(ddtr, h1, dims="tn", out_dtype=BF16, out_window=(ZX_DIM, in_rows), into=g_in, name="ssm_dt_proj_dw")
    net.give("w_in", g_in.reshape(N_DEV, IN_PROJ_SHARD, D_MODEL))
    dx1, d_norm[0][1], dob1 = _mm_norm_bwd(ddtr, w("w_in_t"), x1, nw[0][1], [dx2], b_rows=(ZX_DIM, SSM_HEADS), add=dh1,
                                           name="ssm_dt_proj_dx", comm=net.carry("ssm_norm_bwd"))
    dx0, d_norm[0][0] = ffn_b(x0, dx1, dob1, 0)

    small = {"norm_w": jnp.concatenate([d_norm[l][i] for l in range(2) for i in range(3)], axis=0),
             "ssm_conv_w": d_conv_w, "ssm_conv_b": d_conv_b, "ssm_dt_bias": d_dt_bias, "ssm_a_log": d_a_log,
             "ssm_d": ddg.reshape(1, SSM_HEADS), "ssm_norm_w": d_ssm_norm, "kv_norm_w": d_kvn,
             "b_k": d_bk, "b_v": d_bv, "attn_b_q": d_bq, "attn_sinks": d_sinks, "attn_b_o": d_bo, "final_norm_w": d_final}
    return loss, dx0, small


BLOCK_BYTES = 1 << 20


def _row_tile(rows, cols):
    for t in (512, 256, 128, 64, 32, 16):
        if rows % t == 0 and t * cols * 4 <= BLOCK_BYTES:
            return t
    return rows


def _cast_bf16(x, *, name):
    n_blk, rows, cols = x.shape
    tm = rows if rows * cols * 4 <= 2 * BLOCK_BYTES else _row_tile(rows, cols)
    spec = pl.BlockSpec((None, tm, cols), lambda b, i: (b, i, 0))

    def body(x_ref, o_ref):
        o_ref[...] = x_ref[...].astype(BF16)

    return pl.pallas_call(body, name=name, grid=(n_blk, rows // tm), in_specs=[spec], out_specs=spec,
                          out_shape=jax.ShapeDtypeStruct(x.shape, BF16), compiler_params=_params("parallel", "parallel"))(x)


def _pair_add(grad, theirs, *, name):
    n_slots, rows, cols = theirs.shape
    tm = rows if rows * cols * 4 <= 2 * BLOCK_BYTES else _row_tile(rows, cols)

    def body(g_ref, t_ref, o_ref):
        o_ref[...] = (g_ref[...].astype(F32) + t_ref[...].astype(F32)).astype(BF16)

    spec = pl.BlockSpec((None, tm, cols), lambda s, i: (s, i, 0))
    return pl.pallas_call(
        body, name=name, grid=(n_slots, rows // tm),
        in_specs=[pl.BlockSpec((None, tm, cols), lambda s, i: (2 * s + lax.axis_index("c"), i, 0)), spec], out_specs=spec,
        out_shape=jax.ShapeDtypeStruct(theirs.shape, BF16), compiler_params=_params("parallel", "parallel"),
    )(grad, theirs)


def _adam_update(g, w, m, v):
    m = ADAM_B1 * m + (1.0 - ADAM_B1) * g
    v = ADAM_B2 * v + (1.0 - ADAM_B2) * (g * g)
    m_hat = m / (1.0 - ADAM_B1 ** ADAM_STEP)
    v_hat = v / (1.0 - ADAM_B2 ** ADAM_STEP)
    delta = -ADAM_LR * (m_hat / (jnp.sqrt(v_hat) + ADAM_EPS) + ADAM_WD * w)
    return delta, m, v


def _adamw(parts, w, m, v, first_blk, prev, *, name, comm=None):
    n_blk, rows, cols = w.shape
    tm = _row_tile(rows, cols)
    n_tiles = rows // tm
    spec = pl.BlockSpec((None, tm, cols), lambda b, i: (first_blk + b, i, 0))
    n_prev, n_here = len(prev), len(parts)
    n_parts = parts[0].shape[0]

    def part_spec(q):
        return pl.BlockSpec((n_parts, tm, cols), lambda b, i: (0, jnp.where(b < q, 0, jnp.where(b == q, i, n_tiles - 1)), 0))

    def body(*refs):
        p_refs = refs[:n_here]
        w_ref, m_ref, v_ref = refs[n_here:n_here + 3]
        g_ref, d_ref, nm_ref, nv_ref = refs[n_here + 3 + n_prev:]
        b = pl.program_id(0)
        g = None
        for s in range(n_parts):
            t = p_refs[0][s]
            for q in range(1, n_here):
                t = jnp.where(b == q, p_refs[q][s], t)
            g = t.astype(F32) if g is None else g + t.astype(F32)
        delta, nm, nv = _adam_update(g, w_ref[...], m_ref[...], v_ref[...])
        g_ref[...] = g
        d_ref[...] = delta
        nm_ref[...] = nm
        nv_ref[...] = nv

    return _call(
        body, name=name, grid=(n_here, n_tiles),
        in_specs=[part_spec(q) for q in range(n_here)] + [spec, spec, spec] + [pl.BlockSpec(memory_space=pl.ANY)] * n_prev,
        out_specs=[spec] * 4, out_shape=[jax.ShapeDtypeStruct((n_blk, rows, cols), F32)] * 4,
        aliases={n_here + 3 + q: q for q in range(n_prev)}, sem=("arbitrary", "arbitrary"),
        args=[*parts, w, m, v, *prev], comm=comm)


def _sum_parts(parts, *, name):
    def body(p_ref, o_ref):
        g = p_ref[0]
        for s in range(1, N_DEV):
            g = g + p_ref[s]
        o_ref[...] = g

    return pl.pallas_call(body, name=name, out_shape=jax.ShapeDtypeStruct(parts.shape[1:], F32), compiler_params=_params())(parts)


def _adamw_packed(g, w, m, v, *, name):
    def body(g_ref, w_ref, m_ref, v_ref, d_ref, nm_ref, nv_ref):
        delta, nm, nv = _adam_update(g_ref[...], w_ref[...], m_ref[...], v_ref[...])
        d_ref[...] = delta
        nm_ref[...] = nm
        nv_ref[...] = nv

    return pl.pallas_call(body, name=name, out_shape=[jax.ShapeDtypeStruct(g.shape, F32)] * 3, compiler_params=_params())(g, w, m, v)


SUBLANES = 8


WIDE_PACK = 1024


def _pack(arrs, width=LANES):
    rows = []
    for a in arrs:
        a2 = a.reshape(-1, a.shape[-1])
        a2 = jnp.pad(a2, ((0, 0), (0, (-a2.shape[1]) % width)))
        rows += [a2[:, i * width:(i + 1) * width] for i in range(a2.shape[1] // width)]
    out = jnp.concatenate(rows, axis=0)
    return jnp.pad(out, ((0, (-out.shape[0]) % SUBLANES), (0, 0)))


def _unpack(packed, shapes, width=LANES):
    outs, r = [], 0
    for shp in shapes:
        lead, cols = math.prod(shp[:-1]), shp[-1]
        n_blocks = -(-cols // width)
        blocks = [packed[r + i * lead:r + (i + 1) * lead] for i in range(n_blocks)]
        outs.append(jnp.concatenate(blocks, axis=1)[:, :cols].reshape(shp))
        r += n_blocks * lead
    return outs


WEIGHT_NAMES = ("norm_w", "ffn_w_gate", "ffn_w_up", "ffn_w_down", "ssm_w_in", "ssm_conv_w", "ssm_conv_b", "ssm_dt_bias",
                "ssm_a_log", "ssm_d", "ssm_norm_w", "ssm_w_out", "kv_norm_w", "w_k", "b_k", "w_v", "b_v", "attn_w_q",
                "attn_b_q", "attn_sinks", "attn_w_o", "attn_b_o", "final_norm_w")
MATRIX_NAMES = ("ffn_w_gate", "ffn_w_up", "ffn_w_down", "ssm_w_in", "ssm_w_out", "w_k", "w_v", "attn_w_q", "attn_w_o")
VECTOR_NAMES = tuple(n for n in WEIGHT_NAMES if n not in MATRIX_NAMES)
SHARDED_VECTORS = ("norm_w", "ssm_conv_w", "ssm_conv_b", "ssm_norm_w")


GATHER_PLAN = {
    "gather_stage0": ("gate0", "up0", "down0", "vec"),
    "ffn_fwd0": ("w_in",),
    "ssm_in_proj": ("w_out", "gate1"),
    "ssm_conv_fwd": ("w_k", "w_v", "up1"),
    "ssd_fwd": ("down1", "gate2", "up2"),
    "ssm_out_proj": ("w_q", "w_o"),
    "ffn_fwd1": ("down2", "gate3"),
    "ffn_fwd2": ("up3",),
    "attn_fwd": ("down3",),
}
PAIR_PLAN = {
    "attn_out_proj_dx": ("gate3", "up3", "down3"),
    "ffn_bwd2": ("w_q", "w_o"),
    "k_proj_dx": ("gate2", "up2", "down2"),
    "ssm_out_proj_dx": ("w_k", "w_v", "gate1", "up1", "down1"),
    "ssd_bwd": ("w_out",),
    "ssm_norm_bwd": ("w_in",),
    "ffn_norm_bwd0": ("gate0", "up0", "down0"),
}
CHIP_PLAN = {
    "attn_bwd": ("gate3", "up3"),
    "ffn_bwd2": ("down3",),
    "ffn_bwd1": ("gate2", "up2", "w_q", "w_o"),
    "ssd_bwd": ("down2", "gate1", "up1", "down1", "w_k", "w_v"),
    "ssm_conv_bwd": ("w_out",),
    "ffn_bwd0": ("w_in",),
    "adamw_gate": ("gate0",),
    "adamw_up": ("up0",),
    "adamw_down": ("down0",),
}
FFN_PARAMS = {"gate": "ffn_w_gate", "up": "ffn_w_up", "down": "ffn_w_down"}
SINGLE_MATRICES = {"w_in": "ssm_w_in", "w_out": "ssm_w_out", "w_k": "w_k", "w_v": "w_v", "w_q": "attn_w_q", "w_o": "attn_w_o"}


TRANSPOSED = ("ffn_w_gate", "ffn_w_up", "ssm_w_in")


def _matrix_view(name, a):
    if name in TRANSPOSED:
        a = jnp.swapaxes(a, -1, -2)
    return a.reshape((-1,) + a.shape[-2:])


def _from_matrix_view(name, a, shape):
    if name in TRANSPOSED:
        return jnp.swapaxes(a.reshape(shape[:-2] + (shape[-1], shape[-2])), -1, -2)
    return a.reshape(shape)


class _MeshNet:
    def __init__(self, p):
        self.p = p
        self.views = {n: _matrix_view(n, p[n]) for n in MATRIX_NAMES}
        self.local = {"vec": _pack([p[n] for n in SHARDED_VECTORS])}
        for short, n in FFN_PARAMS.items():
            cast = _cast_bf16(self.views[n], name=f"cast_{short}")
            self.local.update({f"{short}{k}": (cast, k) for k in range(N_FFN)})
        for short, n in SINGLE_MATRICES.items():
            self.local[short] = (_cast_bf16(self.views[n], name=f"cast_{short}"), 0)
        self.gathered_at, self.pairs_at, self.parts_at, self.grads, self.cache = {}, {}, {}, {}, {}

    def carry(self, name):
        comms = []
        if name in GATHER_PLAN:
            keys, comm = GATHER_PLAN[name], _Gather([self.local[k] for k in GATHER_PLAN[name]])
            self.gathered_at.update({k: (comm, i) for i, k in enumerate(keys)})
            comms.append(comm)
        if name in CHIP_PLAN:
            sums = []
            for k in CHIP_PLAN[name]:
                comm, i = self.pairs_at[k]
                sums.append(_pair_add(self.grads[k], comm.results[i], name=f"pair_add_{k}"))
            comm = _ChipExchange(sums)
            self.parts_at.update({k: (comm, i) for i, k in enumerate(CHIP_PLAN[name])})
            comms.append(comm)
        if name in PAIR_PLAN:
            keys, comm = PAIR_PLAN[name], _PairSwap([self.grads[k] for k in PAIR_PLAN[name]])
            self.pairs_at.update({k: (comm, i) for i, k in enumerate(keys)})
            comms.append(comm)
        return comms

    def run(self, name):
        for comm in self.carry(name):
            _run_exchange(comm, name=name)

    def give(self, key, grad):
        self.grads[key] = grad

    def parts(self, key):
        comm, i = self.parts_at[key]
        return comm.results[i]

    def _gathered(self, key):
        comm, i = self.gathered_at[key]
        return comm.results[i]

    def _vec(self, r0, lead, n_blocks):
        vecs = self._gathered("vec")
        return jnp.concatenate([vecs[d, r0 + i * lead:r0 + (i + 1) * lead, :] for d in range(N_DEV) for i in range(n_blocks)], axis=1)

    def _derive(self, name):
        p = self.p
        if name[:-1] in FFN_PARAMS:
            return self._gathered(name)
        if name == "w_in_t":
            return self._gathered("w_in").reshape(N_DEV * IN_PROJ_SHARD, D_MODEL)
        by_rows = {"wout": "w_out", "wk": "w_k", "wv": "w_v", "wq": "w_q", "wo": "w_o"}
        if name in by_rows:
            g = self._gathered(by_rows[name])
            return g.reshape(N_DEV * g.shape[1], g.shape[2])
        vectors = {"norm_w": lambda: self._vec(0, 6, 1).reshape(2, 3, D_MODEL), "conv_w": lambda: self._vec(6, CONV_WIDTH, 3),
                   "conv_b": lambda: self._vec(18, 1, 3), "ssm_norm_w": lambda: self._vec(21, 1, 2)}
        if name in vectors:
            return vectors[name]()
        replicated = {"dt_bias": p["ssm_dt_bias"], "a_log": p["ssm_a_log"], "d_skip": p["ssm_d"], "kv_norm_w": p["kv_norm_w"][None],
                      "b_k": p["b_k"][None], "b_v": p["b_v"][None], "b_q": p["attn_b_q"], "sinks": p["attn_sinks"],
                      "b_o": p["attn_b_o"], "final_norm_w": p["final_norm_w"][None]}
        return replicated[name]

    def w(self, name):
        if name not in self.cache:
            self.cache[name] = self._derive(name)
        return self.cache[name]


def _step(x, target, p, m, v):
    pos = _slot(_position())
    net = _MeshNet(p)
    net.run("gather_stage0")
    loss, grad_x, small = _forward_backward(x, target, net)

    grads, deltas, new_m, new_v = {}, {}, {}, {}
    view = lambda d, n: _matrix_view(n, d[n])
    vec_gather = _Gather([_pack([small[n] for n in VECTOR_NAMES], WIDE_PACK)])
    for short, n in SINGLE_MATRICES.items():
        outs = _adamw([net.parts(short)], net.views[n], view(m, n), view(v, n), 0, [], name=f"adamw_{short}",
                      comm=[vec_gather] if short == "w_in" else None)
        grads[n], deltas[n], new_m[n], new_v[n] = [_from_matrix_view(n, o, p[n].shape) for o in outs]
    ffn_outs = {}
    for short, n in FFN_PARAMS.items():
        ffn_outs[short] = _adamw([net.parts(f"{short}{k}") for k in range(1, N_FFN)], net.views[n], view(m, n), view(v, n), 1, [],
                                 name=f"adamw_{short}", comm=net.carry(f"adamw_{short}"))
    for short, n in FFN_PARAMS.items():
        outs = _adamw([net.parts(f"{short}0")], net.views[n], view(m, n), view(v, n), 0, ffn_outs[short], name=f"adamw_{short}0")
        grads[n], deltas[n], new_m[n], new_v[n] = [_from_matrix_view(n, o, p[n].shape) for o in outs]
    vec_sum = _sum_parts(vec_gather.results[0], name="sum_vector_grads")
    full_shapes = {"norm_w": (2, 3, D_MODEL), "ssm_conv_w": (1, CONV_WIDTH, CONV_DIM), "ssm_conv_b": (1, CONV_DIM),
                   "ssm_norm_w": (1, D_INNER)}
    vec_full = dict(zip(VECTOR_NAMES, _unpack(vec_sum, [full_shapes.get(n, p[n].shape) for n in VECTOR_NAMES], WIDE_PACK)))
    for n in VECTOR_NAMES:
        g = vec_full[n]
        if n in SHARDED_VECTORS:
            per = p[n].shape[-1]
            g = lax.dynamic_slice_in_dim(g, pos * per, per, axis=g.ndim - 1)
        grads[n] = g
    packed = _adamw_packed(*[_pack([d[n] for n in VECTOR_NAMES], WIDE_PACK) for d in (grads, p, m, v)], name="adamw_vectors")
    shapes = [p[n].shape for n in VECTOR_NAMES]
    for d, pk in zip((deltas, new_m, new_v), packed):
        d.update(zip(VECTOR_NAMES, _unpack(pk, shapes, WIDE_PACK)))
    return loss, grad_x, grads, deltas, new_m, new_v


def kernel(x, norm_w, ffn_w_gate, ffn_w_up, ffn_w_down, ssm_w_in, ssm_conv_w, ssm_conv_b, ssm_dt_bias, ssm_a_log, ssm_d, ssm_norm_w, ssm_w_out, kv_norm_w, w_k, b_k, w_v, b_v, attn_w_q, attn_b_q, attn_sinks, attn_w_o, attn_b_o, final_norm_w, loss_target, m_norm_w, m_ffn_w_gate, m_ffn_w_up, m_ffn_w_down, m_ssm_w_in, m_ssm_conv_w, m_ssm_conv_b, m_ssm_dt_bias, m_ssm_a_log, m_ssm_d, m_ssm_norm_w, m_ssm_w_out, m_kv_norm_w, m_w_k, m_b_k, m_w_v, m_b_v, m_attn_w_q, m_attn_b_q, m_attn_sinks, m_attn_w_o, m_attn_b_o, m_final_norm_w, v_norm_w, v_ffn_w_gate, v_ffn_w_up, v_ffn_w_down, v_ssm_w_in, v_ssm_conv_w, v_ssm_conv_b, v_ssm_dt_bias, v_ssm_a_log, v_ssm_d, v_ssm_norm_w, v_ssm_w_out, v_kv_norm_w, v_w_k, v_b_k, v_w_v, v_b_v, v_attn_w_q, v_attn_b_q, v_attn_sinks, v_attn_w_o, v_attn_b_o, v_final_norm_w):
    p = dict(zip(WEIGHT_NAMES, (norm_w, ffn_w_gate, ffn_w_up, ffn_w_down, ssm_w_in, ssm_conv_w, ssm_conv_b, ssm_dt_bias, ssm_a_log, ssm_d, ssm_norm_w, ssm_w_out, kv_norm_w, w_k, b_k, w_v, b_v, attn_w_q, attn_b_q, attn_sinks, attn_w_o, attn_b_o, final_norm_w)))
    m = dict(zip(WEIGHT_NAMES, (m_norm_w, m_ffn_w_gate, m_ffn_w_up, m_ffn_w_down, m_ssm_w_in, m_ssm_conv_w, m_ssm_conv_b, m_ssm_dt_bias, m_ssm_a_log, m_ssm_d, m_ssm_norm_w, m_ssm_w_out, m_kv_norm_w, m_w_k, m_b_k, m_w_v, m_b_v, m_attn_w_q, m_attn_b_q, m_attn_sinks, m_attn_w_o, m_attn_b_o, m_final_norm_w)))
    v = dict(zip(WEIGHT_NAMES, (v_norm_w, v_ffn_w_gate, v_ffn_w_up, v_ffn_w_down, v_ssm_w_in, v_ssm_conv_w, v_ssm_conv_b, v_ssm_dt_bias, v_ssm_a_log, v_ssm_d, v_ssm_norm_w, v_ssm_w_out, v_kv_norm_w, v_w_k, v_b_k, v_w_v, v_b_v, v_attn_w_q, v_attn_b_q, v_attn_sinks, v_attn_w_o, v_attn_b_o, v_final_norm_w)))
    loss, grad_x, grads, deltas, new_m, new_v = _step(x[0], loss_target[0], p, m, v)
    loss = lax.psum(loss[0, 0], ("x", "y", "c"))
    return (loss, grad_x[None], *[grads[n] for n in WEIGHT_NAMES], *[deltas[n] for n in WEIGHT_NAMES],
            *[new_m[n] for n in WEIGHT_NAMES], *[new_v[n] for n in WEIGHT_NAMES])
```

```python
import functools
import math

import jax
import jax.numpy as jnp
from jax import lax
from jax.experimental import pallas as pl
from jax.experimental.pallas import tpu as pltpu

F32 = jnp.float32
BF16 = jnp.bfloat16

N_DEV = 8
SEQ = 2048
D_MODEL = 1024
D_FF_SHARD = 352
N_FFN = 4
D_INNER = 2048
SSM_HEADS = 32
SSM_HEAD_DIM = 64
SSM_GROUPS = 4
HEADS_PER_GROUP = 8
SSM_STATE = 128
CHUNK = 128
N_CHUNKS = SEQ // CHUNK
GN = SSM_GROUPS * SSM_STATE
CONV_DIM = D_INNER + 2 * GN
CONV_WIDTH = 4
ZX_DIM = D_INNER + CONV_DIM
IN_PROJ_SHARD = 644
ATT_HEAD_DIM = 64
N_Q_HEADS = 16
N_KV_HEADS = 4
Q_PER_KV = 4
KV_DIM = N_KV_HEADS * ATT_HEAD_DIM
WINDOW = 128
ROPE_THETA = 10000.0
EPS = 1e-5
FFN_RES_WEIGHT = 0.5
ATT_SCALE = 1.0 / math.sqrt(ATT_HEAD_DIM)
NEG_BIG = -1e30

ADAM_LR = 0.001
ADAM_B1 = 0.9
ADAM_B2 = 0.999
ADAM_EPS = 1e-08
ADAM_WD = 0.01
ADAM_STEP = 10

VMEM_LIMIT_BYTES = 56 * 1024 * 1024
FFN_BWD_VMEM_LIMIT_BYTES = 61 * 1024 * 1024

NN = (((1,), (0,)), ((), ()))
NT = (((1,), (1,)), ((), ()))
TN = (((0,), (0,)), ((), ()))
_DIMS = {"nn": NN, "nt": NT, "tn": TN}


def _params(*sem):
    return pltpu.CompilerParams(dimension_semantics=sem if sem else None, vmem_limit_bytes=VMEM_LIMIT_BYTES)


def _dot(a, b, dims=NN):
    return lax.dot_general(a.astype(BF16), b.astype(BF16), dims, preferred_element_type=F32)


def _sigmoid(x):
    return 1.0 / (1.0 + jnp.exp(-x))


def _dsilu(x, s):
    return s * (1.0 + x * (1.0 - s))


def _rms(x):
    r = lax.rsqrt(jnp.mean(x * x, axis=-1, keepdims=True) + EPS)
    return x * r, r


def _sum_all(x):
    return jnp.sum(jnp.sum(x, axis=1, keepdims=True), axis=0, keepdims=True)


MESH = pl.DeviceIdType.MESH
N_PEERS = N_DEV - 1
N_CHIPS = N_DEV // 2


def _position():
    return lax.axis_index("x"), lax.axis_index("y"), lax.axis_index("c")


def _slot(p):
    return 4 * p[0] + 2 * p[1] + p[2]


class _Exchange:
    def __init__(self, arrays, out_shapes):
        n = len(arrays)
        self.arrays = list(arrays)
        self.out_shapes = out_shapes
        self.scratch = [pltpu.SemaphoreType.DMA((n, N_PEERS)), pltpu.SemaphoreType.DMA((n, N_PEERS)), pltpu.SemaphoreType.DMA((n,))]
        self.results = None

    def relay(self, ins, outs, sems):
        pass


class _Gather(_Exchange):
    def __init__(self, pieces):
        pieces = [p if isinstance(p, tuple) else (p, None) for p in pieces]
        self.blocks = [k for _, k in pieces]
        shapes = [a.shape if k is None else a.shape[1:] for a, k in pieces]
        super().__init__([a for a, _ in pieces], [jax.ShapeDtypeStruct((N_DEV,) + s, a.dtype) for s, (a, _) in zip(shapes, pieces)])

    def _plan(self, ins, outs, sems):
        send_sems, recv_sems, local_sems = sems
        x, y, c = _position()
        me, sibling = (x, y, c), (x, y, 1 - c)
        chips = [(1 - x, y), (x, 1 - y), (1 - x, 1 - y)]
        n = len(ins)
        ins = [r if k is None else r.at[k] for r, k in zip(ins, self.blocks)]

        def copy(a, k, block, to, src=None):
            dst = outs[a].at[_slot(block)]
            return pltpu.make_async_remote_copy(src_ref=dst if src is None else src, dst_ref=dst, send_sem=send_sems.at[a, k],
                                                recv_sem=recv_sems.at[a, k], device_id=to, device_id_type=MESH)

        mine = [pltpu.make_async_copy(ins[a], outs[a].at[_slot(me)], local_sems.at[a]) for a in range(n)]
        first = []
        for a in range(n):
            first.append(copy(a, 0, me, sibling, src=ins[a]))
            first += [copy(a, 1 + j, me, (*chip, c), src=ins[a]) for j, chip in enumerate(chips)]
        return n, c, me, sibling, chips, copy, mine, first

    def start(self, ins, outs, sems):
        _, _, _, _, _, _, mine, first = self._plan(ins, outs, sems)
        for cp in mine + first:
            cp.start()

    def relay(self, ins, outs, sems):
        n, c, me, sibling, chips, copy, _, _ = self._plan(ins, outs, sems)
        for j, chip in enumerate(chips):
            for a in range(n):
                copy(a, 1 + j, (*chip, c), me).wait_recv()
                copy(a, 4 + j, (*chip, c), sibling).start()

    def finish(self, ins, outs, sems):
        n, c, me, sibling, chips, copy, mine, first = self._plan(ins, outs, sems)
        passed = [copy(a, 4 + j, (*chip, c), sibling) for j, chip in enumerate(chips) for a in range(n)]
        for a in range(n):
            copy(a, 0, sibling, me).wait_recv()
            for j, chip in enumerate(chips):
                copy(a, 4 + j, (*chip, 1 - c), me).wait_recv()
        for cp in first + passed:
            cp.wait_send()
        for cp in mine:
            cp.wait()


class _PairSwap(_Exchange):
    def __init__(self, arrays):
        n = len(arrays)
        self.arrays = list(arrays)
        self.out_shapes = [jax.ShapeDtypeStruct((N_CHIPS,) + a.shape[1:], a.dtype) for a in arrays]
        self.scratch = [pltpu.SemaphoreType.DMA((n, N_CHIPS)), pltpu.SemaphoreType.DMA((n, N_CHIPS))]
        self.results = None

    def _plan(self, ins, outs, sems):
        send_sems, recv_sems = sems
        x, y, c = _position()
        return [pltpu.make_async_remote_copy(src_ref=ins[a].at[2 * q + 1 - c], dst_ref=outs[a].at[q], send_sem=send_sems.at[a, q],
                                             recv_sem=recv_sems.at[a, q], device_id=(x, y, 1 - c), device_id_type=MESH)
                for a in range(len(ins)) for q in range(N_CHIPS)]

    def start(self, ins, outs, sems):
        for cp in self._plan(ins, outs, sems):
            cp.start()

    def finish(self, ins, outs, sems):
        for cp in self._plan(ins, outs, sems):
            cp.wait()


class _ChipExchange(_Exchange):
    def __init__(self, arrays):
        n = len(arrays)
        self.arrays = list(arrays)
        self.out_shapes = [jax.ShapeDtypeStruct(a.shape, a.dtype) for a in arrays]
        self.scratch = [pltpu.SemaphoreType.DMA((n, 3)), pltpu.SemaphoreType.DMA((n, 3)), pltpu.SemaphoreType.DMA((n,))]
        self.results = None

    def _plan(self, ins, outs, sems):
        send_sems, recv_sems, local_sems = sems
        x, y, c = _position()
        here = 2 * x + y
        chips = [(1 - x, y), (x, 1 - y), (1 - x, 1 - y)]
        n = len(ins)

        def copy(a, k, src_slot, dst_slot):
            return pltpu.make_async_remote_copy(src_ref=ins[a].at[src_slot], dst_ref=outs[a].at[dst_slot], send_sem=send_sems.at[a, k],
                                                recv_sem=recv_sems.at[a, k], device_id=(*chips[k], c), device_id_type=MESH)

        there = [2 * qx + qy for qx, qy in chips]
        mine = [pltpu.make_async_copy(ins[a].at[here], outs[a].at[here], local_sems.at[a]) for a in range(n)]
        sends = [copy(a, k, there[k], here) for a in range(n) for k in range(3)]
        arrivals = lambda: [copy(a, k, here, there[k]) for a in range(n) for k in range(3)]
        return mine, sends, arrivals

    def start(self, ins, outs, sems):
        mine, sends, _ = self._plan(ins, outs, sems)
        for cp in mine + sends:
            cp.start()

    def finish(self, ins, outs, sems):
        mine, sends, arrivals = self._plan(ins, outs, sems)
        for cp in arrivals():
            cp.wait_recv()
        for cp in sends:
            cp.wait_send()
        for cp in mine:
            cp.wait()


def _call(body, *, name, grid, in_specs, out_specs, out_shape, args, scratch_shapes=(), sem=(), comm=(), aliases=None,
          vmem_limit=VMEM_LIMIT_BYTES):
    single = not isinstance(out_shape, (list, tuple))
    out_shape = [out_shape] if single else list(out_shape)
    out_specs = [out_specs] if single else list(out_specs)
    comms = list(comm or ())
    n_in, n_out, n_scr = len(args), len(out_shape), len(scratch_shapes)
    params = pltpu.CompilerParams(dimension_semantics=tuple(sem) if sem else None, vmem_limit_bytes=vmem_limit)
    if not comms:
        res = pl.pallas_call(body, name=name, grid=grid, in_specs=list(in_specs), out_specs=out_specs, out_shape=out_shape,
                             scratch_shapes=list(scratch_shapes), input_output_aliases=aliases or {}, compiler_params=params)(*args)
        return res[0] if single else res
    counts = [n_in] + [len(c.arrays) for c in comms] + [n_out] + [len(c.out_shapes) for c in comms] + [n_scr] + [len(c.scratch) for c in comms]
    nc = len(comms)

    def carried(*refs):
        pos, groups = 0, []
        for cnt in counts:
            groups.append(refs[pos:pos + cnt])
            pos += cnt
        ins, c_ins = groups[0], groups[1:1 + nc]
        outs, c_outs = groups[1 + nc], groups[2 + nc:2 + 2 * nc]
        scr, c_sems = groups[2 + 2 * nc], groups[3 + 2 * nc:]
        ids = [pl.program_id(d) for d in range(len(grid))]
        is_first = functools.reduce(jnp.logical_and, [i == 0 for i in ids])
        is_last = functools.reduce(jnp.logical_and, [i == g - 1 for i, g in zip(ids, grid)])

        @pl.when(is_first)
        def _():
            for q, c in enumerate(comms):
                c.start(c_ins[q], c_outs[q], c_sems[q])

        if math.prod(grid) > 1:
            @pl.when(is_last)
            def _():
                for q, c in enumerate(comms):
                    c.relay(c_ins[q], c_outs[q], c_sems[q])

        body(*ins, *outs, *scr)

        @pl.when(is_last)
        def _():
            for q, c in enumerate(comms):
                if math.prod(grid) == 1:
                    c.relay(c_ins[q], c_outs[q], c_sems[q])
                c.finish(c_ins[q], c_outs[q], c_sems[q])

    anyspec = pl.BlockSpec(memory_space=pl.ANY)
    c_arrays = [a for c in comms for a in c.arrays]
    c_shapes = [s for c in comms for s in c.out_shapes]
    res = pl.pallas_call(
        carried, name=name, grid=grid, in_specs=list(in_specs) + [anyspec] * len(c_arrays), out_specs=out_specs + [anyspec] * len(c_shapes),
        out_shape=out_shape + c_shapes, scratch_shapes=list(scratch_shapes) + [s for c in comms for s in c.scratch],
        input_output_aliases=aliases or {}, compiler_params=params)(*args, *c_arrays)
    pos = n_out
    for c in comms:
        c.results = list(res[pos:pos + len(c.out_shapes)])
        pos += len(c.out_shapes)
    return res[0] if single else list(res[:n_out])


def _run_exchange(comm, *, name):
    def body(*refs):
        n_ci, n_co = len(comm.arrays), len(comm.out_shapes)
        ins, outs, sems = refs[:n_ci], refs[n_ci:n_ci + n_co], refs[n_ci + n_co:]
        comm.start(ins, outs, sems)
        comm.relay(ins, outs, sems)
        comm.finish(ins, outs, sems)

    anyspec = pl.BlockSpec(memory_space=pl.ANY)
    comm.results = list(pl.pallas_call(
        body, name=name, in_specs=[anyspec] * len(comm.arrays), out_specs=[anyspec] * len(comm.out_shapes),
        out_shape=list(comm.out_shapes), scratch_shapes=list(comm.scratch))(*comm.arrays))
    return comm.results


def _mm(a, b, *, dims="nn", bias=None, res=None, out_dtype=F32, name, tm=1024, tn=1024, tk=1024, comm=None, b_rows=None,
        out_window=None, into=None, colsum_b=False):
    if dims == "tn":
        k_dim, m_dim = a.shape
    else:
        m_dim, k_dim = a.shape
    row0, n_rows = b_rows if b_rows is not None else (0, b.shape[0])
    n_dim = n_rows if dims == "nt" else b.shape[1]
    assert dims == "nt" or n_rows == k_dim, (name, a.shape, b.shape, b_rows)
    tm, tn, tk = min(tm, m_dim), min(tn, n_dim), min(tk, k_dim)
    assert m_dim % tm == 0 and n_dim % tn == 0 and k_dim % tk == 0, (name, a.shape, b.shape)
    nk = k_dim // tk
    a_spec = pl.BlockSpec((tk, tm), lambda i, j, k: (k, i)) if dims == "tn" else pl.BlockSpec((tm, tk), lambda i, j, k: (i, k))
    if dims == "nt":
        assert row0 % tn == 0
        b_spec = pl.BlockSpec((tn, tk), lambda i, j, k: (row0 // tn + j, k))
    else:
        assert row0 % tk == 0
        b_spec = pl.BlockSpec((tk, tn), lambda i, j, k: (row0 // tk + k, j))
    in_specs, args = [a_spec, b_spec], [a, b]
    if bias is not None:
        in_specs.append(pl.BlockSpec((1, tn), lambda i, j, k: (0, j)))
        args.append(bias)
    if res is not None:
        in_specs.append(pl.BlockSpec((tm, tn), lambda i, j, k: (i, j)))
        args.append(res)
    dn = _DIMS[dims]

    if colsum_b:
        assert dims == "tn" and m_dim == tm and into is None and out_window is None

    def body(*refs):
        a_ref, b_ref = refs[0], refs[1]
        acc_ref = refs[-1]
        o_ref = refs[-3] if colsum_b else refs[-2]
        k = pl.program_id(2)

        @pl.when(k == 0)
        def _():
            acc_ref[...] = jnp.zeros_like(acc_ref)
            if colsum_b:
                refs[-2][...] = jnp.zeros_like(refs[-2])

        acc_ref[...] += _dot(a_ref[...], b_ref[...], dn)
        if colsum_b:
            refs[-2][...] += jnp.sum(b_ref[...].astype(F32), axis=0, keepdims=True)

        @pl.when(k == nk - 1)
        def _():
            r = acc_ref[...]
            pos = 2
            if bias is not None:
                r = r + refs[pos][...]
                pos += 1
            if res is not None:
                r = r + refs[pos][...]
            o_ref[...] = r.astype(out_dtype)

    out_row0, out_rows = out_window if out_window is not None else (0, m_dim)
    assert out_row0 % tm == 0
    aliases = None
    if into is not None:
        assert into.shape == (out_rows, n_dim) and into.dtype == out_dtype
        in_specs.append(pl.BlockSpec(memory_space=pl.ANY))
        args.append(into)
        aliases = {len(args) - 1: 0}
    out_spec = pl.BlockSpec((tm, tn), lambda i, j, k: (out_row0 // tm + i, j))
    out_shape = jax.ShapeDtypeStruct((out_rows, n_dim), out_dtype)
    if colsum_b:
        out_spec = [out_spec, pl.BlockSpec((1, tn), lambda i, j, k: (0, j))]
        out_shape = [out_shape, jax.ShapeDtypeStruct((1, n_dim), F32)]
    return _call(
        body, name=name, grid=(m_dim // tm, n_dim // tn, nk), in_specs=in_specs, out_specs=out_spec, out_shape=out_shape,
        aliases=aliases, scratch_shapes=[pltpu.VMEM((tm, tn), F32)], sem=("parallel", "parallel", "arbitrary"), args=args, comm=comm)


def _mm_norm_bwd(a, b, x, nw, res, *, dims="nn", b_rows=None, add=None, name, tm=1024, tk=1024, comm=None):
    m_dim, k_dim = a.shape
    row0, n_rows = b_rows if b_rows is not None else (0, b.shape[0])
    d_dim = x.shape[1]
    tm, tk = min(tm, m_dim), min(tk, k_dim)
    assert m_dim % tm == 0 and k_dim % tk == 0 and (n_rows if dims == "nt" else b.shape[1]) == d_dim, (name, a.shape, b.shape)
    nk = k_dim // tk
    if dims == "nt":
        assert row0 % d_dim == 0
        b_spec = pl.BlockSpec((d_dim, tk), lambda i, k: (row0 // d_dim, k))
    else:
        assert row0 % tk == 0 and n_rows == k_dim
        b_spec = pl.BlockSpec((tk, d_dim), lambda i, k: (row0 // tk + k, 0))
    row = pl.BlockSpec((tm, d_dim), lambda i, k: (i, 0))
    vec = pl.BlockSpec((1, d_dim), lambda i, k: (0, 0))
    extra = ([add] if add is not None else []) + list(res)
    dn = _DIMS[dims]

    def body(*refs):
        a_ref, b_ref, x_ref, nw_ref = refs[:4]
        extra_refs = refs[4:4 + len(extra)]
        dx_ref, dnw_ref, dob_ref, acc_ref = refs[-4:]
        i, k = pl.program_id(0), pl.program_id(1)

        @pl.when(k == 0)
        def _():
            acc_ref[...] = jnp.zeros_like(acc_ref)

        @pl.when((i == 0) & (k == 0))
        def _():
            dnw_ref[...] = jnp.zeros_like(dnw_ref)

        acc_ref[...] += _dot(a_ref[...], b_ref[...], dn)

        @pl.when(k == nk - 1)
        def _():
            dh = acc_ref[...]
            rest = list(extra_refs)
            if add is not None:
                dh = dh + rest.pop(0)[...]
            xhat, r = _rms(x_ref[...])
            dxhat = dh * nw_ref[...]
            dx = r * (dxhat - xhat * jnp.mean(dxhat * xhat, axis=-1, keepdims=True))
            for rr in rest:
                dx = dx + rr[...]
            dx_ref[...] = dx
            dob_ref[...] = (FFN_RES_WEIGHT * dx).astype(BF16)
            dnw_ref[...] += jnp.sum(dh * xhat, axis=0, keepdims=True)

    return _call(
        body, name=name, grid=(m_dim // tm, nk),
        in_specs=[pl.BlockSpec((tm, tk), lambda i, k: (i, k)), b_spec, row, vec] + [row] * len(extra), out_specs=[row, vec, row],
        out_shape=[jax.ShapeDtypeStruct((m_dim, d_dim), F32), jax.ShapeDtypeStruct((1, d_dim), F32),
                   jax.ShapeDtypeStruct((m_dim, d_dim), BF16)],
        scratch_shapes=[pltpu.VMEM((tm, d_dim), F32)], sem=("arbitrary", "arbitrary"), args=[a, b, x, nw] + extra, comm=comm)


def _rope_rotate(x, cos_t, sin_t):
    rows, width = x.shape
    half = ATT_HEAD_DIM // 2
    lane = lax.broadcasted_iota(jnp.int32, (rows, width), 1)
    first = (lane % ATT_HEAD_DIM) < half
    rot = jnp.where(first, -pltpu.roll(x, width - half, 1), pltpu.roll(x, half, 1))
    reps = width // 128
    return x * jnp.tile(cos_t, (1, reps)) + rot * jnp.tile(sin_t, (1, reps))


def _norm_mm(x, nw, w, bias, *, name, tm=1024, tn=1024, comm=None, w_rows=None, rope=None):
    t_dim, d_dim = x.shape
    transposed = w_rows is not None
    n_dim = w_rows if transposed else w.shape[1]
    tn = min(tn, n_dim)
    assert t_dim % tm == 0 and n_dim % tn == 0
    has_bias = bias is not None
    w_spec = pl.BlockSpec((tn, d_dim), lambda i, j: (j, 0)) if transposed else pl.BlockSpec((d_dim, tn), lambda i, j: (0, j))
    dn = NT if transposed else NN
    in_specs = [pl.BlockSpec((tm, d_dim), lambda i, j: (i, 0)), pl.BlockSpec((1, d_dim), lambda i, j: (0, 0)), w_spec]
    args = [x, nw, w]
    if has_bias:
        in_specs.append(pl.BlockSpec((1, tn), lambda i, j: (0, j)))
        args.append(bias)
    if rope is not None:
        in_specs += [pl.BlockSpec((tm, LANES), lambda i, j: (i, 0))] * 2
        args += list(rope)

    def body(*refs):
        x_ref, nw_ref, w_ref = refs[:3]
        o_ref, h_ref = refs[-2], refs[-1]

        @pl.when(pl.program_id(1) == 0)
        def _():
            xhat, _ = _rms(x_ref[...])
            h_ref[...] = (xhat * nw_ref[...]).astype(BF16)

        r = _dot(h_ref[...], w_ref[...], dn)
        if has_bias:
            r = r + refs[3][...]
        if rope is not None:
            r = _rope_rotate(r, refs[-4][...], refs[-3][...])
        o_ref[...] = r

    return _call(
        body, name=name, grid=(t_dim // tm, n_dim // tn), in_specs=in_specs,
        out_specs=[pl.BlockSpec((tm, tn), lambda i, j: (i, j)), pl.BlockSpec((tm, d_dim), lambda i, j: (i, 0))],
        out_shape=[jax.ShapeDtypeStruct((t_dim, n_dim), F32), jax.ShapeDtypeStruct((t_dim, d_dim), BF16)],
        sem=("parallel", "arbitrary"), args=args, comm=comm)


def _norm_bwd(x, nw, dh, res, *, name, tm=512, comm=None):
    t_dim, d_dim = x.shape
    n_res = len(res)
    row = pl.BlockSpec((tm, d_dim), lambda i: (i, 0))
    vec = pl.BlockSpec((1, d_dim), lambda i: (0, 0))

    def body(*refs):
        x_ref, nw_ref, dh_ref = refs[:3]
        dx_ref, dnw_ref = refs[-2], refs[-1]
        xhat, r = _rms(x_ref[...])
        dh = dh_ref[...]
        dxhat = dh * nw_ref[...]
        dx = r * (dxhat - xhat * jnp.mean(dxhat * xhat, axis=-1, keepdims=True))
        for rr in refs[3:3 + n_res]:
            dx = dx + rr[...]
        dx_ref[...] = dx

        @pl.when(pl.program_id(0) == 0)
        def _():
            dnw_ref[...] = jnp.zeros_like(dnw_ref)

        dnw_ref[...] += jnp.sum(dh * xhat, axis=0, keepdims=True)

    return _call(
        body, name=name, grid=(t_dim // tm,), in_specs=[row, vec, row] + [row] * n_res,
        out_specs=[row, vec],
        out_shape=[jax.ShapeDtypeStruct((t_dim, d_dim), F32), jax.ShapeDtypeStruct((1, d_dim), F32)],
        sem=("arbitrary",), args=[x, nw, dh, *res], comm=comm)


FFN_ROW_TILE = 512
FFN_SHARDS_PER_STEP = 2
FFN_STEPS = N_DEV // FFN_SHARDS_PER_STEP
FFN_STEP_COLS = FFN_SHARDS_PER_STEP * D_FF_SHARD


def _ffn_step_view(w):
    return w.reshape(FFN_STEPS, FFN_STEP_COLS, w.shape[-1])


def _ffn_specs(t_dim, d_dim):
    full = pl.BlockSpec((t_dim, d_dim), lambda j: (0, 0))
    wspec = pl.BlockSpec((None, FFN_STEP_COLS, d_dim), lambda j: (j, 0, 0))
    pre = pl.BlockSpec((None, t_dim, FFN_STEP_COLS), lambda j: (j, 0, 0))
    return full, wspec, pre


def _ffn_fwd(x, nw, wg, wu, wd, *, name, comm=None):
    t_dim, d_dim = x.shape
    n_tiles = t_dim // FFN_ROW_TILE

    def body(x_ref, nw_ref, wg_ref, wu_ref, wd_ref, o_ref, g_ref, u_ref, h_ref):
        j = pl.program_id(0)

        @pl.when(j == 0)
        def _():
            xhat, _ = _rms(x_ref[...])
            h_ref[...] = (xhat * nw_ref[...]).astype(BF16)
            o_ref[...] = jnp.zeros_like(o_ref)

        for t in range(n_tiles):
            rows = pl.ds(t * FFN_ROW_TILE, FFN_ROW_TILE)
            h = h_ref[rows, :]
            g = _dot(h, wg_ref[...], NT)
            u = _dot(h, wu_ref[...], NT)
            g_ref[rows, :] = g.astype(BF16)
            u_ref[rows, :] = u.astype(BF16)
            o_ref[rows, :] += _dot(g * _sigmoid(g) * u, wd_ref[...])

        @pl.when(j == FFN_STEPS - 1)
        def _():
            o_ref[...] = x_ref[...] + FFN_RES_WEIGHT * o_ref[...]

    full, wspec, pre = _ffn_specs(t_dim, d_dim)
    pre_shape = jax.ShapeDtypeStruct((FFN_STEPS, t_dim, FFN_STEP_COLS), BF16)
    return _call(
        body, name=name, grid=(FFN_STEPS,),
        in_specs=[full, pl.BlockSpec((1, d_dim), lambda j: (0, 0)), wspec, wspec, wspec],
        out_specs=[full, pre, pre, full],
        out_shape=[jax.ShapeDtypeStruct((t_dim, d_dim), F32), pre_shape, pre_shape, jax.ShapeDtypeStruct((t_dim, d_dim), BF16)],
        sem=("arbitrary",), args=[x, nw, _ffn_step_view(wg), _ffn_step_view(wu), _ffn_step_view(wd)], comm=comm)


def _ffn_bwd(h, dob, pre_g, pre_u, wg, wu, wd, *, name, comm=None):
    t_dim, d_dim = h.shape
    n_tiles = t_dim // FFN_ROW_TILE

    def body(h_ref, dob_ref, g_ref, u_ref, wg_ref, wu_ref, wd_ref, dh_ref, gg_ref, gu_ref, gd_ref, dwg_scr, dwu_scr, dwd_scr):
        @pl.when(pl.program_id(0) == 0)
        def _():
            dh_ref[...] = jnp.zeros_like(dh_ref)

        for t in range(n_tiles):
            rows = pl.ds(t * FFN_ROW_TILE, FFN_ROW_TILE)
            hh = h_ref[rows, :]
            do = dob_ref[rows, :]
            g = g_ref[rows, :].astype(F32)
            u = u_ref[rows, :].astype(F32)
            sg = _sigmoid(g)
            s = g * sg
            da = _dot(do, wd_ref[...], NT)
            dwd = _dot(s * u, do, TN)
            du = (da * s).astype(BF16)
            dg = (da * u * _dsilu(g, sg)).astype(BF16)
            dwg = _dot(dg, hh, TN)
            dwu = _dot(du, hh, TN)
            if t == 0:
                dwd_scr[...] = dwd
                dwg_scr[...] = dwg
                dwu_scr[...] = dwu
            else:
                dwd_scr[...] += dwd
                dwg_scr[...] += dwg
                dwu_scr[...] += dwu
            dh_ref[rows, :] += _dot(dg, wg_ref[...]) + _dot(du, wu_ref[...])
        gg_ref[...] = dwg_scr[...].astype(BF16)
        gu_ref[...] = dwu_scr[...].astype(BF16)
        gd_ref[...] = dwd_scr[...].astype(BF16)

    full, wspec, pre = _ffn_specs(t_dim, d_dim)
    gspec = pl.BlockSpec((None, FFN_STEP_COLS, d_dim), lambda j: (j, 0, 0), pipeline_mode=pl.Buffered(1))
    grad_shape = jax.ShapeDtypeStruct((FFN_STEPS, FFN_STEP_COLS, d_dim), BF16)
    dh, gg, gu, gd = _call(
        body, name=name, grid=(FFN_STEPS,),
        in_specs=[full, full, pre, pre, wspec, wspec, wspec], out_specs=[full, gspec, gspec, gspec],
        out_shape=[jax.ShapeDtypeStruct((t_dim, d_dim), F32)] + [grad_shape] * 3,
        scratch_shapes=[pltpu.VMEM((FFN_STEP_COLS, d_dim), F32)] * 3, sem=("arbitrary",), vmem_limit=FFN_BWD_VMEM_LIMIT_BYTES,
        args=[h, dob, pre_g, pre_u, _ffn_step_view(wg), _ffn_step_view(wu), _ffn_step_view(wd)], comm=comm)
    return dh, gg.reshape(wg.shape), gu.reshape(wu.shape), gd.reshape(wd.shape)


CONV_COLS = 256


def _shift_down(u, s, rows):
    return jnp.where(rows >= s, pltpu.roll(u, s, 0), 0.0)


def _shift_up(u, s, rows, t_dim):
    return jnp.where(rows < t_dim - s, pltpu.roll(u, t_dim - s, 0), 0.0)


def _conv_pre(u, w_ref, b_ref, rows):
    c = b_ref[...] + w_ref[CONV_WIDTH - 1:CONV_WIDTH, :] * u
    for k in range(CONV_WIDTH - 1):
        c = c + w_ref[k:k + 1, :] * _shift_down(u, CONV_WIDTH - 1 - k, rows)
    return c


def _conv_fwd(zx, cw, cb, *, name, comm=None):
    t_dim = zx.shape[0]
    off = D_INNER // CONV_COLS

    def body(u_ref, w_ref, b_ref, o_ref):
        rows = lax.broadcasted_iota(jnp.int32, (t_dim, CONV_COLS), 0)
        c = _conv_pre(u_ref[...], w_ref, b_ref, rows)
        o_ref[...] = c * _sigmoid(c)

    return _call(
        body, name=name, grid=(CONV_DIM // CONV_COLS,),
        in_specs=[pl.BlockSpec((t_dim, CONV_COLS), lambda j: (0, off + j)),
                  pl.BlockSpec((CONV_WIDTH, CONV_COLS), lambda j: (0, j)), pl.BlockSpec((1, CONV_COLS), lambda j: (0, j))],
        out_specs=pl.BlockSpec((t_dim, CONV_COLS), lambda j: (0, j)),
        out_shape=jax.ShapeDtypeStruct((t_dim, CONV_DIM), F32), sem=("parallel",), args=[zx, cw, cb], comm=comm)


def _conv_bwd(zx, cw, cb, dxs, db, dc, dzx, *, name, comm=None):
    t_dim = zx.shape[0]
    off = D_INNER // CONV_COLS
    n_xs = D_INNER // CONV_COLS
    n_b = GN // CONV_COLS

    def body(u_ref, w_ref, b_ref, dxs_ref, db_ref, dc_ref, dzx_in, dzx_ref, dw_ref, dbias_ref):
        j = pl.program_id(0)
        rows = lax.broadcasted_iota(jnp.int32, (t_dim, CONV_COLS), 0)
        u = u_ref[...]
        c = _conv_pre(u, w_ref, b_ref, rows)
        d = jnp.where(j < n_xs, dxs_ref[...], jnp.where(j < n_xs + n_b, db_ref[...], dc_ref[...]))
        dcv = d * _dsilu(c, _sigmoid(c))
        dpre = w_ref[CONV_WIDTH - 1:CONV_WIDTH, :] * dcv
        dw_ref[CONV_WIDTH - 1:CONV_WIDTH, :] = jnp.sum(dcv * u, axis=0, keepdims=True)
        for k in range(CONV_WIDTH - 1):
            s = CONV_WIDTH - 1 - k
            dpre = dpre + w_ref[k:k + 1, :] * _shift_up(dcv, s, rows, t_dim)
            dw_ref[k:k + 1, :] = jnp.sum(dcv * _shift_down(u, s, rows), axis=0, keepdims=True)
        dzx_ref[...] = dpre
        dbias_ref[...] = jnp.sum(dcv, axis=0, keepdims=True)

    blk = lambda n: pl.BlockSpec((t_dim, CONV_COLS), n)
    return _call(
        body, name=name, grid=(CONV_DIM // CONV_COLS,),
        in_specs=[blk(lambda j: (0, off + j)), pl.BlockSpec((CONV_WIDTH, CONV_COLS), lambda j: (0, j)),
                  pl.BlockSpec((1, CONV_COLS), lambda j: (0, j)),
                  blk(lambda j: (0, jnp.minimum(j, n_xs - 1))),
                  blk(lambda j: (0, jnp.clip(j - n_xs, 0, n_b - 1))),
                  blk(lambda j: (0, jnp.clip(j - n_xs - n_b, 0, n_b - 1))),
                  pl.BlockSpec(memory_space=pl.ANY)],
        out_specs=[blk(lambda j: (0, off + j)), pl.BlockSpec((CONV_WIDTH, CONV_COLS), lambda j: (0, j)),
                   pl.BlockSpec((1, CONV_COLS), lambda j: (0, j))],
        out_shape=[jax.ShapeDtypeStruct(dzx.shape, F32), jax.ShapeDtypeStruct((CONV_WIDTH, CONV_DIM), F32),
                   jax.ShapeDtypeStruct((1, CONV_DIM), F32)],
        aliases={6: 0}, sem=("parallel",), args=[zx, cw, cb, dxs, db, dc, dzx], comm=comm)


def _softplus_parts(x):
    e = jnp.exp(-jnp.abs(x))
    u = 1.0 + e
    log1p_e = jnp.where(u == 1.0, e, jnp.log(u) * e / jnp.where(u == 1.0, 1.0, u - 1.0))
    return jnp.maximum(x, 0.0) + log1p_e


def _dt_prep(dtr, dt_bias, a_log, *, name):
    def body(dtr_ref, bias_ref, alog_ref, dt_ref, a_ref):
        dt = _softplus_parts(dtr_ref[...] + bias_ref[...])
        dt_ref[...] = dt
        a_ref[...] = dt * (-jnp.exp(alog_ref[...]))

    return pl.pallas_call(body, name=name, out_shape=[jax.ShapeDtypeStruct(dtr.shape, F32)] * 2,
                          compiler_params=_params())(dtr, dt_bias, a_log)


def _dt_bwd(dtr, dt_bias, a_log, dt, ddt, da, *, name):
    def body(dtr_ref, bias_ref, alog_ref, dt_ref, ddt_ref, da_ref, ddtr_ref, dbias_ref, dalog_ref):
        a_neg = -jnp.exp(alog_ref[...])
        da_v = da_ref[...]
        ddt_tot = ddt_ref[...] + da_v * a_neg
        ddtr = ddt_tot * _sigmoid(dtr_ref[...] + bias_ref[...])
        ddtr_ref[...] = ddtr
        dbias_ref[...] = jnp.sum(ddtr, axis=0, keepdims=True)
        dalog_ref[...] = jnp.sum(da_v * dt_ref[...], axis=0, keepdims=True) * a_neg

    return pl.pallas_call(
        body, name=name,
        out_shape=[jax.ShapeDtypeStruct(dtr.shape, F32), jax.ShapeDtypeStruct((1, SSM_HEADS), F32),
                   jax.ShapeDtypeStruct((1, SSM_HEADS), F32)],
        compiler_params=_params())(dtr, dt_bias, a_log, dt, ddt, da)


GROUP_COLS = HEADS_PER_GROUP * SSM_HEAD_DIM
LANES = 128
HEADS_PER_LANE_BLOCK = LANES // SSM_HEAD_DIM


def _split3(x):
    hi = x.astype(BF16)
    r1 = x - hi.astype(F32)
    mid = r1.astype(BF16)
    lo = (r1 - mid.astype(F32)).astype(BF16)
    return hi, mid, lo


def _dot_select(a, b, dims=NN, data=0):
    out = None
    for part in _split3(a if data == 0 else b):
        lhs, rhs = (part, b.astype(BF16)) if data == 0 else (a.astype(BF16), part)
        t = lax.dot_general(lhs, rhs, dims, preferred_element_type=F32)
        out = t if out is None else out + t
    return out


def _group_sums(vals, expand):
    out = _dot_select(jnp.concatenate(vals, axis=0), expand, NT)
    return [out[i * CHUNK:(i + 1) * CHUNK] for i in range(len(vals))]


def _ssd_chunk_common(a_ref, dt_ref, b_ref, c_ref):
    row = lax.broadcasted_iota(jnp.int32, (CHUNK, CHUNK), 0)
    col = lax.broadcasted_iota(jnp.int32, (CHUNK, CHUNK), 1)
    causal = col <= row
    lower = causal.astype(F32)
    upper = (col >= row).astype(F32)
    head = lax.broadcasted_iota(jnp.int32, (HEADS_PER_GROUP, GROUP_COLS), 0)
    lane = lax.broadcasted_iota(jnp.int32, (HEADS_PER_GROUP, GROUP_COLS), 1)
    expand = ((lane >= head * SSM_HEAD_DIM) & (lane < (head + 1) * SSM_HEAD_DIM)).astype(F32)
    a = a_ref[...]
    cs = _dot_select(lower, a, data=1)
    cs_row = _dot_select(a, upper, TN)
    cs_x = _dot_select(cs, expand)
    dt_x = _dot_select(dt_ref[...], expand)
    e_out_x = jnp.exp(cs_x)
    e_st_x = jnp.exp(cs_x[CHUNK - 1:CHUNK, :] - cs_x)
    bc = b_ref[...]
    cc = c_ref[...]
    cb = _dot(cc, bc, NT)
    return causal, upper, expand.astype(BF16), cs, cs_row, dt_x, e_out_x, e_st_x, bc, cc, cb


def _head_decay(causal, cs, cs_row, h):
    return jnp.exp(jnp.where(causal, cs[:, h:h + 1] - cs_row[h:h + 1, :], NEG_BIG))


def _lane_block_head_masks():
    lane = lax.broadcasted_iota(jnp.int32, (CHUNK, LANES), 1)
    return [(lane >= i * SSM_HEAD_DIM) & (lane < (i + 1) * SSM_HEAD_DIM) for i in range(HEADS_PER_LANE_BLOCK)]


def _decay_state(dst_ref, old, new, cs):
    for h in range(HEADS_PER_GROUP):
        rows = slice(h * SSM_HEAD_DIM, (h + 1) * SSM_HEAD_DIM)
        dst_ref[rows, :] = jnp.exp(cs[CHUNK - 1:CHUNK, h:h + 1]) * old[rows, :] + new[rows, :]


def _ssd_fwd(xbc, dtg, ag, dgx, *, name, comm=None):
    t_dim = xbc.shape[0]

    def body(xs_ref, b_ref, c_ref, dt_ref, a_ref, d_ref, y_ref, st_ref, s_scr):
        @pl.when(pl.program_id(1) == 0)
        def _():
            s_scr[...] = jnp.zeros_like(s_scr)

        causal, _, _, cs, cs_row, dt_x, e_out_x, e_st_x, bc, cc, cb = _ssd_chunk_common(a_ref, dt_ref, b_ref, c_ref)
        masks = _lane_block_head_masks()
        xs = xs_ref[...]
        xdt_x = xs * dt_x
        prev = s_scr[...]
        st_ref[...] = prev
        y_off = e_out_x * _dot(cc, prev, NT) + xs * d_ref[...]
        for blk in range(GROUP_COLS // LANES):
            lanes = slice(blk * LANES, (blk + 1) * LANES)
            x_b = xdt_x[:, lanes].astype(BF16)
            acc = y_off[:, lanes]
            for i in range(HEADS_PER_LANE_BLOCK):
                m = cb * _head_decay(causal, cs, cs_row, blk * HEADS_PER_LANE_BLOCK + i)
                acc = acc + _dot(m, jnp.where(masks[i], x_b, jnp.zeros_like(x_b)))
            y_ref[:, lanes] = acc
        _decay_state(s_scr, prev, _dot(xdt_x * e_st_x, bc, TN), cs)

    xs = pl.BlockSpec((CHUNK, GROUP_COLS), lambda g, c: (c, g))
    bsp = pl.BlockSpec((CHUNK, SSM_STATE), lambda g, c: (c, D_INNER // SSM_STATE + g))
    csp = pl.BlockSpec((CHUNK, SSM_STATE), lambda g, c: (c, (D_INNER + GN) // SSM_STATE + g))
    per_head = pl.BlockSpec((None, CHUNK, HEADS_PER_GROUP), lambda g, c: (g, c, 0))
    dsk = pl.BlockSpec((None, 1, GROUP_COLS), lambda g, c: (g, 0, 0))
    return _call(
        body, name=name, grid=(SSM_GROUPS, N_CHUNKS),
        in_specs=[xs, bsp, csp, per_head, per_head, dsk],
        out_specs=[xs, pl.BlockSpec((None, GROUP_COLS, SSM_STATE), lambda g, c: (c, g, 0))],
        out_shape=[jax.ShapeDtypeStruct((t_dim, D_INNER), F32),
                   jax.ShapeDtypeStruct((N_CHUNKS, D_INNER, SSM_STATE), F32)],
        scratch_shapes=[pltpu.VMEM((GROUP_COLS, SSM_STATE), F32)],
        sem=("parallel", "arbitrary"), args=[xbc, xbc, xbc, dtg, ag, dgx], comm=comm)


def _ssd_bwd(xbc, dtg, ag, dgx, states, dy, *, name, comm=None):
    t_dim = xbc.shape[0]
    last = N_CHUNKS - 1

    def body(xs_ref, b_ref, c_ref, dt_ref, a_ref, d_ref, st_ref, dy_ref,
             dxs_ref, db_ref, dc_ref, ddt_ref, da_ref, dd_ref, ds_scr):
        @pl.when(pl.program_id(1) == 0)
        def _():
            ds_scr[...] = jnp.zeros_like(ds_scr)
            dd_ref[...] = jnp.zeros_like(dd_ref)

        causal, upper, expand, cs, cs_row, dt_x, e_out_x, e_st_x, bc, cc, cb = _ssd_chunk_common(a_ref, dt_ref, b_ref, c_ref)
        masks = _lane_block_head_masks()
        xs = xs_ref[...]
        dy_x = dy_ref[...]
        xdt_x = xs * dt_x
        prev = st_ref[...]
        d_s = ds_scr[...]
        g1_x = _dot(bc, d_s, NT)
        cp_x = _dot(cc, prev, NT)
        d_cb = jnp.zeros((CHUNK, CHUNK), F32)
        lane8 = lax.broadcasted_iota(jnp.int32, (CHUNK, HEADS_PER_GROUP), 1)
        sub8 = lax.broadcasted_iota(jnp.int32, (HEADS_PER_GROUP, CHUNK), 0)
        row_w = jnp.zeros((CHUNK, HEADS_PER_GROUP), F32)
        col_w = jnp.zeros((HEADS_PER_GROUP, CHUNK), F32)
        dxdt_blocks = []
        for blk in range(GROUP_COLS // LANES):
            lanes = slice(blk * LANES, (blk + 1) * LANES)
            dy_b = dy_x[:, lanes].astype(BF16)
            x_b = xdt_x[:, lanes].astype(BF16)
            acc_dx = jnp.zeros((CHUNK, LANES), F32)
            for i in range(HEADS_PER_LANE_BLOCK):
                h = blk * HEADS_PER_LANE_BLOCK + i
                decay = _head_decay(causal, cs, cs_row, h)
                m = cb * decay
                dy_h = jnp.where(masks[i], dy_b, jnp.zeros_like(dy_b))
                acc_dx = acc_dx + _dot(m, dy_h, TN)
                d_m = _dot(dy_h, x_b, NT)
                d_cb = d_cb + d_m * decay
                w = d_m * m
                row_w = jnp.where(lane8 == h, jnp.sum(w, axis=1, keepdims=True), row_w)
                col_w = jnp.where(sub8 == h, jnp.sum(w, axis=0, keepdims=True), col_w)
            dxdt_blocks.append(acc_dx)
        dxdt_x = jnp.concatenate(dxdt_blocks, axis=1) + e_st_x * g1_x
        dxs_ref[...] = dxdt_x * dt_x + dy_x * d_ref[...]
        dye = dy_x * e_out_x
        xde = xdt_x * e_st_x
        ddt, y_off, tl, dskip = _group_sums([dxdt_x * xs, dye * cp_x, xde * g1_x, dy_x * xs], expand)
        ddt_ref[...] = ddt
        dd_ref[...] += jnp.sum(dskip, axis=0, keepdims=True)
        sp = None
        for part in _split3(d_s * prev):
            t = lax.dot_general(expand, part, NN, preferred_element_type=F32)
            sp = t if sp is None else sp + t
        last_col = jnp.exp(cs_row[:, CHUNK - 1:CHUNK]) * jnp.sum(sp, axis=1, keepdims=True)
        eye = lax.broadcasted_iota(jnp.int32, (HEADS_PER_GROUP, HEADS_PER_GROUP), 0) == lax.broadcasted_iota(
            jnp.int32, (HEADS_PER_GROUP, HEADS_PER_GROUP), 1)
        last_row = jnp.sum(jnp.where(eye, last_col, 0.0), axis=0, keepdims=True) + jnp.sum(tl, axis=0, keepdims=True)
        is_last = lax.broadcasted_iota(jnp.int32, (CHUNK, 1), 0) == CHUNK - 1
        d_cs = row_w + y_off - tl + jnp.where(is_last, last_row, 0.0)
        da_ref[...] = _dot_select(upper, d_cs, data=1) - _dot_select(upper, col_w, NT, data=1)
        dc_ref[...] = _dot(d_cb, bc) + _dot(dye, prev)
        db_ref[...] = _dot(d_cb, cc, TN) + _dot(xde, d_s)
        _decay_state(ds_scr, d_s, _dot(dye, cc, TN), cs)

    rev = lambda c: last - c
    xs = pl.BlockSpec((CHUNK, GROUP_COLS), lambda g, c: (rev(c), g))
    bsp = pl.BlockSpec((CHUNK, SSM_STATE), lambda g, c: (rev(c), D_INNER // SSM_STATE + g))
    csp = pl.BlockSpec((CHUNK, SSM_STATE), lambda g, c: (rev(c), (D_INNER + GN) // SSM_STATE + g))
    per_head = pl.BlockSpec((None, CHUNK, HEADS_PER_GROUP), lambda g, c: (g, rev(c), 0))
    dsk = pl.BlockSpec((None, 1, GROUP_COLS), lambda g, c: (g, 0, 0))
    dsum = pl.BlockSpec((None, 1, HEADS_PER_GROUP), lambda g, c: (g, 0, 0))
    st = pl.BlockSpec((None, GROUP_COLS, SSM_STATE), lambda g, c: (rev(c), g, 0))
    grp = pl.BlockSpec((CHUNK, SSM_STATE), lambda g, c: (rev(c), g))
    return _call(
        body, name=name, grid=(SSM_GROUPS, N_CHUNKS),
        in_specs=[xs, bsp, csp, per_head, per_head, dsk, st, xs],
        out_specs=[xs, grp, grp, per_head, per_head, dsum],
        out_shape=[jax.ShapeDtypeStruct((t_dim, D_INNER), F32), jax.ShapeDtypeStruct((t_dim, GN), F32),
                   jax.ShapeDtypeStruct((t_dim, GN), F32),
                   jax.ShapeDtypeStruct((SSM_GROUPS, t_dim, HEADS_PER_GROUP), F32),
                   jax.ShapeDtypeStruct((SSM_GROUPS, t_dim, HEADS_PER_GROUP), F32),
                   jax.ShapeDtypeStruct((SSM_GROUPS, 1, HEADS_PER_GROUP), F32)],
        scratch_shapes=[pltpu.VMEM((GROUP_COLS, SSM_STATE), F32)],
        sem=("parallel", "arbitrary"), args=[xbc, xbc, xbc, dtg, ag, dgx, states, dy], comm=comm)


NORM_GROUP = D_INNER // SSM_GROUPS


def _gate_norm_fwd(y, zx, nw, *, name, tm=256):
    t_dim = y.shape[0]
    row = pl.BlockSpec((tm, D_INNER), lambda i: (i, 0))

    def body(y_ref, z_ref, nw_ref, o_ref):
        z = z_ref[...]
        yz = y_ref[...] * (z * _sigmoid(z))
        for g in range(SSM_GROUPS):
            cols = slice(g * NORM_GROUP, (g + 1) * NORM_GROUP)
            yhat, _ = _rms(yz[:, cols])
            o_ref[:, cols] = (yhat * nw_ref[:, cols]).astype(BF16)

    return pl.pallas_call(
        body, name=name, grid=(t_dim // tm,), in_specs=[row, row, pl.BlockSpec((1, D_INNER), lambda i: (0, 0))],
        out_specs=row, out_shape=jax.ShapeDtypeStruct((t_dim, D_INNER), BF16),
        compiler_params=_params("parallel"),
    )(y, zx, nw)


def _gate_norm_bwd(y, zx, nw, dyn, *, name, tm=256):
    t_dim = y.shape[0]
    row = pl.BlockSpec((tm, D_INNER), lambda i: (i, 0))
    vec = pl.BlockSpec((1, D_INNER), lambda i: (0, 0))

    def body(y_ref, z_ref, nw_ref, dyn_ref, dy_ref, dz_ref, dnw_ref):
        @pl.when(pl.program_id(0) == 0)
        def _():
            dnw_ref[...] = jnp.zeros_like(dnw_ref)

        z = z_ref[...]
        yv = y_ref[...]
        sg = _sigmoid(z)
        silu_z = z * sg
        yz = yv * silu_z
        dyn_v = dyn_ref[...]
        for g in range(SSM_GROUPS):
            cols = slice(g * NORM_GROUP, (g + 1) * NORM_GROUP)
            yhat, r = _rms(yz[:, cols])
            dn = dyn_v[:, cols]
            dnw_ref[:, cols] += jnp.sum(dn * yhat, axis=0, keepdims=True)
            dyhat = dn * nw_ref[:, cols]
            dyz = r * (dyhat - yhat * jnp.mean(dyhat * yhat, axis=-1, keepdims=True))
            dy_ref[:, cols] = dyz * silu_z[:, cols]
            dz_ref[:, cols] = dyz * yv[:, cols] * _dsilu(z[:, cols], sg[:, cols])

    return pl.pallas_call(
        body, name=name, grid=(t_dim // tm,), in_specs=[row, row, vec, row],
        out_specs=[row, row, vec],
        out_shape=[jax.ShapeDtypeStruct((t_dim, D_INNER), F32), jax.ShapeDtypeStruct((t_dim, ZX_DIM), F32),
                   jax.ShapeDtypeStruct((1, D_INNER), F32)],
        compiler_params=_params("arbitrary"),
    )(y, zx, nw, dyn)


HEADS_PER_LANE_TILE = LANES // ATT_HEAD_DIM
STACKED_ROWS = Q_PER_KV * WINDOW


def _att_half_masks():
    lane = lax.broadcasted_iota(jnp.int32, (WINDOW, LANES), 1)
    return [(lane >= i * ATT_HEAD_DIM) & (lane < (i + 1) * ATT_HEAD_DIM) for i in range(HEADS_PER_LANE_TILE)]


def _att_stack_heads(ref, kvh, masks):
    parts = []
    for g in range(Q_PER_KV):
        h = kvh * Q_PER_KV + g
        blk = ref[:, (h // HEADS_PER_LANE_TILE) * LANES:(h // HEADS_PER_LANE_TILE + 1) * LANES]
        parts.append(jnp.where(masks[h % HEADS_PER_LANE_TILE], blk, jnp.zeros_like(blk)))
    return jnp.concatenate(parts, axis=0)


def _att_kv_tile(ref, kvh, masks):
    blk = ref[:, (kvh // HEADS_PER_LANE_TILE) * LANES:(kvh // HEADS_PER_LANE_TILE + 1) * LANES]
    return jnp.where(masks[kvh % HEADS_PER_LANE_TILE], blk, pltpu.roll(blk, ATT_HEAD_DIM, 1)).astype(BF16)


def _att_stacked_masks(n):
    row = lax.bitwise_and(lax.broadcasted_iota(jnp.int32, (STACKED_ROWS, WINDOW), 0), WINDOW - 1)
    col = lax.broadcasted_iota(jnp.int32, (STACKED_ROWS, WINDOW), 1)
    return col <= row, (col > row) & (n > 0)


def _att_stack_columns(ref, kvh, rows):
    cols = [ref[:, kvh * Q_PER_KV + g:kvh * Q_PER_KV + g + 1] for g in range(Q_PER_KV)]
    return jnp.concatenate([jnp.broadcast_to(c, (rows, 1)) for c in cols], axis=0)


def _att_scores(q4, k_tile, mask):
    return jnp.where(mask, _dot(q4, k_tile, NT) * ATT_SCALE, NEG_BIG)


def _att_unstack(x4, kvh, masks, tiles):
    for g in range(Q_PER_KV):
        h = kvh * Q_PER_KV + g
        piece = x4[g * WINDOW:(g + 1) * WINDOW]
        t = h // HEADS_PER_LANE_TILE
        tiles[t] = piece if h % HEADS_PER_LANE_TILE == 0 else jnp.where(masks[1], piece, tiles[t])


def _attn_fwd(q, k, v, sinks, *, name, comm=None):
    t_dim = q.shape[0]

    def body(q_ref, kc_ref, kp_ref, vc_ref, vp_ref, s_ref, o_ref, l_ref):
        n = pl.program_id(0)
        masks = _att_half_masks()
        mask_c, mask_p = _att_stacked_masks(n)
        out_tiles = [None] * (D_MODEL // LANES)
        for kvh in range(N_KV_HEADS):
            q4 = _att_stack_heads(q_ref, kvh, masks).astype(BF16)
            kc, kp = _att_kv_tile(kc_ref, kvh, masks), _att_kv_tile(kp_ref, kvh, masks)
            vc, vp = _att_kv_tile(vc_ref, kvh, masks), _att_kv_tile(vp_ref, kvh, masks)
            sc = _att_scores(q4, kc, mask_c)
            sp = _att_scores(q4, kp, mask_p)
            sink = _att_stack_columns(s_ref, kvh, WINDOW)
            m = jnp.maximum(jnp.maximum(jnp.max(sc, axis=1, keepdims=True), jnp.max(sp, axis=1, keepdims=True)), sink)
            pc = jnp.exp(sc - m)
            pp = jnp.exp(sp - m)
            den = jnp.sum(pc, axis=1, keepdims=True) + jnp.sum(pp, axis=1, keepdims=True) + jnp.exp(sink - m)
            _att_unstack((_dot(pc, vc) + _dot(pp, vp)) / den, kvh, masks, out_tiles)
            lse4 = m + jnp.log(den)
            for g in range(Q_PER_KV):
                h = kvh * Q_PER_KV + g
                l_ref[:, h:h + 1] = lse4[g * WINDOW:(g + 1) * WINDOW]
        for t, tile in enumerate(out_tiles):
            o_ref[:, t * LANES:(t + 1) * LANES] = tile

    cur = lambda w: pl.BlockSpec((WINDOW, w), lambda n: (n, 0))
    prv = lambda w: pl.BlockSpec((WINDOW, w), lambda n: (jnp.maximum(n - 1, 0), 0))
    return _call(
        body, name=name, grid=(t_dim // WINDOW,),
        in_specs=[cur(D_MODEL), cur(KV_DIM), prv(KV_DIM), cur(KV_DIM), prv(KV_DIM), pl.BlockSpec((1, N_Q_HEADS), lambda n: (0, 0))],
        out_specs=[cur(D_MODEL), cur(N_Q_HEADS)],
        out_shape=[jax.ShapeDtypeStruct((t_dim, D_MODEL), F32), jax.ShapeDtypeStruct((t_dim, N_Q_HEADS), F32)],
        sem=("parallel",), args=[q, k, k, v, v, sinks], comm=comm)


def _attn_bwd(q, k, v, sinks, o, lse, do, cos2, sin2, *, name, comm=None):
    t_dim = q.shape[0]

    def body(q_ref, kc_ref, kp_ref, vc_ref, vp_ref, s_ref, o_ref, l_ref, do_ref, cos_ref, sin_ref, cos_all_ref, sin_all_ref,
             dq_ref, dk_ref, dv_ref, dsink_ref):
        n = pl.program_id(0)

        @pl.when(n == 0)
        def _():
            dk_ref[...] = jnp.zeros_like(dk_ref)
            dv_ref[...] = jnp.zeros_like(dv_ref)
            dsink_ref[...] = jnp.zeros_like(dsink_ref)

        masks = _att_half_masks()
        mask_c, mask_p = _att_stacked_masks(n)
        lane_row = lax.broadcasted_iota(jnp.int32, (1, N_Q_HEADS), 1)
        rows_c = pl.ds(pl.multiple_of(n * WINDOW, WINDOW), WINDOW)
        rows_p = pl.ds(pl.multiple_of(jnp.maximum(n - 1, 0) * WINDOW, WINDOW), WINDOW)
        dsink = jnp.zeros((1, N_Q_HEADS), F32)
        dq_tiles = [None] * (D_MODEL // LANES)
        kv_tiles = KV_DIM // LANES
        dkc_tiles, dkp_tiles, dvc_tiles, dvp_tiles = ([None] * kv_tiles for _ in range(4))

        def place(tiles, kvh, x):
            folded = x + pltpu.roll(x, ATT_HEAD_DIM, 1)
            t = kvh // HEADS_PER_LANE_TILE
            tiles[t] = folded if kvh % HEADS_PER_LANE_TILE == 0 else jnp.where(masks[1], folded, tiles[t])

        for kvh in range(N_KV_HEADS):
            q4 = _att_stack_heads(q_ref, kvh, masks).astype(BF16)
            do4 = _att_stack_heads(do_ref, kvh, masks)
            o4 = _att_stack_heads(o_ref, kvh, masks)
            kc, kp = _att_kv_tile(kc_ref, kvh, masks), _att_kv_tile(kp_ref, kvh, masks)
            vc, vp = _att_kv_tile(vc_ref, kvh, masks), _att_kv_tile(vp_ref, kvh, masks)
            l4 = _att_stack_columns(l_ref, kvh, WINDOW)
            pc = jnp.exp(_att_scores(q4, kc, mask_c) - l4)
            pp = jnp.exp(_att_scores(q4, kp, mask_p) - l4)
            delta = jnp.sum(do4 * o4, axis=1, keepdims=True)
            do4b = do4.astype(BF16)
            dsc = pc * (_dot(do4b, vc, NT) - delta)
            dsp = pp * (_dot(do4b, vp, NT) - delta)
            _att_unstack((_dot(dsc, kc) + _dot(dsp, kp)) * ATT_SCALE, kvh, masks, dq_tiles)
            place(dkc_tiles, kvh, _dot(dsc, q4, TN) * ATT_SCALE)
            place(dkp_tiles, kvh, _dot(dsp, q4, TN) * ATT_SCALE)
            place(dvc_tiles, kvh, _dot(pc, do4b, TN))
            place(dvp_tiles, kvh, _dot(pp, do4b, TN))
            p_sink = jnp.exp(_att_stack_columns(s_ref, kvh, WINDOW) - l4) * delta
            for g in range(Q_PER_KV):
                h = kvh * Q_PER_KV + g
                dsink = jnp.where(lane_row == h, -jnp.sum(p_sink[g * WINDOW:(g + 1) * WINDOW], axis=0, keepdims=True), dsink)
        for t, tile in enumerate(dq_tiles):
            dq_ref[:, t * LANES:(t + 1) * LANES] = _rope_rotate(tile, cos_ref[...], -sin_ref[...])
        for t in range(kv_tiles):
            lanes = slice(t * LANES, (t + 1) * LANES)
            dk_ref[rows_c, lanes] += dkc_tiles[t]
            dk_ref[rows_p, lanes] += dkp_tiles[t]
            dv_ref[rows_c, lanes] += dvc_tiles[t]
            dv_ref[rows_p, lanes] += dvp_tiles[t]
        dsink_ref[...] += dsink

        @pl.when(n == t_dim // WINDOW - 1)
        def _():
            dk_ref[...] = _rope_rotate(dk_ref[...], cos_all_ref[...], -sin_all_ref[...])

    cur = lambda w: pl.BlockSpec((WINDOW, w), lambda n: (n, 0))
    prv = lambda w: pl.BlockSpec((WINDOW, w), lambda n: (jnp.maximum(n - 1, 0), 0))
    whole = lambda w: pl.BlockSpec((t_dim, w), lambda n: (0, 0))
    svec = pl.BlockSpec((1, N_Q_HEADS), lambda n: (0, 0))
    return _call(
        body, name=name, grid=(t_dim // WINDOW,),
        in_specs=[cur(D_MODEL), cur(KV_DIM), prv(KV_DIM), cur(KV_DIM), prv(KV_DIM), svec, cur(D_MODEL), cur(N_Q_HEADS), cur(D_MODEL),
                  cur(LANES), cur(LANES), whole(LANES), whole(LANES)],
        out_specs=[cur(D_MODEL), whole(KV_DIM), whole(KV_DIM), svec],
        out_shape=[jax.ShapeDtypeStruct((t_dim, D_MODEL), F32), jax.ShapeDtypeStruct((t_dim, KV_DIM), F32),
                   jax.ShapeDtypeStruct((t_dim, KV_DIM), F32), jax.ShapeDtypeStruct((1, N_Q_HEADS), F32)],
        sem=("arbitrary",), args=[q, k, k, v, v, sinks, o, lse, do, cos2, sin2, cos2, sin2], comm=comm)


def _loss_head(x, nw, target, *, name, tm=512):
    t_dim, d_dim = x.shape
    row = pl.BlockSpec((tm, d_dim), lambda i: (i, 0))
    vec = pl.BlockSpec((1, d_dim), lambda i: (0, 0))

    def body(x_ref, nw_ref, tgt_ref, loss_ref, dx_ref, dnw_ref, dob_ref):
        @pl.when(pl.program_id(0) == 0)
        def _():
            loss_ref[...] = jnp.zeros_like(loss_ref)
            dnw_ref[...] = jnp.zeros_like(dnw_ref)

        xhat, r = _rms(x_ref[...])
        err = xhat * nw_ref[...] - tgt_ref[...]
        loss_ref[...] += 0.5 * _sum_all(jnp.mean(err * err, axis=-1, keepdims=True))
        dy = err * (1.0 / d_dim)
        dnw_ref[...] += jnp.sum(dy * xhat, axis=0, keepdims=True)
        dxhat = dy * nw_ref[...]
        dx = r * (dxhat - xhat * jnp.mean(dxhat * xhat, axis=-1, keepdims=True))
        dx_ref[...] = dx
        dob_ref[...] = (FFN_RES_WEIGHT * dx).astype(BF16)

    return pl.pallas_call(
        body, name=name, grid=(t_dim // tm,), in_specs=[row, vec, row],
        out_specs=[pl.BlockSpec((1, 1), lambda i: (0, 0)), row, vec, row],
        out_shape=[jax.ShapeDtypeStruct((1, 1), F32), jax.ShapeDtypeStruct((t_dim, d_dim), F32),
                   jax.ShapeDtypeStruct((1, d_dim), F32), jax.ShapeDtypeStruct((t_dim, d_dim), BF16)],
        compiler_params=_params("arbitrary"),
    )(x, nw, target)


def _rope_tables():
    pos = jnp.arange(SEQ, dtype=F32)
    inv = 1.0 / (ROPE_THETA ** (jnp.arange(0, ATT_HEAD_DIM, 2, dtype=F32) / ATT_HEAD_DIM))
    ang = pos[:, None] * inv[None, :]
    cos, sin = jnp.cos(ang), jnp.sin(ang)
    return jnp.tile(cos, (1, 4)), jnp.tile(sin, (1, 4))


def _to_groups(t):
    return t.reshape(t.shape[0], SSM_GROUPS, HEADS_PER_GROUP).transpose(1, 0, 2)


def _from_groups(t):
    return t.transpose(1, 0, 2).reshape(t.shape[1], SSM_HEADS)


def _forward_backward(x0, target, net):
    w = net.w
    nw = [[w("norm_w")[l, i][None, :] for i in range(3)] for l in range(2)]
    cos2, sin2 = _rope_tables()
    ffn_norm = [nw[0][0], nw[0][2], nw[1][0], nw[1][2]]

    ffn_pre = {}

    def ffn_f(x, blk):
        name = f"ffn_fwd{blk}"
        out, *ffn_pre[blk] = _ffn_fwd(x, ffn_norm[blk], w(f"gate{blk}"), w(f"up{blk}"), w(f"down{blk}"), name=name,
                                      comm=net.carry(name))
        return out

    x1 = ffn_f(x0, 0)
    zx, h1 = _norm_mm(x1, nw[0][1], w("w_in_t"), None, w_rows=ZX_DIM, name="ssm_in_proj", comm=net.carry("ssm_in_proj"))
    dtr = _mm(h1, w("w_in_t"), dims="nt", b_rows=(ZX_DIM, SSM_HEADS), name="ssm_dt_proj")
    xbc = _conv_fwd(zx, w("conv_w"), w("conv_b"), name="ssm_conv_fwd", comm=net.carry("ssm_conv_fwd"))
    dt, a_dt = _dt_prep(dtr, w("dt_bias"), w("a_log"), name="ssm_dt_prep")
    dtg, ag = _to_groups(dt), _to_groups(a_dt)
    dg = jnp.repeat(w("d_skip").reshape(SSM_GROUPS, 1, HEADS_PER_GROUP), SSM_HEAD_DIM, axis=2)
    y_ssd, states = _ssd_fwd(xbc, dtg, ag, dg, name="ssd_fwd", comm=net.carry("ssd_fwd"))
    yn = _gate_norm_fwd(y_ssd, zx, w("ssm_norm_w"), name="ssm_gate_norm_fwd")
    x2 = _mm(yn, w("wout"), res=x1, name="ssm_out_proj", comm=net.carry("ssm_out_proj"))
    x3 = ffn_f(x2, 1)
    k_rot, hk = _norm_mm(x3, w("kv_norm_w"), w("wk"), w("b_k"), rope=(cos2, sin2), name="k_proj")
    v = _mm(hk, w("wv"), bias=w("b_v"), name="v_proj")
    x4 = ffn_f(x3, 2)
    q_rot, h4 = _norm_mm(x4, nw[1][1], w("wq"), w("b_q"), rope=(cos2, sin2), name="q_proj")
    att, lse = _attn_fwd(q_rot, k_rot, v, w("sinks"), name="attn_fwd", comm=net.carry("attn_fwd"))
    x5 = _mm(att, w("wo"), bias=w("b_o"), res=x4, name="attn_out_proj")
    x6 = ffn_f(x5, 3)
    loss, dx6, d_final, dob6 = _loss_head(x6, w("final_norm_w"), target, name="loss_head")

    d_norm = [[None] * 3 for _ in range(2)]

    def ffn_b(x, dout, dob, blk):
        pre_g, pre_u, h = ffn_pre[blk]
        name = f"ffn_bwd{blk}"
        dh, gg, gu, gd = _ffn_bwd(h, dob, pre_g, pre_u, w(f"gate{blk}"), w(f"up{blk}"), w(f"down{blk}"), name=name,
                                  comm=net.carry(name))
        net.give(f"gate{blk}", gg)
        net.give(f"up{blk}", gu)
        net.give(f"down{blk}", gd)
        return _norm_bwd(x, ffn_norm[blk], dh, [dout], name=f"ffn_norm_bwd{blk}", comm=net.carry(f"ffn_norm_bwd{blk}"))

    by_rows = lambda g: g.reshape(N_DEV, g.shape[0] // N_DEV, g.shape[1])
    dx5, d_norm[1][2] = ffn_b(x5, dx6, dob6, 3)
    d_att = _mm(dx5, w("wo"), dims="nt", name="attn_out_proj_dx", comm=net.carry("attn_out_proj_dx"))
    g_o, d_bo = _mm(att, dx5, dims="tn", out_dtype=BF16, colsum_b=True, name="attn_out_proj_dw")
    net.give("w_o", by_rows(g_o))
    dq, dk, dv, d_sinks = _attn_bwd(q_rot, k_rot, v, w("sinks"), att, lse, d_att, cos2, sin2, name="attn_bwd",
                                    comm=net.carry("attn_bwd"))
    dx4, d_norm[1][1], dob4 = _mm_norm_bwd(dq, w("wq"), x4, nw[1][1], [dx5], dims="nt", name="q_proj_dx")
    g_q, d_bq = _mm(h4, dq, dims="tn", out_dtype=BF16, colsum_b=True, name="q_proj_dw")
    net.give("w_q", by_rows(g_q))
    dx3a, d_norm[1][0] = ffn_b(x3, dx4, dob4, 2)
    dhk = _mm(dk, w("wk"), dims="nt", name="k_proj_dx", comm=net.carry("k_proj_dx"))
    dx3, d_kvn, dob3 = _mm_norm_bwd(dv, w("wv"), x3, w("kv_norm_w"), [dx3a], dims="nt", add=dhk, name="v_proj_dx")
    g_k, d_bk = _mm(hk, dk, dims="tn", out_dtype=BF16, colsum_b=True, name="k_proj_dw")
    g_v, d_bv = _mm(hk, dv, dims="tn", out_dtype=BF16, colsum_b=True, name="v_proj_dw")
    net.give("w_k", by_rows(g_k))
    net.give("w_v", by_rows(g_v))
    dx2, d_norm[0][2] = ffn_b(x2, dx3, dob3, 1)
    d_yn = _mm(dx2, w("wout"), dims="nt", name="ssm_out_proj_dx", comm=net.carry("ssm_out_proj_dx"))
    net.give("w_out", by_rows(_mm(yn, dx2, dims="tn", out_dtype=BF16, name="ssm_out_proj_dw")))
    dy_ssd, dzx, d_ssm_norm = _gate_norm_bwd(y_ssd, zx, w("ssm_norm_w"), d_yn, name="ssm_gate_norm_bwd")
    dxs, d_b, d_c, ddtg, dag, ddg = _ssd_bwd(xbc, dtg, ag, dg, states, dy_ssd, name="ssd_bwd", comm=net.carry("ssd_bwd"))
    dzx, d_conv_w, d_conv_b = _conv_bwd(zx, w("conv_w"), w("conv_b"), dxs, d_b, d_c, dzx, name="ssm_conv_bwd",
                                        comm=net.carry("ssm_conv_bwd"))
    ddtr, d_dt_bias, d_a_log = _dt_bwd(dtr, w("dt_bias"), w("a_log"), dt, _from_groups(ddtg), _from_groups(dag), name="ssm_dt_bwd")
    dh1 = _mm(dzx, w("w_in_t"), b_rows=(0, ZX_DIM), name="ssm_in_proj_dx")
    in_rows = N_DEV * IN_PROJ_SHARD
    g_in = _mm(dzx, h1, dims="tn", out_dtype=BF16, out_window=(0, in_rows), name="ssm_in_proj_dw")
    g_in = _mm(ddtr, h1, dims="tn", out_dtype=BF16, out_window=(ZX_DIM, in_rows), into=g_in, name="ssm_dt_proj_dw")
    net.give("w_in", g_in.reshape(N_DEV, IN_PROJ_SHARD, D_MODEL))
    dx1, d_norm[0][1], dob1 = _mm_norm_bwd(ddtr, w("w_in_t"), x1, nw[0][1], [dx2], b_rows=(ZX_DIM, SSM_HEADS), add=dh1,
                                           name="ssm_dt_proj_dx", comm=net.carry("ssm_norm_bwd"))
    dx0, d_norm[0][0] = ffn_b(x0, dx1, dob1, 0)

    small = {"norm_w": jnp.concatenate([d_norm[l][i] for l in range(2) for i in range(3)], axis=0),
             "ssm_conv_w": d_conv_w, "ssm_conv_b": d_conv_b, "ssm_dt_bias": d_dt_bias, "ssm_a_log": d_a_log,
             "ssm_d": ddg.reshape(1, SSM_HEADS), "ssm_norm_w": d_ssm_norm, "kv_norm_w": d_kvn,
             "b_k": d_bk, "b_v": d_bv, "attn_b_q": d_bq, "attn_sinks": d_sinks, "attn_b_o": d_bo, "final_norm_w": d_final}
    return loss, dx0, small


BLOCK_BYTES = 1 << 20


def _row_tile(rows, cols):
    for t in (512, 256, 128, 64, 32, 16):
        if rows % t == 0 and t * cols * 4 <= BLOCK_BYTES:
            return t
    return rows


def _cast_bf16(x, *, name):
    n_blk, rows, cols = x.shape
    tm = rows if rows * cols * 4 <= 2 * BLOCK_BYTES else _row_tile(rows, cols)
    spec = pl.BlockSpec((None, tm, cols), lambda b, i: (b, i, 0))

    def body(x_ref, o_ref):
        o_ref[...] = x_ref[...].astype(BF16)

    return pl.pallas_call(body, name=name, grid=(n_blk, rows // tm), in_specs=[spec], out_specs=spec,
                          out_shape=jax.ShapeDtypeStruct(x.shape, BF16), compiler_params=_params("parallel", "parallel"))(x)


def _pair_add(grad, theirs, *, name):
    n_slots, rows, cols = theirs.shape
    tm = rows if rows * cols * 4 <= 2 * BLOCK_BYTES else _row_tile(rows, cols)

    def body(g_ref, t_ref, o_ref):
        o_ref[...] = (g_ref[...].astype(F32) + t_ref[...].astype(F32)).astype(BF16)

    spec = pl.BlockSpec((None, tm, cols), lambda s, i: (s, i, 0))
    return pl.pallas_call(
        body, name=name, grid=(n_slots, rows // tm),
        in_specs=[pl.BlockSpec((None, tm, cols), lambda s, i: (2 * s + lax.axis_index("c"), i, 0)), spec], out_specs=spec,
        out_shape=jax.ShapeDtypeStruct(theirs.shape, BF16), compiler_params=_params("parallel", "parallel"),
    )(grad, theirs)


def _adam_update(g, w, m, v):
    m = ADAM_B1 * m + (1.0 - ADAM_B1) * g
    v = ADAM_B2 * v + (1.0 - ADAM_B2) * (g * g)
    m_hat = m / (1.0 - ADAM_B1 ** ADAM_STEP)
    v_hat = v / (1.0 - ADAM_B2 ** ADAM_STEP)
    delta = -ADAM_LR * (m_hat / (jnp.sqrt(v_hat) + ADAM_EPS) + ADAM_WD * w)
    return delta, m, v


def _adamw(parts, w, m, v, first_blk, prev, *, name, comm=None):
    n_blk, rows, cols = w.shape
    tm = _row_tile(rows, cols)
    n_tiles = rows // tm
    spec = pl.BlockSpec((None, tm, cols), lambda b, i: (first_blk + b, i, 0))
    n_prev, n_here = len(prev), len(parts)
    n_parts = parts[0].shape[0]

    def part_spec(q):
        return pl.BlockSpec((n_parts, tm, cols), lambda b, i: (0, jnp.where(b < q, 0, jnp.where(b == q, i, n_tiles - 1)), 0))

    def body(*refs):
        p_refs = refs[:n_here]
        w_ref, m_ref, v_ref = refs[n_here:n_here + 3]
        g_ref, d_ref, nm_ref, nv_ref = refs[n_here + 3 + n_prev:]
        b = pl.program_id(0)
        g = None
        for s in range(n_parts):
            t = p_refs[0][s]
            for q in range(1, n_here):
                t = jnp.where(b == q, p_refs[q][s], t)
            g = t.astype(F32) if g is None else g + t.astype(F32)
        delta, nm, nv = _adam_update(g, w_ref[...], m_ref[...], v_ref[...])
        g_ref[...] = g
        d_ref[...] = delta
        nm_ref[...] = nm
        nv_ref[...] = nv

    return _call(
        body, name=name, grid=(n_here, n_tiles),
        in_specs=[part_spec(q) for q in range(n_here)] + [spec, spec, spec] + [pl.BlockSpec(memory_space=pl.ANY)] * n_prev,
        out_specs=[spec] * 4, out_shape=[jax.ShapeDtypeStruct((n_blk, rows, cols), F32)] * 4,
        aliases={n_here + 3 + q: q for q in range(n_prev)}, sem=("arbitrary", "arbitrary"),
        args=[*parts, w, m, v, *prev], comm=comm)


def _sum_parts(parts, *, name):
    def body(p_ref, o_ref):
        g = p_ref[0]
        for s in range(1, N_DEV):
            g = g + p_ref[s]
        o_ref[...] = g

    return pl.pallas_call(body, name=name, out_shape=jax.ShapeDtypeStruct(parts.shape[1:], F32), compiler_params=_params())(parts)


def _adamw_packed(g, w, m, v, *, name):
    def body(g_ref, w_ref, m_ref, v_ref, d_ref, nm_ref, nv_ref):
        delta, nm, nv = _adam_update(g_ref[...], w_ref[...], m_ref[...], v_ref[...])
        d_ref[...] = delta
        nm_ref[...] = nm
        nv_ref[...] = nv

    return pl.pallas_call(body, name=name, out_shape=[jax.ShapeDtypeStruct(g.shape, F32)] * 3, compiler_params=_params())(g, w, m, v)


SUBLANES = 8


WIDE_PACK = 1024


def _pack(arrs, width=LANES):
    rows = []
    for a in arrs:
        a2 = a.reshape(-1, a.shape[-1])
        a2 = jnp.pad(a2, ((0, 0), (0, (-a2.shape[1]) % width)))
        rows += [a2[:, i * width:(i + 1) * width] for i in range(a2.shape[1] // width)]
    out = jnp.concatenate(rows, axis=0)
    return jnp.pad(out, ((0, (-out.shape[0]) % SUBLANES), (0, 0)))


def _unpack(packed, shapes, width=LANES):
    outs, r = [], 0
    for shp in shapes:
        lead, cols = math.prod(shp[:-1]), shp[-1]
        n_blocks = -(-cols // width)
        blocks = [packed[r + i * lead:r + (i + 1) * lead] for i in range(n_blocks)]
        outs.append(jnp.concatenate(blocks, axis=1)[:, :cols].reshape(shp))
        r += n_blocks * lead
    return outs


WEIGHT_NAMES = ("norm_w", "ffn_w_gate", "ffn_w_up", "ffn_w_down", "ssm_w_in", "ssm_conv_w", "ssm_conv_b", "ssm_dt_bias",
                "ssm_a_log", "ssm_d", "ssm_norm_w", "ssm_w_out", "kv_norm_w", "w_k", "b_k", "w_v", "b_v", "attn_w_q",
                "attn_b_q", "attn_sinks", "attn_w_o", "attn_b_o", "final_norm_w")
MATRIX_NAMES = ("ffn_w_gate", "ffn_w_up", "ffn_w_down", "ssm_w_in", "ssm_w_out", "w_k", "w_v", "attn_w_q", "attn_w_o")
VECTOR_NAMES = tuple(n for n in WEIGHT_NAMES if n not in MATRIX_NAMES)
SHARDED_VECTORS = ("norm_w", "ssm_conv_w", "ssm_conv_b", "ssm_norm_w")


GATHER_PLAN = {
    "gather_stage0": ("gate0", "up0", "down0", "vec"),
    "ffn_fwd0": ("w_in",),
    "ssm_in_proj": ("w_out", "gate1"),
    "ssm_conv_fwd": ("w_k", "w_v", "up1"),
    "ssd_fwd": ("down1", "gate2", "up2"),
    "ssm_out_proj": ("w_q", "w_o"),
    "ffn_fwd1": ("down2", "gate3"),
    "ffn_fwd2": ("up3",),
    "attn_fwd": ("down3",),
}
PAIR_PLAN = {
    "attn_out_proj_dx": ("gate3", "up3", "down3"),
    "ffn_bwd2": ("w_q", "w_o"),
    "k_proj_dx": ("gate2", "up2", "down2"),
    "ssm_out_proj_dx": ("w_k", "w_v", "gate1", "up1", "down1"),
    "ssd_bwd": ("w_out",),
    "ssm_norm_bwd": ("w_in",),
    "ffn_norm_bwd0": ("gate0", "up0", "down0"),
}
CHIP_PLAN = {
    "attn_bwd": ("gate3", "up3"),
    "ffn_bwd2": ("down3",),
    "ffn_bwd1": ("gate2", "up2", "w_q", "w_o"),
    "ssd_bwd": ("down2", "gate1", "up1", "down1", "w_k", "w_v"),
    "ssm_conv_bwd": ("w_out",),
    "ffn_bwd0": ("w_in",),
    "adamw_gate": ("gate0",),
    "adamw_up": ("up0",),
    "adamw_down": ("down0",),
}
FFN_PARAMS = {"gate": "ffn_w_gate", "up": "ffn_w_up", "down": "ffn_w_down"}
SINGLE_MATRICES = {"w_in": "ssm_w_in", "w_out": "ssm_w_out", "w_k": "w_k", "w_v": "w_v", "w_q": "attn_w_q", "w_o": "attn_w_o"}


TRANSPOSED = ("ffn_w_gate", "ffn_w_up", "ssm_w_in")


def _matrix_view(name, a):
    if name in TRANSPOSED:
        a = jnp.swapaxes(a, -1, -2)
    return a.reshape((-1,) + a.shape[-2:])


def _from_matrix_view(name, a, shape):
    if name in TRANSPOSED:
        return jnp.swapaxes(a.reshape(shape[:-2] + (shape[-1], shape[-2])), -1, -2)
    return a.reshape(shape)


class _MeshNet:
    def __init__(self, p):
        self.p = p
        self.views = {n: _matrix_view(n, p[n]) for n in MATRIX_NAMES}
        self.local = {"vec": _pack([p[n] for n in SHARDED_VECTORS])}
        for short, n in FFN_PARAMS.items():
            cast = _cast_bf16(self.views[n], name=f"cast_{short}")
            self.local.update({f"{short}{k}": (cast, k) for k in range(N_FFN)})
        for short, n in SINGLE_MATRICES.items():
            self.local[short] = (_cast_bf16(self.views[n], name=f"cast_{short}"), 0)
        self.gathered_at, self.pairs_at, self.parts_at, self.grads, self.cache = {}, {}, {}, {}, {}

    def carry(self, name):
        comms = []
        if name in GATHER_PLAN:
            keys, comm = GATHER_PLAN[name], _Gather([self.local[k] for k in GATHER_PLAN[name]])
            self.gathered_at.update({k: (comm, i) for i, k in enumerate(keys)})
            comms.append(comm)
        if name in CHIP_PLAN:
            sums = []
            for k in CHIP_PLAN[name]:
                comm, i = self.pairs_at[k]
                sums.append(_pair_add(self.grads[k], comm.results[i], name=f"pair_add_{k}"))
            comm = _ChipExchange(sums)
            self.parts_at.update({k: (comm, i) for i, k in enumerate(CHIP_PLAN[name])})
            comms.append(comm)
        if name in PAIR_PLAN:
            keys, comm = PAIR_PLAN[name], _PairSwap([self.grads[k] for k in PAIR_PLAN[name]])
            self.pairs_at.update({k: (comm, i) for i, k in enumerate(keys)})
            comms.append(comm)
        return comms

    def run(self, name):
        for comm in self.carry(name):
            _run_exchange(comm, name=name)

    def give(self, key, grad):
        self.grads[key] = grad

    def parts(self, key):
        comm, i = self.parts_at[key]
        return comm.results[i]

    def _gathered(self, key):
        comm, i = self.gathered_at[key]
        return comm.results[i]

    def _vec(self, r0, lead, n_blocks):
        vecs = self._gathered("vec")
        return jnp.concatenate([vecs[d, r0 + i * lead:r0 + (i + 1) * lead, :] for d in range(N_DEV) for i in range(n_blocks)], axis=1)

    def _derive(self, name):
        p = self.p
        if name[:-1] in FFN_PARAMS:
            return self._gathered(name)
        if name == "w_in_t":
            return self._gathered("w_in").reshape(N_DEV * IN_PROJ_SHARD, D_MODEL)
        by_rows = {"wout": "w_out", "wk": "w_k", "wv": "w_v", "wq": "w_q", "wo": "w_o"}
        if name in by_rows:
            g = self._gathered(by_rows[name])
            return g.reshape(N_DEV * g.shape[1], g.shape[2])
        vectors = {"norm_w": lambda: self._vec(0, 6, 1).reshape(2, 3, D_MODEL), "conv_w": lambda: self._vec(6, CONV_WIDTH, 3),
                   "conv_b": lambda: self._vec(18, 1, 3), "ssm_norm_w": lambda: self._vec(21, 1, 2)}
        if name in vectors:
            return vectors[name]()
        replicated = {"dt_bias": p["ssm_dt_bias"], "a_log": p["ssm_a_log"], "d_skip": p["ssm_d"], "kv_norm_w": p["kv_norm_w"][None],
                      "b_k": p["b_k"][None], "b_v": p["b_v"][None], "b_q": p["attn_b_q"], "sinks": p["attn_sinks"],
                      "b_o": p["attn_b_o"], "final_norm_w": p["final_norm_w"][None]}
        return replicated[name]

    def w(self, name):
        if name not in self.cache:
            self.cache[name] = self._derive(name)
        return self.cache[name]


def _step(x, target, p, m, v):
    pos = _slot(_position())
    net = _MeshNet(p)
    net.run("gather_stage0")
    loss, grad_x, small = _forward_backward(x, target, net)

    grads, deltas, new_m, new_v = {}, {}, {}, {}
    view = lambda d, n: _matrix_view(n, d[n])
    vec_gather = _Gather([_pack([small[n] for n in VECTOR_NAMES], WIDE_PACK)])
    for short, n in SINGLE_MATRICES.items():
        outs = _adamw([net.parts(short)], net.views[n], view(m, n), view(v, n), 0, [], name=f"adamw_{short}",
                      comm=[vec_gather] if short == "w_in" else None)
        grads[n], deltas[n], new_m[n], new_v[n] = [_from_matrix_view(n, o, p[n].shape) for o in outs]
    ffn_outs = {}
    for short, n in FFN_PARAMS.items():
        ffn_outs[short] = _adamw([net.parts(f"{short}{k}") for k in range(1, N_FFN)], net.views[n], view(m, n), view(v, n), 1, [],
                                 name=f"adamw_{short}", comm=net.carry(f"adamw_{short}"))
    for short, n in FFN_PARAMS.items():
        outs = _adamw([net.parts(f"{short}0")], net.views[n], view(m, n), view(v, n), 0, ffn_outs[short], name=f"adamw_{short}0")
        grads[n], deltas[n], new_m[n], new_v[n] = [_from_matrix_view(n, o, p[n].shape) for o in outs]
    vec_sum = _sum_parts(vec_gather.results[0], name="sum_vector_grads")
    full_shapes = {"norm_w": (2, 3, D_MODEL), "ssm_conv_w": (1, CONV_WIDTH, CONV_DIM), "ssm_conv_b": (1, CONV_DIM),
                   "ssm_norm_w": (1, D_INNER)}
    vec_full = dict(zip(VECTOR_NAMES, _unpack(vec_sum, [full_shapes.get(n, p[n].shape) for n in VECTOR_NAMES], WIDE_PACK)))
    for n in VECTOR_NAMES:
        g = vec_full[n]
        if n in SHARDED_VECTORS:
            per = p[n].shape[-1]
            g = lax.dynamic_slice_in_dim(g, pos * per, per, axis=g.ndim - 1)
        grads[n] = g
    packed = _adamw_packed(*[_pack([d[n] for n in VECTOR_NAMES], WIDE_PACK) for d in (grads, p, m, v)], name="adamw_vectors")
    shapes = [p[n].shape for n in VECTOR_NAMES]
    for d, pk in zip((deltas, new_m, new_v), packed):
        d.update(zip(VECTOR_NAMES, _unpack(pk, shapes, WIDE_PACK)))
    return loss, grad_x, grads, deltas, new_m, new_v


def kernel(x, norm_w, ffn_w_gate, ffn_w_up, ffn_w_down, ssm_w_in, ssm_conv_w, ssm_conv_b, ssm_dt_bias, ssm_a_log, ssm_d, ssm_norm_w, ssm_w_out, kv_norm_w, w_k, b_k, w_v, b_v, attn_w_q, attn_b_q, attn_sinks, attn_w_o, attn_b_o, final_norm_w, loss_target, m_norm_w, m_ffn_w_gate, m_ffn_w_up, m_ffn_w_down, m_ssm_w_in, m_ssm_conv_w, m_ssm_conv_b, m_ssm_dt_bias, m_ssm_a_log, m_ssm_d, m_ssm_norm_w, m_ssm_w_out, m_kv_norm_w, m_w_k, m_b_k, m_w_v, m_b_v, m_attn_w_q, m_attn_b_q, m_attn_sinks, m_attn_w_o, m_attn_b_o, m_final_norm_w, v_norm_w, v_ffn_w_gate, v_ffn_w_up, v_ffn_w_down, v_ssm_w_in, v_ssm_conv_w, v_ssm_conv_b, v_ssm_dt_bias, v_ssm_a_log, v_ssm_d, v_ssm_norm_w, v_ssm_w_out, v_kv_norm_w, v_w_k, v_b_k, v_w_v, v_b_v, v_attn_w_q, v_attn_b_q, v_attn_sinks, v_attn_w_o, v_attn_b_o, v_final_norm_w):
    p = dict(zip(WEIGHT_NAMES, (norm_w, ffn_w_gate, ffn_w_up, ffn_w_down, ssm_w_in, ssm_conv_w, ssm_conv_b, ssm_dt_bias, ssm_a_log, ssm_d, ssm_norm_w, ssm_w_out, kv_norm_w, w_k, b_k, w_v, b_v, attn_w_q, attn_b_q, attn_sinks, attn_w_o, attn_b_o, final_norm_w)))
    m = dict(zip(WEIGHT_NAMES, (m_norm_w, m_ffn_w_gate, m_ffn_w_up, m_ffn_w_down, m_ssm_w_in, m_ssm_conv_w, m_ssm_conv_b, m_ssm_dt_bias, m_ssm_a_log, m_ssm_d, m_ssm_norm_w, m_ssm_w_out, m_kv_norm_w, m_w_k, m_b_k, m_w_v, m_b_v, m_attn_w_q, m_attn_b_q, m_attn_sinks, m_attn_w_o, m_attn_b_o, m_final_norm_w)))
    v = dict(zip(WEIGHT_NAMES, (v_norm_w, v_ffn_w_gate, v_ffn_w_up, v_ffn_w_down, v_ssm_w_in, v_ssm_conv_w, v_ssm_conv_b, v_ssm_dt_bias, v_ssm_a_log, v_ssm_d, v_ssm_norm_w, v_ssm_w_out, v_kv_norm_w, v_w_k, v_b_k, v_w_v, v_b_v, v_attn_w_q, v_attn_b_q, v_attn_sinks, v_attn_w_o, v_attn_b_o, v_final_norm_w)))
    loss, grad_x, grads, deltas, new_m, new_v = _step(x[0], loss_target[0], p, m, v)
    loss = lax.psum(loss[0, 0], ("x", "y", "c"))
    return (loss, grad_x[None], *[grads[n] for n in WEIGHT_NAMES], *[deltas[n] for n in WEIGHT_NAMES],
            *[new_m[n] for n in WEIGHT_NAMES], *[new_v[n] for n in WEIGHT_NAMES])
```

```python
import functools
import math

import jax
import jax.numpy as jnp
from jax import lax
from jax.experimental import pallas as pl
from jax.experimental.pallas import tpu as pltpu

F32 = jnp.float32
BF16 = jnp.bfloat16

N_DEV = 8
SEQ = 2048
D_MODEL = 1024
D_FF_SHARD = 352
N_FFN = 4
D_INNER = 2048
SSM_HEADS = 32
SSM_HEAD_DIM = 64
SSM_GROUPS = 4
HEADS_PER_GROUP = 8
SSM_STATE = 128
CHUNK = 128
N_CHUNKS = SEQ // CHUNK
GN = SSM_GROUPS * SSM_STATE
CONV_DIM = D_INNER + 2 * GN
CONV_WIDTH = 4
ZX_DIM = D_INNER + CONV_DIM
IN_PROJ_SHARD = 644
ATT_HEAD_DIM = 64
N_Q_HEADS = 16
N_KV_HEADS = 4
Q_PER_KV = 4
KV_DIM = N_KV_HEADS * ATT_HEAD_DIM
WINDOW = 128
ROPE_THETA = 10000.0
EPS = 1e-5
FFN_RES_WEIGHT = 0.5
ATT_SCALE = 1.0 / math.sqrt(ATT_HEAD_DIM)
NEG_BIG = -1e30

ADAM_LR = 0.001
ADAM_B1 = 0.9
ADAM_B2 = 0.999
ADAM_EPS = 1e-08
ADAM_WD = 0.01
ADAM_STEP = 10

VMEM_LIMIT_BYTES = 56 * 1024 * 1024
FFN_BWD_VMEM_LIMIT_BYTES = 61 * 1024 * 1024

NN = (((1,), (0,)), ((), ()))
NT = (((1,), (1,)), ((), ()))
TN = (((0,), (0,)), ((), ()))
_DIMS = {"nn": NN, "nt": NT, "tn": TN}


def _params(*sem):
    return pltpu.CompilerParams(dimension_semantics=sem if sem else None, vmem_limit_bytes=VMEM_LIMIT_BYTES)


def _dot(a, b, dims=NN):
    return lax.dot_general(a.astype(BF16), b.astype(BF16), dims, preferred_element_type=F32)


def _sigmoid(x):
    return 1.0 / (1.0 + jnp.exp(-x))


def _dsilu(x, s):
    return s * (1.0 + x * (1.0 - s))


def _rms(x):
    r = lax.rsqrt(jnp.mean(x * x, axis=-1, keepdims=True) + EPS)
    return x * r, r


def _sum_all(x):
    return jnp.sum(jnp.sum(x, axis=1, keepdims=True), axis=0, keepdims=True)


MESH = pl.DeviceIdType.MESH
N_PEERS = N_DEV - 1
N_CHIPS = N_DEV // 2


def _position():
    return lax.axis_index("x"), lax.axis_index("y"), lax.axis_index("c")


def _slot(p):
    return 4 * p[0] + 2 * p[1] + p[2]


class _Exchange:
    def __init__(self, arrays, out_shapes):
        n = len(arrays)
        self.arrays = list(arrays)
        self.out_shapes = out_shapes
        self.scratch = [pltpu.SemaphoreType.DMA((n, N_PEERS)), pltpu.SemaphoreType.DMA((n, N_PEERS)), pltpu.SemaphoreType.DMA((n,))]
        self.results = None

    def relay(self, ins, outs, sems):
        pass


class _Gather(_Exchange):
    def __init__(self, pieces):
        pieces = [p if isinstance(p, tuple) else (p, None) for p in pieces]
        self.blocks = [k for _, k in pieces]
        shapes = [a.shape if k is None else a.shape[1:] for a, k in pieces]
        super().__init__([a for a, _ in pieces], [jax.ShapeDtypeStruct((N_DEV,) + s, a.dtype) for s, (a, _) in zip(shapes, pieces)])

    def _plan(self, ins, outs, sems):
        send_sems, recv_sems, local_sems = sems
        x, y, c = _position()
        me, sibling = (x, y, c), (x, y, 1 - c)
        chips = [(1 - x, y), (x, 1 - y), (1 - x, 1 - y)]
        n = len(ins)
        ins = [r if k is None else r.at[k] for r, k in zip(ins, self.blocks)]

        def copy(a, k, block, to, src=None):
            dst = outs[a].at[_slot(block)]
            return pltpu.make_async_remote_copy(src_ref=dst if src is None else src, dst_ref=dst, send_sem=send_sems.at[a, k],
                                                recv_sem=recv_sems.at[a, k], device_id=to, device_id_type=MESH)

        mine = [pltpu.make_async_copy(ins[a], outs[a].at[_slot(me)], local_sems.at[a]) for a in range(n)]
        first = []
        for a in range(n):
            first.append(copy(a, 0, me, sibling, src=ins[a]))
            first += [copy(a, 1 + j, me, (*chip, c), src=ins[a]) for j, chip in enumerate(chips)]
        return n, c, me, sibling, chips, copy, mine, first

    def start(self, ins, outs, sems):
        _, _, _, _, _, _, mine, first = self._plan(ins, outs, sems)
        for cp in mine + first:
            cp.start()

    def relay(self, ins, outs, sems):
        n, c, me, sibling, chips, copy, _, _ = self._plan(ins, outs, sems)
        for j, chip in enumerate(chips):
            for a in range(n):
                copy(a, 1 + j, (*chip, c), me).wait_recv()
                copy(a, 4 + j, (*chip, c), sibling).start()

    def finish(self, ins, outs, sems):
        n, c, me, sibling, chips, copy, mine, first = self._plan(ins, outs, sems)
        passed = [copy(a, 4 + j, (*chip, c), sibling) for j, chip in enumerate(chips) for a in range(n)]
        for a in range(n):
            copy(a, 0, sibling, me).wait_recv()
            for j, chip in enumerate(chips):
                copy(a, 4 + j, (*chip, 1 - c), me).wait_recv()
        for cp in first + passed:
            cp.wait_send()
        for cp in mine:
            cp.wait()


class _PairSwap(_Exchange):
    def __init__(self, arrays):
        n = len(arrays)
        self.arrays = list(arrays)
        self.out_shapes = [jax.ShapeDtypeStruct((N_CHIPS,) + a.shape[1:], a.dtype) for a in arrays]
        self.scratch = [pltpu.SemaphoreType.DMA((n, N_CHIPS)), pltpu.SemaphoreType.DMA((n, N_CHIPS))]
        self.results = None

    def _plan(self, ins, outs, sems):
        send_sems, recv_sems = sems
        x, y, c = _position()
        return [pltpu.make_async_remote_copy(src_ref=ins[a].at[2 * q + 1 - c], dst_ref=outs[a].at[q], send_sem=send_sems.at[a, q],
                                             recv_sem=recv_sems.at[a, q], device_id=(x, y, 1 - c), device_id_type=MESH)
                for a in range(len(ins)) for q in range(N_CHIPS)]

    def start(self, ins, outs, sems):
        for cp in self._plan(ins, outs, sems):
            cp.start()

    def finish(self, ins, outs, sems):
        for cp in self._plan(ins, outs, sems):
            cp.wait()


class _ChipExchange(_Exchange):
    def __init__(self, arrays):
        n = len(arrays)
        self.arrays = list(arrays)
        self.out_shapes = [jax.ShapeDtypeStruct(a.shape, a.dtype) for a in arrays]
        self.scratch = [pltpu.SemaphoreType.DMA((n, 3)), pltpu.SemaphoreType.DMA((n, 3)), pltpu.SemaphoreType.DMA((n,))]
        self.results = None

    def _plan(self, ins, outs, sems):
        send_sems, recv_sems, local_sems = sems
        x, y, c = _position()
        here = 2 * x + y
        chips = [(1 - x, y), (x, 1 - y), (1 - x, 1 - y)]
        n = len(ins)

        def copy(a, k, src_slot, dst_slot):
            return pltpu.make_async_remote_copy(src_ref=ins[a].at[src_slot], dst_ref=outs[a].at[dst_slot], send_sem=send_sems.at[a, k],
                                                recv_sem=recv_sems.at[a, k], device_id=(*chips[k], c), device_id_type=MESH)

        there = [2 * qx + qy for qx, qy in chips]
        mine = [pltpu.make_async_copy(ins[a].at[here], outs[a].at[here], local_sems.at[a]) for a in range(n)]
        sends = [copy(a, k, there[k], here) for a in range(n) for k in range(3)]
        arrivals = lambda: [copy(a, k, here, there[k]) for a in range(n) for k in range(3)]
        return mine, sends, arrivals

    def start(self, ins, outs, sems):
        mine, sends, _ = self._plan(ins, outs, sems)
        for cp in mine + sends:
            cp.start()

    def finish(self, ins, outs, sems):
        mine, sends, arrivals = self._plan(ins, outs, sems)
        for cp in arrivals():
            cp.wait_recv()
        for cp in sends:
            cp.wait_send()
        for cp in mine:
            cp.wait()


def _call(body, *, name, grid, in_specs, out_specs, out_shape, args, scratch_shapes=(), sem=(), comm=(), aliases=None,
          vmem_limit=VMEM_LIMIT_BYTES):
    single = not isinstance(out_shape, (list, tuple))
    out_shape = [out_shape] if single else list(out_shape)
    out_specs = [out_specs] if single else list(out_specs)
    comms = list(comm or ())
    n_in, n_out, n_scr = len(args), len(out_shape), len(scratch_shapes)
    params = pltpu.CompilerParams(dimension_semantics=tuple(sem) if sem else None, vmem_limit_bytes=vmem_limit)
    if not comms:
        res = pl.pallas_call(body, name=name, grid=grid, in_specs=list(in_specs), out_specs=out_specs, out_shape=out_shape,
                             scratch_shapes=list(scratch_shapes), input_output_aliases=aliases or {}, compiler_params=params)(*args)
        return res[0] if single else res
    counts = [n_in] + [len(c.arrays) for c in comms] + [n_out] + [len(c.out_shapes) for c in comms] + [n_scr] + [len(c.scratch) for c in comms]
    nc = len(comms)

    def carried(*refs):
        pos, groups = 0, []
        for cnt in counts:
            groups.append(refs[pos:pos + cnt])
            pos += cnt
        ins, c_ins = groups[0], groups[1:1 + nc]
        outs, c_outs = groups[1 + nc], groups[2 + nc:2 + 2 * nc]
        scr, c_sems = groups[2 + 2 * nc], groups[3 + 2 * nc:]
        ids = [pl.program_id(d) for d in range(len(grid))]
        is_first = functools.reduce(jnp.logical_and, [i == 0 for i in ids])
        is_last = functools.reduce(jnp.logical_and, [i == g - 1 for i, g in zip(ids, grid)])

        @pl.when(is_first)
        def _():
            for q, c in enumerate(comms):
                c.start(c_ins[q], c_outs[q], c_sems[q])

        body(*ins, *outs, *scr)

        @pl.when(is_last)
        def _():
            for q, c in enumerate(comms):
                c.relay(c_ins[q], c_outs[q], c_sems[q])
                c.finish(c_ins[q], c_outs[q], c_sems[q])

    anyspec = pl.BlockSpec(memory_space=pl.ANY)
    c_arrays = [a for c in comms for a in c.arrays]
    c_shapes = [s for c in comms for s in c.out_shapes]
    res = pl.pallas_call(
        carried, name=name, grid=grid, in_specs=list(in_specs) + [anyspec] * len(c_arrays), out_specs=out_specs + [anyspec] * len(c_shapes),
        out_shape=out_shape + c_shapes, scratch_shapes=list(scratch_shapes) + [s for c in comms for s in c.scratch],
        input_output_aliases=aliases or {}, compiler_params=params)(*args, *c_arrays)
    pos = n_out
    for c in comms:
        c.results = list(res[pos:pos + len(c.out_shapes)])
        pos += len(c.out_shapes)
    return res[0] if single else list(res[:n_out])


def _run_exchange(comm, *, name):
    def body(*refs):
        n_ci, n_co = len(comm.arrays), len(comm.out_shapes)
        ins, outs, sems = refs[:n_ci], refs[n_ci:n_ci + n_co], refs[n_ci + n_co:]
        comm.start(ins, outs, sems)
        comm.relay(ins, outs, sems)
        comm.finish(ins, outs, sems)

    anyspec = pl.BlockSpec(memory_space=pl.ANY)
    comm.results = list(pl.pallas_call(
        body, name=name, in_specs=[anyspec] * len(comm.arrays), out_specs=[anyspec] * len(comm.out_shapes),
        out_shape=list(comm.out_shapes), scratch_shapes=list(comm.scratch))(*comm.arrays))
    return comm.results


def _mm(a, b, *, dims="nn", bias=None, res=None, out_dtype=F32, name, tm=1024, tn=1024, tk=1024, comm=None, b_rows=None,
        out_window=None, into=None, colsum_b=False):
    if dims == "tn":
        k_dim, m_dim = a.shape
    else:
        m_dim, k_dim = a.shape
    row0, n_rows = b_rows if b_rows is not None else (0, b.shape[0])
    n_dim = n_rows if dims == "nt" else b.shape[1]
    assert dims == "nt" or n_rows == k_dim, (name, a.shape, b.shape, b_rows)
    tm, tn, tk = min(tm, m_dim), min(tn, n_dim), min(tk, k_dim)
    assert m_dim % tm == 0 and n_dim % tn == 0 and k_dim % tk == 0, (name, a.shape, b.shape)
    nk = k_dim // tk
    a_spec = pl.BlockSpec((tk, tm), lambda i, j, k: (k, i)) if dims == "tn" else pl.BlockSpec((tm, tk), lambda i, j, k: (i, k))
    if dims == "nt":
        assert row0 % tn == 0
        b_spec = pl.BlockSpec((tn, tk), lambda i, j, k: (row0 // tn + j, k))
    else:
        assert row0 % tk == 0
        b_spec = pl.BlockSpec((tk, tn), lambda i, j, k: (row0 // tk + k, j))
    in_specs, args = [a_spec, b_spec], [a, b]
    if bias is not None:
        in_specs.append(pl.BlockSpec((1, tn), lambda i, j, k: (0, j)))
        args.append(bias)
    if res is not None:
        in_specs.append(pl.BlockSpec((tm, tn), lambda i, j, k: (i, j)))
        args.append(res)
    dn = _DIMS[dims]

    if colsum_b:
        assert dims == "tn" and m_dim == tm and into is None and out_window is None

    def body(*refs):
        a_ref, b_ref = refs[0], refs[1]
        acc_ref = refs[-1]
        o_ref = refs[-3] if colsum_b else refs[-2]
        k = pl.program_id(2)

        @pl.when(k == 0)
        def _():
            acc_ref[...] = jnp.zeros_like(acc_ref)
            if colsum_b:
                refs[-2][...] = jnp.zeros_like(refs[-2])

        acc_ref[...] += _dot(a_ref[...], b_ref[...], dn)
        if colsum_b:
            refs[-2][...] += jnp.sum(b_ref[...].astype(F32), axis=0, keepdims=True)

        @pl.when(k == nk - 1)
        def _():
            r = acc_ref[...]
            pos = 2
            if bias is not None:
                r = r + refs[pos][...]
                pos += 1
            if res is not None:
                r = r + refs[pos][...]
            o_ref[...] = r.astype(out_dtype)

    out_row0, out_rows = out_window if out_window is not None else (0, m_dim)
    assert out_row0 % tm == 0
    aliases = None
    if into is not None:
        assert into.shape == (out_rows, n_dim) and into.dtype == out_dtype
        in_specs.append(pl.BlockSpec(memory_space=pl.ANY))
        args.append(into)
        aliases = {len(args) - 1: 0}
    out_spec = pl.BlockSpec((tm, tn), lambda i, j, k: (out_row0 // tm + i, j))
    out_shape = jax.ShapeDtypeStruct((out_rows, n_dim), out_dtype)
    if colsum_b:
        out_spec = [out_spec, pl.BlockSpec((1, tn), lambda i, j, k: (0, j))]
        out_shape = [out_shape, jax.ShapeDtypeStruct((1, n_dim), F32)]
    return _call(
        body, name=name, grid=(m_dim // tm, n_dim // tn, nk), in_specs=in_specs, out_specs=out_spec, out_shape=out_shape,
        aliases=aliases, scratch_shapes=[pltpu.VMEM((tm, tn), F32)], sem=("parallel", "parallel", "arbitrary"), args=args, comm=comm)


def _mm_norm_bwd(a, b, x, nw, res, *, dims="nn", b_rows=None, add=None, name, tm=1024, tk=1024, comm=None):
    m_dim, k_dim = a.shape
    row0, n_rows = b_rows if b_rows is not None else (0, b.shape[0])
    d_dim = x.shape[1]
    tm, tk = min(tm, m_dim), min(tk, k_dim)
    assert m_dim % tm == 0 and k_dim % tk == 0 and (n_rows if dims == "nt" else b.shape[1]) == d_dim, (name, a.shape, b.shape)
    nk = k_dim // tk
    if dims == "nt":
        assert row0 % d_dim == 0
        b_spec = pl.BlockSpec((d_dim, tk), lambda i, k: (row0 // d_dim, k))
    else:
        assert row0 % tk == 0 and n_rows == k_dim
        b_spec = pl.BlockSpec((tk, d_dim), lambda i, k: (row0 // tk + k, 0))
    row = pl.BlockSpec((tm, d_dim), lambda i, k: (i, 0))
    vec = pl.BlockSpec((1, d_dim), lambda i, k: (0, 0))
    extra = ([add] if add is not None else []) + list(res)
    dn = _DIMS[dims]

    def body(*refs):
        a_ref, b_ref, x_ref, nw_ref = refs[:4]
        extra_refs = refs[4:4 + len(extra)]
        dx_ref, dnw_ref, dob_ref, acc_ref = refs[-4:]
        i, k = pl.program_id(0), pl.program_id(1)

        @pl.when(k == 0)
        def _():
            acc_ref[...] = jnp.zeros_like(acc_ref)

        @pl.when((i == 0) & (k == 0))
        def _():
            dnw_ref[...] = jnp.zeros_like(dnw_ref)

        acc_ref[...] += _dot(a_ref[...], b_ref[...], dn)

        @pl.when(k == nk - 1)
        def _():
            dh = acc_ref[...]
            rest = list(extra_refs)
            if add is not None:
                dh = dh + rest.pop(0)[...]
            xhat, r = _rms(x_ref[...])
            dxhat = dh * nw_ref[...]
            dx = r * (dxhat - xhat * jnp.mean(dxhat * xhat, axis=-1, keepdims=True))
            for rr in rest:
                dx = dx + rr[...]
            dx_ref[...] = dx
            dob_ref[...] = (FFN_RES_WEIGHT * dx).astype(BF16)
            dnw_ref[...] += jnp.sum(dh * xhat, axis=0, keepdims=True)

    return _call(
        body, name=name, grid=(m_dim // tm, nk),
        in_specs=[pl.BlockSpec((tm, tk), lambda i, k: (i, k)), b_spec, row, vec] + [row] * len(extra), out_specs=[row, vec, row],
        out_shape=[jax.ShapeDtypeStruct((m_dim, d_dim), F32), jax.ShapeDtypeStruct((1, d_dim), F32),
                   jax.ShapeDtypeStruct((m_dim, d_dim), BF16)],
        scratch_shapes=[pltpu.VMEM((tm, d_dim), F32)], sem=("arbitrary", "arbitrary"), args=[a, b, x, nw] + extra, comm=comm)


def _rope_rotate(x, cos_t, sin_t):
    rows, width = x.shape
    half = ATT_HEAD_DIM // 2
    lane = lax.broadcasted_iota(jnp.int32, (rows, width), 1)
    first = (lane % ATT_HEAD_DIM) < half
    rot = jnp.where(first, -pltpu.roll(x, width - half, 1), pltpu.roll(x, half, 1))
    reps = width // 128
    return x * jnp.tile(cos_t, (1, reps)) + rot * jnp.tile(sin_t, (1, reps))


def _norm_mm(x, nw, w, bias, *, name, tm=1024, tn=1024, comm=None, w_rows=None, rope=None):
    t_dim, d_dim = x.shape
    transposed = w_rows is not None
    n_dim = w_rows if transposed else w.shape[1]
    tn = min(tn, n_dim)
    assert t_dim % tm == 0 and n_dim % tn == 0
    has_bias = bias is not None
    w_spec = pl.BlockSpec((tn, d_dim), lambda i, j: (j, 0)) if transposed else pl.BlockSpec((d_dim, tn), lambda i, j: (0, j))
    dn = NT if transposed else NN
    in_specs = [pl.BlockSpec((tm, d_dim), lambda i, j: (i, 0)), pl.BlockSpec((1, d_dim), lambda i, j: (0, 0)), w_spec]
    args = [x, nw, w]
    if has_bias:
        in_specs.append(pl.BlockSpec((1, tn), lambda i, j: (0, j)))
        args.append(bias)
    if rope is not None:
        in_specs += [pl.BlockSpec((tm, LANES), lambda i, j: (i, 0))] * 2
        args += list(rope)

    def body(*refs):
        x_ref, nw_ref, w_ref = refs[:3]
        o_ref, h_ref = refs[-2], refs[-1]

        @pl.when(pl.program_id(1) == 0)
        def _():
            xhat, _ = _rms(x_ref[...])
            h_ref[...] = (xhat * nw_ref[...]).astype(BF16)

        r = _dot(h_ref[...], w_ref[...], dn)
        if has_bias:
            r = r + refs[3][...]
        if rope is not None:
            r = _rope_rotate(r, refs[-4][...], refs[-3][...])
        o_ref[...] = r

    return _call(
        body, name=name, grid=(t_dim // tm, n_dim // tn), in_specs=in_specs,
        out_specs=[pl.BlockSpec((tm, tn), lambda i, j: (i, j)), pl.BlockSpec((tm, d_dim), lambda i, j: (i, 0))],
        out_shape=[jax.ShapeDtypeStruct((t_dim, n_dim), F32), jax.ShapeDtypeStruct((t_dim, d_dim), BF16)],
        sem=("parallel", "arbitrary"), args=args, comm=comm)


def _norm_bwd(x, nw, dh, res, *, name, tm=512, comm=None):
    t_dim, d_dim = x.shape
    n_res = len(res)
    row = pl.BlockSpec((tm, d_dim), lambda i: (i, 0))
    vec = pl.BlockSpec((1, d_dim), lambda i: (0, 0))

    def body(*refs):
        x_ref, nw_ref, dh_ref = refs[:3]
        dx_ref, dnw_ref = refs[-2], refs[-1]
        xhat, r = _rms(x_ref[...])
        dh = dh_ref[...]
        dxhat = dh * nw_ref[...]
        dx = r * (dxhat - xhat * jnp.mean(dxhat * xhat, axis=-1, keepdims=True))
        for rr in refs[3:3 + n_res]:
            dx = dx + rr[...]
        dx_ref[...] = dx

        @pl.when(pl.program_id(0) == 0)
        def _():
            dnw_ref[...] = jnp.zeros_like(dnw_ref)

        dnw_ref[...] += jnp.sum(dh * xhat, axis=0, keepdims=True)

    return _call(
        body, name=name, grid=(t_dim // tm,), in_specs=[row, vec, row] + [row] * n_res,
        out_specs=[row, vec],
        out_shape=[jax.ShapeDtypeStruct((t_dim, d_dim), F32), jax.ShapeDtypeStruct((1, d_dim), F32)],
        sem=("arbitrary",), args=[x, nw, dh, *res], comm=comm)


FFN_ROW_TILE = 512
FFN_SHARDS_PER_STEP = 2
FFN_STEPS = N_DEV // FFN_SHARDS_PER_STEP
FFN_STEP_COLS = FFN_SHARDS_PER_STEP * D_FF_SHARD


def _ffn_step_view(w):
    return w.reshape(FFN_STEPS, FFN_STEP_COLS, w.shape[-1])


def _ffn_specs(t_dim, d_dim):
    full = pl.BlockSpec((t_dim, d_dim), lambda j: (0, 0))
    wspec = pl.BlockSpec((None, FFN_STEP_COLS, d_dim), lambda j: (j, 0, 0))
    pre = pl.BlockSpec((None, t_dim, FFN_STEP_COLS), lambda j: (j, 0, 0))
    return full, wspec, pre


def _ffn_fwd(x, nw, wg, wu, wd, *, name, comm=None):
    t_dim, d_dim = x.shape
    n_tiles = t_dim // FFN_ROW_TILE

    def body(x_ref, nw_ref, wg_ref, wu_ref, wd_ref, o_ref, g_ref, u_ref, h_ref):
        j = pl.program_id(0)

        @pl.when(j == 0)
        def _():
            xhat, _ = _rms(x_ref[...])
            h_ref[...] = (xhat * nw_ref[...]).astype(BF16)
            o_ref[...] = jnp.zeros_like(o_ref)

        for t in range(n_tiles):
            rows = pl.ds(t * FFN_ROW_TILE, FFN_ROW_TILE)
            h = h_ref[rows, :]
            g = _dot(h, wg_ref[...], NT)
            u = _dot(h, wu_ref[...], NT)
            g_ref[rows, :] = g.astype(BF16)
            u_ref[rows, :] = u.astype(BF16)
            o_ref[rows, :] += _dot(g * _sigmoid(g) * u, wd_ref[...])

        @pl.when(j == FFN_STEPS - 1)
        def _():
            o_ref[...] = x_ref[...] + FFN_RES_WEIGHT * o_ref[...]

    full, wspec, pre = _ffn_specs(t_dim, d_dim)
    pre_shape = jax.ShapeDtypeStruct((FFN_STEPS, t_dim, FFN_STEP_COLS), BF16)
    return _call(
        body, name=name, grid=(FFN_STEPS,),
        in_specs=[full, pl.BlockSpec((1, d_dim), lambda j: (0, 0)), wspec, wspec, wspec],
        out_specs=[full, pre, pre, full],
        out_shape=[jax.ShapeDtypeStruct((t_dim, d_dim), F32), pre_shape, pre_shape, jax.ShapeDtypeStruct((t_dim, d_dim), BF16)],
        sem=("arbitrary",), args=[x, nw, _ffn_step_view(wg), _ffn_step_view(wu), _ffn_step_view(wd)], comm=comm)


def _ffn_bwd(h, dob, pre_g, pre_u, wg, wu, wd, *, name, comm=None):
    t_dim, d_dim = h.shape
    n_tiles = t_dim // FFN_ROW_TILE

    def body(h_ref, dob_ref, g_ref, u_ref, wg_ref, wu_ref, wd_ref, dh_ref, gg_ref, gu_ref, gd_ref, dwg_scr, dwu_scr, dwd_scr):
        @pl.when(pl.program_id(0) == 0)
        def _():
            dh_ref[...] = jnp.zeros_like(dh_ref)

        for t in range(n_tiles):
            rows = pl.ds(t * FFN_ROW_TILE, FFN_ROW_TILE)
            hh = h_ref[rows, :]
            do = dob_ref[rows, :]
            g = g_ref[rows, :].astype(F32)
            u = u_ref[rows, :].astype(F32)
            sg = _sigmoid(g)
            s = g * sg
            da = _dot(do, wd_ref[...], NT)
            dwd = _dot(s * u, do, TN)
            du = (da * s).astype(BF16)
            dg = (da * u * _dsilu(g, sg)).astype(BF16)
            dwg = _dot(dg, hh, TN)
            dwu = _dot(du, hh, TN)
            if t == 0:
                dwd_scr[...] = dwd
                dwg_scr[...] = dwg
                dwu_scr[...] = dwu
            else:
                dwd_scr[...] += dwd
                dwg_scr[...] += dwg
                dwu_scr[...] += dwu
            dh_ref[rows, :] += _dot(dg, wg_ref[...]) + _dot(du, wu_ref[...])
        gg_ref[...] = dwg_scr[...].astype(BF16)
        gu_ref[...] = dwu_scr[...].astype(BF16)
        gd_ref[...] = dwd_scr[...].astype(BF16)

    full, wspec, pre = _ffn_specs(t_dim, d_dim)
    gspec = pl.BlockSpec((None, FFN_STEP_COLS, d_dim), lambda j: (j, 0, 0), pipeline_mode=pl.Buffered(1))
    grad_shape = jax.ShapeDtypeStruct((FFN_STEPS, FFN_STEP_COLS, d_dim), BF16)
    dh, gg, gu, gd = _call(
        body, name=name, grid=(FFN_STEPS,),
        in_specs=[full, full, pre, pre, wspec, wspec, wspec], out_specs=[full, gspec, gspec, gspec],
        out_shape=[jax.ShapeDtypeStruct((t_dim, d_dim), F32)] + [grad_shape] * 3,
        scratch_shapes=[pltpu.VMEM((FFN_STEP_COLS, d_dim), F32)] * 3, sem=("arbitrary",), vmem_limit=FFN_BWD_VMEM_LIMIT_BYTES,
        args=[h, dob, pre_g, pre_u, _ffn_step_view(wg), _ffn_step_view(wu), _ffn_step_view(wd)], comm=comm)
    return dh, gg.reshape(wg.shape), gu.reshape(wu.shape), gd.reshape(wd.shape)


CONV_COLS = 256


def _shift_down(u, s, rows):
    return jnp.where(rows >= s, pltpu.roll(u, s, 0), 0.0)


def _shift_up(u, s, rows, t_dim):
    return jnp.where(rows < t_dim - s, pltpu.roll(u, t_dim - s, 0), 0.0)


def _conv_pre(u, w_ref, b_ref, rows):
    c = b_ref[...] + w_ref[CONV_WIDTH - 1:CONV_WIDTH, :] * u
    for k in range(CONV_WIDTH - 1):
        c = c + w_ref[k:k + 1, :] * _shift_down(u, CONV_WIDTH - 1 - k, rows)
    return c


def _conv_fwd(zx, cw, cb, *, name, comm=None):
    t_dim = zx.shape[0]
    off = D_INNER // CONV_COLS

    def body(u_ref, w_ref, b_ref, o_ref):
        rows = lax.broadcasted_iota(jnp.int32, (t_dim, CONV_COLS), 0)
        c = _conv_pre(u_ref[...], w_ref, b_ref, rows)
        o_ref[...] = c * _sigmoid(c)

    return _call(
        body, name=name, grid=(CONV_DIM // CONV_COLS,),
        in_specs=[pl.BlockSpec((t_dim, CONV_COLS), lambda j: (0, off + j)),
                  pl.BlockSpec((CONV_WIDTH, CONV_COLS), lambda j: (0, j)), pl.BlockSpec((1, CONV_COLS), lambda j: (0, j))],
        out_specs=pl.BlockSpec((t_dim, CONV_COLS), lambda j: (0, j)),
        out_shape=jax.ShapeDtypeStruct((t_dim, CONV_DIM), F32), sem=("parallel",), args=[zx, cw, cb], comm=comm)


def _conv_bwd(zx, cw, cb, dxs, db, dc, dzx, *, name, comm=None):
    t_dim = zx.shape[0]
    off = D_INNER // CONV_COLS
    n_xs = D_INNER // CONV_COLS
    n_b = GN // CONV_COLS

    def body(u_ref, w_ref, b_ref, dxs_ref, db_ref, dc_ref, dzx_in, dzx_ref, dw_ref, dbias_ref):
        j = pl.program_id(0)
        rows = lax.broadcasted_iota(jnp.int32, (t_dim, CONV_COLS), 0)
        u = u_ref[...]
        c = _conv_pre(u, w_ref, b_ref, rows)
        d = jnp.where(j < n_xs, dxs_ref[...], jnp.where(j < n_xs + n_b, db_ref[...], dc_ref[...]))
        dcv = d * _dsilu(c, _sigmoid(c))
        dpre = w_ref[CONV_WIDTH - 1:CONV_WIDTH, :] * dcv
        dw_ref[CONV_WIDTH - 1:CONV_WIDTH, :] = jnp.sum(dcv * u, axis=0, keepdims=True)
        for k in range(CONV_WIDTH - 1):
            s = CONV_WIDTH - 1 - k
            dpre = dpre + w_ref[k:k + 1, :] * _shift_up(dcv, s, rows, t_dim)
            dw_ref[k:k + 1, :] = jnp.sum(dcv * _shift_down(u, s, rows), axis=0, keepdims=True)
        dzx_ref[...] = dpre
        dbias_ref[...] = jnp.sum(dcv, axis=0, keepdims=True)

    blk = lambda n: pl.BlockSpec((t_dim, CONV_COLS), n)
    return _call(
        body, name=name, grid=(CONV_DIM // CONV_COLS,),
        in_specs=[blk(lambda j: (0, off + j)), pl.BlockSpec((CONV_WIDTH, CONV_COLS), lambda j: (0, j)),
                  pl.BlockSpec((1, CONV_COLS), lambda j: (0, j)),
                  blk(lambda j: (0, jnp.minimum(j, n_xs - 1))),
                  blk(lambda j: (0, jnp.clip(j - n_xs, 0, n_b - 1))),
                  blk(lambda j: (0, jnp.clip(j - n_xs - n_b, 0, n_b - 1))),
                  pl.BlockSpec(memory_space=pl.ANY)],
        out_specs=[blk(lambda j: (0, off + j)), pl.BlockSpec((CONV_WIDTH, CONV_COLS), lambda j: (0, j)),
                   pl.BlockSpec((1, CONV_COLS), lambda j: (0, j))],
        out_shape=[jax.ShapeDtypeStruct(dzx.shape, F32), jax.ShapeDtypeStruct((CONV_WIDTH, CONV_DIM), F32),
                   jax.ShapeDtypeStruct((1, CONV_DIM), F32)],
        aliases={6: 0}, sem=("parallel",), args=[zx, cw, cb, dxs, db, dc, dzx], comm=comm)


def _softplus_parts(x):
    e = jnp.exp(-jnp.abs(x))
    u = 1.0 + e
    log1p_e = jnp.where(u == 1.0, e, jnp.log(u) * e / jnp.where(u == 1.0, 1.0, u - 1.0))
    return jnp.maximum(x, 0.0) + log1p_e


def _dt_prep(dtr, dt_bias, a_log, *, name):
    def body(dtr_ref, bias_ref, alog_ref, dt_ref, a_ref):
        dt = _softplus_parts(dtr_ref[...] + bias_ref[...])
        dt_ref[...] = dt
        a_ref[...] = dt * (-jnp.exp(alog_ref[...]))

    return pl.pallas_call(body, name=name, out_shape=[jax.ShapeDtypeStruct(dtr.shape, F32)] * 2,
                          compiler_params=_params())(dtr, dt_bias, a_log)


def _dt_bwd(dtr, dt_bias, a_log, dt, ddt, da, *, name):
    def body(dtr_ref, bias_ref, alog_ref, dt_ref, ddt_ref, da_ref, ddtr_ref, dbias_ref, dalog_ref):
        a_neg = -jnp.exp(alog_ref[...])
        da_v = da_ref[...]
        ddt_tot = ddt_ref[...] + da_v * a_neg
        ddtr = ddt_tot * _sigmoid(dtr_ref[...] + bias_ref[...])
        ddtr_ref[...] = ddtr
        dbias_ref[...] = jnp.sum(ddtr, axis=0, keepdims=True)
        dalog_ref[...] = jnp.sum(da_v * dt_ref[...], axis=0, keepdims=True) * a_neg

    return pl.pallas_call(
        body, name=name,
        out_shape=[jax.ShapeDtypeStruct(dtr.shape, F32), jax.ShapeDtypeStruct((1, SSM_HEADS), F32),
                   jax.ShapeDtypeStruct((1, SSM_HEADS), F32)],
        compiler_params=_params())(dtr, dt_bias, a_log, dt, ddt, da)


GROUP_COLS = HEADS_PER_GROUP * SSM_HEAD_DIM
LANES = 128
HEADS_PER_LANE_BLOCK = LANES // SSM_HEAD_DIM


def _split3(x):
    hi = x.astype(BF16)
    r1 = x - hi.astype(F32)
    mid = r1.astype(BF16)
    lo = (r1 - mid.astype(F32)).astype(BF16)
    return hi, mid, lo


def _dot_select(a, b, dims=NN, data=0):
    out = None
    for part in _split3(a if data == 0 else b):
        lhs, rhs = (part, b.astype(BF16)) if data == 0 else (a.astype(BF16), part)
        t = lax.dot_general(lhs, rhs, dims, preferred_element_type=F32)
        out = t if out is None else out + t
    return out


def _group_sums(vals, expand):
    out = _dot_select(jnp.concatenate(vals, axis=0), expand, NT)
    return [out[i * CHUNK:(i + 1) * CHUNK] for i in range(len(vals))]


def _ssd_chunk_common(a_ref, dt_ref, b_ref, c_ref):
    row = lax.broadcasted_iota(jnp.int32, (CHUNK, CHUNK), 0)
    col = lax.broadcasted_iota(jnp.int32, (CHUNK, CHUNK), 1)
    causal = col <= row
    lower = causal.astype(F32)
    upper = (col >= row).astype(F32)
    head = lax.broadcasted_iota(jnp.int32, (HEADS_PER_GROUP, GROUP_COLS), 0)
    lane = lax.broadcasted_iota(jnp.int32, (HEADS_PER_GROUP, GROUP_COLS), 1)
    expand = ((lane >= head * SSM_HEAD_DIM) & (lane < (head + 1) * SSM_HEAD_DIM)).astype(F32)
    a = a_ref[...]
    cs = _dot_select(lower, a, data=1)
    cs_row = _dot_select(a, upper, TN)
    cs_x = _dot_select(cs, expand)
    dt_x = _dot_select(dt_ref[...], expand)
    e_out_x = jnp.exp(cs_x)
    e_st_x = jnp.exp(cs_x[CHUNK - 1:CHUNK, :] - cs_x)
    bc = b_ref[...]
    cc = c_ref[...]
    cb = _dot(cc, bc, NT)
    return causal, upper, expand.astype(BF16), cs, cs_row, dt_x, e_out_x, e_st_x, bc, cc, cb


def _head_decay(causal, cs, cs_row, h):
    return jnp.exp(jnp.where(causal, cs[:, h:h + 1] - cs_row[h:h + 1, :], NEG_BIG))


def _lane_block_head_masks():
    lane = lax.broadcasted_iota(jnp.int32, (CHUNK, LANES), 1)
    return [(lane >= i * SSM_HEAD_DIM) & (lane < (i + 1) * SSM_HEAD_DIM) for i in range(HEADS_PER_LANE_BLOCK)]


def _decay_state(dst_ref, old, new, cs):
    for h in range(HEADS_PER_GROUP):
        rows = slice(h * SSM_HEAD_DIM, (h + 1) * SSM_HEAD_DIM)
        dst_ref[rows, :] = jnp.exp(cs[CHUNK - 1:CHUNK, h:h + 1]) * old[rows, :] + new[rows, :]


def _ssd_fwd(xbc, dtg, ag, dgx, *, name, comm=None):
    t_dim = xbc.shape[0]

    def body(xs_ref, b_ref, c_ref, dt_ref, a_ref, d_ref, y_ref, st_ref, s_scr):
        @pl.when(pl.program_id(1) == 0)
        def _():
            s_scr[...] = jnp.zeros_like(s_scr)

        causal, _, _, cs, cs_row, dt_x, e_out_x, e_st_x, bc, cc, cb = _ssd_chunk_common(a_ref, dt_ref, b_ref, c_ref)
        masks = _lane_block_head_masks()
        xs = xs_ref[...]
        xdt_x = xs * dt_x
        prev = s_scr[...]
        st_ref[...] = prev
        y_off = e_out_x * _dot(cc, prev, NT) + xs * d_ref[...]
        for blk in range(GROUP_COLS // LANES):
            lanes = slice(blk * LANES, (blk + 1) * LANES)
            x_b = xdt_x[:, lanes].astype(BF16)
            acc = y_off[:, lanes]
            for i in range(HEADS_PER_LANE_BLOCK):
                m = cb * _head_decay(causal, cs, cs_row, blk * HEADS_PER_LANE_BLOCK + i)
                acc = acc + _dot(m, jnp.where(masks[i], x_b, jnp.zeros_like(x_b)))
            y_ref[:, lanes] = acc
        _decay_state(s_scr, prev, _dot(xdt_x * e_st_x, bc, TN), cs)

    xs = pl.BlockSpec((CHUNK, GROUP_COLS), lambda g, c: (c, g))
    bsp = pl.BlockSpec((CHUNK, SSM_STATE), lambda g, c: (c, D_INNER // SSM_STATE + g))
    csp = pl.BlockSpec((CHUNK, SSM_STATE), lambda g, c: (c, (D_INNER + GN) // SSM_STATE + g))
    per_head = pl.BlockSpec((None, CHUNK, HEADS_PER_GROUP), lambda g, c: (g, c, 0))
    dsk = pl.BlockSpec((None, 1, GROUP_COLS), lambda g, c: (g, 0, 0))
    return _call(
        body, name=name, grid=(SSM_GROUPS, N_CHUNKS),
        in_specs=[xs, bsp, csp, per_head, per_head, dsk],
        out_specs=[xs, pl.BlockSpec((None, GROUP_COLS, SSM_STATE), lambda g, c: (c, g, 0))],
        out_shape=[jax.ShapeDtypeStruct((t_dim, D_INNER), F32),
                   jax.ShapeDtypeStruct((N_CHUNKS, D_INNER, SSM_STATE), F32)],
        scratch_shapes=[pltpu.VMEM((GROUP_COLS, SSM_STATE), F32)],
        sem=("parallel", "arbitrary"), args=[xbc, xbc, xbc, dtg, ag, dgx], comm=comm)


def _ssd_bwd(xbc, dtg, ag, dgx, states, dy, *, name, comm=None):
    t_dim = xbc.shape[0]
    last = N_CHUNKS - 1

    def body(xs_ref, b_ref, c_ref, dt_ref, a_ref, d_ref, st_ref, dy_ref,
             dxs_ref, db_ref, dc_ref, ddt_ref, da_ref, dd_ref, ds_scr):
        @pl.when(pl.program_id(1) == 0)
        def _():
            ds_scr[...] = jnp.zeros_like(ds_scr)
            dd_ref[...] = jnp.zeros_like(dd_ref)

        causal, upper, expand, cs, cs_row, dt_x, e_out_x, e_st_x, bc, cc, cb = _ssd_chunk_common(a_ref, dt_ref, b_ref, c_ref)
        masks = _lane_block_head_masks()
        xs = xs_ref[...]
        dy_x = dy_ref[...]
        xdt_x = xs * dt_x
        prev = st_ref[...]
        d_s = ds_scr[...]
        g1_x = _dot(bc, d_s, NT)
        cp_x = _dot(cc, prev, NT)
        d_cb = jnp.zeros((CHUNK, CHUNK), F32)
        lane8 = lax.broadcasted_iota(jnp.int32, (CHUNK, HEADS_PER_GROUP), 1)
        sub8 = lax.broadcasted_iota(jnp.int32, (HEADS_PER_GROUP, CHUNK), 0)
        row_w = jnp.zeros((CHUNK, HEADS_PER_GROUP), F32)
        col_w = jnp.zeros((HEADS_PER_GROUP, CHUNK), F32)
        dxdt_blocks = []
        for blk in range(GROUP_COLS // LANES):
            lanes = slice(blk * LANES, (blk + 1) * LANES)
            dy_b = dy_x[:, lanes].astype(BF16)
            x_b = xdt_x[:, lanes].astype(BF16)
            acc_dx = jnp.zeros((CHUNK, LANES), F32)
            for i in range(HEADS_PER_LANE_BLOCK):
                h = blk * HEADS_PER_LANE_BLOCK + i
                decay = _head_decay(causal, cs, cs_row, h)
                m = cb * decay
                dy_h = jnp.where(masks[i], dy_b, jnp.zeros_like(dy_b))
                acc_dx = acc_dx + _dot(m, dy_h, TN)
                d_m = _dot(dy_h, x_b, NT)
                d_cb = d_cb + d_m * decay
                w = d_m * m
                row_w = jnp.where(lane8 == h, jnp.sum(w, axis=1, keepdims=True), row_w)
                col_w = jnp.where(sub8 == h, jnp.sum(w, axis=0, keepdims=True), col_w)
            dxdt_blocks.append(acc_dx)
        dxdt_x = jnp.concatenate(dxdt_blocks, axis=1) + e_st_x * g1_x
        dxs_ref[...] = dxdt_x * dt_x + dy_x * d_ref[...]
        dye = dy_x * e_out_x
        xde = xdt_x * e_st_x
        ddt, y_off, tl, dskip = _group_sums([dxdt_x * xs, dye * cp_x, xde * g1_x, dy_x * xs], expand)
        ddt_ref[...] = ddt
        dd_ref[...] += jnp.sum(dskip, axis=0, keepdims=True)
        sp = None
        for part in _split3(d_s * prev):
            t = lax.dot_general(expand, part, NN, preferred_element_type=F32)
            sp = t if sp is None else sp + t
        last_col = jnp.exp(cs_row[:, CHUNK - 1:CHUNK]) * jnp.sum(sp, axis=1, keepdims=True)
        eye = lax.broadcasted_iota(jnp.int32, (HEADS_PER_GROUP, HEADS_PER_GROUP), 0) == lax.broadcasted_iota(
            jnp.int32, (HEADS_PER_GROUP, HEADS_PER_GROUP), 1)
        last_row = jnp.sum(jnp.where(eye, last_col, 0.0), axis=0, keepdims=True) + jnp.sum(tl, axis=0, keepdims=True)
        is_last = lax.broadcasted_iota(jnp.int32, (CHUNK, 1), 0) == CHUNK - 1
        d_cs = row_w + y_off - tl + jnp.where(is_last, last_row, 0.0)
        da_ref[...] = _dot_select(upper, d_cs, data=1) - _dot_select(upper, col_w, NT, data=1)
        dc_ref[...] = _dot(d_cb, bc) + _dot(dye, prev)
        db_ref[...] = _dot(d_cb, cc, TN) + _dot(xde, d_s)
        _decay_state(ds_scr, d_s, _dot(dye, cc, TN), cs)

    rev = lambda c: last - c
    xs = pl.BlockSpec((CHUNK, GROUP_COLS), lambda g, c: (rev(c), g))
    bsp = pl.BlockSpec((CHUNK, SSM_STATE), lambda g, c: (rev(c), D_INNER // SSM_STATE + g))
    csp = pl.BlockSpec((CHUNK, SSM_STATE), lambda g, c: (rev(c), (D_INNER + GN) // SSM_STATE + g))
    per_head = pl.BlockSpec((None, CHUNK, HEADS_PER_GROUP), lambda g, c: (g, rev(c), 0))
    dsk = pl.BlockSpec((None, 1, GROUP_COLS), lambda g, c: (g, 0, 0))
    dsum = pl.BlockSpec((None, 1, HEADS_PER_GROUP), lambda g, c: (g, 0, 0))
    st = pl.BlockSpec((None, GROUP_COLS, SSM_STATE), lambda g, c: (rev(c), g, 0))
    grp = pl.BlockSpec((CHUNK, SSM_STATE), lambda g, c: (rev(c), g))
    return _call(
        body, name=name, grid=(SSM_GROUPS, N_CHUNKS),
        in_specs=[xs, bsp, csp, per_head, per_head, dsk, st, xs],
        out_specs=[xs, grp, grp, per_head, per_head, dsum],
        out_shape=[jax.ShapeDtypeStruct((t_dim, D_INNER), F32), jax.ShapeDtypeStruct((t_dim, GN), F32),
                   jax.ShapeDtypeStruct((t_dim, GN), F32),
                   jax.ShapeDtypeStruct((SSM_GROUPS, t_dim, HEADS_PER_GROUP), F32),
                   jax.ShapeDtypeStruct((SSM_GROUPS, t_dim, HEADS_PER_GROUP), F32),
                   jax.ShapeDtypeStruct((SSM_GROUPS, 1, HEADS_PER_GROUP), F32)],
        scratch_shapes=[pltpu.VMEM((GROUP_COLS, SSM_STATE), F32)],
        sem=("parallel", "arbitrary"), args=[xbc, xbc, xbc, dtg, ag, dgx, states, dy], comm=comm)


NORM_GROUP = D_INNER // SSM_GROUPS


def _gate_norm_fwd(y, zx, nw, *, name, tm=256):
    t_dim = y.shape[0]
    row = pl.BlockSpec((tm, D_INNER), lambda i: (i, 0))

    def body(y_ref, z_ref, nw_ref, o_ref):
        z = z_ref[...]
        yz = y_ref[...] * (z * _sigmoid(z))
        for g in range(SSM_GROUPS):
            cols = slice(g * NORM_GROUP, (g + 1) * NORM_GROUP)
            yhat, _ = _rms(yz[:, cols])
            o_ref[:, cols] = (yhat * nw_ref[:, cols]).astype(BF16)

    return pl.pallas_call(
        body, name=name, grid=(t_dim // tm,), in_specs=[row, row, pl.BlockSpec((1, D_INNER), lambda i: (0, 0))],
        out_specs=row, out_shape=jax.ShapeDtypeStruct((t_dim, D_INNER), BF16),
        compiler_params=_params("parallel"),
    )(y, zx, nw)


def _gate_norm_bwd(y, zx, nw, dyn, *, name, tm=256):
    t_dim = y.shape[0]
    row = pl.BlockSpec((tm, D_INNER), lambda i: (i, 0))
    vec = pl.BlockSpec((1, D_INNER), lambda i: (0, 0))

    def body(y_ref, z_ref, nw_ref, dyn_ref, dy_ref, dz_ref, dnw_ref):
        @pl.when(pl.program_id(0) == 0)
        def _():
            dnw_ref[...] = jnp.zeros_like(dnw_ref)

        z = z_ref[...]
        yv = y_ref[...]
        sg = _sigmoid(z)
        silu_z = z * sg
        yz = yv * silu_z
        dyn_v = dyn_ref[...]
        for g in range(SSM_GROUPS):
            cols = slice(g * NORM_GROUP, (g + 1) * NORM_GROUP)
            yhat, r = _rms(yz[:, cols])
            dn = dyn_v[:, cols]
            dnw_ref[:, cols] += jnp.sum(dn * yhat, axis=0, keepdims=True)
            dyhat = dn * nw_ref[:, cols]
            dyz = r * (dyhat - yhat * jnp.mean(dyhat * yhat, axis=-1, keepdims=True))
            dy_ref[:, cols] = dyz * silu_z[:, cols]
            dz_ref[:, cols] = dyz * yv[:, cols] * _dsilu(z[:, cols], sg[:, cols])

    return pl.pallas_call(
        body, name=name, grid=(t_dim // tm,), in_specs=[row, row, vec, row],
        out_specs=[row, row, vec],
        out_shape=[jax.ShapeDtypeStruct((t_dim, D_INNER), F32), jax.ShapeDtypeStruct((t_dim, ZX_DIM), F32),
                   jax.ShapeDtypeStruct((1, D_INNER), F32)],
        compiler_params=_params("arbitrary"),
    )(y, zx, nw, dyn)


HEADS_PER_LANE_TILE = LANES // ATT_HEAD_DIM
STACKED_ROWS = Q_PER_KV * WINDOW


def _att_half_masks():
    lane = lax.broadcasted_iota(jnp.int32, (WINDOW, LANES), 1)
    return [(lane >= i * ATT_HEAD_DIM) & (lane < (i + 1) * ATT_HEAD_DIM) for i in range(HEADS_PER_LANE_TILE)]


def _att_stack_heads(ref, kvh, masks):
    parts = []
    for g in range(Q_PER_KV):
        h = kvh * Q_PER_KV + g
        blk = ref[:, (h // HEADS_PER_LANE_TILE) * LANES:(h // HEADS_PER_LANE_TILE + 1) * LANES]
        parts.append(jnp.where(masks[h % HEADS_PER_LANE_TILE], blk, jnp.zeros_like(blk)))
    return jnp.concatenate(parts, axis=0)


def _att_kv_tile(ref, kvh, masks):
    blk = ref[:, (kvh // HEADS_PER_LANE_TILE) * LANES:(kvh // HEADS_PER_LANE_TILE + 1) * LANES]
    return jnp.where(masks[kvh % HEADS_PER_LANE_TILE], blk, pltpu.roll(blk, ATT_HEAD_DIM, 1)).astype(BF16)


def _att_stacked_masks(n):
    row = lax.bitwise_and(lax.broadcasted_iota(jnp.int32, (STACKED_ROWS, WINDOW), 0), WINDOW - 1)
    col = lax.broadcasted_iota(jnp.int32, (STACKED_ROWS, WINDOW), 1)
    return col <= row, (col > row) & (n > 0)


def _att_stack_columns(ref, kvh, rows):
    cols = [ref[:, kvh * Q_PER_KV + g:kvh * Q_PER_KV + g + 1] for g in range(Q_PER_KV)]
    return jnp.concatenate([jnp.broadcast_to(c, (rows, 1)) for c in cols], axis=0)


def _att_scores(q4, k_tile, mask):
    return jnp.where(mask, _dot(q4, k_tile, NT) * ATT_SCALE, NEG_BIG)


def _att_unstack(x4, kvh, masks, tiles):
    for g in range(Q_PER_KV):
        h = kvh * Q_PER_KV + g
        piece = x4[g * WINDOW:(g + 1) * WINDOW]
        t = h // HEADS_PER_LANE_TILE
        tiles[t] = piece if h % HEADS_PER_LANE_TILE == 0 else jnp.where(masks[1], piece, tiles[t])


def _attn_fwd(q, k, v, sinks, *, name, comm=None):
    t_dim = q.shape[0]

    def body(q_ref, kc_ref, kp_ref, vc_ref, vp_ref, s_ref, o_ref, l_ref):
        n = pl.program_id(0)
        masks = _att_half_masks()
        mask_c, mask_p = _att_stacked_masks(n)
        out_tiles = [None] * (D_MODEL // LANES)
        for kvh in range(N_KV_HEADS):
            q4 = _att_stack_heads(q_ref, kvh, masks).astype(BF16)
            kc, kp = _att_kv_tile(kc_ref, kvh, masks), _att_kv_tile(kp_ref, kvh, masks)
            vc, vp = _att_kv_tile(vc_ref, kvh, masks), _att_kv_tile(vp_ref, kvh, masks)
            sc = _att_scores(q4, kc, mask_c)
            sp = _att_scores(q4, kp, mask_p)
            sink = _att_stack_columns(s_ref, kvh, WINDOW)
            m = jnp.maximum(jnp.maximum(jnp.max(sc, axis=1, keepdims=True), jnp.max(sp, axis=1, keepdims=True)), sink)
            pc = jnp.exp(sc - m)
            pp = jnp.exp(sp - m)
            den = jnp.sum(pc, axis=1, keepdims=True) + jnp.sum(pp, axis=1, keepdims=True) + jnp.exp(sink - m)
            _att_unstack((_dot(pc, vc) + _dot(pp, vp)) / den, kvh, masks, out_tiles)
            lse4 = m + jnp.log(den)
            for g in range(Q_PER_KV):
                h = kvh * Q_PER_KV + g
                l_ref[:, h:h + 1] = lse4[g * WINDOW:(g + 1) * WINDOW]
        for t, tile in enumerate(out_tiles):
            o_ref[:, t * LANES:(t + 1) * LANES] = tile

    cur = lambda w: pl.BlockSpec((WINDOW, w), lambda n: (n, 0))
    prv = lambda w: pl.BlockSpec((WINDOW, w), lambda n: (jnp.maximum(n - 1, 0), 0))
    return _call(
        body, name=name, grid=(t_dim // WINDOW,),
        in_specs=[cur(D_MODEL), cur(KV_DIM), prv(KV_DIM), cur(KV_DIM), prv(KV_DIM), pl.BlockSpec((1, N_Q_HEADS), lambda n: (0, 0))],
        out_specs=[cur(D_MODEL), cur(N_Q_HEADS)],
        out_shape=[jax.ShapeDtypeStruct((t_dim, D_MODEL), F32), jax.ShapeDtypeStruct((t_dim, N_Q_HEADS), F32)],
        sem=("parallel",), args=[q, k, k, v, v, sinks], comm=comm)


def _attn_bwd(q, k, v, sinks, o, lse, do, cos2, sin2, *, name, comm=None):
    t_dim = q.shape[0]

    def body(q_ref, kc_ref, kp_ref, vc_ref, vp_ref, s_ref, o_ref, l_ref, do_ref, cos_ref, sin_ref, cos_all_ref, sin_all_ref,
             dq_ref, dk_ref, dv_ref, dsink_ref):
        n = pl.program_id(0)

        @pl.when(n == 0)
        def _():
            dk_ref[...] = jnp.zeros_like(dk_ref)
            dv_ref[...] = jnp.zeros_like(dv_ref)
            dsink_ref[...] = jnp.zeros_like(dsink_ref)

        masks = _att_half_masks()
        mask_c, mask_p = _att_stacked_masks(n)
        lane_row = lax.broadcasted_iota(jnp.int32, (1, N_Q_HEADS), 1)
        rows_c = pl.ds(pl.multiple_of(n * WINDOW, WINDOW), WINDOW)
        rows_p = pl.ds(pl.multiple_of(jnp.maximum(n - 1, 0) * WINDOW, WINDOW), WINDOW)
        dsink = jnp.zeros((1, N_Q_HEADS), F32)
        dq_tiles = [None] * (D_MODEL // LANES)
        kv_tiles = KV_DIM // LANES
        dkc_tiles, dkp_tiles, dvc_tiles, dvp_tiles = ([None] * kv_tiles for _ in range(4))

        def place(tiles, kvh, x):
            folded = x + pltpu.roll(x, ATT_HEAD_DIM, 1)
            t = kvh // HEADS_PER_LANE_TILE
            tiles[t] = folded if kvh % HEADS_PER_LANE_TILE == 0 else jnp.where(masks[1], folded, tiles[t])

        for kvh in range(N_KV_HEADS):
            q4 = _att_stack_heads(q_ref, kvh, masks).astype(BF16)
            do4 = _att_stack_heads(do_ref, kvh, masks)
            o4 = _att_stack_heads(o_ref, kvh, masks)
            kc, kp = _att_kv_tile(kc_ref, kvh, masks), _att_kv_tile(kp_ref, kvh, masks)
            vc, vp = _att_kv_tile(vc_ref, kvh, masks), _att_kv_tile(vp_ref, kvh, masks)
            l4 = _att_stack_columns(l_ref, kvh, WINDOW)
            pc = jnp.exp(_att_scores(q4, kc, mask_c) - l4)
            pp = jnp.exp(_att_scores(q4, kp, mask_p) - l4)
            delta = jnp.sum(do4 * o4, axis=1, keepdims=True)
            do4b = do4.astype(BF16)
            dsc = pc * (_dot(do4b, vc, NT) - delta)
            dsp = pp * (_dot(do4b, vp, NT) - delta)
            _att_unstack((_dot(dsc, kc) + _dot(dsp, kp)) * ATT_SCALE, kvh, masks, dq_tiles)
            place(dkc_tiles, kvh, _dot(dsc, q4, TN) * ATT_SCALE)
            place(dkp_tiles, kvh, _dot(dsp, q4, TN) * ATT_SCALE)
            place(dvc_tiles, kvh, _dot(pc, do4b, TN))
            place(dvp_tiles, kvh, _dot(pp, do4b, TN))
            p_sink = jnp.exp(_att_stack_columns(s_ref, kvh, WINDOW) - l4) * delta
            for g in range(Q_PER_KV):
                h = kvh * Q_PER_KV + g
                dsink = jnp.where(lane_row == h, -jnp.sum(p_sink[g * WINDOW:(g + 1) * WINDOW], axis=0, keepdims=True), dsink)
        for t, tile in enumerate(dq_tiles):
            dq_ref[:, t * LANES:(t + 1) * LANES] = _rope_rotate(tile, cos_ref[...], -sin_ref[...])
        for t in range(kv_tiles):
            lanes = slice(t * LANES, (t + 1) * LANES)
            dk_ref[rows_c, lanes] += dkc_tiles[t]
            dk_ref[rows_p, lanes] += dkp_tiles[t]
            dv_ref[rows_c, lanes] += dvc_tiles[t]
            dv_ref[rows_p, lanes] += dvp_tiles[t]
        dsink_ref[...] += dsink

        @pl.when(n == t_dim // WINDOW - 1)
        def _():
            dk_ref[...] = _rope_rotate(dk_ref[...], cos_all_ref[...], -sin_all_ref[...])

    cur = lambda w: pl.BlockSpec((WINDOW, w), lambda n: (n, 0))
    prv = lambda w: pl.BlockSpec((WINDOW, w), lambda n: (jnp.maximum(n - 1, 0), 0))
    whole = lambda w: pl.BlockSpec((t_dim, w), lambda n: (0, 0))
    svec = pl.BlockSpec((1, N_Q_HEADS), lambda n: (0, 0))
    return _call(
        body, name=name, grid=(t_dim // WINDOW,),
        in_specs=[cur(D_MODEL), cur(KV_DIM), prv(KV_DIM), cur(KV_DIM), prv(KV_DIM), svec, cur(D_MODEL), cur(N_Q_HEADS), cur(D_MODEL),
                  cur(LANES), cur(LANES), whole(LANES), whole(LANES)],
        out_specs=[cur(D_MODEL), whole(KV_DIM), whole(KV_DIM), svec],
        out_shape=[jax.ShapeDtypeStruct((t_dim, D_MODEL), F32), jax.ShapeDtypeStruct((t_dim, KV_DIM), F32),
                   jax.ShapeDtypeStruct((t_dim, KV_DIM), F32), jax.ShapeDtypeStruct((1, N_Q_HEADS), F32)],
        sem=("arbitrary",), args=[q, k, k, v, v, sinks, o, lse, do, cos2, sin2, cos2, sin2], comm=comm)


def _loss_head(x, nw, target, *, name, tm=512):
    t_dim, d_dim = x.shape
    row = pl.BlockSpec((tm, d_dim), lambda i: (i, 0))
    vec = pl.BlockSpec((1, d_dim), lambda i: (0, 0))

    def body(x_ref, nw_ref, tgt_ref, loss_ref, dx_ref, dnw_ref, dob_ref):
        @pl.when(pl.program_id(0) == 0)
        def _():
            loss_ref[...] = jnp.zeros_like(loss_ref)
            dnw_ref[...] = jnp.zeros_like(dnw_ref)

        xhat, r = _rms(x_ref[...])
        err = xhat * nw_ref[...] - tgt_ref[...]
        loss_ref[...] += 0.5 * _sum_all(jnp.mean(err * err, axis=-1, keepdims=True))
        dy = err * (1.0 / d_dim)
        dnw_ref[...] += jnp.sum(dy * xhat, axis=0, keepdims=True)
        dxhat = dy * nw_ref[...]
        dx = r * (dxhat - xhat * jnp.mean(dxhat * xhat, axis=-1, keepdims=True))
        dx_ref[...] = dx
        dob_ref[...] = (FFN_RES_WEIGHT * dx).astype(BF16)

    return pl.pallas_call(
        body, name=name, grid=(t_dim // tm,), in_specs=[row, vec, row],
        out_specs=[pl.BlockSpec((1, 1), lambda i: (0, 0)), row, vec, row],
        out_shape=[jax.ShapeDtypeStruct((1, 1), F32), jax.ShapeDtypeStruct((t_dim, d_dim), F32),
                   jax.ShapeDtypeStruct((1, d_dim), F32), jax.ShapeDtypeStruct((t_dim, d_dim), BF16)],
        compiler_params=_params("arbitrary"),
    )(x, nw, target)


def _rope_tables():
    pos = jnp.arange(SEQ, dtype=F32)
    inv = 1.0 / (ROPE_THETA ** (jnp.arange(0, ATT_HEAD_DIM, 2, dtype=F32) / ATT_HEAD_DIM))
    ang = pos[:, None] * inv[None, :]
    cos, sin = jnp.cos(ang), jnp.sin(ang)
    return jnp.tile(cos, (1, 4)), jnp.tile(sin, (1, 4))


def _to_groups(t):
    return t.reshape(t.shape[0], SSM_GROUPS, HEADS_PER_GROUP).transpose(1, 0, 2)


def _from_groups(t):
    return t.transpose(1, 0, 2).reshape(t.shape[1], SSM_HEADS)


def _forward_backward(x0, target, net):
    w = net.w
    nw = [[w("norm_w")[l, i][None, :] for i in range(3)] for l in range(2)]
    cos2, sin2 = _rope_tables()
    ffn_norm = [nw[0][0], nw[0][2], nw[1][0], nw[1][2]]

    ffn_pre = {}

    def ffn_f(x, blk):
        name = f"ffn_fwd{blk}"
        out, *ffn_pre[blk] = _ffn_fwd(x, ffn_norm[blk], w(f"gate{blk}"), w(f"up{blk}"), w(f"down{blk}"), name=name,
                                      comm=net.carry(name))
        return out

    x1 = ffn_f(x0, 0)
    zx, h1 = _norm_mm(x1, nw[0][1], w("w_in_t"), None, w_rows=ZX_DIM, name="ssm_in_proj", comm=net.carry("ssm_in_proj"))
    dtr = _mm(h1, w("w_in_t"), dims="nt", b_rows=(ZX_DIM, SSM_HEADS), name="ssm_dt_proj")
    xbc = _conv_fwd(zx, w("conv_w"), w("conv_b"), name="ssm_conv_fwd", comm=net.carry("ssm_conv_fwd"))
    dt, a_dt = _dt_prep(dtr, w("dt_bias"), w("a_log"), name="ssm_dt_prep")
    dtg, ag = _to_groups(dt), _to_groups(a_dt)
    dg = jnp.repeat(w("d_skip").reshape(SSM_GROUPS, 1, HEADS_PER_GROUP), SSM_HEAD_DIM, axis=2)
    y_ssd, states = _ssd_fwd(xbc, dtg, ag, dg, name="ssd_fwd", comm=net.carry("ssd_fwd"))
    yn = _gate_norm_fwd(y_ssd, zx, w("ssm_norm_w"), name="ssm_gate_norm_fwd")
    x2 = _mm(yn, w("wout"), res=x1, name="ssm_out_proj", comm=net.carry("ssm_out_proj"))
    x3 = ffn_f(x2, 1)
    k_rot, hk = _norm_mm(x3, w("kv_norm_w"), w("wk"), w("b_k"), rope=(cos2, sin2), name="k_proj")
    v = _mm(hk, w("wv"), bias=w("b_v"), name="v_proj")
    x4 = ffn_f(x3, 2)
    q_rot, h4 = _norm_mm(x4, nw[1][1], w("wq"), w("b_q"), rope=(cos2, sin2), name="q_proj")
    att, lse = _attn_fwd(q_rot, k_rot, v, w("sinks"), name="attn_fwd", comm=net.carry("attn_fwd"))
    x5 = _mm(att, w("wo"), bias=w("b_o"), res=x4, name="attn_out_proj")
    x6 = ffn_f(x5, 3)
    loss, dx6, d_final, dob6 = _loss_head(x6, w("final_norm_w"), target, name="loss_head")

    d_norm = [[None] * 3 for _ in range(2)]

    def ffn_b(x, dout, dob, blk):
        pre_g, pre_u, h = ffn_pre[blk]
        name = f"ffn_bwd{blk}"
        dh, gg, gu, gd = _ffn_bwd(h, dob, pre_g, pre_u, w(f"gate{blk}"), w(f"up{blk}"), w(f"down{blk}"), name=name,
                                  comm=net.carry(name))
        net.give(f"gate{blk}", gg)
        net.give(f"up{blk}", gu)
        net.give(f"down{blk}", gd)
        return _norm_bwd(x, ffn_norm[blk], dh, [dout], name=f"ffn_norm_bwd{blk}", comm=net.carry(f"ffn_norm_bwd{blk}"))

    by_rows = lambda g: g.reshape(N_DEV, g.shape[0] // N_DEV, g.shape[1])
    dx5, d_norm[1][2] = ffn_b(x5, dx6, dob6, 3)
    d_att = _mm(dx5, w("wo"), dims="nt", name="attn_out_proj_dx", comm=net.carry("attn_out_proj_dx"))
    g_o, d_bo = _mm(att, dx5, dims="tn", out_dtype=BF16, colsum_b=True, name="attn_out_proj_dw")
    net.give("w_o", by_rows(g_o))
    dq, dk, dv, d_sinks = _attn_bwd(q_rot, k_rot, v, w("sinks"), att, lse, d_att, cos2, sin2, name="attn_bwd",
                                    comm=net.carry("attn_bwd"))
    dx4, d_norm[1][1], dob4 = _mm_norm_bwd(dq, w("wq"), x4, nw[1][1], [dx5], dims="nt", name="q_proj_dx")
    g_q, d_bq = _mm(h4, dq, dims="tn", out_dtype=BF16, colsum_b=True, name="q_proj_dw")
    net.give("w_q", by_rows(g_q))
    dx3a, d_norm[1][0] = ffn_b(x3, dx4, dob4, 2)
    dhk = _mm(dk, w("wk"), dims="nt", name="k_proj_dx", comm=net.carry("k_proj_dx"))
    dx3, d_kvn, dob3 = _mm_norm_bwd(dv, w("wv"), x3, w("kv_norm_w"), [dx3a], dims="nt", add=dhk, name="v_proj_dx")
    g_k, d_bk = _mm(hk, dk, dims="tn", out_dtype=BF16, colsum_b=True, name="k_proj_dw")
    g_v, d_bv = _mm(hk, dv, dims="tn", out_dtype=BF16, colsum_b=True, name="v_proj_dw")
    net.give("w_k", by_rows(g_k))
    net.give("w_v", by_rows(g_v))
    dx2, d_norm[0][2] = ffn_b(x2, dx3, dob3, 1)
    d_yn = _mm(dx2, w("wout"), dims="nt", name="ssm_out_proj_dx", comm=net.carry("ssm_out_proj_dx"))
    net.give("w_out", by_rows(_mm(yn, dx2, dims="tn", out_dtype=BF16, name="ssm_out_proj_dw")))
    dy_ssd, dzx, d_ssm_norm = _gate_norm_bwd(y_ssd, zx, w("ssm_norm_w"), d_yn, name="ssm_gate_norm_bwd")
    dxs, d_b, d_c, ddtg, dag, ddg = _ssd_bwd(xbc, dtg, ag, dg, states, dy_ssd, name="ssd_bwd", comm=net.carry("ssd_bwd"))
    dzx, d_conv_w, d_conv_b = _conv_bwd(zx, w("conv_w"), w("conv_b"), dxs, d_b, d_c, dzx, name="ssm_conv_bwd",
                                        comm=net.carry("ssm_conv_bwd"))
    ddtr, d_dt_bias, d_a_log = _dt_bwd(dtr, w("dt_bias"), w("a_log"), dt, _from_groups(ddtg), _from_groups(dag), name="ssm_dt_bwd")
    dh1 = _mm(dzx, w("w_in_t"), b_rows=(0, ZX_DIM), name="ssm_in_proj_dx")
    in_rows = N_DEV * IN_PROJ_SHARD
    g_in = _mm(dzx, h1, dims="tn", out_dtype=BF16, out_window=(0, in_rows), name="ssm_in_proj_dw")
    g_in = _mm(ddtr, h1, dims="tn", out_dtype=BF16, out_window=(ZX_DIM, in_rows), into=g_in, name="ssm_dt_proj_dw")
    net.give("w_in", g_in.reshape(N_DEV, IN_PROJ_SHARD, D_MODEL))
    dx1, d_norm[0][1], dob1 = _mm_norm_bwd(ddtr, w("w_in_t"), x1, nw[0][1], [dx2], b_rows=(ZX_DIM, SSM_HEADS), add=dh1,
                                           name="ssm_dt_proj_dx", comm=net.carry("ssm_norm_bwd"))
    dx0, d_norm[0][0] = ffn_b(x0, dx1, dob1, 0)

    small = {"norm_w": jnp.concatenate([d_norm[l][i] for l in range(2) for i in range(3)], axis=0),
             "ssm_conv_w": d_conv_w, "ssm_conv_b": d_conv_b, "ssm_dt_bias": d_dt_bias, "ssm_a_log": d_a_log,
             "ssm_d": ddg.reshape(1, SSM_HEADS), "ssm_norm_w": d_ssm_norm, "kv_norm_w": d_kvn,
             "b_k": d_bk, "b_v": d_bv, "attn_b_q": d_bq, "attn_sinks": d_sinks, "attn_b_o": d_bo, "final_norm_w": d_final}
    return loss, dx0, small


BLOCK_BYTES = 1 << 20


def _row_tile(rows, cols):
    for t in (512, 256, 128, 64, 32, 16):
        if rows % t == 0 and t * cols * 4 <= BLOCK_BYTES:
            return t
    return rows


def _cast_bf16(x, *, name):
    n_blk, rows, cols = x.shape
    tm = rows if rows * cols * 4 <= 2 * BLOCK_BYTES else _row_tile(rows, cols)
    spec = pl.BlockSpec((None, tm, cols), lambda b, i: (b, i, 0))

    def body(x_ref, o_ref):
        o_ref[...] = x_ref[...].astype(BF16)

    return pl.pallas_call(body, name=name, grid=(n_blk, rows // tm), in_specs=[spec], out_specs=spec,
                          out_shape=jax.ShapeDtypeStruct(x.shape, BF16), compiler_params=_params("parallel", "parallel"))(x)


def _pair_add(grad, theirs, *, name):
    n_slots, rows, cols = theirs.shape
    tm = rows if rows * cols * 4 <= 2 * BLOCK_BYTES else _row_tile(rows, cols)

    def body(g_ref, t_ref, o_ref):
        o_ref[...] = (g_ref[...].astype(F32) + t_ref[...].astype(F32)).astype(BF16)

    spec = pl.BlockSpec((None, tm, cols), lambda s, i: (s, i, 0))
    return pl.pallas_call(
        body, name=name, grid=(n_slots, rows // tm),
        in_specs=[pl.BlockSpec((None, tm, cols), lambda s, i: (2 * s + lax.axis_index("c"), i, 0)), spec], out_specs=spec,
        out_shape=jax.ShapeDtypeStruct(theirs.shape, BF16), compiler_params=_params("parallel", "parallel"),
    )(grad, theirs)


def _adam_update(g, w, m, v):
    m = ADAM_B1 * m + (1.0 - ADAM_B1) * g
    v = ADAM_B2 * v + (1.0 - ADAM_B2) * (g * g)
    m_hat = m / (1.0 - ADAM_B1 ** ADAM_STEP)
    v_hat = v / (1.0 - ADAM_B2 ** ADAM_STEP)
    delta = -ADAM_LR * (m_hat / (jnp.sqrt(v_hat) + ADAM_EPS) + ADAM_WD * w)
    return delta, m, v


def _adamw(parts, w, m, v, first_blk, prev, *, name, comm=None):
    n_blk, rows, cols = w.shape
    tm = _row_tile(rows, cols)
    n_tiles = rows // tm
    spec = pl.BlockSpec((None, tm, cols), lambda b, i: (first_blk + b, i, 0))
    n_prev, n_here = len(prev), len(parts)
    n_parts = parts[0].shape[0]

    def part_spec(q):
        return pl.BlockSpec((n_parts, tm, cols), lambda b, i: (0, jnp.where(b < q, 0, jnp.where(b == q, i, n_tiles - 1)), 0))

    def body(*refs):
        p_refs = refs[:n_here]
        w_ref, m_ref, v_ref = refs[n_here:n_here + 3]
        g_ref, d_ref, nm_ref, nv_ref = refs[n_here + 3 + n_prev:]
        b = pl.program_id(0)
        g = None
        for s in range(n_parts):
            t = p_refs[0][s]
            for q in range(1, n_here):
                t = jnp.where(b == q, p_refs[q][s], t)
            g = t.astype(F32) if g is None else g + t.astype(F32)
        delta, nm, nv = _adam_update(g, w_ref[...], m_ref[...], v_ref[...])
        g_ref[...] = g
        d_ref[...] = delta
        nm_ref[...] = nm
        nv_ref[...] = nv

    return _call(
        body, name=name, grid=(n_here, n_tiles),
        in_specs=[part_spec(q) for q in range(n_here)] + [spec, spec, spec] + [pl.BlockSpec(memory_space=pl.ANY)] * n_prev,
        out_specs=[spec] * 4, out_shape=[jax.ShapeDtypeStruct((n_blk, rows, cols), F32)] * 4,
        aliases={n_here + 3 + q: q for q in range(n_prev)}, sem=("arbitrary", "arbitrary"),
        args=[*parts, w, m, v, *prev], comm=comm)


def _sum_parts(parts, *, name):
    def body(p_ref, o_ref):
        g = p_ref[0]
        for s in range(1, N_DEV):
            g = g + p_ref[s]
        o_ref[...] = g

    return pl.pallas_call(body, name=name, out_shape=jax.ShapeDtypeStruct(parts.shape[1:], F32), compiler_params=_params())(parts)


def _adamw_packed(g, w, m, v, *, name):
    def body(g_ref, w_ref, m_ref, v_ref, d_ref, nm_ref, nv_ref):
        delta, nm, nv = _adam_update(g_ref[...], w_ref[...], m_ref[...], v_ref[...])
        d_ref[...] = delta
        nm_ref[...] = nm
        nv_ref[...] = nv

    return pl.pallas_call(body, name=name, out_shape=[jax.ShapeDtypeStruct(g.shape, F32)] * 3, compiler_params=_params())(g, w, m, v)


SUBLANES = 8


WIDE_PACK = 1024


def _pack(arrs, width=LANES):
    rows = []
    for a in arrs:
        a2 = a.reshape(-1, a.shape[-1])
        a2 = jnp.pad(a2, ((0, 0), (0, (-a2.shape[1]) % width)))
        rows += [a2[:, i * width:(i + 1) * width] for i in range(a2.shape[1] // width)]
    out = jnp.concatenate(rows, axis=0)
    return jnp.pad(out, ((0, (-out.shape[0]) % SUBLANES), (0, 0)))


def _unpack(packed, shapes, width=LANES):
    outs, r = [], 0
    for shp in shapes:
        lead, cols = math.prod(shp[:-1]), shp[-1]
        n_blocks = -(-cols // width)
        blocks = [packed[r + i * lead:r + (i + 1) * lead] for i in range(n_blocks)]
        outs.append(jnp.concatenate(blocks, axis=1)[:, :cols].reshape(shp))
        r += n_blocks * lead
    return outs


WEIGHT_NAMES = ("norm_w", "ffn_w_gate", "ffn_w_up", "ffn_w_down", "ssm_w_in", "ssm_conv_w", "ssm_conv_b", "ssm_dt_bias",
                "ssm_a_log", "ssm_d", "ssm_norm_w", "ssm_w_out", "kv_norm_w", "w_k", "b_k", "w_v", "b_v", "attn_w_q",
                "attn_b_q", "attn_sinks", "attn_w_o", "attn_b_o", "final_norm_w")
MATRIX_NAMES = ("ffn_w_gate", "ffn_w_up", "ffn_w_down", "ssm_w_in", "ssm_w_out", "w_k", "w_v", "attn_w_q", "attn_w_o")
VECTOR_NAMES = tuple(n for n in WEIGHT_NAMES if n not in MATRIX_NAMES)
SHARDED_VECTORS = ("norm_w", "ssm_conv_w", "ssm_conv_b", "ssm_norm_w")


GATHER_PLAN = {
    "gather_stage0": ("gate0", "up0", "down0", "vec"),
    "ffn_fwd0": ("w_in",),
    "ssm_in_proj": ("w_out", "gate1"),
    "ssm_conv_fwd": ("w_k", "w_v", "up1"),
    "ssd_fwd": ("down1", "gate2", "up2"),
    "ssm_out_proj": ("w_q", "w_o"),
    "ffn_fwd1": ("down2", "gate3"),
    "ffn_fwd2": ("up3",),
    "attn_fwd": ("down3",),
}
PAIR_PLAN = {
    "attn_bwd": ("gate3", "up3", "down3"),
    "ffn_bwd2": ("w_q", "w_o"),
    "ffn_bwd1": ("gate2", "up2", "down2", "w_k", "w_v"),
    "ssd_bwd": ("gate1", "up1", "down1", "w_out"),
    "ssm_norm_bwd": ("w_in",),
    "ffn_norm_bwd0": ("gate0", "up0", "down0"),
}
CHIP_PLAN = {
    "ffn_bwd2": ("gate3", "up3", "down3"),
    "ssd_bwd": ("gate2", "up2", "down2", "w_q", "w_o", "w_k", "w_v"),
    "ssm_conv_bwd": ("gate1", "up1"),
    "ffn_bwd0": ("down1", "w_out", "w_in"),
    "adamw_gate": ("gate0",),
    "adamw_up": ("up0",),
    "adamw_down": ("down0",),
}
FFN_PARAMS = {"gate": "ffn_w_gate", "up": "ffn_w_up", "down": "ffn_w_down"}
SINGLE_MATRICES = {"w_in": "ssm_w_in", "w_out": "ssm_w_out", "w_k": "w_k", "w_v": "w_v", "w_q": "attn_w_q", "w_o": "attn_w_o"}


TRANSPOSED = ("ffn_w_gate", "ffn_w_up", "ssm_w_in")


def _matrix_view(name, a):
    if name in TRANSPOSED:
        a = jnp.swapaxes(a, -1, -2)
    return a.reshape((-1,) + a.shape[-2:])


def _from_matrix_view(name, a, shape):
    if name in TRANSPOSED:
        return jnp.swapaxes(a.reshape(shape[:-2] + (shape[-1], shape[-2])), -1, -2)
    return a.reshape(shape)


class _MeshNet:
    def __init__(self, p):
        self.p = p
        self.views = {n: _matrix_view(n, p[n]) for n in MATRIX_NAMES}
        self.local = {"vec": _pack([p[n] for n in SHARDED_VECTORS])}
        for short, n in FFN_PARAMS.items():
            cast = _cast_bf16(self.views[n], name=f"cast_{short}")
            self.local.update({f"{short}{k}": (cast, k) for k in range(N_FFN)})
        for short, n in SINGLE_MATRICES.items():
            self.local[short] = (_cast_bf16(self.views[n], name=f"cast_{short}"), 0)
        self.gathered_at, self.pairs_at, self.parts_at, self.grads, self.cache = {}, {}, {}, {}, {}

    def carry(self, name):
        comms = []
        if name in GATHER_PLAN:
            keys, comm = GATHER_PLAN[name], _Gather([self.local[k] for k in GATHER_PLAN[name]])
            self.gathered_at.update({k: (comm, i) for i, k in enumerate(keys)})
            comms.append(comm)
        if name in CHIP_PLAN:
            sums = []
            for k in CHIP_PLAN[name]:
                comm, i = self.pairs_at[k]
                sums.append(_pair_add(self.grads[k], comm.results[i], name=f"pair_add_{k}"))
            comm = _ChipExchange(sums)
            self.parts_at.update({k: (comm, i) for i, k in enumerate(CHIP_PLAN[name])})
            comms.append(comm)
        if name in PAIR_PLAN:
            keys, comm = PAIR_PLAN[name], _PairSwap([self.grads[k] for k in PAIR_PLAN[name]])
            self.pairs_at.update({k: (comm, i) for i, k in enumerate(keys)})
            comms.append(comm)
        return comms

    def run(self, name):
        for comm in self.carry(name):
            _run_exchange(comm, name=name)

    def give(self, key, grad):
        self.grads[key] = grad

    def parts(self, key):
        comm, i = self.parts_at[key]
        return comm.results[i]

    def _gathered(self, key):
        comm, i = self.gathered_at[key]
        return comm.results[i]

    def _vec(self, r0, lead, n_blocks):
        vecs = self._gathered("vec")
        return jnp.concatenate([vecs[d, r0 + i * lead:r0 + (i + 1) * lead, :] for d in range(N_DEV) for i in range(n_blocks)], axis=1)

    def _derive(self, name):
        p = self.p
        if name[:-1] in FFN_PARAMS:
            return self._gathered(name)
        if name == "w_in_t":
            return self._gathered("w_in").reshape(N_DEV * IN_PROJ_SHARD, D_MODEL)
        by_rows = {"wout": "w_out", "wk": "w_k", "wv": "w_v", "wq": "w_q", "wo": "w_o"}
        if name in by_rows:
            g = self._gathered(by_rows[name])
            return g.reshape(N_DEV * g.shape[1], g.shape[2])
        vectors = {"norm_w": lambda: self._vec(0, 6, 1).reshape(2, 3, D_MODEL), "conv_w": lambda: self._vec(6, CONV_WIDTH, 3),
                   "conv_b": lambda: self._vec(18, 1, 3), "ssm_norm_w": lambda: self._vec(21, 1, 2)}
        if name in vectors:
            return vectors[name]()
        replicated = {"dt_bias": p["ssm_dt_bias"], "a_log": p["ssm_a_log"], "d_skip": p["ssm_d"], "kv_norm_w": p["kv_norm_w"][None],
                      "b_k": p["b_k"][None], "b_v": p["b_v"][None], "b_q": p["attn_b_q"], "sinks": p["attn_sinks"],
                      "b_o": p["attn_b_o"], "final_norm_w": p["final_norm_w"][None]}
        return replicated[name]

    def w(self, name):
        if name not in self.cache:
            self.cache[name] = self._derive(name)
        return self.cache[name]


def _step(x, target, p, m, v):
    pos = _slot(_position())
    net = _MeshNet(p)
    net.run("gather_stage0")
    loss, grad_x, small = _forward_backward(x, target, net)

    grads, deltas, new_m, new_v = {}, {}, {}, {}
    view = lambda d, n: _matrix_view(n, d[n])
    vec_gather = _Gather([_pack([small[n] for n in VECTOR_NAMES], WIDE_PACK)])
    for short, n in SINGLE_MATRICES.items():
        outs = _adamw([net.parts(short)], net.views[n], view(m, n), view(v, n), 0, [], name=f"adamw_{short}",
                      comm=[vec_gather] if short == "w_in" else None)
        grads[n], deltas[n], new_m[n], new_v[n] = [_from_matrix_view(n, o, p[n].shape) for o in outs]
    ffn_outs = {}
    for short, n in FFN_PARAMS.items():
        ffn_outs[short] = _adamw([net.parts(f"{short}{k}") for k in range(1, N_FFN)], net.views[n], view(m, n), view(v, n), 1, [],
                                 name=f"adamw_{short}", comm=net.carry(f"adamw_{short}"))
    for short, n in FFN_PARAMS.items():
        outs = _adamw([net.parts(f"{short}0")], net.views[n], view(m, n), view(v, n), 0, ffn_outs[short], name=f"adamw_{short}0")
        grads[n], deltas[n], new_m[n], new_v[n] = [_from_matrix_view(n, o, p[n].shape) for o in outs]
    vec_sum = _sum_parts(vec_gather.results[0], name="sum_vector_grads")
    full_shapes = {"norm_w": (2, 3, D_MODEL), "ssm_conv_w": (1, CONV_WIDTH, CONV_DIM), "ssm_conv_b": (1, CONV_DIM),
                   "ssm_norm_w": (1, D_INNER)}
    vec_full = dict(zip(VECTOR_NAMES, _unpack(vec_sum, [full_shapes.get(n, p[n].shape) for n in VECTOR_NAMES], WIDE_PACK)))
    for n in VECTOR_NAMES:
        g = vec_full[n]
        if n in SHARDED_VECTORS:
            per = p[n].shape[-1]
            g = lax.dynamic_slice_in_dim(g, pos * per, per, axis=g.ndim - 1)
        grads[n] = g
    packed = _adamw_packed(*[_pack([d[n] for n in VECTOR_NAMES], WIDE_PACK) for d in (grads, p, m, v)], name="adamw_vectors")
    shapes = [p[n].shape for n in VECTOR_NAMES]
    for d, pk in zip((deltas, new_m, new_v), packed):
        d.update(zip(VECTOR_NAMES, _unpack(pk, shapes, WIDE_PACK)))
    return loss, grad_x, grads, deltas, new_m, new_v


def kernel(x, norm_w, ffn_w_gate, ffn_w_up, ffn_w_down, ssm_w_in, ssm_conv_w, ssm_conv_b, ssm_dt_bias, ssm_a_log, ssm_d, ssm_norm_w, ssm_w_out, kv_norm_w, w_k, b_k, w_v, b_v, attn_w_q, attn_b_q, attn_sinks, attn_w_o, attn_b_o, final_norm_w, loss_target, m_norm_w, m_ffn_w_gate, m_ffn_w_up, m_ffn_w_down, m_ssm_w_in, m_ssm_conv_w, m_ssm_conv_b, m_ssm_dt_bias, m_ssm_a_log, m_ssm_d, m_ssm_norm_w, m_ssm_w_out, m_kv_norm_w, m_w_k, m_b_k, m_w_v, m_b_v, m_attn_w_q, m_attn_b_q, m_attn_sinks, m_attn_w_o, m_attn_b_o, m_final_norm_w, v_norm_w, v_ffn_w_gate, v_ffn_w_up, v_ffn_w_down, v_ssm_w_in, v_ssm_conv_w, v_ssm_conv_b, v_ssm_dt_bias, v_ssm_a_log, v_ssm_d, v_ssm_norm_w, v_ssm_w_out, v_kv_norm_w, v_w_k, v_b_k, v_w_v, v_b_v, v_attn_w_q, v_attn_b_q, v_attn_sinks, v_attn_w_o, v_attn_b_o, v_final_norm_w):
    p = dict(zip(WEIGHT_NAMES, (norm_w, ffn_w_gate, ffn_w_up, ffn_w_down, ssm_w_in, ssm_conv_w, ssm_conv_b, ssm_dt_bias, ssm_a_log, ssm_d, ssm_norm_w, ssm_w_out, kv_norm_w, w_k, b_k, w_v, b_v, attn_w_q, attn_b_q, attn_sinks, attn_w_o, attn_b_o, final_norm_w)))
    m = dict(zip(WEIGHT_NAMES, (m_norm_w, m_ffn_w_gate, m_ffn_w_up, m_ffn_w_down, m_ssm_w_in, m_ssm_conv_w, m_ssm_conv_b, m_ssm_dt_bias, m_ssm_a_log, m_ssm_d, m_ssm_norm_w, m_ssm_w_out, m_kv_norm_w, m_w_k, m_b_k, m_w_v, m_b_v, m_attn_w_q, m_attn_b_q, m_attn_sinks, m_attn_w_o, m_attn_b_o, m_final_norm_w)))
    v = dict(zip(WEIGHT_NAMES, (v_norm_w, v_ffn_w_gate, v_ffn_w_up, v_ffn_w_down, v_ssm_w_in, v_ssm_conv_w, v_ssm_conv_b, v_ssm_dt_bias, v_ssm_a_log, v_ssm_d, v_ssm_norm_w, v_ssm_w_out, v_kv_norm_w, v_w_k, v_b_k, v_w_v, v_b_v, v_attn_w_q, v_attn_b_q, v_attn_sinks, v_attn_w_o, v_attn_b_o, v_final_norm_w)))
    loss, grad_x, grads, deltas, new_m, new_v = _step(x[0], loss_target[0], p, m, v)
    loss = lax.psum(loss[0, 0], ("x", "y", "c"))
    return (loss, grad_x[None], *[grads[n] for n in WEIGHT_NAMES], *[deltas[n] for n in WEIGHT_NAMES],
            *[new_m[n] for n in WEIGHT_NAMES], *[new_v[n] for n in WEIGHT_NAMES])
```

```python
import functools
import math

import jax
import jax.numpy as jnp
from jax import lax
from jax.experimental import pallas as pl
from jax.experimental.pallas import tpu as pltpu

F32 = jnp.float32
BF16 = jnp.bfloat16

N_DEV = 8
SEQ = 2048
D_MODEL = 1024
D_FF_SHARD = 352
N_FFN = 4
D_INNER = 2048
SSM_HEADS = 32
SSM_HEAD_DIM = 64
SSM_GROUPS = 4
HEADS_PER_GROUP = 8
SSM_STATE = 128
CHUNK = 128
N_CHUNKS = SEQ // CHUNK
GN = SSM_GROUPS * SSM_STATE
CONV_DIM = D_INNER + 2 * GN
CONV_WIDTH = 4
ZX_DIM = D_INNER + CONV_DIM
IN_PROJ_SHARD = 644
ATT_HEAD_DIM = 64
N_Q_HEADS = 16
N_KV_HEADS = 4
Q_PER_KV = 4
KV_DIM = N_KV_HEADS * ATT_HEAD_DIM
WINDOW = 128
ROPE_THETA = 10000.0
EPS = 1e-5
FFN_RES_WEIGHT = 0.5
ATT_SCALE = 1.0 / math.sqrt(ATT_HEAD_DIM)
NEG_BIG = -1e30

ADAM_LR = 0.001
ADAM_B1 = 0.9
ADAM_B2 = 0.999
ADAM_EPS = 1e-08
ADAM_WD = 0.01
ADAM_STEP = 10

VMEM_LIMIT_BYTES = 56 * 1024 * 1024
FFN_BWD_VMEM_LIMIT_BYTES = 61 * 1024 * 1024

NN = (((1,), (0,)), ((), ()))
NT = (((1,), (1,)), ((), ()))
TN = (((0,), (0,)), ((), ()))
_DIMS = {"nn": NN, "nt": NT, "tn": TN}


def _params(*sem):
    return pltpu.CompilerParams(dimension_semantics=sem if sem else None, vmem_limit_bytes=VMEM_LIMIT_BYTES)


def _dot(a, b, dims=NN):
    return lax.dot_general(a.astype(BF16), b.astype(BF16), dims, preferred_element_type=F32)


def _sigmoid(x):
    return 1.0 / (1.0 + jnp.exp(-x))


def _dsilu(x, s):
    return s * (1.0 + x * (1.0 - s))


def _rms(x):
    r = lax.rsqrt(jnp.mean(x * x, axis=-1, keepdims=True) + EPS)
    return x * r, r


def _sum_all(x):
    return jnp.sum(jnp.sum(x, axis=1, keepdims=True), axis=0, keepdims=True)


MESH = pl.DeviceIdType.MESH
N_PEERS = N_DEV - 1
N_CHIPS = N_DEV // 2


def _position():
    return lax.axis_index("x"), lax.axis_index("y"), lax.axis_index("c")


def _slot(p):
    return 4 * p[0] + 2 * p[1] + p[2]


class _Exchange:
    def __init__(self, arrays, out_shapes):
        n = len(arrays)
        self.arrays = list(arrays)
        self.out_shapes = out_shapes
        self.scratch = [pltpu.SemaphoreType.DMA((n, N_PEERS)), pltpu.SemaphoreType.DMA((n, N_PEERS)), pltpu.SemaphoreType.DMA((n,))]
        self.results = None

    def relay(self, ins, outs, sems):
        pass


class _Gather(_Exchange):
    def __init__(self, pieces):
        pieces = [p if isinstance(p, tuple) else (p, None) for p in pieces]
        self.blocks = [k for _, k in pieces]
        shapes = [a.shape if k is None else a.shape[1:] for a, k in pieces]
        super().__init__([a for a, _ in pieces], [jax.ShapeDtypeStruct((N_DEV,) + s, a.dtype) for s, (a, _) in zip(shapes, pieces)])

    def _plan(self, ins, outs, sems):
        send_sems, recv_sems, local_sems = sems
        x, y, c = _position()
        me, sibling = (x, y, c), (x, y, 1 - c)
        chips = [(1 - x, y), (x, 1 - y), (1 - x, 1 - y)]
        n = len(ins)
        ins = [r if k is None else r.at[k] for r, k in zip(ins, self.blocks)]

        def copy(a, k, block, to, src=None):
            dst = outs[a].at[_slot(block)]
            return pltpu.make_async_remote_copy(src_ref=dst if src is None else src, dst_ref=dst, send_sem=send_sems.at[a, k],
                                                recv_sem=recv_sems.at[a, k], device_id=to, device_id_type=MESH)

        mine = [pltpu.make_async_copy(ins[a], outs[a].at[_slot(me)], local_sems.at[a]) for a in range(n)]
        first = []
        for a in range(n):
            first.append(copy(a, 0, me, sibling, src=ins[a]))
            first += [copy(a, 1 + j, me, (*chip, c), src=ins[a]) for j, chip in enumerate(chips)]
        return n, c, me, sibling, chips, copy, mine, first

    def start(self, ins, outs, sems):
        _, _, _, _, _, _, mine, first = self._plan(ins, outs, sems)
        for cp in mine + first:
            cp.start()

    def relay(self, ins, outs, sems):
        n, c, me, sibling, chips, copy, _, _ = self._plan(ins, outs, sems)
        for j, chip in enumerate(chips):
            for a in range(n):
                copy(a, 1 + j, (*chip, c), me).wait_recv()
                copy(a, 4 + j, (*chip, c), sibling).start()

    def finish(self, ins, outs, sems):
        n, c, me, sibling, chips, copy, mine, first = self._plan(ins, outs, sems)
        passed = [copy(a, 4 + j, (*chip, c), sibling) for j, chip in enumerate(chips) for a in range(n)]
        for a in range(n):
            copy(a, 0, sibling, me).wait_recv()
            for j, chip in enumerate(chips):
                copy(a, 4 + j, (*chip, 1 - c), me).wait_recv()
        for cp in first + passed:
            cp.wait_send()
        for cp in mine:
            cp.wait()


class _PairSwap(_Exchange):
    def __init__(self, arrays):
        n = len(arrays)
        self.arrays = list(arrays)
        self.out_shapes = [jax.ShapeDtypeStruct((N_CHIPS,) + a.shape[1:], a.dtype) for a in arrays]
        self.scratch = [pltpu.SemaphoreType.DMA((n, N_CHIPS)), pltpu.SemaphoreType.DMA((n, N_CHIPS))]
        self.results = None

    def _plan(self, ins, outs, sems):
        send_sems, recv_sems = sems
        x, y, c = _position()
        return [pltpu.make_async_remote_copy(src_ref=ins[a].at[2 * q + 1 - c], dst_ref=outs[a].at[q], send_sem=send_sems.at[a, q],
                                             recv_sem=recv_sems.at[a, q], device_id=(x, y, 1 - c), device_id_type=MESH)
                for a in range(len(ins)) for q in range(N_CHIPS)]

    def start(self, ins, outs, sems):
        for cp in self._plan(ins, outs, sems):
            cp.start()

    def finish(self, ins, outs, sems):
        for cp in self._plan(ins, outs, sems):
            cp.wait()


class _ChipExchange(_Exchange):
    def __init__(self, arrays):
        n = len(arrays)
        self.arrays = list(arrays)
        self.out_shapes = [jax.ShapeDtypeStruct(a.shape, a.dtype) for a in arrays]
        self.scratch = [pltpu.SemaphoreType.DMA((n, 3)), pltpu.SemaphoreType.DMA((n, 3)), pltpu.SemaphoreType.DMA((n,))]
        self.results = None

    def _plan(self, ins, outs, sems):
        send_sems, recv_sems, local_sems = sems
        x, y, c = _position()
        here = 2 * x + y
        chips = [(1 - x, y), (x, 1 - y), (1 - x, 1 - y)]
        n = len(ins)

        def copy(a, k, src_slot, dst_slot):
            return pltpu.make_async_remote_copy(src_ref=ins[a].at[src_slot], dst_ref=outs[a].at[dst_slot], send_sem=send_sems.at[a, k],
                                                recv_sem=recv_sems.at[a, k], device_id=(*chips[k], c), device_id_type=MESH)

        there = [2 * qx + qy for qx, qy in chips]
        mine = [pltpu.make_async_copy(ins[a].at[here], outs[a].at[here], local_sems.at[a]) for a in range(n)]
        sends = [copy(a, k, there[k], here) for a in range(n) for k in range(3)]
        arrivals = lambda: [copy(a, k, here, there[k]) for a in range(n) for k in range(3)]
        return mine, sends, arrivals

    def start(self, ins, outs, sems):
        mine, sends, _ = self._plan(ins, outs, sems)
        for cp in mine + sends:
            cp.start()

    def finish(self, ins, outs, sems):
        mine, sends, arrivals = self._plan(ins, outs, sems)
        for cp in arrivals():
            cp.wait_recv()
        for cp in sends:
            cp.wait_send()
        for cp in mine:
            cp.wait()


def _call(body, *, name, grid, in_specs, out_specs, out_shape, args, scratch_shapes=(), sem=(), comm=(), aliases=None,
          vmem_limit=VMEM_LIMIT_BYTES):
    single = not isinstance(out_shape, (list, tuple))
    out_shape = [out_shape] if single else list(out_shape)
    out_specs = [out_specs] if single else list(out_specs)
    comms = list(comm or ())
    n_in, n_out, n_scr = len(args), len(out_shape), len(scratch_shapes)
    params = pltpu.CompilerParams(dimension_semantics=tuple(sem) if sem else None, vmem_limit_bytes=vmem_limit)
    if not comms:
        res = pl.pallas_call(body, name=name, grid=grid, in_specs=list(in_specs), out_specs=out_specs, out_shape=out_shape,
                             scratch_shapes=list(scratch_shapes), input_output_aliases=aliases or {}, compiler_params=params)(*args)
        return res[0] if single else res
    counts = [n_in] + [len(c.arrays) for c in comms] + [n_out] + [len(c.out_shapes) for c in comms] + [n_scr] + [len(c.scratch) for c in comms]
    nc = len(comms)

    def carried(*refs):
        pos, groups = 0, []
        for cnt in counts:
            groups.append(refs[pos:pos + cnt])
            pos += cnt
        ins, c_ins = groups[0], groups[1:1 + nc]
        outs, c_outs = groups[1 + nc], groups[2 + nc:2 + 2 * nc]
        scr, c_sems = groups[2 + 2 * nc], groups[3 + 2 * nc:]
        ids = [pl.program_id(d) for d in range(len(grid))]
        is_first = functools.reduce(jnp.logical_and, [i == 0 for i in ids])
        is_last = functools.reduce(jnp.logical_and, [i == g - 1 for i, g in zip(ids, grid)])

        @pl.when(is_first)
        def _():
            for q, c in enumerate(comms):
                c.start(c_ins[q], c_outs[q], c_sems[q])

        body(*ins, *outs, *scr)

        @pl.when(is_last)
        def _():
            for q, c in enumerate(comms):
                c.relay(c_ins[q], c_outs[q], c_sems[q])
                c.finish(c_ins[q], c_outs[q], c_sems[q])

    anyspec = pl.BlockSpec(memory_space=pl.ANY)
    c_arrays = [a for c in comms for a in c.arrays]
    c_shapes = [s for c in comms for s in c.out_shapes]
    res = pl.pallas_call(
        carried, name=name, grid=grid, in_specs=list(in_specs) + [anyspec] * len(c_arrays), out_specs=out_specs + [anyspec] * len(c_shapes),
        out_shape=out_shape + c_shapes, scratch_shapes=list(scratch_shapes) + [s for c in comms for s in c.scratch],
        input_output_aliases=aliases or {}, compiler_params=params)(*args, *c_arrays)
    pos = n_out
    for c in comms:
        c.results = list(res[pos:pos + len(c.out_shapes)])
        pos += len(c.out_shapes)
    return res[0] if single else list(res[:n_out])


def _run_exchange(comm, *, name):
    def body(*refs):
        n_ci, n_co = len(comm.arrays), len(comm.out_shapes)
        ins, outs, sems = refs[:n_ci], refs[n_ci:n_ci + n_co], refs[n_ci + n_co:]
        comm.start(ins, outs, sems)
        comm.relay(ins, outs, sems)
        comm.finish(ins, outs, sems)

    anyspec = pl.BlockSpec(memory_space=pl.ANY)
    comm.results = list(pl.pallas_call(
        body, name=name, in_specs=[anyspec] * len(comm.arrays), out_specs=[anyspec] * len(comm.out_shapes),
        out_shape=list(comm.out_shapes), scratch_shapes=list(comm.scratch))(*comm.arrays))
    return comm.results


def _mm(a, b, *, dims="nn", bias=None, res=None, out_dtype=F32, name, tm=1024, tn=1024, tk=1024, comm=None, b_rows=None,
        out_window=None, into=None, colsum_b=False):
    if dims == "tn":
        k_dim, m_dim = a.shape
    else:
        m_dim, k_dim = a.shape
    row0, n_rows = b_rows if b_rows is not None else (0, b.shape[0])
    n_dim = n_rows if dims == "nt" else b.shape[1]
    assert dims == "nt" or n_rows == k_dim, (name, a.shape, b.shape, b_rows)
    tm, tn, tk = min(tm, m_dim), min(tn, n_dim), min(tk, k_dim)
    assert m_dim % tm == 0 and n_dim % tn == 0 and k_dim % tk == 0, (name, a.shape, b.shape)
    nk = k_dim // tk
    a_spec = pl.BlockSpec((tk, tm), lambda i, j, k: (k, i)) if dims == "tn" else pl.BlockSpec((tm, tk), lambda i, j, k: (i, k))
    if dims == "nt":
        assert row0 % tn == 0
        b_spec = pl.BlockSpec((tn, tk), lambda i, j, k: (row0 // tn + j, k))
    else:
        assert row0 % tk == 0
        b_spec = pl.BlockSpec((tk, tn), lambda i, j, k: (row0 // tk + k, j))
    in_specs, args = [a_spec, b_spec], [a, b]
    if bias is not None:
        in_specs.append(pl.BlockSpec((1, tn), lambda i, j, k: (0, j)))
        args.append(bias)
    if res is not None:
        in_specs.append(pl.BlockSpec((tm, tn), lambda i, j, k: (i, j)))
        args.append(res)
    dn = _DIMS[dims]

    if colsum_b:
        assert dims == "tn" and m_dim == tm and into is None and out_window is None

    def body(*refs):
        a_ref, b_ref = refs[0], refs[1]
        acc_ref = refs[-1]
        o_ref = refs[-3] if colsum_b else refs[-2]
        k = pl.program_id(2)

        @pl.when(k == 0)
        def _():
            acc_ref[...] = jnp.zeros_like(acc_ref)
            if colsum_b:
                refs[-2][...] = jnp.zeros_like(refs[-2])

        acc_ref[...] += _dot(a_ref[...], b_ref[...], dn)
        if colsum_b:
            refs[-2][...] += jnp.sum(b_ref[...].astype(F32), axis=0, keepdims=True)

        @pl.when(k == nk - 1)
        def _():
            r = acc_ref[...]
            pos = 2
            if bias is not None:
                r = r + refs[pos][...]
                pos += 1
            if res is not None:
                r = r + refs[pos][...]
            o_ref[...] = r.astype(out_dtype)

    out_row0, out_rows = out_window if out_window is not None else (0, m_dim)
    assert out_row0 % tm == 0
    aliases = None
    if into is not None:
        assert into.shape == (out_rows, n_dim) and into.dtype == out_dtype
        in_specs.append(pl.BlockSpec(memory_space=pl.ANY))
        args.append(into)
        aliases = {len(args) - 1: 0}
    out_spec = pl.BlockSpec((tm, tn), lambda i, j, k: (out_row0 // tm + i, j))
    out_shape = jax.ShapeDtypeStruct((out_rows, n_dim), out_dtype)
    if colsum_b:
        out_spec = [out_spec, pl.BlockSpec((1, tn), lambda i, j, k: (0, j))]
        out_shape = [out_shape, jax.ShapeDtypeStruct((1, n_dim), F32)]
    return _call(
        body, name=name, grid=(m_dim // tm, n_dim // tn, nk), in_specs=in_specs, out_specs=out_spec, out_shape=out_shape,
        aliases=aliases, scratch_shapes=[pltpu.VMEM((tm, tn), F32)], sem=("parallel", "parallel", "arbitrary"), args=args, comm=comm)


def _mm_norm_bwd(a, b, x, nw, res, *, dims="nn", b_rows=None, add=None, name, tm=1024, tk=1024, comm=None):
    m_dim, k_dim = a.shape
    row0, n_rows = b_rows if b_rows is not None else (0, b.shape[0])
    d_dim = x.shape[1]
    tm, tk = min(tm, m_dim), min(tk, k_dim)
    assert m_dim % tm == 0 and k_dim % tk == 0 and (n_rows if dims == "nt" else b.shape[1]) == d_dim, (name, a.shape, b.shape)
    nk = k_dim // tk
    if dims == "nt":
        assert row0 % d_dim == 0
        b_spec = pl.BlockSpec((d_dim, tk), lambda i, k: (row0 // d_dim, k))
    else:
        assert row0 % tk == 0 and n_rows == k_dim
        b_spec = pl.BlockSpec((tk, d_dim), lambda i, k: (row0 // tk + k, 0))
    row = pl.BlockSpec((tm, d_dim), lambda i, k: (i, 0))
    vec = pl.BlockSpec((1, d_dim), lambda i, k: (0, 0))
    extra = ([add] if add is not None else []) + list(res)
    dn = _DIMS[dims]

    def body(*refs):
        a_ref, b_ref, x_ref, nw_ref = refs[:4]
        extra_refs = refs[4:4 + len(extra)]
        dx_ref, dnw_ref, dob_ref, acc_ref = refs[-4:]
        i, k = pl.program_id(0), pl.program_id(1)

        @pl.when(k == 0)
        def _():
            acc_ref[...] = jnp.zeros_like(acc_ref)

        @pl.when((i == 0) & (k == 0))
        def _():
            dnw_ref[...] = jnp.zeros_like(dnw_ref)

        acc_ref[...] += _dot(a_ref[...], b_ref[...], dn)

        @pl.when(k == nk - 1)
        def _():
            dh = acc_ref[...]
            rest = list(extra_refs)
            if add is not None:
                dh = dh + rest.pop(0)[...]
            xhat, r = _rms(x_ref[...])
            dxhat = dh * nw_ref[...]
            dx = r * (dxhat - xhat * jnp.mean(dxhat * xhat, axis=-1, keepdims=True))
            for rr in rest:
                dx = dx + rr[...]
            dx_ref[...] = dx
            dob_ref[...] = (FFN_RES_WEIGHT * dx).astype(BF16)
            dnw_ref[...] += jnp.sum(dh * xhat, axis=0, keepdims=True)

    return _call(
        body, name=name, grid=(m_dim // tm, nk),
        in_specs=[pl.BlockSpec((tm, tk), lambda i, k: (i, k)), b_spec, row, vec] + [row] * len(extra), out_specs=[row, vec, row],
        out_shape=[jax.ShapeDtypeStruct((m_dim, d_dim), F32), jax.ShapeDtypeStruct((1, d_dim), F32),
                   jax.ShapeDtypeStruct((m_dim, d_dim), BF16)],
        scratch_shapes=[pltpu.VMEM((tm, d_dim), F32)], sem=("arbitrary", "arbitrary"), args=[a, b, x, nw] + extra, comm=comm)


def _rope_rotate(x, cos_t, sin_t):
    rows, width = x.shape
    half = ATT_HEAD_DIM // 2
    lane = lax.broadcasted_iota(jnp.int32, (rows, width), 1)
    first = (lane % ATT_HEAD_DIM) < half
    rot = jnp.where(first, -pltpu.roll(x, width - half, 1), pltpu.roll(x, half, 1))
    reps = width // 128
    return x * jnp.tile(cos_t, (1, reps)) + rot * jnp.tile(sin_t, (1, reps))


def _norm_mm(x, nw, w, bias, *, name, tm=1024, tn=1024, comm=None, w_rows=None, rope=None):
    t_dim, d_dim = x.shape
    transposed = w_rows is not None
    n_dim = w_rows if transposed else w.shape[1]
    tn = min(tn, n_dim)
    assert t_dim % tm == 0 and n_dim % tn == 0
    has_bias = bias is not None
    w_spec = pl.BlockSpec((tn, d_dim), lambda i, j: (j, 0)) if transposed else pl.BlockSpec((d_dim, tn), lambda i, j: (0, j))
    dn = NT if transposed else NN
    in_specs = [pl.BlockSpec((tm, d_dim), lambda i, j: (i, 0)), pl.BlockSpec((1, d_dim), lambda i, j: (0, 0)), w_spec]
    args = [x, nw, w]
    if has_bias:
        in_specs.append(pl.BlockSpec((1, tn), lambda i, j: (0, j)))
        args.append(bias)
    if rope is not None:
        in_specs += [pl.BlockSpec((tm, LANES), lambda i, j: (i, 0))] * 2
        args += list(rope)

    def body(*refs):
        x_ref, nw_ref, w_ref = refs[:3]
        o_ref, h_ref = refs[-2], refs[-1]

        @pl.when(pl.program_id(1) == 0)
        def _():
            xhat, _ = _rms(x_ref[...])
            h_ref[...] = (xhat * nw_ref[...]).astype(BF16)

        r = _dot(h_ref[...], w_ref[...], dn)
        if has_bias:
            r = r + refs[3][...]
        if rope is not None:
            r = _rope_rotate(r, refs[-4][...], refs[-3][...])
        o_ref[...] = r

    return _call(
        body, name=name, grid=(t_dim // tm, n_dim // tn), in_specs=in_specs,
        out_specs=[pl.BlockSpec((tm, tn), lambda i, j: (i, j)), pl.BlockSpec((tm, d_dim), lambda i, j: (i, 0))],
        out_shape=[jax.ShapeDtypeStruct((t_dim, n_dim), F32), jax.ShapeDtypeStruct((t_dim, d_dim), BF16)],
        sem=("parallel", "arbitrary"), args=args, comm=comm)


def _norm_bwd(x, nw, dh, res, *, name, tm=512, comm=None):
    t_dim, d_dim = x.shape
    n_res = len(res)
    row = pl.BlockSpec((tm, d_dim), lambda i: (i, 0))
    vec = pl.BlockSpec((1, d_dim), lambda i: (0, 0))

    def body(*refs):
        x_ref, nw_ref, dh_ref = refs[:3]
        dx_ref, dnw_ref = refs[-2], refs[-1]
        xhat, r = _rms(x_ref[...])
        dh = dh_ref[...]
        dxhat = dh * nw_ref[...]
        dx = r * (dxhat - xhat * jnp.mean(dxhat * xhat, axis=-1, keepdims=True))
        for rr in refs[3:3 + n_res]:
            dx = dx + rr[...]
        dx_ref[...] = dx

        @pl.when(pl.program_id(0) == 0)
        def _():
            dnw_ref[...] = jnp.zeros_like(dnw_ref)

        dnw_ref[...] += jnp.sum(dh * xhat, axis=0, keepdims=True)

    return _call(
        body, name=name, grid=(t_dim // tm,), in_specs=[row, vec, row] + [row] * n_res,
        out_specs=[row, vec],
        out_shape=[jax.ShapeDtypeStruct((t_dim, d_dim), F32), jax.ShapeDtypeStruct((1, d_dim), F32)],
        sem=("arbitrary",), args=[x, nw, dh, *res], comm=comm)


FFN_ROW_TILE = 512
FFN_SHARDS_PER_STEP = 2
FFN_STEPS = N_DEV // FFN_SHARDS_PER_STEP
FFN_STEP_COLS = FFN_SHARDS_PER_STEP * D_FF_SHARD


def _ffn_step_view(w):
    return w.reshape(FFN_STEPS, FFN_STEP_COLS, w.shape[-1])


def _ffn_specs(t_dim, d_dim):
    full = pl.BlockSpec((t_dim, d_dim), lambda j: (0, 0))
    wspec = pl.BlockSpec((None, FFN_STEP_COLS, d_dim), lambda j: (j, 0, 0))
    pre = pl.BlockSpec((None, t_dim, FFN_STEP_COLS), lambda j: (j, 0, 0))
    return full, wspec, pre


def _ffn_fwd(x, nw, wg, wu, wd, *, name, comm=None):
    t_dim, d_dim = x.shape
    n_tiles = t_dim // FFN_ROW_TILE

    def body(x_ref, nw_ref, wg_ref, wu_ref, wd_ref, o_ref, g_ref, u_ref, h_ref):
        j = pl.program_id(0)

        @pl.when(j == 0)
        def _():
            xhat, _ = _rms(x_ref[...])
            h_ref[...] = (xhat * nw_ref[...]).astype(BF16)
            o_ref[...] = jnp.zeros_like(o_ref)

        for t in range(n_tiles):
            rows = pl.ds(t * FFN_ROW_TILE, FFN_ROW_TILE)
            h = h_ref[rows, :]
            g = _dot(h, wg_ref[...], NT)
            u = _dot(h, wu_ref[...], NT)
            g_ref[rows, :] = g.astype(BF16)
            u_ref[rows, :] = u.astype(BF16)
            o_ref[rows, :] += _dot(g * _sigmoid(g) * u, wd_ref[...])

        @pl.when(j == FFN_STEPS - 1)
        def _():
            o_ref[...] = x_ref[...] + FFN_RES_WEIGHT * o_ref[...]

    full, wspec, pre = _ffn_specs(t_dim, d_dim)
    pre_shape = jax.ShapeDtypeStruct((FFN_STEPS, t_dim, FFN_STEP_COLS), BF16)
    return _call(
        body, name=name, grid=(FFN_STEPS,),
        in_specs=[full, pl.BlockSpec((1, d_dim), lambda j: (0, 0)), wspec, wspec, wspec],
        out_specs=[full, pre, pre, full],
        out_shape=[jax.ShapeDtypeStruct((t_dim, d_dim), F32), pre_shape, pre_shape, jax.ShapeDtypeStruct((t_dim, d_dim), BF16)],
        sem=("arbitrary",), args=[x, nw, _ffn_step_view(wg), _ffn_step_view(wu), _ffn_step_view(wd)], comm=comm)


def _ffn_bwd(h, dob, pre_g, pre_u, wg, wu, wd, *, name, comm=None):
    t_dim, d_dim = h.shape
    n_tiles = t_dim // FFN_ROW_TILE

    def body(h_ref, dob_ref, g_ref, u_ref, wg_ref, wu_ref, wd_ref, dh_ref, gg_ref, gu_ref, gd_ref, dwg_scr, dwu_scr, dwd_scr):
        @pl.when(pl.program_id(0) == 0)
        def _():
            dh_ref[...] = jnp.zeros_like(dh_ref)

        for t in range(n_tiles):
            rows = pl.ds(t * FFN_ROW_TILE, FFN_ROW_TILE)
            hh = h_ref[rows, :]
            do = dob_ref[rows, :]
            g = g_ref[rows, :].astype(F32)
            u = u_ref[rows, :].astype(F32)
            sg = _sigmoid(g)
            s = g * sg
            da = _dot(do, wd_ref[...], NT)
            dwd = _dot(s * u, do, TN)
            du = (da * s).astype(BF16)
            dg = (da * u * _dsilu(g, sg)).astype(BF16)
            dwg = _dot(dg, hh, TN)
            dwu = _dot(du, hh, TN)
            if t == 0:
                dwd_scr[...] = dwd
                dwg_scr[...] = dwg
                dwu_scr[...] = dwu
            else:
                dwd_scr[...] += dwd
                dwg_scr[...] += dwg
                dwu_scr[...] += dwu
            dh_ref[rows, :] += _dot(dg, wg_ref[...]) + _dot(du, wu_ref[...])
        gg_ref[...] = dwg_scr[...].astype(BF16)
        gu_ref[...] = dwu_scr[...].astype(BF16)
        gd_ref[...] = dwd_scr[...].astype(BF16)

    full, wspec, pre = _ffn_specs(t_dim, d_dim)
    gspec = pl.BlockSpec((None, FFN_STEP_COLS, d_dim), lambda j: (j, 0, 0), pipeline_mode=pl.Buffered(1))
    grad_shape = jax.ShapeDtypeStruct((FFN_STEPS, FFN_STEP_COLS, d_dim), BF16)
    dh, gg, gu, gd = _call(
        body, name=name, grid=(FFN_STEPS,),
        in_specs=[full, full, pre, pre, wspec, wspec, wspec], out_specs=[full, gspec, gspec, gspec],
        out_shape=[jax.ShapeDtypeStruct((t_dim, d_dim), F32)] + [grad_shape] * 3,
        scratch_shapes=[pltpu.VMEM((FFN_STEP_COLS, d_dim), F32)] * 3, sem=("arbitrary",), vmem_limit=FFN_BWD_VMEM_LIMIT_BYTES,
        args=[h, dob, pre_g, pre_u, _ffn_step_view(wg), _ffn_step_view(wu), _ffn_step_view(wd)], comm=comm)
    return dh, gg.reshape(wg.shape), gu.reshape(wu.shape), gd.reshape(wd.shape)


CONV_COLS = 256


def _shift_down(u, s, rows):
    return jnp.where(rows >= s, pltpu.roll(u, s, 0), 0.0)


def _shift_up(u, s, rows, t_dim):
    return jnp.where(rows < t_dim - s, pltpu.roll(u, t_dim - s, 0), 0.0)


def _conv_pre(u, w_ref, b_ref, rows):
    c = b_ref[...] + w_ref[CONV_WIDTH - 1:CONV_WIDTH, :] * u
    for k in range(CONV_WIDTH - 1):
        c = c + w_ref[k:k + 1, :] * _shift_down(u, CONV_WIDTH - 1 - k, rows)
    return c


def _conv_fwd(zx, cw, cb, *, name, comm=None):
    t_dim = zx.shape[0]
    off = D_INNER // CONV_COLS

    def body(u_ref, w_ref, b_ref, o_ref):
        rows = lax.broadcasted_iota(jnp.int32, (t_dim, CONV_COLS), 0)
        c = _conv_pre(u_ref[...], w_ref, b_ref, rows)
        o_ref[...] = c * _sigmoid(c)

    return _call(
        body, name=name, grid=(CONV_DIM // CONV_COLS,),
        in_specs=[pl.BlockSpec((t_dim, CONV_COLS), lambda j: (0, off + j)),
                  pl.BlockSpec((CONV_WIDTH, CONV_COLS), lambda j: (0, j)), pl.BlockSpec((1, CONV_COLS), lambda j: (0, j))],
        out_specs=pl.BlockSpec((t_dim, CONV_COLS), lambda j: (0, j)),
        out_shape=jax.ShapeDtypeStruct((t_dim, CONV_DIM), F32), sem=("parallel",), args=[zx, cw, cb], comm=comm)


def _conv_bwd(zx, cw, cb, dxs, db, dc, dzx, *, name, comm=None):
    t_dim = zx.shape[0]
    off = D_INNER // CONV_COLS
    n_xs = D_INNER // CONV_COLS
    n_b = GN // CONV_COLS

    def body(u_ref, w_ref, b_ref, dxs_ref, db_ref, dc_ref, dzx_in, dzx_ref, dw_ref, dbias_ref):
        j = pl.program_id(0)
        rows = lax.broadcasted_iota(jnp.int32, (t_dim, CONV_COLS), 0)
        u = u_ref[...]
        c = _conv_pre(u, w_ref, b_ref, rows)
        d = jnp.where(j < n_xs, dxs_ref[...], jnp.where(j < n_xs + n_b, db_ref[...], dc_ref[...]))
        dcv = d * _dsilu(c, _sigmoid(c))
        dpre = w_ref[CONV_WIDTH - 1:CONV_WIDTH, :] * dcv
        dw_ref[CONV_WIDTH - 1:CONV_WIDTH, :] = jnp.sum(dcv * u, axis=0, keepdims=True)
        for k in range(CONV_WIDTH - 1):
            s = CONV_WIDTH - 1 - k
            dpre = dpre + w_ref[k:k + 1, :] * _shift_up(dcv, s, rows, t_dim)
            dw_ref[k:k + 1, :] = jnp.sum(dcv * _shift_down(u, s, rows), axis=0, keepdims=True)
        dzx_ref[...] = dpre
        dbias_ref[...] = jnp.sum(dcv, axis=0, keepdims=True)

    blk = lambda n: pl.BlockSpec((t_dim, CONV_COLS), n)
    return _call(
        body, name=name, grid=(CONV_DIM // CONV_COLS,),
        in_specs=[blk(lambda j: (0, off + j)), pl.BlockSpec((CONV_WIDTH, CONV_COLS), lambda j: (0, j)),
                  pl.BlockSpec((1, CONV_COLS), lambda j: (0, j)),
                  blk(lambda j: (0, jnp.minimum(j, n_xs - 1))),
                  blk(lambda j: (0, jnp.clip(j - n_xs, 0, n_b - 1))),
                  blk(lambda j: (0, jnp.clip(j - n_xs - n_b, 0, n_b - 1))),
                  pl.BlockSpec(memory_space=pl.ANY)],
        out_specs=[blk(lambda j: (0, off + j)), pl.BlockSpec((CONV_WIDTH, CONV_COLS), lambda j: (0, j)),
                   pl.BlockSpec((1, CONV_COLS), lambda j: (0, j))],
        out_shape=[jax.ShapeDtypeStruct(dzx.shape, F32), jax.ShapeDtypeStruct((CONV_WIDTH, CONV_DIM), F32),
                   jax.ShapeDtypeStruct((1, CONV_DIM), F32)],
        aliases={6: 0}, sem=("parallel",), args=[zx, cw, cb, dxs, db, dc, dzx], comm=comm)


def _softplus_parts(x):
    e = jnp.exp(-jnp.abs(x))
    u = 1.0 + e
    log1p_e = jnp.where(u == 1.0, e, jnp.log(u) * e / jnp.where(u == 1.0, 1.0, u - 1.0))
    return jnp.maximum(x, 0.0) + log1p_e


def _dt_prep(dtr, dt_bias, a_log, *, name):
    def body(dtr_ref, bias_ref, alog_ref, dt_ref, a_ref):
        dt = _softplus_parts(dtr_ref[...] + bias_ref[...])
        dt_ref[...] = dt
        a_ref[...] = dt * (-jnp.exp(alog_ref[...]))

    return pl.pallas_call(body, name=name, out_shape=[jax.ShapeDtypeStruct(dtr.shape, F32)] * 2,
                          compiler_params=_params())(dtr, dt_bias, a_log)


def _dt_bwd(dtr, dt_bias, a_log, dt, ddt, da, *, name):
    def body(dtr_ref, bias_ref, alog_ref, dt_ref, ddt_ref, da_ref, ddtr_ref, dbias_ref, dalog_ref):
        a_neg = -jnp.exp(alog_ref[...])
        da_v = da_ref[...]
        ddt_tot = ddt_ref[...] + da_v * a_neg
        ddtr = ddt_tot * _sigmoid(dtr_ref[...] + bias_ref[...])
        ddtr_ref[...] = ddtr
        dbias_ref[...] = jnp.sum(ddtr, axis=0, keepdims=True)
        dalog_ref[...] = jnp.sum(da_v * dt_ref[...], axis=0, keepdims=True) * a_neg

    return pl.pallas_call(
        body, name=name,
        out_shape=[jax.ShapeDtypeStruct(dtr.shape, F32), jax.ShapeDtypeStruct((1, SSM_HEADS), F32),
                   jax.ShapeDtypeStruct((1, SSM_HEADS), F32)],
        compiler_params=_params())(dtr, dt_bias, a_log, dt, ddt, da)


GROUP_COLS = HEADS_PER_GROUP * SSM_HEAD_DIM
LANES = 128
HEADS_PER_LANE_BLOCK = LANES // SSM_HEAD_DIM


def _split3(x):
    hi = x.astype(BF16)
    r1 = x - hi.astype(F32)
    mid = r1.astype(BF16)
    lo = (r1 - mid.astype(F32)).astype(BF16)
    return hi, mid, lo


def _dot_select(a, b, dims=NN, data=0):
    out = None
    for part in _split3(a if data == 0 else b):
        lhs, rhs = (part, b.astype(BF16)) if data == 0 else (a.astype(BF16), part)
        t = lax.dot_general(lhs, rhs, dims, preferred_element_type=F32)
        out = t if out is None else out + t
    return out


def _group_sums(vals, expand):
    out = _dot_select(jnp.concatenate(vals, axis=0), expand, NT)
    return [out[i * CHUNK:(i + 1) * CHUNK] for i in range(len(vals))]


def _ssd_chunk_common(a_ref, dt_ref, b_ref, c_ref):
    row = lax.broadcasted_iota(jnp.int32, (CHUNK, CHUNK), 0)
    col = lax.broadcasted_iota(jnp.int32, (CHUNK, CHUNK), 1)
    causal = col <= row
    lower = causal.astype(F32)
    upper = (col >= row).astype(F32)
    head = lax.broadcasted_iota(jnp.int32, (HEADS_PER_GROUP, GROUP_COLS), 0)
    lane = lax.broadcasted_iota(jnp.int32, (HEADS_PER_GROUP, GROUP_COLS), 1)
    expand = ((lane >= head * SSM_HEAD_DIM) & (lane < (head + 1) * SSM_HEAD_DIM)).astype(F32)
    a = a_ref[...]
    cs = _dot_select(lower, a, data=1)
    cs_row = _dot_select(a, upper, TN)
    cs_x = _dot_select(cs, expand)
    dt_x = _dot_select(dt_ref[...], expand)
    e_out_x = jnp.exp(cs_x)
    e_st_x = jnp.exp(cs_x[CHUNK - 1:CHUNK, :] - cs_x)
    bc = b_ref[...]
    cc = c_ref[...]
    cb = _dot(cc, bc, NT)
    return causal, upper, expand.astype(BF16), cs, cs_row, dt_x, e_out_x, e_st_x, bc, cc, cb


def _head_decay(causal, cs, cs_row, h):
    return jnp.exp(jnp.where(causal, cs[:, h:h + 1] - cs_row[h:h + 1, :], NEG_BIG))


def _lane_block_head_masks():
    lane = lax.broadcasted_iota(jnp.int32, (CHUNK, LANES), 1)
    return [(lane >= i * SSM_HEAD_DIM) & (lane < (i + 1) * SSM_HEAD_DIM) for i in range(HEADS_PER_LANE_BLOCK)]


def _decay_state(dst_ref, old, new, cs):
    for h in range(HEADS_PER_GROUP):
        rows = slice(h * SSM_HEAD_DIM, (h + 1) * SSM_HEAD_DIM)
        dst_ref[rows, :] = jnp.exp(cs[CHUNK - 1:CHUNK, h:h + 1]) * old[rows, :] + new[rows, :]


def _ssd_fwd(xbc, dtg, ag, dgx, *, name, comm=None):
    t_dim = xbc.shape[0]

    def body(xs_ref, b_ref, c_ref, dt_ref, a_ref, d_ref, y_ref, st_ref, s_scr):
        @pl.when(pl.program_id(1) == 0)
        def _():
            s_scr[...] = jnp.zeros_like(s_scr)

        causal, _, _, cs, cs_row, dt_x, e_out_x, e_st_x, bc, cc, cb = _ssd_chunk_common(a_ref, dt_ref, b_ref, c_ref)
        masks = _lane_block_head_masks()
        xs = xs_ref[...]
        xdt_x = xs * dt_x
        prev = s_scr[...]
        st_ref[...] = prev
        y_off = e_out_x * _dot(cc, prev, NT) + xs * d_ref[...]
        for blk in range(GROUP_COLS // LANES):
            lanes = slice(blk * LANES, (blk + 1) * LANES)
            x_b = xdt_x[:, lanes].astype(BF16)
            acc = y_off[:, lanes]
            for i in range(HEADS_PER_LANE_BLOCK):
                m = cb * _head_decay(causal, cs, cs_row, blk * HEADS_PER_LANE_BLOCK + i)
                acc = acc + _dot(m, jnp.where(masks[i], x_b, jnp.zeros_like(x_b)))
            y_ref[:, lanes] = acc
        _decay_state(s_scr, prev, _dot(xdt_x * e_st_x, bc, TN), cs)

    xs = pl.BlockSpec((CHUNK, GROUP_COLS), lambda g, c: (c, g))
    bsp = pl.BlockSpec((CHUNK, SSM_STATE), lambda g, c: (c, D_INNER // SSM_STATE + g))
    csp = pl.BlockSpec((CHUNK, SSM_STATE), lambda g, c: (c, (D_INNER + GN) // SSM_STATE + g))
    per_head = pl.BlockSpec((None, CHUNK, HEADS_PER_GROUP), lambda g, c: (g, c, 0))
    dsk = pl.BlockSpec((None, 1, GROUP_COLS), lambda g, c: (g, 0, 0))
    return _call(
        body, name=name, grid=(SSM_GROUPS, N_CHUNKS),
        in_specs=[xs, bsp, csp, per_head, per_head, dsk],
        out_specs=[xs, pl.BlockSpec((None, GROUP_COLS, SSM_STATE), lambda g, c: (c, g, 0))],
        out_shape=[jax.ShapeDtypeStruct((t_dim, D_INNER), F32),
                   jax.ShapeDtypeStruct((N_CHUNKS, D_INNER, SSM_STATE), F32)],
        scratch_shapes=[pltpu.VMEM((GROUP_COLS, SSM_STATE), F32)],
        sem=("parallel", "arbitrary"), args=[xbc, xbc, xbc, dtg, ag, dgx], comm=comm)


def _ssd_bwd(xbc, dtg, ag, dgx, states, dy, *, name, comm=None):
    t_dim = xbc.shape[0]
    last = N_CHUNKS - 1

    def body(xs_ref, b_ref, c_ref, dt_ref, a_ref, d_ref, st_ref, dy_ref,
             dxs_ref, db_ref, dc_ref, ddt_ref, da_ref, dd_ref, ds_scr):
        @pl.when(pl.program_id(1) == 0)
        def _():
            ds_scr[...] = jnp.zeros_like(ds_scr)
            dd_ref[...] = jnp.zeros_like(dd_ref)

        causal, upper, expand, cs, cs_row, dt_x, e_out_x, e_st_x, bc, cc, cb = _ssd_chunk_common(a_ref, dt_ref, b_ref, c_ref)
        masks = _lane_block_head_masks()
        xs = xs_ref[...]
        dy_x = dy_ref[...]
        xdt_x = xs * dt_x
        prev = st_ref[...]
        d_s = ds_scr[...]
        g1_x = _dot(bc, d_s, NT)
        cp_x = _dot(cc, prev, NT)
        d_cb = jnp.zeros((CHUNK, CHUNK), F32)
        lane8 = lax.broadcasted_iota(jnp.int32, (CHUNK, HEADS_PER_GROUP), 1)
        sub8 = lax.broadcasted_iota(jnp.int32, (HEADS_PER_GROUP, CHUNK), 0)
        row_w = jnp.zeros((CHUNK, HEADS_PER_GROUP), F32)
        col_w = jnp.zeros((HEADS_PER_GROUP, CHUNK), F32)
        dxdt_blocks = []
        for blk in range(GROUP_COLS // LANES):
            lanes = slice(blk * LANES, (blk + 1) * LANES)
            dy_b = dy_x[:, lanes].astype(BF16)
            x_b = xdt_x[:, lanes].astype(BF16)
            acc_dx = jnp.zeros((CHUNK, LANES), F32)
            for i in range(HEADS_PER_LANE_BLOCK):
                h = blk * HEADS_PER_LANE_BLOCK + i
                decay = _head_decay(causal, cs, cs_row, h)
                m = cb * decay
                dy_h = jnp.where(masks[i], dy_b, jnp.zeros_like(dy_b))
                acc_dx = acc_dx + _dot(m, dy_h, TN)
                d_m = _dot(dy_h, x_b, NT)
                d_cb = d_cb + d_m * decay
                w = d_m * m
                row_w = jnp.where(lane8 == h, jnp.sum(w, axis=1, keepdims=True), row_w)
                col_w = jnp.where(sub8 == h, jnp.sum(w, axis=0, keepdims=True), col_w)
            dxdt_blocks.append(acc_dx)
        dxdt_x = jnp.concatenate(dxdt_blocks, axis=1) + e_st_x * g1_x
        dxs_ref[...] = dxdt_x * dt_x + dy_x * d_ref[...]
        dye = dy_x * e_out_x
        xde = xdt_x * e_st_x
        ddt, y_off, tl, dskip = _group_sums([dxdt_x * xs, dye * cp_x, xde * g1_x, dy_x * xs], expand)
        ddt_ref[...] = ddt
        dd_ref[...] += jnp.sum(dskip, axis=0, keepdims=True)
        sp = None
        for part in _split3(d_s * prev):
            t = lax.dot_general(expand, part, NN, preferred_element_type=F32)
            sp = t if sp is None else sp + t
        last_col = jnp.exp(cs_row[:, CHUNK - 1:CHUNK]) * jnp.sum(sp, axis=1, keepdims=True)
        eye = lax.broadcasted_iota(jnp.int32, (HEADS_PER_GROUP, HEADS_PER_GROUP), 0) == lax.broadcasted_iota(
            jnp.int32, (HEADS_PER_GROUP, HEADS_PER_GROUP), 1)
        last_row = jnp.sum(jnp.where(eye, last_col, 0.0), axis=0, keepdims=True) + jnp.sum(tl, axis=0, keepdims=True)
        is_last = lax.broadcasted_iota(jnp.int32, (CHUNK, 1), 0) == CHUNK - 1
        d_cs = row_w + y_off - tl + jnp.where(is_last, last_row, 0.0)
        da_ref[...] = _dot_select(upper, d_cs, data=1) - _dot_select(upper, col_w, NT, data=1)
        dc_ref[...] = _dot(d_cb, bc) + _dot(dye, prev)
        db_ref[...] = _dot(d_cb, cc, TN) + _dot(xde, d_s)
        _decay_state(ds_scr, d_s, _dot(dye, cc, TN), cs)

    rev = lambda c: last - c
    xs = pl.BlockSpec((CHUNK, GROUP_COLS), lambda g, c: (rev(c), g))
    bsp = pl.BlockSpec((CHUNK, SSM_STATE), lambda g, c: (rev(c), D_INNER // SSM_STATE + g))
    csp = pl.BlockSpec((CHUNK, SSM_STATE), lambda g, c: (rev(c), (D_INNER + GN) // SSM_STATE + g))
    per_head = pl.BlockSpec((None, CHUNK, HEADS_PER_GROUP), lambda g, c: (g, rev(c), 0))
    dsk = pl.BlockSpec((None, 1, GROUP_COLS), lambda g, c: (g, 0, 0))
    dsum = pl.BlockSpec((None, 1, HEADS_PER_GROUP), lambda g, c: (g, 0, 0))
    st = pl.BlockSpec((None, GROUP_COLS, SSM_STATE), lambda g, c: (rev(c), g, 0))
    grp = pl.BlockSpec((CHUNK, SSM_STATE), lambda g, c: (rev(c), g))
    return _call(
        body, name=name, grid=(SSM_GROUPS, N_CHUNKS),
        in_specs=[xs, bsp, csp, per_head, per_head, dsk, st, xs],
        out_specs=[xs, grp, grp, per_head, per_head, dsum],
        out_shape=[jax.ShapeDtypeStruct((t_dim, D_INNER), F32), jax.ShapeDtypeStruct((t_dim, GN), F32),
                   jax.ShapeDtypeStruct((t_dim, GN), F32),
                   jax.ShapeDtypeStruct((SSM_GROUPS, t_dim, HEADS_PER_GROUP), F32),
                   jax.ShapeDtypeStruct((SSM_GROUPS, t_dim, HEADS_PER_GROUP), F32),
                   jax.ShapeDtypeStruct((SSM_GROUPS, 1, HEADS_PER_GROUP), F32)],
        scratch_shapes=[pltpu.VMEM((GROUP_COLS, SSM_STATE), F32)],
        sem=("parallel", "arbitrary"), args=[xbc, xbc, xbc, dtg, ag, dgx, states, dy], comm=comm)


NORM_GROUP = D_INNER // SSM_GROUPS


def _gate_norm_fwd(y, zx, nw, *, name, tm=256):
    t_dim = y.shape[0]
    row = pl.BlockSpec((tm, D_INNER), lambda i: (i, 0))

    def body(y_ref, z_ref, nw_ref, o_ref):
        z = z_ref[...]
        yz = y_ref[...] * (z * _sigmoid(z))
        for g in range(SSM_GROUPS):
            cols = slice(g * NORM_GROUP, (g + 1) * NORM_GROUP)
            yhat, _ = _rms(yz[:, cols])
            o_ref[:, cols] = (yhat * nw_ref[:, cols]).astype(BF16)

    return pl.pallas_call(
        body, name=name, grid=(t_dim // tm,), in_specs=[row, row, pl.BlockSpec((1, D_INNER), lambda i: (0, 0))],
        out_specs=row, out_shape=jax.ShapeDtypeStruct((t_dim, D_INNER), BF16),
        compiler_params=_params("parallel"),
    )(y, zx, nw)


def _gate_norm_bwd(y, zx, nw, dyn, *, name, tm=256):
    t_dim = y.shape[0]
    row = pl.BlockSpec((tm, D_INNER), lambda i: (i, 0))
    vec = pl.BlockSpec((1, D_INNER), lambda i: (0, 0))

    def body(y_ref, z_ref, nw_ref, dyn_ref, dy_ref, dz_ref, dnw_ref):
        @pl.when(pl.program_id(0) == 0)
        def _():
            dnw_ref[...] = jnp.zeros_like(dnw_ref)

        z = z_ref[...]
        yv = y_ref[...]
        sg = _sigmoid(z)
        silu_z = z * sg
        yz = yv * silu_z
        dyn_v = dyn_ref[...]
        for g in range(SSM_GROUPS):
            cols = slice(g * NORM_GROUP, (g + 1) * NORM_GROUP)
            yhat, r = _rms(yz[:, cols])
            dn = dyn_v[:, cols]
            dnw_ref[:, cols] += jnp.sum(dn * yhat, axis=0, keepdims=True)
            dyhat = dn * nw_ref[:, cols]
            dyz = r * (dyhat - yhat * jnp.mean(dyhat * yhat, axis=-1, keepdims=True))
            dy_ref[:, cols] = dyz * silu_z[:, cols]
            dz_ref[:, cols] = dyz * yv[:, cols] * _dsilu(z[:, cols], sg[:, cols])

    return pl.pallas_call(
        body, name=name, grid=(t_dim // tm,), in_specs=[row, row, vec, row],
        out_specs=[row, row, vec],
        out_shape=[jax.ShapeDtypeStruct((t_dim, D_INNER), F32), jax.ShapeDtypeStruct((t_dim, ZX_DIM), F32),
                   jax.ShapeDtypeStruct((1, D_INNER), F32)],
        compiler_params=_params("arbitrary"),
    )(y, zx, nw, dyn)


HEADS_PER_LANE_TILE = LANES // ATT_HEAD_DIM
STACKED_ROWS = Q_PER_KV * WINDOW


def _att_half_masks():
    lane = lax.broadcasted_iota(jnp.int32, (WINDOW, LANES), 1)
    return [(lane >= i * ATT_HEAD_DIM) & (lane < (i + 1) * ATT_HEAD_DIM) for i in range(HEADS_PER_LANE_TILE)]


def _att_stack_heads(ref, kvh, masks):
    parts = []
    for g in range(Q_PER_KV):
        h = kvh * Q_PER_KV + g
        blk = ref[:, (h // HEADS_PER_LANE_TILE) * LANES:(h // HEADS_PER_LANE_TILE + 1) * LANES]
        parts.append(jnp.where(masks[h % HEADS_PER_LANE_TILE], blk, jnp.zeros_like(blk)))
    return jnp.concatenate(parts, axis=0)


def _att_kv_tile(ref, kvh, masks):
    blk = ref[:, (kvh // HEADS_PER_LANE_TILE) * LANES:(kvh // HEADS_PER_LANE_TILE + 1) * LANES]
    return jnp.where(masks[kvh % HEADS_PER_LANE_TILE], blk, pltpu.roll(blk, ATT_HEAD_DIM, 1)).astype(BF16)


def _att_stacked_masks(n):
    row = lax.bitwise_and(lax.broadcasted_iota(jnp.int32, (STACKED_ROWS, WINDOW), 0), WINDOW - 1)
    col = lax.broadcasted_iota(jnp.int32, (STACKED_ROWS, WINDOW), 1)
    return col <= row, (col > row) & (n > 0)


def _att_stack_columns(ref, kvh, rows):
    cols = [ref[:, kvh * Q_PER_KV + g:kvh * Q_PER_KV + g + 1] for g in range(Q_PER_KV)]
    return jnp.concatenate([jnp.broadcast_to(c, (rows, 1)) for c in cols], axis=0)


def _att_scores(q4, k_tile, mask):
    return jnp.where(mask, _dot(q4, k_tile, NT) * ATT_SCALE, NEG_BIG)


def _att_unstack(x4, kvh, masks, tiles):
    for g in range(Q_PER_KV):
        h = kvh * Q_PER_KV + g
        piece = x4[g * WINDOW:(g + 1) * WINDOW]
        t = h // HEADS_PER_LANE_TILE
        tiles[t] = piece if h % HEADS_PER_LANE_TILE == 0 else jnp.where(masks[1], piece, tiles[t])


def _attn_fwd(q, k, v, sinks, *, name, comm=None):
    t_dim = q.shape[0]

    def body(q_ref, kc_ref, kp_ref, vc_ref, vp_ref, s_ref, o_ref, l_ref):
        n = pl.program_id(0)
        masks = _att_half_masks()
        mask_c, mask_p = _att_stacked_masks(n)
        out_tiles = [None] * (D_MODEL // LANES)
        for kvh in range(N_KV_HEADS):
            q4 = _att_stack_heads(q_ref, kvh, masks).astype(BF16)
            kc, kp = _att_kv_tile(kc_ref, kvh, masks), _att_kv_tile(kp_ref, kvh, masks)
            vc, vp = _att_kv_tile(vc_ref, kvh, masks), _att_kv_tile(vp_ref, kvh, masks)
            sc = _att_scores(q4, kc, mask_c)
            sp = _att_scores(q4, kp, mask_p)
            sink = _att_stack_columns(s_ref, kvh, WINDOW)
            m = jnp.maximum(jnp.maximum(jnp.max(sc, axis=1, keepdims=True), jnp.max(sp, axis=1, keepdims=True)), sink)
            pc = jnp.exp(sc - m)
            pp = jnp.exp(sp - m)
            den = jnp.sum(pc, axis=1, keepdims=True) + jnp.sum(pp, axis=1, keepdims=True) + jnp.exp(sink - m)
            _att_unstack((_dot(pc, vc) + _dot(pp, vp)) / den, kvh, masks, out_tiles)
            lse4 = m + jnp.log(den)
            for g in range(Q_PER_KV):
                h = kvh * Q_PER_KV + g
                l_ref[:, h:h + 1] = lse4[g * WINDOW:(g + 1) * WINDOW]
        for t, tile in enumerate(out_tiles):
            o_ref[:, t * LANES:(t + 1) * LANES] = tile

    cur = lambda w: pl.BlockSpec((WINDOW, w), lambda n: (n, 0))
    prv = lambda w: pl.BlockSpec((WINDOW, w), lambda n: (jnp.maximum(n - 1, 0), 0))
    return _call(
        body, name=name, grid=(t_dim // WINDOW,),
        in_specs=[cur(D_MODEL), cur(KV_DIM), prv(KV_DIM), cur(KV_DIM), prv(KV_DIM), pl.BlockSpec((1, N_Q_HEADS), lambda n: (0, 0))],
        out_specs=[cur(D_MODEL), cur(N_Q_HEADS)],
        out_shape=[jax.ShapeDtypeStruct((t_dim, D_MODEL), F32), jax.ShapeDtypeStruct((t_dim, N_Q_HEADS), F32)],
        sem=("parallel",), args=[q, k, k, v, v, sinks], comm=comm)


def _attn_bwd(q, k, v, sinks, o, lse, do, cos2, sin2, *, name, comm=None):
    t_dim = q.shape[0]

    def body(q_ref, kc_ref, kp_ref, vc_ref, vp_ref, s_ref, o_ref, l_ref, do_ref, cos_ref, sin_ref, cos_all_ref, sin_all_ref,
             dq_ref, dk_ref, dv_ref, dsink_ref):
        n = pl.program_id(0)

        @pl.when(n == 0)
        def _():
            dk_ref[...] = jnp.zeros_like(dk_ref)
            dv_ref[...] = jnp.zeros_like(dv_ref)
            dsink_ref[...] = jnp.zeros_like(dsink_ref)

        masks = _att_half_masks()
        mask_c, mask_p = _att_stacked_masks(n)
        lane_row = lax.broadcasted_iota(jnp.int32, (1, N_Q_HEADS), 1)
        rows_c = pl.ds(pl.multiple_of(n * WINDOW, WINDOW), WINDOW)
        rows_p = pl.ds(pl.multiple_of(jnp.maximum(n - 1, 0) * WINDOW, WINDOW), WINDOW)
        dsink = jnp.zeros((1, N_Q_HEADS), F32)
        dq_tiles = [None] * (D_MODEL // LANES)
        kv_tiles = KV_DIM // LANES
        dkc_tiles, dkp_tiles, dvc_tiles, dvp_tiles = ([None] * kv_tiles for _ in range(4))

        def place(tiles, kvh, x):
            folded = x + pltpu.roll(x, ATT_HEAD_DIM, 1)
            t = kvh // HEADS_PER_LANE_TILE
            tiles[t] = folded if kvh % HEADS_PER_LANE_TILE == 0 else jnp.where(masks[1], folded, tiles[t])

        for kvh in range(N_KV_HEADS):
            q4 = _att_stack_heads(q_ref, kvh, masks).astype(BF16)
            do4 = _att_stack_heads(do_ref, kvh, masks)
            o4 = _att_stack_heads(o_ref, kvh, masks)
            kc, kp = _att_kv_tile(kc_ref, kvh, masks), _att_kv_tile(kp_ref, kvh, masks)
            vc, vp = _att_kv_tile(vc_ref, kvh, masks), _att_kv_tile(vp_ref, kvh, masks)
            l4 = _att_stack_columns(l_ref, kvh, WINDOW)
            pc = jnp.exp(_att_scores(q4, kc, mask_c) - l4)
            pp = jnp.exp(_att_scores(q4, kp, mask_p) - l4)
            delta = jnp.sum(do4 * o4, axis=1, keepdims=True)
            do4b = do4.astype(BF16)
            dsc = pc * (_dot(do4b, vc, NT) - delta)
            dsp = pp * (_dot(do4b, vp, NT) - delta)
            _att_unstack((_dot(dsc, kc) + _dot(dsp, kp)) * ATT_SCALE, kvh, masks, dq_tiles)
            place(dkc_tiles, kvh, _dot(dsc, q4, TN) * ATT_SCALE)
            place(dkp_tiles, kvh, _dot(dsp, q4, TN) * ATT_SCALE)
            place(dvc_tiles, kvh, _dot(pc, do4b, TN))
            place(dvp_tiles, kvh, _dot(pp, do4b, TN))
            p_sink = jnp.exp(_att_stack_columns(s_ref, kvh, WINDOW) - l4) * delta
            for g in range(Q_PER_KV):
                h = kvh * Q_PER_KV + g
                dsink = jnp.where(lane_row == h, -jnp.sum(p_sink[g * WINDOW:(g + 1) * WINDOW], axis=0, keepdims=True), dsink)
        for t, tile in enumerate(dq_tiles):
            dq_ref[:, t * LANES:(t + 1) * LANES] = _rope_rotate(tile, cos_ref[...], -sin_ref[...])
        for t in range(kv_tiles):
            lanes = slice(t * LANES, (t + 1) * LANES)
            dk_ref[rows_c, lanes] += dkc_tiles[t]
            dk_ref[rows_p, lanes] += dkp_tiles[t]
            dv_ref[rows_c, lanes] += dvc_tiles[t]
            dv_ref[rows_p, lanes] += dvp_tiles[t]
        dsink_ref[...] += dsink

        @pl.when(n == t_dim // WINDOW - 1)
        def _():
            dk_ref[...] = _rope_rotate(dk_ref[...], cos_all_ref[...], -sin_all_ref[...])

    cur = lambda w: pl.BlockSpec((WINDOW, w), lambda n: (n, 0))
    prv = lambda w: pl.BlockSpec((WINDOW, w), lambda n: (jnp.maximum(n - 1, 0), 0))
    whole = lambda w: pl.BlockSpec((t_dim, w), lambda n: (0, 0))
    svec = pl.BlockSpec((1, N_Q_HEADS), lambda n: (0, 0))
    return _call(
        body, name=name, grid=(t_dim // WINDOW,),
        in_specs=[cur(D_MODEL), cur(KV_DIM), prv(KV_DIM), cur(KV_DIM), prv(KV_DIM), svec, cur(D_MODEL), cur(N_Q_HEADS), cur(D_MODEL),
                  cur(LANES), cur(LANES), whole(LANES), whole(LANES)],
        out_specs=[cur(D_MODEL), whole(KV_DIM), whole(KV_DIM), svec],
        out_shape=[jax.ShapeDtypeStruct((t_dim, D_MODEL), F32), jax.ShapeDtypeStruct((t_dim, KV_DIM), F32),
                   jax.ShapeDtypeStruct((t_dim, KV_DIM), F32), jax.ShapeDtypeStruct((1, N_Q_HEADS), F32)],
        sem=("arbitrary",), args=[q, k, k, v, v, sinks, o, lse, do, cos2, sin2, cos2, sin2], comm=comm)


def _loss_head(x, nw, target, *, name, tm=512):
    t_dim, d_dim = x.shape
    row = pl.BlockSpec((tm, d_dim), lambda i: (i, 0))
    vec = pl.BlockSpec((1, d_dim), lambda i: (0, 0))

    def body(x_ref, nw_ref, tgt_ref, loss_ref, dx_ref, dnw_ref, dob_ref):
        @pl.when(pl.program_id(0) == 0)
        def _():
            loss_ref[...] = jnp.zeros_like(loss_ref)
            dnw_ref[...] = jnp.zeros_like(dnw_ref)

        xhat, r = _rms(x_ref[...])
        err = xhat * nw_ref[...] - tgt_ref[...]
        loss_ref[...] += 0.5 * _sum_all(jnp.mean(err * err, axis=-1, keepdims=True))
        dy = err * (1.0 / d_dim)
        dnw_ref[...] += jnp.sum(dy * xhat, axis=0, keepdims=True)
        dxhat = dy * nw_ref[...]
        dx = r * (dxhat - xhat * jnp.mean(dxhat * xhat, axis=-1, keepdims=True))
        dx_ref[...] = dx
        dob_ref[...] = (FFN_RES_WEIGHT * dx).astype(BF16)

    return pl.pallas_call(
        body, name=name, grid=(t_dim // tm,), in_specs=[row, vec, row],
        out_specs=[pl.BlockSpec((1, 1), lambda i: (0, 0)), row, vec, row],
        out_shape=[jax.ShapeDtypeStruct((1, 1), F32), jax.ShapeDtypeStruct((t_dim, d_dim), F32),
                   jax.ShapeDtypeStruct((1, d_dim), F32), jax.ShapeDtypeStruct((t_dim, d_dim), BF16)],
        compiler_params=_params("arbitrary"),
    )(x, nw, target)


def _rope_tables():
    pos = jnp.arange(SEQ, dtype=F32)
    inv = 1.0 / (ROPE_THETA ** (jnp.arange(0, ATT_HEAD_DIM, 2, dtype=F32) / ATT_HEAD_DIM))
    ang = pos[:, None] * inv[None, :]
    cos, sin = jnp.cos(ang), jnp.sin(ang)
    return jnp.tile(cos, (1, 4)), jnp.tile(sin, (1, 4))


def _to_groups(t):
    return t.reshape(t.shape[0], SSM_GROUPS, HEADS_PER_GROUP).transpose(1, 0, 2)


def _from_groups(t):
    return t.transpose(1, 0, 2).reshape(t.shape[1], SSM_HEADS)


def _forward_backward(x0, target, net):
    w = net.w
    nw = [[w("norm_w")[l, i][None, :] for i in range(3)] for l in range(2)]
    cos2, sin2 = _rope_tables()
    ffn_norm = [nw[0][0], nw[0][2], nw[1][0], nw[1][2]]

    ffn_pre = {}

    def ffn_f(x, blk):
        name = f"ffn_fwd{blk}"
        out, *ffn_pre[blk] = _ffn_fwd(x, ffn_norm[blk], w(f"gate{blk}"), w(f"up{blk}"), w(f"down{blk}"), name=name,
                                      comm=net.carry(name))
        return out

    x1 = ffn_f(x0, 0)
    zx, h1 = _norm_mm(x1, nw[0][1], w("w_in_t"), None, w_rows=ZX_DIM, name="ssm_in_proj", comm=net.carry("ssm_in_proj"))
    dtr = _mm(h1, w("w_in_t"), dims="nt", b_rows=(ZX_DIM, SSM_HEADS), name="ssm_dt_proj")
    xbc = _conv_fwd(zx, w("conv_w"), w("conv_b"), name="ssm_conv_fwd", comm=net.carry("ssm_conv_fwd"))
    dt, a_dt = _dt_prep(dtr, w("dt_bias"), w("a_log"), name="ssm_dt_prep")
    dtg, ag = _to_groups(dt), _to_groups(a_dt)
    dg = jnp.repeat(w("d_skip").reshape(SSM_GROUPS, 1, HEADS_PER_GROUP), SSM_HEAD_DIM, axis=2)
    y_ssd, states = _ssd_fwd(xbc, dtg, ag, dg, name="ssd_fwd", comm=net.carry("ssd_fwd"))
    yn = _gate_norm_fwd(y_ssd, zx, w("ssm_norm_w"), name="ssm_gate_norm_fwd")
    x2 = _mm(yn, w("wout"), res=x1, name="ssm_out_proj", comm=net.carry("ssm_out_proj"))
    x3 = ffn_f(x2, 1)
    k_rot, hk = _norm_mm(x3, w("kv_norm_w"), w("wk"), w("b_k"), rope=(cos2, sin2), name="k_proj")
    v = _mm(hk, w("wv"), bias=w("b_v"), name="v_proj")
    x4 = ffn_f(x3, 2)
    q_rot, h4 = _norm_mm(x4, nw[1][1], w("wq"), w("b_q"), rope=(cos2, sin2), name="q_proj")
    att, lse = _attn_fwd(q_rot, k_rot, v, w("sinks"), name="attn_fwd", comm=net.carry("attn_fwd"))
    x5 = _mm(att, w("wo"), bias=w("b_o"), res=x4, name="attn_out_proj")
    x6 = ffn_f(x5, 3)
    loss, dx6, d_final, dob6 = _loss_head(x6, w("final_norm_w"), target, name="loss_head")

    d_norm = [[None] * 3 for _ in range(2)]

    def ffn_b(x, dout, dob, blk):
        pre_g, pre_u, h = ffn_pre[blk]
        name = f"ffn_bwd{blk}"
        dh, gg, gu, gd = _ffn_bwd(h, dob, pre_g, pre_u, w(f"gate{blk}"), w(f"up{blk}"), w(f"down{blk}"), name=name,
                                  comm=net.carry(name))
        net.give(f"gate{blk}", gg)
        net.give(f"up{blk}", gu)
        net.give(f"down{blk}", gd)
        return _norm_bwd(x, ffn_norm[blk], dh, [dout], name=f"ffn_norm_bwd{blk}", comm=net.carry(f"ffn_norm_bwd{blk}"))

    by_rows = lambda g: g.reshape(N_DEV, g.shape[0] // N_DEV, g.shape[1])
    dx5, d_norm[1][2] = ffn_b(x5, dx6, dob6, 3)
    d_att = _mm(dx5, w("wo"), dims="nt", name="attn_out_proj_dx", comm=net.carry("attn_out_proj_dx"))
    g_o, d_bo = _mm(att, dx5, dims="tn", out_dtype=BF16, colsum_b=True, name="attn_out_proj_dw")
    net.give("w_o", by_rows(g_o))
    dq, dk, dv, d_sinks = _attn_bwd(q_rot, k_rot, v, w("sinks"), att, lse, d_att, cos2, sin2, name="attn_bwd",
                                    comm=net.carry("attn_bwd"))
    dx4, d_norm[1][1], dob4 = _mm_norm_bwd(dq, w("wq"), x4, nw[1][1], [dx5], dims="nt", name="q_proj_dx")
    g_q, d_bq = _mm(h4, dq, dims="tn", out_dtype=BF16, colsum_b=True, name="q_proj_dw")
    net.give("w_q", by_rows(g_q))
    dx3a, d_norm[1][0] = ffn_b(x3, dx4, dob4, 2)
    dhk = _mm(dk, w("wk"), dims="nt", name="k_proj_dx", comm=net.carry("k_proj_dx"))
    dx3, d_kvn, dob3 = _mm_norm_bwd(dv, w("wv"), x3, w("kv_norm_w"), [dx3a], dims="nt", add=dhk, name="v_proj_dx")
    g_k, d_bk = _mm(hk, dk, dims="tn", out_dtype=BF16, colsum_b=True, name="k_proj_dw")
    g_v, d_bv = _mm(hk, dv, dims="tn", out_dtype=BF16, colsum_b=True, name="v_proj_dw")
    net.give("w_k", by_rows(g_k))
    net.give("w_v", by_rows(g_v))
    dx2, d_norm[0][2] = ffn_b(x2, dx3, dob3, 1)
    d_yn = _mm(dx2, w("wout"), dims="nt", name="ssm_out_proj_dx", comm=net.carry("ssm_out_proj_dx"))
    net.give("w_out", by_rows(_mm(yn, dx2, dims="tn", out_dtype=BF16, name="ssm_out_proj_dw")))
    dy_ssd, dzx, d_ssm_norm = _gate_norm_bwd(y_ssd, zx, w("ssm_norm_w"), d_yn, name="ssm_gate_norm_bwd")
    dxs, d_b, d_c, ddtg, dag, ddg = _ssd_bwd(xbc, dtg, ag, dg, states, dy_ssd, name="ssd_bwd", comm=net.carry("ssd_bwd"))
    dzx, d_conv_w, d_conv_b = _conv_bwd(zx, w("conv_w"), w("conv_b"), dxs, d_b, d_c, dzx, name="ssm_conv_bwd",
                                        comm=net.carry("ssm_conv_bwd"))
    ddtr, d_dt_bias, d_a_log = _dt_bwd(dtr, w("dt_bias"), w("a_log"), dt, _from_groups(ddtg), _from_groups(dag), name="ssm_dt_bwd")
    dh1 = _mm(dzx, w("w_in_t"), b_rows=(0, ZX_DIM), name="ssm_in_proj_dx", comm=net.carry("ssm_in_proj_dx"))
    in_rows = N_DEV * IN_PROJ_SHARD
    g_in = _mm(dzx, h1, dims="tn", out_dtype=BF16, out_window=(0, in_rows), name="ssm_in_proj_dw")
    g_in = _mm(ddtr, h1, dims="tn", out_dtype=BF16, out_window=(ZX_DIM, in_rows), into=g_in, name="ssm_dt_proj_dw")
    net.give("w_in", g_in.reshape(N_DEV, IN_PROJ_SHARD, D_MODEL))
    dx1, d_norm[0][1], dob1 = _mm_norm_bwd(ddtr, w("w_in_t"), x1, nw[0][1], [dx2], b_rows=(ZX_DIM, SSM_HEADS), add=dh1,
                                           name="ssm_dt_proj_dx", comm=net.carry("ssm_norm_bwd"))
    dx0, d_norm[0][0] = ffn_b(x0, dx1, dob1, 0)

    small = {"norm_w": jnp.concatenate([d_norm[l][i] for l in range(2) for i in range(3)], axis=0),
             "ssm_conv_w": d_conv_w, "ssm_conv_b": d_conv_b, "ssm_dt_bias": d_dt_bias, "ssm_a_log": d_a_log,
             "ssm_d": ddg.reshape(1, SSM_HEADS), "ssm_norm_w": d_ssm_norm, "kv_norm_w": d_kvn,
             "b_k": d_bk, "b_v": d_bv, "attn_b_q": d_bq, "attn_sinks": d_sinks, "attn_b_o": d_bo, "final_norm_w": d_final}
    return loss, dx0, small


BLOCK_BYTES = 1 << 20


def _row_tile(rows, cols):
    for t in (512, 256, 128, 64, 32, 16):
        if rows % t == 0 and t * cols * 4 <= BLOCK_BYTES:
            return t
    return rows


def _cast_bf16(x, *, name):
    n_blk, rows, cols = x.shape
    tm = rows if rows * cols * 4 <= 2 * BLOCK_BYTES else _row_tile(rows, cols)
    spec = pl.BlockSpec((None, tm, cols), lambda b, i: (b, i, 0))

    def body(x_ref, o_ref):
        o_ref[...] = x_ref[...].astype(BF16)

    return pl.pallas_call(body, name=name, grid=(n_blk, rows // tm), in_specs=[spec], out_specs=spec,
                          out_shape=jax.ShapeDtypeStruct(x.shape, BF16), compiler_params=_params("parallel", "parallel"))(x)


def _pair_add(grad, theirs, *, name):
    n_slots, rows, cols = theirs.shape
    tm = rows if rows * cols * 4 <= 2 * BLOCK_BYTES else _row_tile(rows, cols)

    def body(g_ref, t_ref, o_ref):
        o_ref[...] = (g_ref[...].astype(F32) + t_ref[...].astype(F32)).astype(BF16)

    spec = pl.BlockSpec((None, tm, cols), lambda s, i: (s, i, 0))
    return pl.pallas_call(
        body, name=name, grid=(n_slots, rows // tm),
        in_specs=[pl.BlockSpec((None, tm, cols), lambda s, i: (2 * s + lax.axis_index("c"), i, 0)), spec], out_specs=spec,
        out_shape=jax.ShapeDtypeStruct(theirs.shape, BF16), compiler_params=_params("parallel", "parallel"),
    )(grad, theirs)


def _adam_update(g, w, m, v):
    m = ADAM_B1 * m + (1.0 - ADAM_B1) * g
    v = ADAM_B2 * v + (1.0 - ADAM_B2) * (g * g)
    m_hat = m / (1.0 - ADAM_B1 ** ADAM_STEP)
    v_hat = v / (1.0 - ADAM_B2 ** ADAM_STEP)
    delta = -ADAM_LR * (m_hat / (jnp.sqrt(v_hat) + ADAM_EPS) + ADAM_WD * w)
    return delta, m, v


def _adamw(parts, w, m, v, first_blk, prev, *, name, comm=None):
    n_blk, rows, cols = w.shape
    tm = _row_tile(rows, cols)
    n_tiles = rows // tm
    spec = pl.BlockSpec((None, tm, cols), lambda b, i: (first_blk + b, i, 0))
    n_prev, n_here = len(prev), len(parts)
    n_parts = parts[0].shape[0]

    def part_spec(q):
        return pl.BlockSpec((n_parts, tm, cols), lambda b, i: (0, jnp.where(b < q, 0, jnp.where(b == q, i, n_tiles - 1)), 0))

    def body(*refs):
        p_refs = refs[:n_here]
        w_ref, m_ref, v_ref = refs[n_here:n_here + 3]
        g_ref, d_ref, nm_ref, nv_ref = refs[n_here + 3 + n_prev:]
        b = pl.program_id(0)
        g = None
        for s in range(n_parts):
            t = p_refs[0][s]
            for q in range(1, n_here):
                t = jnp.where(b == q, p_refs[q][s], t)
            g = t.astype(F32) if g is None else g + t.astype(F32)
        delta, nm, nv = _adam_update(g, w_ref[...], m_ref[...], v_ref[...])
        g_ref[...] = g
        d_ref[...] = delta
        nm_ref[...] = nm
        nv_ref[...] = nv

    return _call(
        body, name=name, grid=(n_here, n_tiles),
        in_specs=[part_spec(q) for q in range(n_here)] + [spec, spec, spec] + [pl.BlockSpec(memory_space=pl.ANY)] * n_prev,
        out_specs=[spec] * 4, out_shape=[jax.ShapeDtypeStruct((n_blk, rows, cols), F32)] * 4,
        aliases={n_here + 3 + q: q for q in range(n_prev)}, sem=("arbitrary", "arbitrary"),
        args=[*parts, w, m, v, *prev], comm=comm)


def _sum_parts(parts, *, name):
    def body(p_ref, o_ref):
        g = p_ref[0]
        for s in range(1, N_DEV):
            g = g + p_ref[s]
        o_ref[...] = g

    return pl.pallas_call(body, name=name, out_shape=jax.ShapeDtypeStruct(parts.shape[1:], F32), compiler_params=_params())(parts)


def _adamw_packed(g, w, m, v, *, name):
    def body(g_ref, w_ref, m_ref, v_ref, d_ref, nm_ref, nv_ref):
        delta, nm, nv = _adam_update(g_ref[...], w_ref[...], m_ref[...], v_ref[...])
        d_ref[...] = delta
        nm_ref[...] = nm
        nv_ref[...] = nv

    return pl.pallas_call(body, name=name, out_shape=[jax.ShapeDtypeStruct(g.shape, F32)] * 3, compiler_params=_params())(g, w, m, v)


SUBLANES = 8


WIDE_PACK = 1024


def _pack(arrs, width=LANES):
    rows = []
    for a in arrs:
        a2 = a.reshape(-1, a.shape[-1])
        a2 = jnp.pad(a2, ((0, 0), (0, (-a2.shape[1]) % width)))
        rows += [a2[:, i * width:(i + 1) * width] for i in range(a2.shape[1] // width)]
    out = jnp.concatenate(rows, axis=0)
    return jnp.pad(out, ((0, (-out.shape[0]) % SUBLANES), (0, 0)))


def _unpack(packed, shapes, width=LANES):
    outs, r = [], 0
    for shp in shapes:
        lead, cols = math.prod(shp[:-1]), shp[-1]
        n_blocks = -(-cols // width)
        blocks = [packed[r + i * lead:r + (i + 1) * lead] for i in range(n_blocks)]
        outs.append(jnp.concatenate(blocks, axis=1)[:, :cols].reshape(shp))
        r += n_blocks * lead
    return outs


WEIGHT_NAMES = ("norm_w", "ffn_w_gate", "ffn_w_up", "ffn_w_down", "ssm_w_in", "ssm_conv_w", "ssm_conv_b", "ssm_dt_bias",
                "ssm_a_log", "ssm_d", "ssm_norm_w", "ssm_w_out", "kv_norm_w", "w_k", "b_k", "w_v", "b_v", "attn_w_q",
                "attn_b_q", "attn_sinks", "attn_w_o", "attn_b_o", "final_norm_w")
MATRIX_NAMES = ("ffn_w_gate", "ffn_w_up", "ffn_w_down", "ssm_w_in", "ssm_w_out", "w_k", "w_v", "attn_w_q", "attn_w_o")
VECTOR_NAMES = tuple(n for n in WEIGHT_NAMES if n not in MATRIX_NAMES)
SHARDED_VECTORS = ("norm_w", "ssm_conv_w", "ssm_conv_b", "ssm_norm_w")


GATHER_PLAN = {
    "gather_stage0": ("gate0", "up0", "down0", "vec"),
    "ffn_fwd0": ("w_in",),
    "ssm_in_proj": ("w_out", "gate1"),
    "ssm_conv_fwd": ("w_k", "w_v", "up1"),
    "ssd_fwd": ("down1", "gate2", "up2"),
    "ssm_out_proj": ("w_q", "w_o"),
    "ffn_fwd1": ("down2", "gate3"),
    "ffn_fwd2": ("up3",),
    "attn_fwd": ("down3",),
}
PAIR_PLAN = {
    "attn_bwd": ("gate3", "up3", "down3"),
    "ffn_bwd2": ("w_q", "w_o"),
    "ffn_bwd1": ("gate2", "up2", "down2", "w_k", "w_v"),
    "ssd_bwd": ("gate1", "up1", "down1", "w_out"),
    "ssm_norm_bwd": ("w_in",),
    "ffn_norm_bwd0": ("gate0", "up0", "down0"),
}
CHIP_PLAN = {
    "ffn_bwd2": ("gate3", "up3", "down3"),
    "ssd_bwd": ("gate2", "up2", "down2", "w_q", "w_o", "w_k", "w_v"),
    "ssm_conv_bwd": ("gate1", "up1"),
    "ssm_in_proj_dx": ("w_out",),
    "ffn_bwd0": ("down1", "w_in"),
    "adamw_gate": ("gate0",),
    "adamw_up": ("up0",),
    "adamw_down": ("down0",),
}
FFN_PARAMS = {"gate": "ffn_w_gate", "up": "ffn_w_up", "down": "ffn_w_down"}
SINGLE_MATRICES = {"w_in": "ssm_w_in", "w_out": "ssm_w_out", "w_k": "w_k", "w_v": "w_v", "w_q": "attn_w_q", "w_o": "attn_w_o"}


TRANSPOSED = ("ffn_w_gate", "ffn_w_up", "ssm_w_in")


def _matrix_view(name, a):
    if name in TRANSPOSED:
        a = jnp.swapaxes(a, -1, -2)
    return a.reshape((-1,) + a.shape[-2:])


def _from_matrix_view(name, a, shape):
    if name in TRANSPOSED:
        return jnp.swapaxes(a.reshape(shape[:-2] + (shape[-1], shape[-2])), -1, -2)
    return a.reshape(shape)


class _MeshNet:
    def __init__(self, p):
        self.p = p
        self.views = {n: _matrix_view(n, p[n]) for n in MATRIX_NAMES}
        self.local = {"vec": _pack([p[n] for n in SHARDED_VECTORS])}
        for short, n in FFN_PARAMS.items():
            cast = _cast_bf16(self.views[n], name=f"cast_{short}")
            self.local.update({f"{short}{k}": (cast, k) for k in range(N_FFN)})
        for short, n in SINGLE_MATRICES.items():
            self.local[short] = (_cast_bf16(self.views[n], name=f"cast_{short}"), 0)
        self.gathered_at, self.pairs_at, self.parts_at, self.grads, self.cache = {}, {}, {}, {}, {}

    def carry(self, name):
        comms = []
        if name in GATHER_PLAN:
            keys, comm = GATHER_PLAN[name], _Gather([self.local[k] for k in GATHER_PLAN[name]])
            self.gathered_at.update({k: (comm, i) for i, k in enumerate(keys)})
            comms.append(comm)
        if name in CHIP_PLAN:
            sums = []
            for k in CHIP_PLAN[name]:
                comm, i = self.pairs_at[k]
                sums.append(_pair_add(self.grads[k], comm.results[i], name=f"pair_add_{k}"))
            comm = _ChipExchange(sums)
            self.parts_at.update({k: (comm, i) for i, k in enumerate(CHIP_PLAN[name])})
            comms.append(comm)
        if name in PAIR_PLAN:
            keys, comm = PAIR_PLAN[name], _PairSwap([self.grads[k] for k in PAIR_PLAN[name]])
            self.pairs_at.update({k: (comm, i) for i, k in enumerate(keys)})
            comms.append(comm)
        return comms

    def run(self, name):
        for comm in self.carry(name):
            _run_exchange(comm, name=name)

    def give(self, key, grad):
        self.grads[key] = grad

    def parts(self, key):
        comm, i = self.parts_at[key]
        return comm.results[i]

    def _gathered(self, key):
        comm, i = self.gathered_at[key]
        return comm.results[i]

    def _vec(self, r0, lead, n_blocks):
        vecs = self._gathered("vec")
        return jnp.concatenate([vecs[d, r0 + i * lead:r0 + (i + 1) * lead, :] for d in range(N_DEV) for i in range(n_blocks)], axis=1)

    def _derive(self, name):
        p = self.p
        if name[:-1] in FFN_PARAMS:
            return self._gathered(name)
        if name == "w_in_t":
            return self._gathered("w_in").reshape(N_DEV * IN_PROJ_SHARD, D_MODEL)
        by_rows = {"wout": "w_out", "wk": "w_k", "wv": "w_v", "wq": "w_q", "wo": "w_o"}
        if name in by_rows:
            g = self._gathered(by_rows[name])
            return g.reshape(N_DEV * g.shape[1], g.shape[2])
        vectors = {"norm_w": lambda: self._vec(0, 6, 1).reshape(2, 3, D_MODEL), "conv_w": lambda: self._vec(6, CONV_WIDTH, 3),
                   "conv_b": lambda: self._vec(18, 1, 3), "ssm_norm_w": lambda: self._vec(21, 1, 2)}
        if name in vectors:
            return vectors[name]()
        replicated = {"dt_bias": p["ssm_dt_bias"], "a_log": p["ssm_a_log"], "d_skip": p["ssm_d"], "kv_norm_w": p["kv_norm_w"][None],
                      "b_k": p["b_k"][None], "b_v": p["b_v"][None], "b_q": p["attn_b_q"], "sinks": p["attn_sinks"],
                      "b_o": p["attn_b_o"], "final_norm_w": p["final_norm_w"][None]}
        return replicated[name]

    def w(self, name):
        if name not in self.cache:
            self.cache[name] = self._derive(name)
        return self.cache[name]


def _step(x, target, p, m, v):
    pos = _slot(_position())
    net = _MeshNet(p)
    net.run("gather_stage0")
    loss, grad_x, small = _forward_backward(x, target, net)

    grads, deltas, new_m, new_v = {}, {}, {}, {}
    view = lambda d, n: _matrix_view(n, d[n])
    vec_gather = _Gather([_pack([small[n] for n in VECTOR_NAMES], WIDE_PACK)])
    for short, n in SINGLE_MATRICES.items():
        outs = _adamw([net.parts(short)], net.views[n], view(m, n), view(v, n), 0, [], name=f"adamw_{short}",
                      comm=[vec_gather] if short == "w_in" else None)
        grads[n], deltas[n], new_m[n], new_v[n] = [_from_matrix_view(n, o, p[n].shape) for o in outs]
    ffn_outs = {}
    for short, n in FFN_PARAMS.items():
        ffn_outs[short] = _adamw([net.parts(f"{short}{k}") for k in range(1, N_FFN)], net.views[n], view(m, n), view(v, n), 1, [],
                                 name=f"adamw_{short}", comm=net.carry(f"adamw_{short}"))
    for short, n in FFN_PARAMS.items():
        outs = _adamw([net.parts(f"{short}0")], net.views[n], view(m, n), view(v, n), 0, ffn_outs[short], name=f"adamw_{short}0")
        grads[n], deltas[n], new_m[n], new_v[n] = [_from_matrix_view(n, o, p[n].shape) for o in outs]
    vec_sum = _sum_parts(vec_gather.results[0], name="sum_vector_grads")
    full_shapes = {"norm_w": (2, 3, D_MODEL), "ssm_conv_w": (1, CONV_WIDTH, CONV_DIM), "ssm_conv_b": (1, CONV_DIM),
                   "ssm_norm_w": (1, D_INNER)}
    vec_full = dict(zip(VECTOR_NAMES, _unpack(vec_sum, [full_shapes.get(n, p[n].shape) for n in VECTOR_NAMES], WIDE_PACK)))
    for n in VECTOR_NAMES:
        g = vec_full[n]
        if n in SHARDED_VECTORS:
            per = p[n].shape[-1]
            g = lax.dynamic_slice_in_dim(g, pos * per, per, axis=g.ndim - 1)
        grads[n] = g
    packed = _adamw_packed(*[_pack([d[n] for n in VECTOR_NAMES], WIDE_PACK) for d in (grads, p, m, v)], name="adamw_vectors")
    shapes = [p[n].shape for n in VECTOR_NAMES]
    for d, pk in zip((deltas, new_m, new_v), packed):
        d.update(zip(VECTOR_NAMES, _unpack(pk, shapes, WIDE_PACK)))
    return loss, grad_x, grads, deltas, new_m, new_v


def kernel(x, norm_w, ffn_w_gate, ffn_w_up, ffn_w_down, ssm_w_in, ssm_conv_w, ssm_conv_b, ssm_dt_bias, ssm_a_log, ssm_d, ssm_norm_w, ssm_w_out, kv_norm_w, w_k, b_k, w_v, b_v, attn_w_q, attn_b_q, attn_sinks, attn_w_o, attn_b_o, final_norm_w, loss_target, m_norm_w, m_ffn_w_gate, m_ffn_w_up, m_ffn_w_down, m_ssm_w_in, m_ssm_conv_w, m_ssm_conv_b, m_ssm_dt_bias, m_ssm_a_log, m_ssm_d, m_ssm_norm_w, m_ssm_w_out, m_kv_norm_w, m_w_k, m_b_k, m_w_v, m_b_v, m_attn_w_q, m_attn_b_q, m_attn_sinks, m_attn_w_o, m_attn_b_o, m_final_norm_w, v_norm_w, v_ffn_w_gate, v_ffn_w_up, v_ffn_w_down, v_ssm_w_in, v_ssm_conv_w, v_ssm_conv_b, v_ssm_dt_bias, v_ssm_a_log, v_ssm_d, v_ssm_norm_w, v_ssm_w_out, v_kv_norm_w, v_w_k, v_b_k, v_w_v, v_b_v, v_attn_w_q, v_attn_b_q, v_attn_sinks, v_attn_w_o, v_attn_b_o, v_final_norm_w):
    p = dict(zip(WEIGHT_NAMES, (norm_w, ffn_w_gate, ffn_w_up, ffn_w_down, ssm_w_in, ssm_conv_w, ssm_conv_b, ssm_dt_bias, ssm_a_log, ssm_d, ssm_norm_w, ssm_w_out, kv_norm_w, w_k, b_k, w_v, b_v, attn_w_q, attn_b_q, attn_sinks, attn_w_o, attn_b_o, final_norm_w)))
    m = dict(zip(WEIGHT_NAMES, (m_norm_w, m_ffn_w_gate, m_ffn_w_up, m_ffn_w_down, m_ssm_w_in, m_ssm_conv_w, m_ssm_conv_b, m_ssm_dt_bias, m_ssm_a_log, m_ssm_d, m_ssm_norm_w, m_ssm_w_out, m_kv_norm_w, m_w_k, m_b_k, m_w_v, m_b_v, m_attn_w_q, m_attn_b_q, m_attn_sinks, m_attn_w_o, m_attn_b_o, m_final_norm_w)))
    v = dict(zip(WEIGHT_NAMES, (v_norm_w, v_ffn_w_gate, v_ffn_w_up, v_ffn_w_down, v_ssm_w_in, v_ssm_conv_w, v_ssm_conv_b, v_ssm_dt_bias, v_ssm_a_log, v_ssm_d, v_ssm_norm_w, v_ssm_w_out, v_kv_norm_w, v_w_k, v_b_k, v_w_v, v_b_v, v_attn_w_q, v_attn_b_q, v_attn_sinks, v_attn_w_o, v_attn_b_o, v_final_norm_w)))
    loss, grad_x, grads, deltas, new_m, new_v = _step(x[0], loss_target[0], p, m, v)
    loss = lax.psum(loss[0, 0], ("x", "y", "c"))
    return (loss, grad_x[None], *[grads[n] for n in WEIGHT_NAMES], *[deltas[n] for n in WEIGHT_NAMES],
            *[new_m[n] for n in WEIGHT_NAMES], *[new_v[n] for n in WEIGHT_NAMES])
```

```python
import functools
import math

import jax
import jax.numpy as jnp
from jax import lax
from jax.experimental import pallas as pl
from jax.experimental.pallas import tpu as pltpu

F32 = jnp.float32
BF16 = jnp.bfloat16

N_DEV = 8
SEQ = 2048
D_MODEL = 1024
D_FF_SHARD = 352
N_FFN = 4
D_INNER = 2048
SSM_HEADS = 32
SSM_HEAD_DIM = 64
SSM_GROUPS = 4
HEADS_PER_GROUP = 8
SSM_STATE = 128
CHUNK = 128
N_CHUNKS = SEQ // CHUNK
GN = SSM_GROUPS * SSM_STATE
CONV_DIM = D_INNER + 2 * GN
CONV_WIDTH = 4
ZX_DIM = D_INNER + CONV_DIM
IN_PROJ_SHARD = 644
ATT_HEAD_DIM = 64
N_Q_HEADS = 16
N_KV_HEADS = 4
Q_PER_KV = 4
KV_DIM = N_KV_HEADS * ATT_HEAD_DIM
WINDOW = 128
ROPE_THETA = 10000.0
EPS = 1e-5
FFN_RES_WEIGHT = 0.5
ATT_SCALE = 1.0 / math.sqrt(ATT_HEAD_DIM)
NEG_BIG = -1e30

ADAM_LR = 0.001
ADAM_B1 = 0.9
ADAM_B2 = 0.999
ADAM_EPS = 1e-08
ADAM_WD = 0.01
ADAM_STEP = 10

VMEM_LIMIT_BYTES = 56 * 1024 * 1024
FFN_BWD_VMEM_LIMIT_BYTES = 61 * 1024 * 1024

NN = (((1,), (0,)), ((), ()))
NT = (((1,), (1,)), ((), ()))
TN = (((0,), (0,)), ((), ()))
_DIMS = {"nn": NN, "nt": NT, "tn": TN}


def _params(*sem):
    return pltpu.CompilerParams(dimension_semantics=sem if sem else None, vmem_limit_bytes=VMEM_LIMIT_BYTES)


def _dot(a, b, dims=NN):
    return lax.dot_general(a.astype(BF16), b.astype(BF16), dims, preferred_element_type=F32)


def _sigmoid(x):
    return 1.0 / (1.0 + jnp.exp(-x))


def _dsilu(x, s):
    return s * (1.0 + x * (1.0 - s))


def _rms(x):
    r = lax.rsqrt(jnp.mean(x * x, axis=-1, keepdims=True) + EPS)
    return x * r, r


def _sum_all(x):
    return jnp.sum(jnp.sum(x, axis=1, keepdims=True), axis=0, keepdims=True)


MESH = pl.DeviceIdType.MESH
N_PEERS = N_DEV - 1
N_CHIPS = N_DEV // 2


def _position():
    return lax.axis_index("x"), lax.axis_index("y"), lax.axis_index("c")


def _slot(p):
    return 4 * p[0] + 2 * p[1] + p[2]


class _Exchange:
    def __init__(self, arrays, out_shapes):
        n = len(arrays)
        self.arrays = list(arrays)
        self.out_shapes = out_shapes
        self.scratch = [pltpu.SemaphoreType.DMA((n, N_PEERS)), pltpu.SemaphoreType.DMA((n, N_PEERS)), pltpu.SemaphoreType.DMA((n,))]
        self.results = None

    def relay(self, ins, outs, sems):
        pass


class _Gather(_Exchange):
    def __init__(self, pieces):
        pieces = [p if isinstance(p, tuple) else (p, None) for p in pieces]
        self.blocks = [k for _, k in pieces]
        shapes = [a.shape if k is None else a.shape[1:] for a, k in pieces]
        super().__init__([a for a, _ in pieces], [jax.ShapeDtypeStruct((N_DEV,) + s, a.dtype) for s, (a, _) in zip(shapes, pieces)])

    def _plan(self, ins, outs, sems):
        send_sems, recv_sems, local_sems = sems
        x, y, c = _position()
        me, sibling = (x, y, c), (x, y, 1 - c)
        chips = [(1 - x, y), (x, 1 - y), (1 - x, 1 - y)]
        n = len(ins)
        ins = [r if k is None else r.at[k] for r, k in zip(ins, self.blocks)]

        def copy(a, k, block, to, src=None):
            dst = outs[a].at[_slot(block)]
            return pltpu.make_async_remote_copy(src_ref=dst if src is None else src, dst_ref=dst, send_sem=send_sems.at[a, k],
                                                recv_sem=recv_sems.at[a, k], device_id=to, device_id_type=MESH)

        mine = [pltpu.make_async_copy(ins[a], outs[a].at[_slot(me)], local_sems.at[a]) for a in range(n)]
        first = []
        for a in range(n):
            first.append(copy(a, 0, me, sibling, src=ins[a]))
            first += [copy(a, 1 + j, me, (*chip, c), src=ins[a]) for j, chip in enumerate(chips)]
        return n, c, me, sibling, chips, copy, mine, first

    def start(self, ins, outs, sems):
        _, _, _, _, _, _, mine, first = self._plan(ins, outs, sems)
        for cp in mine + first:
            cp.start()

    def relay(self, ins, outs, sems):
        n, c, me, sibling, chips, copy, _, _ = self._plan(ins, outs, sems)
        for j, chip in enumerate(chips):
            for a in range(n):
                copy(a, 1 + j, (*chip, c), me).wait_recv()
                copy(a, 4 + j, (*chip, c), sibling).start()

    def finish(self, ins, outs, sems):
        n, c, me, sibling, chips, copy, mine, first = self._plan(ins, outs, sems)
        passed = [copy(a, 4 + j, (*chip, c), sibling) for j, chip in enumerate(chips) for a in range(n)]
        for a in range(n):
            copy(a, 0, sibling, me).wait_recv()
            for j, chip in enumerate(chips):
                copy(a, 4 + j, (*chip, 1 - c), me).wait_recv()
        for cp in first + passed:
            cp.wait_send()
        for cp in mine:
            cp.wait()


class _PairSwap(_Exchange):
    def __init__(self, arrays):
        n = len(arrays)
        self.arrays = list(arrays)
        self.out_shapes = [jax.ShapeDtypeStruct((N_CHIPS,) + a.shape[1:], a.dtype) for a in arrays]
        self.scratch = [pltpu.SemaphoreType.DMA((n, N_CHIPS)), pltpu.SemaphoreType.DMA((n, N_CHIPS))]
        self.results = None

    def _plan(self, ins, outs, sems):
        send_sems, recv_sems = sems
        x, y, c = _position()
        return [pltpu.make_async_remote_copy(src_ref=ins[a].at[2 * q + 1 - c], dst_ref=outs[a].at[q], send_sem=send_sems.at[a, q],
                                             recv_sem=recv_sems.at[a, q], device_id=(x, y, 1 - c), device_id_type=MESH)
                for a in range(len(ins)) for q in range(N_CHIPS)]

    def start(self, ins, outs, sems):
        for cp in self._plan(ins, outs, sems):
            cp.start()

    def finish(self, ins, outs, sems):
        for cp in self._plan(ins, outs, sems):
            cp.wait()


class _ChipExchange(_Exchange):
    def __init__(self, arrays):
        n = len(arrays)
        self.arrays = list(arrays)
        self.out_shapes = [jax.ShapeDtypeStruct(a.shape, a.dtype) for a in arrays]
        self.scratch = [pltpu.SemaphoreType.DMA((n, 3)), pltpu.SemaphoreType.DMA((n, 3)), pltpu.SemaphoreType.DMA((n,))]
        self.results = None

    def _plan(self, ins, outs, sems):
        send_sems, recv_sems, local_sems = sems
        x, y, c = _position()
        here = 2 * x + y
        chips = [(1 - x, y), (x, 1 - y), (1 - x, 1 - y)]
        n = len(ins)

        def copy(a, k, src_slot, dst_slot):
            return pltpu.make_async_remote_copy(src_ref=ins[a].at[src_slot], dst_ref=outs[a].at[dst_slot], send_sem=send_sems.at[a, k],
                                                recv_sem=recv_sems.at[a, k], device_id=(*chips[k], c), device_id_type=MESH)

        there = [2 * qx + qy for qx, qy in chips]
        mine = [pltpu.make_async_copy(ins[a].at[here], outs[a].at[here], local_sems.at[a]) for a in range(n)]
        sends = [copy(a, k, there[k], here) for a in range(n) for k in range(3)]
        arrivals = lambda: [copy(a, k, here, there[k]) for a in range(n) for k in range(3)]
        return mine, sends, arrivals

    def start(self, ins, outs, sems):
        mine, sends, _ = self._plan(ins, outs, sems)
        for cp in mine + sends:
            cp.start()

    def finish(self, ins, outs, sems):
        mine, sends, arrivals = self._plan(ins, outs, sems)
        for cp in arrivals():
            cp.wait_recv()
        for cp in sends:
            cp.wait_send()
        for cp in mine:
            cp.wait()


def _call(body, *, name, grid, in_specs, out_specs, out_shape, args, scratch_shapes=(), sem=(), comm=(), aliases=None,
          vmem_limit=VMEM_LIMIT_BYTES):
    single = not isinstance(out_shape, (list, tuple))
    out_shape = [out_shape] if single else list(out_shape)
    out_specs = [out_specs] if single else list(out_specs)
    comms = list(comm or ())
    n_in, n_out, n_scr = len(args), len(out_shape), len(scratch_shapes)
    params = pltpu.CompilerParams(dimension_semantics=tuple(sem) if sem else None, vmem_limit_bytes=vmem_limit)
    if not comms:
        res = pl.pallas_call(body, name=name, grid=grid, in_specs=list(in_specs), out_specs=out_specs, out_shape=out_shape,
                             scratch_shapes=list(scratch_shapes), input_output_aliases=aliases or {}, compiler_params=params)(*args)
        return res[0] if single else res
    counts = [n_in] + [len(c.arrays) for c in comms] + [n_out] + [len(c.out_shapes) for c in comms] + [n_scr] + [len(c.scratch) for c in comms]
    nc = len(comms)

    def carried(*refs):
        pos, groups = 0, []
        for cnt in counts:
            groups.append(refs[pos:pos + cnt])
            pos += cnt
        ins, c_ins = groups[0], groups[1:1 + nc]
        outs, c_outs = groups[1 + nc], groups[2 + nc:2 + 2 * nc]
        scr, c_sems = groups[2 + 2 * nc], groups[3 + 2 * nc:]
        ids = [pl.program_id(d) for d in range(len(grid))]
        is_first = functools.reduce(jnp.logical_and, [i == 0 for i in ids])
        is_last = functools.reduce(jnp.logical_and, [i == g - 1 for i, g in zip(ids, grid)])

        @pl.when(is_first)
        def _():
            for q, c in enumerate(comms):
                c.start(c_ins[q], c_outs[q], c_sems[q])

        body(*ins, *outs, *scr)

        @pl.when(is_last)
        def _():
            for q, c in enumerate(comms):
                c.relay(c_ins[q], c_outs[q], c_sems[q])
                c.finish(c_ins[q], c_outs[q], c_sems[q])

    anyspec = pl.BlockSpec(memory_space=pl.ANY)
    c_arrays = [a for c in comms for a in c.arrays]
    c_shapes = [s for c in comms for s in c.out_shapes]
    res = pl.pallas_call(
        carried, name=name, grid=grid, in_specs=list(in_specs) + [anyspec] * len(c_arrays), out_specs=out_specs + [anyspec] * len(c_shapes),
        out_shape=out_shape + c_shapes, scratch_shapes=list(scratch_shapes) + [s for c in comms for s in c.scratch],
        input_output_aliases=aliases or {}, compiler_params=params)(*args, *c_arrays)
    pos = n_out
    for c in comms:
        c.results = list(res[pos:pos + len(c.out_shapes)])
        pos += len(c.out_shapes)
    return res[0] if single else list(res[:n_out])


def _run_exchange(comm, *, name):
    def body(*refs):
        n_ci, n_co = len(comm.arrays), len(comm.out_shapes)
        ins, outs, sems = refs[:n_ci], refs[n_ci:n_ci + n_co], refs[n_ci + n_co:]
        comm.start(ins, outs, sems)
        comm.relay(ins, outs, sems)
        comm.finish(ins, outs, sems)

    anyspec = pl.BlockSpec(memory_space=pl.ANY)
    comm.results = list(pl.pallas_call(
        body, name=name, in_specs=[anyspec] * len(comm.arrays), out_specs=[anyspec] * len(comm.out_shapes),
        out_shape=list(comm.out_shapes), scratch_shapes=list(comm.scratch))(*comm.arrays))
    return comm.results


def _mm(a, b, *, dims="nn", bias=None, res=None, out_dtype=F32, name, tm=1024, tn=1024, tk=1024, comm=None, b_rows=None,
        out_window=None, into=None, colsum_b=False):
    if dims == "tn":
        k_dim, m_dim = a.shape
    else:
        m_dim, k_dim = a.shape
    row0, n_rows = b_rows if b_rows is not None else (0, b.shape[0])
    n_dim = n_rows if dims == "nt" else b.shape[1]
    assert dims == "nt" or n_rows == k_dim, (name, a.shape, b.shape, b_rows)
    tm, tn, tk = min(tm, m_dim), min(tn, n_dim), min(tk, k_dim)
    assert m_dim % tm == 0 and n_dim % tn == 0 and k_dim % tk == 0, (name, a.shape, b.shape)
    nk = k_dim // tk
    a_spec = pl.BlockSpec((tk, tm), lambda i, j, k: (k, i)) if dims == "tn" else pl.BlockSpec((tm, tk), lambda i, j, k: (i, k))
    if dims == "nt":
        assert row0 % tn == 0
        b_spec = pl.BlockSpec((tn, tk), lambda i, j, k: (row0 // tn + j, k))
    else:
        assert row0 % tk == 0
        b_spec = pl.BlockSpec((tk, tn), lambda i, j, k: (row0 // tk + k, j))
    in_specs, args = [a_spec, b_spec], [a, b]
    if bias is not None:
        in_specs.append(pl.BlockSpec((1, tn), lambda i, j, k: (0, j)))
        args.append(bias)
    if res is not None:
        in_specs.append(pl.BlockSpec((tm, tn), lambda i, j, k: (i, j)))
        args.append(res)
    dn = _DIMS[dims]

    if colsum_b:
        assert dims == "tn" and m_dim == tm and into is None and out_window is None

    def body(*refs):
        a_ref, b_ref = refs[0], refs[1]
        acc_ref = refs[-1]
        o_ref = refs[-3] if colsum_b else refs[-2]
        k = pl.program_id(2)

        @pl.when(k == 0)
        def _():
            acc_ref[...] = jnp.zeros_like(acc_ref)
            if colsum_b:
                refs[-2][...] = jnp.zeros_like(refs[-2])

        acc_ref[...] += _dot(a_ref[...], b_ref[...], dn)
        if colsum_b:
            refs[-2][...] += jnp.sum(b_ref[...].astype(F32), axis=0, keepdims=True)

        @pl.when(k == nk - 1)
        def _():
            r = acc_ref[...]
            pos = 2
            if bias is not None:
                r = r + refs[pos][...]
                pos += 1
            if res is not None:
                r = r + refs[pos][...]
            o_ref[...] = r.astype(out_dtype)

    out_row0, out_rows = out_window if out_window is not None else (0, m_dim)
    assert out_row0 % tm == 0
    aliases = None
    if into is not None:
        assert into.shape == (out_rows, n_dim) and into.dtype == out_dtype
        in_specs.append(pl.BlockSpec(memory_space=pl.ANY))
        args.append(into)
        aliases = {len(args) - 1: 0}
    out_spec = pl.BlockSpec((tm, tn), lambda i, j, k: (out_row0 // tm + i, j))
    out_shape = jax.ShapeDtypeStruct((out_rows, n_dim), out_dtype)
    if colsum_b:
        out_spec = [out_spec, pl.BlockSpec((1, tn), lambda i, j, k: (0, j))]
        out_shape = [out_shape, jax.ShapeDtypeStruct((1, n_dim), F32)]
    return _call(
        body, name=name, grid=(m_dim // tm, n_dim // tn, nk), in_specs=in_specs, out_specs=out_spec, out_shape=out_shape,
        aliases=aliases, scratch_shapes=[pltpu.VMEM((tm, tn), F32)], sem=("parallel", "parallel", "arbitrary"), args=args, comm=comm)


def _mm_norm_bwd(a, b, x, nw, res, *, dims="nn", b_rows=None, add=None, name, tm=1024, tk=1024, comm=None):
    m_dim, k_dim = a.shape
    row0, n_rows = b_rows if b_rows is not None else (0, b.shape[0])
    d_dim = x.shape[1]
    tm, tk = min(tm, m_dim), min(tk, k_dim)
    assert m_dim % tm == 0 and k_dim % tk == 0 and (n_rows if dims == "nt" else b.shape[1]) == d_dim, (name, a.shape, b.shape)
    nk = k_dim // tk
    if dims == "nt":
        assert row0 % d_dim == 0
        b_spec = pl.BlockSpec((d_dim, tk), lambda i, k: (row0 // d_dim, k))
    else:
        assert row0 % tk == 0 and n_rows == k_dim
        b_spec = pl.BlockSpec((tk, d_dim), lambda i, k: (row0 // tk + k, 0))
    row = pl.BlockSpec((tm, d_dim), lambda i, k: (i, 0))
    vec = pl.BlockSpec((1, d_dim), lambda i, k: (0, 0))
    extra = ([add] if add is not None else []) + list(res)
    dn = _DIMS[dims]

    def body(*refs):
        a_ref, b_ref, x_ref, nw_ref = refs[:4]
        extra_refs = refs[4:4 + len(extra)]
        dx_ref, dnw_ref, dob_ref, acc_ref = refs[-4:]
        i, k = pl.program_id(0), pl.program_id(1)

        @pl.when(k == 0)
        def _():
            acc_ref[...] = jnp.zeros_like(acc_ref)

        @pl.when((i == 0) & (k == 0))
        def _():
            dnw_ref[...] = jnp.zeros_like(dnw_ref)

        acc_ref[...] += _dot(a_ref[...], b_ref[...], dn)

        @pl.when(k == nk - 1)
        def _():
            dh = acc_ref[...]
            rest = list(extra_refs)
            if add is not None:
                dh = dh + rest.pop(0)[...]
            xhat, r = _rms(x_ref[...])
            dxhat = dh * nw_ref[...]
            dx = r * (dxhat - xhat * jnp.mean(dxhat * xhat, axis=-1, keepdims=True))
            for rr in rest:
                dx = dx + rr[...]
            dx_ref[...] = dx
            dob_ref[...] = (FFN_RES_WEIGHT * dx).astype(BF16)
            dnw_ref[...] += jnp.sum(dh * xhat, axis=0, keepdims=True)

    return _call(
        body, name=name, grid=(m_dim // tm, nk),
        in_specs=[pl.BlockSpec((tm, tk), lambda i, k: (i, k)), b_spec, row, vec] + [row] * len(extra), out_specs=[row, vec, row],
        out_shape=[jax.ShapeDtypeStruct((m_dim, d_dim), F32), jax.ShapeDtypeStruct((1, d_dim), F32),
                   jax.ShapeDtypeStruct((m_dim, d_dim), BF16)],
        scratch_shapes=[pltpu.VMEM((tm, d_dim), F32)], sem=("arbitrary", "arbitrary"), args=[a, b, x, nw] + extra, comm=comm)


def _rope_rotate(x, cos_t, sin_t):
    rows, width = x.shape
    half = ATT_HEAD_DIM // 2
    lane = lax.broadcasted_iota(jnp.int32, (rows, width), 1)
    first = (lane % ATT_HEAD_DIM) < half
    rot = jnp.where(first, -pltpu.roll(x, width - half, 1), pltpu.roll(x, half, 1))
    reps = width // 128
    return x * jnp.tile(cos_t, (1, reps)) + rot * jnp.tile(sin_t, (1, reps))


def _norm_mm(x, nw, w, bias, *, name, tm=1024, tn=1024, comm=None, w_rows=None, rope=None):
    t_dim, d_dim = x.shape
    transposed = w_rows is not None
    n_dim = w_rows if transposed else w.shape[1]
    tn = min(tn, n_dim)
    assert t_dim % tm == 0 and n_dim % tn == 0
    has_bias = bias is not None
    w_spec = pl.BlockSpec((tn, d_dim), lambda i, j: (j, 0)) if transposed else pl.BlockSpec((d_dim, tn), lambda i, j: (0, j))
    dn = NT if transposed else NN
    in_specs = [pl.BlockSpec((tm, d_dim), lambda i, j: (i, 0)), pl.BlockSpec((1, d_dim), lambda i, j: (0, 0)), w_spec]
    args = [x, nw, w]
    if has_bias:
        in_specs.append(pl.BlockSpec((1, tn), lambda i, j: (0, j)))
        args.append(bias)
    if rope is not None:
        in_specs += [pl.BlockSpec((tm, LANES), lambda i, j: (i, 0))] * 2
        args += list(rope)

    def body(*refs):
        x_ref, nw_ref, w_ref = refs[:3]
        o_ref, h_ref = refs[-2], refs[-1]

        @pl.when(pl.program_id(1) == 0)
        def _():
            xhat, _ = _rms(x_ref[...])
            h_ref[...] = (xhat * nw_ref[...]).astype(BF16)

        r = _dot(h_ref[...], w_ref[...], dn)
        if has_bias:
            r = r + refs[3][...]
        if rope is not None:
            r = _rope_rotate(r, refs[-4][...], refs[-3][...])
        o_ref[...] = r

    return _call(
        body, name=name, grid=(t_dim // tm, n_dim // tn), in_specs=in_specs,
        out_specs=[pl.BlockSpec((tm, tn), lambda i, j: (i, j)), pl.BlockSpec((tm, d_dim), lambda i, j: (i, 0))],
        out_shape=[jax.ShapeDtypeStruct((t_dim, n_dim), F32), jax.ShapeDtypeStruct((t_dim, d_dim), BF16)],
        sem=("parallel", "arbitrary"), args=args, comm=comm)


def _norm_bwd(x, nw, dh, res, *, name, tm=512, comm=None):
    t_dim, d_dim = x.shape
    n_res = len(res)
    row = pl.BlockSpec((tm, d_dim), lambda i: (i, 0))
    vec = pl.BlockSpec((1, d_dim), lambda i: (0, 0))

    def body(*refs):
        x_ref, nw_ref, dh_ref = refs[:3]
        dx_ref, dnw_ref = refs[-2], refs[-1]
        xhat, r = _rms(x_ref[...])
        dh = dh_ref[...]
        dxhat = dh * nw_ref[...]
        dx = r * (dxhat - xhat * jnp.mean(dxhat * xhat, axis=-1, keepdims=True))
        for rr in refs[3:3 + n_res]:
            dx = dx + rr[...]
        dx_ref[...] = dx

        @pl.when(pl.program_id(0) == 0)
        def _():
            dnw_ref[...] = jnp.zeros_like(dnw_ref)

        dnw_ref[...] += jnp.sum(dh * xhat, axis=0, keepdims=True)

    return _call(
        body, name=name, grid=(t_dim // tm,), in_specs=[row, vec, row] + [row] * n_res,
        out_specs=[row, vec],
        out_shape=[jax.ShapeDtypeStruct((t_dim, d_dim), F32), jax.ShapeDtypeStruct((1, d_dim), F32)],
        sem=("arbitrary",), args=[x, nw, dh, *res], comm=comm)


FFN_ROW_TILE = 512
FFN_SHARDS_PER_STEP = 2
FFN_STEPS = N_DEV // FFN_SHARDS_PER_STEP
FFN_STEP_COLS = FFN_SHARDS_PER_STEP * D_FF_SHARD


def _ffn_step_view(w):
    return w.reshape(FFN_STEPS, FFN_STEP_COLS, w.shape[-1])


def _ffn_specs(t_dim, d_dim):
    full = pl.BlockSpec((t_dim, d_dim), lambda j: (0, 0))
    wspec = pl.BlockSpec((None, FFN_STEP_COLS, d_dim), lambda j: (j, 0, 0))
    pre = pl.BlockSpec((None, t_dim, FFN_STEP_COLS), lambda j: (j, 0, 0))
    return full, wspec, pre


def _ffn_fwd(x, nw, wg, wu, wd, *, name, comm=None):
    t_dim, d_dim = x.shape
    n_tiles = t_dim // FFN_ROW_TILE

    def body(x_ref, nw_ref, wg_ref, wu_ref, wd_ref, o_ref, g_ref, u_ref, h_ref):
        j = pl.program_id(0)

        @pl.when(j == 0)
        def _():
            xhat, _ = _rms(x_ref[...])
            h_ref[...] = (xhat * nw_ref[...]).astype(BF16)
            o_ref[...] = jnp.zeros_like(o_ref)

        for t in range(n_tiles):
            rows = pl.ds(t * FFN_ROW_TILE, FFN_ROW_TILE)
            h = h_ref[rows, :]
            g = _dot(h, wg_ref[...], NT)
            u = _dot(h, wu_ref[...], NT)
            g_ref[rows, :] = g.astype(BF16)
            u_ref[rows, :] = u.astype(BF16)
            o_ref[rows, :] += _dot(g * _sigmoid(g) * u, wd_ref[...])

        @pl.when(j == FFN_STEPS - 1)
        def _():
            o_ref[...] = x_ref[...] + FFN_RES_WEIGHT * o_ref[...]

    full, wspec, pre = _ffn_specs(t_dim, d_dim)
    pre_shape = jax.ShapeDtypeStruct((FFN_STEPS, t_dim, FFN_STEP_COLS), BF16)
    return _call(
        body, name=name, grid=(FFN_STEPS,),
        in_specs=[full, pl.BlockSpec((1, d_dim), lambda j: (0, 0)), wspec, wspec, wspec],
        out_specs=[full, pre, pre, full],
        out_shape=[jax.ShapeDtypeStruct((t_dim, d_dim), F32), pre_shape, pre_shape, jax.ShapeDtypeStruct((t_dim, d_dim), BF16)],
        sem=("arbitrary",), args=[x, nw, _ffn_step_view(wg), _ffn_step_view(wu), _ffn_step_view(wd)], comm=comm)


def _ffn_bwd(h, dob, pre_g, pre_u, wg, wu, wd, *, name, comm=None):
    t_dim, d_dim = h.shape
    n_tiles = t_dim // FFN_ROW_TILE

    def body(h_ref, dob_ref, g_ref, u_ref, wg_ref, wu_ref, wd_ref, dh_ref, gg_ref, gu_ref, gd_ref, dwg_scr, dwu_scr, dwd_scr):
        @pl.when(pl.program_id(0) == 0)
        def _():
            dh_ref[...] = jnp.zeros_like(dh_ref)

        for t in range(n_tiles):
            rows = pl.ds(t * FFN_ROW_TILE, FFN_ROW_TILE)
            hh = h_ref[rows, :]
            do = dob_ref[rows, :]
            g = g_ref[rows, :].astype(F32)
            u = u_ref[rows, :].astype(F32)
            sg = _sigmoid(g)
            s = g * sg
            da = _dot(do, wd_ref[...], NT)
            dwd = _dot(s * u, do, TN)
            du = (da * s).astype(BF16)
            dg = (da * u * _dsilu(g, sg)).astype(BF16)
            dwg = _dot(dg, hh, TN)
            dwu = _dot(du, hh, TN)
            if t == 0:
                dwd_scr[...] = dwd
                dwg_scr[...] = dwg
                dwu_scr[...] = dwu
            else:
                dwd_scr[...] += dwd
                dwg_scr[...] += dwg
                dwu_scr[...] += dwu
            dh_ref[rows, :] += _dot(dg, wg_ref[...]) + _dot(du, wu_ref[...])
        gg_ref[...] = dwg_scr[...].astype(BF16)
        gu_ref[...] = dwu_scr[...].astype(BF16)
        gd_ref[...] = dwd_scr[...].astype(BF16)

    full, wspec, pre = _ffn_specs(t_dim, d_dim)
    gspec = pl.BlockSpec((None, FFN_STEP_COLS, d_dim), lambda j: (j, 0, 0), pipeline_mode=pl.Buffered(1))
    grad_shape = jax.ShapeDtypeStruct((FFN_STEPS, FFN_STEP_COLS, d_dim), BF16)
    dh, gg, gu, gd = _call(
        body, name=name, grid=(FFN_STEPS,),
        in_specs=[full, full, pre, pre, wspec, wspec, wspec], out_specs=[full, gspec, gspec, gspec],
        out_shape=[jax.ShapeDtypeStruct((t_dim, d_dim), F32)] + [grad_shape] * 3,
        scratch_shapes=[pltpu.VMEM((FFN_STEP_COLS, d_dim), F32)] * 3, sem=("arbitrary",), vmem_limit=FFN_BWD_VMEM_LIMIT_BYTES,
        args=[h, dob, pre_g, pre_u, _ffn_step_view(wg), _ffn_step_view(wu), _ffn_step_view(wd)], comm=comm)
    return dh, gg.reshape(wg.shape), gu.reshape(wu.shape), gd.reshape(wd.shape)


CONV_COLS = 256


def _shift_down(u, s, rows):
    return jnp.where(rows >= s, pltpu.roll(u, s, 0), 0.0)


def _shift_up(u, s, rows, t_dim):
    return jnp.where(rows < t_dim - s, pltpu.roll(u, t_dim - s, 0), 0.0)


def _conv_pre(u, w_ref, b_ref, rows):
    c = b_ref[...] + w_ref[CONV_WIDTH - 1:CONV_WIDTH, :] * u
    for k in range(CONV_WIDTH - 1):
        c = c + w_ref[k:k + 1, :] * _shift_down(u, CONV_WIDTH - 1 - k, rows)
    return c


def _conv_fwd(zx, cw, cb, *, name, comm=None):
    t_dim = zx.shape[0]
    off = D_INNER // CONV_COLS

    def body(u_ref, w_ref, b_ref, o_ref):
        rows = lax.broadcasted_iota(jnp.int32, (t_dim, CONV_COLS), 0)
        c = _conv_pre(u_ref[...], w_ref, b_ref, rows)
        o_ref[...] = c * _sigmoid(c)

    return _call(
        body, name=name, grid=(CONV_DIM // CONV_COLS,),
        in_specs=[pl.BlockSpec((t_dim, CONV_COLS), lambda j: (0, off + j)),
                  pl.BlockSpec((CONV_WIDTH, CONV_COLS), lambda j: (0, j)), pl.BlockSpec((1, CONV_COLS), lambda j: (0, j))],
        out_specs=pl.BlockSpec((t_dim, CONV_COLS), lambda j: (0, j)),
        out_shape=jax.ShapeDtypeStruct((t_dim, CONV_DIM), F32), sem=("parallel",), args=[zx, cw, cb], comm=comm)


def _conv_bwd(zx, cw, cb, dxs, db, dc, dzx, *, name, comm=None):
    t_dim = zx.shape[0]
    off = D_INNER // CONV_COLS
    n_xs = D_INNER // CONV_COLS
    n_b = GN // CONV_COLS

    def body(u_ref, w_ref, b_ref, dxs_ref, db_ref, dc_ref, dzx_in, dzx_ref, dw_ref, dbias_ref):
        j = pl.program_id(0)
        rows = lax.broadcasted_iota(jnp.int32, (t_dim, CONV_COLS), 0)
        u = u_ref[...]
        c = _conv_pre(u, w_ref, b_ref, rows)
        d = jnp.where(j < n_xs, dxs_ref[...], jnp.where(j < n_xs + n_b, db_ref[...], dc_ref[...]))
        dcv = d * _dsilu(c, _sigmoid(c))
        dpre = w_ref[CONV_WIDTH - 1:CONV_WIDTH, :] * dcv
        dw_ref[CONV_WIDTH - 1:CONV_WIDTH, :] = jnp.sum(dcv * u, axis=0, keepdims=True)
        for k in range(CONV_WIDTH - 1):
            s = CONV_WIDTH - 1 - k
            dpre = dpre + w_ref[k:k + 1, :] * _shift_up(dcv, s, rows, t_dim)
            dw_ref[k:k + 1, :] = jnp.sum(dcv * _shift_down(u, s, rows), axis=0, keepdims=True)
        dzx_ref[...] = dpre
        dbias_ref[...] = jnp.sum(dcv, axis=0, keepdims=True)

    blk = lambda n: pl.BlockSpec((t_dim, CONV_COLS), n)
    return _call(
        body, name=name, grid=(CONV_DIM // CONV_COLS,),
        in_specs=[blk(lambda j: (0, off + j)), pl.BlockSpec((CONV_WIDTH, CONV_COLS), lambda j: (0, j)),
                  pl.BlockSpec((1, CONV_COLS), lambda j: (0, j)),
                  blk(lambda j: (0, jnp.minimum(j, n_xs - 1))),
                  blk(lambda j: (0, jnp.clip(j - n_xs, 0, n_b - 1))),
                  blk(lambda j: (0, jnp.clip(j - n_xs - n_b, 0, n_b - 1))),
                  pl.BlockSpec(memory_space=pl.ANY)],
        out_specs=[blk(lambda j: (0, off + j)), pl.BlockSpec((CONV_WIDTH, CONV_COLS), lambda j: (0, j)),
                   pl.BlockSpec((1, CONV_COLS), lambda j: (0, j))],
        out_shape=[jax.ShapeDtypeStruct(dzx.shape, F32), jax.ShapeDtypeStruct((CONV_WIDTH, CONV_DIM), F32),
                   jax.ShapeDtypeStruct((1, CONV_DIM), F32)],
        aliases={6: 0}, sem=("parallel",), args=[zx, cw, cb, dxs, db, dc, dzx], comm=comm)


def _softplus_parts(x):
    e = jnp.exp(-jnp.abs(x))
    u = 1.0 + e
    log1p_e = jnp.where(u == 1.0, e, jnp.log(u) * e / jnp.where(u == 1.0, 1.0, u - 1.0))
    return jnp.maximum(x, 0.0) + log1p_e


def _dt_prep(dtr, dt_bias, a_log, *, name):
    def body(dtr_ref, bias_ref, alog_ref, dt_ref, a_ref):
        dt = _softplus_parts(dtr_ref[...] + bias_ref[...])
        dt_ref[...] = dt
        a_ref[...] = dt * (-jnp.exp(alog_ref[...]))

    return pl.pallas_call(body, name=name, out_shape=[jax.ShapeDtypeStruct(dtr.shape, F32)] * 2,
                          compiler_params=_params())(dtr, dt_bias, a_log)


def _dt_bwd(dtr, dt_bias, a_log, dt, ddt, da, *, name):
    def body(dtr_ref, bias_ref, alog_ref, dt_ref, ddt_ref, da_ref, ddtr_ref, dbias_ref, dalog_ref):
        a_neg = -jnp.exp(alog_ref[...])
        da_v = da_ref[...]
        ddt_tot = ddt_ref[...] + da_v * a_neg
        ddtr = ddt_tot * _sigmoid(dtr_ref[...] + bias_ref[...])
        ddtr_ref[...] = ddtr
        dbias_ref[...] = jnp.sum(ddtr, axis=0, keepdims=True)
        dalog_ref[...] = jnp.sum(da_v * dt_ref[...], axis=0, keepdims=True) * a_neg

    return pl.pallas_call(
        body, name=name,
        out_shape=[jax.ShapeDtypeStruct(dtr.shape, F32), jax.ShapeDtypeStruct((1, SSM_HEADS), F32),
                   jax.ShapeDtypeStruct((1, SSM_HEADS), F32)],
        compiler_params=_params())(dtr, dt_bias, a_log, dt, ddt, da)


GROUP_COLS = HEADS_PER_GROUP * SSM_HEAD_DIM
LANES = 128
HEADS_PER_LANE_BLOCK = LANES // SSM_HEAD_DIM


def _split3(x):
    hi = x.astype(BF16)
    r1 = x - hi.astype(F32)
    mid = r1.astype(BF16)
    lo = (r1 - mid.astype(F32)).astype(BF16)
    return hi, mid, lo


def _dot_select(a, b, dims=NN, data=0):
    out = None
    for part in _split3(a if data == 0 else b):
        lhs, rhs = (part, b.astype(BF16)) if data == 0 else (a.astype(BF16), part)
        t = lax.dot_general(lhs, rhs, dims, preferred_element_type=F32)
        out = t if out is None else out + t
    return out


def _group_sums(vals, expand):
    out = _dot_select(jnp.concatenate(vals, axis=0), expand, NT)
    return [out[i * CHUNK:(i + 1) * CHUNK] for i in range(len(vals))]


def _ssd_chunk_common(a_ref, dt_ref, b_ref, c_ref):
    row = lax.broadcasted_iota(jnp.int32, (CHUNK, CHUNK), 0)
    col = lax.broadcasted_iota(jnp.int32, (CHUNK, CHUNK), 1)
    causal = col <= row
    lower = causal.astype(F32)
    upper = (col >= row).astype(F32)
    head = lax.broadcasted_iota(jnp.int32, (HEADS_PER_GROUP, GROUP_COLS), 0)
    lane = lax.broadcasted_iota(jnp.int32, (HEADS_PER_GROUP, GROUP_COLS), 1)
    expand = ((lane >= head * SSM_HEAD_DIM) & (lane < (head + 1) * SSM_HEAD_DIM)).astype(F32)
    a = a_ref[...]
    cs = _dot_select(lower, a, data=1)
    cs_row = _dot_select(a, upper, TN)
    cs_x = _dot_select(cs, expand)
    dt_x = _dot_select(dt_ref[...], expand)
    e_out_x = jnp.exp(cs_x)
    e_st_x = jnp.exp(cs_x[CHUNK - 1:CHUNK, :] - cs_x)
    bc = b_ref[...]
    cc = c_ref[...]
    cb = _dot(cc, bc, NT)
    return causal, upper, expand.astype(BF16), cs, cs_row, dt_x, e_out_x, e_st_x, bc, cc, cb


def _head_decay(causal, cs, cs_row, h):
    return jnp.exp(jnp.where(causal, cs[:, h:h + 1] - cs_row[h:h + 1, :], NEG_BIG))


def _lane_block_head_masks():
    lane = lax.broadcasted_iota(jnp.int32, (CHUNK, LANES), 1)
    return [(lane >= i * SSM_HEAD_DIM) & (lane < (i + 1) * SSM_HEAD_DIM) for i in range(HEADS_PER_LANE_BLOCK)]


def _decay_state(dst_ref, old, new, cs):
    for h in range(HEADS_PER_GROUP):
        rows = slice(h * SSM_HEAD_DIM, (h + 1) * SSM_HEAD_DIM)
        dst_ref[rows, :] = jnp.exp(cs[CHUNK - 1:CHUNK, h:h + 1]) * old[rows, :] + new[rows, :]


def _ssd_fwd(xbc, dtg, ag, dgx, *, name, comm=None):
    t_dim = xbc.shape[0]

    def body(xs_ref, b_ref, c_ref, dt_ref, a_ref, d_ref, y_ref, st_ref, s_scr):
        @pl.when(pl.program_id(1) == 0)
        def _():
            s_scr[...] = jnp.zeros_like(s_scr)

        causal, _, _, cs, cs_row, dt_x, e_out_x, e_st_x, bc, cc, cb = _ssd_chunk_common(a_ref, dt_ref, b_ref, c_ref)
        masks = _lane_block_head_masks()
        xs = xs_ref[...]
        xdt_x = xs * dt_x
        prev = s_scr[...]
        st_ref[...] = prev
        y_off = e_out_x * _dot(cc, prev, NT) + xs * d_ref[...]
        for blk in range(GROUP_COLS // LANES):
            lanes = slice(blk * LANES, (blk + 1) * LANES)
            x_b = xdt_x[:, lanes].astype(BF16)
            acc = y_off[:, lanes]
            for i in range(HEADS_PER_LANE_BLOCK):
                m = cb * _head_decay(causal, cs, cs_row, blk * HEADS_PER_LANE_BLOCK + i)
                acc = acc + _dot(m, jnp.where(masks[i], x_b, jnp.zeros_like(x_b)))
            y_ref[:, lanes] = acc
        _decay_state(s_scr, prev, _dot(xdt_x * e_st_x, bc, TN), cs)

    xs = pl.BlockSpec((CHUNK, GROUP_COLS), lambda g, c: (c, g))
    bsp = pl.BlockSpec((CHUNK, SSM_STATE), lambda g, c: (c, D_INNER // SSM_STATE + g))
    csp = pl.BlockSpec((CHUNK, SSM_STATE), lambda g, c: (c, (D_INNER + GN) // SSM_STATE + g))
    per_head = pl.BlockSpec((None, CHUNK, HEADS_PER_GROUP), lambda g, c: (g, c, 0))
    dsk = pl.BlockSpec((None, 1, GROUP_COLS), lambda g, c: (g, 0, 0))
    return _call(
        body, name=name, grid=(SSM_GROUPS, N_CHUNKS),
        in_specs=[xs, bsp, csp, per_head, per_head, dsk],
        out_specs=[xs, pl.BlockSpec((None, GROUP_COLS, SSM_STATE), lambda g, c: (c, g, 0))],
        out_shape=[jax.ShapeDtypeStruct((t_dim, D_INNER), F32),
                   jax.ShapeDtypeStruct((N_CHUNKS, D_INNER, SSM_STATE), F32)],
        scratch_shapes=[pltpu.VMEM((GROUP_COLS, SSM_STATE), F32)],
        sem=("parallel", "arbitrary"), args=[xbc, xbc, xbc, dtg, ag, dgx], comm=comm)


def _ssd_bwd(xbc, dtg, ag, dgx, states, dy, *, name, comm=None):
    t_dim = xbc.shape[0]
    last = N_CHUNKS - 1

    def body(xs_ref, b_ref, c_ref, dt_ref, a_ref, d_ref, st_ref, dy_ref,
             dxs_ref, db_ref, dc_ref, ddt_ref, da_ref, dd_ref, ds_scr):
        @pl.when(pl.program_id(1) == 0)
        def _():
            ds_scr[...] = jnp.zeros_like(ds_scr)
            dd_ref[...] = jnp.zeros_like(dd_ref)

        causal, upper, expand, cs, cs_row, dt_x, e_out_x, e_st_x, bc, cc, cb = _ssd_chunk_common(a_ref, dt_ref, b_ref, c_ref)
        masks = _lane_block_head_masks()
        xs = xs_ref[...]
        dy_x = dy_ref[...]
        xdt_x = xs * dt_x
        prev = st_ref[...]
        d_s = ds_scr[...]
        g1_x = _dot(bc, d_s, NT)
        cp_x = _dot(cc, prev, NT)
        d_cb = jnp.zeros((CHUNK, CHUNK), F32)
        lane8 = lax.broadcasted_iota(jnp.int32, (CHUNK, HEADS_PER_GROUP), 1)
        sub8 = lax.broadcasted_iota(jnp.int32, (HEADS_PER_GROUP, CHUNK), 0)
        row_w = jnp.zeros((CHUNK, HEADS_PER_GROUP), F32)
        col_w = jnp.zeros((HEADS_PER_GROUP, CHUNK), F32)
        dxdt_blocks = []
        for blk in range(GROUP_COLS // LANES):
            lanes = slice(blk * LANES, (blk + 1) * LANES)
            dy_b = dy_x[:, lanes].astype(BF16)
            x_b = xdt_x[:, lanes].astype(BF16)
            acc_dx = jnp.zeros((CHUNK, LANES), F32)
            for i in range(HEADS_PER_LANE_BLOCK):
                h = blk * HEADS_PER_LANE_BLOCK + i
                decay = _head_decay(causal, cs, cs_row, h)
                m = cb * decay
                dy_h = jnp.where(masks[i], dy_b, jnp.zeros_like(dy_b))
                acc_dx = acc_dx + _dot(m, dy_h, TN)
                d_m = _dot(dy_h, x_b, NT)
                d_cb = d_cb + d_m * decay
                w = d_m * m
                row_w = jnp.where(lane8 == h, jnp.sum(w, axis=1, keepdims=True), row_w)
                col_w = jnp.where(sub8 == h, jnp.sum(w, axis=0, keepdims=True), col_w)
            dxdt_blocks.append(acc_dx)
        dxdt_x = jnp.concatenate(dxdt_blocks, axis=1) + e_st_x * g1_x
        dxs_ref[...] = dxdt_x * dt_x + dy_x * d_ref[...]
        dye = dy_x * e_out_x
        xde = xdt_x * e_st_x
        ddt, y_off, tl, dskip = _group_sums([dxdt_x * xs, dye * cp_x, xde * g1_x, dy_x * xs], expand)
        ddt_ref[...] = ddt
        dd_ref[...] += jnp.sum(dskip, axis=0, keepdims=True)
        sp = None
        for part in _split3(d_s * prev):
            t = lax.dot_general(expand, part, NN, preferred_element_type=F32)
            sp = t if sp is None else sp + t
        last_col = jnp.exp(cs_row[:, CHUNK - 1:CHUNK]) * jnp.sum(sp, axis=1, keepdims=True)
        eye = lax.broadcasted_iota(jnp.int32, (HEADS_PER_GROUP, HEADS_PER_GROUP), 0) == lax.broadcasted_iota(
            jnp.int32, (HEADS_PER_GROUP, HEADS_PER_GROUP), 1)
        last_row = jnp.sum(jnp.where(eye, last_col, 0.0), axis=0, keepdims=True) + jnp.sum(tl, axis=0, keepdims=True)
        is_last = lax.broadcasted_iota(jnp.int32, (CHUNK, 1), 0) == CHUNK - 1
        d_cs = row_w + y_off - tl + jnp.where(is_last, last_row, 0.0)
        da_ref[...] = _dot_select(upper, d_cs, data=1) - _dot_select(upper, col_w, NT, data=1)
        dc_ref[...] = _dot(d_cb, bc) + _dot(dye, prev)
        db_ref[...] = _dot(d_cb, cc, TN) + _dot(xde, d_s)
        _decay_state(ds_scr, d_s, _dot(dye, cc, TN), cs)

    rev = lambda c: last - c
    xs = pl.BlockSpec((CHUNK, GROUP_COLS), lambda g, c: (rev(c), g))
    bsp = pl.BlockSpec((CHUNK, SSM_STATE), lambda g, c: (rev(c), D_INNER // SSM_STATE + g))
    csp = pl.BlockSpec((CHUNK, SSM_STATE), lambda g, c: (rev(c), (D_INNER + GN) // SSM_STATE + g))
    per_head = pl.BlockSpec((None, CHUNK, HEADS_PER_GROUP), lambda g, c: (g, rev(c), 0))
    dsk = pl.BlockSpec((None, 1, GROUP_COLS), lambda g, c: (g, 0, 0))
    dsum = pl.BlockSpec((None, 1, HEADS_PER_GROUP), lambda g, c: (g, 0, 0))
    st = pl.BlockSpec((None, GROUP_COLS, SSM_STATE), lambda g, c: (rev(c), g, 0))
    grp = pl.BlockSpec((CHUNK, SSM_STATE), lambda g, c: (rev(c), g))
    return _call(
        body, name=name, grid=(SSM_GROUPS, N_CHUNKS),
        in_specs=[xs, bsp, csp, per_head, per_head, dsk, st, xs],
        out_specs=[xs, grp, grp, per_head, per_head, dsum],
        out_shape=[jax.ShapeDtypeStruct((t_dim, D_INNER), F32), jax.ShapeDtypeStruct((t_dim, GN), F32),
                   jax.ShapeDtypeStruct((t_dim, GN), F32),
                   jax.ShapeDtypeStruct((SSM_GROUPS, t_dim, HEADS_PER_GROUP), F32),
                   jax.ShapeDtypeStruct((SSM_GROUPS, t_dim, HEADS_PER_GROUP), F32),
                   jax.ShapeDtypeStruct((SSM_GROUPS, 1, HEADS_PER_GROUP), F32)],
        scratch_shapes=[pltpu.VMEM((GROUP_COLS, SSM_STATE), F32)],
        sem=("parallel", "arbitrary"), args=[xbc, xbc, xbc, dtg, ag, dgx, states, dy], comm=comm)


NORM_GROUP = D_INNER // SSM_GROUPS


def _gate_norm_fwd(y, zx, nw, *, name, tm=256):
    t_dim = y.shape[0]
    row = pl.BlockSpec((tm, D_INNER), lambda i: (i, 0))

    def body(y_ref, z_ref, nw_ref, o_ref):
        z = z_ref[...]
        yz = y_ref[...] * (z * _sigmoid(z))
        for g in range(SSM_GROUPS):
            cols = slice(g * NORM_GROUP, (g + 1) * NORM_GROUP)
            yhat, _ = _rms(yz[:, cols])
            o_ref[:, cols] = (yhat * nw_ref[:, cols]).astype(BF16)

    return pl.pallas_call(
        body, name=name, grid=(t_dim // tm,), in_specs=[row, row, pl.BlockSpec((1, D_INNER), lambda i: (0, 0))],
        out_specs=row, out_shape=jax.ShapeDtypeStruct((t_dim, D_INNER), BF16),
        compiler_params=_params("parallel"),
    )(y, zx, nw)


def _gate_norm_bwd(y, zx, nw, dyn, *, name, tm=256):
    t_dim = y.shape[0]
    row = pl.BlockSpec((tm, D_INNER), lambda i: (i, 0))
    vec = pl.BlockSpec((1, D_INNER), lambda i: (0, 0))

    def body(y_ref, z_ref, nw_ref, dyn_ref, dy_ref, dz_ref, dnw_ref):
        @pl.when(pl.program_id(0) == 0)
        def _():
            dnw_ref[...] = jnp.zeros_like(dnw_ref)

        z = z_ref[...]
        yv = y_ref[...]
        sg = _sigmoid(z)
        silu_z = z * sg
        yz = yv * silu_z
        dyn_v = dyn_ref[...]
        for g in range(SSM_GROUPS):
            cols = slice(g * NORM_GROUP, (g + 1) * NORM_GROUP)
            yhat, r = _rms(yz[:, cols])
            dn = dyn_v[:, cols]
            dnw_ref[:, cols] += jnp.sum(dn * yhat, axis=0, keepdims=True)
            dyhat = dn * nw_ref[:, cols]
            dyz = r * (dyhat - yhat * jnp.mean(dyhat * yhat, axis=-1, keepdims=True))
            dy_ref[:, cols] = dyz * silu_z[:, cols]
            dz_ref[:, cols] = dyz * yv[:, cols] * _dsilu(z[:, cols], sg[:, cols])

    return pl.pallas_call(
        body, name=name, grid=(t_dim // tm,), in_specs=[row, row, vec, row],
        out_specs=[row, row, vec],
        out_shape=[jax.ShapeDtypeStruct((t_dim, D_INNER), F32), jax.ShapeDtypeStruct((t_dim, ZX_DIM), F32),
                   jax.ShapeDtypeStruct((1, D_INNER), F32)],
        compiler_params=_params("arbitrary"),
    )(y, zx, nw, dyn)


HEADS_PER_LANE_TILE = LANES // ATT_HEAD_DIM
STACKED_ROWS = Q_PER_KV * WINDOW


def _att_half_masks():
    lane = lax.broadcasted_iota(jnp.int32, (WINDOW, LANES), 1)
    return [(lane >= i * ATT_HEAD_DIM) & (lane < (i + 1) * ATT_HEAD_DIM) for i in range(HEADS_PER_LANE_TILE)]


def _att_stack_heads(ref, kvh, masks):
    parts = []
    for g in range(Q_PER_KV):
        h = kvh * Q_PER_KV + g
        blk = ref[:, (h // HEADS_PER_LANE_TILE) * LANES:(h // HEADS_PER_LANE_TILE + 1) * LANES]
        parts.append(jnp.where(masks[h % HEADS_PER_LANE_TILE], blk, jnp.zeros_like(blk)))
    return jnp.concatenate(parts, axis=0)


def _att_kv_tile(ref, kvh, masks):
    blk = ref[:, (kvh // HEADS_PER_LANE_TILE) * LANES:(kvh // HEADS_PER_LANE_TILE + 1) * LANES]
    return jnp.where(masks[kvh % HEADS_PER_LANE_TILE], blk, pltpu.roll(blk, ATT_HEAD_DIM, 1)).astype(BF16)


def _att_stacked_masks(n):
    row = lax.bitwise_and(lax.broadcasted_iota(jnp.int32, (STACKED_ROWS, WINDOW), 0), WINDOW - 1)
    col = lax.broadcasted_iota(jnp.int32, (STACKED_ROWS, WINDOW), 1)
    return col <= row, (col > row) & (n > 0)


def _att_stack_columns(ref, kvh, rows):
    cols = [ref[:, kvh * Q_PER_KV + g:kvh * Q_PER_KV + g + 1] for g in range(Q_PER_KV)]
    return jnp.concatenate([jnp.broadcast_to(c, (rows, 1)) for c in cols], axis=0)


def _att_scores(q4, k_tile, mask):
    return jnp.where(mask, _dot(q4, k_tile, NT) * ATT_SCALE, NEG_BIG)


def _att_unstack(x4, kvh, masks, tiles):
    for g in range(Q_PER_KV):
        h = kvh * Q_PER_KV + g
        piece = x4[g * WINDOW:(g + 1) * WINDOW]
        t = h // HEADS_PER_LANE_TILE
        tiles[t] = piece if h % HEADS_PER_LANE_TILE == 0 else jnp.where(masks[1], piece, tiles[t])


def _attn_fwd(q, k, v, sinks, *, name, comm=None):
    t_dim = q.shape[0]

    def body(q_ref, kc_ref, kp_ref, vc_ref, vp_ref, s_ref, o_ref, l_ref):
        n = pl.program_id(0)
        masks = _att_half_masks()
        mask_c, mask_p = _att_stacked_masks(n)
        out_tiles = [None] * (D_MODEL // LANES)
        for kvh in range(N_KV_HEADS):
            q4 = _att_stack_heads(q_ref, kvh, masks).astype(BF16)
            kc, kp = _att_kv_tile(kc_ref, kvh, masks), _att_kv_tile(kp_ref, kvh, masks)
            vc, vp = _att_kv_tile(vc_ref, kvh, masks), _att_kv_tile(vp_ref, kvh, masks)
            sc = _att_scores(q4, kc, mask_c)
            sp = _att_scores(q4, kp, mask_p)
            sink = _att_stack_columns(s_ref, kvh, WINDOW)
            m = jnp.maximum(jnp.maximum(jnp.max(sc, axis=1, keepdims=True), jnp.max(sp, axis=1, keepdims=True)), sink)
            pc = jnp.exp(sc - m)
            pp = jnp.exp(sp - m)
            den = jnp.sum(pc, axis=1, keepdims=True) + jnp.sum(pp, axis=1, keepdims=True) + jnp.exp(sink - m)
            _att_unstack((_dot(pc, vc) + _dot(pp, vp)) / den, kvh, masks, out_tiles)
            lse4 = m + jnp.log(den)
            for g in range(Q_PER_KV):
                h = kvh * Q_PER_KV + g
                l_ref[:, h:h + 1] = lse4[g * WINDOW:(g + 1) * WINDOW]
        for t, tile in enumerate(out_tiles):
            o_ref[:, t * LANES:(t + 1) * LANES] = tile

    cur = lambda w: pl.BlockSpec((WINDOW, w), lambda n: (n, 0))
    prv = lambda w: pl.BlockSpec((WINDOW, w), lambda n: (jnp.maximum(n - 1, 0), 0))
    return _call(
        body, name=name, grid=(t_dim // WINDOW,),
        in_specs=[cur(D_MODEL), cur(KV_DIM), prv(KV_DIM), cur(KV_DIM), prv(KV_DIM), pl.BlockSpec((1, N_Q_HEADS), lambda n: (0, 0))],
        out_specs=[cur(D_MODEL), cur(N_Q_HEADS)],
        out_shape=[jax.ShapeDtypeStruct((t_dim, D_MODEL), F32), jax.ShapeDtypeStruct((t_dim, N_Q_HEADS), F32)],
        sem=("parallel",), args=[q, k, k, v, v, sinks], comm=comm)


def _attn_bwd(q, k, v, sinks, o, lse, do, cos2, sin2, *, name, comm=None):
    t_dim = q.shape[0]

    def body(q_ref, kc_ref, kp_ref, vc_ref, vp_ref, s_ref, o_ref, l_ref, do_ref, cos_ref, sin_ref, cos_all_ref, sin_all_ref,
             dq_ref, dk_ref, dv_ref, dsink_ref):
        n = pl.program_id(0)

        @pl.when(n == 0)
        def _():
            dk_ref[...] = jnp.zeros_like(dk_ref)
            dv_ref[...] = jnp.zeros_like(dv_ref)
            dsink_ref[...] = jnp.zeros_like(dsink_ref)

        masks = _att_half_masks()
        mask_c, mask_p = _att_stacked_masks(n)
        lane_row = lax.broadcasted_iota(jnp.int32, (1, N_Q_HEADS), 1)
        rows_c = pl.ds(pl.multiple_of(n * WINDOW, WINDOW), WINDOW)
        rows_p = pl.ds(pl.multiple_of(jnp.maximum(n - 1, 0) * WINDOW, WINDOW), WINDOW)
        dsink = jnp.zeros((1, N_Q_HEADS), F32)
        dq_tiles = [None] * (D_MODEL // LANES)
        kv_tiles = KV_DIM // LANES
        dkc_tiles, dkp_tiles, dvc_tiles, dvp_tiles = ([None] * kv_tiles for _ in range(4))

        def place(tiles, kvh, x):
            folded = x + pltpu.roll(x, ATT_HEAD_DIM, 1)
            t = kvh // HEADS_PER_LANE_TILE
            tiles[t] = folded if kvh % HEADS_PER_LANE_TILE == 0 else jnp.where(masks[1], folded, tiles[t])

        for kvh in range(N_KV_HEADS):
            q4 = _att_stack_heads(q_ref, kvh, masks).astype(BF16)
            do4 = _att_stack_heads(do_ref, kvh, masks)
            o4 = _att_stack_heads(o_ref, kvh, masks)
            kc, kp = _att_kv_tile(kc_ref, kvh, masks), _att_kv_tile(kp_ref, kvh, masks)
            vc, vp = _att_kv_tile(vc_ref, kvh, masks), _att_kv_tile(vp_ref, kvh, masks)
            l4 = _att_stack_columns(l_ref, kvh, WINDOW)
            pc = jnp.exp(_att_scores(q4, kc, mask_c) - l4)
            pp = jnp.exp(_att_scores(q4, kp, mask_p) - l4)
            delta = jnp.sum(do4 * o4, axis=1, keepdims=True)
            do4b = do4.astype(BF16)
            dsc = pc * (_dot(do4b, vc, NT) - delta)
            dsp = pp * (_dot(do4b, vp, NT) - delta)
            _att_unstack((_dot(dsc, kc) + _dot(dsp, kp)) * ATT_SCALE, kvh, masks, dq_tiles)
            place(dkc_tiles, kvh, _dot(dsc, q4, TN) * ATT_SCALE)
            place(dkp_tiles, kvh, _dot(dsp, q4, TN) * ATT_SCALE)
            place(dvc_tiles, kvh, _dot(pc, do4b, TN))
            place(dvp_tiles, kvh, _dot(pp, do4b, TN))
            p_sink = jnp.exp(_att_stack_columns(s_ref, kvh, WINDOW) - l4) * delta
            for g in range(Q_PER_KV):
                h = kvh * Q_PER_KV + g
                dsink = jnp.where(lane_row == h, -jnp.sum(p_sink[g * WINDOW:(g + 1) * WINDOW], axis=0, keepdims=True), dsink)
        for t, tile in enumerate(dq_tiles):
            dq_ref[:, t * LANES:(t + 1) * LANES] = _rope_rotate(tile, cos_ref[...], -sin_ref[...])
        for t in range(kv_tiles):
            lanes = slice(t * LANES, (t + 1) * LANES)
            dk_ref[rows_c, lanes] += dkc_tiles[t]
            dk_ref[rows_p, lanes] += dkp_tiles[t]
            dv_ref[rows_c, lanes] += dvc_tiles[t]
            dv_ref[rows_p, lanes] += dvp_tiles[t]
        dsink_ref[...] += dsink

        @pl.when(n == t_dim // WINDOW - 1)
        def _():
            dk_ref[...] = _rope_rotate(dk_ref[...], cos_all_ref[...], -sin_all_ref[...])

    cur = lambda w: pl.BlockSpec((WINDOW, w), lambda n: (n, 0))
    prv = lambda w: pl.BlockSpec((WINDOW, w), lambda n: (jnp.maximum(n - 1, 0), 0))
    whole = lambda w: pl.BlockSpec((t_dim, w), lambda n: (0, 0))
    svec = pl.BlockSpec((1, N_Q_HEADS), lambda n: (0, 0))
    return _call(
        body, name=name, grid=(t_dim // WINDOW,),
        in_specs=[cur(D_MODEL), cur(KV_DIM), prv(KV_DIM), cur(KV_DIM), prv(KV_DIM), svec, cur(D_MODEL), cur(N_Q_HEADS), cur(D_MODEL),
                  cur(LANES), cur(LANES), whole(LANES), whole(LANES)],
        out_specs=[cur(D_MODEL), whole(KV_DIM), whole(KV_DIM), svec],
        out_shape=[jax.ShapeDtypeStruct((t_dim, D_MODEL), F32), jax.ShapeDtypeStruct((t_dim, KV_DIM), F32),
                   jax.ShapeDtypeStruct((t_dim, KV_DIM), F32), jax.ShapeDtypeStruct((1, N_Q_HEADS), F32)],
        sem=("arbitrary",), args=[q, k, k, v, v, sinks, o, lse, do, cos2, sin2, cos2, sin2], comm=comm)


def _loss_head(x, nw, target, *, name, tm=512):
    t_dim, d_dim = x.shape
    row = pl.BlockSpec((tm, d_dim), lambda i: (i, 0))
    vec = pl.BlockSpec((1, d_dim), lambda i: (0, 0))

    def body(x_ref, nw_ref, tgt_ref, loss_ref, dx_ref, dnw_ref, dob_ref):
        @pl.when(pl.program_id(0) == 0)
        def _():
            loss_ref[...] = jnp.zeros_like(loss_ref)
            dnw_ref[...] = jnp.zeros_like(dnw_ref)

        xhat, r = _rms(x_ref[...])
        err = xhat * nw_ref[...] - tgt_ref[...]
        loss_ref[...] += 0.5 * _sum_all(jnp.mean(err * err, axis=-1, keepdims=True))
        dy = err * (1.0 / d_dim)
        dnw_ref[...] += jnp.sum(dy * xhat, axis=0, keepdims=True)
        dxhat = dy * nw_ref[...]
        dx = r * (dxhat - xhat * jnp.mean(dxhat * xhat, axis=-1, keepdims=True))
        dx_ref[...] = dx
        dob_ref[...] = (FFN_RES_WEIGHT * dx).astype(BF16)

    return pl.pallas_call(
        body, name=name, grid=(t_dim // tm,), in_specs=[row, vec, row],
        out_specs=[pl.BlockSpec((1, 1), lambda i: (0, 0)), row, vec, row],
        out_shape=[jax.ShapeDtypeStruct((1, 1), F32), jax.ShapeDtypeStruct((t_dim, d_dim), F32),
                   jax.ShapeDtypeStruct((1, d_dim), F32), jax.ShapeDtypeStruct((t_dim, d_dim), BF16)],
        compiler_params=_params("arbitrary"),
    )(x, nw, target)


def _rope_tables():
    pos = jnp.arange(SEQ, dtype=F32)
    inv = 1.0 / (ROPE_THETA ** (jnp.arange(0, ATT_HEAD_DIM, 2, dtype=F32) / ATT_HEAD_DIM))
    ang = pos[:, None] * inv[None, :]
    cos, sin = jnp.cos(ang), jnp.sin(ang)
    return jnp.tile(cos, (1, 4)), jnp.tile(sin, (1, 4))


def _to_groups(t):
    return t.reshape(t.shape[0], SSM_GROUPS, HEADS_PER_GROUP).transpose(1, 0, 2)


def _from_groups(t):
    return t.transpose(1, 0, 2).reshape(t.shape[1], SSM_HEADS)


def _forward_backward(x0, target, net):
    w = net.w
    nw = [[w("norm_w")[l, i][None, :] for i in range(3)] for l in range(2)]
    cos2, sin2 = _rope_tables()
    ffn_norm = [nw[0][0], nw[0][2], nw[1][0], nw[1][2]]

    ffn_pre = {}

    def ffn_f(x, blk):
        name = f"ffn_fwd{blk}"
        out, *ffn_pre[blk] = _ffn_fwd(x, ffn_norm[blk], w(f"gate{blk}"), w(f"up{blk}"), w(f"down{blk}"), name=name,
                                      comm=net.carry(name))
        return out

    x1 = ffn_f(x0, 0)
    zx, h1 = _norm_mm(x1, nw[0][1], w("w_in_t"), None, w_rows=ZX_DIM, name="ssm_in_proj", comm=net.carry("ssm_in_proj"))
    dtr = _mm(h1, w("w_in_t"), dims="nt", b_rows=(ZX_DIM, SSM_HEADS), name="ssm_dt_proj")
    xbc = _conv_fwd(zx, w("conv_w"), w("conv_b"), name="ssm_conv_fwd", comm=net.carry("ssm_conv_fwd"))
    dt, a_dt = _dt_prep(dtr, w("dt_bias"), w("a_log"), name="ssm_dt_prep")
    dtg, ag = _to_groups(dt), _to_groups(a_dt)
    dg = jnp.repeat(w("d_skip").reshape(SSM_GROUPS, 1, HEADS_PER_GROUP), SSM_HEAD_DIM, axis=2)
    y_ssd, states = _ssd_fwd(xbc, dtg, ag, dg, name="ssd_fwd", comm=net.carry("ssd_fwd"))
    yn = _gate_norm_fwd(y_ssd, zx, w("ssm_norm_w"), name="ssm_gate_norm_fwd")
    x2 = _mm(yn, w("wout"), res=x1, name="ssm_out_proj", comm=net.carry("ssm_out_proj"))
    x3 = ffn_f(x2, 1)
    k_rot, hk = _norm_mm(x3, w("kv_norm_w"), w("wk"), w("b_k"), rope=(cos2, sin2), name="k_proj")
    v = _mm(hk, w("wv"), bias=w("b_v"), name="v_proj")
    x4 = ffn_f(x3, 2)
    q_rot, h4 = _norm_mm(x4, nw[1][1], w("wq"), w("b_q"), rope=(cos2, sin2), name="q_proj")
    att, lse = _attn_fwd(q_rot, k_rot, v, w("sinks"), name="attn_fwd", comm=net.carry("attn_fwd"))
    x5 = _mm(att, w("wo"), bias=w("b_o"), res=x4, name="attn_out_proj")
    x6 = ffn_f(x5, 3)
    loss, dx6, d_final, dob6 = _loss_head(x6, w("final_norm_w"), target, name="loss_head")

    d_norm = [[None] * 3 for _ in range(2)]

    def ffn_b(x, dout, dob, blk):
        pre_g, pre_u, h = ffn_pre[blk]
        name = f"ffn_bwd{blk}"
        dh, gg, gu, gd = _ffn_bwd(h, dob, pre_g, pre_u, w(f"gate{blk}"), w(f"up{blk}"), w(f"down{blk}"), name=name,
                                  comm=net.carry(name))
        net.give(f"gate{blk}", gg)
        net.give(f"up{blk}", gu)
        net.give(f"down{blk}", gd)
        return _norm_bwd(x, ffn_norm[blk], dh, [dout], name=f"ffn_norm_bwd{blk}", comm=net.carry(f"ffn_norm_bwd{blk}"))

    by_rows = lambda g: g.reshape(N_DEV, g.shape[0] // N_DEV, g.shape[1])
    dx5, d_norm[1][2] = ffn_b(x5, dx6, dob6, 3)
    d_att = _mm(dx5, w("wo"), dims="nt", name="attn_out_proj_dx", comm=net.carry("attn_out_proj_dx"))
    g_o, d_bo = _mm(att, dx5, dims="tn", out_dtype=BF16, colsum_b=True, name="attn_out_proj_dw")
    net.give("w_o", by_rows(g_o))
    dq, dk, dv, d_sinks = _attn_bwd(q_rot, k_rot, v, w("sinks"), att, lse, d_att, cos2, sin2, name="attn_bwd",
                                    comm=net.carry("attn_bwd"))
    dx4, d_norm[1][1], dob4 = _mm_norm_bwd(dq, w("wq"), x4, nw[1][1], [dx5], dims="nt", name="q_proj_dx")
    g_q, d_bq = _mm(h4, dq, dims="tn", out_dtype=BF16, colsum_b=True, name="q_proj_dw")
    net.give("w_q", by_rows(g_q))
    dx3a, d_norm[1][0] = ffn_b(x3, dx4, dob4, 2)
    dhk = _mm(dk, w("wk"), dims="nt", name="k_proj_dx", comm=net.carry("k_proj_dx"))
    dx3, d_kvn, dob3 = _mm_norm_bwd(dv, w("wv"), x3, w("kv_norm_w"), [dx3a], dims="nt", add=dhk, name="v_proj_dx")
    g_k, d_bk = _mm(hk, dk, dims="tn", out_dtype=BF16, colsum_b=True, name="k_proj_dw")
    g_v, d_bv = _mm(hk, dv, dims="tn", out_dtype=BF16, colsum_b=True, name="v_proj_dw")
    net.give("w_k", by_rows(g_k))
    net.give("w_v", by_rows(g_v))
    dx2, d_norm[0][2] = ffn_b(x2, dx3, dob3, 1)
    d_yn = _mm(dx2, w("wout"), dims="nt", name="ssm_out_proj_dx", comm=net.carry("ssm_out_proj_dx"))
    net.give("w_out", by_rows(_mm(yn, dx2, dims="tn", out_dtype=BF16, name="ssm_out_proj_dw")))
    dy_ssd, dzx, d_ssm_norm = _gate_norm_bwd(y_ssd, zx, w("ssm_norm_w"), d_yn, name="ssm_gate_norm_bwd")
    dxs, d_b, d_c, ddtg, dag, ddg = _ssd_bwd(xbc, dtg, ag, dg, states, dy_ssd, name="ssd_bwd", comm=net.carry("ssd_bwd"))
    dzx, d_conv_w, d_conv_b = _conv_bwd(zx, w("conv_w"), w("conv_b"), dxs, d_b, d_c, dzx, name="ssm_conv_bwd",
                                        comm=net.carry("ssm_conv_bwd"))
    ddtr, d_dt_bias, d_a_log = _dt_bwd(dtr, w("dt_bias"), w("a_log"), dt, _from_groups(ddtg), _from_groups(dag), name="ssm_dt_bwd")
    dh1 = _mm(dzx, w("w_in_t"), b_rows=(0, ZX_DIM), name="ssm_in_proj_dx", comm=net.carry("ssm_in_proj_dx"))
    in_rows = N_DEV * IN_PROJ_SHARD
    g_in = _mm(dzx, h1, dims="tn", out_dtype=BF16, out_window=(0, in_rows), name="ssm_in_proj_dw")
    g_in = _mm(ddtr, h1, dims="tn", out_dtype=BF16, out_window=(ZX_DIM, in_rows), into=g_in, name="ssm_dt_proj_dw")
    net.give("w_in", g_in.reshape(N_DEV, IN_PROJ_SHARD, D_MODEL))
    dx1, d_norm[0][1], dob1 = _mm_norm_bwd(ddtr, w("w_in_t"), x1, nw[0][1], [dx2], b_rows=(ZX_DIM, SSM_HEADS), add=dh1,
                                           name="ssm_dt_proj_dx", comm=net.carry("ssm_norm_bwd"))
    dx0, d_norm[0][0] = ffn_b(x0, dx1, dob1, 0)

    small = {"norm_w": jnp.concatenate([d_norm[l][i] for l in range(2) for i in range(3)], axis=0),
             "ssm_conv_w": d_conv_w, "ssm_conv_b": d_conv_b, "ssm_dt_bias": d_dt_bias, "ssm_a_log": d_a_log,
             "ssm_d": ddg.reshape(1, SSM_HEADS), "ssm_norm_w": d_ssm_norm, "kv_norm_w": d_kvn,
             "b_k": d_bk, "b_v": d_bv, "attn_b_q": d_bq, "attn_sinks": d_sinks, "attn_b_o": d_bo, "final_norm_w": d_final}
    return loss, dx0, small


BLOCK_BYTES = 1 << 20


def _row_tile(rows, cols):
    for t in (512, 256, 128, 64, 32, 16):
        if rows % t == 0 and t * cols * 4 <= BLOCK_BYTES:
            return t
    return rows


def _cast_bf16(x, *, name):
    n_blk, rows, cols = x.shape
    tm = rows if rows * cols * 4 <= 2 * BLOCK_BYTES else _row_tile(rows, cols)
    spec = pl.BlockSpec((None, tm, cols), lambda b, i: (b, i, 0))

    def body(x_ref, o_ref):
        o_ref[...] = x_ref[...].astype(BF16)

    return pl.pallas_call(body, name=name, grid=(n_blk, rows // tm), in_specs=[spec], out_specs=spec,
                          out_shape=jax.ShapeDtypeStruct(x.shape, BF16), compiler_params=_params("parallel", "parallel"))(x)


def _pair_add(grad, theirs, *, name):
    n_slots, rows, cols = theirs.shape
    tm = rows if rows * cols * 4 <= 2 * BLOCK_BYTES else _row_tile(rows, cols)

    def body(g_ref, t_ref, o_ref):
        o_ref[...] = (g_ref[...].astype(F32) + t_ref[...].astype(F32)).astype(BF16)

    spec = pl.BlockSpec((None, tm, cols), lambda s, i: (s, i, 0))
    return pl.pallas_call(
        body, name=name, grid=(n_slots, rows // tm),
        in_specs=[pl.BlockSpec((None, tm, cols), lambda s, i: (2 * s + lax.axis_index("c"), i, 0)), spec], out_specs=spec,
        out_shape=jax.ShapeDtypeStruct(theirs.shape, BF16), compiler_params=_params("parallel", "parallel"),
    )(grad, theirs)


def _adam_update(g, w, m, v):
    m = ADAM_B1 * m + (1.0 - ADAM_B1) * g
    v = ADAM_B2 * v + (1.0 - ADAM_B2) * (g * g)
    m_hat = m / (1.0 - ADAM_B1 ** ADAM_STEP)
    v_hat = v / (1.0 - ADAM_B2 ** ADAM_STEP)
    delta = -ADAM_LR * (m_hat / (jnp.sqrt(v_hat) + ADAM_EPS) + ADAM_WD * w)
    return delta, m, v


def _adamw(parts, w, m, v, first_blk, prev, *, name, comm=None):
    n_blk, rows, cols = w.shape
    tm = _row_tile(rows, cols)
    n_tiles = rows // tm
    spec = pl.BlockSpec((None, tm, cols), lambda b, i: (first_blk + b, i, 0))
    n_prev, n_here = len(prev), len(parts)
    n_parts = parts[0].shape[0]

    def part_spec(q):
        return pl.BlockSpec((n_parts, tm, cols), lambda b, i: (0, jnp.where(b < q, 0, jnp.where(b == q, i, n_tiles - 1)), 0))

    def body(*refs):
        p_refs = refs[:n_here]
        w_ref, m_ref, v_ref = refs[n_here:n_here + 3]
        g_ref, d_ref, nm_ref, nv_ref = refs[n_here + 3 + n_prev:]
        b = pl.program_id(0)
        g = None
        for s in range(n_parts):
            t = p_refs[0][s]
            for q in range(1, n_here):
                t = jnp.where(b == q, p_refs[q][s], t)
            g = t.astype(F32) if g is None else g + t.astype(F32)
        delta, nm, nv = _adam_update(g, w_ref[...], m_ref[...], v_ref[...])
        g_ref[...] = g
        d_ref[...] = delta
        nm_ref[...] = nm
        nv_ref[...] = nv

    return _call(
        body, name=name, grid=(n_here, n_tiles),
        in_specs=[part_spec(q) for q in range(n_here)] + [spec, spec, spec] + [pl.BlockSpec(memory_space=pl.ANY)] * n_prev,
        out_specs=[spec] * 4, out_shape=[jax.ShapeDtypeStruct((n_blk, rows, cols), F32)] * 4,
        aliases={n_here + 3 + q: q for q in range(n_prev)}, sem=("arbitrary", "arbitrary"),
        args=[*parts, w, m, v, *prev], comm=comm)


def _sum_parts(parts, *, name):
    def body(p_ref, o_ref):
        g = p_ref[0]
        for s in range(1, N_DEV):
            g = g + p_ref[s]
        o_ref[...] = g

    return pl.pallas_call(body, name=name, out_shape=jax.ShapeDtypeStruct(parts.shape[1:], F32), compiler_params=_params())(parts)


def _adamw_packed(g, w, m, v, *, name):
    def body(g_ref, w_ref, m_ref, v_ref, d_ref, nm_ref, nv_ref):
        delta, nm, nv = _adam_update(g_ref[...], w_ref[...], m_ref[...], v_ref[...])
        d_ref[...] = delta
        nm_ref[...] = nm
        nv_ref[...] = nv

    return pl.pallas_call(body, name=name, out_shape=[jax.ShapeDtypeStruct(g.shape, F32)] * 3, compiler_params=_params())(g, w, m, v)


SUBLANES = 8


WIDE_PACK = 1024


def _pack(arrs, width=LANES):
    rows = []
    for a in arrs:
        a2 = a.reshape(-1, a.shape[-1])
        a2 = jnp.pad(a2, ((0, 0), (0, (-a2.shape[1]) % width)))
        rows += [a2[:, i * width:(i + 1) * width] for i in range(a2.shape[1] // width)]
    out = jnp.concatenate(rows, axis=0)
    return jnp.pad(out, ((0, (-out.shape[0]) % SUBLANES), (0, 0)))


def _unpack(packed, shapes, width=LANES):
    outs, r = [], 0
    for shp in shapes:
        lead, cols = math.prod(shp[:-1]), shp[-1]
        n_blocks = -(-cols // width)
        blocks = [packed[r + i * lead:r + (i + 1) * lead] for i in range(n_blocks)]
        outs.append(jnp.concatenate(blocks, axis=1)[:, :cols].reshape(shp))
        r += n_blocks * lead
    return outs


WEIGHT_NAMES = ("norm_w", "ffn_w_gate", "ffn_w_up", "ffn_w_down", "ssm_w_in", "ssm_conv_w", "ssm_conv_b", "ssm_dt_bias",
                "ssm_a_log", "ssm_d", "ssm_norm_w", "ssm_w_out", "kv_norm_w", "w_k", "b_k", "w_v", "b_v", "attn_w_q",
                "attn_b_q", "attn_sinks", "attn_w_o", "attn_b_o", "final_norm_w")
MATRIX_NAMES = ("ffn_w_gate", "ffn_w_up", "ffn_w_down", "ssm_w_in", "ssm_w_out", "w_k", "w_v", "attn_w_q", "attn_w_o")
VECTOR_NAMES = tuple(n for n in WEIGHT_NAMES if n not in MATRIX_NAMES)
SHARDED_VECTORS = ("norm_w", "ssm_conv_w", "ssm_conv_b", "ssm_norm_w")


GATHER_PLAN = {
    "gather_stage0": ("gate0", "up0", "down0", "vec"),
    "ffn_fwd0": ("w_in",),
    "ssm_in_proj": ("w_out", "gate1"),
    "ssm_conv_fwd": ("w_k", "w_v", "up1"),
    "ssd_fwd": ("down1", "gate2"),
    "ssm_out_proj": ("w_q", "w_o"),
    "ffn_fwd1": ("up2", "down2"),
    "ffn_fwd2": ("up3",),
    "attn_fwd": ("gate3", "down3"),
}
PAIR_PLAN = {
    "attn_bwd": ("gate3", "up3", "down3"),
    "ffn_bwd2": ("w_q", "w_o"),
    "ffn_bwd1": ("gate2", "up2", "down2", "w_k", "w_v"),
    "ssd_bwd": ("gate1", "up1", "down1", "w_out"),
    "ssm_norm_bwd": ("w_in",),
    "ffn_norm_bwd0": ("gate0", "up0", "down0"),
}
CHIP_PLAN = {
    "ffn_bwd2": ("gate3", "up3", "down3"),
    "ssd_bwd": ("gate2", "up2", "down2", "w_q", "w_o", "w_k", "w_v"),
    "ssm_conv_bwd": ("gate1", "up1"),
    "ssm_in_proj_dx": ("w_out",),
    "ffn_bwd0": ("down1", "w_in"),
    "adamw_gate": ("gate0",),
    "adamw_up": ("up0",),
    "adamw_down": ("down0",),
}
FFN_PARAMS = {"gate": "ffn_w_gate", "up": "ffn_w_up", "down": "ffn_w_down"}
SINGLE_MATRICES = {"w_in": "ssm_w_in", "w_out": "ssm_w_out", "w_k": "w_k", "w_v": "w_v", "w_q": "attn_w_q", "w_o": "attn_w_o"}


TRANSPOSED = ("ffn_w_gate", "ffn_w_up", "ssm_w_in")


def _matrix_view(name, a):
    if name in TRANSPOSED:
        a = jnp.swapaxes(a, -1, -2)
    return a.reshape((-1,) + a.shape[-2:])


def _from_matrix_view(name, a, shape):
    if name in TRANSPOSED:
        return jnp.swapaxes(a.reshape(shape[:-2] + (shape[-1], shape[-2])), -1, -2)
    return a.reshape(shape)


class _MeshNet:
    def __init__(self, p):
        self.p = p
        self.views = {n: _matrix_view(n, p[n]) for n in MATRIX_NAMES}
        self.local = {"vec": _pack([p[n] for n in SHARDED_VECTORS])}
        for short, n in FFN_PARAMS.items():
            cast = _cast_bf16(self.views[n], name=f"cast_{short}")
            self.local.update({f"{short}{k}": (cast, k) for k in range(N_FFN)})
        for short, n in SINGLE_MATRICES.items():
            self.local[short] = (_cast_bf16(self.views[n], name=f"cast_{short}"), 0)
        self.gathered_at, self.pairs_at, self.parts_at, self.grads, self.cache = {}, {}, {}, {}, {}

    def carry(self, name):
        comms = []
        if name in GATHER_PLAN:
            keys, comm = GATHER_PLAN[name], _Gather([self.local[k] for k in GATHER_PLAN[name]])
            self.gathered_at.update({k: (comm, i) for i, k in enumerate(keys)})
            comms.append(comm)
        if name in CHIP_PLAN:
            sums = []
            for k in CHIP_PLAN[name]:
                comm, i = self.pairs_at[k]
                sums.append(_pair_add(self.grads[k], comm.results[i], name=f"pair_add_{k}"))
            comm = _ChipExchange(sums)
            self.parts_at.update({k: (comm, i) for i, k in enumerate(CHIP_PLAN[name])})
            comms.append(comm)
        if name in PAIR_PLAN:
            keys, comm = PAIR_PLAN[name], _PairSwap([self.grads[k] for k in PAIR_PLAN[name]])
            self.pairs_at.update({k: (comm, i) for i, k in enumerate(keys)})
            comms.append(comm)
        return comms

    def run(self, name):
        for comm in self.carry(name):
            _run_exchange(comm, name=name)

    def give(self, key, grad):
        self.grads[key] = grad

    def parts(self, key):
        comm, i = self.parts_at[key]
        return comm.results[i]

    def _gathered(self, key):
        comm, i = self.gathered_at[key]
        return comm.results[i]

    def _vec(self, r0, lead, n_blocks):
        vecs = self._gathered("vec")
        return jnp.concatenate([vecs[d, r0 + i * lead:r0 + (i + 1) * lead, :] for d in range(N_DEV) for i in range(n_blocks)], axis=1)

    def _derive(self, name):
        p = self.p
        if name[:-1] in FFN_PARAMS:
            return self._gathered(name)
        if name == "w_in_t":
            return self._gathered("w_in").reshape(N_DEV * IN_PROJ_SHARD, D_MODEL)
        by_rows = {"wout": "w_out", "wk": "w_k", "wv": "w_v", "wq": "w_q", "wo": "w_o"}
        if name in by_rows:
            g = self._gathered(by_rows[name])
            return g.reshape(N_DEV * g.shape[1], g.shape[2])
        vectors = {"norm_w": lambda: self._vec(0, 6, 1).reshape(2, 3, D_MODEL), "conv_w": lambda: self._vec(6, CONV_WIDTH, 3),
                   "conv_b": lambda: self._vec(18, 1, 3), "ssm_norm_w": lambda: self._vec(21, 1, 2)}
        if name in vectors:
            return vectors[name]()
        replicated = {"dt_bias": p["ssm_dt_bias"], "a_log": p["ssm_a_log"], "d_skip": p["ssm_d"], "kv_norm_w": p["kv_norm_w"][None],
                      "b_k": p["b_k"][None], "b_v": p["b_v"][None], "b_q": p["attn_b_q"], "sinks": p["attn_sinks"],
                      "b_o": p["attn_b_o"], "final_norm_w": p["final_norm_w"][None]}
        return replicated[name]

    def w(self, name):
        if name not in self.cache:
            self.cache[name] = self._derive(name)
        return self.cache[name]


def _step(x, target, p, m, v):
    pos = _slot(_position())
    net = _MeshNet(p)
    net.run("gather_stage0")
    loss, grad_x, small = _forward_backward(x, target, net)

    grads, deltas, new_m, new_v = {}, {}, {}, {}
    view = lambda d, n: _matrix_view(n, d[n])
    vec_gather = _Gather([_pack([small[n] for n in VECTOR_NAMES], WIDE_PACK)])
    for short, n in SINGLE_MATRICES.items():
        outs = _adamw([net.parts(short)], net.views[n], view(m, n), view(v, n), 0, [], name=f"adamw_{short}",
                      comm=[vec_gather] if short == "w_in" else None)
        grads[n], deltas[n], new_m[n], new_v[n] = [_from_matrix_view(n, o, p[n].shape) for o in outs]
    ffn_outs = {}
    for short, n in FFN_PARAMS.items():
        ffn_outs[short] = _adamw([net.parts(f"{short}{k}") for k in range(1, N_FFN)], net.views[n], view(m, n), view(v, n), 1, [],
                                 name=f"adamw_{short}", comm=net.carry(f"adamw_{short}"))
    for short, n in FFN_PARAMS.items():
        outs = _adamw([net.parts(f"{short}0")], net.views[n], view(m, n), view(v, n), 0, ffn_outs[short], name=f"adamw_{short}0")
        grads[n], deltas[n], new_m[n], new_v[n] = [_from_matrix_view(n, o, p[n].shape) for o in outs]
    vec_sum = _sum_parts(vec_gather.results[0], name="sum_vector_grads")
    full_shapes = {"norm_w": (2, 3, D_MODEL), "ssm_conv_w": (1, CONV_WIDTH, CONV_DIM), "ssm_conv_b": (1, CONV_DIM),
                   "ssm_norm_w": (1, D_INNER)}
    vec_full = dict(zip(VECTOR_NAMES, _unpack(vec_sum, [full_shapes.get(n, p[n].shape) for n in VECTOR_NAMES], WIDE_PACK)))
    for n in VECTOR_NAMES:
        g = vec_full[n]
        if n in SHARDED_VECTORS:
            per = p[n].shape[-1]
            g = lax.dynamic_slice_in_dim(g, pos * per, per, axis=g.ndim - 1)
        grads[n] = g
    packed = _adamw_packed(*[_pack([d[n] for n in VECTOR_NAMES], WIDE_PACK) for d in (grads, p, m, v)], name="adamw_vectors")
    shapes = [p[n].shape for n in VECTOR_NAMES]
    for d, pk in zip((deltas, new_m, new_v), packed):
        d.update(zip(VECTOR_NAMES, _unpack(pk, shapes, WIDE_PACK)))
    return loss, grad_x, grads, deltas, new_m, new_v


def kernel(x, norm_w, ffn_w_gate, ffn_w_up, ffn_w_down, ssm_w_in, ssm_conv_w, ssm_conv_b, ssm_dt_bias, ssm_a_log, ssm_d, ssm_norm_w, ssm_w_out, kv_norm_w, w_k, b_k, w_v, b_v, attn_w_q, attn_b_q, attn_sinks, attn_w_o, attn_b_o, final_norm_w, loss_target, m_norm_w, m_ffn_w_gate, m_ffn_w_up, m_ffn_w_down, m_ssm_w_in, m_ssm_conv_w, m_ssm_conv_b, m_ssm_dt_bias, m_ssm_a_log, m_ssm_d, m_ssm_norm_w, m_ssm_w_out, m_kv_norm_w, m_w_k, m_b_k, m_w_v, m_b_v, m_attn_w_q, m_attn_b_q, m_attn_sinks, m_attn_w_o, m_attn_b_o, m_final_norm_w, v_norm_w, v_ffn_w_gate, v_ffn_w_up, v_ffn_w_down, v_ssm_w_in, v_ssm_conv_w, v_ssm_conv_b, v_ssm_dt_bias, v_ssm_a_log, v_ssm_d, v_ssm_norm_w, v_ssm_w_out, v_kv_norm_w, v_w_k, v_b_k, v_w_v, v_b_v, v_attn_w_q, v_attn_b_q, v_attn_sinks, v_attn_w_o, v_attn_b_o, v_final_norm_w):
    p = dict(zip(WEIGHT_NAMES, (norm_w, ffn_w_gate, ffn_w_up, ffn_w_down, ssm_w_in, ssm_conv_w, ssm_conv_b, ssm_dt_bias, ssm_a_log, ssm_d, ssm_norm_w, ssm_w_out, kv_norm_w, w_k, b_k, w_v, b_v, attn_w_q, attn_b_q, attn_sinks, attn_w_o, attn_b_o, final_norm_w)))
    m = dict(zip(WEIGHT_NAMES, (m_norm_w, m_ffn_w_gate, m_ffn_w_up, m_ffn_w_down, m_ssm_w_in, m_ssm_conv_w, m_ssm_conv_b, m_ssm_dt_bias, m_ssm_a_log, m_ssm_d, m_ssm_norm_w, m_ssm_w_out, m_kv_norm_w, m_w_k, m_b_k, m_w_v, m_b_v, m_attn_w_q, m_attn_b_q, m_attn_sinks, m_attn_w_o, m_attn_b_o, m_final_norm_w)))
    v = dict(zip(WEIGHT_NAMES, (v_norm_w, v_ffn_w_gate, v_ffn_w_up, v_ffn_w_down, v_ssm_w_in, v_ssm_conv_w, v_ssm_conv_b, v_ssm_dt_bias, v_ssm_a_log, v_ssm_d, v_ssm_norm_w, v_ssm_w_out, v_kv_norm_w, v_w_k, v_b_k, v_w_v, v_b_v, v_attn_w_q, v_attn_b_q, v_attn_sinks, v_attn_w_o, v_attn_b_o, v_final_norm_w)))
    loss, grad_x, grads, deltas, new_m, new_v = _step(x[0], loss_target[0], p, m, v)
    loss = lax.psum(loss[0, 0], ("x", "y", "c"))
    return (loss, grad_x[None], *[grads[n] for n in WEIGHT_NAMES], *[deltas[n] for n in WEIGHT_NAMES],
            *[new_m[n] for n in WEIGHT_NAMES], *[new_v[n] for n in WEIGHT_NAMES])
```

```python
import functools
import math

import jax
import jax.numpy as jnp
from jax import lax
from jax.experimental import pallas as pl
from jax.experimental.pallas import tpu as pltpu

F32 = jnp.float32
BF16 = jnp.bfloat16

N_DEV = 8
SEQ = 2048
D_MODEL = 1024
D_FF_SHARD = 352
N_FFN = 4
D_INNER = 2048
SSM_HEADS = 32
SSM_HEAD_DIM = 64
SSM_GROUPS = 4
HEADS_PER_GROUP = 8
SSM_STATE = 128
CHUNK = 128
N_CHUNKS = SEQ // CHUNK
GN = SSM_GROUPS * SSM_STATE
CONV_DIM = D_INNER + 2 * GN
CONV_WIDTH = 4
ZX_DIM = D_INNER + CONV_DIM
IN_PROJ_SHARD = 644
ATT_HEAD_DIM = 64
N_Q_HEADS = 16
N_KV_HEADS = 4
Q_PER_KV = 4
KV_DIM = N_KV_HEADS * ATT_HEAD_DIM
WINDOW = 128
ROPE_THETA = 10000.0
EPS = 1e-5
FFN_RES_WEIGHT = 0.5
ATT_SCALE = 1.0 / math.sqrt(ATT_HEAD_DIM)
NEG_BIG = -1e30

ADAM_LR = 0.001
ADAM_B1 = 0.9
ADAM_B2 = 0.999
ADAM_EPS = 1e-08
ADAM_WD = 0.01
ADAM_STEP = 10

VMEM_LIMIT_BYTES = 56 * 1024 * 1024
FFN_BWD_VMEM_LIMIT_BYTES = 61 * 1024 * 1024

NN = (((1,), (0,)), ((), ()))
NT = (((1,), (1,)), ((), ()))
TN = (((0,), (0,)), ((), ()))
_DIMS = {"nn": NN, "nt": NT, "tn": TN}


def _params(*sem):
    return pltpu.CompilerParams(dimension_semantics=sem if sem else None, vmem_limit_bytes=VMEM_LIMIT_BYTES)


def _dot(a, b, dims=NN):
    return lax.dot_general(a.astype(BF16), b.astype(BF16), dims, preferred_element_type=F32)


def _sigmoid(x):
    return 1.0 / (1.0 + jnp.exp(-x))


def _dsilu(x, s):
    return s * (1.0 + x * (1.0 - s))


def _rms(x):
    r = lax.rsqrt(jnp.mean(x * x, axis=-1, keepdims=True) + EPS)
    return x * r, r


def _sum_all(x):
    return jnp.sum(jnp.sum(x, axis=1, keepdims=True), axis=0, keepdims=True)


MESH = pl.DeviceIdType.MESH
N_PEERS = N_DEV - 1
N_CHIPS = N_DEV // 2


def _position():
    return lax.axis_index("x"), lax.axis_index("y"), lax.axis_index("c")


def _slot(p):
    return 4 * p[0] + 2 * p[1] + p[2]


class _Exchange:
    def __init__(self, arrays, out_shapes):
        n = len(arrays)
        self.arrays = list(arrays)
        self.out_shapes = out_shapes
        self.scratch = [pltpu.SemaphoreType.DMA((n, N_PEERS)), pltpu.SemaphoreType.DMA((n, N_PEERS)), pltpu.SemaphoreType.DMA((n,))]
        self.results = None

    def relay(self, ins, outs, sems):
        pass


class _Gather(_Exchange):
    def __init__(self, pieces):
        pieces = [p if isinstance(p, tuple) else (p, None) for p in pieces]
        self.blocks = [k for _, k in pieces]
        shapes = [a.shape if k is None else a.shape[1:] for a, k in pieces]
        super().__init__([a for a, _ in pieces], [jax.ShapeDtypeStruct((N_DEV,) + s, a.dtype) for s, (a, _) in zip(shapes, pieces)])

    def _plan(self, ins, outs, sems):
        send_sems, recv_sems, local_sems = sems
        x, y, c = _position()
        me, sibling = (x, y, c), (x, y, 1 - c)
        chips = [(1 - x, y), (x, 1 - y), (1 - x, 1 - y)]
        n = len(ins)
        ins = [r if k is None else r.at[k] for r, k in zip(ins, self.blocks)]

        def copy(a, k, block, to, src=None):
            dst = outs[a].at[_slot(block)]
            return pltpu.make_async_remote_copy(src_ref=dst if src is None else src, dst_ref=dst, send_sem=send_sems.at[a, k],
                                                recv_sem=recv_sems.at[a, k], device_id=to, device_id_type=MESH)

        mine = [pltpu.make_async_copy(ins[a], outs[a].at[_slot(me)], local_sems.at[a]) for a in range(n)]
        first = []
        for a in range(n):
            first.append(copy(a, 0, me, sibling, src=ins[a]))
            first += [copy(a, 1 + j, me, (*chip, c), src=ins[a]) for j, chip in enumerate(chips)]
        return n, c, me, sibling, chips, copy, mine, first

    def start(self, ins, outs, sems):
        _, _, _, _, _, _, mine, first = self._plan(ins, outs, sems)
        for cp in mine + first:
            cp.start()

    def relay(self, ins, outs, sems):
        n, c, me, sibling, chips, copy, _, _ = self._plan(ins, outs, sems)
        for j, chip in enumerate(chips):
            for a in range(n):
                copy(a, 1 + j, (*chip, c), me).wait_recv()
                copy(a, 4 + j, (*chip, c), sibling).start()

    def finish(self, ins, outs, sems):
        n, c, me, sibling, chips, copy, mine, first = self._plan(ins, outs, sems)
        passed = [copy(a, 4 + j, (*chip, c), sibling) for j, chip in enumerate(chips) for a in range(n)]
        for a in range(n):
            copy(a, 0, sibling, me).wait_recv()
            for j, chip in enumerate(chips):
                copy(a, 4 + j, (*chip, 1 - c), me).wait_recv()
        for cp in first + passed:
            cp.wait_send()
        for cp in mine:
            cp.wait()


class _PairSwap(_Exchange):
    def __init__(self, arrays):
        n = len(arrays)
        self.arrays = list(arrays)
        self.out_shapes = [jax.ShapeDtypeStruct((N_CHIPS,) + a.shape[1:], a.dtype) for a in arrays]
        self.scratch = [pltpu.SemaphoreType.DMA((n, N_CHIPS)), pltpu.SemaphoreType.DMA((n, N_CHIPS))]
        self.results = None

    def _plan(self, ins, outs, sems):
        send_sems, recv_sems = sems
        x, y, c = _position()
        return [pltpu.make_async_remote_copy(src_ref=ins[a].at[2 * q + 1 - c], dst_ref=outs[a].at[q], send_sem=send_sems.at[a, q],
                                             recv_sem=recv_sems.at[a, q], device_id=(x, y, 1 - c), device_id_type=MESH)
                for a in range(len(ins)) for q in range(N_CHIPS)]

    def start(self, ins, outs, sems):
        for cp in self._plan(ins, outs, sems):
            cp.start()

    def finish(self, ins, outs, sems):
        for cp in self._plan(ins, outs, sems):
            cp.wait()


class _ChipExchange(_Exchange):
    def __init__(self, arrays):
        n = len(arrays)
        self.arrays = list(arrays)
        self.out_shapes = [jax.ShapeDtypeStruct(a.shape, a.dtype) for a in arrays]
        self.scratch = [pltpu.SemaphoreType.DMA((n, 3)), pltpu.SemaphoreType.DMA((n, 3)), pltpu.SemaphoreType.DMA((n,))]
        self.results = None

    def _plan(self, ins, outs, sems):
        send_sems, recv_sems, local_sems = sems
        x, y, c = _position()
        here = 2 * x + y
        chips = [(1 - x, y), (x, 1 - y), (1 - x, 1 - y)]
        n = len(ins)

        def copy(a, k, src_slot, dst_slot):
            return pltpu.make_async_remote_copy(src_ref=ins[a].at[src_slot], dst_ref=outs[a].at[dst_slot], send_sem=send_sems.at[a, k],
                                                recv_sem=recv_sems.at[a, k], device_id=(*chips[k], c), device_id_type=MESH)

        there = [2 * qx + qy for qx, qy in chips]
        mine = [pltpu.make_async_copy(ins[a].at[here], outs[a].at[here], local_sems.at[a]) for a in range(n)]
        sends = [copy(a, k, there[k], here) for a in range(n) for k in range(3)]
        arrivals = lambda: [copy(a, k, here, there[k]) for a in range(n) for k in range(3)]
        return mine, sends, arrivals

    def start(self, ins, outs, sems):
        mine, sends, _ = self._plan(ins, outs, sems)
        for cp in mine + sends:
            cp.start()

    def finish(self, ins, outs, sems):
        mine, sends, arrivals = self._plan(ins, outs, sems)
        for cp in arrivals():
            cp.wait_recv()
        for cp in sends:
            cp.wait_send()
        for cp in mine:
            cp.wait()


def _call(body, *, name, grid, in_specs, out_specs, out_shape, args, scratch_shapes=(), sem=(), comm=(), aliases=None,
          vmem_limit=VMEM_LIMIT_BYTES):
    single = not isinstance(out_shape, (list, tuple))
    out_shape = [out_shape] if single else list(out_shape)
    out_specs = [out_specs] if single else list(out_specs)
    comms = list(comm or ())
    n_in, n_out, n_scr = len(args), len(out_shape), len(scratch_shapes)
    params = pltpu.CompilerParams(dimension_semantics=tuple(sem) if sem else None, vmem_limit_bytes=vmem_limit)
    if not comms:
        res = pl.pallas_call(body, name=name, grid=grid, in_specs=list(in_specs), out_specs=out_specs, out_shape=out_shape,
                             scratch_shapes=list(scratch_shapes), input_output_aliases=aliases or {}, compiler_params=params)(*args)
        return res[0] if single else res
    counts = [n_in] + [len(c.arrays) for c in comms] + [n_out] + [len(c.out_shapes) for c in comms] + [n_scr] + [len(c.scratch) for c in comms]
    nc = len(comms)

    def carried(*refs):
        pos, groups = 0, []
        for cnt in counts:
            groups.append(refs[pos:pos + cnt])
            pos += cnt
        ins, c_ins = groups[0], groups[1:1 + nc]
        outs, c_outs = groups[1 + nc], groups[2 + nc:2 + 2 * nc]
        scr, c_sems = groups[2 + 2 * nc], groups[3 + 2 * nc:]
        ids = [pl.program_id(d) for d in range(len(grid))]
        is_first = functools.reduce(jnp.logical_and, [i == 0 for i in ids])
        is_last = functools.reduce(jnp.logical_and, [i == g - 1 for i, g in zip(ids, grid)])

        @pl.when(is_first)
        def _():
            for q, c in enumerate(comms):
                c.start(c_ins[q], c_outs[q], c_sems[q])

        body(*ins, *outs, *scr)

        @pl.when(is_last)
        def _():
            for q, c in enumerate(comms):
                c.relay(c_ins[q], c_outs[q], c_sems[q])
                c.finish(c_ins[q], c_outs[q], c_sems[q])

    anyspec = pl.BlockSpec(memory_space=pl.ANY)
    c_arrays = [a for c in comms for a in c.arrays]
    c_shapes = [s for c in comms for s in c.out_shapes]
    res = pl.pallas_call(
        carried, name=name, grid=grid, in_specs=list(in_specs) + [anyspec] * len(c_arrays), out_specs=out_specs + [anyspec] * len(c_shapes),
        out_shape=out_shape + c_shapes, scratch_shapes=list(scratch_shapes) + [s for c in comms for s in c.scratch],
        input_output_aliases=aliases or {}, compiler_params=params)(*args, *c_arrays)
    pos = n_out
    for c in comms:
        c.results = list(res[pos:pos + len(c.out_shapes)])
        pos += len(c.out_shapes)
    return res[0] if single else list(res[:n_out])


def _run_exchange(comm, *, name):
    def body(*refs):
        n_ci, n_co = len(comm.arrays), len(comm.out_shapes)
        ins, outs, sems = refs[:n_ci], refs[n_ci:n_ci + n_co], refs[n_ci + n_co:]
        comm.start(ins, outs, sems)
        comm.relay(ins, outs, sems)
        comm.finish(ins, outs, sems)

    anyspec = pl.BlockSpec(memory_space=pl.ANY)
    comm.results = list(pl.pallas_call(
        body, name=name, in_specs=[anyspec] * len(comm.arrays), out_specs=[anyspec] * len(comm.out_shapes),
        out_shape=list(comm.out_shapes), scratch_shapes=list(comm.scratch))(*comm.arrays))
    return comm.results


def _mm(a, b, *, dims="nn", bias=None, res=None, out_dtype=F32, name, tm=1024, tn=1024, tk=1024, comm=None, b_rows=None,
        out_window=None, into=None, colsum_b=False):
    if dims == "tn":
        k_dim, m_dim = a.shape
    else:
        m_dim, k_dim = a.shape
    row0, n_rows = b_rows if b_rows is not None else (0, b.shape[0])
    n_dim = n_rows if dims == "nt" else b.shape[1]
    assert dims == "nt" or n_rows == k_dim, (name, a.shape, b.shape, b_rows)
    tm, tn, tk = min(tm, m_dim), min(tn, n_dim), min(tk, k_dim)
    assert m_dim % tm == 0 and n_dim % tn == 0 and k_dim % tk == 0, (name, a.shape, b.shape)
    nk = k_dim // tk
    a_spec = pl.BlockSpec((tk, tm), lambda i, j, k: (k, i)) if dims == "tn" else pl.BlockSpec((tm, tk), lambda i, j, k: (i, k))
    if dims == "nt":
        assert row0 % tn == 0
        b_spec = pl.BlockSpec((tn, tk), lambda i, j, k: (row0 // tn + j, k))
    else:
        assert row0 % tk == 0
        b_spec = pl.BlockSpec((tk, tn), lambda i, j, k: (row0 // tk + k, j))
    in_specs, args = [a_spec, b_spec], [a, b]
    if bias is not None:
        in_specs.append(pl.BlockSpec((1, tn), lambda i, j, k: (0, j)))
        args.append(bias)
    if res is not None:
        in_specs.append(pl.BlockSpec((tm, tn), lambda i, j, k: (i, j)))
        args.append(res)
    dn = _DIMS[dims]

    if colsum_b:
        assert dims == "tn" and m_dim == tm and into is None and out_window is None

    def body(*refs):
        a_ref, b_ref = refs[0], refs[1]
        acc_ref = refs[-1]
        o_ref = refs[-3] if colsum_b else refs[-2]
        k = pl.program_id(2)

        @pl.when(k == 0)
        def _():
            acc_ref[...] = jnp.zeros_like(acc_ref)
            if colsum_b:
                refs[-2][...] = jnp.zeros_like(refs[-2])

        acc_ref[...] += _dot(a_ref[...], b_ref[...], dn)
        if colsum_b:
            refs[-2][...] += jnp.sum(b_ref[...].astype(F32), axis=0, keepdims=True)

        @pl.when(k == nk - 1)
        def _():
            r = acc_ref[...]
            pos = 2
            if bias is not None:
                r = r + refs[pos][...]
                pos += 1
            if res is not None:
                r = r + refs[pos][...]
            o_ref[...] = r.astype(out_dtype)

    out_row0, out_rows = out_window if out_window is not None else (0, m_dim)
    assert out_row0 % tm == 0
    aliases = None
    if into is not None:
        assert into.shape == (out_rows, n_dim) and into.dtype == out_dtype
        in_specs.append(pl.BlockSpec(memory_space=pl.ANY))
        args.append(into)
        aliases = {len(args) - 1: 0}
    out_spec = pl.BlockSpec((tm, tn), lambda i, j, k: (out_row0 // tm + i, j))
    out_shape = jax.ShapeDtypeStruct((out_rows, n_dim), out_dtype)
    if colsum_b:
        out_spec = [out_spec, pl.BlockSpec((1, tn), lambda i, j, k: (0, j))]
        out_shape = [out_shape, jax.ShapeDtypeStruct((1, n_dim), F32)]
    return _call(
        body, name=name, grid=(m_dim // tm, n_dim // tn, nk), in_specs=in_specs, out_specs=out_spec, out_shape=out_shape,
        aliases=aliases, scratch_shapes=[pltpu.VMEM((tm, tn), F32)], sem=("parallel", "parallel", "arbitrary"), args=args, comm=comm)


def _mm_norm_bwd(a, b, x, nw, res, *, dims="nn", b_rows=None, add=None, name, tm=1024, tk=1024, comm=None):
    m_dim, k_dim = a.shape
    row0, n_rows = b_rows if b_rows is not None else (0, b.shape[0])
    d_dim = x.shape[1]
    tm, tk = min(tm, m_dim), min(tk, k_dim)
    assert m_dim % tm == 0 and k_dim % tk == 0 and (n_rows if dims == "nt" else b.shape[1]) == d_dim, (name, a.shape, b.shape)
    nk = k_dim // tk
    if dims == "nt":
        assert row0 % d_dim == 0
        b_spec = pl.BlockSpec((d_dim, tk), lambda i, k: (row0 // d_dim, k))
    else:
        assert row0 % tk == 0 and n_rows == k_dim
        b_spec = pl.BlockSpec((tk, d_dim), lambda i, k: (row0 // tk + k, 0))
    row = pl.BlockSpec((tm, d_dim), lambda i, k: (i, 0))
    vec = pl.BlockSpec((1, d_dim), lambda i, k: (0, 0))
    extra = ([add] if add is not None else []) + list(res)
    dn = _DIMS[dims]

    def body(*refs):
        a_ref, b_ref, x_ref, nw_ref = refs[:4]
        extra_refs = refs[4:4 + len(extra)]
        dx_ref, dnw_ref, dob_ref, acc_ref = refs[-4:]
        i, k = pl.program_id(0), pl.program_id(1)

        @pl.when(k == 0)
        def _():
            acc_ref[...] = jnp.zeros_like(acc_ref)

        @pl.when((i == 0) & (k == 0))
        def _():
            dnw_ref[...] = jnp.zeros_like(dnw_ref)

        acc_ref[...] += _dot(a_ref[...], b_ref[...], dn)

        @pl.when(k == nk - 1)
        def _():
            dh = acc_ref[...]
            rest = list(extra_refs)
            if add is not None:
                dh = dh + rest.pop(0)[...]
            xhat, r = _rms(x_ref[...])
            dxhat = dh * nw_ref[...]
            dx = r * (dxhat - xhat * jnp.mean(dxhat * xhat, axis=-1, keepdims=True))
            for rr in rest:
                dx = dx + rr[...]
            dx_ref[...] = dx
            dob_ref[...] = (FFN_RES_WEIGHT * dx).astype(BF16)
            dnw_ref[...] += jnp.sum(dh * xhat, axis=0, keepdims=True)

    return _call(
        body, name=name, grid=(m_dim // tm, nk),
        in_specs=[pl.BlockSpec((tm, tk), lambda i, k: (i, k)), b_spec, row, vec] + [row] * len(extra), out_specs=[row, vec, row],
        out_shape=[jax.ShapeDtypeStruct((m_dim, d_dim), F32), jax.ShapeDtypeStruct((1, d_dim), F32),
                   jax.ShapeDtypeStruct((m_dim, d_dim), BF16)],
        scratch_shapes=[pltpu.VMEM((tm, d_dim), F32)], sem=("arbitrary", "arbitrary"), args=[a, b, x, nw] + extra, comm=comm)


def _rope_rotate(x, cos_t, sin_t):
    rows, width = x.shape
    half = ATT_HEAD_DIM // 2
    lane = lax.broadcasted_iota(jnp.int32, (rows, width), 1)
    first = (lane % ATT_HEAD_DIM) < half
    rot = jnp.where(first, -pltpu.roll(x, width - half, 1), pltpu.roll(x, half, 1))
    reps = width // 128
    return x * jnp.tile(cos_t, (1, reps)) + rot * jnp.tile(sin_t, (1, reps))


def _norm_mm(x, nw, w, bias, *, name, tm=1024, tn=1024, comm=None, w_rows=None, rope=None):
    t_dim, d_dim = x.shape
    transposed = w_rows is not None
    n_dim = w_rows if transposed else w.shape[1]
    tn = min(tn, n_dim)
    assert t_dim % tm == 0 and n_dim % tn == 0
    has_bias = bias is not None
    w_spec = pl.BlockSpec((tn, d_dim), lambda i, j: (j, 0)) if transposed else pl.BlockSpec((d_dim, tn), lambda i, j: (0, j))
    dn = NT if transposed else NN
    in_specs = [pl.BlockSpec((tm, d_dim), lambda i, j: (i, 0)), pl.BlockSpec((1, d_dim), lambda i, j: (0, 0)), w_spec]
    args = [x, nw, w]
    if has_bias:
        in_specs.append(pl.BlockSpec((1, tn), lambda i, j: (0, j)))
        args.append(bias)
    if rope is not None:
        in_specs += [pl.BlockSpec((tm, LANES), lambda i, j: (i, 0))] * 2
        args += list(rope)

    def body(*refs):
        x_ref, nw_ref, w_ref = refs[:3]
        o_ref, h_ref = refs[-2], refs[-1]

        @pl.when(pl.program_id(1) == 0)
        def _():
            xhat, _ = _rms(x_ref[...])
            h_ref[...] = (xhat * nw_ref[...]).astype(BF16)

        r = _dot(h_ref[...], w_ref[...], dn)
        if has_bias:
            r = r + refs[3][...]
        if rope is not None:
            r = _rope_rotate(r, refs[-4][...], refs[-3][...])
        o_ref[...] = r

    return _call(
        body, name=name, grid=(t_dim // tm, n_dim // tn), in_specs=in_specs,
        out_specs=[pl.BlockSpec((tm, tn), lambda i, j: (i, j)), pl.BlockSpec((tm, d_dim), lambda i, j: (i, 0))],
        out_shape=[jax.ShapeDtypeStruct((t_dim, n_dim), F32), jax.ShapeDtypeStruct((t_dim, d_dim), BF16)],
        sem=("parallel", "arbitrary"), args=args, comm=comm)


def _norm_bwd(x, nw, dh, res, *, name, tm=512, comm=None):
    t_dim, d_dim = x.shape
    n_res = len(res)
    row = pl.BlockSpec((tm, d_dim), lambda i: (i, 0))
    vec = pl.BlockSpec((1, d_dim), lambda i: (0, 0))

    def body(*refs):
        x_ref, nw_ref, dh_ref = refs[:3]
        dx_ref, dnw_ref = refs[-2], refs[-1]
        xhat, r = _rms(x_ref[...])
        dh = dh_ref[...]
        dxhat = dh * nw_ref[...]
        dx = r * (dxhat - xhat * jnp.mean(dxhat * xhat, axis=-1, keepdims=True))
        for rr in refs[3:3 + n_res]:
            dx = dx + rr[...]
        dx_ref[...] = dx

        @pl.when(pl.program_id(0) == 0)
        def _():
            dnw_ref[...] = jnp.zeros_like(dnw_ref)

        dnw_ref[...] += jnp.sum(dh * xhat, axis=0, keepdims=True)

    return _call(
        body, name=name, grid=(t_dim // tm,), in_specs=[row, vec, row] + [row] * n_res,
        out_specs=[row, vec],
        out_shape=[jax.ShapeDtypeStruct((t_dim, d_dim), F32), jax.ShapeDtypeStruct((1, d_dim), F32)],
        sem=("arbitrary",), args=[x, nw, dh, *res], comm=comm)


FFN_ROW_TILE = 512
FFN_SHARDS_PER_STEP = 2
FFN_STEPS = N_DEV // FFN_SHARDS_PER_STEP
FFN_STEP_COLS = FFN_SHARDS_PER_STEP * D_FF_SHARD


def _ffn_step_view(w):
    return w.reshape(FFN_STEPS, FFN_STEP_COLS, w.shape[-1])


def _ffn_specs(t_dim, d_dim):
    full = pl.BlockSpec((t_dim, d_dim), lambda j: (0, 0))
    wspec = pl.BlockSpec((None, FFN_STEP_COLS, d_dim), lambda j: (j, 0, 0))
    pre = pl.BlockSpec((None, t_dim, FFN_STEP_COLS), lambda j: (j, 0, 0))
    return full, wspec, pre


def _ffn_fwd(x, nw, wg, wu, wd, *, name, comm=None):
    t_dim, d_dim = x.shape
    n_tiles = t_dim // FFN_ROW_TILE

    def body(x_ref, nw_ref, wg_ref, wu_ref, wd_ref, o_ref, g_ref, u_ref, h_ref):
        j = pl.program_id(0)

        @pl.when(j == 0)
        def _():
            xhat, _ = _rms(x_ref[...])
            h_ref[...] = (xhat * nw_ref[...]).astype(BF16)
            o_ref[...] = jnp.zeros_like(o_ref)

        for t in range(n_tiles):
            rows = pl.ds(t * FFN_ROW_TILE, FFN_ROW_TILE)
            h = h_ref[rows, :]
            g = _dot(h, wg_ref[...], NT)
            u = _dot(h, wu_ref[...], NT)
            g_ref[rows, :] = g.astype(BF16)
            u_ref[rows, :] = u.astype(BF16)
            o_ref[rows, :] += _dot(g * _sigmoid(g) * u, wd_ref[...])

        @pl.when(j == FFN_STEPS - 1)
        def _():
            o_ref[...] = x_ref[...] + FFN_RES_WEIGHT * o_ref[...]

    full, wspec, pre = _ffn_specs(t_dim, d_dim)
    pre_shape = jax.ShapeDtypeStruct((FFN_STEPS, t_dim, FFN_STEP_COLS), BF16)
    return _call(
        body, name=name, grid=(FFN_STEPS,),
        in_specs=[full, pl.BlockSpec((1, d_dim), lambda j: (0, 0)), wspec, wspec, wspec],
        out_specs=[full, pre, pre, full],
        out_shape=[jax.ShapeDtypeStruct((t_dim, d_dim), F32), pre_shape, pre_shape, jax.ShapeDtypeStruct((t_dim, d_dim), BF16)],
        sem=("arbitrary",), args=[x, nw, _ffn_step_view(wg), _ffn_step_view(wu), _ffn_step_view(wd)], comm=comm)


def _ffn_bwd(h, dob, pre_g, pre_u, wg, wu, wd, *, name, comm=None):
    t_dim, d_dim = h.shape
    n_tiles = t_dim // FFN_ROW_TILE

    def body(h_ref, dob_ref, g_ref, u_ref, wg_ref, wu_ref, wd_ref, dh_ref, gg_ref, gu_ref, gd_ref, dwg_scr, dwu_scr, dwd_scr):
        @pl.when(pl.program_id(0) == 0)
        def _():
            dh_ref[...] = jnp.zeros_like(dh_ref)

        for t in range(n_tiles):
            rows = pl.ds(t * FFN_ROW_TILE, FFN_ROW_TILE)
            hh = h_ref[rows, :]
            do = dob_ref[rows, :]
            g = g_ref[rows, :].astype(F32)
            u = u_ref[rows, :].astype(F32)
            sg = _sigmoid(g)
            s = g * sg
            da = _dot(do, wd_ref[...], NT)
            dwd = _dot(s * u, do, TN)
            du = (da * s).astype(BF16)
            dg = (da * u * _dsilu(g, sg)).astype(BF16)
            dwg = _dot(dg, hh, TN)
            dwu = _dot(du, hh, TN)
            if t == 0:
                dwd_scr[...] = dwd
                dwg_scr[...] = dwg
                dwu_scr[...] = dwu
            else:
                dwd_scr[...] += dwd
                dwg_scr[...] += dwg
                dwu_scr[...] += dwu
            dh_ref[rows, :] += _dot(dg, wg_ref[...]) + _dot(du, wu_ref[...])
        gg_ref[...] = dwg_scr[...].astype(BF16)
        gu_ref[...] = dwu_scr[...].astype(BF16)
        gd_ref[...] = dwd_scr[...].astype(BF16)

    full, wspec, pre = _ffn_specs(t_dim, d_dim)
    gspec = pl.BlockSpec((None, FFN_STEP_COLS, d_dim), lambda j: (j, 0, 0), pipeline_mode=pl.Buffered(1))
    grad_shape = jax.ShapeDtypeStruct((FFN_STEPS, FFN_STEP_COLS, d_dim), BF16)
    dh, gg, gu, gd = _call(
        body, name=name, grid=(FFN_STEPS,),
        in_specs=[full, full, pre, pre, wspec, wspec, wspec], out_specs=[full, gspec, gspec, gspec],
        out_shape=[jax.ShapeDtypeStruct((t_dim, d_dim), F32)] + [grad_shape] * 3,
        scratch_shapes=[pltpu.VMEM((FFN_STEP_COLS, d_dim), F32)] * 3, sem=("arbitrary",), vmem_limit=FFN_BWD_VMEM_LIMIT_BYTES,
        args=[h, dob, pre_g, pre_u, _ffn_step_view(wg), _ffn_step_view(wu), _ffn_step_view(wd)], comm=comm)
    return dh, gg.reshape(wg.shape), gu.reshape(wu.shape), gd.reshape(wd.shape)


CONV_COLS = 256


def _shift_down(u, s, rows):
    return jnp.where(rows >= s, pltpu.roll(u, s, 0), 0.0)


def _shift_up(u, s, rows, t_dim):
    return jnp.where(rows < t_dim - s, pltpu.roll(u, t_dim - s, 0), 0.0)


def _conv_pre(u, w_ref, b_ref, rows):
    c = b_ref[...] + w_ref[CONV_WIDTH - 1:CONV_WIDTH, :] * u
    for k in range(CONV_WIDTH - 1):
        c = c + w_ref[k:k + 1, :] * _shift_down(u, CONV_WIDTH - 1 - k, rows)
    return c


def _conv_fwd(zx, cw, cb, *, name, comm=None):
    t_dim = zx.shape[0]
    off = D_INNER // CONV_COLS

    def body(u_ref, w_ref, b_ref, o_ref):
        rows = lax.broadcasted_iota(jnp.int32, (t_dim, CONV_COLS), 0)
        c = _conv_pre(u_ref[...], w_ref, b_ref, rows)
        o_ref[...] = c * _sigmoid(c)

    return _call(
        body, name=name, grid=(CONV_DIM // CONV_COLS,),
        in_specs=[pl.BlockSpec((t_dim, CONV_COLS), lambda j: (0, off + j)),
                  pl.BlockSpec((CONV_WIDTH, CONV_COLS), lambda j: (0, j)), pl.BlockSpec((1, CONV_COLS), lambda j: (0, j))],
        out_specs=pl.BlockSpec((t_dim, CONV_COLS), lambda j: (0, j)),
        out_shape=jax.ShapeDtypeStruct((t_dim, CONV_DIM), F32), sem=("parallel",), args=[zx, cw, cb], comm=comm)


def _conv_bwd(zx, cw, cb, dxs, db, dc, dzx, *, name, comm=None):
    t_dim = zx.shape[0]
    off = D_INNER // CONV_COLS
    n_xs = D_INNER // CONV_COLS
    n_b = GN // CONV_COLS

    def body(u_ref, w_ref, b_ref, dxs_ref, db_ref, dc_ref, dzx_in, dzx_ref, dw_ref, dbias_ref):
        j = pl.program_id(0)
        rows = lax.broadcasted_iota(jnp.int32, (t_dim, CONV_COLS), 0)
        u = u_ref[...]
        c = _conv_pre(u, w_ref, b_ref, rows)
        d = jnp.where(j < n_xs, dxs_ref[...], jnp.where(j < n_xs + n_b, db_ref[...], dc_ref[...]))
        dcv = d * _dsilu(c, _sigmoid(c))
        dpre = w_ref[CONV_WIDTH - 1:CONV_WIDTH, :] * dcv
        dw_ref[CONV_WIDTH - 1:CONV_WIDTH, :] = jnp.sum(dcv * u, axis=0, keepdims=True)
        for k in range(CONV_WIDTH - 1):
            s = CONV_WIDTH - 1 - k
            dpre = dpre + w_ref[k:k + 1, :] * _shift_up(dcv, s, rows, t_dim)
            dw_ref[k:k + 1, :] = jnp.sum(dcv * _shift_down(u, s, rows), axis=0, keepdims=True)
        dzx_ref[...] = dpre
        dbias_ref[...] = jnp.sum(dcv, axis=0, keepdims=True)

    blk = lambda n: pl.BlockSpec((t_dim, CONV_COLS), n)
    return _call(
        body, name=name, grid=(CONV_DIM // CONV_COLS,),
        in_specs=[blk(lambda j: (0, off + j)), pl.BlockSpec((CONV_WIDTH, CONV_COLS), lambda j: (0, j)),
                  pl.BlockSpec((1, CONV_COLS), lambda j: (0, j)),
                  blk(lambda j: (0, jnp.minimum(j, n_xs - 1))),
                  blk(lambda j: (0, jnp.clip(j - n_xs, 0, n_b - 1))),
                  blk(lambda j: (0, jnp.clip(j - n_xs - n_b, 0, n_b - 1))),
                  pl.BlockSpec(memory_space=pl.ANY)],
        out_specs=[blk(lambda j: (0, off + j)), pl.BlockSpec((CONV_WIDTH, CONV_COLS), lambda j: (0, j)),
                   pl.BlockSpec((1, CONV_COLS), lambda j: (0, j))],
        out_shape=[jax.ShapeDtypeStruct(dzx.shape, F32), jax.ShapeDtypeStruct((CONV_WIDTH, CONV_DIM), F32),
                   jax.ShapeDtypeStruct((1, CONV_DIM), F32)],
        aliases={6: 0}, sem=("parallel",), args=[zx, cw, cb, dxs, db, dc, dzx], comm=comm)


def _softplus_parts(x):
    e = jnp.exp(-jnp.abs(x))
    u = 1.0 + e
    log1p_e = jnp.where(u == 1.0, e, jnp.log(u) * e / jnp.where(u == 1.0, 1.0, u - 1.0))
    return jnp.maximum(x, 0.0) + log1p_e


def _dt_prep(dtr, dt_bias, a_log, *, name):
    def body(dtr_ref, bias_ref, alog_ref, dt_ref, a_ref):
        dt = _softplus_parts(dtr_ref[...] + bias_ref[...])
        dt_ref[...] = dt
        a_ref[...] = dt * (-jnp.exp(alog_ref[...]))

    return pl.pallas_call(body, name=name, out_shape=[jax.ShapeDtypeStruct(dtr.shape, F32)] * 2,
                          compiler_params=_params())(dtr, dt_bias, a_log)


def _dt_bwd(dtr, dt_bias, a_log, dt, ddt, da, *, name):
    def body(dtr_ref, bias_ref, alog_ref, dt_ref, ddt_ref, da_ref, ddtr_ref, dbias_ref, dalog_ref):
        a_neg = -jnp.exp(alog_ref[...])
        da_v = da_ref[...]
        ddt_tot = ddt_ref[...] + da_v * a_neg
        ddtr = ddt_tot * _sigmoid(dtr_ref[...] + bias_ref[...])
        ddtr_ref[...] = ddtr
        dbias_ref[...] = jnp.sum(ddtr, axis=0, keepdims=True)
        dalog_ref[...] = jnp.sum(da_v * dt_ref[...], axis=0, keepdims=True) * a_neg

    return pl.pallas_call(
        body, name=name,
        out_shape=[jax.ShapeDtypeStruct(dtr.shape, F32), jax.ShapeDtypeStruct((1, SSM_HEADS), F32),
                   jax.ShapeDtypeStruct((1, SSM_HEADS), F32)],
        compiler_params=_params())(dtr, dt_bias, a_log, dt, ddt, da)


GROUP_COLS = HEADS_PER_GROUP * SSM_HEAD_DIM
LANES = 128
HEADS_PER_LANE_BLOCK = LANES // SSM_HEAD_DIM


def _split3(x):
    hi = x.astype(BF16)
    r1 = x - hi.astype(F32)
    mid = r1.astype(BF16)
    lo = (r1 - mid.astype(F32)).astype(BF16)
    return hi, mid, lo


def _dot_select(a, b, dims=NN, data=0):
    out = None
    for part in _split3(a if data == 0 else b):
        lhs, rhs = (part, b.astype(BF16)) if data == 0 else (a.astype(BF16), part)
        t = lax.dot_general(lhs, rhs, dims, preferred_element_type=F32)
        out = t if out is None else out + t
    return out


def _group_sums(vals, expand):
    out = _dot_select(jnp.concatenate(vals, axis=0), expand, NT)
    return [out[i * CHUNK:(i + 1) * CHUNK] for i in range(len(vals))]


def _ssd_chunk_common(a_ref, dt_ref, b_ref, c_ref):
    row = lax.broadcasted_iota(jnp.int32, (CHUNK, CHUNK), 0)
    col = lax.broadcasted_iota(jnp.int32, (CHUNK, CHUNK), 1)
    causal = col <= row
    lower = causal.astype(F32)
    upper = (col >= row).astype(F32)
    head = lax.broadcasted_iota(jnp.int32, (HEADS_PER_GROUP, GROUP_COLS), 0)
    lane = lax.broadcasted_iota(jnp.int32, (HEADS_PER_GROUP, GROUP_COLS), 1)
    expand = ((lane >= head * SSM_HEAD_DIM) & (lane < (head + 1) * SSM_HEAD_DIM)).astype(F32)
    a = a_ref[...]
    cs = _dot_select(lower, a, data=1)
    cs_row = _dot_select(a, upper, TN)
    cs_x = _dot_select(cs, expand)
    dt_x = _dot_select(dt_ref[...], expand)
    e_out_x = jnp.exp(cs_x)
    e_st_x = jnp.exp(cs_x[CHUNK - 1:CHUNK, :] - cs_x)
    bc = b_ref[...]
    cc = c_ref[...]
    cb = _dot(cc, bc, NT)
    return causal, upper, expand.astype(BF16), cs, cs_row, dt_x, e_out_x, e_st_x, bc, cc, cb


def _head_decay(causal, cs, cs_row, h):
    return jnp.exp(jnp.where(causal, cs[:, h:h + 1] - cs_row[h:h + 1, :], NEG_BIG))


def _lane_block_head_masks():
    lane = lax.broadcasted_iota(jnp.int32, (CHUNK, LANES), 1)
    return [(lane >= i * SSM_HEAD_DIM) & (lane < (i + 1) * SSM_HEAD_DIM) for i in range(HEADS_PER_LANE_BLOCK)]


def _decay_state(dst_ref, old, new, cs):
    for h in range(HEADS_PER_GROUP):
        rows = slice(h * SSM_HEAD_DIM, (h + 1) * SSM_HEAD_DIM)
        dst_ref[rows, :] = jnp.exp(cs[CHUNK - 1:CHUNK, h:h + 1]) * old[rows, :] + new[rows, :]


def _ssd_fwd(xbc, dtg, ag, dgx, *, name, comm=None):
    t_dim = xbc.shape[0]

    def body(xs_ref, b_ref, c_ref, dt_ref, a_ref, d_ref, y_ref, st_ref, s_scr):
        @pl.when(pl.program_id(1) == 0)
        def _():
            s_scr[...] = jnp.zeros_like(s_scr)

        causal, _, _, cs, cs_row, dt_x, e_out_x, e_st_x, bc, cc, cb = _ssd_chunk_common(a_ref, dt_ref, b_ref, c_ref)
        masks = _lane_block_head_masks()
        xs = xs_ref[...]
        xdt_x = xs * dt_x
        prev = s_scr[...]
        st_ref[...] = prev
        y_off = e_out_x * _dot(cc, prev, NT) + xs * d_ref[...]
        for blk in range(GROUP_COLS // LANES):
            lanes = slice(blk * LANES, (blk + 1) * LANES)
            x_b = xdt_x[:, lanes].astype(BF16)
            acc = y_off[:, lanes]
            for i in range(HEADS_PER_LANE_BLOCK):
                m = cb * _head_decay(causal, cs, cs_row, blk * HEADS_PER_LANE_BLOCK + i)
                acc = acc + _dot(m, jnp.where(masks[i], x_b, jnp.zeros_like(x_b)))
            y_ref[:, lanes] = acc
        _decay_state(s_scr, prev, _dot(xdt_x * e_st_x, bc, TN), cs)

    xs = pl.BlockSpec((CHUNK, GROUP_COLS), lambda g, c: (c, g))
    bsp = pl.BlockSpec((CHUNK, SSM_STATE), lambda g, c: (c, D_INNER // SSM_STATE + g))
    csp = pl.BlockSpec((CHUNK, SSM_STATE), lambda g, c: (c, (D_INNER + GN) // SSM_STATE + g))
    per_head = pl.BlockSpec((None, CHUNK, HEADS_PER_GROUP), lambda g, c: (g, c, 0))
    dsk = pl.BlockSpec((None, 1, GROUP_COLS), lambda g, c: (g, 0, 0))
    return _call(
        body, name=name, grid=(SSM_GROUPS, N_CHUNKS),
        in_specs=[xs, bsp, csp, per_head, per_head, dsk],
        out_specs=[xs, pl.BlockSpec((None, GROUP_COLS, SSM_STATE), lambda g, c: (c, g, 0))],
        out_shape=[jax.ShapeDtypeStruct((t_dim, D_INNER), F32),
                   jax.ShapeDtypeStruct((N_CHUNKS, D_INNER, SSM_STATE), F32)],
        scratch_shapes=[pltpu.VMEM((GROUP_COLS, SSM_STATE), F32)],
        sem=("parallel", "arbitrary"), args=[xbc, xbc, xbc, dtg, ag, dgx], comm=comm)


def _ssd_bwd(xbc, dtg, ag, dgx, states, dy, *, name, comm=None):
    t_dim = xbc.shape[0]
    last = N_CHUNKS - 1

    def body(xs_ref, b_ref, c_ref, dt_ref, a_ref, d_ref, st_ref, dy_ref,
             dxs_ref, db_ref, dc_ref, ddt_ref, da_ref, dd_ref, ds_scr):
        @pl.when(pl.program_id(1) == 0)
        def _():
            ds_scr[...] = jnp.zeros_like(ds_scr)
            dd_ref[...] = jnp.zeros_like(dd_ref)

        causal, upper, expand, cs, cs_row, dt_x, e_out_x, e_st_x, bc, cc, cb = _ssd_chunk_common(a_ref, dt_ref, b_ref, c_ref)
        masks = _lane_block_head_masks()
        xs = xs_ref[...]
        dy_x = dy_ref[...]
        xdt_x = xs * dt_x
        prev = st_ref[...]
        d_s = ds_scr[...]
        g1_x = _dot(bc, d_s, NT)
        cp_x = _dot(cc, prev, NT)
        d_cb = jnp.zeros((CHUNK, CHUNK), F32)
        lane8 = lax.broadcasted_iota(jnp.int32, (CHUNK, HEADS_PER_GROUP), 1)
        sub8 = lax.broadcasted_iota(jnp.int32, (HEADS_PER_GROUP, CHUNK), 0)
        row_w = jnp.zeros((CHUNK, HEADS_PER_GROUP), F32)
        col_w = jnp.zeros((HEADS_PER_GROUP, CHUNK), F32)
        dxdt_blocks = []
        for blk in range(GROUP_COLS // LANES):
            lanes = slice(blk * LANES, (blk + 1) * LANES)
            dy_b = dy_x[:, lanes].astype(BF16)
            x_b = xdt_x[:, lanes].astype(BF16)
            acc_dx = jnp.zeros((CHUNK, LANES), F32)
            for i in range(HEADS_PER_LANE_BLOCK):
                h = blk * HEADS_PER_LANE_BLOCK + i
                decay = _head_decay(causal, cs, cs_row, h)
                m = cb * decay
                dy_h = jnp.where(masks[i], dy_b, jnp.zeros_like(dy_b))
                acc_dx = acc_dx + _dot(m, dy_h, TN)
                d_m = _dot(dy_h, x_b, NT)
                d_cb = d_cb + d_m * decay
                w = d_m * m
                row_w = jnp.where(lane8 == h, jnp.sum(w, axis=1, keepdims=True), row_w)
                col_w = jnp.where(sub8 == h, jnp.sum(w, axis=0, keepdims=True), col_w)
            dxdt_blocks.append(acc_dx)
        dxdt_x = jnp.concatenate(dxdt_blocks, axis=1) + e_st_x * g1_x
        dxs_ref[...] = dxdt_x * dt_x + dy_x * d_ref[...]
        dye = dy_x * e_out_x
        xde = xdt_x * e_st_x
        ddt, y_off, tl, dskip = _group_sums([dxdt_x * xs, dye * cp_x, xde * g1_x, dy_x * xs], expand)
        ddt_ref[...] = ddt
        dd_ref[...] += jnp.sum(dskip, axis=0, keepdims=True)
        sp = None
        for part in _split3(d_s * prev):
            t = lax.dot_general(expand, part, NN, preferred_element_type=F32)
            sp = t if sp is None else sp + t
        last_col = jnp.exp(cs_row[:, CHUNK - 1:CHUNK]) * jnp.sum(sp, axis=1, keepdims=True)
        eye = lax.broadcasted_iota(jnp.int32, (HEADS_PER_GROUP, HEADS_PER_GROUP), 0) == lax.broadcasted_iota(
            jnp.int32, (HEADS_PER_GROUP, HEADS_PER_GROUP), 1)
        last_row = jnp.sum(jnp.where(eye, last_col, 0.0), axis=0, keepdims=True) + jnp.sum(tl, axis=0, keepdims=True)
        is_last = lax.broadcasted_iota(jnp.int32, (CHUNK, 1), 0) == CHUNK - 1
        d_cs = row_w + y_off - tl + jnp.where(is_last, last_row, 0.0)
        da_ref[...] = _dot_select(upper, d_cs, data=1) - _dot_select(upper, col_w, NT, data=1)
        dc_ref[...] = _dot(d_cb, bc) + _dot(dye, prev)
        db_ref[...] = _dot(d_cb, cc, TN) + _dot(xde, d_s)
        _decay_state(ds_scr, d_s, _dot(dye, cc, TN), cs)

    rev = lambda c: last - c
    xs = pl.BlockSpec((CHUNK, GROUP_COLS), lambda g, c: (rev(c), g))
    bsp = pl.BlockSpec((CHUNK, SSM_STATE), lambda g, c: (rev(c), D_INNER // SSM_STATE + g))
    csp = pl.BlockSpec((CHUNK, SSM_STATE), lambda g, c: (rev(c), (D_INNER + GN) // SSM_STATE + g))
    per_head = pl.BlockSpec((None, CHUNK, HEADS_PER_GROUP), lambda g, c: (g, rev(c), 0))
    dsk = pl.BlockSpec((None, 1, GROUP_COLS), lambda g, c: (g, 0, 0))
    dsum = pl.BlockSpec((None, 1, HEADS_PER_GROUP), lambda g, c: (g, 0, 0))
    st = pl.BlockSpec((None, GROUP_COLS, SSM_STATE), lambda g, c: (rev(c), g, 0))
    grp = pl.BlockSpec((CHUNK, SSM_STATE), lambda g, c: (rev(c), g))
    return _call(
        body, name=name, grid=(SSM_GROUPS, N_CHUNKS),
        in_specs=[xs, bsp, csp, per_head, per_head, dsk, st, xs],
        out_specs=[xs, grp, grp, per_head, per_head, dsum],
        out_shape=[jax.ShapeDtypeStruct((t_dim, D_INNER), F32), jax.ShapeDtypeStruct((t_dim, GN), F32),
                   jax.ShapeDtypeStruct((t_dim, GN), F32),
                   jax.ShapeDtypeStruct((SSM_GROUPS, t_dim, HEADS_PER_GROUP), F32),
                   jax.ShapeDtypeStruct((SSM_GROUPS, t_dim, HEADS_PER_GROUP), F32),
                   jax.ShapeDtypeStruct((SSM_GROUPS, 1, HEADS_PER_GROUP), F32)],
        scratch_shapes=[pltpu.VMEM((GROUP_COLS, SSM_STATE), F32)],
        sem=("parallel", "arbitrary"), args=[xbc, xbc, xbc, dtg, ag, dgx, states, dy], comm=comm)


NORM_GROUP = D_INNER // SSM_GROUPS


def _gate_norm_fwd(y, zx, nw, *, name, tm=256):
    t_dim = y.shape[0]
    row = pl.BlockSpec((tm, D_INNER), lambda i: (i, 0))

    def body(y_ref, z_ref, nw_ref, o_ref):
        z = z_ref[...]
        yz = y_ref[...] * (z * _sigmoid(z))
        for g in range(SSM_GROUPS):
            cols = slice(g * NORM_GROUP, (g + 1) * NORM_GROUP)
            yhat, _ = _rms(yz[:, cols])
            o_ref[:, cols] = (yhat * nw_ref[:, cols]).astype(BF16)

    return pl.pallas_call(
        body, name=name, grid=(t_dim // tm,), in_specs=[row, row, pl.BlockSpec((1, D_INNER), lambda i: (0, 0))],
        out_specs=row, out_shape=jax.ShapeDtypeStruct((t_dim, D_INNER), BF16),
        compiler_params=_params("parallel"),
    )(y, zx, nw)


def _gate_norm_bwd(y, zx, nw, dyn, *, name, tm=256):
    t_dim = y.shape[0]
    row = pl.BlockSpec((tm, D_INNER), lambda i: (i, 0))
    vec = pl.BlockSpec((1, D_INNER), lambda i: (0, 0))

    def body(y_ref, z_ref, nw_ref, dyn_ref, dy_ref, dz_ref, dnw_ref):
        @pl.when(pl.program_id(0) == 0)
        def _():
            dnw_ref[...] = jnp.zeros_like(dnw_ref)

        z = z_ref[...]
        yv = y_ref[...]
        sg = _sigmoid(z)
        silu_z = z * sg
        yz = yv * silu_z
        dyn_v = dyn_ref[...]
        for g in range(SSM_GROUPS):
            cols = slice(g * NORM_GROUP, (g + 1) * NORM_GROUP)
            yhat, r = _rms(yz[:, cols])
            dn = dyn_v[:, cols]
            dnw_ref[:, cols] += jnp.sum(dn * yhat, axis=0, keepdims=True)
            dyhat = dn * nw_ref[:, cols]
            dyz = r * (dyhat - yhat * jnp.mean(dyhat * yhat, axis=-1, keepdims=True))
            dy_ref[:, cols] = dyz * silu_z[:, cols]
            dz_ref[:, cols] = dyz * yv[:, cols] * _dsilu(z[:, cols], sg[:, cols])

    return pl.pallas_call(
        body, name=name, grid=(t_dim // tm,), in_specs=[row, row, vec, row],
        out_specs=[row, row, vec],
        out_shape=[jax.ShapeDtypeStruct((t_dim, D_INNER), F32), jax.ShapeDtypeStruct((t_dim, ZX_DIM), F32),
                   jax.ShapeDtypeStruct((1, D_INNER), F32)],
        compiler_params=_params("arbitrary"),
    )(y, zx, nw, dyn)


HEADS_PER_LANE_TILE = LANES // ATT_HEAD_DIM
STACKED_ROWS = Q_PER_KV * WINDOW


def _att_half_masks():
    lane = lax.broadcasted_iota(jnp.int32, (WINDOW, LANES), 1)
    return [(lane >= i * ATT_HEAD_DIM) & (lane < (i + 1) * ATT_HEAD_DIM) for i in range(HEADS_PER_LANE_TILE)]


def _att_stack_heads(ref, kvh, masks):
    parts = []
    for g in range(Q_PER_KV):
        h = kvh * Q_PER_KV + g
        blk = ref[:, (h // HEADS_PER_LANE_TILE) * LANES:(h // HEADS_PER_LANE_TILE + 1) * LANES]
        parts.append(jnp.where(masks[h % HEADS_PER_LANE_TILE], blk, jnp.zeros_like(blk)))
    return jnp.concatenate(parts, axis=0)


def _att_kv_tile(ref, kvh, masks):
    blk = ref[:, (kvh // HEADS_PER_LANE_TILE) * LANES:(kvh // HEADS_PER_LANE_TILE + 1) * LANES]
    return jnp.where(masks[kvh % HEADS_PER_LANE_TILE], blk, pltpu.roll(blk, ATT_HEAD_DIM, 1)).astype(BF16)


def _att_stacked_masks(n):
    row = lax.bitwise_and(lax.broadcasted_iota(jnp.int32, (STACKED_ROWS, WINDOW), 0), WINDOW - 1)
    col = lax.broadcasted_iota(jnp.int32, (STACKED_ROWS, WINDOW), 1)
    return col <= row, (col > row) & (n > 0)


def _att_stack_columns(ref, kvh, rows):
    cols = [ref[:, kvh * Q_PER_KV + g:kvh * Q_PER_KV + g + 1] for g in range(Q_PER_KV)]
    return jnp.concatenate([jnp.broadcast_to(c, (rows, 1)) for c in cols], axis=0)


def _att_scores(q4, k_tile, mask):
    return jnp.where(mask, _dot(q4, k_tile, NT) * ATT_SCALE, NEG_BIG)


def _att_unstack(x4, kvh, masks, tiles):
    for g in range(Q_PER_KV):
        h = kvh * Q_PER_KV + g
        piece = x4[g * WINDOW:(g + 1) * WINDOW]
        t = h // HEADS_PER_LANE_TILE
        tiles[t] = piece if h % HEADS_PER_LANE_TILE == 0 else jnp.where(masks[1], piece, tiles[t])


def _attn_fwd(q, k, v, sinks, *, name, comm=None):
    t_dim = q.shape[0]

    def body(q_ref, kc_ref, kp_ref, vc_ref, vp_ref, s_ref, o_ref, l_ref):
        n = pl.program_id(0)
        masks = _att_half_masks()
        mask_c, mask_p = _att_stacked_masks(n)
        out_tiles = [None] * (D_MODEL // LANES)
        for kvh in range(N_KV_HEADS):
            q4 = _att_stack_heads(q_ref, kvh, masks).astype(BF16)
            kc, kp = _att_kv_tile(kc_ref, kvh, masks), _att_kv_tile(kp_ref, kvh, masks)
            vc, vp = _att_kv_tile(vc_ref, kvh, masks), _att_kv_tile(vp_ref, kvh, masks)
            sc = _att_scores(q4, kc, mask_c)
            sp = _att_scores(q4, kp, mask_p)
            sink = _att_stack_columns(s_ref, kvh, WINDOW)
            m = jnp.maximum(jnp.maximum(jnp.max(sc, axis=1, keepdims=True), jnp.max(sp, axis=1, keepdims=True)), sink)
            pc = jnp.exp(sc - m)
            pp = jnp.exp(sp - m)
            den = jnp.sum(pc, axis=1, keepdims=True) + jnp.sum(pp, axis=1, keepdims=True) + jnp.exp(sink - m)
            _att_unstack((_dot(pc, vc) + _dot(pp, vp)) / den, kvh, masks, out_tiles)
            lse4 = m + jnp.log(den)
            for g in range(Q_PER_KV):
                h = kvh * Q_PER_KV + g
                l_ref[:, h:h + 1] = lse4[g * WINDOW:(g + 1) * WINDOW]
        for t, tile in enumerate(out_tiles):
            o_ref[:, t * LANES:(t + 1) * LANES] = tile

    cur = lambda w: pl.BlockSpec((WINDOW, w), lambda n: (n, 0))
    prv = lambda w: pl.BlockSpec((WINDOW, w), lambda n: (jnp.maximum(n - 1, 0), 0))
    return _call(
        body, name=name, grid=(t_dim // WINDOW,),
        in_specs=[cur(D_MODEL), cur(KV_DIM), prv(KV_DIM), cur(KV_DIM), prv(KV_DIM), pl.BlockSpec((1, N_Q_HEADS), lambda n: (0, 0))],
        out_specs=[cur(D_MODEL), cur(N_Q_HEADS)],
        out_shape=[jax.ShapeDtypeStruct((t_dim, D_MODEL), F32), jax.ShapeDtypeStruct((t_dim, N_Q_HEADS), F32)],
        sem=("parallel",), args=[q, k, k, v, v, sinks], comm=comm)


def _attn_bwd(q, k, v, sinks, o, lse, do, cos2, sin2, *, name, comm=None):
    t_dim = q.shape[0]

    def body(q_ref, kc_ref, kp_ref, vc_ref, vp_ref, s_ref, o_ref, l_ref, do_ref, cos_ref, sin_ref, cos_all_ref, sin_all_ref,
             dq_ref, dk_ref, dv_ref, dsink_ref):
        n = pl.program_id(0)

        @pl.when(n == 0)
        def _():
            dk_ref[...] = jnp.zeros_like(dk_ref)
            dv_ref[...] = jnp.zeros_like(dv_ref)
            dsink_ref[...] = jnp.zeros_like(dsink_ref)

        masks = _att_half_masks()
        mask_c, mask_p = _att_stacked_masks(n)
        lane_row = lax.broadcasted_iota(jnp.int32, (1, N_Q_HEADS), 1)
        rows_c = pl.ds(pl.multiple_of(n * WINDOW, WINDOW), WINDOW)
        rows_p = pl.ds(pl.multiple_of(jnp.maximum(n - 1, 0) * WINDOW, WINDOW), WINDOW)
        dsink = jnp.zeros((1, N_Q_HEADS), F32)
        dq_tiles = [None] * (D_MODEL // LANES)
        kv_tiles = KV_DIM // LANES
        dkc_tiles, dkp_tiles, dvc_tiles, dvp_tiles = ([None] * kv_tiles for _ in range(4))

        def place(tiles, kvh, x):
            folded = x + pltpu.roll(x, ATT_HEAD_DIM, 1)
            t = kvh // HEADS_PER_LANE_TILE
            tiles[t] = folded if kvh % HEADS_PER_LANE_TILE == 0 else jnp.where(masks[1], folded, tiles[t])

        for kvh in range(N_KV_HEADS):
            q4 = _att_stack_heads(q_ref, kvh, masks).astype(BF16)
            do4 = _att_stack_heads(do_ref, kvh, masks)
            o4 = _att_stack_heads(o_ref, kvh, masks)
            kc, kp = _att_kv_tile(kc_ref, kvh, masks), _att_kv_tile(kp_ref, kvh, masks)
            vc, vp = _att_kv_tile(vc_ref, kvh, masks), _att_kv_tile(vp_ref, kvh, masks)
            l4 = _att_stack_columns(l_ref, kvh, WINDOW)
            pc = jnp.exp(_att_scores(q4, kc, mask_c) - l4)
            pp = jnp.exp(_att_scores(q4, kp, mask_p) - l4)
            delta = jnp.sum(do4 * o4, axis=1, keepdims=True)
            do4b = do4.astype(BF16)
            dsc = pc * (_dot(do4b, vc, NT) - delta)
            dsp = pp * (_dot(do4b, vp, NT) - delta)
            _att_unstack((_dot(dsc, kc) + _dot(dsp, kp)) * ATT_SCALE, kvh, masks, dq_tiles)
            place(dkc_tiles, kvh, _dot(dsc, q4, TN) * ATT_SCALE)
            place(dkp_tiles, kvh, _dot(dsp, q4, TN) * ATT_SCALE)
            place(dvc_tiles, kvh, _dot(pc, do4b, TN))
            place(dvp_tiles, kvh, _dot(pp, do4b, TN))
            p_sink = jnp.exp(_att_stack_columns(s_ref, kvh, WINDOW) - l4) * delta
            for g in range(Q_PER_KV):
                h = kvh * Q_PER_KV + g
                dsink = jnp.where(lane_row == h, -jnp.sum(p_sink[g * WINDOW:(g + 1) * WINDOW], axis=0, keepdims=True), dsink)
        for t, tile in enumerate(dq_tiles):
            dq_ref[:, t * LANES:(t + 1) * LANES] = _rope_rotate(tile, cos_ref[...], -sin_ref[...])
        for t in range(kv_tiles):
            lanes = slice(t * LANES, (t + 1) * LANES)
            dk_ref[rows_c, lanes] += dkc_tiles[t]
            dk_ref[rows_p, lanes] += dkp_tiles[t]
            dv_ref[rows_c, lanes] += dvc_tiles[t]
            dv_ref[rows_p, lanes] += dvp_tiles[t]
        dsink_ref[...] += dsink

        @pl.when(n == t_dim // WINDOW - 1)
        def _():
            dk_ref[...] = _rope_rotate(dk_ref[...], cos_all_ref[...], -sin_all_ref[...])

    cur = lambda w: pl.BlockSpec((WINDOW, w), lambda n: (n, 0))
    prv = lambda w: pl.BlockSpec((WINDOW, w), lambda n: (jnp.maximum(n - 1, 0), 0))
    whole = lambda w: pl.BlockSpec((t_dim, w), lambda n: (0, 0))
    svec = pl.BlockSpec((1, N_Q_HEADS), lambda n: (0, 0))
    return _call(
        body, name=name, grid=(t_dim // WINDOW,),
        in_specs=[cur(D_MODEL), cur(KV_DIM), prv(KV_DIM), cur(KV_DIM), prv(KV_DIM), svec, cur(D_MODEL), cur(N_Q_HEADS), cur(D_MODEL),
                  cur(LANES), cur(LANES), whole(LANES), whole(LANES)],
        out_specs=[cur(D_MODEL), whole(KV_DIM), whole(KV_DIM), svec],
        out_shape=[jax.ShapeDtypeStruct((t_dim, D_MODEL), F32), jax.ShapeDtypeStruct((t_dim, KV_DIM), F32),
                   jax.ShapeDtypeStruct((t_dim, KV_DIM), F32), jax.ShapeDtypeStruct((1, N_Q_HEADS), F32)],
        sem=("arbitrary",), args=[q, k, k, v, v, sinks, o, lse, do, cos2, sin2, cos2, sin2], comm=comm)


def _loss_head(x, nw, target, *, name, tm=512):
    t_dim, d_dim = x.shape
    row = pl.BlockSpec((tm, d_dim), lambda i: (i, 0))
    vec = pl.BlockSpec((1, d_dim), lambda i: (0, 0))

    def body(x_ref, nw_ref, tgt_ref, loss_ref, dx_ref, dnw_ref, dob_ref):
        @pl.when(pl.program_id(0) == 0)
        def _():
            loss_ref[...] = jnp.zeros_like(loss_ref)
            dnw_ref[...] = jnp.zeros_like(dnw_ref)

        xhat, r = _rms(x_ref[...])
        err = xhat * nw_ref[...] - tgt_ref[...]
        loss_ref[...] += 0.5 * _sum_all(jnp.mean(err * err, axis=-1, keepdims=True))
        dy = err * (1.0 / d_dim)
        dnw_ref[...] += jnp.sum(dy * xhat, axis=0, keepdims=True)
        dxhat = dy * nw_ref[...]
        dx = r * (dxhat - xhat * jnp.mean(dxhat * xhat, axis=-1, keepdims=True))
        dx_ref[...] = dx
        dob_ref[...] = (FFN_RES_WEIGHT * dx).astype(BF16)

    return pl.pallas_call(
        body, name=name, grid=(t_dim // tm,), in_specs=[row, vec, row],
        out_specs=[pl.BlockSpec((1, 1), lambda i: (0, 0)), row, vec, row],
        out_shape=[jax.ShapeDtypeStruct((1, 1), F32), jax.ShapeDtypeStruct((t_dim, d_dim), F32),
                   jax.ShapeDtypeStruct((1, d_dim), F32), jax.ShapeDtypeStruct((t_dim, d_dim), BF16)],
        compiler_params=_params("arbitrary"),
    )(x, nw, target)


def _rope_tables():
    pos = jnp.arange(SEQ, dtype=F32)
    inv = 1.0 / (ROPE_THETA ** (jnp.arange(0, ATT_HEAD_DIM, 2, dtype=F32) / ATT_HEAD_DIM))
    ang = pos[:, None] * inv[None, :]
    cos, sin = jnp.cos(ang), jnp.sin(ang)
    return jnp.tile(cos, (1, 4)), jnp.tile(sin, (1, 4))


def _to_groups(t):
    return t.reshape(t.shape[0], SSM_GROUPS, HEADS_PER_GROUP).transpose(1, 0, 2)


def _from_groups(t):
    return t.transpose(1, 0, 2).reshape(t.shape[1], SSM_HEADS)


def _forward_backward(x0, target, net):
    w = net.w
    nw = [[w("norm_w")[l, i][None, :] for i in range(3)] for l in range(2)]
    cos2, sin2 = _rope_tables()
    ffn_norm = [nw[0][0], nw[0][2], nw[1][0], nw[1][2]]

    ffn_pre = {}

    def ffn_f(x, blk):
        name = f"ffn_fwd{blk}"
        out, *ffn_pre[blk] = _ffn_fwd(x, ffn_norm[blk], w(f"gate{blk}"), w(f"up{blk}"), w(f"down{blk}"), name=name,
                                      comm=net.carry(name))
        return out

    x1 = ffn_f(x0, 0)
    zx, h1 = _norm_mm(x1, nw[0][1], w("w_in_t"), None, w_rows=ZX_DIM, name="ssm_in_proj", comm=net.carry("ssm_in_proj"))
    dtr = _mm(h1, w("w_in_t"), dims="nt", b_rows=(ZX_DIM, SSM_HEADS), name="ssm_dt_proj")
    xbc = _conv_fwd(zx, w("conv_w"), w("conv_b"), name="ssm_conv_fwd", comm=net.carry("ssm_conv_fwd"))
    dt, a_dt = _dt_prep(dtr, w("dt_bias"), w("a_log"), name="ssm_dt_prep")
    dtg, ag = _to_groups(dt), _to_groups(a_dt)
    dg = jnp.repeat(w("d_skip").reshape(SSM_GROUPS, 1, HEADS_PER_GROUP), SSM_HEAD_DIM, axis=2)
    y_ssd, states = _ssd_fwd(xbc, dtg, ag, dg, name="ssd_fwd", comm=net.carry("ssd_fwd"))
    yn = _gate_norm_fwd(y_ssd, zx, w("ssm_norm_w"), name="ssm_gate_norm_fwd")
    x2 = _mm(yn, w("wout"), res=x1, name="ssm_out_proj", comm=net.carry("ssm_out_proj"))
    x3 = ffn_f(x2, 1)
    k_rot, hk = _norm_mm(x3, w("kv_norm_w"), w("wk"), w("b_k"), rope=(cos2, sin2), name="k_proj")
    v = _mm(hk, w("wv"), bias=w("b_v"), name="v_proj")
    x4 = ffn_f(x3, 2)
    q_rot, h4 = _norm_mm(x4, nw[1][1], w("wq"), w("b_q"), rope=(cos2, sin2), name="q_proj")
    att, lse = _attn_fwd(q_rot, k_rot, v, w("sinks"), name="attn_fwd", comm=net.carry("attn_fwd"))
    x5 = _mm(att, w("wo"), bias=w("b_o"), res=x4, name="attn_out_proj")
    x6 = ffn_f(x5, 3)
    loss, dx6, d_final, dob6 = _loss_head(x6, w("final_norm_w"), target, name="loss_head")

    d_norm = [[None] * 3 for _ in range(2)]

    def ffn_b(x, dout, dob, blk):
        pre_g, pre_u, h = ffn_pre[blk]
        name = f"ffn_bwd{blk}"
        dh, gg, gu, gd = _ffn_bwd(h, dob, pre_g, pre_u, w(f"gate{blk}"), w(f"up{blk}"), w(f"down{blk}"), name=name,
                                  comm=net.carry(name))
        net.give(f"gate{blk}", gg)
        net.give(f"up{blk}", gu)
        net.give(f"down{blk}", gd)
        return _norm_bwd(x, ffn_norm[blk], dh, [dout], name=f"ffn_norm_bwd{blk}", comm=net.carry(f"ffn_norm_bwd{blk}"))

    by_rows = lambda g: g.reshape(N_DEV, g.shape[0] // N_DEV, g.shape[1])
    dx5, d_norm[1][2] = ffn_b(x5, dx6, dob6, 3)
    d_att = _mm(dx5, w("wo"), dims="nt", name="attn_out_proj_dx", comm=net.carry("attn_out_proj_dx"))
    g_o, d_bo = _mm(att, dx5, dims="tn", out_dtype=BF16, colsum_b=True, name="attn_out_proj_dw")
    net.give("w_o", by_rows(g_o))
    dq, dk, dv, d_sinks = _attn_bwd(q_rot, k_rot, v, w("sinks"), att, lse, d_att, cos2, sin2, name="attn_bwd",
                                    comm=net.carry("attn_bwd"))
    dx4, d_norm[1][1], dob4 = _mm_norm_bwd(dq, w("wq"), x4, nw[1][1], [dx5], dims="nt", name="q_proj_dx")
    g_q, d_bq = _mm(h4, dq, dims="tn", out_dtype=BF16, colsum_b=True, name="q_proj_dw")
    net.give("w_q", by_rows(g_q))
    dx3a, d_norm[1][0] = ffn_b(x3, dx4, dob4, 2)
    dhk = _mm(dk, w("wk"), dims="nt", name="k_proj_dx", comm=net.carry("k_proj_dx"))
    dx3, d_kvn, dob3 = _mm_norm_bwd(dv, w("wv"), x3, w("kv_norm_w"), [dx3a], dims="nt", add=dhk, name="v_proj_dx")
    g_k, d_bk = _mm(hk, dk, dims="tn", out_dtype=BF16, colsum_b=True, name="k_proj_dw")
    g_v, d_bv = _mm(hk, dv, dims="tn", out_dtype=BF16, colsum_b=True, name="v_proj_dw")
    net.give("w_k", by_rows(g_k))
    net.give("w_v", by_rows(g_v))
    dx2, d_norm[0][2] = ffn_b(x2, dx3, dob3, 1)
    d_yn = _mm(dx2, w("wout"), dims="nt", name="ssm_out_proj_dx", comm=net.carry("ssm_out_proj_dx"))
    net.give("w_out", by_rows(_mm(yn, dx2, dims="tn", out_dtype=BF16, name="ssm_out_proj_dw")))
    dy_ssd, dzx, d_ssm_norm = _gate_norm_bwd(y_ssd, zx, w("ssm_norm_w"), d_yn, name="ssm_gate_norm_bwd")
    dxs, d_b, d_c, ddtg, dag, ddg = _ssd_bwd(xbc, dtg, ag, dg, states, dy_ssd, name="ssd_bwd", comm=net.carry("ssd_bwd"))
    dzx, d_conv_w, d_conv_b = _conv_bwd(zx, w("conv_w"), w("conv_b"), dxs, d_b, d_c, dzx, name="ssm_conv_bwd",
                                        comm=net.carry("ssm_conv_bwd"))
    ddtr, d_dt_bias, d_a_log = _dt_bwd(dtr, w("dt_bias"), w("a_log"), dt, _from_groups(ddtg), _from_groups(dag), name="ssm_dt_bwd")
    dh1 = _mm(dzx, w("w_in_t"), b_rows=(0, ZX_DIM), name="ssm_in_proj_dx", comm=net.carry("ssm_in_proj_dx"))
    in_rows = N_DEV * IN_PROJ_SHARD
    g_in = _mm(dzx, h1, dims="tn", out_dtype=BF16, out_window=(0, in_rows), name="ssm_in_proj_dw")
    g_in = _mm(ddtr, h1, dims="tn", out_dtype=BF16, out_window=(ZX_DIM, in_rows), into=g_in, name="ssm_dt_proj_dw")
    net.give("w_in", g_in.reshape(N_DEV, IN_PROJ_SHARD, D_MODEL))
    dx1, d_norm[0][1], dob1 = _mm_norm_bwd(ddtr, w("w_in_t"), x1, nw[0][1], [dx2], b_rows=(ZX_DIM, SSM_HEADS), add=dh1,
                                           name="ssm_dt_proj_dx", comm=net.carry("ssm_norm_bwd"))
    dx0, d_norm[0][0] = ffn_b(x0, dx1, dob1, 0)

    small = {"norm_w": jnp.concatenate([d_norm[l][i] for l in range(2) for i in range(3)], axis=0),
             "ssm_conv_w": d_conv_w, "ssm_conv_b": d_conv_b, "ssm_dt_bias": d_dt_bias, "ssm_a_log": d_a_log,
             "ssm_d": ddg.reshape(1, SSM_HEADS), "ssm_norm_w": d_ssm_norm, "kv_norm_w": d_kvn,
             "b_k": d_bk, "b_v": d_bv, "attn_b_q": d_bq, "attn_sinks": d_sinks, "attn_b_o": d_bo, "final_norm_w": d_final}
    return loss, dx0, small


BLOCK_BYTES = 1 << 20


def _row_tile(rows, cols):
    for t in (512, 256, 128, 64, 32, 16):
        if rows % t == 0 and t * cols * 4 <= BLOCK_BYTES:
            return t
    return rows


def _cast_bf16(x, *, name):
    n_blk, rows, cols = x.shape
    tm = rows if rows * cols * 4 <= 2 * BLOCK_BYTES else _row_tile(rows, cols)
    spec = pl.BlockSpec((None, tm, cols), lambda b, i: (b, i, 0))

    def body(x_ref, o_ref):
        o_ref[...] = x_ref[...].astype(BF16)

    return pl.pallas_call(body, name=name, grid=(n_blk, rows // tm), in_specs=[spec], out_specs=spec,
                          out_shape=jax.ShapeDtypeStruct(x.shape, BF16), compiler_params=_params("parallel", "parallel"))(x)


def _pair_add(grad, theirs, *, name):
    n_slots, rows, cols = theirs.shape
    tm = rows if rows * cols * 4 <= 2 * BLOCK_BYTES else _row_tile(rows, cols)

    def body(g_ref, t_ref, o_ref):
        o_ref[...] = (g_ref[...].astype(F32) + t_ref[...].astype(F32)).astype(BF16)

    spec = pl.BlockSpec((None, tm, cols), lambda s, i: (s, i, 0))
    return pl.pallas_call(
        body, name=name, grid=(n_slots, rows // tm),
        in_specs=[pl.BlockSpec((None, tm, cols), lambda s, i: (2 * s + lax.axis_index("c"), i, 0)), spec], out_specs=spec,
        out_shape=jax.ShapeDtypeStruct(theirs.shape, BF16), compiler_params=_params("parallel", "parallel"),
    )(grad, theirs)


def _adam_update(g, w, m, v):
    m = ADAM_B1 * m + (1.0 - ADAM_B1) * g
    v = ADAM_B2 * v + (1.0 - ADAM_B2) * (g * g)
    m_hat = m / (1.0 - ADAM_B1 ** ADAM_STEP)
    v_hat = v / (1.0 - ADAM_B2 ** ADAM_STEP)
    delta = -ADAM_LR * (m_hat / (jnp.sqrt(v_hat) + ADAM_EPS) + ADAM_WD * w)
    return delta, m, v


def _adamw(parts, w, m, v, first_blk, prev, *, name, comm=None):
    n_blk, rows, cols = w.shape
    tm = _row_tile(rows, cols)
    n_tiles = rows // tm
    spec = pl.BlockSpec((None, tm, cols), lambda b, i: (first_blk + b, i, 0))
    n_prev, n_here = len(prev), len(parts)
    n_parts = parts[0].shape[0]

    def part_spec(q):
        return pl.BlockSpec((n_parts, tm, cols), lambda b, i: (0, jnp.where(b < q, 0, jnp.where(b == q, i, n_tiles - 1)), 0))

    def body(*refs):
        p_refs = refs[:n_here]
        w_ref, m_ref, v_ref = refs[n_here:n_here + 3]
        g_ref, d_ref, nm_ref, nv_ref = refs[n_here + 3 + n_prev:]
        b = pl.program_id(0)
        g = None
        for s in range(n_parts):
            t = p_refs[0][s]
            for q in range(1, n_here):
                t = jnp.where(b == q, p_refs[q][s], t)
            g = t.astype(F32) if g is None else g + t.astype(F32)
        delta, nm, nv = _adam_update(g, w_ref[...], m_ref[...], v_ref[...])
        g_ref[...] = g
        d_ref[...] = delta
        nm_ref[...] = nm
        nv_ref[...] = nv

    return _call(
        body, name=name, grid=(n_here, n_tiles),
        in_specs=[part_spec(q) for q in range(n_here)] + [spec, spec, spec] + [pl.BlockSpec(memory_space=pl.ANY)] * n_prev,
        out_specs=[spec] * 4, out_shape=[jax.ShapeDtypeStruct((n_blk, rows, cols), F32)] * 4,
        aliases={n_here + 3 + q: q for q in range(n_prev)}, sem=("arbitrary", "arbitrary"),
        args=[*parts, w, m, v, *prev], comm=comm)


def _sum_parts(parts, *, name):
    def body(p_ref, o_ref):
        g = p_ref[0]
        for s in range(1, N_DEV):
            g = g + p_ref[s]
        o_ref[...] = g

    return pl.pallas_call(body, name=name, out_shape=jax.ShapeDtypeStruct(parts.shape[1:], F32), compiler_params=_params())(parts)


def _adamw_packed(g, w, m, v, *, name):
    def body(g_ref, w_ref, m_ref, v_ref, d_ref, nm_ref, nv_ref):
        delta, nm, nv = _adam_update(g_ref[...], w_ref[...], m_ref[...], v_ref[...])
        d_ref[...] = delta
        nm_ref[...] = nm
        nv_ref[...] = nv

    return pl.pallas_call(body, name=name, out_shape=[jax.ShapeDtypeStruct(g.shape, F32)] * 3, compiler_params=_params())(g, w, m, v)


SUBLANES = 8


WIDE_PACK = 1024


def _pack(arrs, width=LANES):
    rows = []
    for a in arrs:
        a2 = a.reshape(-1, a.shape[-1])
        a2 = jnp.pad(a2, ((0, 0), (0, (-a2.shape[1]) % width)))
        rows += [a2[:, i * width:(i + 1) * width] for i in range(a2.shape[1] // width)]
    out = jnp.concatenate(rows, axis=0)
    return jnp.pad(out, ((0, (-out.shape[0]) % SUBLANES), (0, 0)))


def _unpack(packed, shapes, width=LANES):
    outs, r = [], 0
    for shp in shapes:
        lead, cols = math.prod(shp[:-1]), shp[-1]
        n_blocks = -(-cols // width)
        blocks = [packed[r + i * lead:r + (i + 1) * lead] for i in range(n_blocks)]
        outs.append(jnp.concatenate(blocks, axis=1)[:, :cols].reshape(shp))
        r += n_blocks * lead
    return outs


WEIGHT_NAMES = ("norm_w", "ffn_w_gate", "ffn_w_up", "ffn_w_down", "ssm_w_in", "ssm_conv_w", "ssm_conv_b", "ssm_dt_bias",
                "ssm_a_log", "ssm_d", "ssm_norm_w", "ssm_w_out", "kv_norm_w", "w_k", "b_k", "w_v", "b_v", "attn_w_q",
                "attn_b_q", "attn_sinks", "attn_w_o", "attn_b_o", "final_norm_w")
MATRIX_NAMES = ("ffn_w_gate", "ffn_w_up", "ffn_w_down", "ssm_w_in", "ssm_w_out", "w_k", "w_v", "attn_w_q", "attn_w_o")
VECTOR_NAMES = tuple(n for n in WEIGHT_NAMES if n not in MATRIX_NAMES)
SHARDED_VECTORS = ("norm_w", "ssm_conv_w", "ssm_conv_b", "ssm_norm_w")


GATHER_PLAN = {
    "gather_stage0": ("gate0", "up0", "down0", "vec"),
    "ffn_fwd0": ("w_in",),
    "ssm_in_proj": ("w_out", "gate1"),
    "ssm_conv_fwd": ("w_k", "w_v", "up1"),
    "ssd_fwd": ("down1", "gate2"),
    "ffn_fwd1": ("up2", "down2"),
    "ffn_fwd2": ("up3", "w_q", "w_o"),
    "attn_fwd": ("gate3", "down3"),
}
PAIR_PLAN = {
    "attn_bwd": ("gate3", "up3", "down3"),
    "ffn_bwd2": ("w_q", "w_o"),
    "ffn_bwd1": ("gate2", "up2", "down2", "w_k", "w_v"),
    "ssd_bwd": ("gate1", "up1", "down1", "w_out"),
    "ssm_norm_bwd": ("w_in",),
    "ffn_norm_bwd0": ("gate0", "up0", "down0"),
}
CHIP_PLAN = {
    "ffn_bwd2": ("gate3", "up3", "down3"),
    "ssd_bwd": ("gate2", "up2", "down2", "w_q", "w_o", "w_k", "w_v"),
    "ssm_conv_bwd": ("gate1", "up1"),
    "ssm_in_proj_dx": ("w_out",),
    "ffn_bwd0": ("down1", "w_in"),
    "adamw_gate": ("gate0",),
    "adamw_up": ("up0",),
    "adamw_down": ("down0",),
}
FFN_PARAMS = {"gate": "ffn_w_gate", "up": "ffn_w_up", "down": "ffn_w_down"}
SINGLE_MATRICES = {"w_in": "ssm_w_in", "w_out": "ssm_w_out", "w_k": "w_k", "w_v": "w_v", "w_q": "attn_w_q", "w_o": "attn_w_o"}


TRANSPOSED = ("ffn_w_gate", "ffn_w_up", "ssm_w_in")


def _matrix_view(name, a):
    if name in TRANSPOSED:
        a = jnp.swapaxes(a, -1, -2)
    return a.reshape((-1,) + a.shape[-2:])


def _from_matrix_view(name, a, shape):
    if name in TRANSPOSED:
        return jnp.swapaxes(a.reshape(shape[:-2] + (shape[-1], shape[-2])), -1, -2)
    return a.reshape(shape)


class _MeshNet:
    def __init__(self, p):
        self.p = p
        self.views = {n: _matrix_view(n, p[n]) for n in MATRIX_NAMES}
        self.local = {"vec": _pack([p[n] for n in SHARDED_VECTORS])}
        for short, n in FFN_PARAMS.items():
            cast = _cast_bf16(self.views[n], name=f"cast_{short}")
            self.local.update({f"{short}{k}": (cast, k) for k in range(N_FFN)})
        for short, n in SINGLE_MATRICES.items():
            self.local[short] = (_cast_bf16(self.views[n], name=f"cast_{short}"), 0)
        self.gathered_at, self.pairs_at, self.parts_at, self.grads, self.cache = {}, {}, {}, {}, {}

    def carry(self, name):
        comms = []
        if name in GATHER_PLAN:
            keys, comm = GATHER_PLAN[name], _Gather([self.local[k] for k in GATHER_PLAN[name]])
            self.gathered_at.update({k: (comm, i) for i, k in enumerate(keys)})
            comms.append(comm)
        if name in CHIP_PLAN:
            sums = []
            for k in CHIP_PLAN[name]:
                comm, i = self.pairs_at[k]
                sums.append(_pair_add(self.grads[k], comm.results[i], name=f"pair_add_{k}"))
            comm = _ChipExchange(sums)
            self.parts_at.update({k: (comm, i) for i, k in enumerate(CHIP_PLAN[name])})
            comms.append(comm)
        if name in PAIR_PLAN:
            keys, comm = PAIR_PLAN[name], _PairSwap([self.grads[k] for k in PAIR_PLAN[name]])
            self.pairs_at.update({k: (comm, i) for i, k in enumerate(keys)})
            comms.append(comm)
        return comms

    def run(self, name):
        for comm in self.carry(name):
            _run_exchange(comm, name=name)

    def give(self, key, grad):
        self.grads[key] = grad

    def parts(self, key):
        comm, i = self.parts_at[key]
        return comm.results[i]

    def _gathered(self, key):
        comm, i = self.gathered_at[key]
        return comm.results[i]

    def _vec(self, r0, lead, n_blocks):
        vecs = self._gathered("vec")
        return jnp.concatenate([vecs[d, r0 + i * lead:r0 + (i + 1) * lead, :] for d in range(N_DEV) for i in range(n_blocks)], axis=1)

    def _derive(self, name):
        p = self.p
        if name[:-1] in FFN_PARAMS:
            return self._gathered(name)
        if name == "w_in_t":
            return self._gathered("w_in").reshape(N_DEV * IN_PROJ_SHARD, D_MODEL)
        by_rows = {"wout": "w_out", "wk": "w_k", "wv": "w_v", "wq": "w_q", "wo": "w_o"}
        if name in by_rows:
            g = self._gathered(by_rows[name])
            return g.reshape(N_DEV * g.shape[1], g.shape[2])
        vectors = {"norm_w": lambda: self._vec(0, 6, 1).reshape(2, 3, D_MODEL), "conv_w": lambda: self._vec(6, CONV_WIDTH, 3),
                   "conv_b": lambda: self._vec(18, 1, 3), "ssm_norm_w": lambda: self._vec(21, 1, 2)}
        if name in vectors:
            return vectors[name]()
        replicated = {"dt_bias": p["ssm_dt_bias"], "a_log": p["ssm_a_log"], "d_skip": p["ssm_d"], "kv_norm_w": p["kv_norm_w"][None],
                      "b_k": p["b_k"][None], "b_v": p["b_v"][None], "b_q": p["attn_b_q"], "sinks": p["attn_sinks"],
                      "b_o": p["attn_b_o"], "final_norm_w": p["final_norm_w"][None]}
        return replicated[name]

    def w(self, name):
        if name not in self.cache:
            self.cache[name] = self._derive(name)
        return self.cache[name]


def _step(x, target, p, m, v):
    pos = _slot(_position())
    net = _MeshNet(p)
    net.run("gather_stage0")
    loss, grad_x, small = _forward_backward(x, target, net)

    grads, deltas, new_m, new_v = {}, {}, {}, {}
    view = lambda d, n: _matrix_view(n, d[n])
    vec_gather = _Gather([_pack([small[n] for n in VECTOR_NAMES], WIDE_PACK)])
    for short, n in SINGLE_MATRICES.items():
        outs = _adamw([net.parts(short)], net.views[n], view(m, n), view(v, n), 0, [], name=f"adamw_{short}",
                      comm=[vec_gather] if short == "w_in" else None)
        grads[n], deltas[n], new_m[n], new_v[n] = [_from_matrix_view(n, o, p[n].shape) for o in outs]
    ffn_outs = {}
    for short, n in FFN_PARAMS.items():
        ffn_outs[short] = _adamw([net.parts(f"{short}{k}") for k in range(1, N_FFN)], net.views[n], view(m, n), view(v, n), 1, [],
                                 name=f"adamw_{short}", comm=net.carry(f"adamw_{short}"))
    for short, n in FFN_PARAMS.items():
        outs = _adamw([net.parts(f"{short}0")], net.views[n], view(m, n), view(v, n), 0, ffn_outs[short], name=f"adamw_{short}0")
        grads[n], deltas[n], new_m[n], new_v[n] = [_from_matrix_view(n, o, p[n].shape) for o in outs]
    vec_sum = _sum_parts(vec_gather.results[0], name="sum_vector_grads")
    full_shapes = {"norm_w": (2, 3, D_MODEL), "ssm_conv_w": (1, CONV_WIDTH, CONV_DIM), "ssm_conv_b": (1, CONV_DIM),
                   "ssm_norm_w": (1, D_INNER)}
    vec_full = dict(zip(VECTOR_NAMES, _unpack(vec_sum, [full_shapes.get(n, p[n].shape) for n in VECTOR_NAMES], WIDE_PACK)))
    for n in VECTOR_NAMES:
        g = vec_full[n]
        if n in SHARDED_VECTORS:
            per = p[n].shape[-1]
            g = lax.dynamic_slice_in_dim(g, pos * per, per, axis=g.ndim - 1)
        grads[n] = g
    packed = _adamw_packed(*[_pack([d[n] for n in VECTOR_NAMES], WIDE_PACK) for d in (grads, p, m, v)], name="adamw_vectors")
    shapes = [p[n].shape for n in VECTOR_NAMES]
    for d, pk in zip((deltas, new_m, new_v), packed):
        d.update(zip(VECTOR_NAMES, _unpack(pk, shapes, WIDE_PACK)))
    return loss, grad_x, grads, deltas, new_m, new_v


def kernel(x, norm_w, ffn_w_gate, ffn_w_up, ffn_w_down, ssm_w_in, ssm_conv_w, ssm_conv_b, ssm_dt_bias, ssm_a_log, ssm_d, ssm_norm_w, ssm_w_out, kv_norm_w, w_k, b_k, w_v, b_v, attn_w_q, attn_b_q, attn_sinks, attn_w_o, attn_b_o, final_norm_w, loss_target, m_norm_w, m_ffn_w_gate, m_ffn_w_up, m_ffn_w_down, m_ssm_w_in, m_ssm_conv_w, m_ssm_conv_b, m_ssm_dt_bias, m_ssm_a_log, m_ssm_d, m_ssm_norm_w, m_ssm_w_out, m_kv_norm_w, m_w_k, m_b_k, m_w_v, m_b_v, m_attn_w_q, m_attn_b_q, m_attn_sinks, m_attn_w_o, m_attn_b_o, m_final_norm_w, v_norm_w, v_ffn_w_gate, v_ffn_w_up, v_ffn_w_down, v_ssm_w_in, v_ssm_conv_w, v_ssm_conv_b, v_ssm_dt_bias, v_ssm_a_log, v_ssm_d, v_ssm_norm_w, v_ssm_w_out, v_kv_norm_w, v_w_k, v_b_k, v_w_v, v_b_v, v_attn_w_q, v_attn_b_q, v_attn_sinks, v_attn_w_o, v_attn_b_o, v_final_norm_w):
    p = dict(zip(WEIGHT_NAMES, (norm_w, ffn_w_gate, ffn_w_up, ffn_w_down, ssm_w_in, ssm_conv_w, ssm_conv_b, ssm_dt_bias, ssm_a_log, ssm_d, ssm_norm_w, ssm_w_out, kv_norm_w, w_k, b_k, w_v, b_v, attn_w_q, attn_b_q, attn_sinks, attn_w_o, attn_b_o, final_norm_w)))
    m = dict(zip(WEIGHT_NAMES, (m_norm_w, m_ffn_w_gate, m_ffn_w_up, m_ffn_w_down, m_ssm_w_in, m_ssm_conv_w, m_ssm_conv_b, m_ssm_dt_bias, m_ssm_a_log, m_ssm_d, m_ssm_norm_w, m_ssm_w_out, m_kv_norm_w, m_w_k, m_b_k, m_w_v, m_b_v, m_attn_w_q, m_attn_b_q, m_attn_sinks, m_attn_w_o, m_attn_b_o, m_final_norm_w)))
    v = dict(zip(WEIGHT_NAMES, (v_norm_w, v_ffn_w_gate, v_ffn_w_up, v_ffn_w_down, v_ssm_w_in, v_ssm_conv_w, v_ssm_conv_b, v_ssm_dt_bias, v_ssm_a_log, v_ssm_d, v_ssm_norm_w, v_ssm_w_out, v_kv_norm_w, v_w_k, v_b_k, v_w_v, v_b_v, v_attn_w_q, v_attn_b_q, v_attn_sinks, v_attn_w_o, v_attn_b_o, v_final_norm_w)))
    loss, grad_x, grads, deltas, new_m, new_v = _step(x[0], loss_target[0], p, m, v)
    loss = lax.psum(loss[0, 0], ("x", "y", "c"))
    return (loss, grad_x[None], *[grads[n] for n in WEIGHT_NAMES], *[deltas[n] for n in WEIGHT_NAMES],
            *[new_m[n] for n in WEIGHT_NAMES], *[new_v[n] for n in WEIGHT_NAMES])
```

```python
import functools
import math

import jax
import jax.numpy as jnp
from jax import lax
from jax.experimental import pallas as pl
from jax.experimental.pallas import tpu as pltpu

F32 = jnp.float32
BF16 = jnp.bfloat16

N_DEV = 8
SEQ = 2048
D_MODEL = 1024
D_FF_SHARD = 352
N_FFN = 4
D_INNER = 2048
SSM_HEADS = 32
SSM_HEAD_DIM = 64
SSM_GROUPS = 4
HEADS_PER_GROUP = 8
SSM_STATE = 128
CHUNK = 128
N_CHUNKS = SEQ // CHUNK
GN = SSM_GROUPS * SSM_STATE
CONV_DIM = D_INNER + 2 * GN
CONV_WIDTH = 4
ZX_DIM = D_INNER + CONV_DIM
IN_PROJ_SHARD = 644
ATT_HEAD_DIM = 64
N_Q_HEADS = 16
N_KV_HEADS = 4
Q_PER_KV = 4
KV_DIM = N_KV_HEADS * ATT_HEAD_DIM
WINDOW = 128
ROPE_THETA = 10000.0
EPS = 1e-5
FFN_RES_WEIGHT = 0.5
ATT_SCALE = 1.0 / math.sqrt(ATT_HEAD_DIM)
NEG_BIG = -1e30

ADAM_LR = 0.001
ADAM_B1 = 0.9
ADAM_B2 = 0.999
ADAM_EPS = 1e-08
ADAM_WD = 0.01
ADAM_STEP = 10

VMEM_LIMIT_BYTES = 56 * 1024 * 1024
FFN_BWD_VMEM_LIMIT_BYTES = 61 * 1024 * 1024

NN = (((1,), (0,)), ((), ()))
NT = (((1,), (1,)), ((), ()))
TN = (((0,), (0,)), ((), ()))
_DIMS = {"nn": NN, "nt": NT, "tn": TN}


def _params(*sem):
    return pltpu.CompilerParams(dimension_semantics=sem if sem else None, vmem_limit_bytes=VMEM_LIMIT_BYTES)


def _dot(a, b, dims=NN):
    return lax.dot_general(a.astype(BF16), b.astype(BF16), dims, preferred_element_type=F32)


def _sigmoid(x):
    return 1.0 / (1.0 + jnp.exp(-x))


def _dsilu(x, s):
    return s * (1.0 + x * (1.0 - s))


def _rms(x):
    r = lax.rsqrt(jnp.mean(x * x, axis=-1, keepdims=True) + EPS)
    return x * r, r


def _sum_all(x):
    return jnp.sum(jnp.sum(x, axis=1, keepdims=True), axis=0, keepdims=True)


MESH = pl.DeviceIdType.MESH
N_PEERS = N_DEV - 1
N_CHIPS = N_DEV // 2


def _position():
    return lax.axis_index("x"), lax.axis_index("y"), lax.axis_index("c")


def _slot(p):
    return 4 * p[0] + 2 * p[1] + p[2]


class _Exchange:
    def __init__(self, arrays, out_shapes):
        n = len(arrays)
        self.arrays = list(arrays)
        self.out_shapes = out_shapes
        self.scratch = [pltpu.SemaphoreType.DMA((n, N_PEERS)), pltpu.SemaphoreType.DMA((n, N_PEERS)), pltpu.SemaphoreType.DMA((n,))]
        self.results = None

    def relay(self, ins, outs, sems):
        pass


class _Gather(_Exchange):
    def __init__(self, pieces):
        pieces = [p if isinstance(p, tuple) else (p, None) for p in pieces]
        self.blocks = [k for _, k in pieces]
        shapes = [a.shape if k is None else a.shape[1:] for a, k in pieces]
        super().__init__([a for a, _ in pieces], [jax.ShapeDtypeStruct((N_DEV,) + s, a.dtype) for s, (a, _) in zip(shapes, pieces)])

    def _plan(self, ins, outs, sems):
        send_sems, recv_sems, local_sems = sems
        x, y, c = _position()
        me, sibling = (x, y, c), (x, y, 1 - c)
        chips = [(1 - x, y), (x, 1 - y), (1 - x, 1 - y)]
        n = len(ins)
        ins = [r if k is None else r.at[k] for r, k in zip(ins, self.blocks)]

        def copy(a, k, block, to, src=None):
            dst = outs[a].at[_slot(block)]
            return pltpu.make_async_remote_copy(src_ref=dst if src is None else src, dst_ref=dst, send_sem=send_sems.at[a, k],
                                                recv_sem=recv_sems.at[a, k], device_id=to, device_id_type=MESH)

        mine = [pltpu.make_async_copy(ins[a], outs[a].at[_slot(me)], local_sems.at[a]) for a in range(n)]
        first = []
        for a in range(n):
            first.append(copy(a, 0, me, sibling, src=ins[a]))
            first += [copy(a, 1 + j, me, (*chip, c), src=ins[a]) for j, chip in enumerate(chips)]
        return n, c, me, sibling, chips, copy, mine, first

    def start(self, ins, outs, sems):
        _, _, _, _, _, _, mine, first = self._plan(ins, outs, sems)
        for cp in mine + first:
            cp.start()

    def relay(self, ins, outs, sems):
        n, c, me, sibling, chips, copy, _, _ = self._plan(ins, outs, sems)
        for j, chip in enumerate(chips):
            for a in range(n):
                copy(a, 1 + j, (*chip, c), me).wait_recv()
                copy(a, 4 + j, (*chip, c), sibling).start()

    def finish(self, ins, outs, sems):
        n, c, me, sibling, chips, copy, mine, first = self._plan(ins, outs, sems)
        passed = [copy(a, 4 + j, (*chip, c), sibling) for j, chip in enumerate(chips) for a in range(n)]
        for a in range(n):
            copy(a, 0, sibling, me).wait_recv()
            for j, chip in enumerate(chips):
                copy(a, 4 + j, (*chip, 1 - c), me).wait_recv()
        for cp in first + passed:
            cp.wait_send()
        for cp in mine:
            cp.wait()


class _PairSwap(_Exchange):
    def __init__(self, arrays):
        n = len(arrays)
        self.arrays = list(arrays)
        self.out_shapes = [jax.ShapeDtypeStruct((N_CHIPS,) + a.shape[1:], a.dtype) for a in arrays]
        self.scratch = [pltpu.SemaphoreType.DMA((n, N_CHIPS)), pltpu.SemaphoreType.DMA((n, N_CHIPS))]
        self.results = None

    def _plan(self, ins, outs, sems):
        send_sems, recv_sems = sems
        x, y, c = _position()
        return [pltpu.make_async_remote_copy(src_ref=ins[a].at[2 * q + 1 - c], dst_ref=outs[a].at[q], send_sem=send_sems.at[a, q],
                                             recv_sem=recv_sems.at[a, q], device_id=(x, y, 1 - c), device_id_type=MESH)
                for a in range(len(ins)) for q in range(N_CHIPS)]

    def start(self, ins, outs, sems):
        for cp in self._plan(ins, outs, sems):
            cp.start()

    def finish(self, ins, outs, sems):
        for cp in self._plan(ins, outs, sems):
            cp.wait()


class _ChipExchange(_Exchange):
    def __init__(self, arrays):
        n = len(arrays)
        self.arrays = list(arrays)
        self.out_shapes = [jax.ShapeDtypeStruct(a.shape, a.dtype) for a in arrays]
        self.scratch = [pltpu.SemaphoreType.DMA((n, 3)), pltpu.SemaphoreType.DMA((n, 3)), pltpu.SemaphoreType.DMA((n,))]
        self.results = None

    def _plan(self, ins, outs, sems):
        send_sems, recv_sems, local_sems = sems
        x, y, c = _position()
        here = 2 * x + y
        chips = [(1 - x, y), (x, 1 - y), (1 - x, 1 - y)]
        n = len(ins)

        def copy(a, k, src_slot, dst_slot):
            return pltpu.make_async_remote_copy(src_ref=ins[a].at[src_slot], dst_ref=outs[a].at[dst_slot], send_sem=send_sems.at[a, k],
                                                recv_sem=recv_sems.at[a, k], device_id=(*chips[k], c), device_id_type=MESH)

        there = [2 * qx + qy for qx, qy in chips]
        mine = [pltpu.make_async_copy(ins[a].at[here], outs[a].at[here], local_sems.at[a]) for a in range(n)]
        sends = [copy(a, k, there[k], here) for a in range(n) for k in range(3)]
        arrivals = lambda: [copy(a, k, here, there[k]) for a in range(n) for k in range(3)]
        return mine, sends, arrivals

    def start(self, ins, outs, sems):
        mine, sends, _ = self._plan(ins, outs, sems)
        for cp in mine + sends:
            cp.start()

    def finish(self, ins, outs, sems):
        mine, sends, arrivals = self._plan(ins, outs, sems)
        for cp in arrivals():
            cp.wait_recv()
        for cp in sends:
            cp.wait_send()
        for cp in mine:
            cp.wait()


def _call(body, *, name, grid, in_specs, out_specs, out_shape, args, scratch_shapes=(), sem=(), comm=(), aliases=None,
          vmem_limit=VMEM_LIMIT_BYTES):
    single = not isinstance(out_shape, (list, tuple))
    out_shape = [out_shape] if single else list(out_shape)
    out_specs = [out_specs] if single else list(out_specs)
    comms = list(comm or ())
    n_in, n_out, n_scr = len(args), len(out_shape), len(scratch_shapes)
    params = pltpu.CompilerParams(dimension_semantics=tuple(sem) if sem else None, vmem_limit_bytes=vmem_limit)
    if not comms:
        res = pl.pallas_call(body, name=name, grid=grid, in_specs=list(in_specs), out_specs=out_specs, out_shape=out_shape,
                             scratch_shapes=list(scratch_shapes), input_output_aliases=aliases or {}, compiler_params=params)(*args)
        return res[0] if single else res
    counts = [n_in] + [len(c.arrays) for c in comms] + [n_out] + [len(c.out_shapes) for c in comms] + [n_scr] + [len(c.scratch) for c in comms]
    nc = len(comms)

    def carried(*refs):
        pos, groups = 0, []
        for cnt in counts:
            groups.append(refs[pos:pos + cnt])
            pos += cnt
        ins, c_ins = groups[0], groups[1:1 + nc]
        outs, c_outs = groups[1 + nc], groups[2 + nc:2 + 2 * nc]
        scr, c_sems = groups[2 + 2 * nc], groups[3 + 2 * nc:]
        ids = [pl.program_id(d) for d in range(len(grid))]
        is_first = functools.reduce(jnp.logical_and, [i == 0 for i in ids])
        is_last = functools.reduce(jnp.logical_and, [i == g - 1 for i, g in zip(ids, grid)])

        @pl.when(is_first)
        def _():
            for q, c in enumerate(comms):
                c.start(c_ins[q], c_outs[q], c_sems[q])

        body(*ins, *outs, *scr)

        @pl.when(is_last)
        def _():
            for q, c in enumerate(comms):
                c.relay(c_ins[q], c_outs[q], c_sems[q])
                c.finish(c_ins[q], c_outs[q], c_sems[q])

    anyspec = pl.BlockSpec(memory_space=pl.ANY)
    c_arrays = [a for c in comms for a in c.arrays]
    c_shapes = [s for c in comms for s in c.out_shapes]
    res = pl.pallas_call(
        carried, name=name, grid=grid, in_specs=list(in_specs) + [anyspec] * len(c_arrays), out_specs=out_specs + [anyspec] * len(c_shapes),
        out_shape=out_shape + c_shapes, scratch_shapes=list(scratch_shapes) + [s for c in comms for s in c.scratch],
        input_output_aliases=aliases or {}, compiler_params=params)(*args, *c_arrays)
    pos = n_out
    for c in comms:
        c.results = list(res[pos:pos + len(c.out_shapes)])
        pos += len(c.out_shapes)
    return res[0] if single else list(res[:n_out])


def _run_exchange(comm, *, name):
    def body(*refs):
        n_ci, n_co = len(comm.arrays), len(comm.out_shapes)
        ins, outs, sems = refs[:n_ci], refs[n_ci:n_ci + n_co], refs[n_ci + n_co:]
        comm.start(ins, outs, sems)
        comm.relay(ins, outs, sems)
        comm.finish(ins, outs, sems)

    anyspec = pl.BlockSpec(memory_space=pl.ANY)
    comm.results = list(pl.pallas_call(
        body, name=name, in_specs=[anyspec] * len(comm.arrays), out_specs=[anyspec] * len(comm.out_shapes),
        out_shape=list(comm.out_shapes), scratch_shapes=list(comm.scratch))(*comm.arrays))
    return comm.results


def _mm(a, b, *, dims="nn", bias=None, res=None, out_dtype=F32, name, tm=1024, tn=1024, tk=1024, comm=None, b_rows=None,
        out_window=None, into=None, colsum_b=False):
    if dims == "tn":
        k_dim, m_dim = a.shape
    else:
        m_dim, k_dim = a.shape
    row0, n_rows = b_rows if b_rows is not None else (0, b.shape[0])
    n_dim = n_rows if dims == "nt" else b.shape[1]
    assert dims == "nt" or n_rows == k_dim, (name, a.shape, b.shape, b_rows)
    tm, tn, tk = min(tm, m_dim), min(tn, n_dim), min(tk, k_dim)
    assert m_dim % tm == 0 and n_dim % tn == 0 and k_dim % tk == 0, (name, a.shape, b.shape)
    nk = k_dim // tk
    a_spec = pl.BlockSpec((tk, tm), lambda i, j, k: (k, i)) if dims == "tn" else pl.BlockSpec((tm, tk), lambda i, j, k: (i, k))
    if dims == "nt":
        assert row0 % tn == 0
        b_spec = pl.BlockSpec((tn, tk), lambda i, j, k: (row0 // tn + j, k))
    else:
        assert row0 % tk == 0
        b_spec = pl.BlockSpec((tk, tn), lambda i, j, k: (row0 // tk + k, j))
    in_specs, args = [a_spec, b_spec], [a, b]
    if bias is not None:
        in_specs.append(pl.BlockSpec((1, tn), lambda i, j, k: (0, j)))
        args.append(bias)
    if res is not None:
        in_specs.append(pl.BlockSpec((tm, tn), lambda i, j, k: (i, j)))
        args.append(res)
    dn = _DIMS[dims]

    if colsum_b:
        assert dims == "tn" and m_dim == tm and into is None and out_window is None

    def body(*refs):
        a_ref, b_ref = refs[0], refs[1]
        acc_ref = refs[-1]
        o_ref = refs[-3] if colsum_b else refs[-2]
        k = pl.program_id(2)

        @pl.when(k == 0)
        def _():
            acc_ref[...] = jnp.zeros_like(acc_ref)
            if colsum_b:
                refs[-2][...] = jnp.zeros_like(refs[-2])

        acc_ref[...] += _dot(a_ref[...], b_ref[...], dn)
        if colsum_b:
            refs[-2][...] += jnp.sum(b_ref[...].astype(F32), axis=0, keepdims=True)

        @pl.when(k == nk - 1)
        def _():
            r = acc_ref[...]
            pos = 2
            if bias is not None:
                r = r + refs[pos][...]
                pos += 1
            if res is not None:
                r = r + refs[pos][...]
            o_ref[...] = r.astype(out_dtype)

    out_row0, out_rows = out_window if out_window is not None else (0, m_dim)
    assert out_row0 % tm == 0
    aliases = None
    if into is not None:
        assert into.shape == (out_rows, n_dim) and into.dtype == out_dtype
        in_specs.append(pl.BlockSpec(memory_space=pl.ANY))
        args.append(into)
        aliases = {len(args) - 1: 0}
    out_spec = pl.BlockSpec((tm, tn), lambda i, j, k: (out_row0 // tm + i, j))
    out_shape = jax.ShapeDtypeStruct((out_rows, n_dim), out_dtype)
    if colsum_b:
        out_spec = [out_spec, pl.BlockSpec((1, tn), lambda i, j, k: (0, j))]
        out_shape = [out_shape, jax.ShapeDtypeStruct((1, n_dim), F32)]
    return _call(
        body, name=name, grid=(m_dim // tm, n_dim // tn, nk), in_specs=in_specs, out_specs=out_spec, out_shape=out_shape,
        aliases=aliases, scratch_shapes=[pltpu.VMEM((tm, tn), F32)], sem=("parallel", "parallel", "arbitrary"), args=args, comm=comm)


def _mm_norm_bwd(a, b, x, nw, res, *, dims="nn", b_rows=None, add=None, name, tm=1024, tk=1024, comm=None):
    m_dim, k_dim = a.shape
    row0, n_rows = b_rows if b_rows is not None else (0, b.shape[0])
    d_dim = x.shape[1]
    tm, tk = min(tm, m_dim), min(tk, k_dim)
    assert m_dim % tm == 0 and k_dim % tk == 0 and (n_rows if dims == "nt" else b.shape[1]) == d_dim, (name, a.shape, b.shape)
    nk = k_dim // tk
    if dims == "nt":
        assert row0 % d_dim == 0
        b_spec = pl.BlockSpec((d_dim, tk), lambda i, k: (row0 // d_dim, k))
    else:
        assert row0 % tk == 0 and n_rows == k_dim
        b_spec = pl.BlockSpec((tk, d_dim), lambda i, k: (row0 // tk + k, 0))
    row = pl.BlockSpec((tm, d_dim), lambda i, k: (i, 0))
    vec = pl.BlockSpec((1, d_dim), lambda i, k: (0, 0))
    extra = ([add] if add is not None else []) + list(res)
    dn = _DIMS[dims]

    def body(*refs):
        a_ref, b_ref, x_ref, nw_ref = refs[:4]
        extra_refs = refs[4:4 + len(extra)]
        dx_ref, dnw_ref, dob_ref, acc_ref = refs[-4:]
        i, k = pl.program_id(0), pl.program_id(1)

        @pl.when(k == 0)
        def _():
            acc_ref[...] = jnp.zeros_like(acc_ref)

        @pl.when((i == 0) & (k == 0))
        def _():
            dnw_ref[...] = jnp.zeros_like(dnw_ref)

        acc_ref[...] += _dot(a_ref[...], b_ref[...], dn)

        @pl.when(k == nk - 1)
        def _():
            dh = acc_ref[...]
            rest = list(extra_refs)
            if add is not None:
                dh = dh + rest.pop(0)[...]
            xhat, r = _rms(x_ref[...])
            dxhat = dh * nw_ref[...]
            dx = r * (dxhat - xhat * jnp.mean(dxhat * xhat, axis=-1, keepdims=True))
            for rr in rest:
                dx = dx + rr[...]
            dx_ref[...] = dx
            dob_ref[...] = (FFN_RES_WEIGHT * dx).astype(BF16)
            dnw_ref[...] += jnp.sum(dh * xhat, axis=0, keepdims=True)

    return _call(
        body, name=name, grid=(m_dim // tm, nk),
        in_specs=[pl.BlockSpec((tm, tk), lambda i, k: (i, k)), b_spec, row, vec] + [row] * len(extra), out_specs=[row, vec, row],
        out_shape=[jax.ShapeDtypeStruct((m_dim, d_dim), F32), jax.ShapeDtypeStruct((1, d_dim), F32),
                   jax.ShapeDtypeStruct((m_dim, d_dim), BF16)],
        scratch_shapes=[pltpu.VMEM((tm, d_dim), F32)], sem=("arbitrary", "arbitrary"), args=[a, b, x, nw] + extra, comm=comm)


def _rope_rotate(x, cos_t, sin_t):
    rows, width = x.shape
    half = ATT_HEAD_DIM // 2
    lane = lax.broadcasted_iota(jnp.int32, (rows, width), 1)
    first = (lane % ATT_HEAD_DIM) < half
    rot = jnp.where(first, -pltpu.roll(x, width - half, 1), pltpu.roll(x, half, 1))
    reps = width // 128
    return x * jnp.tile(cos_t, (1, reps)) + rot * jnp.tile(sin_t, (1, reps))


def _norm_mm(x, nw, w, bias, *, name, tm=1024, tn=1024, comm=None, w_rows=None, rope=None):
    t_dim, d_dim = x.shape
    transposed = w_rows is not None
    n_dim = w_rows if transposed else w.shape[1]
    tn = min(tn, n_dim)
    assert t_dim % tm == 0 and n_dim % tn == 0
    has_bias = bias is not None
    w_spec = pl.BlockSpec((tn, d_dim), lambda i, j: (j, 0)) if transposed else pl.BlockSpec((d_dim, tn), lambda i, j: (0, j))
    dn = NT if transposed else NN
    in_specs = [pl.BlockSpec((tm, d_dim), lambda i, j: (i, 0)), pl.BlockSpec((1, d_dim), lambda i, j: (0, 0)), w_spec]
    args = [x, nw, w]
    if has_bias:
        in_specs.append(pl.BlockSpec((1, tn), lambda i, j: (0, j)))
        args.append(bias)
    if rope is not None:
        in_specs += [pl.BlockSpec((tm, LANES), lambda i, j: (i, 0))] * 2
        args += list(rope)

    def body(*refs):
        x_ref, nw_ref, w_ref = refs[:3]
        o_ref, h_ref = refs[-2], refs[-1]

        @pl.when(pl.program_id(1) == 0)
        def _():
            xhat, _ = _rms(x_ref[...])
            h_ref[...] = (xhat * nw_ref[...]).astype(BF16)

        r = _dot(h_ref[...], w_ref[...], dn)
        if has_bias:
            r = r + refs[3][...]
        if rope is not None:
            r = _rope_rotate(r, refs[-4][...], refs[-3][...])
        o_ref[...] = r

    return _call(
        body, name=name, grid=(t_dim // tm, n_dim // tn), in_specs=in_specs,
        out_specs=[pl.BlockSpec((tm, tn), lambda i, j: (i, j)), pl.BlockSpec((tm, d_dim), lambda i, j: (i, 0))],
        out_shape=[jax.ShapeDtypeStruct((t_dim, n_dim), F32), jax.ShapeDtypeStruct((t_dim, d_dim), BF16)],
        sem=("parallel", "arbitrary"), args=args, comm=comm)


def _norm_bwd(x, nw, dh, res, *, name, tm=512, comm=None):
    t_dim, d_dim = x.shape
    n_res = len(res)
    row = pl.BlockSpec((tm, d_dim), lambda i: (i, 0))
    vec = pl.BlockSpec((1, d_dim), lambda i: (0, 0))

    def body(*refs):
        x_ref, nw_ref, dh_ref = refs[:3]
        dx_ref, dnw_ref = refs[-2], refs[-1]
        xhat, r = _rms(x_ref[...])
        dh = dh_ref[...]
        dxhat = dh * nw_ref[...]
        dx = r * (dxhat - xhat * jnp.mean(dxhat * xhat, axis=-1, keepdims=True))
        for rr in refs[3:3 + n_res]:
            dx = dx + rr[...]
        dx_ref[...] = dx

        @pl.when(pl.program_id(0) == 0)
        def _():
            dnw_ref[...] = jnp.zeros_like(dnw_ref)

        dnw_ref[...] += jnp.sum(dh * xhat, axis=0, keepdims=True)

    return _call(
        body, name=name, grid=(t_dim // tm,), in_specs=[row, vec, row] + [row] * n_res,
        out_specs=[row, vec],
        out_shape=[jax.ShapeDtypeStruct((t_dim, d_dim), F32), jax.ShapeDtypeStruct((1, d_dim), F32)],
        sem=("arbitrary",), args=[x, nw, dh, *res], comm=comm)


FFN_ROW_TILE = 512
FFN_SHARDS_PER_STEP = 2
FFN_STEPS = N_DEV // FFN_SHARDS_PER_STEP
FFN_STEP_COLS = FFN_SHARDS_PER_STEP * D_FF_SHARD


def _ffn_step_view(w):
    return w.reshape(FFN_STEPS, FFN_STEP_COLS, w.shape[-1])


def _ffn_specs(t_dim, d_dim):
    full = pl.BlockSpec((t_dim, d_dim), lambda j: (0, 0))
    wspec = pl.BlockSpec((None, FFN_STEP_COLS, d_dim), lambda j: (j, 0, 0))
    pre = pl.BlockSpec((None, t_dim, FFN_STEP_COLS), lambda j: (j, 0, 0))
    return full, wspec, pre


def _ffn_fwd(x, nw, wg, wu, wd, *, name, comm=None):
    t_dim, d_dim = x.shape
    n_tiles = t_dim // FFN_ROW_TILE

    def body(x_ref, nw_ref, wg_ref, wu_ref, wd_ref, o_ref, g_ref, u_ref, h_ref):
        j = pl.program_id(0)

        @pl.when(j == 0)
        def _():
            xhat, _ = _rms(x_ref[...])
            h_ref[...] = (xhat * nw_ref[...]).astype(BF16)
            o_ref[...] = jnp.zeros_like(o_ref)

        for t in range(n_tiles):
            rows = pl.ds(t * FFN_ROW_TILE, FFN_ROW_TILE)
            h = h_ref[rows, :]
            g = _dot(h, wg_ref[...], NT)
            u = _dot(h, wu_ref[...], NT)
            g_ref[rows, :] = g.astype(BF16)
            u_ref[rows, :] = u.astype(BF16)
            o_ref[rows, :] += _dot(g * _sigmoid(g) * u, wd_ref[...])

        @pl.when(j == FFN_STEPS - 1)
        def _():
            o_ref[...] = x_ref[...] + FFN_RES_WEIGHT * o_ref[...]

    full, wspec, pre = _ffn_specs(t_dim, d_dim)
    pre_shape = jax.ShapeDtypeStruct((FFN_STEPS, t_dim, FFN_STEP_COLS), BF16)
    return _call(
        body, name=name, grid=(FFN_STEPS,),
        in_specs=[full, pl.BlockSpec((1, d_dim), lambda j: (0, 0)), wspec, wspec, wspec],
        out_specs=[full, pre, pre, full],
        out_shape=[jax.ShapeDtypeStruct((t_dim, d_dim), F32), pre_shape, pre_shape, jax.ShapeDtypeStruct((t_dim, d_dim), BF16)],
        sem=("arbitrary",), args=[x, nw, _ffn_step_view(wg), _ffn_step_view(wu), _ffn_step_view(wd)], comm=comm)


def _ffn_bwd(h, dob, pre_g, pre_u, wg, wu, wd, *, name, comm=None):
    t_dim, d_dim = h.shape
    n_tiles = t_dim // FFN_ROW_TILE

    def body(h_ref, dob_ref, g_ref, u_ref, wg_ref, wu_ref, wd_ref, dh_ref, gg_ref, gu_ref, gd_ref, dwg_scr, dwu_scr, dwd_scr):
        @pl.when(pl.program_id(0) == 0)
        def _():
            dh_ref[...] = jnp.zeros_like(dh_ref)

        for t in range(n_tiles):
            rows = pl.ds(t * FFN_ROW_TILE, FFN_ROW_TILE)
            hh = h_ref[rows, :]
            do = dob_ref[rows, :]
            g = g_ref[rows, :].astype(F32)
            u = u_ref[rows, :].astype(F32)
            sg = _sigmoid(g)
            s = g * sg
            da = _dot(do, wd_ref[...], NT)
            dwd = _dot(s * u, do, TN)
            du = (da * s).astype(BF16)
            dg = (da * u * _dsilu(g, sg)).astype(BF16)
            dwg = _dot(dg, hh, TN)
            dwu = _dot(du, hh, TN)
            if t == 0:
                dwd_scr[...] = dwd
                dwg_scr[...] = dwg
                dwu_scr[...] = dwu
            else:
                dwd_scr[...] += dwd
                dwg_scr[...] += dwg
                dwu_scr[...] += dwu
            dh_ref[rows, :] += _dot(dg, wg_ref[...]) + _dot(du, wu_ref[...])
        gg_ref[...] = dwg_scr[...].astype(BF16)
        gu_ref[...] = dwu_scr[...].astype(BF16)
        gd_ref[...] = dwd_scr[...].astype(BF16)

    full, wspec, pre = _ffn_specs(t_dim, d_dim)
    gspec = pl.BlockSpec((None, FFN_STEP_COLS, d_dim), lambda j: (j, 0, 0), pipeline_mode=pl.Buffered(1))
    grad_shape = jax.ShapeDtypeStruct((FFN_STEPS, FFN_STEP_COLS, d_dim), BF16)
    dh, gg, gu, gd = _call(
        body, name=name, grid=(FFN_STEPS,),
        in_specs=[full, full, pre, pre, wspec, wspec, wspec], out_specs=[full, gspec, gspec, gspec],
        out_shape=[jax.ShapeDtypeStruct((t_dim, d_dim), F32)] + [grad_shape] * 3,
        scratch_shapes=[pltpu.VMEM((FFN_STEP_COLS, d_dim), F32)] * 3, sem=("arbitrary",), vmem_limit=FFN_BWD_VMEM_LIMIT_BYTES,
        args=[h, dob, pre_g, pre_u, _ffn_step_view(wg), _ffn_step_view(wu), _ffn_step_view(wd)], comm=comm)
    return dh, gg.reshape(wg.shape), gu.reshape(wu.shape), gd.reshape(wd.shape)


CONV_COLS = 256


def _shift_down(u, s, rows):
    return jnp.where(rows >= s, pltpu.roll(u, s, 0), 0.0)


def _shift_up(u, s, rows, t_dim):
    return jnp.where(rows < t_dim - s, pltpu.roll(u, t_dim - s, 0), 0.0)


def _conv_pre(u, w_ref, b_ref, rows):
    c = b_ref[...] + w_ref[CONV_WIDTH - 1:CONV_WIDTH, :] * u
    for k in range(CONV_WIDTH - 1):
        c = c + w_ref[k:k + 1, :] * _shift_down(u, CONV_WIDTH - 1 - k, rows)
    return c


def _conv_fwd(zx, cw, cb, *, name, comm=None):
    t_dim = zx.shape[0]
    off = D_INNER // CONV_COLS

    def body(u_ref, w_ref, b_ref, o_ref):
        rows = lax.broadcasted_iota(jnp.int32, (t_dim, CONV_COLS), 0)
        c = _conv_pre(u_ref[...], w_ref, b_ref, rows)
        o_ref[...] = c * _sigmoid(c)

    return _call(
        body, name=name, grid=(CONV_DIM // CONV_COLS,),
        in_specs=[pl.BlockSpec((t_dim, CONV_COLS), lambda j: (0, off + j)),
                  pl.BlockSpec((CONV_WIDTH, CONV_COLS), lambda j: (0, j)), pl.BlockSpec((1, CONV_COLS), lambda j: (0, j))],
        out_specs=pl.BlockSpec((t_dim, CONV_COLS), lambda j: (0, j)),
        out_shape=jax.ShapeDtypeStruct((t_dim, CONV_DIM), F32), sem=("parallel",), args=[zx, cw, cb], comm=comm)


def _conv_bwd(zx, cw, cb, dxs, db, dc, dzx, *, name, comm=None):
    t_dim = zx.shape[0]
    off = D_INNER // CONV_COLS
    n_xs = D_INNER // CONV_COLS
    n_b = GN // CONV_COLS

    def body(u_ref, w_ref, b_ref, dxs_ref, db_ref, dc_ref, dzx_in, dzx_ref, dw_ref, dbias_ref):
        j = pl.program_id(0)
        rows = lax.broadcasted_iota(jnp.int32, (t_dim, CONV_COLS), 0)
        u = u_ref[...]
        c = _conv_pre(u, w_ref, b_ref, rows)
        d = jnp.where(j < n_xs, dxs_ref[...], jnp.where(j < n_xs + n_b, db_ref[...], dc_ref[...]))
        dcv = d * _dsilu(c, _sigmoid(c))
        dpre = w_ref[CONV_WIDTH - 1:CONV_WIDTH, :] * dcv
        dw_ref[CONV_WIDTH - 1:CONV_WIDTH, :] = jnp.sum(dcv * u, axis=0, keepdims=True)
        for k in range(CONV_WIDTH - 1):
            s = CONV_WIDTH - 1 - k
            dpre = dpre + w_ref[k:k + 1, :] * _shift_up(dcv, s, rows, t_dim)
            dw_ref[k:k + 1, :] = jnp.sum(dcv * _shift_down(u, s, rows), axis=0, keepdims=True)
        dzx_ref[...] = dpre
        dbias_ref[...] = jnp.sum(dcv, axis=0, keepdims=True)

    blk = lambda n: pl.BlockSpec((t_dim, CONV_COLS), n)
    return _call(
        body, name=name, grid=(CONV_DIM // CONV_COLS,),
        in_specs=[blk(lambda j: (0, off + j)), pl.BlockSpec((CONV_WIDTH, CONV_COLS), lambda j: (0, j)),
                  pl.BlockSpec((1, CONV_COLS), lambda j: (0, j)),
                  blk(lambda j: (0, jnp.minimum(j, n_xs - 1))),
                  blk(lambda j: (0, jnp.clip(j - n_xs, 0, n_b - 1))),
                  blk(lambda j: (0, jnp.clip(j - n_xs - n_b, 0, n_b - 1))),
                  pl.BlockSpec(memory_space=pl.ANY)],
        out_specs=[blk(lambda j: (0, off + j)), pl.BlockSpec((CONV_WIDTH, CONV_COLS), lambda j: (0, j)),
                   pl.BlockSpec((1, CONV_COLS), lambda j: (0, j))],
        out_shape=[jax.ShapeDtypeStruct(dzx.shape, F32), jax.ShapeDtypeStruct((CONV_WIDTH, CONV_DIM), F32),
                   jax.ShapeDtypeStruct((1, CONV_DIM), F32)],
        aliases={6: 0}, sem=("parallel",), args=[zx, cw, cb, dxs, db, dc, dzx], comm=comm)


def _softplus_parts(x):
    e = jnp.exp(-jnp.abs(x))
    u = 1.0 + e
    log1p_e = jnp.where(u == 1.0, e, jnp.log(u) * e / jnp.where(u == 1.0, 1.0, u - 1.0))
    return jnp.maximum(x, 0.0) + log1p_e


def _dt_prep(dtr, dt_bias, a_log, *, name):
    def body(dtr_ref, bias_ref, alog_ref, dt_ref, a_ref):
        dt = _softplus_parts(dtr_ref[...] + bias_ref[...])
        dt_ref[...] = dt
        a_ref[...] = dt * (-jnp.exp(alog_ref[...]))

    return pl.pallas_call(body, name=name, out_shape=[jax.ShapeDtypeStruct(dtr.shape, F32)] * 2,
                          compiler_params=_params())(dtr, dt_bias, a_log)


def _dt_bwd(dtr, dt_bias, a_log, dt, ddt, da, *, name):
    def body(dtr_ref, bias_ref, alog_ref, dt_ref, ddt_ref, da_ref, ddtr_ref, dbias_ref, dalog_ref):
        a_neg = -jnp.exp(alog_ref[...])
        da_v = da_ref[...]
        ddt_tot = ddt_ref[...] + da_v * a_neg
        ddtr = ddt_tot * _sigmoid(dtr_ref[...] + bias_ref[...])
        ddtr_ref[...] = ddtr
        dbias_ref[...] = jnp.sum(ddtr, axis=0, keepdims=True)
        dalog_ref[...] = jnp.sum(da_v * dt_ref[...], axis=0, keepdims=True) * a_neg

    return pl.pallas_call(
        body, name=name,
        out_shape=[jax.ShapeDtypeStruct(dtr.shape, F32), jax.ShapeDtypeStruct((1, SSM_HEADS), F32),
                   jax.ShapeDtypeStruct((1, SSM_HEADS), F32)],
        compiler_params=_params())(dtr, dt_bias, a_log, dt, ddt, da)


GROUP_COLS = HEADS_PER_GROUP * SSM_HEAD_DIM
LANES = 128
HEADS_PER_LANE_BLOCK = LANES // SSM_HEAD_DIM


def _split3(x):
    hi = x.astype(BF16)
    r1 = x - hi.astype(F32)
    mid = r1.astype(BF16)
    lo = (r1 - mid.astype(F32)).astype(BF16)
    return hi, mid, lo


def _dot_select(a, b, dims=NN, data=0):
    out = None
    for part in _split3(a if data == 0 else b):
        lhs, rhs = (part, b.astype(BF16)) if data == 0 else (a.astype(BF16), part)
        t = lax.dot_general(lhs, rhs, dims, preferred_element_type=F32)
        out = t if out is None else out + t
    return out


def _group_sums(vals, expand):
    out = _dot_select(jnp.concatenate(vals, axis=0), expand, NT)
    return [out[i * CHUNK:(i + 1) * CHUNK] for i in range(len(vals))]


def _ssd_chunk_common(a_ref, dt_ref, b_ref, c_ref):
    row = lax.broadcasted_iota(jnp.int32, (CHUNK, CHUNK), 0)
    col = lax.broadcasted_iota(jnp.int32, (CHUNK, CHUNK), 1)
    causal = col <= row
    lower = causal.astype(F32)
    upper = (col >= row).astype(F32)
    head = lax.broadcasted_iota(jnp.int32, (HEADS_PER_GROUP, GROUP_COLS), 0)
    lane = lax.broadcasted_iota(jnp.int32, (HEADS_PER_GROUP, GROUP_COLS), 1)
    expand = ((lane >= head * SSM_HEAD_DIM) & (lane < (head + 1) * SSM_HEAD_DIM)).astype(F32)
    a = a_ref[...]
    cs = _dot_select(lower, a, data=1)
    cs_row = _dot_select(a, upper, TN)
    cs_x = _dot_select(cs, expand)
    dt_x = _dot_select(dt_ref[...], expand)
    e_out_x = jnp.exp(cs_x)
    e_st_x = jnp.exp(cs_x[CHUNK - 1:CHUNK, :] - cs_x)
    bc = b_ref[...]
    cc = c_ref[...]
    cb = _dot(cc, bc, NT)
    return causal, upper, expand.astype(BF16), cs, cs_row, dt_x, e_out_x, e_st_x, bc, cc, cb


def _head_decay(causal, cs, cs_row, h):
    return jnp.exp(jnp.where(causal, cs[:, h:h + 1] - cs_row[h:h + 1, :], NEG_BIG))


def _lane_block_head_masks():
    lane = lax.broadcasted_iota(jnp.int32, (CHUNK, LANES), 1)
    return [(lane >= i * SSM_HEAD_DIM) & (lane < (i + 1) * SSM_HEAD_DIM) for i in range(HEADS_PER_LANE_BLOCK)]


def _decay_state(dst_ref, old, new, cs):
    for h in range(HEADS_PER_GROUP):
        rows = slice(h * SSM_HEAD_DIM, (h + 1) * SSM_HEAD_DIM)
        dst_ref[rows, :] = jnp.exp(cs[CHUNK - 1:CHUNK, h:h + 1]) * old[rows, :] + new[rows, :]


def _ssd_fwd(xbc, dtg, ag, dgx, *, name, comm=None):
    t_dim = xbc.shape[0]

    def body(xs_ref, b_ref, c_ref, dt_ref, a_ref, d_ref, y_ref, st_ref, s_scr):
        @pl.when(pl.program_id(1) == 0)
        def _():
            s_scr[...] = jnp.zeros_like(s_scr)

        causal, _, _, cs, cs_row, dt_x, e_out_x, e_st_x, bc, cc, cb = _ssd_chunk_common(a_ref, dt_ref, b_ref, c_ref)
        masks = _lane_block_head_masks()
        xs = xs_ref[...]
        xdt_x = xs * dt_x
        prev = s_scr[...]
        st_ref[...] = prev
        y_off = e_out_x * _dot(cc, prev, NT) + xs * d_ref[...]
        for blk in range(GROUP_COLS // LANES):
            lanes = slice(blk * LANES, (blk + 1) * LANES)
            x_b = xdt_x[:, lanes].astype(BF16)
            acc = y_off[:, lanes]
            for i in range(HEADS_PER_LANE_BLOCK):
                m = cb * _head_decay(causal, cs, cs_row, blk * HEADS_PER_LANE_BLOCK + i)
                acc = acc + _dot(m, jnp.where(masks[i], x_b, jnp.zeros_like(x_b)))
            y_ref[:, lanes] = acc
        _decay_state(s_scr, prev, _dot(xdt_x * e_st_x, bc, TN), cs)

    xs = pl.BlockSpec((CHUNK, GROUP_COLS), lambda g, c: (c, g))
    bsp = pl.BlockSpec((CHUNK, SSM_STATE), lambda g, c: (c, D_INNER // SSM_STATE + g))
    csp = pl.BlockSpec((CHUNK, SSM_STATE), lambda g, c: (c, (D_INNER + GN) // SSM_STATE + g))
    per_head = pl.BlockSpec((None, CHUNK, HEADS_PER_GROUP), lambda g, c: (g, c, 0))
    dsk = pl.BlockSpec((None, 1, GROUP_COLS), lambda g, c: (g, 0, 0))
    return _call(
        body, name=name, grid=(SSM_GROUPS, N_CHUNKS),
        in_specs=[xs, bsp, csp, per_head, per_head, dsk],
        out_specs=[xs, pl.BlockSpec((None, GROUP_COLS, SSM_STATE), lambda g, c: (c, g, 0))],
        out_shape=[jax.ShapeDtypeStruct((t_dim, D_INNER), F32),
                   jax.ShapeDtypeStruct((N_CHUNKS, D_INNER, SSM_STATE), F32)],
        scratch_shapes=[pltpu.VMEM((GROUP_COLS, SSM_STATE), F32)],
        sem=("parallel", "arbitrary"), args=[xbc, xbc, xbc, dtg, ag, dgx], comm=comm)


def _ssd_bwd(xbc, dtg, ag, dgx, states, dy, *, name, comm=None):
    t_dim = xbc.shape[0]
    last = N_CHUNKS - 1

    def body(xs_ref, b_ref, c_ref, dt_ref, a_ref, d_ref, st_ref, dy_ref,
             dxs_ref, db_ref, dc_ref, ddt_ref, da_ref, dd_ref, ds_scr):
        @pl.when(pl.program_id(1) == 0)
        def _():
            ds_scr[...] = jnp.zeros_like(ds_scr)
            dd_ref[...] = jnp.zeros_like(dd_ref)

        causal, upper, expand, cs, cs_row, dt_x, e_out_x, e_st_x, bc, cc, cb = _ssd_chunk_common(a_ref, dt_ref, b_ref, c_ref)
        masks = _lane_block_head_masks()
        xs = xs_ref[...]
        dy_x = dy_ref[...]
        xdt_x = xs * dt_x
        prev = st_ref[...]
        d_s = ds_scr[...]
        g1_x = _dot(bc, d_s, NT)
        cp_x = _dot(cc, prev, NT)
        d_cb = jnp.zeros((CHUNK, CHUNK), F32)
        lane8 = lax.broadcasted_iota(jnp.int32, (CHUNK, HEADS_PER_GROUP), 1)
        sub8 = lax.broadcasted_iota(jnp.int32, (HEADS_PER_GROUP, CHUNK), 0)
        row_w = jnp.zeros((CHUNK, HEADS_PER_GROUP), F32)
        col_w = jnp.zeros((HEADS_PER_GROUP, CHUNK), F32)
        dxdt_blocks = []
        for blk in range(GROUP_COLS // LANES):
            lanes = slice(blk * LANES, (blk + 1) * LANES)
            dy_b = dy_x[:, lanes].astype(BF16)
            x_b = xdt_x[:, lanes].astype(BF16)
            acc_dx = jnp.zeros((CHUNK, LANES), F32)
            for i in range(HEADS_PER_LANE_BLOCK):
                h = blk * HEADS_PER_LANE_BLOCK + i
                decay = _head_decay(causal, cs, cs_row, h)
                m = cb * decay
                dy_h = jnp.where(masks[i], dy_b, jnp.zeros_like(dy_b))
                acc_dx = acc_dx + _dot(m, dy_h, TN)
                d_m = _dot(dy_h, x_b, NT)
                d_cb = d_cb + d_m * decay
                w = d_m * m
                row_w = jnp.where(lane8 == h, jnp.sum(w, axis=1, keepdims=True), row_w)
                col_w = jnp.where(sub8 == h, jnp.sum(w, axis=0, keepdims=True), col_w)
            dxdt_blocks.append(acc_dx)
        dxdt_x = jnp.concatenate(dxdt_blocks, axis=1) + e_st_x * g1_x
        dxs_ref[...] = dxdt_x * dt_x + dy_x * d_ref[...]
        dye = dy_x * e_out_x
        xde = xdt_x * e_st_x
        ddt, y_off, tl, dskip = _group_sums([dxdt_x * xs, dye * cp_x, xde * g1_x, dy_x * xs], expand)
        ddt_ref[...] = ddt
        dd_ref[...] += jnp.sum(dskip, axis=0, keepdims=True)
        sp = None
        for part in _split3(d_s * prev):
            t = lax.dot_general(expand, part, NN, preferred_element_type=F32)
            sp = t if sp is None else sp + t
        last_col = jnp.exp(cs_row[:, CHUNK - 1:CHUNK]) * jnp.sum(sp, axis=1, keepdims=True)
        eye = lax.broadcasted_iota(jnp.int32, (HEADS_PER_GROUP, HEADS_PER_GROUP), 0) == lax.broadcasted_iota(
            jnp.int32, (HEADS_PER_GROUP, HEADS_PER_GROUP), 1)
        last_row = jnp.sum(jnp.where(eye, last_col, 0.0), axis=0, keepdims=True) + jnp.sum(tl, axis=0, keepdims=True)
        is_last = lax.broadcasted_iota(jnp.int32, (CHUNK, 1), 0) == CHUNK - 1
        d_cs = row_w + y_off - tl + jnp.where(is_last, last_row, 0.0)
        da_ref[...] = _dot_select(upper, d_cs, data=1) - _dot_select(upper, col_w, NT, data=1)
        dc_ref[...] = _dot(d_cb, bc) + _dot(dye, prev)
        db_ref[...] = _dot(d_cb, cc, TN) + _dot(xde, d_s)
        _decay_state(ds_scr, d_s, _dot(dye, cc, TN), cs)

    rev = lambda c: last - c
    xs = pl.BlockSpec((CHUNK, GROUP_COLS), lambda g, c: (rev(c), g))
    bsp = pl.BlockSpec((CHUNK, SSM_STATE), lambda g, c: (rev(c), D_INNER // SSM_STATE + g))
    csp = pl.BlockSpec((CHUNK, SSM_STATE), lambda g, c: (rev(c), (D_INNER + GN) // SSM_STATE + g))
    per_head = pl.BlockSpec((None, CHUNK, HEADS_PER_GROUP), lambda g, c: (g, rev(c), 0))
    dsk = pl.BlockSpec((None, 1, GROUP_COLS), lambda g, c: (g, 0, 0))
    dsum = pl.BlockSpec((None, 1, HEADS_PER_GROUP), lambda g, c: (g, 0, 0))
    st = pl.BlockSpec((None, GROUP_COLS, SSM_STATE), lambda g, c: (rev(c), g, 0))
    grp = pl.BlockSpec((CHUNK, SSM_STATE), lambda g, c: (rev(c), g))
    return _call(
        body, name=name, grid=(SSM_GROUPS, N_CHUNKS),
        in_specs=[xs, bsp, csp, per_head, per_head, dsk, st, xs],
        out_specs=[xs, grp, grp, per_head, per_head, dsum],
        out_shape=[jax.ShapeDtypeStruct((t_dim, D_INNER), F32), jax.ShapeDtypeStruct((t_dim, GN), F32),
                   jax.ShapeDtypeStruct((t_dim, GN), F32),
                   jax.ShapeDtypeStruct((SSM_GROUPS, t_dim, HEADS_PER_GROUP), F32),
                   jax.ShapeDtypeStruct((SSM_GROUPS, t_dim, HEADS_PER_GROUP), F32),
                   jax.ShapeDtypeStruct((SSM_GROUPS, 1, HEADS_PER_GROUP), F32)],
        scratch_shapes=[pltpu.VMEM((GROUP_COLS, SSM_STATE), F32)],
        sem=("parallel", "arbitrary"), args=[xbc, xbc, xbc, dtg, ag, dgx, states, dy], comm=comm)


NORM_GROUP = D_INNER // SSM_GROUPS


def _gate_norm_fwd(y, zx, nw, *, name, tm=256):
    t_dim = y.shape[0]
    row = pl.BlockSpec((tm, D_INNER), lambda i: (i, 0))

    def body(y_ref, z_ref, nw_ref, o_ref):
        z = z_ref[...]
        yz = y_ref[...] * (z * _sigmoid(z))
        for g in range(SSM_GROUPS):
            cols = slice(g * NORM_GROUP, (g + 1) * NORM_GROUP)
            yhat, _ = _rms(yz[:, cols])
            o_ref[:, cols] = (yhat * nw_ref[:, cols]).astype(BF16)

    return pl.pallas_call(
        body, name=name, grid=(t_dim // tm,), in_specs=[row, row, pl.BlockSpec((1, D_INNER), lambda i: (0, 0))],
        out_specs=row, out_shape=jax.ShapeDtypeStruct((t_dim, D_INNER), BF16),
        compiler_params=_params("parallel"),
    )(y, zx, nw)


def _gate_norm_bwd(y, zx, nw, dyn, *, name, tm=256):
    t_dim = y.shape[0]
    row = pl.BlockSpec((tm, D_INNER), lambda i: (i, 0))
    vec = pl.BlockSpec((1, D_INNER), lambda i: (0, 0))

    def body(y_ref, z_ref, nw_ref, dyn_ref, dy_ref, dz_ref, dnw_ref):
        @pl.when(pl.program_id(0) == 0)
        def _():
            dnw_ref[...] = jnp.zeros_like(dnw_ref)

        z = z_ref[...]
        yv = y_ref[...]
        sg = _sigmoid(z)
        silu_z = z * sg
        yz = yv * silu_z
        dyn_v = dyn_ref[...]
        for g in range(SSM_GROUPS):
            cols = slice(g * NORM_GROUP, (g + 1) * NORM_GROUP)
            yhat, r = _rms(yz[:, cols])
            dn = dyn_v[:, cols]
            dnw_ref[:, cols] += jnp.sum(dn * yhat, axis=0, keepdims=True)
            dyhat = dn * nw_ref[:, cols]
            dyz = r * (dyhat - yhat * jnp.mean(dyhat * yhat, axis=-1, keepdims=True))
            dy_ref[:, cols] = dyz * silu_z[:, cols]
            dz_ref[:, cols] = dyz * yv[:, cols] * _dsilu(z[:, cols], sg[:, cols])

    return pl.pallas_call(
        body, name=name, grid=(t_dim // tm,), in_specs=[row, row, vec, row],
        out_specs=[row, row, vec],
        out_shape=[jax.ShapeDtypeStruct((t_dim, D_INNER), F32), jax.ShapeDtypeStruct((t_dim, ZX_DIM), F32),
                   jax.ShapeDtypeStruct((1, D_INNER), F32)],
        compiler_params=_params("arbitrary"),
    )(y, zx, nw, dyn)


HEADS_PER_LANE_TILE = LANES // ATT_HEAD_DIM
STACKED_ROWS = Q_PER_KV * WINDOW


def _att_half_masks():
    lane = lax.broadcasted_iota(jnp.int32, (WINDOW, LANES), 1)
    return [(lane >= i * ATT_HEAD_DIM) & (lane < (i + 1) * ATT_HEAD_DIM) for i in range(HEADS_PER_LANE_TILE)]


def _att_stack_heads(ref, kvh, masks):
    parts = []
    for g in range(Q_PER_KV):
        h = kvh * Q_PER_KV + g
        blk = ref[:, (h // HEADS_PER_LANE_TILE) * LANES:(h // HEADS_PER_LANE_TILE + 1) * LANES]
        parts.append(jnp.where(masks[h % HEADS_PER_LANE_TILE], blk, jnp.zeros_like(blk)))
    return jnp.concatenate(parts, axis=0)


def _att_kv_tile(ref, kvh, masks):
    blk = ref[:, (kvh // HEADS_PER_LANE_TILE) * LANES:(kvh // HEADS_PER_LANE_TILE + 1) * LANES]
    return jnp.where(masks[kvh % HEADS_PER_LANE_TILE], blk, pltpu.roll(blk, ATT_HEAD_DIM, 1)).astype(BF16)


def _att_stacked_masks(n):
    row = lax.bitwise_and(lax.broadcasted_iota(jnp.int32, (STACKED_ROWS, WINDOW), 0), WINDOW - 1)
    col = lax.broadcasted_iota(jnp.int32, (STACKED_ROWS, WINDOW), 1)
    return col <= row, (col > row) & (n > 0)


def _att_stack_columns(ref, kvh, rows):
    cols = [ref[:, kvh * Q_PER_KV + g:kvh * Q_PER_KV + g + 1] for g in range(Q_PER_KV)]
    return jnp.concatenate([jnp.broadcast_to(c, (rows, 1)) for c in cols], axis=0)


def _att_scores(q4, k_tile, mask):
    return jnp.where(mask, _dot(q4, k_tile, NT) * ATT_SCALE, NEG_BIG)


def _att_unstack(x4, kvh, masks, tiles):
    for g in range(Q_PER_KV):
        h = kvh * Q_PER_KV + g
        piece = x4[g * WINDOW:(g + 1) * WINDOW]
        t = h // HEADS_PER_LANE_TILE
        tiles[t] = piece if h % HEADS_PER_LANE_TILE == 0 else jnp.where(masks[1], piece, tiles[t])


def _attn_fwd(q, k, v, sinks, *, name, comm=None):
    t_dim = q.shape[0]

    def body(q_ref, kc_ref, kp_ref, vc_ref, vp_ref, s_ref, o_ref, l_ref):
        n = pl.program_id(0)
        masks = _att_half_masks()
        mask_c, mask_p = _att_stacked_masks(n)
        out_tiles = [None] * (D_MODEL // LANES)
        for kvh in range(N_KV_HEADS):
            q4 = _att_stack_heads(q_ref, kvh, masks).astype(BF16)
            kc, kp = _att_kv_tile(kc_ref, kvh, masks), _att_kv_tile(kp_ref, kvh, masks)
            vc, vp = _att_kv_tile(vc_ref, kvh, masks), _att_kv_tile(vp_ref, kvh, masks)
            sc = _att_scores(q4, kc, mask_c)
            sp = _att_scores(q4, kp, mask_p)
            sink = _att_stack_columns(s_ref, kvh, WINDOW)
            m = jnp.maximum(jnp.maximum(jnp.max(sc, axis=1, keepdims=True), jnp.max(sp, axis=1, keepdims=True)), sink)
            pc = jnp.exp(sc - m)
            pp = jnp.exp(sp - m)
            den = jnp.sum(pc, axis=1, keepdims=True) + jnp.sum(pp, axis=1, keepdims=True) + jnp.exp(sink - m)
            _att_unstack((_dot(pc, vc) + _dot(pp, vp)) / den, kvh, masks, out_tiles)
            lse4 = m + jnp.log(den)
            for g in range(Q_PER_KV):
                h = kvh * Q_PER_KV + g
                l_ref[:, h:h + 1] = lse4[g * WINDOW:(g + 1) * WINDOW]
        for t, tile in enumerate(out_tiles):
            o_ref[:, t * LANES:(t + 1) * LANES] = tile

    cur = lambda w: pl.BlockSpec((WINDOW, w), lambda n: (n, 0))
    prv = lambda w: pl.BlockSpec((WINDOW, w), lambda n: (jnp.maximum(n - 1, 0), 0))
    return _call(
        body, name=name, grid=(t_dim // WINDOW,),
        in_specs=[cur(D_MODEL), cur(KV_DIM), prv(KV_DIM), cur(KV_DIM), prv(KV_DIM), pl.BlockSpec((1, N_Q_HEADS), lambda n: (0, 0))],
        out_specs=[cur(D_MODEL), cur(N_Q_HEADS)],
        out_shape=[jax.ShapeDtypeStruct((t_dim, D_MODEL), F32), jax.ShapeDtypeStruct((t_dim, N_Q_HEADS), F32)],
        sem=("parallel",), args=[q, k, k, v, v, sinks], comm=comm)


def _attn_bwd(q, k, v, sinks, o, lse, do, cos2, sin2, *, name, comm=None):
    t_dim = q.shape[0]

    def body(q_ref, kc_ref, kp_ref, vc_ref, vp_ref, s_ref, o_ref, l_ref, do_ref, cos_ref, sin_ref, cos_all_ref, sin_all_ref,
             dq_ref, dk_ref, dv_ref, dsink_ref):
        n = pl.program_id(0)

        @pl.when(n == 0)
        def _():
            dk_ref[...] = jnp.zeros_like(dk_ref)
            dv_ref[...] = jnp.zeros_like(dv_ref)
            dsink_ref[...] = jnp.zeros_like(dsink_ref)

        masks = _att_half_masks()
        mask_c, mask_p = _att_stacked_masks(n)
        lane_row = lax.broadcasted_iota(jnp.int32, (1, N_Q_HEADS), 1)
        rows_c = pl.ds(pl.multiple_of(n * WINDOW, WINDOW), WINDOW)
        rows_p = pl.ds(pl.multiple_of(jnp.maximum(n - 1, 0) * WINDOW, WINDOW), WINDOW)
        dsink = jnp.zeros((1, N_Q_HEADS), F32)
        dq_tiles = [None] * (D_MODEL // LANES)
        kv_tiles = KV_DIM // LANES
        dkc_tiles, dkp_tiles, dvc_tiles, dvp_tiles = ([None] * kv_tiles for _ in range(4))

        def place(tiles, kvh, x):
            folded = x + pltpu.roll(x, ATT_HEAD_DIM, 1)
            t = kvh // HEADS_PER_LANE_TILE
            tiles[t] = folded if kvh % HEADS_PER_LANE_TILE == 0 else jnp.where(masks[1], folded, tiles[t])

        for kvh in range(N_KV_HEADS):
            q4 = _att_stack_heads(q_ref, kvh, masks).astype(BF16)
            do4 = _att_stack_heads(do_ref, kvh, masks)
            o4 = _att_stack_heads(o_ref, kvh, masks)
            kc, kp = _att_kv_tile(kc_ref, kvh, masks), _att_kv_tile(kp_ref, kvh, masks)
            vc, vp = _att_kv_tile(vc_ref, kvh, masks), _att_kv_tile(vp_ref, kvh, masks)
            l4 = _att_stack_columns(l_ref, kvh, WINDOW)
            pc = jnp.exp(_att_scores(q4, kc, mask_c) - l4)
            pp = jnp.exp(_att_scores(q4, kp, mask_p) - l4)
            delta = jnp.sum(do4 * o4, axis=1, keepdims=True)
            do4b = do4.astype(BF16)
            dsc = pc * (_dot(do4b, vc, NT) - delta)
            dsp = pp * (_dot(do4b, vp, NT) - delta)
            _att_unstack((_dot(dsc, kc) + _dot(dsp, kp)) * ATT_SCALE, kvh, masks, dq_tiles)
            place(dkc_tiles, kvh, _dot(dsc, q4, TN) * ATT_SCALE)
            place(dkp_tiles, kvh, _dot(dsp, q4, TN) * ATT_SCALE)
            place(dvc_tiles, kvh, _dot(pc, do4b, TN))
            place(dvp_tiles, kvh, _dot(pp, do4b, TN))
            p_sink = jnp.exp(_att_stack_columns(s_ref, kvh, WINDOW) - l4) * delta
            for g in range(Q_PER_KV):
                h = kvh * Q_PER_KV + g
                dsink = jnp.where(lane_row == h, -jnp.sum(p_sink[g * WINDOW:(g + 1) * WINDOW], axis=0, keepdims=True), dsink)
        for t, tile in enumerate(dq_tiles):
            dq_ref[:, t * LANES:(t + 1) * LANES] = _rope_rotate(tile, cos_ref[...], -sin_ref[...])
        for t in range(kv_tiles):
            lanes = slice(t * LANES, (t + 1) * LANES)
            dk_ref[rows_c, lanes] += dkc_tiles[t]
            dk_ref[rows_p, lanes] += dkp_tiles[t]
            dv_ref[rows_c, lanes] += dvc_tiles[t]
            dv_ref[rows_p, lanes] += dvp_tiles[t]
        dsink_ref[...] += dsink

        @pl.when(n == t_dim // WINDOW - 1)
        def _():
            dk_ref[...] = _rope_rotate(dk_ref[...], cos_all_ref[...], -sin_all_ref[...])

    cur = lambda w: pl.BlockSpec((WINDOW, w), lambda n: (n, 0))
    prv = lambda w: pl.BlockSpec((WINDOW, w), lambda n: (jnp.maximum(n - 1, 0), 0))
    whole = lambda w: pl.BlockSpec((t_dim, w), lambda n: (0, 0))
    svec = pl.BlockSpec((1, N_Q_HEADS), lambda n: (0, 0))
    return _call(
        body, name=name, grid=(t_dim // WINDOW,),
        in_specs=[cur(D_MODEL), cur(KV_DIM), prv(KV_DIM), cur(KV_DIM), prv(KV_DIM), svec, cur(D_MODEL), cur(N_Q_HEADS), cur(D_MODEL),
                  cur(LANES), cur(LANES), whole(LANES), whole(LANES)],
        out_specs=[cur(D_MODEL), whole(KV_DIM), whole(KV_DIM), svec],
        out_shape=[jax.ShapeDtypeStruct((t_dim, D_MODEL), F32), jax.ShapeDtypeStruct((t_dim, KV_DIM), F32),
                   jax.ShapeDtypeStruct((t_dim, KV_DIM), F32), jax.ShapeDtypeStruct((1, N_Q_HEADS), F32)],
        sem=("arbitrary",), args=[q, k, k, v, v, sinks, o, lse, do, cos2, sin2, cos2, sin2], comm=comm)


def _loss_head(x, nw, target, *, name, tm=512):
    t_dim, d_dim = x.shape
    row = pl.BlockSpec((tm, d_dim), lambda i: (i, 0))
    vec = pl.BlockSpec((1, d_dim), lambda i: (0, 0))

    def body(x_ref, nw_ref, tgt_ref, loss_ref, dx_ref, dnw_ref, dob_ref):
        @pl.when(pl.program_id(0) == 0)
        def _():
            loss_ref[...] = jnp.zeros_like(loss_ref)
            dnw_ref[...] = jnp.zeros_like(dnw_ref)

        xhat, r = _rms(x_ref[...])
        err = xhat * nw_ref[...] - tgt_ref[...]
        loss_ref[...] += 0.5 * _sum_all(jnp.mean(err * err, axis=-1, keepdims=True))
        dy = err * (1.0 / d_dim)
        dnw_ref[...] += jnp.sum(dy * xhat, axis=0, keepdims=True)
        dxhat = dy * nw_ref[...]
        dx = r * (dxhat - xhat * jnp.mean(dxhat * xhat, axis=-1, keepdims=True))
        dx_ref[...] = dx
        dob_ref[...] = (FFN_RES_WEIGHT * dx).astype(BF16)

    return pl.pallas_call(
        body, name=name, grid=(t_dim // tm,), in_specs=[row, vec, row],
        out_specs=[pl.BlockSpec((1, 1), lambda i: (0, 0)), row, vec, row],
        out_shape=[jax.ShapeDtypeStruct((1, 1), F32), jax.ShapeDtypeStruct((t_dim, d_dim), F32),
                   jax.ShapeDtypeStruct((1, d_dim), F32), jax.ShapeDtypeStruct((t_dim, d_dim), BF16)],
        compiler_params=_params("arbitrary"),
    )(x, nw, target)


def _rope_tables():
    pos = jnp.arange(SEQ, dtype=F32)
    inv = 1.0 / (ROPE_THETA ** (jnp.arange(0, ATT_HEAD_DIM, 2, dtype=F32) / ATT_HEAD_DIM))
    ang = pos[:, None] * inv[None, :]
    cos, sin = jnp.cos(ang), jnp.sin(ang)
    return jnp.tile(cos, (1, 4)), jnp.tile(sin, (1, 4))


def _to_groups(t):
    return t.reshape(t.shape[0], SSM_GROUPS, HEADS_PER_GROUP).transpose(1, 0, 2)


def _from_groups(t):
    return t.transpose(1, 0, 2).reshape(t.shape[1], SSM_HEADS)


def _forward_backward(x0, target, net):
    w = net.w
    nw = [[w("norm_w")[l, i][None, :] for i in range(3)] for l in range(2)]
    cos2, sin2 = _rope_tables()
    ffn_norm = [nw[0][0], nw[0][2], nw[1][0], nw[1][2]]

    ffn_pre = {}

    def ffn_f(x, blk):
        name = f"ffn_fwd{blk}"
        out, *ffn_pre[blk] = _ffn_fwd(x, ffn_norm[blk], w(f"gate{blk}"), w(f"up{blk}"), w(f"down{blk}"), name=name,
                                      comm=net.carry(name))
        return out

    x1 = ffn_f(x0, 0)
    zx, h1 = _norm_mm(x1, nw[0][1], w("w_in_t"), None, w_rows=ZX_DIM, name="ssm_in_proj", comm=net.carry("ssm_in_proj"))
    dtr = _mm(h1, w("w_in_t"), dims="nt", b_rows=(ZX_DIM, SSM_HEADS), name="ssm_dt_proj")
    xbc = _conv_fwd(zx, w("conv_w"), w("conv_b"), name="ssm_conv_fwd", comm=net.carry("ssm_conv_fwd"))
    dt, a_dt = _dt_prep(dtr, w("dt_bias"), w("a_log"), name="ssm_dt_prep")
    dtg, ag = _to_groups(dt), _to_groups(a_dt)
    dg = jnp.repeat(w("d_skip").reshape(SSM_GROUPS, 1, HEADS_PER_GROUP), SSM_HEAD_DIM, axis=2)
    y_ssd, states = _ssd_fwd(xbc, dtg, ag, dg, name="ssd_fwd", comm=net.carry("ssd_fwd"))
    yn = _gate_norm_fwd(y_ssd, zx, w("ssm_norm_w"), name="ssm_gate_norm_fwd")
    x2 = _mm(yn, w("wout"), res=x1, name="ssm_out_proj", comm=net.carry("ssm_out_proj"))
    x3 = ffn_f(x2, 1)
    k_rot, hk = _norm_mm(x3, w("kv_norm_w"), w("wk"), w("b_k"), rope=(cos2, sin2), name="k_proj")
    v = _mm(hk, w("wv"), bias=w("b_v"), name="v_proj")
    x4 = ffn_f(x3, 2)
    q_rot, h4 = _norm_mm(x4, nw[1][1], w("wq"), w("b_q"), rope=(cos2, sin2), name="q_proj")
    att, lse = _attn_fwd(q_rot, k_rot, v, w("sinks"), name="attn_fwd", comm=net.carry("attn_fwd"))
    x5 = _mm(att, w("wo"), bias=w("b_o"), res=x4, name="attn_out_proj")
    x6 = ffn_f(x5, 3)
    loss, dx6, d_final, dob6 = _loss_head(x6, w("final_norm_w"), target, name="loss_head")

    d_norm = [[None] * 3 for _ in range(2)]

    def ffn_b(x, dout, dob, blk):
        pre_g, pre_u, h = ffn_pre[blk]
        name = f"ffn_bwd{blk}"
        dh, gg, gu, gd = _ffn_bwd(h, dob, pre_g, pre_u, w(f"gate{blk}"), w(f"up{blk}"), w(f"down{blk}"), name=name,
                                  comm=net.carry(name))
        net.give(f"gate{blk}", gg)
        net.give(f"up{blk}", gu)
        net.give(f"down{blk}", gd)
        return _norm_bwd(x, ffn_norm[blk], dh, [dout], name=f"ffn_norm_bwd{blk}", comm=net.carry(f"ffn_norm_bwd{blk}"))

    by_rows = lambda g: g.reshape(N_DEV, g.shape[0] // N_DEV, g.shape[1])
    dx5, d_norm[1][2] = ffn_b(x5, dx6, dob6, 3)
    d_att = _mm(dx5, w("wo"), dims="nt", name="attn_out_proj_dx", comm=net.carry("attn_out_proj_dx"))
    g_o, d_bo = _mm(att, dx5, dims="tn", out_dtype=BF16, colsum_b=True, name="attn_out_proj_dw")
    net.give("w_o", by_rows(g_o))
    dq, dk, dv, d_sinks = _attn_bwd(q_rot, k_rot, v, w("sinks"), att, lse, d_att, cos2, sin2, name="attn_bwd",
                                    comm=net.carry("attn_bwd"))
    dx4, d_norm[1][1], dob4 = _mm_norm_bwd(dq, w("wq"), x4, nw[1][1], [dx5], dims="nt", name="q_proj_dx")
    g_q, d_bq = _mm(h4, dq, dims="tn", out_dtype=BF16, colsum_b=True, name="q_proj_dw")
    net.give("w_q", by_rows(g_q))
    dx3a, d_norm[1][0] = ffn_b(x3, dx4, dob4, 2)
    dhk = _mm(dk, w("wk"), dims="nt", name="k_proj_dx", comm=net.carry("k_proj_dx"))
    dx3, d_kvn, dob3 = _mm_norm_bwd(dv, w("wv"), x3, w("kv_norm_w"), [dx3a], dims="nt", add=dhk, name="v_proj_dx")
    g_k, d_bk = _mm(hk, dk, dims="tn", out_dtype=BF16, colsum_b=True, name="k_proj_dw")
    g_v, d_bv = _mm(hk, dv, dims="tn", out_dtype=BF16, colsum_b=True, name="v_proj_dw")
    net.give("w_k", by_rows(g_k))
    net.give("w_v", by_rows(g_v))
    dx2, d_norm[0][2] = ffn_b(x2, dx3, dob3, 1)
    d_yn = _mm(dx2, w("wout"), dims="nt", name="ssm_out_proj_dx", comm=net.carry("ssm_out_proj_dx"))
    net.give("w_out", by_rows(_mm(yn, dx2, dims="tn", out_dtype=BF16, name="ssm_out_proj_dw")))
    dy_ssd, dzx, d_ssm_norm = _gate_norm_bwd(y_ssd, zx, w("ssm_norm_w"), d_yn, name="ssm_gate_norm_bwd")
    dxs, d_b, d_c, ddtg, dag, ddg = _ssd_bwd(xbc, dtg, ag, dg, states, dy_ssd, name="ssd_bwd", comm=net.carry("ssd_bwd"))
    dzx, d_conv_w, d_conv_b = _conv_bwd(zx, w("conv_w"), w("conv_b"), dxs, d_b, d_c, dzx, name="ssm_conv_bwd",
                                        comm=net.carry("ssm_conv_bwd"))
    ddtr, d_dt_bias, d_a_log = _dt_bwd(dtr, w("dt_bias"), w("a_log"), dt, _from_groups(ddtg), _from_groups(dag), name="ssm_dt_bwd")
    dh1 = _mm(dzx, w("w_in_t"), b_rows=(0, ZX_DIM), name="ssm_in_proj_dx", comm=net.carry("ssm_in_proj_dx"))
    in_rows = N_DEV * IN_PROJ_SHARD
    g_in = _mm(dzx, h1, dims="tn", out_dtype=BF16, out_window=(0, in_rows), name="ssm_in_proj_dw")
    g_in = _mm(ddtr, h1, dims="tn", out_dtype=BF16, out_window=(ZX_DIM, in_rows), into=g_in, name="ssm_dt_proj_dw")
    net.give("w_in", g_in.reshape(N_DEV, IN_PROJ_SHARD, D_MODEL))
    dx1, d_norm[0][1], dob1 = _mm_norm_bwd(ddtr, w("w_in_t"), x1, nw[0][1], [dx2], b_rows=(ZX_DIM, SSM_HEADS), add=dh1,
                                           name="ssm_dt_proj_dx", comm=net.carry("ssm_norm_bwd"))
    dx0, d_norm[0][0] = ffn_b(x0, dx1, dob1, 0)

    small = {"norm_w": jnp.concatenate([d_norm[l][i] for l in range(2) for i in range(3)], axis=0),
             "ssm_conv_w": d_conv_w, "ssm_conv_b": d_conv_b, "ssm_dt_bias": d_dt_bias, "ssm_a_log": d_a_log,
             "ssm_d": ddg.reshape(1, SSM_HEADS), "ssm_norm_w": d_ssm_norm, "kv_norm_w": d_kvn,
             "b_k": d_bk, "b_v": d_bv, "attn_b_q": d_bq, "attn_sinks": d_sinks, "attn_b_o": d_bo, "final_norm_w": d_final}
    return loss, dx0, small


BLOCK_BYTES = 1 << 20


def _row_tile(rows, cols):
    for t in (512, 256, 128, 64, 32, 16):
        if rows % t == 0 and t * cols * 4 <= BLOCK_BYTES:
            return t
    return rows


def _cast_bf16(x, *, name):
    n_blk, rows, cols = x.shape
    tm = rows if rows * cols * 4 <= 2 * BLOCK_BYTES else _row_tile(rows, cols)
    spec = pl.BlockSpec((None, tm, cols), lambda b, i: (b, i, 0))

    def body(x_ref, o_ref):
        o_ref[...] = x_ref[...].astype(BF16)

    return pl.pallas_call(body, name=name, grid=(n_blk, rows // tm), in_specs=[spec], out_specs=spec,
                          out_shape=jax.ShapeDtypeStruct(x.shape, BF16), compiler_params=_params("parallel", "parallel"))(x)


def _pair_add(grad, theirs, *, name):
    n_slots, rows, cols = theirs.shape
    tm = rows if rows * cols * 4 <= 2 * BLOCK_BYTES else _row_tile(rows, cols)

    def body(g_ref, t_ref, o_ref):
        o_ref[...] = (g_ref[...].astype(F32) + t_ref[...].astype(F32)).astype(BF16)

    spec = pl.BlockSpec((None, tm, cols), lambda s, i: (s, i, 0))
    return pl.pallas_call(
        body, name=name, grid=(n_slots, rows // tm),
        in_specs=[pl.BlockSpec((None, tm, cols), lambda s, i: (2 * s + lax.axis_index("c"), i, 0)), spec], out_specs=spec,
        out_shape=jax.ShapeDtypeStruct(theirs.shape, BF16), compiler_params=_params("parallel", "parallel"),
    )(grad, theirs)


def _adam_update(g, w, m, v):
    m = ADAM_B1 * m + (1.0 - ADAM_B1) * g
    v = ADAM_B2 * v + (1.0 - ADAM_B2) * (g * g)
    m_hat = m / (1.0 - ADAM_B1 ** ADAM_STEP)
    v_hat = v / (1.0 - ADAM_B2 ** ADAM_STEP)
    delta = -ADAM_LR * (m_hat / (jnp.sqrt(v_hat) + ADAM_EPS) + ADAM_WD * w)
    return delta, m, v


def _adamw(parts, w, m, v, first_blk, prev, *, name, comm=None):
    n_blk, rows, cols = w.shape
    tm = _row_tile(rows, cols)
    n_tiles = rows // tm
    spec = pl.BlockSpec((None, tm, cols), lambda b, i: (first_blk + b, i, 0))
    n_prev, n_here = len(prev), len(parts)
    n_parts = parts[0].shape[0]

    def part_spec(q):
        return pl.BlockSpec((n_parts, tm, cols), lambda b, i: (0, jnp.where(b < q, 0, jnp.where(b == q, i, n_tiles - 1)), 0))

    def body(*refs):
        p_refs = refs[:n_here]
        w_ref, m_ref, v_ref = refs[n_here:n_here + 3]
        g_ref, d_ref, nm_ref, nv_ref = refs[n_here + 3 + n_prev:]
        b = pl.program_id(0)
        g = None
        for s in range(n_parts):
            t = p_refs[0][s]
            for q in range(1, n_here):
                t = jnp.where(b == q, p_refs[q][s], t)
            g = t.astype(F32) if g is None else g + t.astype(F32)
        delta, nm, nv = _adam_update(g, w_ref[...], m_ref[...], v_ref[...])
        g_ref[...] = g
        d_ref[...] = delta
        nm_ref[...] = nm
        nv_ref[...] = nv

    return _call(
        body, name=name, grid=(n_here, n_tiles),
        in_specs=[part_spec(q) for q in range(n_here)] + [spec, spec, spec] + [pl.BlockSpec(memory_space=pl.ANY)] * n_prev,
        out_specs=[spec] * 4, out_shape=[jax.ShapeDtypeStruct((n_blk, rows, cols), F32)] * 4,
        aliases={n_here + 3 + q: q for q in range(n_prev)}, sem=("arbitrary", "arbitrary"),
        args=[*parts, w, m, v, *prev], comm=comm)


def _sum_parts(parts, *, name):
    def body(p_ref, o_ref):
        g = p_ref[0]
        for s in range(1, N_DEV):
            g = g + p_ref[s]
        o_ref[...] = g

    return pl.pallas_call(body, name=name, out_shape=jax.ShapeDtypeStruct(parts.shape[1:], F32), compiler_params=_params())(parts)


def _adamw_packed(g, w, m, v, *, name):
    def body(g_ref, w_ref, m_ref, v_ref, d_ref, nm_ref, nv_ref):
        delta, nm, nv = _adam_update(g_ref[...], w_ref[...], m_ref[...], v_ref[...])
        d_ref[...] = delta
        nm_ref[...] = nm
        nv_ref[...] = nv

    return pl.pallas_call(body, name=name, out_shape=[jax.ShapeDtypeStruct(g.shape, F32)] * 3, compiler_params=_params())(g, w, m, v)


SUBLANES = 8


WIDE_PACK = 1024


def _pack(arrs, width=LANES):
    rows = []
    for a in arrs:
        a2 = a.reshape(-1, a.shape[-1])
        a2 = jnp.pad(a2, ((0, 0), (0, (-a2.shape[1]) % width)))
        rows += [a2[:, i * width:(i + 1) * width] for i in range(a2.shape[1] // width)]
    out = jnp.concatenate(rows, axis=0)
    return jnp.pad(out, ((0, (-out.shape[0]) % SUBLANES), (0, 0)))


def _unpack(packed, shapes, width=LANES):
    outs, r = [], 0
    for shp in shapes:
        lead, cols = math.prod(shp[:-1]), shp[-1]
        n_blocks = -(-cols // width)
        blocks = [packed[r + i * lead:r + (i + 1) * lead] for i in range(n_blocks)]
        outs.append(jnp.concatenate(blocks, axis=1)[:, :cols].reshape(shp))
        r += n_blocks * lead
    return outs


WEIGHT_NAMES = ("norm_w", "ffn_w_gate", "ffn_w_up", "ffn_w_down", "ssm_w_in", "ssm_conv_w", "ssm_conv_b", "ssm_dt_bias",
                "ssm_a_log", "ssm_d", "ssm_norm_w", "ssm_w_out", "kv_norm_w", "w_k", "b_k", "w_v", "b_v", "attn_w_q",
                "attn_b_q", "attn_sinks", "attn_w_o", "attn_b_o", "final_norm_w")
MATRIX_NAMES = ("ffn_w_gate", "ffn_w_up", "ffn_w_down", "ssm_w_in", "ssm_w_out", "w_k", "w_v", "attn_w_q", "attn_w_o")
VECTOR_NAMES = tuple(n for n in WEIGHT_NAMES if n not in MATRIX_NAMES)
SHARDED_VECTORS = ("norm_w", "ssm_conv_w", "ssm_conv_b", "ssm_norm_w")


GATHER_PLAN = {
    "gather_stage0": ("gate0", "up0", "down0", "vec"),
    "ffn_fwd0": ("w_in",),
    "ssm_in_proj": ("w_out", "gate1"),
    "ssm_conv_fwd": ("w_k", "w_v", "up1"),
    "ssd_fwd": ("down1", "gate2"),
    "ssm_out_proj": ("w_q", "w_o"),
    "ffn_fwd1": ("up2", "down2"),
    "ffn_fwd2": ("up3",),
    "attn_fwd": ("gate3", "down3"),
}
PAIR_PLAN = {
    "attn_bwd": ("gate3", "up3", "down3"),
    "ffn_bwd2": ("w_q", "w_o"),
    "ffn_bwd1": ("gate2", "up2", "down2", "w_k", "w_v"),
    "ssd_bwd": ("gate1", "up1", "down1", "w_out"),
    "ssm_norm_bwd": ("w_in",),
    "ffn_norm_bwd0": ("gate0", "up0", "down0"),
}
CHIP_PLAN = {
    "ffn_bwd2": ("gate3", "up3"),
    "ffn_bwd1": ("down3",),
    "ssd_bwd": ("gate2", "up2", "down2", "w_q", "w_o", "w_k", "w_v"),
    "ssm_conv_bwd": ("gate1", "up1"),
    "ssm_in_proj_dx": ("w_out",),
    "ffn_bwd0": ("down1", "w_in"),
    "adamw_gate": ("gate0",),
    "adamw_up": ("up0",),
    "adamw_down": ("down0",),
}
FFN_PARAMS = {"gate": "ffn_w_gate", "up": "ffn_w_up", "down": "ffn_w_down"}
SINGLE_MATRICES = {"w_in": "ssm_w_in", "w_out": "ssm_w_out", "w_k": "w_k", "w_v": "w_v", "w_q": "attn_w_q", "w_o": "attn_w_o"}


TRANSPOSED = ("ffn_w_gate", "ffn_w_up", "ssm_w_in")


def _matrix_view(name, a):
    if name in TRANSPOSED:
        a = jnp.swapaxes(a, -1, -2)
    return a.reshape((-1,) + a.shape[-2:])


def _from_matrix_view(name, a, shape):
    if name in TRANSPOSED:
        return jnp.swapaxes(a.reshape(shape[:-2] + (shape[-1], shape[-2])), -1, -2)
    return a.reshape(shape)


class _MeshNet:
    def __init__(self, p):
        self.p = p
        self.views = {n: _matrix_view(n, p[n]) for n in MATRIX_NAMES}
        self.local = {"vec": _pack([p[n] for n in SHARDED_VECTORS])}
        for short, n in FFN_PARAMS.items():
            cast = _cast_bf16(self.views[n], name=f"cast_{short}")
            self.local.update({f"{short}{k}": (cast, k) for k in range(N_FFN)})
        for short, n in SINGLE_MATRICES.items():
            self.local[short] = (_cast_bf16(self.views[n], name=f"cast_{short}"), 0)
        self.gathered_at, self.pairs_at, self.parts_at, self.grads, self.cache = {}, {}, {}, {}, {}

    def carry(self, name):
        comms = []
        if name in GATHER_PLAN:
            keys, comm = GATHER_PLAN[name], _Gather([self.local[k] for k in GATHER_PLAN[name]])
            self.gathered_at.update({k: (comm, i) for i, k in enumerate(keys)})
            comms.append(comm)
        if name in CHIP_PLAN:
            sums = []
            for k in CHIP_PLAN[name]:
                comm, i = self.pairs_at[k]
                sums.append(_pair_add(self.grads[k], comm.results[i], name=f"pair_add_{k}"))
            comm = _ChipExchange(sums)
            self.parts_at.update({k: (comm, i) for i, k in enumerate(CHIP_PLAN[name])})
            comms.append(comm)
        if name in PAIR_PLAN:
            keys, comm = PAIR_PLAN[name], _PairSwap([self.grads[k] for k in PAIR_PLAN[name]])
            self.pairs_at.update({k: (comm, i) for i, k in enumerate(keys)})
            comms.append(comm)
        return comms

    def run(self, name):
        for comm in self.carry(name):
            _run_exchange(comm, name=name)

    def give(self, key, grad):
        self.grads[key] = grad

    def parts(self, key):
        comm, i = self.parts_at[key]
        return comm.results[i]

    def _gathered(self, key):
        comm, i = self.gathered_at[key]
        return comm.results[i]

    def _vec(self, r0, lead, n_blocks):
        vecs = self._gathered("vec")
        return jnp.concatenate([vecs[d, r0 + i * lead:r0 + (i + 1) * lead, :] for d in range(N_DEV) for i in range(n_blocks)], axis=1)

    def _derive(self, name):
        p = self.p
        if name[:-1] in FFN_PARAMS:
            return self._gathered(name)
        if name == "w_in_t":
            return self._gathered("w_in").reshape(N_DEV * IN_PROJ_SHARD, D_MODEL)
        by_rows = {"wout": "w_out", "wk": "w_k", "wv": "w_v", "wq": "w_q", "wo": "w_o"}
        if name in by_rows:
            g = self._gathered(by_rows[name])
            return g.reshape(N_DEV * g.shape[1], g.shape[2])
        vectors = {"norm_w": lambda: self._vec(0, 6, 1).reshape(2, 3, D_MODEL), "conv_w": lambda: self._vec(6, CONV_WIDTH, 3),
                   "conv_b": lambda: self._vec(18, 1, 3), "ssm_norm_w": lambda: self._vec(21, 1, 2)}
        if name in vectors:
            return vectors[name]()
        replicated = {"dt_bias": p["ssm_dt_bias"], "a_log": p["ssm_a_log"], "d_skip": p["ssm_d"], "kv_norm_w": p["kv_norm_w"][None],
                      "b_k": p["b_k"][None], "b_v": p["b_v"][None], "b_q": p["attn_b_q"], "sinks": p["attn_sinks"],
                      "b_o": p["attn_b_o"], "final_norm_w": p["final_norm_w"][None]}
        return replicated[name]

    def w(self, name):
        if name not in self.cache:
            self.cache[name] = self._derive(name)
        return self.cache[name]


def _step(x, target, p, m, v):
    pos = _slot(_position())
    net = _MeshNet(p)
    net.run("gather_stage0")
    loss, grad_x, small = _forward_backward(x, target, net)

    grads, deltas, new_m, new_v = {}, {}, {}, {}
    view = lambda d, n: _matrix_view(n, d[n])
    vec_gather = _Gather([_pack([small[n] for n in VECTOR_NAMES], WIDE_PACK)])
    for short, n in SINGLE_MATRICES.items():
        outs = _adamw([net.parts(short)], net.views[n], view(m, n), view(v, n), 0, [], name=f"adamw_{short}",
                      comm=[vec_gather] if short == "w_in" else None)
        grads[n], deltas[n], new_m[n], new_v[n] = [_from_matrix_view(n, o, p[n].shape) for o in outs]
    ffn_outs = {}
    for short, n in FFN_PARAMS.items():
        ffn_outs[short] = _adamw([net.parts(f"{short}{k}") for k in range(1, N_FFN)], net.views[n], view(m, n), view(v, n), 1, [],
                                 name=f"adamw_{short}", comm=net.carry(f"adamw_{short}"))
    for short, n in FFN_PARAMS.items():
        outs = _adamw([net.parts(f"{short}0")], net.views[n], view(m, n), view(v, n), 0, ffn_outs[short], name=f"adamw_{short}0")
        grads[n], deltas[n], new_m[n], new_v[n] = [_from_matrix_view(n, o, p[n].shape) for o in outs]
    vec_sum = _sum_parts(vec_gather.results[0], name="sum_vector_grads")
    full_shapes = {"norm_w": (2, 3, D_MODEL), "ssm_conv_w": (1, CONV_WIDTH, CONV_DIM), "ssm_conv_b": (1, CONV_DIM),
                   "ssm_norm_w": (1, D_INNER)}
    vec_full = dict(zip(VECTOR_NAMES, _unpack(vec_sum, [full_shapes.get(n, p[n].shape) for n in VECTOR_NAMES], WIDE_PACK)))
    for n in VECTOR_NAMES:
        g = vec_full[n]
        if n in SHARDED_VECTORS:
            per = p[n].shape[-1]
            g = lax.dynamic_slice_in_dim(g, pos * per, per, axis=g.ndim - 1)
        grads[n] = g
    packed = _adamw_packed(*[_pack([d[n] for n in VECTOR_NAMES], WIDE_PACK) for d in (grads, p, m, v)], name="adamw_vectors")
    shapes = [p[n].shape for n in VECTOR_NAMES]
    for d, pk in zip((deltas, new_m, new_v), packed):
        d.update(zip(VECTOR_NAMES, _unpack(pk, shapes, WIDE_PACK)))
    return loss, grad_x, grads, deltas, new_m, new_v


def kernel(x, norm_w, ffn_w_gate, ffn_w_up, ffn_w_down, ssm_w_in, ssm_conv_w, ssm_conv_b, ssm_dt_bias, ssm_a_log, ssm_d, ssm_norm_w, ssm_w_out, kv_norm_w, w_k, b_k, w_v, b_v, attn_w_q, attn_b_q, attn_sinks, attn_w_o, attn_b_o, final_norm_w, loss_target, m_norm_w, m_ffn_w_gate, m_ffn_w_up, m_ffn_w_down, m_ssm_w_in, m_ssm_conv_w, m_ssm_conv_b, m_ssm_dt_bias, m_ssm_a_log, m_ssm_d, m_ssm_norm_w, m_ssm_w_out, m_kv_norm_w, m_w_k, m_b_k, m_w_v, m_b_v, m_attn_w_q, m_attn_b_q, m_attn_sinks, m_attn_w_o, m_attn_b_o, m_final_norm_w, v_norm_w, v_ffn_w_gate, v_ffn_w_up, v_ffn_w_down, v_ssm_w_in, v_ssm_conv_w, v_ssm_conv_b, v_ssm_dt_bias, v_ssm_a_log, v_ssm_d, v_ssm_norm_w, v_ssm_w_out, v_kv_norm_w, v_w_k, v_b_k, v_w_v, v_b_v, v_attn_w_q, v_attn_b_q, v_attn_sinks, v_attn_w_o, v_attn_b_o, v_final_norm_w):
    p = dict(zip(WEIGHT_NAMES, (norm_w, ffn_w_gate, ffn_w_up, ffn_w_down, ssm_w_in, ssm_conv_w, ssm_conv_b, ssm_dt_bias, ssm_a_log, ssm_d, ssm_norm_w, ssm_w_out, kv_norm_w, w_k, b_k, w_v, b_v, attn_w_q, attn_b_q, attn_sinks, attn_w_o, attn_b_o, final_norm_w)))
    m = dict(zip(WEIGHT_NAMES, (m_norm_w, m_ffn_w_gate, m_ffn_w_up, m_ffn_w_down, m_ssm_w_in, m_ssm_conv_w, m_ssm_conv_b, m_ssm_dt_bias, m_ssm_a_log, m_ssm_d, m_ssm_norm_w, m_ssm_w_out, m_kv_norm_w, m_w_k, m_b_k, m_w_v, m_b_v, m_attn_w_q, m_attn_b_q, m_attn_sinks, m_attn_w_o, m_attn_b_o, m_final_norm_w)))
    v = dict(zip(WEIGHT_NAMES, (v_norm_w, v_ffn_w_gate, v_ffn_w_up, v_ffn_w_down, v_ssm_w_in, v_ssm_conv_w, v_ssm_conv_b, v_ssm_dt_bias, v_ssm_a_log, v_ssm_d, v_ssm_norm_w, v_ssm_w_out, v_kv_norm_w, v_w_k, v_b_k, v_w_v, v_b_v, v_attn_w_q, v_attn_b_q, v_attn_sinks, v_attn_w_o, v_attn_b_o, v_final_norm_w)))
    loss, grad_x, grads, deltas, new_m, new_v = _step(x[0], loss_target[0], p, m, v)
    loss = lax.psum(loss[0, 0], ("x", "y", "c"))
    return (loss, grad_x[None], *[grads[n] for n in WEIGHT_NAMES], *[deltas[n] for n in WEIGHT_NAMES],
            *[new_m[n] for n in WEIGHT_NAMES], *[new_v[n] for n in WEIGHT_NAMES])
```

```python
import functools
import math

import jax
import jax.numpy as jnp
from jax import lax
from jax.experimental import pallas as pl
from jax.experimental.pallas import tpu as pltpu

F32 = jnp.float32
BF16 = jnp.bfloat16

N_DEV = 8
SEQ = 2048
D_MODEL = 1024
D_FF_SHARD = 352
N_FFN = 4
D_INNER = 2048
SSM_HEADS = 32
SSM_HEAD_DIM = 64
SSM_GROUPS = 4
HEADS_PER_GROUP = 8
SSM_STATE = 128
CHUNK = 128
N_CHUNKS = SEQ // CHUNK
GN = SSM_GROUPS * SSM_STATE
CONV_DIM = D_INNER + 2 * GN
CONV_WIDTH = 4
ZX_DIM = D_INNER + CONV_DIM
IN_PROJ_SHARD = 644
ATT_HEAD_DIM = 64
N_Q_HEADS = 16
N_KV_HEADS = 4
Q_PER_KV = 4
KV_DIM = N_KV_HEADS * ATT_HEAD_DIM
WINDOW = 128
ROPE_THETA = 10000.0
EPS = 1e-5
FFN_RES_WEIGHT = 0.5
ATT_SCALE = 1.0 / math.sqrt(ATT_HEAD_DIM)
NEG_BIG = -1e30

ADAM_LR = 0.001
ADAM_B1 = 0.9
ADAM_B2 = 0.999
ADAM_EPS = 1e-08
ADAM_WD = 0.01
ADAM_STEP = 10

VMEM_LIMIT_BYTES = 56 * 1024 * 1024
FFN_BWD_VMEM_LIMIT_BYTES = 61 * 1024 * 1024

NN = (((1,), (0,)), ((), ()))
NT = (((1,), (1,)), ((), ()))
TN = (((0,), (0,)), ((), ()))
_DIMS = {"nn": NN, "nt": NT, "tn": TN}


def _params(*sem):
    return pltpu.CompilerParams(dimension_semantics=sem if sem else None, vmem_limit_bytes=VMEM_LIMIT_BYTES)


def _dot(a, b, dims=NN):
    return lax.dot_general(a.astype(BF16), b.astype(BF16), dims, preferred_element_type=F32)


def _sigmoid(x):
    return 1.0 / (1.0 + jnp.exp(-x))


def _dsilu(x, s):
    return s * (1.0 + x * (1.0 - s))


def _rms(x):
    r = lax.rsqrt(jnp.mean(x * x, axis=-1, keepdims=True) + EPS)
    return x * r, r


def _sum_all(x):
    return jnp.sum(jnp.sum(x, axis=1, keepdims=True), axis=0, keepdims=True)


MESH = pl.DeviceIdType.MESH
N_PEERS = N_DEV - 1
N_CHIPS = N_DEV // 2


def _position():
    return lax.axis_index("x"), lax.axis_index("y"), lax.axis_index("c")


def _slot(p):
    return 4 * p[0] + 2 * p[1] + p[2]


class _Exchange:
    def __init__(self, arrays, out_shapes):
        n = len(arrays)
        self.arrays = list(arrays)
        self.out_shapes = out_shapes
        self.scratch = [pltpu.SemaphoreType.DMA((n, N_PEERS)), pltpu.SemaphoreType.DMA((n, N_PEERS)), pltpu.SemaphoreType.DMA((n,))]
        self.results = None

    def relay(self, ins, outs, sems):
        pass


class _Gather(_Exchange):
    def __init__(self, pieces):
        pieces = [p if isinstance(p, tuple) else (p, None) for p in pieces]
        self.blocks = [k for _, k in pieces]
        shapes = [a.shape if k is None else a.shape[1:] for a, k in pieces]
        super().__init__([a for a, _ in pieces], [jax.ShapeDtypeStruct((N_DEV,) + s, a.dtype) for s, (a, _) in zip(shapes, pieces)])

    def _plan(self, ins, outs, sems):
        send_sems, recv_sems, local_sems = sems
        x, y, c = _position()
        me, sibling = (x, y, c), (x, y, 1 - c)
        chips = [(1 - x, y), (x, 1 - y), (1 - x, 1 - y)]
        n = len(ins)
        ins = [r if k is None else r.at[k] for r, k in zip(ins, self.blocks)]

        def copy(a, k, block, to, src=None):
            dst = outs[a].at[_slot(block)]
            return pltpu.make_async_remote_copy(src_ref=dst if src is None else src, dst_ref=dst, send_sem=send_sems.at[a, k],
                                                recv_sem=recv_sems.at[a, k], device_id=to, device_id_type=MESH)

        mine = [pltpu.make_async_copy(ins[a], outs[a].at[_slot(me)], local_sems.at[a]) for a in range(n)]
        first = []
        for a in range(n):
            first.append(copy(a, 0, me, sibling, src=ins[a]))
            first += [copy(a, 1 + j, me, (*chip, c), src=ins[a]) for j, chip in enumerate(chips)]
        return n, c, me, sibling, chips, copy, mine, first

    def start(self, ins, outs, sems):
        _, _, _, _, _, _, mine, first = self._plan(ins, outs, sems)
        for cp in mine + first:
            cp.start()

    def relay(self, ins, outs, sems):
        n, c, me, sibling, chips, copy, _, _ = self._plan(ins, outs, sems)
        for j, chip in enumerate(chips):
            for a in range(n):
                copy(a, 1 + j, (*chip, c), me).wait_recv()
                copy(a, 4 + j, (*chip, c), sibling).start()

    def finish(self, ins, outs, sems):
        n, c, me, sibling, chips, copy, mine, first = self._plan(ins, outs, sems)
        passed = [copy(a, 4 + j, (*chip, c), sibling) for j, chip in enumerate(chips) for a in range(n)]
        for a in range(n):
            copy(a, 0, sibling, me).wait_recv()
            for j, chip in enumerate(chips):
                copy(a, 4 + j, (*chip, 1 - c), me).wait_recv()
        for cp in first + passed:
            cp.wait_send()
        for cp in mine:
            cp.wait()


class _PairSwap(_Exchange):
    def __init__(self, arrays):
        n = len(arrays)
        self.arrays = list(arrays)
        self.out_shapes = [jax.ShapeDtypeStruct((N_CHIPS,) + a.shape[1:], a.dtype) for a in arrays]
        self.scratch = [pltpu.SemaphoreType.DMA((n, N_CHIPS)), pltpu.SemaphoreType.DMA((n, N_CHIPS))]
        self.results = None

    def _plan(self, ins, outs, sems):
        send_sems, recv_sems = sems
        x, y, c = _position()
        return [pltpu.make_async_remote_copy(src_ref=ins[a].at[2 * q + 1 - c], dst_ref=outs[a].at[q], send_sem=send_sems.at[a, q],
                                             recv_sem=recv_sems.at[a, q], device_id=(x, y, 1 - c), device_id_type=MESH)
                for a in range(len(ins)) for q in range(N_CHIPS)]

    def start(self, ins, outs, sems):
        for cp in self._plan(ins, outs, sems):
            cp.start()

    def finish(self, ins, outs, sems):
        for cp in self._plan(ins, outs, sems):
            cp.wait()


class _ChipExchange(_Exchange):
    def __init__(self, arrays):
        n = len(arrays)
        self.arrays = list(arrays)
        self.out_shapes = [jax.ShapeDtypeStruct(a.shape, a.dtype) for a in arrays]
        self.scratch = [pltpu.SemaphoreType.DMA((n, 3)), pltpu.SemaphoreType.DMA((n, 3)), pltpu.SemaphoreType.DMA((n,))]
        self.results = None

    def _plan(self, ins, outs, sems):
        send_sems, recv_sems, local_sems = sems
        x, y, c = _position()
        here = 2 * x + y
        chips = [(1 - x, y), (x, 1 - y), (1 - x, 1 - y)]
        n = len(ins)

        def copy(a, k, src_slot, dst_slot):
            return pltpu.make_async_remote_copy(src_ref=ins[a].at[src_slot], dst_ref=outs[a].at[dst_slot], send_sem=send_sems.at[a, k],
                                                recv_sem=recv_sems.at[a, k], device_id=(*chips[k], c), device_id_type=MESH)

        there = [2 * qx + qy for qx, qy in chips]
        mine = [pltpu.make_async_copy(ins[a].at[here], outs[a].at[here], local_sems.at[a]) for a in range(n)]
        sends = [copy(a, k, there[k], here) for a in range(n) for k in range(3)]
        arrivals = lambda: [copy(a, k, here, there[k]) for a in range(n) for k in range(3)]
        return mine, sends, arrivals

    def start(self, ins, outs, sems):
        mine, sends, _ = self._plan(ins, outs, sems)
        for cp in mine + sends:
            cp.start()

    def finish(self, ins, outs, sems):
        mine, sends, arrivals = self._plan(ins, outs, sems)
        for cp in arrivals():
            cp.wait_recv()
        for cp in sends:
            cp.wait_send()
        for cp in mine:
            cp.wait()


def _call(body, *, name, grid, in_specs, out_specs, out_shape, args, scratch_shapes=(), sem=(), comm=(), aliases=None,
          vmem_limit=VMEM_LIMIT_BYTES):
    single = not isinstance(out_shape, (list, tuple))
    out_shape = [out_shape] if single else list(out_shape)
    out_specs = [out_specs] if single else list(out_specs)
    comms = list(comm or ())
    n_in, n_out, n_scr = len(args), len(out_shape), len(scratch_shapes)
    params = pltpu.CompilerParams(dimension_semantics=tuple(sem) if sem else None, vmem_limit_bytes=vmem_limit)
    if not comms:
        res = pl.pallas_call(body, name=name, grid=grid, in_specs=list(in_specs), out_specs=out_specs, out_shape=out_shape,
                             scratch_shapes=list(scratch_shapes), input_output_aliases=aliases or {}, compiler_params=params)(*args)
        return res[0] if single else res
    counts = [n_in] + [len(c.arrays) for c in comms] + [n_out] + [len(c.out_shapes) for c in comms] + [n_scr] + [len(c.scratch) for c in comms]
    nc = len(comms)

    def carried(*refs):
        pos, groups = 0, []
        for cnt in counts:
            groups.append(refs[pos:pos + cnt])
            pos += cnt
        ins, c_ins = groups[0], groups[1:1 + nc]
        outs, c_outs = groups[1 + nc], groups[2 + nc:2 + 2 * nc]
        scr, c_sems = groups[2 + 2 * nc], groups[3 + 2 * nc:]
        ids = [pl.program_id(d) for d in range(len(grid))]
        is_first = functools.reduce(jnp.logical_and, [i == 0 for i in ids])
        is_last = functools.reduce(jnp.logical_and, [i == g - 1 for i, g in zip(ids, grid)])

        @pl.when(is_first)
        def _():
            for q, c in enumerate(comms):
                c.start(c_ins[q], c_outs[q], c_sems[q])

        body(*ins, *outs, *scr)

        @pl.when(is_last)
        def _():
            for q, c in enumerate(comms):
                c.relay(c_ins[q], c_outs[q], c_sems[q])
                c.finish(c_ins[q], c_outs[q], c_sems[q])

    anyspec = pl.BlockSpec(memory_space=pl.ANY)
    c_arrays = [a for c in comms for a in c.arrays]
    c_shapes = [s for c in comms for s in c.out_shapes]
    res = pl.pallas_call(
        carried, name=name, grid=grid, in_specs=list(in_specs) + [anyspec] * len(c_arrays), out_specs=out_specs + [anyspec] * len(c_shapes),
        out_shape=out_shape + c_shapes, scratch_shapes=list(scratch_shapes) + [s for c in comms for s in c.scratch],
        input_output_aliases=aliases or {}, compiler_params=params)(*args, *c_arrays)
    pos = n_out
    for c in comms:
        c.results = list(res[pos:pos + len(c.out_shapes)])
        pos += len(c.out_shapes)
    return res[0] if single else list(res[:n_out])


def _run_exchange(comm, *, name):
    def body(*refs):
        n_ci, n_co = len(comm.arrays), len(comm.out_shapes)
        ins, outs, sems = refs[:n_ci], refs[n_ci:n_ci + n_co], refs[n_ci + n_co:]
        comm.start(ins, outs, sems)
        comm.relay(ins, outs, sems)
        comm.finish(ins, outs, sems)

    anyspec = pl.BlockSpec(memory_space=pl.ANY)
    comm.results = list(pl.pallas_call(
        body, name=name, in_specs=[anyspec] * len(comm.arrays), out_specs=[anyspec] * len(comm.out_shapes),
        out_shape=list(comm.out_shapes), scratch_shapes=list(comm.scratch))(*comm.arrays))
    return comm.results


def _mm(a, b, *, dims="nn", bias=None, res=None, out_dtype=F32, name, tm=1024, tn=1024, tk=1024, comm=None, b_rows=None,
        out_window=None, into=None, colsum_b=False):
    if dims == "tn":
        k_dim, m_dim = a.shape
    else:
        m_dim, k_dim = a.shape
    row0, n_rows = b_rows if b_rows is not None else (0, b.shape[0])
    n_dim = n_rows if dims == "nt" else b.shape[1]
    assert dims == "nt" or n_rows == k_dim, (name, a.shape, b.shape, b_rows)
    tm, tn, tk = min(tm, m_dim), min(tn, n_dim), min(tk, k_dim)
    assert m_dim % tm == 0 and n_dim % tn == 0 and k_dim % tk == 0, (name, a.shape, b.shape)
    nk = k_dim // tk
    a_spec = pl.BlockSpec((tk, tm), lambda i, j, k: (k, i)) if dims == "tn" else pl.BlockSpec((tm, tk), lambda i, j, k: (i, k))
    if dims == "nt":
        assert row0 % tn == 0
        b_spec = pl.BlockSpec((tn, tk), lambda i, j, k: (row0 // tn + j, k))
    else:
        assert row0 % tk == 0
        b_spec = pl.BlockSpec((tk, tn), lambda i, j, k: (row0 // tk + k, j))
    in_specs, args = [a_spec, b_spec], [a, b]
    if bias is not None:
        in_specs.append(pl.BlockSpec((1, tn), lambda i, j, k: (0, j)))
        args.append(bias)
    if res is not None:
        in_specs.append(pl.BlockSpec((tm, tn), lambda i, j, k: (i, j)))
        args.append(res)
    dn = _DIMS[dims]

    if colsum_b:
        assert dims == "tn" and m_dim == tm and into is None and out_window is None

    def body(*refs):
        a_ref, b_ref = refs[0], refs[1]
        acc_ref = refs[-1]
        o_ref = refs[-3] if colsum_b else refs[-2]
        k = pl.program_id(2)

        @pl.when(k == 0)
        def _():
            acc_ref[...] = jnp.zeros_like(acc_ref)
            if colsum_b:
                refs[-2][...] = jnp.zeros_like(refs[-2])

        acc_ref[...] += _dot(a_ref[...], b_ref[...], dn)
        if colsum_b:
            refs[-2][...] += jnp.sum(b_ref[...].astype(F32), axis=0, keepdims=True)

        @pl.when(k == nk - 1)
        def _():
            r = acc_ref[...]
            pos = 2
            if bias is not None:
                r = r + refs[pos][...]
                pos += 1
            if res is not None:
                r = r + refs[pos][...]
            o_ref[...] = r.astype(out_dtype)

    out_row0, out_rows = out_window if out_window is not None else (0, m_dim)
    assert out_row0 % tm == 0
    aliases = None
    if into is not None:
        assert into.shape == (out_rows, n_dim) and into.dtype == out_dtype
        in_specs.append(pl.BlockSpec(memory_space=pl.ANY))
        args.append(into)
        aliases = {len(args) - 1: 0}
    out_spec = pl.BlockSpec((tm, tn), lambda i, j, k: (out_row0 // tm + i, j))
    out_shape = jax.ShapeDtypeStruct((out_rows, n_dim), out_dtype)
    if colsum_b:
        out_spec = [out_spec, pl.BlockSpec((1, tn), lambda i, j, k: (0, j))]
        out_shape = [out_shape, jax.ShapeDtypeStruct((1, n_dim), F32)]
    return _call(
        body, name=name, grid=(m_dim // tm, n_dim // tn, nk), in_specs=in_specs, out_specs=out_spec, out_shape=out_shape,
        aliases=aliases, scratch_shapes=[pltpu.VMEM((tm, tn), F32)], sem=("parallel", "parallel", "arbitrary"), args=args, comm=comm)


def _mm_norm_bwd(a, b, x, nw, res, *, dims="nn", b_rows=None, add=None, name, tm=1024, tk=1024, comm=None):
    m_dim, k_dim = a.shape
    row0, n_rows = b_rows if b_rows is not None else (0, b.shape[0])
    d_dim = x.shape[1]
    tm, tk = min(tm, m_dim), min(tk, k_dim)
    assert m_dim % tm == 0 and k_dim % tk == 0 and (n_rows if dims == "nt" else b.shape[1]) == d_dim, (name, a.shape, b.shape)
    nk = k_dim // tk
    if dims == "nt":
        assert row0 % d_dim == 0
        b_spec = pl.BlockSpec((d_dim, tk), lambda i, k: (row0 // d_dim, k))
    else:
        assert row0 % tk == 0 and n_rows == k_dim
        b_spec = pl.BlockSpec((tk, d_dim), lambda i, k: (row0 // tk + k, 0))
    row = pl.BlockSpec((tm, d_dim), lambda i, k: (i, 0))
    vec = pl.BlockSpec((1, d_dim), lambda i, k: (0, 0))
    extra = ([add] if add is not None else []) + list(res)
    dn = _DIMS[dims]

    def body(*refs):
        a_ref, b_ref, x_ref, nw_ref = refs[:4]
        extra_refs = refs[4:4 + len(extra)]
        dx_ref, dnw_ref, dob_ref, acc_ref = refs[-4:]
        i, k = pl.program_id(0), pl.program_id(1)

        @pl.when(k == 0)
        def _():
            acc_ref[...] = jnp.zeros_like(acc_ref)

        @pl.when((i == 0) & (k == 0))
        def _():
            dnw_ref[...] = jnp.zeros_like(dnw_ref)

        acc_ref[...] += _dot(a_ref[...], b_ref[...], dn)

        @pl.when(k == nk - 1)
        def _():
            dh = acc_ref[...]
            rest = list(extra_refs)
            if add is not None:
                dh = dh + rest.pop(0)[...]
            xhat, r = _rms(x_ref[...])
            dxhat = dh * nw_ref[...]
            dx = r * (dxhat - xhat * jnp.mean(dxhat * xhat, axis=-1, keepdims=True))
            for rr in rest:
                dx = dx + rr[...]
            dx_ref[...] = dx
            dob_ref[...] = (FFN_RES_WEIGHT * dx).astype(BF16)
            dnw_ref[...] += jnp.sum(dh * xhat, axis=0, keepdims=True)

    return _call(
        body, name=name, grid=(m_dim // tm, nk),
        in_specs=[pl.BlockSpec((tm, tk), lambda i, k: (i, k)), b_spec, row, vec] + [row] * len(extra), out_specs=[row, vec, row],
        out_shape=[jax.ShapeDtypeStruct((m_dim, d_dim), F32), jax.ShapeDtypeStruct((1, d_dim), F32),
                   jax.ShapeDtypeStruct((m_dim, d_dim), BF16)],
        scratch_shapes=[pltpu.VMEM((tm, d_dim), F32)], sem=("arbitrary", "arbitrary"), args=[a, b, x, nw] + extra, comm=comm)


def _rope_rotate(x, cos_t, sin_t):
    rows, width = x.shape
    half = ATT_HEAD_DIM // 2
    lane = lax.broadcasted_iota(jnp.int32, (rows, width), 1)
    first = (lane % ATT_HEAD_DIM) < half
    rot = jnp.where(first, -pltpu.roll(x, width - half, 1), pltpu.roll(x, half, 1))
    reps = width // 128
    return x * jnp.tile(cos_t, (1, reps)) + rot * jnp.tile(sin_t, (1, reps))


def _norm_mm(x, nw, w, bias, *, name, tm=1024, tn=1024, comm=None, w_rows=None, rope=None):
    t_dim, d_dim = x.shape
    transposed = w_rows is not None
    n_dim = w_rows if transposed else w.shape[1]
    tn = min(tn, n_dim)
    assert t_dim % tm == 0 and n_dim % tn == 0
    has_bias = bias is not None
    w_spec = pl.BlockSpec((tn, d_dim), lambda i, j: (j, 0)) if transposed else pl.BlockSpec((d_dim, tn), lambda i, j: (0, j))
    dn = NT if transposed else NN
    in_specs = [pl.BlockSpec((tm, d_dim), lambda i, j: (i, 0)), pl.BlockSpec((1, d_dim), lambda i, j: (0, 0)), w_spec]
    args = [x, nw, w]
    if has_bias:
        in_specs.append(pl.BlockSpec((1, tn), lambda i, j: (0, j)))
        args.append(bias)
    if rope is not None:
        in_specs += [pl.BlockSpec((tm, LANES), lambda i, j: (i, 0))] * 2
        args += list(rope)

    def body(*refs):
        x_ref, nw_ref, w_ref = refs[:3]
        o_ref, h_ref = refs[-2], refs[-1]

        @pl.when(pl.program_id(1) == 0)
        def _():
            xhat, _ = _rms(x_ref[...])
            h_ref[...] = (xhat * nw_ref[...]).astype(BF16)

        r = _dot(h_ref[...], w_ref[...], dn)
        if has_bias:
            r = r + refs[3][...]
        if rope is not None:
            r = _rope_rotate(r, refs[-4][...], refs[-3][...])
        o_ref[...] = r

    return _call(
        body, name=name, grid=(t_dim // tm, n_dim // tn), in_specs=in_specs,
        out_specs=[pl.BlockSpec((tm, tn), lambda i, j: (i, j)), pl.BlockSpec((tm, d_dim), lambda i, j: (i, 0))],
        out_shape=[jax.ShapeDtypeStruct((t_dim, n_dim), F32), jax.ShapeDtypeStruct((t_dim, d_dim), BF16)],
        sem=("parallel", "arbitrary"), args=args, comm=comm)


def _norm_bwd(x, nw, dh, res, *, name, tm=512, comm=None):
    t_dim, d_dim = x.shape
    n_res = len(res)
    row = pl.BlockSpec((tm, d_dim), lambda i: (i, 0))
    vec = pl.BlockSpec((1, d_dim), lambda i: (0, 0))

    def body(*refs):
        x_ref, nw_ref, dh_ref = refs[:3]
        dx_ref, dnw_ref = refs[-2], refs[-1]
        xhat, r = _rms(x_ref[...])
        dh = dh_ref[...]
        dxhat = dh * nw_ref[...]
        dx = r * (dxhat - xhat * jnp.mean(dxhat * xhat, axis=-1, keepdims=True))
        for rr in refs[3:3 + n_res]:
            dx = dx + rr[...]
        dx_ref[...] = dx

        @pl.when(pl.program_id(0) == 0)
        def _():
            dnw_ref[...] = jnp.zeros_like(dnw_ref)

        dnw_ref[...] += jnp.sum(dh * xhat, axis=0, keepdims=True)

    return _call(
        body, name=name, grid=(t_dim // tm,), in_specs=[row, vec, row] + [row] * n_res,
        out_specs=[row, vec],
        out_shape=[jax.ShapeDtypeStruct((t_dim, d_dim), F32), jax.ShapeDtypeStruct((1, d_dim), F32)],
        sem=("arbitrary",), args=[x, nw, dh, *res], comm=comm)


FFN_ROW_TILE = 512
FFN_SHARDS_PER_STEP = 2
FFN_STEPS = N_DEV // FFN_SHARDS_PER_STEP
FFN_STEP_COLS = FFN_SHARDS_PER_STEP * D_FF_SHARD


def _ffn_step_view(w):
    return w.reshape(FFN_STEPS, FFN_STEP_COLS, w.shape[-1])


def _ffn_specs(t_dim, d_dim):
    full = pl.BlockSpec((t_dim, d_dim), lambda j: (0, 0))
    wspec = pl.BlockSpec((None, FFN_STEP_COLS, d_dim), lambda j: (j, 0, 0))
    pre = pl.BlockSpec((None, t_dim, FFN_STEP_COLS), lambda j: (j, 0, 0))
    return full, wspec, pre


def _ffn_fwd(x, nw, wg, wu, wd, *, name, comm=None):
    t_dim, d_dim = x.shape
    n_tiles = t_dim // FFN_ROW_TILE

    def body(x_ref, nw_ref, wg_ref, wu_ref, wd_ref, o_ref, g_ref, u_ref, h_ref):
        j = pl.program_id(0)

        @pl.when(j == 0)
        def _():
            xhat, _ = _rms(x_ref[...])
            h_ref[...] = (xhat * nw_ref[...]).astype(BF16)
            o_ref[...] = jnp.zeros_like(o_ref)

        for t in range(n_tiles):
            rows = pl.ds(t * FFN_ROW_TILE, FFN_ROW_TILE)
            h = h_ref[rows, :]
            g = _dot(h, wg_ref[...], NT)
            u = _dot(h, wu_ref[...], NT)
            g_ref[rows, :] = g.astype(BF16)
            u_ref[rows, :] = u.astype(BF16)
            o_ref[rows, :] += _dot(g * _sigmoid(g) * u, wd_ref[...])

        @pl.when(j == FFN_STEPS - 1)
        def _():
            o_ref[...] = x_ref[...] + FFN_RES_WEIGHT * o_ref[...]

    full, wspec, pre = _ffn_specs(t_dim, d_dim)
    pre_shape = jax.ShapeDtypeStruct((FFN_STEPS, t_dim, FFN_STEP_COLS), BF16)
    return _call(
        body, name=name, grid=(FFN_STEPS,),
        in_specs=[full, pl.BlockSpec((1, d_dim), lambda j: (0, 0)), wspec, wspec, wspec],
        out_specs=[full, pre, pre, full],
        out_shape=[jax.ShapeDtypeStruct((t_dim, d_dim), F32), pre_shape, pre_shape, jax.ShapeDtypeStruct((t_dim, d_dim), BF16)],
        sem=("arbitrary",), args=[x, nw, _ffn_step_view(wg), _ffn_step_view(wu), _ffn_step_view(wd)], comm=comm)


def _ffn_bwd(h, dob, pre_g, pre_u, wg, wu, wd, *, name, comm=None):
    t_dim, d_dim = h.shape
    n_tiles = t_dim // FFN_ROW_TILE

    def body(h_ref, dob_ref, g_ref, u_ref, wg_ref, wu_ref, wd_ref, dh_ref, gg_ref, gu_ref, gd_ref, dwg_scr, dwu_scr, dwd_scr):
        @pl.when(pl.program_id(0) == 0)
        def _():
            dh_ref[...] = jnp.zeros_like(dh_ref)

        for t in range(n_tiles):
            rows = pl.ds(t * FFN_ROW_TILE, FFN_ROW_TILE)
            hh = h_ref[rows, :]
            do = dob_ref[rows, :]
            g = g_ref[rows, :].astype(F32)
            u = u_ref[rows, :].astype(F32)
            sg = _sigmoid(g)
            s = g * sg
            da = _dot(do, wd_ref[...], NT)
            dwd = _dot(s * u, do, TN)
            du = (da * s).astype(BF16)
            dg = (da * u * _dsilu(g, sg)).astype(BF16)
            dwg = _dot(dg, hh, TN)
            dwu = _dot(du, hh, TN)
            if t == 0:
                dwd_scr[...] = dwd
                dwg_scr[...] = dwg
                dwu_scr[...] = dwu
            else:
                dwd_scr[...] += dwd
                dwg_scr[...] += dwg
                dwu_scr[...] += dwu
            dh_ref[rows, :] += _dot(dg, wg_ref[...]) + _dot(du, wu_ref[...])
        gg_ref[...] = dwg_scr[...].astype(BF16)
        gu_ref[...] = dwu_scr[...].astype(BF16)
        gd_ref[...] = dwd_scr[...].astype(BF16)

    full, wspec, pre = _ffn_specs(t_dim, d_dim)
    gspec = pl.BlockSpec((None, FFN_STEP_COLS, d_dim), lambda j: (j, 0, 0), pipeline_mode=pl.Buffered(1))
    grad_shape = jax.ShapeDtypeStruct((FFN_STEPS, FFN_STEP_COLS, d_dim), BF16)
    dh, gg, gu, gd = _call(
        body, name=name, grid=(FFN_STEPS,),
        in_specs=[full, full, pre, pre, wspec, wspec, wspec], out_specs=[full, gspec, gspec, gspec],
        out_shape=[jax.ShapeDtypeStruct((t_dim, d_dim), F32)] + [grad_shape] * 3,
        scratch_shapes=[pltpu.VMEM((FFN_STEP_COLS, d_dim), F32)] * 3, sem=("arbitrary",), vmem_limit=FFN_BWD_VMEM_LIMIT_BYTES,
        args=[h, dob, pre_g, pre_u, _ffn_step_view(wg), _ffn_step_view(wu), _ffn_step_view(wd)], comm=comm)
    return dh, gg.reshape(wg.shape), gu.reshape(wu.shape), gd.reshape(wd.shape)


CONV_COLS = 256


def _shift_down(u, s, rows):
    return jnp.where(rows >= s, pltpu.roll(u, s, 0), 0.0)


def _shift_up(u, s, rows, t_dim):
    return jnp.where(rows < t_dim - s, pltpu.roll(u, t_dim - s, 0), 0.0)


def _conv_pre(u, w_ref, b_ref, rows):
    c = b_ref[...] + w_ref[CONV_WIDTH - 1:CONV_WIDTH, :] * u
    for k in range(CONV_WIDTH - 1):
        c = c + w_ref[k:k + 1, :] * _shift_down(u, CONV_WIDTH - 1 - k, rows)
    return c


def _conv_fwd(zx, cw, cb, *, name, comm=None):
    t_dim = zx.shape[0]
    off = D_INNER // CONV_COLS

    def body(u_ref, w_ref, b_ref, o_ref):
        rows = lax.broadcasted_iota(jnp.int32, (t_dim, CONV_COLS), 0)
        c = _conv_pre(u_ref[...], w_ref, b_ref, rows)
        o_ref[...] = c * _sigmoid(c)

    return _call(
        body, name=name, grid=(CONV_DIM // CONV_COLS,),
        in_specs=[pl.BlockSpec((t_dim, CONV_COLS), lambda j: (0, off + j)),
                  pl.BlockSpec((CONV_WIDTH, CONV_COLS), lambda j: (0, j)), pl.BlockSpec((1, CONV_COLS), lambda j: (0, j))],
        out_specs=pl.BlockSpec((t_dim, CONV_COLS), lambda j: (0, j)),
        out_shape=jax.ShapeDtypeStruct((t_dim, CONV_DIM), F32), sem=("parallel",), args=[zx, cw, cb], comm=comm)


def _conv_bwd(zx, cw, cb, dxs, db, dc, dzx, *, name, comm=None):
    t_dim = zx.shape[0]
    off = D_INNER // CONV_COLS
    n_xs = D_INNER // CONV_COLS
    n_b = GN // CONV_COLS

    def body(u_ref, w_ref, b_ref, dxs_ref, db_ref, dc_ref, dzx_in, dzx_ref, dw_ref, dbias_ref):
        j = pl.program_id(0)
        rows = lax.broadcasted_iota(jnp.int32, (t_dim, CONV_COLS), 0)
        u = u_ref[...]
        c = _conv_pre(u, w_ref, b_ref, rows)
        d = jnp.where(j < n_xs, dxs_ref[...], jnp.where(j < n_xs + n_b, db_ref[...], dc_ref[...]))
        dcv = d * _dsilu(c, _sigmoid(c))
        dpre = w_ref[CONV_WIDTH - 1:CONV_WIDTH, :] * dcv
        dw_ref[CONV_WIDTH - 1:CONV_WIDTH, :] = jnp.sum(dcv * u, axis=0, keepdims=True)
        for k in range(CONV_WIDTH - 1):
            s = CONV_WIDTH - 1 - k
            dpre = dpre + w_ref[k:k + 1, :] * _shift_up(dcv, s, rows, t_dim)
            dw_ref[k:k + 1, :] = jnp.sum(dcv * _shift_down(u, s, rows), axis=0, keepdims=True)
        dzx_ref[...] = dpre
        dbias_ref[...] = jnp.sum(dcv, axis=0, keepdims=True)

    blk = lambda n: pl.BlockSpec((t_dim, CONV_COLS), n)
    return _call(
        body, name=name, grid=(CONV_DIM // CONV_COLS,),
        in_specs=[blk(lambda j: (0, off + j)), pl.BlockSpec((CONV_WIDTH, CONV_COLS), lambda j: (0, j)),
                  pl.BlockSpec((1, CONV_COLS), lambda j: (0, j)),
                  blk(lambda j: (0, jnp.minimum(j, n_xs - 1))),
                  blk(lambda j: (0, jnp.clip(j - n_xs, 0, n_b - 1))),
                  blk(lambda j: (0, jnp.clip(j - n_xs - n_b, 0, n_b - 1))),
                  pl.BlockSpec(memory_space=pl.ANY)],
        out_specs=[blk(lambda j: (0, off + j)), pl.BlockSpec((CONV_WIDTH, CONV_COLS), lambda j: (0, j)),
                   pl.BlockSpec((1, CONV_COLS), lambda j: (0, j))],
        out_shape=[jax.ShapeDtypeStruct(dzx.shape, F32), jax.ShapeDtypeStruct((CONV_WIDTH, CONV_DIM), F32),
                   jax.ShapeDtypeStruct((1, CONV_DIM), F32)],
        aliases={6: 0}, sem=("parallel",), args=[zx, cw, cb, dxs, db, dc, dzx], comm=comm)


def _softplus_parts(x):
    e = jnp.exp(-jnp.abs(x))
    u = 1.0 + e
    log1p_e = jnp.where(u == 1.0, e, jnp.log(u) * e / jnp.where(u == 1.0, 1.0, u - 1.0))
    return jnp.maximum(x, 0.0) + log1p_e


def _dt_prep(dtr, dt_bias, a_log, *, name):
    def body(dtr_ref, bias_ref, alog_ref, dt_ref, a_ref):
        dt = _softplus_parts(dtr_ref[...] + bias_ref[...])
        dt_ref[...] = dt
        a_ref[...] = dt * (-jnp.exp(alog_ref[...]))

    return pl.pallas_call(body, name=name, out_shape=[jax.ShapeDtypeStruct(dtr.shape, F32)] * 2,
                          compiler_params=_params())(dtr, dt_bias, a_log)


def _dt_bwd(dtr, dt_bias, a_log, dt, ddt, da, *, name):
    def body(dtr_ref, bias_ref, alog_ref, dt_ref, ddt_ref, da_ref, ddtr_ref, dbias_ref, dalog_ref):
        a_neg = -jnp.exp(alog_ref[...])
        da_v = da_ref[...]
        ddt_tot = ddt_ref[...] + da_v * a_neg
        ddtr = ddt_tot * _sigmoid(dtr_ref[...] + bias_ref[...])
        ddtr_ref[...] = ddtr
        dbias_ref[...] = jnp.sum(ddtr, axis=0, keepdims=True)
        dalog_ref[...] = jnp.sum(da_v * dt_ref[...], axis=0, keepdims=True) * a_neg

    return pl.pallas_call(
        body, name=name,
        out_shape=[jax.ShapeDtypeStruct(dtr.shape, F32), jax.ShapeDtypeStruct((1, SSM_HEADS), F32),
                   jax.ShapeDtypeStruct((1, SSM_HEADS), F32)],
        compiler_params=_params())(dtr, dt_bias, a_log, dt, ddt, da)


GROUP_COLS = HEADS_PER_GROUP * SSM_HEAD_DIM
LANES = 128
HEADS_PER_LANE_BLOCK = LANES // SSM_HEAD_DIM


def _split3(x):
    hi = x.astype(BF16)
    r1 = x - hi.astype(F32)
    mid = r1.astype(BF16)
    lo = (r1 - mid.astype(F32)).astype(BF16)
    return hi, mid, lo


def _dot_select(a, b, dims=NN, data=0):
    out = None
    for part in _split3(a if data == 0 else b):
        lhs, rhs = (part, b.astype(BF16)) if data == 0 else (a.astype(BF16), part)
        t = lax.dot_general(lhs, rhs, dims, preferred_element_type=F32)
        out = t if out is None else out + t
    return out


def _group_sums(vals, expand):
    out = _dot_select(jnp.concatenate(vals, axis=0), expand, NT)
    return [out[i * CHUNK:(i + 1) * CHUNK] for i in range(len(vals))]


def _ssd_chunk_common(a_ref, dt_ref, b_ref, c_ref):
    row = lax.broadcasted_iota(jnp.int32, (CHUNK, CHUNK), 0)
    col = lax.broadcasted_iota(jnp.int32, (CHUNK, CHUNK), 1)
    causal = col <= row
    lower = causal.astype(F32)
    upper = (col >= row).astype(F32)
    head = lax.broadcasted_iota(jnp.int32, (HEADS_PER_GROUP, GROUP_COLS), 0)
    lane = lax.broadcasted_iota(jnp.int32, (HEADS_PER_GROUP, GROUP_COLS), 1)
    expand = ((lane >= head * SSM_HEAD_DIM) & (lane < (head + 1) * SSM_HEAD_DIM)).astype(F32)
    a = a_ref[...]
    cs = _dot_select(lower, a, data=1)
    cs_row = _dot_select(a, upper, TN)
    cs_x = _dot_select(cs, expand)
    dt_x = _dot_select(dt_ref[...], expand)
    e_out_x = jnp.exp(cs_x)
    e_st_x = jnp.exp(cs_x[CHUNK - 1:CHUNK, :] - cs_x)
    bc = b_ref[...]
    cc = c_ref[...]
    cb = _dot(cc, bc, NT)
    return causal, upper, expand.astype(BF16), cs, cs_row, dt_x, e_out_x, e_st_x, bc, cc, cb


def _head_decay(causal, cs, cs_row, h):
    return jnp.exp(jnp.where(causal, cs[:, h:h + 1] - cs_row[h:h + 1, :], NEG_BIG))


def _lane_block_head_masks():
    lane = lax.broadcasted_iota(jnp.int32, (CHUNK, LANES), 1)
    return [(lane >= i * SSM_HEAD_DIM) & (lane < (i + 1) * SSM_HEAD_DIM) for i in range(HEADS_PER_LANE_BLOCK)]


def _decay_state(dst_ref, old, new, cs):
    for h in range(HEADS_PER_GROUP):
        rows = slice(h * SSM_HEAD_DIM, (h + 1) * SSM_HEAD_DIM)
        dst_ref[rows, :] = jnp.exp(cs[CHUNK - 1:CHUNK, h:h + 1]) * old[rows, :] + new[rows, :]


def _ssd_fwd(xbc, dtg, ag, dgx, *, name, comm=None):
    t_dim = xbc.shape[0]

    def body(xs_ref, b_ref, c_ref, dt_ref, a_ref, d_ref, y_ref, st_ref, s_scr):
        @pl.when(pl.program_id(1) == 0)
        def _():
            s_scr[...] = jnp.zeros_like(s_scr)

        causal, _, _, cs, cs_row, dt_x, e_out_x, e_st_x, bc, cc, cb = _ssd_chunk_common(a_ref, dt_ref, b_ref, c_ref)
        masks = _lane_block_head_masks()
        xs = xs_ref[...]
        xdt_x = xs * dt_x
        prev = s_scr[...]
        st_ref[...] = prev
        y_off = e_out_x * _dot(cc, prev, NT) + xs * d_ref[...]
        for blk in range(GROUP_COLS // LANES):
            lanes = slice(blk * LANES, (blk + 1) * LANES)
            x_b = xdt_x[:, lanes].astype(BF16)
            acc = y_off[:, lanes]
            for i in range(HEADS_PER_LANE_BLOCK):
                m = cb * _head_decay(causal, cs, cs_row, blk * HEADS_PER_LANE_BLOCK + i)
                acc = acc + _dot(m, jnp.where(masks[i], x_b, jnp.zeros_like(x_b)))
            y_ref[:, lanes] = acc
        _decay_state(s_scr, prev, _dot(xdt_x * e_st_x, bc, TN), cs)

    xs = pl.BlockSpec((CHUNK, GROUP_COLS), lambda g, c: (c, g))
    bsp = pl.BlockSpec((CHUNK, SSM_STATE), lambda g, c: (c, D_INNER // SSM_STATE + g))
    csp = pl.BlockSpec((CHUNK, SSM_STATE), lambda g, c: (c, (D_INNER + GN) // SSM_STATE + g))
    per_head = pl.BlockSpec((None, CHUNK, HEADS_PER_GROUP), lambda g, c: (g, c, 0))
    dsk = pl.BlockSpec((None, 1, GROUP_COLS), lambda g, c: (g, 0, 0))
    return _call(
        body, name=name, grid=(SSM_GROUPS, N_CHUNKS),
        in_specs=[xs, bsp, csp, per_head, per_head, dsk],
        out_specs=[xs, pl.BlockSpec((None, GROUP_COLS, SSM_STATE), lambda g, c: (c, g, 0))],
        out_shape=[jax.ShapeDtypeStruct((t_dim, D_INNER), F32),
                   jax.ShapeDtypeStruct((N_CHUNKS, D_INNER, SSM_STATE), F32)],
        scratch_shapes=[pltpu.VMEM((GROUP_COLS, SSM_STATE), F32)],
        sem=("parallel", "arbitrary"), args=[xbc, xbc, xbc, dtg, ag, dgx], comm=comm)


def _ssd_bwd(xbc, dtg, ag, dgx, states, dy, *, name, comm=None):
    t_dim = xbc.shape[0]
    last = N_CHUNKS - 1

    def body(xs_ref, b_ref, c_ref, dt_ref, a_ref, d_ref, st_ref, dy_ref,
             dxs_ref, db_ref, dc_ref, ddt_ref, da_ref, dd_ref, ds_scr):
        @pl.when(pl.program_id(1) == 0)
        def _():
            ds_scr[...] = jnp.zeros_like(ds_scr)
            dd_ref[...] = jnp.zeros_like(dd_ref)

        causal, upper, expand, cs, cs_row, dt_x, e_out_x, e_st_x, bc, cc, cb = _ssd_chunk_common(a_ref, dt_ref, b_ref, c_ref)
        masks = _lane_block_head_masks()
        xs = xs_ref[...]
        dy_x = dy_ref[...]
        xdt_x = xs * dt_x
        prev = st_ref[...]
        d_s = ds_scr[...]
        g1_x = _dot(bc, d_s, NT)
        cp_x = _dot(cc, prev, NT)
        d_cb = jnp.zeros((CHUNK, CHUNK), F32)
        lane8 = lax.broadcasted_iota(jnp.int32, (CHUNK, HEADS_PER_GROUP), 1)
        sub8 = lax.broadcasted_iota(jnp.int32, (HEADS_PER_GROUP, CHUNK), 0)
        row_w = jnp.zeros((CHUNK, HEADS_PER_GROUP), F32)
        col_w = jnp.zeros((HEADS_PER_GROUP, CHUNK), F32)
        dxdt_blocks = []
        for blk in range(GROUP_COLS // LANES):
            lanes = slice(blk * LANES, (blk + 1) * LANES)
            dy_b = dy_x[:, lanes].astype(BF16)
            x_b = xdt_x[:, lanes].astype(BF16)
            acc_dx = jnp.zeros((CHUNK, LANES), F32)
            for i in range(HEADS_PER_LANE_BLOCK):
                h = blk * HEADS_PER_LANE_BLOCK + i
                decay = _head_decay(causal, cs, cs_row, h)
                m = cb * decay
                dy_h = jnp.where(masks[i], dy_b, jnp.zeros_like(dy_b))
                acc_dx = acc_dx + _dot(m, dy_h, TN)
                d_m = _dot(dy_h, x_b, NT)
                d_cb = d_cb + d_m * decay
                w = d_m * m
                row_w = jnp.where(lane8 == h, jnp.sum(w, axis=1, keepdims=True), row_w)
                col_w = jnp.where(sub8 == h, jnp.sum(w, axis=0, keepdims=True), col_w)
            dxdt_blocks.append(acc_dx)
        dxdt_x = jnp.concatenate(dxdt_blocks, axis=1) + e_st_x * g1_x
        dxs_ref[...] = dxdt_x * dt_x + dy_x * d_ref[...]
        dye = dy_x * e_out_x
        xde = xdt_x * e_st_x
        ddt, y_off, tl, dskip = _group_sums([dxdt_x * xs, dye * cp_x, xde * g1_x, dy_x * xs], expand)
        ddt_ref[...] = ddt
        dd_ref[...] += jnp.sum(dskip, axis=0, keepdims=True)
        sp = None
        for part in _split3(d_s * prev):
            t = lax.dot_general(expand, part, NN, preferred_element_type=F32)
            sp = t if sp is None else sp + t
        last_col = jnp.exp(cs_row[:, CHUNK - 1:CHUNK]) * jnp.sum(sp, axis=1, keepdims=True)
        eye = lax.broadcasted_iota(jnp.int32, (HEADS_PER_GROUP, HEADS_PER_GROUP), 0) == lax.broadcasted_iota(
            jnp.int32, (HEADS_PER_GROUP, HEADS_PER_GROUP), 1)
        last_row = jnp.sum(jnp.where(eye, last_col, 0.0), axis=0, keepdims=True) + jnp.sum(tl, axis=0, keepdims=True)
        is_last = lax.broadcasted_iota(jnp.int32, (CHUNK, 1), 0) == CHUNK - 1
        d_cs = row_w + y_off - tl + jnp.where(is_last, last_row, 0.0)
        da_ref[...] = _dot_select(upper, d_cs, data=1) - _dot_select(upper, col_w, NT, data=1)
        dc_ref[...] = _dot(d_cb, bc) + _dot(dye, prev)
        db_ref[...] = _dot(d_cb, cc, TN) + _dot(xde, d_s)
        _decay_state(ds_scr, d_s, _dot(dye, cc, TN), cs)

    rev = lambda c: last - c
    xs = pl.BlockSpec((CHUNK, GROUP_COLS), lambda g, c: (rev(c), g))
    bsp = pl.BlockSpec((CHUNK, SSM_STATE), lambda g, c: (rev(c), D_INNER // SSM_STATE + g))
    csp = pl.BlockSpec((CHUNK, SSM_STATE), lambda g, c: (rev(c), (D_INNER + GN) // SSM_STATE + g))
    per_head = pl.BlockSpec((None, CHUNK, HEADS_PER_GROUP), lambda g, c: (g, rev(c), 0))
    dsk = pl.BlockSpec((None, 1, GROUP_COLS), lambda g, c: (g, 0, 0))
    dsum = pl.BlockSpec((None, 1, HEADS_PER_GROUP), lambda g, c: (g, 0, 0))
    st = pl.BlockSpec((None, GROUP_COLS, SSM_STATE), lambda g, c: (rev(c), g, 0))
    grp = pl.BlockSpec((CHUNK, SSM_STATE), lambda g, c: (rev(c), g))
    return _call(
        body, name=name, grid=(SSM_GROUPS, N_CHUNKS),
        in_specs=[xs, bsp, csp, per_head, per_head, dsk, st, xs],
        out_specs=[xs, grp, grp, per_head, per_head, dsum],
        out_shape=[jax.ShapeDtypeStruct((t_dim, D_INNER), F32), jax.ShapeDtypeStruct((t_dim, GN), F32),
                   jax.ShapeDtypeStruct((t_dim, GN), F32),
                   jax.ShapeDtypeStruct((SSM_GROUPS, t_dim, HEADS_PER_GROUP), F32),
                   jax.ShapeDtypeStruct((SSM_GROUPS, t_dim, HEADS_PER_GROUP), F32),
                   jax.ShapeDtypeStruct((SSM_GROUPS, 1, HEADS_PER_GROUP), F32)],
        scratch_shapes=[pltpu.VMEM((GROUP_COLS, SSM_STATE), F32)],
        sem=("parallel", "arbitrary"), args=[xbc, xbc, xbc, dtg, ag, dgx, states, dy], comm=comm)


NORM_GROUP = D_INNER // SSM_GROUPS


def _gate_norm_fwd(y, zx, nw, *, name, tm=256):
    t_dim = y.shape[0]
    row = pl.BlockSpec((tm, D_INNER), lambda i: (i, 0))

    def body(y_ref, z_ref, nw_ref, o_ref):
        z = z_ref[...]
        yz = y_ref[...] * (z * _sigmoid(z))
        for g in range(SSM_GROUPS):
            cols = slice(g * NORM_GROUP, (g + 1) * NORM_GROUP)
            yhat, _ = _rms(yz[:, cols])
            o_ref[:, cols] = (yhat * nw_ref[:, cols]).astype(BF16)

    return pl.pallas_call(
        body, name=name, grid=(t_dim // tm,), in_specs=[row, row, pl.BlockSpec((1, D_INNER), lambda i: (0, 0))],
        out_specs=row, out_shape=jax.ShapeDtypeStruct((t_dim, D_INNER), BF16),
        compiler_params=_params("parallel"),
    )(y, zx, nw)


def _gate_norm_bwd(y, zx, nw, dyn, *, name, tm=256):
    t_dim = y.shape[0]
    row = pl.BlockSpec((tm, D_INNER), lambda i: (i, 0))
    vec = pl.BlockSpec((1, D_INNER), lambda i: (0, 0))

    def body(y_ref, z_ref, nw_ref, dyn_ref, dy_ref, dz_ref, dnw_ref):
        @pl.when(pl.program_id(0) == 0)
        def _():
            dnw_ref[...] = jnp.zeros_like(dnw_ref)

        z = z_ref[...]
        yv = y_ref[...]
        sg = _sigmoid(z)
        silu_z = z * sg
        yz = yv * silu_z
        dyn_v = dyn_ref[...]
        for g in range(SSM_GROUPS):
            cols = slice(g * NORM_GROUP, (g + 1) * NORM_GROUP)
            yhat, r = _rms(yz[:, cols])
            dn = dyn_v[:, cols]
            dnw_ref[:, cols] += jnp.sum(dn * yhat, axis=0, keepdims=True)
            dyhat = dn * nw_ref[:, cols]
            dyz = r * (dyhat - yhat * jnp.mean(dyhat * yhat, axis=-1, keepdims=True))
            dy_ref[:, cols] = dyz * silu_z[:, cols]
            dz_ref[:, cols] = dyz * yv[:, cols] * _dsilu(z[:, cols], sg[:, cols])

    return pl.pallas_call(
        body, name=name, grid=(t_dim // tm,), in_specs=[row, row, vec, row],
        out_specs=[row, row, vec],
        out_shape=[jax.ShapeDtypeStruct((t_dim, D_INNER), F32), jax.ShapeDtypeStruct((t_dim, ZX_DIM), F32),
                   jax.ShapeDtypeStruct((1, D_INNER), F32)],
        compiler_params=_params("arbitrary"),
    )(y, zx, nw, dyn)


HEADS_PER_LANE_TILE = LANES // ATT_HEAD_DIM
STACKED_ROWS = Q_PER_KV * WINDOW


def _att_half_masks():
    lane = lax.broadcasted_iota(jnp.int32, (WINDOW, LANES), 1)
    return [(lane >= i * ATT_HEAD_DIM) & (lane < (i + 1) * ATT_HEAD_DIM) for i in range(HEADS_PER_LANE_TILE)]


def _att_stack_heads(ref, kvh, masks):
    parts = []
    for g in range(Q_PER_KV):
        h = kvh * Q_PER_KV + g
        blk = ref[:, (h // HEADS_PER_LANE_TILE) * LANES:(h // HEADS_PER_LANE_TILE + 1) * LANES]
        parts.append(jnp.where(masks[h % HEADS_PER_LANE_TILE], blk, jnp.zeros_like(blk)))
    return jnp.concatenate(parts, axis=0)


def _att_kv_tile(ref, kvh, masks):
    blk = ref[:, (kvh // HEADS_PER_LANE_TILE) * LANES:(kvh // HEADS_PER_LANE_TILE + 1) * LANES]
    return jnp.where(masks[kvh % HEADS_PER_LANE_TILE], blk, pltpu.roll(blk, ATT_HEAD_DIM, 1)).astype(BF16)


def _att_stacked_masks(n):
    row = lax.bitwise_and(lax.broadcasted_iota(jnp.int32, (STACKED_ROWS, WINDOW), 0), WINDOW - 1)
    col = lax.broadcasted_iota(jnp.int32, (STACKED_ROWS, WINDOW), 1)
    return col <= row, (col > row) & (n > 0)


def _att_stack_columns(ref, kvh, rows):
    cols = [ref[:, kvh * Q_PER_KV + g:kvh * Q_PER_KV + g + 1] for g in range(Q_PER_KV)]
    return jnp.concatenate([jnp.broadcast_to(c, (rows, 1)) for c in cols], axis=0)


def _att_scores(q4, k_tile, mask):
    return jnp.where(mask, _dot(q4, k_tile, NT) * ATT_SCALE, NEG_BIG)


def _att_unstack(x4, kvh, masks, tiles):
    for g in range(Q_PER_KV):
        h = kvh * Q_PER_KV + g
        piece = x4[g * WINDOW:(g + 1) * WINDOW]
        t = h // HEADS_PER_LANE_TILE
        tiles[t] = piece if h % HEADS_PER_LANE_TILE == 0 else jnp.where(masks[1], piece, tiles[t])


def _attn_fwd(q, k, v, sinks, *, name, comm=None):
    t_dim = q.shape[0]

    def body(q_ref, kc_ref, kp_ref, vc_ref, vp_ref, s_ref, o_ref, l_ref):
        n = pl.program_id(0)
        masks = _att_half_masks()
        mask_c, mask_p = _att_stacked_masks(n)
        out_tiles = [None] * (D_MODEL // LANES)
        for kvh in range(N_KV_HEADS):
            q4 = _att_stack_heads(q_ref, kvh, masks).astype(BF16)
            kc, kp = _att_kv_tile(kc_ref, kvh, masks), _att_kv_tile(kp_ref, kvh, masks)
            vc, vp = _att_kv_tile(vc_ref, kvh, masks), _att_kv_tile(vp_ref, kvh, masks)
            sc = _att_scores(q4, kc, mask_c)
            sp = _att_scores(q4, kp, mask_p)
            sink = _att_stack_columns(s_ref, kvh, WINDOW)
            m = jnp.maximum(jnp.maximum(jnp.max(sc, axis=1, keepdims=True), jnp.max(sp, axis=1, keepdims=True)), sink)
            pc = jnp.exp(sc - m)
            pp = jnp.exp(sp - m)
            den = jnp.sum(pc, axis=1, keepdims=True) + jnp.sum(pp, axis=1, keepdims=True) + jnp.exp(sink - m)
            _att_unstack((_dot(pc, vc) + _dot(pp, vp)) / den, kvh, masks, out_tiles)
            lse4 = m + jnp.log(den)
            for g in range(Q_PER_KV):
                h = kvh * Q_PER_KV + g
                l_ref[:, h:h + 1] = lse4[g * WINDOW:(g + 1) * WINDOW]
        for t, tile in enumerate(out_tiles):
            o_ref[:, t * LANES:(t + 1) * LANES] = tile

    cur = lambda w: pl.BlockSpec((WINDOW, w), lambda n: (n, 0))
    prv = lambda w: pl.BlockSpec((WINDOW, w), lambda n: (jnp.maximum(n - 1, 0), 0))
    return _call(
        body, name=name, grid=(t_dim // WINDOW,),
        in_specs=[cur(D_MODEL), cur(KV_DIM), prv(KV_DIM), cur(KV_DIM), prv(KV_DIM), pl.BlockSpec((1, N_Q_HEADS), lambda n: (0, 0))],
        out_specs=[cur(D_MODEL), cur(N_Q_HEADS)],
        out_shape=[jax.ShapeDtypeStruct((t_dim, D_MODEL), F32), jax.ShapeDtypeStruct((t_dim, N_Q_HEADS), F32)],
        sem=("parallel",), args=[q, k, k, v, v, sinks], comm=comm)


def _attn_bwd(q, k, v, sinks, o, lse, do, cos2, sin2, *, name, comm=None):
    t_dim = q.shape[0]

    def body(q_ref, kc_ref, kp_ref, vc_ref, vp_ref, s_ref, o_ref, l_ref, do_ref, cos_ref, sin_ref, cos_all_ref, sin_all_ref,
             dq_ref, dk_ref, dv_ref, dsink_ref):
        n = pl.program_id(0)

        @pl.when(n == 0)
        def _():
            dk_ref[...] = jnp.zeros_like(dk_ref)
            dv_ref[...] = jnp.zeros_like(dv_ref)
            dsink_ref[...] = jnp.zeros_like(dsink_ref)

        masks = _att_half_masks()
        mask_c, mask_p = _att_stacked_masks(n)
        lane_row = lax.broadcasted_iota(jnp.int32, (1, N_Q_HEADS), 1)
        rows_c = pl.ds(pl.multiple_of(n * WINDOW, WINDOW), WINDOW)
        rows_p = pl.ds(pl.multiple_of(jnp.maximum(n - 1, 0) * WINDOW, WINDOW), WINDOW)
        dsink = jnp.zeros((1, N_Q_HEADS), F32)
        dq_tiles = [None] * (D_MODEL // LANES)
        kv_tiles = KV_DIM // LANES
        dkc_tiles, dkp_tiles, dvc_tiles, dvp_tiles = ([None] * kv_tiles for _ in range(4))

        def place(tiles, kvh, x):
            folded = x + pltpu.roll(x, ATT_HEAD_DIM, 1)
            t = kvh // HEADS_PER_LANE_TILE
            tiles[t] = folded if kvh % HEADS_PER_LANE_TILE == 0 else jnp.where(masks[1], folded, tiles[t])

        for kvh in range(N_KV_HEADS):
            q4 = _att_stack_heads(q_ref, kvh, masks).astype(BF16)
            do4 = _att_stack_heads(do_ref, kvh, masks)
            o4 = _att_stack_heads(o_ref, kvh, masks)
            kc, kp = _att_kv_tile(kc_ref, kvh, masks), _att_kv_tile(kp_ref, kvh, masks)
            vc, vp = _att_kv_tile(vc_ref, kvh, masks), _att_kv_tile(vp_ref, kvh, masks)
            l4 = _att_stack_columns(l_ref, kvh, WINDOW)
            pc = jnp.exp(_att_scores(q4, kc, mask_c) - l4)
            pp = jnp.exp(_att_scores(q4, kp, mask_p) - l4)
            delta = jnp.sum(do4 * o4, axis=1, keepdims=True)
            do4b = do4.astype(BF16)
            dsc = pc * (_dot(do4b, vc, NT) - delta)
            dsp = pp * (_dot(do4b, vp, NT) - delta)
            _att_unstack((_dot(dsc, kc) + _dot(dsp, kp)) * ATT_SCALE, kvh, masks, dq_tiles)
            place(dkc_tiles, kvh, _dot(dsc, q4, TN) * ATT_SCALE)
            place(dkp_tiles, kvh, _dot(dsp, q4, TN) * ATT_SCALE)
            place(dvc_tiles, kvh, _dot(pc, do4b, TN))
            place(dvp_tiles, kvh, _dot(pp, do4b, TN))
            p_sink = jnp.exp(_att_stack_columns(s_ref, kvh, WINDOW) - l4) * delta
            for g in range(Q_PER_KV):
                h = kvh * Q_PER_KV + g
                dsink = jnp.where(lane_row == h, -jnp.sum(p_sink[g * WINDOW:(g + 1) * WINDOW], axis=0, keepdims=True), dsink)
        for t, tile in enumerate(dq_tiles):
            dq_ref[:, t * LANES:(t + 1) * LANES] = _rope_rotate(tile, cos_ref[...], -sin_ref[...])
        for t in range(kv_tiles):
            lanes = slice(t * LANES, (t + 1) * LANES)
            dk_ref[rows_c, lanes] += dkc_tiles[t]
            dk_ref[rows_p, lanes] += dkp_tiles[t]
            dv_ref[rows_c, lanes] += dvc_tiles[t]
            dv_ref[rows_p, lanes] += dvp_tiles[t]
        dsink_ref[...] += dsink

        @pl.when(n == t_dim // WINDOW - 1)
        def _():
            dk_ref[...] = _rope_rotate(dk_ref[...], cos_all_ref[...], -sin_all_ref[...])

    cur = lambda w: pl.BlockSpec((WINDOW, w), lambda n: (n, 0))
    prv = lambda w: pl.BlockSpec((WINDOW, w), lambda n: (jnp.maximum(n - 1, 0), 0))
    whole = lambda w: pl.BlockSpec((t_dim, w), lambda n: (0, 0))
    svec = pl.BlockSpec((1, N_Q_HEADS), lambda n: (0, 0))
    return _call(
        body, name=name, grid=(t_dim // WINDOW,),
        in_specs=[cur(D_MODEL), cur(KV_DIM), prv(KV_DIM), cur(KV_DIM), prv(KV_DIM), svec, cur(D_MODEL), cur(N_Q_HEADS), cur(D_MODEL),
                  cur(LANES), cur(LANES), whole(LANES), whole(LANES)],
        out_specs=[cur(D_MODEL), whole(KV_DIM), whole(KV_DIM), svec],
        out_shape=[jax.ShapeDtypeStruct((t_dim, D_MODEL), F32), jax.ShapeDtypeStruct((t_dim, KV_DIM), F32),
                   jax.ShapeDtypeStruct((t_dim, KV_DIM), F32), jax.ShapeDtypeStruct((1, N_Q_HEADS), F32)],
        sem=("arbitrary",), args=[q, k, k, v, v, sinks, o, lse, do, cos2, sin2, cos2, sin2], comm=comm)


def _loss_head(x, nw, target, *, name, tm=512):
    t_dim, d_dim = x.shape
    row = pl.BlockSpec((tm, d_dim), lambda i: (i, 0))
    vec = pl.BlockSpec((1, d_dim), lambda i: (0, 0))

    def body(x_ref, nw_ref, tgt_ref, loss_ref, dx_ref, dnw_ref, dob_ref):
        @pl.when(pl.program_id(0) == 0)
        def _():
            loss_ref[...] = jnp.zeros_like(loss_ref)
            dnw_ref[...] = jnp.zeros_like(dnw_ref)

        xhat, r = _rms(x_ref[...])
        err = xhat * nw_ref[...] - tgt_ref[...]
        loss_ref[...] += 0.5 * _sum_all(jnp.mean(err * err, axis=-1, keepdims=True))
        dy = err * (1.0 / d_dim)
        dnw_ref[...] += jnp.sum(dy * xhat, axis=0, keepdims=True)
        dxhat = dy * nw_ref[...]
        dx = r * (dxhat - xhat * jnp.mean(dxhat * xhat, axis=-1, keepdims=True))
        dx_ref[...] = dx
        dob_ref[...] = (FFN_RES_WEIGHT * dx).astype(BF16)

    return pl.pallas_call(
        body, name=name, grid=(t_dim // tm,), in_specs=[row, vec, row],
        out_specs=[pl.BlockSpec((1, 1), lambda i: (0, 0)), row, vec, row],
        out_shape=[jax.ShapeDtypeStruct((1, 1), F32), jax.ShapeDtypeStruct((t_dim, d_dim), F32),
                   jax.ShapeDtypeStruct((1, d_dim), F32), jax.ShapeDtypeStruct((t_dim, d_dim), BF16)],
        compiler_params=_params("arbitrary"),
    )(x, nw, target)


def _rope_tables():
    pos = jnp.arange(SEQ, dtype=F32)
    inv = 1.0 / (ROPE_THETA ** (jnp.arange(0, ATT_HEAD_DIM, 2, dtype=F32) / ATT_HEAD_DIM))
    ang = pos[:, None] * inv[None, :]
    cos, sin = jnp.cos(ang), jnp.sin(ang)
    return jnp.tile(cos, (1, 4)), jnp.tile(sin, (1, 4))


def _to_groups(t):
    return t.reshape(t.shape[0], SSM_GROUPS, HEADS_PER_GROUP).transpose(1, 0, 2)


def _from_groups(t):
    return t.transpose(1, 0, 2).reshape(t.shape[1], SSM_HEADS)


def _forward_backward(x0, target, net):
    w = net.w
    nw = [[w("norm_w")[l, i][None, :] for i in range(3)] for l in range(2)]
    cos2, sin2 = _rope_tables()
    ffn_norm = [nw[0][0], nw[0][2], nw[1][0], nw[1][2]]

    ffn_pre = {}

    def ffn_f(x, blk):
        name = f"ffn_fwd{blk}"
        out, *ffn_pre[blk] = _ffn_fwd(x, ffn_norm[blk], w(f"gate{blk}"), w(f"up{blk}"), w(f"down{blk}"), name=name,
                                      comm=net.carry(name))
        return out

    x1 = ffn_f(x0, 0)
    zx, h1 = _norm_mm(x1, nw[0][1], w("w_in_t"), None, w_rows=ZX_DIM, name="ssm_in_proj", comm=net.carry("ssm_in_proj"))
    dtr = _mm(h1, w("w_in_t"), dims="nt", b_rows=(ZX_DIM, SSM_HEADS), name="ssm_dt_proj")
    xbc = _conv_fwd(zx, w("conv_w"), w("conv_b"), name="ssm_conv_fwd", comm=net.carry("ssm_conv_fwd"))
    dt, a_dt = _dt_prep(dtr, w("dt_bias"), w("a_log"), name="ssm_dt_prep")
    dtg, ag = _to_groups(dt), _to_groups(a_dt)
    dg = jnp.repeat(w("d_skip").reshape(SSM_GROUPS, 1, HEADS_PER_GROUP), SSM_HEAD_DIM, axis=2)
    y_ssd, states = _ssd_fwd(xbc, dtg, ag, dg, name="ssd_fwd", comm=net.carry("ssd_fwd"))
    yn = _gate_norm_fwd(y_ssd, zx, w("ssm_norm_w"), name="ssm_gate_norm_fwd")
    x2 = _mm(yn, w("wout"), res=x1, name="ssm_out_proj", comm=net.carry("ssm_out_proj"))
    x3 = ffn_f(x2, 1)
    k_rot, hk = _norm_mm(x3, w("kv_norm_w"), w("wk"), w("b_k"), rope=(cos2, sin2), name="k_proj")
    v = _mm(hk, w("wv"), bias=w("b_v"), name="v_proj")
    x4 = ffn_f(x3, 2)
    q_rot, h4 = _norm_mm(x4, nw[1][1], w("wq"), w("b_q"), rope=(cos2, sin2), name="q_proj")
    att, lse = _attn_fwd(q_rot, k_rot, v, w("sinks"), name="attn_fwd", comm=net.carry("attn_fwd"))
    x5 = _mm(att, w("wo"), bias=w("b_o"), res=x4, name="attn_out_proj")
    x6 = ffn_f(x5, 3)
    loss, dx6, d_final, dob6 = _loss_head(x6, w("final_norm_w"), target, name="loss_head")

    d_norm = [[None] * 3 for _ in range(2)]

    def ffn_b(x, dout, dob, blk):
        pre_g, pre_u, h = ffn_pre[blk]
        name = f"ffn_bwd{blk}"
        dh, gg, gu, gd = _ffn_bwd(h, dob, pre_g, pre_u, w(f"gate{blk}"), w(f"up{blk}"), w(f"down{blk}"), name=name,
                                  comm=net.carry(name))
        net.give(f"gate{blk}", gg)
        net.give(f"up{blk}", gu)
        net.give(f"down{blk}", gd)
        return _norm_bwd(x, ffn_norm[blk], dh, [dout], name=f"ffn_norm_bwd{blk}", comm=net.carry(f"ffn_norm_bwd{blk}"))

    by_rows = lambda g: g.reshape(N_DEV, g.shape[0] // N_DEV, g.shape[1])
    dx5, d_norm[1][2] = ffn_b(x5, dx6, dob6, 3)
    d_att = _mm(dx5, w("wo"), dims="nt", name="attn_out_proj_dx", comm=net.carry("attn_out_proj_dx"))
    g_o, d_bo = _mm(att, dx5, dims="tn", out_dtype=BF16, colsum_b=True, name="attn_out_proj_dw")
    net.give("w_o", by_rows(g_o))
    dq, dk, dv, d_sinks = _attn_bwd(q_rot, k_rot, v, w("sinks"), att, lse, d_att, cos2, sin2, name="attn_bwd",
                                    comm=net.carry("attn_bwd"))
    dx4, d_norm[1][1], dob4 = _mm_norm_bwd(dq, w("wq"), x4, nw[1][1], [dx5], dims="nt", name="q_proj_dx")
    g_q, d_bq = _mm(h4, dq, dims="tn", out_dtype=BF16, colsum_b=True, name="q_proj_dw")
    net.give("w_q", by_rows(g_q))
    dx3a, d_norm[1][0] = ffn_b(x3, dx4, dob4, 2)
    dhk = _mm(dk, w("wk"), dims="nt", name="k_proj_dx", comm=net.carry("k_proj_dx"))
    dx3, d_kvn, dob3 = _mm_norm_bwd(dv, w("wv"), x3, w("kv_norm_w"), [dx3a], dims="nt", add=dhk, name="v_proj_dx")
    g_k, d_bk = _mm(hk, dk, dims="tn", out_dtype=BF16, colsum_b=True, name="k_proj_dw")
    g_v, d_bv = _mm(hk, dv, dims="tn", out_dtype=BF16, colsum_b=True, name="v_proj_dw")
    net.give("w_k", by_rows(g_k))
    net.give("w_v", by_rows(g_v))
    dx2, d_norm[0][2] = ffn_b(x2, dx3, dob3, 1)
    d_yn = _mm(dx2, w("wout"), dims="nt", name="ssm_out_proj_dx", comm=net.carry("ssm_out_proj_dx"))
    net.give("w_out", by_rows(_mm(yn, dx2, dims="tn", out_dtype=BF16, name="ssm_out_proj_dw")))
    dy_ssd, dzx, d_ssm_norm = _gate_norm_bwd(y_ssd, zx, w("ssm_norm_w"), d_yn, name="ssm_gate_norm_bwd")
    dxs, d_b, d_c, ddtg, dag, ddg = _ssd_bwd(xbc, dtg, ag, dg, states, dy_ssd, name="ssd_bwd", comm=net.carry("ssd_bwd"))
    dzx, d_conv_w, d_conv_b = _conv_bwd(zx, w("conv_w"), w("conv_b"), dxs, d_b, d_c, dzx, name="ssm_conv_bwd",
                                        comm=net.carry("ssm_conv_bwd"))
    ddtr, d_dt_bias, d_a_log = _dt_bwd(dtr, w("dt_bias"), w("a_log"), dt, _from_groups(ddtg), _from_groups(dag), name="ssm_dt_bwd")
    dh1 = _mm(dzx, w("w_in_t"), b_rows=(0, ZX_DIM), name="ssm_in_proj_dx", comm=net.carry("ssm_in_proj_dx"))
    in_rows = N_DEV * IN_PROJ_SHARD
    g_in = _mm(dzx, h1, dims="tn", out_dtype=BF16, out_window=(0, in_rows), name="ssm_in_proj_dw")
    g_in = _mm(ddtr, h1, dims="tn", out_dtype=BF16, out_window=(ZX_DIM, in_rows), into=g_in, name="ssm_dt_proj_dw")
    net.give("w_in", g_in.reshape(N_DEV, IN_PROJ_SHARD, D_MODEL))
    dx1, d_norm[0][1], dob1 = _mm_norm_bwd(ddtr, w("w_in_t"), x1, nw[0][1], [dx2], b_rows=(ZX_DIM, SSM_HEADS), add=dh1,
                                           name="ssm_dt_proj_dx", comm=net.carry("ssm_norm_bwd"))
    dx0, d_norm[0][0] = ffn_b(x0, dx1, dob1, 0)

    small = {"norm_w": jnp.concatenate([d_norm[l][i] for l in range(2) for i in range(3)], axis=0),
             "ssm_conv_w": d_conv_w, "ssm_conv_b": d_conv_b, "ssm_dt_bias": d_dt_bias, "ssm_a_log": d_a_log,
             "ssm_d": ddg.reshape(1, SSM_HEADS), "ssm_norm_w": d_ssm_norm, "kv_norm_w": d_kvn,
             "b_k": d_bk, "b_v": d_bv, "attn_b_q": d_bq, "attn_sinks": d_sinks, "attn_b_o": d_bo, "final_norm_w": d_final}
    return loss, dx0, small


BLOCK_BYTES = 1 << 20


def _row_tile(rows, cols):
    for t in (512, 256, 128, 64, 32, 16):
        if rows % t == 0 and t * cols * 4 <= BLOCK_BYTES:
            return t
    return rows


def _cast_bf16(x, *, name):
    n_blk, rows, cols = x.shape
    tm = rows if rows * cols * 4 <= 2 * BLOCK_BYTES else _row_tile(rows, cols)
    spec = pl.BlockSpec((None, tm, cols), lambda b, i: (b, i, 0))

    def body(x_ref, o_ref):
        o_ref[...] = x_ref[...].astype(BF16)

    return pl.pallas_call(body, name=name, grid=(n_blk, rows // tm), in_specs=[spec], out_specs=spec,
                          out_shape=jax.ShapeDtypeStruct(x.shape, BF16), compiler_params=_params("parallel", "parallel"))(x)


def _cast_all_bf16(xs, *, name, comm=None):
    def body(*refs):
        for x_ref, o_ref in zip(refs[:len(xs)], refs[len(xs):]):
            o_ref[...] = x_ref[...].astype(BF16)

    specs = [pl.BlockSpec(x.shape, lambda i: (0, 0, 0)) for x in xs]
    return _call(body, name=name, grid=(1,), in_specs=specs, out_specs=specs,
                 out_shape=[jax.ShapeDtypeStruct(x.shape, BF16) for x in xs], sem=("arbitrary",), args=list(xs), comm=comm)


def _pair_add(grad, theirs, *, name):
    n_slots, rows, cols = theirs.shape
    tm = rows if rows * cols * 4 <= 2 * BLOCK_BYTES else _row_tile(rows, cols)

    def body(g_ref, t_ref, o_ref):
        o_ref[...] = (g_ref[...].astype(F32) + t_ref[...].astype(F32)).astype(BF16)

    spec = pl.BlockSpec((None, tm, cols), lambda s, i: (s, i, 0))
    return pl.pallas_call(
        body, name=name, grid=(n_slots, rows // tm),
        in_specs=[pl.BlockSpec((None, tm, cols), lambda s, i: (2 * s + lax.axis_index("c"), i, 0)), spec], out_specs=spec,
        out_shape=jax.ShapeDtypeStruct(theirs.shape, BF16), compiler_params=_params("parallel", "parallel"),
    )(grad, theirs)


def _adam_update(g, w, m, v):
    m = ADAM_B1 * m + (1.0 - ADAM_B1) * g
    v = ADAM_B2 * v + (1.0 - ADAM_B2) * (g * g)
    m_hat = m / (1.0 - ADAM_B1 ** ADAM_STEP)
    v_hat = v / (1.0 - ADAM_B2 ** ADAM_STEP)
    delta = -ADAM_LR * (m_hat / (jnp.sqrt(v_hat) + ADAM_EPS) + ADAM_WD * w)
    return delta, m, v


def _adamw(parts, w, m, v, first_blk, prev, *, name, comm=None):
    n_blk, rows, cols = w.shape
    tm = _row_tile(rows, cols)
    n_tiles = rows // tm
    spec = pl.BlockSpec((None, tm, cols), lambda b, i: (first_blk + b, i, 0))
    n_prev, n_here = len(prev), len(parts)
    n_parts = parts[0].shape[0]

    def part_spec(q):
        return pl.BlockSpec((n_parts, tm, cols), lambda b, i: (0, jnp.where(b < q, 0, jnp.where(b == q, i, n_tiles - 1)), 0))

    def body(*refs):
        p_refs = refs[:n_here]
        w_ref, m_ref, v_ref = refs[n_here:n_here + 3]
        g_ref, d_ref, nm_ref, nv_ref = refs[n_here + 3 + n_prev:]
        b = pl.program_id(0)
        g = None
        for s in range(n_parts):
            t = p_refs[0][s]
            for q in range(1, n_here):
                t = jnp.where(b == q, p_refs[q][s], t)
            g = t.astype(F32) if g is None else g + t.astype(F32)
        delta, nm, nv = _adam_update(g, w_ref[...], m_ref[...], v_ref[...])
        g_ref[...] = g
        d_ref[...] = delta
        nm_ref[...] = nm
        nv_ref[...] = nv

    return _call(
        body, name=name, grid=(n_here, n_tiles),
        in_specs=[part_spec(q) for q in range(n_here)] + [spec, spec, spec] + [pl.BlockSpec(memory_space=pl.ANY)] * n_prev,
        out_specs=[spec] * 4, out_shape=[jax.ShapeDtypeStruct((n_blk, rows, cols), F32)] * 4,
        aliases={n_here + 3 + q: q for q in range(n_prev)}, sem=("arbitrary", "arbitrary"),
        args=[*parts, w, m, v, *prev], comm=comm)


def _sum_parts(parts, *, name):
    def body(p_ref, o_ref):
        g = p_ref[0]
        for s in range(1, N_DEV):
            g = g + p_ref[s]
        o_ref[...] = g

    return pl.pallas_call(body, name=name, out_shape=jax.ShapeDtypeStruct(parts.shape[1:], F32), compiler_params=_params())(parts)


def _adamw_packed(g, w, m, v, *, name):
    def body(g_ref, w_ref, m_ref, v_ref, d_ref, nm_ref, nv_ref):
        delta, nm, nv = _adam_update(g_ref[...], w_ref[...], m_ref[...], v_ref[...])
        d_ref[...] = delta
        nm_ref[...] = nm
        nv_ref[...] = nv

    return pl.pallas_call(body, name=name, out_shape=[jax.ShapeDtypeStruct(g.shape, F32)] * 3, compiler_params=_params())(g, w, m, v)


SUBLANES = 8


WIDE_PACK = 1024


def _pack(arrs, width=LANES):
    rows = []
    for a in arrs:
        a2 = a.reshape(-1, a.shape[-1])
        a2 = jnp.pad(a2, ((0, 0), (0, (-a2.shape[1]) % width)))
        rows += [a2[:, i * width:(i + 1) * width] for i in range(a2.shape[1] // width)]
    out = jnp.concatenate(rows, axis=0)
    return jnp.pad(out, ((0, (-out.shape[0]) % SUBLANES), (0, 0)))


def _unpack(packed, shapes, width=LANES):
    outs, r = [], 0
    for shp in shapes:
        lead, cols = math.prod(shp[:-1]), shp[-1]
        n_blocks = -(-cols // width)
        blocks = [packed[r + i * lead:r + (i + 1) * lead] for i in range(n_blocks)]
        outs.append(jnp.concatenate(blocks, axis=1)[:, :cols].reshape(shp))
        r += n_blocks * lead
    return outs


WEIGHT_NAMES = ("norm_w", "ffn_w_gate", "ffn_w_up", "ffn_w_down", "ssm_w_in", "ssm_conv_w", "ssm_conv_b", "ssm_dt_bias",
                "ssm_a_log", "ssm_d", "ssm_norm_w", "ssm_w_out", "kv_norm_w", "w_k", "b_k", "w_v", "b_v", "attn_w_q",
                "attn_b_q", "attn_sinks", "attn_w_o", "attn_b_o", "final_norm_w")
MATRIX_NAMES = ("ffn_w_gate", "ffn_w_up", "ffn_w_down", "ssm_w_in", "ssm_w_out", "w_k", "w_v", "attn_w_q", "attn_w_o")
VECTOR_NAMES = tuple(n for n in WEIGHT_NAMES if n not in MATRIX_NAMES)
SHARDED_VECTORS = ("norm_w", "ssm_conv_w", "ssm_conv_b", "ssm_norm_w")


GATHER_PLAN = {
    "gather_stage0": ("gate0", "up0", "down0", "vec"),
    "ffn_fwd0": ("w_in",),
    "ssm_in_proj": ("w_out", "gate1"),
    "ssm_conv_fwd": ("w_k", "w_v", "up1"),
    "ssd_fwd": ("down1", "gate2"),
    "ssm_out_proj": ("w_q", "w_o"),
    "ffn_fwd1": ("up2", "down2"),
    "ffn_fwd2": ("up3",),
    "attn_fwd": ("gate3", "down3"),
}
PAIR_PLAN = {
    "attn_bwd": ("gate3", "up3", "down3"),
    "ffn_bwd2": ("w_q", "w_o"),
    "ffn_bwd1": ("gate2", "up2", "down2", "w_k", "w_v"),
    "ssd_bwd": ("gate1", "up1", "down1", "w_out"),
    "ssm_norm_bwd": ("w_in",),
    "ffn_norm_bwd0": ("gate0", "up0", "down0"),
}
CHIP_PLAN = {
    "ffn_bwd2": ("gate3", "up3", "down3"),
    "ssd_bwd": ("gate2", "up2", "down2", "w_q", "w_o", "w_k", "w_v"),
    "ssm_conv_bwd": ("gate1", "up1"),
    "ssm_in_proj_dx": ("w_out",),
    "ffn_bwd0": ("down1", "w_in"),
    "adamw_gate": ("gate0",),
    "adamw_up": ("up0",),
    "adamw_down": ("down0",),
}
FFN_PARAMS = {"gate": "ffn_w_gate", "up": "ffn_w_up", "down": "ffn_w_down"}
SINGLE_MATRICES = {"w_in": "ssm_w_in", "w_out": "ssm_w_out", "w_k": "w_k", "w_v": "w_v", "w_q": "attn_w_q", "w_o": "attn_w_o"}


TRANSPOSED = ("ffn_w_gate", "ffn_w_up", "ssm_w_in")


def _matrix_view(name, a):
    if name in TRANSPOSED:
        a = jnp.swapaxes(a, -1, -2)
    return a.reshape((-1,) + a.shape[-2:])


def _from_matrix_view(name, a, shape):
    if name in TRANSPOSED:
        return jnp.swapaxes(a.reshape(shape[:-2] + (shape[-1], shape[-2])), -1, -2)
    return a.reshape(shape)


class _MeshNet:
    def __init__(self, p):
        self.p = p
        self.views = {n: _matrix_view(n, p[n]) for n in MATRIX_NAMES}
        self.local = {"vec": _pack([p[n] for n in SHARDED_VECTORS])}
        for short, n in FFN_PARAMS.items():
            cast = _cast_bf16(self.views[n], name=f"cast_{short}")
            self.local.update({f"{short}{k}": (cast, k) for k in range(N_FFN)})
        self.gathered_at, self.pairs_at, self.parts_at, self.grads, self.cache = {}, {}, {}, {}, {}
        singles = _cast_all_bf16([self.views[n] for n in SINGLE_MATRICES.values()], name="gather_stage0",
                                 comm=self.carry("gather_stage0"))
        self.local.update({short: (cast, 0) for short, cast in zip(SINGLE_MATRICES, singles)})

    def carry(self, name):
        comms = []
        if name in GATHER_PLAN:
            keys, comm = GATHER_PLAN[name], _Gather([self.local[k] for k in GATHER_PLAN[name]])
            self.gathered_at.update({k: (comm, i) for i, k in enumerate(keys)})
            comms.append(comm)
        if name in CHIP_PLAN:
            sums = []
            for k in CHIP_PLAN[name]:
                comm, i = self.pairs_at[k]
                sums.append(_pair_add(self.grads[k], comm.results[i], name=f"pair_add_{k}"))
            comm = _ChipExchange(sums)
            self.parts_at.update({k: (comm, i) for i, k in enumerate(CHIP_PLAN[name])})
            comms.append(comm)
        if name in PAIR_PLAN:
            keys, comm = PAIR_PLAN[name], _PairSwap([self.grads[k] for k in PAIR_PLAN[name]])
            self.pairs_at.update({k: (comm, i) for i, k in enumerate(keys)})
            comms.append(comm)
        return comms

    def run(self, name):
        for comm in self.carry(name):
            _run_exchange(comm, name=name)

    def give(self, key, grad):
        self.grads[key] = grad

    def parts(self, key):
        comm, i = self.parts_at[key]
        return comm.results[i]

    def _gathered(self, key):
        comm, i = self.gathered_at[key]
        return comm.results[i]

    def _vec(self, r0, lead, n_blocks):
        vecs = self._gathered("vec")
        return jnp.concatenate([vecs[d, r0 + i * lead:r0 + (i + 1) * lead, :] for d in range(N_DEV) for i in range(n_blocks)], axis=1)

    def _derive(self, name):
        p = self.p
        if name[:-1] in FFN_PARAMS:
            return self._gathered(name)
        if name == "w_in_t":
            return self._gathered("w_in").reshape(N_DEV * IN_PROJ_SHARD, D_MODEL)
        by_rows = {"wout": "w_out", "wk": "w_k", "wv": "w_v", "wq": "w_q", "wo": "w_o"}
        if name in by_rows:
            g = self._gathered(by_rows[name])
            return g.reshape(N_DEV * g.shape[1], g.shape[2])
        vectors = {"norm_w": lambda: self._vec(0, 6, 1).reshape(2, 3, D_MODEL), "conv_w": lambda: self._vec(6, CONV_WIDTH, 3),
                   "conv_b": lambda: self._vec(18, 1, 3), "ssm_norm_w": lambda: self._vec(21, 1, 2)}
        if name in vectors:
            return vectors[name]()
        replicated = {"dt_bias": p["ssm_dt_bias"], "a_log": p["ssm_a_log"], "d_skip": p["ssm_d"], "kv_norm_w": p["kv_norm_w"][None],
                      "b_k": p["b_k"][None], "b_v": p["b_v"][None], "b_q": p["attn_b_q"], "sinks": p["attn_sinks"],
                      "b_o": p["attn_b_o"], "final_norm_w": p["final_norm_w"][None]}
        return replicated[name]

    def w(self, name):
        if name not in self.cache:
            self.cache[name] = self._derive(name)
        return self.cache[name]


def _step(x, target, p, m, v):
    pos = _slot(_position())
    net = _MeshNet(p)
    loss, grad_x, small = _forward_backward(x, target, net)

    grads, deltas, new_m, new_v = {}, {}, {}, {}
    view = lambda d, n: _matrix_view(n, d[n])
    vec_gather = _Gather([_pack([small[n] for n in VECTOR_NAMES], WIDE_PACK)])
    for short, n in SINGLE_MATRICES.items():
        outs = _adamw([net.parts(short)], net.views[n], view(m, n), view(v, n), 0, [], name=f"adamw_{short}",
                      comm=[vec_gather] if short == "w_in" else None)
        grads[n], deltas[n], new_m[n], new_v[n] = [_from_matrix_view(n, o, p[n].shape) for o in outs]
    ffn_outs = {}
    for short, n in FFN_PARAMS.items():
        ffn_outs[short] = _adamw([net.parts(f"{short}{k}") for k in range(1, N_FFN)], net.views[n], view(m, n), view(v, n), 1, [],
                                 name=f"adamw_{short}", comm=net.carry(f"adamw_{short}"))
    for short, n in FFN_PARAMS.items():
        outs = _adamw([net.parts(f"{short}0")], net.views[n], view(m, n), view(v, n), 0, ffn_outs[short], name=f"adamw_{short}0")
        grads[n], deltas[n], new_m[n], new_v[n] = [_from_matrix_view(n, o, p[n].shape) for o in outs]
    vec_sum = _sum_parts(vec_gather.results[0], name="sum_vector_grads")
    full_shapes = {"norm_w": (2, 3, D_MODEL), "ssm_conv_w": (1, CONV_WIDTH, CONV_DIM), "ssm_conv_b": (1, CONV_DIM),
                   "ssm_norm_w": (1, D_INNER)}
    vec_full = dict(zip(VECTOR_NAMES, _unpack(vec_sum, [full_shapes.get(n, p[n].shape) for n in VECTOR_NAMES], WIDE_PACK)))
    for n in VECTOR_NAMES:
        g = vec_full[n]
        if n in SHARDED_VECTORS:
            per = p[n].shape[-1]
            g = lax.dynamic_slice_in_dim(g, pos * per, per, axis=g.ndim - 1)
        grads[n] = g
    packed = _adamw_packed(*[_pack([d[n] for n in VECTOR_NAMES], WIDE_PACK) for d in (grads, p, m, v)], name="adamw_vectors")
    shapes = [p[n].shape for n in VECTOR_NAMES]
    for d, pk in zip((deltas, new_m, new_v), packed):
        d.update(zip(VECTOR_NAMES, _unpack(pk, shapes, WIDE_PACK)))
    return loss, grad_x, grads, deltas, new_m, new_v


def kernel(x, norm_w, ffn_w_gate, ffn_w_up, ffn_w_down, ssm_w_in, ssm_conv_w, ssm_conv_b, ssm_dt_bias, ssm_a_log, ssm_d, ssm_norm_w, ssm_w_out, kv_norm_w, w_k, b_k, w_v, b_v, attn_w_q, attn_b_q, attn_sinks, attn_w_o, attn_b_o, final_norm_w, loss_target, m_norm_w, m_ffn_w_gate, m_ffn_w_up, m_ffn_w_down, m_ssm_w_in, m_ssm_conv_w, m_ssm_conv_b, m_ssm_dt_bias, m_ssm_a_log, m_ssm_d, m_ssm_norm_w, m_ssm_w_out, m_kv_norm_w, m_w_k, m_b_k, m_w_v, m_b_v, m_attn_w_q, m_attn_b_q, m_attn_sinks, m_attn_w_o, m_attn_b_o, m_final_norm_w, v_norm_w, v_ffn_w_gate, v_ffn_w_up, v_ffn_w_down, v_ssm_w_in, v_ssm_conv_w, v_ssm_conv_b, v_ssm_dt_bias, v_ssm_a_log, v_ssm_d, v_ssm_norm_w, v_ssm_w_out, v_kv_norm_w, v_w_k, v_b_k, v_w_v, v_b_v, v_attn_w_q, v_attn_b_q, v_attn_sinks, v_attn_w_o, v_attn_b_o, v_final_norm_w):
    p = dict(zip(WEIGHT_NAMES, (norm_w, ffn_w_gate, ffn_w_up, ffn_w_down, ssm_w_in, ssm_conv_w, ssm_conv_b, ssm_dt_bias, ssm_a_log, ssm_d, ssm_norm_w, ssm_w_out, kv_norm_w, w_k, b_k, w_v, b_v, attn_w_q, attn_b_q, attn_sinks, attn_w_o, attn_b_o, final_norm_w)))
    m = dict(zip(WEIGHT_NAMES, (m_norm_w, m_ffn_w_gate, m_ffn_w_up, m_ffn_w_down, m_ssm_w_in, m_ssm_conv_w, m_ssm_conv_b, m_ssm_dt_bias, m_ssm_a_log, m_ssm_d, m_ssm_norm_w, m_ssm_w_out, m_kv_norm_w, m_w_k, m_b_k, m_w_v, m_b_v, m_attn_w_q, m_attn_b_q, m_attn_sinks, m_attn_w_o, m_attn_b_o, m_final_norm_w)))
    v = dict(zip(WEIGHT_NAMES, (v_norm_w, v_ffn_w_gate, v_ffn_w_up, v_ffn_w_down, v_ssm_w_in, v_ssm_conv_w, v_ssm_conv_b, v_ssm_dt_bias, v_ssm_a_log, v_ssm_d, v_ssm_norm_w, v_ssm_w_out, v_kv_norm_w, v_w_k, v_b_k, v_w_v, v_b_v, v_attn_w_q, v_attn_b_q, v_attn_sinks, v_attn_w_o, v_attn_b_o, v_final_norm_w)))
    loss, grad_x, grads, deltas, new_m, new_v = _step(x[0], loss_target[0], p, m, v)
    loss = lax.psum(loss[0, 0], ("x", "y", "c"))
    return (loss, grad_x[None], *[grads[n] for n in WEIGHT_NAMES], *[deltas[n] for n in WEIGHT_NAMES],
            *[new_m[n] for n in WEIGHT_NAMES], *[new_v[n] for n in WEIGHT_NAMES])
```
